```python
import jax, jax.numpy as jnp
from jax import lax
import numpy as np

D_MODEL = 1024
BATCH = 8
SEQ = 8192
DEPTH = 2

CHUNK = 64
CONV_W = 4
EPS = 1e-6
F32 = jnp.float32

GDN_HEADS = 4
GDN_DK = 128
GDN_DV = 128
GDN_W = GDN_HEADS * GDN_DV
SSD_HEADS = 16
SSD_P = 64
SSD_N = 128
SSD_GROUPS = 2
SSD_W = SSD_HEADS * SSD_P
RET_HEADS = 4
RET_DK = 128
RET_DV = 128
RET_W = RET_HEADS * RET_DV
ROPE_BASE = 10000.0

MIX_W = GDN_W + SSD_W + RET_W

GDN_SIZES = [GDN_HEADS * GDN_DK, GDN_HEADS * GDN_DK, GDN_W, GDN_W, GDN_HEADS, GDN_HEADS]
SSD_SIZES = [SSD_W, SSD_GROUPS * SSD_N, SSD_GROUPS * SSD_N, SSD_W, SSD_HEADS]
RET_SIZES = [RET_HEADS * RET_DK, RET_HEADS * RET_DK, RET_W, RET_W]
IN_SIZES = GDN_SIZES + SSD_SIZES + RET_SIZES
N_IN = sum(IN_SIZES)
GDN_CONV_CH = 2 * GDN_HEADS * GDN_DK + GDN_W
SSD_CONV_CH = SSD_W + 2 * SSD_GROUPS * SSD_N

kernel_name = "hybrid_gdn_ssd_retention_parallel_groups"


def rmsnorm(x, w):
    xf = x.astype(F32)
    return xf * lax.rsqrt(jnp.mean(xf * xf, axis=-1, keepdims=True) + EPS) * w.astype(F32)


def split_cols(t, sizes):
    out, start = [], 0
    for s in sizes:
        out.append(t[..., start:start + s])
        start += s
    return out


def causal_conv(x, w):
    k = w.shape[0]
    seq = x.shape[1]
    xp = jnp.pad(x, ((0, 0), (k - 1, 0), (0, 0)))
    return sum(xp[:, i:i + seq] * w[i].astype(F32) for i in range(k))


def to_chunks(t):
    b, l = t.shape[:2]
    return jnp.moveaxis(t.reshape(b, l // CHUNK, CHUNK, *t.shape[2:]), 1, 0)


def from_chunks(t):
    n, b, c = t.shape[:3]
    return jnp.moveaxis(t, 0, 1).reshape(b, n * c, *t.shape[3:])


def l2norm(t):
    return t * lax.rsqrt(jnp.sum(t * t, axis=-1, keepdims=True) + EPS)


def rotary(t, pos):
    half = t.shape[-1] // 2
    inv = ROPE_BASE ** (-jnp.arange(half, dtype=F32) / half)
    ang = pos.astype(F32)[:, None] * inv[None, :]
    cos = jnp.cos(ang)[None, :, None, :]
    sin = jnp.sin(ang)[None, :, None, :]
    t1, t2 = t[..., :half], t[..., half:]
    return jnp.concatenate([t1 * cos - t2 * sin, t1 * sin + t2 * cos], axis=-1)


def gated_deltanet(q, k, v, b_raw, a_raw, A_log, dt_bias):
    bsz = q.shape[0]
    q = l2norm(q) * (GDN_DK ** -0.5)
    k = l2norm(k)
    beta = jax.nn.sigmoid(b_raw)
    g = -jnp.exp(A_log.astype(F32)) * jax.nn.softplus(a_raw + dt_bias.astype(F32))
    causal = jnp.tril(jnp.ones((CHUNK, CHUNK), bool))
    strict = jnp.tril(jnp.ones((CHUNK, CHUNK), F32), -1)
    eye = jnp.eye(CHUNK, dtype=F32)

    def step(S, inp):
        qi, ki, vi, bi, gi = inp
        gcum = jnp.cumsum(gi, axis=1)
        gh = jnp.swapaxes(gcum, 1, 2)
        diff = gh[..., :, None] - gh[..., None, :]
        decay = jnp.exp(jnp.where(causal, diff, -jnp.inf))
        kb = ki * bi[..., None]
        a_low = jnp.einsum('bihd,bjhd->bhij', kb, ki) * decay * strict
        rhs = jnp.concatenate([vi * bi[..., None], kb * jnp.exp(gcum)[..., None]], axis=-1)
        rhs = jnp.swapaxes(rhs, 1, 2)
        sol = lax.linalg.triangular_solve(a_low + eye, rhs, left_side=True, lower=True)
        u, w = sol[..., :GDN_DV], sol[..., GDN_DV:]
        v_new = u - jnp.einsum('bhcd,bhdv->bhcv', w, S)
        attn = jnp.einsum('bihd,bjhd->bhij', qi, ki) * decay
        o = (jnp.einsum('bihd,bhdv->bihv', qi * jnp.exp(gcum)[..., None], S)
             + jnp.einsum('bhij,bhjv->bihv', attn, v_new))
        glast = gcum[:, -1]
        kdec = ki * jnp.exp(glast[:, None, :] - gcum)[..., None]
        S = S * jnp.exp(glast)[:, :, None, None] + jnp.einsum('bjhd,bhjv->bhdv', kdec, v_new)
        return S, o

    s0 = jnp.zeros((bsz, GDN_HEADS, GDN_DK, GDN_DV), F32)
    _, o = lax.scan(step, s0, (to_chunks(q), to_chunks(k), to_chunks(v), to_chunks(beta), to_chunks(g)))
    return from_chunks(o)


def ssd(x, Bm, Cm, dt_raw, A_log, dt_bias, D):
    bsz, seq = x.shape[:2]
    hg = SSD_HEADS // SSD_GROUPS
    dt = jax.nn.softplus(dt_raw + dt_bias.astype(F32))
    a = (dt * -jnp.exp(A_log.astype(F32))).reshape(bsz, seq, SSD_GROUPS, hg)
    xdt = (x * dt[..., None]).reshape(bsz, seq, SSD_GROUPS, hg, SSD_P)
    causal = jnp.tril(jnp.ones((CHUNK, CHUNK), bool))

    def step(hs, inp):
        xi, bi, ci, ai = inp
        acum = jnp.cumsum(ai, axis=1)
        diff = acum[:, :, None] - acum[:, None]
        lmat = jnp.exp(jnp.where(causal[None, :, :, None, None], diff, -jnp.inf))
        cb = jnp.einsum('bign,bjgn->bijg', ci, bi)
        y = jnp.einsum('bijg,bijgh,bjghp->bighp', cb, lmat, xi)
        y = y + jnp.einsum('bign,bghpn->bighp', ci, hs) * jnp.exp(acum)[..., None]
        alast = acum[:, -1]
        wdec = jnp.exp(alast[:, None] - acum)
        hs = hs * jnp.exp(alast)[..., None, None] + jnp.einsum('bjgn,bjgh,bjghp->bghpn', bi, wdec, xi)
        return hs, y

    h0 = jnp.zeros((bsz, SSD_GROUPS, hg, SSD_P, SSD_N), F32)
    _, y = lax.scan(step, h0, (to_chunks(xdt), to_chunks(Bm), to_chunks(Cm), to_chunks(a)))
    y = from_chunks(y).reshape(bsz, seq, SSD_HEADS, SSD_P)
    return y + x * D.astype(F32)[:, None]


def retention(q, k, v):
    bsz = q.shape[0]
    lg = jnp.log(1.0 - 2.0 ** (-5.0 - jnp.arange(RET_HEADS, dtype=F32)))
    idx = jnp.arange(CHUNK, dtype=F32)
    rel = idx[:, None] - idx[None, :]
    dmat = jnp.where(rel[None] >= 0, jnp.exp(jnp.maximum(rel, 0.0)[None] * lg[:, None, None]), 0.0)
    qdec = jnp.exp((idx[:, None] + 1.0) * lg[None, :])
    kdec = jnp.exp((CHUNK - 1.0 - idx)[:, None] * lg[None, :])
    cdec = jnp.exp(CHUNK * lg)
    k = k * (RET_DK ** -0.5)

    def step(R, inp):
        qi, ki, vi = inp
        s = jnp.einsum('bihd,bjhd->bhij', qi, ki) * dmat
        o = (jnp.einsum('bhij,bjhv->bihv', s, vi)
             + jnp.einsum('bihd,bhdv->bihv', qi, R) * qdec[None, :, :, None])
        R = R * cdec[None, :, None, None] + jnp.einsum('bjhd,bjhv->bhdv', ki * kdec[None, :, :, None], vi)
        return R, o

    r0 = jnp.zeros((bsz, RET_HEADS, RET_DK, RET_DV), F32)
    _, o = lax.scan(step, r0, (to_chunks(q), to_chunks(k), to_chunks(v)))
    return from_chunks(o)


def hybrid_layer(x, pre_norm, post_norm, w_in, gdn_conv, gdn_A_log, gdn_dt_bias, gdn_norm,
                 ssd_conv, ssd_conv_b, ssd_A_log, ssd_dt_bias, ssd_D, ssd_norm, ret_norm, w_out, pos):
    bsz, seq, _ = x.shape
    h = rmsnorm(x, pre_norm)
    proj = jnp.einsum('bld,de->ble', h, w_in.astype(F32))
    (gq, gk, gv, gz, gb, ga, sx, sB, sC, sz, sdt, rq, rk, rv, rg) = split_cols(proj, IN_SIZES)

    qkv = jax.nn.silu(causal_conv(jnp.concatenate([gq, gk, gv], axis=-1), gdn_conv))
    gq, gk, gv = split_cols(qkv, [GDN_HEADS * GDN_DK, GDN_HEADS * GDN_DK, GDN_W])
    o_a = gated_deltanet(gq.reshape(bsz, seq, GDN_HEADS, GDN_DK), gk.reshape(bsz, seq, GDN_HEADS, GDN_DK),
                         gv.reshape(bsz, seq, GDN_HEADS, GDN_DV), gb, ga, gdn_A_log, gdn_dt_bias)
    o_a = rmsnorm(o_a, gdn_norm) * jax.nn.silu(gz.reshape(bsz, seq, GDN_HEADS, GDN_DV))

    xbc = jax.nn.silu(causal_conv(jnp.concatenate([sx, sB, sC], axis=-1), ssd_conv) + ssd_conv_b.astype(F32))
    sx, sB, sC = split_cols(xbc, [SSD_W, SSD_GROUPS * SSD_N, SSD_GROUPS * SSD_N])
    y_b = ssd(sx.reshape(bsz, seq, SSD_HEADS, SSD_P), sB.reshape(bsz, seq, SSD_GROUPS, SSD_N),
              sC.reshape(bsz, seq, SSD_GROUPS, SSD_N), sdt, ssd_A_log, ssd_dt_bias, ssd_D)
    y_b = y_b * jax.nn.silu(sz.reshape(bsz, seq, SSD_HEADS, SSD_P))
    o_b = rmsnorm(y_b.reshape(bsz, seq, SSD_GROUPS, SSD_W // SSD_GROUPS),
                  ssd_norm.reshape(SSD_GROUPS, SSD_W // SSD_GROUPS))

    q_c = rotary(rq.reshape(bsz, seq, RET_HEADS, RET_DK), pos)
    k_c = rotary(rk.reshape(bsz, seq, RET_HEADS, RET_DK), pos)
    o_c = retention(q_c, k_c, rv.reshape(bsz, seq, RET_HEADS, RET_DV))
    o_c = rmsnorm(o_c, ret_norm) * jax.nn.silu(rg.reshape(bsz, seq, RET_HEADS, RET_DV))

    mixed = jnp.concatenate([o_a.reshape(bsz, seq, GDN_W), o_b.reshape(bsz, seq, SSD_W),
                             o_c.reshape(bsz, seq, RET_W)], axis=-1)
    out = jnp.einsum('ble,ed->bld', mixed, w_out.astype(F32))
    return x + rmsnorm(out, post_norm).astype(x.dtype)


def _fwd_setup_inputs(seed: int = 0) -> dict:
    key = jax.random.key(seed)
    ks = jax.random.split(key, 16)
    nrm = jax.random.normal

    def inv_softplus_dt(k, shape):
        dt = jnp.exp(jax.random.uniform(k, shape, F32, np.log(1e-3), np.log(1e-1)))
        return dt + jnp.log(-jnp.expm1(-dt))

    return {
        "x": nrm(ks[0], (BATCH, SEQ, D_MODEL), F32),
        "pre_norm": 1.0 + 0.05 * nrm(ks[1], (DEPTH, D_MODEL), F32),
        "post_norm": 1.0 + 0.05 * nrm(ks[2], (DEPTH, D_MODEL), F32),
        "w_in": nrm(ks[3], (DEPTH, D_MODEL, N_IN), F32) * D_MODEL ** -0.5,
        "gdn_conv": nrm(ks[4], (DEPTH, CONV_W, GDN_CONV_CH), F32) * CONV_W ** -0.5,
        "gdn_A_log": jnp.log(jax.random.uniform(ks[5], (DEPTH, GDN_HEADS), F32, 1.0, 16.0)),
        "gdn_dt_bias": inv_softplus_dt(ks[6], (DEPTH, GDN_HEADS)),
        "gdn_norm": 1.0 + 0.05 * nrm(ks[7], (DEPTH, GDN_DV), F32),
        "ssd_conv": nrm(ks[8], (DEPTH, CONV_W, SSD_CONV_CH), F32) * CONV_W ** -0.5,
        "ssd_conv_b": 0.02 * nrm(ks[9], (DEPTH, SSD_CONV_CH), F32),
        "ssd_A_log": jnp.log(jax.random.uniform(ks[10], (DEPTH, SSD_HEADS), F32, 1.0, 16.0)),
        "ssd_dt_bias": inv_softplus_dt(ks[11], (DEPTH, SSD_HEADS)),
        "ssd_D": 1.0 + 0.1 * nrm(ks[12], (DEPTH, SSD_HEADS), F32),
        "ssd_norm": 1.0 + 0.05 * nrm(ks[13], (DEPTH, SSD_W), F32),
        "ret_norm": 1.0 + 0.05 * nrm(ks[14], (DEPTH, RET_DV), F32),
        "w_out": nrm(ks[15], (DEPTH, MIX_W, D_MODEL), F32) * MIX_W ** -0.5,
    }


def _fwd_reference(x, pre_norm, post_norm, w_in, gdn_conv, gdn_A_log, gdn_dt_bias, gdn_norm,
              ssd_conv, ssd_conv_b, ssd_A_log, ssd_dt_bias, ssd_D, ssd_norm, ret_norm, w_out):
    pos = jnp.arange(x.shape[1], dtype=jnp.int32)
    for l in range(DEPTH):
        x = hybrid_layer(x, pre_norm[l], post_norm[l], w_in[l], gdn_conv[l], gdn_A_log[l], gdn_dt_bias[l],
                         gdn_norm[l], ssd_conv[l], ssd_conv_b[l], ssd_A_log[l], ssd_dt_bias[l], ssd_D[l],
                         ssd_norm[l], ret_norm[l], w_out[l], pos)
    return x


import jax as _jax
import jax.numpy as _jnp

TWIN_FORMAT = 'train_step'
FWD_PARAMS = ['x', 'pre_norm', 'post_norm', 'w_in', 'gdn_conv', 'gdn_A_log', 'gdn_dt_bias', 'gdn_norm', 'ssd_conv', 'ssd_conv_b', 'ssd_A_log', 'ssd_dt_bias', 'ssd_D', 'ssd_norm', 'ret_norm', 'w_out']
TWIN_WEIGHTS = ['pre_norm', 'post_norm', 'w_in', 'gdn_conv', 'gdn_A_log', 'gdn_dt_bias', 'gdn_norm', 'ssd_conv', 'ssd_conv_b', 'ssd_A_log', 'ssd_dt_bias', 'ssd_D', 'ssd_norm', 'ret_norm', 'w_out']
TWIN_DIFF_INPUT = 'x'
TWIN_INPUTS = ['x', 'pre_norm', 'post_norm', 'w_in', 'gdn_conv', 'gdn_A_log', 'gdn_dt_bias', 'gdn_norm', 'ssd_conv', 'ssd_conv_b', 'ssd_A_log', 'ssd_dt_bias', 'ssd_D', 'ssd_norm', 'ret_norm', 'w_out', 'loss_target', 'm_pre_norm', 'm_post_norm', 'm_w_in', 'm_gdn_conv', 'm_gdn_A_log', 'm_gdn_dt_bias', 'm_gdn_norm', 'm_ssd_conv', 'm_ssd_conv_b', 'm_ssd_A_log', 'm_ssd_dt_bias', 'm_ssd_D', 'm_ssd_norm', 'm_ret_norm', 'm_w_out', 'v_pre_norm', 'v_post_norm', 'v_w_in', 'v_gdn_conv', 'v_gdn_A_log', 'v_gdn_dt_bias', 'v_gdn_norm', 'v_ssd_conv', 'v_ssd_conv_b', 'v_ssd_A_log', 'v_ssd_dt_bias', 'v_ssd_D', 'v_ssd_norm', 'v_ret_norm', 'v_w_out']
TWIN_OUTPUTS = ['loss', 'grad_x', 'grad_pre_norm', 'grad_post_norm', 'grad_w_in', 'grad_gdn_conv', 'grad_gdn_A_log', 'grad_gdn_dt_bias', 'grad_gdn_norm', 'grad_ssd_conv', 'grad_ssd_conv_b', 'grad_ssd_A_log', 'grad_ssd_dt_bias', 'grad_ssd_D', 'grad_ssd_norm', 'grad_ret_norm', 'grad_w_out', 'delta_pre_norm', 'delta_post_norm', 'delta_w_in', 'delta_gdn_conv', 'delta_gdn_A_log', 'delta_gdn_dt_bias', 'delta_gdn_norm', 'delta_ssd_conv', 'delta_ssd_conv_b', 'delta_ssd_A_log', 'delta_ssd_dt_bias', 'delta_ssd_D', 'delta_ssd_norm', 'delta_ret_norm', 'delta_w_out', 'new_m_pre_norm', 'new_m_post_norm', 'new_m_w_in', 'new_m_gdn_conv', 'new_m_gdn_A_log', 'new_m_gdn_dt_bias', 'new_m_gdn_norm', 'new_m_ssd_conv', 'new_m_ssd_conv_b', 'new_m_ssd_A_log', 'new_m_ssd_dt_bias', 'new_m_ssd_D', 'new_m_ssd_norm', 'new_m_ret_norm', 'new_m_w_out', 'new_v_pre_norm', 'new_v_post_norm', 'new_v_w_in', 'new_v_gdn_conv', 'new_v_gdn_A_log', 'new_v_gdn_dt_bias', 'new_v_gdn_norm', 'new_v_ssd_conv', 'new_v_ssd_conv_b', 'new_v_ssd_A_log', 'new_v_ssd_dt_bias', 'new_v_ssd_D', 'new_v_ssd_norm', 'new_v_ret_norm', 'new_v_w_out']
TWIN_LEAF_KINDS = {'loss': 'loss', 'grad_x': 'grad_x', 'grad_pre_norm': 'grad_w', 'grad_post_norm': 'grad_w', 'grad_w_in': 'grad_w', 'grad_gdn_conv': 'grad_w', 'grad_gdn_A_log': 'grad_w', 'grad_gdn_dt_bias': 'grad_w', 'grad_gdn_norm': 'grad_w', 'grad_ssd_conv': 'grad_w', 'grad_ssd_conv_b': 'grad_w', 'grad_ssd_A_log': 'grad_w', 'grad_ssd_dt_bias': 'grad_w', 'grad_ssd_D': 'grad_w', 'grad_ssd_norm': 'grad_w', 'grad_ret_norm': 'grad_w', 'grad_w_out': 'grad_w', 'delta_pre_norm': 'delta_w', 'delta_post_norm': 'delta_w', 'delta_w_in': 'delta_w', 'delta_gdn_conv': 'delta_w', 'delta_gdn_A_log': 'delta_w', 'delta_gdn_dt_bias': 'delta_w', 'delta_gdn_norm': 'delta_w', 'delta_ssd_conv': 'delta_w', 'delta_ssd_conv_b': 'delta_w', 'delta_ssd_A_log': 'delta_w', 'delta_ssd_dt_bias': 'delta_w', 'delta_ssd_D': 'delta_w', 'delta_ssd_norm': 'delta_w', 'delta_ret_norm': 'delta_w', 'delta_w_out': 'delta_w', 'new_m_pre_norm': 'new_m', 'new_m_post_norm': 'new_m', 'new_m_w_in': 'new_m', 'new_m_gdn_conv': 'new_m', 'new_m_gdn_A_log': 'new_m', 'new_m_gdn_dt_bias': 'new_m', 'new_m_gdn_norm': 'new_m', 'new_m_ssd_conv': 'new_m', 'new_m_ssd_conv_b': 'new_m', 'new_m_ssd_A_log': 'new_m', 'new_m_ssd_dt_bias': 'new_m', 'new_m_ssd_D': 'new_m', 'new_m_ssd_norm': 'new_m', 'new_m_ret_norm': 'new_m', 'new_m_w_out': 'new_m', 'new_v_pre_norm': 'new_v', 'new_v_post_norm': 'new_v', 'new_v_w_in': 'new_v', 'new_v_gdn_conv': 'new_v', 'new_v_gdn_A_log': 'new_v', 'new_v_gdn_dt_bias': 'new_v', 'new_v_gdn_norm': 'new_v', 'new_v_ssd_conv': 'new_v', 'new_v_ssd_conv_b': 'new_v', 'new_v_ssd_A_log': 'new_v', 'new_v_ssd_dt_bias': 'new_v', 'new_v_ssd_D': 'new_v', 'new_v_ssd_norm': 'new_v', 'new_v_ret_norm': 'new_v', 'new_v_w_out': 'new_v'}


def _forward(args):
    return _fwd_reference(*[args[k] for k in FWD_PARAMS])


def _output_shape():
    out = _jax.eval_shape(lambda: _forward(_fwd_setup_inputs(0)))
    return out.shape, out.dtype

N_MICROBATCH = 1
ADAM_LR = 0.001
ADAM_B1 = 0.9
ADAM_B2 = 0.999
ADAM_EPS = 1e-08
ADAM_WD = 0.01
ADAM_STEP = 10
PER_EXAMPLE_BATCH_AXIS = {'x': 0, 'loss_target': 0}
SHARED_INPUTS = []
_WEIGHT_DTYPES = {'pre_norm': _jnp.float32, 'post_norm': _jnp.float32, 'w_in': _jnp.float32, 'gdn_conv': _jnp.float32, 'gdn_A_log': _jnp.float32, 'gdn_dt_bias': _jnp.float32, 'gdn_norm': _jnp.float32, 'ssd_conv': _jnp.float32, 'ssd_conv_b': _jnp.float32, 'ssd_A_log': _jnp.float32, 'ssd_dt_bias': _jnp.float32, 'ssd_D': _jnp.float32, 'ssd_norm': _jnp.float32, 'ret_norm': _jnp.float32, 'w_out': _jnp.float32}
MOMENT_SCALE = {'pre_norm': 9.733165e-01, 'post_norm': 6.397206e+01, 'w_in': 4.194637e-01, 'gdn_conv': 3.370524e-01, 'gdn_A_log': 2.597771e+00, 'gdn_dt_bias': 2.531148e+00, 'gdn_norm': 1.989052e+00, 'ssd_conv': 8.835521e-01, 'ssd_conv_b': 3.656260e+00, 'ssd_A_log': 8.754562e+00, 'ssd_dt_bias': 1.593543e+00, 'ssd_D': 6.556549e+00, 'ssd_norm': 1.657786e+00, 'ret_norm': 5.943003e-01, 'w_out': 1.683692e+00}


def _to_microbatches(a, axis):
    t = _jnp.moveaxis(a, axis, 0)
    t = t.reshape((N_MICROBATCH, t.shape[0] // N_MICROBATCH) + t.shape[1:])
    return _jnp.moveaxis(t, 1, axis + 1)


def setup_inputs(seed: int = 0) -> dict:
    inp = _fwd_setup_inputs(seed)
    key = _jax.random.fold_in(_jax.random.key(seed), 7919)
    shape, _ = _output_shape()
    out = dict(inp)
    out["loss_target"] = _jax.random.normal(_jax.random.fold_in(key, 0), shape, _jnp.float32)
    for i, name in enumerate(TWIN_WEIGHTS):
        w = inp[name].astype(_jnp.float32)
        if MOMENT_SCALE is None:
            s = _jnp.sqrt(_jnp.mean(_jnp.square(w)) + 1e-30)
        else:
            s = MOMENT_SCALE[name]
        km, kv = _jax.random.split(_jax.random.fold_in(key, i + 1))
        out[name] = w
        out["m_" + name] = s * _jax.random.normal(km, w.shape, _jnp.float32)
        out["v_" + name] = (s * s) * _jax.random.uniform(kv, w.shape, _jnp.float32, 0.5, 1.5)
    if N_MICROBATCH > 1:
        for name, axis in PER_EXAMPLE_BATCH_AXIS.items():
            out[name] = _to_microbatches(out[name], axis)
    return {'x': out['x'], 'pre_norm': out['pre_norm'], 'post_norm': out['post_norm'], 'w_in': out['w_in'], 'gdn_conv': out['gdn_conv'], 'gdn_A_log': out['gdn_A_log'], 'gdn_dt_bias': out['gdn_dt_bias'], 'gdn_norm': out['gdn_norm'], 'ssd_conv': out['ssd_conv'], 'ssd_conv_b': out['ssd_conv_b'], 'ssd_A_log': out['ssd_A_log'], 'ssd_dt_bias': out['ssd_dt_bias'], 'ssd_D': out['ssd_D'], 'ssd_norm': out['ssd_norm'], 'ret_norm': out['ret_norm'], 'w_out': out['w_out'], 'loss_target': out['loss_target'], 'm_pre_norm': out['m_pre_norm'], 'm_post_norm': out['m_post_norm'], 'm_w_in': out['m_w_in'], 'm_gdn_conv': out['m_gdn_conv'], 'm_gdn_A_log': out['m_gdn_A_log'], 'm_gdn_dt_bias': out['m_gdn_dt_bias'], 'm_gdn_norm': out['m_gdn_norm'], 'm_ssd_conv': out['m_ssd_conv'], 'm_ssd_conv_b': out['m_ssd_conv_b'], 'm_ssd_A_log': out['m_ssd_A_log'], 'm_ssd_dt_bias': out['m_ssd_dt_bias'], 'm_ssd_D': out['m_ssd_D'], 'm_ssd_norm': out['m_ssd_norm'], 'm_ret_norm': out['m_ret_norm'], 'm_w_out': out['m_w_out'], 'v_pre_norm': out['v_pre_norm'], 'v_post_norm': out['v_post_norm'], 'v_w_in': out['v_w_in'], 'v_gdn_conv': out['v_gdn_conv'], 'v_gdn_A_log': out['v_gdn_A_log'], 'v_gdn_dt_bias': out['v_gdn_dt_bias'], 'v_gdn_norm': out['v_gdn_norm'], 'v_ssd_conv': out['v_ssd_conv'], 'v_ssd_conv_b': out['v_ssd_conv_b'], 'v_ssd_A_log': out['v_ssd_A_log'], 'v_ssd_dt_bias': out['v_ssd_dt_bias'], 'v_ssd_D': out['v_ssd_D'], 'v_ssd_norm': out['v_ssd_norm'], 'v_ret_norm': out['v_ret_norm'], 'v_w_out': out['v_w_out']}


def _loss(weights, diff, rest, loss_target):
    with _jax.named_scope("forward"):
        args = {**rest, TWIN_DIFF_INPUT: diff, **{k: w.astype(_WEIGHT_DTYPES[k]) for k, w in weights.items()}}
        y = _forward(args)
    with _jax.named_scope("loss_head"):
        err = _jnp.square(y.astype(_jnp.float32) - loss_target)
        return 0.5 * _jnp.sum(_jnp.mean(err, axis=-1)) if err.ndim else 0.5 * err


def _adamw(w, g, m, v):
    m = ADAM_B1 * m + (1.0 - ADAM_B1) * g
    v = ADAM_B2 * v + (1.0 - ADAM_B2) * _jnp.square(g)
    m_hat = m / (1.0 - ADAM_B1 ** ADAM_STEP)
    v_hat = v / (1.0 - ADAM_B2 ** ADAM_STEP)
    delta = -ADAM_LR * (m_hat / (_jnp.sqrt(v_hat) + ADAM_EPS) + ADAM_WD * w)
    return delta, m, v


def reference(x, pre_norm, post_norm, w_in, gdn_conv, gdn_A_log, gdn_dt_bias, gdn_norm, ssd_conv, ssd_conv_b, ssd_A_log, ssd_dt_bias, ssd_D, ssd_norm, ret_norm, w_out, loss_target, m_pre_norm, m_post_norm, m_w_in, m_gdn_conv, m_gdn_A_log, m_gdn_dt_bias, m_gdn_norm, m_ssd_conv, m_ssd_conv_b, m_ssd_A_log, m_ssd_dt_bias, m_ssd_D, m_ssd_norm, m_ret_norm, m_w_out, v_pre_norm, v_post_norm, v_w_in, v_gdn_conv, v_gdn_A_log, v_gdn_dt_bias, v_gdn_norm, v_ssd_conv, v_ssd_conv_b, v_ssd_A_log, v_ssd_dt_bias, v_ssd_D, v_ssd_norm, v_ret_norm, v_w_out):
    given = dict(x=x, pre_norm=pre_norm, post_norm=post_norm, w_in=w_in, gdn_conv=gdn_conv, gdn_A_log=gdn_A_log, gdn_dt_bias=gdn_dt_bias, gdn_norm=gdn_norm, ssd_conv=ssd_conv, ssd_conv_b=ssd_conv_b, ssd_A_log=ssd_A_log, ssd_dt_bias=ssd_dt_bias, ssd_D=ssd_D, ssd_norm=ssd_norm, ret_norm=ret_norm, w_out=w_out, loss_target=loss_target, m_pre_norm=m_pre_norm, m_post_norm=m_post_norm, m_w_in=m_w_in, m_gdn_conv=m_gdn_conv, m_gdn_A_log=m_gdn_A_log, m_gdn_dt_bias=m_gdn_dt_bias, m_gdn_norm=m_gdn_norm, m_ssd_conv=m_ssd_conv, m_ssd_conv_b=m_ssd_conv_b, m_ssd_A_log=m_ssd_A_log, m_ssd_dt_bias=m_ssd_dt_bias, m_ssd_D=m_ssd_D, m_ssd_norm=m_ssd_norm, m_ret_norm=m_ret_norm, m_w_out=m_w_out, v_pre_norm=v_pre_norm, v_post_norm=v_post_norm, v_w_in=v_w_in, v_gdn_conv=v_gdn_conv, v_gdn_A_log=v_gdn_A_log, v_gdn_dt_bias=v_gdn_dt_bias, v_gdn_norm=v_gdn_norm, v_ssd_conv=v_ssd_conv, v_ssd_conv_b=v_ssd_conv_b, v_ssd_A_log=v_ssd_A_log, v_ssd_dt_bias=v_ssd_dt_bias, v_ssd_D=v_ssd_D, v_ssd_norm=v_ssd_norm, v_ret_norm=v_ret_norm, v_w_out=v_w_out)
    weights = {n: given[n] for n in TWIN_WEIGHTS}
    shared = {n: given[n] for n in SHARED_INPUTS}
    per_example = {n: given[n] for n in ['x']}
    grad_fn = _jax.value_and_grad(_loss, argnums=(0, 1))

    def one_microbatch(ex, loss_target):
        ex = dict(ex)
        diff = ex.pop(TWIN_DIFF_INPUT)
        return grad_fn(weights, diff, {**shared, **ex}, loss_target)

    if N_MICROBATCH == 1:
        loss, (grad_w, grad_x) = one_microbatch(per_example, given["loss_target"])
    else:
        def body(carry, xs):
            loss_sum, grad_sum = carry
            l_k, (gw_k, gx_k) = one_microbatch(xs[0], xs[1])
            with _jax.named_scope("update"):
                return (loss_sum + l_k, _jax.tree.map(_jnp.add, grad_sum, gw_k)), gx_k

        init = (_jnp.zeros((), _jnp.float32), _jax.tree.map(_jnp.zeros_like, weights))
        (loss, grad_w), grad_x = _jax.lax.scan(body, init, (per_example, given["loss_target"]))
    with _jax.named_scope("update"):
        delta_w, new_m, new_v = {}, {}, {}
        for n in TWIN_WEIGHTS:
            delta_w[n], new_m[n], new_v[n] = _adamw(weights[n], grad_w[n], given["m_" + n], given["v_" + n])
    return (loss, grad_x, *[grad_w[n] for n in TWIN_WEIGHTS], *[delta_w[n] for n in TWIN_WEIGHTS],
            *[new_m[n] for n in TWIN_WEIGHTS], *[new_v[n] for n in TWIN_WEIGHTS])
```

```python
import functools
import math

import jax
import jax.numpy as jnp
from jax import lax
from jax.experimental import pallas as pl
from jax.experimental.pallas import tpu as pltpu

F32 = jnp.float32
BF16 = jnp.bfloat16
HI = lax.Precision.HIGHEST

D_MODEL = 1024
DEPTH = 2
CH = 64
CONV_W = 4
EPS = 1e-6
GDN_H, GDN_D = 4, 128
SSD_H, SSD_P, SSD_N, SSD_G = 16, 64, 128, 2
SSD_W = SSD_H * SSD_P
RET_H, RET_D = 4, 128
ROPE_BASE = 10000.0
N_IN = 6680
NEG = -1e30

VMEM_LIMIT = 56 * 1024 * 1024


def _dot(a, b):
    return jnp.dot(a.astype(BF16), b.astype(BF16), preferred_element_type=F32)


def _dot_nt(a, b):
    return lax.dot_general(a.astype(BF16), b.astype(BF16), (((1,), (1,)), ((), ())), preferred_element_type=F32)


def _dot_tn(a, b):
    return lax.dot_general(a.astype(BF16), b.astype(BF16), (((0,), (0,)), ((), ())), preferred_element_type=F32)


def _dotx(a, b):
    return jnp.dot(a, b, preferred_element_type=F32, precision=HI)


def _dotx_tn(a, b):
    return lax.dot_general(a, b, (((0,), (0,)), ((), ())), preferred_element_type=F32, precision=HI)


def _sigmoid(x):
    return jax.nn.sigmoid(x)


def _silu(x):
    return x * _sigmoid(x)


def _dsilu(x):
    s = _sigmoid(x)
    return s * (1.0 + x * (1.0 - s))


def _softplus(x):
    return jnp.maximum(x, 0.0) + jnp.log1p(jnp.exp(-jnp.abs(x)))


def _iota2(shape, dim):
    return lax.broadcasted_iota(jnp.int32, shape, dim)


def _chunk_tri(tb, upper=False):
    r = _iota2((tb, tb), 0)
    c = _iota2((tb, tb), 1)
    same = jnp.right_shift(r, 6) == jnp.right_shift(c, 6)
    return (same & ((c >= r) if upper else (c <= r))).astype(F32)


def _masks():
    r = _iota2((CH, CH), 0)
    c = _iota2((CH, CH), 1)
    return r >= c, r > c, (r == c).astype(F32)


def _inv_unit_lower(a, eye):
    x = eye - a
    p = a
    for _ in range(5):
        p = _dotx(p, p)
        x = x + _dotx(x, p)
    return x


def _put_lane(col, lane_idx, width=128):
    lane = _iota2((col.shape[0], width), 1)
    return jnp.where(lane == lane_idx, col, 0.0)


def _conv_taps(raw, halo8, tb):
    ext = jnp.concatenate([halo8, raw], axis=0)
    return [raw] + [pltpu.roll(ext, s, axis=0)[8:] for s in (1, 2, 3)]


def _conv_back(dpre, nxt8, tb):
    ext = jnp.concatenate([dpre, nxt8], axis=0)
    return [dpre] + [pltpu.roll(ext, tb + 8 - s, axis=0)[:tb] for s in (1, 2, 3)]


def _rms_fwd(o, w, n):
    r = lax.rsqrt(jnp.sum(o * o, axis=-1, keepdims=True) * (1.0 / n) + EPS)
    on = o * r
    return on, r, on * w


def _rms_bwd(dy, on, r, w, n):
    don = dy * w
    return r * (don - on * (jnp.sum(don * on, axis=-1, keepdims=True) * (1.0 / n))), dy * on


def _put_cols(v, g, gw):
    z = jnp.zeros_like(v)
    return jnp.concatenate([v, z] if g == 0 else [z, v], axis=1)


def _gdn_common(pg_ref, halo8, sm, cw, prm, tb):
    raw = pg_ref[:, 0:1536]
    taps = _conv_taps(raw, halo8, tb)
    pre = taps[0] * cw[3:4, :] + taps[1] * cw[2:3, :] + taps[2] * cw[1:2, :] + taps[3] * cw[0:1, :]
    act = _silu(pre)
    beta = _sigmoid(sm)
    sp_in = sm + prm[1:2, :]
    g = -jnp.exp(prm[0:1, :]) * _softplus(sp_in)
    gc = _dotx(_chunk_tri(tb), g)
    return raw, taps, pre, act, beta, sp_in, g, gc


def _gdn_head(act, beta, gc, gct, eg_all, c, h, masks):
    causal, strict, eye = masks
    r0 = c * CH
    rows = slice(r0, r0 + CH)
    qr = act[rows, h * 128:(h + 1) * 128]
    kr = act[rows, 512 + h * 128:512 + (h + 1) * 128]
    vh = act[rows, 1024 + h * 128:1024 + (h + 1) * 128]
    rq = lax.rsqrt(jnp.sum(qr * qr, axis=-1, keepdims=True) + EPS)
    rk = lax.rsqrt(jnp.sum(kr * kr, axis=-1, keepdims=True) + EPS)
    qn = qr * rq
    kh = kr * rk
    qh = qn * (GDN_D ** -0.5)
    bh = beta[rows, h:h + 1]
    gcol = gc[rows, 4 + h:5 + h]
    grow = gct[4 + h:5 + h, r0:r0 + CH]
    eg = eg_all[rows, 4 + h:5 + h]
    dmat = gcol - grow
    decay = jnp.exp(jnp.where(causal, dmat, NEG))
    kb = kh * bh
    pm = _dot_nt(kb, kh)
    a = jnp.where(strict, pm * decay, 0.0)
    t = _inv_unit_lower(a, eye)
    vb = vh * bh
    kg = kb * eg
    sol = _dotx(t, jnp.concatenate([vb, kg], axis=1))
    u, w = sol[:, :128], sol[:, 128:]
    qk = _dot_nt(qh, kh)
    attn = qk * decay
    glast = gc[r0 + CH - 1:r0 + CH, 4 + h:5 + h]
    kd_scale = jnp.exp(glast - gcol)
    kdec = kh * kd_scale
    return dict(qn=qn, rq=rq, kh=kh, rk=rk, qh=qh, vh=vh, bh=bh, gcol=gcol, eg=eg, decay=decay, kb=kb, a=a, t=t,
                vb=vb, kg=kg, u=u, w=w, attn=attn, glast=glast, kd_scale=kd_scale, kdec=kdec, qg=qh * eg)


def _make_gdn_fwd(seq, tb):
    ncb = tb // CH
    nb = seq // tb

    def body(pg_ref, sm_ref, cw_ref, prm_ref, nw_ref, oa_ref, st_ref, s_scr, halo_scr):
        @pl.when(pl.program_id(0) == 0)
        def _():
            s_scr[...] = jnp.zeros_like(s_scr)
            halo_scr[...] = jnp.zeros_like(halo_scr)

        masks = _masks()
        sm = sm_ref[...]
        raw, _, _, act, beta, _, _, gc = _gdn_common(pg_ref, halo_scr[...], sm, cw_ref[...], prm_ref[...], tb)
        halo_scr[...] = raw[tb - 8:tb, :]
        gct = gc.T
        eg_all = jnp.exp(gc)
        nw = nw_ref[0:1, :]
        for c in range(ncb):
            rows = slice(c * CH, (c + 1) * CH)
            for h in range(GDN_H):
                d = _gdn_head(act, beta, gc, gct, eg_all, c, h, masks)
                s = s_scr[h]
                st_ref[c, h] = s
                v_new = d["u"] - _dot(d["w"], s)
                o = _dot(d["qg"], s) + _dot(d["attn"], v_new)
                s_scr[h] = s * jnp.exp(d["glast"]) + _dot_tn(d["kdec"], v_new)
                _, _, y = _rms_fwd(o, nw, GDN_D)
                z = pg_ref[rows, 1536 + h * 128:1536 + (h + 1) * 128]
                oa_ref[rows, h * 128:(h + 1) * 128] = y * _silu(z)

    def call(pg, sm, cw, prm, nw):
        return pl.pallas_call(
            body,
            grid=(nb,),
            in_specs=[
                pl.BlockSpec((tb, 2048), lambda i: (i, 0)),
                pl.BlockSpec((tb, 128), lambda i: (i, 0)),
                pl.BlockSpec((8, 1536), lambda i: (0, 0)),
                pl.BlockSpec((8, 128), lambda i: (0, 0)),
                pl.BlockSpec((8, 128), lambda i: (0, 0)),
            ],
            out_specs=[
                pl.BlockSpec((tb, 512), lambda i: (i, 0)),
                pl.BlockSpec((ncb, GDN_H, 128, 128), lambda i: (i, 0, 0, 0)),
            ],
            out_shape=[
                jax.ShapeDtypeStruct((seq, 512), F32),
                jax.ShapeDtypeStruct((seq // CH, GDN_H, 128, 128), F32),
            ],
            scratch_shapes=[pltpu.VMEM((GDN_H, 128, 128), F32), pltpu.VMEM((8, 1536), F32)],
            compiler_params=pltpu.CompilerParams(dimension_semantics=("arbitrary",), vmem_limit_bytes=VMEM_LIMIT),
            name="gdn_fwd",
        )(pg, sm, cw, prm, nw)

    return call


def _make_gdn_bwd(seq, tb):
    ncb = tb // CH
    nb = seq // tb
    hb = tb // 8

    def body(pg_ref, prev_ref, sm_ref, cw_ref, prm_ref, nw_ref, st_ref, doa_ref,
             dpg_ref, dsm_ref, dcw_ref, dprm_ref, dnw_ref, ds_scr, nxt_scr):
        i = pl.program_id(0)

        @pl.when(i == 0)
        def _():
            ds_scr[...] = jnp.zeros_like(ds_scr)
            nxt_scr[...] = jnp.zeros_like(nxt_scr)
            dcw_ref[...] = jnp.zeros_like(dcw_ref)
            dprm_ref[...] = jnp.zeros_like(dprm_ref)
            dnw_ref[...] = jnp.zeros_like(dnw_ref)

        masks = _masks()
        causal, strict, eye = masks
        sm = sm_ref[...]
        cw = cw_ref[...]
        prm = prm_ref[...]
        halo8 = jnp.where(i == nb - 1, 0.0, prev_ref[...])
        raw, taps, pre, act, beta, sp_in, g, gc = _gdn_common(pg_ref, halo8, sm, cw, prm, tb)
        gct = gc.T
        eg_all = jnp.exp(gc)
        nw = nw_ref[0:1, :]
        ones_cl = jnp.ones((CH, 128), F32)
        row_id = _iota2((CH, 1), 0)

        dact_q = [[None] * GDN_H for _ in range(ncb)]
        dact_k = [[None] * GDN_H for _ in range(ncb)]
        dact_v = [[None] * GDN_H for _ in range(ncb)]
        dz_l = [[None] * GDN_H for _ in range(ncb)]
        dgc_l = [None] * ncb
        dbeta_l = [None] * ncb
        dnw_acc = jnp.zeros((1, 128), F32)

        for c in reversed(range(ncb)):
            rows = slice(c * CH, (c + 1) * CH)
            dgc_c = jnp.zeros((CH, 128), F32)
            dbeta_c = jnp.zeros((CH, 128), F32)
            for h in range(GDN_H):
                d = _gdn_head(act, beta, gc, gct, eg_all, c, h, masks)
                s = st_ref[c, h]
                v_new = d["u"] - _dot(d["w"], s)
                o = _dot(d["qg"], s) + _dot(d["attn"], v_new)
                z = pg_ref[rows, 1536 + h * 128:1536 + (h + 1) * 128]
                doa = doa_ref[rows, h * 128:(h + 1) * 128]
                on, r, y = _rms_fwd(o, nw, GDN_D)
                dz_l[c][h] = doa * y * _dsilu(z)
                do, dnw_rows = _rms_bwd(doa * _silu(z), on, r, nw, GDN_D)
                dnw_acc = dnw_acc + jnp.sum(dnw_rows, axis=0, keepdims=True)
                dsn = ds_scr[h]
                egl = jnp.exp(d["glast"])
                dqg = _dot_nt(do, s)
                dattn = _dot_nt(do, v_new)
                dvn = _dot_tn(d["attn"], do) + _dot(d["kdec"], dsn)
                dkdec = _dot_nt(v_new, dsn)
                dglast = egl * jnp.sum(jnp.sum(s * dsn, axis=1, keepdims=True), axis=0, keepdims=True)
                ds_scr[h] = egl * dsn + _dot_tn(d["qg"], do) - _dot_tn(d["w"], dvn)
                dw = -_dot_nt(dvn, s)
                drhs = _dotx_tn(d["t"], jnp.concatenate([dvn, dw], axis=1))
                dvb, dkg = drhs[:, :128], drhs[:, 128:]
                da = jnp.where(strict, -(_dot_nt(dvb, d["u"]) + _dot_nt(dkg, d["w"])), 0.0)
                dp = da * d["decay"]
                dq_m = dattn * d["decay"]
                m = da * d["a"] + dattn * d["attn"]
                dkb = _dot(dp, d["kh"]) + dkg * d["eg"]
                kdk = dkdec * d["kdec"]
                dk = _dot_tn(dp, d["kb"]) + _dot_tn(dq_m, d["qh"]) + dkdec * d["kd_scale"] + dkb * d["bh"]
                dq = _dot(dq_m, d["kh"]) + dqg * d["eg"]
                kdk_row = jnp.sum(kdk, axis=1, keepdims=True)
                dglast = dglast + jnp.sum(kdk_row, axis=0, keepdims=True)
                dgcol = (jnp.sum(m, axis=1, keepdims=True) - _dotx_tn(m, ones_cl)[:, 0:1]
                         + jnp.sum(dqg * d["qg"], axis=1, keepdims=True) + jnp.sum(dkg * d["kg"], axis=1, keepdims=True)
                         - kdk_row + jnp.where(row_id == CH - 1, dglast, 0.0))
                dbeta = jnp.sum(dkb * d["kh"], axis=1, keepdims=True) + jnp.sum(dvb * d["vh"], axis=1, keepdims=True)
                dgc_c = dgc_c + _put_lane(dgcol, 4 + h)
                dbeta_c = dbeta_c + _put_lane(dbeta, h)
                dn = dq * (GDN_D ** -0.5)
                dact_q[c][h] = d["rq"] * (dn - d["qn"] * jnp.sum(dn * d["qn"], axis=1, keepdims=True))
                dact_k[c][h] = d["rk"] * (dk - d["kh"] * jnp.sum(dk * d["kh"], axis=1, keepdims=True))
                dact_v[c][h] = dvb * d["bh"]
            dgc_l[c] = dgc_c
            dbeta_l[c] = dbeta_c

        dgc_all = jnp.concatenate(dgc_l, axis=0)
        dbeta_all = jnp.concatenate(dbeta_l, axis=0)
        dg = _dotx(_chunk_tri(tb, upper=True), dgc_all)
        neg_ea = -jnp.exp(prm[0:1, :])
        da_raw = dg * neg_ea * _sigmoid(sp_in)
        db_raw = dbeta_all * beta * (1.0 - beta)
        dsm_ref[...] = (da_raw + db_raw).astype(dsm_ref.dtype)
        lane8 = _iota2((8, 128), 1)
        sub8 = _iota2((8, 128), 0)
        dalog = jnp.sum(dg * g, axis=0, keepdims=True)
        ddtb = jnp.sum(da_raw, axis=0, keepdims=True)
        dprm_ref[...] += jnp.where(sub8 == 0, dalog, 0.0) + jnp.where(sub8 == 1, ddtb, 0.0)
        dnw_ref[...] += jnp.where(sub8 == 0, dnw_acc, 0.0)

        dact = jnp.concatenate(
            [jnp.concatenate([jnp.concatenate(dl[c], axis=1) for c in range(ncb)], axis=0)
             for dl in (dact_q, dact_k, dact_v)], axis=1)
        dpre = dact * _dsilu(pre)
        back = _conv_back(dpre, nxt_scr[...], tb)
        nxt_scr[...] = dpre[0:8, :]
        draw = back[0] * cw[3:4, :] + back[1] * cw[2:3, :] + back[2] * cw[1:2, :] + back[3] * cw[0:1, :]
        dpg_ref[:, 0:1536] = draw.astype(dpg_ref.dtype)
        dz = jnp.concatenate([jnp.concatenate(dz_l[c], axis=1) for c in range(ncb)], axis=0)
        dpg_ref[:, 1536:2048] = dz.astype(dpg_ref.dtype)
        sub_c = _iota2((8, 1536), 0)
        dcw_new = jnp.zeros((8, 1536), F32)
        for s_ in range(CONV_W):
            dcw_new = dcw_new + jnp.where(sub_c == 3 - s_, jnp.sum(dpre * taps[s_], axis=0, keepdims=True), 0.0)
        dcw_ref[...] += dcw_new

    def call(pg, sm, cw, prm, nw, st, doa):
        rev = lambda i: (nb - 1 - i, 0)
        const = lambda i: (0, 0)
        return pl.pallas_call(
            body,
            grid=(nb,),
            in_specs=[
                pl.BlockSpec((tb, 2048), rev),
                pl.BlockSpec((8, 1536), lambda i: (jnp.maximum((nb - 1 - i) * hb - 1, 0), 0)),
                pl.BlockSpec((tb, 128), rev),
                pl.BlockSpec((8, 1536), const),
                pl.BlockSpec((8, 128), const),
                pl.BlockSpec((8, 128), const),
                pl.BlockSpec((ncb, GDN_H, 128, 128), lambda i: (nb - 1 - i, 0, 0, 0)),
                pl.BlockSpec((tb, 512), rev),
            ],
            out_specs=[
                pl.BlockSpec((tb, 2048), rev),
                pl.BlockSpec((tb, 128), rev),
                pl.BlockSpec((8, 1536), const),
                pl.BlockSpec((8, 128), const),
                pl.BlockSpec((8, 128), const),
            ],
            out_shape=[
                jax.ShapeDtypeStruct((seq, 2048), BF16),
                jax.ShapeDtypeStruct((seq, 128), BF16),
                jax.ShapeDtypeStruct((8, 1536), F32),
                jax.ShapeDtypeStruct((8, 128), F32),
                jax.ShapeDtypeStruct((8, 128), F32),
            ],
            scratch_shapes=[pltpu.VMEM((GDN_H, 128, 128), F32), pltpu.VMEM((8, 1536), F32)],
            compiler_params=pltpu.CompilerParams(dimension_semantics=("arbitrary",), vmem_limit_bytes=VMEM_LIMIT),
            name="gdn_bwd",
        )(pg, pg, sm, cw, prm, nw, st, doa)

    return call


def _expand_mat():
    r = _iota2((128, SSD_W), 0)
    c = _iota2((128, SSD_W), 1)
    return (jnp.right_shift(c, 6) == r).astype(F32)


def _reduce_heads(v, e):
    return lax.dot_general(v, e, (((1,), (1,)), ((), ())), preferred_element_type=F32, precision=HI)


def _row8(v):
    return jnp.broadcast_to(v, (8, v.shape[1]))


def _ssd_common(ps_ref, halo8, ss, cw, cb, prm, tb):
    raw = ps_ref[:, 0:1536]
    taps = _conv_taps(raw, halo8, tb)
    pre = taps[0] * cw[3:4, :] + taps[1] * cw[2:3, :] + taps[2] * cw[1:2, :] + taps[3] * cw[0:1, :] + cb[0:1, :]
    act = _silu(pre)
    dt_in = ss + prm[1:2, :]
    dt = _softplus(dt_in)
    a = dt * (-jnp.exp(prm[0:1, :]))
    acum = _dotx(_chunk_tri(tb), a)
    e = _expand_mat()
    dt_e = _dotx(dt, e)
    xdt = act[:, 0:SSD_W] * dt_e
    ea_e = _dotx(jnp.exp(acum), e)
    d_e = _dotx(_row8(prm[2:3, :]), e)[0:1, :]
    return raw, taps, pre, act, dt_in, dt, a, acum, e, dt_e, xdt, ea_e, d_e


def _ssd_chunk(act, acum, act_t, e, c):
    r0 = c * CH
    rows = slice(r0, r0 + CH)
    alast = acum[r0 + CH - 1:r0 + CH, :]
    wdec = jnp.exp(alast - acum[rows, :])
    wd_e = _dotx(wdec, e)
    eal_e = _dotx(_row8(jnp.exp(alast)), e)[0:1, :]
    return rows, wd_e, eal_e


def _ssd_lmat(acum, act_t, c, h, causal):
    r0 = c * CH
    acol = acum[r0:r0 + CH, h:h + 1]
    arow = act_t[h:h + 1, r0:r0 + CH]
    return jnp.exp(jnp.where(causal, acol - arow, NEG))


def _make_ssd_fwd(seq, tb):
    ncb = tb // CH
    nb = seq // tb
    hg = SSD_H // SSD_G
    gw = SSD_W // SSD_G

    def body(ps_ref, ss_ref, cw_ref, cb_ref, prm_ref, nw_ref, ob_ref, st_ref, hs_scr, halo_scr):
        @pl.when(pl.program_id(0) == 0)
        def _():
            hs_scr[...] = jnp.zeros_like(hs_scr)
            halo_scr[...] = jnp.zeros_like(halo_scr)

        causal, _, _ = _masks()
        (raw, _, _, act, _, _, _, acum, e, _, xdt, ea_e, d_e) = _ssd_common(
            ps_ref, halo_scr[...], ss_ref[...], cw_ref[...], cb_ref[...], prm_ref[...], tb)
        halo_scr[...] = raw[tb - 8:tb, :]
        act_t = acum.T
        nw = nw_ref[0:1, :]
        for c in range(ncb):
            rows, wd_e, eal_e = _ssd_chunk(act, acum, act_t, e, c)
            st_ref[c] = hs_scr[...]
            ys = []
            for g in range(SSD_G):
                gc_ = slice(g * gw, (g + 1) * gw)
                bg = act[rows, SSD_W + g * 128:SSD_W + (g + 1) * 128]
                cg = act[rows, SSD_W + 256 + g * 128:SSD_W + 256 + (g + 1) * 128]
                cbm = _dot_nt(cg, bg)
                hs = hs_scr[:, gc_]
                yin = _dot(cg, hs)
                yh = []
                for hh in range(hg):
                    h = g * hg + hh
                    lm = _ssd_lmat(acum, act_t, c, h, causal)
                    yh.append(_dot(cbm * lm, xdt[rows, h * SSD_P:(h + 1) * SSD_P]))
                ys.append(jnp.concatenate(yh, axis=1) + yin * ea_e[rows, gc_])
                hs_scr[:, gc_] = hs * eal_e[:, gc_] + _dot_tn(bg, xdt[rows, gc_] * wd_e[:, gc_])
            y = jnp.concatenate(ys, axis=1) + act[rows, 0:SSD_W] * d_e
            yz = y * _silu(ps_ref[rows, 1536:2560])
            outs = [_rms_fwd(yz[:, g * gw:(g + 1) * gw], nw[:, g * gw:(g + 1) * gw], gw)[2] for g in range(SSD_G)]
            ob_ref[rows, :] = jnp.concatenate(outs, axis=1)

    def call(ps, ss, cw, cb, prm, nw):
        const = lambda i: (0, 0)
        return pl.pallas_call(
            body,
            grid=(nb,),
            in_specs=[
                pl.BlockSpec((tb, 2560), lambda i: (i, 0)),
                pl.BlockSpec((tb, 128), lambda i: (i, 0)),
                pl.BlockSpec((8, 1536), const),
                pl.BlockSpec((8, 1536), const),
                pl.BlockSpec((8, 128), const),
                pl.BlockSpec((8, SSD_W), const),
            ],
            out_specs=[
                pl.BlockSpec((tb, SSD_W), lambda i: (i, 0)),
                pl.BlockSpec((ncb, SSD_N, SSD_W), lambda i: (i, 0, 0)),
            ],
            out_shape=[
                jax.ShapeDtypeStruct((seq, SSD_W), F32),
                jax.ShapeDtypeStruct((seq // CH, SSD_N, SSD_W), F32),
            ],
            scratch_shapes=[pltpu.VMEM((SSD_N, SSD_W), F32), pltpu.VMEM((8, 1536), F32)],
            compiler_params=pltpu.CompilerParams(dimension_semantics=("arbitrary",), vmem_limit_bytes=VMEM_LIMIT),
            name="ssd_fwd",
        )(ps, ss, cw, cb, prm, nw)

    return call


def _make_ssd_bwd(seq, tb):
    ncb = tb // CH
    nb = seq // tb
    hb = tb // 8
    hg = SSD_H // SSD_G
    gw = SSD_W // SSD_G

    def body(ps_ref, prev_ref, ss_ref, cw_ref, cb_ref, prm_ref, nw_ref, st_ref, dob_ref,
             dps_ref, dss_ref, dcw_ref, dcb_ref, dprm_ref, dnw_ref, dhs_scr, nxt_scr):
        i = pl.program_id(0)

        @pl.when(i == 0)
        def _():
            dhs_scr[...] = jnp.zeros_like(dhs_scr)
            nxt_scr[...] = jnp.zeros_like(nxt_scr)
            dcw_ref[...] = jnp.zeros_like(dcw_ref)
            dcb_ref[...] = jnp.zeros_like(dcb_ref)
            dprm_ref[...] = jnp.zeros_like(dprm_ref)
            dnw_ref[...] = jnp.zeros_like(dnw_ref)

        causal, _, _ = _masks()
        cw = cw_ref[...]
        prm = prm_ref[...]
        halo8 = jnp.where(i == nb - 1, 0.0, prev_ref[...])
        (raw, taps, pre, act, dt_in, dt, a, acum, e, dt_e, xdt, ea_e, d_e) = _ssd_common(
            ps_ref, halo8, ss_ref[...], cw, cb_ref[...], prm, tb)
        act_t = acum.T
        nw = nw_ref[0:1, :]
        row_id = _iota2((CH, 1), 0)

        dx_l, db_l, dc_l, dz_l, dacum_l, ddt_l, da_in_l = ([None] * ncb for _ in range(7))
        upper_tri = (_iota2((CH, CH), 1) >= _iota2((CH, CH), 0)).astype(F32)
        below = jnp.bitwise_and(_iota2((CH, gw), 1), CH - 1) < _iota2((CH, gw), 0)
        dnw_acc = jnp.zeros((1, SSD_W), F32)
        dd_acc = jnp.zeros((1, SSD_W), F32)

        for c in reversed(range(ncb)):
            rows, wd_e, eal_e = _ssd_chunk(act, acum, act_t, e, c)
            xc = act[rows, 0:SSD_W]
            z = ps_ref[rows, 1536:2560]
            dob = dob_ref[rows, :]
            sz = _silu(z)
            dy_g, dz_g, zacc_g, dxdt_g, dal_g, db_g, dc_g, da_in_g = [], [], [], [], [], [], [], []
            for g in range(SSD_G):
                gc_ = slice(g * gw, (g + 1) * gw)
                bg = act[rows, SSD_W + g * 128:SSD_W + (g + 1) * 128]
                cg = act[rows, SSD_W + 256 + g * 128:SSD_W + 256 + (g + 1) * 128]
                cbm = _dot_nt(cg, bg)
                hs = st_ref[c, :, gc_]
                yin = _dot(cg, hs)
                lms, yh = [], []
                for hh in range(hg):
                    h = g * hg + hh
                    lm = cbm * _ssd_lmat(acum, act_t, c, h, causal)
                    lms.append(lm)
                    yh.append(_dot(lm, xdt[rows, h * SSD_P:(h + 1) * SSD_P]))
                y_intra = jnp.concatenate(yh, axis=1)
                ea_g = ea_e[rows, gc_]
                y = y_intra + yin * ea_g + xc[:, gc_] * d_e[:, gc_]
                yz = y * sz[:, gc_]
                on, r, _ = _rms_fwd(yz, nw[:, gc_], gw)
                dyz, dnw_rows = _rms_bwd(dob[:, gc_], on, r, nw[:, gc_], gw)
                dnw_acc = dnw_acc + _put_cols(jnp.sum(dnw_rows, axis=0, keepdims=True), g, gw)
                dy = dyz * sz[:, gc_]
                dz_g.append(dyz * y * _dsilu(z[:, gc_]))
                dd_acc = dd_acc + _put_cols(jnp.sum(dy * xc[:, gc_], axis=0, keepdims=True), g, gw)
                dhs_n = dhs_scr[:, gc_]
                dyin = dy * ea_g
                dcg = _dot_nt(dyin, hs)
                xw = xdt[rows, gc_] * wd_e[:, gc_]
                dbg = _dot_nt(xw, dhs_n)
                dxw = _dot(bg, dhs_n)
                dhs_scr[:, gc_] = dhs_n * eal_e[:, gc_] + _dot_tn(cg, dyin)
                dal_g.append(jnp.sum(hs * dhs_n, axis=0, keepdims=True) * eal_e[:, gc_]
                             + jnp.sum(dxw * xw, axis=0, keepdims=True))
                dxi, ms, dcbm = [], [], jnp.zeros((CH, CH), F32)
                for hh in range(hg):
                    h = g * hg + hh
                    hc = slice(hh * SSD_P, (hh + 1) * SSD_P)
                    dyh = dy[:, hc]
                    dxi.append(_dot_tn(lms[hh], dyh))
                    dlm = _dot_nt(dyh, xdt[rows, h * SSD_P:(h + 1) * SSD_P])
                    ms.append(dlm * lms[hh])
                    dcbm = dcbm + dlm * _ssd_lmat(acum, act_t, c, h, causal)
                dx_intra = jnp.concatenate(dxi, axis=1)
                ncat = _dot(upper_tri, jnp.concatenate(ms, axis=1))
                da_in_g.append(jnp.where(below, ncat, 0.0))
                zacc_g.append(dy * yin * ea_g - dxw * xw)
                dxdt_g.append(dx_intra + dxw * wd_e[:, gc_])
                dy_g.append(dy)
                db_g.append(dbg + _dot_tn(dcbm, cg))
                dc_g.append(dcg + _dot(dcbm, bg))
            dy = jnp.concatenate(dy_g, axis=1)
            dxdt = jnp.concatenate(dxdt_g, axis=1)
            dx_l[c] = dxdt * dt_e[rows, :] + dy * d_e
            db_l[c] = jnp.concatenate(db_g, axis=1)
            dc_l[c] = jnp.concatenate(dc_g, axis=1)
            dz_l[c] = jnp.concatenate(dz_g, axis=1)
            ddt_l[c] = _reduce_heads(dxdt * xc, e)
            dalast = _reduce_heads(_row8(jnp.concatenate(dal_g, axis=1)), e)[0:1, :]
            dacum_l[c] = _reduce_heads(jnp.concatenate(zacc_g, axis=1), e) + jnp.where(row_id == CH - 1, dalast, 0.0)
            da_in_l[c] = _reduce_heads(jnp.concatenate(da_in_g, axis=1), e)

        dacum_all = jnp.concatenate(dacum_l, axis=0)
        da = _dotx(_chunk_tri(tb, upper=True), dacum_all) + jnp.concatenate(da_in_l, axis=0)
        neg_ea = -jnp.exp(prm[0:1, :])
        ddt = jnp.concatenate(ddt_l, axis=0) + da * neg_ea
        ddt_in = ddt * _sigmoid(dt_in)
        dss_ref[...] = ddt_in.astype(dss_ref.dtype)
        sub8 = _iota2((8, 128), 0)
        dalog = jnp.sum(da * a, axis=0, keepdims=True)
        ddtb = jnp.sum(ddt_in, axis=0, keepdims=True)
        dd = _reduce_heads(_row8(dd_acc), e)[0:1, :]
        dprm_ref[...] += (jnp.where(sub8 == 0, dalog, 0.0) + jnp.where(sub8 == 1, ddtb, 0.0)
                          + jnp.where(sub8 == 2, dd, 0.0))
        dnw_ref[...] += jnp.where(_iota2((8, SSD_W), 0) == 0, dnw_acc, 0.0)

        dact = jnp.concatenate([jnp.concatenate(dx_l, axis=0), jnp.concatenate(db_l, axis=0),
                                jnp.concatenate(dc_l, axis=0)], axis=1)
        dpre = dact * _dsilu(pre)
        back = _conv_back(dpre, nxt_scr[...], tb)
        nxt_scr[...] = dpre[0:8, :]
        draw = back[0] * cw[3:4, :] + back[1] * cw[2:3, :] + back[2] * cw[1:2, :] + back[3] * cw[0:1, :]
        dps_ref[:, 0:1536] = draw.astype(dps_ref.dtype)
        dps_ref[:, 1536:2560] = jnp.concatenate(dz_l, axis=0).astype(dps_ref.dtype)
        sub_c = _iota2((8, 1536), 0)
        dcw_new = jnp.zeros((8, 1536), F32)
        for s_ in range(CONV_W):
            dcw_new = dcw_new + jnp.where(sub_c == 3 - s_, jnp.sum(dpre * taps[s_], axis=0, keepdims=True), 0.0)
        dcw_ref[...] += dcw_new
        dcb_ref[...] += jnp.where(sub_c == 0, jnp.sum(dpre, axis=0, keepdims=True), 0.0)

    def call(ps, ss, cw, cb, prm, nw, st, dob):
        rev = lambda i: (nb - 1 - i, 0)
        const = lambda i: (0, 0)
        return pl.pallas_call(
            body,
            grid=(nb,),
            in_specs=[
                pl.BlockSpec((tb, 2560), rev),
                pl.BlockSpec((8, 1536), lambda i: (jnp.maximum((nb - 1 - i) * hb - 1, 0), 0)),
                pl.BlockSpec((tb, 128), rev),
                pl.BlockSpec((8, 1536), const),
                pl.BlockSpec((8, 1536), const),
                pl.BlockSpec((8, 128), const),
                pl.BlockSpec((8, SSD_W), const),
                pl.BlockSpec((ncb, SSD_N, SSD_W), lambda i: (nb - 1 - i, 0, 0)),
                pl.BlockSpec((tb, SSD_W), rev),
            ],
            out_specs=[
                pl.BlockSpec((tb, 2560), rev),
                pl.BlockSpec((tb, 128), rev),
                pl.BlockSpec((8, 1536), const),
                pl.BlockSpec((8, 1536), const),
                pl.BlockSpec((8, 128), const),
                pl.BlockSpec((8, SSD_W), const),
            ],
            out_shape=[
                jax.ShapeDtypeStruct((seq, 2560), BF16),
                jax.ShapeDtypeStruct((seq, 128), BF16),
                jax.ShapeDtypeStruct((8, 1536), F32),
                jax.ShapeDtypeStruct((8, 1536), F32),
                jax.ShapeDtypeStruct((8, 128), F32),
                jax.ShapeDtypeStruct((8, SSD_W), F32),
            ],
            scratch_shapes=[pltpu.VMEM((SSD_N, SSD_W), F32), pltpu.VMEM((8, 1536), F32)],
            compiler_params=pltpu.CompilerParams(dimension_semantics=("arbitrary",), vmem_limit_bytes=VMEM_LIMIT),
            name="ssd_bwd",
        )(ps, ps, ss, cw, cb, prm, nw, st, dob)

    return call


def _ret_consts(h):
    lg = math.log(1.0 - 2.0 ** (-5.0 - h))
    r = _iota2((CH, CH), 0)
    c = _iota2((CH, CH), 1)
    rel = (r - c).astype(F32)
    dmat = jnp.where(r >= c, jnp.exp(jnp.maximum(rel, 0.0) * lg), 0.0)
    idx = _iota2((CH, 1), 0).astype(F32)
    qdec = jnp.exp((idx + 1.0) * lg)
    kdec = jnp.exp((CH - 1.0 - idx) * lg)
    cdec = math.exp(CH * lg)
    return dmat, qdec, kdec, cdec


def _rot(t, cc, ss):
    return t * cc + pltpu.roll(t, 64, axis=1) * ss


def _rot_bwd(d, cc, ss):
    return d * cc + pltpu.roll(d * ss, 64, axis=1)


def _make_ret_fwd(seq, tb):
    ncb = tb // CH
    nb = seq // tb

    def body(pr_ref, cc_ref, ss_ref, nw_ref, oc_ref, st_ref, r_scr):
        @pl.when(pl.program_id(0) == 0)
        def _():
            r_scr[...] = jnp.zeros_like(r_scr)

        nw = nw_ref[0:1, :]
        for h in range(RET_H):
            dmat, qdec, kdec, cdec = _ret_consts(h)
            hc = slice(h * 128, (h + 1) * 128)
            for c in range(ncb):
                rows = slice(c * CH, (c + 1) * CH)
                cc, ss = cc_ref[rows, :], ss_ref[rows, :]
                q = _rot(pr_ref[rows, h * 128:(h + 1) * 128], cc, ss)
                k = _rot(pr_ref[rows, 512 + h * 128:512 + (h + 1) * 128], cc, ss) * (RET_D ** -0.5)
                v = pr_ref[rows, 1024 + h * 128:1024 + (h + 1) * 128]
                rs = r_scr[h]
                st_ref[c, h] = rs
                s = _dot_nt(q, k) * dmat
                o = _dot(s, v) + _dot(q, rs) * qdec
                r_scr[h] = rs * cdec + _dot_tn(k * kdec, v)
                _, _, y = _rms_fwd(o, nw, RET_D)
                oc_ref[rows, hc] = y * _silu(pr_ref[rows, 1536 + h * 128:1536 + (h + 1) * 128])

    def call(pr, cc, ss, nw):
        return pl.pallas_call(
            body,
            grid=(nb,),
            in_specs=[
                pl.BlockSpec((tb, 2048), lambda i: (i, 0)),
                pl.BlockSpec((tb, 128), lambda i: (i, 0)),
                pl.BlockSpec((tb, 128), lambda i: (i, 0)),
                pl.BlockSpec((8, 128), lambda i: (0, 0)),
            ],
            out_specs=[
                pl.BlockSpec((tb, 512), lambda i: (i, 0)),
                pl.BlockSpec((ncb, RET_H, 128, 128), lambda i: (i, 0, 0, 0)),
            ],
            out_shape=[
                jax.ShapeDtypeStruct((seq, 512), F32),
                jax.ShapeDtypeStruct((seq // CH, RET_H, 128, 128), F32),
            ],
            scratch_shapes=[pltpu.VMEM((RET_H, 128, 128), F32)],
            compiler_params=pltpu.CompilerParams(dimension_semantics=("arbitrary",), vmem_limit_bytes=VMEM_LIMIT),
            name="ret_fwd",
        )(pr, cc, ss, nw)

    return call


def _make_ret_bwd(seq, tb):
    ncb = tb // CH
    nb = seq // tb

    def body(pr_ref, cc_ref, ss_ref, nw_ref, st_ref, doc_ref, dpr_ref, dnw_ref, dr_scr):
        @pl.when(pl.program_id(0) == 0)
        def _():
            dr_scr[...] = jnp.zeros_like(dr_scr)
            dnw_ref[...] = jnp.zeros_like(dnw_ref)

        nw = nw_ref[0:1, :]
        dnw_acc = jnp.zeros((1, 128), F32)
        scale = RET_D ** -0.5
        for h in range(RET_H):
            dmat, qdec, kdec, cdec = _ret_consts(h)
            for c in reversed(range(ncb)):
                rows = slice(c * CH, (c + 1) * CH)
                cc, ss = cc_ref[rows, :], ss_ref[rows, :]
                q = _rot(pr_ref[rows, h * 128:(h + 1) * 128], cc, ss)
                k = _rot(pr_ref[rows, 512 + h * 128:512 + (h + 1) * 128], cc, ss) * scale
                v = pr_ref[rows, 1024 + h * 128:1024 + (h + 1) * 128]
                z = pr_ref[rows, 1536 + h * 128:1536 + (h + 1) * 128]
                rs = st_ref[c, h]
                s = _dot_nt(q, k) * dmat
                o = _dot(s, v) + _dot(q, rs) * qdec
                doc = doc_ref[rows, h * 128:(h + 1) * 128]
                on, r, y = _rms_fwd(o, nw, RET_D)
                dz = doc * y * _dsilu(z)
                do, dnw_rows = _rms_bwd(doc * _silu(z), on, r, nw, RET_D)
                dnw_acc = dnw_acc + jnp.sum(dnw_rows, axis=0, keepdims=True)
                drn = dr_scr[h]
                dqd = do * qdec
                ds = _dot_nt(do, v) * dmat
                kd = k * kdec
                dq = _dot(ds, k) + _dot_nt(dqd, rs)
                dk = _dot_tn(ds, q) + _dot_nt(v, drn) * kdec
                dv = _dot_tn(s, do) + _dot(kd, drn)
                dr_scr[h] = _dot_tn(q, dqd) + cdec * drn
                dpr_ref[rows, h * 128:(h + 1) * 128] = _rot_bwd(dq, cc, ss).astype(dpr_ref.dtype)
                dpr_ref[rows, 512 + h * 128:512 + (h + 1) * 128] = _rot_bwd(dk * scale, cc, ss).astype(dpr_ref.dtype)
                dpr_ref[rows, 1024 + h * 128:1024 + (h + 1) * 128] = dv.astype(dpr_ref.dtype)
                dpr_ref[rows, 1536 + h * 128:1536 + (h + 1) * 128] = dz.astype(dpr_ref.dtype)
        dnw_ref[...] += jnp.where(_iota2((8, 128), 0) == 0, dnw_acc, 0.0)

    def call(pr, cc, ss, nw, st, doc):
        rev = lambda i: (nb - 1 - i, 0)
        return pl.pallas_call(
            body,
            grid=(nb,),
            in_specs=[
                pl.BlockSpec((tb, 2048), rev),
                pl.BlockSpec((tb, 128), rev),
                pl.BlockSpec((tb, 128), rev),
                pl.BlockSpec((8, 128), lambda i: (0, 0)),
                pl.BlockSpec((ncb, RET_H, 128, 128), lambda i: (nb - 1 - i, 0, 0, 0)),
                pl.BlockSpec((tb, 512), rev),
            ],
            out_specs=[
                pl.BlockSpec((tb, 2048), rev),
                pl.BlockSpec((8, 128), lambda i: (0, 0)),
            ],
            out_shape=[
                jax.ShapeDtypeStruct((seq, 2048), BF16),
                jax.ShapeDtypeStruct((8, 128), F32),
            ],
            scratch_shapes=[pltpu.VMEM((RET_H, 128, 128), F32)],
            compiler_params=pltpu.CompilerParams(dimension_semantics=("arbitrary",), vmem_limit_bytes=VMEM_LIMIT),
            name="ret_bwd",
        )(pr, cc, ss, nw, st, doc)

    return call


def _rope_tables(seq):
    half = RET_D // 2
    inv = ROPE_BASE ** (-jnp.arange(half, dtype=F32) / half)
    ang = jnp.arange(seq, dtype=jnp.int32).astype(F32)[:, None] * inv[None, :]
    cos, sin = jnp.cos(ang), jnp.sin(ang)
    return jnp.concatenate([cos, cos], axis=1), jnp.concatenate([-sin, sin], axis=1)


SEG_G, SEG_S, SEG_R, SEG_GS, SEG_SS = (0, 2048), (2048, 4608), (4608, 6656), (6656, 6784), (6784, 6912)
NP = 6912
SEGS = (SEG_G, SEG_S, SEG_R, SEG_GS, SEG_SS)


def _resident(shape):
    return pl.BlockSpec(shape, lambda i: (0,) * len(shape), pipeline_mode=pl.Buffered(1))


def _make_inproj(seq, tl):
    def body(x_ref, pn_ref, w_ref, pg_ref, ps_ref, pr_ref, gs_ref, ss_ref):
        x = x_ref[...]
        _, _, hn = _rms_fwd(x, pn_ref[0:1, :], D_MODEL)
        h = hn.astype(BF16)
        for (a, b), o_ref in zip(SEGS, (pg_ref, ps_ref, pr_ref, gs_ref, ss_ref)):
            o_ref[...] = jnp.dot(h, w_ref[:, a:b], preferred_element_type=F32)

    def call(x, pn, w):
        row = lambda i: (i, 0)
        return pl.pallas_call(
            body,
            grid=(seq // tl,),
            in_specs=[pl.BlockSpec((tl, D_MODEL), row), _resident((8, D_MODEL)), _resident((D_MODEL, NP))],
            out_specs=[pl.BlockSpec((tl, b - a), row) for a, b in SEGS],
            out_shape=[jax.ShapeDtypeStruct((seq, b - a), F32) for a, b in SEGS],
            compiler_params=pltpu.CompilerParams(dimension_semantics=("arbitrary",), vmem_limit_bytes=VMEM_LIMIT),
            name="inproj",
        )(x, pn, w)

    return call


def _make_outproj(seq, tl):
    def body(oa_ref, ob_ref, oc_ref, w_ref, x_ref, qn_ref, out_ref, xn_ref):
        out = (jnp.dot(oa_ref[...].astype(BF16), w_ref[0:512, :], preferred_element_type=F32)
               + jnp.dot(ob_ref[...].astype(BF16), w_ref[512:1536, :], preferred_element_type=F32)
               + jnp.dot(oc_ref[...].astype(BF16), w_ref[1536:2048, :], preferred_element_type=F32))
        out_ref[...] = out
        _, _, y = _rms_fwd(out, qn_ref[0:1, :], D_MODEL)
        xn_ref[...] = x_ref[...] + y

    def call(oa, ob, oc, w, x, qn):
        row = lambda i: (i, 0)
        return pl.pallas_call(
            body,
            grid=(seq // tl,),
            in_specs=[pl.BlockSpec((tl, 512), row), pl.BlockSpec((tl, 1024), row), pl.BlockSpec((tl, 512), row),
                      _resident((2048, D_MODEL)), pl.BlockSpec((tl, D_MODEL), row), _resident((8, D_MODEL))],
            out_specs=[pl.BlockSpec((tl, D_MODEL), row), pl.BlockSpec((tl, D_MODEL), row)],
            out_shape=[jax.ShapeDtypeStruct((seq, D_MODEL), F32), jax.ShapeDtypeStruct((seq, D_MODEL), F32)],
            compiler_params=pltpu.CompilerParams(dimension_semantics=("arbitrary",), vmem_limit_bytes=VMEM_LIMIT),
            name="outproj",
        )(oa, ob, oc, w, x, qn)

    return call


def _make_loss_head(seq, tl):
    def body(y_ref, t_ref, dy_ref, loss_ref):
        @pl.when(pl.program_id(0) == 0)
        def _():
            loss_ref[...] = jnp.zeros_like(loss_ref)

        err = y_ref[...] - t_ref[...]
        dy_ref[...] = err * (1.0 / D_MODEL)
        part = jnp.sum(jnp.sum(err * err, axis=1, keepdims=True), axis=0, keepdims=True) * (0.5 / D_MODEL)
        loss_ref[...] += jnp.where((_iota2((8, 128), 0) == 0) & (_iota2((8, 128), 1) == 0), part, 0.0)

    def call(y, t):
        row = lambda i: (i, 0)
        return pl.pallas_call(
            body,
            grid=(seq // tl,),
            in_specs=[pl.BlockSpec((tl, D_MODEL), row), pl.BlockSpec((tl, D_MODEL), row)],
            out_specs=[pl.BlockSpec((tl, D_MODEL), row), pl.BlockSpec((8, 128), lambda i: (0, 0))],
            out_shape=[jax.ShapeDtypeStruct((seq, D_MODEL), F32), jax.ShapeDtypeStruct((8, 128), F32)],
            compiler_params=pltpu.CompilerParams(dimension_semantics=("arbitrary",)),
            name="loss_head",
        )(y, t)

    return call


def _make_outproj_bwd(seq, tl):
    def body(dxn_ref, out_ref, oa_ref, ob_ref, oc_ref, w_ref, qn_ref, doa_ref, dob_ref, doc_ref, dqn_ref, dw_ref):
        @pl.when(pl.program_id(0) == 0)
        def _():
            dqn_ref[...] = jnp.zeros_like(dqn_ref)
            dw_ref[...] = jnp.zeros_like(dw_ref)

        qn = qn_ref[0:1, :]
        on, r, _ = _rms_fwd(out_ref[...], qn, D_MODEL)
        dout, dqn_rows = _rms_bwd(dxn_ref[...], on, r, qn, D_MODEL)
        dqn_ref[...] += jnp.where(_iota2((8, D_MODEL), 0) == 0, jnp.sum(dqn_rows, axis=0, keepdims=True), 0.0)
        db = dout.astype(BF16)
        nt = (((1,), (1,)), ((), ()))
        tn = (((0,), (0,)), ((), ()))
        doa_ref[...] = lax.dot_general(db, w_ref[0:512, :], nt, preferred_element_type=F32)
        dob_ref[...] = lax.dot_general(db, w_ref[512:1536, :], nt, preferred_element_type=F32)
        doc_ref[...] = lax.dot_general(db, w_ref[1536:2048, :], nt, preferred_element_type=F32)
        dw_ref[0:512, :] += lax.dot_general(oa_ref[...].astype(BF16), db, tn, preferred_element_type=F32)
        dw_ref[512:1536, :] += lax.dot_general(ob_ref[...].astype(BF16), db, tn, preferred_element_type=F32)
        dw_ref[1536:2048, :] += lax.dot_general(oc_ref[...].astype(BF16), db, tn, preferred_element_type=F32)

    def call(dxn, out, oa, ob, oc, w, qn):
        row = lambda i: (i, 0)
        const = lambda i: (0, 0)
        return pl.pallas_call(
            body,
            grid=(seq // tl,),
            in_specs=[pl.BlockSpec((tl, D_MODEL), row), pl.BlockSpec((tl, D_MODEL), row),
                      pl.BlockSpec((tl, 512), row), pl.BlockSpec((tl, 1024), row), pl.BlockSpec((tl, 512), row),
                      _resident((2048, D_MODEL)), _resident((8, D_MODEL))],
            out_specs=[pl.BlockSpec((tl, 512), row), pl.BlockSpec((tl, 1024), row), pl.BlockSpec((tl, 512), row),
                       pl.BlockSpec((8, D_MODEL), const), pl.BlockSpec((2048, D_MODEL), const)],
            out_shape=[jax.ShapeDtypeStruct((seq, 512), F32), jax.ShapeDtypeStruct((seq, 1024), F32),
                       jax.ShapeDtypeStruct((seq, 512), F32), jax.ShapeDtypeStruct((8, D_MODEL), F32),
                       jax.ShapeDtypeStruct((2048, D_MODEL), F32)],
            compiler_params=pltpu.CompilerParams(dimension_semantics=("arbitrary",), vmem_limit_bytes=VMEM_LIMIT),
            name="outproj_bwd",
        )(dxn, out, oa, ob, oc, w, qn)

    return call


def _make_inproj_bwd_dx(seq, tl):
    def body(dg_ref, ds_ref, dr_ref, dgs_ref, dss_ref, w_ref, x_ref, pn_ref, dxn_ref, dx_ref, dpn_ref):
        @pl.when(pl.program_id(0) == 0)
        def _():
            dpn_ref[...] = jnp.zeros_like(dpn_ref)

        nt = (((1,), (1,)), ((), ()))
        dh = jnp.zeros((tl, D_MODEL), F32)
        for (a, b), d_ref in zip(SEGS, (dg_ref, ds_ref, dr_ref, dgs_ref, dss_ref)):
            dh = dh + lax.dot_general(d_ref[...], w_ref[:, a:b], nt, preferred_element_type=F32)
        pn = pn_ref[0:1, :]
        on, r, _ = _rms_fwd(x_ref[...], pn, D_MODEL)
        dx, dpn_rows = _rms_bwd(dh, on, r, pn, D_MODEL)
        dx_ref[...] = dx + dxn_ref[...]
        dpn_ref[...] += jnp.where(_iota2((8, D_MODEL), 0) == 0, jnp.sum(dpn_rows, axis=0, keepdims=True), 0.0)

    def call(dg, ds, dr, dgs, dss, w, x, pn, dxn):
        row = lambda i: (i, 0)
        return pl.pallas_call(
            body,
            grid=(seq // tl,),
            in_specs=[pl.BlockSpec((tl, b - a), row) for a, b in SEGS]
            + [_resident((D_MODEL, NP)), pl.BlockSpec((tl, D_MODEL), row), _resident((8, D_MODEL)),
               pl.BlockSpec((tl, D_MODEL), row)],
            out_specs=[pl.BlockSpec((tl, D_MODEL), row), pl.BlockSpec((8, D_MODEL), lambda i: (0, 0))],
            out_shape=[jax.ShapeDtypeStruct((seq, D_MODEL), F32), jax.ShapeDtypeStruct((8, D_MODEL), F32)],
            compiler_params=pltpu.CompilerParams(dimension_semantics=("arbitrary",), vmem_limit_bytes=VMEM_LIMIT),
            name="inproj_bwd_dx",
        )(dg, ds, dr, dgs, dss, w, x, pn, dxn)

    return call


def _make_inproj_bwd_dw(seq, tl, width, name):
    def body(x_ref, pn_ref, d_ref, dw_ref):
        @pl.when(pl.program_id(0) == 0)
        def _():
            dw_ref[...] = jnp.zeros_like(dw_ref)

        _, _, hn = _rms_fwd(x_ref[...], pn_ref[0:1, :], D_MODEL)
        dw_ref[...] += lax.dot_general(hn.astype(BF16), d_ref[...], (((0,), (0,)), ((), ())),
                                       preferred_element_type=F32)

    def call(x, pn, d):
        row = lambda i: (i, 0)
        return pl.pallas_call(
            body,
            grid=(seq // tl,),
            in_specs=[pl.BlockSpec((tl, D_MODEL), row), _resident((8, D_MODEL)), pl.BlockSpec((tl, width), row)],
            out_specs=pl.BlockSpec((D_MODEL, width), lambda i: (0, 0)),
            out_shape=jax.ShapeDtypeStruct((D_MODEL, width), F32),
            compiler_params=pltpu.CompilerParams(dimension_semantics=("arbitrary",), vmem_limit_bytes=VMEM_LIMIT),
            name=name,
        )(x, pn, d)

    return call


ADAM_LR, ADAM_B1, ADAM_B2, ADAM_EPS, ADAM_WD, ADAM_STEP = 0.001, 0.9, 0.999, 1e-08, 0.01, 10


def _adam_math(w, g, m, v):
    m = ADAM_B1 * m + (1.0 - ADAM_B1) * g
    v = ADAM_B2 * v + (1.0 - ADAM_B2) * (g * g)
    m_hat = m / (1.0 - ADAM_B1 ** ADAM_STEP)
    v_hat = v / (1.0 - ADAM_B2 ** ADAM_STEP)
    delta = -ADAM_LR * (m_hat / (jnp.sqrt(v_hat) + ADAM_EPS) + ADAM_WD * w)
    return delta, m, v


def _adamw(w, g, m, v, name, g2=None):
    shape = w.shape
    cols = shape[-1]
    rows = w.size // cols
    tr = rows if rows <= 512 else 256
    assert rows % tr == 0
    ops = [a.reshape(rows, cols) for a in ((w, g, m, v) if g2 is None else (w, g, g2, m, v))]

    def body(*refs):
        if g2 is None:
            w_ref, g_ref, m_ref, v_ref, go_ref, d_ref, mo_ref, vo_ref = refs
            g_ = g_ref[...]
        else:
            w_ref, g_ref, g2_ref, m_ref, v_ref, go_ref, d_ref, mo_ref, vo_ref = refs
            g_ = g_ref[...] + g2_ref[...]
        go_ref[...] = g_
        d_ref[...], mo_ref[...], vo_ref[...] = _adam_math(w_ref[...], g_, m_ref[...], v_ref[...])

    spec = pl.BlockSpec((tr, cols), lambda i: (i, 0))
    outs = pl.pallas_call(
        body,
        grid=(rows // tr,),
        in_specs=[spec] * len(ops),
        out_specs=[spec] * 4,
        out_shape=[jax.ShapeDtypeStruct((rows, cols), F32)] * 4,
        compiler_params=pltpu.CompilerParams(dimension_semantics=("arbitrary",), vmem_limit_bytes=VMEM_LIMIT),
        name=name,
    )(*ops)
    return tuple(o.reshape(shape) for o in outs)


MESH = pl.DeviceIdType.MESH
ANY = pl.BlockSpec(memory_space=pl.ANY)
CHIP_REL = ((1, 0), (0, 1), (1, 1))


def _flip(v, d):
    return 1 - v if d else v


def _ag_chips(arrs, name):
    n = len(arrs)

    def body(*refs):
        ins, outs = refs[:n], refs[n:2 * n]
        send_sems, recv_sems, loc_sems = refs[2 * n:]
        x, y, c = lax.axis_index("x"), lax.axis_index("y"), lax.axis_index("c")
        me = 2 * x + y

        def remote(a, k, slot):
            dx, dy = CHIP_REL[k]
            return pltpu.make_async_remote_copy(
                src_ref=ins[a], dst_ref=outs[a].at[slot], send_sem=send_sems.at[a * 3 + k],
                recv_sem=recv_sems.at[a * 3 + k], device_id=(_flip(x, dx), _flip(y, dy), c), device_id_type=MESH)

        local = [pltpu.make_async_copy(ins[a], outs[a].at[me], loc_sems.at[a]) for a in range(n)]
        for cp in local:
            cp.start()
        for a in range(n):
            for k in range(3):
                remote(a, k, me).start()
        for a in range(n):
            for k, (dx, dy) in enumerate(CHIP_REL):
                remote(a, k, 2 * _flip(x, dx) + _flip(y, dy)).wait_recv()
        for a in range(n):
            for k in range(3):
                remote(a, k, me).wait_send()
        for cp in local:
            cp.wait()

    return pl.pallas_call(
        body,
        in_specs=[ANY] * n,
        out_specs=[ANY] * n,
        out_shape=[jax.ShapeDtypeStruct((4,) + a.shape, a.dtype) for a in arrs],
        scratch_shapes=[pltpu.SemaphoreType.DMA((3 * n,)), pltpu.SemaphoreType.DMA((3 * n,)),
                        pltpu.SemaphoreType.DMA((n,))],
        compiler_params=pltpu.CompilerParams(has_side_effects=True),
        name=name,
    )(*arrs)


def _rs_chips(arrs, name):
    n = len(arrs)

    def body(*refs):
        ins, outs = refs[:n], refs[n:2 * n]
        send_sems, recv_sems = refs[2 * n:]
        x, y, c = lax.axis_index("x"), lax.axis_index("y"), lax.axis_index("c")

        def remote(a, k):
            dx, dy = CHIP_REL[k]
            px, py = _flip(x, dx), _flip(y, dy)
            return pltpu.make_async_remote_copy(
                src_ref=ins[a].at[2 * px + py], dst_ref=outs[a].at[k], send_sem=send_sems.at[a * 3 + k],
                recv_sem=recv_sems.at[a * 3 + k], device_id=(px, py, c), device_id_type=MESH)

        cps = [remote(a, k) for a in range(n) for k in range(3)]
        for cp in cps:
            cp.start()
        for cp in cps:
            cp.wait_recv()
        for cp in cps:
            cp.wait_send()

    return pl.pallas_call(
        body,
        in_specs=[ANY] * n,
        out_specs=[ANY] * n,
        out_shape=[jax.ShapeDtypeStruct((3,) + a.shape[1:], a.dtype) for a in arrs],
        scratch_shapes=[pltpu.SemaphoreType.DMA((3 * n,)), pltpu.SemaphoreType.DMA((3 * n,))],
        compiler_params=pltpu.CompilerParams(has_side_effects=True),
        name=name,
    )(*arrs)


def _swap_sibling(arrs, name):
    n = len(arrs)

    def body(*refs):
        ins, outs = refs[:n], refs[n:2 * n]
        send_sems, recv_sems = refs[2 * n:]
        x, y, c = lax.axis_index("x"), lax.axis_index("y"), lax.axis_index("c")
        cps = [pltpu.make_async_remote_copy(src_ref=ins[a], dst_ref=outs[a], send_sem=send_sems.at[a],
                                            recv_sem=recv_sems.at[a], device_id=(x, y, 1 - c), device_id_type=MESH)
               for a in range(n)]
        for cp in cps:
            cp.start()
        for cp in cps:
            cp.wait_recv()
        for cp in cps:
            cp.wait_send()

    return pl.pallas_call(
        body,
        in_specs=[ANY] * n,
        out_specs=[ANY] * n,
        out_shape=[jax.ShapeDtypeStruct(a.shape, a.dtype) for a in arrs],
        scratch_shapes=[pltpu.SemaphoreType.DMA((n,)), pltpu.SemaphoreType.DMA((n,))],
        compiler_params=pltpu.CompilerParams(has_side_effects=True),
        name=name,
    )(*arrs)


def _allreduce_small(vec, name):
    rows = vec.shape[0]

    def body(v_ref, out_ref, gat_ref, send_sems, recv_sems):
        x, y, c = lax.axis_index("x"), lax.axis_index("y"), lax.axis_index("c")
        me = 4 * x + 2 * y + c

        def remote(k, slot):
            dx, dy, dc = (k >> 2) & 1, (k >> 1) & 1, k & 1
            return pltpu.make_async_remote_copy(
                src_ref=v_ref, dst_ref=gat_ref.at[slot], send_sem=send_sems.at[k - 1], recv_sem=recv_sems.at[k - 1],
                device_id=(_flip(x, dx), _flip(y, dy), _flip(c, dc)), device_id_type=MESH)

        gat_ref[me] = v_ref[...]
        for k in range(1, 8):
            remote(k, me).start()
        for k in range(1, 8):
            dx, dy, dc = (k >> 2) & 1, (k >> 1) & 1, k & 1
            remote(k, 4 * _flip(x, dx) + 2 * _flip(y, dy) + _flip(c, dc)).wait_recv()
        for k in range(1, 8):
            remote(k, me).wait_send()
        acc = gat_ref[0]
        for j in range(1, 8):
            acc = acc + gat_ref[j]
        out_ref[...] = acc

    vm = pl.BlockSpec(memory_space=pltpu.VMEM)
    return pl.pallas_call(
        body,
        in_specs=[vm],
        out_specs=vm,
        out_shape=jax.ShapeDtypeStruct(vec.shape, F32),
        scratch_shapes=[pltpu.VMEM((8, rows, 128), F32), pltpu.SemaphoreType.DMA((7,)), pltpu.SemaphoreType.DMA((7,))],
        compiler_params=pltpu.CompilerParams(has_side_effects=True),
        name=name,
    )(vec)


def _sum4(own, recv, name):
    shape = own.shape
    cols = shape[-1]
    rows = own.size // cols
    tr = 256
    assert rows % tr == 0

    def body(o_ref, r_ref, s_ref):
        s_ref[...] = ((o_ref[...] + r_ref[0]) + r_ref[1]) + r_ref[2]

    out = pl.pallas_call(
        body,
        grid=(rows // tr,),
        in_specs=[pl.BlockSpec((tr, cols), lambda i: (i, 0)), pl.BlockSpec((3, tr, cols), lambda i: (0, i, 0))],
        out_specs=pl.BlockSpec((tr, cols), lambda i: (i, 0)),
        out_shape=jax.ShapeDtypeStruct((rows, cols), F32),
        compiler_params=pltpu.CompilerParams(dimension_semantics=("arbitrary",), vmem_limit_bytes=VMEM_LIMIT),
        name=name,
    )(own.reshape(rows, cols), recv.reshape(3, rows, cols))
    return out.reshape(shape)


def _pad8(v, width, lane0=0):
    v = v.reshape(1, -1) if v.ndim == 1 else v
    return jnp.zeros((8, width), F32).at[:v.shape[0], lane0:lane0 + v.shape[1]].set(v.astype(F32))


def _relayout_w_in(w):
    z = lambda n: jnp.zeros(w.shape[:-1] + (n,), w.dtype)
    return jnp.concatenate([w[..., 0:2048], w[..., 2056:4616], w[..., 4632:6680],
                            w[..., 2048:2056], z(120), w[..., 4616:4632], z(112)], axis=-1)


def _unlayout_dw_in(dg, ds, dr, dgs, dss):
    return jnp.concatenate([dg, dgs[:, 0:8], ds, dss[:, 0:16], dr], axis=1)


TB = 256
TL = 256


def kernel(x, pre_norm, post_norm, w_in, gdn_conv, gdn_A_log, gdn_dt_bias, gdn_norm, ssd_conv, ssd_conv_b, ssd_A_log, ssd_dt_bias, ssd_D, ssd_norm, ret_norm, w_out, loss_target, m_pre_norm, m_post_norm, m_w_in, m_gdn_conv, m_gdn_A_log, m_gdn_dt_bias, m_gdn_norm, m_ssd_conv, m_ssd_conv_b, m_ssd_A_log, m_ssd_dt_bias, m_ssd_D, m_ssd_norm, m_ret_norm, m_w_out, v_pre_norm, v_post_norm, v_w_in, v_gdn_conv, v_gdn_A_log, v_gdn_dt_bias, v_gdn_norm, v_ssd_conv, v_ssd_conv_b, v_ssd_A_log, v_ssd_dt_bias, v_ssd_D, v_ssd_norm, v_ret_norm, v_w_out):
    seq = x.shape[1]
    chip = 2 * lax.axis_index("x") + lax.axis_index("y")
    x0 = x[0]

    wi_g, wo_g, gcv_g, scv_g = _ag_chips([w_in.astype(BF16), w_out.astype(BF16), gdn_conv, ssd_conv], "ag_weights")
    wp = _relayout_w_in(jnp.transpose(wi_g, (1, 2, 0, 3)).reshape(DEPTH, D_MODEL, N_IN))
    wo = jnp.transpose(wo_g, (1, 0, 2, 3)).reshape(DEPTH, 2048, D_MODEL)
    gcv = jnp.transpose(gcv_g, (1, 2, 0, 3)).reshape(DEPTH, CONV_W, 1536)
    scv = jnp.transpose(scv_g, (1, 2, 0, 3)).reshape(DEPTH, CONV_W, 1536)
    rope_c, rope_s = _rope_tables(seq)

    saved = []
    xc = x0
    for l in range(DEPTH):
        p = dict(
            pn=_pad8(pre_norm[l], D_MODEL), qn=_pad8(post_norm[l], D_MODEL),
            g_cw=_pad8(gcv[l], 1536), g_prm=_pad8(jnp.stack([gdn_A_log[l], gdn_dt_bias[l]]), 128, 4),
            g_nw=_pad8(gdn_norm[l], 128),
            s_cw=_pad8(scv[l], 1536), s_cb=_pad8(ssd_conv_b[l], 1536),
            s_prm=_pad8(jnp.stack([ssd_A_log[l], ssd_dt_bias[l], ssd_D[l]]), 128), s_nw=_pad8(ssd_norm[l], SSD_W),
            r_nw=_pad8(ret_norm[l], 128))
        pg, ps, pr, gs, ss = _make_inproj(seq, TL)(xc, p["pn"], wp[l])
        oa, stg = _make_gdn_fwd(seq, TB)(pg, gs, p["g_cw"], p["g_prm"], p["g_nw"])
        ob, sts = _make_ssd_fwd(seq, TB)(ps, ss, p["s_cw"], p["s_cb"], p["s_prm"], p["s_nw"])
        oc, str_ = _make_ret_fwd(seq, TB)(pr, rope_c, rope_s, p["r_nw"])
        out, xn = _make_outproj(seq, TL)(oa, ob, oc, wo[l], xc, p["qn"])
        saved.append(dict(p=p, x=xc, pg=pg, ps=ps, pr=pr, gs=gs, ss=ss, stg=stg, sts=sts, str=str_,
                          oa=oa, ob=ob, oc=oc, out=out))
        xc = xn

    dxn, lossp = _make_loss_head(seq, TL)(xc, loss_target[0])

    small = [None] * DEPTH
    dwi = [None] * DEPTH
    dwo = [None] * DEPTH
    for l in reversed(range(DEPTH)):
        s = saved[l]
        p = s["p"]
        doa, dob, doc, dqn, dwo[l] = _make_outproj_bwd(seq, TL)(dxn, s["out"], s["oa"], s["ob"], s["oc"], wo[l], p["qn"])
        dpg, dgs, dcw_g, dprm_g, dnw_g = _make_gdn_bwd(seq, TB)(s["pg"], s["gs"], p["g_cw"], p["g_prm"], p["g_nw"],
                                                                s["stg"], doa)
        dps, dss, dcw_s, dcb_s, dprm_s, dnw_s = _make_ssd_bwd(seq, TB)(s["ps"], s["ss"], p["s_cw"], p["s_cb"],
                                                                       p["s_prm"], p["s_nw"], s["sts"], dob)
        dpr, dnw_r = _make_ret_bwd(seq, TB)(s["pr"], rope_c, rope_s, p["r_nw"], s["str"], doc)
        dx, dpn = _make_inproj_bwd_dx(seq, TL)(dpg, dps, dpr, dgs, dss, wp[l], s["x"], p["pn"], dxn)
        dws = [_make_inproj_bwd_dw(seq, TL, d.shape[1], f"inproj_bwd_dw{i}")(s["x"], p["pn"], d)
               for i, d in enumerate((dpg, dps, dpr, dgs, dss))]
        dwi[l] = _unlayout_dw_in(dws[0], dws[1], dws[2], dws[3], dws[4])
        small[l] = [dpn[0], dqn[0], dcw_g[0:4].reshape(-1), dprm_g[0, 4:8], dprm_g[1, 4:8], dnw_g[0],
                    dcw_s[0:4].reshape(-1), dcb_s[0], dprm_s[0, 0:16], dprm_s[1, 0:16], dprm_s[2, 0:16],
                    dnw_s[0], dnw_r[0]]
        dxn = dx
    grad_x = dxn[None]

    sizes = [a.shape[0] for a in small[0]]
    flat = jnp.concatenate(small[0] + small[1] + [lossp[0, 0:1]])
    n_flat = flat.shape[0]
    rows = -(-n_flat // 1024) * 8
    red = _allreduce_small(jnp.pad(flat, (0, rows * 128 - n_flat)).reshape(rows, 128), "allreduce_small").reshape(-1)
    per = sum(sizes)
    loss = red[2 * per]

    def pick(i):
        off = sum(sizes[:i])
        return jnp.stack([red[l * per + off:l * per + off + sizes[i]] for l in range(DEPTH)])

    g_small = dict(
        pre_norm=pick(0), post_norm=pick(1),
        gdn_conv=lax.dynamic_slice_in_dim(pick(2).reshape(DEPTH, CONV_W, 1536), chip * 384, 384, axis=2),
        gdn_A_log=pick(3), gdn_dt_bias=pick(4), gdn_norm=pick(5),
        ssd_conv=lax.dynamic_slice_in_dim(pick(6).reshape(DEPTH, CONV_W, 1536), chip * 384, 384, axis=2),
        ssd_conv_b=pick(7), ssd_A_log=pick(8), ssd_dt_bias=pick(9), ssd_D=pick(10), ssd_norm=pick(11),
        ret_norm=pick(12))

    gin = jnp.transpose(jnp.stack(dwi).reshape(DEPTH, D_MODEL, 4, N_IN // 4), (2, 0, 1, 3))
    gout = jnp.transpose(jnp.stack(dwo).reshape(DEPTH, 4, 512, D_MODEL), (1, 0, 2, 3))
    r_in, r_out = _rs_chips([gin, gout], "rs_grads")
    s_in = _sum4(lax.dynamic_index_in_dim(gin, chip, 0, keepdims=False), r_in, "sum4_w_in")
    s_out = _sum4(lax.dynamic_index_in_dim(gout, chip, 0, keepdims=False), r_out, "sum4_w_out")
    t_in, t_out = _swap_sibling([s_in, s_out], "swap_grads")

    weights = dict(pre_norm=pre_norm, post_norm=post_norm, w_in=w_in, gdn_conv=gdn_conv, gdn_A_log=gdn_A_log,
                   gdn_dt_bias=gdn_dt_bias, gdn_norm=gdn_norm, ssd_conv=ssd_conv, ssd_conv_b=ssd_conv_b,
                   ssd_A_log=ssd_A_log, ssd_dt_bias=ssd_dt_bias, ssd_D=ssd_D, ssd_norm=ssd_norm, ret_norm=ret_norm,
                   w_out=w_out)
    ms = dict(pre_norm=m_pre_norm, post_norm=m_post_norm, w_in=m_w_in, gdn_conv=m_gdn_conv, gdn_A_log=m_gdn_A_log,
              gdn_dt_bias=m_gdn_dt_bias, gdn_norm=m_gdn_norm, ssd_conv=m_ssd_conv, ssd_conv_b=m_ssd_conv_b,
              ssd_A_log=m_ssd_A_log, ssd_dt_bias=m_ssd_dt_bias, ssd_D=m_ssd_D, ssd_norm=m_ssd_norm,
              ret_norm=m_ret_norm, w_out=m_w_out)
    vs = dict(pre_norm=v_pre_norm, post_norm=v_post_norm, w_in=v_w_in, gdn_conv=v_gdn_conv, gdn_A_log=v_gdn_A_log,
              gdn_dt_bias=v_gdn_dt_bias, gdn_norm=v_gdn_norm, ssd_conv=v_ssd_conv, ssd_conv_b=v_ssd_conv_b,
              ssd_A_log=v_ssd_A_log, ssd_dt_bias=v_ssd_dt_bias, ssd_D=v_ssd_D, ssd_norm=v_ssd_norm,
              ret_norm=v_ret_norm, w_out=v_w_out)
    names = list(weights)
    res = {}
    for nme in names:
        if nme == "w_in":
            res[nme] = _adamw(w_in, s_in, m_w_in, v_w_in, "adamw_w_in", g2=t_in)
        elif nme == "w_out":
            res[nme] = _adamw(w_out, s_out, m_w_out, v_w_out, "adamw_w_out", g2=t_out)
        else:
            res[nme] = _adamw(weights[nme], g_small[nme], ms[nme], vs[nme], "adamw_" + nme)
    return (loss, grad_x, *[res[n][0] for n in names], *[res[n][1] for n in names],
            *[res[n][2] for n in names], *[res[n][3] for n in names])
```

```python
import functools
import math

import jax
import jax.numpy as jnp
from jax import lax
from jax.experimental import pallas as pl
from jax.experimental.pallas import tpu as pltpu

F32 = jnp.float32
BF16 = jnp.bfloat16
HI = lax.Precision.HIGHEST

D_MODEL = 1024
DEPTH = 2
CH = 64
CONV_W = 4
EPS = 1e-6
GDN_H, GDN_D = 4, 128
SSD_H, SSD_P, SSD_N, SSD_G = 16, 64, 128, 2
SSD_W = SSD_H * SSD_P
RET_H, RET_D = 4, 128
ROPE_BASE = 10000.0
N_IN = 6680
NEG = -1e30

VMEM_LIMIT = 56 * 1024 * 1024


def _dot(a, b):
    return jnp.dot(a.astype(BF16), b.astype(BF16), preferred_element_type=F32)


def _dot_nt(a, b):
    return lax.dot_general(a.astype(BF16), b.astype(BF16), (((1,), (1,)), ((), ())), preferred_element_type=F32)


def _dot_tn(a, b):
    return lax.dot_general(a.astype(BF16), b.astype(BF16), (((0,), (0,)), ((), ())), preferred_element_type=F32)


def _dotx(a, b):
    return jnp.dot(a, b, preferred_element_type=F32, precision=HI)


def _dotx_tn(a, b):
    return lax.dot_general(a, b, (((0,), (0,)), ((), ())), preferred_element_type=F32, precision=HI)


def _sigmoid(x):
    return jax.nn.sigmoid(x)


def _silu(x):
    return x * _sigmoid(x)


def _dsilu(x):
    s = _sigmoid(x)
    return s * (1.0 + x * (1.0 - s))


def _softplus(x):
    return jnp.maximum(x, 0.0) + jnp.log1p(jnp.exp(-jnp.abs(x)))


def _iota2(shape, dim):
    return lax.broadcasted_iota(jnp.int32, shape, dim)


def _chunk_tri(tb, upper=False):
    r = _iota2((tb, tb), 0)
    c = _iota2((tb, tb), 1)
    same = jnp.right_shift(r, 6) == jnp.right_shift(c, 6)
    return (same & ((c >= r) if upper else (c <= r))).astype(F32)


def _masks():
    r = _iota2((CH, CH), 0)
    c = _iota2((CH, CH), 1)
    return r >= c, r > c, (r == c).astype(F32)


def _inv_unit_lower(a, eye):
    x = eye - a
    p = a
    for _ in range(5):
        p = _dotx(p, p)
        x = x + _dotx(x, p)
    return x


def _put_lane(col, lane_idx, width=128):
    lane = _iota2((col.shape[0], width), 1)
    return jnp.where(lane == lane_idx, col, 0.0)


def _conv_taps(raw, halo8, tb):
    ext = jnp.concatenate([halo8, raw], axis=0)
    return [raw] + [pltpu.roll(ext, s, axis=0)[8:] for s in (1, 2, 3)]


def _conv_back(dpre, nxt8, tb):
    ext = jnp.concatenate([dpre, nxt8], axis=0)
    return [dpre] + [pltpu.roll(ext, tb + 8 - s, axis=0)[:tb] for s in (1, 2, 3)]


def _rms_fwd(o, w, n):
    r = lax.rsqrt(jnp.sum(o * o, axis=-1, keepdims=True) * (1.0 / n) + EPS)
    on = o * r
    return on, r, on * w


def _rms_bwd(dy, on, r, w, n):
    don = dy * w
    return r * (don - on * (jnp.sum(don * on, axis=-1, keepdims=True) * (1.0 / n))), dy * on


def _put_cols(v, g, gw):
    z = jnp.zeros_like(v)
    return jnp.concatenate([v, z] if g == 0 else [z, v], axis=1)


def _gdn_common(pg_ref, halo8, sm, cw, prm, tb):
    raw = pg_ref[:, 0:1536]
    taps = _conv_taps(raw, halo8, tb)
    pre = taps[0] * cw[3:4, :] + taps[1] * cw[2:3, :] + taps[2] * cw[1:2, :] + taps[3] * cw[0:1, :]
    act = _silu(pre)
    beta = _sigmoid(sm)
    sp_in = sm + prm[1:2, :]
    g = -jnp.exp(prm[0:1, :]) * _softplus(sp_in)
    gc = _dotx(_chunk_tri(tb), g)
    return raw, taps, pre, act, beta, sp_in, g, gc


_NN = (((2,), (1,)), ((0,), (0,)))
_NT = (((2,), (2,)), ((0,), (0,)))
_TN = (((1,), (1,)), ((0,), (0,)))


def _bdot(a, b, dn):
    return lax.dot_general(a.astype(BF16), b.astype(BF16), dn, preferred_element_type=F32)


def _split(a):
    hi = a.astype(BF16)
    return hi, (a - hi.astype(F32)).astype(BF16)


def _dot3_parts(ah, al, bh, bl, dn):
    f = lambda p, q: lax.dot_general(p, q, dn, preferred_element_type=F32)
    return f(ah, bh) + (f(ah, bl) + f(al, bh))


def _bdot3(a, b, dn):
    ah, al = _split(a)
    bh, bl = _split(b)
    return _dot3_parts(ah, al, bh, bl, dn)


def _binv_unit_lower(a, eye):
    x = eye - a
    ph, pl_ = _split(a)
    for _ in range(5):
        ph, pl_ = _split(_dot3_parts(ph, pl_, ph, pl_, _NN))
        xh, xl = _split(x)
        x = x + _dot3_parts(xh, xl, ph, pl_, _NN)
    return x


def _rsum(v):
    return jnp.sum(v, axis=-1, keepdims=True)


def _gdn_batch(act, beta, gc, gct, eg_all, ncb, masks):
    causal, strict, _ = masks

    def st(fn):
        return jnp.stack([fn(c, h, slice(c * CH, (c + 1) * CH)) for c in range(ncb) for h in range(GDN_H)])

    qr = st(lambda c, h, r: act[r, h * 128:(h + 1) * 128])
    kr = st(lambda c, h, r: act[r, 512 + h * 128:512 + (h + 1) * 128])
    vh = st(lambda c, h, r: act[r, 1024 + h * 128:1024 + (h + 1) * 128])
    bh = st(lambda c, h, r: beta[r, h:h + 1])
    gcol = st(lambda c, h, r: gc[r, 4 + h:5 + h])
    grow = st(lambda c, h, r: gct[4 + h:5 + h, r])
    eg = st(lambda c, h, r: eg_all[r, 4 + h:5 + h])
    glast = st(lambda c, h, r: gc[(c + 1) * CH - 1:(c + 1) * CH, 4 + h:5 + h])
    rq = lax.rsqrt(_rsum(qr * qr) + EPS)
    rk = lax.rsqrt(_rsum(kr * kr) + EPS)
    qn = qr * rq
    kh = kr * rk
    qh = qn * (GDN_D ** -0.5)
    decay = jnp.exp(jnp.where(causal, gcol - grow, NEG))
    kb = kh * bh
    kd_scale = jnp.exp(glast - gcol)
    return dict(qn=qn, rq=rq, kh=kh, rk=rk, qh=qh, vh=vh, bh=bh, eg=eg, decay=decay, kb=kb, vb=vh * bh, kg=kb * eg,
                qg=qh * eg, kd_scale=kd_scale, kdec=kh * kd_scale, egl=jnp.exp(glast),
                a=jnp.where(strict, _bdot(kb, kh, _NT) * decay, 0.0), attn=_bdot(qh, kh, _NT) * decay)


def _make_gdn_fwd(seq, tb):
    ncb = tb // CH
    nb = seq // tb
    n = ncb * GDN_H

    def body(pg_ref, sm_ref, cw_ref, prm_ref, nw_ref, oa_ref, st_ref, ti_ref, uw_ref, s_scr, halo_scr):
        @pl.when(pl.program_id(0) == 0)
        def _():
            s_scr[...] = jnp.zeros_like(s_scr)
            halo_scr[...] = jnp.zeros_like(halo_scr)

        masks = _masks()
        sm = sm_ref[...]
        raw, _, _, act, beta, _, _, gc = _gdn_common(pg_ref, halo_scr[...], sm, cw_ref[...], prm_ref[...], tb)
        halo_scr[...] = raw[tb - 8:tb, :]
        d = _gdn_batch(act, beta, gc, gc.T, jnp.exp(gc), ncb, masks)
        t = _binv_unit_lower(d["a"], masks[2])
        sol = _bdot3(t, jnp.concatenate([d["vb"], d["kg"]], axis=2), _NN)
        ti_ref[...] = t.reshape(ncb, GDN_H, CH, CH)
        uw_ref[...] = sol.reshape(ncb, GDN_H, CH, 256)
        u, w = sol[:, :, :128], sol[:, :, 128:]
        vns = []
        for c in range(ncb):
            bs = slice(c * GDN_H, (c + 1) * GDN_H)
            s = s_scr[...]
            st_ref[c] = s
            vn = u[bs] - _bdot(w[bs], s, _NN)
            s_scr[...] = s * d["egl"][bs] + _bdot(d["kdec"][bs], vn, _TN)
            vns.append(vn)
        v_new = jnp.concatenate(vns, axis=0)
        s_prev = st_ref[...].reshape(n, 128, 128)
        o = _bdot(d["qg"], s_prev, _NN) + _bdot(d["attn"], v_new, _NN)
        _, _, y = _rms_fwd(o, nw_ref[0:1, :], GDN_D)
        for c in range(ncb):
            rows = slice(c * CH, (c + 1) * CH)
            for h in range(GDN_H):
                z = pg_ref[rows, 1536 + h * 128:1536 + (h + 1) * 128]
                oa_ref[rows, h * 128:(h + 1) * 128] = y[c * GDN_H + h] * _silu(z)

    def call(pg, sm, cw, prm, nw):
        blk4 = lambda i: (i, 0, 0, 0)
        return pl.pallas_call(
            body,
            grid=(nb,),
            in_specs=[
                pl.BlockSpec((tb, 2048), lambda i: (i, 0)),
                pl.BlockSpec((tb, 128), lambda i: (i, 0)),
                pl.BlockSpec((8, 1536), lambda i: (0, 0)),
                pl.BlockSpec((8, 128), lambda i: (0, 0)),
                pl.BlockSpec((8, 128), lambda i: (0, 0)),
            ],
            out_specs=[
                pl.BlockSpec((tb, 512), lambda i: (i, 0)),
                pl.BlockSpec((ncb, GDN_H, 128, 128), blk4),
                pl.BlockSpec((ncb, GDN_H, CH, CH), blk4),
                pl.BlockSpec((ncb, GDN_H, CH, 256), blk4),
            ],
            out_shape=[
                jax.ShapeDtypeStruct((seq, 512), F32),
                jax.ShapeDtypeStruct((seq // CH, GDN_H, 128, 128), F32),
                jax.ShapeDtypeStruct((seq // CH, GDN_H, CH, CH), F32),
                jax.ShapeDtypeStruct((seq // CH, GDN_H, CH, 256), F32),
            ],
            scratch_shapes=[pltpu.VMEM((GDN_H, 128, 128), F32), pltpu.VMEM((8, 1536), F32)],
            compiler_params=pltpu.CompilerParams(dimension_semantics=("arbitrary",), vmem_limit_bytes=VMEM_LIMIT),
            name="gdn_fwd",
        )(pg, sm, cw, prm, nw)

    return call


def _make_gdn_bwd(seq, tb):
    ncb = tb // CH
    nb = seq // tb
    hb = tb // 8
    n = ncb * GDN_H

    def body(pg_ref, prev_ref, sm_ref, cw_ref, prm_ref, nw_ref, st_ref, ti_ref, uw_ref, doa_ref,
             dpg_ref, dsm_ref, dcw_ref, dprm_ref, dnw_ref, ds_scr, nxt_scr):
        i = pl.program_id(0)

        @pl.when(i == 0)
        def _():
            ds_scr[...] = jnp.zeros_like(ds_scr)
            nxt_scr[...] = jnp.zeros_like(nxt_scr)
            dcw_ref[...] = jnp.zeros_like(dcw_ref)
            dprm_ref[...] = jnp.zeros_like(dprm_ref)
            dnw_ref[...] = jnp.zeros_like(dnw_ref)

        masks = _masks()
        strict = masks[1]
        sm = sm_ref[...]
        cw = cw_ref[...]
        prm = prm_ref[...]
        halo8 = jnp.where(i == nb - 1, 0.0, prev_ref[...])
        raw, taps, pre, act, beta, sp_in, g, gc = _gdn_common(pg_ref, halo8, sm, cw, prm, tb)
        nw = nw_ref[0:1, :]
        row_id = _iota2((CH, 1), 0)
        d = _gdn_batch(act, beta, gc, gc.T, jnp.exp(gc), ncb, masks)
        t = ti_ref[...].reshape(n, CH, CH)
        sol = uw_ref[...].reshape(n, CH, 256)
        u, w = sol[:, :, :128], sol[:, :, 128:]
        s_prev = st_ref[...].reshape(n, 128, 128)
        v_new = u - _bdot(w, s_prev, _NN)
        o = _bdot(d["qg"], s_prev, _NN) + _bdot(d["attn"], v_new, _NN)

        pairs = [(c, h) for c in range(ncb) for h in range(GDN_H)]
        z = jnp.stack([pg_ref[c * CH:(c + 1) * CH, 1536 + h * 128:1536 + (h + 1) * 128] for c, h in pairs])
        doa = jnp.stack([doa_ref[c * CH:(c + 1) * CH, h * 128:(h + 1) * 128] for c, h in pairs])
        on, r, y = _rms_fwd(o, nw, GDN_D)
        dz = doa * y * _dsilu(z)
        do, dnw_rows = _rms_bwd(doa * _silu(z), on, r, nw, GDN_D)
        dnw_acc = jnp.sum(jnp.sum(dnw_rows, axis=0), axis=0, keepdims=True)

        dvn_in = _bdot(d["attn"], do, _TN)
        qgtdo = _bdot(d["qg"], do, _TN)
        dvn_l, dkdec_l, dgl_l = [None] * ncb, [None] * ncb, [None] * ncb
        for c in reversed(range(ncb)):
            bs = slice(c * GDN_H, (c + 1) * GDN_H)
            dsn = ds_scr[...]
            dvn_c = dvn_in[bs] + _bdot(d["kdec"][bs], dsn, _NN)
            ds_scr[...] = d["egl"][bs] * dsn + qgtdo[bs] - _bdot(w[bs], dvn_c, _TN)
            dvn_l[c] = dvn_c
            dkdec_l[c] = _bdot(v_new[bs], dsn, _NT)
            dgl_l[c] = d["egl"][bs] * jnp.sum(_rsum(s_prev[bs] * dsn), axis=1, keepdims=True)
        dvn = jnp.concatenate(dvn_l, axis=0)
        dkdec = jnp.concatenate(dkdec_l, axis=0)
        dglast = jnp.concatenate(dgl_l, axis=0)

        dqg = _bdot(do, s_prev, _NT)
        dattn = _bdot(do, v_new, _NT)
        dw = -_bdot(dvn, s_prev, _NT)
        drhs = _bdot3(t, jnp.concatenate([dvn, dw], axis=2), _TN)
        dvb, dkg = drhs[:, :, :128], drhs[:, :, 128:]
        da = jnp.where(strict, -(_bdot(dvb, u, _NT) + _bdot(dkg, w, _NT)), 0.0)
        dp = da * d["decay"]
        dq_m = dattn * d["decay"]
        m = da * d["a"] + dattn * d["attn"]
        upper_tri = jnp.broadcast_to((_iota2((CH, CH), 1) >= _iota2((CH, CH), 0)).astype(BF16), (n, CH, CH))
        dg_in = _rsum(jnp.where(strict, _bdot(upper_tri, m, _NN), 0.0))
        dkb = _bdot(dp, d["kh"], _NN) + dkg * d["eg"]
        kdk_row = _rsum(dkdec * d["kdec"])
        dk = _bdot(dp, d["kb"], _TN) + _bdot(dq_m, d["qh"], _TN) + dkdec * d["kd_scale"] + dkb * d["bh"]
        dq = _bdot(dq_m, d["kh"], _NN) + dqg * d["eg"]
        dglast = dglast + jnp.sum(kdk_row, axis=1, keepdims=True)
        dgcol = (_rsum(dqg * d["qg"]) + _rsum(dkg * d["kg"]) - kdk_row + jnp.where(row_id == CH - 1, dglast, 0.0))
        dbeta = _rsum(dkb * d["kh"]) + _rsum(dvb * d["vh"])
        dn = dq * (GDN_D ** -0.5)
        dact_q = d["rq"] * (dn - d["qn"] * _rsum(dn * d["qn"]))
        dact_k = d["rk"] * (dk - d["kh"] * _rsum(dk * d["kh"]))
        dact_v = dvb * d["bh"]

        def lanes(v, lane0):
            return jnp.concatenate(
                [sum(_put_lane(v[c * GDN_H + h], lane0 + h) for h in range(GDN_H)) for c in range(ncb)], axis=0)

        def tokens(v):
            return jnp.concatenate(
                [jnp.concatenate([v[c * GDN_H + h] for h in range(GDN_H)], axis=1) for c in range(ncb)], axis=0)

        dbeta_all = lanes(dbeta, 0)
        dg = _dotx(_chunk_tri(tb, upper=True), lanes(dgcol, 4)) + lanes(dg_in, 4)
        neg_ea = -jnp.exp(prm[0:1, :])
        da_raw = dg * neg_ea * _sigmoid(sp_in)
        db_raw = dbeta_all * beta * (1.0 - beta)
        dsm_ref[...] = (da_raw + db_raw).astype(dsm_ref.dtype)
        lane8 = _iota2((8, 128), 1)
        sub8 = _iota2((8, 128), 0)
        dalog = jnp.sum(dg * g, axis=0, keepdims=True)
        ddtb = jnp.sum(da_raw, axis=0, keepdims=True)
        dprm_ref[...] += jnp.where(sub8 == 0, dalog, 0.0) + jnp.where(sub8 == 1, ddtb, 0.0)
        dnw_ref[...] += jnp.where(sub8 == 0, dnw_acc, 0.0)

        dact = jnp.concatenate([tokens(dact_q), tokens(dact_k), tokens(dact_v)], axis=1)
        dpre = dact * _dsilu(pre)
        back = _conv_back(dpre, nxt_scr[...], tb)
        nxt_scr[...] = dpre[0:8, :]
        draw = back[0] * cw[3:4, :] + back[1] * cw[2:3, :] + back[2] * cw[1:2, :] + back[3] * cw[0:1, :]
        dpg_ref[:, 0:1536] = draw.astype(dpg_ref.dtype)
        dpg_ref[:, 1536:2048] = tokens(dz).astype(dpg_ref.dtype)
        sub_c = _iota2((8, 1536), 0)
        dcw_new = jnp.zeros((8, 1536), F32)
        for s_ in range(CONV_W):
            dcw_new = dcw_new + jnp.where(sub_c == 3 - s_, jnp.sum(dpre * taps[s_], axis=0, keepdims=True), 0.0)
        dcw_ref[...] += dcw_new

    def call(pg, sm, cw, prm, nw, st, ti, uw, doa):
        rev = lambda i: (nb - 1 - i, 0)
        const = lambda i: (0, 0)
        return pl.pallas_call(
            body,
            grid=(nb,),
            in_specs=[
                pl.BlockSpec((tb, 2048), rev),
                pl.BlockSpec((8, 1536), lambda i: (jnp.maximum((nb - 1 - i) * hb - 1, 0), 0)),
                pl.BlockSpec((tb, 128), rev),
                pl.BlockSpec((8, 1536), const),
                pl.BlockSpec((8, 128), const),
                pl.BlockSpec((8, 128), const),
                pl.BlockSpec((ncb, GDN_H, 128, 128), lambda i: (nb - 1 - i, 0, 0, 0)),
                pl.BlockSpec((ncb, GDN_H, CH, CH), lambda i: (nb - 1 - i, 0, 0, 0)),
                pl.BlockSpec((ncb, GDN_H, CH, 256), lambda i: (nb - 1 - i, 0, 0, 0)),
                pl.BlockSpec((tb, 512), rev),
            ],
            out_specs=[
                pl.BlockSpec((tb, 2048), rev),
                pl.BlockSpec((tb, 128), rev),
                pl.BlockSpec((8, 1536), const),
                pl.BlockSpec((8, 128), const),
                pl.BlockSpec((8, 128), const),
            ],
            out_shape=[
                jax.ShapeDtypeStruct((seq, 2048), BF16),
                jax.ShapeDtypeStruct((seq, 128), BF16),
                jax.ShapeDtypeStruct((8, 1536), F32),
                jax.ShapeDtypeStruct((8, 128), F32),
                jax.ShapeDtypeStruct((8, 128), F32),
            ],
            scratch_shapes=[pltpu.VMEM((GDN_H, 128, 128), F32), pltpu.VMEM((8, 1536), F32)],
            compiler_params=pltpu.CompilerParams(dimension_semantics=("arbitrary",), vmem_limit_bytes=VMEM_LIMIT),
            name="gdn_bwd",
        )(pg, pg, sm, cw, prm, nw, st, ti, uw, doa)

    return call


def _expand_mat():
    r = _iota2((128, SSD_W), 0)
    c = _iota2((128, SSD_W), 1)
    return (jnp.right_shift(c, 6) == r).astype(F32)


def _reduce_heads(v, e):
    return lax.dot_general(v, e, (((1,), (1,)), ((), ())), preferred_element_type=F32, precision=HI)


def _row8(v):
    return jnp.broadcast_to(v, (8, v.shape[1]))


def _ssd_common(ps_ref, halo8, ss, cw, cb, prm, tb):
    raw = ps_ref[:, 0:1536]
    taps = _conv_taps(raw, halo8, tb)
    pre = taps[0] * cw[3:4, :] + taps[1] * cw[2:3, :] + taps[2] * cw[1:2, :] + taps[3] * cw[0:1, :] + cb[0:1, :]
    act = _silu(pre)
    dt_in = ss + prm[1:2, :]
    dt = _softplus(dt_in)
    a = dt * (-jnp.exp(prm[0:1, :]))
    acum = _dotx(_chunk_tri(tb), a)
    e = _expand_mat()
    dt_e = _dotx(dt, e)
    xdt = act[:, 0:SSD_W] * dt_e
    ea_e = _dotx(jnp.exp(acum), e)
    d_e = _dotx(_row8(prm[2:3, :]), e)[0:1, :]
    return raw, taps, pre, act, dt_in, dt, a, acum, e, dt_e, xdt, ea_e, d_e


def _ssd_chunk(act, acum, act_t, e, c):
    r0 = c * CH
    rows = slice(r0, r0 + CH)
    alast = acum[r0 + CH - 1:r0 + CH, :]
    wdec = jnp.exp(alast - acum[rows, :])
    wd_e = _dotx(wdec, e)
    eal_e = _dotx(_row8(jnp.exp(alast)), e)[0:1, :]
    return rows, wd_e, eal_e


def _ssd_lmat(acum, act_t, c, h, causal):
    r0 = c * CH
    acol = acum[r0:r0 + CH, h:h + 1]
    arow = act_t[h:h + 1, r0:r0 + CH]
    return jnp.exp(jnp.where(causal, acol - arow, NEG))


def _make_ssd_fwd(seq, tb):
    ncb = tb // CH
    nb = seq // tb
    hg = SSD_H // SSD_G
    gw = SSD_W // SSD_G

    def body(ps_ref, ss_ref, cw_ref, cb_ref, prm_ref, nw_ref, ob_ref, st_ref, hs_scr, halo_scr):
        @pl.when(pl.program_id(0) == 0)
        def _():
            hs_scr[...] = jnp.zeros_like(hs_scr)
            halo_scr[...] = jnp.zeros_like(halo_scr)

        causal, _, _ = _masks()
        (raw, _, _, act, _, _, _, acum, e, _, xdt, ea_e, d_e) = _ssd_common(
            ps_ref, halo_scr[...], ss_ref[...], cw_ref[...], cb_ref[...], prm_ref[...], tb)
        halo_scr[...] = raw[tb - 8:tb, :]
        act_t = acum.T
        nw = nw_ref[0:1, :]
        for c in range(ncb):
            rows, wd_e, eal_e = _ssd_chunk(act, acum, act_t, e, c)
            st_ref[c] = hs_scr[...]
            ys = []
            for g in range(SSD_G):
                gc_ = slice(g * gw, (g + 1) * gw)
                bg = act[rows, SSD_W + g * 128:SSD_W + (g + 1) * 128]
                cg = act[rows, SSD_W + 256 + g * 128:SSD_W + 256 + (g + 1) * 128]
                cbm = _dot_nt(cg, bg)
                hs = hs_scr[:, gc_]
                yin = _dot(cg, hs)
                yh = []
                for hh in range(hg):
                    h = g * hg + hh
                    lm = _ssd_lmat(acum, act_t, c, h, causal)
                    yh.append(_dot(cbm * lm, xdt[rows, h * SSD_P:(h + 1) * SSD_P]))
                ys.append(jnp.concatenate(yh, axis=1) + yin * ea_e[rows, gc_])
                hs_scr[:, gc_] = hs * eal_e[:, gc_] + _dot_tn(bg, xdt[rows, gc_] * wd_e[:, gc_])
            y = jnp.concatenate(ys, axis=1) + act[rows, 0:SSD_W] * d_e
            yz = y * _silu(ps_ref[rows, 1536:2560])
            outs = [_rms_fwd(yz[:, g * gw:(g + 1) * gw], nw[:, g * gw:(g + 1) * gw], gw)[2] for g in range(SSD_G)]
            ob_ref[rows, :] = jnp.concatenate(outs, axis=1)

    def call(ps, ss, cw, cb, prm, nw):
        const = lambda i: (0, 0)
        return pl.pallas_call(
            body,
            grid=(nb,),
            in_specs=[
                pl.BlockSpec((tb, 2560), lambda i: (i, 0)),
                pl.BlockSpec((tb, 128), lambda i: (i, 0)),
                pl.BlockSpec((8, 1536), const),
                pl.BlockSpec((8, 1536), const),
                pl.BlockSpec((8, 128), const),
                pl.BlockSpec((8, SSD_W), const),
            ],
            out_specs=[
                pl.BlockSpec((tb, SSD_W), lambda i: (i, 0)),
                pl.BlockSpec((ncb, SSD_N, SSD_W), lambda i: (i, 0, 0)),
            ],
            out_shape=[
                jax.ShapeDtypeStruct((seq, SSD_W), F32),
                jax.ShapeDtypeStruct((seq // CH, SSD_N, SSD_W), F32),
            ],
            scratch_shapes=[pltpu.VMEM((SSD_N, SSD_W), F32), pltpu.VMEM((8, 1536), F32)],
            compiler_params=pltpu.CompilerParams(dimension_semantics=("arbitrary",), vmem_limit_bytes=VMEM_LIMIT),
            name="ssd_fwd",
        )(ps, ss, cw, cb, prm, nw)

    return call


def _make_ssd_bwd(seq, tb):
    ncb = tb // CH
    nb = seq // tb
    hb = tb // 8
    hg = SSD_H // SSD_G
    gw = SSD_W // SSD_G

    def body(ps_ref, prev_ref, ss_ref, cw_ref, cb_ref, prm_ref, nw_ref, st_ref, dob_ref,
             dps_ref, dss_ref, dcw_ref, dcb_ref, dprm_ref, dnw_ref, dhs_scr, nxt_scr):
        i = pl.program_id(0)

        @pl.when(i == 0)
        def _():
            dhs_scr[...] = jnp.zeros_like(dhs_scr)
            nxt_scr[...] = jnp.zeros_like(nxt_scr)
            dcw_ref[...] = jnp.zeros_like(dcw_ref)
            dcb_ref[...] = jnp.zeros_like(dcb_ref)
            dprm_ref[...] = jnp.zeros_like(dprm_ref)
            dnw_ref[...] = jnp.zeros_like(dnw_ref)

        causal, _, _ = _masks()
        cw = cw_ref[...]
        prm = prm_ref[...]
        halo8 = jnp.where(i == nb - 1, 0.0, prev_ref[...])
        (raw, taps, pre, act, dt_in, dt, a, acum, e, dt_e, xdt, ea_e, d_e) = _ssd_common(
            ps_ref, halo8, ss_ref[...], cw, cb_ref[...], prm, tb)
        act_t = acum.T
        nw = nw_ref[0:1, :]
        row_id = _iota2((CH, 1), 0)

        dx_l, db_l, dc_l, dz_l, dacum_l, ddt_l, da_in_l = ([None] * ncb for _ in range(7))
        upper_tri = (_iota2((CH, CH), 1) >= _iota2((CH, CH), 0)).astype(F32)
        below = jnp.bitwise_and(_iota2((CH, gw), 1), CH - 1) < _iota2((CH, gw), 0)
        dnw_acc = jnp.zeros((1, SSD_W), F32)
        dd_acc = jnp.zeros((1, SSD_W), F32)

        for c in reversed(range(ncb)):
            rows, wd_e, eal_e = _ssd_chunk(act, acum, act_t, e, c)
            xc = act[rows, 0:SSD_W]
            z = ps_ref[rows, 1536:2560]
            dob = dob_ref[rows, :]
            sz = _silu(z)
            dy_g, dz_g, zacc_g, dxdt_g, dal_g, db_g, dc_g, da_in_g = [], [], [], [], [], [], [], []
            for g in range(SSD_G):
                gc_ = slice(g * gw, (g + 1) * gw)
                bg = act[rows, SSD_W + g * 128:SSD_W + (g + 1) * 128]
                cg = act[rows, SSD_W + 256 + g * 128:SSD_W + 256 + (g + 1) * 128]
                cbm = _dot_nt(cg, bg)
                hs = st_ref[c, :, gc_]
                yin = _dot(cg, hs)
                lms, yh = [], []
                for hh in range(hg):
                    h = g * hg + hh
                    lm = cbm * _ssd_lmat(acum, act_t, c, h, causal)
                    lms.append(lm)
                    yh.append(_dot(lm, xdt[rows, h * SSD_P:(h + 1) * SSD_P]))
                y_intra = jnp.concatenate(yh, axis=1)
                ea_g = ea_e[rows, gc_]
                y = y_intra + yin * ea_g + xc[:, gc_] * d_e[:, gc_]
                yz = y * sz[:, gc_]
                on, r, _ = _rms_fwd(yz, nw[:, gc_], gw)
                dyz, dnw_rows = _rms_bwd(dob[:, gc_], on, r, nw[:, gc_], gw)
                dnw_acc = dnw_acc + _put_cols(jnp.sum(dnw_rows, axis=0, keepdims=True), g, gw)
                dy = dyz * sz[:, gc_]
                dz_g.append(dyz * y * _dsilu(z[:, gc_]))
                dd_acc = dd_acc + _put_cols(jnp.sum(dy * xc[:, gc_], axis=0, keepdims=True), g, gw)
                dhs_n = dhs_scr[:, gc_]
                dyin = dy * ea_g
                dcg = _dot_nt(dyin, hs)
                xw = xdt[rows, gc_] * wd_e[:, gc_]
                dbg = _dot_nt(xw, dhs_n)
                dxw = _dot(bg, dhs_n)
                dhs_scr[:, gc_] = dhs_n * eal_e[:, gc_] + _dot_tn(cg, dyin)
                dal_g.append(jnp.sum(hs * dhs_n, axis=0, keepdims=True) * eal_e[:, gc_]
                             + jnp.sum(dxw * xw, axis=0, keepdims=True))
                dxi, ms, dcbm = [], [], jnp.zeros((CH, CH), F32)
                for hh in range(hg):
                    h = g * hg + hh
                    hc = slice(hh * SSD_P, (hh + 1) * SSD_P)
                    dyh = dy[:, hc]
                    dxi.append(_dot_tn(lms[hh], dyh))
                    dlm = _dot_nt(dyh, xdt[rows, h * SSD_P:(h + 1) * SSD_P])
                    ms.append(dlm * lms[hh])
                    dcbm = dcbm + dlm * _ssd_lmat(acum, act_t, c, h, causal)
                dx_intra = jnp.concatenate(dxi, axis=1)
                ncat = _dot(upper_tri, jnp.concatenate(ms, axis=1))
                da_in_g.append(jnp.where(below, ncat, 0.0))
                zacc_g.append(dy * yin * ea_g - dxw * xw)
                dxdt_g.append(dx_intra + dxw * wd_e[:, gc_])
                dy_g.append(dy)
                db_g.append(dbg + _dot_tn(dcbm, cg))
                dc_g.append(dcg + _dot(dcbm, bg))
            dy = jnp.concatenate(dy_g, axis=1)
            dxdt = jnp.concatenate(dxdt_g, axis=1)
            dx_l[c] = dxdt * dt_e[rows, :] + dy * d_e
            db_l[c] = jnp.concatenate(db_g, axis=1)
            dc_l[c] = jnp.concatenate(dc_g, axis=1)
            dz_l[c] = jnp.concatenate(dz_g, axis=1)
            ddt_l[c] = _reduce_heads(dxdt * xc, e)
            dalast = _reduce_heads(_row8(jnp.concatenate(dal_g, axis=1)), e)[0:1, :]
            dacum_l[c] = _reduce_heads(jnp.concatenate(zacc_g, axis=1), e) + jnp.where(row_id == CH - 1, dalast, 0.0)
            da_in_l[c] = _reduce_heads(jnp.concatenate(da_in_g, axis=1), e)

        dacum_all = jnp.concatenate(dacum_l, axis=0)
        da = _dotx(_chunk_tri(tb, upper=True), dacum_all) + jnp.concatenate(da_in_l, axis=0)
        neg_ea = -jnp.exp(prm[0:1, :])
        ddt = jnp.concatenate(ddt_l, axis=0) + da * neg_ea
        ddt_in = ddt * _sigmoid(dt_in)
        dss_ref[...] = ddt_in.astype(dss_ref.dtype)
        sub8 = _iota2((8, 128), 0)
        dalog = jnp.sum(da * a, axis=0, keepdims=True)
        ddtb = jnp.sum(ddt_in, axis=0, keepdims=True)
        dd = _reduce_heads(_row8(dd_acc), e)[0:1, :]
        dprm_ref[...] += (jnp.where(sub8 == 0, dalog, 0.0) + jnp.where(sub8 == 1, ddtb, 0.0)
                          + jnp.where(sub8 == 2, dd, 0.0))
        dnw_ref[...] += jnp.where(_iota2((8, SSD_W), 0) == 0, dnw_acc, 0.0)

        dact = jnp.concatenate([jnp.concatenate(dx_l, axis=0), jnp.concatenate(db_l, axis=0),
                                jnp.concatenate(dc_l, axis=0)], axis=1)
        dpre = dact * _dsilu(pre)
        back = _conv_back(dpre, nxt_scr[...], tb)
        nxt_scr[...] = dpre[0:8, :]
        draw = back[0] * cw[3:4, :] + back[1] * cw[2:3, :] + back[2] * cw[1:2, :] + back[3] * cw[0:1, :]
        dps_ref[:, 0:1536] = draw.astype(dps_ref.dtype)
        dps_ref[:, 1536:2560] = jnp.concatenate(dz_l, axis=0).astype(dps_ref.dtype)
        sub_c = _iota2((8, 1536), 0)
        dcw_new = jnp.zeros((8, 1536), F32)
        for s_ in range(CONV_W):
            dcw_new = dcw_new + jnp.where(sub_c == 3 - s_, jnp.sum(dpre * taps[s_], axis=0, keepdims=True), 0.0)
        dcw_ref[...] += dcw_new
        dcb_ref[...] += jnp.where(sub_c == 0, jnp.sum(dpre, axis=0, keepdims=True), 0.0)

    def call(ps, ss, cw, cb, prm, nw, st, dob):
        rev = lambda i: (nb - 1 - i, 0)
        const = lambda i: (0, 0)
        return pl.pallas_call(
            body,
            grid=(nb,),
            in_specs=[
                pl.BlockSpec((tb, 2560), rev),
                pl.BlockSpec((8, 1536), lambda i: (jnp.maximum((nb - 1 - i) * hb - 1, 0), 0)),
                pl.BlockSpec((tb, 128), rev),
                pl.BlockSpec((8, 1536), const),
                pl.BlockSpec((8, 1536), const),
                pl.BlockSpec((8, 128), const),
                pl.BlockSpec((8, SSD_W), const),
                pl.BlockSpec((ncb, SSD_N, SSD_W), lambda i: (nb - 1 - i, 0, 0)),
                pl.BlockSpec((tb, SSD_W), rev),
            ],
            out_specs=[
                pl.BlockSpec((tb, 2560), rev),
                pl.BlockSpec((tb, 128), rev),
                pl.BlockSpec((8, 1536), const),
                pl.BlockSpec((8, 1536), const),
                pl.BlockSpec((8, 128), const),
                pl.BlockSpec((8, SSD_W), const),
            ],
            out_shape=[
                jax.ShapeDtypeStruct((seq, 2560), BF16),
                jax.ShapeDtypeStruct((seq, 128), BF16),
                jax.ShapeDtypeStruct((8, 1536), F32),
                jax.ShapeDtypeStruct((8, 1536), F32),
                jax.ShapeDtypeStruct((8, 128), F32),
                jax.ShapeDtypeStruct((8, SSD_W), F32),
            ],
            scratch_shapes=[pltpu.VMEM((SSD_N, SSD_W), F32), pltpu.VMEM((8, 1536), F32)],
            compiler_params=pltpu.CompilerParams(dimension_semantics=("arbitrary",), vmem_limit_bytes=VMEM_LIMIT),
            name="ssd_bwd",
        )(ps, ps, ss, cw, cb, prm, nw, st, dob)

    return call


def _ret_consts(h):
    lg = math.log(1.0 - 2.0 ** (-5.0 - h))
    r = _iota2((CH, CH), 0)
    c = _iota2((CH, CH), 1)
    rel = (r - c).astype(F32)
    dmat = jnp.where(r >= c, jnp.exp(jnp.maximum(rel, 0.0) * lg), 0.0)
    idx = _iota2((CH, 1), 0).astype(F32)
    qdec = jnp.exp((idx + 1.0) * lg)
    kdec = jnp.exp((CH - 1.0 - idx) * lg)
    cdec = math.exp(CH * lg)
    return dmat, qdec, kdec, cdec


def _rot(t, cc, ss):
    return t * cc + pltpu.roll(t, 64, axis=1) * ss


def _rot_bwd(d, cc, ss):
    return d * cc + pltpu.roll(d * ss, 64, axis=1)


def _make_ret_fwd(seq, tb):
    ncb = tb // CH
    nb = seq // tb

    def body(pr_ref, cc_ref, ss_ref, nw_ref, oc_ref, st_ref, r_scr):
        @pl.when(pl.program_id(0) == 0)
        def _():
            r_scr[...] = jnp.zeros_like(r_scr)

        nw = nw_ref[0:1, :]
        for h in range(RET_H):
            dmat, qdec, kdec, cdec = _ret_consts(h)
            hc = slice(h * 128, (h + 1) * 128)
            for c in range(ncb):
                rows = slice(c * CH, (c + 1) * CH)
                cc, ss = cc_ref[rows, :], ss_ref[rows, :]
                q = _rot(pr_ref[rows, h * 128:(h + 1) * 128], cc, ss)
                k = _rot(pr_ref[rows, 512 + h * 128:512 + (h + 1) * 128], cc, ss) * (RET_D ** -0.5)
                v = pr_ref[rows, 1024 + h * 128:1024 + (h + 1) * 128]
                rs = r_scr[h]
                st_ref[c, h] = rs
                s = _dot_nt(q, k) * dmat
                o = _dot(s, v) + _dot(q, rs) * qdec
                r_scr[h] = rs * cdec + _dot_tn(k * kdec, v)
                _, _, y = _rms_fwd(o, nw, RET_D)
                oc_ref[rows, hc] = y * _silu(pr_ref[rows, 1536 + h * 128:1536 + (h + 1) * 128])

    def call(pr, cc, ss, nw):
        return pl.pallas_call(
            body,
            grid=(nb,),
            in_specs=[
                pl.BlockSpec((tb, 2048), lambda i: (i, 0)),
                pl.BlockSpec((tb, 128), lambda i: (i, 0)),
                pl.BlockSpec((tb, 128), lambda i: (i, 0)),
                pl.BlockSpec((8, 128), lambda i: (0, 0)),
            ],
            out_specs=[
                pl.BlockSpec((tb, 512), lambda i: (i, 0)),
                pl.BlockSpec((ncb, RET_H, 128, 128), lambda i: (i, 0, 0, 0)),
            ],
            out_shape=[
                jax.ShapeDtypeStruct((seq, 512), F32),
                jax.ShapeDtypeStruct((seq // CH, RET_H, 128, 128), F32),
            ],
            scratch_shapes=[pltpu.VMEM((RET_H, 128, 128), F32)],
            compiler_params=pltpu.CompilerParams(dimension_semantics=("arbitrary",), vmem_limit_bytes=VMEM_LIMIT),
            name="ret_fwd",
        )(pr, cc, ss, nw)

    return call


def _make_ret_bwd(seq, tb):
    ncb = tb // CH
    nb = seq // tb

    def body(pr_ref, cc_ref, ss_ref, nw_ref, st_ref, doc_ref, dpr_ref, dnw_ref, dr_scr):
        @pl.when(pl.program_id(0) == 0)
        def _():
            dr_scr[...] = jnp.zeros_like(dr_scr)
            dnw_ref[...] = jnp.zeros_like(dnw_ref)

        nw = nw_ref[0:1, :]
        dnw_acc = jnp.zeros((1, 128), F32)
        scale = RET_D ** -0.5
        for h in range(RET_H):
            dmat, qdec, kdec, cdec = _ret_consts(h)
            for c in reversed(range(ncb)):
                rows = slice(c * CH, (c + 1) * CH)
                cc, ss = cc_ref[rows, :], ss_ref[rows, :]
                q = _rot(pr_ref[rows, h * 128:(h + 1) * 128], cc, ss)
                k = _rot(pr_ref[rows, 512 + h * 128:512 + (h + 1) * 128], cc, ss) * scale
                v = pr_ref[rows, 1024 + h * 128:1024 + (h + 1) * 128]
                z = pr_ref[rows, 1536 + h * 128:1536 + (h + 1) * 128]
                rs = st_ref[c, h]
                s = _dot_nt(q, k) * dmat
                o = _dot(s, v) + _dot(q, rs) * qdec
                doc = doc_ref[rows, h * 128:(h + 1) * 128]
                on, r, y = _rms_fwd(o, nw, RET_D)
                dz = doc * y * _dsilu(z)
                do, dnw_rows = _rms_bwd(doc * _silu(z), on, r, nw, RET_D)
                dnw_acc = dnw_acc + jnp.sum(dnw_rows, axis=0, keepdims=True)
                drn = dr_scr[h]
                dqd = do * qdec
                ds = _dot_nt(do, v) * dmat
                kd = k * kdec
                dq = _dot(ds, k) + _dot_nt(dqd, rs)
                dk = _dot_tn(ds, q) + _dot_nt(v, drn) * kdec
                dv = _dot_tn(s, do) + _dot(kd, drn)
                dr_scr[h] = _dot_tn(q, dqd) + cdec * drn
                dpr_ref[rows, h * 128:(h + 1) * 128] = _rot_bwd(dq, cc, ss).astype(dpr_ref.dtype)
                dpr_ref[rows, 512 + h * 128:512 + (h + 1) * 128] = _rot_bwd(dk * scale, cc, ss).astype(dpr_ref.dtype)
                dpr_ref[rows, 1024 + h * 128:1024 + (h + 1) * 128] = dv.astype(dpr_ref.dtype)
                dpr_ref[rows, 1536 + h * 128:1536 + (h + 1) * 128] = dz.astype(dpr_ref.dtype)
        dnw_ref[...] += jnp.where(_iota2((8, 128), 0) == 0, dnw_acc, 0.0)

    def call(pr, cc, ss, nw, st, doc):
        rev = lambda i: (nb - 1 - i, 0)
        return pl.pallas_call(
            body,
            grid=(nb,),
            in_specs=[
                pl.BlockSpec((tb, 2048), rev),
                pl.BlockSpec((tb, 128), rev),
                pl.BlockSpec((tb, 128), rev),
                pl.BlockSpec((8, 128), lambda i: (0, 0)),
                pl.BlockSpec((ncb, RET_H, 128, 128), lambda i: (nb - 1 - i, 0, 0, 0)),
                pl.BlockSpec((tb, 512), rev),
            ],
            out_specs=[
                pl.BlockSpec((tb, 2048), rev),
                pl.BlockSpec((8, 128), lambda i: (0, 0)),
            ],
            out_shape=[
                jax.ShapeDtypeStruct((seq, 2048), BF16),
                jax.ShapeDtypeStruct((8, 128), F32),
            ],
            scratch_shapes=[pltpu.VMEM((RET_H, 128, 128), F32)],
            compiler_params=pltpu.CompilerParams(dimension_semantics=("arbitrary",), vmem_limit_bytes=VMEM_LIMIT),
            name="ret_bwd",
        )(pr, cc, ss, nw, st, doc)

    return call


def _rope_tables(seq):
    half = RET_D // 2
    inv = ROPE_BASE ** (-jnp.arange(half, dtype=F32) / half)
    ang = jnp.arange(seq, dtype=jnp.int32).astype(F32)[:, None] * inv[None, :]
    cos, sin = jnp.cos(ang), jnp.sin(ang)
    return jnp.concatenate([cos, cos], axis=1), jnp.concatenate([-sin, sin], axis=1)


SEG_G, SEG_S, SEG_R, SEG_GS, SEG_SS = (0, 2048), (2048, 4608), (4608, 6656), (6656, 6784), (6784, 6912)
NP = 6912
SEGS = (SEG_G, SEG_S, SEG_R, SEG_GS, SEG_SS)


def _resident(shape):
    return pl.BlockSpec(shape, lambda i: (0,) * len(shape), pipeline_mode=pl.Buffered(1))


def _make_inproj(seq, tl):
    def body(x_ref, pn_ref, w_ref, pg_ref, ps_ref, pr_ref, gs_ref, ss_ref):
        x = x_ref[...]
        _, _, hn = _rms_fwd(x, pn_ref[0:1, :], D_MODEL)
        h = hn.astype(BF16)
        for (a, b), o_ref in zip(SEGS, (pg_ref, ps_ref, pr_ref, gs_ref, ss_ref)):
            o_ref[...] = jnp.dot(h, w_ref[:, a:b], preferred_element_type=F32)

    def call(x, pn, w):
        row = lambda i: (i, 0)
        return pl.pallas_call(
            body,
            grid=(seq // tl,),
            in_specs=[pl.BlockSpec((tl, D_MODEL), row), _resident((8, D_MODEL)), _resident((D_MODEL, NP))],
            out_specs=[pl.BlockSpec((tl, b - a), row) for a, b in SEGS],
            out_shape=[jax.ShapeDtypeStruct((seq, b - a), F32) for a, b in SEGS],
            compiler_params=pltpu.CompilerParams(dimension_semantics=("arbitrary",), vmem_limit_bytes=VMEM_LIMIT),
            name="inproj",
        )(x, pn, w)

    return call


def _make_outproj(seq, tl):
    def body(oa_ref, ob_ref, oc_ref, w_ref, x_ref, qn_ref, out_ref, xn_ref):
        out = (jnp.dot(oa_ref[...].astype(BF16), w_ref[0:512, :], preferred_element_type=F32)
               + jnp.dot(ob_ref[...].astype(BF16), w_ref[512:1536, :], preferred_element_type=F32)
               + jnp.dot(oc_ref[...].astype(BF16), w_ref[1536:2048, :], preferred_element_type=F32))
        out_ref[...] = out
        _, _, y = _rms_fwd(out, qn_ref[0:1, :], D_MODEL)
        xn_ref[...] = x_ref[...] + y

    def call(oa, ob, oc, w, x, qn):
        row = lambda i: (i, 0)
        return pl.pallas_call(
            body,
            grid=(seq // tl,),
            in_specs=[pl.BlockSpec((tl, 512), row), pl.BlockSpec((tl, 1024), row), pl.BlockSpec((tl, 512), row),
                      _resident((2048, D_MODEL)), pl.BlockSpec((tl, D_MODEL), row), _resident((8, D_MODEL))],
            out_specs=[pl.BlockSpec((tl, D_MODEL), row), pl.BlockSpec((tl, D_MODEL), row)],
            out_shape=[jax.ShapeDtypeStruct((seq, D_MODEL), F32), jax.ShapeDtypeStruct((seq, D_MODEL), F32)],
            compiler_params=pltpu.CompilerParams(dimension_semantics=("arbitrary",), vmem_limit_bytes=VMEM_LIMIT),
            name="outproj",
        )(oa, ob, oc, w, x, qn)

    return call


def _make_loss_head(seq, tl):
    def body(y_ref, t_ref, dy_ref, loss_ref):
        @pl.when(pl.program_id(0) == 0)
        def _():
            loss_ref[...] = jnp.zeros_like(loss_ref)

        err = y_ref[...] - t_ref[...]
        dy_ref[...] = err * (1.0 / D_MODEL)
        part = jnp.sum(jnp.sum(err * err, axis=1, keepdims=True), axis=0, keepdims=True) * (0.5 / D_MODEL)
        loss_ref[...] += jnp.where((_iota2((8, 128), 0) == 0) & (_iota2((8, 128), 1) == 0), part, 0.0)

    def call(y, t):
        row = lambda i: (i, 0)
        return pl.pallas_call(
            body,
            grid=(seq // tl,),
            in_specs=[pl.BlockSpec((tl, D_MODEL), row), pl.BlockSpec((tl, D_MODEL), row)],
            out_specs=[pl.BlockSpec((tl, D_MODEL), row), pl.BlockSpec((8, 128), lambda i: (0, 0))],
            out_shape=[jax.ShapeDtypeStruct((seq, D_MODEL), F32), jax.ShapeDtypeStruct((8, 128), F32)],
            compiler_params=pltpu.CompilerParams(dimension_semantics=("arbitrary",)),
            name="loss_head",
        )(y, t)

    return call


def _make_outproj_bwd(seq, tl):
    def body(dxn_ref, out_ref, oa_ref, ob_ref, oc_ref, w_ref, qn_ref, doa_ref, dob_ref, doc_ref, dqn_ref, dw_ref):
        @pl.when(pl.program_id(0) == 0)
        def _():
            dqn_ref[...] = jnp.zeros_like(dqn_ref)
            dw_ref[...] = jnp.zeros_like(dw_ref)

        qn = qn_ref[0:1, :]
        on, r, _ = _rms_fwd(out_ref[...], qn, D_MODEL)
        dout, dqn_rows = _rms_bwd(dxn_ref[...], on, r, qn, D_MODEL)
        dqn_ref[...] += jnp.where(_iota2((8, D_MODEL), 0) == 0, jnp.sum(dqn_rows, axis=0, keepdims=True), 0.0)
        db = dout.astype(BF16)
        nt = (((1,), (1,)), ((), ()))
        tn = (((0,), (0,)), ((), ()))
        doa_ref[...] = lax.dot_general(db, w_ref[0:512, :], nt, preferred_element_type=F32)
        dob_ref[...] = lax.dot_general(db, w_ref[512:1536, :], nt, preferred_element_type=F32)
        doc_ref[...] = lax.dot_general(db, w_ref[1536:2048, :], nt, preferred_element_type=F32)
        dw_ref[0:512, :] += lax.dot_general(oa_ref[...].astype(BF16), db, tn, preferred_element_type=F32)
        dw_ref[512:1536, :] += lax.dot_general(ob_ref[...].astype(BF16), db, tn, preferred_element_type=F32)
        dw_ref[1536:2048, :] += lax.dot_general(oc_ref[...].astype(BF16), db, tn, preferred_element_type=F32)

    def call(dxn, out, oa, ob, oc, w, qn):
        row = lambda i: (i, 0)
        const = lambda i: (0, 0)
        return pl.pallas_call(
            body,
            grid=(seq // tl,),
            in_specs=[pl.BlockSpec((tl, D_MODEL), row), pl.BlockSpec((tl, D_MODEL), row),
                      pl.BlockSpec((tl, 512), row), pl.BlockSpec((tl, 1024), row), pl.BlockSpec((tl, 512), row),
                      _resident((2048, D_MODEL)), _resident((8, D_MODEL))],
            out_specs=[pl.BlockSpec((tl, 512), row), pl.BlockSpec((tl, 1024), row), pl.BlockSpec((tl, 512), row),
                       pl.BlockSpec((8, D_MODEL), const), pl.BlockSpec((2048, D_MODEL), const)],
            out_shape=[jax.ShapeDtypeStruct((seq, 512), F32), jax.ShapeDtypeStruct((seq, 1024), F32),
                       jax.ShapeDtypeStruct((seq, 512), F32), jax.ShapeDtypeStruct((8, D_MODEL), F32),
                       jax.ShapeDtypeStruct((2048, D_MODEL), F32)],
            compiler_params=pltpu.CompilerParams(dimension_semantics=("arbitrary",), vmem_limit_bytes=VMEM_LIMIT),
            name="outproj_bwd",
        )(dxn, out, oa, ob, oc, w, qn)

    return call


def _make_inproj_bwd_dx(seq, tl):
    def body(dg_ref, ds_ref, dr_ref, dgs_ref, dss_ref, w_ref, x_ref, pn_ref, dxn_ref, dx_ref, dpn_ref):
        @pl.when(pl.program_id(0) == 0)
        def _():
            dpn_ref[...] = jnp.zeros_like(dpn_ref)

        nt = (((1,), (1,)), ((), ()))
        dh = jnp.zeros((tl, D_MODEL), F32)
        for (a, b), d_ref in zip(SEGS, (dg_ref, ds_ref, dr_ref, dgs_ref, dss_ref)):
            dh = dh + lax.dot_general(d_ref[...], w_ref[:, a:b], nt, preferred_element_type=F32)
        pn = pn_ref[0:1, :]
        on, r, _ = _rms_fwd(x_ref[...], pn, D_MODEL)
        dx, dpn_rows = _rms_bwd(dh, on, r, pn, D_MODEL)
        dx_ref[...] = dx + dxn_ref[...]
        dpn_ref[...] += jnp.where(_iota2((8, D_MODEL), 0) == 0, jnp.sum(dpn_rows, axis=0, keepdims=True), 0.0)

    def call(dg, ds, dr, dgs, dss, w, x, pn, dxn):
        row = lambda i: (i, 0)
        return pl.pallas_call(
            body,
            grid=(seq // tl,),
            in_specs=[pl.BlockSpec((tl, b - a), row) for a, b in SEGS]
            + [_resident((D_MODEL, NP)), pl.BlockSpec((tl, D_MODEL), row), _resident((8, D_MODEL)),
               pl.BlockSpec((tl, D_MODEL), row)],
            out_specs=[pl.BlockSpec((tl, D_MODEL), row), pl.BlockSpec((8, D_MODEL), lambda i: (0, 0))],
            out_shape=[jax.ShapeDtypeStruct((seq, D_MODEL), F32), jax.ShapeDtypeStruct((8, D_MODEL), F32)],
            compiler_params=pltpu.CompilerParams(dimension_semantics=("arbitrary",), vmem_limit_bytes=VMEM_LIMIT),
            name="inproj_bwd_dx",
        )(dg, ds, dr, dgs, dss, w, x, pn, dxn)

    return call


def _make_inproj_bwd_dw(seq, tl, width, name):
    def body(x_ref, pn_ref, d_ref, dw_ref):
        @pl.when(pl.program_id(0) == 0)
        def _():
            dw_ref[...] = jnp.zeros_like(dw_ref)

        _, _, hn = _rms_fwd(x_ref[...], pn_ref[0:1, :], D_MODEL)
        dw_ref[...] += lax.dot_general(hn.astype(BF16), d_ref[...], (((0,), (0,)), ((), ())),
                                       preferred_element_type=F32)

    def call(x, pn, d):
        row = lambda i: (i, 0)
        return pl.pallas_call(
            body,
            grid=(seq // tl,),
            in_specs=[pl.BlockSpec((tl, D_MODEL), row), _resident((8, D_MODEL)), pl.BlockSpec((tl, width), row)],
            out_specs=pl.BlockSpec((D_MODEL, width), lambda i: (0, 0)),
            out_shape=jax.ShapeDtypeStruct((D_MODEL, width), F32),
            compiler_params=pltpu.CompilerParams(dimension_semantics=("arbitrary",), vmem_limit_bytes=VMEM_LIMIT),
            name=name,
        )(x, pn, d)

    return call


ADAM_LR, ADAM_B1, ADAM_B2, ADAM_EPS, ADAM_WD, ADAM_STEP = 0.001, 0.9, 0.999, 1e-08, 0.01, 10


def _adam_math(w, g, m, v):
    m = ADAM_B1 * m + (1.0 - ADAM_B1) * g
    v = ADAM_B2 * v + (1.0 - ADAM_B2) * (g * g)
    m_hat = m / (1.0 - ADAM_B1 ** ADAM_STEP)
    v_hat = v / (1.0 - ADAM_B2 ** ADAM_STEP)
    delta = -ADAM_LR * (m_hat / (jnp.sqrt(v_hat) + ADAM_EPS) + ADAM_WD * w)
    return delta, m, v


def _adamw(w, g, m, v, name, g2=None):
    shape = w.shape
    cols = shape[-1]
    rows = w.size // cols
    tr = rows if rows <= 512 else 256
    assert rows % tr == 0
    ops = [a.reshape(rows, cols) for a in ((w, g, m, v) if g2 is None else (w, g, g2, m, v))]

    def body(*refs):
        if g2 is None:
            w_ref, g_ref, m_ref, v_ref, go_ref, d_ref, mo_ref, vo_ref = refs
            g_ = g_ref[...]
        else:
            w_ref, g_ref, g2_ref, m_ref, v_ref, go_ref, d_ref, mo_ref, vo_ref = refs
            g_ = g_ref[...] + g2_ref[...]
        go_ref[...] = g_
        d_ref[...], mo_ref[...], vo_ref[...] = _adam_math(w_ref[...], g_, m_ref[...], v_ref[...])

    spec = pl.BlockSpec((tr, cols), lambda i: (i, 0))
    outs = pl.pallas_call(
        body,
        grid=(rows // tr,),
        in_specs=[spec] * len(ops),
        out_specs=[spec] * 4,
        out_shape=[jax.ShapeDtypeStruct((rows, cols), F32)] * 4,
        compiler_params=pltpu.CompilerParams(dimension_semantics=("arbitrary",), vmem_limit_bytes=VMEM_LIMIT),
        name=name,
    )(*ops)
    return tuple(o.reshape(shape) for o in outs)


MESH = pl.DeviceIdType.MESH
ANY = pl.BlockSpec(memory_space=pl.ANY)
CHIP_REL = ((1, 0), (0, 1), (1, 1))


def _flip(v, d):
    return 1 - v if d else v


def _ag_chips(arrs, name):
    n = len(arrs)

    def body(*refs):
        ins, outs = refs[:n], refs[n:2 * n]
        send_sems, recv_sems, loc_sems = refs[2 * n:]
        x, y, c = lax.axis_index("x"), lax.axis_index("y"), lax.axis_index("c")
        me = 2 * x + y

        def remote(a, k, slot):
            dx, dy = CHIP_REL[k]
            return pltpu.make_async_remote_copy(
                src_ref=ins[a], dst_ref=outs[a].at[slot], send_sem=send_sems.at[a * 3 + k],
                recv_sem=recv_sems.at[a * 3 + k], device_id=(_flip(x, dx), _flip(y, dy), c), device_id_type=MESH)

        local = [pltpu.make_async_copy(ins[a], outs[a].at[me], loc_sems.at[a]) for a in range(n)]
        for cp in local:
            cp.start()
        for a in range(n):
            for k in range(3):
                remote(a, k, me).start()
        for a in range(n):
            for k, (dx, dy) in enumerate(CHIP_REL):
                remote(a, k, 2 * _flip(x, dx) + _flip(y, dy)).wait_recv()
        for a in range(n):
            for k in range(3):
                remote(a, k, me).wait_send()
        for cp in local:
            cp.wait()

    return pl.pallas_call(
        body,
        in_specs=[ANY] * n,
        out_specs=[ANY] * n,
        out_shape=[jax.ShapeDtypeStruct((4,) + a.shape, a.dtype) for a in arrs],
        scratch_shapes=[pltpu.SemaphoreType.DMA((3 * n,)), pltpu.SemaphoreType.DMA((3 * n,)),
                        pltpu.SemaphoreType.DMA((n,))],
        compiler_params=pltpu.CompilerParams(has_side_effects=True),
        name=name,
    )(*arrs)


def _rs_chips(arrs, name):
    n = len(arrs)

    def body(*refs):
        ins, outs = refs[:n], refs[n:2 * n]
        send_sems, recv_sems = refs[2 * n:]
        x, y, c = lax.axis_index("x"), lax.axis_index("y"), lax.axis_index("c")

        def remote(a, k):
            dx, dy = CHIP_REL[k]
            px, py = _flip(x, dx), _flip(y, dy)
            return pltpu.make_async_remote_copy(
                src_ref=ins[a].at[2 * px + py], dst_ref=outs[a].at[k], send_sem=send_sems.at[a * 3 + k],
                recv_sem=recv_sems.at[a * 3 + k], device_id=(px, py, c), device_id_type=MESH)

        cps = [remote(a, k) for a in range(n) for k in range(3)]
        for cp in cps:
            cp.start()
        for cp in cps:
            cp.wait_recv()
        for cp in cps:
            cp.wait_send()

    return pl.pallas_call(
        body,
        in_specs=[ANY] * n,
        out_specs=[ANY] * n,
        out_shape=[jax.ShapeDtypeStruct((3,) + a.shape[1:], a.dtype) for a in arrs],
        scratch_shapes=[pltpu.SemaphoreType.DMA((3 * n,)), pltpu.SemaphoreType.DMA((3 * n,))],
        compiler_params=pltpu.CompilerParams(has_side_effects=True),
        name=name,
    )(*arrs)


def _swap_sibling(arrs, name):
    n = len(arrs)

    def body(*refs):
        ins, outs = refs[:n], refs[n:2 * n]
        send_sems, recv_sems = refs[2 * n:]
        x, y, c = lax.axis_index("x"), lax.axis_index("y"), lax.axis_index("c")
        cps = [pltpu.make_async_remote_copy(src_ref=ins[a], dst_ref=outs[a], send_sem=send_sems.at[a],
                                            recv_sem=recv_sems.at[a], device_id=(x, y, 1 - c), device_id_type=MESH)
               for a in range(n)]
        for cp in cps:
            cp.start()
        for cp in cps:
            cp.wait_recv()
        for cp in cps:
            cp.wait_send()

    return pl.pallas_call(
        body,
        in_specs=[ANY] * n,
        out_specs=[ANY] * n,
        out_shape=[jax.ShapeDtypeStruct(a.shape, a.dtype) for a in arrs],
        scratch_shapes=[pltpu.SemaphoreType.DMA((n,)), pltpu.SemaphoreType.DMA((n,))],
        compiler_params=pltpu.CompilerParams(has_side_effects=True),
        name=name,
    )(*arrs)


def _allreduce_small(vec, name):
    rows = vec.shape[0]

    def body(v_ref, out_ref, gat_ref, send_sems, recv_sems):
        x, y, c = lax.axis_index("x"), lax.axis_index("y"), lax.axis_index("c")
        me = 4 * x + 2 * y + c

        def remote(k, slot):
            dx, dy, dc = (k >> 2) & 1, (k >> 1) & 1, k & 1
            return pltpu.make_async_remote_copy(
                src_ref=v_ref, dst_ref=gat_ref.at[slot], send_sem=send_sems.at[k - 1], recv_sem=recv_sems.at[k - 1],
                device_id=(_flip(x, dx), _flip(y, dy), _flip(c, dc)), device_id_type=MESH)

        gat_ref[me] = v_ref[...]
        for k in range(1, 8):
            remote(k, me).start()
        for k in range(1, 8):
            dx, dy, dc = (k >> 2) & 1, (k >> 1) & 1, k & 1
            remote(k, 4 * _flip(x, dx) + 2 * _flip(y, dy) + _flip(c, dc)).wait_recv()
        for k in range(1, 8):
            remote(k, me).wait_send()
        acc = gat_ref[0]
        for j in range(1, 8):
            acc = acc + gat_ref[j]
        out_ref[...] = acc

    vm = pl.BlockSpec(memory_space=pltpu.VMEM)
    return pl.pallas_call(
        body,
        in_specs=[vm],
        out_specs=vm,
        out_shape=jax.ShapeDtypeStruct(vec.shape, F32),
        scratch_shapes=[pltpu.VMEM((8, rows, 128), F32), pltpu.SemaphoreType.DMA((7,)), pltpu.SemaphoreType.DMA((7,))],
        compiler_params=pltpu.CompilerParams(has_side_effects=True),
        name=name,
    )(vec)


def _sum4(own, recv, name):
    shape = own.shape
    cols = shape[-1]
    rows = own.size // cols
    tr = 256
    assert rows % tr == 0

    def body(o_ref, r_ref, s_ref):
        s_ref[...] = ((o_ref[...] + r_ref[0]) + r_ref[1]) + r_ref[2]

    out = pl.pallas_call(
        body,
        grid=(rows // tr,),
        in_specs=[pl.BlockSpec((tr, cols), lambda i: (i, 0)), pl.BlockSpec((3, tr, cols), lambda i: (0, i, 0))],
        out_specs=pl.BlockSpec((tr, cols), lambda i: (i, 0)),
        out_shape=jax.ShapeDtypeStruct((rows, cols), F32),
        compiler_params=pltpu.CompilerParams(dimension_semantics=("arbitrary",), vmem_limit_bytes=VMEM_LIMIT),
        name=name,
    )(own.reshape(rows, cols), recv.reshape(3, rows, cols))
    return out.reshape(shape)


def _pad8(v, width, lane0=0):
    v = v.reshape(1, -1) if v.ndim == 1 else v
    return jnp.zeros((8, width), F32).at[:v.shape[0], lane0:lane0 + v.shape[1]].set(v.astype(F32))


def _relayout_w_in(w):
    z = lambda n: jnp.zeros(w.shape[:-1] + (n,), w.dtype)
    return jnp.concatenate([w[..., 0:2048], w[..., 2056:4616], w[..., 4632:6680],
                            w[..., 2048:2056], z(120), w[..., 4616:4632], z(112)], axis=-1)


def _unlayout_dw_in(dg, ds, dr, dgs, dss):
    return jnp.concatenate([dg, dgs[:, 0:8], ds, dss[:, 0:16], dr], axis=1)


TB = 256
TL = 256


def kernel(x, pre_norm, post_norm, w_in, gdn_conv, gdn_A_log, gdn_dt_bias, gdn_norm, ssd_conv, ssd_conv_b, ssd_A_log, ssd_dt_bias, ssd_D, ssd_norm, ret_norm, w_out, loss_target, m_pre_norm, m_post_norm, m_w_in, m_gdn_conv, m_gdn_A_log, m_gdn_dt_bias, m_gdn_norm, m_ssd_conv, m_ssd_conv_b, m_ssd_A_log, m_ssd_dt_bias, m_ssd_D, m_ssd_norm, m_ret_norm, m_w_out, v_pre_norm, v_post_norm, v_w_in, v_gdn_conv, v_gdn_A_log, v_gdn_dt_bias, v_gdn_norm, v_ssd_conv, v_ssd_conv_b, v_ssd_A_log, v_ssd_dt_bias, v_ssd_D, v_ssd_norm, v_ret_norm, v_w_out):
    seq = x.shape[1]
    chip = 2 * lax.axis_index("x") + lax.axis_index("y")
    x0 = x[0]

    wi_g, wo_g, gcv_g, scv_g = _ag_chips([w_in.astype(BF16), w_out.astype(BF16), gdn_conv, ssd_conv], "ag_weights")
    wp = _relayout_w_in(jnp.transpose(wi_g, (1, 2, 0, 3)).reshape(DEPTH, D_MODEL, N_IN))
    wo = jnp.transpose(wo_g, (1, 0, 2, 3)).reshape(DEPTH, 2048, D_MODEL)
    gcv = jnp.transpose(gcv_g, (1, 2, 0, 3)).reshape(DEPTH, CONV_W, 1536)
    scv = jnp.transpose(scv_g, (1, 2, 0, 3)).reshape(DEPTH, CONV_W, 1536)
    rope_c, rope_s = _rope_tables(seq)

    saved = []
    xc = x0
    for l in range(DEPTH):
        p = dict(
            pn=_pad8(pre_norm[l], D_MODEL), qn=_pad8(post_norm[l], D_MODEL),
            g_cw=_pad8(gcv[l], 1536), g_prm=_pad8(jnp.stack([gdn_A_log[l], gdn_dt_bias[l]]), 128, 4),
            g_nw=_pad8(gdn_norm[l], 128),
            s_cw=_pad8(scv[l], 1536), s_cb=_pad8(ssd_conv_b[l], 1536),
            s_prm=_pad8(jnp.stack([ssd_A_log[l], ssd_dt_bias[l], ssd_D[l]]), 128), s_nw=_pad8(ssd_norm[l], SSD_W),
            r_nw=_pad8(ret_norm[l], 128))
        pg, ps, pr, gs, ss = _make_inproj(seq, TL)(xc, p["pn"], wp[l])
        oa, stg, tig, uwg = _make_gdn_fwd(seq, TB)(pg, gs, p["g_cw"], p["g_prm"], p["g_nw"])
        ob, sts = _make_ssd_fwd(seq, TB)(ps, ss, p["s_cw"], p["s_cb"], p["s_prm"], p["s_nw"])
        oc, str_ = _make_ret_fwd(seq, TB)(pr, rope_c, rope_s, p["r_nw"])
        out, xn = _make_outproj(seq, TL)(oa, ob, oc, wo[l], xc, p["qn"])
        saved.append(dict(p=p, x=xc, pg=pg, ps=ps, pr=pr, gs=gs, ss=ss, stg=stg, tig=tig, uwg=uwg, sts=sts, str=str_,
                          oa=oa, ob=ob, oc=oc, out=out))
        xc = xn

    dxn, lossp = _make_loss_head(seq, TL)(xc, loss_target[0])

    small = [None] * DEPTH
    dwi = [None] * DEPTH
    dwo = [None] * DEPTH
    for l in reversed(range(DEPTH)):
        s = saved[l]
        p = s["p"]
        doa, dob, doc, dqn, dwo[l] = _make_outproj_bwd(seq, TL)(dxn, s["out"], s["oa"], s["ob"], s["oc"], wo[l], p["qn"])
        dpg, dgs, dcw_g, dprm_g, dnw_g = _make_gdn_bwd(seq, TB)(s["pg"], s["gs"], p["g_cw"], p["g_prm"], p["g_nw"],
                                                                s["stg"], s["tig"], s["uwg"], doa)
        dps, dss, dcw_s, dcb_s, dprm_s, dnw_s = _make_ssd_bwd(seq, TB)(s["ps"], s["ss"], p["s_cw"], p["s_cb"],
                                                                       p["s_prm"], p["s_nw"], s["sts"], dob)
        dpr, dnw_r = _make_ret_bwd(seq, TB)(s["pr"], rope_c, rope_s, p["r_nw"], s["str"], doc)
        dx, dpn = _make_inproj_bwd_dx(seq, TL)(dpg, dps, dpr, dgs, dss, wp[l], s["x"], p["pn"], dxn)
        dws = [_make_inproj_bwd_dw(seq, TL, d.shape[1], f"inproj_bwd_dw{i}")(s["x"], p["pn"], d)
               for i, d in enumerate((dpg, dps, dpr, dgs, dss))]
        dwi[l] = _unlayout_dw_in(dws[0], dws[1], dws[2], dws[3], dws[4])
        small[l] = [dpn[0], dqn[0], dcw_g[0:4].reshape(-1), dprm_g[0, 4:8], dprm_g[1, 4:8], dnw_g[0],
                    dcw_s[0:4].reshape(-1), dcb_s[0], dprm_s[0, 0:16], dprm_s[1, 0:16], dprm_s[2, 0:16],
                    dnw_s[0], dnw_r[0]]
        dxn = dx
    grad_x = dxn[None]

    sizes = [a.shape[0] for a in small[0]]
    flat = jnp.concatenate(small[0] + small[1] + [lossp[0, 0:1]])
    n_flat = flat.shape[0]
    rows = -(-n_flat // 1024) * 8
    red = _allreduce_small(jnp.pad(flat, (0, rows * 128 - n_flat)).reshape(rows, 128), "allreduce_small").reshape(-1)
    per = sum(sizes)
    loss = red[2 * per]

    def pick(i):
        off = sum(sizes[:i])
        return jnp.stack([red[l * per + off:l * per + off + sizes[i]] for l in range(DEPTH)])

    g_small = dict(
        pre_norm=pick(0), post_norm=pick(1),
        gdn_conv=lax.dynamic_slice_in_dim(pick(2).reshape(DEPTH, CONV_W, 1536), chip * 384, 384, axis=2),
        gdn_A_log=pick(3), gdn_dt_bias=pick(4), gdn_norm=pick(5),
        ssd_conv=lax.dynamic_slice_in_dim(pick(6).reshape(DEPTH, CONV_W, 1536), chip * 384, 384, axis=2),
        ssd_conv_b=pick(7), ssd_A_log=pick(8), ssd_dt_bias=pick(9), ssd_D=pick(10), ssd_norm=pick(11),
        ret_norm=pick(12))

    gin = jnp.transpose(jnp.stack(dwi).reshape(DEPTH, D_MODEL, 4, N_IN // 4), (2, 0, 1, 3))
    gout = jnp.transpose(jnp.stack(dwo).reshape(DEPTH, 4, 512, D_MODEL), (1, 0, 2, 3))
    r_in, r_out = _rs_chips([gin, gout], "rs_grads")
    s_in = _sum4(lax.dynamic_index_in_dim(gin, chip, 0, keepdims=False), r_in, "sum4_w_in")
    s_out = _sum4(lax.dynamic_index_in_dim(gout, chip, 0, keepdims=False), r_out, "sum4_w_out")
    t_in, t_out = _swap_sibling([s_in, s_out], "swap_grads")

    weights = dict(pre_norm=pre_norm, post_norm=post_norm, w_in=w_in, gdn_conv=gdn_conv, gdn_A_log=gdn_A_log,
                   gdn_dt_bias=gdn_dt_bias, gdn_norm=gdn_norm, ssd_conv=ssd_conv, ssd_conv_b=ssd_conv_b,
                   ssd_A_log=ssd_A_log, ssd_dt_bias=ssd_dt_bias, ssd_D=ssd_D, ssd_norm=ssd_norm, ret_norm=ret_norm,
                   w_out=w_out)
    ms = dict(pre_norm=m_pre_norm, post_norm=m_post_norm, w_in=m_w_in, gdn_conv=m_gdn_conv, gdn_A_log=m_gdn_A_log,
              gdn_dt_bias=m_gdn_dt_bias, gdn_norm=m_gdn_norm, ssd_conv=m_ssd_conv, ssd_conv_b=m_ssd_conv_b,
              ssd_A_log=m_ssd_A_log, ssd_dt_bias=m_ssd_dt_bias, ssd_D=m_ssd_D, ssd_norm=m_ssd_norm,
              ret_norm=m_ret_norm, w_out=m_w_out)
    vs = dict(pre_norm=v_pre_norm, post_norm=v_post_norm, w_in=v_w_in, gdn_conv=v_gdn_conv, gdn_A_log=v_gdn_A_log,
              gdn_dt_bias=v_gdn_dt_bias, gdn_norm=v_gdn_norm, ssd_conv=v_ssd_conv, ssd_conv_b=v_ssd_conv_b,
              ssd_A_log=v_ssd_A_log, ssd_dt_bias=v_ssd_dt_bias, ssd_D=v_ssd_D, ssd_norm=v_ssd_norm,
              ret_norm=v_ret_norm, w_out=v_w_out)
    names = list(weights)
    res = {}
    for nme in names:
        if nme == "w_in":
            res[nme] = _adamw(w_in, s_in, m_w_in, v_w_in, "adamw_w_in", g2=t_in)
        elif nme == "w_out":
            res[nme] = _adamw(w_out, s_out, m_w_out, v_w_out, "adamw_w_out", g2=t_out)
        else:
            res[nme] = _adamw(weights[nme], g_small[nme], ms[nme], vs[nme], "adamw_" + nme)
    return (loss, grad_x, *[res[n][0] for n in names], *[res[n][1] for n in names],
            *[res[n][2] for n in names], *[res[n][3] for n in names])
```

```python
import functools
import math

import jax
import jax.numpy as jnp
from jax import lax
from jax.experimental import pallas as pl
from jax.experimental.pallas import tpu as pltpu

F32 = jnp.float32
BF16 = jnp.bfloat16
HI = lax.Precision.HIGHEST

D_MODEL = 1024
DEPTH = 2
CH = 64
CONV_W = 4
EPS = 1e-6
GDN_H, GDN_D = 4, 128
SSD_H, SSD_P, SSD_N, SSD_G = 16, 64, 128, 2
SSD_W = SSD_H * SSD_P
RET_H, RET_D = 4, 128
ROPE_BASE = 10000.0
N_IN = 6680
NEG = -1e30

VMEM_LIMIT = 56 * 1024 * 1024


def _dot(a, b):
    return jnp.dot(a.astype(BF16), b.astype(BF16), preferred_element_type=F32)


def _dot_nt(a, b):
    return lax.dot_general(a.astype(BF16), b.astype(BF16), (((1,), (1,)), ((), ())), preferred_element_type=F32)


def _dot_tn(a, b):
    return lax.dot_general(a.astype(BF16), b.astype(BF16), (((0,), (0,)), ((), ())), preferred_element_type=F32)


def _split(a):
    hi = a.astype(BF16)
    return hi, (a - hi.astype(F32)).astype(BF16)


def _dot01l(m, v):
    vh, vl = _split(v)
    mb = m.astype(BF16)
    return jnp.dot(mb, vh, preferred_element_type=F32) + jnp.dot(mb, vl, preferred_element_type=F32)


def _dot01r(v, m):
    vh, vl = _split(v)
    mb = m.astype(BF16)
    return jnp.dot(vh, mb, preferred_element_type=F32) + jnp.dot(vl, mb, preferred_element_type=F32)


def _sigmoid(x):
    return jax.nn.sigmoid(x)


def _silu(x):
    return x * _sigmoid(x)


def _dsilu(x):
    s = _sigmoid(x)
    return s * (1.0 + x * (1.0 - s))


def _softplus(x):
    return jnp.maximum(x, 0.0) + jnp.log1p(jnp.exp(-jnp.abs(x)))


def _iota2(shape, dim):
    return lax.broadcasted_iota(jnp.int32, shape, dim)


def _chunk_tri(tb, upper=False):
    r = _iota2((tb, tb), 0)
    c = _iota2((tb, tb), 1)
    same = jnp.right_shift(r, 6) == jnp.right_shift(c, 6)
    return (same & ((c >= r) if upper else (c <= r))).astype(F32)


def _masks():
    r = _iota2((CH, CH), 0)
    c = _iota2((CH, CH), 1)
    return r >= c, r > c, (r == c).astype(F32)


def _put_lane(col, lane_idx, width=128):
    lane = _iota2((col.shape[0], width), 1)
    return jnp.where(lane == lane_idx, col, 0.0)


def _conv_taps(raw, halo8, tb):
    ext = jnp.concatenate([halo8, raw], axis=0)
    return [raw] + [pltpu.roll(ext, s, axis=0)[8:] for s in (1, 2, 3)]


def _conv_back(dpre, nxt8, tb):
    ext = jnp.concatenate([dpre, nxt8], axis=0)
    return [dpre] + [pltpu.roll(ext, tb + 8 - s, axis=0)[:tb] for s in (1, 2, 3)]


def _rms_fwd(o, w, n):
    r = lax.rsqrt(jnp.sum(o * o, axis=-1, keepdims=True) * (1.0 / n) + EPS)
    on = o * r
    return on, r, on * w


def _rms_bwd(dy, on, r, w, n):
    don = dy * w
    return r * (don - on * (jnp.sum(don * on, axis=-1, keepdims=True) * (1.0 / n))), dy * on


def _put_cols(v, g, gw):
    z = jnp.zeros_like(v)
    return jnp.concatenate([v, z] if g == 0 else [z, v], axis=1)


def _gdn_common(pg_ref, halo8, sm, cw, prm, tb):
    raw = pg_ref[:, 0:1536]
    taps = _conv_taps(raw, halo8, tb)
    pre = taps[0] * cw[3:4, :] + taps[1] * cw[2:3, :] + taps[2] * cw[1:2, :] + taps[3] * cw[0:1, :]
    act = _silu(pre)
    beta = _sigmoid(sm)
    sp_in = sm + prm[1:2, :]
    g = -jnp.exp(prm[0:1, :]) * _softplus(sp_in)
    gc = _dot01l(_chunk_tri(tb), g)
    return raw, taps, pre, act, beta, sp_in, g, gc


_NN = (((2,), (1,)), ((0,), (0,)))
_NT = (((2,), (2,)), ((0,), (0,)))
_TN = (((1,), (1,)), ((0,), (0,)))


def _bdot(a, b, dn):
    return lax.dot_general(a.astype(BF16), b.astype(BF16), dn, preferred_element_type=F32)


def _dot3_parts(ah, al, bh, bl, dn):
    f = lambda p, q: lax.dot_general(p, q, dn, preferred_element_type=F32)
    return f(ah, bh) + (f(ah, bl) + f(al, bh))


def _bdot3(a, b, dn):
    ah, al = _split(a)
    bh, bl = _split(b)
    return _dot3_parts(ah, al, bh, bl, dn)


def _binv_unit_lower(a, eye):
    x = eye - a
    ph, pl_ = _split(a)
    for _ in range(5):
        ph, pl_ = _split(_dot3_parts(ph, pl_, ph, pl_, _NN))
        xh, xl = _split(x)
        x = x + _dot3_parts(xh, xl, ph, pl_, _NN)
    return x


def _rsum(v):
    return jnp.sum(v, axis=-1, keepdims=True)


def _gdn_batch(act, beta, gc, gct, eg_all, ncb, masks):
    causal, strict, _ = masks

    def st(fn):
        return jnp.stack([fn(c, h, slice(c * CH, (c + 1) * CH)) for c in range(ncb) for h in range(GDN_H)])

    qr = st(lambda c, h, r: act[r, h * 128:(h + 1) * 128])
    kr = st(lambda c, h, r: act[r, 512 + h * 128:512 + (h + 1) * 128])
    vh = st(lambda c, h, r: act[r, 1024 + h * 128:1024 + (h + 1) * 128])
    bh = st(lambda c, h, r: beta[r, h:h + 1])
    gcol = st(lambda c, h, r: gc[r, 4 + h:5 + h])
    grow = st(lambda c, h, r: gct[4 + h:5 + h, r])
    eg = st(lambda c, h, r: eg_all[r, 4 + h:5 + h])
    glast = st(lambda c, h, r: gc[(c + 1) * CH - 1:(c + 1) * CH, 4 + h:5 + h])
    rq = lax.rsqrt(_rsum(qr * qr) + EPS)
    rk = lax.rsqrt(_rsum(kr * kr) + EPS)
    qn = qr * rq
    kh = kr * rk
    qh = qn * (GDN_D ** -0.5)
    decay = jnp.exp(jnp.where(causal, gcol - grow, NEG))
    kb = kh * bh
    kd_scale = jnp.exp(glast - gcol)
    return dict(qn=qn, rq=rq, kh=kh, rk=rk, qh=qh, vh=vh, bh=bh, eg=eg, decay=decay, kb=kb, vb=vh * bh, kg=kb * eg,
                qg=qh * eg, kd_scale=kd_scale, kdec=kh * kd_scale, egl=jnp.exp(glast),
                a=jnp.where(strict, _bdot(kb, kh, _NT) * decay, 0.0), attn=_bdot(qh, kh, _NT) * decay)


def _make_gdn_fwd(seq, tb):
    ncb = tb // CH
    nb = seq // tb
    n = ncb * GDN_H

    def body(pg_ref, sm_ref, cw_ref, prm_ref, nw_ref, oa_ref, st_ref, ti_ref, uw_ref, s_scr, halo_scr):
        @pl.when(pl.program_id(0) == 0)
        def _():
            s_scr[...] = jnp.zeros_like(s_scr)
            halo_scr[...] = jnp.zeros_like(halo_scr)

        masks = _masks()
        sm = sm_ref[...]
        raw, _, _, act, beta, _, _, gc = _gdn_common(pg_ref, halo_scr[...], sm, cw_ref[...], prm_ref[...], tb)
        halo_scr[...] = raw[tb - 8:tb, :]
        d = _gdn_batch(act, beta, gc, gc.T, jnp.exp(gc), ncb, masks)
        t = _binv_unit_lower(d["a"], masks[2])
        sol = _bdot3(t, jnp.concatenate([d["vb"], d["kg"]], axis=2), _NN)
        ti_ref[...] = t.reshape(ncb, GDN_H, CH, CH)
        uw_ref[...] = sol.reshape(ncb, GDN_H, CH, 256)
        u, w = sol[:, :, :128], sol[:, :, 128:]
        vns = []
        for c in range(ncb):
            bs = slice(c * GDN_H, (c + 1) * GDN_H)
            s = s_scr[...]
            st_ref[c] = s
            vn = u[bs] - _bdot(w[bs], s, _NN)
            s_scr[...] = s * d["egl"][bs] + _bdot(d["kdec"][bs], vn, _TN)
            vns.append(vn)
        v_new = jnp.concatenate(vns, axis=0)
        s_prev = st_ref[...].reshape(n, 128, 128)
        o = _bdot(d["qg"], s_prev, _NN) + _bdot(d["attn"], v_new, _NN)
        _, _, y = _rms_fwd(o, nw_ref[0:1, :], GDN_D)
        for c in range(ncb):
            rows = slice(c * CH, (c + 1) * CH)
            for h in range(GDN_H):
                z = pg_ref[rows, 1536 + h * 128:1536 + (h + 1) * 128]
                oa_ref[rows, h * 128:(h + 1) * 128] = y[c * GDN_H + h] * _silu(z)

    def call(pg, sm, cw, prm, nw):
        blk4 = lambda i: (i, 0, 0, 0)
        return pl.pallas_call(
            body,
            grid=(nb,),
            in_specs=[
                pl.BlockSpec((tb, 2048), lambda i: (i, 0)),
                pl.BlockSpec((tb, 128), lambda i: (i, 0)),
                pl.BlockSpec((8, 1536), lambda i: (0, 0)),
                pl.BlockSpec((8, 128), lambda i: (0, 0)),
                pl.BlockSpec((8, 128), lambda i: (0, 0)),
            ],
            out_specs=[
                pl.BlockSpec((tb, 512), lambda i: (i, 0)),
                pl.BlockSpec((ncb, GDN_H, 128, 128), blk4),
                pl.BlockSpec((ncb, GDN_H, CH, CH), blk4),
                pl.BlockSpec((ncb, GDN_H, CH, 256), blk4),
            ],
            out_shape=[
                jax.ShapeDtypeStruct((seq, 512), F32),
                jax.ShapeDtypeStruct((seq // CH, GDN_H, 128, 128), F32),
                jax.ShapeDtypeStruct((seq // CH, GDN_H, CH, CH), F32),
                jax.ShapeDtypeStruct((seq // CH, GDN_H, CH, 256), F32),
            ],
            scratch_shapes=[pltpu.VMEM((GDN_H, 128, 128), F32), pltpu.VMEM((8, 1536), F32)],
            compiler_params=pltpu.CompilerParams(dimension_semantics=("arbitrary",), vmem_limit_bytes=VMEM_LIMIT),
            name="gdn_fwd",
        )(pg, sm, cw, prm, nw)

    return call


def _make_gdn_bwd(seq, tb):
    ncb = tb // CH
    nb = seq // tb
    hb = tb // 8
    n = ncb * GDN_H

    def body(pg_ref, prev_ref, sm_ref, cw_ref, prm_ref, nw_ref, st_ref, ti_ref, uw_ref, doa_ref,
             dpg_ref, dsm_ref, dcw_ref, dprm_ref, dnw_ref, ds_scr, nxt_scr):
        i = pl.program_id(0)

        @pl.when(i == 0)
        def _():
            ds_scr[...] = jnp.zeros_like(ds_scr)
            nxt_scr[...] = jnp.zeros_like(nxt_scr)
            dcw_ref[...] = jnp.zeros_like(dcw_ref)
            dprm_ref[...] = jnp.zeros_like(dprm_ref)
            dnw_ref[...] = jnp.zeros_like(dnw_ref)

        masks = _masks()
        strict = masks[1]
        sm = sm_ref[...]
        cw = cw_ref[...]
        prm = prm_ref[...]
        halo8 = jnp.where(i == nb - 1, 0.0, prev_ref[...])
        raw, taps, pre, act, beta, sp_in, g, gc = _gdn_common(pg_ref, halo8, sm, cw, prm, tb)
        nw = nw_ref[0:1, :]
        row_id = _iota2((CH, 1), 0)
        d = _gdn_batch(act, beta, gc, gc.T, jnp.exp(gc), ncb, masks)
        t = ti_ref[...].reshape(n, CH, CH)
        sol = uw_ref[...].reshape(n, CH, 256)
        u, w = sol[:, :, :128], sol[:, :, 128:]
        s_prev = st_ref[...].reshape(n, 128, 128)
        v_new = u - _bdot(w, s_prev, _NN)
        o = _bdot(d["qg"], s_prev, _NN) + _bdot(d["attn"], v_new, _NN)

        pairs = [(c, h) for c in range(ncb) for h in range(GDN_H)]
        z = jnp.stack([pg_ref[c * CH:(c + 1) * CH, 1536 + h * 128:1536 + (h + 1) * 128] for c, h in pairs])
        doa = jnp.stack([doa_ref[c * CH:(c + 1) * CH, h * 128:(h + 1) * 128] for c, h in pairs])
        on, r, y = _rms_fwd(o, nw, GDN_D)
        dz = doa * y * _dsilu(z)
        do, dnw_rows = _rms_bwd(doa * _silu(z), on, r, nw, GDN_D)
        dnw_acc = jnp.sum(jnp.sum(dnw_rows, axis=0), axis=0, keepdims=True)

        dvn_in = _bdot(d["attn"], do, _TN)
        qgtdo = _bdot(d["qg"], do, _TN)
        dvn_l, dkdec_l, dgl_l = [None] * ncb, [None] * ncb, [None] * ncb
        for c in reversed(range(ncb)):
            bs = slice(c * GDN_H, (c + 1) * GDN_H)
            dsn = ds_scr[...]
            dvn_c = dvn_in[bs] + _bdot(d["kdec"][bs], dsn, _NN)
            ds_scr[...] = d["egl"][bs] * dsn + qgtdo[bs] - _bdot(w[bs], dvn_c, _TN)
            dvn_l[c] = dvn_c
            dkdec_l[c] = _bdot(v_new[bs], dsn, _NT)
            dgl_l[c] = d["egl"][bs] * jnp.sum(_rsum(s_prev[bs] * dsn), axis=1, keepdims=True)
        dvn = jnp.concatenate(dvn_l, axis=0)
        dkdec = jnp.concatenate(dkdec_l, axis=0)
        dglast = jnp.concatenate(dgl_l, axis=0)

        dqg = _bdot(do, s_prev, _NT)
        dattn = _bdot(do, v_new, _NT)
        dw = -_bdot(dvn, s_prev, _NT)
        drhs = _bdot3(t, jnp.concatenate([dvn, dw], axis=2), _TN)
        dvb, dkg = drhs[:, :, :128], drhs[:, :, 128:]
        da = jnp.where(strict, -(_bdot(dvb, u, _NT) + _bdot(dkg, w, _NT)), 0.0)
        dp = da * d["decay"]
        dq_m = dattn * d["decay"]
        m = da * d["a"] + dattn * d["attn"]
        upper_tri = jnp.broadcast_to((_iota2((CH, CH), 1) >= _iota2((CH, CH), 0)).astype(BF16), (n, CH, CH))
        dg_in = _rsum(jnp.where(strict, _bdot(upper_tri, m, _NN), 0.0))
        dkb = _bdot(dp, d["kh"], _NN) + dkg * d["eg"]
        kdk_row = _rsum(dkdec * d["kdec"])
        dk = _bdot(dp, d["kb"], _TN) + _bdot(dq_m, d["qh"], _TN) + dkdec * d["kd_scale"] + dkb * d["bh"]
        dq = _bdot(dq_m, d["kh"], _NN) + dqg * d["eg"]
        dglast = dglast + jnp.sum(kdk_row, axis=1, keepdims=True)
        dgcol = (_rsum(dqg * d["qg"]) + _rsum(dkg * d["kg"]) - kdk_row + jnp.where(row_id == CH - 1, dglast, 0.0))
        dbeta = _rsum(dkb * d["kh"]) + _rsum(dvb * d["vh"])
        dn = dq * (GDN_D ** -0.5)
        dact_q = d["rq"] * (dn - d["qn"] * _rsum(dn * d["qn"]))
        dact_k = d["rk"] * (dk - d["kh"] * _rsum(dk * d["kh"]))
        dact_v = dvb * d["bh"]

        def lanes(v, lane0):
            return jnp.concatenate(
                [sum(_put_lane(v[c * GDN_H + h], lane0 + h) for h in range(GDN_H)) for c in range(ncb)], axis=0)

        def tokens(v):
            return jnp.concatenate(
                [jnp.concatenate([v[c * GDN_H + h] for h in range(GDN_H)], axis=1) for c in range(ncb)], axis=0)

        dbeta_all = lanes(dbeta, 0)
        dg = _dot01l(_chunk_tri(tb, upper=True), lanes(dgcol, 4)) + lanes(dg_in, 4)
        neg_ea = -jnp.exp(prm[0:1, :])
        da_raw = dg * neg_ea * _sigmoid(sp_in)
        db_raw = dbeta_all * beta * (1.0 - beta)
        dsm_ref[...] = (da_raw + db_raw).astype(dsm_ref.dtype)
        lane8 = _iota2((8, 128), 1)
        sub8 = _iota2((8, 128), 0)
        dalog = jnp.sum(dg * g, axis=0, keepdims=True)
        ddtb = jnp.sum(da_raw, axis=0, keepdims=True)
        dprm_ref[...] += jnp.where(sub8 == 0, dalog, 0.0) + jnp.where(sub8 == 1, ddtb, 0.0)
        dnw_ref[...] += jnp.where(sub8 == 0, dnw_acc, 0.0)

        dact = jnp.concatenate([tokens(dact_q), tokens(dact_k), tokens(dact_v)], axis=1)
        dpre = dact * _dsilu(pre)
        back = _conv_back(dpre, nxt_scr[...], tb)
        nxt_scr[...] = dpre[0:8, :]
        draw = back[0] * cw[3:4, :] + back[1] * cw[2:3, :] + back[2] * cw[1:2, :] + back[3] * cw[0:1, :]
        dpg_ref[:, 0:1536] = draw.astype(dpg_ref.dtype)
        dpg_ref[:, 1536:2048] = tokens(dz).astype(dpg_ref.dtype)
        sub_c = _iota2((8, 1536), 0)
        dcw_new = jnp.zeros((8, 1536), F32)
        for s_ in range(CONV_W):
            dcw_new = dcw_new + jnp.where(sub_c == 3 - s_, jnp.sum(dpre * taps[s_], axis=0, keepdims=True), 0.0)
        dcw_ref[...] += dcw_new

    def call(pg, sm, cw, prm, nw, st, ti, uw, doa):
        rev = lambda i: (nb - 1 - i, 0)
        const = lambda i: (0, 0)
        return pl.pallas_call(
            body,
            grid=(nb,),
            in_specs=[
                pl.BlockSpec((tb, 2048), rev),
                pl.BlockSpec((8, 1536), lambda i: (jnp.maximum((nb - 1 - i) * hb - 1, 0), 0)),
                pl.BlockSpec((tb, 128), rev),
                pl.BlockSpec((8, 1536), const),
                pl.BlockSpec((8, 128), const),
                pl.BlockSpec((8, 128), const),
                pl.BlockSpec((ncb, GDN_H, 128, 128), lambda i: (nb - 1 - i, 0, 0, 0)),
                pl.BlockSpec((ncb, GDN_H, CH, CH), lambda i: (nb - 1 - i, 0, 0, 0)),
                pl.BlockSpec((ncb, GDN_H, CH, 256), lambda i: (nb - 1 - i, 0, 0, 0)),
                pl.BlockSpec((tb, 512), rev),
            ],
            out_specs=[
                pl.BlockSpec((tb, 2048), rev),
                pl.BlockSpec((tb, 128), rev),
                pl.BlockSpec((8, 1536), const),
                pl.BlockSpec((8, 128), const),
                pl.BlockSpec((8, 128), const),
            ],
            out_shape=[
                jax.ShapeDtypeStruct((seq, 2048), BF16),
                jax.ShapeDtypeStruct((seq, 128), BF16),
                jax.ShapeDtypeStruct((8, 1536), F32),
                jax.ShapeDtypeStruct((8, 128), F32),
                jax.ShapeDtypeStruct((8, 128), F32),
            ],
            scratch_shapes=[pltpu.VMEM((GDN_H, 128, 128), F32), pltpu.VMEM((8, 1536), F32)],
            compiler_params=pltpu.CompilerParams(dimension_semantics=("arbitrary",), vmem_limit_bytes=VMEM_LIMIT),
            name="gdn_bwd",
        )(pg, pg, sm, cw, prm, nw, st, ti, uw, doa)

    return call


def _expand_mat():
    r = _iota2((128, SSD_W), 0)
    c = _iota2((128, SSD_W), 1)
    return (jnp.right_shift(c, 6) == r).astype(F32)


def _reduce_heads(v, e):
    vh, vl = _split(v)
    eb = e.astype(BF16)
    nt = (((1,), (1,)), ((), ()))
    return (lax.dot_general(vh, eb, nt, preferred_element_type=F32)
            + lax.dot_general(vl, eb, nt, preferred_element_type=F32))


def _row8(v):
    return jnp.broadcast_to(v, (8, v.shape[1]))


def _ssd_common(ps_ref, halo8, ss, cw, cb, prm, tb):
    raw = ps_ref[:, 0:1536]
    taps = _conv_taps(raw, halo8, tb)
    pre = taps[0] * cw[3:4, :] + taps[1] * cw[2:3, :] + taps[2] * cw[1:2, :] + taps[3] * cw[0:1, :] + cb[0:1, :]
    act = _silu(pre)
    dt_in = ss + prm[1:2, :]
    dt = _softplus(dt_in)
    a = dt * (-jnp.exp(prm[0:1, :]))
    acum = _dot01l(_chunk_tri(tb), a)
    e = _expand_mat()
    dt_e = _dot01r(dt, e)
    xdt = act[:, 0:SSD_W] * dt_e
    ea_e = _dot01r(jnp.exp(acum), e)
    d_e = _dot01r(_row8(prm[2:3, :]), e)[0:1, :]
    return raw, taps, pre, act, dt_in, dt, a, acum, e, dt_e, xdt, ea_e, d_e


def _ssd_chunk(act, acum, act_t, e, c):
    r0 = c * CH
    rows = slice(r0, r0 + CH)
    alast = acum[r0 + CH - 1:r0 + CH, :]
    wdec = jnp.exp(alast - acum[rows, :])
    wd_e = _dot01r(wdec, e)
    eal_e = _dot01r(_row8(jnp.exp(alast)), e)[0:1, :]
    return rows, wd_e, eal_e


def _ssd_lmat(acum, act_t, c, h, causal):
    r0 = c * CH
    acol = acum[r0:r0 + CH, h:h + 1]
    arow = act_t[h:h + 1, r0:r0 + CH]
    return jnp.exp(jnp.where(causal, acol - arow, NEG))


def _make_ssd_fwd(seq, tb):
    ncb = tb // CH
    nb = seq // tb
    hg = SSD_H // SSD_G
    gw = SSD_W // SSD_G

    def body(ps_ref, ss_ref, cw_ref, cb_ref, prm_ref, nw_ref, ob_ref, st_ref, hs_scr, halo_scr):
        @pl.when(pl.program_id(0) == 0)
        def _():
            hs_scr[...] = jnp.zeros_like(hs_scr)
            halo_scr[...] = jnp.zeros_like(halo_scr)

        causal, _, _ = _masks()
        (raw, _, _, act, _, _, _, acum, e, _, xdt, ea_e, d_e) = _ssd_common(
            ps_ref, halo_scr[...], ss_ref[...], cw_ref[...], cb_ref[...], prm_ref[...], tb)
        halo_scr[...] = raw[tb - 8:tb, :]
        act_t = acum.T
        nw = nw_ref[0:1, :]
        for c in range(ncb):
            rows, wd_e, eal_e = _ssd_chunk(act, acum, act_t, e, c)
            st_ref[c] = hs_scr[...]
            ys = []
            for g in range(SSD_G):
                gc_ = slice(g * gw, (g + 1) * gw)
                bg = act[rows, SSD_W + g * 128:SSD_W + (g + 1) * 128]
                cg = act[rows, SSD_W + 256 + g * 128:SSD_W + 256 + (g + 1) * 128]
                cbm = _dot_nt(cg, bg)
                hs = hs_scr[:, gc_]
                yin = _dot(cg, hs)
                yh = []
                for hh in range(hg):
                    h = g * hg + hh
                    lm = _ssd_lmat(acum, act_t, c, h, causal)
                    yh.append(_dot(cbm * lm, xdt[rows, h * SSD_P:(h + 1) * SSD_P]))
                ys.append(jnp.concatenate(yh, axis=1) + yin * ea_e[rows, gc_])
                hs_scr[:, gc_] = hs * eal_e[:, gc_] + _dot_tn(bg, xdt[rows, gc_] * wd_e[:, gc_])
            y = jnp.concatenate(ys, axis=1) + act[rows, 0:SSD_W] * d_e
            yz = y * _silu(ps_ref[rows, 1536:2560])
            outs = [_rms_fwd(yz[:, g * gw:(g + 1) * gw], nw[:, g * gw:(g + 1) * gw], gw)[2] for g in range(SSD_G)]
            ob_ref[rows, :] = jnp.concatenate(outs, axis=1)

    def call(ps, ss, cw, cb, prm, nw):
        const = lambda i: (0, 0)
        return pl.pallas_call(
            body,
            grid=(nb,),
            in_specs=[
                pl.BlockSpec((tb, 2560), lambda i: (i, 0)),
                pl.BlockSpec((tb, 128), lambda i: (i, 0)),
                pl.BlockSpec((8, 1536), const),
                pl.BlockSpec((8, 1536), const),
                pl.BlockSpec((8, 128), const),
                pl.BlockSpec((8, SSD_W), const),
            ],
            out_specs=[
                pl.BlockSpec((tb, SSD_W), lambda i: (i, 0)),
                pl.BlockSpec((ncb, SSD_N, SSD_W), lambda i: (i, 0, 0)),
            ],
            out_shape=[
                jax.ShapeDtypeStruct((seq, SSD_W), F32),
                jax.ShapeDtypeStruct((seq // CH, SSD_N, SSD_W), F32),
            ],
            scratch_shapes=[pltpu.VMEM((SSD_N, SSD_W), F32), pltpu.VMEM((8, 1536), F32)],
            compiler_params=pltpu.CompilerParams(dimension_semantics=("arbitrary",), vmem_limit_bytes=VMEM_LIMIT),
            name="ssd_fwd",
        )(ps, ss, cw, cb, prm, nw)

    return call


def _make_ssd_bwd(seq, tb):
    ncb = tb // CH
    nb = seq // tb
    hb = tb // 8
    hg = SSD_H // SSD_G
    gw = SSD_W // SSD_G

    def body(ps_ref, prev_ref, ss_ref, cw_ref, cb_ref, prm_ref, nw_ref, st_ref, dob_ref,
             dps_ref, dss_ref, dcw_ref, dcb_ref, dprm_ref, dnw_ref, dhs_scr, nxt_scr):
        i = pl.program_id(0)

        @pl.when(i == 0)
        def _():
            dhs_scr[...] = jnp.zeros_like(dhs_scr)
            nxt_scr[...] = jnp.zeros_like(nxt_scr)
            dcw_ref[...] = jnp.zeros_like(dcw_ref)
            dcb_ref[...] = jnp.zeros_like(dcb_ref)
            dprm_ref[...] = jnp.zeros_like(dprm_ref)
            dnw_ref[...] = jnp.zeros_like(dnw_ref)

        causal, _, _ = _masks()
        cw = cw_ref[...]
        prm = prm_ref[...]
        halo8 = jnp.where(i == nb - 1, 0.0, prev_ref[...])
        (raw, taps, pre, act, dt_in, dt, a, acum, e, dt_e, xdt, ea_e, d_e) = _ssd_common(
            ps_ref, halo8, ss_ref[...], cw, cb_ref[...], prm, tb)
        act_t = acum.T
        nw = nw_ref[0:1, :]
        row_id = _iota2((CH, 1), 0)

        dx_l, db_l, dc_l, dz_l, dacum_l, ddt_l, da_in_l = ([None] * ncb for _ in range(7))
        upper_tri = (_iota2((CH, CH), 1) >= _iota2((CH, CH), 0)).astype(F32)
        below = jnp.bitwise_and(_iota2((CH, gw), 1), CH - 1) < _iota2((CH, gw), 0)
        dnw_acc = jnp.zeros((1, SSD_W), F32)
        dd_acc = jnp.zeros((1, SSD_W), F32)

        for c in reversed(range(ncb)):
            rows, wd_e, eal_e = _ssd_chunk(act, acum, act_t, e, c)
            xc = act[rows, 0:SSD_W]
            z = ps_ref[rows, 1536:2560]
            dob = dob_ref[rows, :]
            sz = _silu(z)
            dy_g, dz_g, zacc_g, dxdt_g, dal_g, db_g, dc_g, da_in_g = [], [], [], [], [], [], [], []
            for g in range(SSD_G):
                gc_ = slice(g * gw, (g + 1) * gw)
                bg = act[rows, SSD_W + g * 128:SSD_W + (g + 1) * 128]
                cg = act[rows, SSD_W + 256 + g * 128:SSD_W + 256 + (g + 1) * 128]
                cbm = _dot_nt(cg, bg)
                hs = st_ref[c, :, gc_]
                yin = _dot(cg, hs)
                lms, yh = [], []
                for hh in range(hg):
                    h = g * hg + hh
                    lm = cbm * _ssd_lmat(acum, act_t, c, h, causal)
                    lms.append(lm)
                    yh.append(_dot(lm, xdt[rows, h * SSD_P:(h + 1) * SSD_P]))
                y_intra = jnp.concatenate(yh, axis=1)
                ea_g = ea_e[rows, gc_]
                y = y_intra + yin * ea_g + xc[:, gc_] * d_e[:, gc_]
                yz = y * sz[:, gc_]
                on, r, _ = _rms_fwd(yz, nw[:, gc_], gw)
                dyz, dnw_rows = _rms_bwd(dob[:, gc_], on, r, nw[:, gc_], gw)
                dnw_acc = dnw_acc + _put_cols(jnp.sum(dnw_rows, axis=0, keepdims=True), g, gw)
                dy = dyz * sz[:, gc_]
                dz_g.append(dyz * y * _dsilu(z[:, gc_]))
                dd_acc = dd_acc + _put_cols(jnp.sum(dy * xc[:, gc_], axis=0, keepdims=True), g, gw)
                dhs_n = dhs_scr[:, gc_]
                dyin = dy * ea_g
                dcg = _dot_nt(dyin, hs)
                xw = xdt[rows, gc_] * wd_e[:, gc_]
                dbg = _dot_nt(xw, dhs_n)
                dxw = _dot(bg, dhs_n)
                dhs_scr[:, gc_] = dhs_n * eal_e[:, gc_] + _dot_tn(cg, dyin)
                dal_g.append(jnp.sum(hs * dhs_n, axis=0, keepdims=True) * eal_e[:, gc_]
                             + jnp.sum(dxw * xw, axis=0, keepdims=True))
                dxi, ms, dcbm = [], [], jnp.zeros((CH, CH), F32)
                for hh in range(hg):
                    h = g * hg + hh
                    hc = slice(hh * SSD_P, (hh + 1) * SSD_P)
                    dyh = dy[:, hc]
                    dxi.append(_dot_tn(lms[hh], dyh))
                    dlm = _dot_nt(dyh, xdt[rows, h * SSD_P:(h + 1) * SSD_P])
                    ms.append(dlm * lms[hh])
                    dcbm = dcbm + dlm * _ssd_lmat(acum, act_t, c, h, causal)
                dx_intra = jnp.concatenate(dxi, axis=1)
                ncat = _dot(upper_tri, jnp.concatenate(ms, axis=1))
                da_in_g.append(jnp.where(below, ncat, 0.0))
                zacc_g.append(dy * yin * ea_g - dxw * xw)
                dxdt_g.append(dx_intra + dxw * wd_e[:, gc_])
                dy_g.append(dy)
                db_g.append(dbg + _dot_tn(dcbm, cg))
                dc_g.append(dcg + _dot(dcbm, bg))
            dy = jnp.concatenate(dy_g, axis=1)
            dxdt = jnp.concatenate(dxdt_g, axis=1)
            dx_l[c] = dxdt * dt_e[rows, :] + dy * d_e
            db_l[c] = jnp.concatenate(db_g, axis=1)
            dc_l[c] = jnp.concatenate(dc_g, axis=1)
            dz_l[c] = jnp.concatenate(dz_g, axis=1)
            ddt_l[c] = _reduce_heads(dxdt * xc, e)
            dalast = _reduce_heads(_row8(jnp.concatenate(dal_g, axis=1)), e)[0:1, :]
            dacum_l[c] = _reduce_heads(jnp.concatenate(zacc_g, axis=1), e) + jnp.where(row_id == CH - 1, dalast, 0.0)
            da_in_l[c] = _reduce_heads(jnp.concatenate(da_in_g, axis=1), e)

        dacum_all = jnp.concatenate(dacum_l, axis=0)
        da = _dot01l(_chunk_tri(tb, upper=True), dacum_all) + jnp.concatenate(da_in_l, axis=0)
        neg_ea = -jnp.exp(prm[0:1, :])
        ddt = jnp.concatenate(ddt_l, axis=0) + da * neg_ea
        ddt_in = ddt * _sigmoid(dt_in)
        dss_ref[...] = ddt_in.astype(dss_ref.dtype)
        sub8 = _iota2((8, 128), 0)
        dalog = jnp.sum(da * a, axis=0, keepdims=True)
        ddtb = jnp.sum(ddt_in, axis=0, keepdims=True)
        dd = _reduce_heads(_row8(dd_acc), e)[0:1, :]
        dprm_ref[...] += (jnp.where(sub8 == 0, dalog, 0.0) + jnp.where(sub8 == 1, ddtb, 0.0)
                          + jnp.where(sub8 == 2, dd, 0.0))
        dnw_ref[...] += jnp.where(_iota2((8, SSD_W), 0) == 0, dnw_acc, 0.0)

        dact = jnp.concatenate([jnp.concatenate(dx_l, axis=0), jnp.concatenate(db_l, axis=0),
                                jnp.concatenate(dc_l, axis=0)], axis=1)
        dpre = dact * _dsilu(pre)
        back = _conv_back(dpre, nxt_scr[...], tb)
        nxt_scr[...] = dpre[0:8, :]
        draw = back[0] * cw[3:4, :] + back[1] * cw[2:3, :] + back[2] * cw[1:2, :] + back[3] * cw[0:1, :]
        dps_ref[:, 0:1536] = draw.astype(dps_ref.dtype)
        dps_ref[:, 1536:2560] = jnp.concatenate(dz_l, axis=0).astype(dps_ref.dtype)
        sub_c = _iota2((8, 1536), 0)
        dcw_new = jnp.zeros((8, 1536), F32)
        for s_ in range(CONV_W):
            dcw_new = dcw_new + jnp.where(sub_c == 3 - s_, jnp.sum(dpre * taps[s_], axis=0, keepdims=True), 0.0)
        dcw_ref[...] += dcw_new
        dcb_ref[...] += jnp.where(sub_c == 0, jnp.sum(dpre, axis=0, keepdims=True), 0.0)

    def call(ps, ss, cw, cb, prm, nw, st, dob):
        rev = lambda i: (nb - 1 - i, 0)
        const = lambda i: (0, 0)
        return pl.pallas_call(
            body,
            grid=(nb,),
            in_specs=[
                pl.BlockSpec((tb, 2560), rev),
                pl.BlockSpec((8, 1536), lambda i: (jnp.maximum((nb - 1 - i) * hb - 1, 0), 0)),
                pl.BlockSpec((tb, 128), rev),
                pl.BlockSpec((8, 1536), const),
                pl.BlockSpec((8, 1536), const),
                pl.BlockSpec((8, 128), const),
                pl.BlockSpec((8, SSD_W), const),
                pl.BlockSpec((ncb, SSD_N, SSD_W), lambda i: (nb - 1 - i, 0, 0)),
                pl.BlockSpec((tb, SSD_W), rev),
            ],
            out_specs=[
                pl.BlockSpec((tb, 2560), rev),
                pl.BlockSpec((tb, 128), rev),
                pl.BlockSpec((8, 1536), const),
                pl.BlockSpec((8, 1536), const),
                pl.BlockSpec((8, 128), const),
                pl.BlockSpec((8, SSD_W), const),
            ],
            out_shape=[
                jax.ShapeDtypeStruct((seq, 2560), BF16),
                jax.ShapeDtypeStruct((seq, 128), BF16),
                jax.ShapeDtypeStruct((8, 1536), F32),
                jax.ShapeDtypeStruct((8, 1536), F32),
                jax.ShapeDtypeStruct((8, 128), F32),
                jax.ShapeDtypeStruct((8, SSD_W), F32),
            ],
            scratch_shapes=[pltpu.VMEM((SSD_N, SSD_W), F32), pltpu.VMEM((8, 1536), F32)],
            compiler_params=pltpu.CompilerParams(dimension_semantics=("arbitrary",), vmem_limit_bytes=VMEM_LIMIT),
            name="ssd_bwd",
        )(ps, ps, ss, cw, cb, prm, nw, st, dob)

    return call


def _ret_consts(h):
    lg = math.log(1.0 - 2.0 ** (-5.0 - h))
    r = _iota2((CH, CH), 0)
    c = _iota2((CH, CH), 1)
    rel = (r - c).astype(F32)
    dmat = jnp.where(r >= c, jnp.exp(jnp.maximum(rel, 0.0) * lg), 0.0)
    idx = _iota2((CH, 1), 0).astype(F32)
    qdec = jnp.exp((idx + 1.0) * lg)
    kdec = jnp.exp((CH - 1.0 - idx) * lg)
    cdec = math.exp(CH * lg)
    return dmat, qdec, kdec, cdec


def _ret_batch(pr_ref, cc_ref, ss_ref, ncb):
    pairs = [(c, h) for c in range(ncb) for h in range(RET_H)]

    def st(off):
        return jnp.stack([pr_ref[c * CH:(c + 1) * CH, off + h * 128:off + (h + 1) * 128] for c, h in pairs])

    cc = jnp.stack([cc_ref[c * CH:(c + 1) * CH, :] for c, _ in pairs])
    ss = jnp.stack([ss_ref[c * CH:(c + 1) * CH, :] for c, _ in pairs])
    consts = [_ret_consts(h) for h in range(RET_H)]
    dmat = jnp.stack([consts[h][0] for _, h in pairs])
    qdec = jnp.stack([consts[h][1] for _, h in pairs])
    kdec = jnp.stack([consts[h][2] for _, h in pairs])
    cdec = jnp.stack([jnp.full((1, 1), consts[h][3], F32) for h in range(RET_H)])
    q = _rot(st(0), cc, ss)
    k = _rot(st(512), cc, ss) * (RET_D ** -0.5)
    return dict(q=q, k=k, v=st(1024), z=st(1536), cc=cc, ss=ss, dmat=dmat, qdec=qdec, kdec=kdec, cdec=cdec,
                s=_bdot(q, k, _NT) * dmat)


def _rot(t, cc, ss):
    return t * cc + pltpu.roll(t, 64, axis=t.ndim - 1) * ss


def _rot_bwd(d, cc, ss):
    return d * cc + pltpu.roll(d * ss, 64, axis=d.ndim - 1)


def _make_ret_fwd(seq, tb):
    ncb = tb // CH
    nb = seq // tb

    def body(pr_ref, cc_ref, ss_ref, nw_ref, oc_ref, st_ref, r_scr):
        @pl.when(pl.program_id(0) == 0)
        def _():
            r_scr[...] = jnp.zeros_like(r_scr)

        d = _ret_batch(pr_ref, cc_ref, ss_ref, ncb)
        kd = d["k"] * d["kdec"]
        for c in range(ncb):
            bs = slice(c * RET_H, (c + 1) * RET_H)
            rs = r_scr[...]
            st_ref[c] = rs
            r_scr[...] = rs * d["cdec"] + _bdot(kd[bs], d["v"][bs], _TN)
        r_prev = st_ref[...].reshape(ncb * RET_H, 128, 128)
        o = _bdot(d["s"], d["v"], _NN) + _bdot(d["q"], r_prev, _NN) * d["qdec"]
        _, _, y = _rms_fwd(o, nw_ref[0:1, :], RET_D)
        out = y * _silu(d["z"])
        for c in range(ncb):
            for h in range(RET_H):
                oc_ref[c * CH:(c + 1) * CH, h * 128:(h + 1) * 128] = out[c * RET_H + h]

    def call(pr, cc, ss, nw):
        return pl.pallas_call(
            body,
            grid=(nb,),
            in_specs=[
                pl.BlockSpec((tb, 2048), lambda i: (i, 0)),
                pl.BlockSpec((tb, 128), lambda i: (i, 0)),
                pl.BlockSpec((tb, 128), lambda i: (i, 0)),
                pl.BlockSpec((8, 128), lambda i: (0, 0)),
            ],
            out_specs=[
                pl.BlockSpec((tb, 512), lambda i: (i, 0)),
                pl.BlockSpec((ncb, RET_H, 128, 128), lambda i: (i, 0, 0, 0)),
            ],
            out_shape=[
                jax.ShapeDtypeStruct((seq, 512), F32),
                jax.ShapeDtypeStruct((seq // CH, RET_H, 128, 128), F32),
            ],
            scratch_shapes=[pltpu.VMEM((RET_H, 128, 128), F32)],
            compiler_params=pltpu.CompilerParams(dimension_semantics=("arbitrary",), vmem_limit_bytes=VMEM_LIMIT),
            name="ret_fwd",
        )(pr, cc, ss, nw)

    return call


def _make_ret_bwd(seq, tb):
    ncb = tb // CH
    nb = seq // tb

    def body(pr_ref, cc_ref, ss_ref, nw_ref, st_ref, doc_ref, dpr_ref, dnw_ref, dr_scr):
        @pl.when(pl.program_id(0) == 0)
        def _():
            dr_scr[...] = jnp.zeros_like(dr_scr)
            dnw_ref[...] = jnp.zeros_like(dnw_ref)

        nw = nw_ref[0:1, :]
        scale = RET_D ** -0.5
        n = ncb * RET_H
        d = _ret_batch(pr_ref, cc_ref, ss_ref, ncb)
        q, k, v, z, s = d["q"], d["k"], d["v"], d["z"], d["s"]
        r_prev = st_ref[...].reshape(n, 128, 128)
        o = _bdot(s, v, _NN) + _bdot(q, r_prev, _NN) * d["qdec"]
        doc = jnp.stack([doc_ref[c * CH:(c + 1) * CH, h * 128:(h + 1) * 128]
                         for c in range(ncb) for h in range(RET_H)])
        on, r, y = _rms_fwd(o, nw, RET_D)
        dz = doc * y * _dsilu(z)
        do, dnw_rows = _rms_bwd(doc * _silu(z), on, r, nw, RET_D)
        dnw_acc = jnp.sum(jnp.sum(dnw_rows, axis=0), axis=0, keepdims=True)
        dqd = do * d["qdec"]
        qtd = _bdot(q, dqd, _TN)
        drn_l = [None] * ncb
        for c in reversed(range(ncb)):
            drn_l[c] = dr_scr[...]
            dr_scr[...] = qtd[c * RET_H:(c + 1) * RET_H] + d["cdec"] * drn_l[c]
        drn = jnp.concatenate(drn_l, axis=0)
        ds = _bdot(do, v, _NT) * d["dmat"]
        dq = _rot_bwd(_bdot(ds, k, _NN) + _bdot(dqd, r_prev, _NT), d["cc"], d["ss"])
        dk = _rot_bwd((_bdot(ds, q, _TN) + _bdot(v, drn, _NT) * d["kdec"]) * scale, d["cc"], d["ss"])
        dv = _bdot(s, do, _TN) + _bdot(k * d["kdec"], drn, _NN)
        for c in range(ncb):
            rows = slice(c * CH, (c + 1) * CH)
            for h in range(RET_H):
                b = c * RET_H + h
                for j, val in enumerate((dq, dk, dv, dz)):
                    dpr_ref[rows, j * 512 + h * 128:j * 512 + (h + 1) * 128] = val[b].astype(dpr_ref.dtype)
        dnw_ref[...] += jnp.where(_iota2((8, 128), 0) == 0, dnw_acc, 0.0)

    def call(pr, cc, ss, nw, st, doc):
        rev = lambda i: (nb - 1 - i, 0)
        return pl.pallas_call(
            body,
            grid=(nb,),
            in_specs=[
                pl.BlockSpec((tb, 2048), rev),
                pl.BlockSpec((tb, 128), rev),
                pl.BlockSpec((tb, 128), rev),
                pl.BlockSpec((8, 128), lambda i: (0, 0)),
                pl.BlockSpec((ncb, RET_H, 128, 128), lambda i: (nb - 1 - i, 0, 0, 0)),
                pl.BlockSpec((tb, 512), rev),
            ],
            out_specs=[
                pl.BlockSpec((tb, 2048), rev),
                pl.BlockSpec((8, 128), lambda i: (0, 0)),
            ],
            out_shape=[
                jax.ShapeDtypeStruct((seq, 2048), BF16),
                jax.ShapeDtypeStruct((8, 128), F32),
            ],
            scratch_shapes=[pltpu.VMEM((RET_H, 128, 128), F32)],
            compiler_params=pltpu.CompilerParams(dimension_semantics=("arbitrary",), vmem_limit_bytes=VMEM_LIMIT),
            name="ret_bwd",
        )(pr, cc, ss, nw, st, doc)

    return call


def _rope_tables(seq):
    half = RET_D // 2
    inv = ROPE_BASE ** (-jnp.arange(half, dtype=F32) / half)
    ang = jnp.arange(seq, dtype=jnp.int32).astype(F32)[:, None] * inv[None, :]
    cos, sin = jnp.cos(ang), jnp.sin(ang)
    return jnp.concatenate([cos, cos], axis=1), jnp.concatenate([-sin, sin], axis=1)


SEG_G, SEG_S, SEG_R, SEG_GS, SEG_SS = (0, 2048), (2048, 4608), (4608, 6656), (6656, 6784), (6784, 6912)
NP = 6912
SEGS = (SEG_G, SEG_S, SEG_R, SEG_GS, SEG_SS)


def _resident(shape):
    return pl.BlockSpec(shape, lambda i: (0,) * len(shape), pipeline_mode=pl.Buffered(1))


def _make_inproj(seq, tl):
    def body(x_ref, pn_ref, w_ref, pg_ref, ps_ref, pr_ref, gs_ref, ss_ref, ht_ref):
        x = x_ref[...]
        _, _, hn = _rms_fwd(x, pn_ref[0:1, :], D_MODEL)
        h = hn.astype(BF16)
        ht_ref[...] = hn.T.astype(BF16)
        for (a, b), o_ref in zip(SEGS, (pg_ref, ps_ref, pr_ref, gs_ref, ss_ref)):
            o_ref[...] = jnp.dot(h, w_ref[:, a:b], preferred_element_type=F32)

    def call(x, pn, w):
        row = lambda i: (i, 0)
        return pl.pallas_call(
            body,
            grid=(seq // tl,),
            in_specs=[pl.BlockSpec((tl, D_MODEL), row), _resident((8, D_MODEL)), _resident((D_MODEL, NP))],
            out_specs=[pl.BlockSpec((tl, b - a), row) for a, b in SEGS]
            + [pl.BlockSpec((D_MODEL, tl), lambda i: (0, i))],
            out_shape=[jax.ShapeDtypeStruct((seq, b - a), F32) for a, b in SEGS]
            + [jax.ShapeDtypeStruct((D_MODEL, seq), BF16)],
            compiler_params=pltpu.CompilerParams(dimension_semantics=("arbitrary",), vmem_limit_bytes=VMEM_LIMIT),
            name="inproj",
        )(x, pn, w)

    return call


def _make_outproj(seq, tl):
    def body(oa_ref, ob_ref, oc_ref, w_ref, x_ref, qn_ref, out_ref, xn_ref):
        out = (jnp.dot(oa_ref[...].astype(BF16), w_ref[0:512, :], preferred_element_type=F32)
               + jnp.dot(ob_ref[...].astype(BF16), w_ref[512:1536, :], preferred_element_type=F32)
               + jnp.dot(oc_ref[...].astype(BF16), w_ref[1536:2048, :], preferred_element_type=F32))
        out_ref[...] = out
        _, _, y = _rms_fwd(out, qn_ref[0:1, :], D_MODEL)
        xn_ref[...] = x_ref[...] + y

    def call(oa, ob, oc, w, x, qn):
        row = lambda i: (i, 0)
        return pl.pallas_call(
            body,
            grid=(seq // tl,),
            in_specs=[pl.BlockSpec((tl, 512), row), pl.BlockSpec((tl, 1024), row), pl.BlockSpec((tl, 512), row),
                      _resident((2048, D_MODEL)), pl.BlockSpec((tl, D_MODEL), row), _resident((8, D_MODEL))],
            out_specs=[pl.BlockSpec((tl, D_MODEL), row), pl.BlockSpec((tl, D_MODEL), row)],
            out_shape=[jax.ShapeDtypeStruct((seq, D_MODEL), F32), jax.ShapeDtypeStruct((seq, D_MODEL), F32)],
            compiler_params=pltpu.CompilerParams(dimension_semantics=("arbitrary",), vmem_limit_bytes=VMEM_LIMIT),
            name="outproj",
        )(oa, ob, oc, w, x, qn)

    return call


def _make_loss_head(seq, tl):
    def body(y_ref, t_ref, dy_ref, loss_ref):
        @pl.when(pl.program_id(0) == 0)
        def _():
            loss_ref[...] = jnp.zeros_like(loss_ref)

        err = y_ref[...] - t_ref[...]
        dy_ref[...] = err * (1.0 / D_MODEL)
        part = jnp.sum(jnp.sum(err * err, axis=1, keepdims=True), axis=0, keepdims=True) * (0.5 / D_MODEL)
        loss_ref[...] += jnp.where((_iota2((8, 128), 0) == 0) & (_iota2((8, 128), 1) == 0), part, 0.0)

    def call(y, t):
        row = lambda i: (i, 0)
        return pl.pallas_call(
            body,
            grid=(seq // tl,),
            in_specs=[pl.BlockSpec((tl, D_MODEL), row), pl.BlockSpec((tl, D_MODEL), row)],
            out_specs=[pl.BlockSpec((tl, D_MODEL), row), pl.BlockSpec((8, 128), lambda i: (0, 0))],
            out_shape=[jax.ShapeDtypeStruct((seq, D_MODEL), F32), jax.ShapeDtypeStruct((8, 128), F32)],
            compiler_params=pltpu.CompilerParams(dimension_semantics=("arbitrary",)),
            name="loss_head",
        )(y, t)

    return call


def _make_outproj_bwd(seq, tl):
    def body(dxn_ref, out_ref, oa_ref, ob_ref, oc_ref, w_ref, qn_ref, doa_ref, dob_ref, doc_ref, dqn_ref, dw_ref):
        @pl.when(pl.program_id(0) == 0)
        def _():
            dqn_ref[...] = jnp.zeros_like(dqn_ref)
            dw_ref[...] = jnp.zeros_like(dw_ref)

        qn = qn_ref[0:1, :]
        on, r, _ = _rms_fwd(out_ref[...], qn, D_MODEL)
        dout, dqn_rows = _rms_bwd(dxn_ref[...], on, r, qn, D_MODEL)
        dqn_ref[...] += jnp.where(_iota2((8, D_MODEL), 0) == 0, jnp.sum(dqn_rows, axis=0, keepdims=True), 0.0)
        db = dout.astype(BF16)
        nt = (((1,), (1,)), ((), ()))
        tn = (((0,), (0,)), ((), ()))
        doa_ref[...] = lax.dot_general(db, w_ref[0:512, :], nt, preferred_element_type=F32)
        dob_ref[...] = lax.dot_general(db, w_ref[512:1536, :], nt, preferred_element_type=F32)
        doc_ref[...] = lax.dot_general(db, w_ref[1536:2048, :], nt, preferred_element_type=F32)
        dw_ref[0:512, :] += lax.dot_general(oa_ref[...].astype(BF16), db, tn, preferred_element_type=F32)
        dw_ref[512:1536, :] += lax.dot_general(ob_ref[...].astype(BF16), db, tn, preferred_element_type=F32)
        dw_ref[1536:2048, :] += lax.dot_general(oc_ref[...].astype(BF16), db, tn, preferred_element_type=F32)

    def call(dxn, out, oa, ob, oc, w, qn):
        row = lambda i: (i, 0)
        const = lambda i: (0, 0)
        return pl.pallas_call(
            body,
            grid=(seq // tl,),
            in_specs=[pl.BlockSpec((tl, D_MODEL), row), pl.BlockSpec((tl, D_MODEL), row),
                      pl.BlockSpec((tl, 512), row), pl.BlockSpec((tl, 1024), row), pl.BlockSpec((tl, 512), row),
                      _resident((2048, D_MODEL)), _resident((8, D_MODEL))],
            out_specs=[pl.BlockSpec((tl, 512), row), pl.BlockSpec((tl, 1024), row), pl.BlockSpec((tl, 512), row),
                       pl.BlockSpec((8, D_MODEL), const), pl.BlockSpec((2048, D_MODEL), const)],
            out_shape=[jax.ShapeDtypeStruct((seq, 512), F32), jax.ShapeDtypeStruct((seq, 1024), F32),
                       jax.ShapeDtypeStruct((seq, 512), F32), jax.ShapeDtypeStruct((8, D_MODEL), F32),
                       jax.ShapeDtypeStruct((2048, D_MODEL), F32)],
            compiler_params=pltpu.CompilerParams(dimension_semantics=("arbitrary",), vmem_limit_bytes=VMEM_LIMIT),
            name="outproj_bwd",
        )(dxn, out, oa, ob, oc, w, qn)

    return call


def _make_inproj_bwd_dx(seq, tl):
    def body(dg_ref, ds_ref, dr_ref, dgs_ref, dss_ref, w_ref, x_ref, pn_ref, dxn_ref, dx_ref, dpn_ref):
        @pl.when(pl.program_id(0) == 0)
        def _():
            dpn_ref[...] = jnp.zeros_like(dpn_ref)

        nt = (((1,), (1,)), ((), ()))
        dh = jnp.zeros((tl, D_MODEL), F32)
        for (a, b), d_ref in zip(SEGS, (dg_ref, ds_ref, dr_ref, dgs_ref, dss_ref)):
            dh = dh + lax.dot_general(d_ref[...], w_ref[:, a:b], nt, preferred_element_type=F32)
        pn = pn_ref[0:1, :]
        on, r, _ = _rms_fwd(x_ref[...], pn, D_MODEL)
        dx, dpn_rows = _rms_bwd(dh, on, r, pn, D_MODEL)
        dx_ref[...] = dx + dxn_ref[...]
        dpn_ref[...] += jnp.where(_iota2((8, D_MODEL), 0) == 0, jnp.sum(dpn_rows, axis=0, keepdims=True), 0.0)

    def call(dg, ds, dr, dgs, dss, w, x, pn, dxn):
        row = lambda i: (i, 0)
        return pl.pallas_call(
            body,
            grid=(seq // tl,),
            in_specs=[pl.BlockSpec((tl, b - a), row) for a, b in SEGS]
            + [_resident((D_MODEL, NP)), pl.BlockSpec((tl, D_MODEL), row), _resident((8, D_MODEL)),
               pl.BlockSpec((tl, D_MODEL), row)],
            out_specs=[pl.BlockSpec((tl, D_MODEL), row), pl.BlockSpec((8, D_MODEL), lambda i: (0, 0))],
            out_shape=[jax.ShapeDtypeStruct((seq, D_MODEL), F32), jax.ShapeDtypeStruct((8, D_MODEL), F32)],
            compiler_params=pltpu.CompilerParams(dimension_semantics=("arbitrary",), vmem_limit_bytes=VMEM_LIMIT),
            name="inproj_bwd_dx",
        )(dg, ds, dr, dgs, dss, w, x, pn, dxn)

    return call


def _make_inproj_bwd_dw(seq, tl, width, tn, name):
    def body(ht_ref, d_ref, dw_ref):
        @pl.when(pl.program_id(1) == 0)
        def _():
            dw_ref[...] = jnp.zeros_like(dw_ref)

        dw_ref[...] += jnp.dot(ht_ref[...], d_ref[...], preferred_element_type=F32)

    def call(ht, d):
        return pl.pallas_call(
            body,
            grid=(width // tn, seq // tl),
            in_specs=[pl.BlockSpec((D_MODEL, tl), lambda j, i: (0, i)), pl.BlockSpec((tl, tn), lambda j, i: (i, j))],
            out_specs=pl.BlockSpec((D_MODEL, tn), lambda j, i: (0, j)),
            out_shape=jax.ShapeDtypeStruct((D_MODEL, width), F32),
            compiler_params=pltpu.CompilerParams(dimension_semantics=("arbitrary", "arbitrary"),
                                                 vmem_limit_bytes=VMEM_LIMIT),
            name=name,
        )(ht, d)

    return call


ADAM_LR, ADAM_B1, ADAM_B2, ADAM_EPS, ADAM_WD, ADAM_STEP = 0.001, 0.9, 0.999, 1e-08, 0.01, 10


def _adam_math(w, g, m, v):
    m = ADAM_B1 * m + (1.0 - ADAM_B1) * g
    v = ADAM_B2 * v + (1.0 - ADAM_B2) * (g * g)
    m_hat = m / (1.0 - ADAM_B1 ** ADAM_STEP)
    v_hat = v / (1.0 - ADAM_B2 ** ADAM_STEP)
    delta = -ADAM_LR * (m_hat / (jnp.sqrt(v_hat) + ADAM_EPS) + ADAM_WD * w)
    return delta, m, v


def _adamw(w, g, m, v, name):
    shape = w.shape
    cols = shape[-1]
    rows = w.size // cols
    tr = rows if rows <= 512 else 256
    assert rows % tr == 0

    def body(w_ref, g_ref, m_ref, v_ref, d_ref, mo_ref, vo_ref):
        d_ref[...], mo_ref[...], vo_ref[...] = _adam_math(w_ref[...], g_ref[...], m_ref[...], v_ref[...])

    spec = pl.BlockSpec((tr, cols), lambda i: (i, 0))
    outs = pl.pallas_call(
        body,
        grid=(rows // tr,),
        in_specs=[spec] * 4,
        out_specs=[spec] * 3,
        out_shape=[jax.ShapeDtypeStruct((rows, cols), F32)] * 3,
        compiler_params=pltpu.CompilerParams(dimension_semantics=("arbitrary",), vmem_limit_bytes=VMEM_LIMIT),
        name=name,
    )(*[a.reshape(rows, cols) for a in (w, g, m, v)])
    return (g,) + tuple(o.reshape(shape) for o in outs)


MESH = pl.DeviceIdType.MESH
ANY = pl.BlockSpec(memory_space=pl.ANY)
CHIP_REL = ((1, 0), (0, 1), (1, 1))


def _flip(v, d):
    return 1 - v if d else v


def _ag_chips(arrs, name):
    n = len(arrs)

    def body(*refs):
        ins, outs = refs[:n], refs[n:2 * n]
        send_sems, recv_sems, loc_sems = refs[2 * n:]
        x, y, c = lax.axis_index("x"), lax.axis_index("y"), lax.axis_index("c")
        me = 2 * x + y

        def remote(a, k, slot):
            dx, dy = CHIP_REL[k]
            return pltpu.make_async_remote_copy(
                src_ref=ins[a], dst_ref=outs[a].at[slot], send_sem=send_sems.at[a * 3 + k],
                recv_sem=recv_sems.at[a * 3 + k], device_id=(_flip(x, dx), _flip(y, dy), c), device_id_type=MESH)

        local = [pltpu.make_async_copy(ins[a], outs[a].at[me], loc_sems.at[a]) for a in range(n)]
        for cp in local:
            cp.start()
        for a in range(n):
            for k in range(3):
                remote(a, k, me).start()
        for a in range(n):
            for k, (dx, dy) in enumerate(CHIP_REL):
                remote(a, k, 2 * _flip(x, dx) + _flip(y, dy)).wait_recv()
        for a in range(n):
            for k in range(3):
                remote(a, k, me).wait_send()
        for cp in local:
            cp.wait()

    return pl.pallas_call(
        body,
        in_specs=[ANY] * n,
        out_specs=[ANY] * n,
        out_shape=[jax.ShapeDtypeStruct((4,) + a.shape, a.dtype) for a in arrs],
        scratch_shapes=[pltpu.SemaphoreType.DMA((3 * n,)), pltpu.SemaphoreType.DMA((3 * n,)),
                        pltpu.SemaphoreType.DMA((n,))],
        compiler_params=pltpu.CompilerParams(has_side_effects=True),
        name=name,
    )(*arrs)


def _rs_chips(arrs, name):
    n = len(arrs)

    def body(*refs):
        ins, outs = refs[:n], refs[n:2 * n]
        send_sems, recv_sems = refs[2 * n:]
        x, y, c = lax.axis_index("x"), lax.axis_index("y"), lax.axis_index("c")

        def remote(a, k):
            dx, dy = CHIP_REL[k]
            px, py = _flip(x, dx), _flip(y, dy)
            return pltpu.make_async_remote_copy(
                src_ref=ins[a].at[2 * px + py], dst_ref=outs[a].at[k], send_sem=send_sems.at[a * 3 + k],
                recv_sem=recv_sems.at[a * 3 + k], device_id=(px, py, c), device_id_type=MESH)

        cps = [remote(a, k) for a in range(n) for k in range(3)]
        for cp in cps:
            cp.start()
        for cp in cps:
            cp.wait_recv()
        for cp in cps:
            cp.wait_send()

    return pl.pallas_call(
        body,
        in_specs=[ANY] * n,
        out_specs=[ANY] * n,
        out_shape=[jax.ShapeDtypeStruct((3,) + a.shape[1:], a.dtype) for a in arrs],
        scratch_shapes=[pltpu.SemaphoreType.DMA((3 * n,)), pltpu.SemaphoreType.DMA((3 * n,))],
        compiler_params=pltpu.CompilerParams(has_side_effects=True),
        name=name,
    )(*arrs)


def _half(ref_or_shape, half):
    r = ref_or_shape[-2] // 2
    return pl.ds(half * r, r)


def _ag_rows(arrs, name):
    n = len(arrs)

    def body(*refs):
        ins, outs = refs[:n], refs[n:2 * n]
        send_sems, recv_sems, fsend_sems, frecv_sems, loc_sems = refs[2 * n:]
        x, y, c = lax.axis_index("x"), lax.axis_index("y"), lax.axis_index("c")
        me = 2 * x + y
        sib = (x, y, 1 - c)

        def chip_of(k):
            dx, dy = CHIP_REL[k]
            return _flip(x, dx), _flip(y, dy)

        def ici(a, k, slot):
            px, py = chip_of(k)
            rows = _half(arrs[a].shape, c)
            return pltpu.make_async_remote_copy(
                src_ref=ins[a].at[:, rows, :], dst_ref=outs[a].at[slot, :, rows, :], send_sem=send_sems.at[a * 3 + k],
                recv_sem=recv_sems.at[a * 3 + k], device_id=(px, py, c), device_id_type=MESH)

        def fwd(a, k, half):
            px, py = chip_of(k)
            blk = outs[a].at[2 * px + py, :, _half(arrs[a].shape, half), :]
            return pltpu.make_async_remote_copy(
                src_ref=blk, dst_ref=blk, send_sem=fsend_sems.at[a * 3 + k], recv_sem=frecv_sems.at[a * 3 + k],
                device_id=sib, device_id_type=MESH)

        local = [pltpu.make_async_copy(ins[a], outs[a].at[me], loc_sems.at[a]) for a in range(n)]
        for cp in local:
            cp.start()
        for a in range(n):
            for k in range(3):
                ici(a, k, me).start()
        for a in range(n):
            for k in range(3):
                px, py = chip_of(k)
                ici(a, k, 2 * px + py).wait_recv()
                fwd(a, k, c).start()
        for a in range(n):
            for k in range(3):
                fwd(a, k, 1 - c).wait_recv()
        for a in range(n):
            for k in range(3):
                ici(a, k, me).wait_send()
                fwd(a, k, c).wait_send()
        for cp in local:
            cp.wait()

    return pl.pallas_call(
        body,
        in_specs=[ANY] * n,
        out_specs=[ANY] * n,
        out_shape=[jax.ShapeDtypeStruct((4,) + a.shape, a.dtype) for a in arrs],
        scratch_shapes=[pltpu.SemaphoreType.DMA((3 * n,)) for _ in range(4)] + [pltpu.SemaphoreType.DMA((n,))],
        compiler_params=pltpu.CompilerParams(has_side_effects=True),
        name=name,
    )(*arrs)


def _rs_to_sibling(arrs, name):
    n = len(arrs)

    def body(*refs):
        ins, outs = refs[:n], refs[n:2 * n]
        send_sems, recv_sems = refs[2 * n:]
        x, y, c = lax.axis_index("x"), lax.axis_index("y"), lax.axis_index("c")
        cps = [pltpu.make_async_remote_copy(
            src_ref=ins[a].at[:, :, _half(arrs[a].shape, 1 - c), :], dst_ref=outs[a], send_sem=send_sems.at[a],
            recv_sem=recv_sems.at[a], device_id=(x, y, 1 - c), device_id_type=MESH) for a in range(n)]
        for cp in cps:
            cp.start()
        for cp in cps:
            cp.wait_recv()
        for cp in cps:
            cp.wait_send()

    return pl.pallas_call(
        body,
        in_specs=[ANY] * n,
        out_specs=[ANY] * n,
        out_shape=[jax.ShapeDtypeStruct(a.shape[:2] + (a.shape[2] // 2, a.shape[3]), a.dtype) for a in arrs],
        scratch_shapes=[pltpu.SemaphoreType.DMA((n,)), pltpu.SemaphoreType.DMA((n,))],
        compiler_params=pltpu.CompilerParams(has_side_effects=True),
        name=name,
    )(*arrs)


def _add_halves(full, recv, core, name):
    _, na, r, cols = full.shape
    rh = r // 2
    tr = 256
    assert rh % tr == 0
    nt = rh // tr

    def body(core_ref, f_ref, r_ref, s_ref, sb_ref):
        s = f_ref[...] + r_ref[...]
        s_ref[...] = s
        sb_ref[...] = s.astype(BF16)

    blk = (None, None, tr, cols)
    return pl.pallas_call(
        body,
        grid_spec=pltpu.PrefetchScalarGridSpec(
            num_scalar_prefetch=1,
            grid=(4, na, nt),
            in_specs=[pl.BlockSpec(blk, lambda k, a, i, cr: (k, a, cr[0] * nt + i, 0)),
                      pl.BlockSpec(blk, lambda k, a, i, cr: (k, a, i, 0))],
            out_specs=[pl.BlockSpec(blk, lambda k, a, i, cr: (k, a, i, 0)),
                       pl.BlockSpec(blk, lambda k, a, i, cr: (k, a, i, 0))]),
        out_shape=[jax.ShapeDtypeStruct(recv.shape, F32), jax.ShapeDtypeStruct(recv.shape, BF16)],
        compiler_params=pltpu.CompilerParams(dimension_semantics=("arbitrary",) * 3, vmem_limit_bytes=VMEM_LIMIT),
        name=name,
    )(core, full, recv)


def _sum_chips(own, recv, chip, name):
    _, na, r, cols = own.shape
    tr = 256
    assert r % tr == 0

    def body(chip_ref, o_ref, r_ref, s_ref):
        s_ref[...] = ((o_ref[...] + r_ref[0].astype(F32)) + r_ref[1].astype(F32)) + r_ref[2].astype(F32)

    return pl.pallas_call(
        body,
        grid_spec=pltpu.PrefetchScalarGridSpec(
            num_scalar_prefetch=1,
            grid=(na, r // tr),
            in_specs=[pl.BlockSpec((None, None, tr, cols), lambda a, i, ch: (ch[0], a, i, 0)),
                      pl.BlockSpec((3, None, tr, cols), lambda a, i, ch: (0, a, i, 0))],
            out_specs=pl.BlockSpec((None, tr, cols), lambda a, i, ch: (a, i, 0))),
        out_shape=jax.ShapeDtypeStruct((na, r, cols), F32),
        compiler_params=pltpu.CompilerParams(dimension_semantics=("arbitrary",) * 2, vmem_limit_bytes=VMEM_LIMIT),
        name=name,
    )(chip, own, recv)


def _join_halves(arrs, name):
    n = len(arrs)

    def body(*refs):
        ins, outs = refs[:n], refs[n:2 * n]
        send_sems, recv_sems, loc_sems = refs[2 * n:]
        x, y, c = lax.axis_index("x"), lax.axis_index("y"), lax.axis_index("c")
        local, remote = [], []
        for a in range(n):
            mine = outs[a].at[:, _half(outs[a].shape, c), :]
            local.append(pltpu.make_async_copy(ins[a], mine, loc_sems.at[a]))
            remote.append(pltpu.make_async_remote_copy(
                src_ref=ins[a], dst_ref=mine, send_sem=send_sems.at[a], recv_sem=recv_sems.at[a],
                device_id=(x, y, 1 - c), device_id_type=MESH))
        for cp in local + remote:
            cp.start()
        for a in range(n):
            theirs = outs[a].at[:, _half(outs[a].shape, 1 - c), :]
            pltpu.make_async_remote_copy(src_ref=ins[a], dst_ref=theirs, send_sem=send_sems.at[a],
                                         recv_sem=recv_sems.at[a], device_id=(x, y, 1 - c),
                                         device_id_type=MESH).wait_recv()
        for cp in remote:
            cp.wait_send()
        for cp in local:
            cp.wait()

    return pl.pallas_call(
        body,
        in_specs=[ANY] * n,
        out_specs=[ANY] * n,
        out_shape=[jax.ShapeDtypeStruct((a.shape[0], 2 * a.shape[1], a.shape[2]), a.dtype) for a in arrs],
        scratch_shapes=[pltpu.SemaphoreType.DMA((n,)), pltpu.SemaphoreType.DMA((n,)), pltpu.SemaphoreType.DMA((n,))],
        compiler_params=pltpu.CompilerParams(has_side_effects=True),
        name=name,
    )(*arrs)


def _swap_sibling(arrs, name):
    n = len(arrs)

    def body(*refs):
        ins, outs = refs[:n], refs[n:2 * n]
        send_sems, recv_sems = refs[2 * n:]
        x, y, c = lax.axis_index("x"), lax.axis_index("y"), lax.axis_index("c")
        cps = [pltpu.make_async_remote_copy(src_ref=ins[a], dst_ref=outs[a], send_sem=send_sems.at[a],
                                            recv_sem=recv_sems.at[a], device_id=(x, y, 1 - c), device_id_type=MESH)
               for a in range(n)]
        for cp in cps:
            cp.start()
        for cp in cps:
            cp.wait_recv()
        for cp in cps:
            cp.wait_send()

    return pl.pallas_call(
        body,
        in_specs=[ANY] * n,
        out_specs=[ANY] * n,
        out_shape=[jax.ShapeDtypeStruct(a.shape, a.dtype) for a in arrs],
        scratch_shapes=[pltpu.SemaphoreType.DMA((n,)), pltpu.SemaphoreType.DMA((n,))],
        compiler_params=pltpu.CompilerParams(has_side_effects=True),
        name=name,
    )(*arrs)


def _allreduce_small(vec, name):
    rows = vec.shape[0]

    def body(v_ref, out_ref, gat_ref, send_sems, recv_sems):
        x, y, c = lax.axis_index("x"), lax.axis_index("y"), lax.axis_index("c")
        me = 4 * x + 2 * y + c

        def remote(k, slot):
            dx, dy, dc = (k >> 2) & 1, (k >> 1) & 1, k & 1
            return pltpu.make_async_remote_copy(
                src_ref=v_ref, dst_ref=gat_ref.at[slot], send_sem=send_sems.at[k - 1], recv_sem=recv_sems.at[k - 1],
                device_id=(_flip(x, dx), _flip(y, dy), _flip(c, dc)), device_id_type=MESH)

        gat_ref[me] = v_ref[...]
        for k in range(1, 8):
            remote(k, me).start()
        for k in range(1, 8):
            dx, dy, dc = (k >> 2) & 1, (k >> 1) & 1, k & 1
            remote(k, 4 * _flip(x, dx) + 2 * _flip(y, dy) + _flip(c, dc)).wait_recv()
        for k in range(1, 8):
            remote(k, me).wait_send()
        acc = gat_ref[0]
        for j in range(1, 8):
            acc = acc + gat_ref[j]
        out_ref[...] = acc

    vm = pl.BlockSpec(memory_space=pltpu.VMEM)
    return pl.pallas_call(
        body,
        in_specs=[vm],
        out_specs=vm,
        out_shape=jax.ShapeDtypeStruct(vec.shape, F32),
        scratch_shapes=[pltpu.VMEM((8, rows, 128), F32), pltpu.SemaphoreType.DMA((7,)), pltpu.SemaphoreType.DMA((7,))],
        compiler_params=pltpu.CompilerParams(has_side_effects=True),
        name=name,
    )(vec)


def _sum4(own, recv, name):
    shape = own.shape
    cols = shape[-1]
    rows = own.size // cols
    tr = 256
    assert rows % tr == 0

    def body(o_ref, r_ref, s_ref):
        s_ref[...] = ((o_ref[...] + r_ref[0]) + r_ref[1]) + r_ref[2]

    out = pl.pallas_call(
        body,
        grid=(rows // tr,),
        in_specs=[pl.BlockSpec((tr, cols), lambda i: (i, 0)), pl.BlockSpec((3, tr, cols), lambda i: (0, i, 0))],
        out_specs=pl.BlockSpec((tr, cols), lambda i: (i, 0)),
        out_shape=jax.ShapeDtypeStruct((rows, cols), F32),
        compiler_params=pltpu.CompilerParams(dimension_semantics=("arbitrary",), vmem_limit_bytes=VMEM_LIMIT),
        name=name,
    )(own.reshape(rows, cols), recv.reshape(3, rows, cols))
    return out.reshape(shape)


def _pad8(v, width, lane0=0):
    v = v.reshape(1, -1) if v.ndim == 1 else v
    return jnp.zeros((8, width), F32).at[:v.shape[0], lane0:lane0 + v.shape[1]].set(v.astype(F32))


def _relayout_w_in(w):
    z = lambda n: jnp.zeros(w.shape[:-1] + (n,), w.dtype)
    return jnp.concatenate([w[..., 0:2048], w[..., 2056:4616], w[..., 4632:6680],
                            w[..., 2048:2056], z(120), w[..., 4616:4632], z(112)], axis=-1)


def _unlayout_dw_in(dg, ds, dr, dgs, dss):
    return jnp.concatenate([dg, dgs[:, 0:8], ds, dss[:, 0:16], dr], axis=1)


TB = 256
TL = 256
TK = 1024


def kernel(x, pre_norm, post_norm, w_in, gdn_conv, gdn_A_log, gdn_dt_bias, gdn_norm, ssd_conv, ssd_conv_b, ssd_A_log, ssd_dt_bias, ssd_D, ssd_norm, ret_norm, w_out, loss_target, m_pre_norm, m_post_norm, m_w_in, m_gdn_conv, m_gdn_A_log, m_gdn_dt_bias, m_gdn_norm, m_ssd_conv, m_ssd_conv_b, m_ssd_A_log, m_ssd_dt_bias, m_ssd_D, m_ssd_norm, m_ret_norm, m_w_out, v_pre_norm, v_post_norm, v_w_in, v_gdn_conv, v_gdn_A_log, v_gdn_dt_bias, v_gdn_norm, v_ssd_conv, v_ssd_conv_b, v_ssd_A_log, v_ssd_dt_bias, v_ssd_D, v_ssd_norm, v_ret_norm, v_w_out):
    seq = x.shape[1]
    chip = 2 * lax.axis_index("x") + lax.axis_index("y")
    x0 = x[0]

    wi_g, wo_g = _ag_rows([w_in.astype(BF16), w_out.astype(BF16)], "ag_weights")
    gcv_g, scv_g = _ag_chips([gdn_conv, ssd_conv], "ag_conv")
    wp = _relayout_w_in(jnp.transpose(wi_g, (1, 2, 0, 3)).reshape(DEPTH, D_MODEL, N_IN))
    wo = jnp.transpose(wo_g, (1, 0, 2, 3)).reshape(DEPTH, 2048, D_MODEL)
    gcv = jnp.transpose(gcv_g, (1, 2, 0, 3)).reshape(DEPTH, CONV_W, 1536)
    scv = jnp.transpose(scv_g, (1, 2, 0, 3)).reshape(DEPTH, CONV_W, 1536)
    rope_c, rope_s = _rope_tables(seq)

    saved = []
    xc = x0
    for l in range(DEPTH):
        p = dict(
            pn=_pad8(pre_norm[l], D_MODEL), qn=_pad8(post_norm[l], D_MODEL),
            g_cw=_pad8(gcv[l], 1536), g_prm=_pad8(jnp.stack([gdn_A_log[l], gdn_dt_bias[l]]), 128, 4),
            g_nw=_pad8(gdn_norm[l], 128),
            s_cw=_pad8(scv[l], 1536), s_cb=_pad8(ssd_conv_b[l], 1536),
            s_prm=_pad8(jnp.stack([ssd_A_log[l], ssd_dt_bias[l], ssd_D[l]]), 128), s_nw=_pad8(ssd_norm[l], SSD_W),
            r_nw=_pad8(ret_norm[l], 128))
        pg, ps, pr, gs, ss, ht = _make_inproj(seq, TL)(xc, p["pn"], wp[l])
        oa, stg, tig, uwg = _make_gdn_fwd(seq, TB)(pg, gs, p["g_cw"], p["g_prm"], p["g_nw"])
        ob, sts = _make_ssd_fwd(seq, TB)(ps, ss, p["s_cw"], p["s_cb"], p["s_prm"], p["s_nw"])
        oc, str_ = _make_ret_fwd(seq, TB)(pr, rope_c, rope_s, p["r_nw"])
        out, xn = _make_outproj(seq, TL)(oa, ob, oc, wo[l], xc, p["qn"])
        saved.append(dict(p=p, x=xc, ht=ht, pg=pg, ps=ps, pr=pr, gs=gs, ss=ss, stg=stg, tig=tig, uwg=uwg, sts=sts, str=str_,
                          oa=oa, ob=ob, oc=oc, out=out))
        xc = xn

    dxn, lossp = _make_loss_head(seq, TL)(xc, loss_target[0])

    small = [None] * DEPTH
    dwi = [None] * DEPTH
    dwo = [None] * DEPTH
    for l in reversed(range(DEPTH)):
        s = saved[l]
        p = s["p"]
        doa, dob, doc, dqn, dwo[l] = _make_outproj_bwd(seq, TL)(dxn, s["out"], s["oa"], s["ob"], s["oc"], wo[l], p["qn"])
        dpg, dgs, dcw_g, dprm_g, dnw_g = _make_gdn_bwd(seq, TB)(s["pg"], s["gs"], p["g_cw"], p["g_prm"], p["g_nw"],
                                                                s["stg"], s["tig"], s["uwg"], doa)
        dps, dss, dcw_s, dcb_s, dprm_s, dnw_s = _make_ssd_bwd(seq, TB)(s["ps"], s["ss"], p["s_cw"], p["s_cb"],
                                                                       p["s_prm"], p["s_nw"], s["sts"], dob)
        dpr, dnw_r = _make_ret_bwd(seq, TB)(s["pr"], rope_c, rope_s, p["r_nw"], s["str"], doc)
        dx, dpn = _make_inproj_bwd_dx(seq, TL)(dpg, dps, dpr, dgs, dss, wp[l], s["x"], p["pn"], dxn)
        dws = [_make_inproj_bwd_dw(seq, TK, d.shape[1], tn, f"inproj_bwd_dw{i}")(s["ht"], d)
               for i, (d, tn) in enumerate(((dpg, 1024), (dps, 1280), (dpr, 1024),
                                            (jnp.concatenate([dgs, dss], axis=1), 256)))]
        dwi[l] = _unlayout_dw_in(dws[0], dws[1], dws[2], dws[3][:, 0:128], dws[3][:, 128:256])
        small[l] = [dpn[0], dqn[0], dcw_g[0:4].reshape(-1), dprm_g[0, 4:8], dprm_g[1, 4:8], dnw_g[0],
                    dcw_s[0:4].reshape(-1), dcb_s[0], dprm_s[0, 0:16], dprm_s[1, 0:16], dprm_s[2, 0:16],
                    dnw_s[0], dnw_r[0]]
        dxn = dx
    grad_x = dxn[None]

    sizes = [a.shape[0] for a in small[0]]
    flat = jnp.concatenate(small[0] + small[1] + [lossp[0, 0:1]])
    n_flat = flat.shape[0]
    rows = -(-n_flat // 1024) * 8
    red = _allreduce_small(jnp.pad(flat, (0, rows * 128 - n_flat)).reshape(rows, 128), "allreduce_small").reshape(-1)
    per = sum(sizes)
    loss = red[2 * per]

    def pick(i):
        off = sum(sizes[:i])
        return jnp.stack([red[l * per + off:l * per + off + sizes[i]] for l in range(DEPTH)])

    g_small = dict(
        pre_norm=pick(0), post_norm=pick(1),
        gdn_conv=lax.dynamic_slice_in_dim(pick(2).reshape(DEPTH, CONV_W, 1536), chip * 384, 384, axis=2),
        gdn_A_log=pick(3), gdn_dt_bias=pick(4), gdn_norm=pick(5),
        ssd_conv=lax.dynamic_slice_in_dim(pick(6).reshape(DEPTH, CONV_W, 1536), chip * 384, 384, axis=2),
        ssd_conv_b=pick(7), ssd_A_log=pick(8), ssd_dt_bias=pick(9), ssd_D=pick(10), ssd_norm=pick(11),
        ret_norm=pick(12))

    gin = jnp.transpose(jnp.stack(dwi).reshape(DEPTH, D_MODEL, 4, N_IN // 4), (2, 0, 1, 3))
    gout = jnp.transpose(jnp.stack(dwo).reshape(DEPTH, 4, 512, D_MODEL), (1, 0, 2, 3))
    core1 = lax.axis_index("c").astype(jnp.int32).reshape(1)
    chip1 = chip.astype(jnp.int32).reshape(1)
    sib_in, sib_out = _rs_to_sibling([gin, gout], "rs_to_sibling")
    h_in, hb_in = _add_halves(gin, sib_in, core1, "add_halves_w_in")
    h_out, hb_out = _add_halves(gout, sib_out, core1, "add_halves_w_out")
    q_in, q_out = _rs_chips([hb_in, hb_out], "rs_grads")
    g_in, g_out = _join_halves([_sum_chips(h_in, q_in, chip1, "sum_chips_w_in"),
                                _sum_chips(h_out, q_out, chip1, "sum_chips_w_out")], "join_halves")

    weights = dict(pre_norm=pre_norm, post_norm=post_norm, w_in=w_in, gdn_conv=gdn_conv, gdn_A_log=gdn_A_log,
                   gdn_dt_bias=gdn_dt_bias, gdn_norm=gdn_norm, ssd_conv=ssd_conv, ssd_conv_b=ssd_conv_b,
                   ssd_A_log=ssd_A_log, ssd_dt_bias=ssd_dt_bias, ssd_D=ssd_D, ssd_norm=ssd_norm, ret_norm=ret_norm,
                   w_out=w_out)
    ms = dict(pre_norm=m_pre_norm, post_norm=m_post_norm, w_in=m_w_in, gdn_conv=m_gdn_conv, gdn_A_log=m_gdn_A_log,
              gdn_dt_bias=m_gdn_dt_bias, gdn_norm=m_gdn_norm, ssd_conv=m_ssd_conv, ssd_conv_b=m_ssd_conv_b,
              ssd_A_log=m_ssd_A_log, ssd_dt_bias=m_ssd_dt_bias, ssd_D=m_ssd_D, ssd_norm=m_ssd_norm,
              ret_norm=m_ret_norm, w_out=m_w_out)
    vs = dict(pre_norm=v_pre_norm, post_norm=v_post_norm, w_in=v_w_in, gdn_conv=v_gdn_conv, gdn_A_log=v_gdn_A_log,
              gdn_dt_bias=v_gdn_dt_bias, gdn_norm=v_gdn_norm, ssd_conv=v_ssd_conv, ssd_conv_b=v_ssd_conv_b,
              ssd_A_log=v_ssd_A_log, ssd_dt_bias=v_ssd_dt_bias, ssd_D=v_ssd_D, ssd_norm=v_ssd_norm,
              ret_norm=v_ret_norm, w_out=v_w_out)
    names = list(weights)
    res = {}
    for nme in names:
        if nme == "w_in":
            res[nme] = _adamw(w_in, g_in, m_w_in, v_w_in, "adamw_w_in")
        elif nme == "w_out":
            res[nme] = _adamw(w_out, g_out, m_w_out, v_w_out, "adamw_w_out")
        else:
            res[nme] = _adamw(weights[nme], g_small[nme], ms[nme], vs[nme], "adamw_" + nme)
    return (loss, grad_x, *[res[n][0] for n in names], *[res[n][1] for n in names],
            *[res[n][2] for n in names], *[res[n][3] for n in names])
```

```python
import functools
import math

import jax
import jax.numpy as jnp
from jax import lax
from jax.experimental import pallas as pl
from jax.experimental.pallas import tpu as pltpu

F32 = jnp.float32
BF16 = jnp.bfloat16
HI = lax.Precision.HIGHEST

D_MODEL = 1024
DEPTH = 2
CH = 64
CONV_W = 4
EPS = 1e-6
GDN_H, GDN_D = 4, 128
SSD_H, SSD_P, SSD_N, SSD_G = 16, 64, 128, 2
SSD_W = SSD_H * SSD_P
RET_H, RET_D = 4, 128
ROPE_BASE = 10000.0
N_IN = 6680
NEG = -1e30

VMEM_LIMIT = 56 * 1024 * 1024


def _dot(a, b):
    return jnp.dot(a.astype(BF16), b.astype(BF16), preferred_element_type=F32)


def _dot_nt(a, b):
    return lax.dot_general(a.astype(BF16), b.astype(BF16), (((1,), (1,)), ((), ())), preferred_element_type=F32)


def _dot_tn(a, b):
    return lax.dot_general(a.astype(BF16), b.astype(BF16), (((0,), (0,)), ((), ())), preferred_element_type=F32)


def _split(a):
    hi = a.astype(BF16)
    return hi, (a - hi.astype(F32)).astype(BF16)


def _dot01l(m, v):
    vh, vl = _split(v)
    mb = m.astype(BF16)
    return jnp.dot(mb, vh, preferred_element_type=F32) + jnp.dot(mb, vl, preferred_element_type=F32)


def _dot01r(v, m):
    vh, vl = _split(v)
    mb = m.astype(BF16)
    return jnp.dot(vh, mb, preferred_element_type=F32) + jnp.dot(vl, mb, preferred_element_type=F32)


def _sigmoid(x):
    return jax.nn.sigmoid(x)


def _silu(x):
    return x * _sigmoid(x)


def _dsilu(x):
    s = _sigmoid(x)
    return s * (1.0 + x * (1.0 - s))


def _softplus(x):
    return jnp.maximum(x, 0.0) + jnp.log1p(jnp.exp(-jnp.abs(x)))


def _iota2(shape, dim):
    return lax.broadcasted_iota(jnp.int32, shape, dim)


def _chunk_tri(tb, upper=False):
    r = _iota2((tb, tb), 0)
    c = _iota2((tb, tb), 1)
    same = jnp.right_shift(r, 6) == jnp.right_shift(c, 6)
    return (same & ((c >= r) if upper else (c <= r))).astype(F32)


def _masks():
    r = _iota2((CH, CH), 0)
    c = _iota2((CH, CH), 1)
    return r >= c, r > c, (r == c).astype(F32)


def _put_lane(col, lane_idx, width=128):
    lane = _iota2((col.shape[0], width), 1)
    return jnp.where(lane == lane_idx, col, 0.0)


def _conv_taps(raw, halo8, tb):
    ext = jnp.concatenate([halo8, raw], axis=0)
    return [raw] + [pltpu.roll(ext, s, axis=0)[8:] for s in (1, 2, 3)]


def _conv_back(dpre, nxt8, tb):
    ext = jnp.concatenate([dpre, nxt8], axis=0)
    return [dpre] + [pltpu.roll(ext, tb + 8 - s, axis=0)[:tb] for s in (1, 2, 3)]


def _rms_fwd(o, w, n):
    r = lax.rsqrt(jnp.sum(o * o, axis=-1, keepdims=True) * (1.0 / n) + EPS)
    on = o * r
    return on, r, on * w


def _rms_bwd(dy, on, r, w, n):
    don = dy * w
    return r * (don - on * (jnp.sum(don * on, axis=-1, keepdims=True) * (1.0 / n))), dy * on


def _put_cols(v, g, gw):
    z = jnp.zeros_like(v)
    return jnp.concatenate([v, z] if g == 0 else [z, v], axis=1)


def _gdn_common(pg_ref, halo8, sm, cw, prm, tb):
    raw = pg_ref[:, 0:1536]
    taps = _conv_taps(raw, halo8, tb)
    pre = taps[0] * cw[3:4, :] + taps[1] * cw[2:3, :] + taps[2] * cw[1:2, :] + taps[3] * cw[0:1, :]
    act = _silu(pre)
    beta = _sigmoid(sm)
    sp_in = sm + prm[1:2, :]
    g = -jnp.exp(prm[0:1, :]) * _softplus(sp_in)
    gc = _dot01l(_chunk_tri(tb), g)
    return raw, taps, pre, act, beta, sp_in, g, gc


_NN = (((2,), (1,)), ((0,), (0,)))
_NT = (((2,), (2,)), ((0,), (0,)))
_TN = (((1,), (1,)), ((0,), (0,)))


def _bdot(a, b, dn):
    return lax.dot_general(a.astype(BF16), b.astype(BF16), dn, preferred_element_type=F32)


def _dot3_parts(ah, al, bh, bl, dn):
    f = lambda p, q: lax.dot_general(p, q, dn, preferred_element_type=F32)
    return f(ah, bh) + (f(ah, bl) + f(al, bh))


def _bdot3(a, b, dn):
    ah, al = _split(a)
    bh, bl = _split(b)
    return _dot3_parts(ah, al, bh, bl, dn)


def _binv_unit_lower(a, eye):
    x = eye - a
    ph, pl_ = _split(a)
    for _ in range(5):
        ph, pl_ = _split(_dot3_parts(ph, pl_, ph, pl_, _NN))
        xh, xl = _split(x)
        x = x + _dot3_parts(xh, xl, ph, pl_, _NN)
    return x


def _rsum(v):
    return jnp.sum(v, axis=-1, keepdims=True)


def _gdn_batch(act, beta, gc, gct, eg_all, ncb, masks):
    causal, strict, _ = masks

    def st(fn):
        return jnp.stack([fn(c, h, slice(c * CH, (c + 1) * CH)) for c in range(ncb) for h in range(GDN_H)])

    qr = st(lambda c, h, r: act[r, h * 128:(h + 1) * 128])
    kr = st(lambda c, h, r: act[r, 512 + h * 128:512 + (h + 1) * 128])
    vh = st(lambda c, h, r: act[r, 1024 + h * 128:1024 + (h + 1) * 128])
    bh = st(lambda c, h, r: beta[r, h:h + 1])
    gcol = st(lambda c, h, r: gc[r, 4 + h:5 + h])
    grow = st(lambda c, h, r: gct[4 + h:5 + h, r])
    eg = st(lambda c, h, r: eg_all[r, 4 + h:5 + h])
    glast = st(lambda c, h, r: gc[(c + 1) * CH - 1:(c + 1) * CH, 4 + h:5 + h])
    rq = lax.rsqrt(_rsum(qr * qr) + EPS)
    rk = lax.rsqrt(_rsum(kr * kr) + EPS)
    qn = qr * rq
    kh = kr * rk
    qh = qn * (GDN_D ** -0.5)
    decay = jnp.exp(jnp.where(causal, gcol - grow, NEG))
    kb = kh * bh
    kd_scale = jnp.exp(glast - gcol)
    return dict(qn=qn, rq=rq, kh=kh, rk=rk, qh=qh, vh=vh, bh=bh, eg=eg, decay=decay, kb=kb, vb=vh * bh, kg=kb * eg,
                qg=qh * eg, kd_scale=kd_scale, kdec=kh * kd_scale, egl=jnp.exp(glast),
                a=jnp.where(strict, _bdot(kb, kh, _NT) * decay, 0.0), attn=_bdot(qh, kh, _NT) * decay)


def _make_gdn_fwd(seq, tb):
    ncb = tb // CH
    nb = seq // tb
    n = ncb * GDN_H

    def body(pg_ref, sm_ref, cw_ref, prm_ref, nw_ref, oa_ref, st_ref, ti_ref, uw_ref, s_scr, halo_scr):
        @pl.when(pl.program_id(0) == 0)
        def _():
            s_scr[...] = jnp.zeros_like(s_scr)
            halo_scr[...] = jnp.zeros_like(halo_scr)

        masks = _masks()
        sm = sm_ref[...]
        raw, _, _, act, beta, _, _, gc = _gdn_common(pg_ref, halo_scr[...], sm, cw_ref[...], prm_ref[...], tb)
        halo_scr[...] = raw[tb - 8:tb, :]
        d = _gdn_batch(act, beta, gc, gc.T, jnp.exp(gc), ncb, masks)
        t = _binv_unit_lower(d["a"], masks[2])
        sol = _bdot3(t, jnp.concatenate([d["vb"], d["kg"]], axis=2), _NN)
        ti_ref[...] = t.reshape(ncb, GDN_H, CH, CH)
        uw_ref[...] = sol.reshape(ncb, GDN_H, CH, 256)
        u, w = sol[:, :, :128], sol[:, :, 128:]
        vns = []
        for c in range(ncb):
            bs = slice(c * GDN_H, (c + 1) * GDN_H)
            s = s_scr[...]
            st_ref[c] = s
            vn = u[bs] - _bdot(w[bs], s, _NN)
            s_scr[...] = s * d["egl"][bs] + _bdot(d["kdec"][bs], vn, _TN)
            vns.append(vn)
        v_new = jnp.concatenate(vns, axis=0)
        s_prev = st_ref[...].reshape(n, 128, 128)
        o = _bdot(d["qg"], s_prev, _NN) + _bdot(d["attn"], v_new, _NN)
        _, _, y = _rms_fwd(o, nw_ref[0:1, :], GDN_D)
        for c in range(ncb):
            rows = slice(c * CH, (c + 1) * CH)
            for h in range(GDN_H):
                z = pg_ref[rows, 1536 + h * 128:1536 + (h + 1) * 128]
                oa_ref[rows, h * 128:(h + 1) * 128] = (y[c * GDN_H + h] * _silu(z)).astype(oa_ref.dtype)

    def call(pg, sm, cw, prm, nw):
        blk4 = lambda i: (i, 0, 0, 0)
        return pl.pallas_call(
            body,
            grid=(nb,),
            in_specs=[
                pl.BlockSpec((tb, 2048), lambda i: (i, 0)),
                pl.BlockSpec((tb, 128), lambda i: (i, 0)),
                pl.BlockSpec((8, 1536), lambda i: (0, 0)),
                pl.BlockSpec((8, 128), lambda i: (0, 0)),
                pl.BlockSpec((8, 128), lambda i: (0, 0)),
            ],
            out_specs=[
                pl.BlockSpec((tb, 512), lambda i: (i, 0)),
                pl.BlockSpec((ncb, GDN_H, 128, 128), blk4),
                pl.BlockSpec((ncb, GDN_H, CH, CH), blk4),
                pl.BlockSpec((ncb, GDN_H, CH, 256), blk4),
            ],
            out_shape=[
                jax.ShapeDtypeStruct((seq, 512), BF16),
                jax.ShapeDtypeStruct((seq // CH, GDN_H, 128, 128), F32),
                jax.ShapeDtypeStruct((seq // CH, GDN_H, CH, CH), F32),
                jax.ShapeDtypeStruct((seq // CH, GDN_H, CH, 256), F32),
            ],
            scratch_shapes=[pltpu.VMEM((GDN_H, 128, 128), F32), pltpu.VMEM((8, 1536), F32)],
            compiler_params=pltpu.CompilerParams(dimension_semantics=("arbitrary",), vmem_limit_bytes=VMEM_LIMIT),
            name="gdn_fwd",
        )(pg, sm, cw, prm, nw)

    return call


def _make_gdn_bwd(seq, tb):
    ncb = tb // CH
    nb = seq // tb
    hb = tb // 8
    n = ncb * GDN_H

    def body(pg_ref, prev_ref, sm_ref, cw_ref, prm_ref, nw_ref, st_ref, ti_ref, uw_ref, doa_ref,
             dpg_ref, dsm_ref, dcw_ref, dprm_ref, dnw_ref, ds_scr, nxt_scr):
        i = pl.program_id(0)

        @pl.when(i == 0)
        def _():
            ds_scr[...] = jnp.zeros_like(ds_scr)
            nxt_scr[...] = jnp.zeros_like(nxt_scr)
            dcw_ref[...] = jnp.zeros_like(dcw_ref)
            dprm_ref[...] = jnp.zeros_like(dprm_ref)
            dnw_ref[...] = jnp.zeros_like(dnw_ref)

        masks = _masks()
        strict = masks[1]
        sm = sm_ref[...]
        cw = cw_ref[...]
        prm = prm_ref[...]
        halo8 = jnp.where(i == nb - 1, 0.0, prev_ref[...])
        raw, taps, pre, act, beta, sp_in, g, gc = _gdn_common(pg_ref, halo8, sm, cw, prm, tb)
        nw = nw_ref[0:1, :]
        row_id = _iota2((CH, 1), 0)
        d = _gdn_batch(act, beta, gc, gc.T, jnp.exp(gc), ncb, masks)
        t = ti_ref[...].reshape(n, CH, CH)
        sol = uw_ref[...].reshape(n, CH, 256)
        u, w = sol[:, :, :128], sol[:, :, 128:]
        s_prev = st_ref[...].reshape(n, 128, 128)
        v_new = u - _bdot(w, s_prev, _NN)
        o = _bdot(d["qg"], s_prev, _NN) + _bdot(d["attn"], v_new, _NN)

        pairs = [(c, h) for c in range(ncb) for h in range(GDN_H)]
        z = jnp.stack([pg_ref[c * CH:(c + 1) * CH, 1536 + h * 128:1536 + (h + 1) * 128] for c, h in pairs])
        doa = jnp.stack([doa_ref[c * CH:(c + 1) * CH, h * 128:(h + 1) * 128] for c, h in pairs])
        on, r, y = _rms_fwd(o, nw, GDN_D)
        dz = doa * y * _dsilu(z)
        do, dnw_rows = _rms_bwd(doa * _silu(z), on, r, nw, GDN_D)
        dnw_acc = jnp.sum(jnp.sum(dnw_rows, axis=0), axis=0, keepdims=True)

        dvn_in = _bdot(d["attn"], do, _TN)
        qgtdo = _bdot(d["qg"], do, _TN)
        dvn_l, dkdec_l, dgl_l = [None] * ncb, [None] * ncb, [None] * ncb
        for c in reversed(range(ncb)):
            bs = slice(c * GDN_H, (c + 1) * GDN_H)
            dsn = ds_scr[...]
            dvn_c = dvn_in[bs] + _bdot(d["kdec"][bs], dsn, _NN)
            ds_scr[...] = d["egl"][bs] * dsn + qgtdo[bs] - _bdot(w[bs], dvn_c, _TN)
            dvn_l[c] = dvn_c
            dkdec_l[c] = _bdot(v_new[bs], dsn, _NT)
            dgl_l[c] = d["egl"][bs] * jnp.sum(_rsum(s_prev[bs] * dsn), axis=1, keepdims=True)
        dvn = jnp.concatenate(dvn_l, axis=0)
        dkdec = jnp.concatenate(dkdec_l, axis=0)
        dglast = jnp.concatenate(dgl_l, axis=0)

        dqg = _bdot(do, s_prev, _NT)
        dattn = _bdot(do, v_new, _NT)
        dw = -_bdot(dvn, s_prev, _NT)
        drhs = _bdot3(t, jnp.concatenate([dvn, dw], axis=2), _TN)
        dvb, dkg = drhs[:, :, :128], drhs[:, :, 128:]
        da = jnp.where(strict, -(_bdot(dvb, u, _NT) + _bdot(dkg, w, _NT)), 0.0)
        dp = da * d["decay"]
        dq_m = dattn * d["decay"]
        m = da * d["a"] + dattn * d["attn"]
        upper_tri = jnp.broadcast_to((_iota2((CH, CH), 1) >= _iota2((CH, CH), 0)).astype(BF16), (n, CH, CH))
        dg_in = _rsum(jnp.where(strict, _bdot(upper_tri, m, _NN), 0.0))
        dkb = _bdot(dp, d["kh"], _NN) + dkg * d["eg"]
        kdk_row = _rsum(dkdec * d["kdec"])
        dk = _bdot(dp, d["kb"], _TN) + _bdot(dq_m, d["qh"], _TN) + dkdec * d["kd_scale"] + dkb * d["bh"]
        dq = _bdot(dq_m, d["kh"], _NN) + dqg * d["eg"]
        dglast = dglast + jnp.sum(kdk_row, axis=1, keepdims=True)
        dgcol = (_rsum(dqg * d["qg"]) + _rsum(dkg * d["kg"]) - kdk_row + jnp.where(row_id == CH - 1, dglast, 0.0))
        dbeta = _rsum(dkb * d["kh"]) + _rsum(dvb * d["vh"])
        dn = dq * (GDN_D ** -0.5)
        dact_q = d["rq"] * (dn - d["qn"] * _rsum(dn * d["qn"]))
        dact_k = d["rk"] * (dk - d["kh"] * _rsum(dk * d["kh"]))
        dact_v = dvb * d["bh"]

        def lanes(v, lane0):
            return jnp.concatenate(
                [sum(_put_lane(v[c * GDN_H + h], lane0 + h) for h in range(GDN_H)) for c in range(ncb)], axis=0)

        def tokens(v):
            return jnp.concatenate(
                [jnp.concatenate([v[c * GDN_H + h] for h in range(GDN_H)], axis=1) for c in range(ncb)], axis=0)

        dbeta_all = lanes(dbeta, 0)
        dg = _dot01l(_chunk_tri(tb, upper=True), lanes(dgcol, 4)) + lanes(dg_in, 4)
        neg_ea = -jnp.exp(prm[0:1, :])
        da_raw = dg * neg_ea * _sigmoid(sp_in)
        db_raw = dbeta_all * beta * (1.0 - beta)
        dsm_ref[...] = (da_raw + db_raw).astype(dsm_ref.dtype)
        lane8 = _iota2((8, 128), 1)
        sub8 = _iota2((8, 128), 0)
        dalog = jnp.sum(dg * g, axis=0, keepdims=True)
        ddtb = jnp.sum(da_raw, axis=0, keepdims=True)
        dprm_ref[...] += jnp.where(sub8 == 0, dalog, 0.0) + jnp.where(sub8 == 1, ddtb, 0.0)
        dnw_ref[...] += jnp.where(sub8 == 0, dnw_acc, 0.0)

        dact = jnp.concatenate([tokens(dact_q), tokens(dact_k), tokens(dact_v)], axis=1)
        dpre = dact * _dsilu(pre)
        back = _conv_back(dpre, nxt_scr[...], tb)
        nxt_scr[...] = dpre[0:8, :]
        draw = back[0] * cw[3:4, :] + back[1] * cw[2:3, :] + back[2] * cw[1:2, :] + back[3] * cw[0:1, :]
        dpg_ref[:, 0:1536] = draw.astype(dpg_ref.dtype)
        dpg_ref[:, 1536:2048] = tokens(dz).astype(dpg_ref.dtype)
        sub_c = _iota2((8, 1536), 0)
        dcw_new = jnp.zeros((8, 1536), F32)
        for s_ in range(CONV_W):
            dcw_new = dcw_new + jnp.where(sub_c == 3 - s_, jnp.sum(dpre * taps[s_], axis=0, keepdims=True), 0.0)
        dcw_ref[...] += dcw_new

    def call(pg, sm, cw, prm, nw, st, ti, uw, doa):
        rev = lambda i: (nb - 1 - i, 0)
        const = lambda i: (0, 0)
        return pl.pallas_call(
            body,
            grid=(nb,),
            in_specs=[
                pl.BlockSpec((tb, 2048), rev),
                pl.BlockSpec((8, 1536), lambda i: (jnp.maximum((nb - 1 - i) * hb - 1, 0), 0)),
                pl.BlockSpec((tb, 128), rev),
                pl.BlockSpec((8, 1536), const),
                pl.BlockSpec((8, 128), const),
                pl.BlockSpec((8, 128), const),
                pl.BlockSpec((ncb, GDN_H, 128, 128), lambda i: (nb - 1 - i, 0, 0, 0)),
                pl.BlockSpec((ncb, GDN_H, CH, CH), lambda i: (nb - 1 - i, 0, 0, 0)),
                pl.BlockSpec((ncb, GDN_H, CH, 256), lambda i: (nb - 1 - i, 0, 0, 0)),
                pl.BlockSpec((tb, 512), rev),
            ],
            out_specs=[
                pl.BlockSpec((tb, 2048), rev),
                pl.BlockSpec((tb, 128), rev),
                pl.BlockSpec((8, 1536), const),
                pl.BlockSpec((8, 128), const),
                pl.BlockSpec((8, 128), const),
            ],
            out_shape=[
                jax.ShapeDtypeStruct((seq, 2048), BF16),
                jax.ShapeDtypeStruct((seq, 128), BF16),
                jax.ShapeDtypeStruct((8, 1536), F32),
                jax.ShapeDtypeStruct((8, 128), F32),
                jax.ShapeDtypeStruct((8, 128), F32),
            ],
            scratch_shapes=[pltpu.VMEM((GDN_H, 128, 128), F32), pltpu.VMEM((8, 1536), F32)],
            compiler_params=pltpu.CompilerParams(dimension_semantics=("arbitrary",), vmem_limit_bytes=VMEM_LIMIT),
            name="gdn_bwd",
        )(pg, pg, sm, cw, prm, nw, st, ti, uw, doa)

    return call


def _expand_mat():
    r = _iota2((128, SSD_W), 0)
    c = _iota2((128, SSD_W), 1)
    return (jnp.right_shift(c, 6) == r).astype(F32)


def _reduce_heads(v, e):
    vh, vl = _split(v)
    eb = e.astype(BF16)
    nt = (((1,), (1,)), ((), ()))
    return (lax.dot_general(vh, eb, nt, preferred_element_type=F32)
            + lax.dot_general(vl, eb, nt, preferred_element_type=F32))


def _row8(v):
    return jnp.broadcast_to(v, (8, v.shape[1]))


def _ssd_common(ps_ref, halo8, ss, cw, cb, prm, tb):
    raw = ps_ref[:, 0:1536]
    taps = _conv_taps(raw, halo8, tb)
    pre = taps[0] * cw[3:4, :] + taps[1] * cw[2:3, :] + taps[2] * cw[1:2, :] + taps[3] * cw[0:1, :] + cb[0:1, :]
    act = _silu(pre)
    dt_in = ss + prm[1:2, :]
    dt = _softplus(dt_in)
    a = dt * (-jnp.exp(prm[0:1, :]))
    acum = _dot01l(_chunk_tri(tb), a)
    e = _expand_mat()
    dt_e = _dot01r(dt, e)
    xdt = act[:, 0:SSD_W] * dt_e
    ea_e = _dot01r(jnp.exp(acum), e)
    d_e = _dot01r(_row8(prm[2:3, :]), e)[0:1, :]
    return raw, taps, pre, act, dt_in, dt, a, acum, e, dt_e, xdt, ea_e, d_e


def _ssd_chunk(act, acum, act_t, e, c):
    r0 = c * CH
    rows = slice(r0, r0 + CH)
    alast = acum[r0 + CH - 1:r0 + CH, :]
    wdec = jnp.exp(alast - acum[rows, :])
    wd_e = _dot01r(wdec, e)
    eal_e = _dot01r(_row8(jnp.exp(alast)), e)[0:1, :]
    return rows, wd_e, eal_e


def _ssd_lmat(acum, act_t, c, h, causal):
    r0 = c * CH
    acol = acum[r0:r0 + CH, h:h + 1]
    arow = act_t[h:h + 1, r0:r0 + CH]
    return jnp.exp(jnp.where(causal, acol - arow, NEG))


def _make_ssd_fwd(seq, tb):
    ncb = tb // CH
    nb = seq // tb
    hg = SSD_H // SSD_G
    gw = SSD_W // SSD_G

    def body(ps_ref, ss_ref, cw_ref, cb_ref, prm_ref, nw_ref, ob_ref, st_ref, hs_scr, halo_scr):
        @pl.when(pl.program_id(0) == 0)
        def _():
            hs_scr[...] = jnp.zeros_like(hs_scr)
            halo_scr[...] = jnp.zeros_like(halo_scr)

        causal, _, _ = _masks()
        (raw, _, _, act, _, _, _, acum, e, _, xdt, ea_e, d_e) = _ssd_common(
            ps_ref, halo_scr[...], ss_ref[...], cw_ref[...], cb_ref[...], prm_ref[...], tb)
        halo_scr[...] = raw[tb - 8:tb, :]
        act_t = acum.T
        nw = nw_ref[0:1, :]
        for c in range(ncb):
            rows, wd_e, eal_e = _ssd_chunk(act, acum, act_t, e, c)
            st_ref[c] = hs_scr[...]
            ys = []
            for g in range(SSD_G):
                gc_ = slice(g * gw, (g + 1) * gw)
                bg = act[rows, SSD_W + g * 128:SSD_W + (g + 1) * 128]
                cg = act[rows, SSD_W + 256 + g * 128:SSD_W + 256 + (g + 1) * 128]
                cbm = _dot_nt(cg, bg)
                hs = hs_scr[:, gc_]
                yin = _dot(cg, hs)
                yh = []
                for hh in range(hg):
                    h = g * hg + hh
                    lm = _ssd_lmat(acum, act_t, c, h, causal)
                    yh.append(_dot(cbm * lm, xdt[rows, h * SSD_P:(h + 1) * SSD_P]))
                ys.append(jnp.concatenate(yh, axis=1) + yin * ea_e[rows, gc_])
                hs_scr[:, gc_] = hs * eal_e[:, gc_] + _dot_tn(bg, xdt[rows, gc_] * wd_e[:, gc_])
            y = jnp.concatenate(ys, axis=1) + act[rows, 0:SSD_W] * d_e
            yz = y * _silu(ps_ref[rows, 1536:2560])
            outs = [_rms_fwd(yz[:, g * gw:(g + 1) * gw], nw[:, g * gw:(g + 1) * gw], gw)[2] for g in range(SSD_G)]
            ob_ref[rows, :] = jnp.concatenate(outs, axis=1).astype(ob_ref.dtype)

    def call(ps, ss, cw, cb, prm, nw):
        const = lambda i: (0, 0)
        return pl.pallas_call(
            body,
            grid=(nb,),
            in_specs=[
                pl.BlockSpec((tb, 2560), lambda i: (i, 0)),
                pl.BlockSpec((tb, 128), lambda i: (i, 0)),
                pl.BlockSpec((8, 1536), const),
                pl.BlockSpec((8, 1536), const),
                pl.BlockSpec((8, 128), const),
                pl.BlockSpec((8, SSD_W), const),
            ],
            out_specs=[
                pl.BlockSpec((tb, SSD_W), lambda i: (i, 0)),
                pl.BlockSpec((ncb, SSD_N, SSD_W), lambda i: (i, 0, 0)),
            ],
            out_shape=[
                jax.ShapeDtypeStruct((seq, SSD_W), BF16),
                jax.ShapeDtypeStruct((seq // CH, SSD_N, SSD_W), F32),
            ],
            scratch_shapes=[pltpu.VMEM((SSD_N, SSD_W), F32), pltpu.VMEM((8, 1536), F32)],
            compiler_params=pltpu.CompilerParams(dimension_semantics=("arbitrary",), vmem_limit_bytes=VMEM_LIMIT),
            name="ssd_fwd",
        )(ps, ss, cw, cb, prm, nw)

    return call


def _make_ssd_bwd(seq, tb):
    ncb = tb // CH
    nb = seq // tb
    hb = tb // 8
    hg = SSD_H // SSD_G
    gw = SSD_W // SSD_G

    def body(ps_ref, prev_ref, ss_ref, cw_ref, cb_ref, prm_ref, nw_ref, st_ref, dob_ref,
             dps_ref, dss_ref, dcw_ref, dcb_ref, dprm_ref, dnw_ref, dhs_scr, nxt_scr):
        i = pl.program_id(0)

        @pl.when(i == 0)
        def _():
            dhs_scr[...] = jnp.zeros_like(dhs_scr)
            nxt_scr[...] = jnp.zeros_like(nxt_scr)
            dcw_ref[...] = jnp.zeros_like(dcw_ref)
            dcb_ref[...] = jnp.zeros_like(dcb_ref)
            dprm_ref[...] = jnp.zeros_like(dprm_ref)
            dnw_ref[...] = jnp.zeros_like(dnw_ref)

        causal, _, _ = _masks()
        cw = cw_ref[...]
        prm = prm_ref[...]
        halo8 = jnp.where(i == nb - 1, 0.0, prev_ref[...])
        (raw, taps, pre, act, dt_in, dt, a, acum, e, dt_e, xdt, ea_e, d_e) = _ssd_common(
            ps_ref, halo8, ss_ref[...], cw, cb_ref[...], prm, tb)
        act_t = acum.T
        nw = nw_ref[0:1, :]
        row_id = _iota2((CH, 1), 0)

        dx_l, db_l, dc_l, dz_l, dacum_l, ddt_l, da_in_l = ([None] * ncb for _ in range(7))
        upper_tri = (_iota2((CH, CH), 1) >= _iota2((CH, CH), 0)).astype(F32)
        below = jnp.bitwise_and(_iota2((CH, gw), 1), CH - 1) < _iota2((CH, gw), 0)
        dnw_acc = jnp.zeros((1, SSD_W), F32)
        dd_acc = jnp.zeros((1, SSD_W), F32)

        for c in reversed(range(ncb)):
            rows, wd_e, eal_e = _ssd_chunk(act, acum, act_t, e, c)
            xc = act[rows, 0:SSD_W]
            z = ps_ref[rows, 1536:2560]
            dob = dob_ref[rows, :]
            sz = _silu(z)
            dy_g, dz_g, zacc_g, dxdt_g, dal_g, db_g, dc_g, da_in_g = [], [], [], [], [], [], [], []
            for g in range(SSD_G):
                gc_ = slice(g * gw, (g + 1) * gw)
                bg = act[rows, SSD_W + g * 128:SSD_W + (g + 1) * 128]
                cg = act[rows, SSD_W + 256 + g * 128:SSD_W + 256 + (g + 1) * 128]
                cbm = _dot_nt(cg, bg)
                hs = st_ref[c, :, gc_]
                yin = _dot(cg, hs)
                lms, yh = [], []
                for hh in range(hg):
                    h = g * hg + hh
                    lm = cbm * _ssd_lmat(acum, act_t, c, h, causal)
                    lms.append(lm)
                    yh.append(_dot(lm, xdt[rows, h * SSD_P:(h + 1) * SSD_P]))
                y_intra = jnp.concatenate(yh, axis=1)
                ea_g = ea_e[rows, gc_]
                y = y_intra + yin * ea_g + xc[:, gc_] * d_e[:, gc_]
                yz = y * sz[:, gc_]
                on, r, _ = _rms_fwd(yz, nw[:, gc_], gw)
                dyz, dnw_rows = _rms_bwd(dob[:, gc_], on, r, nw[:, gc_], gw)
                dnw_acc = dnw_acc + _put_cols(jnp.sum(dnw_rows, axis=0, keepdims=True), g, gw)
                dy = dyz * sz[:, gc_]
                dz_g.append(dyz * y * _dsilu(z[:, gc_]))
                dd_acc = dd_acc + _put_cols(jnp.sum(dy * xc[:, gc_], axis=0, keepdims=True), g, gw)
                dhs_n = dhs_scr[:, gc_]
                dyin = dy * ea_g
                dcg = _dot_nt(dyin, hs)
                xw = xdt[rows, gc_] * wd_e[:, gc_]
                dbg = _dot_nt(xw, dhs_n)
                dxw = _dot(bg, dhs_n)
                dhs_scr[:, gc_] = dhs_n * eal_e[:, gc_] + _dot_tn(cg, dyin)
                dal_g.append(jnp.sum(hs * dhs_n, axis=0, keepdims=True) * eal_e[:, gc_]
                             + jnp.sum(dxw * xw, axis=0, keepdims=True))
                dxi, ms, dcbm = [], [], jnp.zeros((CH, CH), F32)
                for hh in range(hg):
                    h = g * hg + hh
                    hc = slice(hh * SSD_P, (hh + 1) * SSD_P)
                    dyh = dy[:, hc]
                    dxi.append(_dot_tn(lms[hh], dyh))
                    dlm = _dot_nt(dyh, xdt[rows, h * SSD_P:(h + 1) * SSD_P])
                    ms.append(dlm * lms[hh])
                    dcbm = dcbm + dlm * _ssd_lmat(acum, act_t, c, h, causal)
                dx_intra = jnp.concatenate(dxi, axis=1)
                ncat = _dot(upper_tri, jnp.concatenate(ms, axis=1))
                da_in_g.append(jnp.where(below, ncat, 0.0))
                zacc_g.append(dy * yin * ea_g - dxw * xw)
                dxdt_g.append(dx_intra + dxw * wd_e[:, gc_])
                dy_g.append(dy)
                db_g.append(dbg + _dot_tn(dcbm, cg))
                dc_g.append(dcg + _dot(dcbm, bg))
            dy = jnp.concatenate(dy_g, axis=1)
            dxdt = jnp.concatenate(dxdt_g, axis=1)
            dx_l[c] = dxdt * dt_e[rows, :] + dy * d_e
            db_l[c] = jnp.concatenate(db_g, axis=1)
            dc_l[c] = jnp.concatenate(dc_g, axis=1)
            dz_l[c] = jnp.concatenate(dz_g, axis=1)
            ddt_l[c] = _reduce_heads(dxdt * xc, e)
            dalast = _reduce_heads(_row8(jnp.concatenate(dal_g, axis=1)), e)[0:1, :]
            dacum_l[c] = _reduce_heads(jnp.concatenate(zacc_g, axis=1), e) + jnp.where(row_id == CH - 1, dalast, 0.0)
            da_in_l[c] = _reduce_heads(jnp.concatenate(da_in_g, axis=1), e)

        dacum_all = jnp.concatenate(dacum_l, axis=0)
        da = _dot01l(_chunk_tri(tb, upper=True), dacum_all) + jnp.concatenate(da_in_l, axis=0)
        neg_ea = -jnp.exp(prm[0:1, :])
        ddt = jnp.concatenate(ddt_l, axis=0) + da * neg_ea
        ddt_in = ddt * _sigmoid(dt_in)
        dss_ref[...] = ddt_in.astype(dss_ref.dtype)
        sub8 = _iota2((8, 128), 0)
        dalog = jnp.sum(da * a, axis=0, keepdims=True)
        ddtb = jnp.sum(ddt_in, axis=0, keepdims=True)
        dd = _reduce_heads(_row8(dd_acc), e)[0:1, :]
        dprm_ref[...] += (jnp.where(sub8 == 0, dalog, 0.0) + jnp.where(sub8 == 1, ddtb, 0.0)
                          + jnp.where(sub8 == 2, dd, 0.0))
        dnw_ref[...] += jnp.where(_iota2((8, SSD_W), 0) == 0, dnw_acc, 0.0)

        dact = jnp.concatenate([jnp.concatenate(dx_l, axis=0), jnp.concatenate(db_l, axis=0),
                                jnp.concatenate(dc_l, axis=0)], axis=1)
        dpre = dact * _dsilu(pre)
        back = _conv_back(dpre, nxt_scr[...], tb)
        nxt_scr[...] = dpre[0:8, :]
        draw = back[0] * cw[3:4, :] + back[1] * cw[2:3, :] + back[2] * cw[1:2, :] + back[3] * cw[0:1, :]
        dps_ref[:, 0:1536] = draw.astype(dps_ref.dtype)
        dps_ref[:, 1536:2560] = jnp.concatenate(dz_l, axis=0).astype(dps_ref.dtype)
        sub_c = _iota2((8, 1536), 0)
        dcw_new = jnp.zeros((8, 1536), F32)
        for s_ in range(CONV_W):
            dcw_new = dcw_new + jnp.where(sub_c == 3 - s_, jnp.sum(dpre * taps[s_], axis=0, keepdims=True), 0.0)
        dcw_ref[...] += dcw_new
        dcb_ref[...] += jnp.where(sub_c == 0, jnp.sum(dpre, axis=0, keepdims=True), 0.0)

    def call(ps, ss, cw, cb, prm, nw, st, dob):
        rev = lambda i: (nb - 1 - i, 0)
        const = lambda i: (0, 0)
        return pl.pallas_call(
            body,
            grid=(nb,),
            in_specs=[
                pl.BlockSpec((tb, 2560), rev),
                pl.BlockSpec((8, 1536), lambda i: (jnp.maximum((nb - 1 - i) * hb - 1, 0), 0)),
                pl.BlockSpec((tb, 128), rev),
                pl.BlockSpec((8, 1536), const),
                pl.BlockSpec((8, 1536), const),
                pl.BlockSpec((8, 128), const),
                pl.BlockSpec((8, SSD_W), const),
                pl.BlockSpec((ncb, SSD_N, SSD_W), lambda i: (nb - 1 - i, 0, 0)),
                pl.BlockSpec((tb, SSD_W), rev),
            ],
            out_specs=[
                pl.BlockSpec((tb, 2560), rev),
                pl.BlockSpec((tb, 128), rev),
                pl.BlockSpec((8, 1536), const),
                pl.BlockSpec((8, 1536), const),
                pl.BlockSpec((8, 128), const),
                pl.BlockSpec((8, SSD_W), const),
            ],
            out_shape=[
                jax.ShapeDtypeStruct((seq, 2560), BF16),
                jax.ShapeDtypeStruct((seq, 128), BF16),
                jax.ShapeDtypeStruct((8, 1536), F32),
                jax.ShapeDtypeStruct((8, 1536), F32),
                jax.ShapeDtypeStruct((8, 128), F32),
                jax.ShapeDtypeStruct((8, SSD_W), F32),
            ],
            scratch_shapes=[pltpu.VMEM((SSD_N, SSD_W), F32), pltpu.VMEM((8, 1536), F32)],
            compiler_params=pltpu.CompilerParams(dimension_semantics=("arbitrary",), vmem_limit_bytes=VMEM_LIMIT),
            name="ssd_bwd",
        )(ps, ps, ss, cw, cb, prm, nw, st, dob)

    return call


def _ret_consts(h):
    lg = math.log(1.0 - 2.0 ** (-5.0 - h))
    r = _iota2((CH, CH), 0)
    c = _iota2((CH, CH), 1)
    rel = (r - c).astype(F32)
    dmat = jnp.where(r >= c, jnp.exp(jnp.maximum(rel, 0.0) * lg), 0.0)
    idx = _iota2((CH, 1), 0).astype(F32)
    qdec = jnp.exp((idx + 1.0) * lg)
    kdec = jnp.exp((CH - 1.0 - idx) * lg)
    cdec = math.exp(CH * lg)
    return dmat, qdec, kdec, cdec


def _ret_batch(pr_ref, cc_ref, ss_ref, ncb):
    pairs = [(c, h) for c in range(ncb) for h in range(RET_H)]

    def st(off):
        return jnp.stack([pr_ref[c * CH:(c + 1) * CH, off + h * 128:off + (h + 1) * 128] for c, h in pairs])

    cc = jnp.stack([cc_ref[c * CH:(c + 1) * CH, :] for c, _ in pairs])
    ss = jnp.stack([ss_ref[c * CH:(c + 1) * CH, :] for c, _ in pairs])
    consts = [_ret_consts(h) for h in range(RET_H)]
    dmat = jnp.stack([consts[h][0] for _, h in pairs])
    qdec = jnp.stack([consts[h][1] for _, h in pairs])
    kdec = jnp.stack([consts[h][2] for _, h in pairs])
    cdec = jnp.stack([jnp.full((1, 1), consts[h][3], F32) for h in range(RET_H)])
    q = _rot(st(0), cc, ss)
    k = _rot(st(512), cc, ss) * (RET_D ** -0.5)
    return dict(q=q, k=k, v=st(1024), z=st(1536), cc=cc, ss=ss, dmat=dmat, qdec=qdec, kdec=kdec, cdec=cdec,
                s=_bdot(q, k, _NT) * dmat)


def _rot(t, cc, ss):
    return t * cc + pltpu.roll(t, 64, axis=t.ndim - 1) * ss


def _rot_bwd(d, cc, ss):
    return d * cc + pltpu.roll(d * ss, 64, axis=d.ndim - 1)


def _make_ret_fwd(seq, tb):
    ncb = tb // CH
    nb = seq // tb

    def body(pr_ref, cc_ref, ss_ref, nw_ref, oc_ref, st_ref, r_scr):
        @pl.when(pl.program_id(0) == 0)
        def _():
            r_scr[...] = jnp.zeros_like(r_scr)

        d = _ret_batch(pr_ref, cc_ref, ss_ref, ncb)
        kd = d["k"] * d["kdec"]
        for c in range(ncb):
            bs = slice(c * RET_H, (c + 1) * RET_H)
            rs = r_scr[...]
            st_ref[c] = rs
            r_scr[...] = rs * d["cdec"] + _bdot(kd[bs], d["v"][bs], _TN)
        r_prev = st_ref[...].reshape(ncb * RET_H, 128, 128)
        o = _bdot(d["s"], d["v"], _NN) + _bdot(d["q"], r_prev, _NN) * d["qdec"]
        _, _, y = _rms_fwd(o, nw_ref[0:1, :], RET_D)
        out = y * _silu(d["z"])
        for c in range(ncb):
            for h in range(RET_H):
                oc_ref[c * CH:(c + 1) * CH, h * 128:(h + 1) * 128] = out[c * RET_H + h].astype(oc_ref.dtype)

    def call(pr, cc, ss, nw):
        return pl.pallas_call(
            body,
            grid=(nb,),
            in_specs=[
                pl.BlockSpec((tb, 2048), lambda i: (i, 0)),
                pl.BlockSpec((tb, 128), lambda i: (i, 0)),
                pl.BlockSpec((tb, 128), lambda i: (i, 0)),
                pl.BlockSpec((8, 128), lambda i: (0, 0)),
            ],
            out_specs=[
                pl.BlockSpec((tb, 512), lambda i: (i, 0)),
                pl.BlockSpec((ncb, RET_H, 128, 128), lambda i: (i, 0, 0, 0)),
            ],
            out_shape=[
                jax.ShapeDtypeStruct((seq, 512), BF16),
                jax.ShapeDtypeStruct((seq // CH, RET_H, 128, 128), F32),
            ],
            scratch_shapes=[pltpu.VMEM((RET_H, 128, 128), F32)],
            compiler_params=pltpu.CompilerParams(dimension_semantics=("arbitrary",), vmem_limit_bytes=VMEM_LIMIT),
            name="ret_fwd",
        )(pr, cc, ss, nw)

    return call


def _make_ret_bwd(seq, tb):
    ncb = tb // CH
    nb = seq // tb

    def body(pr_ref, cc_ref, ss_ref, nw_ref, st_ref, doc_ref, dpr_ref, dnw_ref, dr_scr):
        @pl.when(pl.program_id(0) == 0)
        def _():
            dr_scr[...] = jnp.zeros_like(dr_scr)
            dnw_ref[...] = jnp.zeros_like(dnw_ref)

        nw = nw_ref[0:1, :]
        scale = RET_D ** -0.5
        n = ncb * RET_H
        d = _ret_batch(pr_ref, cc_ref, ss_ref, ncb)
        q, k, v, z, s = d["q"], d["k"], d["v"], d["z"], d["s"]
        r_prev = st_ref[...].reshape(n, 128, 128)
        o = _bdot(s, v, _NN) + _bdot(q, r_prev, _NN) * d["qdec"]
        doc = jnp.stack([doc_ref[c * CH:(c + 1) * CH, h * 128:(h + 1) * 128]
                         for c in range(ncb) for h in range(RET_H)])
        on, r, y = _rms_fwd(o, nw, RET_D)
        dz = doc * y * _dsilu(z)
        do, dnw_rows = _rms_bwd(doc * _silu(z), on, r, nw, RET_D)
        dnw_acc = jnp.sum(jnp.sum(dnw_rows, axis=0), axis=0, keepdims=True)
        dqd = do * d["qdec"]
        qtd = _bdot(q, dqd, _TN)
        drn_l = [None] * ncb
        for c in reversed(range(ncb)):
            drn_l[c] = dr_scr[...]
            dr_scr[...] = qtd[c * RET_H:(c + 1) * RET_H] + d["cdec"] * drn_l[c]
        drn = jnp.concatenate(drn_l, axis=0)
        ds = _bdot(do, v, _NT) * d["dmat"]
        dq = _rot_bwd(_bdot(ds, k, _NN) + _bdot(dqd, r_prev, _NT), d["cc"], d["ss"])
        dk = _rot_bwd((_bdot(ds, q, _TN) + _bdot(v, drn, _NT) * d["kdec"]) * scale, d["cc"], d["ss"])
        dv = _bdot(s, do, _TN) + _bdot(k * d["kdec"], drn, _NN)
        for c in range(ncb):
            rows = slice(c * CH, (c + 1) * CH)
            for h in range(RET_H):
                b = c * RET_H + h
                for j, val in enumerate((dq, dk, dv, dz)):
                    dpr_ref[rows, j * 512 + h * 128:j * 512 + (h + 1) * 128] = val[b].astype(dpr_ref.dtype)
        dnw_ref[...] += jnp.where(_iota2((8, 128), 0) == 0, dnw_acc, 0.0)

    def call(pr, cc, ss, nw, st, doc):
        rev = lambda i: (nb - 1 - i, 0)
        return pl.pallas_call(
            body,
            grid=(nb,),
            in_specs=[
                pl.BlockSpec((tb, 2048), rev),
                pl.BlockSpec((tb, 128), rev),
                pl.BlockSpec((tb, 128), rev),
                pl.BlockSpec((8, 128), lambda i: (0, 0)),
                pl.BlockSpec((ncb, RET_H, 128, 128), lambda i: (nb - 1 - i, 0, 0, 0)),
                pl.BlockSpec((tb, 512), rev),
            ],
            out_specs=[
                pl.BlockSpec((tb, 2048), rev),
                pl.BlockSpec((8, 128), lambda i: (0, 0)),
            ],
            out_shape=[
                jax.ShapeDtypeStruct((seq, 2048), BF16),
                jax.ShapeDtypeStruct((8, 128), F32),
            ],
            scratch_shapes=[pltpu.VMEM((RET_H, 128, 128), F32)],
            compiler_params=pltpu.CompilerParams(dimension_semantics=("arbitrary",), vmem_limit_bytes=VMEM_LIMIT),
            name="ret_bwd",
        )(pr, cc, ss, nw, st, doc)

    return call


def _rope_tables(seq):
    half = RET_D // 2
    inv = ROPE_BASE ** (-jnp.arange(half, dtype=F32) / half)
    ang = jnp.arange(seq, dtype=jnp.int32).astype(F32)[:, None] * inv[None, :]
    cos, sin = jnp.cos(ang), jnp.sin(ang)
    return jnp.concatenate([cos, cos], axis=1), jnp.concatenate([-sin, sin], axis=1)


SEG_G, SEG_S, SEG_R, SEG_GS, SEG_SS = (0, 2048), (2048, 4608), (4608, 6656), (6656, 6784), (6784, 6912)
NP = 6912
SEGS = (SEG_G, SEG_S, SEG_R, SEG_GS, SEG_SS)


def _resident(shape):
    return pl.BlockSpec(shape, lambda i: (0,) * len(shape), pipeline_mode=pl.Buffered(1))


def _make_inproj(seq, tl):
    def body(x_ref, pn_ref, w_ref, pg_ref, ps_ref, pr_ref, gs_ref, ss_ref, ht_ref):
        x = x_ref[...]
        _, _, hn = _rms_fwd(x, pn_ref[0:1, :], D_MODEL)
        h = hn.astype(BF16)
        ht_ref[...] = hn.T.astype(BF16)
        for (a, b), o_ref in zip(SEGS, (pg_ref, ps_ref, pr_ref, gs_ref, ss_ref)):
            o_ref[...] = jnp.dot(h, w_ref[:, a:b], preferred_element_type=F32)

    def call(x, pn, w):
        row = lambda i: (i, 0)
        return pl.pallas_call(
            body,
            grid=(seq // tl,),
            in_specs=[pl.BlockSpec((tl, D_MODEL), row), _resident((8, D_MODEL)), _resident((D_MODEL, NP))],
            out_specs=[pl.BlockSpec((tl, b - a), row) for a, b in SEGS]
            + [pl.BlockSpec((D_MODEL, tl), lambda i: (0, i))],
            out_shape=[jax.ShapeDtypeStruct((seq, b - a), F32) for a, b in SEGS]
            + [jax.ShapeDtypeStruct((D_MODEL, seq), BF16)],
            compiler_params=pltpu.CompilerParams(dimension_semantics=("arbitrary",), vmem_limit_bytes=VMEM_LIMIT),
            name="inproj",
        )(x, pn, w)

    return call


def _make_outproj(seq, tl):
    def body(oa_ref, ob_ref, oc_ref, w_ref, x_ref, qn_ref, out_ref, xn_ref):
        out = (jnp.dot(oa_ref[...], w_ref[0:512, :], preferred_element_type=F32)
               + jnp.dot(ob_ref[...], w_ref[512:1536, :], preferred_element_type=F32)
               + jnp.dot(oc_ref[...], w_ref[1536:2048, :], preferred_element_type=F32))
        out_ref[...] = out
        _, _, y = _rms_fwd(out, qn_ref[0:1, :], D_MODEL)
        xn_ref[...] = x_ref[...] + y

    def call(oa, ob, oc, w, x, qn):
        row = lambda i: (i, 0)
        return pl.pallas_call(
            body,
            grid=(seq // tl,),
            in_specs=[pl.BlockSpec((tl, 512), row), pl.BlockSpec((tl, 1024), row), pl.BlockSpec((tl, 512), row),
                      _resident((2048, D_MODEL)), pl.BlockSpec((tl, D_MODEL), row), _resident((8, D_MODEL))],
            out_specs=[pl.BlockSpec((tl, D_MODEL), row), pl.BlockSpec((tl, D_MODEL), row)],
            out_shape=[jax.ShapeDtypeStruct((seq, D_MODEL), F32), jax.ShapeDtypeStruct((seq, D_MODEL), F32)],
            compiler_params=pltpu.CompilerParams(dimension_semantics=("arbitrary",), vmem_limit_bytes=VMEM_LIMIT),
            name="outproj",
        )(oa, ob, oc, w, x, qn)

    return call


def _make_loss_head(seq, tl):
    def body(y_ref, t_ref, dy_ref, loss_ref):
        @pl.when(pl.program_id(0) == 0)
        def _():
            loss_ref[...] = jnp.zeros_like(loss_ref)

        err = y_ref[...] - t_ref[...]
        dy_ref[...] = err * (1.0 / D_MODEL)
        part = jnp.sum(jnp.sum(err * err, axis=1, keepdims=True), axis=0, keepdims=True) * (0.5 / D_MODEL)
        loss_ref[...] += jnp.where((_iota2((8, 128), 0) == 0) & (_iota2((8, 128), 1) == 0), part, 0.0)

    def call(y, t):
        row = lambda i: (i, 0)
        return pl.pallas_call(
            body,
            grid=(seq // tl,),
            in_specs=[pl.BlockSpec((tl, D_MODEL), row), pl.BlockSpec((tl, D_MODEL), row)],
            out_specs=[pl.BlockSpec((tl, D_MODEL), row), pl.BlockSpec((8, 128), lambda i: (0, 0))],
            out_shape=[jax.ShapeDtypeStruct((seq, D_MODEL), F32), jax.ShapeDtypeStruct((8, 128), F32)],
            compiler_params=pltpu.CompilerParams(dimension_semantics=("arbitrary",)),
            name="loss_head",
        )(y, t)

    return call


def _make_outproj_bwd(seq, tl):
    def body(dxn_ref, out_ref, oa_ref, ob_ref, oc_ref, w_ref, qn_ref, doa_ref, dob_ref, doc_ref, dqn_ref, dw_ref):
        @pl.when(pl.program_id(0) == 0)
        def _():
            dqn_ref[...] = jnp.zeros_like(dqn_ref)
            dw_ref[...] = jnp.zeros_like(dw_ref)

        qn = qn_ref[0:1, :]
        on, r, _ = _rms_fwd(out_ref[...], qn, D_MODEL)
        dout, dqn_rows = _rms_bwd(dxn_ref[...], on, r, qn, D_MODEL)
        dqn_ref[...] += jnp.where(_iota2((8, D_MODEL), 0) == 0, jnp.sum(dqn_rows, axis=0, keepdims=True), 0.0)
        db = dout.astype(BF16)
        nt = (((1,), (1,)), ((), ()))
        tn = (((0,), (0,)), ((), ()))
        doa_ref[...] = lax.dot_general(db, w_ref[0:512, :], nt, preferred_element_type=F32)
        dob_ref[...] = lax.dot_general(db, w_ref[512:1536, :], nt, preferred_element_type=F32)
        doc_ref[...] = lax.dot_general(db, w_ref[1536:2048, :], nt, preferred_element_type=F32)
        dw_ref[0:512, :] += lax.dot_general(oa_ref[...], db, tn, preferred_element_type=F32)
        dw_ref[512:1536, :] += lax.dot_general(ob_ref[...], db, tn, preferred_element_type=F32)
        dw_ref[1536:2048, :] += lax.dot_general(oc_ref[...], db, tn, preferred_element_type=F32)

    def call(dxn, out, oa, ob, oc, w, qn):
        row = lambda i: (i, 0)
        const = lambda i: (0, 0)
        return pl.pallas_call(
            body,
            grid=(seq // tl,),
            in_specs=[pl.BlockSpec((tl, D_MODEL), row), pl.BlockSpec((tl, D_MODEL), row),
                      pl.BlockSpec((tl, 512), row), pl.BlockSpec((tl, 1024), row), pl.BlockSpec((tl, 512), row),
                      _resident((2048, D_MODEL)), _resident((8, D_MODEL))],
            out_specs=[pl.BlockSpec((tl, 512), row), pl.BlockSpec((tl, 1024), row), pl.BlockSpec((tl, 512), row),
                       pl.BlockSpec((8, D_MODEL), const), pl.BlockSpec((2048, D_MODEL), const)],
            out_shape=[jax.ShapeDtypeStruct((seq, 512), F32), jax.ShapeDtypeStruct((seq, 1024), F32),
                       jax.ShapeDtypeStruct((seq, 512), F32), jax.ShapeDtypeStruct((8, D_MODEL), F32),
                       jax.ShapeDtypeStruct((2048, D_MODEL), F32)],
            compiler_params=pltpu.CompilerParams(dimension_semantics=("arbitrary",), vmem_limit_bytes=VMEM_LIMIT),
            name="outproj_bwd",
        )(dxn, out, oa, ob, oc, w, qn)

    return call


def _make_inproj_bwd_dx(seq, tl):
    def body(dg_ref, ds_ref, dr_ref, dgs_ref, dss_ref, w_ref, x_ref, pn_ref, dxn_ref, dx_ref, dpn_ref):
        @pl.when(pl.program_id(0) == 0)
        def _():
            dpn_ref[...] = jnp.zeros_like(dpn_ref)

        nt = (((1,), (1,)), ((), ()))
        dh = jnp.zeros((tl, D_MODEL), F32)
        for (a, b), d_ref in zip(SEGS, (dg_ref, ds_ref, dr_ref, dgs_ref, dss_ref)):
            dh = dh + lax.dot_general(d_ref[...], w_ref[:, a:b], nt, preferred_element_type=F32)
        pn = pn_ref[0:1, :]
        on, r, _ = _rms_fwd(x_ref[...], pn, D_MODEL)
        dx, dpn_rows = _rms_bwd(dh, on, r, pn, D_MODEL)
        dx_ref[...] = dx + dxn_ref[...]
        dpn_ref[...] += jnp.where(_iota2((8, D_MODEL), 0) == 0, jnp.sum(dpn_rows, axis=0, keepdims=True), 0.0)

    def call(dg, ds, dr, dgs, dss, w, x, pn, dxn):
        row = lambda i: (i, 0)
        return pl.pallas_call(
            body,
            grid=(seq // tl,),
            in_specs=[pl.BlockSpec((tl, b - a), row) for a, b in SEGS]
            + [_resident((D_MODEL, NP)), pl.BlockSpec((tl, D_MODEL), row), _resident((8, D_MODEL)),
               pl.BlockSpec((tl, D_MODEL), row)],
            out_specs=[pl.BlockSpec((tl, D_MODEL), row), pl.BlockSpec((8, D_MODEL), lambda i: (0, 0))],
            out_shape=[jax.ShapeDtypeStruct((seq, D_MODEL), F32), jax.ShapeDtypeStruct((8, D_MODEL), F32)],
            compiler_params=pltpu.CompilerParams(dimension_semantics=("arbitrary",), vmem_limit_bytes=VMEM_LIMIT),
            name="inproj_bwd_dx",
        )(dg, ds, dr, dgs, dss, w, x, pn, dxn)

    return call


def _make_inproj_bwd_dw(seq, tl, width, tn, name):
    def body(ht_ref, d_ref, dw_ref):
        @pl.when(pl.program_id(1) == 0)
        def _():
            dw_ref[...] = jnp.zeros_like(dw_ref)

        dw_ref[...] += jnp.dot(ht_ref[...], d_ref[...], preferred_element_type=F32)

    def call(ht, d):
        return pl.pallas_call(
            body,
            grid=(width // tn, seq // tl),
            in_specs=[pl.BlockSpec((D_MODEL, tl), lambda j, i: (0, i)), pl.BlockSpec((tl, tn), lambda j, i: (i, j))],
            out_specs=pl.BlockSpec((D_MODEL, tn), lambda j, i: (0, j)),
            out_shape=jax.ShapeDtypeStruct((D_MODEL, width), F32),
            compiler_params=pltpu.CompilerParams(dimension_semantics=("arbitrary", "arbitrary"),
                                                 vmem_limit_bytes=VMEM_LIMIT),
            name=name,
        )(ht, d)

    return call


ADAM_LR, ADAM_B1, ADAM_B2, ADAM_EPS, ADAM_WD, ADAM_STEP = 0.001, 0.9, 0.999, 1e-08, 0.01, 10


def _adam_math(w, g, m, v):
    m = ADAM_B1 * m + (1.0 - ADAM_B1) * g
    v = ADAM_B2 * v + (1.0 - ADAM_B2) * (g * g)
    m_hat = m / (1.0 - ADAM_B1 ** ADAM_STEP)
    v_hat = v / (1.0 - ADAM_B2 ** ADAM_STEP)
    delta = -ADAM_LR * (m_hat / (jnp.sqrt(v_hat) + ADAM_EPS) + ADAM_WD * w)
    return delta, m, v


def _adamw(w, g, m, v, name):
    shape = w.shape
    cols = shape[-1]
    rows = w.size // cols
    tr = rows if rows <= 512 else 256
    assert rows % tr == 0

    def body(w_ref, g_ref, m_ref, v_ref, d_ref, mo_ref, vo_ref):
        d_ref[...], mo_ref[...], vo_ref[...] = _adam_math(w_ref[...], g_ref[...], m_ref[...], v_ref[...])

    spec = pl.BlockSpec((tr, cols), lambda i: (i, 0))
    outs = pl.pallas_call(
        body,
        grid=(rows // tr,),
        in_specs=[spec] * 4,
        out_specs=[spec] * 3,
        out_shape=[jax.ShapeDtypeStruct((rows, cols), F32)] * 3,
        compiler_params=pltpu.CompilerParams(dimension_semantics=("arbitrary",), vmem_limit_bytes=VMEM_LIMIT),
        name=name,
    )(*[a.reshape(rows, cols) for a in (w, g, m, v)])
    return (g,) + tuple(o.reshape(shape) for o in outs)


def _adamw_halves(w, mine, theirs, m, v, core, name):
    na, r, cols = w.shape
    tr = 256
    nh = r // 2 // tr
    assert nh * tr * 2 == r

    def body(core_ref, w_ref, a_ref, b_ref, m_ref, v_ref, g_ref, d_ref, mo_ref, vo_ref):
        g = jnp.where(pl.program_id(1) // nh == core_ref[0], a_ref[...], b_ref[...])
        g_ref[...] = g
        d_ref[...], mo_ref[...], vo_ref[...] = _adam_math(w_ref[...], g, m_ref[...], v_ref[...])

    full = pl.BlockSpec((None, tr, cols), lambda a, i, cr: (a, i, 0))
    half = lambda other: pl.BlockSpec(
        (None, tr, cols), lambda a, i, cr: (a, jnp.clip(i - ((1 - cr[0]) if other else cr[0]) * nh, 0, nh - 1), 0))
    return pl.pallas_call(
        body,
        grid_spec=pltpu.PrefetchScalarGridSpec(
            num_scalar_prefetch=1,
            grid=(na, r // tr),
            in_specs=[full, half(False), half(True), full, full],
            out_specs=[full] * 4),
        out_shape=[jax.ShapeDtypeStruct(w.shape, F32)] * 4,
        compiler_params=pltpu.CompilerParams(dimension_semantics=("arbitrary",) * 2, vmem_limit_bytes=VMEM_LIMIT),
        name=name,
    )(core, w, mine, theirs, m, v)


MESH = pl.DeviceIdType.MESH
ANY = pl.BlockSpec(memory_space=pl.ANY)
CHIP_REL = ((1, 0), (0, 1), (1, 1))


def _flip(v, d):
    return 1 - v if d else v


def _ag_chips(arrs, name):
    n = len(arrs)

    def body(*refs):
        ins, outs = refs[:n], refs[n:2 * n]
        send_sems, recv_sems, loc_sems = refs[2 * n:]
        x, y, c = lax.axis_index("x"), lax.axis_index("y"), lax.axis_index("c")
        me = 2 * x + y

        def remote(a, k, slot):
            dx, dy = CHIP_REL[k]
            return pltpu.make_async_remote_copy(
                src_ref=ins[a], dst_ref=outs[a].at[slot], send_sem=send_sems.at[a * 3 + k],
                recv_sem=recv_sems.at[a * 3 + k], device_id=(_flip(x, dx), _flip(y, dy), c), device_id_type=MESH)

        local = [pltpu.make_async_copy(ins[a], outs[a].at[me], loc_sems.at[a]) for a in range(n)]
        for cp in local:
            cp.start()
        for a in range(n):
            for k in range(3):
                remote(a, k, me).start()
        for a in range(n):
            for k, (dx, dy) in enumerate(CHIP_REL):
                remote(a, k, 2 * _flip(x, dx) + _flip(y, dy)).wait_recv()
        for a in range(n):
            for k in range(3):
                remote(a, k, me).wait_send()
        for cp in local:
            cp.wait()

    return pl.pallas_call(
        body,
        in_specs=[ANY] * n,
        out_specs=[ANY] * n,
        out_shape=[jax.ShapeDtypeStruct((4,) + a.shape, a.dtype) for a in arrs],
        scratch_shapes=[pltpu.SemaphoreType.DMA((3 * n,)), pltpu.SemaphoreType.DMA((3 * n,)),
                        pltpu.SemaphoreType.DMA((n,))],
        compiler_params=pltpu.CompilerParams(has_side_effects=True),
        name=name,
    )(*arrs)


def _rs_chips(arrs, name):
    n = len(arrs)

    def body(*refs):
        ins, outs = refs[:n], refs[n:2 * n]
        send_sems, recv_sems = refs[2 * n:]
        x, y, c = lax.axis_index("x"), lax.axis_index("y"), lax.axis_index("c")

        def remote(a, k):
            dx, dy = CHIP_REL[k]
            px, py = _flip(x, dx), _flip(y, dy)
            return pltpu.make_async_remote_copy(
                src_ref=ins[a].at[2 * px + py], dst_ref=outs[a].at[k], send_sem=send_sems.at[a * 3 + k],
                recv_sem=recv_sems.at[a * 3 + k], device_id=(px, py, c), device_id_type=MESH)

        cps = [remote(a, k) for a in range(n) for k in range(3)]
        for cp in cps:
            cp.start()
        for cp in cps:
            cp.wait_recv()
        for cp in cps:
            cp.wait_send()

    return pl.pallas_call(
        body,
        in_specs=[ANY] * n,
        out_specs=[ANY] * n,
        out_shape=[jax.ShapeDtypeStruct((3,) + a.shape[1:], a.dtype) for a in arrs],
        scratch_shapes=[pltpu.SemaphoreType.DMA((3 * n,)), pltpu.SemaphoreType.DMA((3 * n,))],
        compiler_params=pltpu.CompilerParams(has_side_effects=True),
        name=name,
    )(*arrs)


def _half(ref_or_shape, half):
    r = ref_or_shape[-2] // 2
    return pl.ds(half * r, r)


def _ag_rows(arrs, name):
    n = len(arrs)

    def body(*refs):
        ins, outs = refs[:n], refs[n:2 * n]
        send_sems, recv_sems, fsend_sems, frecv_sems, loc_sems = refs[2 * n:]
        x, y, c = lax.axis_index("x"), lax.axis_index("y"), lax.axis_index("c")
        me = 2 * x + y
        sib = (x, y, 1 - c)

        def chip_of(k):
            dx, dy = CHIP_REL[k]
            return _flip(x, dx), _flip(y, dy)

        def ici(a, k, slot):
            px, py = chip_of(k)
            rows = _half(arrs[a].shape, c)
            return pltpu.make_async_remote_copy(
                src_ref=ins[a].at[:, rows, :], dst_ref=outs[a].at[slot, :, rows, :], send_sem=send_sems.at[a * 3 + k],
                recv_sem=recv_sems.at[a * 3 + k], device_id=(px, py, c), device_id_type=MESH)

        def fwd(a, k, half):
            px, py = chip_of(k)
            blk = outs[a].at[2 * px + py, :, _half(arrs[a].shape, half), :]
            return pltpu.make_async_remote_copy(
                src_ref=blk, dst_ref=blk, send_sem=fsend_sems.at[a * 3 + k], recv_sem=frecv_sems.at[a * 3 + k],
                device_id=sib, device_id_type=MESH)

        own = [pltpu.make_async_remote_copy(src_ref=ins[a], dst_ref=outs[a].at[me], send_sem=loc_sems.at[a],
                                            recv_sem=loc_sems.at[n + a], device_id=sib, device_id_type=MESH)
               for a in range(n)]
        for cp in own:
            cp.start()
        for a in range(n):
            for k in range(3):
                ici(a, k, me).start()
        for a in range(n):
            for k in range(3):
                px, py = chip_of(k)
                ici(a, k, 2 * px + py).wait_recv()
                fwd(a, k, c).start()
        for a in range(n):
            for k in range(3):
                fwd(a, k, 1 - c).wait_recv()
        for a in range(n):
            for k in range(3):
                ici(a, k, me).wait_send()
                fwd(a, k, c).wait_send()
        for cp in own:
            cp.wait()

    return pl.pallas_call(
        body,
        in_specs=[ANY] * n,
        out_specs=[ANY] * n,
        out_shape=[jax.ShapeDtypeStruct((4,) + a.shape, a.dtype) for a in arrs],
        scratch_shapes=[pltpu.SemaphoreType.DMA((3 * n,)) for _ in range(4)] + [pltpu.SemaphoreType.DMA((2 * n,))],
        compiler_params=pltpu.CompilerParams(has_side_effects=True),
        name=name,
    )(*arrs)


def _rs_to_sibling(arrs, name):
    n = len(arrs)

    def body(*refs):
        ins, outs = refs[:n], refs[n:2 * n]
        send_sems, recv_sems = refs[2 * n:]
        x, y, c = lax.axis_index("x"), lax.axis_index("y"), lax.axis_index("c")
        cps = [pltpu.make_async_remote_copy(
            src_ref=ins[a].at[:, :, _half(arrs[a].shape, 1 - c), :], dst_ref=outs[a], send_sem=send_sems.at[a],
            recv_sem=recv_sems.at[a], device_id=(x, y, 1 - c), device_id_type=MESH) for a in range(n)]
        for cp in cps:
            cp.start()
        for cp in cps:
            cp.wait_recv()
        for cp in cps:
            cp.wait_send()

    return pl.pallas_call(
        body,
        in_specs=[ANY] * n,
        out_specs=[ANY] * n,
        out_shape=[jax.ShapeDtypeStruct(a.shape[:2] + (a.shape[2] // 2, a.shape[3]), a.dtype) for a in arrs],
        scratch_shapes=[pltpu.SemaphoreType.DMA((n,)), pltpu.SemaphoreType.DMA((n,))],
        compiler_params=pltpu.CompilerParams(has_side_effects=True),
        name=name,
    )(*arrs)


def _add_halves(full, recv, core, name):
    _, na, r, cols = full.shape
    rh = r // 2
    tr = 256
    assert rh % tr == 0
    nt = rh // tr

    def body(core_ref, f_ref, r_ref, s_ref, sb_ref):
        s = f_ref[...] + r_ref[...]
        s_ref[...] = s
        sb_ref[...] = s.astype(BF16)

    blk = (None, None, tr, cols)
    return pl.pallas_call(
        body,
        grid_spec=pltpu.PrefetchScalarGridSpec(
            num_scalar_prefetch=1,
            grid=(4, na, nt),
            in_specs=[pl.BlockSpec(blk, lambda k, a, i, cr: (k, a, cr[0] * nt + i, 0)),
                      pl.BlockSpec(blk, lambda k, a, i, cr: (k, a, i, 0))],
            out_specs=[pl.BlockSpec(blk, lambda k, a, i, cr: (k, a, i, 0)),
                       pl.BlockSpec(blk, lambda k, a, i, cr: (k, a, i, 0))]),
        out_shape=[jax.ShapeDtypeStruct(recv.shape, F32), jax.ShapeDtypeStruct(recv.shape, BF16)],
        compiler_params=pltpu.CompilerParams(dimension_semantics=("arbitrary",) * 3, vmem_limit_bytes=VMEM_LIMIT),
        name=name,
    )(core, full, recv)


def _sum_chips(own, recv, chip, name):
    _, na, r, cols = own.shape
    tr = 256
    assert r % tr == 0

    def body(chip_ref, o_ref, r_ref, s_ref):
        s_ref[...] = ((o_ref[...] + r_ref[0].astype(F32)) + r_ref[1].astype(F32)) + r_ref[2].astype(F32)

    return pl.pallas_call(
        body,
        grid_spec=pltpu.PrefetchScalarGridSpec(
            num_scalar_prefetch=1,
            grid=(na, r // tr),
            in_specs=[pl.BlockSpec((None, None, tr, cols), lambda a, i, ch: (ch[0], a, i, 0)),
                      pl.BlockSpec((3, None, tr, cols), lambda a, i, ch: (0, a, i, 0))],
            out_specs=pl.BlockSpec((None, tr, cols), lambda a, i, ch: (a, i, 0))),
        out_shape=jax.ShapeDtypeStruct((na, r, cols), F32),
        compiler_params=pltpu.CompilerParams(dimension_semantics=("arbitrary",) * 2, vmem_limit_bytes=VMEM_LIMIT),
        name=name,
    )(chip, own, recv)


def _swap_sibling(arrs, name):
    n = len(arrs)

    def body(*refs):
        ins, outs = refs[:n], refs[n:2 * n]
        send_sems, recv_sems = refs[2 * n:]
        x, y, c = lax.axis_index("x"), lax.axis_index("y"), lax.axis_index("c")
        cps = [pltpu.make_async_remote_copy(src_ref=ins[a], dst_ref=outs[a], send_sem=send_sems.at[a],
                                            recv_sem=recv_sems.at[a], device_id=(x, y, 1 - c), device_id_type=MESH)
               for a in range(n)]
        for cp in cps:
            cp.start()
        for cp in cps:
            cp.wait_recv()
        for cp in cps:
            cp.wait_send()

    return pl.pallas_call(
        body,
        in_specs=[ANY] * n,
        out_specs=[ANY] * n,
        out_shape=[jax.ShapeDtypeStruct(a.shape, a.dtype) for a in arrs],
        scratch_shapes=[pltpu.SemaphoreType.DMA((n,)), pltpu.SemaphoreType.DMA((n,))],
        compiler_params=pltpu.CompilerParams(has_side_effects=True),
        name=name,
    )(*arrs)


def _allreduce_small(vec, name):
    rows = vec.shape[0]

    def body(v_ref, out_ref, gat_ref, send_sems, recv_sems):
        x, y, c = lax.axis_index("x"), lax.axis_index("y"), lax.axis_index("c")
        me = 4 * x + 2 * y + c

        def remote(k, slot):
            dx, dy, dc = (k >> 2) & 1, (k >> 1) & 1, k & 1
            return pltpu.make_async_remote_copy(
                src_ref=v_ref, dst_ref=gat_ref.at[slot], send_sem=send_sems.at[k - 1], recv_sem=recv_sems.at[k - 1],
                device_id=(_flip(x, dx), _flip(y, dy), _flip(c, dc)), device_id_type=MESH)

        gat_ref[me] = v_ref[...]
        for k in range(1, 8):
            remote(k, me).start()
        for k in range(1, 8):
            dx, dy, dc = (k >> 2) & 1, (k >> 1) & 1, k & 1
            remote(k, 4 * _flip(x, dx) + 2 * _flip(y, dy) + _flip(c, dc)).wait_recv()
        for k in range(1, 8):
            remote(k, me).wait_send()
        acc = gat_ref[0]
        for j in range(1, 8):
            acc = acc + gat_ref[j]
        out_ref[...] = acc

    vm = pl.BlockSpec(memory_space=pltpu.VMEM)
    return pl.pallas_call(
        body,
        in_specs=[vm],
        out_specs=vm,
        out_shape=jax.ShapeDtypeStruct(vec.shape, F32),
        scratch_shapes=[pltpu.VMEM((8, rows, 128), F32), pltpu.SemaphoreType.DMA((7,)), pltpu.SemaphoreType.DMA((7,))],
        compiler_params=pltpu.CompilerParams(has_side_effects=True),
        name=name,
    )(vec)


def _pad8(v, width, lane0=0):
    v = v.reshape(1, -1) if v.ndim == 1 else v
    return jnp.zeros((8, width), F32).at[:v.shape[0], lane0:lane0 + v.shape[1]].set(v.astype(F32))


def _relayout_w_in(w):
    z = lambda n: jnp.zeros(w.shape[:-1] + (n,), w.dtype)
    return jnp.concatenate([w[..., 0:2048], w[..., 2056:4616], w[..., 4632:6680],
                            w[..., 2048:2056], z(120), w[..., 4616:4632], z(112)], axis=-1)


def _unlayout_dw_in(dg, ds, dr, dgs, dss):
    return jnp.concatenate([dg, dgs[:, 0:8], ds, dss[:, 0:16], dr], axis=1)


TB = 256
TL = 256
TK = 1024


def kernel(x, pre_norm, post_norm, w_in, gdn_conv, gdn_A_log, gdn_dt_bias, gdn_norm, ssd_conv, ssd_conv_b, ssd_A_log, ssd_dt_bias, ssd_D, ssd_norm, ret_norm, w_out, loss_target, m_pre_norm, m_post_norm, m_w_in, m_gdn_conv, m_gdn_A_log, m_gdn_dt_bias, m_gdn_norm, m_ssd_conv, m_ssd_conv_b, m_ssd_A_log, m_ssd_dt_bias, m_ssd_D, m_ssd_norm, m_ret_norm, m_w_out, v_pre_norm, v_post_norm, v_w_in, v_gdn_conv, v_gdn_A_log, v_gdn_dt_bias, v_gdn_norm, v_ssd_conv, v_ssd_conv_b, v_ssd_A_log, v_ssd_dt_bias, v_ssd_D, v_ssd_norm, v_ret_norm, v_w_out):
    seq = x.shape[1]
    chip = 2 * lax.axis_index("x") + lax.axis_index("y")
    x0 = x[0]

    wi_g, wo_g = _ag_rows([w_in.astype(BF16), w_out.astype(BF16)], "ag_weights")
    gcv_g, scv_g = _ag_chips([gdn_conv, ssd_conv], "ag_conv")
    wp = _relayout_w_in(jnp.transpose(wi_g, (1, 2, 0, 3)).reshape(DEPTH, D_MODEL, N_IN))
    wo = jnp.transpose(wo_g, (1, 0, 2, 3)).reshape(DEPTH, 2048, D_MODEL)
    gcv = jnp.transpose(gcv_g, (1, 2, 0, 3)).reshape(DEPTH, CONV_W, 1536)
    scv = jnp.transpose(scv_g, (1, 2, 0, 3)).reshape(DEPTH, CONV_W, 1536)
    rope_c, rope_s = _rope_tables(seq)

    saved = []
    xc = x0
    for l in range(DEPTH):
        p = dict(
            pn=_pad8(pre_norm[l], D_MODEL), qn=_pad8(post_norm[l], D_MODEL),
            g_cw=_pad8(gcv[l], 1536), g_prm=_pad8(jnp.stack([gdn_A_log[l], gdn_dt_bias[l]]), 128, 4),
            g_nw=_pad8(gdn_norm[l], 128),
            s_cw=_pad8(scv[l], 1536), s_cb=_pad8(ssd_conv_b[l], 1536),
            s_prm=_pad8(jnp.stack([ssd_A_log[l], ssd_dt_bias[l], ssd_D[l]]), 128), s_nw=_pad8(ssd_norm[l], SSD_W),
            r_nw=_pad8(ret_norm[l], 128))
        pg, ps, pr, gs, ss, ht = _make_inproj(seq, TL)(xc, p["pn"], wp[l])
        oa, stg, tig, uwg = _make_gdn_fwd(seq, TB)(pg, gs, p["g_cw"], p["g_prm"], p["g_nw"])
        ob, sts = _make_ssd_fwd(seq, TB)(ps, ss, p["s_cw"], p["s_cb"], p["s_prm"], p["s_nw"])
        oc, str_ = _make_ret_fwd(seq, TB)(pr, rope_c, rope_s, p["r_nw"])
        out, xn = _make_outproj(seq, TL)(oa, ob, oc, wo[l], xc, p["qn"])
        saved.append(dict(p=p, x=xc, ht=ht, pg=pg, ps=ps, pr=pr, gs=gs, ss=ss, stg=stg, tig=tig, uwg=uwg, sts=sts, str=str_,
                          oa=oa, ob=ob, oc=oc, out=out))
        xc = xn

    dxn, lossp = _make_loss_head(seq, TL)(xc, loss_target[0])

    small = [None] * DEPTH
    dwi = [None] * DEPTH
    dwo = [None] * DEPTH
    for l in reversed(range(DEPTH)):
        s = saved[l]
        p = s["p"]
        doa, dob, doc, dqn, dwo[l] = _make_outproj_bwd(seq, TL)(dxn, s["out"], s["oa"], s["ob"], s["oc"], wo[l], p["qn"])
        dpg, dgs, dcw_g, dprm_g, dnw_g = _make_gdn_bwd(seq, TB)(s["pg"], s["gs"], p["g_cw"], p["g_prm"], p["g_nw"],
                                                                s["stg"], s["tig"], s["uwg"], doa)
        dps, dss, dcw_s, dcb_s, dprm_s, dnw_s = _make_ssd_bwd(seq, TB)(s["ps"], s["ss"], p["s_cw"], p["s_cb"],
                                                                       p["s_prm"], p["s_nw"], s["sts"], dob)
        dpr, dnw_r = _make_ret_bwd(seq, TB)(s["pr"], rope_c, rope_s, p["r_nw"], s["str"], doc)
        dx, dpn = _make_inproj_bwd_dx(seq, TL)(dpg, dps, dpr, dgs, dss, wp[l], s["x"], p["pn"], dxn)
        dws = [_make_inproj_bwd_dw(seq, TK, d.shape[1], tn, f"inproj_bwd_dw{i}")(s["ht"], d)
               for i, (d, tn) in enumerate(((dpg, 1024), (dps, 1280), (dpr, 1024),
                                            (jnp.concatenate([dgs, dss], axis=1), 256)))]
        dwi[l] = _unlayout_dw_in(dws[0], dws[1], dws[2], dws[3][:, 0:128], dws[3][:, 128:256])
        small[l] = [dpn[0], dqn[0], dcw_g[0:4].reshape(-1), dprm_g[0, 4:8], dprm_g[1, 4:8], dnw_g[0],
                    dcw_s[0:4].reshape(-1), dcb_s[0], dprm_s[0, 0:16], dprm_s[1, 0:16], dprm_s[2, 0:16],
                    dnw_s[0], dnw_r[0]]
        dxn = dx
    grad_x = dxn[None]

    sizes = [a.shape[0] for a in small[0]]
    flat = jnp.concatenate(small[0] + small[1] + [lossp[0, 0:1]])
    n_flat = flat.shape[0]
    rows = -(-n_flat // 1024) * 8
    red = _allreduce_small(jnp.pad(flat, (0, rows * 128 - n_flat)).reshape(rows, 128), "allreduce_small").reshape(-1)
    per = sum(sizes)
    loss = red[2 * per]

    def pick(i):
        off = sum(sizes[:i])
        return jnp.stack([red[l * per + off:l * per + off + sizes[i]] for l in range(DEPTH)])

    g_small = dict(
        pre_norm=pick(0), post_norm=pick(1),
        gdn_conv=lax.dynamic_slice_in_dim(pick(2).reshape(DEPTH, CONV_W, 1536), chip * 384, 384, axis=2),
        gdn_A_log=pick(3), gdn_dt_bias=pick(4), gdn_norm=pick(5),
        ssd_conv=lax.dynamic_slice_in_dim(pick(6).reshape(DEPTH, CONV_W, 1536), chip * 384, 384, axis=2),
        ssd_conv_b=pick(7), ssd_A_log=pick(8), ssd_dt_bias=pick(9), ssd_D=pick(10), ssd_norm=pick(11),
        ret_norm=pick(12))

    gin = jnp.transpose(jnp.stack(dwi).reshape(DEPTH, D_MODEL, 4, N_IN // 4), (2, 0, 1, 3))
    gout = jnp.transpose(jnp.stack(dwo).reshape(DEPTH, 4, 512, D_MODEL), (1, 0, 2, 3))
    core1 = lax.axis_index("c").astype(jnp.int32).reshape(1)
    chip1 = chip.astype(jnp.int32).reshape(1)
    sib_in, sib_out = _rs_to_sibling([gin, gout], "rs_to_sibling")
    h_in, hb_in = _add_halves(gin, sib_in, core1, "add_halves_w_in")
    h_out, hb_out = _add_halves(gout, sib_out, core1, "add_halves_w_out")
    q_in, q_out = _rs_chips([hb_in, hb_out], "rs_grads")
    s_in = _sum_chips(h_in, q_in, chip1, "sum_chips_w_in")
    s_out = _sum_chips(h_out, q_out, chip1, "sum_chips_w_out")
    t_in, t_out = _swap_sibling([s_in, s_out], "swap_grads")

    weights = dict(pre_norm=pre_norm, post_norm=post_norm, w_in=w_in, gdn_conv=gdn_conv, gdn_A_log=gdn_A_log,
                   gdn_dt_bias=gdn_dt_bias, gdn_norm=gdn_norm, ssd_conv=ssd_conv, ssd_conv_b=ssd_conv_b,
                   ssd_A_log=ssd_A_log, ssd_dt_bias=ssd_dt_bias, ssd_D=ssd_D, ssd_norm=ssd_norm, ret_norm=ret_norm,
                   w_out=w_out)
    ms = dict(pre_norm=m_pre_norm, post_norm=m_post_norm, w_in=m_w_in, gdn_conv=m_gdn_conv, gdn_A_log=m_gdn_A_log,
              gdn_dt_bias=m_gdn_dt_bias, gdn_norm=m_gdn_norm, ssd_conv=m_ssd_conv, ssd_conv_b=m_ssd_conv_b,
              ssd_A_log=m_ssd_A_log, ssd_dt_bias=m_ssd_dt_bias, ssd_D=m_ssd_D, ssd_norm=m_ssd_norm,
              ret_norm=m_ret_norm, w_out=m_w_out)
    vs = dict(pre_norm=v_pre_norm, post_norm=v_post_norm, w_in=v_w_in, gdn_conv=v_gdn_conv, gdn_A_log=v_gdn_A_log,
              gdn_dt_bias=v_gdn_dt_bias, gdn_norm=v_gdn_norm, ssd_conv=v_ssd_conv, ssd_conv_b=v_ssd_conv_b,
              ssd_A_log=v_ssd_A_log, ssd_dt_bias=v_ssd_dt_bias, ssd_D=v_ssd_D, ssd_norm=v_ssd_norm,
              ret_norm=v_ret_norm, w_out=v_w_out)
    names = list(weights)
    res = {}
    for nme in names:
        if nme == "w_in":
            res[nme] = _adamw_halves(w_in, s_in, t_in, m_w_in, v_w_in, core1, "adamw_w_in")
        elif nme == "w_out":
            res[nme] = _adamw_halves(w_out, s_out, t_out, m_w_out, v_w_out, core1, "adamw_w_out")
        else:
            res[nme] = _adamw(weights[nme], g_small[nme], ms[nme], vs[nme], "adamw_" + nme)
    return (loss, grad_x, *[res[n][0] for n in names], *[res[n][1] for n in names],
            *[res[n][2] for n in names], *[res[n][3] for n in names])
```

```python
import functools
import math

import jax
import jax.numpy as jnp
from jax import lax
from jax.experimental import pallas as pl
from jax.experimental.pallas import tpu as pltpu

F32 = jnp.float32
BF16 = jnp.bfloat16
HI = lax.Precision.HIGHEST

D_MODEL = 1024
DEPTH = 2
CH = 64
CONV_W = 4
EPS = 1e-6
GDN_H, GDN_D = 4, 128
SSD_H, SSD_P, SSD_N, SSD_G = 16, 64, 128, 2
SSD_W = SSD_H * SSD_P
RET_H, RET_D = 4, 128
ROPE_BASE = 10000.0
N_IN = 6680
NEG = -1e30

VMEM_LIMIT = 56 * 1024 * 1024


def _dot(a, b):
    return jnp.dot(a.astype(BF16), b.astype(BF16), preferred_element_type=F32)


def _dot_nt(a, b):
    return lax.dot_general(a.astype(BF16), b.astype(BF16), (((1,), (1,)), ((), ())), preferred_element_type=F32)


def _dot_tn(a, b):
    return lax.dot_general(a.astype(BF16), b.astype(BF16), (((0,), (0,)), ((), ())), preferred_element_type=F32)


def _split(a):
    hi = a.astype(BF16)
    return hi, (a - hi.astype(F32)).astype(BF16)


def _dot01l(m, v):
    vh, vl = _split(v)
    mb = m.astype(BF16)
    return jnp.dot(mb, vh, preferred_element_type=F32) + jnp.dot(mb, vl, preferred_element_type=F32)


def _dot01r(v, m):
    vh, vl = _split(v)
    mb = m.astype(BF16)
    return jnp.dot(vh, mb, preferred_element_type=F32) + jnp.dot(vl, mb, preferred_element_type=F32)


def _sigmoid(x):
    return jax.nn.sigmoid(x)


def _silu(x):
    return x * _sigmoid(x)


def _dsilu(x):
    s = _sigmoid(x)
    return s * (1.0 + x * (1.0 - s))


def _softplus(x):
    return jnp.maximum(x, 0.0) + jnp.log1p(jnp.exp(-jnp.abs(x)))


def _iota2(shape, dim):
    return lax.broadcasted_iota(jnp.int32, shape, dim)


def _chunk_tri(tb, upper=False):
    r = _iota2((tb, tb), 0)
    c = _iota2((tb, tb), 1)
    same = jnp.right_shift(r, 6) == jnp.right_shift(c, 6)
    return (same & ((c >= r) if upper else (c <= r))).astype(F32)


def _masks():
    r = _iota2((CH, CH), 0)
    c = _iota2((CH, CH), 1)
    return r >= c, r > c, (r == c).astype(F32)


def _put_lane(col, lane_idx, width=128):
    lane = _iota2((col.shape[0], width), 1)
    return jnp.where(lane == lane_idx, col, 0.0)


def _conv_taps(raw, halo8, tb):
    ext = jnp.concatenate([halo8, raw], axis=0)
    return [raw] + [pltpu.roll(ext, s, axis=0)[8:] for s in (1, 2, 3)]


def _conv_back(dpre, nxt8, tb):
    ext = jnp.concatenate([dpre, nxt8], axis=0)
    return [dpre] + [pltpu.roll(ext, tb + 8 - s, axis=0)[:tb] for s in (1, 2, 3)]


def _rms_fwd(o, w, n):
    r = lax.rsqrt(jnp.sum(o * o, axis=-1, keepdims=True) * (1.0 / n) + EPS)
    on = o * r
    return on, r, on * w


def _rms_bwd(dy, on, r, w, n):
    don = dy * w
    return r * (don - on * (jnp.sum(don * on, axis=-1, keepdims=True) * (1.0 / n))), dy * on


def _put_cols(v, g, gw):
    z = jnp.zeros_like(v)
    return jnp.concatenate([v, z] if g == 0 else [z, v], axis=1)


def _gdn_common(pg_ref, halo8, sm, cw, prm, tb):
    raw = pg_ref[:, 0:1536]
    taps = _conv_taps(raw, halo8, tb)
    pre = taps[0] * cw[3:4, :] + taps[1] * cw[2:3, :] + taps[2] * cw[1:2, :] + taps[3] * cw[0:1, :]
    act = _silu(pre)
    beta = _sigmoid(sm)
    sp_in = sm + prm[1:2, :]
    g = -jnp.exp(prm[0:1, :]) * _softplus(sp_in)
    gc = _dot01l(_chunk_tri(tb), g)
    return raw, taps, pre, act, beta, sp_in, g, gc


_NN = (((2,), (1,)), ((0,), (0,)))
_NT = (((2,), (2,)), ((0,), (0,)))
_TN = (((1,), (1,)), ((0,), (0,)))


def _bdot(a, b, dn):
    return lax.dot_general(a.astype(BF16), b.astype(BF16), dn, preferred_element_type=F32)


def _dot3_parts(ah, al, bh, bl, dn):
    f = lambda p, q: lax.dot_general(p, q, dn, preferred_element_type=F32)
    return f(ah, bh) + (f(ah, bl) + f(al, bh))


def _bdot3(a, b, dn):
    ah, al = _split(a)
    bh, bl = _split(b)
    return _dot3_parts(ah, al, bh, bl, dn)


def _binv_unit_lower(a, eye):
    x = eye - a
    ph, pl_ = _split(a)
    for _ in range(5):
        ph, pl_ = _split(_dot3_parts(ph, pl_, ph, pl_, _NN))
        xh, xl = _split(x)
        x = x + _dot3_parts(xh, xl, ph, pl_, _NN)
    return x


def _rsum(v):
    return jnp.sum(v, axis=-1, keepdims=True)


def _gdn_batch(act, beta, gc, gct, eg_all, ncb, masks):
    causal, strict, _ = masks

    def st(fn):
        return jnp.stack([fn(c, h, slice(c * CH, (c + 1) * CH)) for c in range(ncb) for h in range(GDN_H)])

    qr = st(lambda c, h, r: act[r, h * 128:(h + 1) * 128])
    kr = st(lambda c, h, r: act[r, 512 + h * 128:512 + (h + 1) * 128])
    vh = st(lambda c, h, r: act[r, 1024 + h * 128:1024 + (h + 1) * 128])
    bh = st(lambda c, h, r: beta[r, h:h + 1])
    gcol = st(lambda c, h, r: gc[r, 4 + h:5 + h])
    grow = st(lambda c, h, r: gct[4 + h:5 + h, r])
    eg = st(lambda c, h, r: eg_all[r, 4 + h:5 + h])
    glast = st(lambda c, h, r: gc[(c + 1) * CH - 1:(c + 1) * CH, 4 + h:5 + h])
    rq = lax.rsqrt(_rsum(qr * qr) + EPS)
    rk = lax.rsqrt(_rsum(kr * kr) + EPS)
    qn = qr * rq
    kh = kr * rk
    qh = qn * (GDN_D ** -0.5)
    decay = jnp.exp(jnp.where(causal, gcol - grow, NEG))
    kb = kh * bh
    kd_scale = jnp.exp(glast - gcol)
    return dict(qn=qn, rq=rq, kh=kh, rk=rk, qh=qh, vh=vh, bh=bh, eg=eg, decay=decay, kb=kb, vb=vh * bh, kg=kb * eg,
                qg=qh * eg, kd_scale=kd_scale, kdec=kh * kd_scale, egl=jnp.exp(glast),
                a=jnp.where(strict, _bdot(kb, kh, _NT) * decay, 0.0), attn=_bdot(qh, kh, _NT) * decay)


def _make_gdn_fwd(seq, tb):
    ncb = tb // CH
    nb = seq // tb
    n = ncb * GDN_H

    def body(pg_ref, sm_ref, cw_ref, prm_ref, nw_ref, oa_ref, st_ref, ti_ref, uw_ref, s_scr, halo_scr):
        @pl.when(pl.program_id(0) == 0)
        def _():
            s_scr[...] = jnp.zeros_like(s_scr)
            halo_scr[...] = jnp.zeros_like(halo_scr)

        masks = _masks()
        sm = sm_ref[...]
        raw, _, _, act, beta, _, _, gc = _gdn_common(pg_ref, halo_scr[...], sm, cw_ref[...], prm_ref[...], tb)
        halo_scr[...] = raw[tb - 8:tb, :]
        d = _gdn_batch(act, beta, gc, gc.T, jnp.exp(gc), ncb, masks)
        t = _binv_unit_lower(d["a"], masks[2])
        sol = _bdot3(t, jnp.concatenate([d["vb"], d["kg"]], axis=2), _NN)
        ti_ref[...] = t.reshape(ncb, GDN_H, CH, CH)
        uw_ref[...] = sol.reshape(ncb, GDN_H, CH, 256)
        u, w = sol[:, :, :128], sol[:, :, 128:]
        vns = []
        for c in range(ncb):
            bs = slice(c * GDN_H, (c + 1) * GDN_H)
            s = s_scr[...]
            st_ref[c] = s
            vn = u[bs] - _bdot(w[bs], s, _NN)
            s_scr[...] = s * d["egl"][bs] + _bdot(d["kdec"][bs], vn, _TN)
            vns.append(vn)
        v_new = jnp.concatenate(vns, axis=0)
        s_prev = st_ref[...].reshape(n, 128, 128)
        o = _bdot(d["qg"], s_prev, _NN) + _bdot(d["attn"], v_new, _NN)
        _, _, y = _rms_fwd(o, nw_ref[0:1, :], GDN_D)
        for c in range(ncb):
            rows = slice(c * CH, (c + 1) * CH)
            for h in range(GDN_H):
                z = pg_ref[rows, 1536 + h * 128:1536 + (h + 1) * 128]
                oa_ref[rows, h * 128:(h + 1) * 128] = (y[c * GDN_H + h] * _silu(z)).astype(oa_ref.dtype)

    def call(pg, sm, cw, prm, nw, comm=None, comm_args=()):
        blk4 = lambda i: (i, 0, 0, 0)
        cx = _exchange_specs(comm)
        return pl.pallas_call(
            _with_exchange(body, comm, 5, 4, nb),
            grid=(nb,),
            in_specs=[
                pl.BlockSpec((tb, 2048), lambda i: (i, 0)),
                pl.BlockSpec((tb, 128), lambda i: (i, 0)),
                pl.BlockSpec((8, 1536), lambda i: (0, 0)),
                pl.BlockSpec((8, 128), lambda i: (0, 0)),
                pl.BlockSpec((8, 128), lambda i: (0, 0)),
            ] + cx["specs"],
            out_specs=[
                pl.BlockSpec((tb, 512), lambda i: (i, 0)),
                pl.BlockSpec((ncb, GDN_H, 128, 128), blk4),
                pl.BlockSpec((ncb, GDN_H, CH, CH), blk4),
                pl.BlockSpec((ncb, GDN_H, CH, 256), blk4),
            ] + cx["specs"],
            out_shape=[
                jax.ShapeDtypeStruct((seq, 512), BF16),
                jax.ShapeDtypeStruct((seq // CH, GDN_H, 128, 128), F32),
                jax.ShapeDtypeStruct((seq // CH, GDN_H, CH, CH), F32),
                jax.ShapeDtypeStruct((seq // CH, GDN_H, CH, 256), F32),
            ] + cx["out_shape"],
            scratch_shapes=[pltpu.VMEM((GDN_H, 128, 128), F32), pltpu.VMEM((8, 1536), F32)] + cx["scratch"],
            compiler_params=pltpu.CompilerParams(dimension_semantics=("arbitrary",), vmem_limit_bytes=VMEM_LIMIT,
                                                 has_side_effects=comm is not None),
            name="gdn_fwd" + cx["tag"],
        )(pg, sm, cw, prm, nw, *comm_args)

    return call


def _make_gdn_bwd(seq, tb):
    ncb = tb // CH
    nb = seq // tb
    hb = tb // 8
    n = ncb * GDN_H

    def body(pg_ref, prev_ref, sm_ref, cw_ref, prm_ref, nw_ref, st_ref, ti_ref, uw_ref, doa_ref,
             dpg_ref, dsm_ref, dcw_ref, dprm_ref, dnw_ref, ds_scr, nxt_scr):
        i = pl.program_id(0)

        @pl.when(i == 0)
        def _():
            ds_scr[...] = jnp.zeros_like(ds_scr)
            nxt_scr[...] = jnp.zeros_like(nxt_scr)
            dcw_ref[...] = jnp.zeros_like(dcw_ref)
            dprm_ref[...] = jnp.zeros_like(dprm_ref)
            dnw_ref[...] = jnp.zeros_like(dnw_ref)

        masks = _masks()
        strict = masks[1]
        sm = sm_ref[...]
        cw = cw_ref[...]
        prm = prm_ref[...]
        halo8 = jnp.where(i == nb - 1, 0.0, prev_ref[...])
        raw, taps, pre, act, beta, sp_in, g, gc = _gdn_common(pg_ref, halo8, sm, cw, prm, tb)
        nw = nw_ref[0:1, :]
        row_id = _iota2((CH, 1), 0)
        d = _gdn_batch(act, beta, gc, gc.T, jnp.exp(gc), ncb, masks)
        t = ti_ref[...].reshape(n, CH, CH)
        sol = uw_ref[...].reshape(n, CH, 256)
        u, w = sol[:, :, :128], sol[:, :, 128:]
        s_prev = st_ref[...].reshape(n, 128, 128)
        v_new = u - _bdot(w, s_prev, _NN)
        o = _bdot(d["qg"], s_prev, _NN) + _bdot(d["attn"], v_new, _NN)

        pairs = [(c, h) for c in range(ncb) for h in range(GDN_H)]
        z = jnp.stack([pg_ref[c * CH:(c + 1) * CH, 1536 + h * 128:1536 + (h + 1) * 128] for c, h in pairs])
        doa = jnp.stack([doa_ref[c * CH:(c + 1) * CH, h * 128:(h + 1) * 128] for c, h in pairs])
        on, r, y = _rms_fwd(o, nw, GDN_D)
        dz = doa * y * _dsilu(z)
        do, dnw_rows = _rms_bwd(doa * _silu(z), on, r, nw, GDN_D)
        dnw_acc = jnp.sum(jnp.sum(dnw_rows, axis=0), axis=0, keepdims=True)

        dvn_in = _bdot(d["attn"], do, _TN)
        qgtdo = _bdot(d["qg"], do, _TN)
        dvn_l, dkdec_l, dgl_l = [None] * ncb, [None] * ncb, [None] * ncb
        for c in reversed(range(ncb)):
            bs = slice(c * GDN_H, (c + 1) * GDN_H)
            dsn = ds_scr[...]
            dvn_c = dvn_in[bs] + _bdot(d["kdec"][bs], dsn, _NN)
            ds_scr[...] = d["egl"][bs] * dsn + qgtdo[bs] - _bdot(w[bs], dvn_c, _TN)
            dvn_l[c] = dvn_c
            dkdec_l[c] = _bdot(v_new[bs], dsn, _NT)
            dgl_l[c] = d["egl"][bs] * jnp.sum(_rsum(s_prev[bs] * dsn), axis=1, keepdims=True)
        dvn = jnp.concatenate(dvn_l, axis=0)
        dkdec = jnp.concatenate(dkdec_l, axis=0)
        dglast = jnp.concatenate(dgl_l, axis=0)

        dqg = _bdot(do, s_prev, _NT)
        dattn = _bdot(do, v_new, _NT)
        dw = -_bdot(dvn, s_prev, _NT)
        drhs = _bdot3(t, jnp.concatenate([dvn, dw], axis=2), _TN)
        dvb, dkg = drhs[:, :, :128], drhs[:, :, 128:]
        da = jnp.where(strict, -(_bdot(dvb, u, _NT) + _bdot(dkg, w, _NT)), 0.0)
        dp = da * d["decay"]
        dq_m = dattn * d["decay"]
        m = da * d["a"] + dattn * d["attn"]
        upper_tri = jnp.broadcast_to((_iota2((CH, CH), 1) >= _iota2((CH, CH), 0)).astype(BF16), (n, CH, CH))
        dg_in = _rsum(jnp.where(strict, _bdot(upper_tri, m, _NN), 0.0))
        dkb = _bdot(dp, d["kh"], _NN) + dkg * d["eg"]
        kdk_row = _rsum(dkdec * d["kdec"])
        dk = _bdot(dp, d["kb"], _TN) + _bdot(dq_m, d["qh"], _TN) + dkdec * d["kd_scale"] + dkb * d["bh"]
        dq = _bdot(dq_m, d["kh"], _NN) + dqg * d["eg"]
        dglast = dglast + jnp.sum(kdk_row, axis=1, keepdims=True)
        dgcol = (_rsum(dqg * d["qg"]) + _rsum(dkg * d["kg"]) - kdk_row + jnp.where(row_id == CH - 1, dglast, 0.0))
        dbeta = _rsum(dkb * d["kh"]) + _rsum(dvb * d["vh"])
        dn = dq * (GDN_D ** -0.5)
        dact_q = d["rq"] * (dn - d["qn"] * _rsum(dn * d["qn"]))
        dact_k = d["rk"] * (dk - d["kh"] * _rsum(dk * d["kh"]))
        dact_v = dvb * d["bh"]

        def lanes(v, lane0):
            return jnp.concatenate(
                [sum(_put_lane(v[c * GDN_H + h], lane0 + h) for h in range(GDN_H)) for c in range(ncb)], axis=0)

        def tokens(v):
            return jnp.concatenate(
                [jnp.concatenate([v[c * GDN_H + h] for h in range(GDN_H)], axis=1) for c in range(ncb)], axis=0)

        dbeta_all = lanes(dbeta, 0)
        dg = _dot01l(_chunk_tri(tb, upper=True), lanes(dgcol, 4)) + lanes(dg_in, 4)
        neg_ea = -jnp.exp(prm[0:1, :])
        da_raw = dg * neg_ea * _sigmoid(sp_in)
        db_raw = dbeta_all * beta * (1.0 - beta)
        dsm_ref[...] = (da_raw + db_raw).astype(dsm_ref.dtype)
        lane8 = _iota2((8, 128), 1)
        sub8 = _iota2((8, 128), 0)
        dalog = jnp.sum(dg * g, axis=0, keepdims=True)
        ddtb = jnp.sum(da_raw, axis=0, keepdims=True)
        dprm_ref[...] += jnp.where(sub8 == 0, dalog, 0.0) + jnp.where(sub8 == 1, ddtb, 0.0)
        dnw_ref[...] += jnp.where(sub8 == 0, dnw_acc, 0.0)

        dact = jnp.concatenate([tokens(dact_q), tokens(dact_k), tokens(dact_v)], axis=1)
        dpre = dact * _dsilu(pre)
        back = _conv_back(dpre, nxt_scr[...], tb)
        nxt_scr[...] = dpre[0:8, :]
        draw = back[0] * cw[3:4, :] + back[1] * cw[2:3, :] + back[2] * cw[1:2, :] + back[3] * cw[0:1, :]
        dpg_ref[:, 0:1536] = draw.astype(dpg_ref.dtype)
        dpg_ref[:, 1536:2048] = tokens(dz).astype(dpg_ref.dtype)
        sub_c = _iota2((8, 1536), 0)
        dcw_new = jnp.zeros((8, 1536), F32)
        for s_ in range(CONV_W):
            dcw_new = dcw_new + jnp.where(sub_c == 3 - s_, jnp.sum(dpre * taps[s_], axis=0, keepdims=True), 0.0)
        dcw_ref[...] += dcw_new

    def call(pg, sm, cw, prm, nw, st, ti, uw, doa):
        rev = lambda i: (nb - 1 - i, 0)
        const = lambda i: (0, 0)
        return pl.pallas_call(
            body,
            grid=(nb,),
            in_specs=[
                pl.BlockSpec((tb, 2048), rev),
                pl.BlockSpec((8, 1536), lambda i: (jnp.maximum((nb - 1 - i) * hb - 1, 0), 0)),
                pl.BlockSpec((tb, 128), rev),
                pl.BlockSpec((8, 1536), const),
                pl.BlockSpec((8, 128), const),
                pl.BlockSpec((8, 128), const),
                pl.BlockSpec((ncb, GDN_H, 128, 128), lambda i: (nb - 1 - i, 0, 0, 0)),
                pl.BlockSpec((ncb, GDN_H, CH, CH), lambda i: (nb - 1 - i, 0, 0, 0)),
                pl.BlockSpec((ncb, GDN_H, CH, 256), lambda i: (nb - 1 - i, 0, 0, 0)),
                pl.BlockSpec((tb, 512), rev),
            ],
            out_specs=[
                pl.BlockSpec((tb, 2048), rev),
                pl.BlockSpec((tb, 128), rev),
                pl.BlockSpec((8, 1536), const),
                pl.BlockSpec((8, 128), const),
                pl.BlockSpec((8, 128), const),
            ],
            out_shape=[
                jax.ShapeDtypeStruct((seq, 2048), BF16),
                jax.ShapeDtypeStruct((seq, 128), BF16),
                jax.ShapeDtypeStruct((8, 1536), F32),
                jax.ShapeDtypeStruct((8, 128), F32),
                jax.ShapeDtypeStruct((8, 128), F32),
            ],
            scratch_shapes=[pltpu.VMEM((GDN_H, 128, 128), F32), pltpu.VMEM((8, 1536), F32)],
            compiler_params=pltpu.CompilerParams(dimension_semantics=("arbitrary",), vmem_limit_bytes=VMEM_LIMIT),
            name="gdn_bwd",
        )(pg, pg, sm, cw, prm, nw, st, ti, uw, doa)

    return call


def _expand_mat():
    r = _iota2((128, SSD_W), 0)
    c = _iota2((128, SSD_W), 1)
    return (jnp.right_shift(c, 6) == r).astype(F32)


def _reduce_heads(v, e):
    vh, vl = _split(v)
    eb = e.astype(BF16)
    nt = (((1,), (1,)), ((), ()))
    return (lax.dot_general(vh, eb, nt, preferred_element_type=F32)
            + lax.dot_general(vl, eb, nt, preferred_element_type=F32))


def _row8(v):
    return jnp.broadcast_to(v, (8, v.shape[1]))


def _ssd_common(ps_ref, halo8, ss, cw, cb, prm, tb):
    raw = ps_ref[:, 0:1536]
    taps = _conv_taps(raw, halo8, tb)
    pre = taps[0] * cw[3:4, :] + taps[1] * cw[2:3, :] + taps[2] * cw[1:2, :] + taps[3] * cw[0:1, :] + cb[0:1, :]
    act = _silu(pre)
    dt_in = ss + prm[1:2, :]
    dt = _softplus(dt_in)
    a = dt * (-jnp.exp(prm[0:1, :]))
    acum = _dot01l(_chunk_tri(tb), a)
    e = _expand_mat()
    dt_e = _dot01r(dt, e)
    xdt = act[:, 0:SSD_W] * dt_e
    ea_e = _dot01r(jnp.exp(acum), e)
    d_e = _dot01r(_row8(prm[2:3, :]), e)[0:1, :]
    return raw, taps, pre, act, dt_in, dt, a, acum, e, dt_e, xdt, ea_e, d_e


def _ssd_chunk(act, acum, act_t, e, c):
    r0 = c * CH
    rows = slice(r0, r0 + CH)
    alast = acum[r0 + CH - 1:r0 + CH, :]
    wdec = jnp.exp(alast - acum[rows, :])
    wd_e = _dot01r(wdec, e)
    eal_e = _dot01r(_row8(jnp.exp(alast)), e)[0:1, :]
    return rows, wd_e, eal_e


def _ssd_lmat(acum, act_t, c, h, causal):
    r0 = c * CH
    acol = acum[r0:r0 + CH, h:h + 1]
    arow = act_t[h:h + 1, r0:r0 + CH]
    return jnp.exp(jnp.where(causal, acol - arow, NEG))


def _make_ssd_fwd(seq, tb):
    ncb = tb // CH
    nb = seq // tb
    hg = SSD_H // SSD_G
    gw = SSD_W // SSD_G

    def body(ps_ref, ss_ref, cw_ref, cb_ref, prm_ref, nw_ref, ob_ref, st_ref, hs_scr, halo_scr):
        @pl.when(pl.program_id(0) == 0)
        def _():
            hs_scr[...] = jnp.zeros_like(hs_scr)
            halo_scr[...] = jnp.zeros_like(halo_scr)

        causal, _, _ = _masks()
        (raw, _, _, act, _, _, _, acum, e, _, xdt, ea_e, d_e) = _ssd_common(
            ps_ref, halo_scr[...], ss_ref[...], cw_ref[...], cb_ref[...], prm_ref[...], tb)
        halo_scr[...] = raw[tb - 8:tb, :]
        act_t = acum.T
        nw = nw_ref[0:1, :]
        for c in range(ncb):
            rows, wd_e, eal_e = _ssd_chunk(act, acum, act_t, e, c)
            st_ref[c] = hs_scr[...]
            ys = []
            for g in range(SSD_G):
                gc_ = slice(g * gw, (g + 1) * gw)
                bg = act[rows, SSD_W + g * 128:SSD_W + (g + 1) * 128]
                cg = act[rows, SSD_W + 256 + g * 128:SSD_W + 256 + (g + 1) * 128]
                cbm = _dot_nt(cg, bg)
                hs = hs_scr[:, gc_]
                yin = _dot(cg, hs)
                yh = []
                for hh in range(hg):
                    h = g * hg + hh
                    lm = _ssd_lmat(acum, act_t, c, h, causal)
                    yh.append(_dot(cbm * lm, xdt[rows, h * SSD_P:(h + 1) * SSD_P]))
                ys.append(jnp.concatenate(yh, axis=1) + yin * ea_e[rows, gc_])
                hs_scr[:, gc_] = hs * eal_e[:, gc_] + _dot_tn(bg, xdt[rows, gc_] * wd_e[:, gc_])
            y = jnp.concatenate(ys, axis=1) + act[rows, 0:SSD_W] * d_e
            yz = y * _silu(ps_ref[rows, 1536:2560])
            outs = [_rms_fwd(yz[:, g * gw:(g + 1) * gw], nw[:, g * gw:(g + 1) * gw], gw)[2] for g in range(SSD_G)]
            ob_ref[rows, :] = jnp.concatenate(outs, axis=1).astype(ob_ref.dtype)

    def call(ps, ss, cw, cb, prm, nw):
        const = lambda i: (0, 0)
        return pl.pallas_call(
            body,
            grid=(nb,),
            in_specs=[
                pl.BlockSpec((tb, 2560), lambda i: (i, 0)),
                pl.BlockSpec((tb, 128), lambda i: (i, 0)),
                pl.BlockSpec((8, 1536), const),
                pl.BlockSpec((8, 1536), const),
                pl.BlockSpec((8, 128), const),
                pl.BlockSpec((8, SSD_W), const),
            ],
            out_specs=[
                pl.BlockSpec((tb, SSD_W), lambda i: (i, 0)),
                pl.BlockSpec((ncb, SSD_N, SSD_W), lambda i: (i, 0, 0)),
            ],
            out_shape=[
                jax.ShapeDtypeStruct((seq, SSD_W), BF16),
                jax.ShapeDtypeStruct((seq // CH, SSD_N, SSD_W), F32),
            ],
            scratch_shapes=[pltpu.VMEM((SSD_N, SSD_W), F32), pltpu.VMEM((8, 1536), F32)],
            compiler_params=pltpu.CompilerParams(dimension_semantics=("arbitrary",), vmem_limit_bytes=VMEM_LIMIT),
            name="ssd_fwd",
        )(ps, ss, cw, cb, prm, nw)

    return call


def _make_ssd_bwd(seq, tb):
    ncb = tb // CH
    nb = seq // tb
    hb = tb // 8
    hg = SSD_H // SSD_G
    gw = SSD_W // SSD_G

    def body(ps_ref, prev_ref, ss_ref, cw_ref, cb_ref, prm_ref, nw_ref, st_ref, dob_ref,
             dps_ref, dss_ref, dcw_ref, dcb_ref, dprm_ref, dnw_ref, dhs_scr, nxt_scr):
        i = pl.program_id(0)

        @pl.when(i == 0)
        def _():
            dhs_scr[...] = jnp.zeros_like(dhs_scr)
            nxt_scr[...] = jnp.zeros_like(nxt_scr)
            dcw_ref[...] = jnp.zeros_like(dcw_ref)
            dcb_ref[...] = jnp.zeros_like(dcb_ref)
            dprm_ref[...] = jnp.zeros_like(dprm_ref)
            dnw_ref[...] = jnp.zeros_like(dnw_ref)

        causal, _, _ = _masks()
        cw = cw_ref[...]
        prm = prm_ref[...]
        halo8 = jnp.where(i == nb - 1, 0.0, prev_ref[...])
        (raw, taps, pre, act, dt_in, dt, a, acum, e, dt_e, xdt, ea_e, d_e) = _ssd_common(
            ps_ref, halo8, ss_ref[...], cw, cb_ref[...], prm, tb)
        act_t = acum.T
        nw = nw_ref[0:1, :]
        row_id = _iota2((CH, 1), 0)

        dx_l, db_l, dc_l, dz_l, dacum_l, ddt_l, da_in_l = ([None] * ncb for _ in range(7))
        upper_tri = (_iota2((CH, CH), 1) >= _iota2((CH, CH), 0)).astype(F32)
        below = jnp.bitwise_and(_iota2((CH, gw), 1), CH - 1) < _iota2((CH, gw), 0)
        dnw_acc = jnp.zeros((1, SSD_W), F32)
        dd_acc = jnp.zeros((1, SSD_W), F32)

        for c in reversed(range(ncb)):
            rows, wd_e, eal_e = _ssd_chunk(act, acum, act_t, e, c)
            xc = act[rows, 0:SSD_W]
            z = ps_ref[rows, 1536:2560]
            dob = dob_ref[rows, :]
            sz = _silu(z)
            dy_g, dz_g, zacc_g, dxdt_g, dal_g, db_g, dc_g, da_in_g = [], [], [], [], [], [], [], []
            for g in range(SSD_G):
                gc_ = slice(g * gw, (g + 1) * gw)
                bg = act[rows, SSD_W + g * 128:SSD_W + (g + 1) * 128]
                cg = act[rows, SSD_W + 256 + g * 128:SSD_W + 256 + (g + 1) * 128]
                cbm = _dot_nt(cg, bg)
                hs = st_ref[c, :, gc_]
                yin = _dot(cg, hs)
                lms, yh = [], []
                for hh in range(hg):
                    h = g * hg + hh
                    lm = cbm * _ssd_lmat(acum, act_t, c, h, causal)
                    lms.append(lm)
                    yh.append(_dot(lm, xdt[rows, h * SSD_P:(h + 1) * SSD_P]))
                y_intra = jnp.concatenate(yh, axis=1)
                ea_g = ea_e[rows, gc_]
                y = y_intra + yin * ea_g + xc[:, gc_] * d_e[:, gc_]
                yz = y * sz[:, gc_]
                on, r, _ = _rms_fwd(yz, nw[:, gc_], gw)
                dyz, dnw_rows = _rms_bwd(dob[:, gc_], on, r, nw[:, gc_], gw)
                dnw_acc = dnw_acc + _put_cols(jnp.sum(dnw_rows, axis=0, keepdims=True), g, gw)
                dy = dyz * sz[:, gc_]
                dz_g.append(dyz * y * _dsilu(z[:, gc_]))
                dd_acc = dd_acc + _put_cols(jnp.sum(dy * xc[:, gc_], axis=0, keepdims=True), g, gw)
                dhs_n = dhs_scr[:, gc_]
                dyin = dy * ea_g
                dcg = _dot_nt(dyin, hs)
                xw = xdt[rows, gc_] * wd_e[:, gc_]
                dbg = _dot_nt(xw, dhs_n)
                dxw = _dot(bg, dhs_n)
                dhs_scr[:, gc_] = dhs_n * eal_e[:, gc_] + _dot_tn(cg, dyin)
                dal_g.append(jnp.sum(hs * dhs_n, axis=0, keepdims=True) * eal_e[:, gc_]
                             + jnp.sum(dxw * xw, axis=0, keepdims=True))
                dxi, ms, dcbm = [], [], jnp.zeros((CH, CH), F32)
                for hh in range(hg):
                    h = g * hg + hh
                    hc = slice(hh * SSD_P, (hh + 1) * SSD_P)
                    dyh = dy[:, hc]
                    dxi.append(_dot_tn(lms[hh], dyh))
                    dlm = _dot_nt(dyh, xdt[rows, h * SSD_P:(h + 1) * SSD_P])
                    ms.append(dlm * lms[hh])
                    dcbm = dcbm + dlm * _ssd_lmat(acum, act_t, c, h, causal)
                dx_intra = jnp.concatenate(dxi, axis=1)
                ncat = _dot(upper_tri, jnp.concatenate(ms, axis=1))
                da_in_g.append(jnp.where(below, ncat, 0.0))
                zacc_g.append(dy * yin * ea_g - dxw * xw)
                dxdt_g.append(dx_intra + dxw * wd_e[:, gc_])
                dy_g.append(dy)
                db_g.append(dbg + _dot_tn(dcbm, cg))
                dc_g.append(dcg + _dot(dcbm, bg))
            dy = jnp.concatenate(dy_g, axis=1)
            dxdt = jnp.concatenate(dxdt_g, axis=1)
            dx_l[c] = dxdt * dt_e[rows, :] + dy * d_e
            db_l[c] = jnp.concatenate(db_g, axis=1)
            dc_l[c] = jnp.concatenate(dc_g, axis=1)
            dz_l[c] = jnp.concatenate(dz_g, axis=1)
            ddt_l[c] = _reduce_heads(dxdt * xc, e)
            dalast = _reduce_heads(_row8(jnp.concatenate(dal_g, axis=1)), e)[0:1, :]
            dacum_l[c] = _reduce_heads(jnp.concatenate(zacc_g, axis=1), e) + jnp.where(row_id == CH - 1, dalast, 0.0)
            da_in_l[c] = _reduce_heads(jnp.concatenate(da_in_g, axis=1), e)

        dacum_all = jnp.concatenate(dacum_l, axis=0)
        da = _dot01l(_chunk_tri(tb, upper=True), dacum_all) + jnp.concatenate(da_in_l, axis=0)
        neg_ea = -jnp.exp(prm[0:1, :])
        ddt = jnp.concatenate(ddt_l, axis=0) + da * neg_ea
        ddt_in = ddt * _sigmoid(dt_in)
        dss_ref[...] = ddt_in.astype(dss_ref.dtype)
        sub8 = _iota2((8, 128), 0)
        dalog = jnp.sum(da * a, axis=0, keepdims=True)
        ddtb = jnp.sum(ddt_in, axis=0, keepdims=True)
        dd = _reduce_heads(_row8(dd_acc), e)[0:1, :]
        dprm_ref[...] += (jnp.where(sub8 == 0, dalog, 0.0) + jnp.where(sub8 == 1, ddtb, 0.0)
                          + jnp.where(sub8 == 2, dd, 0.0))
        dnw_ref[...] += jnp.where(_iota2((8, SSD_W), 0) == 0, dnw_acc, 0.0)

        dact = jnp.concatenate([jnp.concatenate(dx_l, axis=0), jnp.concatenate(db_l, axis=0),
                                jnp.concatenate(dc_l, axis=0)], axis=1)
        dpre = dact * _dsilu(pre)
        back = _conv_back(dpre, nxt_scr[...], tb)
        nxt_scr[...] = dpre[0:8, :]
        draw = back[0] * cw[3:4, :] + back[1] * cw[2:3, :] + back[2] * cw[1:2, :] + back[3] * cw[0:1, :]
        dps_ref[:, 0:1536] = draw.astype(dps_ref.dtype)
        dps_ref[:, 1536:2560] = jnp.concatenate(dz_l, axis=0).astype(dps_ref.dtype)
        sub_c = _iota2((8, 1536), 0)
        dcw_new = jnp.zeros((8, 1536), F32)
        for s_ in range(CONV_W):
            dcw_new = dcw_new + jnp.where(sub_c == 3 - s_, jnp.sum(dpre * taps[s_], axis=0, keepdims=True), 0.0)
        dcw_ref[...] += dcw_new
        dcb_ref[...] += jnp.where(sub_c == 0, jnp.sum(dpre, axis=0, keepdims=True), 0.0)

    def call(ps, ss, cw, cb, prm, nw, st, dob, comm=None, comm_args=()):
        rev = lambda i: (nb - 1 - i, 0)
        const = lambda i: (0, 0)
        cx = _exchange_specs(comm)
        return pl.pallas_call(
            _with_exchange(body, comm, 9, 6, nb),
            grid=(nb,),
            in_specs=[
                pl.BlockSpec((tb, 2560), rev),
                pl.BlockSpec((8, 1536), lambda i: (jnp.maximum((nb - 1 - i) * hb - 1, 0), 0)),
                pl.BlockSpec((tb, 128), rev),
                pl.BlockSpec((8, 1536), const),
                pl.BlockSpec((8, 1536), const),
                pl.BlockSpec((8, 128), const),
                pl.BlockSpec((8, SSD_W), const),
                pl.BlockSpec((ncb, SSD_N, SSD_W), lambda i: (nb - 1 - i, 0, 0)),
                pl.BlockSpec((tb, SSD_W), rev),
            ] + cx["specs"],
            out_specs=[
                pl.BlockSpec((tb, 2560), rev),
                pl.BlockSpec((tb, 128), rev),
                pl.BlockSpec((8, 1536), const),
                pl.BlockSpec((8, 1536), const),
                pl.BlockSpec((8, 128), const),
                pl.BlockSpec((8, SSD_W), const),
            ] + cx["specs"],
            out_shape=[
                jax.ShapeDtypeStruct((seq, 2560), BF16),
                jax.ShapeDtypeStruct((seq, 128), BF16),
                jax.ShapeDtypeStruct((8, 1536), F32),
                jax.ShapeDtypeStruct((8, 1536), F32),
                jax.ShapeDtypeStruct((8, 128), F32),
                jax.ShapeDtypeStruct((8, SSD_W), F32),
            ] + cx["out_shape"],
            scratch_shapes=[pltpu.VMEM((SSD_N, SSD_W), F32), pltpu.VMEM((8, 1536), F32)] + cx["scratch"],
            compiler_params=pltpu.CompilerParams(dimension_semantics=("arbitrary",), vmem_limit_bytes=VMEM_LIMIT,
                                                 has_side_effects=comm is not None),
            name="ssd_bwd" + cx["tag"],
        )(ps, ps, ss, cw, cb, prm, nw, st, dob, *comm_args)

    return call


def _ret_consts(h):
    lg = math.log(1.0 - 2.0 ** (-5.0 - h))
    r = _iota2((CH, CH), 0)
    c = _iota2((CH, CH), 1)
    rel = (r - c).astype(F32)
    dmat = jnp.where(r >= c, jnp.exp(jnp.maximum(rel, 0.0) * lg), 0.0)
    idx = _iota2((CH, 1), 0).astype(F32)
    qdec = jnp.exp((idx + 1.0) * lg)
    kdec = jnp.exp((CH - 1.0 - idx) * lg)
    cdec = math.exp(CH * lg)
    return dmat, qdec, kdec, cdec


def _ret_batch(pr_ref, cc_ref, ss_ref, ncb):
    pairs = [(c, h) for c in range(ncb) for h in range(RET_H)]

    def st(off):
        return jnp.stack([pr_ref[c * CH:(c + 1) * CH, off + h * 128:off + (h + 1) * 128] for c, h in pairs])

    cc = jnp.stack([cc_ref[c * CH:(c + 1) * CH, :] for c, _ in pairs])
    ss = jnp.stack([ss_ref[c * CH:(c + 1) * CH, :] for c, _ in pairs])
    consts = [_ret_consts(h) for h in range(RET_H)]
    dmat = jnp.stack([consts[h][0] for _, h in pairs])
    qdec = jnp.stack([consts[h][1] for _, h in pairs])
    kdec = jnp.stack([consts[h][2] for _, h in pairs])
    cdec = jnp.stack([jnp.full((1, 1), consts[h][3], F32) for h in range(RET_H)])
    q = _rot(st(0), cc, ss)
    k = _rot(st(512), cc, ss) * (RET_D ** -0.5)
    return dict(q=q, k=k, v=st(1024), z=st(1536), cc=cc, ss=ss, dmat=dmat, qdec=qdec, kdec=kdec, cdec=cdec,
                s=_bdot(q, k, _NT) * dmat)


def _rot(t, cc, ss):
    return t * cc + pltpu.roll(t, 64, axis=t.ndim - 1) * ss


def _rot_bwd(d, cc, ss):
    return d * cc + pltpu.roll(d * ss, 64, axis=d.ndim - 1)


def _make_ret_fwd(seq, tb):
    ncb = tb // CH
    nb = seq // tb

    def body(pr_ref, cc_ref, ss_ref, nw_ref, oc_ref, st_ref, r_scr):
        @pl.when(pl.program_id(0) == 0)
        def _():
            r_scr[...] = jnp.zeros_like(r_scr)

        d = _ret_batch(pr_ref, cc_ref, ss_ref, ncb)
        kd = d["k"] * d["kdec"]
        for c in range(ncb):
            bs = slice(c * RET_H, (c + 1) * RET_H)
            rs = r_scr[...]
            st_ref[c] = rs
            r_scr[...] = rs * d["cdec"] + _bdot(kd[bs], d["v"][bs], _TN)
        r_prev = st_ref[...].reshape(ncb * RET_H, 128, 128)
        o = _bdot(d["s"], d["v"], _NN) + _bdot(d["q"], r_prev, _NN) * d["qdec"]
        _, _, y = _rms_fwd(o, nw_ref[0:1, :], RET_D)
        out = y * _silu(d["z"])
        for c in range(ncb):
            for h in range(RET_H):
                oc_ref[c * CH:(c + 1) * CH, h * 128:(h + 1) * 128] = out[c * RET_H + h].astype(oc_ref.dtype)

    def call(pr, cc, ss, nw):
        return pl.pallas_call(
            body,
            grid=(nb,),
            in_specs=[
                pl.BlockSpec((tb, 2048), lambda i: (i, 0)),
                pl.BlockSpec((tb, 128), lambda i: (i, 0)),
                pl.BlockSpec((tb, 128), lambda i: (i, 0)),
                pl.BlockSpec((8, 128), lambda i: (0, 0)),
            ],
            out_specs=[
                pl.BlockSpec((tb, 512), lambda i: (i, 0)),
                pl.BlockSpec((ncb, RET_H, 128, 128), lambda i: (i, 0, 0, 0)),
            ],
            out_shape=[
                jax.ShapeDtypeStruct((seq, 512), BF16),
                jax.ShapeDtypeStruct((seq // CH, RET_H, 128, 128), F32),
            ],
            scratch_shapes=[pltpu.VMEM((RET_H, 128, 128), F32)],
            compiler_params=pltpu.CompilerParams(dimension_semantics=("arbitrary",), vmem_limit_bytes=VMEM_LIMIT),
            name="ret_fwd",
        )(pr, cc, ss, nw)

    return call


def _make_ret_bwd(seq, tb):
    ncb = tb // CH
    nb = seq // tb

    def body(pr_ref, cc_ref, ss_ref, nw_ref, st_ref, doc_ref, dpr_ref, dnw_ref, dr_scr):
        @pl.when(pl.program_id(0) == 0)
        def _():
            dr_scr[...] = jnp.zeros_like(dr_scr)
            dnw_ref[...] = jnp.zeros_like(dnw_ref)

        nw = nw_ref[0:1, :]
        scale = RET_D ** -0.5
        n = ncb * RET_H
        d = _ret_batch(pr_ref, cc_ref, ss_ref, ncb)
        q, k, v, z, s = d["q"], d["k"], d["v"], d["z"], d["s"]
        r_prev = st_ref[...].reshape(n, 128, 128)
        o = _bdot(s, v, _NN) + _bdot(q, r_prev, _NN) * d["qdec"]
        doc = jnp.stack([doc_ref[c * CH:(c + 1) * CH, h * 128:(h + 1) * 128]
                         for c in range(ncb) for h in range(RET_H)])
        on, r, y = _rms_fwd(o, nw, RET_D)
        dz = doc * y * _dsilu(z)
        do, dnw_rows = _rms_bwd(doc * _silu(z), on, r, nw, RET_D)
        dnw_acc = jnp.sum(jnp.sum(dnw_rows, axis=0), axis=0, keepdims=True)
        dqd = do * d["qdec"]
        qtd = _bdot(q, dqd, _TN)
        drn_l = [None] * ncb
        for c in reversed(range(ncb)):
            drn_l[c] = dr_scr[...]
            dr_scr[...] = qtd[c * RET_H:(c + 1) * RET_H] + d["cdec"] * drn_l[c]
        drn = jnp.concatenate(drn_l, axis=0)
        ds = _bdot(do, v, _NT) * d["dmat"]
        dq = _rot_bwd(_bdot(ds, k, _NN) + _bdot(dqd, r_prev, _NT), d["cc"], d["ss"])
        dk = _rot_bwd((_bdot(ds, q, _TN) + _bdot(v, drn, _NT) * d["kdec"]) * scale, d["cc"], d["ss"])
        dv = _bdot(s, do, _TN) + _bdot(k * d["kdec"], drn, _NN)
        for c in range(ncb):
            rows = slice(c * CH, (c + 1) * CH)
            for h in range(RET_H):
                b = c * RET_H + h
                for j, val in enumerate((dq, dk, dv, dz)):
                    dpr_ref[rows, j * 512 + h * 128:j * 512 + (h + 1) * 128] = val[b].astype(dpr_ref.dtype)
        dnw_ref[...] += jnp.where(_iota2((8, 128), 0) == 0, dnw_acc, 0.0)

    def call(pr, cc, ss, nw, st, doc):
        rev = lambda i: (nb - 1 - i, 0)
        return pl.pallas_call(
            body,
            grid=(nb,),
            in_specs=[
                pl.BlockSpec((tb, 2048), rev),
                pl.BlockSpec((tb, 128), rev),
                pl.BlockSpec((tb, 128), rev),
                pl.BlockSpec((8, 128), lambda i: (0, 0)),
                pl.BlockSpec((ncb, RET_H, 128, 128), lambda i: (nb - 1 - i, 0, 0, 0)),
                pl.BlockSpec((tb, 512), rev),
            ],
            out_specs=[
                pl.BlockSpec((tb, 2048), rev),
                pl.BlockSpec((8, 128), lambda i: (0, 0)),
            ],
            out_shape=[
                jax.ShapeDtypeStruct((seq, 2048), BF16),
                jax.ShapeDtypeStruct((8, 128), F32),
            ],
            scratch_shapes=[pltpu.VMEM((RET_H, 128, 128), F32)],
            compiler_params=pltpu.CompilerParams(dimension_semantics=("arbitrary",), vmem_limit_bytes=VMEM_LIMIT),
            name="ret_bwd",
        )(pr, cc, ss, nw, st, doc)

    return call


def _rope_tables(seq):
    half = RET_D // 2
    inv = ROPE_BASE ** (-jnp.arange(half, dtype=F32) / half)
    ang = jnp.arange(seq, dtype=jnp.int32).astype(F32)[:, None] * inv[None, :]
    cos, sin = jnp.cos(ang), jnp.sin(ang)
    return jnp.concatenate([cos, cos], axis=1), jnp.concatenate([-sin, sin], axis=1)


SEG_G, SEG_S, SEG_R, SEG_GS, SEG_SS = (0, 2048), (2048, 4608), (4608, 6656), (6656, 6784), (6784, 6912)
NP = 6912
SEGS = (SEG_G, SEG_S, SEG_R, SEG_GS, SEG_SS)


def _resident(shape):
    return pl.BlockSpec(shape, lambda i: (0,) * len(shape), pipeline_mode=pl.Buffered(1))


def _make_inproj(seq, tl):
    def body(x_ref, pn_ref, w_ref, pg_ref, ps_ref, pr_ref, gs_ref, ss_ref, ht_ref):
        x = x_ref[...]
        _, _, hn = _rms_fwd(x, pn_ref[0:1, :], D_MODEL)
        h = hn.astype(BF16)
        ht_ref[...] = hn.T.astype(BF16)
        for (a, b), o_ref in zip(SEGS, (pg_ref, ps_ref, pr_ref, gs_ref, ss_ref)):
            o_ref[...] = jnp.dot(h, w_ref[:, a:b], preferred_element_type=F32)

    def call(x, pn, w):
        row = lambda i: (i, 0)
        return pl.pallas_call(
            body,
            grid=(seq // tl,),
            in_specs=[pl.BlockSpec((tl, D_MODEL), row), _resident((8, D_MODEL)), _resident((D_MODEL, NP))],
            out_specs=[pl.BlockSpec((tl, b - a), row) for a, b in SEGS]
            + [pl.BlockSpec((D_MODEL, tl), lambda i: (0, i))],
            out_shape=[jax.ShapeDtypeStruct((seq, b - a), F32) for a, b in SEGS]
            + [jax.ShapeDtypeStruct((D_MODEL, seq), BF16)],
            compiler_params=pltpu.CompilerParams(dimension_semantics=("arbitrary",), vmem_limit_bytes=VMEM_LIMIT),
            name="inproj",
        )(x, pn, w)

    return call


def _make_outproj(seq, tl):
    def body(oa_ref, ob_ref, oc_ref, w_ref, x_ref, qn_ref, out_ref, xn_ref):
        out = (jnp.dot(oa_ref[...], w_ref[0:512, :], preferred_element_type=F32)
               + jnp.dot(ob_ref[...], w_ref[512:1536, :], preferred_element_type=F32)
               + jnp.dot(oc_ref[...], w_ref[1536:2048, :], preferred_element_type=F32))
        out_ref[...] = out
        _, _, y = _rms_fwd(out, qn_ref[0:1, :], D_MODEL)
        xn_ref[...] = x_ref[...] + y

    def call(oa, ob, oc, w, x, qn):
        row = lambda i: (i, 0)
        return pl.pallas_call(
            body,
            grid=(seq // tl,),
            in_specs=[pl.BlockSpec((tl, 512), row), pl.BlockSpec((tl, 1024), row), pl.BlockSpec((tl, 512), row),
                      _resident((2048, D_MODEL)), pl.BlockSpec((tl, D_MODEL), row), _resident((8, D_MODEL))],
            out_specs=[pl.BlockSpec((tl, D_MODEL), row), pl.BlockSpec((tl, D_MODEL), row)],
            out_shape=[jax.ShapeDtypeStruct((seq, D_MODEL), F32), jax.ShapeDtypeStruct((seq, D_MODEL), F32)],
            compiler_params=pltpu.CompilerParams(dimension_semantics=("arbitrary",), vmem_limit_bytes=VMEM_LIMIT),
            name="outproj",
        )(oa, ob, oc, w, x, qn)

    return call


def _make_loss_head(seq, tl):
    def body(y_ref, t_ref, dy_ref, loss_ref):
        @pl.when(pl.program_id(0) == 0)
        def _():
            loss_ref[...] = jnp.zeros_like(loss_ref)

        err = y_ref[...] - t_ref[...]
        dy_ref[...] = err * (1.0 / D_MODEL)
        part = jnp.sum(jnp.sum(err * err, axis=1, keepdims=True), axis=0, keepdims=True) * (0.5 / D_MODEL)
        loss_ref[...] += jnp.where((_iota2((8, 128), 0) == 0) & (_iota2((8, 128), 1) == 0), part, 0.0)

    def call(y, t):
        row = lambda i: (i, 0)
        return pl.pallas_call(
            body,
            grid=(seq // tl,),
            in_specs=[pl.BlockSpec((tl, D_MODEL), row), pl.BlockSpec((tl, D_MODEL), row)],
            out_specs=[pl.BlockSpec((tl, D_MODEL), row), pl.BlockSpec((8, 128), lambda i: (0, 0))],
            out_shape=[jax.ShapeDtypeStruct((seq, D_MODEL), F32), jax.ShapeDtypeStruct((8, 128), F32)],
            compiler_params=pltpu.CompilerParams(dimension_semantics=("arbitrary",)),
            name="loss_head",
        )(y, t)

    return call


def _make_outproj_bwd(seq, tl):
    def body(dxn_ref, out_ref, oa_ref, ob_ref, oc_ref, w_ref, qn_ref, doa_ref, dob_ref, doc_ref, dqn_ref, dw_ref):
        @pl.when(pl.program_id(0) == 0)
        def _():
            dqn_ref[...] = jnp.zeros_like(dqn_ref)
            dw_ref[...] = jnp.zeros_like(dw_ref)

        qn = qn_ref[0:1, :]
        on, r, _ = _rms_fwd(out_ref[...], qn, D_MODEL)
        dout, dqn_rows = _rms_bwd(dxn_ref[...], on, r, qn, D_MODEL)
        dqn_ref[...] += jnp.where(_iota2((8, D_MODEL), 0) == 0, jnp.sum(dqn_rows, axis=0, keepdims=True), 0.0)
        db = dout.astype(BF16)
        nt = (((1,), (1,)), ((), ()))
        tn = (((0,), (0,)), ((), ()))
        doa_ref[...] = lax.dot_general(db, w_ref[0:512, :], nt, preferred_element_type=F32)
        dob_ref[...] = lax.dot_general(db, w_ref[512:1536, :], nt, preferred_element_type=F32)
        doc_ref[...] = lax.dot_general(db, w_ref[1536:2048, :], nt, preferred_element_type=F32)
        dw_ref[0:512, :] += lax.dot_general(oa_ref[...], db, tn, preferred_element_type=F32)
        dw_ref[512:1536, :] += lax.dot_general(ob_ref[...], db, tn, preferred_element_type=F32)
        dw_ref[1536:2048, :] += lax.dot_general(oc_ref[...], db, tn, preferred_element_type=F32)

    def call(dxn, out, oa, ob, oc, w, qn):
        row = lambda i: (i, 0)
        const = lambda i: (0, 0)
        return pl.pallas_call(
            body,
            grid=(seq // tl,),
            in_specs=[pl.BlockSpec((tl, D_MODEL), row), pl.BlockSpec((tl, D_MODEL), row),
                      pl.BlockSpec((tl, 512), row), pl.BlockSpec((tl, 1024), row), pl.BlockSpec((tl, 512), row),
                      _resident((2048, D_MODEL)), _resident((8, D_MODEL))],
            out_specs=[pl.BlockSpec((tl, 512), row), pl.BlockSpec((tl, 1024), row), pl.BlockSpec((tl, 512), row),
                       pl.BlockSpec((8, D_MODEL), const), pl.BlockSpec((2048, D_MODEL), const)],
            out_shape=[jax.ShapeDtypeStruct((seq, 512), F32), jax.ShapeDtypeStruct((seq, 1024), F32),
                       jax.ShapeDtypeStruct((seq, 512), F32), jax.ShapeDtypeStruct((8, D_MODEL), F32),
                       jax.ShapeDtypeStruct((2048, D_MODEL), F32)],
            compiler_params=pltpu.CompilerParams(dimension_semantics=("arbitrary",), vmem_limit_bytes=VMEM_LIMIT),
            name="outproj_bwd",
        )(dxn, out, oa, ob, oc, w, qn)

    return call


def _make_inproj_bwd_dx(seq, tl):
    def body(dg_ref, ds_ref, dr_ref, dgs_ref, dss_ref, w_ref, x_ref, pn_ref, dxn_ref, dx_ref, dpn_ref):
        @pl.when(pl.program_id(0) == 0)
        def _():
            dpn_ref[...] = jnp.zeros_like(dpn_ref)

        nt = (((1,), (1,)), ((), ()))
        dh = jnp.zeros((tl, D_MODEL), F32)
        for (a, b), d_ref in zip(SEGS, (dg_ref, ds_ref, dr_ref, dgs_ref, dss_ref)):
            dh = dh + lax.dot_general(d_ref[...], w_ref[:, a:b], nt, preferred_element_type=F32)
        pn = pn_ref[0:1, :]
        on, r, _ = _rms_fwd(x_ref[...], pn, D_MODEL)
        dx, dpn_rows = _rms_bwd(dh, on, r, pn, D_MODEL)
        dx_ref[...] = dx + dxn_ref[...]
        dpn_ref[...] += jnp.where(_iota2((8, D_MODEL), 0) == 0, jnp.sum(dpn_rows, axis=0, keepdims=True), 0.0)

    def call(dg, ds, dr, dgs, dss, w, x, pn, dxn, comm=None, comm_args=()):
        row = lambda i: (i, 0)
        cx = _exchange_specs(comm)
        return pl.pallas_call(
            _with_exchange(body, comm, 9, 2, seq // tl),
            grid=(seq // tl,),
            in_specs=[pl.BlockSpec((tl, b - a), row) for a, b in SEGS]
            + [_resident((D_MODEL, NP)), pl.BlockSpec((tl, D_MODEL), row), _resident((8, D_MODEL)),
               pl.BlockSpec((tl, D_MODEL), row)] + cx["specs"],
            out_specs=[pl.BlockSpec((tl, D_MODEL), row), pl.BlockSpec((8, D_MODEL), lambda i: (0, 0))] + cx["specs"],
            out_shape=[jax.ShapeDtypeStruct((seq, D_MODEL), F32), jax.ShapeDtypeStruct((8, D_MODEL), F32)]
            + cx["out_shape"],
            scratch_shapes=cx["scratch"],
            compiler_params=pltpu.CompilerParams(dimension_semantics=("arbitrary",), vmem_limit_bytes=VMEM_LIMIT,
                                                 has_side_effects=comm is not None),
            name="inproj_bwd_dx" + cx["tag"],
        )(dg, ds, dr, dgs, dss, w, x, pn, dxn, *comm_args)

    return call


def _make_inproj_bwd_dw(seq, tl, width, tn, name):
    def body(ht_ref, d_ref, dw_ref):
        @pl.when(pl.program_id(1) == 0)
        def _():
            dw_ref[...] = jnp.zeros_like(dw_ref)

        dw_ref[...] += jnp.dot(ht_ref[...], d_ref[...], preferred_element_type=F32)

    def call(ht, d):
        return pl.pallas_call(
            body,
            grid=(width // tn, seq // tl),
            in_specs=[pl.BlockSpec((D_MODEL, tl), lambda j, i: (0, i)), pl.BlockSpec((tl, tn), lambda j, i: (i, j))],
            out_specs=pl.BlockSpec((D_MODEL, tn), lambda j, i: (0, j)),
            out_shape=jax.ShapeDtypeStruct((D_MODEL, width), F32),
            compiler_params=pltpu.CompilerParams(dimension_semantics=("arbitrary", "arbitrary"),
                                                 vmem_limit_bytes=VMEM_LIMIT),
            name=name,
        )(ht, d)

    return call


ADAM_LR, ADAM_B1, ADAM_B2, ADAM_EPS, ADAM_WD, ADAM_STEP = 0.001, 0.9, 0.999, 1e-08, 0.01, 10


def _adam_math(w, g, m, v):
    m = ADAM_B1 * m + (1.0 - ADAM_B1) * g
    v = ADAM_B2 * v + (1.0 - ADAM_B2) * (g * g)
    m_hat = m / (1.0 - ADAM_B1 ** ADAM_STEP)
    v_hat = v / (1.0 - ADAM_B2 ** ADAM_STEP)
    delta = -ADAM_LR * (m_hat / (jnp.sqrt(v_hat) + ADAM_EPS) + ADAM_WD * w)
    return delta, m, v


def _adamw(w, g, m, v, name):
    shape = w.shape
    cols = shape[-1]
    rows = w.size // cols
    tr = rows if rows <= 512 else 256
    assert rows % tr == 0

    def body(w_ref, g_ref, m_ref, v_ref, d_ref, mo_ref, vo_ref):
        d_ref[...], mo_ref[...], vo_ref[...] = _adam_math(w_ref[...], g_ref[...], m_ref[...], v_ref[...])

    spec = pl.BlockSpec((tr, cols), lambda i: (i, 0))
    outs = pl.pallas_call(
        body,
        grid=(rows // tr,),
        in_specs=[spec] * 4,
        out_specs=[spec] * 3,
        out_shape=[jax.ShapeDtypeStruct((rows, cols), F32)] * 3,
        compiler_params=pltpu.CompilerParams(dimension_semantics=("arbitrary",), vmem_limit_bytes=VMEM_LIMIT),
        name=name,
    )(*[a.reshape(rows, cols) for a in (w, g, m, v)])
    return (g,) + tuple(o.reshape(shape) for o in outs)


def _adamw_pairs(w, mine, theirs, m, v, name):
    na, r, cols = w.shape
    assert na == 2
    tr = 256
    assert r % tr == 0

    def body(w_ref, a0_ref, b0_ref, a1_ref, b1_ref, m_ref, v_ref, g_ref, d_ref, mo_ref, vo_ref):
        g = jnp.where(pl.program_id(0) == 0, a0_ref[...] + b0_ref[...], a1_ref[...] + b1_ref[...])
        g_ref[...] = g
        d_ref[...], mo_ref[...], vo_ref[...] = _adam_math(w_ref[...], g, m_ref[...], v_ref[...])

    full = pl.BlockSpec((None, tr, cols), lambda a, i: (a, i, 0))
    one = pl.BlockSpec((None, tr, cols), lambda a, i: (0, i, 0))
    return pl.pallas_call(
        body,
        grid=(na, r // tr),
        in_specs=[full, one, one, one, one, full, full],
        out_specs=[full] * 4,
        out_shape=[jax.ShapeDtypeStruct(w.shape, F32)] * 4,
        compiler_params=pltpu.CompilerParams(dimension_semantics=("arbitrary",) * 2, vmem_limit_bytes=VMEM_LIMIT),
        name=name,
    )(w, mine[0], theirs[0], mine[1], theirs[1], m, v)


MESH = pl.DeviceIdType.MESH
ANY = pl.BlockSpec(memory_space=pl.ANY)
CHIP_REL = ((1, 0), (0, 1), (1, 1))


def _flip(v, d):
    return 1 - v if d else v


def _ag_chips(arrs, name):
    n = len(arrs)

    def body(*refs):
        ins, outs = refs[:n], refs[n:2 * n]
        send_sems, recv_sems, loc_sems = refs[2 * n:]
        x, y, c = lax.axis_index("x"), lax.axis_index("y"), lax.axis_index("c")
        me = 2 * x + y

        def remote(a, k, slot):
            dx, dy = CHIP_REL[k]
            return pltpu.make_async_remote_copy(
                src_ref=ins[a], dst_ref=outs[a].at[slot], send_sem=send_sems.at[a * 3 + k],
                recv_sem=recv_sems.at[a * 3 + k], device_id=(_flip(x, dx), _flip(y, dy), c), device_id_type=MESH)

        local = [pltpu.make_async_copy(ins[a], outs[a].at[me], loc_sems.at[a]) for a in range(n)]
        for cp in local:
            cp.start()
        for a in range(n):
            for k in range(3):
                remote(a, k, me).start()
        for a in range(n):
            for k, (dx, dy) in enumerate(CHIP_REL):
                remote(a, k, 2 * _flip(x, dx) + _flip(y, dy)).wait_recv()
        for a in range(n):
            for k in range(3):
                remote(a, k, me).wait_send()
        for cp in local:
            cp.wait()

    return pl.pallas_call(
        body,
        in_specs=[ANY] * n,
        out_specs=[ANY] * n,
        out_shape=[jax.ShapeDtypeStruct((4,) + a.shape, a.dtype) for a in arrs],
        scratch_shapes=[pltpu.SemaphoreType.DMA((3 * n,)), pltpu.SemaphoreType.DMA((3 * n,)),
                        pltpu.SemaphoreType.DMA((n,))],
        compiler_params=pltpu.CompilerParams(has_side_effects=True),
        name=name,
    )(*arrs)


class _ChipExchange:
    def __init__(self, kind, arrs):
        self.kind, self.n = kind, len(arrs)
        if kind == "gather":
            self.out_shape = [jax.ShapeDtypeStruct((4,) + a.shape, a.dtype) for a in arrs]
        else:
            self.out_shape = [jax.ShapeDtypeStruct((3,) + a.shape[1:], a.dtype) for a in arrs]
        self.scratch = [pltpu.SemaphoreType.DMA((4 * self.n,)), pltpu.SemaphoreType.DMA((4 * self.n,))]

    def _copies(self, ins, outs, sems):
        send_sems, recv_sems = sems
        x, y, c = lax.axis_index("x"), lax.axis_index("y"), lax.axis_index("c")
        me = 2 * x + y
        pairs = []
        for a in range(self.n):
            for k, (dx, dy) in enumerate(CHIP_REL):
                px, py = _flip(x, dx), _flip(y, dy)
                sem = dict(send_sem=send_sems.at[4 * a + k], recv_sem=recv_sems.at[4 * a + k],
                           device_id=(px, py, c), device_id_type=MESH)
                if self.kind == "gather":
                    out = pltpu.make_async_remote_copy(src_ref=ins[a], dst_ref=outs[a].at[me], **sem)
                    inc = pltpu.make_async_remote_copy(src_ref=ins[a], dst_ref=outs[a].at[2 * px + py], **sem)
                else:
                    out = pltpu.make_async_remote_copy(src_ref=ins[a].at[2 * px + py], dst_ref=outs[a].at[k], **sem)
                    inc = out
                pairs.append((out, inc))
            if self.kind == "gather":
                own = pltpu.make_async_remote_copy(
                    src_ref=ins[a], dst_ref=outs[a].at[me], send_sem=send_sems.at[4 * a + 3],
                    recv_sem=recv_sems.at[4 * a + 3], device_id=(x, y, 1 - c), device_id_type=MESH)
                pairs.append((own, own))
        return pairs

    def start(self, ins, outs, sems):
        for out, _ in self._copies(ins, outs, sems):
            out.start()

    def finish(self, ins, outs, sems):
        pairs = self._copies(ins, outs, sems)
        for _, inc in pairs:
            inc.wait_recv()
        for out, _ in pairs:
            out.wait_send()


def _with_exchange(body, comm, n_in, n_out, nb):
    if comm is None:
        return body

    def wrapped(*refs):
        ins = refs[:n_in]
        c_in = refs[n_in:n_in + comm.n]
        outs = refs[n_in + comm.n:n_in + comm.n + n_out]
        c_out = refs[n_in + comm.n + n_out:n_in + 2 * comm.n + n_out]
        rest = refs[n_in + 2 * comm.n + n_out:]
        scratch, sems = rest[:len(rest) - 2], rest[len(rest) - 2:]

        @pl.when(pl.program_id(0) == 0)
        def _():
            comm.start(c_in, c_out, sems)

        body(*ins, *outs, *scratch)

        @pl.when(pl.program_id(0) == nb - 1)
        def _():
            comm.finish(c_in, c_out, sems)

    return wrapped


def _exchange_specs(comm):
    if comm is None:
        return dict(specs=[], out_shape=[], scratch=[], tag="")
    return dict(specs=[pl.BlockSpec(memory_space=pl.ANY)] * comm.n, out_shape=list(comm.out_shape),
                scratch=list(comm.scratch), tag="_" + comm.kind)


def _half(ref_or_shape, half):
    r = ref_or_shape[-2] // 2
    return pl.ds(half * r, r)


def _ag_rows(arrs, name):
    n = len(arrs)

    def body(*refs):
        ins, outs = refs[:n], refs[n:2 * n]
        send_sems, recv_sems, fsend_sems, frecv_sems, loc_sems = refs[2 * n:]
        x, y, c = lax.axis_index("x"), lax.axis_index("y"), lax.axis_index("c")
        me = 2 * x + y
        sib = (x, y, 1 - c)

        def chip_of(k):
            dx, dy = CHIP_REL[k]
            return _flip(x, dx), _flip(y, dy)

        def ici(a, k, slot):
            px, py = chip_of(k)
            rows = _half(arrs[a].shape, c)
            return pltpu.make_async_remote_copy(
                src_ref=ins[a].at[:, rows, :], dst_ref=outs[a].at[slot, :, rows, :], send_sem=send_sems.at[a * 3 + k],
                recv_sem=recv_sems.at[a * 3 + k], device_id=(px, py, c), device_id_type=MESH)

        def fwd(a, k, half):
            px, py = chip_of(k)
            blk = outs[a].at[2 * px + py, :, _half(arrs[a].shape, half), :]
            return pltpu.make_async_remote_copy(
                src_ref=blk, dst_ref=blk, send_sem=fsend_sems.at[a * 3 + k], recv_sem=frecv_sems.at[a * 3 + k],
                device_id=sib, device_id_type=MESH)

        own = [pltpu.make_async_remote_copy(src_ref=ins[a], dst_ref=outs[a].at[me], send_sem=loc_sems.at[a],
                                            recv_sem=loc_sems.at[n + a], device_id=sib, device_id_type=MESH)
               for a in range(n)]
        for cp in own:
            cp.start()
        for a in range(n):
            for k in range(3):
                ici(a, k, me).start()
        for a in range(n):
            for k in range(3):
                px, py = chip_of(k)
                ici(a, k, 2 * px + py).wait_recv()
                fwd(a, k, c).start()
        for a in range(n):
            for k in range(3):
                fwd(a, k, 1 - c).wait_recv()
        for a in range(n):
            for k in range(3):
                ici(a, k, me).wait_send()
                fwd(a, k, c).wait_send()
        for cp in own:
            cp.wait()

    return pl.pallas_call(
        body,
        in_specs=[ANY] * n,
        out_specs=[ANY] * n,
        out_shape=[jax.ShapeDtypeStruct((4,) + a.shape, a.dtype) for a in arrs],
        scratch_shapes=[pltpu.SemaphoreType.DMA((3 * n,)) for _ in range(4)] + [pltpu.SemaphoreType.DMA((2 * n,))],
        compiler_params=pltpu.CompilerParams(has_side_effects=True),
        name=name,
    )(*arrs)


def _sum_chips(own, recv, chip, name):
    _, na, r, cols = own.shape
    tr = 256
    assert r % tr == 0

    def body(chip_ref, o_ref, r_ref, s_ref):
        s_ref[...] = ((o_ref[...] + r_ref[0].astype(F32)) + r_ref[1].astype(F32)) + r_ref[2].astype(F32)

    return pl.pallas_call(
        body,
        grid_spec=pltpu.PrefetchScalarGridSpec(
            num_scalar_prefetch=1,
            grid=(na, r // tr),
            in_specs=[pl.BlockSpec((None, None, tr, cols), lambda a, i, ch: (ch[0], a, i, 0)),
                      pl.BlockSpec((3, None, tr, cols), lambda a, i, ch: (0, a, i, 0))],
            out_specs=pl.BlockSpec((None, tr, cols), lambda a, i, ch: (a, i, 0))),
        out_shape=jax.ShapeDtypeStruct((na, r, cols), F32),
        compiler_params=pltpu.CompilerParams(dimension_semantics=("arbitrary",) * 2, vmem_limit_bytes=VMEM_LIMIT),
        name=name,
    )(chip, own, recv)


def _swap_sibling(arrs, name):
    n = len(arrs)

    def body(*refs):
        ins, outs = refs[:n], refs[n:2 * n]
        send_sems, recv_sems = refs[2 * n:]
        x, y, c = lax.axis_index("x"), lax.axis_index("y"), lax.axis_index("c")
        cps = [pltpu.make_async_remote_copy(src_ref=ins[a], dst_ref=outs[a], send_sem=send_sems.at[a],
                                            recv_sem=recv_sems.at[a], device_id=(x, y, 1 - c), device_id_type=MESH)
               for a in range(n)]
        for cp in cps:
            cp.start()
        for cp in cps:
            cp.wait_recv()
        for cp in cps:
            cp.wait_send()

    return pl.pallas_call(
        body,
        in_specs=[ANY] * n,
        out_specs=[ANY] * n,
        out_shape=[jax.ShapeDtypeStruct(a.shape, a.dtype) for a in arrs],
        scratch_shapes=[pltpu.SemaphoreType.DMA((n,)), pltpu.SemaphoreType.DMA((n,))],
        compiler_params=pltpu.CompilerParams(has_side_effects=True),
        name=name,
    )(*arrs)


def _allreduce_small(vec, name):
    rows = vec.shape[0]

    def body(v_ref, out_ref, gat_ref, send_sems, recv_sems):
        x, y, c = lax.axis_index("x"), lax.axis_index("y"), lax.axis_index("c")
        me = 4 * x + 2 * y + c

        def remote(k, slot):
            dx, dy, dc = (k >> 2) & 1, (k >> 1) & 1, k & 1
            return pltpu.make_async_remote_copy(
                src_ref=v_ref, dst_ref=gat_ref.at[slot], send_sem=send_sems.at[k - 1], recv_sem=recv_sems.at[k - 1],
                device_id=(_flip(x, dx), _flip(y, dy), _flip(c, dc)), device_id_type=MESH)

        gat_ref[me] = v_ref[...]
        for k in range(1, 8):
            remote(k, me).start()
        for k in range(1, 8):
            dx, dy, dc = (k >> 2) & 1, (k >> 1) & 1, k & 1
            remote(k, 4 * _flip(x, dx) + 2 * _flip(y, dy) + _flip(c, dc)).wait_recv()
        for k in range(1, 8):
            remote(k, me).wait_send()
        acc = gat_ref[0]
        for j in range(1, 8):
            acc = acc + gat_ref[j]
        out_ref[...] = acc

    vm = pl.BlockSpec(memory_space=pltpu.VMEM)
    return pl.pallas_call(
        body,
        in_specs=[vm],
        out_specs=vm,
        out_shape=jax.ShapeDtypeStruct(vec.shape, F32),
        scratch_shapes=[pltpu.VMEM((8, rows, 128), F32), pltpu.SemaphoreType.DMA((7,)), pltpu.SemaphoreType.DMA((7,))],
        compiler_params=pltpu.CompilerParams(has_side_effects=True),
        name=name,
    )(vec)


def _pad8(v, width, lane0=0):
    v = v.reshape(1, -1) if v.ndim == 1 else v
    return jnp.zeros((8, width), F32).at[:v.shape[0], lane0:lane0 + v.shape[1]].set(v.astype(F32))


def _relayout_w_in(w):
    z = lambda n: jnp.zeros(w.shape[:-1] + (n,), w.dtype)
    return jnp.concatenate([w[..., 0:2048], w[..., 2056:4616], w[..., 4632:6680],
                            w[..., 2048:2056], z(120), w[..., 4616:4632], z(112)], axis=-1)


def _unlayout_dw_in(dg, ds, dr, dgs, dss):
    return jnp.concatenate([dg, dgs[:, 0:8], ds, dss[:, 0:16], dr], axis=1)


TB = 256
TL = 256
TK = 1024


def kernel(x, pre_norm, post_norm, w_in, gdn_conv, gdn_A_log, gdn_dt_bias, gdn_norm, ssd_conv, ssd_conv_b, ssd_A_log, ssd_dt_bias, ssd_D, ssd_norm, ret_norm, w_out, loss_target, m_pre_norm, m_post_norm, m_w_in, m_gdn_conv, m_gdn_A_log, m_gdn_dt_bias, m_gdn_norm, m_ssd_conv, m_ssd_conv_b, m_ssd_A_log, m_ssd_dt_bias, m_ssd_D, m_ssd_norm, m_ret_norm, m_w_out, v_pre_norm, v_post_norm, v_w_in, v_gdn_conv, v_gdn_A_log, v_gdn_dt_bias, v_gdn_norm, v_ssd_conv, v_ssd_conv_b, v_ssd_A_log, v_ssd_dt_bias, v_ssd_D, v_ssd_norm, v_ret_norm, v_w_out):
    seq = x.shape[1]
    chip = 2 * lax.axis_index("x") + lax.axis_index("y")
    x0 = x[0]

    wi_b, wo_b = w_in.astype(BF16), w_out.astype(BF16)
    wi0_g, wo0_g = _ag_rows([wi_b[0:1], wo_b[0:1]], "ag_weights")
    gcv_g, scv_g = _ag_chips([gdn_conv, ssd_conv], "ag_conv")
    full_w_in = lambda g: _relayout_w_in(jnp.transpose(g, (1, 0, 2)).reshape(D_MODEL, N_IN))
    wp = [full_w_in(wi0_g[:, 0]), None]
    wo = [wo0_g[:, 0].reshape(2048, D_MODEL), None]
    ag1 = _ChipExchange("gather", [wi_b[1], wo_b[1]])
    gcv = jnp.transpose(gcv_g, (1, 2, 0, 3)).reshape(DEPTH, CONV_W, 1536)
    scv = jnp.transpose(scv_g, (1, 2, 0, 3)).reshape(DEPTH, CONV_W, 1536)
    rope_c, rope_s = _rope_tables(seq)

    saved = []
    xc = x0
    for l in range(DEPTH):
        p = dict(
            pn=_pad8(pre_norm[l], D_MODEL), qn=_pad8(post_norm[l], D_MODEL),
            g_cw=_pad8(gcv[l], 1536), g_prm=_pad8(jnp.stack([gdn_A_log[l], gdn_dt_bias[l]]), 128, 4),
            g_nw=_pad8(gdn_norm[l], 128),
            s_cw=_pad8(scv[l], 1536), s_cb=_pad8(ssd_conv_b[l], 1536),
            s_prm=_pad8(jnp.stack([ssd_A_log[l], ssd_dt_bias[l], ssd_D[l]]), 128), s_nw=_pad8(ssd_norm[l], SSD_W),
            r_nw=_pad8(ret_norm[l], 128))
        pg, ps, pr, gs, ss, ht = _make_inproj(seq, TL)(xc, p["pn"], wp[l])
        if l == 0:
            oa, stg, tig, uwg, wi1_g, wo1_g = _make_gdn_fwd(seq, TB)(
                pg, gs, p["g_cw"], p["g_prm"], p["g_nw"], comm=ag1, comm_args=(wi_b[1], wo_b[1]))
            wp[1], wo[1] = full_w_in(wi1_g), wo1_g.reshape(2048, D_MODEL)
        else:
            oa, stg, tig, uwg = _make_gdn_fwd(seq, TB)(pg, gs, p["g_cw"], p["g_prm"], p["g_nw"])
        ob, sts = _make_ssd_fwd(seq, TB)(ps, ss, p["s_cw"], p["s_cb"], p["s_prm"], p["s_nw"])
        oc, str_ = _make_ret_fwd(seq, TB)(pr, rope_c, rope_s, p["r_nw"])
        out, xn = _make_outproj(seq, TL)(oa, ob, oc, wo[l], xc, p["qn"])
        saved.append(dict(p=p, x=xc, ht=ht, pg=pg, ps=ps, pr=pr, gs=gs, ss=ss, stg=stg, tig=tig, uwg=uwg, sts=sts, str=str_,
                          oa=oa, ob=ob, oc=oc, out=out))
        xc = xn

    dxn, lossp = _make_loss_head(seq, TL)(xc, loss_target[0])

    small = [None] * DEPTH
    gin, gout, q_in, q_out = ([None] * DEPTH for _ in range(4))

    def per_chip(dwi_l, dwo_l):
        return (jnp.transpose(dwi_l.reshape(D_MODEL, 4, N_IN // 4), (1, 0, 2)), dwo_l.reshape(4, 512, D_MODEL))

    for l in reversed(range(DEPTH)):
        s = saved[l]
        p = s["p"]
        doa, dob, doc, dqn, dwo_l = _make_outproj_bwd(seq, TL)(dxn, s["out"], s["oa"], s["ob"], s["oc"], wo[l], p["qn"])
        dpg, dgs, dcw_g, dprm_g, dnw_g = _make_gdn_bwd(seq, TB)(s["pg"], s["gs"], p["g_cw"], p["g_prm"], p["g_nw"],
                                                                s["stg"], s["tig"], s["uwg"], doa)
        ssd_args = (s["ps"], s["ss"], p["s_cw"], p["s_cb"], p["s_prm"], p["s_nw"], s["sts"], dob)
        if l == 0:
            payload = (gin[1].astype(BF16), gout[1].astype(BF16))
            dps, dss, dcw_s, dcb_s, dprm_s, dnw_s, q_in[1], q_out[1] = _make_ssd_bwd(seq, TB)(
                *ssd_args, comm=_ChipExchange("scatter", payload), comm_args=payload)
        else:
            dps, dss, dcw_s, dcb_s, dprm_s, dnw_s = _make_ssd_bwd(seq, TB)(*ssd_args)
        dpr, dnw_r = _make_ret_bwd(seq, TB)(s["pr"], rope_c, rope_s, p["r_nw"], s["str"], doc)
        dws = [_make_inproj_bwd_dw(seq, TK, d.shape[1], tn, f"inproj_bwd_dw{i}")(s["ht"], d)
               for i, (d, tn) in enumerate(((dpg, 1024), (dps, 1280), (dpr, 1024),
                                            (jnp.concatenate([dgs, dss], axis=1), 256)))]
        gin[l], gout[l] = per_chip(_unlayout_dw_in(dws[0], dws[1], dws[2], dws[3][:, 0:128], dws[3][:, 128:256]), dwo_l)
        dx_args = (dpg, dps, dpr, dgs, dss, wp[l], s["x"], p["pn"], dxn)
        if l == 0:
            payload = (gin[0].astype(BF16), gout[0].astype(BF16))
            dx, dpn, q_in[0], q_out[0] = _make_inproj_bwd_dx(seq, TL)(
                *dx_args, comm=_ChipExchange("scatter", payload), comm_args=payload)
        else:
            dx, dpn = _make_inproj_bwd_dx(seq, TL)(*dx_args)
        small[l] = [dpn[0], dqn[0], dcw_g[0:4].reshape(-1), dprm_g[0, 4:8], dprm_g[1, 4:8], dnw_g[0],
                    dcw_s[0:4].reshape(-1), dcb_s[0], dprm_s[0, 0:16], dprm_s[1, 0:16], dprm_s[2, 0:16],
                    dnw_s[0], dnw_r[0]]
        dxn = dx
    grad_x = dxn[None]

    sizes = [a.shape[0] for a in small[0]]
    flat = jnp.concatenate(small[0] + small[1] + [lossp[0, 0:1]])
    n_flat = flat.shape[0]
    rows = -(-n_flat // 1024) * 8
    red = _allreduce_small(jnp.pad(flat, (0, rows * 128 - n_flat)).reshape(rows, 128), "allreduce_small").reshape(-1)
    per = sum(sizes)
    loss = red[2 * per]

    def pick(i):
        off = sum(sizes[:i])
        return jnp.stack([red[l * per + off:l * per + off + sizes[i]] for l in range(DEPTH)])

    g_small = dict(
        pre_norm=pick(0), post_norm=pick(1),
        gdn_conv=lax.dynamic_slice_in_dim(pick(2).reshape(DEPTH, CONV_W, 1536), chip * 384, 384, axis=2),
        gdn_A_log=pick(3), gdn_dt_bias=pick(4), gdn_norm=pick(5),
        ssd_conv=lax.dynamic_slice_in_dim(pick(6).reshape(DEPTH, CONV_W, 1536), chip * 384, 384, axis=2),
        ssd_conv_b=pick(7), ssd_A_log=pick(8), ssd_dt_bias=pick(9), ssd_D=pick(10), ssd_norm=pick(11),
        ret_norm=pick(12))

    chip1 = chip.astype(jnp.int32).reshape(1)
    s_in = [_sum_chips(gin[l][:, None], q_in[l][:, None], chip1, f"sum_chips_w_in{l}") for l in range(DEPTH)]
    s_out = [_sum_chips(gout[l][:, None], q_out[l][:, None], chip1, f"sum_chips_w_out{l}") for l in range(DEPTH)]
    t_all = _swap_sibling(s_in + s_out, "swap_grads")
    t_in, t_out = t_all[:DEPTH], t_all[DEPTH:]

    weights = dict(pre_norm=pre_norm, post_norm=post_norm, w_in=w_in, gdn_conv=gdn_conv, gdn_A_log=gdn_A_log,
                   gdn_dt_bias=gdn_dt_bias, gdn_norm=gdn_norm, ssd_conv=ssd_conv, ssd_conv_b=ssd_conv_b,
                   ssd_A_log=ssd_A_log, ssd_dt_bias=ssd_dt_bias, ssd_D=ssd_D, ssd_norm=ssd_norm, ret_norm=ret_norm,
                   w_out=w_out)
    ms = dict(pre_norm=m_pre_norm, post_norm=m_post_norm, w_in=m_w_in, gdn_conv=m_gdn_conv, gdn_A_log=m_gdn_A_log,
              gdn_dt_bias=m_gdn_dt_bias, gdn_norm=m_gdn_norm, ssd_conv=m_ssd_conv, ssd_conv_b=m_ssd_conv_b,
              ssd_A_log=m_ssd_A_log, ssd_dt_bias=m_ssd_dt_bias, ssd_D=m_ssd_D, ssd_norm=m_ssd_norm,
              ret_norm=m_ret_norm, w_out=m_w_out)
    vs = dict(pre_norm=v_pre_norm, post_norm=v_post_norm, w_in=v_w_in, gdn_conv=v_gdn_conv, gdn_A_log=v_gdn_A_log,
              gdn_dt_bias=v_gdn_dt_bias, gdn_norm=v_gdn_norm, ssd_conv=v_ssd_conv, ssd_conv_b=v_ssd_conv_b,
              ssd_A_log=v_ssd_A_log, ssd_dt_bias=v_ssd_dt_bias, ssd_D=v_ssd_D, ssd_norm=v_ssd_norm,
              ret_norm=v_ret_norm, w_out=v_w_out)
    names = list(weights)
    res = {}
    for nme in names:
        if nme == "w_in":
            res[nme] = _adamw_pairs(w_in, s_in, t_in, m_w_in, v_w_in, "adamw_w_in")
        elif nme == "w_out":
            res[nme] = _adamw_pairs(w_out, s_out, t_out, m_w_out, v_w_out, "adamw_w_out")
        else:
            res[nme] = _adamw(weights[nme], g_small[nme], ms[nme], vs[nme], "adamw_" + nme)
    return (loss, grad_x, *[res[n][0] for n in names], *[res[n][1] for n in names],
            *[res[n][2] for n in names], *[res[n][3] for n in names])
```

```python
import functools
import math

import jax
import jax.numpy as jnp
from jax import lax
from jax.experimental import pallas as pl
from jax.experimental.pallas import tpu as pltpu

F32 = jnp.float32
BF16 = jnp.bfloat16
HI = lax.Precision.HIGHEST

D_MODEL = 1024
DEPTH = 2
CH = 64
CONV_W = 4
EPS = 1e-6
GDN_H, GDN_D = 4, 128
SSD_H, SSD_P, SSD_N, SSD_G = 16, 64, 128, 2
SSD_W = SSD_H * SSD_P
RET_H, RET_D = 4, 128
ROPE_BASE = 10000.0
N_IN = 6680
NEG = -1e30

VMEM_LIMIT = 56 * 1024 * 1024


def _dot(a, b):
    return jnp.dot(a.astype(BF16), b.astype(BF16), preferred_element_type=F32)


def _dot_nt(a, b):
    return lax.dot_general(a.astype(BF16), b.astype(BF16), (((1,), (1,)), ((), ())), preferred_element_type=F32)


def _dot_tn(a, b):
    return lax.dot_general(a.astype(BF16), b.astype(BF16), (((0,), (0,)), ((), ())), preferred_element_type=F32)


def _split(a):
    hi = a.astype(BF16)
    return hi, (a - hi.astype(F32)).astype(BF16)


def _dot01l(m, v):
    vh, vl = _split(v)
    mb = m.astype(BF16)
    return jnp.dot(mb, vh, preferred_element_type=F32) + jnp.dot(mb, vl, preferred_element_type=F32)


def _dot01r(v, m):
    vh, vl = _split(v)
    mb = m.astype(BF16)
    return jnp.dot(vh, mb, preferred_element_type=F32) + jnp.dot(vl, mb, preferred_element_type=F32)


def _sigmoid(x):
    return jax.nn.sigmoid(x)


def _silu(x):
    return x * _sigmoid(x)


def _dsilu(x):
    s = _sigmoid(x)
    return s * (1.0 + x * (1.0 - s))


def _softplus(x):
    return jnp.maximum(x, 0.0) + jnp.log1p(jnp.exp(-jnp.abs(x)))


def _iota2(shape, dim):
    return lax.broadcasted_iota(jnp.int32, shape, dim)


def _chunk_tri(tb, upper=False):
    r = _iota2((tb, tb), 0)
    c = _iota2((tb, tb), 1)
    same = jnp.right_shift(r, 6) == jnp.right_shift(c, 6)
    return (same & ((c >= r) if upper else (c <= r))).astype(F32)


def _masks():
    r = _iota2((CH, CH), 0)
    c = _iota2((CH, CH), 1)
    return r >= c, r > c, (r == c).astype(F32)


def _put_lane(col, lane_idx, width=128):
    lane = _iota2((col.shape[0], width), 1)
    return jnp.where(lane == lane_idx, col, 0.0)


def _conv_taps(raw, halo8, tb):
    ext = jnp.concatenate([halo8, raw], axis=0)
    return [raw] + [pltpu.roll(ext, s, axis=0)[8:] for s in (1, 2, 3)]


def _conv_back(dpre, nxt8, tb):
    ext = jnp.concatenate([dpre, nxt8], axis=0)
    return [dpre] + [pltpu.roll(ext, tb + 8 - s, axis=0)[:tb] for s in (1, 2, 3)]


def _rms_fwd(o, w, n):
    r = lax.rsqrt(jnp.sum(o * o, axis=-1, keepdims=True) * (1.0 / n) + EPS)
    on = o * r
    return on, r, on * w


def _rms_bwd(dy, on, r, w, n):
    don = dy * w
    return r * (don - on * (jnp.sum(don * on, axis=-1, keepdims=True) * (1.0 / n))), dy * on


def _put_cols(v, g, gw):
    z = jnp.zeros_like(v)
    return jnp.concatenate([v, z] if g == 0 else [z, v], axis=1)


def _gdn_common(pg_ref, halo8, sm, cw, prm, tb):
    raw = pg_ref[:, 0:1536]
    taps = _conv_taps(raw, halo8, tb)
    pre = taps[0] * cw[3:4, :] + taps[1] * cw[2:3, :] + taps[2] * cw[1:2, :] + taps[3] * cw[0:1, :]
    act = _silu(pre)
    beta = _sigmoid(sm)
    sp_in = sm + prm[1:2, :]
    g = -jnp.exp(prm[0:1, :]) * _softplus(sp_in)
    gc = _dot01l(_chunk_tri(tb), g)
    return raw, taps, pre, act, beta, sp_in, g, gc


_NN = (((2,), (1,)), ((0,), (0,)))
_NT = (((2,), (2,)), ((0,), (0,)))
_TN = (((1,), (1,)), ((0,), (0,)))


def _bdot(a, b, dn):
    return lax.dot_general(a.astype(BF16), b.astype(BF16), dn, preferred_element_type=F32)


def _dot3_parts(ah, al, bh, bl, dn):
    f = lambda p, q: lax.dot_general(p, q, dn, preferred_element_type=F32)
    return f(ah, bh) + (f(ah, bl) + f(al, bh))


def _bdot3(a, b, dn):
    ah, al = _split(a)
    bh, bl = _split(b)
    return _dot3_parts(ah, al, bh, bl, dn)


def _binv_unit_lower(a, eye):
    x = eye - a
    ph, pl_ = _split(a)
    for _ in range(5):
        ph, pl_ = _split(_dot3_parts(ph, pl_, ph, pl_, _NN))
        xh, xl = _split(x)
        x = x + _dot3_parts(xh, xl, ph, pl_, _NN)
    return x


def _rsum(v):
    return jnp.sum(v, axis=-1, keepdims=True)


def _gdn_batch(act, beta, gc, gct, eg_all, ncb, masks):
    causal, strict, _ = masks

    def st(fn):
        return jnp.stack([fn(c, h, slice(c * CH, (c + 1) * CH)) for c in range(ncb) for h in range(GDN_H)])

    qr = st(lambda c, h, r: act[r, h * 128:(h + 1) * 128])
    kr = st(lambda c, h, r: act[r, 512 + h * 128:512 + (h + 1) * 128])
    vh = st(lambda c, h, r: act[r, 1024 + h * 128:1024 + (h + 1) * 128])
    bh = st(lambda c, h, r: beta[r, h:h + 1])
    gcol = st(lambda c, h, r: gc[r, 4 + h:5 + h])
    grow = st(lambda c, h, r: gct[4 + h:5 + h, r])
    eg = st(lambda c, h, r: eg_all[r, 4 + h:5 + h])
    glast = st(lambda c, h, r: gc[(c + 1) * CH - 1:(c + 1) * CH, 4 + h:5 + h])
    rq = lax.rsqrt(_rsum(qr * qr) + EPS)
    rk = lax.rsqrt(_rsum(kr * kr) + EPS)
    qn = qr * rq
    kh = kr * rk
    qh = qn * (GDN_D ** -0.5)
    decay = jnp.exp(jnp.where(causal, gcol - grow, NEG))
    kb = kh * bh
    kd_scale = jnp.exp(glast - gcol)
    return dict(qn=qn, rq=rq, kh=kh, rk=rk, qh=qh, vh=vh, bh=bh, eg=eg, decay=decay, kb=kb, vb=vh * bh, kg=kb * eg,
                qg=qh * eg, kd_scale=kd_scale, kdec=kh * kd_scale, egl=jnp.exp(glast),
                a=jnp.where(strict, _bdot(kb, kh, _NT) * decay, 0.0), attn=_bdot(qh, kh, _NT) * decay)


def _make_gdn_fwd(seq, tb):
    ncb = tb // CH
    nb = seq // tb
    n = ncb * GDN_H

    def body(pg_ref, sm_ref, cw_ref, prm_ref, nw_ref, oa_ref, st_ref, ti_ref, uw_ref, s_scr, halo_scr):
        @pl.when(pl.program_id(0) == 0)
        def _():
            s_scr[...] = jnp.zeros_like(s_scr)
            halo_scr[...] = jnp.zeros_like(halo_scr)

        masks = _masks()
        sm = sm_ref[...]
        raw, _, _, act, beta, _, _, gc = _gdn_common(pg_ref, halo_scr[...], sm, cw_ref[...], prm_ref[...], tb)
        halo_scr[...] = raw[tb - 8:tb, :]
        d = _gdn_batch(act, beta, gc, gc.T, jnp.exp(gc), ncb, masks)
        t = _binv_unit_lower(d["a"], masks[2])
        sol = _bdot3(t, jnp.concatenate([d["vb"], d["kg"]], axis=2), _NN)
        ti_ref[...] = t.reshape(ncb, GDN_H, CH, CH)
        uw_ref[...] = sol.reshape(ncb, GDN_H, CH, 256)
        u, w = sol[:, :, :128], sol[:, :, 128:]
        vns = []
        for c in range(ncb):
            bs = slice(c * GDN_H, (c + 1) * GDN_H)
            s = s_scr[...]
            st_ref[c] = s
            vn = u[bs] - _bdot(w[bs], s, _NN)
            s_scr[...] = s * d["egl"][bs] + _bdot(d["kdec"][bs], vn, _TN)
            vns.append(vn)
        v_new = jnp.concatenate(vns, axis=0)
        s_prev = st_ref[...].reshape(n, 128, 128)
        o = _bdot(d["qg"], s_prev, _NN) + _bdot(d["attn"], v_new, _NN)
        _, _, y = _rms_fwd(o, nw_ref[0:1, :], GDN_D)
        for c in range(ncb):
            rows = slice(c * CH, (c + 1) * CH)
            for h in range(GDN_H):
                z = pg_ref[rows, 1536 + h * 128:1536 + (h + 1) * 128]
                oa_ref[rows, h * 128:(h + 1) * 128] = (y[c * GDN_H + h] * _silu(z)).astype(oa_ref.dtype)

    def call(pg, sm, cw, prm, nw, comm=None, comm_args=()):
        blk4 = lambda i: (i, 0, 0, 0)
        cx = _exchange_specs(comm)
        return pl.pallas_call(
            _with_exchange(body, comm, 5, 4, nb),
            grid=(nb,),
            in_specs=[
                pl.BlockSpec((tb, 2048), lambda i: (i, 0)),
                pl.BlockSpec((tb, 128), lambda i: (i, 0)),
                pl.BlockSpec((8, 1536), lambda i: (0, 0)),
                pl.BlockSpec((8, 128), lambda i: (0, 0)),
                pl.BlockSpec((8, 128), lambda i: (0, 0)),
            ] + cx["specs"],
            out_specs=[
                pl.BlockSpec((tb, 512), lambda i: (i, 0)),
                pl.BlockSpec((ncb, GDN_H, 128, 128), blk4),
                pl.BlockSpec((ncb, GDN_H, CH, CH), blk4),
                pl.BlockSpec((ncb, GDN_H, CH, 256), blk4),
            ] + cx["specs"],
            out_shape=[
                jax.ShapeDtypeStruct((seq, 512), BF16),
                jax.ShapeDtypeStruct((seq // CH, GDN_H, 128, 128), F32),
                jax.ShapeDtypeStruct((seq // CH, GDN_H, CH, CH), F32),
                jax.ShapeDtypeStruct((seq // CH, GDN_H, CH, 256), F32),
            ] + cx["out_shape"],
            scratch_shapes=[pltpu.VMEM((GDN_H, 128, 128), F32), pltpu.VMEM((8, 1536), F32)] + cx["scratch"],
            compiler_params=pltpu.CompilerParams(dimension_semantics=("arbitrary",), vmem_limit_bytes=VMEM_LIMIT,
                                                 has_side_effects=comm is not None),
            name="gdn_fwd" + cx["tag"],
        )(pg, sm, cw, prm, nw, *comm_args)

    return call


def _make_gdn_bwd(seq, tb):
    ncb = tb // CH
    nb = seq // tb
    hb = tb // 8
    n = ncb * GDN_H

    def body(pg_ref, prev_ref, sm_ref, cw_ref, prm_ref, nw_ref, st_ref, ti_ref, uw_ref, doa_ref,
             dpg_ref, dsm_ref, dcw_ref, dprm_ref, dnw_ref, ds_scr, nxt_scr):
        i = pl.program_id(0)

        @pl.when(i == 0)
        def _():
            ds_scr[...] = jnp.zeros_like(ds_scr)
            nxt_scr[...] = jnp.zeros_like(nxt_scr)
            dcw_ref[...] = jnp.zeros_like(dcw_ref)
            dprm_ref[...] = jnp.zeros_like(dprm_ref)
            dnw_ref[...] = jnp.zeros_like(dnw_ref)

        masks = _masks()
        strict = masks[1]
        sm = sm_ref[...]
        cw = cw_ref[...]
        prm = prm_ref[...]
        halo8 = jnp.where(i == nb - 1, 0.0, prev_ref[...])
        raw, taps, pre, act, beta, sp_in, g, gc = _gdn_common(pg_ref, halo8, sm, cw, prm, tb)
        nw = nw_ref[0:1, :]
        row_id = _iota2((CH, 1), 0)
        d = _gdn_batch(act, beta, gc, gc.T, jnp.exp(gc), ncb, masks)
        t = ti_ref[...].reshape(n, CH, CH)
        sol = uw_ref[...].reshape(n, CH, 256)
        u, w = sol[:, :, :128], sol[:, :, 128:]
        s_prev = st_ref[...].reshape(n, 128, 128)
        v_new = u - _bdot(w, s_prev, _NN)
        o = _bdot(d["qg"], s_prev, _NN) + _bdot(d["attn"], v_new, _NN)

        pairs = [(c, h) for c in range(ncb) for h in range(GDN_H)]
        z = jnp.stack([pg_ref[c * CH:(c + 1) * CH, 1536 + h * 128:1536 + (h + 1) * 128] for c, h in pairs])
        doa = jnp.stack([doa_ref[c * CH:(c + 1) * CH, h * 128:(h + 1) * 128] for c, h in pairs])
        on, r, y = _rms_fwd(o, nw, GDN_D)
        dz = doa * y * _dsilu(z)
        do, dnw_rows = _rms_bwd(doa * _silu(z), on, r, nw, GDN_D)
        dnw_acc = jnp.sum(jnp.sum(dnw_rows, axis=0), axis=0, keepdims=True)

        dvn_in = _bdot(d["attn"], do, _TN)
        qgtdo = _bdot(d["qg"], do, _TN)
        dvn_l, dkdec_l, dgl_l = [None] * ncb, [None] * ncb, [None] * ncb
        for c in reversed(range(ncb)):
            bs = slice(c * GDN_H, (c + 1) * GDN_H)
            dsn = ds_scr[...]
            dvn_c = dvn_in[bs] + _bdot(d["kdec"][bs], dsn, _NN)
            ds_scr[...] = d["egl"][bs] * dsn + qgtdo[bs] - _bdot(w[bs], dvn_c, _TN)
            dvn_l[c] = dvn_c
            dkdec_l[c] = _bdot(v_new[bs], dsn, _NT)
            dgl_l[c] = d["egl"][bs] * jnp.sum(_rsum(s_prev[bs] * dsn), axis=1, keepdims=True)
        dvn = jnp.concatenate(dvn_l, axis=0)
        dkdec = jnp.concatenate(dkdec_l, axis=0)
        dglast = jnp.concatenate(dgl_l, axis=0)

        dqg = _bdot(do, s_prev, _NT)
        dattn = _bdot(do, v_new, _NT)
        dw = -_bdot(dvn, s_prev, _NT)
        drhs = _bdot3(t, jnp.concatenate([dvn, dw], axis=2), _TN)
        dvb, dkg = drhs[:, :, :128], drhs[:, :, 128:]
        da = jnp.where(strict, -(_bdot(dvb, u, _NT) + _bdot(dkg, w, _NT)), 0.0)
        dp = da * d["decay"]
        dq_m = dattn * d["decay"]
        m = da * d["a"] + dattn * d["attn"]
        upper_tri = jnp.broadcast_to((_iota2((CH, CH), 1) >= _iota2((CH, CH), 0)).astype(BF16), (n, CH, CH))
        dg_in = _rsum(jnp.where(strict, _bdot(upper_tri, m, _NN), 0.0))
        dkb = _bdot(dp, d["kh"], _NN) + dkg * d["eg"]
        kdk_row = _rsum(dkdec * d["kdec"])
        dk = _bdot(dp, d["kb"], _TN) + _bdot(dq_m, d["qh"], _TN) + dkdec * d["kd_scale"] + dkb * d["bh"]
        dq = _bdot(dq_m, d["kh"], _NN) + dqg * d["eg"]
        dglast = dglast + jnp.sum(kdk_row, axis=1, keepdims=True)
        dgcol = (_rsum(dqg * d["qg"]) + _rsum(dkg * d["kg"]) - kdk_row + jnp.where(row_id == CH - 1, dglast, 0.0))
        dbeta = _rsum(dkb * d["kh"]) + _rsum(dvb * d["vh"])
        dn = dq * (GDN_D ** -0.5)
        dact_q = d["rq"] * (dn - d["qn"] * _rsum(dn * d["qn"]))
        dact_k = d["rk"] * (dk - d["kh"] * _rsum(dk * d["kh"]))
        dact_v = dvb * d["bh"]

        def lanes(v, lane0):
            return jnp.concatenate(
                [sum(_put_lane(v[c * GDN_H + h], lane0 + h) for h in range(GDN_H)) for c in range(ncb)], axis=0)

        def tokens(v):
            return jnp.concatenate(
                [jnp.concatenate([v[c * GDN_H + h] for h in range(GDN_H)], axis=1) for c in range(ncb)], axis=0)

        dbeta_all = lanes(dbeta, 0)
        dg = _dot01l(_chunk_tri(tb, upper=True), lanes(dgcol, 4)) + lanes(dg_in, 4)
        neg_ea = -jnp.exp(prm[0:1, :])
        da_raw = dg * neg_ea * _sigmoid(sp_in)
        db_raw = dbeta_all * beta * (1.0 - beta)
        dsm_ref[...] = (da_raw + db_raw).astype(dsm_ref.dtype)
        lane8 = _iota2((8, 128), 1)
        sub8 = _iota2((8, 128), 0)
        dalog = jnp.sum(dg * g, axis=0, keepdims=True)
        ddtb = jnp.sum(da_raw, axis=0, keepdims=True)
        dprm_ref[...] += jnp.where(sub8 == 0, dalog, 0.0) + jnp.where(sub8 == 1, ddtb, 0.0)
        dnw_ref[...] += jnp.where(sub8 == 0, dnw_acc, 0.0)

        dact = jnp.concatenate([tokens(dact_q), tokens(dact_k), tokens(dact_v)], axis=1)
        dpre = dact * _dsilu(pre)
        back = _conv_back(dpre, nxt_scr[...], tb)
        nxt_scr[...] = dpre[0:8, :]
        draw = back[0] * cw[3:4, :] + back[1] * cw[2:3, :] + back[2] * cw[1:2, :] + back[3] * cw[0:1, :]
        dpg_ref[:, 0:1536] = draw.astype(dpg_ref.dtype)
        dpg_ref[:, 1536:2048] = tokens(dz).astype(dpg_ref.dtype)
        sub_c = _iota2((8, 1536), 0)
        dcw_new = jnp.zeros((8, 1536), F32)
        for s_ in range(CONV_W):
            dcw_new = dcw_new + jnp.where(sub_c == 3 - s_, jnp.sum(dpre * taps[s_], axis=0, keepdims=True), 0.0)
        dcw_ref[...] += dcw_new

    def call(pg, sm, cw, prm, nw, st, ti, uw, doa, comm=None, comm_args=()):
        rev = lambda i: (nb - 1 - i, 0)
        const = lambda i: (0, 0)
        cx = _exchange_specs(comm)
        return pl.pallas_call(
            _with_exchange(body, comm, 10, 5, nb),
            grid=(nb,),
            in_specs=[
                pl.BlockSpec((tb, 2048), rev),
                pl.BlockSpec((8, 1536), lambda i: (jnp.maximum((nb - 1 - i) * hb - 1, 0), 0)),
                pl.BlockSpec((tb, 128), rev),
                pl.BlockSpec((8, 1536), const),
                pl.BlockSpec((8, 128), const),
                pl.BlockSpec((8, 128), const),
                pl.BlockSpec((ncb, GDN_H, 128, 128), lambda i: (nb - 1 - i, 0, 0, 0)),
                pl.BlockSpec((ncb, GDN_H, CH, CH), lambda i: (nb - 1 - i, 0, 0, 0)),
                pl.BlockSpec((ncb, GDN_H, CH, 256), lambda i: (nb - 1 - i, 0, 0, 0)),
                pl.BlockSpec((tb, 512), rev),
            ] + cx["specs"],
            out_specs=[
                pl.BlockSpec((tb, 2048), rev),
                pl.BlockSpec((tb, 128), rev),
                pl.BlockSpec((8, 1536), const),
                pl.BlockSpec((8, 128), const),
                pl.BlockSpec((8, 128), const),
            ] + cx["specs"],
            out_shape=[
                jax.ShapeDtypeStruct((seq, 2048), BF16),
                jax.ShapeDtypeStruct((seq, 128), BF16),
                jax.ShapeDtypeStruct((8, 1536), F32),
                jax.ShapeDtypeStruct((8, 128), F32),
                jax.ShapeDtypeStruct((8, 128), F32),
            ] + cx["out_shape"],
            scratch_shapes=[pltpu.VMEM((GDN_H, 128, 128), F32), pltpu.VMEM((8, 1536), F32)] + cx["scratch"],
            compiler_params=pltpu.CompilerParams(dimension_semantics=("arbitrary",), vmem_limit_bytes=VMEM_LIMIT,
                                                 has_side_effects=comm is not None),
            name="gdn_bwd" + cx["tag"],
        )(pg, pg, sm, cw, prm, nw, st, ti, uw, doa, *comm_args)

    return call


def _expand_mat():
    r = _iota2((128, SSD_W), 0)
    c = _iota2((128, SSD_W), 1)
    return (jnp.right_shift(c, 6) == r).astype(F32)


def _reduce_heads(v, e):
    vh, vl = _split(v)
    eb = e.astype(BF16)
    nt = (((1,), (1,)), ((), ()))
    return (lax.dot_general(vh, eb, nt, preferred_element_type=F32)
            + lax.dot_general(vl, eb, nt, preferred_element_type=F32))


def _row8(v):
    return jnp.broadcast_to(v, (8, v.shape[1]))


def _ssd_common(ps_ref, halo8, ss, cw, cb, prm, tb):
    raw = ps_ref[:, 0:1536]
    taps = _conv_taps(raw, halo8, tb)
    pre = taps[0] * cw[3:4, :] + taps[1] * cw[2:3, :] + taps[2] * cw[1:2, :] + taps[3] * cw[0:1, :] + cb[0:1, :]
    act = _silu(pre)
    dt_in = ss + prm[1:2, :]
    dt = _softplus(dt_in)
    a = dt * (-jnp.exp(prm[0:1, :]))
    acum = _dot01l(_chunk_tri(tb), a)
    e = _expand_mat()
    dt_e = _dot01r(dt, e)
    xdt = act[:, 0:SSD_W] * dt_e
    ea_e = _dot01r(jnp.exp(acum), e)
    d_e = _dot01r(_row8(prm[2:3, :]), e)[0:1, :]
    return raw, taps, pre, act, dt_in, dt, a, acum, e, dt_e, xdt, ea_e, d_e


def _ssd_chunk(act, acum, act_t, e, c):
    r0 = c * CH
    rows = slice(r0, r0 + CH)
    alast = acum[r0 + CH - 1:r0 + CH, :]
    wdec = jnp.exp(alast - acum[rows, :])
    wd_e = _dot01r(wdec, e)
    eal_e = _dot01r(_row8(jnp.exp(alast)), e)[0:1, :]
    return rows, wd_e, eal_e


def _ssd_lmat(acum, act_t, c, h, causal):
    r0 = c * CH
    acol = acum[r0:r0 + CH, h:h + 1]
    arow = act_t[h:h + 1, r0:r0 + CH]
    return jnp.exp(jnp.where(causal, acol - arow, NEG))


def _make_ssd_fwd(seq, tb):
    ncb = tb // CH
    nb = seq // tb
    hg = SSD_H // SSD_G
    gw = SSD_W // SSD_G

    def body(ps_ref, ss_ref, cw_ref, cb_ref, prm_ref, nw_ref, ob_ref, st_ref, hs_scr, halo_scr):
        @pl.when(pl.program_id(0) == 0)
        def _():
            hs_scr[...] = jnp.zeros_like(hs_scr)
            halo_scr[...] = jnp.zeros_like(halo_scr)

        causal, _, _ = _masks()
        (raw, _, _, act, _, _, _, acum, e, _, xdt, ea_e, d_e) = _ssd_common(
            ps_ref, halo_scr[...], ss_ref[...], cw_ref[...], cb_ref[...], prm_ref[...], tb)
        halo_scr[...] = raw[tb - 8:tb, :]
        act_t = acum.T
        nw = nw_ref[0:1, :]
        for c in range(ncb):
            rows, wd_e, eal_e = _ssd_chunk(act, acum, act_t, e, c)
            st_ref[c] = hs_scr[...]
            ys = []
            for g in range(SSD_G):
                gc_ = slice(g * gw, (g + 1) * gw)
                bg = act[rows, SSD_W + g * 128:SSD_W + (g + 1) * 128]
                cg = act[rows, SSD_W + 256 + g * 128:SSD_W + 256 + (g + 1) * 128]
                cbm = _dot_nt(cg, bg)
                hs = hs_scr[:, gc_]
                yin = _dot(cg, hs)
                yh = []
                for hh in range(hg):
                    h = g * hg + hh
                    lm = _ssd_lmat(acum, act_t, c, h, causal)
                    yh.append(_dot(cbm * lm, xdt[rows, h * SSD_P:(h + 1) * SSD_P]))
                ys.append(jnp.concatenate(yh, axis=1) + yin * ea_e[rows, gc_])
                hs_scr[:, gc_] = hs * eal_e[:, gc_] + _dot_tn(bg, xdt[rows, gc_] * wd_e[:, gc_])
            y = jnp.concatenate(ys, axis=1) + act[rows, 0:SSD_W] * d_e
            yz = y * _silu(ps_ref[rows, 1536:2560])
            outs = [_rms_fwd(yz[:, g * gw:(g + 1) * gw], nw[:, g * gw:(g + 1) * gw], gw)[2] for g in range(SSD_G)]
            ob_ref[rows, :] = jnp.concatenate(outs, axis=1).astype(ob_ref.dtype)

    def call(ps, ss, cw, cb, prm, nw):
        const = lambda i: (0, 0)
        return pl.pallas_call(
            body,
            grid=(nb,),
            in_specs=[
                pl.BlockSpec((tb, 2560), lambda i: (i, 0)),
                pl.BlockSpec((tb, 128), lambda i: (i, 0)),
                pl.BlockSpec((8, 1536), const),
                pl.BlockSpec((8, 1536), const),
                pl.BlockSpec((8, 128), const),
                pl.BlockSpec((8, SSD_W), const),
            ],
            out_specs=[
                pl.BlockSpec((tb, SSD_W), lambda i: (i, 0)),
                pl.BlockSpec((ncb, SSD_N, SSD_W), lambda i: (i, 0, 0)),
            ],
            out_shape=[
                jax.ShapeDtypeStruct((seq, SSD_W), BF16),
                jax.ShapeDtypeStruct((seq // CH, SSD_N, SSD_W), F32),
            ],
            scratch_shapes=[pltpu.VMEM((SSD_N, SSD_W), F32), pltpu.VMEM((8, 1536), F32)],
            compiler_params=pltpu.CompilerParams(dimension_semantics=("arbitrary",), vmem_limit_bytes=VMEM_LIMIT),
            name="ssd_fwd",
        )(ps, ss, cw, cb, prm, nw)

    return call


def _make_ssd_bwd(seq, tb):
    ncb = tb // CH
    nb = seq // tb
    hb = tb // 8
    hg = SSD_H // SSD_G
    gw = SSD_W // SSD_G

    def body(ps_ref, prev_ref, ss_ref, cw_ref, cb_ref, prm_ref, nw_ref, st_ref, dob_ref,
             dps_ref, dss_ref, dcw_ref, dcb_ref, dprm_ref, dnw_ref, dhs_scr, nxt_scr):
        i = pl.program_id(0)

        @pl.when(i == 0)
        def _():
            dhs_scr[...] = jnp.zeros_like(dhs_scr)
            nxt_scr[...] = jnp.zeros_like(nxt_scr)
            dcw_ref[...] = jnp.zeros_like(dcw_ref)
            dcb_ref[...] = jnp.zeros_like(dcb_ref)
            dprm_ref[...] = jnp.zeros_like(dprm_ref)
            dnw_ref[...] = jnp.zeros_like(dnw_ref)

        causal, _, _ = _masks()
        cw = cw_ref[...]
        prm = prm_ref[...]
        halo8 = jnp.where(i == nb - 1, 0.0, prev_ref[...])
        (raw, taps, pre, act, dt_in, dt, a, acum, e, dt_e, xdt, ea_e, d_e) = _ssd_common(
            ps_ref, halo8, ss_ref[...], cw, cb_ref[...], prm, tb)
        act_t = acum.T
        nw = nw_ref[0:1, :]
        row_id = _iota2((CH, 1), 0)

        dx_l, db_l, dc_l, dz_l, dacum_l, ddt_l, da_in_l = ([None] * ncb for _ in range(7))
        upper_tri = (_iota2((CH, CH), 1) >= _iota2((CH, CH), 0)).astype(F32)
        below = jnp.bitwise_and(_iota2((CH, gw), 1), CH - 1) < _iota2((CH, gw), 0)
        dnw_acc = jnp.zeros((1, SSD_W), F32)
        dd_acc = jnp.zeros((1, SSD_W), F32)

        for c in reversed(range(ncb)):
            rows, wd_e, eal_e = _ssd_chunk(act, acum, act_t, e, c)
            xc = act[rows, 0:SSD_W]
            z = ps_ref[rows, 1536:2560]
            dob = dob_ref[rows, :]
            sz = _silu(z)
            dy_g, dz_g, zacc_g, dxdt_g, dal_g, db_g, dc_g, da_in_g = [], [], [], [], [], [], [], []
            for g in range(SSD_G):
                gc_ = slice(g * gw, (g + 1) * gw)
                bg = act[rows, SSD_W + g * 128:SSD_W + (g + 1) * 128]
                cg = act[rows, SSD_W + 256 + g * 128:SSD_W + 256 + (g + 1) * 128]
                cbm = _dot_nt(cg, bg)
                hs = st_ref[c, :, gc_]
                yin = _dot(cg, hs)
                lms, yh = [], []
                for hh in range(hg):
                    h = g * hg + hh
                    lm = cbm * _ssd_lmat(acum, act_t, c, h, causal)
                    lms.append(lm)
                    yh.append(_dot(lm, xdt[rows, h * SSD_P:(h + 1) * SSD_P]))
                y_intra = jnp.concatenate(yh, axis=1)
                ea_g = ea_e[rows, gc_]
                y = y_intra + yin * ea_g + xc[:, gc_] * d_e[:, gc_]
                yz = y * sz[:, gc_]
                on, r, _ = _rms_fwd(yz, nw[:, gc_], gw)
                dyz, dnw_rows = _rms_bwd(dob[:, gc_], on, r, nw[:, gc_], gw)
                dnw_acc = dnw_acc + _put_cols(jnp.sum(dnw_rows, axis=0, keepdims=True), g, gw)
                dy = dyz * sz[:, gc_]
                dz_g.append(dyz * y * _dsilu(z[:, gc_]))
                dd_acc = dd_acc + _put_cols(jnp.sum(dy * xc[:, gc_], axis=0, keepdims=True), g, gw)
                dhs_n = dhs_scr[:, gc_]
                dyin = dy * ea_g
                dcg = _dot_nt(dyin, hs)
                xw = xdt[rows, gc_] * wd_e[:, gc_]
                dbg = _dot_nt(xw, dhs_n)
                dxw = _dot(bg, dhs_n)
                dhs_scr[:, gc_] = dhs_n * eal_e[:, gc_] + _dot_tn(cg, dyin)
                dal_g.append(jnp.sum(hs * dhs_n, axis=0, keepdims=True) * eal_e[:, gc_]
                             + jnp.sum(dxw * xw, axis=0, keepdims=True))
                dxi, ms, dcbm = [], [], jnp.zeros((CH, CH), F32)
                for hh in range(hg):
                    h = g * hg + hh
                    hc = slice(hh * SSD_P, (hh + 1) * SSD_P)
                    dyh = dy[:, hc]
                    dxi.append(_dot_tn(lms[hh], dyh))
                    dlm = _dot_nt(dyh, xdt[rows, h * SSD_P:(h + 1) * SSD_P])
                    ms.append(dlm * lms[hh])
                    dcbm = dcbm + dlm * _ssd_lmat(acum, act_t, c, h, causal)
                dx_intra = jnp.concatenate(dxi, axis=1)
                ncat = _dot(upper_tri, jnp.concatenate(ms, axis=1))
                da_in_g.append(jnp.where(below, ncat, 0.0))
                zacc_g.append(dy * yin * ea_g - dxw * xw)
                dxdt_g.append(dx_intra + dxw * wd_e[:, gc_])
                dy_g.append(dy)
                db_g.append(dbg + _dot_tn(dcbm, cg))
                dc_g.append(dcg + _dot(dcbm, bg))
            dy = jnp.concatenate(dy_g, axis=1)
            dxdt = jnp.concatenate(dxdt_g, axis=1)
            dx_l[c] = dxdt * dt_e[rows, :] + dy * d_e
            db_l[c] = jnp.concatenate(db_g, axis=1)
            dc_l[c] = jnp.concatenate(dc_g, axis=1)
            dz_l[c] = jnp.concatenate(dz_g, axis=1)
            ddt_l[c] = _reduce_heads(dxdt * xc, e)
            dalast = _reduce_heads(_row8(jnp.concatenate(dal_g, axis=1)), e)[0:1, :]
            dacum_l[c] = _reduce_heads(jnp.concatenate(zacc_g, axis=1), e) + jnp.where(row_id == CH - 1, dalast, 0.0)
            da_in_l[c] = _reduce_heads(jnp.concatenate(da_in_g, axis=1), e)

        dacum_all = jnp.concatenate(dacum_l, axis=0)
        da = _dot01l(_chunk_tri(tb, upper=True), dacum_all) + jnp.concatenate(da_in_l, axis=0)
        neg_ea = -jnp.exp(prm[0:1, :])
        ddt = jnp.concatenate(ddt_l, axis=0) + da * neg_ea
        ddt_in = ddt * _sigmoid(dt_in)
        dss_ref[...] = ddt_in.astype(dss_ref.dtype)
        sub8 = _iota2((8, 128), 0)
        dalog = jnp.sum(da * a, axis=0, keepdims=True)
        ddtb = jnp.sum(ddt_in, axis=0, keepdims=True)
        dd = _reduce_heads(_row8(dd_acc), e)[0:1, :]
        dprm_ref[...] += (jnp.where(sub8 == 0, dalog, 0.0) + jnp.where(sub8 == 1, ddtb, 0.0)
                          + jnp.where(sub8 == 2, dd, 0.0))
        dnw_ref[...] += jnp.where(_iota2((8, SSD_W), 0) == 0, dnw_acc, 0.0)

        dact = jnp.concatenate([jnp.concatenate(dx_l, axis=0), jnp.concatenate(db_l, axis=0),
                                jnp.concatenate(dc_l, axis=0)], axis=1)
        dpre = dact * _dsilu(pre)
        back = _conv_back(dpre, nxt_scr[...], tb)
        nxt_scr[...] = dpre[0:8, :]
        draw = back[0] * cw[3:4, :] + back[1] * cw[2:3, :] + back[2] * cw[1:2, :] + back[3] * cw[0:1, :]
        dps_ref[:, 0:1536] = draw.astype(dps_ref.dtype)
        dps_ref[:, 1536:2560] = jnp.concatenate(dz_l, axis=0).astype(dps_ref.dtype)
        sub_c = _iota2((8, 1536), 0)
        dcw_new = jnp.zeros((8, 1536), F32)
        for s_ in range(CONV_W):
            dcw_new = dcw_new + jnp.where(sub_c == 3 - s_, jnp.sum(dpre * taps[s_], axis=0, keepdims=True), 0.0)
        dcw_ref[...] += dcw_new
        dcb_ref[...] += jnp.where(sub_c == 0, jnp.sum(dpre, axis=0, keepdims=True), 0.0)

    def call(ps, ss, cw, cb, prm, nw, st, dob, comm=None, comm_args=()):
        rev = lambda i: (nb - 1 - i, 0)
        const = lambda i: (0, 0)
        cx = _exchange_specs(comm)
        return pl.pallas_call(
            _with_exchange(body, comm, 9, 6, nb),
            grid=(nb,),
            in_specs=[
                pl.BlockSpec((tb, 2560), rev),
                pl.BlockSpec((8, 1536), lambda i: (jnp.maximum((nb - 1 - i) * hb - 1, 0), 0)),
                pl.BlockSpec((tb, 128), rev),
                pl.BlockSpec((8, 1536), const),
                pl.BlockSpec((8, 1536), const),
                pl.BlockSpec((8, 128), const),
                pl.BlockSpec((8, SSD_W), const),
                pl.BlockSpec((ncb, SSD_N, SSD_W), lambda i: (nb - 1 - i, 0, 0)),
                pl.BlockSpec((tb, SSD_W), rev),
            ] + cx["specs"],
            out_specs=[
                pl.BlockSpec((tb, 2560), rev),
                pl.BlockSpec((tb, 128), rev),
                pl.BlockSpec((8, 1536), const),
                pl.BlockSpec((8, 1536), const),
                pl.BlockSpec((8, 128), const),
                pl.BlockSpec((8, SSD_W), const),
            ] + cx["specs"],
            out_shape=[
                jax.ShapeDtypeStruct((seq, 2560), BF16),
                jax.ShapeDtypeStruct((seq, 128), BF16),
                jax.ShapeDtypeStruct((8, 1536), F32),
                jax.ShapeDtypeStruct((8, 1536), F32),
                jax.ShapeDtypeStruct((8, 128), F32),
                jax.ShapeDtypeStruct((8, SSD_W), F32),
            ] + cx["out_shape"],
            scratch_shapes=[pltpu.VMEM((SSD_N, SSD_W), F32), pltpu.VMEM((8, 1536), F32)] + cx["scratch"],
            compiler_params=pltpu.CompilerParams(dimension_semantics=("arbitrary",), vmem_limit_bytes=VMEM_LIMIT,
                                                 has_side_effects=comm is not None),
            name="ssd_bwd" + cx["tag"],
        )(ps, ps, ss, cw, cb, prm, nw, st, dob, *comm_args)

    return call


def _ret_consts(h):
    lg = math.log(1.0 - 2.0 ** (-5.0 - h))
    r = _iota2((CH, CH), 0)
    c = _iota2((CH, CH), 1)
    rel = (r - c).astype(F32)
    dmat = jnp.where(r >= c, jnp.exp(jnp.maximum(rel, 0.0) * lg), 0.0)
    idx = _iota2((CH, 1), 0).astype(F32)
    qdec = jnp.exp((idx + 1.0) * lg)
    kdec = jnp.exp((CH - 1.0 - idx) * lg)
    cdec = math.exp(CH * lg)
    return dmat, qdec, kdec, cdec


def _ret_batch(pr_ref, cc_ref, ss_ref, ncb):
    pairs = [(c, h) for c in range(ncb) for h in range(RET_H)]

    def st(off):
        return jnp.stack([pr_ref[c * CH:(c + 1) * CH, off + h * 128:off + (h + 1) * 128] for c, h in pairs])

    cc = jnp.stack([cc_ref[c * CH:(c + 1) * CH, :] for c, _ in pairs])
    ss = jnp.stack([ss_ref[c * CH:(c + 1) * CH, :] for c, _ in pairs])
    consts = [_ret_consts(h) for h in range(RET_H)]
    dmat = jnp.stack([consts[h][0] for _, h in pairs])
    qdec = jnp.stack([consts[h][1] for _, h in pairs])
    kdec = jnp.stack([consts[h][2] for _, h in pairs])
    cdec = jnp.stack([jnp.full((1, 1), consts[h][3], F32) for h in range(RET_H)])
    q = _rot(st(0), cc, ss)
    k = _rot(st(512), cc, ss) * (RET_D ** -0.5)
    return dict(q=q, k=k, v=st(1024), z=st(1536), cc=cc, ss=ss, dmat=dmat, qdec=qdec, kdec=kdec, cdec=cdec,
                s=_bdot(q, k, _NT) * dmat)


def _rot(t, cc, ss):
    return t * cc + pltpu.roll(t, 64, axis=t.ndim - 1) * ss


def _rot_bwd(d, cc, ss):
    return d * cc + pltpu.roll(d * ss, 64, axis=d.ndim - 1)


def _make_ret_fwd(seq, tb):
    ncb = tb // CH
    nb = seq // tb

    def body(pr_ref, cc_ref, ss_ref, nw_ref, oc_ref, st_ref, r_scr):
        @pl.when(pl.program_id(0) == 0)
        def _():
            r_scr[...] = jnp.zeros_like(r_scr)

        d = _ret_batch(pr_ref, cc_ref, ss_ref, ncb)
        kd = d["k"] * d["kdec"]
        for c in range(ncb):
            bs = slice(c * RET_H, (c + 1) * RET_H)
            rs = r_scr[...]
            st_ref[c] = rs
            r_scr[...] = rs * d["cdec"] + _bdot(kd[bs], d["v"][bs], _TN)
        r_prev = st_ref[...].reshape(ncb * RET_H, 128, 128)
        o = _bdot(d["s"], d["v"], _NN) + _bdot(d["q"], r_prev, _NN) * d["qdec"]
        _, _, y = _rms_fwd(o, nw_ref[0:1, :], RET_D)
        out = y * _silu(d["z"])
        for c in range(ncb):
            for h in range(RET_H):
                oc_ref[c * CH:(c + 1) * CH, h * 128:(h + 1) * 128] = out[c * RET_H + h].astype(oc_ref.dtype)

    def call(pr, cc, ss, nw):
        return pl.pallas_call(
            body,
            grid=(nb,),
            in_specs=[
                pl.BlockSpec((tb, 2048), lambda i: (i, 0)),
                pl.BlockSpec((tb, 128), lambda i: (i, 0)),
                pl.BlockSpec((tb, 128), lambda i: (i, 0)),
                pl.BlockSpec((8, 128), lambda i: (0, 0)),
            ],
            out_specs=[
                pl.BlockSpec((tb, 512), lambda i: (i, 0)),
                pl.BlockSpec((ncb, RET_H, 128, 128), lambda i: (i, 0, 0, 0)),
            ],
            out_shape=[
                jax.ShapeDtypeStruct((seq, 512), BF16),
                jax.ShapeDtypeStruct((seq // CH, RET_H, 128, 128), F32),
            ],
            scratch_shapes=[pltpu.VMEM((RET_H, 128, 128), F32)],
            compiler_params=pltpu.CompilerParams(dimension_semantics=("arbitrary",), vmem_limit_bytes=VMEM_LIMIT),
            name="ret_fwd",
        )(pr, cc, ss, nw)

    return call


def _make_ret_bwd(seq, tb):
    ncb = tb // CH
    nb = seq // tb

    def body(pr_ref, cc_ref, ss_ref, nw_ref, st_ref, doc_ref, dpr_ref, dnw_ref, dr_scr):
        @pl.when(pl.program_id(0) == 0)
        def _():
            dr_scr[...] = jnp.zeros_like(dr_scr)
            dnw_ref[...] = jnp.zeros_like(dnw_ref)

        nw = nw_ref[0:1, :]
        scale = RET_D ** -0.5
        n = ncb * RET_H
        d = _ret_batch(pr_ref, cc_ref, ss_ref, ncb)
        q, k, v, z, s = d["q"], d["k"], d["v"], d["z"], d["s"]
        r_prev = st_ref[...].reshape(n, 128, 128)
        o = _bdot(s, v, _NN) + _bdot(q, r_prev, _NN) * d["qdec"]
        doc = jnp.stack([doc_ref[c * CH:(c + 1) * CH, h * 128:(h + 1) * 128]
                         for c in range(ncb) for h in range(RET_H)])
        on, r, y = _rms_fwd(o, nw, RET_D)
        dz = doc * y * _dsilu(z)
        do, dnw_rows = _rms_bwd(doc * _silu(z), on, r, nw, RET_D)
        dnw_acc = jnp.sum(jnp.sum(dnw_rows, axis=0), axis=0, keepdims=True)
        dqd = do * d["qdec"]
        qtd = _bdot(q, dqd, _TN)
        drn_l = [None] * ncb
        for c in reversed(range(ncb)):
            drn_l[c] = dr_scr[...]
            dr_scr[...] = qtd[c * RET_H:(c + 1) * RET_H] + d["cdec"] * drn_l[c]
        drn = jnp.concatenate(drn_l, axis=0)
        ds = _bdot(do, v, _NT) * d["dmat"]
        dq = _rot_bwd(_bdot(ds, k, _NN) + _bdot(dqd, r_prev, _NT), d["cc"], d["ss"])
        dk = _rot_bwd((_bdot(ds, q, _TN) + _bdot(v, drn, _NT) * d["kdec"]) * scale, d["cc"], d["ss"])
        dv = _bdot(s, do, _TN) + _bdot(k * d["kdec"], drn, _NN)
        for c in range(ncb):
            rows = slice(c * CH, (c + 1) * CH)
            for h in range(RET_H):
                b = c * RET_H + h
                for j, val in enumerate((dq, dk, dv, dz)):
                    dpr_ref[rows, j * 512 + h * 128:j * 512 + (h + 1) * 128] = val[b].astype(dpr_ref.dtype)
        dnw_ref[...] += jnp.where(_iota2((8, 128), 0) == 0, dnw_acc, 0.0)

    def call(pr, cc, ss, nw, st, doc):
        rev = lambda i: (nb - 1 - i, 0)
        return pl.pallas_call(
            body,
            grid=(nb,),
            in_specs=[
                pl.BlockSpec((tb, 2048), rev),
                pl.BlockSpec((tb, 128), rev),
                pl.BlockSpec((tb, 128), rev),
                pl.BlockSpec((8, 128), lambda i: (0, 0)),
                pl.BlockSpec((ncb, RET_H, 128, 128), lambda i: (nb - 1 - i, 0, 0, 0)),
                pl.BlockSpec((tb, 512), rev),
            ],
            out_specs=[
                pl.BlockSpec((tb, 2048), rev),
                pl.BlockSpec((8, 128), lambda i: (0, 0)),
            ],
            out_shape=[
                jax.ShapeDtypeStruct((seq, 2048), BF16),
                jax.ShapeDtypeStruct((8, 128), F32),
            ],
            scratch_shapes=[pltpu.VMEM((RET_H, 128, 128), F32)],
            compiler_params=pltpu.CompilerParams(dimension_semantics=("arbitrary",), vmem_limit_bytes=VMEM_LIMIT),
            name="ret_bwd",
        )(pr, cc, ss, nw, st, doc)

    return call


def _rope_tables(seq):
    half = RET_D // 2
    inv = ROPE_BASE ** (-jnp.arange(half, dtype=F32) / half)
    ang = jnp.arange(seq, dtype=jnp.int32).astype(F32)[:, None] * inv[None, :]
    cos, sin = jnp.cos(ang), jnp.sin(ang)
    return jnp.concatenate([cos, cos], axis=1), jnp.concatenate([-sin, sin], axis=1)


SEG_G, SEG_S, SEG_R, SEG_GS, SEG_SS = (0, 2048), (2048, 4608), (4608, 6656), (6656, 6784), (6784, 6912)
NP = 6912
SEGS = (SEG_G, SEG_S, SEG_R, SEG_GS, SEG_SS)


def _resident(shape):
    return pl.BlockSpec(shape, lambda i: (0,) * len(shape), pipeline_mode=pl.Buffered(1))


def _make_inproj(seq, tl):
    def body(x_ref, pn_ref, w_ref, pg_ref, ps_ref, pr_ref, gs_ref, ss_ref, ht_ref):
        x = x_ref[...]
        _, _, hn = _rms_fwd(x, pn_ref[0:1, :], D_MODEL)
        h = hn.astype(BF16)
        ht_ref[...] = hn.T.astype(BF16)
        for (a, b), o_ref in zip(SEGS, (pg_ref, ps_ref, pr_ref, gs_ref, ss_ref)):
            o_ref[...] = jnp.dot(h, w_ref[:, a:b], preferred_element_type=F32)

    def call(x, pn, w, comm=None, comm_args=()):
        row = lambda i: (i, 0)
        cx = _exchange_specs(comm)
        return pl.pallas_call(
            _with_exchange(body, comm, 3, 6, seq // tl),
            grid=(seq // tl,),
            in_specs=[pl.BlockSpec((tl, D_MODEL), row), _resident((8, D_MODEL)), _resident((D_MODEL, NP))]
            + cx["specs"],
            out_specs=[pl.BlockSpec((tl, b - a), row) for a, b in SEGS]
            + [pl.BlockSpec((D_MODEL, tl), lambda i: (0, i))] + cx["specs"],
            out_shape=[jax.ShapeDtypeStruct((seq, b - a), F32) for a, b in SEGS]
            + [jax.ShapeDtypeStruct((D_MODEL, seq), BF16)] + cx["out_shape"],
            scratch_shapes=cx["scratch"],
            compiler_params=pltpu.CompilerParams(dimension_semantics=("arbitrary",), vmem_limit_bytes=VMEM_LIMIT,
                                                 has_side_effects=comm is not None),
            name="inproj" + cx["tag"],
        )(x, pn, w, *comm_args)

    return call


def _make_outproj(seq, tl):
    def body(oa_ref, ob_ref, oc_ref, w_ref, x_ref, qn_ref, out_ref, xn_ref):
        out = (jnp.dot(oa_ref[...], w_ref[0:512, :], preferred_element_type=F32)
               + jnp.dot(ob_ref[...], w_ref[512:1536, :], preferred_element_type=F32)
               + jnp.dot(oc_ref[...], w_ref[1536:2048, :], preferred_element_type=F32))
        out_ref[...] = out
        _, _, y = _rms_fwd(out, qn_ref[0:1, :], D_MODEL)
        xn_ref[...] = x_ref[...] + y

    def call(oa, ob, oc, w, x, qn):
        row = lambda i: (i, 0)
        return pl.pallas_call(
            body,
            grid=(seq // tl,),
            in_specs=[pl.BlockSpec((tl, 512), row), pl.BlockSpec((tl, 1024), row), pl.BlockSpec((tl, 512), row),
                      _resident((2048, D_MODEL)), pl.BlockSpec((tl, D_MODEL), row), _resident((8, D_MODEL))],
            out_specs=[pl.BlockSpec((tl, D_MODEL), row), pl.BlockSpec((tl, D_MODEL), row)],
            out_shape=[jax.ShapeDtypeStruct((seq, D_MODEL), F32), jax.ShapeDtypeStruct((seq, D_MODEL), F32)],
            compiler_params=pltpu.CompilerParams(dimension_semantics=("arbitrary",), vmem_limit_bytes=VMEM_LIMIT),
            name="outproj",
        )(oa, ob, oc, w, x, qn)

    return call


def _make_loss_head(seq, tl):
    def body(y_ref, t_ref, dy_ref, loss_ref):
        @pl.when(pl.program_id(0) == 0)
        def _():
            loss_ref[...] = jnp.zeros_like(loss_ref)

        err = y_ref[...] - t_ref[...]
        dy_ref[...] = err * (1.0 / D_MODEL)
        part = jnp.sum(jnp.sum(err * err, axis=1, keepdims=True), axis=0, keepdims=True) * (0.5 / D_MODEL)
        loss_ref[...] += jnp.where((_iota2((8, 128), 0) == 0) & (_iota2((8, 128), 1) == 0), part, 0.0)

    def call(y, t):
        row = lambda i: (i, 0)
        return pl.pallas_call(
            body,
            grid=(seq // tl,),
            in_specs=[pl.BlockSpec((tl, D_MODEL), row), pl.BlockSpec((tl, D_MODEL), row)],
            out_specs=[pl.BlockSpec((tl, D_MODEL), row), pl.BlockSpec((8, 128), lambda i: (0, 0))],
            out_shape=[jax.ShapeDtypeStruct((seq, D_MODEL), F32), jax.ShapeDtypeStruct((8, 128), F32)],
            compiler_params=pltpu.CompilerParams(dimension_semantics=("arbitrary",)),
            name="loss_head",
        )(y, t)

    return call


def _make_outproj_bwd(seq, tl):
    def body(dxn_ref, out_ref, oa_ref, ob_ref, oc_ref, w_ref, qn_ref, doa_ref, dob_ref, doc_ref, dqn_ref, dw_ref):
        @pl.when(pl.program_id(0) == 0)
        def _():
            dqn_ref[...] = jnp.zeros_like(dqn_ref)
            dw_ref[...] = jnp.zeros_like(dw_ref)

        qn = qn_ref[0:1, :]
        on, r, _ = _rms_fwd(out_ref[...], qn, D_MODEL)
        dout, dqn_rows = _rms_bwd(dxn_ref[...], on, r, qn, D_MODEL)
        dqn_ref[...] += jnp.where(_iota2((8, D_MODEL), 0) == 0, jnp.sum(dqn_rows, axis=0, keepdims=True), 0.0)
        db = dout.astype(BF16)
        nt = (((1,), (1,)), ((), ()))
        tn = (((0,), (0,)), ((), ()))
        doa_ref[...] = lax.dot_general(db, w_ref[0:512, :], nt, preferred_element_type=F32)
        dob_ref[...] = lax.dot_general(db, w_ref[512:1536, :], nt, preferred_element_type=F32)
        doc_ref[...] = lax.dot_general(db, w_ref[1536:2048, :], nt, preferred_element_type=F32)
        dw_ref[0:512, :] += lax.dot_general(oa_ref[...], db, tn, preferred_element_type=F32)
        dw_ref[512:1536, :] += lax.dot_general(ob_ref[...], db, tn, preferred_element_type=F32)
        dw_ref[1536:2048, :] += lax.dot_general(oc_ref[...], db, tn, preferred_element_type=F32)

    def call(dxn, out, oa, ob, oc, w, qn):
        row = lambda i: (i, 0)
        const = lambda i: (0, 0)
        return pl.pallas_call(
            body,
            grid=(seq // tl,),
            in_specs=[pl.BlockSpec((tl, D_MODEL), row), pl.BlockSpec((tl, D_MODEL), row),
                      pl.BlockSpec((tl, 512), row), pl.BlockSpec((tl, 1024), row), pl.BlockSpec((tl, 512), row),
                      _resident((2048, D_MODEL)), _resident((8, D_MODEL))],
            out_specs=[pl.BlockSpec((tl, 512), row), pl.BlockSpec((tl, 1024), row), pl.BlockSpec((tl, 512), row),
                       pl.BlockSpec((8, D_MODEL), const), pl.BlockSpec((2048, D_MODEL), const)],
            out_shape=[jax.ShapeDtypeStruct((seq, 512), F32), jax.ShapeDtypeStruct((seq, 1024), F32),
                       jax.ShapeDtypeStruct((seq, 512), F32), jax.ShapeDtypeStruct((8, D_MODEL), F32),
                       jax.ShapeDtypeStruct((2048, D_MODEL), F32)],
            compiler_params=pltpu.CompilerParams(dimension_semantics=("arbitrary",), vmem_limit_bytes=VMEM_LIMIT),
            name="outproj_bwd",
        )(dxn, out, oa, ob, oc, w, qn)

    return call


def _make_inproj_bwd_dx(seq, tl):
    def body(dg_ref, ds_ref, dr_ref, dgs_ref, dss_ref, w_ref, x_ref, pn_ref, dxn_ref, dx_ref, dpn_ref):
        @pl.when(pl.program_id(0) == 0)
        def _():
            dpn_ref[...] = jnp.zeros_like(dpn_ref)

        nt = (((1,), (1,)), ((), ()))
        dh = jnp.zeros((tl, D_MODEL), F32)
        for (a, b), d_ref in zip(SEGS, (dg_ref, ds_ref, dr_ref, dgs_ref, dss_ref)):
            dh = dh + lax.dot_general(d_ref[...], w_ref[:, a:b], nt, preferred_element_type=F32)
        pn = pn_ref[0:1, :]
        on, r, _ = _rms_fwd(x_ref[...], pn, D_MODEL)
        dx, dpn_rows = _rms_bwd(dh, on, r, pn, D_MODEL)
        dx_ref[...] = dx + dxn_ref[...]
        dpn_ref[...] += jnp.where(_iota2((8, D_MODEL), 0) == 0, jnp.sum(dpn_rows, axis=0, keepdims=True), 0.0)

    def call(dg, ds, dr, dgs, dss, w, x, pn, dxn, comm=None, comm_args=()):
        row = lambda i: (i, 0)
        cx = _exchange_specs(comm)
        return pl.pallas_call(
            _with_exchange(body, comm, 9, 2, seq // tl),
            grid=(seq // tl,),
            in_specs=[pl.BlockSpec((tl, b - a), row) for a, b in SEGS]
            + [_resident((D_MODEL, NP)), pl.BlockSpec((tl, D_MODEL), row), _resident((8, D_MODEL)),
               pl.BlockSpec((tl, D_MODEL), row)] + cx["specs"],
            out_specs=[pl.BlockSpec((tl, D_MODEL), row), pl.BlockSpec((8, D_MODEL), lambda i: (0, 0))] + cx["specs"],
            out_shape=[jax.ShapeDtypeStruct((seq, D_MODEL), F32), jax.ShapeDtypeStruct((8, D_MODEL), F32)]
            + cx["out_shape"],
            scratch_shapes=cx["scratch"],
            compiler_params=pltpu.CompilerParams(dimension_semantics=("arbitrary",), vmem_limit_bytes=VMEM_LIMIT,
                                                 has_side_effects=comm is not None),
            name="inproj_bwd_dx" + cx["tag"],
        )(dg, ds, dr, dgs, dss, w, x, pn, dxn, *comm_args)

    return call


def _make_inproj_bwd_dw(seq, tl, width, tn, name):
    def body(ht_ref, d_ref, dw_ref):
        @pl.when(pl.program_id(1) == 0)
        def _():
            dw_ref[...] = jnp.zeros_like(dw_ref)

        dw_ref[...] += jnp.dot(ht_ref[...], d_ref[...], preferred_element_type=F32)

    def call(ht, d):
        return pl.pallas_call(
            body,
            grid=(width // tn, seq // tl),
            in_specs=[pl.BlockSpec((D_MODEL, tl), lambda j, i: (0, i)), pl.BlockSpec((tl, tn), lambda j, i: (i, j))],
            out_specs=pl.BlockSpec((D_MODEL, tn), lambda j, i: (0, j)),
            out_shape=jax.ShapeDtypeStruct((D_MODEL, width), F32),
            compiler_params=pltpu.CompilerParams(dimension_semantics=("arbitrary", "arbitrary"),
                                                 vmem_limit_bytes=VMEM_LIMIT),
            name=name,
        )(ht, d)

    return call


ADAM_LR, ADAM_B1, ADAM_B2, ADAM_EPS, ADAM_WD, ADAM_STEP = 0.001, 0.9, 0.999, 1e-08, 0.01, 10


def _adam_math(w, g, m, v):
    m = ADAM_B1 * m + (1.0 - ADAM_B1) * g
    v = ADAM_B2 * v + (1.0 - ADAM_B2) * (g * g)
    m_hat = m / (1.0 - ADAM_B1 ** ADAM_STEP)
    v_hat = v / (1.0 - ADAM_B2 ** ADAM_STEP)
    delta = -ADAM_LR * (m_hat / (jnp.sqrt(v_hat) + ADAM_EPS) + ADAM_WD * w)
    return delta, m, v


def _adamw(w, g, m, v, name):
    shape = w.shape
    cols = shape[-1]
    rows = w.size // cols
    tr = rows if rows <= 512 else 256
    assert rows % tr == 0

    def body(w_ref, g_ref, m_ref, v_ref, d_ref, mo_ref, vo_ref):
        d_ref[...], mo_ref[...], vo_ref[...] = _adam_math(w_ref[...], g_ref[...], m_ref[...], v_ref[...])

    spec = pl.BlockSpec((tr, cols), lambda i: (i, 0))
    outs = pl.pallas_call(
        body,
        grid=(rows // tr,),
        in_specs=[spec] * 4,
        out_specs=[spec] * 3,
        out_shape=[jax.ShapeDtypeStruct((rows, cols), F32)] * 3,
        compiler_params=pltpu.CompilerParams(dimension_semantics=("arbitrary",), vmem_limit_bytes=VMEM_LIMIT),
        name=name,
    )(*[a.reshape(rows, cols) for a in (w, g, m, v)])
    return (g,) + tuple(o.reshape(shape) for o in outs)


def _adamw_pairs(w, mine, theirs, m, v, name):
    na, r, cols = w.shape
    assert na == 2
    tr = 256
    assert r % tr == 0

    def body(w_ref, a0_ref, b0_ref, a1_ref, b1_ref, m_ref, v_ref, g_ref, d_ref, mo_ref, vo_ref):
        g = jnp.where(pl.program_id(0) == 0, a0_ref[...] + b0_ref[...], a1_ref[...] + b1_ref[...])
        g_ref[...] = g
        d_ref[...], mo_ref[...], vo_ref[...] = _adam_math(w_ref[...], g, m_ref[...], v_ref[...])

    full = pl.BlockSpec((None, tr, cols), lambda a, i: (a, i, 0))
    one = pl.BlockSpec((None, tr, cols), lambda a, i: (0, i, 0))
    return pl.pallas_call(
        body,
        grid=(na, r // tr),
        in_specs=[full, one, one, one, one, full, full],
        out_specs=[full] * 4,
        out_shape=[jax.ShapeDtypeStruct(w.shape, F32)] * 4,
        compiler_params=pltpu.CompilerParams(dimension_semantics=("arbitrary",) * 2, vmem_limit_bytes=VMEM_LIMIT),
        name=name,
    )(w, mine[0], theirs[0], mine[1], theirs[1], m, v)


MESH = pl.DeviceIdType.MESH
ANY = pl.BlockSpec(memory_space=pl.ANY)
CHIP_REL = ((1, 0), (0, 1), (1, 1))


def _flip(v, d):
    return 1 - v if d else v


def _ag_chips(arrs, name):
    n = len(arrs)

    def body(*refs):
        ins, outs = refs[:n], refs[n:2 * n]
        send_sems, recv_sems, loc_sems = refs[2 * n:]
        x, y, c = lax.axis_index("x"), lax.axis_index("y"), lax.axis_index("c")
        me = 2 * x + y

        def remote(a, k, slot):
            dx, dy = CHIP_REL[k]
            return pltpu.make_async_remote_copy(
                src_ref=ins[a], dst_ref=outs[a].at[slot], send_sem=send_sems.at[a * 3 + k],
                recv_sem=recv_sems.at[a * 3 + k], device_id=(_flip(x, dx), _flip(y, dy), c), device_id_type=MESH)

        local = [pltpu.make_async_copy(ins[a], outs[a].at[me], loc_sems.at[a]) for a in range(n)]
        for cp in local:
            cp.start()
        for a in range(n):
            for k in range(3):
                remote(a, k, me).start()
        for a in range(n):
            for k, (dx, dy) in enumerate(CHIP_REL):
                remote(a, k, 2 * _flip(x, dx) + _flip(y, dy)).wait_recv()
        for a in range(n):
            for k in range(3):
                remote(a, k, me).wait_send()
        for cp in local:
            cp.wait()

    return pl.pallas_call(
        body,
        in_specs=[ANY] * n,
        out_specs=[ANY] * n,
        out_shape=[jax.ShapeDtypeStruct((4,) + a.shape, a.dtype) for a in arrs],
        scratch_shapes=[pltpu.SemaphoreType.DMA((3 * n,)), pltpu.SemaphoreType.DMA((3 * n,)),
                        pltpu.SemaphoreType.DMA((n,))],
        compiler_params=pltpu.CompilerParams(has_side_effects=True),
        name=name,
    )(*arrs)


class _ChipExchange:
    def __init__(self, kind, arrs):
        self.kind, self.n = kind, len(arrs)
        if kind == "gather":
            self.out_shape = [jax.ShapeDtypeStruct((4,) + a.shape, a.dtype) for a in arrs]
        else:
            self.out_shape = [jax.ShapeDtypeStruct((3,) + a.shape[1:], a.dtype) for a in arrs]
        self.scratch = [pltpu.SemaphoreType.DMA((4 * self.n,)), pltpu.SemaphoreType.DMA((4 * self.n,))]

    def _copies(self, ins, outs, sems):
        send_sems, recv_sems = sems
        x, y, c = lax.axis_index("x"), lax.axis_index("y"), lax.axis_index("c")
        me = 2 * x + y
        pairs = []
        for a in range(self.n):
            for k, (dx, dy) in enumerate(CHIP_REL):
                px, py = _flip(x, dx), _flip(y, dy)
                sem = dict(send_sem=send_sems.at[4 * a + k], recv_sem=recv_sems.at[4 * a + k],
                           device_id=(px, py, c), device_id_type=MESH)
                if self.kind == "gather":
                    out = pltpu.make_async_remote_copy(src_ref=ins[a], dst_ref=outs[a].at[me], **sem)
                    inc = pltpu.make_async_remote_copy(src_ref=ins[a], dst_ref=outs[a].at[2 * px + py], **sem)
                else:
                    out = pltpu.make_async_remote_copy(src_ref=ins[a].at[2 * px + py], dst_ref=outs[a].at[k], **sem)
                    inc = out
                pairs.append((out, inc))
            if self.kind == "gather":
                own = pltpu.make_async_remote_copy(
                    src_ref=ins[a], dst_ref=outs[a].at[me], send_sem=send_sems.at[4 * a + 3],
                    recv_sem=recv_sems.at[4 * a + 3], device_id=(x, y, 1 - c), device_id_type=MESH)
                pairs.append((own, own))
        return pairs

    def start(self, ins, outs, sems):
        for out, _ in self._copies(ins, outs, sems):
            out.start()

    def finish(self, ins, outs, sems):
        pairs = self._copies(ins, outs, sems)
        for _, inc in pairs:
            inc.wait_recv()
        for out, _ in pairs:
            out.wait_send()


def _with_exchange(body, comm, n_in, n_out, nb):
    if comm is None:
        return body

    def wrapped(*refs):
        ins = refs[:n_in]
        c_in = refs[n_in:n_in + comm.n]
        outs = refs[n_in + comm.n:n_in + comm.n + n_out]
        c_out = refs[n_in + comm.n + n_out:n_in + 2 * comm.n + n_out]
        rest = refs[n_in + 2 * comm.n + n_out:]
        scratch, sems = rest[:len(rest) - 2], rest[len(rest) - 2:]

        @pl.when(pl.program_id(0) == 0)
        def _():
            comm.start(c_in, c_out, sems)

        body(*ins, *outs, *scratch)

        @pl.when(pl.program_id(0) == nb - 1)
        def _():
            comm.finish(c_in, c_out, sems)

    return wrapped


def _exchange_specs(comm):
    if comm is None:
        return dict(specs=[], out_shape=[], scratch=[], tag="")
    return dict(specs=[pl.BlockSpec(memory_space=pl.ANY)] * comm.n, out_shape=list(comm.out_shape),
                scratch=list(comm.scratch), tag="_" + comm.kind)


def _half(ref_or_shape, half):
    r = ref_or_shape[-2] // 2
    return pl.ds(half * r, r)


def _ag_rows(arrs, name):
    n = len(arrs)

    def body(*refs):
        ins, outs = refs[:n], refs[n:2 * n]
        send_sems, recv_sems, fsend_sems, frecv_sems, loc_sems = refs[2 * n:]
        x, y, c = lax.axis_index("x"), lax.axis_index("y"), lax.axis_index("c")
        me = 2 * x + y
        sib = (x, y, 1 - c)

        def chip_of(k):
            dx, dy = CHIP_REL[k]
            return _flip(x, dx), _flip(y, dy)

        def ici(a, k, slot):
            px, py = chip_of(k)
            rows = _half(arrs[a].shape, c)
            return pltpu.make_async_remote_copy(
                src_ref=ins[a].at[:, rows, :], dst_ref=outs[a].at[slot, :, rows, :], send_sem=send_sems.at[a * 3 + k],
                recv_sem=recv_sems.at[a * 3 + k], device_id=(px, py, c), device_id_type=MESH)

        def fwd(a, k, half):
            px, py = chip_of(k)
            blk = outs[a].at[2 * px + py, :, _half(arrs[a].shape, half), :]
            return pltpu.make_async_remote_copy(
                src_ref=blk, dst_ref=blk, send_sem=fsend_sems.at[a * 3 + k], recv_sem=frecv_sems.at[a * 3 + k],
                device_id=sib, device_id_type=MESH)

        own = [pltpu.make_async_remote_copy(src_ref=ins[a], dst_ref=outs[a].at[me], send_sem=loc_sems.at[a],
                                            recv_sem=loc_sems.at[n + a], device_id=sib, device_id_type=MESH)
               for a in range(n)]
        for cp in own:
            cp.start()
        for a in range(n):
            for k in range(3):
                ici(a, k, me).start()
        for a in range(n):
            for k in range(3):
                px, py = chip_of(k)
                ici(a, k, 2 * px + py).wait_recv()
                fwd(a, k, c).start()
        for a in range(n):
            for k in range(3):
                fwd(a, k, 1 - c).wait_recv()
        for a in range(n):
            for k in range(3):
                ici(a, k, me).wait_send()
                fwd(a, k, c).wait_send()
        for cp in own:
            cp.wait()

    return pl.pallas_call(
        body,
        in_specs=[ANY] * n,
        out_specs=[ANY] * n,
        out_shape=[jax.ShapeDtypeStruct((4,) + a.shape, a.dtype) for a in arrs],
        scratch_shapes=[pltpu.SemaphoreType.DMA((3 * n,)) for _ in range(4)] + [pltpu.SemaphoreType.DMA((2 * n,))],
        compiler_params=pltpu.CompilerParams(has_side_effects=True),
        name=name,
    )(*arrs)


def _sum_chips(own, recv, chip, name):
    _, na, r, cols = own.shape
    tr = 256
    assert r % tr == 0

    def body(chip_ref, o_ref, r_ref, s_ref):
        s_ref[...] = ((o_ref[...] + r_ref[0].astype(F32)) + r_ref[1].astype(F32)) + r_ref[2].astype(F32)

    return pl.pallas_call(
        body,
        grid_spec=pltpu.PrefetchScalarGridSpec(
            num_scalar_prefetch=1,
            grid=(na, r // tr),
            in_specs=[pl.BlockSpec((None, None, tr, cols), lambda a, i, ch: (ch[0], a, i, 0)),
                      pl.BlockSpec((3, None, tr, cols), lambda a, i, ch: (0, a, i, 0))],
            out_specs=pl.BlockSpec((None, tr, cols), lambda a, i, ch: (a, i, 0))),
        out_shape=jax.ShapeDtypeStruct((na, r, cols), F32),
        compiler_params=pltpu.CompilerParams(dimension_semantics=("arbitrary",) * 2, vmem_limit_bytes=VMEM_LIMIT),
        name=name,
    )(chip, own, recv)


def _swap_sibling(arrs, name):
    n = len(arrs)

    def body(*refs):
        ins, outs = refs[:n], refs[n:2 * n]
        send_sems, recv_sems = refs[2 * n:]
        x, y, c = lax.axis_index("x"), lax.axis_index("y"), lax.axis_index("c")
        cps = [pltpu.make_async_remote_copy(src_ref=ins[a], dst_ref=outs[a], send_sem=send_sems.at[a],
                                            recv_sem=recv_sems.at[a], device_id=(x, y, 1 - c), device_id_type=MESH)
               for a in range(n)]
        for cp in cps:
            cp.start()
        for cp in cps:
            cp.wait_recv()
        for cp in cps:
            cp.wait_send()

    return pl.pallas_call(
        body,
        in_specs=[ANY] * n,
        out_specs=[ANY] * n,
        out_shape=[jax.ShapeDtypeStruct(a.shape, a.dtype) for a in arrs],
        scratch_shapes=[pltpu.SemaphoreType.DMA((n,)), pltpu.SemaphoreType.DMA((n,))],
        compiler_params=pltpu.CompilerParams(has_side_effects=True),
        name=name,
    )(*arrs)


def _allreduce_small(vec, name):
    rows = vec.shape[0]

    def body(v_ref, out_ref, gat_ref, send_sems, recv_sems):
        x, y, c = lax.axis_index("x"), lax.axis_index("y"), lax.axis_index("c")
        me = 4 * x + 2 * y + c

        def remote(k, slot):
            dx, dy, dc = (k >> 2) & 1, (k >> 1) & 1, k & 1
            return pltpu.make_async_remote_copy(
                src_ref=v_ref, dst_ref=gat_ref.at[slot], send_sem=send_sems.at[k - 1], recv_sem=recv_sems.at[k - 1],
                device_id=(_flip(x, dx), _flip(y, dy), _flip(c, dc)), device_id_type=MESH)

        gat_ref[me] = v_ref[...]
        for k in range(1, 8):
            remote(k, me).start()
        for k in range(1, 8):
            dx, dy, dc = (k >> 2) & 1, (k >> 1) & 1, k & 1
            remote(k, 4 * _flip(x, dx) + 2 * _flip(y, dy) + _flip(c, dc)).wait_recv()
        for k in range(1, 8):
            remote(k, me).wait_send()
        acc = gat_ref[0]
        for j in range(1, 8):
            acc = acc + gat_ref[j]
        out_ref[...] = acc

    vm = pl.BlockSpec(memory_space=pltpu.VMEM)
    return pl.pallas_call(
        body,
        in_specs=[vm],
        out_specs=vm,
        out_shape=jax.ShapeDtypeStruct(vec.shape, F32),
        scratch_shapes=[pltpu.VMEM((8, rows, 128), F32), pltpu.SemaphoreType.DMA((7,)), pltpu.SemaphoreType.DMA((7,))],
        compiler_params=pltpu.CompilerParams(has_side_effects=True),
        name=name,
    )(vec)


def _pad8(v, width, lane0=0):
    v = v.reshape(1, -1) if v.ndim == 1 else v
    return jnp.zeros((8, width), F32).at[:v.shape[0], lane0:lane0 + v.shape[1]].set(v.astype(F32))


def _relayout_w_in(g):
    tr = 128
    q = N_IN // 4

    def body(g_ref, o_ref):
        w = jnp.concatenate([g_ref[j] for j in range(4)], axis=1)
        z = lambda n: jnp.zeros((tr, n), w.dtype)
        o_ref[...] = jnp.concatenate([w[:, 0:2048], w[:, 2056:4616], w[:, 4632:6680],
                                      w[:, 2048:2056], z(120), w[:, 4616:4632], z(112)], axis=1)

    return pl.pallas_call(
        body,
        grid=(D_MODEL // tr,),
        in_specs=[pl.BlockSpec((4, tr, q), lambda i: (0, i, 0))],
        out_specs=pl.BlockSpec((tr, NP), lambda i: (i, 0)),
        out_shape=jax.ShapeDtypeStruct((D_MODEL, NP), g.dtype),
        compiler_params=pltpu.CompilerParams(dimension_semantics=("arbitrary",), vmem_limit_bytes=VMEM_LIMIT),
        name="relayout_w_in",
    )(g)


def _unlayout_dw_in(dg, ds, dr, dsm):
    tr = 128
    q = N_IN // 4

    def body(g_ref, s_ref, r_ref, sm_ref, o_ref, ob_ref):
        w = jnp.concatenate([g_ref[...], sm_ref[:, 0:8], s_ref[...], sm_ref[:, 128:144], r_ref[...]], axis=1)
        for j in range(4):
            blk = w[:, q * j:q * (j + 1)]
            o_ref[j] = blk
            ob_ref[j] = blk.astype(BF16)

    row = lambda i: (i, 0)
    return pl.pallas_call(
        body,
        grid=(D_MODEL // tr,),
        in_specs=[pl.BlockSpec((tr, d.shape[1]), row) for d in (dg, ds, dr, dsm)],
        out_specs=[pl.BlockSpec((4, tr, q), lambda i: (0, i, 0))] * 2,
        out_shape=[jax.ShapeDtypeStruct((4, D_MODEL, q), F32), jax.ShapeDtypeStruct((4, D_MODEL, q), BF16)],
        compiler_params=pltpu.CompilerParams(dimension_semantics=("arbitrary",), vmem_limit_bytes=VMEM_LIMIT),
        name="unlayout_dw_in",
    )(dg, ds, dr, dsm)


TB = 256
TL = 256
TK = 1024


def kernel(x, pre_norm, post_norm, w_in, gdn_conv, gdn_A_log, gdn_dt_bias, gdn_norm, ssd_conv, ssd_conv_b, ssd_A_log, ssd_dt_bias, ssd_D, ssd_norm, ret_norm, w_out, loss_target, m_pre_norm, m_post_norm, m_w_in, m_gdn_conv, m_gdn_A_log, m_gdn_dt_bias, m_gdn_norm, m_ssd_conv, m_ssd_conv_b, m_ssd_A_log, m_ssd_dt_bias, m_ssd_D, m_ssd_norm, m_ret_norm, m_w_out, v_pre_norm, v_post_norm, v_w_in, v_gdn_conv, v_gdn_A_log, v_gdn_dt_bias, v_gdn_norm, v_ssd_conv, v_ssd_conv_b, v_ssd_A_log, v_ssd_dt_bias, v_ssd_D, v_ssd_norm, v_ret_norm, v_w_out):
    seq = x.shape[1]
    chip = 2 * lax.axis_index("x") + lax.axis_index("y")
    x0 = x[0]

    wi_b, wo_b = w_in.astype(BF16), w_out.astype(BF16)
    (wi0_g,) = _ag_rows([wi_b[0:1]], "ag_weights")
    gcv_g, scv_g = _ag_chips([gdn_conv, ssd_conv], "ag_conv")
    full_w_in = _relayout_w_in
    wp = [full_w_in(wi0_g[:, 0]), None]
    wo = [None, None]
    ag0 = _ChipExchange("gather", [wo_b[0]])
    ag1 = _ChipExchange("gather", [wi_b[1], wo_b[1]])
    gcv = jnp.transpose(gcv_g, (1, 2, 0, 3)).reshape(DEPTH, CONV_W, 1536)
    scv = jnp.transpose(scv_g, (1, 2, 0, 3)).reshape(DEPTH, CONV_W, 1536)
    rope_c, rope_s = _rope_tables(seq)

    saved = []
    xc = x0
    for l in range(DEPTH):
        p = dict(
            pn=_pad8(pre_norm[l], D_MODEL), qn=_pad8(post_norm[l], D_MODEL),
            g_cw=_pad8(gcv[l], 1536), g_prm=_pad8(jnp.stack([gdn_A_log[l], gdn_dt_bias[l]]), 128, 4),
            g_nw=_pad8(gdn_norm[l], 128),
            s_cw=_pad8(scv[l], 1536), s_cb=_pad8(ssd_conv_b[l], 1536),
            s_prm=_pad8(jnp.stack([ssd_A_log[l], ssd_dt_bias[l], ssd_D[l]]), 128), s_nw=_pad8(ssd_norm[l], SSD_W),
            r_nw=_pad8(ret_norm[l], 128))
        if l == 0:
            pg, ps, pr, gs, ss, ht, wo0_g = _make_inproj(seq, TL)(xc, p["pn"], wp[l], comm=ag0, comm_args=(wo_b[0],))
            wo[0] = wo0_g.reshape(2048, D_MODEL)
        else:
            pg, ps, pr, gs, ss, ht = _make_inproj(seq, TL)(xc, p["pn"], wp[l])
        if l == 0:
            oa, stg, tig, uwg, wi1_g, wo1_g = _make_gdn_fwd(seq, TB)(
                pg, gs, p["g_cw"], p["g_prm"], p["g_nw"], comm=ag1, comm_args=(wi_b[1], wo_b[1]))
            wp[1], wo[1] = full_w_in(wi1_g), wo1_g.reshape(2048, D_MODEL)
        else:
            oa, stg, tig, uwg = _make_gdn_fwd(seq, TB)(pg, gs, p["g_cw"], p["g_prm"], p["g_nw"])
        ob, sts = _make_ssd_fwd(seq, TB)(ps, ss, p["s_cw"], p["s_cb"], p["s_prm"], p["s_nw"])
        oc, str_ = _make_ret_fwd(seq, TB)(pr, rope_c, rope_s, p["r_nw"])
        out, xn = _make_outproj(seq, TL)(oa, ob, oc, wo[l], xc, p["qn"])
        saved.append(dict(p=p, x=xc, ht=ht, pg=pg, ps=ps, pr=pr, gs=gs, ss=ss, stg=stg, tig=tig, uwg=uwg, sts=sts, str=str_,
                          oa=oa, ob=ob, oc=oc, out=out))
        xc = xn

    dxn, lossp = _make_loss_head(seq, TL)(xc, loss_target[0])

    small = [None] * DEPTH
    gin, gin_b, gout, q_in, q_out = ([None] * DEPTH for _ in range(5))

    for l in reversed(range(DEPTH)):
        s = saved[l]
        p = s["p"]
        doa, dob, doc, dqn, dwo_l = _make_outproj_bwd(seq, TL)(dxn, s["out"], s["oa"], s["ob"], s["oc"], wo[l], p["qn"])
        gout[l] = dwo_l.reshape(4, 512, D_MODEL)
        gdn_args = (s["pg"], s["gs"], p["g_cw"], p["g_prm"], p["g_nw"], s["stg"], s["tig"], s["uwg"], doa)
        if l == 0:
            payload = (gout[0].astype(BF16),)
            dpg, dgs, dcw_g, dprm_g, dnw_g, q_out[0] = _make_gdn_bwd(seq, TB)(
                *gdn_args, comm=_ChipExchange("scatter", payload), comm_args=payload)
        else:
            dpg, dgs, dcw_g, dprm_g, dnw_g = _make_gdn_bwd(seq, TB)(*gdn_args)
        ssd_args = (s["ps"], s["ss"], p["s_cw"], p["s_cb"], p["s_prm"], p["s_nw"], s["sts"], dob)
        if l == 0:
            payload = (gin_b[1], gout[1].astype(BF16))
            dps, dss, dcw_s, dcb_s, dprm_s, dnw_s, q_in[1], q_out[1] = _make_ssd_bwd(seq, TB)(
                *ssd_args, comm=_ChipExchange("scatter", payload), comm_args=payload)
        else:
            dps, dss, dcw_s, dcb_s, dprm_s, dnw_s = _make_ssd_bwd(seq, TB)(*ssd_args)
        dpr, dnw_r = _make_ret_bwd(seq, TB)(s["pr"], rope_c, rope_s, p["r_nw"], s["str"], doc)
        dws = [_make_inproj_bwd_dw(seq, TK, d.shape[1], tn, f"inproj_bwd_dw{i}")(s["ht"], d)
               for i, (d, tn) in enumerate(((dpg, 1024), (dps, 1280), (dpr, 1024),
                                            (jnp.concatenate([dgs, dss], axis=1), 256)))]
        gin[l], gin_b[l] = _unlayout_dw_in(*dws)
        dx_args = (dpg, dps, dpr, dgs, dss, wp[l], s["x"], p["pn"], dxn)
        if l == 0:
            payload = (gin_b[0],)
            dx, dpn, q_in[0] = _make_inproj_bwd_dx(seq, TL)(
                *dx_args, comm=_ChipExchange("scatter", payload), comm_args=payload)
        else:
            dx, dpn = _make_inproj_bwd_dx(seq, TL)(*dx_args)
        small[l] = [dpn[0], dqn[0], dcw_g[0:4].reshape(-1), dprm_g[0, 4:8], dprm_g[1, 4:8], dnw_g[0],
                    dcw_s[0:4].reshape(-1), dcb_s[0], dprm_s[0, 0:16], dprm_s[1, 0:16], dprm_s[2, 0:16],
                    dnw_s[0], dnw_r[0]]
        dxn = dx
    grad_x = dxn[None]

    sizes = [a.shape[0] for a in small[0]]
    flat = jnp.concatenate(small[0] + small[1] + [lossp[0, 0:1]])
    n_flat = flat.shape[0]
    rows = -(-n_flat // 1024) * 8
    red = _allreduce_small(jnp.pad(flat, (0, rows * 128 - n_flat)).reshape(rows, 128), "allreduce_small").reshape(-1)
    per = sum(sizes)
    loss = red[2 * per]

    def pick(i):
        off = sum(sizes[:i])
        return jnp.stack([red[l * per + off:l * per + off + sizes[i]] for l in range(DEPTH)])

    g_small = dict(
        pre_norm=pick(0), post_norm=pick(1),
        gdn_conv=lax.dynamic_slice_in_dim(pick(2).reshape(DEPTH, CONV_W, 1536), chip * 384, 384, axis=2),
        gdn_A_log=pick(3), gdn_dt_bias=pick(4), gdn_norm=pick(5),
        ssd_conv=lax.dynamic_slice_in_dim(pick(6).reshape(DEPTH, CONV_W, 1536), chip * 384, 384, axis=2),
        ssd_conv_b=pick(7), ssd_A_log=pick(8), ssd_dt_bias=pick(9), ssd_D=pick(10), ssd_norm=pick(11),
        ret_norm=pick(12))

    chip1 = chip.astype(jnp.int32).reshape(1)
    s_in = [_sum_chips(gin[l][:, None], q_in[l][:, None], chip1, f"sum_chips_w_in{l}") for l in range(DEPTH)]
    s_out = [_sum_chips(gout[l][:, None], q_out[l][:, None], chip1, f"sum_chips_w_out{l}") for l in range(DEPTH)]
    t_all = _swap_sibling(s_in + s_out, "swap_grads")
    t_in, t_out = t_all[:DEPTH], t_all[DEPTH:]

    weights = dict(pre_norm=pre_norm, post_norm=post_norm, w_in=w_in, gdn_conv=gdn_conv, gdn_A_log=gdn_A_log,
                   gdn_dt_bias=gdn_dt_bias, gdn_norm=gdn_norm, ssd_conv=ssd_conv, ssd_conv_b=ssd_conv_b,
                   ssd_A_log=ssd_A_log, ssd_dt_bias=ssd_dt_bias, ssd_D=ssd_D, ssd_norm=ssd_norm, ret_norm=ret_norm,
                   w_out=w_out)
    ms = dict(pre_norm=m_pre_norm, post_norm=m_post_norm, w_in=m_w_in, gdn_conv=m_gdn_conv, gdn_A_log=m_gdn_A_log,
              gdn_dt_bias=m_gdn_dt_bias, gdn_norm=m_gdn_norm, ssd_conv=m_ssd_conv, ssd_conv_b=m_ssd_conv_b,
              ssd_A_log=m_ssd_A_log, ssd_dt_bias=m_ssd_dt_bias, ssd_D=m_ssd_D, ssd_norm=m_ssd_norm,
              ret_norm=m_ret_norm, w_out=m_w_out)
    vs = dict(pre_norm=v_pre_norm, post_norm=v_post_norm, w_in=v_w_in, gdn_conv=v_gdn_conv, gdn_A_log=v_gdn_A_log,
              gdn_dt_bias=v_gdn_dt_bias, gdn_norm=v_gdn_norm, ssd_conv=v_ssd_conv, ssd_conv_b=v_ssd_conv_b,
              ssd_A_log=v_ssd_A_log, ssd_dt_bias=v_ssd_dt_bias, ssd_D=v_ssd_D, ssd_norm=v_ssd_norm,
              ret_norm=v_ret_norm, w_out=v_w_out)
    names = list(weights)
    res = {}
    for nme in names:
        if nme == "w_in":
            res[nme] = _adamw_pairs(w_in, s_in, t_in, m_w_in, v_w_in, "adamw_w_in")
        elif nme == "w_out":
            res[nme] = _adamw_pairs(w_out, s_out, t_out, m_w_out, v_w_out, "adamw_w_out")
        else:
            res[nme] = _adamw(weights[nme], g_small[nme], ms[nme], vs[nme], "adamw_" + nme)
    return (loss, grad_x, *[res[n][0] for n in names], *[res[n][1] for n in names],
            *[res[n][2] for n in names], *[res[n][3] for n in names])
```

```python
import functools
import math

import jax
import jax.numpy as jnp
from jax import lax
from jax.experimental import pallas as pl
from jax.experimental.pallas import tpu as pltpu

F32 = jnp.float32
BF16 = jnp.bfloat16
HI = lax.Precision.HIGHEST

D_MODEL = 1024
DEPTH = 2
CH = 64
CONV_W = 4
EPS = 1e-6
GDN_H, GDN_D = 4, 128
SSD_H, SSD_P, SSD_N, SSD_G = 16, 64, 128, 2
SSD_W = SSD_H * SSD_P
RET_H, RET_D = 4, 128
ROPE_BASE = 10000.0
N_IN = 6680
NEG = -1e30

VMEM_LIMIT = 56 * 1024 * 1024


def _dot(a, b):
    return jnp.dot(a.astype(BF16), b.astype(BF16), preferred_element_type=F32)


def _dot_nt(a, b):
    return lax.dot_general(a.astype(BF16), b.astype(BF16), (((1,), (1,)), ((), ())), preferred_element_type=F32)


def _dot_tn(a, b):
    return lax.dot_general(a.astype(BF16), b.astype(BF16), (((0,), (0,)), ((), ())), preferred_element_type=F32)


def _split(a):
    hi = a.astype(BF16)
    return hi, (a - hi.astype(F32)).astype(BF16)


def _dot01l(m, v):
    vh, vl = _split(v)
    mb = m.astype(BF16)
    return jnp.dot(mb, vh, preferred_element_type=F32) + jnp.dot(mb, vl, preferred_element_type=F32)


def _dot01r(v, m):
    vh, vl = _split(v)
    mb = m.astype(BF16)
    return jnp.dot(vh, mb, preferred_element_type=F32) + jnp.dot(vl, mb, preferred_element_type=F32)


def _sigmoid(x):
    return jax.nn.sigmoid(x)


def _silu(x):
    return x * _sigmoid(x)


def _dsilu(x):
    s = _sigmoid(x)
    return s * (1.0 + x * (1.0 - s))


def _softplus(x):
    return jnp.maximum(x, 0.0) + jnp.log1p(jnp.exp(-jnp.abs(x)))


def _iota2(shape, dim):
    return lax.broadcasted_iota(jnp.int32, shape, dim)


def _chunk_tri(tb, upper=False):
    r = _iota2((tb, tb), 0)
    c = _iota2((tb, tb), 1)
    same = jnp.right_shift(r, 6) == jnp.right_shift(c, 6)
    return (same & ((c >= r) if upper else (c <= r))).astype(F32)


def _masks():
    r = _iota2((CH, CH), 0)
    c = _iota2((CH, CH), 1)
    return r >= c, r > c, (r == c).astype(F32)


def _put_lane(col, lane_idx, width=128):
    lane = _iota2((col.shape[0], width), 1)
    return jnp.where(lane == lane_idx, col, 0.0)


def _conv_taps(raw, halo8, tb):
    ext = jnp.concatenate([halo8, raw], axis=0)
    return [raw] + [pltpu.roll(ext, s, axis=0)[8:] for s in (1, 2, 3)]


def _conv_back(dpre, nxt8, tb):
    ext = jnp.concatenate([dpre, nxt8], axis=0)
    return [dpre] + [pltpu.roll(ext, tb + 8 - s, axis=0)[:tb] for s in (1, 2, 3)]


def _rms_fwd(o, w, n):
    r = lax.rsqrt(jnp.sum(o * o, axis=-1, keepdims=True) * (1.0 / n) + EPS)
    on = o * r
    return on, r, on * w


def _rms_bwd(dy, on, r, w, n):
    don = dy * w
    return r * (don - on * (jnp.sum(don * on, axis=-1, keepdims=True) * (1.0 / n))), dy * on


def _put_cols(v, g, gw):
    z = jnp.zeros_like(v)
    return jnp.concatenate([v, z] if g == 0 else [z, v], axis=1)


def _gdn_common(pg_ref, halo8, sm, cw, prm, tb, pre=None):
    raw = pg_ref[:, 0:1536]
    if pre is None:
        taps = _conv_taps(raw, halo8, tb)
        pre = taps[0] * cw[3:4, :] + taps[1] * cw[2:3, :] + taps[2] * cw[1:2, :] + taps[3] * cw[0:1, :]
    act = _silu(pre)
    beta = _sigmoid(sm)
    sp_in = sm + prm[1:2, :]
    g = -jnp.exp(prm[0:1, :]) * _softplus(sp_in)
    gc = _dot01l(_chunk_tri(tb), g)
    return raw, pre, act, beta, sp_in, g, gc


_NN = (((2,), (1,)), ((0,), (0,)))
_NT = (((2,), (2,)), ((0,), (0,)))
_TN = (((1,), (1,)), ((0,), (0,)))


def _bdot(a, b, dn):
    return lax.dot_general(a.astype(BF16), b.astype(BF16), dn, preferred_element_type=F32)


def _dot3_parts(ah, al, bh, bl, dn):
    f = lambda p, q: lax.dot_general(p, q, dn, preferred_element_type=F32)
    return f(ah, bh) + (f(ah, bl) + f(al, bh))


def _bdot3(a, b, dn):
    ah, al = _split(a)
    bh, bl = _split(b)
    return _dot3_parts(ah, al, bh, bl, dn)


def _binv_unit_lower(a, eye):
    x = eye - a
    ph, pl_ = _split(a)
    for _ in range(5):
        ph, pl_ = _split(_dot3_parts(ph, pl_, ph, pl_, _NN))
        xh, xl = _split(x)
        x = x + _dot3_parts(xh, xl, ph, pl_, _NN)
    return x


def _rsum(v):
    return jnp.sum(v, axis=-1, keepdims=True)


def _gdn_batch(act, beta, gc, gct, eg_all, ncb, masks):
    causal, strict, _ = masks

    def st(fn):
        return jnp.stack([fn(c, h, slice(c * CH, (c + 1) * CH)) for c in range(ncb) for h in range(GDN_H)])

    qr = st(lambda c, h, r: act[r, h * 128:(h + 1) * 128])
    kr = st(lambda c, h, r: act[r, 512 + h * 128:512 + (h + 1) * 128])
    vh = st(lambda c, h, r: act[r, 1024 + h * 128:1024 + (h + 1) * 128])
    bh = st(lambda c, h, r: beta[r, h:h + 1])
    gcol = st(lambda c, h, r: gc[r, 4 + h:5 + h])
    grow = st(lambda c, h, r: gct[4 + h:5 + h, r])
    eg = st(lambda c, h, r: eg_all[r, 4 + h:5 + h])
    glast = st(lambda c, h, r: gc[(c + 1) * CH - 1:(c + 1) * CH, 4 + h:5 + h])
    rq = lax.rsqrt(_rsum(qr * qr) + EPS)
    rk = lax.rsqrt(_rsum(kr * kr) + EPS)
    qn = qr * rq
    kh = kr * rk
    qh = qn * (GDN_D ** -0.5)
    decay = jnp.exp(jnp.where(causal, gcol - grow, NEG))
    kb = kh * bh
    kd_scale = jnp.exp(glast - gcol)
    return dict(qn=qn, rq=rq, kh=kh, rk=rk, qh=qh, vh=vh, bh=bh, eg=eg, decay=decay, kb=kb, vb=vh * bh, kg=kb * eg,
                qg=qh * eg, kd_scale=kd_scale, kdec=kh * kd_scale, egl=jnp.exp(glast),
                a=jnp.where(strict, _bdot(kb, kh, _NT) * decay, 0.0), attn=_bdot(qh, kh, _NT) * decay)


def _make_gdn_fwd(seq, tb):
    ncb = tb // CH
    nb = seq // tb
    n = ncb * GDN_H

    def body(pg_ref, sm_ref, cw_ref, prm_ref, nw_ref, oa_ref, st_ref, ti_ref, uw_ref, pre_ref, s_scr, halo_scr):
        @pl.when(pl.program_id(0) == 0)
        def _():
            s_scr[...] = jnp.zeros_like(s_scr)
            halo_scr[...] = jnp.zeros_like(halo_scr)

        masks = _masks()
        sm = sm_ref[...]
        raw, pre, act, beta, _, _, gc = _gdn_common(pg_ref, halo_scr[...], sm, cw_ref[...], prm_ref[...], tb)
        halo_scr[...] = raw[tb - 8:tb, :]
        pre_ref[...] = pre
        d = _gdn_batch(act, beta, gc, gc.T, jnp.exp(gc), ncb, masks)
        t = _binv_unit_lower(d["a"], masks[2])
        sol = _bdot3(t, jnp.concatenate([d["vb"], d["kg"]], axis=2), _NN)
        ti_ref[...] = t.reshape(ncb, GDN_H, CH, CH)
        uw_ref[...] = sol.reshape(ncb, GDN_H, CH, 256)
        u, w = sol[:, :, :128], sol[:, :, 128:]
        vns = []
        for c in range(ncb):
            bs = slice(c * GDN_H, (c + 1) * GDN_H)
            s = s_scr[...]
            st_ref[c] = s
            vn = u[bs] - _bdot(w[bs], s, _NN)
            s_scr[...] = s * d["egl"][bs] + _bdot(d["kdec"][bs], vn, _TN)
            vns.append(vn)
        v_new = jnp.concatenate(vns, axis=0)
        s_prev = st_ref[...].reshape(n, 128, 128)
        o = _bdot(d["qg"], s_prev, _NN) + _bdot(d["attn"], v_new, _NN)
        _, _, y = _rms_fwd(o, nw_ref[0:1, :], GDN_D)
        for c in range(ncb):
            rows = slice(c * CH, (c + 1) * CH)
            for h in range(GDN_H):
                z = pg_ref[rows, 1536 + h * 128:1536 + (h + 1) * 128]
                oa_ref[rows, h * 128:(h + 1) * 128] = (y[c * GDN_H + h] * _silu(z)).astype(oa_ref.dtype)

    def call(pg, sm, cw, prm, nw, comm=None, comm_args=()):
        blk4 = lambda i: (i, 0, 0, 0)
        cx = _exchange_specs(comm)
        return pl.pallas_call(
            _with_exchange(body, comm, 5, 5, nb),
            grid=(nb,),
            in_specs=[
                pl.BlockSpec((tb, 2048), lambda i: (i, 0)),
                pl.BlockSpec((tb, 128), lambda i: (i, 0)),
                pl.BlockSpec((8, 1536), lambda i: (0, 0)),
                pl.BlockSpec((8, 128), lambda i: (0, 0)),
                pl.BlockSpec((8, 128), lambda i: (0, 0)),
            ] + cx["specs"],
            out_specs=[
                pl.BlockSpec((tb, 512), lambda i: (i, 0)),
                pl.BlockSpec((ncb, GDN_H, 128, 128), blk4),
                pl.BlockSpec((ncb, GDN_H, CH, CH), blk4),
                pl.BlockSpec((ncb, GDN_H, CH, 256), blk4),
                pl.BlockSpec((tb, 1536), lambda i: (i, 0)),
            ] + cx["specs"],
            out_shape=[
                jax.ShapeDtypeStruct((seq, 512), BF16),
                jax.ShapeDtypeStruct((seq // CH, GDN_H, 128, 128), F32),
                jax.ShapeDtypeStruct((seq // CH, GDN_H, CH, CH), F32),
                jax.ShapeDtypeStruct((seq // CH, GDN_H, CH, 256), F32),
                jax.ShapeDtypeStruct((seq, 1536), F32),
            ] + cx["out_shape"],
            scratch_shapes=[pltpu.VMEM((GDN_H, 128, 128), F32), pltpu.VMEM((8, 1536), F32)] + cx["scratch"],
            compiler_params=pltpu.CompilerParams(dimension_semantics=("arbitrary",), vmem_limit_bytes=VMEM_LIMIT,
                                                 has_side_effects=comm is not None),
            name="gdn_fwd" + cx["tag"],
        )(pg, sm, cw, prm, nw, *comm_args)

    return call


def _make_gdn_bwd(seq, tb):
    ncb = tb // CH
    nb = seq // tb
    hb = tb // 8
    n = ncb * GDN_H

    def body(pg_ref, pre_ref, sm_ref, cw_ref, prm_ref, nw_ref, st_ref, ti_ref, uw_ref, doa_ref,
             dpg_ref, dsm_ref, dcw_ref, dprm_ref, dnw_ref, ds_scr, nxt_scr):
        i = pl.program_id(0)

        @pl.when(i == 0)
        def _():
            ds_scr[...] = jnp.zeros_like(ds_scr)
            nxt_scr[...] = jnp.zeros_like(nxt_scr)
            dcw_ref[...] = jnp.zeros_like(dcw_ref)
            dprm_ref[...] = jnp.zeros_like(dprm_ref)
            dnw_ref[...] = jnp.zeros_like(dnw_ref)

        masks = _masks()
        strict = masks[1]
        sm = sm_ref[...]
        cw = cw_ref[...]
        prm = prm_ref[...]
        raw, pre, act, beta, sp_in, g, gc = _gdn_common(pg_ref, None, sm, cw, prm, tb, pre=pre_ref[...])
        nw = nw_ref[0:1, :]
        row_id = _iota2((CH, 1), 0)
        d = _gdn_batch(act, beta, gc, gc.T, jnp.exp(gc), ncb, masks)
        t = ti_ref[...].reshape(n, CH, CH)
        sol = uw_ref[...].reshape(n, CH, 256)
        u, w = sol[:, :, :128], sol[:, :, 128:]
        s_prev = st_ref[...].reshape(n, 128, 128)
        v_new = u - _bdot(w, s_prev, _NN)
        o = _bdot(d["qg"], s_prev, _NN) + _bdot(d["attn"], v_new, _NN)

        pairs = [(c, h) for c in range(ncb) for h in range(GDN_H)]
        z = jnp.stack([pg_ref[c * CH:(c + 1) * CH, 1536 + h * 128:1536 + (h + 1) * 128] for c, h in pairs])
        doa = jnp.stack([doa_ref[c * CH:(c + 1) * CH, h * 128:(h + 1) * 128] for c, h in pairs])
        on, r, y = _rms_fwd(o, nw, GDN_D)
        dz = doa * y * _dsilu(z)
        do, dnw_rows = _rms_bwd(doa * _silu(z), on, r, nw, GDN_D)
        dnw_acc = jnp.sum(jnp.sum(dnw_rows, axis=0), axis=0, keepdims=True)

        dvn_in = _bdot(d["attn"], do, _TN)
        qgtdo = _bdot(d["qg"], do, _TN)
        dvn_l, dkdec_l, dgl_l = [None] * ncb, [None] * ncb, [None] * ncb
        for c in reversed(range(ncb)):
            bs = slice(c * GDN_H, (c + 1) * GDN_H)
            dsn = ds_scr[...]
            dvn_c = dvn_in[bs] + _bdot(d["kdec"][bs], dsn, _NN)
            ds_scr[...] = d["egl"][bs] * dsn + qgtdo[bs] - _bdot(w[bs], dvn_c, _TN)
            dvn_l[c] = dvn_c
            dkdec_l[c] = _bdot(v_new[bs], dsn, _NT)
            dgl_l[c] = d["egl"][bs] * jnp.sum(_rsum(s_prev[bs] * dsn), axis=1, keepdims=True)
        dvn = jnp.concatenate(dvn_l, axis=0)
        dkdec = jnp.concatenate(dkdec_l, axis=0)
        dglast = jnp.concatenate(dgl_l, axis=0)

        dqg = _bdot(do, s_prev, _NT)
        dattn = _bdot(do, v_new, _NT)
        dw = -_bdot(dvn, s_prev, _NT)
        drhs = _bdot3(t, jnp.concatenate([dvn, dw], axis=2), _TN)
        dvb, dkg = drhs[:, :, :128], drhs[:, :, 128:]
        da = jnp.where(strict, -(_bdot(dvb, u, _NT) + _bdot(dkg, w, _NT)), 0.0)
        dp = da * d["decay"]
        dq_m = dattn * d["decay"]
        m = da * d["a"] + dattn * d["attn"]
        upper_tri = jnp.broadcast_to((_iota2((CH, CH), 1) >= _iota2((CH, CH), 0)).astype(BF16), (n, CH, CH))
        dg_in = _rsum(jnp.where(strict, _bdot(upper_tri, m, _NN), 0.0))
        dkb = _bdot(dp, d["kh"], _NN) + dkg * d["eg"]
        kdk_row = _rsum(dkdec * d["kdec"])
        dk = _bdot(dp, d["kb"], _TN) + _bdot(dq_m, d["qh"], _TN) + dkdec * d["kd_scale"] + dkb * d["bh"]
        dq = _bdot(dq_m, d["kh"], _NN) + dqg * d["eg"]
        dglast = dglast + jnp.sum(kdk_row, axis=1, keepdims=True)
        dgcol = (_rsum(dqg * d["qg"]) + _rsum(dkg * d["kg"]) - kdk_row + jnp.where(row_id == CH - 1, dglast, 0.0))
        dbeta = _rsum(dkb * d["kh"]) + _rsum(dvb * d["vh"])
        dn = dq * (GDN_D ** -0.5)
        dact_q = d["rq"] * (dn - d["qn"] * _rsum(dn * d["qn"]))
        dact_k = d["rk"] * (dk - d["kh"] * _rsum(dk * d["kh"]))
        dact_v = dvb * d["bh"]

        def lanes(v, lane0):
            return jnp.concatenate(
                [sum(_put_lane(v[c * GDN_H + h], lane0 + h) for h in range(GDN_H)) for c in range(ncb)], axis=0)

        def tokens(v):
            return jnp.concatenate(
                [jnp.concatenate([v[c * GDN_H + h] for h in range(GDN_H)], axis=1) for c in range(ncb)], axis=0)

        dbeta_all = lanes(dbeta, 0)
        dg = _dot01l(_chunk_tri(tb, upper=True), lanes(dgcol, 4)) + lanes(dg_in, 4)
        neg_ea = -jnp.exp(prm[0:1, :])
        da_raw = dg * neg_ea * _sigmoid(sp_in)
        db_raw = dbeta_all * beta * (1.0 - beta)
        dsm_ref[...] = (da_raw + db_raw).astype(dsm_ref.dtype)
        lane8 = _iota2((8, 128), 1)
        sub8 = _iota2((8, 128), 0)
        dalog = jnp.sum(dg * g, axis=0, keepdims=True)
        ddtb = jnp.sum(da_raw, axis=0, keepdims=True)
        dprm_ref[...] += jnp.where(sub8 == 0, dalog, 0.0) + jnp.where(sub8 == 1, ddtb, 0.0)
        dnw_ref[...] += jnp.where(sub8 == 0, dnw_acc, 0.0)

        dact = jnp.concatenate([tokens(dact_q), tokens(dact_k), tokens(dact_v)], axis=1)
        dpre = dact * _dsilu(pre)
        back = _conv_back(dpre, nxt_scr[...], tb)
        nxt_scr[...] = dpre[0:8, :]
        draw = back[0] * cw[3:4, :] + back[1] * cw[2:3, :] + back[2] * cw[1:2, :] + back[3] * cw[0:1, :]
        dpg_ref[:, 0:1536] = draw.astype(dpg_ref.dtype)
        dpg_ref[:, 1536:2048] = tokens(dz).astype(dpg_ref.dtype)
        sub_c = _iota2((8, 1536), 0)
        dcw_new = jnp.zeros((8, 1536), F32)
        for s_ in range(CONV_W):
            dcw_new = dcw_new + jnp.where(sub_c == 3 - s_, jnp.sum(back[s_] * raw, axis=0, keepdims=True), 0.0)
        dcw_ref[...] += dcw_new

    def call(pg, pre, sm, cw, prm, nw, st, ti, uw, doa, comm=None, comm_args=()):
        rev = lambda i: (nb - 1 - i, 0)
        const = lambda i: (0, 0)
        cx = _exchange_specs(comm)
        return pl.pallas_call(
            _with_exchange(body, comm, 10, 5, nb),
            grid=(nb,),
            in_specs=[
                pl.BlockSpec((tb, 2048), rev),
                pl.BlockSpec((tb, 1536), rev),
                pl.BlockSpec((tb, 128), rev),
                pl.BlockSpec((8, 1536), const),
                pl.BlockSpec((8, 128), const),
                pl.BlockSpec((8, 128), const),
                pl.BlockSpec((ncb, GDN_H, 128, 128), lambda i: (nb - 1 - i, 0, 0, 0)),
                pl.BlockSpec((ncb, GDN_H, CH, CH), lambda i: (nb - 1 - i, 0, 0, 0)),
                pl.BlockSpec((ncb, GDN_H, CH, 256), lambda i: (nb - 1 - i, 0, 0, 0)),
                pl.BlockSpec((tb, 512), rev),
            ] + cx["specs"],
            out_specs=[
                pl.BlockSpec((tb, 2048), rev),
                pl.BlockSpec((tb, 128), rev),
                pl.BlockSpec((8, 1536), const),
                pl.BlockSpec((8, 128), const),
                pl.BlockSpec((8, 128), const),
            ] + cx["specs"],
            out_shape=[
                jax.ShapeDtypeStruct((seq, 2048), BF16),
                jax.ShapeDtypeStruct((seq, 128), BF16),
                jax.ShapeDtypeStruct((8, 1536), F32),
                jax.ShapeDtypeStruct((8, 128), F32),
                jax.ShapeDtypeStruct((8, 128), F32),
            ] + cx["out_shape"],
            scratch_shapes=[pltpu.VMEM((GDN_H, 128, 128), F32), pltpu.VMEM((8, 1536), F32)] + cx["scratch"],
            compiler_params=pltpu.CompilerParams(dimension_semantics=("arbitrary",), vmem_limit_bytes=VMEM_LIMIT,
                                                 has_side_effects=comm is not None),
            name="gdn_bwd" + cx["tag"],
        )(pg, pre, sm, cw, prm, nw, st, ti, uw, doa, *comm_args)

    return call


def _expand_mat():
    r = _iota2((128, SSD_W), 0)
    c = _iota2((128, SSD_W), 1)
    return (jnp.right_shift(c, 6) == r).astype(F32)


def _reduce_heads(v, e):
    vh, vl = _split(v)
    eb = e.astype(BF16)
    nt = (((1,), (1,)), ((), ()))
    return (lax.dot_general(vh, eb, nt, preferred_element_type=F32)
            + lax.dot_general(vl, eb, nt, preferred_element_type=F32))


def _row8(v):
    return jnp.broadcast_to(v, (8, v.shape[1]))


def _ssd_common(ps_ref, halo8, ss, cw, cb, prm, tb, pre=None):
    raw = ps_ref[:, 0:1536]
    taps = None
    if pre is None:
        taps = _conv_taps(raw, halo8, tb)
        pre = taps[0] * cw[3:4, :] + taps[1] * cw[2:3, :] + taps[2] * cw[1:2, :] + taps[3] * cw[0:1, :] + cb[0:1, :]
    act = _silu(pre)
    dt_in = ss + prm[1:2, :]
    dt = _softplus(dt_in)
    a = dt * (-jnp.exp(prm[0:1, :]))
    acum = _dot01l(_chunk_tri(tb), a)
    e = _expand_mat()
    dt_e = _dot01r(dt, e)
    xdt = act[:, 0:SSD_W] * dt_e
    ea_e = _dot01r(jnp.exp(acum), e)
    d_e = _dot01r(_row8(prm[2:3, :]), e)[0:1, :]
    return raw, taps, pre, act, dt_in, dt, a, acum, e, dt_e, xdt, ea_e, d_e


def _ssd_chunk(act, acum, act_t, e, c):
    r0 = c * CH
    rows = slice(r0, r0 + CH)
    alast = acum[r0 + CH - 1:r0 + CH, :]
    wdec = jnp.exp(alast - acum[rows, :])
    wd_e = _dot01r(wdec, e)
    eal_e = _dot01r(_row8(jnp.exp(alast)), e)[0:1, :]
    return rows, wd_e, eal_e


def _ssd_lmat(acum, act_t, c, h, causal):
    r0 = c * CH
    acol = acum[r0:r0 + CH, h:h + 1]
    arow = act_t[h:h + 1, r0:r0 + CH]
    return jnp.exp(jnp.where(causal, acol - arow, NEG))


def _make_ssd_fwd(seq, tb):
    ncb = tb // CH
    nb = seq // tb
    hg = SSD_H // SSD_G
    gw = SSD_W // SSD_G

    def body(ps_ref, ss_ref, cw_ref, cb_ref, prm_ref, nw_ref, ob_ref, st_ref, pre_ref, y_ref, hs_scr, halo_scr):
        @pl.when(pl.program_id(0) == 0)
        def _():
            hs_scr[...] = jnp.zeros_like(hs_scr)
            halo_scr[...] = jnp.zeros_like(halo_scr)

        causal, _, _ = _masks()
        (raw, _, pre, act, _, _, _, acum, e, _, xdt, ea_e, d_e) = _ssd_common(
            ps_ref, halo_scr[...], ss_ref[...], cw_ref[...], cb_ref[...], prm_ref[...], tb)
        halo_scr[...] = raw[tb - 8:tb, :]
        pre_ref[...] = pre
        act_t = acum.T
        nw = nw_ref[0:1, :]
        for c in range(ncb):
            rows, wd_e, eal_e = _ssd_chunk(act, acum, act_t, e, c)
            st_ref[c] = hs_scr[...]
            ys = []
            for g in range(SSD_G):
                gc_ = slice(g * gw, (g + 1) * gw)
                bg = act[rows, SSD_W + g * 128:SSD_W + (g + 1) * 128]
                cg = act[rows, SSD_W + 256 + g * 128:SSD_W + 256 + (g + 1) * 128]
                cbm = _dot_nt(cg, bg)
                hs = hs_scr[:, gc_]
                yin = _dot(cg, hs)
                yh = []
                for hh in range(hg):
                    h = g * hg + hh
                    lm = _ssd_lmat(acum, act_t, c, h, causal)
                    yh.append(_dot(cbm * lm, xdt[rows, h * SSD_P:(h + 1) * SSD_P]))
                ys.append(jnp.concatenate(yh, axis=1) + yin * ea_e[rows, gc_])
                hs_scr[:, gc_] = hs * eal_e[:, gc_] + _dot_tn(bg, xdt[rows, gc_] * wd_e[:, gc_])
            y = jnp.concatenate(ys, axis=1) + act[rows, 0:SSD_W] * d_e
            y_ref[rows, :] = y
            yz = y * _silu(ps_ref[rows, 1536:2560])
            outs = [_rms_fwd(yz[:, g * gw:(g + 1) * gw], nw[:, g * gw:(g + 1) * gw], gw)[2] for g in range(SSD_G)]
            ob_ref[rows, :] = jnp.concatenate(outs, axis=1).astype(ob_ref.dtype)

    def call(ps, ss, cw, cb, prm, nw):
        const = lambda i: (0, 0)
        return pl.pallas_call(
            body,
            grid=(nb,),
            in_specs=[
                pl.BlockSpec((tb, 2560), lambda i: (i, 0)),
                pl.BlockSpec((tb, 128), lambda i: (i, 0)),
                pl.BlockSpec((8, 1536), const),
                pl.BlockSpec((8, 1536), const),
                pl.BlockSpec((8, 128), const),
                pl.BlockSpec((8, SSD_W), const),
            ],
            out_specs=[
                pl.BlockSpec((tb, SSD_W), lambda i: (i, 0)),
                pl.BlockSpec((ncb, SSD_N, SSD_W), lambda i: (i, 0, 0)),
                pl.BlockSpec((tb, 1536), lambda i: (i, 0)),
                pl.BlockSpec((tb, SSD_W), lambda i: (i, 0)),
            ],
            out_shape=[
                jax.ShapeDtypeStruct((seq, SSD_W), BF16),
                jax.ShapeDtypeStruct((seq // CH, SSD_N, SSD_W), F32),
                jax.ShapeDtypeStruct((seq, 1536), F32),
                jax.ShapeDtypeStruct((seq, SSD_W), F32),
            ],
            scratch_shapes=[pltpu.VMEM((SSD_N, SSD_W), F32), pltpu.VMEM((8, 1536), F32)],
            compiler_params=pltpu.CompilerParams(dimension_semantics=("arbitrary",), vmem_limit_bytes=VMEM_LIMIT),
            name="ssd_fwd",
        )(ps, ss, cw, cb, prm, nw)

    return call


def _make_ssd_bwd(seq, tb):
    ncb = tb // CH
    nb = seq // tb
    hb = tb // 8
    hg = SSD_H // SSD_G
    gw = SSD_W // SSD_G

    def body(ps_ref, pre_ref, y_ref, ss_ref, cw_ref, cb_ref, prm_ref, nw_ref, st_ref, dob_ref,
             dps_ref, dss_ref, dcw_ref, dcb_ref, dprm_ref, dnw_ref, dhs_scr, nxt_scr):
        i = pl.program_id(0)

        @pl.when(i == 0)
        def _():
            dhs_scr[...] = jnp.zeros_like(dhs_scr)
            nxt_scr[...] = jnp.zeros_like(nxt_scr)
            dcw_ref[...] = jnp.zeros_like(dcw_ref)
            dcb_ref[...] = jnp.zeros_like(dcb_ref)
            dprm_ref[...] = jnp.zeros_like(dprm_ref)
            dnw_ref[...] = jnp.zeros_like(dnw_ref)

        causal, _, _ = _masks()
        cw = cw_ref[...]
        prm = prm_ref[...]
        (raw, _, pre, act, dt_in, dt, a, acum, e, dt_e, xdt, ea_e, d_e) = _ssd_common(
            ps_ref, None, ss_ref[...], cw, cb_ref[...], prm, tb, pre=pre_ref[...])
        act_t = acum.T
        nw = nw_ref[0:1, :]
        row_id = _iota2((CH, 1), 0)

        dx_l, db_l, dc_l, dz_l, dacum_l, ddt_l, da_in_l = ([None] * ncb for _ in range(7))
        upper_tri = (_iota2((CH, CH), 1) >= _iota2((CH, CH), 0)).astype(F32)
        below = jnp.bitwise_and(_iota2((CH, gw), 1), CH - 1) < _iota2((CH, gw), 0)
        dnw_acc = jnp.zeros((1, SSD_W), F32)
        dd_acc = jnp.zeros((1, SSD_W), F32)

        for c in reversed(range(ncb)):
            rows, wd_e, eal_e = _ssd_chunk(act, acum, act_t, e, c)
            xc = act[rows, 0:SSD_W]
            z = ps_ref[rows, 1536:2560]
            dob = dob_ref[rows, :]
            sz = _silu(z)
            dy_g, dz_g, zacc_g, dxdt_g, dal_g, db_g, dc_g, da_in_g = [], [], [], [], [], [], [], []
            for g in range(SSD_G):
                gc_ = slice(g * gw, (g + 1) * gw)
                bg = act[rows, SSD_W + g * 128:SSD_W + (g + 1) * 128]
                cg = act[rows, SSD_W + 256 + g * 128:SSD_W + 256 + (g + 1) * 128]
                cbm = _dot_nt(cg, bg)
                hs = st_ref[c, :, gc_]
                yin = _dot(cg, hs)
                lmats = [_ssd_lmat(acum, act_t, c, g * hg + hh, causal) for hh in range(hg)]
                ea_g = ea_e[rows, gc_]
                y = y_ref[rows, gc_]
                yz = y * sz[:, gc_]
                on, r, _ = _rms_fwd(yz, nw[:, gc_], gw)
                dyz, dnw_rows = _rms_bwd(dob[:, gc_], on, r, nw[:, gc_], gw)
                dnw_acc = dnw_acc + _put_cols(jnp.sum(dnw_rows, axis=0, keepdims=True), g, gw)
                dy = dyz * sz[:, gc_]
                dz_g.append(dyz * y * _dsilu(z[:, gc_]))
                dd_acc = dd_acc + _put_cols(jnp.sum(dy * xc[:, gc_], axis=0, keepdims=True), g, gw)
                dhs_n = dhs_scr[:, gc_]
                dyin = dy * ea_g
                dcg = _dot_nt(dyin, hs)
                xw = xdt[rows, gc_] * wd_e[:, gc_]
                dbg = _dot_nt(xw, dhs_n)
                dxw = _dot(bg, dhs_n)
                dhs_scr[:, gc_] = dhs_n * eal_e[:, gc_] + _dot_tn(cg, dyin)
                dal_g.append(jnp.sum(hs * dhs_n, axis=0, keepdims=True) * eal_e[:, gc_]
                             + jnp.sum(dxw * xw, axis=0, keepdims=True))
                dxi, ms, dcbm = [], [], jnp.zeros((CH, CH), F32)
                for hh in range(hg):
                    h = g * hg + hh
                    hc = slice(hh * SSD_P, (hh + 1) * SSD_P)
                    dyh = dy[:, hc]
                    lm = cbm * lmats[hh]
                    dxi.append(_dot_tn(lm, dyh))
                    dlm = _dot_nt(dyh, xdt[rows, h * SSD_P:(h + 1) * SSD_P])
                    ms.append(dlm * lm)
                    dcbm = dcbm + dlm * lmats[hh]
                dx_intra = jnp.concatenate(dxi, axis=1)
                ncat = _dot(upper_tri, jnp.concatenate(ms, axis=1))
                da_in_g.append(jnp.where(below, ncat, 0.0))
                zacc_g.append(dy * yin * ea_g - dxw * xw)
                dxdt_g.append(dx_intra + dxw * wd_e[:, gc_])
                dy_g.append(dy)
                db_g.append(dbg + _dot_tn(dcbm, cg))
                dc_g.append(dcg + _dot(dcbm, bg))
            dy = jnp.concatenate(dy_g, axis=1)
            dxdt = jnp.concatenate(dxdt_g, axis=1)
            dx_l[c] = dxdt * dt_e[rows, :] + dy * d_e
            db_l[c] = jnp.concatenate(db_g, axis=1)
            dc_l[c] = jnp.concatenate(dc_g, axis=1)
            dz_l[c] = jnp.concatenate(dz_g, axis=1)
            ddt_l[c] = _reduce_heads(dxdt * xc, e)
            dalast = _reduce_heads(_row8(jnp.concatenate(dal_g, axis=1)), e)[0:1, :]
            dacum_l[c] = _reduce_heads(jnp.concatenate(zacc_g, axis=1), e) + jnp.where(row_id == CH - 1, dalast, 0.0)
            da_in_l[c] = _reduce_heads(jnp.concatenate(da_in_g, axis=1), e)

        dacum_all = jnp.concatenate(dacum_l, axis=0)
        da = _dot01l(_chunk_tri(tb, upper=True), dacum_all) + jnp.concatenate(da_in_l, axis=0)
        neg_ea = -jnp.exp(prm[0:1, :])
        ddt = jnp.concatenate(ddt_l, axis=0) + da * neg_ea
        ddt_in = ddt * _sigmoid(dt_in)
        dss_ref[...] = ddt_in.astype(dss_ref.dtype)
        sub8 = _iota2((8, 128), 0)
        dalog = jnp.sum(da * a, axis=0, keepdims=True)
        ddtb = jnp.sum(ddt_in, axis=0, keepdims=True)
        dd = _reduce_heads(_row8(dd_acc), e)[0:1, :]
        dprm_ref[...] += (jnp.where(sub8 == 0, dalog, 0.0) + jnp.where(sub8 == 1, ddtb, 0.0)
                          + jnp.where(sub8 == 2, dd, 0.0))
        dnw_ref[...] += jnp.where(_iota2((8, SSD_W), 0) == 0, dnw_acc, 0.0)

        dact = jnp.concatenate([jnp.concatenate(dx_l, axis=0), jnp.concatenate(db_l, axis=0),
                                jnp.concatenate(dc_l, axis=0)], axis=1)
        dpre = dact * _dsilu(pre)
        back = _conv_back(dpre, nxt_scr[...], tb)
        nxt_scr[...] = dpre[0:8, :]
        draw = back[0] * cw[3:4, :] + back[1] * cw[2:3, :] + back[2] * cw[1:2, :] + back[3] * cw[0:1, :]
        dps_ref[:, 0:1536] = draw.astype(dps_ref.dtype)
        dps_ref[:, 1536:2560] = jnp.concatenate(dz_l, axis=0).astype(dps_ref.dtype)
        sub_c = _iota2((8, 1536), 0)
        dcw_new = jnp.zeros((8, 1536), F32)
        for s_ in range(CONV_W):
            dcw_new = dcw_new + jnp.where(sub_c == 3 - s_, jnp.sum(back[s_] * raw, axis=0, keepdims=True), 0.0)
        dcw_ref[...] += dcw_new
        dcb_ref[...] += jnp.where(sub_c == 0, jnp.sum(dpre, axis=0, keepdims=True), 0.0)

    def call(ps, pre, y, ss, cw, cb, prm, nw, st, dob, comm=None, comm_args=()):
        rev = lambda i: (nb - 1 - i, 0)
        const = lambda i: (0, 0)
        cx = _exchange_specs(comm)
        return pl.pallas_call(
            _with_exchange(body, comm, 10, 6, nb),
            grid=(nb,),
            in_specs=[
                pl.BlockSpec((tb, 2560), rev),
                pl.BlockSpec((tb, 1536), rev),
                pl.BlockSpec((tb, SSD_W), rev),
                pl.BlockSpec((tb, 128), rev),
                pl.BlockSpec((8, 1536), const),
                pl.BlockSpec((8, 1536), const),
                pl.BlockSpec((8, 128), const),
                pl.BlockSpec((8, SSD_W), const),
                pl.BlockSpec((ncb, SSD_N, SSD_W), lambda i: (nb - 1 - i, 0, 0)),
                pl.BlockSpec((tb, SSD_W), rev),
            ] + cx["specs"],
            out_specs=[
                pl.BlockSpec((tb, 2560), rev),
                pl.BlockSpec((tb, 128), rev),
                pl.BlockSpec((8, 1536), const),
                pl.BlockSpec((8, 1536), const),
                pl.BlockSpec((8, 128), const),
                pl.BlockSpec((8, SSD_W), const),
            ] + cx["specs"],
            out_shape=[
                jax.ShapeDtypeStruct((seq, 2560), BF16),
                jax.ShapeDtypeStruct((seq, 128), BF16),
                jax.ShapeDtypeStruct((8, 1536), F32),
                jax.ShapeDtypeStruct((8, 1536), F32),
                jax.ShapeDtypeStruct((8, 128), F32),
                jax.ShapeDtypeStruct((8, SSD_W), F32),
            ] + cx["out_shape"],
            scratch_shapes=[pltpu.VMEM((SSD_N, SSD_W), F32), pltpu.VMEM((8, 1536), F32)] + cx["scratch"],
            compiler_params=pltpu.CompilerParams(dimension_semantics=("arbitrary",), vmem_limit_bytes=VMEM_LIMIT,
                                                 has_side_effects=comm is not None),
            name="ssd_bwd" + cx["tag"],
        )(ps, pre, y, ss, cw, cb, prm, nw, st, dob, *comm_args)

    return call


def _ret_consts(h):
    lg = math.log(1.0 - 2.0 ** (-5.0 - h))
    r = _iota2((CH, CH), 0)
    c = _iota2((CH, CH), 1)
    rel = (r - c).astype(F32)
    dmat = jnp.where(r >= c, jnp.exp(jnp.maximum(rel, 0.0) * lg), 0.0)
    idx = _iota2((CH, 1), 0).astype(F32)
    qdec = jnp.exp((idx + 1.0) * lg)
    kdec = jnp.exp((CH - 1.0 - idx) * lg)
    cdec = math.exp(CH * lg)
    return dmat, qdec, kdec, cdec


def _ret_batch(pr_ref, cc_ref, ss_ref, ncb):
    pairs = [(c, h) for c in range(ncb) for h in range(RET_H)]

    def st(off):
        return jnp.stack([pr_ref[c * CH:(c + 1) * CH, off + h * 128:off + (h + 1) * 128] for c, h in pairs])

    cc = jnp.stack([cc_ref[c * CH:(c + 1) * CH, :] for c, _ in pairs])
    ss = jnp.stack([ss_ref[c * CH:(c + 1) * CH, :] for c, _ in pairs])
    consts = [_ret_consts(h) for h in range(RET_H)]
    dmat = jnp.stack([consts[h][0] for _, h in pairs])
    qdec = jnp.stack([consts[h][1] for _, h in pairs])
    kdec = jnp.stack([consts[h][2] for _, h in pairs])
    cdec = jnp.stack([jnp.full((1, 1), consts[h][3], F32) for h in range(RET_H)])
    q = _rot(st(0), cc, ss)
    k = _rot(st(512), cc, ss) * (RET_D ** -0.5)
    return dict(q=q, k=k, v=st(1024), z=st(1536), cc=cc, ss=ss, dmat=dmat, qdec=qdec, kdec=kdec, cdec=cdec,
                s=_bdot(q, k, _NT) * dmat)


def _rot(t, cc, ss):
    return t * cc + pltpu.roll(t, 64, axis=t.ndim - 1) * ss


def _rot_bwd(d, cc, ss):
    return d * cc + pltpu.roll(d * ss, 64, axis=d.ndim - 1)


def _make_ret_fwd(seq, tb):
    ncb = tb // CH
    nb = seq // tb

    def body(pr_ref, cc_ref, ss_ref, nw_ref, oc_ref, st_ref, r_scr):
        @pl.when(pl.program_id(0) == 0)
        def _():
            r_scr[...] = jnp.zeros_like(r_scr)

        d = _ret_batch(pr_ref, cc_ref, ss_ref, ncb)
        kd = d["k"] * d["kdec"]
        for c in range(ncb):
            bs = slice(c * RET_H, (c + 1) * RET_H)
            rs = r_scr[...]
            st_ref[c] = rs
            r_scr[...] = rs * d["cdec"] + _bdot(kd[bs], d["v"][bs], _TN)
        r_prev = st_ref[...].reshape(ncb * RET_H, 128, 128)
        o = _bdot(d["s"], d["v"], _NN) + _bdot(d["q"], r_prev, _NN) * d["qdec"]
        _, _, y = _rms_fwd(o, nw_ref[0:1, :], RET_D)
        out = y * _silu(d["z"])
        for c in range(ncb):
            for h in range(RET_H):
                oc_ref[c * CH:(c + 1) * CH, h * 128:(h + 1) * 128] = out[c * RET_H + h].astype(oc_ref.dtype)

    def call(pr, cc, ss, nw):
        return pl.pallas_call(
            body,
            grid=(nb,),
            in_specs=[
                pl.BlockSpec((tb, 2048), lambda i: (i, 0)),
                pl.BlockSpec((tb, 128), lambda i: (i, 0)),
                pl.BlockSpec((tb, 128), lambda i: (i, 0)),
                pl.BlockSpec((8, 128), lambda i: (0, 0)),
            ],
            out_specs=[
                pl.BlockSpec((tb, 512), lambda i: (i, 0)),
                pl.BlockSpec((ncb, RET_H, 128, 128), lambda i: (i, 0, 0, 0)),
            ],
            out_shape=[
                jax.ShapeDtypeStruct((seq, 512), BF16),
                jax.ShapeDtypeStruct((seq // CH, RET_H, 128, 128), F32),
            ],
            scratch_shapes=[pltpu.VMEM((RET_H, 128, 128), F32)],
            compiler_params=pltpu.CompilerParams(dimension_semantics=("arbitrary",), vmem_limit_bytes=VMEM_LIMIT),
            name="ret_fwd",
        )(pr, cc, ss, nw)

    return call


def _make_ret_bwd(seq, tb):
    ncb = tb // CH
    nb = seq // tb

    def body(pr_ref, cc_ref, ss_ref, nw_ref, st_ref, doc_ref, dpr_ref, dnw_ref, dr_scr):
        @pl.when(pl.program_id(0) == 0)
        def _():
            dr_scr[...] = jnp.zeros_like(dr_scr)
            dnw_ref[...] = jnp.zeros_like(dnw_ref)

        nw = nw_ref[0:1, :]
        scale = RET_D ** -0.5
        n = ncb * RET_H
        d = _ret_batch(pr_ref, cc_ref, ss_ref, ncb)
        q, k, v, z, s = d["q"], d["k"], d["v"], d["z"], d["s"]
        r_prev = st_ref[...].reshape(n, 128, 128)
        o = _bdot(s, v, _NN) + _bdot(q, r_prev, _NN) * d["qdec"]
        doc = jnp.stack([doc_ref[c * CH:(c + 1) * CH, h * 128:(h + 1) * 128]
                         for c in range(ncb) for h in range(RET_H)])
        on, r, y = _rms_fwd(o, nw, RET_D)
        dz = doc * y * _dsilu(z)
        do, dnw_rows = _rms_bwd(doc * _silu(z), on, r, nw, RET_D)
        dnw_acc = jnp.sum(jnp.sum(dnw_rows, axis=0), axis=0, keepdims=True)
        dqd = do * d["qdec"]
        qtd = _bdot(q, dqd, _TN)
        drn_l = [None] * ncb
        for c in reversed(range(ncb)):
            drn_l[c] = dr_scr[...]
            dr_scr[...] = qtd[c * RET_H:(c + 1) * RET_H] + d["cdec"] * drn_l[c]
        drn = jnp.concatenate(drn_l, axis=0)
        ds = _bdot(do, v, _NT) * d["dmat"]
        dq = _rot_bwd(_bdot(ds, k, _NN) + _bdot(dqd, r_prev, _NT), d["cc"], d["ss"])
        dk = _rot_bwd((_bdot(ds, q, _TN) + _bdot(v, drn, _NT) * d["kdec"]) * scale, d["cc"], d["ss"])
        dv = _bdot(s, do, _TN) + _bdot(k * d["kdec"], drn, _NN)
        for c in range(ncb):
            rows = slice(c * CH, (c + 1) * CH)
            for h in range(RET_H):
                b = c * RET_H + h
                for j, val in enumerate((dq, dk, dv, dz)):
                    dpr_ref[rows, j * 512 + h * 128:j * 512 + (h + 1) * 128] = val[b].astype(dpr_ref.dtype)
        dnw_ref[...] += jnp.where(_iota2((8, 128), 0) == 0, dnw_acc, 0.0)

    def call(pr, cc, ss, nw, st, doc):
        rev = lambda i: (nb - 1 - i, 0)
        return pl.pallas_call(
            body,
            grid=(nb,),
            in_specs=[
                pl.BlockSpec((tb, 2048), rev),
                pl.BlockSpec((tb, 128), rev),
                pl.BlockSpec((tb, 128), rev),
                pl.BlockSpec((8, 128), lambda i: (0, 0)),
                pl.BlockSpec((ncb, RET_H, 128, 128), lambda i: (nb - 1 - i, 0, 0, 0)),
                pl.BlockSpec((tb, 512), rev),
            ],
            out_specs=[
                pl.BlockSpec((tb, 2048), rev),
                pl.BlockSpec((8, 128), lambda i: (0, 0)),
            ],
            out_shape=[
                jax.ShapeDtypeStruct((seq, 2048), BF16),
                jax.ShapeDtypeStruct((8, 128), F32),
            ],
            scratch_shapes=[pltpu.VMEM((RET_H, 128, 128), F32)],
            compiler_params=pltpu.CompilerParams(dimension_semantics=("arbitrary",), vmem_limit_bytes=VMEM_LIMIT),
            name="ret_bwd",
        )(pr, cc, ss, nw, st, doc)

    return call


def _rope_tables(seq):
    half = RET_D // 2
    inv = ROPE_BASE ** (-jnp.arange(half, dtype=F32) / half)
    ang = jnp.arange(seq, dtype=jnp.int32).astype(F32)[:, None] * inv[None, :]
    cos, sin = jnp.cos(ang), jnp.sin(ang)
    return jnp.concatenate([cos, cos], axis=1), jnp.concatenate([-sin, sin], axis=1)


SEG_G, SEG_S, SEG_R, SEG_GS, SEG_SS = (0, 2048), (2048, 4608), (4608, 6656), (6656, 6784), (6784, 6912)
NP = 6912
SEGS = (SEG_G, SEG_S, SEG_R, SEG_GS, SEG_SS)


def _resident(shape):
    return pl.BlockSpec(shape, lambda i: (0,) * len(shape), pipeline_mode=pl.Buffered(1))


def _make_inproj(seq, tl):
    def body(x_ref, pn_ref, w_ref, pg_ref, ps_ref, pr_ref, gs_ref, ss_ref, ht_ref):
        x = x_ref[...]
        _, _, hn = _rms_fwd(x, pn_ref[0:1, :], D_MODEL)
        h = hn.astype(BF16)
        ht_ref[...] = hn.T.astype(BF16)
        for (a, b), o_ref in zip(SEGS, (pg_ref, ps_ref, pr_ref, gs_ref, ss_ref)):
            o_ref[...] = jnp.dot(h, w_ref[:, a:b], preferred_element_type=F32)

    def call(x, pn, w, comm=None, comm_args=()):
        row = lambda i: (i, 0)
        cx = _exchange_specs(comm)
        return pl.pallas_call(
            _with_exchange(body, comm, 3, 6, seq // tl),
            grid=(seq // tl,),
            in_specs=[pl.BlockSpec((tl, D_MODEL), row), _resident((8, D_MODEL)), _resident((D_MODEL, NP))]
            + cx["specs"],
            out_specs=[pl.BlockSpec((tl, b - a), row) for a, b in SEGS]
            + [pl.BlockSpec((D_MODEL, tl), lambda i: (0, i))] + cx["specs"],
            out_shape=[jax.ShapeDtypeStruct((seq, b - a), F32) for a, b in SEGS]
            + [jax.ShapeDtypeStruct((D_MODEL, seq), BF16)] + cx["out_shape"],
            scratch_shapes=cx["scratch"],
            compiler_params=pltpu.CompilerParams(dimension_semantics=("arbitrary",), vmem_limit_bytes=VMEM_LIMIT,
                                                 has_side_effects=comm is not None),
            name="inproj" + cx["tag"],
        )(x, pn, w, *comm_args)

    return call


def _make_outproj(seq, tl):
    def body(oa_ref, ob_ref, oc_ref, w_ref, x_ref, qn_ref, out_ref, xn_ref):
        out = (jnp.dot(oa_ref[...], w_ref[0:512, :], preferred_element_type=F32)
               + jnp.dot(ob_ref[...], w_ref[512:1536, :], preferred_element_type=F32)
               + jnp.dot(oc_ref[...], w_ref[1536:2048, :], preferred_element_type=F32))
        out_ref[...] = out
        _, _, y = _rms_fwd(out, qn_ref[0:1, :], D_MODEL)
        xn_ref[...] = x_ref[...] + y

    def call(oa, ob, oc, w, x, qn):
        row = lambda i: (i, 0)
        return pl.pallas_call(
            body,
            grid=(seq // tl,),
            in_specs=[pl.BlockSpec((tl, 512), row), pl.BlockSpec((tl, 1024), row), pl.BlockSpec((tl, 512), row),
                      _resident((2048, D_MODEL)), pl.BlockSpec((tl, D_MODEL), row), _resident((8, D_MODEL))],
            out_specs=[pl.BlockSpec((tl, D_MODEL), row), pl.BlockSpec((tl, D_MODEL), row)],
            out_shape=[jax.ShapeDtypeStruct((seq, D_MODEL), F32), jax.ShapeDtypeStruct((seq, D_MODEL), F32)],
            compiler_params=pltpu.CompilerParams(dimension_semantics=("arbitrary",), vmem_limit_bytes=VMEM_LIMIT),
            name="outproj",
        )(oa, ob, oc, w, x, qn)

    return call


def _make_loss_head(seq, tl):
    def body(y_ref, t_ref, dy_ref, loss_ref):
        @pl.when(pl.program_id(0) == 0)
        def _():
            loss_ref[...] = jnp.zeros_like(loss_ref)

        err = y_ref[...] - t_ref[...]
        dy_ref[...] = err * (1.0 / D_MODEL)
        part = jnp.sum(jnp.sum(err * err, axis=1, keepdims=True), axis=0, keepdims=True) * (0.5 / D_MODEL)
        loss_ref[...] += jnp.where((_iota2((8, 128), 0) == 0) & (_iota2((8, 128), 1) == 0), part, 0.0)

    def call(y, t):
        row = lambda i: (i, 0)
        return pl.pallas_call(
            body,
            grid=(seq // tl,),
            in_specs=[pl.BlockSpec((tl, D_MODEL), row), pl.BlockSpec((tl, D_MODEL), row)],
            out_specs=[pl.BlockSpec((tl, D_MODEL), row), pl.BlockSpec((8, 128), lambda i: (0, 0))],
            out_shape=[jax.ShapeDtypeStruct((seq, D_MODEL), F32), jax.ShapeDtypeStruct((8, 128), F32)],
            compiler_params=pltpu.CompilerParams(dimension_semantics=("arbitrary",)),
            name="loss_head",
        )(y, t)

    return call


def _make_outproj_bwd(seq, tl):
    def body(dxn_ref, out_ref, oa_ref, ob_ref, oc_ref, w_ref, qn_ref, doa_ref, dob_ref, doc_ref, dqn_ref, dw_ref):
        @pl.when(pl.program_id(0) == 0)
        def _():
            dqn_ref[...] = jnp.zeros_like(dqn_ref)
            dw_ref[...] = jnp.zeros_like(dw_ref)

        qn = qn_ref[0:1, :]
        on, r, _ = _rms_fwd(out_ref[...], qn, D_MODEL)
        dout, dqn_rows = _rms_bwd(dxn_ref[...], on, r, qn, D_MODEL)
        dqn_ref[...] += jnp.where(_iota2((8, D_MODEL), 0) == 0, jnp.sum(dqn_rows, axis=0, keepdims=True), 0.0)
        db = dout.astype(BF16)
        nt = (((1,), (1,)), ((), ()))
        tn = (((0,), (0,)), ((), ()))
        doa_ref[...] = lax.dot_general(db, w_ref[0:512, :], nt, preferred_element_type=F32)
        dob_ref[...] = lax.dot_general(db, w_ref[512:1536, :], nt, preferred_element_type=F32)
        doc_ref[...] = lax.dot_general(db, w_ref[1536:2048, :], nt, preferred_element_type=F32)
        dw_ref[0:512, :] += lax.dot_general(oa_ref[...], db, tn, preferred_element_type=F32)
        dw_ref[512:1536, :] += lax.dot_general(ob_ref[...], db, tn, preferred_element_type=F32)
        dw_ref[1536:2048, :] += lax.dot_general(oc_ref[...], db, tn, preferred_element_type=F32)

    def call(dxn, out, oa, ob, oc, w, qn):
        row = lambda i: (i, 0)
        const = lambda i: (0, 0)
        return pl.pallas_call(
            body,
            grid=(seq // tl,),
            in_specs=[pl.BlockSpec((tl, D_MODEL), row), pl.BlockSpec((tl, D_MODEL), row),
                      pl.BlockSpec((tl, 512), row), pl.BlockSpec((tl, 1024), row), pl.BlockSpec((tl, 512), row),
                      _resident((2048, D_MODEL)), _resident((8, D_MODEL))],
            out_specs=[pl.BlockSpec((tl, 512), row), pl.BlockSpec((tl, 1024), row), pl.BlockSpec((tl, 512), row),
                       pl.BlockSpec((8, D_MODEL), const), pl.BlockSpec((2048, D_MODEL), const)],
            out_shape=[jax.ShapeDtypeStruct((seq, 512), F32), jax.ShapeDtypeStruct((seq, 1024), F32),
                       jax.ShapeDtypeStruct((seq, 512), F32), jax.ShapeDtypeStruct((8, D_MODEL), F32),
                       jax.ShapeDtypeStruct((2048, D_MODEL), F32)],
            compiler_params=pltpu.CompilerParams(dimension_semantics=("arbitrary",), vmem_limit_bytes=VMEM_LIMIT),
            name="outproj_bwd",
        )(dxn, out, oa, ob, oc, w, qn)

    return call


def _make_inproj_bwd_dx(seq, tl):
    def body(dg_ref, ds_ref, dr_ref, dgs_ref, dss_ref, w_ref, x_ref, pn_ref, dxn_ref, dx_ref, dpn_ref):
        @pl.when(pl.program_id(0) == 0)
        def _():
            dpn_ref[...] = jnp.zeros_like(dpn_ref)

        nt = (((1,), (1,)), ((), ()))
        dh = jnp.zeros((tl, D_MODEL), F32)
        for (a, b), d_ref in zip(SEGS, (dg_ref, ds_ref, dr_ref, dgs_ref, dss_ref)):
            dh = dh + lax.dot_general(d_ref[...], w_ref[:, a:b], nt, preferred_element_type=F32)
        pn = pn_ref[0:1, :]
        on, r, _ = _rms_fwd(x_ref[...], pn, D_MODEL)
        dx, dpn_rows = _rms_bwd(dh, on, r, pn, D_MODEL)
        dx_ref[...] = dx + dxn_ref[...]
        dpn_ref[...] += jnp.where(_iota2((8, D_MODEL), 0) == 0, jnp.sum(dpn_rows, axis=0, keepdims=True), 0.0)

    def call(dg, ds, dr, dgs, dss, w, x, pn, dxn, comm=None, comm_args=()):
        row = lambda i: (i, 0)
        cx = _exchange_specs(comm)
        return pl.pallas_call(
            _with_exchange(body, comm, 9, 2, seq // tl),
            grid=(seq // tl,),
            in_specs=[pl.BlockSpec((tl, b - a), row) for a, b in SEGS]
            + [_resident((D_MODEL, NP)), pl.BlockSpec((tl, D_MODEL), row), _resident((8, D_MODEL)),
               pl.BlockSpec((tl, D_MODEL), row)] + cx["specs"],
            out_specs=[pl.BlockSpec((tl, D_MODEL), row), pl.BlockSpec((8, D_MODEL), lambda i: (0, 0))] + cx["specs"],
            out_shape=[jax.ShapeDtypeStruct((seq, D_MODEL), F32), jax.ShapeDtypeStruct((8, D_MODEL), F32)]
            + cx["out_shape"],
            scratch_shapes=cx["scratch"],
            compiler_params=pltpu.CompilerParams(dimension_semantics=("arbitrary",), vmem_limit_bytes=VMEM_LIMIT,
                                                 has_side_effects=comm is not None),
            name="inproj_bwd_dx" + cx["tag"],
        )(dg, ds, dr, dgs, dss, w, x, pn, dxn, *comm_args)

    return call


def _make_inproj_bwd_dw(seq, tl, width, tn, name):
    def body(ht_ref, d_ref, dw_ref):
        @pl.when(pl.program_id(1) == 0)
        def _():
            dw_ref[...] = jnp.zeros_like(dw_ref)

        dw_ref[...] += jnp.dot(ht_ref[...], d_ref[...], preferred_element_type=F32)

    def call(ht, d):
        return pl.pallas_call(
            body,
            grid=(width // tn, seq // tl),
            in_specs=[pl.BlockSpec((D_MODEL, tl), lambda j, i: (0, i)), pl.BlockSpec((tl, tn), lambda j, i: (i, j))],
            out_specs=pl.BlockSpec((D_MODEL, tn), lambda j, i: (0, j)),
            out_shape=jax.ShapeDtypeStruct((D_MODEL, width), F32),
            compiler_params=pltpu.CompilerParams(dimension_semantics=("arbitrary", "arbitrary"),
                                                 vmem_limit_bytes=VMEM_LIMIT),
            name=name,
        )(ht, d)

    return call


ADAM_LR, ADAM_B1, ADAM_B2, ADAM_EPS, ADAM_WD, ADAM_STEP = 0.001, 0.9, 0.999, 1e-08, 0.01, 10


def _adam_math(w, g, m, v):
    m = ADAM_B1 * m + (1.0 - ADAM_B1) * g
    v = ADAM_B2 * v + (1.0 - ADAM_B2) * (g * g)
    m_hat = m / (1.0 - ADAM_B1 ** ADAM_STEP)
    v_hat = v / (1.0 - ADAM_B2 ** ADAM_STEP)
    delta = -ADAM_LR * (m_hat / (jnp.sqrt(v_hat) + ADAM_EPS) + ADAM_WD * w)
    return delta, m, v


def _adamw(w, g, m, v, name):
    shape = w.shape
    cols = shape[-1]
    rows = w.size // cols
    tr = rows if rows <= 512 else 256
    assert rows % tr == 0

    def body(w_ref, g_ref, m_ref, v_ref, d_ref, mo_ref, vo_ref):
        d_ref[...], mo_ref[...], vo_ref[...] = _adam_math(w_ref[...], g_ref[...], m_ref[...], v_ref[...])

    spec = pl.BlockSpec((tr, cols), lambda i: (i, 0))
    outs = pl.pallas_call(
        body,
        grid=(rows // tr,),
        in_specs=[spec] * 4,
        out_specs=[spec] * 3,
        out_shape=[jax.ShapeDtypeStruct((rows, cols), F32)] * 3,
        compiler_params=pltpu.CompilerParams(dimension_semantics=("arbitrary",), vmem_limit_bytes=VMEM_LIMIT),
        name=name,
    )(*[a.reshape(rows, cols) for a in (w, g, m, v)])
    return (g,) + tuple(o.reshape(shape) for o in outs)


def _adamw_pairs(w, mine, theirs, m, v, name):
    na, r, cols = w.shape
    assert na == 2
    tr = 256
    assert r % tr == 0

    def body(w_ref, a0_ref, b0_ref, a1_ref, b1_ref, m_ref, v_ref, g_ref, d_ref, mo_ref, vo_ref):
        g = jnp.where(pl.program_id(0) == 0, a0_ref[...] + b0_ref[...], a1_ref[...] + b1_ref[...])
        g_ref[...] = g
        d_ref[...], mo_ref[...], vo_ref[...] = _adam_math(w_ref[...], g, m_ref[...], v_ref[...])

    full = pl.BlockSpec((None, tr, cols), lambda a, i: (a, i, 0))
    one = pl.BlockSpec((None, tr, cols), lambda a, i: (0, i, 0))
    return pl.pallas_call(
        body,
        grid=(na, r // tr),
        in_specs=[full, one, one, one, one, full, full],
        out_specs=[full] * 4,
        out_shape=[jax.ShapeDtypeStruct(w.shape, F32)] * 4,
        compiler_params=pltpu.CompilerParams(dimension_semantics=("arbitrary",) * 2, vmem_limit_bytes=VMEM_LIMIT),
        name=name,
    )(w, mine[0], theirs[0], mine[1], theirs[1], m, v)


MESH = pl.DeviceIdType.MESH
ANY = pl.BlockSpec(memory_space=pl.ANY)
CHIP_REL = ((1, 0), (0, 1), (1, 1))


def _flip(v, d):
    return 1 - v if d else v


def _ag_chips(arrs, name):
    n = len(arrs)

    def body(*refs):
        ins, outs = refs[:n], refs[n:2 * n]
        send_sems, recv_sems, loc_sems = refs[2 * n:]
        x, y, c = lax.axis_index("x"), lax.axis_index("y"), lax.axis_index("c")
        me = 2 * x + y

        def remote(a, k, slot):
            dx, dy = CHIP_REL[k]
            return pltpu.make_async_remote_copy(
                src_ref=ins[a], dst_ref=outs[a].at[slot], send_sem=send_sems.at[a * 3 + k],
                recv_sem=recv_sems.at[a * 3 + k], device_id=(_flip(x, dx), _flip(y, dy), c), device_id_type=MESH)

        local = [pltpu.make_async_copy(ins[a], outs[a].at[me], loc_sems.at[a]) for a in range(n)]
        for cp in local:
            cp.start()
        for a in range(n):
            for k in range(3):
                remote(a, k, me).start()
        for a in range(n):
            for k, (dx, dy) in enumerate(CHIP_REL):
                remote(a, k, 2 * _flip(x, dx) + _flip(y, dy)).wait_recv()
        for a in range(n):
            for k in range(3):
                remote(a, k, me).wait_send()
        for cp in local:
            cp.wait()

    return pl.pallas_call(
        body,
        in_specs=[ANY] * n,
        out_specs=[ANY] * n,
        out_shape=[jax.ShapeDtypeStruct((4,) + a.shape, a.dtype) for a in arrs],
        scratch_shapes=[pltpu.SemaphoreType.DMA((3 * n,)), pltpu.SemaphoreType.DMA((3 * n,)),
                        pltpu.SemaphoreType.DMA((n,))],
        compiler_params=pltpu.CompilerParams(has_side_effects=True),
        name=name,
    )(*arrs)


class _ChipExchange:
    def __init__(self, kind, arrs):
        self.kind, self.n = kind, len(arrs)
        if kind == "gather":
            self.out_shape = [jax.ShapeDtypeStruct((4,) + a.shape, a.dtype) for a in arrs]
        else:
            self.out_shape = [jax.ShapeDtypeStruct((3,) + a.shape[1:], a.dtype) for a in arrs]
        self.scratch = [pltpu.SemaphoreType.DMA((4 * self.n,)), pltpu.SemaphoreType.DMA((4 * self.n,))]

    def _copies(self, ins, outs, sems):
        send_sems, recv_sems = sems
        x, y, c = lax.axis_index("x"), lax.axis_index("y"), lax.axis_index("c")
        me = 2 * x + y
        pairs = []
        for a in range(self.n):
            for k, (dx, dy) in enumerate(CHIP_REL):
                px, py = _flip(x, dx), _flip(y, dy)
                sem = dict(send_sem=send_sems.at[4 * a + k], recv_sem=recv_sems.at[4 * a + k],
                           device_id=(px, py, c), device_id_type=MESH)
                if self.kind == "gather":
                    out = pltpu.make_async_remote_copy(src_ref=ins[a], dst_ref=outs[a].at[me], **sem)
                    inc = pltpu.make_async_remote_copy(src_ref=ins[a], dst_ref=outs[a].at[2 * px + py], **sem)
                else:
                    out = pltpu.make_async_remote_copy(src_ref=ins[a].at[2 * px + py], dst_ref=outs[a].at[k], **sem)
                    inc = out
                pairs.append((out, inc))
            if self.kind == "gather":
                own = pltpu.make_async_remote_copy(
                    src_ref=ins[a], dst_ref=outs[a].at[me], send_sem=send_sems.at[4 * a + 3],
                    recv_sem=recv_sems.at[4 * a + 3], device_id=(x, y, 1 - c), device_id_type=MESH)
                pairs.append((own, own))
        return pairs

    def start(self, ins, outs, sems):
        for out, _ in self._copies(ins, outs, sems):
            out.start()

    def finish(self, ins, outs, sems):
        pairs = self._copies(ins, outs, sems)
        for _, inc in pairs:
            inc.wait_recv()
        for out, _ in pairs:
            out.wait_send()


def _with_exchange(body, comm, n_in, n_out, nb):
    if comm is None:
        return body

    def wrapped(*refs):
        ins = refs[:n_in]
        c_in = refs[n_in:n_in + comm.n]
        outs = refs[n_in + comm.n:n_in + comm.n + n_out]
        c_out = refs[n_in + comm.n + n_out:n_in + 2 * comm.n + n_out]
        rest = refs[n_in + 2 * comm.n + n_out:]
        scratch, sems = rest[:len(rest) - 2], rest[len(rest) - 2:]

        @pl.when(pl.program_id(0) == 0)
        def _():
            comm.start(c_in, c_out, sems)

        body(*ins, *outs, *scratch)

        @pl.when(pl.program_id(0) == nb - 1)
        def _():
            comm.finish(c_in, c_out, sems)

    return wrapped


def _exchange_specs(comm):
    if comm is None:
        return dict(specs=[], out_shape=[], scratch=[], tag="")
    return dict(specs=[pl.BlockSpec(memory_space=pl.ANY)] * comm.n, out_shape=list(comm.out_shape),
                scratch=list(comm.scratch), tag="_" + comm.kind)


def _half(ref_or_shape, half):
    r = ref_or_shape[-2] // 2
    return pl.ds(half * r, r)


def _ag_rows(arrs, name):
    n = len(arrs)

    def body(*refs):
        ins, outs = refs[:n], refs[n:2 * n]
        send_sems, recv_sems, fsend_sems, frecv_sems, loc_sems = refs[2 * n:]
        x, y, c = lax.axis_index("x"), lax.axis_index("y"), lax.axis_index("c")
        me = 2 * x + y
        sib = (x, y, 1 - c)

        def chip_of(k):
            dx, dy = CHIP_REL[k]
            return _flip(x, dx), _flip(y, dy)

        def ici(a, k, slot):
            px, py = chip_of(k)
            rows = _half(arrs[a].shape, c)
            return pltpu.make_async_remote_copy(
                src_ref=ins[a].at[:, rows, :], dst_ref=outs[a].at[slot, :, rows, :], send_sem=send_sems.at[a * 3 + k],
                recv_sem=recv_sems.at[a * 3 + k], device_id=(px, py, c), device_id_type=MESH)

        def fwd(a, k, half):
            px, py = chip_of(k)
            blk = outs[a].at[2 * px + py, :, _half(arrs[a].shape, half), :]
            return pltpu.make_async_remote_copy(
                src_ref=blk, dst_ref=blk, send_sem=fsend_sems.at[a * 3 + k], recv_sem=frecv_sems.at[a * 3 + k],
                device_id=sib, device_id_type=MESH)

        own = [pltpu.make_async_remote_copy(src_ref=ins[a], dst_ref=outs[a].at[me], send_sem=loc_sems.at[a],
                                            recv_sem=loc_sems.at[n + a], device_id=sib, device_id_type=MESH)
               for a in range(n)]
        for cp in own:
            cp.start()
        for a in range(n):
            for k in range(3):
                ici(a, k, me).start()
        for a in range(n):
            for k in range(3):
                px, py = chip_of(k)
                ici(a, k, 2 * px + py).wait_recv()
                fwd(a, k, c).start()
        for a in range(n):
            for k in range(3):
                fwd(a, k, 1 - c).wait_recv()
        for a in range(n):
            for k in range(3):
                ici(a, k, me).wait_send()
                fwd(a, k, c).wait_send()
        for cp in own:
            cp.wait()

    return pl.pallas_call(
        body,
        in_specs=[ANY] * n,
        out_specs=[ANY] * n,
        out_shape=[jax.ShapeDtypeStruct((4,) + a.shape, a.dtype) for a in arrs],
        scratch_shapes=[pltpu.SemaphoreType.DMA((3 * n,)) for _ in range(4)] + [pltpu.SemaphoreType.DMA((2 * n,))],
        compiler_params=pltpu.CompilerParams(has_side_effects=True),
        name=name,
    )(*arrs)


def _sum_chips(own, recv, chip, name):
    _, na, r, cols = own.shape
    tr = 256
    assert r % tr == 0

    def body(chip_ref, o_ref, r_ref, s_ref):
        s_ref[...] = ((o_ref[...] + r_ref[0].astype(F32)) + r_ref[1].astype(F32)) + r_ref[2].astype(F32)

    return pl.pallas_call(
        body,
        grid_spec=pltpu.PrefetchScalarGridSpec(
            num_scalar_prefetch=1,
            grid=(na, r // tr),
            in_specs=[pl.BlockSpec((None, None, tr, cols), lambda a, i, ch: (ch[0], a, i, 0)),
                      pl.BlockSpec((3, None, tr, cols), lambda a, i, ch: (0, a, i, 0))],
            out_specs=pl.BlockSpec((None, tr, cols), lambda a, i, ch: (a, i, 0))),
        out_shape=jax.ShapeDtypeStruct((na, r, cols), F32),
        compiler_params=pltpu.CompilerParams(dimension_semantics=("arbitrary",) * 2, vmem_limit_bytes=VMEM_LIMIT),
        name=name,
    )(chip, own, recv)


def _swap_sibling(arrs, name):
    n = len(arrs)

    def body(*refs):
        ins, outs = refs[:n], refs[n:2 * n]
        send_sems, recv_sems = refs[2 * n:]
        x, y, c = lax.axis_index("x"), lax.axis_index("y"), lax.axis_index("c")
        cps = [pltpu.make_async_remote_copy(src_ref=ins[a], dst_ref=outs[a], send_sem=send_sems.at[a],
                                            recv_sem=recv_sems.at[a], device_id=(x, y, 1 - c), device_id_type=MESH)
               for a in range(n)]
        for cp in cps:
            cp.start()
        for cp in cps:
            cp.wait_recv()
        for cp in cps:
            cp.wait_send()

    return pl.pallas_call(
        body,
        in_specs=[ANY] * n,
        out_specs=[ANY] * n,
        out_shape=[jax.ShapeDtypeStruct(a.shape, a.dtype) for a in arrs],
        scratch_shapes=[pltpu.SemaphoreType.DMA((n,)), pltpu.SemaphoreType.DMA((n,))],
        compiler_params=pltpu.CompilerParams(has_side_effects=True),
        name=name,
    )(*arrs)


def _allreduce_small(vec, name):
    rows = vec.shape[0]

    def body(v_ref, out_ref, gat_ref, send_sems, recv_sems):
        x, y, c = lax.axis_index("x"), lax.axis_index("y"), lax.axis_index("c")
        me = 4 * x + 2 * y + c

        def remote(k, slot):
            dx, dy, dc = (k >> 2) & 1, (k >> 1) & 1, k & 1
            return pltpu.make_async_remote_copy(
                src_ref=v_ref, dst_ref=gat_ref.at[slot], send_sem=send_sems.at[k - 1], recv_sem=recv_sems.at[k - 1],
                device_id=(_flip(x, dx), _flip(y, dy), _flip(c, dc)), device_id_type=MESH)

        gat_ref[me] = v_ref[...]
        for k in range(1, 8):
            remote(k, me).start()
        for k in range(1, 8):
            dx, dy, dc = (k >> 2) & 1, (k >> 1) & 1, k & 1
            remote(k, 4 * _flip(x, dx) + 2 * _flip(y, dy) + _flip(c, dc)).wait_recv()
        for k in range(1, 8):
            remote(k, me).wait_send()
        acc = gat_ref[0]
        for j in range(1, 8):
            acc = acc + gat_ref[j]
        out_ref[...] = acc

    vm = pl.BlockSpec(memory_space=pltpu.VMEM)
    return pl.pallas_call(
        body,
        in_specs=[vm],
        out_specs=vm,
        out_shape=jax.ShapeDtypeStruct(vec.shape, F32),
        scratch_shapes=[pltpu.VMEM((8, rows, 128), F32), pltpu.SemaphoreType.DMA((7,)), pltpu.SemaphoreType.DMA((7,))],
        compiler_params=pltpu.CompilerParams(has_side_effects=True),
        name=name,
    )(vec)


def _pad8(v, width, lane0=0):
    v = v.reshape(1, -1) if v.ndim == 1 else v
    return jnp.zeros((8, width), F32).at[:v.shape[0], lane0:lane0 + v.shape[1]].set(v.astype(F32))


def _relayout_w_in(g):
    tr = 128
    q = N_IN // 4

    def body(g_ref, o_ref):
        w = jnp.concatenate([g_ref[j] for j in range(4)], axis=1)
        z = lambda n: jnp.zeros((tr, n), w.dtype)
        o_ref[...] = jnp.concatenate([w[:, 0:2048], w[:, 2056:4616], w[:, 4632:6680],
                                      w[:, 2048:2056], z(120), w[:, 4616:4632], z(112)], axis=1)

    return pl.pallas_call(
        body,
        grid=(D_MODEL // tr,),
        in_specs=[pl.BlockSpec((4, tr, q), lambda i: (0, i, 0))],
        out_specs=pl.BlockSpec((tr, NP), lambda i: (i, 0)),
        out_shape=jax.ShapeDtypeStruct((D_MODEL, NP), g.dtype),
        compiler_params=pltpu.CompilerParams(dimension_semantics=("arbitrary",), vmem_limit_bytes=VMEM_LIMIT),
        name="relayout_w_in",
    )(g)


def _unlayout_dw_in(dg, ds, dr, dsm):
    tr = 128
    q = N_IN // 4

    def body(g_ref, s_ref, r_ref, sm_ref, o_ref, ob_ref):
        w = jnp.concatenate([g_ref[...], sm_ref[:, 0:8], s_ref[...], sm_ref[:, 128:144], r_ref[...]], axis=1)
        for j in range(4):
            blk = w[:, q * j:q * (j + 1)]
            o_ref[j] = blk
            ob_ref[j] = blk.astype(BF16)

    row = lambda i: (i, 0)
    return pl.pallas_call(
        body,
        grid=(D_MODEL // tr,),
        in_specs=[pl.BlockSpec((tr, d.shape[1]), row) for d in (dg, ds, dr, dsm)],
        out_specs=[pl.BlockSpec((4, tr, q), lambda i: (0, i, 0))] * 2,
        out_shape=[jax.ShapeDtypeStruct((4, D_MODEL, q), F32), jax.ShapeDtypeStruct((4, D_MODEL, q), BF16)],
        compiler_params=pltpu.CompilerParams(dimension_semantics=("arbitrary",), vmem_limit_bytes=VMEM_LIMIT),
        name="unlayout_dw_in",
    )(dg, ds, dr, dsm)


TB = 256
TL = 256
TK = 1024


def kernel(x, pre_norm, post_norm, w_in, gdn_conv, gdn_A_log, gdn_dt_bias, gdn_norm, ssd_conv, ssd_conv_b, ssd_A_log, ssd_dt_bias, ssd_D, ssd_norm, ret_norm, w_out, loss_target, m_pre_norm, m_post_norm, m_w_in, m_gdn_conv, m_gdn_A_log, m_gdn_dt_bias, m_gdn_norm, m_ssd_conv, m_ssd_conv_b, m_ssd_A_log, m_ssd_dt_bias, m_ssd_D, m_ssd_norm, m_ret_norm, m_w_out, v_pre_norm, v_post_norm, v_w_in, v_gdn_conv, v_gdn_A_log, v_gdn_dt_bias, v_gdn_norm, v_ssd_conv, v_ssd_conv_b, v_ssd_A_log, v_ssd_dt_bias, v_ssd_D, v_ssd_norm, v_ret_norm, v_w_out):
    seq = x.shape[1]
    chip = 2 * lax.axis_index("x") + lax.axis_index("y")
    x0 = x[0]

    wi_b, wo_b = w_in.astype(BF16), w_out.astype(BF16)
    (wi0_g,) = _ag_rows([wi_b[0:1]], "ag_weights")
    gcv_g, scv_g = _ag_chips([gdn_conv, ssd_conv], "ag_conv")
    full_w_in = _relayout_w_in
    wp = [full_w_in(wi0_g[:, 0]), None]
    wo = [None, None]
    ag0 = _ChipExchange("gather", [wo_b[0]])
    ag1 = _ChipExchange("gather", [wi_b[1], wo_b[1]])
    gcv = jnp.transpose(gcv_g, (1, 2, 0, 3)).reshape(DEPTH, CONV_W, 1536)
    scv = jnp.transpose(scv_g, (1, 2, 0, 3)).reshape(DEPTH, CONV_W, 1536)
    rope_c, rope_s = _rope_tables(seq)

    saved = []
    xc = x0
    for l in range(DEPTH):
        p = dict(
            pn=_pad8(pre_norm[l], D_MODEL), qn=_pad8(post_norm[l], D_MODEL),
            g_cw=_pad8(gcv[l], 1536), g_prm=_pad8(jnp.stack([gdn_A_log[l], gdn_dt_bias[l]]), 128, 4),
            g_nw=_pad8(gdn_norm[l], 128),
            s_cw=_pad8(scv[l], 1536), s_cb=_pad8(ssd_conv_b[l], 1536),
            s_prm=_pad8(jnp.stack([ssd_A_log[l], ssd_dt_bias[l], ssd_D[l]]), 128), s_nw=_pad8(ssd_norm[l], SSD_W),
            r_nw=_pad8(ret_norm[l], 128))
        if l == 0:
            pg, ps, pr, gs, ss, ht, wo0_g = _make_inproj(seq, TL)(xc, p["pn"], wp[l], comm=ag0, comm_args=(wo_b[0],))
            wo[0] = wo0_g.reshape(2048, D_MODEL)
        else:
            pg, ps, pr, gs, ss, ht = _make_inproj(seq, TL)(xc, p["pn"], wp[l])
        if l == 0:
            oa, stg, tig, uwg, gpre, wi1_g, wo1_g = _make_gdn_fwd(seq, TB)(
                pg, gs, p["g_cw"], p["g_prm"], p["g_nw"], comm=ag1, comm_args=(wi_b[1], wo_b[1]))
            wp[1], wo[1] = full_w_in(wi1_g), wo1_g.reshape(2048, D_MODEL)
        else:
            oa, stg, tig, uwg, gpre = _make_gdn_fwd(seq, TB)(pg, gs, p["g_cw"], p["g_prm"], p["g_nw"])
        ob, sts, spre, sy = _make_ssd_fwd(seq, TB)(ps, ss, p["s_cw"], p["s_cb"], p["s_prm"], p["s_nw"])
        oc, str_ = _make_ret_fwd(seq, TB)(pr, rope_c, rope_s, p["r_nw"])
        out, xn = _make_outproj(seq, TL)(oa, ob, oc, wo[l], xc, p["qn"])
        saved.append(dict(p=p, x=xc, ht=ht, spre=spre, sy=sy, gpre=gpre, pg=pg, ps=ps, pr=pr, gs=gs, ss=ss, stg=stg, tig=tig, uwg=uwg, sts=sts, str=str_,
                          oa=oa, ob=ob, oc=oc, out=out))
        xc = xn

    dxn, lossp = _make_loss_head(seq, TL)(xc, loss_target[0])

    small = [None] * DEPTH
    gin, gin_b, gout, q_in, q_out = ([None] * DEPTH for _ in range(5))

    for l in reversed(range(DEPTH)):
        s = saved[l]
        p = s["p"]
        doa, dob, doc, dqn, dwo_l = _make_outproj_bwd(seq, TL)(dxn, s["out"], s["oa"], s["ob"], s["oc"], wo[l], p["qn"])
        gout[l] = dwo_l.reshape(4, 512, D_MODEL)
        gdn_args = (s["pg"], s["gpre"], s["gs"], p["g_cw"], p["g_prm"], p["g_nw"], s["stg"], s["tig"], s["uwg"], doa)
        if l == 0:
            payload = (gout[0].astype(BF16),)
            dpg, dgs, dcw_g, dprm_g, dnw_g, q_out[0] = _make_gdn_bwd(seq, TB)(
                *gdn_args, comm=_ChipExchange("scatter", payload), comm_args=payload)
        else:
            dpg, dgs, dcw_g, dprm_g, dnw_g = _make_gdn_bwd(seq, TB)(*gdn_args)
        ssd_args = (s["ps"], s["spre"], s["sy"], s["ss"], p["s_cw"], p["s_cb"], p["s_prm"], p["s_nw"], s["sts"], dob)
        if l == 0:
            payload = (gin_b[1], gout[1].astype(BF16))
            dps, dss, dcw_s, dcb_s, dprm_s, dnw_s, q_in[1], q_out[1] = _make_ssd_bwd(seq, TB)(
                *ssd_args, comm=_ChipExchange("scatter", payload), comm_args=payload)
        else:
            dps, dss, dcw_s, dcb_s, dprm_s, dnw_s = _make_ssd_bwd(seq, TB)(*ssd_args)
        dpr, dnw_r = _make_ret_bwd(seq, TB)(s["pr"], rope_c, rope_s, p["r_nw"], s["str"], doc)
        dws = [_make_inproj_bwd_dw(seq, TK, d.shape[1], tn, f"inproj_bwd_dw{i}")(s["ht"], d)
               for i, (d, tn) in enumerate(((dpg, 1024), (dps, 1280), (dpr, 1024),
                                            (jnp.concatenate([dgs, dss], axis=1), 256)))]
        gin[l], gin_b[l] = _unlayout_dw_in(*dws)
        dx_args = (dpg, dps, dpr, dgs, dss, wp[l], s["x"], p["pn"], dxn)
        if l == 0:
            payload = (gin_b[0],)
            dx, dpn, q_in[0] = _make_inproj_bwd_dx(seq, TL)(
                *dx_args, comm=_ChipExchange("scatter", payload), comm_args=payload)
        else:
            dx, dpn = _make_inproj_bwd_dx(seq, TL)(*dx_args)
        small[l] = [dpn[0], dqn[0], dcw_g[0:4].reshape(-1), dprm_g[0, 4:8], dprm_g[1, 4:8], dnw_g[0],
                    dcw_s[0:4].reshape(-1), dcb_s[0], dprm_s[0, 0:16], dprm_s[1, 0:16], dprm_s[2, 0:16],
                    dnw_s[0], dnw_r[0]]
        dxn = dx
    grad_x = dxn[None]

    sizes = [a.shape[0] for a in small[0]]
    flat = jnp.concatenate(small[0] + small[1] + [lossp[0, 0:1]])
    n_flat = flat.shape[0]
    rows = -(-n_flat // 1024) * 8
    red = _allreduce_small(jnp.pad(flat, (0, rows * 128 - n_flat)).reshape(rows, 128), "allreduce_small").reshape(-1)
    per = sum(sizes)
    loss = red[2 * per]

    def pick(i):
        off = sum(sizes[:i])
        return jnp.stack([red[l * per + off:l * per + off + sizes[i]] for l in range(DEPTH)])

    g_small = dict(
        pre_norm=pick(0), post_norm=pick(1),
        gdn_conv=lax.dynamic_slice_in_dim(pick(2).reshape(DEPTH, CONV_W, 1536), chip * 384, 384, axis=2),
        gdn_A_log=pick(3), gdn_dt_bias=pick(4), gdn_norm=pick(5),
        ssd_conv=lax.dynamic_slice_in_dim(pick(6).reshape(DEPTH, CONV_W, 1536), chip * 384, 384, axis=2),
        ssd_conv_b=pick(7), ssd_A_log=pick(8), ssd_dt_bias=pick(9), ssd_D=pick(10), ssd_norm=pick(11),
        ret_norm=pick(12))

    chip1 = chip.astype(jnp.int32).reshape(1)
    s_in = [_sum_chips(gin[l][:, None], q_in[l][:, None], chip1, f"sum_chips_w_in{l}") for l in range(DEPTH)]
    s_out = [_sum_chips(gout[l][:, None], q_out[l][:, None], chip1, f"sum_chips_w_out{l}") for l in range(DEPTH)]
    t_all = _swap_sibling(s_in + s_out, "swap_grads")
    t_in, t_out = t_all[:DEPTH], t_all[DEPTH:]

    weights = dict(pre_norm=pre_norm, post_norm=post_norm, w_in=w_in, gdn_conv=gdn_conv, gdn_A_log=gdn_A_log,
                   gdn_dt_bias=gdn_dt_bias, gdn_norm=gdn_norm, ssd_conv=ssd_conv, ssd_conv_b=ssd_conv_b,
                   ssd_A_log=ssd_A_log, ssd_dt_bias=ssd_dt_bias, ssd_D=ssd_D, ssd_norm=ssd_norm, ret_norm=ret_norm,
                   w_out=w_out)
    ms = dict(pre_norm=m_pre_norm, post_norm=m_post_norm, w_in=m_w_in, gdn_conv=m_gdn_conv, gdn_A_log=m_gdn_A_log,
              gdn_dt_bias=m_gdn_dt_bias, gdn_norm=m_gdn_norm, ssd_conv=m_ssd_conv, ssd_conv_b=m_ssd_conv_b,
              ssd_A_log=m_ssd_A_log, ssd_dt_bias=m_ssd_dt_bias, ssd_D=m_ssd_D, ssd_norm=m_ssd_norm,
              ret_norm=m_ret_norm, w_out=m_w_out)
    vs = dict(pre_norm=v_pre_norm, post_norm=v_post_norm, w_in=v_w_in, gdn_conv=v_gdn_conv, gdn_A_log=v_gdn_A_log,
              gdn_dt_bias=v_gdn_dt_bias, gdn_norm=v_gdn_norm, ssd_conv=v_ssd_conv, ssd_conv_b=v_ssd_conv_b,
              ssd_A_log=v_ssd_A_log, ssd_dt_bias=v_ssd_dt_bias, ssd_D=v_ssd_D, ssd_norm=v_ssd_norm,
              ret_norm=v_ret_norm, w_out=v_w_out)
    names = list(weights)
    res = {}
    for nme in names:
        if nme == "w_in":
            res[nme] = _adamw_pairs(w_in, s_in, t_in, m_w_in, v_w_in, "adamw_w_in")
        elif nme == "w_out":
            res[nme] = _adamw_pairs(w_out, s_out, t_out, m_w_out, v_w_out, "adamw_w_out")
        else:
            res[nme] = _adamw(weights[nme], g_small[nme], ms[nme], vs[nme], "adamw_" + nme)
    return (loss, grad_x, *[res[n][0] for n in names], *[res[n][1] for n in names],
            *[res[n][2] for n in names], *[res[n][3] for n in names])
```

```python
import functools
import math

import jax
import jax.numpy as jnp
from jax import lax
from jax.experimental import pallas as pl
from jax.experimental.pallas import tpu as pltpu

F32 = jnp.float32
BF16 = jnp.bfloat16
HI = lax.Precision.HIGHEST

D_MODEL = 1024
DEPTH = 2
CH = 64
CONV_W = 4
EPS = 1e-6
GDN_H, GDN_D = 4, 128
SSD_H, SSD_P, SSD_N, SSD_G = 16, 64, 128, 2
SSD_W = SSD_H * SSD_P
RET_H, RET_D = 4, 128
ROPE_BASE = 10000.0
N_IN = 6680
NEG = -1e30

VMEM_LIMIT = 56 * 1024 * 1024


def _dot(a, b):
    return jnp.dot(a.astype(BF16), b.astype(BF16), preferred_element_type=F32)


def _dot_nt(a, b):
    return lax.dot_general(a.astype(BF16), b.astype(BF16), (((1,), (1,)), ((), ())), preferred_element_type=F32)


def _dot_tn(a, b):
    return lax.dot_general(a.astype(BF16), b.astype(BF16), (((0,), (0,)), ((), ())), preferred_element_type=F32)


def _split(a):
    hi = a.astype(BF16)
    return hi, (a - hi.astype(F32)).astype(BF16)


def _dot01l(m, v):
    vh, vl = _split(v)
    mb = m.astype(BF16)
    return jnp.dot(mb, vh, preferred_element_type=F32) + jnp.dot(mb, vl, preferred_element_type=F32)


def _dot01r(v, m):
    vh, vl = _split(v)
    mb = m.astype(BF16)
    return jnp.dot(vh, mb, preferred_element_type=F32) + jnp.dot(vl, mb, preferred_element_type=F32)


def _sigmoid(x):
    return jax.nn.sigmoid(x)


def _silu(x):
    return x * _sigmoid(x)


def _dsilu(x):
    s = _sigmoid(x)
    return s * (1.0 + x * (1.0 - s))


def _softplus(x):
    return jnp.maximum(x, 0.0) + jnp.log1p(jnp.exp(-jnp.abs(x)))


def _iota2(shape, dim):
    return lax.broadcasted_iota(jnp.int32, shape, dim)


def _chunk_tri(tb, upper=False):
    r = _iota2((tb, tb), 0)
    c = _iota2((tb, tb), 1)
    same = jnp.right_shift(r, 6) == jnp.right_shift(c, 6)
    return (same & ((c >= r) if upper else (c <= r))).astype(F32)


def _masks():
    r = _iota2((CH, CH), 0)
    c = _iota2((CH, CH), 1)
    return r >= c, r > c, (r == c).astype(F32)


def _put_lane(col, lane_idx, width=128):
    lane = _iota2((col.shape[0], width), 1)
    return jnp.where(lane == lane_idx, col, 0.0)


def _conv_taps(raw, halo8, tb):
    ext = jnp.concatenate([halo8, raw], axis=0)
    return [raw] + [pltpu.roll(ext, s, axis=0)[8:] for s in (1, 2, 3)]


def _conv_back(dpre, nxt8, tb):
    ext = jnp.concatenate([dpre, nxt8], axis=0)
    return [dpre] + [pltpu.roll(ext, tb + 8 - s, axis=0)[:tb] for s in (1, 2, 3)]


def _rms_fwd(o, w, n):
    r = lax.rsqrt(jnp.sum(o * o, axis=-1, keepdims=True) * (1.0 / n) + EPS)
    on = o * r
    return on, r, on * w


def _rms_bwd(dy, on, r, w, n):
    don = dy * w
    return r * (don - on * (jnp.sum(don * on, axis=-1, keepdims=True) * (1.0 / n))), dy * on


def _put_cols(v, g, gw):
    z = jnp.zeros_like(v)
    return jnp.concatenate([v, z] if g == 0 else [z, v], axis=1)


def _gdn_common(pg_ref, halo8, sm, cw, prm, tb, pre=None):
    raw = pg_ref[:, 0:1536]
    if pre is None:
        taps = _conv_taps(raw, halo8, tb)
        pre = taps[0] * cw[3:4, :] + taps[1] * cw[2:3, :] + taps[2] * cw[1:2, :] + taps[3] * cw[0:1, :]
    act = _silu(pre)
    beta = _sigmoid(sm)
    sp_in = sm + prm[1:2, :]
    g = -jnp.exp(prm[0:1, :]) * _softplus(sp_in)
    gc = _dot01l(_chunk_tri(tb), g)
    return raw, pre, act, beta, sp_in, g, gc


_NN = (((2,), (1,)), ((0,), (0,)))
_NT = (((2,), (2,)), ((0,), (0,)))
_TN = (((1,), (1,)), ((0,), (0,)))


def _bdot(a, b, dn):
    return lax.dot_general(a.astype(BF16), b.astype(BF16), dn, preferred_element_type=F32)


def _dot3_parts(ah, al, bh, bl, dn):
    f = lambda p, q: lax.dot_general(p, q, dn, preferred_element_type=F32)
    return f(ah, bh) + (f(ah, bl) + f(al, bh))


def _bdot3(a, b, dn):
    ah, al = _split(a)
    bh, bl = _split(b)
    return _dot3_parts(ah, al, bh, bl, dn)


def _binv_unit_lower(a, eye):
    r = _iota2((CH, CH), 0)
    c = _iota2((CH, CH), 1)
    d = eye - jnp.where((jnp.right_shift(r, 1) == jnp.right_shift(c, 1)), a, 0.0)
    ah, al = _split(a)
    zero = jnp.zeros((), BF16)
    for lb in range(1, 6):
        same = jnp.right_shift(r, lb + 1) == jnp.right_shift(c, lb + 1)
        low = (jnp.bitwise_and(jnp.right_shift(r, lb), 1) == 1) & (jnp.bitwise_and(jnp.right_shift(c, lb), 1) == 0)
        oh, ol = jnp.where(same & low, ah, zero), jnp.where(same & low, al, zero)
        dh, dl = _split(d)
        t = _dot3_parts(oh, ol, dh, dl, _NN)
        th, tl = _split(t)
        d = d - _dot3_parts(dh, dl, th, tl, _NN)
    return d


def _rsum(v):
    return jnp.sum(v, axis=-1, keepdims=True)


def _gdn_batch(act, beta, gc, gct, eg_all, ncb, masks):
    causal, strict, _ = masks

    def st(fn):
        return jnp.stack([fn(c, h, slice(c * CH, (c + 1) * CH)) for c in range(ncb) for h in range(GDN_H)])

    qr = st(lambda c, h, r: act[r, h * 128:(h + 1) * 128])
    kr = st(lambda c, h, r: act[r, 512 + h * 128:512 + (h + 1) * 128])
    vh = st(lambda c, h, r: act[r, 1024 + h * 128:1024 + (h + 1) * 128])
    bh = st(lambda c, h, r: beta[r, h:h + 1])
    gcol = st(lambda c, h, r: gc[r, 4 + h:5 + h])
    grow = st(lambda c, h, r: gct[4 + h:5 + h, r])
    eg = st(lambda c, h, r: eg_all[r, 4 + h:5 + h])
    glast = st(lambda c, h, r: gc[(c + 1) * CH - 1:(c + 1) * CH, 4 + h:5 + h])
    rq = lax.rsqrt(_rsum(qr * qr) + EPS)
    rk = lax.rsqrt(_rsum(kr * kr) + EPS)
    qn = qr * rq
    kh = kr * rk
    qh = qn * (GDN_D ** -0.5)
    decay = jnp.exp(jnp.where(causal, gcol - grow, NEG))
    kb = kh * bh
    kd_scale = jnp.exp(glast - gcol)
    return dict(qn=qn, rq=rq, kh=kh, rk=rk, qh=qh, vh=vh, bh=bh, eg=eg, decay=decay, kb=kb, vb=vh * bh, kg=kb * eg,
                qg=qh * eg, kd_scale=kd_scale, kdec=kh * kd_scale, egl=jnp.exp(glast),
                a=jnp.where(strict, _bdot(kb, kh, _NT) * decay, 0.0), attn=_bdot(qh, kh, _NT) * decay)


def _make_gdn_fwd(seq, tb):
    ncb = tb // CH
    nb = seq // tb
    n = ncb * GDN_H

    def body(pg_ref, sm_ref, cw_ref, prm_ref, nw_ref, oa_ref, st_ref, ti_ref, uw_ref, pre_ref, s_scr, halo_scr):
        @pl.when(pl.program_id(0) == 0)
        def _():
            s_scr[...] = jnp.zeros_like(s_scr)
            halo_scr[...] = jnp.zeros_like(halo_scr)

        masks = _masks()
        sm = sm_ref[...]
        raw, pre, act, beta, _, _, gc = _gdn_common(pg_ref, halo_scr[...], sm, cw_ref[...], prm_ref[...], tb)
        halo_scr[...] = raw[tb - 8:tb, :]
        pre_ref[...] = pre
        d = _gdn_batch(act, beta, gc, gc.T, jnp.exp(gc), ncb, masks)
        t = _binv_unit_lower(d["a"], masks[2])
        sol = _bdot3(t, jnp.concatenate([d["vb"], d["kg"]], axis=2), _NN)
        ti_ref[...] = t.reshape(ncb, GDN_H, CH, CH)
        uw_ref[...] = sol.reshape(ncb, GDN_H, CH, 256)
        u, w = sol[:, :, :128], sol[:, :, 128:]
        vns = []
        for c in range(ncb):
            bs = slice(c * GDN_H, (c + 1) * GDN_H)
            s = s_scr[...]
            st_ref[c] = s
            vn = u[bs] - _bdot(w[bs], s, _NN)
            s_scr[...] = s * d["egl"][bs] + _bdot(d["kdec"][bs], vn, _TN)
            vns.append(vn)
        v_new = jnp.concatenate(vns, axis=0)
        s_prev = st_ref[...].reshape(n, 128, 128)
        o = _bdot(d["qg"], s_prev, _NN) + _bdot(d["attn"], v_new, _NN)
        _, _, y = _rms_fwd(o, nw_ref[0:1, :], GDN_D)
        for c in range(ncb):
            rows = slice(c * CH, (c + 1) * CH)
            for h in range(GDN_H):
                z = pg_ref[rows, 1536 + h * 128:1536 + (h + 1) * 128]
                oa_ref[rows, h * 128:(h + 1) * 128] = (y[c * GDN_H + h] * _silu(z)).astype(oa_ref.dtype)

    def call(pg, sm, cw, prm, nw, comm=None, comm_args=()):
        blk4 = lambda i: (i, 0, 0, 0)
        cx = _exchange_specs(comm)
        return pl.pallas_call(
            _with_exchange(body, comm, 5, 5, nb),
            grid=(nb,),
            in_specs=[
                pl.BlockSpec((tb, 2048), lambda i: (i, 0)),
                pl.BlockSpec((tb, 128), lambda i: (i, 0)),
                pl.BlockSpec((8, 1536), lambda i: (0, 0)),
                pl.BlockSpec((8, 128), lambda i: (0, 0)),
                pl.BlockSpec((8, 128), lambda i: (0, 0)),
            ] + cx["specs"],
            out_specs=[
                pl.BlockSpec((tb, 512), lambda i: (i, 0)),
                pl.BlockSpec((ncb, GDN_H, 128, 128), blk4),
                pl.BlockSpec((ncb, GDN_H, CH, CH), blk4),
                pl.BlockSpec((ncb, GDN_H, CH, 256), blk4),
                pl.BlockSpec((tb, 1536), lambda i: (i, 0)),
            ] + cx["specs"],
            out_shape=[
                jax.ShapeDtypeStruct((seq, 512), BF16),
                jax.ShapeDtypeStruct((seq // CH, GDN_H, 128, 128), F32),
                jax.ShapeDtypeStruct((seq // CH, GDN_H, CH, CH), F32),
                jax.ShapeDtypeStruct((seq // CH, GDN_H, CH, 256), F32),
                jax.ShapeDtypeStruct((seq, 1536), F32),
            ] + cx["out_shape"],
            scratch_shapes=[pltpu.VMEM((GDN_H, 128, 128), F32), pltpu.VMEM((8, 1536), F32)] + cx["scratch"],
            compiler_params=pltpu.CompilerParams(dimension_semantics=("arbitrary",), vmem_limit_bytes=VMEM_LIMIT,
                                                 has_side_effects=comm is not None),
            name="gdn_fwd" + cx["tag"],
        )(pg, sm, cw, prm, nw, *comm_args)

    return call


def _make_gdn_bwd(seq, tb):
    ncb = tb // CH
    nb = seq // tb
    hb = tb // 8
    n = ncb * GDN_H

    def body(pg_ref, pre_ref, sm_ref, cw_ref, prm_ref, nw_ref, st_ref, ti_ref, uw_ref, doa_ref,
             dpg_ref, dsm_ref, dcw_ref, dprm_ref, dnw_ref, ds_scr, nxt_scr):
        i = pl.program_id(0)

        @pl.when(i == 0)
        def _():
            ds_scr[...] = jnp.zeros_like(ds_scr)
            nxt_scr[...] = jnp.zeros_like(nxt_scr)
            dcw_ref[...] = jnp.zeros_like(dcw_ref)
            dprm_ref[...] = jnp.zeros_like(dprm_ref)
            dnw_ref[...] = jnp.zeros_like(dnw_ref)

        masks = _masks()
        strict = masks[1]
        sm = sm_ref[...]
        cw = cw_ref[...]
        prm = prm_ref[...]
        raw, pre, act, beta, sp_in, g, gc = _gdn_common(pg_ref, None, sm, cw, prm, tb, pre=pre_ref[...])
        nw = nw_ref[0:1, :]
        row_id = _iota2((CH, 1), 0)
        d = _gdn_batch(act, beta, gc, gc.T, jnp.exp(gc), ncb, masks)
        t = ti_ref[...].reshape(n, CH, CH)
        sol = uw_ref[...].reshape(n, CH, 256)
        u, w = sol[:, :, :128], sol[:, :, 128:]
        s_prev = st_ref[...].reshape(n, 128, 128)
        v_new = u - _bdot(w, s_prev, _NN)
        o = _bdot(d["qg"], s_prev, _NN) + _bdot(d["attn"], v_new, _NN)

        pairs = [(c, h) for c in range(ncb) for h in range(GDN_H)]
        z = jnp.stack([pg_ref[c * CH:(c + 1) * CH, 1536 + h * 128:1536 + (h + 1) * 128] for c, h in pairs])
        doa = jnp.stack([doa_ref[c * CH:(c + 1) * CH, h * 128:(h + 1) * 128] for c, h in pairs])
        on, r, y = _rms_fwd(o, nw, GDN_D)
        dz = doa * y * _dsilu(z)
        do, dnw_rows = _rms_bwd(doa * _silu(z), on, r, nw, GDN_D)
        dnw_acc = jnp.sum(jnp.sum(dnw_rows, axis=0), axis=0, keepdims=True)

        dvn_in = _bdot(d["attn"], do, _TN)
        qgtdo = _bdot(d["qg"], do, _TN)
        dvn_l, dkdec_l, dgl_l = [None] * ncb, [None] * ncb, [None] * ncb
        for c in reversed(range(ncb)):
            bs = slice(c * GDN_H, (c + 1) * GDN_H)
            dsn = ds_scr[...]
            dvn_c = dvn_in[bs] + _bdot(d["kdec"][bs], dsn, _NN)
            ds_scr[...] = d["egl"][bs] * dsn + qgtdo[bs] - _bdot(w[bs], dvn_c, _TN)
            dvn_l[c] = dvn_c
            dkdec_l[c] = _bdot(v_new[bs], dsn, _NT)
            dgl_l[c] = d["egl"][bs] * jnp.sum(_rsum(s_prev[bs] * dsn), axis=1, keepdims=True)
        dvn = jnp.concatenate(dvn_l, axis=0)
        dkdec = jnp.concatenate(dkdec_l, axis=0)
        dglast = jnp.concatenate(dgl_l, axis=0)

        dqg = _bdot(do, s_prev, _NT)
        dattn = _bdot(do, v_new, _NT)
        dw = -_bdot(dvn, s_prev, _NT)
        drhs = _bdot3(t, jnp.concatenate([dvn, dw], axis=2), _TN)
        dvb, dkg = drhs[:, :, :128], drhs[:, :, 128:]
        da = jnp.where(strict, -(_bdot(dvb, u, _NT) + _bdot(dkg, w, _NT)), 0.0)
        dp = da * d["decay"]
        dq_m = dattn * d["decay"]
        m = da * d["a"] + dattn * d["attn"]
        upper_tri = jnp.broadcast_to((_iota2((CH, CH), 1) >= _iota2((CH, CH), 0)).astype(BF16), (n, CH, CH))
        dg_in = _rsum(jnp.where(strict, _bdot(upper_tri, m, _NN), 0.0))
        dkb = _bdot(dp, d["kh"], _NN) + dkg * d["eg"]
        kdk_row = _rsum(dkdec * d["kdec"])
        dk = _bdot(dp, d["kb"], _TN) + _bdot(dq_m, d["qh"], _TN) + dkdec * d["kd_scale"] + dkb * d["bh"]
        dq = _bdot(dq_m, d["kh"], _NN) + dqg * d["eg"]
        dglast = dglast + jnp.sum(kdk_row, axis=1, keepdims=True)
        dgcol = (_rsum(dqg * d["qg"]) + _rsum(dkg * d["kg"]) - kdk_row + jnp.where(row_id == CH - 1, dglast, 0.0))
        dbeta = _rsum(dkb * d["kh"]) + _rsum(dvb * d["vh"])
        dn = dq * (GDN_D ** -0.5)
        dact_q = d["rq"] * (dn - d["qn"] * _rsum(dn * d["qn"]))
        dact_k = d["rk"] * (dk - d["kh"] * _rsum(dk * d["kh"]))
        dact_v = dvb * d["bh"]

        def lanes(v, lane0):
            return jnp.concatenate(
                [sum(_put_lane(v[c * GDN_H + h], lane0 + h) for h in range(GDN_H)) for c in range(ncb)], axis=0)

        def tokens(v):
            return jnp.concatenate(
                [jnp.concatenate([v[c * GDN_H + h] for h in range(GDN_H)], axis=1) for c in range(ncb)], axis=0)

        dbeta_all = lanes(dbeta, 0)
        dg = _dot01l(_chunk_tri(tb, upper=True), lanes(dgcol, 4)) + lanes(dg_in, 4)
        neg_ea = -jnp.exp(prm[0:1, :])
        da_raw = dg * neg_ea * _sigmoid(sp_in)
        db_raw = dbeta_all * beta * (1.0 - beta)
        dsm_ref[...] = (da_raw + db_raw).astype(dsm_ref.dtype)
        lane8 = _iota2((8, 128), 1)
        sub8 = _iota2((8, 128), 0)
        dalog = jnp.sum(dg * g, axis=0, keepdims=True)
        ddtb = jnp.sum(da_raw, axis=0, keepdims=True)
        dprm_ref[...] += jnp.where(sub8 == 0, dalog, 0.0) + jnp.where(sub8 == 1, ddtb, 0.0)
        dnw_ref[...] += jnp.where(sub8 == 0, dnw_acc, 0.0)

        dact = jnp.concatenate([tokens(dact_q), tokens(dact_k), tokens(dact_v)], axis=1)
        dpre = dact * _dsilu(pre)
        back = _conv_back(dpre, nxt_scr[...], tb)
        nxt_scr[...] = dpre[0:8, :]
        draw = back[0] * cw[3:4, :] + back[1] * cw[2:3, :] + back[2] * cw[1:2, :] + back[3] * cw[0:1, :]
        dpg_ref[:, 0:1536] = draw.astype(dpg_ref.dtype)
        dpg_ref[:, 1536:2048] = tokens(dz).astype(dpg_ref.dtype)
        sub_c = _iota2((8, 1536), 0)
        dcw_new = jnp.zeros((8, 1536), F32)
        for s_ in range(CONV_W):
            dcw_new = dcw_new + jnp.where(sub_c == 3 - s_, jnp.sum(back[s_] * raw, axis=0, keepdims=True), 0.0)
        dcw_ref[...] += dcw_new

    def call(pg, pre, sm, cw, prm, nw, st, ti, uw, doa, comm=None, comm_args=()):
        rev = lambda i: (nb - 1 - i, 0)
        const = lambda i: (0, 0)
        cx = _exchange_specs(comm)
        return pl.pallas_call(
            _with_exchange(body, comm, 10, 5, nb),
            grid=(nb,),
            in_specs=[
                pl.BlockSpec((tb, 2048), rev),
                pl.BlockSpec((tb, 1536), rev),
                pl.BlockSpec((tb, 128), rev),
                pl.BlockSpec((8, 1536), const),
                pl.BlockSpec((8, 128), const),
                pl.BlockSpec((8, 128), const),
                pl.BlockSpec((ncb, GDN_H, 128, 128), lambda i: (nb - 1 - i, 0, 0, 0)),
                pl.BlockSpec((ncb, GDN_H, CH, CH), lambda i: (nb - 1 - i, 0, 0, 0)),
                pl.BlockSpec((ncb, GDN_H, CH, 256), lambda i: (nb - 1 - i, 0, 0, 0)),
                pl.BlockSpec((tb, 512), rev),
            ] + cx["specs"],
            out_specs=[
                pl.BlockSpec((tb, 2048), rev),
                pl.BlockSpec((tb, 128), rev),
                pl.BlockSpec((8, 1536), const),
                pl.BlockSpec((8, 128), const),
                pl.BlockSpec((8, 128), const),
            ] + cx["specs"],
            out_shape=[
                jax.ShapeDtypeStruct((seq, 2048), BF16),
                jax.ShapeDtypeStruct((seq, 128), BF16),
                jax.ShapeDtypeStruct((8, 1536), F32),
                jax.ShapeDtypeStruct((8, 128), F32),
                jax.ShapeDtypeStruct((8, 128), F32),
            ] + cx["out_shape"],
            scratch_shapes=[pltpu.VMEM((GDN_H, 128, 128), F32), pltpu.VMEM((8, 1536), F32)] + cx["scratch"],
            compiler_params=pltpu.CompilerParams(dimension_semantics=("arbitrary",), vmem_limit_bytes=VMEM_LIMIT,
                                                 has_side_effects=comm is not None),
            name="gdn_bwd" + cx["tag"],
        )(pg, pre, sm, cw, prm, nw, st, ti, uw, doa, *comm_args)

    return call


def _expand_mat():
    r = _iota2((128, SSD_W), 0)
    c = _iota2((128, SSD_W), 1)
    return (jnp.right_shift(c, 6) == r).astype(F32)


def _reduce_heads(v, e):
    vh, vl = _split(v)
    eb = e.astype(BF16)
    nt = (((1,), (1,)), ((), ()))
    return (lax.dot_general(vh, eb, nt, preferred_element_type=F32)
            + lax.dot_general(vl, eb, nt, preferred_element_type=F32))


def _row8(v):
    return jnp.broadcast_to(v, (8, v.shape[1]))


def _ssd_common(ps_ref, halo8, ss, cw, cb, prm, tb, pre=None):
    raw = ps_ref[:, 0:1536]
    taps = None
    if pre is None:
        taps = _conv_taps(raw, halo8, tb)
        pre = taps[0] * cw[3:4, :] + taps[1] * cw[2:3, :] + taps[2] * cw[1:2, :] + taps[3] * cw[0:1, :] + cb[0:1, :]
    act = _silu(pre)
    dt_in = ss + prm[1:2, :]
    dt = _softplus(dt_in)
    a = dt * (-jnp.exp(prm[0:1, :]))
    acum = _dot01l(_chunk_tri(tb), a)
    e = _expand_mat()
    dt_e = _dot01r(dt, e)
    xdt = act[:, 0:SSD_W] * dt_e
    ea_e = _dot01r(jnp.exp(acum), e)
    d_e = _dot01r(_row8(prm[2:3, :]), e)[0:1, :]
    return raw, taps, pre, act, dt_in, dt, a, acum, e, dt_e, xdt, ea_e, d_e


def _ssd_chunk(act, acum, act_t, e, c):
    r0 = c * CH
    rows = slice(r0, r0 + CH)
    alast = acum[r0 + CH - 1:r0 + CH, :]
    wdec = jnp.exp(alast - acum[rows, :])
    wd_e = _dot01r(wdec, e)
    eal_e = _dot01r(_row8(jnp.exp(alast)), e)[0:1, :]
    return rows, wd_e, eal_e


def _ssd_lmat(acum, act_t, c, h, causal):
    r0 = c * CH
    acol = acum[r0:r0 + CH, h:h + 1]
    arow = act_t[h:h + 1, r0:r0 + CH]
    return jnp.exp(jnp.where(causal, acol - arow, NEG))


def _make_ssd_fwd(seq, tb):
    ncb = tb // CH
    nb = seq // tb
    hg = SSD_H // SSD_G
    gw = SSD_W // SSD_G

    def body(ps_ref, ss_ref, cw_ref, cb_ref, prm_ref, nw_ref, ob_ref, st_ref, pre_ref, y_ref, hs_scr, halo_scr):
        @pl.when(pl.program_id(0) == 0)
        def _():
            hs_scr[...] = jnp.zeros_like(hs_scr)
            halo_scr[...] = jnp.zeros_like(halo_scr)

        causal, _, _ = _masks()
        (raw, _, pre, act, _, _, _, acum, e, _, xdt, ea_e, d_e) = _ssd_common(
            ps_ref, halo_scr[...], ss_ref[...], cw_ref[...], cb_ref[...], prm_ref[...], tb)
        halo_scr[...] = raw[tb - 8:tb, :]
        pre_ref[...] = pre
        act_t = acum.T
        nw = nw_ref[0:1, :]
        for c in range(ncb):
            rows, wd_e, eal_e = _ssd_chunk(act, acum, act_t, e, c)
            st_ref[c] = hs_scr[...]
            ys = []
            for g in range(SSD_G):
                gc_ = slice(g * gw, (g + 1) * gw)
                bg = act[rows, SSD_W + g * 128:SSD_W + (g + 1) * 128]
                cg = act[rows, SSD_W + 256 + g * 128:SSD_W + 256 + (g + 1) * 128]
                cbm = _dot_nt(cg, bg)
                hs = hs_scr[:, gc_]
                yin = _dot(cg, hs)
                yh = []
                for hh in range(hg):
                    h = g * hg + hh
                    lm = _ssd_lmat(acum, act_t, c, h, causal)
                    yh.append(_dot(cbm * lm, xdt[rows, h * SSD_P:(h + 1) * SSD_P]))
                ys.append(jnp.concatenate(yh, axis=1) + yin * ea_e[rows, gc_])
                hs_scr[:, gc_] = hs * eal_e[:, gc_] + _dot_tn(bg, xdt[rows, gc_] * wd_e[:, gc_])
            y = jnp.concatenate(ys, axis=1) + act[rows, 0:SSD_W] * d_e
            y_ref[rows, :] = y
            yz = y * _silu(ps_ref[rows, 1536:2560])
            outs = [_rms_fwd(yz[:, g * gw:(g + 1) * gw], nw[:, g * gw:(g + 1) * gw], gw)[2] for g in range(SSD_G)]
            ob_ref[rows, :] = jnp.concatenate(outs, axis=1).astype(ob_ref.dtype)

    def call(ps, ss, cw, cb, prm, nw):
        const = lambda i: (0, 0)
        return pl.pallas_call(
            body,
            grid=(nb,),
            in_specs=[
                pl.BlockSpec((tb, 2560), lambda i: (i, 0)),
                pl.BlockSpec((tb, 128), lambda i: (i, 0)),
                pl.BlockSpec((8, 1536), const),
                pl.BlockSpec((8, 1536), const),
                pl.BlockSpec((8, 128), const),
                pl.BlockSpec((8, SSD_W), const),
            ],
            out_specs=[
                pl.BlockSpec((tb, SSD_W), lambda i: (i, 0)),
                pl.BlockSpec((ncb, SSD_N, SSD_W), lambda i: (i, 0, 0)),
                pl.BlockSpec((tb, 1536), lambda i: (i, 0)),
                pl.BlockSpec((tb, SSD_W), lambda i: (i, 0)),
            ],
            out_shape=[
                jax.ShapeDtypeStruct((seq, SSD_W), BF16),
                jax.ShapeDtypeStruct((seq // CH, SSD_N, SSD_W), F32),
                jax.ShapeDtypeStruct((seq, 1536), F32),
                jax.ShapeDtypeStruct((seq, SSD_W), F32),
            ],
            scratch_shapes=[pltpu.VMEM((SSD_N, SSD_W), F32), pltpu.VMEM((8, 1536), F32)],
            compiler_params=pltpu.CompilerParams(dimension_semantics=("arbitrary",), vmem_limit_bytes=VMEM_LIMIT),
            name="ssd_fwd",
        )(ps, ss, cw, cb, prm, nw)

    return call


def _make_ssd_bwd(seq, tb):
    ncb = tb // CH
    nb = seq // tb
    hb = tb // 8
    hg = SSD_H // SSD_G
    gw = SSD_W // SSD_G

    def body(ps_ref, pre_ref, y_ref, ss_ref, cw_ref, cb_ref, prm_ref, nw_ref, st_ref, dob_ref,
             dps_ref, dss_ref, dcw_ref, dcb_ref, dprm_ref, dnw_ref, dhs_scr, nxt_scr):
        i = pl.program_id(0)

        @pl.when(i == 0)
        def _():
            dhs_scr[...] = jnp.zeros_like(dhs_scr)
            nxt_scr[...] = jnp.zeros_like(nxt_scr)
            dcw_ref[...] = jnp.zeros_like(dcw_ref)
            dcb_ref[...] = jnp.zeros_like(dcb_ref)
            dprm_ref[...] = jnp.zeros_like(dprm_ref)
            dnw_ref[...] = jnp.zeros_like(dnw_ref)

        causal, _, _ = _masks()
        cw = cw_ref[...]
        prm = prm_ref[...]
        (raw, _, pre, act, dt_in, dt, a, acum, e, dt_e, xdt, ea_e, d_e) = _ssd_common(
            ps_ref, None, ss_ref[...], cw, cb_ref[...], prm, tb, pre=pre_ref[...])
        act_t = acum.T
        nw = nw_ref[0:1, :]
        row_id = _iota2((CH, 1), 0)

        dx_l, db_l, dc_l, dz_l, dacum_l, ddt_l, da_in_l = ([None] * ncb for _ in range(7))
        upper_tri = (_iota2((CH, CH), 1) >= _iota2((CH, CH), 0)).astype(F32)
        below = jnp.bitwise_and(_iota2((CH, gw), 1), CH - 1) < _iota2((CH, gw), 0)
        dnw_acc = jnp.zeros((1, SSD_W), F32)
        dd_acc = jnp.zeros((1, SSD_W), F32)

        for c in reversed(range(ncb)):
            rows, wd_e, eal_e = _ssd_chunk(act, acum, act_t, e, c)
            xc = act[rows, 0:SSD_W]
            z = ps_ref[rows, 1536:2560]
            dob = dob_ref[rows, :]
            sz = _silu(z)
            dy_g, dz_g, zacc_g, dxdt_g, dal_g, db_g, dc_g, da_in_g = [], [], [], [], [], [], [], []
            for g in range(SSD_G):
                gc_ = slice(g * gw, (g + 1) * gw)
                bg = act[rows, SSD_W + g * 128:SSD_W + (g + 1) * 128]
                cg = act[rows, SSD_W + 256 + g * 128:SSD_W + 256 + (g + 1) * 128]
                cbm = _dot_nt(cg, bg)
                hs = st_ref[c, :, gc_]
                yin = _dot(cg, hs)
                lmats = [_ssd_lmat(acum, act_t, c, g * hg + hh, causal) for hh in range(hg)]
                ea_g = ea_e[rows, gc_]
                y = y_ref[rows, gc_]
                yz = y * sz[:, gc_]
                on, r, _ = _rms_fwd(yz, nw[:, gc_], gw)
                dyz, dnw_rows = _rms_bwd(dob[:, gc_], on, r, nw[:, gc_], gw)
                dnw_acc = dnw_acc + _put_cols(jnp.sum(dnw_rows, axis=0, keepdims=True), g, gw)
                dy = dyz * sz[:, gc_]
                dz_g.append(dyz * y * _dsilu(z[:, gc_]))
                dd_acc = dd_acc + _put_cols(jnp.sum(dy * xc[:, gc_], axis=0, keepdims=True), g, gw)
                dhs_n = dhs_scr[:, gc_]
                dyin = dy * ea_g
                dcg = _dot_nt(dyin, hs)
                xw = xdt[rows, gc_] * wd_e[:, gc_]
                dbg = _dot_nt(xw, dhs_n)
                dxw = _dot(bg, dhs_n)
                dhs_scr[:, gc_] = dhs_n * eal_e[:, gc_] + _dot_tn(cg, dyin)
                dal_g.append(jnp.sum(hs * dhs_n, axis=0, keepdims=True) * eal_e[:, gc_]
                             + jnp.sum(dxw * xw, axis=0, keepdims=True))
                dxi, ms, dcbm = [], [], jnp.zeros((CH, CH), F32)
                for hh in range(hg):
                    h = g * hg + hh
                    hc = slice(hh * SSD_P, (hh + 1) * SSD_P)
                    dyh = dy[:, hc]
                    lm = cbm * lmats[hh]
                    dxi.append(_dot_tn(lm, dyh))
                    dlm = _dot_nt(dyh, xdt[rows, h * SSD_P:(h + 1) * SSD_P])
                    ms.append(dlm * lm)
                    dcbm = dcbm + dlm * lmats[hh]
                dx_intra = jnp.concatenate(dxi, axis=1)
                ncat = _dot(upper_tri, jnp.concatenate(ms, axis=1))
                da_in_g.append(jnp.where(below, ncat, 0.0))
                zacc_g.append(dy * yin * ea_g - dxw * xw)
                dxdt_g.append(dx_intra + dxw * wd_e[:, gc_])
                dy_g.append(dy)
                db_g.append(dbg + _dot_tn(dcbm, cg))
                dc_g.append(dcg + _dot(dcbm, bg))
            dy = jnp.concatenate(dy_g, axis=1)
            dxdt = jnp.concatenate(dxdt_g, axis=1)
            dx_l[c] = dxdt * dt_e[rows, :] + dy * d_e
            db_l[c] = jnp.concatenate(db_g, axis=1)
            dc_l[c] = jnp.concatenate(dc_g, axis=1)
            dz_l[c] = jnp.concatenate(dz_g, axis=1)
            ddt_l[c] = _reduce_heads(dxdt * xc, e)
            dalast = _reduce_heads(_row8(jnp.concatenate(dal_g, axis=1)), e)[0:1, :]
            dacum_l[c] = _reduce_heads(jnp.concatenate(zacc_g, axis=1), e) + jnp.where(row_id == CH - 1, dalast, 0.0)
            da_in_l[c] = _reduce_heads(jnp.concatenate(da_in_g, axis=1), e)

        dacum_all = jnp.concatenate(dacum_l, axis=0)
        da = _dot01l(_chunk_tri(tb, upper=True), dacum_all) + jnp.concatenate(da_in_l, axis=0)
        neg_ea = -jnp.exp(prm[0:1, :])
        ddt = jnp.concatenate(ddt_l, axis=0) + da * neg_ea
        ddt_in = ddt * _sigmoid(dt_in)
        dss_ref[...] = ddt_in.astype(dss_ref.dtype)
        sub8 = _iota2((8, 128), 0)
        dalog = jnp.sum(da * a, axis=0, keepdims=True)
        ddtb = jnp.sum(ddt_in, axis=0, keepdims=True)
        dd = _reduce_heads(_row8(dd_acc), e)[0:1, :]
        dprm_ref[...] += (jnp.where(sub8 == 0, dalog, 0.0) + jnp.where(sub8 == 1, ddtb, 0.0)
                          + jnp.where(sub8 == 2, dd, 0.0))
        dnw_ref[...] += jnp.where(_iota2((8, SSD_W), 0) == 0, dnw_acc, 0.0)

        dact = jnp.concatenate([jnp.concatenate(dx_l, axis=0), jnp.concatenate(db_l, axis=0),
                                jnp.concatenate(dc_l, axis=0)], axis=1)
        dpre = dact * _dsilu(pre)
        back = _conv_back(dpre, nxt_scr[...], tb)
        nxt_scr[...] = dpre[0:8, :]
        draw = back[0] * cw[3:4, :] + back[1] * cw[2:3, :] + back[2] * cw[1:2, :] + back[3] * cw[0:1, :]
        dps_ref[:, 0:1536] = draw.astype(dps_ref.dtype)
        dps_ref[:, 1536:2560] = jnp.concatenate(dz_l, axis=0).astype(dps_ref.dtype)
        sub_c = _iota2((8, 1536), 0)
        dcw_new = jnp.zeros((8, 1536), F32)
        for s_ in range(CONV_W):
            dcw_new = dcw_new + jnp.where(sub_c == 3 - s_, jnp.sum(back[s_] * raw, axis=0, keepdims=True), 0.0)
        dcw_ref[...] += dcw_new
        dcb_ref[...] += jnp.where(sub_c == 0, jnp.sum(dpre, axis=0, keepdims=True), 0.0)

    def call(ps, pre, y, ss, cw, cb, prm, nw, st, dob, comm=None, comm_args=()):
        rev = lambda i: (nb - 1 - i, 0)
        const = lambda i: (0, 0)
        cx = _exchange_specs(comm)
        return pl.pallas_call(
            _with_exchange(body, comm, 10, 6, nb),
            grid=(nb,),
            in_specs=[
                pl.BlockSpec((tb, 2560), rev),
                pl.BlockSpec((tb, 1536), rev),
                pl.BlockSpec((tb, SSD_W), rev),
                pl.BlockSpec((tb, 128), rev),
                pl.BlockSpec((8, 1536), const),
                pl.BlockSpec((8, 1536), const),
                pl.BlockSpec((8, 128), const),
                pl.BlockSpec((8, SSD_W), const),
                pl.BlockSpec((ncb, SSD_N, SSD_W), lambda i: (nb - 1 - i, 0, 0)),
                pl.BlockSpec((tb, SSD_W), rev),
            ] + cx["specs"],
            out_specs=[
                pl.BlockSpec((tb, 2560), rev),
                pl.BlockSpec((tb, 128), rev),
                pl.BlockSpec((8, 1536), const),
                pl.BlockSpec((8, 1536), const),
                pl.BlockSpec((8, 128), const),
                pl.BlockSpec((8, SSD_W), const),
            ] + cx["specs"],
            out_shape=[
                jax.ShapeDtypeStruct((seq, 2560), BF16),
                jax.ShapeDtypeStruct((seq, 128), BF16),
                jax.ShapeDtypeStruct((8, 1536), F32),
                jax.ShapeDtypeStruct((8, 1536), F32),
                jax.ShapeDtypeStruct((8, 128), F32),
                jax.ShapeDtypeStruct((8, SSD_W), F32),
            ] + cx["out_shape"],
            scratch_shapes=[pltpu.VMEM((SSD_N, SSD_W), F32), pltpu.VMEM((8, 1536), F32)] + cx["scratch"],
            compiler_params=pltpu.CompilerParams(dimension_semantics=("arbitrary",), vmem_limit_bytes=VMEM_LIMIT,
                                                 has_side_effects=comm is not None),
            name="ssd_bwd" + cx["tag"],
        )(ps, pre, y, ss, cw, cb, prm, nw, st, dob, *comm_args)

    return call


def _ret_consts(h):
    lg = math.log(1.0 - 2.0 ** (-5.0 - h))
    r = _iota2((CH, CH), 0)
    c = _iota2((CH, CH), 1)
    rel = (r - c).astype(F32)
    dmat = jnp.where(r >= c, jnp.exp(jnp.maximum(rel, 0.0) * lg), 0.0)
    idx = _iota2((CH, 1), 0).astype(F32)
    qdec = jnp.exp((idx + 1.0) * lg)
    kdec = jnp.exp((CH - 1.0 - idx) * lg)
    cdec = math.exp(CH * lg)
    return dmat, qdec, kdec, cdec


def _ret_batch(pr_ref, cc_ref, ss_ref, ncb):
    pairs = [(c, h) for c in range(ncb) for h in range(RET_H)]

    def st(off):
        return jnp.stack([pr_ref[c * CH:(c + 1) * CH, off + h * 128:off + (h + 1) * 128] for c, h in pairs])

    cc = jnp.stack([cc_ref[c * CH:(c + 1) * CH, :] for c, _ in pairs])
    ss = jnp.stack([ss_ref[c * CH:(c + 1) * CH, :] for c, _ in pairs])
    consts = [_ret_consts(h) for h in range(RET_H)]
    dmat = jnp.stack([consts[h][0] for _, h in pairs])
    qdec = jnp.stack([consts[h][1] for _, h in pairs])
    kdec = jnp.stack([consts[h][2] for _, h in pairs])
    cdec = jnp.stack([jnp.full((1, 1), consts[h][3], F32) for h in range(RET_H)])
    q = _rot(st(0), cc, ss)
    k = _rot(st(512), cc, ss) * (RET_D ** -0.5)
    return dict(q=q, k=k, v=st(1024), z=st(1536), cc=cc, ss=ss, dmat=dmat, qdec=qdec, kdec=kdec, cdec=cdec,
                s=_bdot(q, k, _NT) * dmat)


def _rot(t, cc, ss):
    return t * cc + pltpu.roll(t, 64, axis=t.ndim - 1) * ss


def _rot_bwd(d, cc, ss):
    return d * cc + pltpu.roll(d * ss, 64, axis=d.ndim - 1)


def _make_ret_fwd(seq, tb):
    ncb = tb // CH
    nb = seq // tb

    def body(pr_ref, cc_ref, ss_ref, nw_ref, oc_ref, st_ref, r_scr):
        @pl.when(pl.program_id(0) == 0)
        def _():
            r_scr[...] = jnp.zeros_like(r_scr)

        d = _ret_batch(pr_ref, cc_ref, ss_ref, ncb)
        kd = d["k"] * d["kdec"]
        for c in range(ncb):
            bs = slice(c * RET_H, (c + 1) * RET_H)
            rs = r_scr[...]
            st_ref[c] = rs
            r_scr[...] = rs * d["cdec"] + _bdot(kd[bs], d["v"][bs], _TN)
        r_prev = st_ref[...].reshape(ncb * RET_H, 128, 128)
        o = _bdot(d["s"], d["v"], _NN) + _bdot(d["q"], r_prev, _NN) * d["qdec"]
        _, _, y = _rms_fwd(o, nw_ref[0:1, :], RET_D)
        out = y * _silu(d["z"])
        for c in range(ncb):
            for h in range(RET_H):
                oc_ref[c * CH:(c + 1) * CH, h * 128:(h + 1) * 128] = out[c * RET_H + h].astype(oc_ref.dtype)

    def call(pr, cc, ss, nw):
        return pl.pallas_call(
            body,
            grid=(nb,),
            in_specs=[
                pl.BlockSpec((tb, 2048), lambda i: (i, 0)),
                pl.BlockSpec((tb, 128), lambda i: (i, 0)),
                pl.BlockSpec((tb, 128), lambda i: (i, 0)),
                pl.BlockSpec((8, 128), lambda i: (0, 0)),
            ],
            out_specs=[
                pl.BlockSpec((tb, 512), lambda i: (i, 0)),
                pl.BlockSpec((ncb, RET_H, 128, 128), lambda i: (i, 0, 0, 0)),
            ],
            out_shape=[
                jax.ShapeDtypeStruct((seq, 512), BF16),
                jax.ShapeDtypeStruct((seq // CH, RET_H, 128, 128), F32),
            ],
            scratch_shapes=[pltpu.VMEM((RET_H, 128, 128), F32)],
            compiler_params=pltpu.CompilerParams(dimension_semantics=("arbitrary",), vmem_limit_bytes=VMEM_LIMIT),
            name="ret_fwd",
        )(pr, cc, ss, nw)

    return call


def _make_ret_bwd(seq, tb):
    ncb = tb // CH
    nb = seq // tb

    def body(pr_ref, cc_ref, ss_ref, nw_ref, st_ref, doc_ref, dpr_ref, dnw_ref, dr_scr):
        @pl.when(pl.program_id(0) == 0)
        def _():
            dr_scr[...] = jnp.zeros_like(dr_scr)
            dnw_ref[...] = jnp.zeros_like(dnw_ref)

        nw = nw_ref[0:1, :]
        scale = RET_D ** -0.5
        n = ncb * RET_H
        d = _ret_batch(pr_ref, cc_ref, ss_ref, ncb)
        q, k, v, z, s = d["q"], d["k"], d["v"], d["z"], d["s"]
        r_prev = st_ref[...].reshape(n, 128, 128)
        o = _bdot(s, v, _NN) + _bdot(q, r_prev, _NN) * d["qdec"]
        doc = jnp.stack([doc_ref[c * CH:(c + 1) * CH, h * 128:(h + 1) * 128]
                         for c in range(ncb) for h in range(RET_H)])
        on, r, y = _rms_fwd(o, nw, RET_D)
        dz = doc * y * _dsilu(z)
        do, dnw_rows = _rms_bwd(doc * _silu(z), on, r, nw, RET_D)
        dnw_acc = jnp.sum(jnp.sum(dnw_rows, axis=0), axis=0, keepdims=True)
        dqd = do * d["qdec"]
        qtd = _bdot(q, dqd, _TN)
        drn_l = [None] * ncb
        for c in reversed(range(ncb)):
            drn_l[c] = dr_scr[...]
            dr_scr[...] = qtd[c * RET_H:(c + 1) * RET_H] + d["cdec"] * drn_l[c]
        drn = jnp.concatenate(drn_l, axis=0)
        ds = _bdot(do, v, _NT) * d["dmat"]
        dq = _rot_bwd(_bdot(ds, k, _NN) + _bdot(dqd, r_prev, _NT), d["cc"], d["ss"])
        dk = _rot_bwd((_bdot(ds, q, _TN) + _bdot(v, drn, _NT) * d["kdec"]) * scale, d["cc"], d["ss"])
        dv = _bdot(s, do, _TN) + _bdot(k * d["kdec"], drn, _NN)
        for c in range(ncb):
            rows = slice(c * CH, (c + 1) * CH)
            for h in range(RET_H):
                b = c * RET_H + h
                for j, val in enumerate((dq, dk, dv, dz)):
                    dpr_ref[rows, j * 512 + h * 128:j * 512 + (h + 1) * 128] = val[b].astype(dpr_ref.dtype)
        dnw_ref[...] += jnp.where(_iota2((8, 128), 0) == 0, dnw_acc, 0.0)

    def call(pr, cc, ss, nw, st, doc):
        rev = lambda i: (nb - 1 - i, 0)
        return pl.pallas_call(
            body,
            grid=(nb,),
            in_specs=[
                pl.BlockSpec((tb, 2048), rev),
                pl.BlockSpec((tb, 128), rev),
                pl.BlockSpec((tb, 128), rev),
                pl.BlockSpec((8, 128), lambda i: (0, 0)),
                pl.BlockSpec((ncb, RET_H, 128, 128), lambda i: (nb - 1 - i, 0, 0, 0)),
                pl.BlockSpec((tb, 512), rev),
            ],
            out_specs=[
                pl.BlockSpec((tb, 2048), rev),
                pl.BlockSpec((8, 128), lambda i: (0, 0)),
            ],
            out_shape=[
                jax.ShapeDtypeStruct((seq, 2048), BF16),
                jax.ShapeDtypeStruct((8, 128), F32),
            ],
            scratch_shapes=[pltpu.VMEM((RET_H, 128, 128), F32)],
            compiler_params=pltpu.CompilerParams(dimension_semantics=("arbitrary",), vmem_limit_bytes=VMEM_LIMIT),
            name="ret_bwd",
        )(pr, cc, ss, nw, st, doc)

    return call


def _rope_tables(seq):
    half = RET_D // 2
    inv = ROPE_BASE ** (-jnp.arange(half, dtype=F32) / half)
    ang = jnp.arange(seq, dtype=jnp.int32).astype(F32)[:, None] * inv[None, :]
    cos, sin = jnp.cos(ang), jnp.sin(ang)
    return jnp.concatenate([cos, cos], axis=1), jnp.concatenate([-sin, sin], axis=1)


SEG_G, SEG_S, SEG_R, SEG_GS, SEG_SS = (0, 2048), (2048, 4608), (4608, 6656), (6656, 6784), (6784, 6912)
NP = 6912
SEGS = (SEG_G, SEG_S, SEG_R, SEG_GS, SEG_SS)


def _resident(shape):
    return pl.BlockSpec(shape, lambda i: (0,) * len(shape), pipeline_mode=pl.Buffered(1))


def _make_inproj(seq, tl):
    def body(x_ref, pn_ref, w_ref, pg_ref, ps_ref, pr_ref, gs_ref, ss_ref, ht_ref):
        x = x_ref[...]
        _, _, hn = _rms_fwd(x, pn_ref[0:1, :], D_MODEL)
        h = hn.astype(BF16)
        ht_ref[...] = hn.T.astype(BF16)
        for (a, b), o_ref in zip(SEGS, (pg_ref, ps_ref, pr_ref, gs_ref, ss_ref)):
            o_ref[...] = jnp.dot(h, w_ref[:, a:b], preferred_element_type=F32)

    def call(x, pn, w, comm=None, comm_args=()):
        row = lambda i: (i, 0)
        cx = _exchange_specs(comm)
        return pl.pallas_call(
            _with_exchange(body, comm, 3, 6, seq // tl),
            grid=(seq // tl,),
            in_specs=[pl.BlockSpec((tl, D_MODEL), row), _resident((8, D_MODEL)), _resident((D_MODEL, NP))]
            + cx["specs"],
            out_specs=[pl.BlockSpec((tl, b - a), row) for a, b in SEGS]
            + [pl.BlockSpec((D_MODEL, tl), lambda i: (0, i))] + cx["specs"],
            out_shape=[jax.ShapeDtypeStruct((seq, b - a), F32) for a, b in SEGS]
            + [jax.ShapeDtypeStruct((D_MODEL, seq), BF16)] + cx["out_shape"],
            scratch_shapes=cx["scratch"],
            compiler_params=pltpu.CompilerParams(dimension_semantics=("arbitrary",), vmem_limit_bytes=VMEM_LIMIT,
                                                 has_side_effects=comm is not None),
            name="inproj" + cx["tag"],
        )(x, pn, w, *comm_args)

    return call


def _make_outproj(seq, tl):
    def body(oa_ref, ob_ref, oc_ref, w_ref, x_ref, qn_ref, out_ref, xn_ref):
        out = (jnp.dot(oa_ref[...], w_ref[0:512, :], preferred_element_type=F32)
               + jnp.dot(ob_ref[...], w_ref[512:1536, :], preferred_element_type=F32)
               + jnp.dot(oc_ref[...], w_ref[1536:2048, :], preferred_element_type=F32))
        out_ref[...] = out
        _, _, y = _rms_fwd(out, qn_ref[0:1, :], D_MODEL)
        xn_ref[...] = x_ref[...] + y

    def call(oa, ob, oc, w, x, qn):
        row = lambda i: (i, 0)
        return pl.pallas_call(
            body,
            grid=(seq // tl,),
            in_specs=[pl.BlockSpec((tl, 512), row), pl.BlockSpec((tl, 1024), row), pl.BlockSpec((tl, 512), row),
                      _resident((2048, D_MODEL)), pl.BlockSpec((tl, D_MODEL), row), _resident((8, D_MODEL))],
            out_specs=[pl.BlockSpec((tl, D_MODEL), row), pl.BlockSpec((tl, D_MODEL), row)],
            out_shape=[jax.ShapeDtypeStruct((seq, D_MODEL), F32), jax.ShapeDtypeStruct((seq, D_MODEL), F32)],
            compiler_params=pltpu.CompilerParams(dimension_semantics=("arbitrary",), vmem_limit_bytes=VMEM_LIMIT),
            name="outproj",
        )(oa, ob, oc, w, x, qn)

    return call


def _make_outproj_loss(seq, tl):
    def body(oa_ref, ob_ref, oc_ref, w_ref, x_ref, qn_ref, t_ref, out_ref, dy_ref, loss_ref):
        @pl.when(pl.program_id(0) == 0)
        def _():
            loss_ref[...] = jnp.zeros_like(loss_ref)

        out = (jnp.dot(oa_ref[...], w_ref[0:512, :], preferred_element_type=F32)
               + jnp.dot(ob_ref[...], w_ref[512:1536, :], preferred_element_type=F32)
               + jnp.dot(oc_ref[...], w_ref[1536:2048, :], preferred_element_type=F32))
        out_ref[...] = out
        _, _, y = _rms_fwd(out, qn_ref[0:1, :], D_MODEL)
        err = (x_ref[...] + y) - t_ref[...]
        dy_ref[...] = err * (1.0 / D_MODEL)
        part = jnp.sum(jnp.sum(err * err, axis=1, keepdims=True), axis=0, keepdims=True) * (0.5 / D_MODEL)
        loss_ref[...] += jnp.where((_iota2((8, 128), 0) == 0) & (_iota2((8, 128), 1) == 0), part, 0.0)

    def call(oa, ob, oc, w, x, qn, t):
        row = lambda i: (i, 0)
        return pl.pallas_call(
            body,
            grid=(seq // tl,),
            in_specs=[pl.BlockSpec((tl, 512), row), pl.BlockSpec((tl, 1024), row), pl.BlockSpec((tl, 512), row),
                      _resident((2048, D_MODEL)), pl.BlockSpec((tl, D_MODEL), row), _resident((8, D_MODEL)),
                      pl.BlockSpec((tl, D_MODEL), row)],
            out_specs=[pl.BlockSpec((tl, D_MODEL), row), pl.BlockSpec((tl, D_MODEL), row),
                       pl.BlockSpec((8, 128), lambda i: (0, 0))],
            out_shape=[jax.ShapeDtypeStruct((seq, D_MODEL), F32), jax.ShapeDtypeStruct((seq, D_MODEL), F32),
                       jax.ShapeDtypeStruct((8, 128), F32)],
            compiler_params=pltpu.CompilerParams(dimension_semantics=("arbitrary",), vmem_limit_bytes=VMEM_LIMIT),
            name="outproj_loss",
        )(oa, ob, oc, w, x, qn, t)

    return call


def _make_outproj_bwd(seq, tl):
    def body(dxn_ref, out_ref, oa_ref, ob_ref, oc_ref, w_ref, qn_ref, doa_ref, dob_ref, doc_ref, dqn_ref, dw_ref):
        @pl.when(pl.program_id(0) == 0)
        def _():
            dqn_ref[...] = jnp.zeros_like(dqn_ref)
            dw_ref[...] = jnp.zeros_like(dw_ref)

        qn = qn_ref[0:1, :]
        on, r, _ = _rms_fwd(out_ref[...], qn, D_MODEL)
        dout, dqn_rows = _rms_bwd(dxn_ref[...], on, r, qn, D_MODEL)
        dqn_ref[...] += jnp.where(_iota2((8, D_MODEL), 0) == 0, jnp.sum(dqn_rows, axis=0, keepdims=True), 0.0)
        db = dout.astype(BF16)
        nt = (((1,), (1,)), ((), ()))
        tn = (((0,), (0,)), ((), ()))
        doa_ref[...] = lax.dot_general(db, w_ref[0:512, :], nt, preferred_element_type=F32).astype(BF16)
        dob_ref[...] = lax.dot_general(db, w_ref[512:1536, :], nt, preferred_element_type=F32).astype(BF16)
        doc_ref[...] = lax.dot_general(db, w_ref[1536:2048, :], nt, preferred_element_type=F32).astype(BF16)
        dw_ref[0:512, :] += lax.dot_general(oa_ref[...], db, tn, preferred_element_type=F32)
        dw_ref[512:1536, :] += lax.dot_general(ob_ref[...], db, tn, preferred_element_type=F32)
        dw_ref[1536:2048, :] += lax.dot_general(oc_ref[...], db, tn, preferred_element_type=F32)

    def call(dxn, out, oa, ob, oc, w, qn):
        row = lambda i: (i, 0)
        const = lambda i: (0, 0)
        return pl.pallas_call(
            body,
            grid=(seq // tl,),
            in_specs=[pl.BlockSpec((tl, D_MODEL), row), pl.BlockSpec((tl, D_MODEL), row),
                      pl.BlockSpec((tl, 512), row), pl.BlockSpec((tl, 1024), row), pl.BlockSpec((tl, 512), row),
                      _resident((2048, D_MODEL)), _resident((8, D_MODEL))],
            out_specs=[pl.BlockSpec((tl, 512), row), pl.BlockSpec((tl, 1024), row), pl.BlockSpec((tl, 512), row),
                       pl.BlockSpec((8, D_MODEL), const), pl.BlockSpec((2048, D_MODEL), const)],
            out_shape=[jax.ShapeDtypeStruct((seq, 512), BF16), jax.ShapeDtypeStruct((seq, 1024), BF16),
                       jax.ShapeDtypeStruct((seq, 512), BF16), jax.ShapeDtypeStruct((8, D_MODEL), F32),
                       jax.ShapeDtypeStruct((2048, D_MODEL), F32)],
            compiler_params=pltpu.CompilerParams(dimension_semantics=("arbitrary",), vmem_limit_bytes=VMEM_LIMIT),
            name="outproj_bwd",
        )(dxn, out, oa, ob, oc, w, qn)

    return call


def _make_inproj_bwd_dx(seq, tl):
    def body(dg_ref, ds_ref, dr_ref, dgs_ref, dss_ref, w_ref, x_ref, pn_ref, dxn_ref, dx_ref, dpn_ref):
        @pl.when(pl.program_id(0) == 0)
        def _():
            dpn_ref[...] = jnp.zeros_like(dpn_ref)

        nt = (((1,), (1,)), ((), ()))
        dh = jnp.zeros((tl, D_MODEL), F32)
        for (a, b), d_ref in zip(SEGS, (dg_ref, ds_ref, dr_ref, dgs_ref, dss_ref)):
            dh = dh + lax.dot_general(d_ref[...], w_ref[:, a:b], nt, preferred_element_type=F32)
        pn = pn_ref[0:1, :]
        on, r, _ = _rms_fwd(x_ref[...], pn, D_MODEL)
        dx, dpn_rows = _rms_bwd(dh, on, r, pn, D_MODEL)
        dx_ref[...] = dx + dxn_ref[...]
        dpn_ref[...] += jnp.where(_iota2((8, D_MODEL), 0) == 0, jnp.sum(dpn_rows, axis=0, keepdims=True), 0.0)

    def call(dg, ds, dr, dgs, dss, w, x, pn, dxn, comm=None, comm_args=()):
        row = lambda i: (i, 0)
        cx = _exchange_specs(comm)
        return pl.pallas_call(
            _with_exchange(body, comm, 9, 2, seq // tl),
            grid=(seq // tl,),
            in_specs=[pl.BlockSpec((tl, b - a), row) for a, b in SEGS]
            + [_resident((D_MODEL, NP)), pl.BlockSpec((tl, D_MODEL), row), _resident((8, D_MODEL)),
               pl.BlockSpec((tl, D_MODEL), row)] + cx["specs"],
            out_specs=[pl.BlockSpec((tl, D_MODEL), row), pl.BlockSpec((8, D_MODEL), lambda i: (0, 0))] + cx["specs"],
            out_shape=[jax.ShapeDtypeStruct((seq, D_MODEL), F32), jax.ShapeDtypeStruct((8, D_MODEL), F32)]
            + cx["out_shape"],
            scratch_shapes=cx["scratch"],
            compiler_params=pltpu.CompilerParams(dimension_semantics=("arbitrary",), vmem_limit_bytes=VMEM_LIMIT,
                                                 has_side_effects=comm is not None),
            name="inproj_bwd_dx" + cx["tag"],
        )(dg, ds, dr, dgs, dss, w, x, pn, dxn, *comm_args)

    return call


def _make_inproj_bwd_dw(seq, tl, width, tn, name):
    def body(ht_ref, d_ref, dw_ref):
        @pl.when(pl.program_id(1) == 0)
        def _():
            dw_ref[...] = jnp.zeros_like(dw_ref)

        dw_ref[...] += jnp.dot(ht_ref[...], d_ref[...], preferred_element_type=F32)

    def call(ht, d):
        return pl.pallas_call(
            body,
            grid=(width // tn, seq // tl),
            in_specs=[pl.BlockSpec((D_MODEL, tl), lambda j, i: (0, i)), pl.BlockSpec((tl, tn), lambda j, i: (i, j))],
            out_specs=pl.BlockSpec((D_MODEL, tn), lambda j, i: (0, j)),
            out_shape=jax.ShapeDtypeStruct((D_MODEL, width), F32),
            compiler_params=pltpu.CompilerParams(dimension_semantics=("arbitrary", "arbitrary"),
                                                 vmem_limit_bytes=VMEM_LIMIT),
            name=name,
        )(ht, d)

    return call


ADAM_LR, ADAM_B1, ADAM_B2, ADAM_EPS, ADAM_WD, ADAM_STEP = 0.001, 0.9, 0.999, 1e-08, 0.01, 10


def _adam_math(w, g, m, v):
    m = ADAM_B1 * m + (1.0 - ADAM_B1) * g
    v = ADAM_B2 * v + (1.0 - ADAM_B2) * (g * g)
    m_hat = m / (1.0 - ADAM_B1 ** ADAM_STEP)
    v_hat = v / (1.0 - ADAM_B2 ** ADAM_STEP)
    delta = -ADAM_LR * (m_hat / (jnp.sqrt(v_hat) + ADAM_EPS) + ADAM_WD * w)
    return delta, m, v


def _adamw(w, g, m, v, name):
    shape = w.shape
    cols = shape[-1]
    rows = w.size // cols
    tr = rows if rows <= 512 else 256
    assert rows % tr == 0

    def body(w_ref, g_ref, m_ref, v_ref, d_ref, mo_ref, vo_ref):
        d_ref[...], mo_ref[...], vo_ref[...] = _adam_math(w_ref[...], g_ref[...], m_ref[...], v_ref[...])

    spec = pl.BlockSpec((tr, cols), lambda i: (i, 0))
    outs = pl.pallas_call(
        body,
        grid=(rows // tr,),
        in_specs=[spec] * 4,
        out_specs=[spec] * 3,
        out_shape=[jax.ShapeDtypeStruct((rows, cols), F32)] * 3,
        compiler_params=pltpu.CompilerParams(dimension_semantics=("arbitrary",), vmem_limit_bytes=VMEM_LIMIT),
        name=name,
    )(*[a.reshape(rows, cols) for a in (w, g, m, v)])
    return (g,) + tuple(o.reshape(shape) for o in outs)


def _adamw_pairs(w, mine, theirs, m, v, name):
    na, r, cols = w.shape
    assert na == 2
    tr = 256
    assert r % tr == 0

    def body(w_ref, a0_ref, b0_ref, a1_ref, b1_ref, m_ref, v_ref, g_ref, d_ref, mo_ref, vo_ref):
        g = jnp.where(pl.program_id(0) == 0, a0_ref[...] + b0_ref[...], a1_ref[...] + b1_ref[...])
        g_ref[...] = g
        d_ref[...], mo_ref[...], vo_ref[...] = _adam_math(w_ref[...], g, m_ref[...], v_ref[...])

    full = pl.BlockSpec((None, tr, cols), lambda a, i: (a, i, 0))
    one = pl.BlockSpec((None, tr, cols), lambda a, i: (0, i, 0))
    return pl.pallas_call(
        body,
        grid=(na, r // tr),
        in_specs=[full, one, one, one, one, full, full],
        out_specs=[full] * 4,
        out_shape=[jax.ShapeDtypeStruct(w.shape, F32)] * 4,
        compiler_params=pltpu.CompilerParams(dimension_semantics=("arbitrary",) * 2, vmem_limit_bytes=VMEM_LIMIT),
        name=name,
    )(w, mine[0], theirs[0], mine[1], theirs[1], m, v)


MESH = pl.DeviceIdType.MESH
ANY = pl.BlockSpec(memory_space=pl.ANY)
CHIP_REL = ((1, 0), (0, 1), (1, 1))


def _flip(v, d):
    return 1 - v if d else v


def _ag_chips(arrs, name):
    n = len(arrs)

    def body(*refs):
        ins, outs = refs[:n], refs[n:2 * n]
        send_sems, recv_sems, loc_sems = refs[2 * n:]
        x, y, c = lax.axis_index("x"), lax.axis_index("y"), lax.axis_index("c")
        me = 2 * x + y

        def remote(a, k, slot):
            dx, dy = CHIP_REL[k]
            return pltpu.make_async_remote_copy(
                src_ref=ins[a], dst_ref=outs[a].at[slot], send_sem=send_sems.at[a * 3 + k],
                recv_sem=recv_sems.at[a * 3 + k], device_id=(_flip(x, dx), _flip(y, dy), c), device_id_type=MESH)

        local = [pltpu.make_async_copy(ins[a], outs[a].at[me], loc_sems.at[a]) for a in range(n)]
        for cp in local:
            cp.start()
        for a in range(n):
            for k in range(3):
                remote(a, k, me).start()
        for a in range(n):
            for k, (dx, dy) in enumerate(CHIP_REL):
                remote(a, k, 2 * _flip(x, dx) + _flip(y, dy)).wait_recv()
        for a in range(n):
            for k in range(3):
                remote(a, k, me).wait_send()
        for cp in local:
            cp.wait()

    return pl.pallas_call(
        body,
        in_specs=[ANY] * n,
        out_specs=[ANY] * n,
        out_shape=[jax.ShapeDtypeStruct((4,) + a.shape, a.dtype) for a in arrs],
        scratch_shapes=[pltpu.SemaphoreType.DMA((3 * n,)), pltpu.SemaphoreType.DMA((3 * n,)),
                        pltpu.SemaphoreType.DMA((n,))],
        compiler_params=pltpu.CompilerParams(has_side_effects=True),
        name=name,
    )(*arrs)


class _ChipExchange:
    def __init__(self, kind, arrs):
        self.kind, self.n = kind, len(arrs)
        if kind == "gather":
            self.out_shape = [jax.ShapeDtypeStruct((4,) + a.shape, a.dtype) for a in arrs]
        else:
            self.out_shape = [jax.ShapeDtypeStruct((3,) + a.shape[1:], a.dtype) for a in arrs]
        self.scratch = [pltpu.SemaphoreType.DMA((4 * self.n,)), pltpu.SemaphoreType.DMA((4 * self.n,))]

    def _copies(self, ins, outs, sems):
        send_sems, recv_sems = sems
        x, y, c = lax.axis_index("x"), lax.axis_index("y"), lax.axis_index("c")
        me = 2 * x + y
        pairs = []
        for a in range(self.n):
            for k, (dx, dy) in enumerate(CHIP_REL):
                px, py = _flip(x, dx), _flip(y, dy)
                sem = dict(send_sem=send_sems.at[4 * a + k], recv_sem=recv_sems.at[4 * a + k],
                           device_id=(px, py, c), device_id_type=MESH)
                if self.kind == "gather":
                    out = pltpu.make_async_remote_copy(src_ref=ins[a], dst_ref=outs[a].at[me], **sem)
                    inc = pltpu.make_async_remote_copy(src_ref=ins[a], dst_ref=outs[a].at[2 * px + py], **sem)
                else:
                    out = pltpu.make_async_remote_copy(src_ref=ins[a].at[2 * px + py], dst_ref=outs[a].at[k], **sem)
                    inc = out
                pairs.append((out, inc))
            if self.kind == "gather":
                own = pltpu.make_async_remote_copy(
                    src_ref=ins[a], dst_ref=outs[a].at[me], send_sem=send_sems.at[4 * a + 3],
                    recv_sem=recv_sems.at[4 * a + 3], device_id=(x, y, 1 - c), device_id_type=MESH)
                pairs.append((own, own))
        return pairs

    def start(self, ins, outs, sems):
        for out, _ in self._copies(ins, outs, sems):
            out.start()

    def finish(self, ins, outs, sems):
        pairs = self._copies(ins, outs, sems)
        for _, inc in pairs:
            inc.wait_recv()
        for out, _ in pairs:
            out.wait_send()


def _with_exchange(body, comm, n_in, n_out, nb):
    if comm is None:
        return body

    def wrapped(*refs):
        ins = refs[:n_in]
        c_in = refs[n_in:n_in + comm.n]
        outs = refs[n_in + comm.n:n_in + comm.n + n_out]
        c_out = refs[n_in + comm.n + n_out:n_in + 2 * comm.n + n_out]
        rest = refs[n_in + 2 * comm.n + n_out:]
        scratch, sems = rest[:len(rest) - 2], rest[len(rest) - 2:]

        @pl.when(pl.program_id(0) == 0)
        def _():
            comm.start(c_in, c_out, sems)

        body(*ins, *outs, *scratch)

        @pl.when(pl.program_id(0) == nb - 1)
        def _():
            comm.finish(c_in, c_out, sems)

    return wrapped


def _exchange_specs(comm):
    if comm is None:
        return dict(specs=[], out_shape=[], scratch=[], tag="")
    return dict(specs=[pl.BlockSpec(memory_space=pl.ANY)] * comm.n, out_shape=list(comm.out_shape),
                scratch=list(comm.scratch), tag="_" + comm.kind)


def _half(ref_or_shape, half):
    r = ref_or_shape[-2] // 2
    return pl.ds(half * r, r)


def _ag_rows(arrs, name):
    n = len(arrs)

    def body(*refs):
        ins, outs = refs[:n], refs[n:2 * n]
        send_sems, recv_sems, fsend_sems, frecv_sems, loc_sems = refs[2 * n:]
        x, y, c = lax.axis_index("x"), lax.axis_index("y"), lax.axis_index("c")
        me = 2 * x + y
        sib = (x, y, 1 - c)

        def chip_of(k):
            dx, dy = CHIP_REL[k]
            return _flip(x, dx), _flip(y, dy)

        def ici(a, k, slot):
            px, py = chip_of(k)
            rows = _half(arrs[a].shape, c)
            return pltpu.make_async_remote_copy(
                src_ref=ins[a].at[:, rows, :], dst_ref=outs[a].at[slot, :, rows, :], send_sem=send_sems.at[a * 3 + k],
                recv_sem=recv_sems.at[a * 3 + k], device_id=(px, py, c), device_id_type=MESH)

        def fwd(a, k, half):
            px, py = chip_of(k)
            blk = outs[a].at[2 * px + py, :, _half(arrs[a].shape, half), :]
            return pltpu.make_async_remote_copy(
                src_ref=blk, dst_ref=blk, send_sem=fsend_sems.at[a * 3 + k], recv_sem=frecv_sems.at[a * 3 + k],
                device_id=sib, device_id_type=MESH)

        own = [pltpu.make_async_remote_copy(src_ref=ins[a], dst_ref=outs[a].at[me], send_sem=loc_sems.at[a],
                                            recv_sem=loc_sems.at[n + a], device_id=sib, device_id_type=MESH)
               for a in range(n)]
        for cp in own:
            cp.start()
        for a in range(n):
            for k in range(3):
                ici(a, k, me).start()
        for a in range(n):
            for k in range(3):
                px, py = chip_of(k)
                ici(a, k, 2 * px + py).wait_recv()
                fwd(a, k, c).start()
        for a in range(n):
            for k in range(3):
                fwd(a, k, 1 - c).wait_recv()
        for a in range(n):
            for k in range(3):
                ici(a, k, me).wait_send()
                fwd(a, k, c).wait_send()
        for cp in own:
            cp.wait()

    return pl.pallas_call(
        body,
        in_specs=[ANY] * n,
        out_specs=[ANY] * n,
        out_shape=[jax.ShapeDtypeStruct((4,) + a.shape, a.dtype) for a in arrs],
        scratch_shapes=[pltpu.SemaphoreType.DMA((3 * n,)) for _ in range(4)] + [pltpu.SemaphoreType.DMA((2 * n,))],
        compiler_params=pltpu.CompilerParams(has_side_effects=True),
        name=name,
    )(*arrs)


def _sum_chips(own, recv, chip, name):
    _, na, r, cols = own.shape
    tr = 256
    assert r % tr == 0

    def body(chip_ref, o_ref, r_ref, s_ref):
        s_ref[...] = ((o_ref[...] + r_ref[0].astype(F32)) + r_ref[1].astype(F32)) + r_ref[2].astype(F32)

    return pl.pallas_call(
        body,
        grid_spec=pltpu.PrefetchScalarGridSpec(
            num_scalar_prefetch=1,
            grid=(na, r // tr),
            in_specs=[pl.BlockSpec((None, None, tr, cols), lambda a, i, ch: (ch[0], a, i, 0)),
                      pl.BlockSpec((3, None, tr, cols), lambda a, i, ch: (0, a, i, 0))],
            out_specs=pl.BlockSpec((None, tr, cols), lambda a, i, ch: (a, i, 0))),
        out_shape=jax.ShapeDtypeStruct((na, r, cols), F32),
        compiler_params=pltpu.CompilerParams(dimension_semantics=("arbitrary",) * 2, vmem_limit_bytes=VMEM_LIMIT),
        name=name,
    )(chip, own, recv)


def _swap_sibling(arrs, name):
    n = len(arrs)

    def body(*refs):
        ins, outs = refs[:n], refs[n:2 * n]
        send_sems, recv_sems = refs[2 * n:]
        x, y, c = lax.axis_index("x"), lax.axis_index("y"), lax.axis_index("c")
        cps = [pltpu.make_async_remote_copy(src_ref=ins[a], dst_ref=outs[a], send_sem=send_sems.at[a],
                                            recv_sem=recv_sems.at[a], device_id=(x, y, 1 - c), device_id_type=MESH)
               for a in range(n)]
        for cp in cps:
            cp.start()
        for cp in cps:
            cp.wait_recv()
        for cp in cps:
            cp.wait_send()

    return pl.pallas_call(
        body,
        in_specs=[ANY] * n,
        out_specs=[ANY] * n,
        out_shape=[jax.ShapeDtypeStruct(a.shape, a.dtype) for a in arrs],
        scratch_shapes=[pltpu.SemaphoreType.DMA((n,)), pltpu.SemaphoreType.DMA((n,))],
        compiler_params=pltpu.CompilerParams(has_side_effects=True),
        name=name,
    )(*arrs)


def _allreduce_small(vec, name):
    rows = vec.shape[0]

    def body(v_ref, out_ref, gat_ref, send_sems, recv_sems):
        x, y, c = lax.axis_index("x"), lax.axis_index("y"), lax.axis_index("c")
        me = 4 * x + 2 * y + c

        def remote(k, slot):
            dx, dy, dc = (k >> 2) & 1, (k >> 1) & 1, k & 1
            return pltpu.make_async_remote_copy(
                src_ref=v_ref, dst_ref=gat_ref.at[slot], send_sem=send_sems.at[k - 1], recv_sem=recv_sems.at[k - 1],
                device_id=(_flip(x, dx), _flip(y, dy), _flip(c, dc)), device_id_type=MESH)

        gat_ref[me] = v_ref[...]
        for k in range(1, 8):
            remote(k, me).start()
        for k in range(1, 8):
            dx, dy, dc = (k >> 2) & 1, (k >> 1) & 1, k & 1
            remote(k, 4 * _flip(x, dx) + 2 * _flip(y, dy) + _flip(c, dc)).wait_recv()
        for k in range(1, 8):
            remote(k, me).wait_send()
        acc = gat_ref[0]
        for j in range(1, 8):
            acc = acc + gat_ref[j]
        out_ref[...] = acc

    vm = pl.BlockSpec(memory_space=pltpu.VMEM)
    return pl.pallas_call(
        body,
        in_specs=[vm],
        out_specs=vm,
        out_shape=jax.ShapeDtypeStruct(vec.shape, F32),
        scratch_shapes=[pltpu.VMEM((8, rows, 128), F32), pltpu.SemaphoreType.DMA((7,)), pltpu.SemaphoreType.DMA((7,))],
        compiler_params=pltpu.CompilerParams(has_side_effects=True),
        name=name,
    )(vec)


def _pad8(v, width, lane0=0):
    v = v.reshape(1, -1) if v.ndim == 1 else v
    return jnp.zeros((8, width), F32).at[:v.shape[0], lane0:lane0 + v.shape[1]].set(v.astype(F32))


def _relayout_w_in(g):
    tr = 128
    q = N_IN // 4

    def body(g_ref, o_ref):
        w = jnp.concatenate([g_ref[j] for j in range(4)], axis=1)
        z = lambda n: jnp.zeros((tr, n), w.dtype)
        o_ref[...] = jnp.concatenate([w[:, 0:2048], w[:, 2056:4616], w[:, 4632:6680],
                                      w[:, 2048:2056], z(120), w[:, 4616:4632], z(112)], axis=1)

    return pl.pallas_call(
        body,
        grid=(D_MODEL // tr,),
        in_specs=[pl.BlockSpec((4, tr, q), lambda i: (0, i, 0))],
        out_specs=pl.BlockSpec((tr, NP), lambda i: (i, 0)),
        out_shape=jax.ShapeDtypeStruct((D_MODEL, NP), g.dtype),
        compiler_params=pltpu.CompilerParams(dimension_semantics=("arbitrary",), vmem_limit_bytes=VMEM_LIMIT),
        name="relayout_w_in",
    )(g)


def _unlayout_dw_in(dg, ds, dr, dsm):
    tr = 128
    q = N_IN // 4

    def body(g_ref, s_ref, r_ref, sm_ref, o_ref, ob_ref):
        w = jnp.concatenate([g_ref[...], sm_ref[:, 0:8], s_ref[...], sm_ref[:, 128:144], r_ref[...]], axis=1)
        for j in range(4):
            blk = w[:, q * j:q * (j + 1)]
            o_ref[j] = blk
            ob_ref[j] = blk.astype(BF16)

    row = lambda i: (i, 0)
    return pl.pallas_call(
        body,
        grid=(D_MODEL // tr,),
        in_specs=[pl.BlockSpec((tr, d.shape[1]), row) for d in (dg, ds, dr, dsm)],
        out_specs=[pl.BlockSpec((4, tr, q), lambda i: (0, i, 0))] * 2,
        out_shape=[jax.ShapeDtypeStruct((4, D_MODEL, q), F32), jax.ShapeDtypeStruct((4, D_MODEL, q), BF16)],
        compiler_params=pltpu.CompilerParams(dimension_semantics=("arbitrary",), vmem_limit_bytes=VMEM_LIMIT),
        name="unlayout_dw_in",
    )(dg, ds, dr, dsm)


TB = 256
TL = 256
TK = 1024


def kernel(x, pre_norm, post_norm, w_in, gdn_conv, gdn_A_log, gdn_dt_bias, gdn_norm, ssd_conv, ssd_conv_b, ssd_A_log, ssd_dt_bias, ssd_D, ssd_norm, ret_norm, w_out, loss_target, m_pre_norm, m_post_norm, m_w_in, m_gdn_conv, m_gdn_A_log, m_gdn_dt_bias, m_gdn_norm, m_ssd_conv, m_ssd_conv_b, m_ssd_A_log, m_ssd_dt_bias, m_ssd_D, m_ssd_norm, m_ret_norm, m_w_out, v_pre_norm, v_post_norm, v_w_in, v_gdn_conv, v_gdn_A_log, v_gdn_dt_bias, v_gdn_norm, v_ssd_conv, v_ssd_conv_b, v_ssd_A_log, v_ssd_dt_bias, v_ssd_D, v_ssd_norm, v_ret_norm, v_w_out):
    seq = x.shape[1]
    chip = 2 * lax.axis_index("x") + lax.axis_index("y")
    x0 = x[0]

    wi_b, wo_b = w_in.astype(BF16), w_out.astype(BF16)
    (wi0_g,) = _ag_rows([wi_b[0:1]], "ag_weights")
    gcv_g, scv_g = _ag_chips([gdn_conv, ssd_conv], "ag_conv")
    full_w_in = _relayout_w_in
    wp = [full_w_in(wi0_g[:, 0]), None]
    wo = [None, None]
    ag0 = _ChipExchange("gather", [wo_b[0]])
    ag1 = _ChipExchange("gather", [wi_b[1], wo_b[1]])
    gcv = jnp.transpose(gcv_g, (1, 2, 0, 3)).reshape(DEPTH, CONV_W, 1536)
    scv = jnp.transpose(scv_g, (1, 2, 0, 3)).reshape(DEPTH, CONV_W, 1536)
    rope_c, rope_s = _rope_tables(seq)

    saved = []
    xc = x0
    for l in range(DEPTH):
        p = dict(
            pn=_pad8(pre_norm[l], D_MODEL), qn=_pad8(post_norm[l], D_MODEL),
            g_cw=_pad8(gcv[l], 1536), g_prm=_pad8(jnp.stack([gdn_A_log[l], gdn_dt_bias[l]]), 128, 4),
            g_nw=_pad8(gdn_norm[l], 128),
            s_cw=_pad8(scv[l], 1536), s_cb=_pad8(ssd_conv_b[l], 1536),
            s_prm=_pad8(jnp.stack([ssd_A_log[l], ssd_dt_bias[l], ssd_D[l]]), 128), s_nw=_pad8(ssd_norm[l], SSD_W),
            r_nw=_pad8(ret_norm[l], 128))
        if l == 0:
            pg, ps, pr, gs, ss, ht, wo0_g = _make_inproj(seq, TL)(xc, p["pn"], wp[l], comm=ag0, comm_args=(wo_b[0],))
            wo[0] = wo0_g.reshape(2048, D_MODEL)
        else:
            pg, ps, pr, gs, ss, ht = _make_inproj(seq, TL)(xc, p["pn"], wp[l])
        if l == 0:
            oa, stg, tig, uwg, gpre, wi1_g, wo1_g = _make_gdn_fwd(seq, TB)(
                pg, gs, p["g_cw"], p["g_prm"], p["g_nw"], comm=ag1, comm_args=(wi_b[1], wo_b[1]))
            wp[1], wo[1] = full_w_in(wi1_g), wo1_g.reshape(2048, D_MODEL)
        else:
            oa, stg, tig, uwg, gpre = _make_gdn_fwd(seq, TB)(pg, gs, p["g_cw"], p["g_prm"], p["g_nw"])
        ob, sts, spre, sy = _make_ssd_fwd(seq, TB)(ps, ss, p["s_cw"], p["s_cb"], p["s_prm"], p["s_nw"])
        oc, str_ = _make_ret_fwd(seq, TB)(pr, rope_c, rope_s, p["r_nw"])
        if l == DEPTH - 1:
            out, dxn, lossp = _make_outproj_loss(seq, TL)(oa, ob, oc, wo[l], xc, p["qn"], loss_target[0])
            xn = None
        else:
            out, xn = _make_outproj(seq, TL)(oa, ob, oc, wo[l], xc, p["qn"])
        saved.append(dict(p=p, x=xc, ht=ht, spre=spre, sy=sy, gpre=gpre, pg=pg, ps=ps, pr=pr, gs=gs, ss=ss, stg=stg, tig=tig, uwg=uwg, sts=sts, str=str_,
                          oa=oa, ob=ob, oc=oc, out=out))
        xc = xn

    small = [None] * DEPTH
    gin, gin_b, gout, q_in, q_out = ([None] * DEPTH for _ in range(5))

    for l in reversed(range(DEPTH)):
        s = saved[l]
        p = s["p"]
        doa, dob, doc, dqn, dwo_l = _make_outproj_bwd(seq, TL)(dxn, s["out"], s["oa"], s["ob"], s["oc"], wo[l], p["qn"])
        gout[l] = dwo_l.reshape(4, 512, D_MODEL)
        gdn_args = (s["pg"], s["gpre"], s["gs"], p["g_cw"], p["g_prm"], p["g_nw"], s["stg"], s["tig"], s["uwg"], doa)
        if l == 0:
            payload = (gout[0].astype(BF16),)
            dpg, dgs, dcw_g, dprm_g, dnw_g, q_out[0] = _make_gdn_bwd(seq, TB)(
                *gdn_args, comm=_ChipExchange("scatter", payload), comm_args=payload)
        else:
            dpg, dgs, dcw_g, dprm_g, dnw_g = _make_gdn_bwd(seq, TB)(*gdn_args)
        ssd_args = (s["ps"], s["spre"], s["sy"], s["ss"], p["s_cw"], p["s_cb"], p["s_prm"], p["s_nw"], s["sts"], dob)
        if l == 0:
            payload = (gin_b[1], gout[1].astype(BF16))
            dps, dss, dcw_s, dcb_s, dprm_s, dnw_s, q_in[1], q_out[1] = _make_ssd_bwd(seq, TB)(
                *ssd_args, comm=_ChipExchange("scatter", payload), comm_args=payload)
        else:
            dps, dss, dcw_s, dcb_s, dprm_s, dnw_s = _make_ssd_bwd(seq, TB)(*ssd_args)
        dpr, dnw_r = _make_ret_bwd(seq, TB)(s["pr"], rope_c, rope_s, p["r_nw"], s["str"], doc)
        dws = [_make_inproj_bwd_dw(seq, TK, d.shape[1], tn, f"inproj_bwd_dw{i}")(s["ht"], d)
               for i, (d, tn) in enumerate(((dpg, 1024), (dps, 1280), (dpr, 1024),
                                            (jnp.concatenate([dgs, dss], axis=1), 256)))]
        gin[l], gin_b[l] = _unlayout_dw_in(*dws)
        dx_args = (dpg, dps, dpr, dgs, dss, wp[l], s["x"], p["pn"], dxn)
        if l == 0:
            payload = (gin_b[0],)
            dx, dpn, q_in[0] = _make_inproj_bwd_dx(seq, TL)(
                *dx_args, comm=_ChipExchange("scatter", payload), comm_args=payload)
        else:
            dx, dpn = _make_inproj_bwd_dx(seq, TL)(*dx_args)
        small[l] = [dpn[0], dqn[0], dcw_g[0:4].reshape(-1), dprm_g[0, 4:8], dprm_g[1, 4:8], dnw_g[0],
                    dcw_s[0:4].reshape(-1), dcb_s[0], dprm_s[0, 0:16], dprm_s[1, 0:16], dprm_s[2, 0:16],
                    dnw_s[0], dnw_r[0]]
        dxn = dx
    grad_x = dxn[None]

    sizes = [a.shape[0] for a in small[0]]
    flat = jnp.concatenate(small[0] + small[1] + [lossp[0, 0:1]])
    n_flat = flat.shape[0]
    rows = -(-n_flat // 1024) * 8
    red = _allreduce_small(jnp.pad(flat, (0, rows * 128 - n_flat)).reshape(rows, 128), "allreduce_small").reshape(-1)
    per = sum(sizes)
    loss = red[2 * per]

    def pick(i):
        off = sum(sizes[:i])
        return jnp.stack([red[l * per + off:l * per + off + sizes[i]] for l in range(DEPTH)])

    g_small = dict(
        pre_norm=pick(0), post_norm=pick(1),
        gdn_conv=lax.dynamic_slice_in_dim(pick(2).reshape(DEPTH, CONV_W, 1536), chip * 384, 384, axis=2),
        gdn_A_log=pick(3), gdn_dt_bias=pick(4), gdn_norm=pick(5),
        ssd_conv=lax.dynamic_slice_in_dim(pick(6).reshape(DEPTH, CONV_W, 1536), chip * 384, 384, axis=2),
        ssd_conv_b=pick(7), ssd_A_log=pick(8), ssd_dt_bias=pick(9), ssd_D=pick(10), ssd_norm=pick(11),
        ret_norm=pick(12))

    chip1 = chip.astype(jnp.int32).reshape(1)
    s_in = [_sum_chips(gin[l][:, None], q_in[l][:, None], chip1, f"sum_chips_w_in{l}") for l in range(DEPTH)]
    s_out = [_sum_chips(gout[l][:, None], q_out[l][:, None], chip1, f"sum_chips_w_out{l}") for l in range(DEPTH)]
    t_all = _swap_sibling(s_in + s_out, "swap_grads")
    t_in, t_out = t_all[:DEPTH], t_all[DEPTH:]

    weights = dict(pre_norm=pre_norm, post_norm=post_norm, w_in=w_in, gdn_conv=gdn_conv, gdn_A_log=gdn_A_log,
                   gdn_dt_bias=gdn_dt_bias, gdn_norm=gdn_norm, ssd_conv=ssd_conv, ssd_conv_b=ssd_conv_b,
                   ssd_A_log=ssd_A_log, ssd_dt_bias=ssd_dt_bias, ssd_D=ssd_D, ssd_norm=ssd_norm, ret_norm=ret_norm,
                   w_out=w_out)
    ms = dict(pre_norm=m_pre_norm, post_norm=m_post_norm, w_in=m_w_in, gdn_conv=m_gdn_conv, gdn_A_log=m_gdn_A_log,
              gdn_dt_bias=m_gdn_dt_bias, gdn_norm=m_gdn_norm, ssd_conv=m_ssd_conv, ssd_conv_b=m_ssd_conv_b,
              ssd_A_log=m_ssd_A_log, ssd_dt_bias=m_ssd_dt_bias, ssd_D=m_ssd_D, ssd_norm=m_ssd_norm,
              ret_norm=m_ret_norm, w_out=m_w_out)
    vs = dict(pre_norm=v_pre_norm, post_norm=v_post_norm, w_in=v_w_in, gdn_conv=v_gdn_conv, gdn_A_log=v_gdn_A_log,
              gdn_dt_bias=v_gdn_dt_bias, gdn_norm=v_gdn_norm, ssd_conv=v_ssd_conv, ssd_conv_b=v_ssd_conv_b,
              ssd_A_log=v_ssd_A_log, ssd_dt_bias=v_ssd_dt_bias, ssd_D=v_ssd_D, ssd_norm=v_ssd_norm,
              ret_norm=v_ret_norm, w_out=v_w_out)
    names = list(weights)
    res = {}
    for nme in names:
        if nme == "w_in":
            res[nme] = _adamw_pairs(w_in, s_in, t_in, m_w_in, v_w_in, "adamw_w_in")
        elif nme == "w_out":
            res[nme] = _adamw_pairs(w_out, s_out, t_out, m_w_out, v_w_out, "adamw_w_out")
        else:
            res[nme] = _adamw(weights[nme], g_small[nme], ms[nme], vs[nme], "adamw_" + nme)
    return (loss, grad_x, *[res[n][0] for n in names], *[res[n][1] for n in names],
            *[res[n][2] for n in names], *[res[n][3] for n in names])
```

```python
import functools
import math

import jax
import jax.numpy as jnp
from jax import lax
from jax.experimental import pallas as pl
from jax.experimental.pallas import tpu as pltpu

F32 = jnp.float32
BF16 = jnp.bfloat16
HI = lax.Precision.HIGHEST

D_MODEL = 1024
DEPTH = 2
CH = 64
CONV_W = 4
EPS = 1e-6
GDN_H, GDN_D = 4, 128
SSD_H, SSD_P, SSD_N, SSD_G = 16, 64, 128, 2
SSD_W = SSD_H * SSD_P
RET_H, RET_D = 4, 128
ROPE_BASE = 10000.0
N_IN = 6680
NEG = -1e30

VMEM_LIMIT = 56 * 1024 * 1024


def _dot(a, b):
    return jnp.dot(a.astype(BF16), b.astype(BF16), preferred_element_type=F32)


def _dot_nt(a, b):
    return lax.dot_general(a.astype(BF16), b.astype(BF16), (((1,), (1,)), ((), ())), preferred_element_type=F32)


def _dot_tn(a, b):
    return lax.dot_general(a.astype(BF16), b.astype(BF16), (((0,), (0,)), ((), ())), preferred_element_type=F32)


def _split(a):
    hi = a.astype(BF16)
    return hi, (a - hi.astype(F32)).astype(BF16)


def _dot01l(m, v):
    vh, vl = _split(v)
    mb = m.astype(BF16)
    return jnp.dot(mb, vh, preferred_element_type=F32) + jnp.dot(mb, vl, preferred_element_type=F32)


def _dot01r(v, m):
    vh, vl = _split(v)
    mb = m.astype(BF16)
    return jnp.dot(vh, mb, preferred_element_type=F32) + jnp.dot(vl, mb, preferred_element_type=F32)


def _sigmoid(x):
    return jax.nn.sigmoid(x)


def _silu(x):
    return x * _sigmoid(x)


def _dsilu(x):
    s = _sigmoid(x)
    return s * (1.0 + x * (1.0 - s))


def _softplus(x):
    return jnp.maximum(x, 0.0) + jnp.log1p(jnp.exp(-jnp.abs(x)))


def _iota2(shape, dim):
    return lax.broadcasted_iota(jnp.int32, shape, dim)


def _chunk_tri(tb, upper=False):
    r = _iota2((tb, tb), 0)
    c = _iota2((tb, tb), 1)
    same = jnp.right_shift(r, 6) == jnp.right_shift(c, 6)
    return (same & ((c >= r) if upper else (c <= r))).astype(F32)


def _masks():
    r = _iota2((CH, CH), 0)
    c = _iota2((CH, CH), 1)
    return r >= c, r > c, (r == c).astype(F32)


def _put_lane(col, lane_idx, width=128):
    lane = _iota2((col.shape[0], width), 1)
    return jnp.where(lane == lane_idx, col, 0.0)


def _conv_taps(raw, halo8, tb):
    ext = jnp.concatenate([halo8, raw], axis=0)
    return [raw] + [pltpu.roll(ext, s, axis=0)[8:] for s in (1, 2, 3)]


def _conv_back(dpre, nxt8, tb):
    ext = jnp.concatenate([dpre, nxt8], axis=0)
    return [dpre] + [pltpu.roll(ext, tb + 8 - s, axis=0)[:tb] for s in (1, 2, 3)]


def _rms_fwd(o, w, n):
    r = lax.rsqrt(jnp.sum(o * o, axis=-1, keepdims=True) * (1.0 / n) + EPS)
    on = o * r
    return on, r, on * w


def _rms_bwd(dy, on, r, w, n):
    don = dy * w
    return r * (don - on * (jnp.sum(don * on, axis=-1, keepdims=True) * (1.0 / n))), dy * on


def _put_cols(v, g, gw):
    z = jnp.zeros_like(v)
    return jnp.concatenate([v, z] if g == 0 else [z, v], axis=1)


def _gdn_common(pg_ref, halo8, sm, cw, prm, tb, pre=None):
    raw = pg_ref[:, 0:1536]
    if pre is None:
        taps = _conv_taps(raw, halo8, tb)
        pre = taps[0] * cw[3:4, :] + taps[1] * cw[2:3, :] + taps[2] * cw[1:2, :] + taps[3] * cw[0:1, :]
    act = _silu(pre)
    beta = _sigmoid(sm)
    sp_in = sm + prm[1:2, :]
    g = -jnp.exp(prm[0:1, :]) * _softplus(sp_in)
    gc = _dot01l(_chunk_tri(tb), g)
    return raw, pre, act, beta, sp_in, g, gc


_NN = (((2,), (1,)), ((0,), (0,)))
_NT = (((2,), (2,)), ((0,), (0,)))
_TN = (((1,), (1,)), ((0,), (0,)))


def _bdot(a, b, dn):
    return lax.dot_general(a.astype(BF16), b.astype(BF16), dn, preferred_element_type=F32)


def _dot3_parts(ah, al, bh, bl, dn):
    f = lambda p, q: lax.dot_general(p, q, dn, preferred_element_type=F32)
    return f(ah, bh) + (f(ah, bl) + f(al, bh))


def _bdot3(a, b, dn):
    ah, al = _split(a)
    bh, bl = _split(b)
    return _dot3_parts(ah, al, bh, bl, dn)


def _binv_unit_lower(a, eye):
    r = _iota2((CH, CH), 0)
    c = _iota2((CH, CH), 1)
    d = eye - jnp.where((jnp.right_shift(r, 1) == jnp.right_shift(c, 1)), a, 0.0)
    ah, al = _split(a)
    zero = jnp.zeros((), BF16)
    for lb in range(1, 6):
        same = jnp.right_shift(r, lb + 1) == jnp.right_shift(c, lb + 1)
        low = (jnp.bitwise_and(jnp.right_shift(r, lb), 1) == 1) & (jnp.bitwise_and(jnp.right_shift(c, lb), 1) == 0)
        oh, ol = jnp.where(same & low, ah, zero), jnp.where(same & low, al, zero)
        dh, dl = _split(d)
        t = _dot3_parts(oh, ol, dh, dl, _NN)
        th, tl = _split(t)
        d = d - _dot3_parts(dh, dl, th, tl, _NN)
    return d


def _rsum(v):
    return jnp.sum(v, axis=-1, keepdims=True)


def _gdn_batch(act, beta, gc, gct, eg_all, ncb, masks):
    causal, strict, _ = masks

    def st(fn):
        return jnp.stack([fn(c, h, slice(c * CH, (c + 1) * CH)) for c in range(ncb) for h in range(GDN_H)])

    qr = st(lambda c, h, r: act[r, h * 128:(h + 1) * 128])
    kr = st(lambda c, h, r: act[r, 512 + h * 128:512 + (h + 1) * 128])
    vh = st(lambda c, h, r: act[r, 1024 + h * 128:1024 + (h + 1) * 128])
    bh = st(lambda c, h, r: beta[r, h:h + 1])
    gcol = st(lambda c, h, r: gc[r, 4 + h:5 + h])
    grow = st(lambda c, h, r: gct[4 + h:5 + h, r])
    eg = st(lambda c, h, r: eg_all[r, 4 + h:5 + h])
    glast = st(lambda c, h, r: gc[(c + 1) * CH - 1:(c + 1) * CH, 4 + h:5 + h])
    rq = lax.rsqrt(_rsum(qr * qr) + EPS)
    rk = lax.rsqrt(_rsum(kr * kr) + EPS)
    qn = qr * rq
    kh = kr * rk
    qh = qn * (GDN_D ** -0.5)
    decay = jnp.exp(jnp.where(causal, gcol - grow, NEG))
    kb = kh * bh
    kd_scale = jnp.exp(glast - gcol)
    return dict(qn=qn, rq=rq, kh=kh, rk=rk, qh=qh, vh=vh, bh=bh, eg=eg, decay=decay, kb=kb, vb=vh * bh, kg=kb * eg,
                qg=qh * eg, kd_scale=kd_scale, kdec=kh * kd_scale, egl=jnp.exp(glast),
                a=jnp.where(strict, _bdot(kb, kh, _NT) * decay, 0.0), attn=_bdot(qh, kh, _NT) * decay)


def _make_gdn_fwd(seq, tb):
    ncb = tb // CH
    nb = seq // tb
    n = ncb * GDN_H

    def body(pg_ref, sm_ref, cw_ref, prm_ref, nw_ref, oa_ref, st_ref, ti_ref, uw_ref, pre_ref, s_scr, halo_scr):
        @pl.when(pl.program_id(0) == 0)
        def _():
            s_scr[...] = jnp.zeros_like(s_scr)
            halo_scr[...] = jnp.zeros_like(halo_scr)

        masks = _masks()
        sm = sm_ref[...]
        raw, pre, act, beta, _, _, gc = _gdn_common(pg_ref, halo_scr[...], sm, cw_ref[...], prm_ref[...], tb)
        halo_scr[...] = raw[tb - 8:tb, :]
        pre_ref[...] = pre
        d = _gdn_batch(act, beta, gc, gc.T, jnp.exp(gc), ncb, masks)
        t = _binv_unit_lower(d["a"], masks[2])
        sol = _bdot3(t, jnp.concatenate([d["vb"], d["kg"]], axis=2), _NN)
        ti_ref[...] = t.reshape(ncb, GDN_H, CH, CH)
        uw_ref[...] = sol.reshape(ncb, GDN_H, CH, 256)
        u, w = sol[:, :, :128], sol[:, :, 128:]
        vns = []
        for c in range(ncb):
            bs = slice(c * GDN_H, (c + 1) * GDN_H)
            s = s_scr[...]
            st_ref[c] = s
            vn = u[bs] - _bdot(w[bs], s, _NN)
            s_scr[...] = s * d["egl"][bs] + _bdot(d["kdec"][bs], vn, _TN)
            vns.append(vn)
        v_new = jnp.concatenate(vns, axis=0)
        s_prev = st_ref[...].reshape(n, 128, 128)
        o = _bdot(d["qg"], s_prev, _NN) + _bdot(d["attn"], v_new, _NN)
        _, _, y = _rms_fwd(o, nw_ref[0:1, :], GDN_D)
        for c in range(ncb):
            rows = slice(c * CH, (c + 1) * CH)
            for h in range(GDN_H):
                z = pg_ref[rows, 1536 + h * 128:1536 + (h + 1) * 128]
                oa_ref[rows, h * 128:(h + 1) * 128] = (y[c * GDN_H + h] * _silu(z)).astype(oa_ref.dtype)

    def call(pg, sm, cw, prm, nw, comm=None, comm_args=()):
        blk4 = lambda i: (i, 0, 0, 0)
        cx = _exchange_specs(comm)
        return pl.pallas_call(
            _with_exchange(body, comm, 5, 5, nb),
            grid=(nb,),
            in_specs=[
                pl.BlockSpec((tb, 2048), lambda i: (i, 0)),
                pl.BlockSpec((tb, 128), lambda i: (i, 0)),
                pl.BlockSpec((8, 1536), lambda i: (0, 0)),
                pl.BlockSpec((8, 128), lambda i: (0, 0)),
                pl.BlockSpec((8, 128), lambda i: (0, 0)),
            ] + cx["specs"],
            out_specs=[
                pl.BlockSpec((tb, 512), lambda i: (i, 0)),
                pl.BlockSpec((ncb, GDN_H, 128, 128), blk4),
                pl.BlockSpec((ncb, GDN_H, CH, CH), blk4),
                pl.BlockSpec((ncb, GDN_H, CH, 256), blk4),
                pl.BlockSpec((tb, 1536), lambda i: (i, 0)),
            ] + cx["specs"],
            out_shape=[
                jax.ShapeDtypeStruct((seq, 512), BF16),
                jax.ShapeDtypeStruct((seq // CH, GDN_H, 128, 128), F32),
                jax.ShapeDtypeStruct((seq // CH, GDN_H, CH, CH), F32),
                jax.ShapeDtypeStruct((seq // CH, GDN_H, CH, 256), F32),
                jax.ShapeDtypeStruct((seq, 1536), F32),
            ] + cx["out_shape"],
            scratch_shapes=[pltpu.VMEM((GDN_H, 128, 128), F32), pltpu.VMEM((8, 1536), F32)] + cx["scratch"],
            compiler_params=pltpu.CompilerParams(dimension_semantics=("arbitrary",), vmem_limit_bytes=VMEM_LIMIT,
                                                 has_side_effects=comm is not None),
            name="gdn_fwd" + cx["tag"],
        )(pg, sm, cw, prm, nw, *comm_args)

    return call


def _make_gdn_bwd(seq, tb):
    ncb = tb // CH
    nb = seq // tb
    hb = tb // 8
    n = ncb * GDN_H

    def body(pg_ref, pre_ref, sm_ref, cw_ref, prm_ref, nw_ref, st_ref, ti_ref, uw_ref, doa_ref,
             dpg_ref, dsm_ref, dcw_ref, dprm_ref, dnw_ref, ds_scr, nxt_scr):
        i = pl.program_id(0)

        @pl.when(i == 0)
        def _():
            ds_scr[...] = jnp.zeros_like(ds_scr)
            nxt_scr[...] = jnp.zeros_like(nxt_scr)
            dcw_ref[...] = jnp.zeros_like(dcw_ref)
            dprm_ref[...] = jnp.zeros_like(dprm_ref)
            dnw_ref[...] = jnp.zeros_like(dnw_ref)

        masks = _masks()
        strict = masks[1]
        sm = sm_ref[...]
        cw = cw_ref[...]
        prm = prm_ref[...]
        raw, pre, act, beta, sp_in, g, gc = _gdn_common(pg_ref, None, sm, cw, prm, tb, pre=pre_ref[...])
        nw = nw_ref[0:1, :]
        row_id = _iota2((CH, 1), 0)
        d = _gdn_batch(act, beta, gc, gc.T, jnp.exp(gc), ncb, masks)
        t = ti_ref[...].reshape(n, CH, CH)
        sol = uw_ref[...].reshape(n, CH, 256)
        u, w = sol[:, :, :128], sol[:, :, 128:]
        s_prev = st_ref[...].reshape(n, 128, 128)
        v_new = u - _bdot(w, s_prev, _NN)
        o = _bdot(d["qg"], s_prev, _NN) + _bdot(d["attn"], v_new, _NN)

        pairs = [(c, h) for c in range(ncb) for h in range(GDN_H)]
        z = jnp.stack([pg_ref[c * CH:(c + 1) * CH, 1536 + h * 128:1536 + (h + 1) * 128] for c, h in pairs])
        doa = jnp.stack([doa_ref[c * CH:(c + 1) * CH, h * 128:(h + 1) * 128] for c, h in pairs])
        on, r, y = _rms_fwd(o, nw, GDN_D)
        dz = doa * y * _dsilu(z)
        do, dnw_rows = _rms_bwd(doa * _silu(z), on, r, nw, GDN_D)
        dnw_acc = jnp.sum(jnp.sum(dnw_rows, axis=0), axis=0, keepdims=True)

        dvn_in = _bdot(d["attn"], do, _TN)
        qgtdo = _bdot(d["qg"], do, _TN)
        dvn_l, dkdec_l, dgl_l = [None] * ncb, [None] * ncb, [None] * ncb
        for c in reversed(range(ncb)):
            bs = slice(c * GDN_H, (c + 1) * GDN_H)
            dsn = ds_scr[...]
            dvn_c = dvn_in[bs] + _bdot(d["kdec"][bs], dsn, _NN)
            ds_scr[...] = d["egl"][bs] * dsn + qgtdo[bs] - _bdot(w[bs], dvn_c, _TN)
            dvn_l[c] = dvn_c
            dkdec_l[c] = _bdot(v_new[bs], dsn, _NT)
            dgl_l[c] = d["egl"][bs] * jnp.sum(_rsum(s_prev[bs] * dsn), axis=1, keepdims=True)
        dvn = jnp.concatenate(dvn_l, axis=0)
        dkdec = jnp.concatenate(dkdec_l, axis=0)
        dglast = jnp.concatenate(dgl_l, axis=0)

        dqg = _bdot(do, s_prev, _NT)
        dattn = _bdot(do, v_new, _NT)
        dw = -_bdot(dvn, s_prev, _NT)
        drhs = _bdot3(t, jnp.concatenate([dvn, dw], axis=2), _TN)
        dvb, dkg = drhs[:, :, :128], drhs[:, :, 128:]
        da = jnp.where(strict, -(_bdot(dvb, u, _NT) + _bdot(dkg, w, _NT)), 0.0)
        dp = da * d["decay"]
        dq_m = dattn * d["decay"]
        m = da * d["a"] + dattn * d["attn"]
        upper_tri = jnp.broadcast_to((_iota2((CH, CH), 1) >= _iota2((CH, CH), 0)).astype(BF16), (n, CH, CH))
        dg_in = _rsum(jnp.where(strict, _bdot(upper_tri, m, _NN), 0.0))
        dkb = _bdot(dp, d["kh"], _NN) + dkg * d["eg"]
        kdk_row = _rsum(dkdec * d["kdec"])
        dk = _bdot(dp, d["kb"], _TN) + _bdot(dq_m, d["qh"], _TN) + dkdec * d["kd_scale"] + dkb * d["bh"]
        dq = _bdot(dq_m, d["kh"], _NN) + dqg * d["eg"]
        dglast = dglast + jnp.sum(kdk_row, axis=1, keepdims=True)
        dgcol = (_rsum(dqg * d["qg"]) + _rsum(dkg * d["kg"]) - kdk_row + jnp.where(row_id == CH - 1, dglast, 0.0))
        dbeta = _rsum(dkb * d["kh"]) + _rsum(dvb * d["vh"])
        dn = dq * (GDN_D ** -0.5)
        dact_q = d["rq"] * (dn - d["qn"] * _rsum(dn * d["qn"]))
        dact_k = d["rk"] * (dk - d["kh"] * _rsum(dk * d["kh"]))
        dact_v = dvb * d["bh"]

        def lanes(v, lane0):
            return jnp.concatenate(
                [sum(_put_lane(v[c * GDN_H + h], lane0 + h) for h in range(GDN_H)) for c in range(ncb)], axis=0)

        def tokens(v):
            return jnp.concatenate(
                [jnp.concatenate([v[c * GDN_H + h] for h in range(GDN_H)], axis=1) for c in range(ncb)], axis=0)

        dbeta_all = lanes(dbeta, 0)
        dg = _dot01l(_chunk_tri(tb, upper=True), lanes(dgcol, 4)) + lanes(dg_in, 4)
        neg_ea = -jnp.exp(prm[0:1, :])
        da_raw = dg * neg_ea * _sigmoid(sp_in)
        db_raw = dbeta_all * beta * (1.0 - beta)
        dsm_ref[...] = (da_raw + db_raw).astype(dsm_ref.dtype)
        lane8 = _iota2((8, 128), 1)
        sub8 = _iota2((8, 128), 0)
        dalog = jnp.sum(dg * g, axis=0, keepdims=True)
        ddtb = jnp.sum(da_raw, axis=0, keepdims=True)
        dprm_ref[...] += jnp.where(sub8 == 0, dalog, 0.0) + jnp.where(sub8 == 1, ddtb, 0.0)
        dnw_ref[...] += jnp.where(sub8 == 0, dnw_acc, 0.0)

        dact = jnp.concatenate([tokens(dact_q), tokens(dact_k), tokens(dact_v)], axis=1)
        dpre = dact * _dsilu(pre)
        back = _conv_back(dpre, nxt_scr[...], tb)
        nxt_scr[...] = dpre[0:8, :]
        draw = back[0] * cw[3:4, :] + back[1] * cw[2:3, :] + back[2] * cw[1:2, :] + back[3] * cw[0:1, :]
        dpg_ref[:, 0:1536] = draw.astype(dpg_ref.dtype)
        dpg_ref[:, 1536:2048] = tokens(dz).astype(dpg_ref.dtype)
        sub_c = _iota2((8, 1536), 0)
        dcw_new = jnp.zeros((8, 1536), F32)
        for s_ in range(CONV_W):
            dcw_new = dcw_new + jnp.where(sub_c == 3 - s_, jnp.sum(back[s_] * raw, axis=0, keepdims=True), 0.0)
        dcw_ref[...] += dcw_new

    def call(pg, pre, sm, cw, prm, nw, st, ti, uw, doa, comm=None, comm_args=()):
        rev = lambda i: (nb - 1 - i, 0)
        const = lambda i: (0, 0)
        cx = _exchange_specs(comm)
        return pl.pallas_call(
            _with_exchange(body, comm, 10, 5, nb),
            grid=(nb,),
            in_specs=[
                pl.BlockSpec((tb, 2048), rev),
                pl.BlockSpec((tb, 1536), rev),
                pl.BlockSpec((tb, 128), rev),
                pl.BlockSpec((8, 1536), const),
                pl.BlockSpec((8, 128), const),
                pl.BlockSpec((8, 128), const),
                pl.BlockSpec((ncb, GDN_H, 128, 128), lambda i: (nb - 1 - i, 0, 0, 0)),
                pl.BlockSpec((ncb, GDN_H, CH, CH), lambda i: (nb - 1 - i, 0, 0, 0)),
                pl.BlockSpec((ncb, GDN_H, CH, 256), lambda i: (nb - 1 - i, 0, 0, 0)),
                pl.BlockSpec((tb, 512), rev),
            ] + cx["specs"],
            out_specs=[
                pl.BlockSpec((tb, 2048), rev),
                pl.BlockSpec((tb, 128), rev),
                pl.BlockSpec((8, 1536), const),
                pl.BlockSpec((8, 128), const),
                pl.BlockSpec((8, 128), const),
            ] + cx["specs"],
            out_shape=[
                jax.ShapeDtypeStruct((seq, 2048), BF16),
                jax.ShapeDtypeStruct((seq, 128), BF16),
                jax.ShapeDtypeStruct((8, 1536), F32),
                jax.ShapeDtypeStruct((8, 128), F32),
                jax.ShapeDtypeStruct((8, 128), F32),
            ] + cx["out_shape"],
            scratch_shapes=[pltpu.VMEM((GDN_H, 128, 128), F32), pltpu.VMEM((8, 1536), F32)] + cx["scratch"],
            compiler_params=pltpu.CompilerParams(dimension_semantics=("arbitrary",), vmem_limit_bytes=VMEM_LIMIT,
                                                 has_side_effects=comm is not None),
            name="gdn_bwd" + cx["tag"],
        )(pg, pre, sm, cw, prm, nw, st, ti, uw, doa, *comm_args)

    return call


def _expand_mat():
    r = _iota2((128, SSD_W), 0)
    c = _iota2((128, SSD_W), 1)
    return (jnp.right_shift(c, 6) == r).astype(F32)


def _reduce_heads(v, e):
    vh, vl = _split(v)
    eb = e.astype(BF16)
    nt = (((1,), (1,)), ((), ()))
    return (lax.dot_general(vh, eb, nt, preferred_element_type=F32)
            + lax.dot_general(vl, eb, nt, preferred_element_type=F32))


def _row8(v):
    return jnp.broadcast_to(v, (8, v.shape[1]))


def _ssd_common(ps_ref, halo8, ss, cw, cb, prm, tb, pre=None):
    raw = ps_ref[:, 0:1536]
    taps = None
    if pre is None:
        taps = _conv_taps(raw, halo8, tb)
        pre = taps[0] * cw[3:4, :] + taps[1] * cw[2:3, :] + taps[2] * cw[1:2, :] + taps[3] * cw[0:1, :] + cb[0:1, :]
    act = _silu(pre)
    dt_in = ss + prm[1:2, :]
    dt = _softplus(dt_in)
    a = dt * (-jnp.exp(prm[0:1, :]))
    acum = _dot01l(_chunk_tri(tb), a)
    e = _expand_mat()
    dt_e = _dot01r(dt, e)
    xdt = act[:, 0:SSD_W] * dt_e
    ea_e = _dot01r(jnp.exp(acum), e)
    d_e = _dot01r(_row8(prm[2:3, :]), e)[0:1, :]
    return raw, taps, pre, act, dt_in, dt, a, acum, e, dt_e, xdt, ea_e, d_e


def _ssd_chunk(act, acum, act_t, e, c):
    r0 = c * CH
    rows = slice(r0, r0 + CH)
    alast = acum[r0 + CH - 1:r0 + CH, :]
    wdec = jnp.exp(alast - acum[rows, :])
    wd_e = _dot01r(wdec, e)
    eal_e = _dot01r(_row8(jnp.exp(alast)), e)[0:1, :]
    return rows, wd_e, eal_e


def _ssd_lmat(acum, act_t, c, h, causal):
    r0 = c * CH
    acol = acum[r0:r0 + CH, h:h + 1]
    arow = act_t[h:h + 1, r0:r0 + CH]
    return jnp.exp(jnp.where(causal, acol - arow, NEG))


def _make_ssd_fwd(seq, tb):
    ncb = tb // CH
    nb = seq // tb
    hg = SSD_H // SSD_G
    gw = SSD_W // SSD_G

    def body(ps_ref, ss_ref, cw_ref, cb_ref, prm_ref, nw_ref, ob_ref, st_ref, pre_ref, y_ref, hs_scr, halo_scr):
        @pl.when(pl.program_id(0) == 0)
        def _():
            hs_scr[...] = jnp.zeros_like(hs_scr)
            halo_scr[...] = jnp.zeros_like(halo_scr)

        causal, _, _ = _masks()
        (raw, _, pre, act, _, _, _, acum, e, _, xdt, ea_e, d_e) = _ssd_common(
            ps_ref, halo_scr[...], ss_ref[...], cw_ref[...], cb_ref[...], prm_ref[...], tb)
        halo_scr[...] = raw[tb - 8:tb, :]
        pre_ref[...] = pre
        act_t = acum.T
        nw = nw_ref[0:1, :]
        for c in range(ncb):
            rows, wd_e, eal_e = _ssd_chunk(act, acum, act_t, e, c)
            st_ref[c] = hs_scr[...]
            ys = []
            for g in range(SSD_G):
                gc_ = slice(g * gw, (g + 1) * gw)
                bg = act[rows, SSD_W + g * 128:SSD_W + (g + 1) * 128]
                cg = act[rows, SSD_W + 256 + g * 128:SSD_W + 256 + (g + 1) * 128]
                cbm = _dot_nt(cg, bg)
                hs = hs_scr[:, gc_]
                yin = _dot(cg, hs)
                yh = []
                for hh in range(hg):
                    h = g * hg + hh
                    lm = _ssd_lmat(acum, act_t, c, h, causal)
                    yh.append(_dot(cbm * lm, xdt[rows, h * SSD_P:(h + 1) * SSD_P]))
                ys.append(jnp.concatenate(yh, axis=1) + yin * ea_e[rows, gc_])
                hs_scr[:, gc_] = hs * eal_e[:, gc_] + _dot_tn(bg, xdt[rows, gc_] * wd_e[:, gc_])
            y = jnp.concatenate(ys, axis=1) + act[rows, 0:SSD_W] * d_e
            y_ref[rows, :] = y
            yz = y * _silu(ps_ref[rows, 1536:2560])
            outs = [_rms_fwd(yz[:, g * gw:(g + 1) * gw], nw[:, g * gw:(g + 1) * gw], gw)[2] for g in range(SSD_G)]
            ob_ref[rows, :] = jnp.concatenate(outs, axis=1).astype(ob_ref.dtype)

    def call(ps, ss, cw, cb, prm, nw):
        const = lambda i: (0, 0)
        return pl.pallas_call(
            body,
            grid=(nb,),
            in_specs=[
                pl.BlockSpec((tb, 2560), lambda i: (i, 0)),
                pl.BlockSpec((tb, 128), lambda i: (i, 0)),
                pl.BlockSpec((8, 1536), const),
                pl.BlockSpec((8, 1536), const),
                pl.BlockSpec((8, 128), const),
                pl.BlockSpec((8, SSD_W), const),
            ],
            out_specs=[
                pl.BlockSpec((tb, SSD_W), lambda i: (i, 0)),
                pl.BlockSpec((ncb, SSD_N, SSD_W), lambda i: (i, 0, 0)),
                pl.BlockSpec((tb, 1536), lambda i: (i, 0)),
                pl.BlockSpec((tb, SSD_W), lambda i: (i, 0)),
            ],
            out_shape=[
                jax.ShapeDtypeStruct((seq, SSD_W), BF16),
                jax.ShapeDtypeStruct((seq // CH, SSD_N, SSD_W), F32),
                jax.ShapeDtypeStruct((seq, 1536), F32),
                jax.ShapeDtypeStruct((seq, SSD_W), F32),
            ],
            scratch_shapes=[pltpu.VMEM((SSD_N, SSD_W), F32), pltpu.VMEM((8, 1536), F32)],
            compiler_params=pltpu.CompilerParams(dimension_semantics=("arbitrary",), vmem_limit_bytes=VMEM_LIMIT),
            name="ssd_fwd",
        )(ps, ss, cw, cb, prm, nw)

    return call


def _make_ssd_bwd(seq, tb):
    ncb = tb // CH
    nb = seq // tb
    hb = tb // 8
    hg = SSD_H // SSD_G
    gw = SSD_W // SSD_G

    def body(ps_ref, pre_ref, y_ref, ss_ref, cw_ref, cb_ref, prm_ref, nw_ref, st_ref, dob_ref,
             dps_ref, dss_ref, dcw_ref, dcb_ref, dprm_ref, dnw_ref, dhs_scr, nxt_scr):
        i = pl.program_id(0)

        @pl.when(i == 0)
        def _():
            dhs_scr[...] = jnp.zeros_like(dhs_scr)
            nxt_scr[...] = jnp.zeros_like(nxt_scr)
            dcw_ref[...] = jnp.zeros_like(dcw_ref)
            dcb_ref[...] = jnp.zeros_like(dcb_ref)
            dprm_ref[...] = jnp.zeros_like(dprm_ref)
            dnw_ref[...] = jnp.zeros_like(dnw_ref)

        causal, _, _ = _masks()
        cw = cw_ref[...]
        prm = prm_ref[...]
        (raw, _, pre, act, dt_in, dt, a, acum, e, dt_e, xdt, ea_e, d_e) = _ssd_common(
            ps_ref, None, ss_ref[...], cw, cb_ref[...], prm, tb, pre=pre_ref[...])
        act_t = acum.T
        nw = nw_ref[0:1, :]
        row_id = _iota2((CH, 1), 0)

        dx_l, db_l, dc_l, dz_l, dacum_l, ddt_l, da_in_l = ([None] * ncb for _ in range(7))
        upper_tri = (_iota2((CH, CH), 1) >= _iota2((CH, CH), 0)).astype(F32)
        below = jnp.bitwise_and(_iota2((CH, gw), 1), CH - 1) < _iota2((CH, gw), 0)
        dnw_acc = jnp.zeros((1, SSD_W), F32)
        dd_acc = jnp.zeros((1, SSD_W), F32)

        for c in reversed(range(ncb)):
            rows, wd_e, eal_e = _ssd_chunk(act, acum, act_t, e, c)
            xc = act[rows, 0:SSD_W]
            z = ps_ref[rows, 1536:2560]
            dob = dob_ref[rows, :]
            sz = _silu(z)
            dy_g, dz_g, zacc_g, dxdt_g, dal_g, db_g, dc_g, da_in_g = [], [], [], [], [], [], [], []
            for g in range(SSD_G):
                gc_ = slice(g * gw, (g + 1) * gw)
                bg = act[rows, SSD_W + g * 128:SSD_W + (g + 1) * 128]
                cg = act[rows, SSD_W + 256 + g * 128:SSD_W + 256 + (g + 1) * 128]
                cbm = _dot_nt(cg, bg)
                hs = st_ref[c, :, gc_]
                yin = _dot(cg, hs)
                lmats = [_ssd_lmat(acum, act_t, c, g * hg + hh, causal) for hh in range(hg)]
                ea_g = ea_e[rows, gc_]
                y = y_ref[rows, gc_]
                yz = y * sz[:, gc_]
                on, r, _ = _rms_fwd(yz, nw[:, gc_], gw)
                dyz, dnw_rows = _rms_bwd(dob[:, gc_], on, r, nw[:, gc_], gw)
                dnw_acc = dnw_acc + _put_cols(jnp.sum(dnw_rows, axis=0, keepdims=True), g, gw)
                dy = dyz * sz[:, gc_]
                dz_g.append(dyz * y * _dsilu(z[:, gc_]))
                dd_acc = dd_acc + _put_cols(jnp.sum(dy * xc[:, gc_], axis=0, keepdims=True), g, gw)
                dhs_n = dhs_scr[:, gc_]
                dyin = dy * ea_g
                dcg = _dot_nt(dyin, hs)
                xw = xdt[rows, gc_] * wd_e[:, gc_]
                dbg = _dot_nt(xw, dhs_n)
                dxw = _dot(bg, dhs_n)
                dhs_scr[:, gc_] = dhs_n * eal_e[:, gc_] + _dot_tn(cg, dyin)
                dal_g.append(jnp.sum(hs * dhs_n, axis=0, keepdims=True) * eal_e[:, gc_]
                             + jnp.sum(dxw * xw, axis=0, keepdims=True))
                dxi, ms, dcbm = [], [], jnp.zeros((CH, CH), F32)
                for hh in range(hg):
                    h = g * hg + hh
                    hc = slice(hh * SSD_P, (hh + 1) * SSD_P)
                    dyh = dy[:, hc]
                    lm = cbm * lmats[hh]
                    dxi.append(_dot_tn(lm, dyh))
                    dlm = _dot_nt(dyh, xdt[rows, h * SSD_P:(h + 1) * SSD_P])
                    ms.append(dlm * lm)
                    dcbm = dcbm + dlm * lmats[hh]
                dx_intra = jnp.concatenate(dxi, axis=1)
                ncat = _dot(upper_tri, jnp.concatenate(ms, axis=1))
                da_in_g.append(jnp.where(below, ncat, 0.0))
                zacc_g.append(dy * yin * ea_g - dxw * xw)
                dxdt_g.append(dx_intra + dxw * wd_e[:, gc_])
                dy_g.append(dy)
                db_g.append(dbg + _dot_tn(dcbm, cg))
                dc_g.append(dcg + _dot(dcbm, bg))
            dy = jnp.concatenate(dy_g, axis=1)
            dxdt = jnp.concatenate(dxdt_g, axis=1)
            dx_l[c] = dxdt * dt_e[rows, :] + dy * d_e
            db_l[c] = jnp.concatenate(db_g, axis=1)
            dc_l[c] = jnp.concatenate(dc_g, axis=1)
            dz_l[c] = jnp.concatenate(dz_g, axis=1)
            ddt_l[c] = _reduce_heads(dxdt * xc, e)
            dalast = _reduce_heads(_row8(jnp.concatenate(dal_g, axis=1)), e)[0:1, :]
            dacum_l[c] = _reduce_heads(jnp.concatenate(zacc_g, axis=1), e) + jnp.where(row_id == CH - 1, dalast, 0.0)
            da_in_l[c] = _reduce_heads(jnp.concatenate(da_in_g, axis=1), e)

        dacum_all = jnp.concatenate(dacum_l, axis=0)
        da = _dot01l(_chunk_tri(tb, upper=True), dacum_all) + jnp.concatenate(da_in_l, axis=0)
        neg_ea = -jnp.exp(prm[0:1, :])
        ddt = jnp.concatenate(ddt_l, axis=0) + da * neg_ea
        ddt_in = ddt * _sigmoid(dt_in)
        dss_ref[...] = ddt_in.astype(dss_ref.dtype)
        sub8 = _iota2((8, 128), 0)
        dalog = jnp.sum(da * a, axis=0, keepdims=True)
        ddtb = jnp.sum(ddt_in, axis=0, keepdims=True)
        dd = _reduce_heads(_row8(dd_acc), e)[0:1, :]
        dprm_ref[...] += (jnp.where(sub8 == 0, dalog, 0.0) + jnp.where(sub8 == 1, ddtb, 0.0)
                          + jnp.where(sub8 == 2, dd, 0.0))
        dnw_ref[...] += jnp.where(_iota2((8, SSD_W), 0) == 0, dnw_acc, 0.0)

        dact = jnp.concatenate([jnp.concatenate(dx_l, axis=0), jnp.concatenate(db_l, axis=0),
                                jnp.concatenate(dc_l, axis=0)], axis=1)
        dpre = dact * _dsilu(pre)
        back = _conv_back(dpre, nxt_scr[...], tb)
        nxt_scr[...] = dpre[0:8, :]
        draw = back[0] * cw[3:4, :] + back[1] * cw[2:3, :] + back[2] * cw[1:2, :] + back[3] * cw[0:1, :]
        dps_ref[:, 0:1536] = draw.astype(dps_ref.dtype)
        dps_ref[:, 1536:2560] = jnp.concatenate(dz_l, axis=0).astype(dps_ref.dtype)
        sub_c = _iota2((8, 1536), 0)
        dcw_new = jnp.zeros((8, 1536), F32)
        for s_ in range(CONV_W):
            dcw_new = dcw_new + jnp.where(sub_c == 3 - s_, jnp.sum(back[s_] * raw, axis=0, keepdims=True), 0.0)
        dcw_ref[...] += dcw_new
        dcb_ref[...] += jnp.where(sub_c == 0, jnp.sum(dpre, axis=0, keepdims=True), 0.0)

    def call(ps, pre, y, ss, cw, cb, prm, nw, st, dob, comm=None, comm_args=()):
        rev = lambda i: (nb - 1 - i, 0)
        const = lambda i: (0, 0)
        cx = _exchange_specs(comm)
        return pl.pallas_call(
            _with_exchange(body, comm, 10, 6, nb),
            grid=(nb,),
            in_specs=[
                pl.BlockSpec((tb, 2560), rev),
                pl.BlockSpec((tb, 1536), rev),
                pl.BlockSpec((tb, SSD_W), rev),
                pl.BlockSpec((tb, 128), rev),
                pl.BlockSpec((8, 1536), const),
                pl.BlockSpec((8, 1536), const),
                pl.BlockSpec((8, 128), const),
                pl.BlockSpec((8, SSD_W), const),
                pl.BlockSpec((ncb, SSD_N, SSD_W), lambda i: (nb - 1 - i, 0, 0)),
                pl.BlockSpec((tb, SSD_W), rev),
            ] + cx["specs"],
            out_specs=[
                pl.BlockSpec((tb, 2560), rev),
                pl.BlockSpec((tb, 128), rev),
                pl.BlockSpec((8, 1536), const),
                pl.BlockSpec((8, 1536), const),
                pl.BlockSpec((8, 128), const),
                pl.BlockSpec((8, SSD_W), const),
            ] + cx["specs"],
            out_shape=[
                jax.ShapeDtypeStruct((seq, 2560), BF16),
                jax.ShapeDtypeStruct((seq, 128), BF16),
                jax.ShapeDtypeStruct((8, 1536), F32),
                jax.ShapeDtypeStruct((8, 1536), F32),
                jax.ShapeDtypeStruct((8, 128), F32),
                jax.ShapeDtypeStruct((8, SSD_W), F32),
            ] + cx["out_shape"],
            scratch_shapes=[pltpu.VMEM((SSD_N, SSD_W), F32), pltpu.VMEM((8, 1536), F32)] + cx["scratch"],
            compiler_params=pltpu.CompilerParams(dimension_semantics=("arbitrary",), vmem_limit_bytes=VMEM_LIMIT,
                                                 has_side_effects=comm is not None),
            name="ssd_bwd" + cx["tag"],
        )(ps, pre, y, ss, cw, cb, prm, nw, st, dob, *comm_args)

    return call


def _ret_consts(h):
    lg = math.log(1.0 - 2.0 ** (-5.0 - h))
    r = _iota2((CH, CH), 0)
    c = _iota2((CH, CH), 1)
    rel = (r - c).astype(F32)
    dmat = jnp.where(r >= c, jnp.exp(jnp.maximum(rel, 0.0) * lg), 0.0)
    idx = _iota2((CH, 1), 0).astype(F32)
    qdec = jnp.exp((idx + 1.0) * lg)
    kdec = jnp.exp((CH - 1.0 - idx) * lg)
    cdec = math.exp(CH * lg)
    return dmat, qdec, kdec, cdec


def _ret_batch(pr_ref, cc_ref, ss_ref, ncb):
    pairs = [(c, h) for c in range(ncb) for h in range(RET_H)]

    def st(off):
        return jnp.stack([pr_ref[c * CH:(c + 1) * CH, off + h * 128:off + (h + 1) * 128] for c, h in pairs])

    cc = jnp.stack([cc_ref[c * CH:(c + 1) * CH, :] for c, _ in pairs])
    ss = jnp.stack([ss_ref[c * CH:(c + 1) * CH, :] for c, _ in pairs])
    consts = [_ret_consts(h) for h in range(RET_H)]
    dmat = jnp.stack([consts[h][0] for _, h in pairs])
    qdec = jnp.stack([consts[h][1] for _, h in pairs])
    kdec = jnp.stack([consts[h][2] for _, h in pairs])
    cdec = jnp.stack([jnp.full((1, 1), consts[h][3], F32) for h in range(RET_H)])
    q = _rot(st(0), cc, ss)
    k = _rot(st(512), cc, ss) * (RET_D ** -0.5)
    return dict(q=q, k=k, v=st(1024), z=st(1536), cc=cc, ss=ss, dmat=dmat, qdec=qdec, kdec=kdec, cdec=cdec,
                s=_bdot(q, k, _NT) * dmat)


def _rot(t, cc, ss):
    return t * cc + pltpu.roll(t, 64, axis=t.ndim - 1) * ss


def _rot_bwd(d, cc, ss):
    return d * cc + pltpu.roll(d * ss, 64, axis=d.ndim - 1)


def _make_ret_fwd(seq, tb):
    ncb = tb // CH
    nb = seq // tb

    def body(pr_ref, cc_ref, ss_ref, nw_ref, oc_ref, st_ref, r_scr):
        @pl.when(pl.program_id(0) == 0)
        def _():
            r_scr[...] = jnp.zeros_like(r_scr)

        d = _ret_batch(pr_ref, cc_ref, ss_ref, ncb)
        kd = d["k"] * d["kdec"]
        for c in range(ncb):
            bs = slice(c * RET_H, (c + 1) * RET_H)
            rs = r_scr[...]
            st_ref[c] = rs
            r_scr[...] = rs * d["cdec"] + _bdot(kd[bs], d["v"][bs], _TN)
        r_prev = st_ref[...].reshape(ncb * RET_H, 128, 128)
        o = _bdot(d["s"], d["v"], _NN) + _bdot(d["q"], r_prev, _NN) * d["qdec"]
        _, _, y = _rms_fwd(o, nw_ref[0:1, :], RET_D)
        out = y * _silu(d["z"])
        for c in range(ncb):
            for h in range(RET_H):
                oc_ref[c * CH:(c + 1) * CH, h * 128:(h + 1) * 128] = out[c * RET_H + h].astype(oc_ref.dtype)

    def call(pr, cc, ss, nw):
        return pl.pallas_call(
            body,
            grid=(nb,),
            in_specs=[
                pl.BlockSpec((tb, 2048), lambda i: (i, 0)),
                pl.BlockSpec((tb, 128), lambda i: (i, 0)),
                pl.BlockSpec((tb, 128), lambda i: (i, 0)),
                pl.BlockSpec((8, 128), lambda i: (0, 0)),
            ],
            out_specs=[
                pl.BlockSpec((tb, 512), lambda i: (i, 0)),
                pl.BlockSpec((ncb, RET_H, 128, 128), lambda i: (i, 0, 0, 0)),
            ],
            out_shape=[
                jax.ShapeDtypeStruct((seq, 512), BF16),
                jax.ShapeDtypeStruct((seq // CH, RET_H, 128, 128), F32),
            ],
            scratch_shapes=[pltpu.VMEM((RET_H, 128, 128), F32)],
            compiler_params=pltpu.CompilerParams(dimension_semantics=("arbitrary",), vmem_limit_bytes=VMEM_LIMIT),
            name="ret_fwd",
        )(pr, cc, ss, nw)

    return call


def _make_ret_bwd(seq, tb):
    ncb = tb // CH
    nb = seq // tb

    def body(pr_ref, cc_ref, ss_ref, nw_ref, st_ref, doc_ref, dpr_ref, dnw_ref, dr_scr):
        @pl.when(pl.program_id(0) == 0)
        def _():
            dr_scr[...] = jnp.zeros_like(dr_scr)
            dnw_ref[...] = jnp.zeros_like(dnw_ref)

        nw = nw_ref[0:1, :]
        scale = RET_D ** -0.5
        n = ncb * RET_H
        d = _ret_batch(pr_ref, cc_ref, ss_ref, ncb)
        q, k, v, z, s = d["q"], d["k"], d["v"], d["z"], d["s"]
        r_prev = st_ref[...].reshape(n, 128, 128)
        o = _bdot(s, v, _NN) + _bdot(q, r_prev, _NN) * d["qdec"]
        doc = jnp.stack([doc_ref[c * CH:(c + 1) * CH, h * 128:(h + 1) * 128]
                         for c in range(ncb) for h in range(RET_H)])
        on, r, y = _rms_fwd(o, nw, RET_D)
        dz = doc * y * _dsilu(z)
        do, dnw_rows = _rms_bwd(doc * _silu(z), on, r, nw, RET_D)
        dnw_acc = jnp.sum(jnp.sum(dnw_rows, axis=0), axis=0, keepdims=True)
        dqd = do * d["qdec"]
        qtd = _bdot(q, dqd, _TN)
        drn_l = [None] * ncb
        for c in reversed(range(ncb)):
            drn_l[c] = dr_scr[...]
            dr_scr[...] = qtd[c * RET_H:(c + 1) * RET_H] + d["cdec"] * drn_l[c]
        drn = jnp.concatenate(drn_l, axis=0)
        ds = _bdot(do, v, _NT) * d["dmat"]
        dq = _rot_bwd(_bdot(ds, k, _NN) + _bdot(dqd, r_prev, _NT), d["cc"], d["ss"])
        dk = _rot_bwd((_bdot(ds, q, _TN) + _bdot(v, drn, _NT) * d["kdec"]) * scale, d["cc"], d["ss"])
        dv = _bdot(s, do, _TN) + _bdot(k * d["kdec"], drn, _NN)
        for c in range(ncb):
            rows = slice(c * CH, (c + 1) * CH)
            for h in range(RET_H):
                b = c * RET_H + h
                for j, val in enumerate((dq, dk, dv, dz)):
                    dpr_ref[rows, j * 512 + h * 128:j * 512 + (h + 1) * 128] = val[b].astype(dpr_ref.dtype)
        dnw_ref[...] += jnp.where(_iota2((8, 128), 0) == 0, dnw_acc, 0.0)

    def call(pr, cc, ss, nw, st, doc):
        rev = lambda i: (nb - 1 - i, 0)
        return pl.pallas_call(
            body,
            grid=(nb,),
            in_specs=[
                pl.BlockSpec((tb, 2048), rev),
                pl.BlockSpec((tb, 128), rev),
                pl.BlockSpec((tb, 128), rev),
                pl.BlockSpec((8, 128), lambda i: (0, 0)),
                pl.BlockSpec((ncb, RET_H, 128, 128), lambda i: (nb - 1 - i, 0, 0, 0)),
                pl.BlockSpec((tb, 512), rev),
            ],
            out_specs=[
                pl.BlockSpec((tb, 2048), rev),
                pl.BlockSpec((8, 128), lambda i: (0, 0)),
            ],
            out_shape=[
                jax.ShapeDtypeStruct((seq, 2048), BF16),
                jax.ShapeDtypeStruct((8, 128), F32),
            ],
            scratch_shapes=[pltpu.VMEM((RET_H, 128, 128), F32)],
            compiler_params=pltpu.CompilerParams(dimension_semantics=("arbitrary",), vmem_limit_bytes=VMEM_LIMIT),
            name="ret_bwd",
        )(pr, cc, ss, nw, st, doc)

    return call


def _rope_tables(seq):
    half = RET_D // 2
    inv = ROPE_BASE ** (-jnp.arange(half, dtype=F32) / half)
    ang = jnp.arange(seq, dtype=jnp.int32).astype(F32)[:, None] * inv[None, :]
    cos, sin = jnp.cos(ang), jnp.sin(ang)
    return jnp.concatenate([cos, cos], axis=1), jnp.concatenate([-sin, sin], axis=1)


SEG_G, SEG_S, SEG_R, SEG_GS, SEG_SS = (0, 2048), (2048, 4608), (4608, 6656), (6656, 6784), (6784, 6912)
NP = 6912
SEGS = (SEG_G, SEG_S, SEG_R, SEG_GS, SEG_SS)


def _resident(shape):
    return pl.BlockSpec(shape, lambda i: (0,) * len(shape), pipeline_mode=pl.Buffered(1))


def _make_inproj(seq, tl):
    def body(x_ref, pn_ref, w_ref, pg_ref, ps_ref, pr_ref, gs_ref, ss_ref, ht_ref):
        x = x_ref[...]
        _, _, hn = _rms_fwd(x, pn_ref[0:1, :], D_MODEL)
        h = hn.astype(BF16)
        ht_ref[...] = hn.T.astype(BF16)
        for (a, b), o_ref in zip(SEGS, (pg_ref, ps_ref, pr_ref, gs_ref, ss_ref)):
            o_ref[...] = jnp.dot(h, w_ref[:, a:b], preferred_element_type=F32)

    def call(x, pn, w, comm=None, comm_args=()):
        row = lambda i: (i, 0)
        cx = _exchange_specs(comm)
        return pl.pallas_call(
            _with_exchange(body, comm, 3, 6, seq // tl),
            grid=(seq // tl,),
            in_specs=[pl.BlockSpec((tl, D_MODEL), row), _resident((8, D_MODEL)), _resident((D_MODEL, NP))]
            + cx["specs"],
            out_specs=[pl.BlockSpec((tl, b - a), row) for a, b in SEGS]
            + [pl.BlockSpec((D_MODEL, tl), lambda i: (0, i))] + cx["specs"],
            out_shape=[jax.ShapeDtypeStruct((seq, b - a), F32) for a, b in SEGS]
            + [jax.ShapeDtypeStruct((D_MODEL, seq), BF16)] + cx["out_shape"],
            scratch_shapes=cx["scratch"],
            compiler_params=pltpu.CompilerParams(dimension_semantics=("arbitrary",), vmem_limit_bytes=VMEM_LIMIT,
                                                 has_side_effects=comm is not None),
            name="inproj" + cx["tag"],
        )(x, pn, w, *comm_args)

    return call


def _make_outproj(seq, tl):
    def body(oa_ref, ob_ref, oc_ref, w_ref, x_ref, qn_ref, out_ref, xn_ref):
        out = (jnp.dot(oa_ref[...], w_ref[0:512, :], preferred_element_type=F32)
               + jnp.dot(ob_ref[...], w_ref[512:1536, :], preferred_element_type=F32)
               + jnp.dot(oc_ref[...], w_ref[1536:2048, :], preferred_element_type=F32))
        out_ref[...] = out
        _, _, y = _rms_fwd(out, qn_ref[0:1, :], D_MODEL)
        xn_ref[...] = x_ref[...] + y

    def call(oa, ob, oc, w, x, qn):
        row = lambda i: (i, 0)
        return pl.pallas_call(
            body,
            grid=(seq // tl,),
            in_specs=[pl.BlockSpec((tl, 512), row), pl.BlockSpec((tl, 1024), row), pl.BlockSpec((tl, 512), row),
                      _resident((2048, D_MODEL)), pl.BlockSpec((tl, D_MODEL), row), _resident((8, D_MODEL))],
            out_specs=[pl.BlockSpec((tl, D_MODEL), row), pl.BlockSpec((tl, D_MODEL), row)],
            out_shape=[jax.ShapeDtypeStruct((seq, D_MODEL), F32), jax.ShapeDtypeStruct((seq, D_MODEL), F32)],
            compiler_params=pltpu.CompilerParams(dimension_semantics=("arbitrary",), vmem_limit_bytes=VMEM_LIMIT),
            name="outproj",
        )(oa, ob, oc, w, x, qn)

    return call


def _make_outproj_loss(seq, tl):
    def body(oa_ref, ob_ref, oc_ref, w_ref, x_ref, qn_ref, t_ref, out_ref, dy_ref, loss_ref):
        @pl.when(pl.program_id(0) == 0)
        def _():
            loss_ref[...] = jnp.zeros_like(loss_ref)

        out = (jnp.dot(oa_ref[...], w_ref[0:512, :], preferred_element_type=F32)
               + jnp.dot(ob_ref[...], w_ref[512:1536, :], preferred_element_type=F32)
               + jnp.dot(oc_ref[...], w_ref[1536:2048, :], preferred_element_type=F32))
        out_ref[...] = out
        _, _, y = _rms_fwd(out, qn_ref[0:1, :], D_MODEL)
        err = (x_ref[...] + y) - t_ref[...]
        dy_ref[...] = err * (1.0 / D_MODEL)
        part = jnp.sum(jnp.sum(err * err, axis=1, keepdims=True), axis=0, keepdims=True) * (0.5 / D_MODEL)
        loss_ref[...] += jnp.where((_iota2((8, 128), 0) == 0) & (_iota2((8, 128), 1) == 0), part, 0.0)

    def call(oa, ob, oc, w, x, qn, t):
        row = lambda i: (i, 0)
        return pl.pallas_call(
            body,
            grid=(seq // tl,),
            in_specs=[pl.BlockSpec((tl, 512), row), pl.BlockSpec((tl, 1024), row), pl.BlockSpec((tl, 512), row),
                      _resident((2048, D_MODEL)), pl.BlockSpec((tl, D_MODEL), row), _resident((8, D_MODEL)),
                      pl.BlockSpec((tl, D_MODEL), row)],
            out_specs=[pl.BlockSpec((tl, D_MODEL), row), pl.BlockSpec((tl, D_MODEL), row),
                       pl.BlockSpec((8, 128), lambda i: (0, 0))],
            out_shape=[jax.ShapeDtypeStruct((seq, D_MODEL), F32), jax.ShapeDtypeStruct((seq, D_MODEL), F32),
                       jax.ShapeDtypeStruct((8, 128), F32)],
            compiler_params=pltpu.CompilerParams(dimension_semantics=("arbitrary",), vmem_limit_bytes=VMEM_LIMIT),
            name="outproj_loss",
        )(oa, ob, oc, w, x, qn, t)

    return call


def _make_outproj_bwd(seq, tl):
    def body(dxn_ref, out_ref, oa_ref, ob_ref, oc_ref, w_ref, qn_ref, doa_ref, dob_ref, doc_ref, dqn_ref, dw_ref):
        @pl.when(pl.program_id(0) == 0)
        def _():
            dqn_ref[...] = jnp.zeros_like(dqn_ref)
            dw_ref[...] = jnp.zeros_like(dw_ref)

        qn = qn_ref[0:1, :]
        on, r, _ = _rms_fwd(out_ref[...], qn, D_MODEL)
        dout, dqn_rows = _rms_bwd(dxn_ref[...], on, r, qn, D_MODEL)
        dqn_ref[...] += jnp.where(_iota2((8, D_MODEL), 0) == 0, jnp.sum(dqn_rows, axis=0, keepdims=True), 0.0)
        db = dout.astype(BF16)
        nt = (((1,), (1,)), ((), ()))
        tn = (((0,), (0,)), ((), ()))
        doa_ref[...] = lax.dot_general(db, w_ref[0:512, :], nt, preferred_element_type=F32).astype(BF16)
        dob_ref[...] = lax.dot_general(db, w_ref[512:1536, :], nt, preferred_element_type=F32).astype(BF16)
        doc_ref[...] = lax.dot_general(db, w_ref[1536:2048, :], nt, preferred_element_type=F32).astype(BF16)
        dw_ref[0:512, :] += lax.dot_general(oa_ref[...], db, tn, preferred_element_type=F32)
        dw_ref[512:1536, :] += lax.dot_general(ob_ref[...], db, tn, preferred_element_type=F32)
        dw_ref[1536:2048, :] += lax.dot_general(oc_ref[...], db, tn, preferred_element_type=F32)

    def call(dxn, out, oa, ob, oc, w, qn):
        row = lambda i: (i, 0)
        const = lambda i: (0, 0)
        return pl.pallas_call(
            body,
            grid=(seq // tl,),
            in_specs=[pl.BlockSpec((tl, D_MODEL), row), pl.BlockSpec((tl, D_MODEL), row),
                      pl.BlockSpec((tl, 512), row), pl.BlockSpec((tl, 1024), row), pl.BlockSpec((tl, 512), row),
                      _resident((2048, D_MODEL)), _resident((8, D_MODEL))],
            out_specs=[pl.BlockSpec((tl, 512), row), pl.BlockSpec((tl, 1024), row), pl.BlockSpec((tl, 512), row),
                       pl.BlockSpec((8, D_MODEL), const), pl.BlockSpec((2048, D_MODEL), const)],
            out_shape=[jax.ShapeDtypeStruct((seq, 512), BF16), jax.ShapeDtypeStruct((seq, 1024), BF16),
                       jax.ShapeDtypeStruct((seq, 512), BF16), jax.ShapeDtypeStruct((8, D_MODEL), F32),
                       jax.ShapeDtypeStruct((2048, D_MODEL), F32)],
            compiler_params=pltpu.CompilerParams(dimension_semantics=("arbitrary",), vmem_limit_bytes=VMEM_LIMIT),
            name="outproj_bwd",
        )(dxn, out, oa, ob, oc, w, qn)

    return call


def _make_inproj_bwd_dx(seq, tl):
    def body(dg_ref, ds_ref, dr_ref, dgs_ref, dss_ref, w_ref, x_ref, pn_ref, dxn_ref, dx_ref, dpn_ref):
        @pl.when(pl.program_id(0) == 0)
        def _():
            dpn_ref[...] = jnp.zeros_like(dpn_ref)

        nt = (((1,), (1,)), ((), ()))
        dh = jnp.zeros((tl, D_MODEL), F32)
        for (a, b), d_ref in zip(SEGS, (dg_ref, ds_ref, dr_ref, dgs_ref, dss_ref)):
            dh = dh + lax.dot_general(d_ref[...], w_ref[:, a:b], nt, preferred_element_type=F32)
        pn = pn_ref[0:1, :]
        on, r, _ = _rms_fwd(x_ref[...], pn, D_MODEL)
        dx, dpn_rows = _rms_bwd(dh, on, r, pn, D_MODEL)
        dx_ref[...] = dx + dxn_ref[...]
        dpn_ref[...] += jnp.where(_iota2((8, D_MODEL), 0) == 0, jnp.sum(dpn_rows, axis=0, keepdims=True), 0.0)

    def call(dg, ds, dr, dgs, dss, w, x, pn, dxn, comm=None, comm_args=()):
        row = lambda i: (i, 0)
        cx = _exchange_specs(comm)
        return pl.pallas_call(
            _with_exchange(body, comm, 9, 2, seq // tl),
            grid=(seq // tl,),
            in_specs=[pl.BlockSpec((tl, b - a), row) for a, b in SEGS]
            + [_resident((D_MODEL, NP)), pl.BlockSpec((tl, D_MODEL), row), _resident((8, D_MODEL)),
               pl.BlockSpec((tl, D_MODEL), row)] + cx["specs"],
            out_specs=[pl.BlockSpec((tl, D_MODEL), row), pl.BlockSpec((8, D_MODEL), lambda i: (0, 0))] + cx["specs"],
            out_shape=[jax.ShapeDtypeStruct((seq, D_MODEL), F32), jax.ShapeDtypeStruct((8, D_MODEL), F32)]
            + cx["out_shape"],
            scratch_shapes=cx["scratch"],
            compiler_params=pltpu.CompilerParams(dimension_semantics=("arbitrary",), vmem_limit_bytes=VMEM_LIMIT,
                                                 has_side_effects=comm is not None),
            name="inproj_bwd_dx" + cx["tag"],
        )(dg, ds, dr, dgs, dss, w, x, pn, dxn, *comm_args)

    return call


def _make_inproj_bwd_dw(seq, tl, width, tn, name):
    def body(ht_ref, d_ref, dw_ref):
        @pl.when(pl.program_id(1) == 0)
        def _():
            dw_ref[...] = jnp.zeros_like(dw_ref)

        dw_ref[...] += jnp.dot(ht_ref[...], d_ref[...], preferred_element_type=F32)

    def call(ht, d):
        return pl.pallas_call(
            body,
            grid=(width // tn, seq // tl),
            in_specs=[pl.BlockSpec((D_MODEL, tl), lambda j, i: (0, i)), pl.BlockSpec((tl, tn), lambda j, i: (i, j))],
            out_specs=pl.BlockSpec((D_MODEL, tn), lambda j, i: (0, j)),
            out_shape=jax.ShapeDtypeStruct((D_MODEL, width), F32),
            compiler_params=pltpu.CompilerParams(dimension_semantics=("arbitrary", "arbitrary"),
                                                 vmem_limit_bytes=VMEM_LIMIT),
            name=name,
        )(ht, d)

    return call


ADAM_LR, ADAM_B1, ADAM_B2, ADAM_EPS, ADAM_WD, ADAM_STEP = 0.001, 0.9, 0.999, 1e-08, 0.01, 10


def _adam_math(w, g, m, v):
    m = ADAM_B1 * m + (1.0 - ADAM_B1) * g
    v = ADAM_B2 * v + (1.0 - ADAM_B2) * (g * g)
    m_hat = m / (1.0 - ADAM_B1 ** ADAM_STEP)
    v_hat = v / (1.0 - ADAM_B2 ** ADAM_STEP)
    delta = -ADAM_LR * (m_hat / (jnp.sqrt(v_hat) + ADAM_EPS) + ADAM_WD * w)
    return delta, m, v


def _adamw(w, g, m, v, name):
    shape = w.shape
    cols = shape[-1]
    rows = w.size // cols
    tr = rows if rows <= 512 else 256
    assert rows % tr == 0

    def body(w_ref, g_ref, m_ref, v_ref, d_ref, mo_ref, vo_ref):
        d_ref[...], mo_ref[...], vo_ref[...] = _adam_math(w_ref[...], g_ref[...], m_ref[...], v_ref[...])

    spec = pl.BlockSpec((tr, cols), lambda i: (i, 0))
    outs = pl.pallas_call(
        body,
        grid=(rows // tr,),
        in_specs=[spec] * 4,
        out_specs=[spec] * 3,
        out_shape=[jax.ShapeDtypeStruct((rows, cols), F32)] * 3,
        compiler_params=pltpu.CompilerParams(dimension_semantics=("arbitrary",), vmem_limit_bytes=VMEM_LIMIT),
        name=name,
    )(*[a.reshape(rows, cols) for a in (w, g, m, v)])
    return (g,) + tuple(o.reshape(shape) for o in outs)


def _adamw_pairs(w, mine, theirs, m, v, name):
    na, r, cols = w.shape
    assert na == 2
    tr = 256
    assert r % tr == 0

    def body(w_ref, a0_ref, b0_ref, a1_ref, b1_ref, m_ref, v_ref, g_ref, d_ref, mo_ref, vo_ref):
        g = jnp.where(pl.program_id(0) == 0, a0_ref[...] + b0_ref[...], a1_ref[...] + b1_ref[...])
        g_ref[...] = g
        d_ref[...], mo_ref[...], vo_ref[...] = _adam_math(w_ref[...], g, m_ref[...], v_ref[...])

    nblk = r // tr
    full = pl.BlockSpec((None, tr, cols), lambda a, i: (a, i, 0))
    lay0 = pl.BlockSpec((None, tr, cols), lambda a, i: (0, i * (1 - a) + (nblk - 1) * a, 0))
    lay1 = pl.BlockSpec((None, tr, cols), lambda a, i: (0, i * a, 0))
    return pl.pallas_call(
        body,
        grid=(na, nblk),
        in_specs=[full, lay0, lay0, lay1, lay1, full, full],
        out_specs=[full] * 4,
        out_shape=[jax.ShapeDtypeStruct(w.shape, F32)] * 4,
        compiler_params=pltpu.CompilerParams(dimension_semantics=("arbitrary",) * 2, vmem_limit_bytes=VMEM_LIMIT),
        name=name,
    )(w, mine[0], theirs[0], mine[1], theirs[1], m, v)


MESH = pl.DeviceIdType.MESH
ANY = pl.BlockSpec(memory_space=pl.ANY)
CHIP_REL = ((1, 0), (0, 1), (1, 1))


def _flip(v, d):
    return 1 - v if d else v


def _ag_chips(arrs, name):
    n = len(arrs)

    def body(*refs):
        ins, outs = refs[:n], refs[n:2 * n]
        send_sems, recv_sems, loc_sems = refs[2 * n:]
        x, y, c = lax.axis_index("x"), lax.axis_index("y"), lax.axis_index("c")
        me = 2 * x + y

        def remote(a, k, slot):
            dx, dy = CHIP_REL[k]
            return pltpu.make_async_remote_copy(
                src_ref=ins[a], dst_ref=outs[a].at[slot], send_sem=send_sems.at[a * 3 + k],
                recv_sem=recv_sems.at[a * 3 + k], device_id=(_flip(x, dx), _flip(y, dy), c), device_id_type=MESH)

        local = [pltpu.make_async_copy(ins[a], outs[a].at[me], loc_sems.at[a]) for a in range(n)]
        for cp in local:
            cp.start()
        for a in range(n):
            for k in range(3):
                remote(a, k, me).start()
        for a in range(n):
            for k, (dx, dy) in enumerate(CHIP_REL):
                remote(a, k, 2 * _flip(x, dx) + _flip(y, dy)).wait_recv()
        for a in range(n):
            for k in range(3):
                remote(a, k, me).wait_send()
        for cp in local:
            cp.wait()

    return pl.pallas_call(
        body,
        in_specs=[ANY] * n,
        out_specs=[ANY] * n,
        out_shape=[jax.ShapeDtypeStruct((4,) + a.shape, a.dtype) for a in arrs],
        scratch_shapes=[pltpu.SemaphoreType.DMA((3 * n,)), pltpu.SemaphoreType.DMA((3 * n,)),
                        pltpu.SemaphoreType.DMA((n,))],
        compiler_params=pltpu.CompilerParams(has_side_effects=True),
        name=name,
    )(*arrs)


class _ChipExchange:
    def __init__(self, kind, arrs):
        self.kind, self.n = kind, len(arrs)
        if kind == "gather":
            self.out_shape = [jax.ShapeDtypeStruct((4,) + a.shape, a.dtype) for a in arrs]
        else:
            self.out_shape = [jax.ShapeDtypeStruct((3,) + a.shape[1:], a.dtype) for a in arrs]
        self.scratch = [pltpu.SemaphoreType.DMA((4 * self.n,)), pltpu.SemaphoreType.DMA((4 * self.n,))]

    def _copies(self, ins, outs, sems):
        send_sems, recv_sems = sems
        x, y, c = lax.axis_index("x"), lax.axis_index("y"), lax.axis_index("c")
        me = 2 * x + y
        pairs = []
        for a in range(self.n):
            for k, (dx, dy) in enumerate(CHIP_REL):
                px, py = _flip(x, dx), _flip(y, dy)
                sem = dict(send_sem=send_sems.at[4 * a + k], recv_sem=recv_sems.at[4 * a + k],
                           device_id=(px, py, c), device_id_type=MESH)
                if self.kind == "gather":
                    out = pltpu.make_async_remote_copy(src_ref=ins[a], dst_ref=outs[a].at[me], **sem)
                    inc = pltpu.make_async_remote_copy(src_ref=ins[a], dst_ref=outs[a].at[2 * px + py], **sem)
                else:
                    out = pltpu.make_async_remote_copy(src_ref=ins[a].at[2 * px + py], dst_ref=outs[a].at[k], **sem)
                    inc = out
                pairs.append((out, inc))
            if self.kind == "gather":
                own = pltpu.make_async_remote_copy(
                    src_ref=ins[a], dst_ref=outs[a].at[me], send_sem=send_sems.at[4 * a + 3],
                    recv_sem=recv_sems.at[4 * a + 3], device_id=(x, y, 1 - c), device_id_type=MESH)
                pairs.append((own, own))
        return pairs

    def start(self, ins, outs, sems):
        for out, _ in self._copies(ins, outs, sems):
            out.start()

    def finish(self, ins, outs, sems):
        pairs = self._copies(ins, outs, sems)
        for _, inc in pairs:
            inc.wait_recv()
        for out, _ in pairs:
            out.wait_send()


def _with_exchange(body, comm, n_in, n_out, nb):
    if comm is None:
        return body

    def wrapped(*refs):
        ins = refs[:n_in]
        c_in = refs[n_in:n_in + comm.n]
        outs = refs[n_in + comm.n:n_in + comm.n + n_out]
        c_out = refs[n_in + comm.n + n_out:n_in + 2 * comm.n + n_out]
        rest = refs[n_in + 2 * comm.n + n_out:]
        scratch, sems = rest[:len(rest) - 2], rest[len(rest) - 2:]

        @pl.when(pl.program_id(0) == 0)
        def _():
            comm.start(c_in, c_out, sems)

        body(*ins, *outs, *scratch)

        @pl.when(pl.program_id(0) == nb - 1)
        def _():
            comm.finish(c_in, c_out, sems)

    return wrapped


def _exchange_specs(comm):
    if comm is None:
        return dict(specs=[], out_shape=[], scratch=[], tag="")
    return dict(specs=[pl.BlockSpec(memory_space=pl.ANY)] * comm.n, out_shape=list(comm.out_shape),
                scratch=list(comm.scratch), tag="_" + comm.kind)


def _half(ref_or_shape, half):
    r = ref_or_shape[-2] // 2
    return pl.ds(half * r, r)


def _ag_rows(arrs, name):
    n = len(arrs)

    def body(*refs):
        ins, outs = refs[:n], refs[n:2 * n]
        send_sems, recv_sems, fsend_sems, frecv_sems, loc_sems = refs[2 * n:]
        x, y, c = lax.axis_index("x"), lax.axis_index("y"), lax.axis_index("c")
        me = 2 * x + y
        sib = (x, y, 1 - c)

        def chip_of(k):
            dx, dy = CHIP_REL[k]
            return _flip(x, dx), _flip(y, dy)

        def ici(a, k, slot):
            px, py = chip_of(k)
            rows = _half(arrs[a].shape, c)
            return pltpu.make_async_remote_copy(
                src_ref=ins[a].at[:, rows, :], dst_ref=outs[a].at[slot, :, rows, :], send_sem=send_sems.at[a * 3 + k],
                recv_sem=recv_sems.at[a * 3 + k], device_id=(px, py, c), device_id_type=MESH)

        def fwd(a, k, half):
            px, py = chip_of(k)
            blk = outs[a].at[2 * px + py, :, _half(arrs[a].shape, half), :]
            return pltpu.make_async_remote_copy(
                src_ref=blk, dst_ref=blk, send_sem=fsend_sems.at[a * 3 + k], recv_sem=frecv_sems.at[a * 3 + k],
                device_id=sib, device_id_type=MESH)

        own = [pltpu.make_async_remote_copy(src_ref=ins[a], dst_ref=outs[a].at[me], send_sem=loc_sems.at[a],
                                            recv_sem=loc_sems.at[n + a], device_id=sib, device_id_type=MESH)
               for a in range(n)]
        for cp in own:
            cp.start()
        for a in range(n):
            for k in range(3):
                ici(a, k, me).start()
        for a in range(n):
            for k in range(3):
                px, py = chip_of(k)
                ici(a, k, 2 * px + py).wait_recv()
                fwd(a, k, c).start()
        for a in range(n):
            for k in range(3):
                fwd(a, k, 1 - c).wait_recv()
        for a in range(n):
            for k in range(3):
                ici(a, k, me).wait_send()
                fwd(a, k, c).wait_send()
        for cp in own:
            cp.wait()

    return pl.pallas_call(
        body,
        in_specs=[ANY] * n,
        out_specs=[ANY] * n,
        out_shape=[jax.ShapeDtypeStruct((4,) + a.shape, a.dtype) for a in arrs],
        scratch_shapes=[pltpu.SemaphoreType.DMA((3 * n,)) for _ in range(4)] + [pltpu.SemaphoreType.DMA((2 * n,))],
        compiler_params=pltpu.CompilerParams(has_side_effects=True),
        name=name,
    )(*arrs)


def _sum_chips(own, recv, chip, name):
    _, na, r, cols = own.shape
    tr = 256
    assert r % tr == 0

    def body(chip_ref, o_ref, r_ref, s_ref):
        s_ref[...] = ((o_ref[...] + r_ref[0].astype(F32)) + r_ref[1].astype(F32)) + r_ref[2].astype(F32)

    return pl.pallas_call(
        body,
        grid_spec=pltpu.PrefetchScalarGridSpec(
            num_scalar_prefetch=1,
            grid=(na, r // tr),
            in_specs=[pl.BlockSpec((None, None, tr, cols), lambda a, i, ch: (ch[0], a, i, 0)),
                      pl.BlockSpec((3, None, tr, cols), lambda a, i, ch: (0, a, i, 0))],
            out_specs=pl.BlockSpec((None, tr, cols), lambda a, i, ch: (a, i, 0))),
        out_shape=jax.ShapeDtypeStruct((na, r, cols), F32),
        compiler_params=pltpu.CompilerParams(dimension_semantics=("arbitrary",) * 2, vmem_limit_bytes=VMEM_LIMIT),
        name=name,
    )(chip, own, recv)


def _swap_sibling(arrs, name):
    n = len(arrs)

    def body(*refs):
        ins, outs = refs[:n], refs[n:2 * n]
        send_sems, recv_sems = refs[2 * n:]
        x, y, c = lax.axis_index("x"), lax.axis_index("y"), lax.axis_index("c")
        cps = [pltpu.make_async_remote_copy(src_ref=ins[a], dst_ref=outs[a], send_sem=send_sems.at[a],
                                            recv_sem=recv_sems.at[a], device_id=(x, y, 1 - c), device_id_type=MESH)
               for a in range(n)]
        for cp in cps:
            cp.start()
        for cp in cps:
            cp.wait_recv()
        for cp in cps:
            cp.wait_send()

    return pl.pallas_call(
        body,
        in_specs=[ANY] * n,
        out_specs=[ANY] * n,
        out_shape=[jax.ShapeDtypeStruct(a.shape, a.dtype) for a in arrs],
        scratch_shapes=[pltpu.SemaphoreType.DMA((n,)), pltpu.SemaphoreType.DMA((n,))],
        compiler_params=pltpu.CompilerParams(has_side_effects=True),
        name=name,
    )(*arrs)


def _allreduce_small(vec, name):
    rows = vec.shape[0]

    def body(v_ref, out_ref, gat_ref, send_sems, recv_sems):
        x, y, c = lax.axis_index("x"), lax.axis_index("y"), lax.axis_index("c")
        me = 4 * x + 2 * y + c

        def remote(k, slot):
            dx, dy, dc = (k >> 2) & 1, (k >> 1) & 1, k & 1
            return pltpu.make_async_remote_copy(
                src_ref=v_ref, dst_ref=gat_ref.at[slot], send_sem=send_sems.at[k - 1], recv_sem=recv_sems.at[k - 1],
                device_id=(_flip(x, dx), _flip(y, dy), _flip(c, dc)), device_id_type=MESH)

        gat_ref[me] = v_ref[...]
        for k in range(1, 8):
            remote(k, me).start()
        for k in range(1, 8):
            dx, dy, dc = (k >> 2) & 1, (k >> 1) & 1, k & 1
            remote(k, 4 * _flip(x, dx) + 2 * _flip(y, dy) + _flip(c, dc)).wait_recv()
        for k in range(1, 8):
            remote(k, me).wait_send()
        acc = gat_ref[0]
        for j in range(1, 8):
            acc = acc + gat_ref[j]
        out_ref[...] = acc

    vm = pl.BlockSpec(memory_space=pltpu.VMEM)
    return pl.pallas_call(
        body,
        in_specs=[vm],
        out_specs=vm,
        out_shape=jax.ShapeDtypeStruct(vec.shape, F32),
        scratch_shapes=[pltpu.VMEM((8, rows, 128), F32), pltpu.SemaphoreType.DMA((7,)), pltpu.SemaphoreType.DMA((7,))],
        compiler_params=pltpu.CompilerParams(has_side_effects=True),
        name=name,
    )(vec)


def _pad8(v, width, lane0=0):
    v = v.reshape(1, -1) if v.ndim == 1 else v
    return jnp.zeros((8, width), F32).at[:v.shape[0], lane0:lane0 + v.shape[1]].set(v.astype(F32))


def _relayout_w_in(g):
    tr = 128
    q = N_IN // 4

    def body(g_ref, o_ref):
        w = jnp.concatenate([g_ref[j] for j in range(4)], axis=1)
        z = lambda n: jnp.zeros((tr, n), w.dtype)
        o_ref[...] = jnp.concatenate([w[:, 0:2048], w[:, 2056:4616], w[:, 4632:6680],
                                      w[:, 2048:2056], z(120), w[:, 4616:4632], z(112)], axis=1)

    return pl.pallas_call(
        body,
        grid=(D_MODEL // tr,),
        in_specs=[pl.BlockSpec((4, tr, q), lambda i: (0, i, 0))],
        out_specs=pl.BlockSpec((tr, NP), lambda i: (i, 0)),
        out_shape=jax.ShapeDtypeStruct((D_MODEL, NP), g.dtype),
        compiler_params=pltpu.CompilerParams(dimension_semantics=("arbitrary",), vmem_limit_bytes=VMEM_LIMIT),
        name="relayout_w_in",
    )(g)


def _unlayout_dw_in(dg, ds, dr, dsm):
    tr = 128
    q = N_IN // 4

    def body(g_ref, s_ref, r_ref, sm_ref, o_ref, ob_ref):
        w = jnp.concatenate([g_ref[...], sm_ref[:, 0:8], s_ref[...], sm_ref[:, 128:144], r_ref[...]], axis=1)
        for j in range(4):
            blk = w[:, q * j:q * (j + 1)]
            o_ref[j] = blk
            ob_ref[j] = blk.astype(BF16)

    row = lambda i: (i, 0)
    return pl.pallas_call(
        body,
        grid=(D_MODEL // tr,),
        in_specs=[pl.BlockSpec((tr, d.shape[1]), row) for d in (dg, ds, dr, dsm)],
        out_specs=[pl.BlockSpec((4, tr, q), lambda i: (0, i, 0))] * 2,
        out_shape=[jax.ShapeDtypeStruct((4, D_MODEL, q), F32), jax.ShapeDtypeStruct((4, D_MODEL, q), BF16)],
        compiler_params=pltpu.CompilerParams(dimension_semantics=("arbitrary",), vmem_limit_bytes=VMEM_LIMIT),
        name="unlayout_dw_in",
    )(dg, ds, dr, dsm)


TB = 256
TL = 256
TL_IN = 512
TL_OB = 1024
TK = 2048


def kernel(x, pre_norm, post_norm, w_in, gdn_conv, gdn_A_log, gdn_dt_bias, gdn_norm, ssd_conv, ssd_conv_b, ssd_A_log, ssd_dt_bias, ssd_D, ssd_norm, ret_norm, w_out, loss_target, m_pre_norm, m_post_norm, m_w_in, m_gdn_conv, m_gdn_A_log, m_gdn_dt_bias, m_gdn_norm, m_ssd_conv, m_ssd_conv_b, m_ssd_A_log, m_ssd_dt_bias, m_ssd_D, m_ssd_norm, m_ret_norm, m_w_out, v_pre_norm, v_post_norm, v_w_in, v_gdn_conv, v_gdn_A_log, v_gdn_dt_bias, v_gdn_norm, v_ssd_conv, v_ssd_conv_b, v_ssd_A_log, v_ssd_dt_bias, v_ssd_D, v_ssd_norm, v_ret_norm, v_w_out):
    seq = x.shape[1]
    chip = 2 * lax.axis_index("x") + lax.axis_index("y")
    x0 = x[0]

    wi_b, wo_b = w_in.astype(BF16), w_out.astype(BF16)
    (wi0_g,) = _ag_rows([wi_b[0:1]], "ag_weights")
    gcv_g, scv_g = _ag_chips([gdn_conv, ssd_conv], "ag_conv")
    full_w_in = _relayout_w_in
    wp = [full_w_in(wi0_g[:, 0]), None]
    wo = [None, None]
    ag0 = _ChipExchange("gather", [wo_b[0]])
    ag1 = _ChipExchange("gather", [wi_b[1], wo_b[1]])
    gcv = jnp.transpose(gcv_g, (1, 2, 0, 3)).reshape(DEPTH, CONV_W, 1536)
    scv = jnp.transpose(scv_g, (1, 2, 0, 3)).reshape(DEPTH, CONV_W, 1536)
    rope_c, rope_s = _rope_tables(seq)

    saved = []
    xc = x0
    for l in range(DEPTH):
        p = dict(
            pn=_pad8(pre_norm[l], D_MODEL), qn=_pad8(post_norm[l], D_MODEL),
            g_cw=_pad8(gcv[l], 1536), g_prm=_pad8(jnp.stack([gdn_A_log[l], gdn_dt_bias[l]]), 128, 4),
            g_nw=_pad8(gdn_norm[l], 128),
            s_cw=_pad8(scv[l], 1536), s_cb=_pad8(ssd_conv_b[l], 1536),
            s_prm=_pad8(jnp.stack([ssd_A_log[l], ssd_dt_bias[l], ssd_D[l]]), 128), s_nw=_pad8(ssd_norm[l], SSD_W),
            r_nw=_pad8(ret_norm[l], 128))
        if l == 0:
            pg, ps, pr, gs, ss, ht, wo0_g = _make_inproj(seq, TL_IN)(xc, p["pn"], wp[l], comm=ag0, comm_args=(wo_b[0],))
            wo[0] = wo0_g.reshape(2048, D_MODEL)
        else:
            pg, ps, pr, gs, ss, ht = _make_inproj(seq, TL_IN)(xc, p["pn"], wp[l])
        if l == 0:
            oa, stg, tig, uwg, gpre, wi1_g, wo1_g = _make_gdn_fwd(seq, TB)(
                pg, gs, p["g_cw"], p["g_prm"], p["g_nw"], comm=ag1, comm_args=(wi_b[1], wo_b[1]))
            wp[1], wo[1] = full_w_in(wi1_g), wo1_g.reshape(2048, D_MODEL)
        else:
            oa, stg, tig, uwg, gpre = _make_gdn_fwd(seq, TB)(pg, gs, p["g_cw"], p["g_prm"], p["g_nw"])
        ob, sts, spre, sy = _make_ssd_fwd(seq, TB)(ps, ss, p["s_cw"], p["s_cb"], p["s_prm"], p["s_nw"])
        oc, str_ = _make_ret_fwd(seq, TB)(pr, rope_c, rope_s, p["r_nw"])
        if l == DEPTH - 1:
            out, dxn, lossp = _make_outproj_loss(seq, TL)(oa, ob, oc, wo[l], xc, p["qn"], loss_target[0])
            xn = None
        else:
            out, xn = _make_outproj(seq, TL)(oa, ob, oc, wo[l], xc, p["qn"])
        saved.append(dict(p=p, x=xc, ht=ht, spre=spre, sy=sy, gpre=gpre, pg=pg, ps=ps, pr=pr, gs=gs, ss=ss, stg=stg, tig=tig, uwg=uwg, sts=sts, str=str_,
                          oa=oa, ob=ob, oc=oc, out=out))
        xc = xn

    small = [None] * DEPTH
    gin, gin_b, gout, q_in, q_out = ([None] * DEPTH for _ in range(5))

    for l in reversed(range(DEPTH)):
        s = saved[l]
        p = s["p"]
        doa, dob, doc, dqn, dwo_l = _make_outproj_bwd(seq, TL_OB)(dxn, s["out"], s["oa"], s["ob"], s["oc"], wo[l], p["qn"])
        gout[l] = dwo_l.reshape(4, 512, D_MODEL)
        gdn_args = (s["pg"], s["gpre"], s["gs"], p["g_cw"], p["g_prm"], p["g_nw"], s["stg"], s["tig"], s["uwg"], doa)
        if l == 0:
            payload = (gout[0].astype(BF16),)
            dpg, dgs, dcw_g, dprm_g, dnw_g, q_out[0] = _make_gdn_bwd(seq, TB)(
                *gdn_args, comm=_ChipExchange("scatter", payload), comm_args=payload)
        else:
            dpg, dgs, dcw_g, dprm_g, dnw_g = _make_gdn_bwd(seq, TB)(*gdn_args)
        ssd_args = (s["ps"], s["spre"], s["sy"], s["ss"], p["s_cw"], p["s_cb"], p["s_prm"], p["s_nw"], s["sts"], dob)
        if l == 0:
            payload = (gin_b[1], gout[1].astype(BF16))
            dps, dss, dcw_s, dcb_s, dprm_s, dnw_s, q_in[1], q_out[1] = _make_ssd_bwd(seq, TB)(
                *ssd_args, comm=_ChipExchange("scatter", payload), comm_args=payload)
        else:
            dps, dss, dcw_s, dcb_s, dprm_s, dnw_s = _make_ssd_bwd(seq, TB)(*ssd_args)
        dpr, dnw_r = _make_ret_bwd(seq, TB)(s["pr"], rope_c, rope_s, p["r_nw"], s["str"], doc)
        dws = [_make_inproj_bwd_dw(seq, TK, d.shape[1], tn, f"inproj_bwd_dw{i}")(s["ht"], d)
               for i, (d, tn) in enumerate(((dpg, 2048), (dps, 1280), (dpr, 2048),
                                            (jnp.concatenate([dgs, dss], axis=1), 256)))]
        gin[l], gin_b[l] = _unlayout_dw_in(*dws)
        dx_args = (dpg, dps, dpr, dgs, dss, wp[l], s["x"], p["pn"], dxn)
        if l == 0:
            payload = (gin_b[0],)
            dx, dpn, q_in[0] = _make_inproj_bwd_dx(seq, TL_IN)(
                *dx_args, comm=_ChipExchange("scatter", payload), comm_args=payload)
        else:
            dx, dpn = _make_inproj_bwd_dx(seq, TL_IN)(*dx_args)
        small[l] = [dpn[0], dqn[0], dcw_g[0:4].reshape(-1), dprm_g[0, 4:8], dprm_g[1, 4:8], dnw_g[0],
                    dcw_s[0:4].reshape(-1), dcb_s[0], dprm_s[0, 0:16], dprm_s[1, 0:16], dprm_s[2, 0:16],
                    dnw_s[0], dnw_r[0]]
        dxn = dx
    grad_x = dxn[None]

    sizes = [a.shape[0] for a in small[0]]
    flat = jnp.concatenate(small[0] + small[1] + [lossp[0, 0:1]])
    n_flat = flat.shape[0]
    rows = -(-n_flat // 1024) * 8
    red = _allreduce_small(jnp.pad(flat, (0, rows * 128 - n_flat)).reshape(rows, 128), "allreduce_small").reshape(-1)
    per = sum(sizes)
    loss = red[2 * per]

    def pick(i):
        off = sum(sizes[:i])
        return jnp.stack([red[l * per + off:l * per + off + sizes[i]] for l in range(DEPTH)])

    g_small = dict(
        pre_norm=pick(0), post_norm=pick(1),
        gdn_conv=lax.dynamic_slice_in_dim(pick(2).reshape(DEPTH, CONV_W, 1536), chip * 384, 384, axis=2),
        gdn_A_log=pick(3), gdn_dt_bias=pick(4), gdn_norm=pick(5),
        ssd_conv=lax.dynamic_slice_in_dim(pick(6).reshape(DEPTH, CONV_W, 1536), chip * 384, 384, axis=2),
        ssd_conv_b=pick(7), ssd_A_log=pick(8), ssd_dt_bias=pick(9), ssd_D=pick(10), ssd_norm=pick(11),
        ret_norm=pick(12))

    chip1 = chip.astype(jnp.int32).reshape(1)
    s_in = [_sum_chips(gin[l][:, None], q_in[l][:, None], chip1, f"sum_chips_w_in{l}") for l in range(DEPTH)]
    s_out = [_sum_chips(gout[l][:, None], q_out[l][:, None], chip1, f"sum_chips_w_out{l}") for l in range(DEPTH)]
    t_all = _swap_sibling(s_in + s_out, "swap_grads")
    t_in, t_out = t_all[:DEPTH], t_all[DEPTH:]

    weights = dict(pre_norm=pre_norm, post_norm=post_norm, w_in=w_in, gdn_conv=gdn_conv, gdn_A_log=gdn_A_log,
                   gdn_dt_bias=gdn_dt_bias, gdn_norm=gdn_norm, ssd_conv=ssd_conv, ssd_conv_b=ssd_conv_b,
                   ssd_A_log=ssd_A_log, ssd_dt_bias=ssd_dt_bias, ssd_D=ssd_D, ssd_norm=ssd_norm, ret_norm=ret_norm,
                   w_out=w_out)
    ms = dict(pre_norm=m_pre_norm, post_norm=m_post_norm, w_in=m_w_in, gdn_conv=m_gdn_conv, gdn_A_log=m_gdn_A_log,
              gdn_dt_bias=m_gdn_dt_bias, gdn_norm=m_gdn_norm, ssd_conv=m_ssd_conv, ssd_conv_b=m_ssd_conv_b,
              ssd_A_log=m_ssd_A_log, ssd_dt_bias=m_ssd_dt_bias, ssd_D=m_ssd_D, ssd_norm=m_ssd_norm,
              ret_norm=m_ret_norm, w_out=m_w_out)
    vs = dict(pre_norm=v_pre_norm, post_norm=v_post_norm, w_in=v_w_in, gdn_conv=v_gdn_conv, gdn_A_log=v_gdn_A_log,
              gdn_dt_bias=v_gdn_dt_bias, gdn_norm=v_gdn_norm, ssd_conv=v_ssd_conv, ssd_conv_b=v_ssd_conv_b,
              ssd_A_log=v_ssd_A_log, ssd_dt_bias=v_ssd_dt_bias, ssd_D=v_ssd_D, ssd_norm=v_ssd_norm,
              ret_norm=v_ret_norm, w_out=v_w_out)
    names = list(weights)
    res = {}
    for nme in names:
        if nme == "w_in":
            res[nme] = _adamw_pairs(w_in, s_in, t_in, m_w_in, v_w_in, "adamw_w_in")
        elif nme == "w_out":
            res[nme] = _adamw_pairs(w_out, s_out, t_out, m_w_out, v_w_out, "adamw_w_out")
        else:
            res[nme] = _adamw(weights[nme], g_small[nme], ms[nme], vs[nme], "adamw_" + nme)
    return (loss, grad_x, *[res[n][0] for n in names], *[res[n][1] for n in names],
            *[res[n][2] for n in names], *[res[n][3] for n in names])
```

```python
import functools
import math

import jax
import jax.numpy as jnp
from jax import lax
from jax.experimental import pallas as pl
from jax.experimental.pallas import tpu as pltpu

F32 = jnp.float32
BF16 = jnp.bfloat16
HI = lax.Precision.HIGHEST

D_MODEL = 1024
DEPTH = 2
CH = 64
CONV_W = 4
EPS = 1e-6
GDN_H, GDN_D = 4, 128
SSD_H, SSD_P, SSD_N, SSD_G = 16, 64, 128, 2
SSD_W = SSD_H * SSD_P
RET_H, RET_D = 4, 128
ROPE_BASE = 10000.0
N_IN = 6680
NEG = -1e30

VMEM_LIMIT = 56 * 1024 * 1024


def _dot(a, b):
    return jnp.dot(a.astype(BF16), b.astype(BF16), preferred_element_type=F32)


def _dot_nt(a, b):
    return lax.dot_general(a.astype(BF16), b.astype(BF16), (((1,), (1,)), ((), ())), preferred_element_type=F32)


def _dot_tn(a, b):
    return lax.dot_general(a.astype(BF16), b.astype(BF16), (((0,), (0,)), ((), ())), preferred_element_type=F32)


def _split(a):
    hi = a.astype(BF16)
    return hi, (a - hi.astype(F32)).astype(BF16)


def _dot01l(m, v):
    vh, vl = _split(v)
    mb = m.astype(BF16)
    return jnp.dot(mb, vh, preferred_element_type=F32) + jnp.dot(mb, vl, preferred_element_type=F32)


def _dot01r(v, m):
    vh, vl = _split(v)
    mb = m.astype(BF16)
    return jnp.dot(vh, mb, preferred_element_type=F32) + jnp.dot(vl, mb, preferred_element_type=F32)


def _sigmoid(x):
    return jax.nn.sigmoid(x)


def _silu(x):
    return x * _sigmoid(x)


def _dsilu(x):
    s = _sigmoid(x)
    return s * (1.0 + x * (1.0 - s))


def _softplus(x):
    return jnp.maximum(x, 0.0) + jnp.log1p(jnp.exp(-jnp.abs(x)))


def _iota2(shape, dim):
    return lax.broadcasted_iota(jnp.int32, shape, dim)


def _chunk_tri(tb, upper=False):
    r = _iota2((tb, tb), 0)
    c = _iota2((tb, tb), 1)
    same = jnp.right_shift(r, 6) == jnp.right_shift(c, 6)
    return (same & ((c >= r) if upper else (c <= r))).astype(F32)


def _masks():
    r = _iota2((CH, CH), 0)
    c = _iota2((CH, CH), 1)
    return r >= c, r > c, (r == c).astype(F32)


def _put_lane(col, lane_idx, width=128):
    lane = _iota2((col.shape[0], width), 1)
    return jnp.where(lane == lane_idx, col, 0.0)


def _conv_taps(raw, halo8, tb):
    ext = jnp.concatenate([halo8, raw], axis=0)
    return [raw] + [pltpu.roll(ext, s, axis=0)[8:] for s in (1, 2, 3)]


def _conv_back(dpre, nxt8, tb):
    ext = jnp.concatenate([dpre, nxt8], axis=0)
    return [dpre] + [pltpu.roll(ext, tb + 8 - s, axis=0)[:tb] for s in (1, 2, 3)]


def _rms_fwd(o, w, n):
    r = lax.rsqrt(jnp.sum(o * o, axis=-1, keepdims=True) * (1.0 / n) + EPS)
    on = o * r
    return on, r, on * w


def _rms_bwd(dy, on, r, w, n):
    don = dy * w
    return r * (don - on * (jnp.sum(don * on, axis=-1, keepdims=True) * (1.0 / n))), dy * on


def _put_cols(v, g, gw):
    z = jnp.zeros_like(v)
    return jnp.concatenate([v, z] if g == 0 else [z, v], axis=1)


def _gdn_common(pg_ref, halo8, sm, cw, prm, tb, pre=None):
    raw = pg_ref[:, 0:1536]
    if pre is None:
        taps = _conv_taps(raw, halo8, tb)
        pre = taps[0] * cw[3:4, :] + taps[1] * cw[2:3, :] + taps[2] * cw[1:2, :] + taps[3] * cw[0:1, :]
    act = _silu(pre)
    beta = _sigmoid(sm)
    sp_in = sm + prm[1:2, :]
    g = -jnp.exp(prm[0:1, :]) * _softplus(sp_in)
    gc = _dot01l(_chunk_tri(tb), g)
    return raw, pre, act, beta, sp_in, g, gc


_NN = (((2,), (1,)), ((0,), (0,)))
_NT = (((2,), (2,)), ((0,), (0,)))
_TN = (((1,), (1,)), ((0,), (0,)))


def _bdot(a, b, dn):
    return lax.dot_general(a.astype(BF16), b.astype(BF16), dn, preferred_element_type=F32)


def _dot3_parts(ah, al, bh, bl, dn):
    f = lambda p, q: lax.dot_general(p, q, dn, preferred_element_type=F32)
    return f(ah, bh) + (f(ah, bl) + f(al, bh))


def _bdot3(a, b, dn):
    ah, al = _split(a)
    bh, bl = _split(b)
    return _dot3_parts(ah, al, bh, bl, dn)


def _binv_unit_lower(a, eye):
    r = _iota2((CH, CH), 0)
    c = _iota2((CH, CH), 1)
    d = eye - jnp.where((jnp.right_shift(r, 1) == jnp.right_shift(c, 1)), a, 0.0)
    ah, al = _split(a)
    zero = jnp.zeros((), BF16)
    for lb in range(1, 6):
        same = jnp.right_shift(r, lb + 1) == jnp.right_shift(c, lb + 1)
        low = (jnp.bitwise_and(jnp.right_shift(r, lb), 1) == 1) & (jnp.bitwise_and(jnp.right_shift(c, lb), 1) == 0)
        oh, ol = jnp.where(same & low, ah, zero), jnp.where(same & low, al, zero)
        dh, dl = _split(d)
        t = _dot3_parts(oh, ol, dh, dl, _NN)
        th, tl = _split(t)
        d = d - _dot3_parts(dh, dl, th, tl, _NN)
    return d


def _rsum(v):
    return jnp.sum(v, axis=-1, keepdims=True)


def _gdn_batch(act, beta, gc, gct, eg_all, ncb, masks):
    causal, strict, _ = masks

    def st(fn):
        return jnp.stack([fn(c, h, slice(c * CH, (c + 1) * CH)) for c in range(ncb) for h in range(GDN_H)])

    qr = st(lambda c, h, r: act[r, h * 128:(h + 1) * 128])
    kr = st(lambda c, h, r: act[r, 512 + h * 128:512 + (h + 1) * 128])
    vh = st(lambda c, h, r: act[r, 1024 + h * 128:1024 + (h + 1) * 128])
    bh = st(lambda c, h, r: beta[r, h:h + 1])
    gcol = st(lambda c, h, r: gc[r, 4 + h:5 + h])
    grow = st(lambda c, h, r: gct[4 + h:5 + h, r])
    eg = st(lambda c, h, r: eg_all[r, 4 + h:5 + h])
    glast = st(lambda c, h, r: gc[(c + 1) * CH - 1:(c + 1) * CH, 4 + h:5 + h])
    rq = lax.rsqrt(_rsum(qr * qr) + EPS)
    rk = lax.rsqrt(_rsum(kr * kr) + EPS)
    qn = qr * rq
    kh = kr * rk
    qh = qn * (GDN_D ** -0.5)
    decay = jnp.exp(jnp.where(causal, gcol - grow, NEG))
    kb = kh * bh
    kd_scale = jnp.exp(glast - gcol)
    return dict(qn=qn, rq=rq, kh=kh, rk=rk, qh=qh, vh=vh, bh=bh, eg=eg, decay=decay, kb=kb, vb=vh * bh, kg=kb * eg,
                qg=qh * eg, kd_scale=kd_scale, kdec=kh * kd_scale, egl=jnp.exp(glast),
                a=jnp.where(strict, _bdot(kb, kh, _NT) * decay, 0.0), attn=_bdot(qh, kh, _NT) * decay)


def _make_gdn_fwd(seq, tb):
    ncb = tb // CH
    nb = seq // tb
    n = ncb * GDN_H

    def body(pg_ref, sm_ref, cw_ref, prm_ref, nw_ref, oa_ref, st_ref, ti_ref, uw_ref, pre_ref, s_scr, halo_scr):
        @pl.when(pl.program_id(0) == 0)
        def _():
            s_scr[...] = jnp.zeros_like(s_scr)
            halo_scr[...] = jnp.zeros_like(halo_scr)

        masks = _masks()
        sm = sm_ref[...]
        raw, pre, act, beta, _, _, gc = _gdn_common(pg_ref, halo_scr[...], sm, cw_ref[...], prm_ref[...], tb)
        halo_scr[...] = raw[tb - 8:tb, :]
        pre_ref[...] = pre
        d = _gdn_batch(act, beta, gc, gc.T, jnp.exp(gc), ncb, masks)
        t = _binv_unit_lower(d["a"], masks[2])
        sol = _bdot3(t, jnp.concatenate([d["vb"], d["kg"]], axis=2), _NN)
        ti_ref[...] = t.reshape(ncb, GDN_H, CH, CH)
        uw_ref[...] = sol.reshape(ncb, GDN_H, CH, 256)
        u, w = sol[:, :, :128], sol[:, :, 128:]
        vns = []
        for c in range(ncb):
            bs = slice(c * GDN_H, (c + 1) * GDN_H)
            s = s_scr[...]
            st_ref[c] = s
            vn = u[bs] - _bdot(w[bs], s, _NN)
            s_scr[...] = s * d["egl"][bs] + _bdot(d["kdec"][bs], vn, _TN)
            vns.append(vn)
        v_new = jnp.concatenate(vns, axis=0)
        s_prev = st_ref[...].reshape(n, 128, 128)
        o = _bdot(d["qg"], s_prev, _NN) + _bdot(d["attn"], v_new, _NN)
        _, _, y = _rms_fwd(o, nw_ref[0:1, :], GDN_D)
        for c in range(ncb):
            rows = slice(c * CH, (c + 1) * CH)
            for h in range(GDN_H):
                z = pg_ref[rows, 1536 + h * 128:1536 + (h + 1) * 128]
                oa_ref[rows, h * 128:(h + 1) * 128] = (y[c * GDN_H + h] * _silu(z)).astype(oa_ref.dtype)

    def call(pg, sm, cw, prm, nw, comm=None, comm_args=()):
        blk4 = lambda i: (i, 0, 0, 0)
        cx = _exchange_specs(comm)
        return pl.pallas_call(
            _with_exchange(body, comm, 5, 5, nb),
            grid=(nb,),
            in_specs=[
                pl.BlockSpec((tb, 2048), lambda i: (i, 0)),
                pl.BlockSpec((tb, 128), lambda i: (i, 0)),
                pl.BlockSpec((8, 1536), lambda i: (0, 0)),
                pl.BlockSpec((8, 128), lambda i: (0, 0)),
                pl.BlockSpec((8, 128), lambda i: (0, 0)),
            ] + cx["specs"],
            out_specs=[
                pl.BlockSpec((tb, 512), lambda i: (i, 0)),
                pl.BlockSpec((ncb, GDN_H, 128, 128), blk4),
                pl.BlockSpec((ncb, GDN_H, CH, CH), blk4),
                pl.BlockSpec((ncb, GDN_H, CH, 256), blk4),
                pl.BlockSpec((tb, 1536), lambda i: (i, 0)),
            ] + cx["specs"],
            out_shape=[
                jax.ShapeDtypeStruct((seq, 512), BF16),
                jax.ShapeDtypeStruct((seq // CH, GDN_H, 128, 128), F32),
                jax.ShapeDtypeStruct((seq // CH, GDN_H, CH, CH), F32),
                jax.ShapeDtypeStruct((seq // CH, GDN_H, CH, 256), F32),
                jax.ShapeDtypeStruct((seq, 1536), F32),
            ] + cx["out_shape"],
            scratch_shapes=[pltpu.VMEM((GDN_H, 128, 128), F32), pltpu.VMEM((8, 1536), F32)] + cx["scratch"],
            compiler_params=pltpu.CompilerParams(dimension_semantics=("arbitrary",), vmem_limit_bytes=VMEM_LIMIT,
                                                 has_side_effects=comm is not None),
            name="gdn_fwd" + cx["tag"],
        )(pg, sm, cw, prm, nw, *comm_args)

    return call


def _make_gdn_bwd(seq, tb):
    ncb = tb // CH
    nb = seq // tb
    hb = tb // 8
    n = ncb * GDN_H

    def body(pg_ref, pre_ref, sm_ref, cw_ref, prm_ref, nw_ref, st_ref, ti_ref, uw_ref, doa_ref,
             dpg_ref, dsm_ref, dcw_ref, dprm_ref, dnw_ref, ds_scr, nxt_scr):
        i = pl.program_id(0)

        @pl.when(i == 0)
        def _():
            ds_scr[...] = jnp.zeros_like(ds_scr)
            nxt_scr[...] = jnp.zeros_like(nxt_scr)
            dcw_ref[...] = jnp.zeros_like(dcw_ref)
            dprm_ref[...] = jnp.zeros_like(dprm_ref)
            dnw_ref[...] = jnp.zeros_like(dnw_ref)

        masks = _masks()
        strict = masks[1]
        sm = sm_ref[...]
        cw = cw_ref[...]
        prm = prm_ref[...]
        raw, pre, act, beta, sp_in, g, gc = _gdn_common(pg_ref, None, sm, cw, prm, tb, pre=pre_ref[...])
        nw = nw_ref[0:1, :]
        row_id = _iota2((CH, 1), 0)
        d = _gdn_batch(act, beta, gc, gc.T, jnp.exp(gc), ncb, masks)
        t = ti_ref[...].reshape(n, CH, CH)
        sol = uw_ref[...].reshape(n, CH, 256)
        u, w = sol[:, :, :128], sol[:, :, 128:]
        s_prev = st_ref[...].reshape(n, 128, 128)
        v_new = u - _bdot(w, s_prev, _NN)
        o = _bdot(d["qg"], s_prev, _NN) + _bdot(d["attn"], v_new, _NN)

        pairs = [(c, h) for c in range(ncb) for h in range(GDN_H)]
        z = jnp.stack([pg_ref[c * CH:(c + 1) * CH, 1536 + h * 128:1536 + (h + 1) * 128] for c, h in pairs])
        doa = jnp.stack([doa_ref[c * CH:(c + 1) * CH, h * 128:(h + 1) * 128] for c, h in pairs])
        on, r, y = _rms_fwd(o, nw, GDN_D)
        dz = doa * y * _dsilu(z)
        do, dnw_rows = _rms_bwd(doa * _silu(z), on, r, nw, GDN_D)
        dnw_acc = jnp.sum(jnp.sum(dnw_rows, axis=0), axis=0, keepdims=True)

        dvn_in = _bdot(d["attn"], do, _TN)
        qgtdo = _bdot(d["qg"], do, _TN)
        dvn_l, dkdec_l, dgl_l = [None] * ncb, [None] * ncb, [None] * ncb
        for c in reversed(range(ncb)):
            bs = slice(c * GDN_H, (c + 1) * GDN_H)
            dsn = ds_scr[...]
            dvn_c = dvn_in[bs] + _bdot(d["kdec"][bs], dsn, _NN)
            ds_scr[...] = d["egl"][bs] * dsn + qgtdo[bs] - _bdot(w[bs], dvn_c, _TN)
            dvn_l[c] = dvn_c
            dkdec_l[c] = _bdot(v_new[bs], dsn, _NT)
            dgl_l[c] = d["egl"][bs] * jnp.sum(_rsum(s_prev[bs] * dsn), axis=1, keepdims=True)
        dvn = jnp.concatenate(dvn_l, axis=0)
        dkdec = jnp.concatenate(dkdec_l, axis=0)
        dglast = jnp.concatenate(dgl_l, axis=0)

        dqg = _bdot(do, s_prev, _NT)
        dattn = _bdot(do, v_new, _NT)
        dw = -_bdot(dvn, s_prev, _NT)
        drhs = _bdot3(t, jnp.concatenate([dvn, dw], axis=2), _TN)
        dvb, dkg = drhs[:, :, :128], drhs[:, :, 128:]
        da = jnp.where(strict, -(_bdot(dvb, u, _NT) + _bdot(dkg, w, _NT)), 0.0)
        dp = da * d["decay"]
        dq_m = dattn * d["decay"]
        m = da * d["a"] + dattn * d["attn"]
        upper_tri = jnp.broadcast_to((_iota2((CH, CH), 1) >= _iota2((CH, CH), 0)).astype(BF16), (n, CH, CH))
        dg_in = _rsum(jnp.where(strict, _bdot(upper_tri, m, _NN), 0.0))
        dkb = _bdot(dp, d["kh"], _NN) + dkg * d["eg"]
        kdk_row = _rsum(dkdec * d["kdec"])
        dk = _bdot(dp, d["kb"], _TN) + _bdot(dq_m, d["qh"], _TN) + dkdec * d["kd_scale"] + dkb * d["bh"]
        dq = _bdot(dq_m, d["kh"], _NN) + dqg * d["eg"]
        dglast = dglast + jnp.sum(kdk_row, axis=1, keepdims=True)
        dgcol = (_rsum(dqg * d["qg"]) + _rsum(dkg * d["kg"]) - kdk_row + jnp.where(row_id == CH - 1, dglast, 0.0))
        dbeta = _rsum(dkb * d["kh"]) + _rsum(dvb * d["vh"])
        dn = dq * (GDN_D ** -0.5)
        dact_q = d["rq"] * (dn - d["qn"] * _rsum(dn * d["qn"]))
        dact_k = d["rk"] * (dk - d["kh"] * _rsum(dk * d["kh"]))
        dact_v = dvb * d["bh"]

        def lanes(v, lane0):
            return jnp.concatenate(
                [sum(_put_lane(v[c * GDN_H + h], lane0 + h) for h in range(GDN_H)) for c in range(ncb)], axis=0)

        def tokens(v):
            return jnp.concatenate(
                [jnp.concatenate([v[c * GDN_H + h] for h in range(GDN_H)], axis=1) for c in range(ncb)], axis=0)

        dbeta_all = lanes(dbeta, 0)
        dg = _dot01l(_chunk_tri(tb, upper=True), lanes(dgcol, 4)) + lanes(dg_in, 4)
        neg_ea = -jnp.exp(prm[0:1, :])
        da_raw = dg * neg_ea * _sigmoid(sp_in)
        db_raw = dbeta_all * beta * (1.0 - beta)
        dsm_ref[...] = (da_raw + db_raw).astype(dsm_ref.dtype)
        lane8 = _iota2((8, 128), 1)
        sub8 = _iota2((8, 128), 0)
        dalog = jnp.sum(dg * g, axis=0, keepdims=True)
        ddtb = jnp.sum(da_raw, axis=0, keepdims=True)
        dprm_ref[...] += jnp.where(sub8 == 0, dalog, 0.0) + jnp.where(sub8 == 1, ddtb, 0.0)
        dnw_ref[...] += jnp.where(sub8 == 0, dnw_acc, 0.0)

        dact = jnp.concatenate([tokens(dact_q), tokens(dact_k), tokens(dact_v)], axis=1)
        dpre = dact * _dsilu(pre)
        back = _conv_back(dpre, nxt_scr[...], tb)
        nxt_scr[...] = dpre[0:8, :]
        draw = back[0] * cw[3:4, :] + back[1] * cw[2:3, :] + back[2] * cw[1:2, :] + back[3] * cw[0:1, :]
        dpg_ref[:, 0:1536] = draw.astype(dpg_ref.dtype)
        dpg_ref[:, 1536:2048] = tokens(dz).astype(dpg_ref.dtype)
        sub_c = _iota2((8, 1536), 0)
        dcw_new = jnp.zeros((8, 1536), F32)
        for s_ in range(CONV_W):
            dcw_new = dcw_new + jnp.where(sub_c == 3 - s_, jnp.sum(back[s_] * raw, axis=0, keepdims=True), 0.0)
        dcw_ref[...] += dcw_new

    def call(pg, pre, sm, cw, prm, nw, st, ti, uw, doa, comm=None, comm_args=()):
        rev = lambda i: (nb - 1 - i, 0)
        const = lambda i: (0, 0)
        cx = _exchange_specs(comm)
        return pl.pallas_call(
            _with_exchange(body, comm, 10, 5, nb),
            grid=(nb,),
            in_specs=[
                pl.BlockSpec((tb, 2048), rev),
                pl.BlockSpec((tb, 1536), rev),
                pl.BlockSpec((tb, 128), rev),
                pl.BlockSpec((8, 1536), const),
                pl.BlockSpec((8, 128), const),
                pl.BlockSpec((8, 128), const),
                pl.BlockSpec((ncb, GDN_H, 128, 128), lambda i: (nb - 1 - i, 0, 0, 0)),
                pl.BlockSpec((ncb, GDN_H, CH, CH), lambda i: (nb - 1 - i, 0, 0, 0)),
                pl.BlockSpec((ncb, GDN_H, CH, 256), lambda i: (nb - 1 - i, 0, 0, 0)),
                pl.BlockSpec((tb, 512), rev),
            ] + cx["specs"],
            out_specs=[
                pl.BlockSpec((tb, 2048), rev),
                pl.BlockSpec((tb, 128), rev),
                pl.BlockSpec((8, 1536), const),
                pl.BlockSpec((8, 128), const),
                pl.BlockSpec((8, 128), const),
            ] + cx["specs"],
            out_shape=[
                jax.ShapeDtypeStruct((seq, 2048), BF16),
                jax.ShapeDtypeStruct((seq, 128), BF16),
                jax.ShapeDtypeStruct((8, 1536), F32),
                jax.ShapeDtypeStruct((8, 128), F32),
                jax.ShapeDtypeStruct((8, 128), F32),
            ] + cx["out_shape"],
            scratch_shapes=[pltpu.VMEM((GDN_H, 128, 128), F32), pltpu.VMEM((8, 1536), F32)] + cx["scratch"],
            compiler_params=pltpu.CompilerParams(dimension_semantics=("arbitrary",), vmem_limit_bytes=VMEM_LIMIT,
                                                 has_side_effects=comm is not None),
            name="gdn_bwd" + cx["tag"],
        )(pg, pre, sm, cw, prm, nw, st, ti, uw, doa, *comm_args)

    return call


def _expand_mat():
    r = _iota2((128, SSD_W), 0)
    c = _iota2((128, SSD_W), 1)
    return (jnp.right_shift(c, 6) == r).astype(F32)


def _reduce_heads(v, e):
    vh, vl = _split(v)
    eb = e.astype(BF16)
    nt = (((1,), (1,)), ((), ()))
    return (lax.dot_general(vh, eb, nt, preferred_element_type=F32)
            + lax.dot_general(vl, eb, nt, preferred_element_type=F32))


def _row8(v):
    return jnp.broadcast_to(v, (8, v.shape[1]))


def _ssd_common(ps_ref, halo8, ss, cw, cb, prm, tb, pre=None):
    raw = ps_ref[:, 0:1536]
    taps = None
    if pre is None:
        taps = _conv_taps(raw, halo8, tb)
        pre = taps[0] * cw[3:4, :] + taps[1] * cw[2:3, :] + taps[2] * cw[1:2, :] + taps[3] * cw[0:1, :] + cb[0:1, :]
    act = _silu(pre)
    dt_in = ss + prm[1:2, :]
    dt = _softplus(dt_in)
    a = dt * (-jnp.exp(prm[0:1, :]))
    acum = _dot01l(_chunk_tri(tb), a)
    e = _expand_mat()
    dt_e = _dot01r(dt, e)
    xdt = act[:, 0:SSD_W] * dt_e
    ea_e = _dot01r(jnp.exp(acum), e)
    d_e = _dot01r(_row8(prm[2:3, :]), e)[0:1, :]
    return raw, taps, pre, act, dt_in, dt, a, acum, e, dt_e, xdt, ea_e, d_e


def _ssd_chunk(act, acum, act_t, e, c):
    r0 = c * CH
    rows = slice(r0, r0 + CH)
    alast = acum[r0 + CH - 1:r0 + CH, :]
    wdec = jnp.exp(alast - acum[rows, :])
    wd_e = _dot01r(wdec, e)
    eal_e = _dot01r(_row8(jnp.exp(alast)), e)[0:1, :]
    return rows, wd_e, eal_e


def _ssd_lmat(acum, act_t, c, h, causal):
    r0 = c * CH
    acol = acum[r0:r0 + CH, h:h + 1]
    arow = act_t[h:h + 1, r0:r0 + CH]
    return jnp.exp(jnp.where(causal, acol - arow, NEG))


def _make_ssd_fwd(seq, tb):
    ncb = tb // CH
    nb = seq // tb
    hg = SSD_H // SSD_G
    gw = SSD_W // SSD_G

    def body(ps_ref, ss_ref, cw_ref, cb_ref, prm_ref, nw_ref, ob_ref, st_ref, pre_ref, y_ref, hs_scr, halo_scr):
        @pl.when(pl.program_id(0) == 0)
        def _():
            hs_scr[...] = jnp.zeros_like(hs_scr)
            halo_scr[...] = jnp.zeros_like(halo_scr)

        causal, _, _ = _masks()
        (raw, _, pre, act, _, _, _, acum, e, _, xdt, ea_e, d_e) = _ssd_common(
            ps_ref, halo_scr[...], ss_ref[...], cw_ref[...], cb_ref[...], prm_ref[...], tb)
        halo_scr[...] = raw[tb - 8:tb, :]
        pre_ref[...] = pre
        act_t = acum.T
        nw = nw_ref[0:1, :]
        for c in range(ncb):
            rows, wd_e, eal_e = _ssd_chunk(act, acum, act_t, e, c)
            st_ref[c] = hs_scr[...]
            ys = []
            for g in range(SSD_G):
                gc_ = slice(g * gw, (g + 1) * gw)
                bg = act[rows, SSD_W + g * 128:SSD_W + (g + 1) * 128]
                cg = act[rows, SSD_W + 256 + g * 128:SSD_W + 256 + (g + 1) * 128]
                cbm = _dot_nt(cg, bg)
                hs = hs_scr[:, gc_]
                yin = _dot(cg, hs)
                yh = []
                for hh in range(hg):
                    h = g * hg + hh
                    lm = _ssd_lmat(acum, act_t, c, h, causal)
                    yh.append(_dot(cbm * lm, xdt[rows, h * SSD_P:(h + 1) * SSD_P]))
                ys.append(jnp.concatenate(yh, axis=1) + yin * ea_e[rows, gc_])
                hs_scr[:, gc_] = hs * eal_e[:, gc_] + _dot_tn(bg, xdt[rows, gc_] * wd_e[:, gc_])
            y = jnp.concatenate(ys, axis=1) + act[rows, 0:SSD_W] * d_e
            y_ref[rows, :] = y
            yz = y * _silu(ps_ref[rows, 1536:2560])
            outs = [_rms_fwd(yz[:, g * gw:(g + 1) * gw], nw[:, g * gw:(g + 1) * gw], gw)[2] for g in range(SSD_G)]
            ob_ref[rows, :] = jnp.concatenate(outs, axis=1).astype(ob_ref.dtype)

    def call(ps, ss, cw, cb, prm, nw):
        const = lambda i: (0, 0)
        return pl.pallas_call(
            body,
            grid=(nb,),
            in_specs=[
                pl.BlockSpec((tb, 2560), lambda i: (i, 0)),
                pl.BlockSpec((tb, 128), lambda i: (i, 0)),
                pl.BlockSpec((8, 1536), const),
                pl.BlockSpec((8, 1536), const),
                pl.BlockSpec((8, 128), const),
                pl.BlockSpec((8, SSD_W), const),
            ],
            out_specs=[
                pl.BlockSpec((tb, SSD_W), lambda i: (i, 0)),
                pl.BlockSpec((ncb, SSD_N, SSD_W), lambda i: (i, 0, 0)),
                pl.BlockSpec((tb, 1536), lambda i: (i, 0)),
                pl.BlockSpec((tb, SSD_W), lambda i: (i, 0)),
            ],
            out_shape=[
                jax.ShapeDtypeStruct((seq, SSD_W), BF16),
                jax.ShapeDtypeStruct((seq // CH, SSD_N, SSD_W), F32),
                jax.ShapeDtypeStruct((seq, 1536), F32),
                jax.ShapeDtypeStruct((seq, SSD_W), F32),
            ],
            scratch_shapes=[pltpu.VMEM((SSD_N, SSD_W), F32), pltpu.VMEM((8, 1536), F32)],
            compiler_params=pltpu.CompilerParams(dimension_semantics=("arbitrary",), vmem_limit_bytes=VMEM_LIMIT),
            name="ssd_fwd",
        )(ps, ss, cw, cb, prm, nw)

    return call


def _make_ssd_bwd(seq, tb):
    ncb = tb // CH
    nb = seq // tb
    hb = tb // 8
    hg = SSD_H // SSD_G
    gw = SSD_W // SSD_G

    def body(ps_ref, pre_ref, y_ref, ss_ref, cw_ref, cb_ref, prm_ref, nw_ref, st_ref, dob_ref,
             dps_ref, dss_ref, dcw_ref, dcb_ref, dprm_ref, dnw_ref, dhs_scr, nxt_scr):
        i = pl.program_id(0)

        @pl.when(i == 0)
        def _():
            dhs_scr[...] = jnp.zeros_like(dhs_scr)
            nxt_scr[...] = jnp.zeros_like(nxt_scr)
            dcw_ref[...] = jnp.zeros_like(dcw_ref)
            dcb_ref[...] = jnp.zeros_like(dcb_ref)
            dprm_ref[...] = jnp.zeros_like(dprm_ref)
            dnw_ref[...] = jnp.zeros_like(dnw_ref)

        causal, _, _ = _masks()
        cw = cw_ref[...]
        prm = prm_ref[...]
        (raw, _, pre, act, dt_in, dt, a, acum, e, dt_e, xdt, ea_e, d_e) = _ssd_common(
            ps_ref, None, ss_ref[...], cw, cb_ref[...], prm, tb, pre=pre_ref[...])
        act_t = acum.T
        nw = nw_ref[0:1, :]
        row_id = _iota2((CH, 1), 0)

        dx_l, db_l, dc_l, dz_l, dacum_l, ddt_l, da_in_l = ([None] * ncb for _ in range(7))
        upper_tri = (_iota2((CH, CH), 1) >= _iota2((CH, CH), 0)).astype(F32)
        below = jnp.bitwise_and(_iota2((CH, gw), 1), CH - 1) < _iota2((CH, gw), 0)
        dnw_acc = jnp.zeros((1, SSD_W), F32)
        dd_acc = jnp.zeros((1, SSD_W), F32)

        for c in reversed(range(ncb)):
            rows, wd_e, eal_e = _ssd_chunk(act, acum, act_t, e, c)
            xc = act[rows, 0:SSD_W]
            z = ps_ref[rows, 1536:2560]
            dob = dob_ref[rows, :]
            sz = _silu(z)
            dy_g, dz_g, zacc_g, dxdt_g, dal_g, db_g, dc_g, da_in_g = [], [], [], [], [], [], [], []
            for g in range(SSD_G):
                gc_ = slice(g * gw, (g + 1) * gw)
                bg = act[rows, SSD_W + g * 128:SSD_W + (g + 1) * 128]
                cg = act[rows, SSD_W + 256 + g * 128:SSD_W + 256 + (g + 1) * 128]
                cbm = _dot_nt(cg, bg)
                hs = st_ref[c, :, gc_]
                yin = _dot(cg, hs)
                lmats = [_ssd_lmat(acum, act_t, c, g * hg + hh, causal) for hh in range(hg)]
                ea_g = ea_e[rows, gc_]
                y = y_ref[rows, gc_]
                yz = y * sz[:, gc_]
                on, r, _ = _rms_fwd(yz, nw[:, gc_], gw)
                dyz, dnw_rows = _rms_bwd(dob[:, gc_], on, r, nw[:, gc_], gw)
                dnw_acc = dnw_acc + _put_cols(jnp.sum(dnw_rows, axis=0, keepdims=True), g, gw)
                dy = dyz * sz[:, gc_]
                dz_g.append(dyz * y * _dsilu(z[:, gc_]))
                dd_acc = dd_acc + _put_cols(jnp.sum(dy * xc[:, gc_], axis=0, keepdims=True), g, gw)
                dhs_n = dhs_scr[:, gc_]
                dyin = dy * ea_g
                dcg = _dot_nt(dyin, hs)
                xw = xdt[rows, gc_] * wd_e[:, gc_]
                dbg = _dot_nt(xw, dhs_n)
                dxw = _dot(bg, dhs_n)
                dhs_scr[:, gc_] = dhs_n * eal_e[:, gc_] + _dot_tn(cg, dyin)
                dal_g.append(jnp.sum(hs * dhs_n, axis=0, keepdims=True) * eal_e[:, gc_]
                             + jnp.sum(dxw * xw, axis=0, keepdims=True))
                dxi, ms, dcbm = [], [], jnp.zeros((CH, CH), F32)
                for hh in range(hg):
                    h = g * hg + hh
                    hc = slice(hh * SSD_P, (hh + 1) * SSD_P)
                    dyh = dy[:, hc]
                    lm = cbm * lmats[hh]
                    dxi.append(_dot_tn(lm, dyh))
                    dlm = _dot_nt(dyh, xdt[rows, h * SSD_P:(h + 1) * SSD_P])
                    ms.append(dlm * lm)
                    dcbm = dcbm + dlm * lmats[hh]
                dx_intra = jnp.concatenate(dxi, axis=1)
                ncat = _dot(upper_tri, jnp.concatenate(ms, axis=1))
                da_in_g.append(jnp.where(below, ncat, 0.0))
                zacc_g.append(dy * yin * ea_g - dxw * xw)
                dxdt_g.append(dx_intra + dxw * wd_e[:, gc_])
                dy_g.append(dy)
                db_g.append(dbg + _dot_tn(dcbm, cg))
                dc_g.append(dcg + _dot(dcbm, bg))
            dy = jnp.concatenate(dy_g, axis=1)
            dxdt = jnp.concatenate(dxdt_g, axis=1)
            dx_l[c] = dxdt * dt_e[rows, :] + dy * d_e
            db_l[c] = jnp.concatenate(db_g, axis=1)
            dc_l[c] = jnp.concatenate(dc_g, axis=1)
            dz_l[c] = jnp.concatenate(dz_g, axis=1)
            ddt_l[c] = _reduce_heads(dxdt * xc, e)
            dalast = _reduce_heads(_row8(jnp.concatenate(dal_g, axis=1)), e)[0:1, :]
            dacum_l[c] = _reduce_heads(jnp.concatenate(zacc_g, axis=1), e) + jnp.where(row_id == CH - 1, dalast, 0.0)
            da_in_l[c] = _reduce_heads(jnp.concatenate(da_in_g, axis=1), e)

        dacum_all = jnp.concatenate(dacum_l, axis=0)
        da = _dot01l(_chunk_tri(tb, upper=True), dacum_all) + jnp.concatenate(da_in_l, axis=0)
        neg_ea = -jnp.exp(prm[0:1, :])
        ddt = jnp.concatenate(ddt_l, axis=0) + da * neg_ea
        ddt_in = ddt * _sigmoid(dt_in)
        dss_ref[...] = ddt_in.astype(dss_ref.dtype)
        sub8 = _iota2((8, 128), 0)
        dalog = jnp.sum(da * a, axis=0, keepdims=True)
        ddtb = jnp.sum(ddt_in, axis=0, keepdims=True)
        dd = _reduce_heads(_row8(dd_acc), e)[0:1, :]
        dprm_ref[...] += (jnp.where(sub8 == 0, dalog, 0.0) + jnp.where(sub8 == 1, ddtb, 0.0)
                          + jnp.where(sub8 == 2, dd, 0.0))
        dnw_ref[...] += jnp.where(_iota2((8, SSD_W), 0) == 0, dnw_acc, 0.0)

        dact = jnp.concatenate([jnp.concatenate(dx_l, axis=0), jnp.concatenate(db_l, axis=0),
                                jnp.concatenate(dc_l, axis=0)], axis=1)
        dpre = dact * _dsilu(pre)
        back = _conv_back(dpre, nxt_scr[...], tb)
        nxt_scr[...] = dpre[0:8, :]
        draw = back[0] * cw[3:4, :] + back[1] * cw[2:3, :] + back[2] * cw[1:2, :] + back[3] * cw[0:1, :]
        dps_ref[:, 0:1536] = draw.astype(dps_ref.dtype)
        dps_ref[:, 1536:2560] = jnp.concatenate(dz_l, axis=0).astype(dps_ref.dtype)
        sub_c = _iota2((8, 1536), 0)
        dcw_new = jnp.zeros((8, 1536), F32)
        for s_ in range(CONV_W):
            dcw_new = dcw_new + jnp.where(sub_c == 3 - s_, jnp.sum(back[s_] * raw, axis=0, keepdims=True), 0.0)
        dcw_ref[...] += dcw_new
        dcb_ref[...] += jnp.where(sub_c == 0, jnp.sum(dpre, axis=0, keepdims=True), 0.0)

    def call(ps, pre, y, ss, cw, cb, prm, nw, st, dob, comm=None, comm_args=()):
        rev = lambda i: (nb - 1 - i, 0)
        const = lambda i: (0, 0)
        cx = _exchange_specs(comm)
        return pl.pallas_call(
            _with_exchange(body, comm, 10, 6, nb),
            grid=(nb,),
            in_specs=[
                pl.BlockSpec((tb, 2560), rev),
                pl.BlockSpec((tb, 1536), rev),
                pl.BlockSpec((tb, SSD_W), rev),
                pl.BlockSpec((tb, 128), rev),
                pl.BlockSpec((8, 1536), const),
                pl.BlockSpec((8, 1536), const),
                pl.BlockSpec((8, 128), const),
                pl.BlockSpec((8, SSD_W), const),
                pl.BlockSpec((ncb, SSD_N, SSD_W), lambda i: (nb - 1 - i, 0, 0)),
                pl.BlockSpec((tb, SSD_W), rev),
            ] + cx["specs"],
            out_specs=[
                pl.BlockSpec((tb, 2560), rev),
                pl.BlockSpec((tb, 128), rev),
                pl.BlockSpec((8, 1536), const),
                pl.BlockSpec((8, 1536), const),
                pl.BlockSpec((8, 128), const),
                pl.BlockSpec((8, SSD_W), const),
            ] + cx["specs"],
            out_shape=[
                jax.ShapeDtypeStruct((seq, 2560), BF16),
                jax.ShapeDtypeStruct((seq, 128), BF16),
                jax.ShapeDtypeStruct((8, 1536), F32),
                jax.ShapeDtypeStruct((8, 1536), F32),
                jax.ShapeDtypeStruct((8, 128), F32),
                jax.ShapeDtypeStruct((8, SSD_W), F32),
            ] + cx["out_shape"],
            scratch_shapes=[pltpu.VMEM((SSD_N, SSD_W), F32), pltpu.VMEM((8, 1536), F32)] + cx["scratch"],
            compiler_params=pltpu.CompilerParams(dimension_semantics=("arbitrary",), vmem_limit_bytes=VMEM_LIMIT,
                                                 has_side_effects=comm is not None),
            name="ssd_bwd" + cx["tag"],
        )(ps, pre, y, ss, cw, cb, prm, nw, st, dob, *comm_args)

    return call


def _ret_consts(h):
    lg = math.log(1.0 - 2.0 ** (-5.0 - h))
    r = _iota2((CH, CH), 0)
    c = _iota2((CH, CH), 1)
    rel = (r - c).astype(F32)
    dmat = jnp.where(r >= c, jnp.exp(jnp.maximum(rel, 0.0) * lg), 0.0)
    idx = _iota2((CH, 1), 0).astype(F32)
    qdec = jnp.exp((idx + 1.0) * lg)
    kdec = jnp.exp((CH - 1.0 - idx) * lg)
    cdec = math.exp(CH * lg)
    return dmat, qdec, kdec, cdec


def _ret_batch(pr_ref, cc_ref, ss_ref, ncb):
    pairs = [(c, h) for c in range(ncb) for h in range(RET_H)]

    def st(off):
        return jnp.stack([pr_ref[c * CH:(c + 1) * CH, off + h * 128:off + (h + 1) * 128] for c, h in pairs])

    cc = jnp.stack([cc_ref[c * CH:(c + 1) * CH, :] for c, _ in pairs])
    ss = jnp.stack([ss_ref[c * CH:(c + 1) * CH, :] for c, _ in pairs])
    consts = [_ret_consts(h) for h in range(RET_H)]
    dmat = jnp.stack([consts[h][0] for _, h in pairs])
    qdec = jnp.stack([consts[h][1] for _, h in pairs])
    kdec = jnp.stack([consts[h][2] for _, h in pairs])
    cdec = jnp.stack([jnp.full((1, 1), consts[h][3], F32) for h in range(RET_H)])
    q = _rot(st(0), cc, ss)
    k = _rot(st(512), cc, ss) * (RET_D ** -0.5)
    return dict(q=q, k=k, v=st(1024), z=st(1536), cc=cc, ss=ss, dmat=dmat, qdec=qdec, kdec=kdec, cdec=cdec,
                s=_bdot(q, k, _NT) * dmat)


def _rot(t, cc, ss):
    return t * cc + pltpu.roll(t, 64, axis=t.ndim - 1) * ss


def _rot_bwd(d, cc, ss):
    return d * cc + pltpu.roll(d * ss, 64, axis=d.ndim - 1)


def _make_ret_fwd(seq, tb):
    ncb = tb // CH
    nb = seq // tb

    def body(pr_ref, cc_ref, ss_ref, nw_ref, oc_ref, st_ref, r_scr):
        @pl.when(pl.program_id(0) == 0)
        def _():
            r_scr[...] = jnp.zeros_like(r_scr)

        d = _ret_batch(pr_ref, cc_ref, ss_ref, ncb)
        kd = d["k"] * d["kdec"]
        for c in range(ncb):
            bs = slice(c * RET_H, (c + 1) * RET_H)
            rs = r_scr[...]
            st_ref[c] = rs
            r_scr[...] = rs * d["cdec"] + _bdot(kd[bs], d["v"][bs], _TN)
        r_prev = st_ref[...].reshape(ncb * RET_H, 128, 128)
        o = _bdot(d["s"], d["v"], _NN) + _bdot(d["q"], r_prev, _NN) * d["qdec"]
        _, _, y = _rms_fwd(o, nw_ref[0:1, :], RET_D)
        out = y * _silu(d["z"])
        for c in range(ncb):
            for h in range(RET_H):
                oc_ref[c * CH:(c + 1) * CH, h * 128:(h + 1) * 128] = out[c * RET_H + h].astype(oc_ref.dtype)

    def call(pr, cc, ss, nw):
        return pl.pallas_call(
            body,
            grid=(nb,),
            in_specs=[
                pl.BlockSpec((tb, 2048), lambda i: (i, 0)),
                pl.BlockSpec((tb, 128), lambda i: (i, 0)),
                pl.BlockSpec((tb, 128), lambda i: (i, 0)),
                pl.BlockSpec((8, 128), lambda i: (0, 0)),
            ],
            out_specs=[
                pl.BlockSpec((tb, 512), lambda i: (i, 0)),
                pl.BlockSpec((ncb, RET_H, 128, 128), lambda i: (i, 0, 0, 0)),
            ],
            out_shape=[
                jax.ShapeDtypeStruct((seq, 512), BF16),
                jax.ShapeDtypeStruct((seq // CH, RET_H, 128, 128), F32),
            ],
            scratch_shapes=[pltpu.VMEM((RET_H, 128, 128), F32)],
            compiler_params=pltpu.CompilerParams(dimension_semantics=("arbitrary",), vmem_limit_bytes=VMEM_LIMIT),
            name="ret_fwd",
        )(pr, cc, ss, nw)

    return call


def _make_ret_bwd(seq, tb):
    ncb = tb // CH
    nb = seq // tb

    def body(pr_ref, cc_ref, ss_ref, nw_ref, st_ref, doc_ref, dpr_ref, dnw_ref, dr_scr):
        @pl.when(pl.program_id(0) == 0)
        def _():
            dr_scr[...] = jnp.zeros_like(dr_scr)
            dnw_ref[...] = jnp.zeros_like(dnw_ref)

        nw = nw_ref[0:1, :]
        scale = RET_D ** -0.5
        n = ncb * RET_H
        d = _ret_batch(pr_ref, cc_ref, ss_ref, ncb)
        q, k, v, z, s = d["q"], d["k"], d["v"], d["z"], d["s"]
        r_prev = st_ref[...].reshape(n, 128, 128)
        o = _bdot(s, v, _NN) + _bdot(q, r_prev, _NN) * d["qdec"]
        doc = jnp.stack([doc_ref[c * CH:(c + 1) * CH, h * 128:(h + 1) * 128]
                         for c in range(ncb) for h in range(RET_H)])
        on, r, y = _rms_fwd(o, nw, RET_D)
        dz = doc * y * _dsilu(z)
        do, dnw_rows = _rms_bwd(doc * _silu(z), on, r, nw, RET_D)
        dnw_acc = jnp.sum(jnp.sum(dnw_rows, axis=0), axis=0, keepdims=True)
        dqd = do * d["qdec"]
        qtd = _bdot(q, dqd, _TN)
        drn_l = [None] * ncb
        for c in reversed(range(ncb)):
            drn_l[c] = dr_scr[...]
            dr_scr[...] = qtd[c * RET_H:(c + 1) * RET_H] + d["cdec"] * drn_l[c]
        drn = jnp.concatenate(drn_l, axis=0)
        ds = _bdot(do, v, _NT) * d["dmat"]
        dq = _rot_bwd(_bdot(ds, k, _NN) + _bdot(dqd, r_prev, _NT), d["cc"], d["ss"])
        dk = _rot_bwd((_bdot(ds, q, _TN) + _bdot(v, drn, _NT) * d["kdec"]) * scale, d["cc"], d["ss"])
        dv = _bdot(s, do, _TN) + _bdot(k * d["kdec"], drn, _NN)
        for c in range(ncb):
            rows = slice(c * CH, (c + 1) * CH)
            for h in range(RET_H):
                b = c * RET_H + h
                for j, val in enumerate((dq, dk, dv, dz)):
                    dpr_ref[rows, j * 512 + h * 128:j * 512 + (h + 1) * 128] = val[b].astype(dpr_ref.dtype)
        dnw_ref[...] += jnp.where(_iota2((8, 128), 0) == 0, dnw_acc, 0.0)

    def call(pr, cc, ss, nw, st, doc):
        rev = lambda i: (nb - 1 - i, 0)
        return pl.pallas_call(
            body,
            grid=(nb,),
            in_specs=[
                pl.BlockSpec((tb, 2048), rev),
                pl.BlockSpec((tb, 128), rev),
                pl.BlockSpec((tb, 128), rev),
                pl.BlockSpec((8, 128), lambda i: (0, 0)),
                pl.BlockSpec((ncb, RET_H, 128, 128), lambda i: (nb - 1 - i, 0, 0, 0)),
                pl.BlockSpec((tb, 512), rev),
            ],
            out_specs=[
                pl.BlockSpec((tb, 2048), rev),
                pl.BlockSpec((8, 128), lambda i: (0, 0)),
            ],
            out_shape=[
                jax.ShapeDtypeStruct((seq, 2048), BF16),
                jax.ShapeDtypeStruct((8, 128), F32),
            ],
            scratch_shapes=[pltpu.VMEM((RET_H, 128, 128), F32)],
            compiler_params=pltpu.CompilerParams(dimension_semantics=("arbitrary",), vmem_limit_bytes=VMEM_LIMIT),
            name="ret_bwd",
        )(pr, cc, ss, nw, st, doc)

    return call


def _rope_tables(seq):
    half = RET_D // 2
    inv = ROPE_BASE ** (-jnp.arange(half, dtype=F32) / half)
    ang = jnp.arange(seq, dtype=jnp.int32).astype(F32)[:, None] * inv[None, :]
    cos, sin = jnp.cos(ang), jnp.sin(ang)
    return jnp.concatenate([cos, cos], axis=1), jnp.concatenate([-sin, sin], axis=1)


SEG_G, SEG_S, SEG_R, SEG_GS, SEG_SS = (0, 2048), (2048, 4608), (4608, 6656), (6656, 6784), (6784, 6912)
NP = 6912
SEGS = (SEG_G, SEG_S, SEG_R, SEG_GS, SEG_SS)


def _resident(shape):
    return pl.BlockSpec(shape, lambda i: (0,) * len(shape), pipeline_mode=pl.Buffered(1))


def _make_inproj(seq, tl):
    def body(x_ref, pn_ref, w_ref, pg_ref, ps_ref, pr_ref, gs_ref, ss_ref, ht_ref):
        x = x_ref[...]
        _, _, hn = _rms_fwd(x, pn_ref[0:1, :], D_MODEL)
        h = hn.astype(BF16)
        ht_ref[...] = hn.T.astype(BF16)
        for (a, b), o_ref in zip(SEGS, (pg_ref, ps_ref, pr_ref, gs_ref, ss_ref)):
            o_ref[...] = jnp.dot(h, w_ref[:, a:b], preferred_element_type=F32)

    def call(x, pn, w, comm=None, comm_args=()):
        row = lambda i: (i, 0)
        cx = _exchange_specs(comm)
        return pl.pallas_call(
            _with_exchange(body, comm, 3, 6, seq // tl),
            grid=(seq // tl,),
            in_specs=[pl.BlockSpec((tl, D_MODEL), row), _resident((8, D_MODEL)), _resident((D_MODEL, NP))]
            + cx["specs"],
            out_specs=[pl.BlockSpec((tl, b - a), row) for a, b in SEGS]
            + [pl.BlockSpec((D_MODEL, tl), lambda i: (0, i))] + cx["specs"],
            out_shape=[jax.ShapeDtypeStruct((seq, b - a), F32) for a, b in SEGS]
            + [jax.ShapeDtypeStruct((D_MODEL, seq), BF16)] + cx["out_shape"],
            scratch_shapes=cx["scratch"],
            compiler_params=pltpu.CompilerParams(dimension_semantics=("arbitrary",), vmem_limit_bytes=VMEM_LIMIT,
                                                 has_side_effects=comm is not None),
            name="inproj" + cx["tag"],
        )(x, pn, w, *comm_args)

    return call


def _make_outproj(seq, tl):
    def body(oa_ref, ob_ref, oc_ref, w_ref, x_ref, qn_ref, out_ref, xn_ref):
        out = (jnp.dot(oa_ref[...], w_ref[0:512, :], preferred_element_type=F32)
               + jnp.dot(ob_ref[...], w_ref[512:1536, :], preferred_element_type=F32)
               + jnp.dot(oc_ref[...], w_ref[1536:2048, :], preferred_element_type=F32))
        out_ref[...] = out
        _, _, y = _rms_fwd(out, qn_ref[0:1, :], D_MODEL)
        xn_ref[...] = x_ref[...] + y

    def call(oa, ob, oc, w, x, qn):
        row = lambda i: (i, 0)
        return pl.pallas_call(
            body,
            grid=(seq // tl,),
            in_specs=[pl.BlockSpec((tl, 512), row), pl.BlockSpec((tl, 1024), row), pl.BlockSpec((tl, 512), row),
                      _resident((2048, D_MODEL)), pl.BlockSpec((tl, D_MODEL), row), _resident((8, D_MODEL))],
            out_specs=[pl.BlockSpec((tl, D_MODEL), row), pl.BlockSpec((tl, D_MODEL), row)],
            out_shape=[jax.ShapeDtypeStruct((seq, D_MODEL), F32), jax.ShapeDtypeStruct((seq, D_MODEL), F32)],
            compiler_params=pltpu.CompilerParams(dimension_semantics=("arbitrary",), vmem_limit_bytes=VMEM_LIMIT),
            name="outproj",
        )(oa, ob, oc, w, x, qn)

    return call


def _make_outproj_loss(seq, tl):
    def body(oa_ref, ob_ref, oc_ref, w_ref, x_ref, qn_ref, t_ref, out_ref, dy_ref, loss_ref):
        @pl.when(pl.program_id(0) == 0)
        def _():
            loss_ref[...] = jnp.zeros_like(loss_ref)

        out = (jnp.dot(oa_ref[...], w_ref[0:512, :], preferred_element_type=F32)
               + jnp.dot(ob_ref[...], w_ref[512:1536, :], preferred_element_type=F32)
               + jnp.dot(oc_ref[...], w_ref[1536:2048, :], preferred_element_type=F32))
        out_ref[...] = out
        _, _, y = _rms_fwd(out, qn_ref[0:1, :], D_MODEL)
        err = (x_ref[...] + y) - t_ref[...]
        dy_ref[...] = err * (1.0 / D_MODEL)
        part = jnp.sum(jnp.sum(err * err, axis=1, keepdims=True), axis=0, keepdims=True) * (0.5 / D_MODEL)
        loss_ref[...] += jnp.where((_iota2((8, 128), 0) == 0) & (_iota2((8, 128), 1) == 0), part, 0.0)

    def call(oa, ob, oc, w, x, qn, t):
        row = lambda i: (i, 0)
        return pl.pallas_call(
            body,
            grid=(seq // tl,),
            in_specs=[pl.BlockSpec((tl, 512), row), pl.BlockSpec((tl, 1024), row), pl.BlockSpec((tl, 512), row),
                      _resident((2048, D_MODEL)), pl.BlockSpec((tl, D_MODEL), row), _resident((8, D_MODEL)),
                      pl.BlockSpec((tl, D_MODEL), row)],
            out_specs=[pl.BlockSpec((tl, D_MODEL), row), pl.BlockSpec((tl, D_MODEL), row),
                       pl.BlockSpec((8, 128), lambda i: (0, 0))],
            out_shape=[jax.ShapeDtypeStruct((seq, D_MODEL), F32), jax.ShapeDtypeStruct((seq, D_MODEL), F32),
                       jax.ShapeDtypeStruct((8, 128), F32)],
            compiler_params=pltpu.CompilerParams(dimension_semantics=("arbitrary",), vmem_limit_bytes=VMEM_LIMIT),
            name="outproj_loss",
        )(oa, ob, oc, w, x, qn, t)

    return call


def _make_outproj_bwd(seq, tl):
    def body(dxn_ref, out_ref, oa_ref, ob_ref, oc_ref, w_ref, qn_ref, doa_ref, dob_ref, doc_ref, dqn_ref, dw_ref):
        @pl.when(pl.program_id(0) == 0)
        def _():
            dqn_ref[...] = jnp.zeros_like(dqn_ref)
            dw_ref[...] = jnp.zeros_like(dw_ref)

        qn = qn_ref[0:1, :]
        on, r, _ = _rms_fwd(out_ref[...], qn, D_MODEL)
        dout, dqn_rows = _rms_bwd(dxn_ref[...], on, r, qn, D_MODEL)
        dqn_ref[...] += jnp.where(_iota2((8, D_MODEL), 0) == 0, jnp.sum(dqn_rows, axis=0, keepdims=True), 0.0)
        db = dout.astype(BF16)
        nt = (((1,), (1,)), ((), ()))
        tn = (((0,), (0,)), ((), ()))
        doa_ref[...] = lax.dot_general(db, w_ref[0:512, :], nt, preferred_element_type=F32).astype(BF16)
        dob_ref[...] = lax.dot_general(db, w_ref[512:1536, :], nt, preferred_element_type=F32).astype(BF16)
        doc_ref[...] = lax.dot_general(db, w_ref[1536:2048, :], nt, preferred_element_type=F32).astype(BF16)
        dw_ref[0:512, :] += lax.dot_general(oa_ref[...], db, tn, preferred_element_type=F32)
        dw_ref[512:1536, :] += lax.dot_general(ob_ref[...], db, tn, preferred_element_type=F32)
        dw_ref[1536:2048, :] += lax.dot_general(oc_ref[...], db, tn, preferred_element_type=F32)

    def call(dxn, out, oa, ob, oc, w, qn):
        row = lambda i: (i, 0)
        const = lambda i: (0, 0)
        return pl.pallas_call(
            body,
            grid=(seq // tl,),
            in_specs=[pl.BlockSpec((tl, D_MODEL), row), pl.BlockSpec((tl, D_MODEL), row),
                      pl.BlockSpec((tl, 512), row), pl.BlockSpec((tl, 1024), row), pl.BlockSpec((tl, 512), row),
                      _resident((2048, D_MODEL)), _resident((8, D_MODEL))],
            out_specs=[pl.BlockSpec((tl, 512), row), pl.BlockSpec((tl, 1024), row), pl.BlockSpec((tl, 512), row),
                       pl.BlockSpec((8, D_MODEL), const), pl.BlockSpec((2048, D_MODEL), const)],
            out_shape=[jax.ShapeDtypeStruct((seq, 512), BF16), jax.ShapeDtypeStruct((seq, 1024), BF16),
                       jax.ShapeDtypeStruct((seq, 512), BF16), jax.ShapeDtypeStruct((8, D_MODEL), F32),
                       jax.ShapeDtypeStruct((2048, D_MODEL), F32)],
            compiler_params=pltpu.CompilerParams(dimension_semantics=("arbitrary",), vmem_limit_bytes=VMEM_LIMIT),
            name="outproj_bwd",
        )(dxn, out, oa, ob, oc, w, qn)

    return call


def _make_inproj_bwd_dx(seq, tl):
    def body(dg_ref, ds_ref, dr_ref, dgs_ref, dss_ref, w_ref, x_ref, pn_ref, dxn_ref, dx_ref, dpn_ref):
        @pl.when(pl.program_id(0) == 0)
        def _():
            dpn_ref[...] = jnp.zeros_like(dpn_ref)

        nt = (((1,), (1,)), ((), ()))
        dh = jnp.zeros((tl, D_MODEL), F32)
        for (a, b), d_ref in zip(SEGS, (dg_ref, ds_ref, dr_ref, dgs_ref, dss_ref)):
            dh = dh + lax.dot_general(d_ref[...], w_ref[:, a:b], nt, preferred_element_type=F32)
        pn = pn_ref[0:1, :]
        on, r, _ = _rms_fwd(x_ref[...], pn, D_MODEL)
        dx, dpn_rows = _rms_bwd(dh, on, r, pn, D_MODEL)
        dx_ref[...] = dx + dxn_ref[...]
        dpn_ref[...] += jnp.where(_iota2((8, D_MODEL), 0) == 0, jnp.sum(dpn_rows, axis=0, keepdims=True), 0.0)

    def call(dg, ds, dr, dgs, dss, w, x, pn, dxn, comm=None, comm_args=()):
        row = lambda i: (i, 0)
        cx = _exchange_specs(comm)
        return pl.pallas_call(
            _with_exchange(body, comm, 9, 2, seq // tl),
            grid=(seq // tl,),
            in_specs=[pl.BlockSpec((tl, b - a), row) for a, b in SEGS]
            + [_resident((D_MODEL, NP)), pl.BlockSpec((tl, D_MODEL), row), _resident((8, D_MODEL)),
               pl.BlockSpec((tl, D_MODEL), row)] + cx["specs"],
            out_specs=[pl.BlockSpec((tl, D_MODEL), row), pl.BlockSpec((8, D_MODEL), lambda i: (0, 0))] + cx["specs"],
            out_shape=[jax.ShapeDtypeStruct((seq, D_MODEL), F32), jax.ShapeDtypeStruct((8, D_MODEL), F32)]
            + cx["out_shape"],
            scratch_shapes=cx["scratch"],
            compiler_params=pltpu.CompilerParams(dimension_semantics=("arbitrary",), vmem_limit_bytes=VMEM_LIMIT,
                                                 has_side_effects=comm is not None),
            name="inproj_bwd_dx" + cx["tag"],
        )(dg, ds, dr, dgs, dss, w, x, pn, dxn, *comm_args)

    return call


def _make_inproj_bwd_dw(seq, tl, width, tn, name):
    def body(ht_ref, d_ref, dw_ref):
        @pl.when(pl.program_id(1) == 0)
        def _():
            dw_ref[...] = jnp.zeros_like(dw_ref)

        dw_ref[...] += jnp.dot(ht_ref[...], d_ref[...], preferred_element_type=F32)

    def call(ht, d):
        return pl.pallas_call(
            body,
            grid=(width // tn, seq // tl),
            in_specs=[pl.BlockSpec((D_MODEL, tl), lambda j, i: (0, i)), pl.BlockSpec((tl, tn), lambda j, i: (i, j))],
            out_specs=pl.BlockSpec((D_MODEL, tn), lambda j, i: (0, j)),
            out_shape=jax.ShapeDtypeStruct((D_MODEL, width), F32),
            compiler_params=pltpu.CompilerParams(dimension_semantics=("arbitrary", "arbitrary"),
                                                 vmem_limit_bytes=VMEM_LIMIT),
            name=name,
        )(ht, d)

    return call


ADAM_LR, ADAM_B1, ADAM_B2, ADAM_EPS, ADAM_WD, ADAM_STEP = 0.001, 0.9, 0.999, 1e-08, 0.01, 10


def _adam_math(w, g, m, v):
    m = ADAM_B1 * m + (1.0 - ADAM_B1) * g
    v = ADAM_B2 * v + (1.0 - ADAM_B2) * (g * g)
    m_hat = m / (1.0 - ADAM_B1 ** ADAM_STEP)
    v_hat = v / (1.0 - ADAM_B2 ** ADAM_STEP)
    delta = -ADAM_LR * (m_hat / (jnp.sqrt(v_hat) + ADAM_EPS) + ADAM_WD * w)
    return delta, m, v


def _adamw(w, g, m, v, name):
    shape = w.shape
    cols = shape[-1]
    rows = w.size // cols
    tr = rows if rows <= 512 else 256
    assert rows % tr == 0

    def body(w_ref, g_ref, m_ref, v_ref, d_ref, mo_ref, vo_ref):
        d_ref[...], mo_ref[...], vo_ref[...] = _adam_math(w_ref[...], g_ref[...], m_ref[...], v_ref[...])

    spec = pl.BlockSpec((tr, cols), lambda i: (i, 0))
    outs = pl.pallas_call(
        body,
        grid=(rows // tr,),
        in_specs=[spec] * 4,
        out_specs=[spec] * 3,
        out_shape=[jax.ShapeDtypeStruct((rows, cols), F32)] * 3,
        compiler_params=pltpu.CompilerParams(dimension_semantics=("arbitrary",), vmem_limit_bytes=VMEM_LIMIT),
        name=name,
    )(*[a.reshape(rows, cols) for a in (w, g, m, v)])
    return (g,) + tuple(o.reshape(shape) for o in outs)


def _adamw_pairs(w, mine, theirs, m, v, name):
    na, r, cols = w.shape
    assert na == 2
    tr = 256
    assert r % tr == 0

    def body(w_ref, a0_ref, b0_ref, a1_ref, b1_ref, m_ref, v_ref, g_ref, d_ref, mo_ref, vo_ref):
        g = jnp.where(pl.program_id(0) == 0, a0_ref[...] + b0_ref[...], a1_ref[...] + b1_ref[...])
        g_ref[...] = g
        d_ref[...], mo_ref[...], vo_ref[...] = _adam_math(w_ref[...], g, m_ref[...], v_ref[...])

    nblk = r // tr
    full = pl.BlockSpec((None, tr, cols), lambda a, i: (a, i, 0))
    lay0 = pl.BlockSpec((None, tr, cols), lambda a, i: (0, i * (1 - a) + (nblk - 1) * a, 0))
    lay1 = pl.BlockSpec((None, tr, cols), lambda a, i: (0, i * a, 0))
    return pl.pallas_call(
        body,
        grid=(na, nblk),
        in_specs=[full, lay0, lay0, lay1, lay1, full, full],
        out_specs=[full] * 4,
        out_shape=[jax.ShapeDtypeStruct(w.shape, F32)] * 4,
        compiler_params=pltpu.CompilerParams(dimension_semantics=("arbitrary",) * 2, vmem_limit_bytes=VMEM_LIMIT),
        name=name,
    )(w, mine[0], theirs[0], mine[1], theirs[1], m, v)


def _adamw_pairs_t(w, mine, theirs, m, v, name):
    na, r, cols = w.shape
    assert na == 2
    tr = 128
    assert r % tr == 0

    def body(w_ref, a0_ref, b0_ref, a1_ref, b1_ref, m_ref, v_ref, g_ref, d_ref, mo_ref, vo_ref):
        for l, (a_ref, b_ref) in enumerate(((a0_ref, b0_ref), (a1_ref, b1_ref))):
            g = (a_ref[...] + b_ref[...]).T
            g_ref[:, l, :] = g
            d_ref[:, l, :], mo_ref[:, l, :], vo_ref[:, l, :] = _adam_math(w_ref[:, l, :], g, m_ref[:, l, :],
                                                                          v_ref[:, l, :])

    to_t = lambda a: jnp.transpose(a, (2, 0, 1))
    slab = pl.BlockSpec((cols, na, tr), lambda j: (0, 0, j), pipeline_mode=pl.Buffered(1))
    part = pl.BlockSpec((None, tr, cols), lambda j: (0, j, 0))
    outs = pl.pallas_call(
        body,
        grid=(r // tr,),
        in_specs=[slab, part, part, part, part, slab, slab],
        out_specs=[slab] * 4,
        out_shape=[jax.ShapeDtypeStruct((cols, na, r), F32)] * 4,
        compiler_params=pltpu.CompilerParams(dimension_semantics=("arbitrary",), vmem_limit_bytes=60 * 1024 * 1024),
        name=name,
    )(to_t(w), mine[0], theirs[0], mine[1], theirs[1], to_t(m), to_t(v))
    return tuple(jnp.transpose(o, (1, 2, 0)) for o in outs)


MESH = pl.DeviceIdType.MESH
ANY = pl.BlockSpec(memory_space=pl.ANY)
CHIP_REL = ((1, 0), (0, 1), (1, 1))


def _flip(v, d):
    return 1 - v if d else v


def _ag_chips(arrs, name):
    n = len(arrs)

    def body(*refs):
        ins, outs = refs[:n], refs[n:2 * n]
        send_sems, recv_sems, loc_sems = refs[2 * n:]
        x, y, c = lax.axis_index("x"), lax.axis_index("y"), lax.axis_index("c")
        me = 2 * x + y

        def remote(a, k, slot):
            dx, dy = CHIP_REL[k]
            return pltpu.make_async_remote_copy(
                src_ref=ins[a], dst_ref=outs[a].at[slot], send_sem=send_sems.at[a * 3 + k],
                recv_sem=recv_sems.at[a * 3 + k], device_id=(_flip(x, dx), _flip(y, dy), c), device_id_type=MESH)

        local = [pltpu.make_async_copy(ins[a], outs[a].at[me], loc_sems.at[a]) for a in range(n)]
        for cp in local:
            cp.start()
        for a in range(n):
            for k in range(3):
                remote(a, k, me).start()
        for a in range(n):
            for k, (dx, dy) in enumerate(CHIP_REL):
                remote(a, k, 2 * _flip(x, dx) + _flip(y, dy)).wait_recv()
        for a in range(n):
            for k in range(3):
                remote(a, k, me).wait_send()
        for cp in local:
            cp.wait()

    return pl.pallas_call(
        body,
        in_specs=[ANY] * n,
        out_specs=[ANY] * n,
        out_shape=[jax.ShapeDtypeStruct((4,) + a.shape, a.dtype) for a in arrs],
        scratch_shapes=[pltpu.SemaphoreType.DMA((3 * n,)), pltpu.SemaphoreType.DMA((3 * n,)),
                        pltpu.SemaphoreType.DMA((n,))],
        compiler_params=pltpu.CompilerParams(has_side_effects=True),
        name=name,
    )(*arrs)


class _ChipExchange:
    def __init__(self, kind, arrs):
        self.kind, self.n = kind, len(arrs)
        if kind == "gather":
            self.out_shape = [jax.ShapeDtypeStruct((4,) + a.shape, a.dtype) for a in arrs]
        else:
            self.out_shape = [jax.ShapeDtypeStruct((3,) + a.shape[1:], a.dtype) for a in arrs]
        self.scratch = [pltpu.SemaphoreType.DMA((4 * self.n,)), pltpu.SemaphoreType.DMA((4 * self.n,))]

    def _copies(self, ins, outs, sems):
        send_sems, recv_sems = sems
        x, y, c = lax.axis_index("x"), lax.axis_index("y"), lax.axis_index("c")
        me = 2 * x + y
        pairs = []
        for a in range(self.n):
            for k, (dx, dy) in enumerate(CHIP_REL):
                px, py = _flip(x, dx), _flip(y, dy)
                sem = dict(send_sem=send_sems.at[4 * a + k], recv_sem=recv_sems.at[4 * a + k],
                           device_id=(px, py, c), device_id_type=MESH)
                if self.kind == "gather":
                    out = pltpu.make_async_remote_copy(src_ref=ins[a], dst_ref=outs[a].at[me], **sem)
                    inc = pltpu.make_async_remote_copy(src_ref=ins[a], dst_ref=outs[a].at[2 * px + py], **sem)
                else:
                    out = pltpu.make_async_remote_copy(src_ref=ins[a].at[2 * px + py], dst_ref=outs[a].at[k], **sem)
                    inc = out
                pairs.append((out, inc))
            if self.kind == "gather":
                own = pltpu.make_async_remote_copy(
                    src_ref=ins[a], dst_ref=outs[a].at[me], send_sem=send_sems.at[4 * a + 3],
                    recv_sem=recv_sems.at[4 * a + 3], device_id=(x, y, 1 - c), device_id_type=MESH)
                pairs.append((own, own))
        return pairs

    def start(self, ins, outs, sems):
        for out, _ in self._copies(ins, outs, sems):
            out.start()

    def finish(self, ins, outs, sems):
        pairs = self._copies(ins, outs, sems)
        for _, inc in pairs:
            inc.wait_recv()
        for out, _ in pairs:
            out.wait_send()


def _with_exchange(body, comm, n_in, n_out, nb):
    if comm is None:
        return body

    def wrapped(*refs):
        ins = refs[:n_in]
        c_in = refs[n_in:n_in + comm.n]
        outs = refs[n_in + comm.n:n_in + comm.n + n_out]
        c_out = refs[n_in + comm.n + n_out:n_in + 2 * comm.n + n_out]
        rest = refs[n_in + 2 * comm.n + n_out:]
        scratch, sems = rest[:len(rest) - 2], rest[len(rest) - 2:]

        @pl.when(pl.program_id(0) == 0)
        def _():
            comm.start(c_in, c_out, sems)

        body(*ins, *outs, *scratch)

        @pl.when(pl.program_id(0) == nb - 1)
        def _():
            comm.finish(c_in, c_out, sems)

    return wrapped


def _exchange_specs(comm):
    if comm is None:
        return dict(specs=[], out_shape=[], scratch=[], tag="")
    return dict(specs=[pl.BlockSpec(memory_space=pl.ANY)] * comm.n, out_shape=list(comm.out_shape),
                scratch=list(comm.scratch), tag="_" + comm.kind)


def _half(ref_or_shape, half):
    r = ref_or_shape[-2] // 2
    return pl.ds(half * r, r)


def _ag_rows(arrs, name):
    n = len(arrs)

    def body(*refs):
        ins, outs = refs[:n], refs[n:2 * n]
        send_sems, recv_sems, fsend_sems, frecv_sems, loc_sems = refs[2 * n:]
        x, y, c = lax.axis_index("x"), lax.axis_index("y"), lax.axis_index("c")
        me = 2 * x + y
        sib = (x, y, 1 - c)

        def chip_of(k):
            dx, dy = CHIP_REL[k]
            return _flip(x, dx), _flip(y, dy)

        def ici(a, k, slot):
            px, py = chip_of(k)
            rows = _half(arrs[a].shape, c)
            return pltpu.make_async_remote_copy(
                src_ref=ins[a].at[:, rows, :], dst_ref=outs[a].at[slot, :, rows, :], send_sem=send_sems.at[a * 3 + k],
                recv_sem=recv_sems.at[a * 3 + k], device_id=(px, py, c), device_id_type=MESH)

        def fwd(a, k, half):
            px, py = chip_of(k)
            blk = outs[a].at[2 * px + py, :, _half(arrs[a].shape, half), :]
            return pltpu.make_async_remote_copy(
                src_ref=blk, dst_ref=blk, send_sem=fsend_sems.at[a * 3 + k], recv_sem=frecv_sems.at[a * 3 + k],
                device_id=sib, device_id_type=MESH)

        own = [pltpu.make_async_remote_copy(src_ref=ins[a], dst_ref=outs[a].at[me], send_sem=loc_sems.at[a],
                                            recv_sem=loc_sems.at[n + a], device_id=sib, device_id_type=MESH)
               for a in range(n)]
        for cp in own:
            cp.start()
        for a in range(n):
            for k in range(3):
                ici(a, k, me).start()
        for a in range(n):
            for k in range(3):
                px, py = chip_of(k)
                ici(a, k, 2 * px + py).wait_recv()
                fwd(a, k, c).start()
        for a in range(n):
            for k in range(3):
                fwd(a, k, 1 - c).wait_recv()
        for a in range(n):
            for k in range(3):
                ici(a, k, me).wait_send()
                fwd(a, k, c).wait_send()
        for cp in own:
            cp.wait()

    return pl.pallas_call(
        body,
        in_specs=[ANY] * n,
        out_specs=[ANY] * n,
        out_shape=[jax.ShapeDtypeStruct((4,) + a.shape, a.dtype) for a in arrs],
        scratch_shapes=[pltpu.SemaphoreType.DMA((3 * n,)) for _ in range(4)] + [pltpu.SemaphoreType.DMA((2 * n,))],
        compiler_params=pltpu.CompilerParams(has_side_effects=True),
        name=name,
    )(*arrs)


def _sum_chips(own, recv, chip, name):
    _, na, r, cols = own.shape
    tr = 256
    assert r % tr == 0

    def body(chip_ref, o_ref, r_ref, s_ref):
        s_ref[...] = ((o_ref[...] + r_ref[0].astype(F32)) + r_ref[1].astype(F32)) + r_ref[2].astype(F32)

    return pl.pallas_call(
        body,
        grid_spec=pltpu.PrefetchScalarGridSpec(
            num_scalar_prefetch=1,
            grid=(na, r // tr),
            in_specs=[pl.BlockSpec((None, None, tr, cols), lambda a, i, ch: (ch[0], a, i, 0)),
                      pl.BlockSpec((3, None, tr, cols), lambda a, i, ch: (0, a, i, 0))],
            out_specs=pl.BlockSpec((None, tr, cols), lambda a, i, ch: (a, i, 0))),
        out_shape=jax.ShapeDtypeStruct((na, r, cols), F32),
        compiler_params=pltpu.CompilerParams(dimension_semantics=("arbitrary",) * 2, vmem_limit_bytes=VMEM_LIMIT),
        name=name,
    )(chip, own, recv)


def _swap_sibling(arrs, name):
    n = len(arrs)

    def body(*refs):
        ins, outs = refs[:n], refs[n:2 * n]
        send_sems, recv_sems = refs[2 * n:]
        x, y, c = lax.axis_index("x"), lax.axis_index("y"), lax.axis_index("c")
        cps = [pltpu.make_async_remote_copy(src_ref=ins[a], dst_ref=outs[a], send_sem=send_sems.at[a],
                                            recv_sem=recv_sems.at[a], device_id=(x, y, 1 - c), device_id_type=MESH)
               for a in range(n)]
        for cp in cps:
            cp.start()
        for cp in cps:
            cp.wait_recv()
        for cp in cps:
            cp.wait_send()

    return pl.pallas_call(
        body,
        in_specs=[ANY] * n,
        out_specs=[ANY] * n,
        out_shape=[jax.ShapeDtypeStruct(a.shape, a.dtype) for a in arrs],
        scratch_shapes=[pltpu.SemaphoreType.DMA((n,)), pltpu.SemaphoreType.DMA((n,))],
        compiler_params=pltpu.CompilerParams(has_side_effects=True),
        name=name,
    )(*arrs)


def _allreduce_small(vec, name):
    rows = vec.shape[0]

    def body(v_ref, out_ref, gat_ref, send_sems, recv_sems):
        x, y, c = lax.axis_index("x"), lax.axis_index("y"), lax.axis_index("c")
        me = 4 * x + 2 * y + c

        def remote(k, slot):
            dx, dy, dc = (k >> 2) & 1, (k >> 1) & 1, k & 1
            return pltpu.make_async_remote_copy(
                src_ref=v_ref, dst_ref=gat_ref.at[slot], send_sem=send_sems.at[k - 1], recv_sem=recv_sems.at[k - 1],
                device_id=(_flip(x, dx), _flip(y, dy), _flip(c, dc)), device_id_type=MESH)

        gat_ref[me] = v_ref[...]
        for k in range(1, 8):
            remote(k, me).start()
        for k in range(1, 8):
            dx, dy, dc = (k >> 2) & 1, (k >> 1) & 1, k & 1
            remote(k, 4 * _flip(x, dx) + 2 * _flip(y, dy) + _flip(c, dc)).wait_recv()
        for k in range(1, 8):
            remote(k, me).wait_send()
        acc = gat_ref[0]
        for j in range(1, 8):
            acc = acc + gat_ref[j]
        out_ref[...] = acc

    vm = pl.BlockSpec(memory_space=pltpu.VMEM)
    return pl.pallas_call(
        body,
        in_specs=[vm],
        out_specs=vm,
        out_shape=jax.ShapeDtypeStruct(vec.shape, F32),
        scratch_shapes=[pltpu.VMEM((8, rows, 128), F32), pltpu.SemaphoreType.DMA((7,)), pltpu.SemaphoreType.DMA((7,))],
        compiler_params=pltpu.CompilerParams(has_side_effects=True),
        name=name,
    )(vec)


def _pad8(v, width, lane0=0):
    v = v.reshape(1, -1) if v.ndim == 1 else v
    return jnp.zeros((8, width), F32).at[:v.shape[0], lane0:lane0 + v.shape[1]].set(v.astype(F32))


def _relayout_w_in(g):
    tr = 128
    q = N_IN // 4

    def body(g_ref, o_ref):
        w = jnp.concatenate([g_ref[j] for j in range(4)], axis=1)
        z = lambda n: jnp.zeros((tr, n), w.dtype)
        o_ref[...] = jnp.concatenate([w[:, 0:2048], w[:, 2056:4616], w[:, 4632:6680],
                                      w[:, 2048:2056], z(120), w[:, 4616:4632], z(112)], axis=1)

    return pl.pallas_call(
        body,
        grid=(D_MODEL // tr,),
        in_specs=[pl.BlockSpec((4, tr, q), lambda i: (0, i, 0))],
        out_specs=pl.BlockSpec((tr, NP), lambda i: (i, 0)),
        out_shape=jax.ShapeDtypeStruct((D_MODEL, NP), g.dtype),
        compiler_params=pltpu.CompilerParams(dimension_semantics=("arbitrary",), vmem_limit_bytes=VMEM_LIMIT),
        name="relayout_w_in",
    )(g)


def _unlayout_dw_in(dg, ds, dr, dsm):
    tr = 128
    q = N_IN // 4

    def body(g_ref, s_ref, r_ref, sm_ref, o_ref, ob_ref):
        w = jnp.concatenate([g_ref[...], sm_ref[:, 0:8], s_ref[...], sm_ref[:, 128:144], r_ref[...]], axis=1)
        for j in range(4):
            blk = w[:, q * j:q * (j + 1)]
            o_ref[j] = blk
            ob_ref[j] = blk.astype(BF16)

    row = lambda i: (i, 0)
    return pl.pallas_call(
        body,
        grid=(D_MODEL // tr,),
        in_specs=[pl.BlockSpec((tr, d.shape[1]), row) for d in (dg, ds, dr, dsm)],
        out_specs=[pl.BlockSpec((4, tr, q), lambda i: (0, i, 0))] * 2,
        out_shape=[jax.ShapeDtypeStruct((4, D_MODEL, q), F32), jax.ShapeDtypeStruct((4, D_MODEL, q), BF16)],
        compiler_params=pltpu.CompilerParams(dimension_semantics=("arbitrary",), vmem_limit_bytes=VMEM_LIMIT),
        name="unlayout_dw_in",
    )(dg, ds, dr, dsm)


TB = 256
TL = 256
TL_IN = 512
TL_OB = 1024
TK = 2048


def kernel(x, pre_norm, post_norm, w_in, gdn_conv, gdn_A_log, gdn_dt_bias, gdn_norm, ssd_conv, ssd_conv_b, ssd_A_log, ssd_dt_bias, ssd_D, ssd_norm, ret_norm, w_out, loss_target, m_pre_norm, m_post_norm, m_w_in, m_gdn_conv, m_gdn_A_log, m_gdn_dt_bias, m_gdn_norm, m_ssd_conv, m_ssd_conv_b, m_ssd_A_log, m_ssd_dt_bias, m_ssd_D, m_ssd_norm, m_ret_norm, m_w_out, v_pre_norm, v_post_norm, v_w_in, v_gdn_conv, v_gdn_A_log, v_gdn_dt_bias, v_gdn_norm, v_ssd_conv, v_ssd_conv_b, v_ssd_A_log, v_ssd_dt_bias, v_ssd_D, v_ssd_norm, v_ret_norm, v_w_out):
    seq = x.shape[1]
    chip = 2 * lax.axis_index("x") + lax.axis_index("y")
    x0 = x[0]

    wi_b, wo_b = w_in.astype(BF16), w_out.astype(BF16)
    (wi0_g,) = _ag_rows([wi_b[0:1]], "ag_weights")
    gcv_g, scv_g = _ag_chips([gdn_conv, ssd_conv], "ag_conv")
    full_w_in = _relayout_w_in
    wp = [full_w_in(wi0_g[:, 0]), None]
    wo = [None, None]
    ag0 = _ChipExchange("gather", [wo_b[0]])
    ag1 = _ChipExchange("gather", [wi_b[1], wo_b[1]])
    gcv = jnp.transpose(gcv_g, (1, 2, 0, 3)).reshape(DEPTH, CONV_W, 1536)
    scv = jnp.transpose(scv_g, (1, 2, 0, 3)).reshape(DEPTH, CONV_W, 1536)
    rope_c, rope_s = _rope_tables(seq)

    saved = []
    xc = x0
    for l in range(DEPTH):
        p = dict(
            pn=_pad8(pre_norm[l], D_MODEL), qn=_pad8(post_norm[l], D_MODEL),
            g_cw=_pad8(gcv[l], 1536), g_prm=_pad8(jnp.stack([gdn_A_log[l], gdn_dt_bias[l]]), 128, 4),
            g_nw=_pad8(gdn_norm[l], 128),
            s_cw=_pad8(scv[l], 1536), s_cb=_pad8(ssd_conv_b[l], 1536),
            s_prm=_pad8(jnp.stack([ssd_A_log[l], ssd_dt_bias[l], ssd_D[l]]), 128), s_nw=_pad8(ssd_norm[l], SSD_W),
            r_nw=_pad8(ret_norm[l], 128))
        if l == 0:
            pg, ps, pr, gs, ss, ht, wo0_g = _make_inproj(seq, TL_IN)(xc, p["pn"], wp[l], comm=ag0, comm_args=(wo_b[0],))
            wo[0] = wo0_g.reshape(2048, D_MODEL)
        else:
            pg, ps, pr, gs, ss, ht = _make_inproj(seq, TL_IN)(xc, p["pn"], wp[l])
        if l == 0:
            oa, stg, tig, uwg, gpre, wi1_g, wo1_g = _make_gdn_fwd(seq, TB)(
                pg, gs, p["g_cw"], p["g_prm"], p["g_nw"], comm=ag1, comm_args=(wi_b[1], wo_b[1]))
            wp[1], wo[1] = full_w_in(wi1_g), wo1_g.reshape(2048, D_MODEL)
        else:
            oa, stg, tig, uwg, gpre = _make_gdn_fwd(seq, TB)(pg, gs, p["g_cw"], p["g_prm"], p["g_nw"])
        ob, sts, spre, sy = _make_ssd_fwd(seq, TB)(ps, ss, p["s_cw"], p["s_cb"], p["s_prm"], p["s_nw"])
        oc, str_ = _make_ret_fwd(seq, TB)(pr, rope_c, rope_s, p["r_nw"])
        if l == DEPTH - 1:
            out, dxn, lossp = _make_outproj_loss(seq, TL)(oa, ob, oc, wo[l], xc, p["qn"], loss_target[0])
            xn = None
        else:
            out, xn = _make_outproj(seq, TL)(oa, ob, oc, wo[l], xc, p["qn"])
        saved.append(dict(p=p, x=xc, ht=ht, spre=spre, sy=sy, gpre=gpre, pg=pg, ps=ps, pr=pr, gs=gs, ss=ss, stg=stg, tig=tig, uwg=uwg, sts=sts, str=str_,
                          oa=oa, ob=ob, oc=oc, out=out))
        xc = xn

    small = [None] * DEPTH
    gin, gin_b, gout, q_in, q_out = ([None] * DEPTH for _ in range(5))

    for l in reversed(range(DEPTH)):
        s = saved[l]
        p = s["p"]
        doa, dob, doc, dqn, dwo_l = _make_outproj_bwd(seq, TL_OB)(dxn, s["out"], s["oa"], s["ob"], s["oc"], wo[l], p["qn"])
        gout[l] = dwo_l.reshape(4, 512, D_MODEL)
        gdn_args = (s["pg"], s["gpre"], s["gs"], p["g_cw"], p["g_prm"], p["g_nw"], s["stg"], s["tig"], s["uwg"], doa)
        if l == 0:
            payload = (gout[0].astype(BF16),)
            dpg, dgs, dcw_g, dprm_g, dnw_g, q_out[0] = _make_gdn_bwd(seq, TB)(
                *gdn_args, comm=_ChipExchange("scatter", payload), comm_args=payload)
        else:
            dpg, dgs, dcw_g, dprm_g, dnw_g = _make_gdn_bwd(seq, TB)(*gdn_args)
        ssd_args = (s["ps"], s["spre"], s["sy"], s["ss"], p["s_cw"], p["s_cb"], p["s_prm"], p["s_nw"], s["sts"], dob)
        if l == 0:
            payload = (gin_b[1], gout[1].astype(BF16))
            dps, dss, dcw_s, dcb_s, dprm_s, dnw_s, q_in[1], q_out[1] = _make_ssd_bwd(seq, TB)(
                *ssd_args, comm=_ChipExchange("scatter", payload), comm_args=payload)
        else:
            dps, dss, dcw_s, dcb_s, dprm_s, dnw_s = _make_ssd_bwd(seq, TB)(*ssd_args)
        dpr, dnw_r = _make_ret_bwd(seq, TB)(s["pr"], rope_c, rope_s, p["r_nw"], s["str"], doc)
        dws = [_make_inproj_bwd_dw(seq, TK, d.shape[1], tn, f"inproj_bwd_dw{i}")(s["ht"], d)
               for i, (d, tn) in enumerate(((dpg, 2048), (dps, 1280), (dpr, 2048),
                                            (jnp.concatenate([dgs, dss], axis=1), 256)))]
        gin[l], gin_b[l] = _unlayout_dw_in(*dws)
        dx_args = (dpg, dps, dpr, dgs, dss, wp[l], s["x"], p["pn"], dxn)
        if l == 0:
            payload = (gin_b[0],)
            dx, dpn, q_in[0] = _make_inproj_bwd_dx(seq, TL_IN)(
                *dx_args, comm=_ChipExchange("scatter", payload), comm_args=payload)
        else:
            dx, dpn = _make_inproj_bwd_dx(seq, TL_IN)(*dx_args)
        small[l] = [dpn[0], dqn[0], dcw_g[0:4].reshape(-1), dprm_g[0, 4:8], dprm_g[1, 4:8], dnw_g[0],
                    dcw_s[0:4].reshape(-1), dcb_s[0], dprm_s[0, 0:16], dprm_s[1, 0:16], dprm_s[2, 0:16],
                    dnw_s[0], dnw_r[0]]
        dxn = dx
    grad_x = dxn[None]

    sizes = [a.shape[0] for a in small[0]]
    flat = jnp.concatenate(small[0] + small[1] + [lossp[0, 0:1]])
    n_flat = flat.shape[0]
    rows = -(-n_flat // 1024) * 8
    red = _allreduce_small(jnp.pad(flat, (0, rows * 128 - n_flat)).reshape(rows, 128), "allreduce_small").reshape(-1)
    per = sum(sizes)
    loss = red[2 * per]

    def pick(i):
        off = sum(sizes[:i])
        return jnp.stack([red[l * per + off:l * per + off + sizes[i]] for l in range(DEPTH)])

    g_small = dict(
        pre_norm=pick(0), post_norm=pick(1),
        gdn_conv=lax.dynamic_slice_in_dim(pick(2).reshape(DEPTH, CONV_W, 1536), chip * 384, 384, axis=2),
        gdn_A_log=pick(3), gdn_dt_bias=pick(4), gdn_norm=pick(5),
        ssd_conv=lax.dynamic_slice_in_dim(pick(6).reshape(DEPTH, CONV_W, 1536), chip * 384, 384, axis=2),
        ssd_conv_b=pick(7), ssd_A_log=pick(8), ssd_dt_bias=pick(9), ssd_D=pick(10), ssd_norm=pick(11),
        ret_norm=pick(12))

    chip1 = chip.astype(jnp.int32).reshape(1)
    s_in = [_sum_chips(gin[l][:, None], q_in[l][:, None], chip1, f"sum_chips_w_in{l}") for l in range(DEPTH)]
    s_out = [_sum_chips(gout[l][:, None], q_out[l][:, None], chip1, f"sum_chips_w_out{l}") for l in range(DEPTH)]
    t_all = _swap_sibling(s_in + s_out, "swap_grads")
    t_in, t_out = t_all[:DEPTH], t_all[DEPTH:]

    weights = dict(pre_norm=pre_norm, post_norm=post_norm, w_in=w_in, gdn_conv=gdn_conv, gdn_A_log=gdn_A_log,
                   gdn_dt_bias=gdn_dt_bias, gdn_norm=gdn_norm, ssd_conv=ssd_conv, ssd_conv_b=ssd_conv_b,
                   ssd_A_log=ssd_A_log, ssd_dt_bias=ssd_dt_bias, ssd_D=ssd_D, ssd_norm=ssd_norm, ret_norm=ret_norm,
                   w_out=w_out)
    ms = dict(pre_norm=m_pre_norm, post_norm=m_post_norm, w_in=m_w_in, gdn_conv=m_gdn_conv, gdn_A_log=m_gdn_A_log,
              gdn_dt_bias=m_gdn_dt_bias, gdn_norm=m_gdn_norm, ssd_conv=m_ssd_conv, ssd_conv_b=m_ssd_conv_b,
              ssd_A_log=m_ssd_A_log, ssd_dt_bias=m_ssd_dt_bias, ssd_D=m_ssd_D, ssd_norm=m_ssd_norm,
              ret_norm=m_ret_norm, w_out=m_w_out)
    vs = dict(pre_norm=v_pre_norm, post_norm=v_post_norm, w_in=v_w_in, gdn_conv=v_gdn_conv, gdn_A_log=v_gdn_A_log,
              gdn_dt_bias=v_gdn_dt_bias, gdn_norm=v_gdn_norm, ssd_conv=v_ssd_conv, ssd_conv_b=v_ssd_conv_b,
              ssd_A_log=v_ssd_A_log, ssd_dt_bias=v_ssd_dt_bias, ssd_D=v_ssd_D, ssd_norm=v_ssd_norm,
              ret_norm=v_ret_norm, w_out=v_w_out)
    names = list(weights)
    res = {}
    for nme in names:
        if nme == "w_in":
            res[nme] = _adamw_pairs_t(w_in, s_in, t_in, m_w_in, v_w_in, "adamw_w_in")
        elif nme == "w_out":
            res[nme] = _adamw_pairs(w_out, s_out, t_out, m_w_out, v_w_out, "adamw_w_out")
        else:
            res[nme] = _adamw(weights[nme], g_small[nme], ms[nme], vs[nme], "adamw_" + nme)
    return (loss, grad_x, *[res[n][0] for n in names], *[res[n][1] for n in names],
            *[res[n][2] for n in names], *[res[n][3] for n in names])
```

```python
import functools
import math

import jax
import jax.numpy as jnp
from jax import lax
from jax.experimental import pallas as pl
from jax.experimental.pallas import tpu as pltpu

F32 = jnp.float32
BF16 = jnp.bfloat16
HI = lax.Precision.HIGHEST

D_MODEL = 1024
DEPTH = 2
CH = 64
CONV_W = 4
EPS = 1e-6
GDN_H, GDN_D = 4, 128
SSD_H, SSD_P, SSD_N, SSD_G = 16, 64, 128, 2
SSD_W = SSD_H * SSD_P
RET_H, RET_D = 4, 128
ROPE_BASE = 10000.0
N_IN = 6680
NEG = -1e30

VMEM_LIMIT = 56 * 1024 * 1024


def _dot(a, b):
    return jnp.dot(a.astype(BF16), b.astype(BF16), preferred_element_type=F32)


def _dot_nt(a, b):
    return lax.dot_general(a.astype(BF16), b.astype(BF16), (((1,), (1,)), ((), ())), preferred_element_type=F32)


def _dot_tn(a, b):
    return lax.dot_general(a.astype(BF16), b.astype(BF16), (((0,), (0,)), ((), ())), preferred_element_type=F32)


def _split(a):
    hi = a.astype(BF16)
    return hi, (a - hi.astype(F32)).astype(BF16)


def _dot01l(m, v):
    vh, vl = _split(v)
    mb = m.astype(BF16)
    return jnp.dot(mb, vh, preferred_element_type=F32) + jnp.dot(mb, vl, preferred_element_type=F32)


def _dot01r(v, m):
    vh, vl = _split(v)
    mb = m.astype(BF16)
    return jnp.dot(vh, mb, preferred_element_type=F32) + jnp.dot(vl, mb, preferred_element_type=F32)


def _sigmoid(x):
    return jax.nn.sigmoid(x)


def _silu(x):
    return x * _sigmoid(x)


def _dsilu(x):
    s = _sigmoid(x)
    return s * (1.0 + x * (1.0 - s))


def _softplus(x):
    return jnp.maximum(x, 0.0) + jnp.log1p(jnp.exp(-jnp.abs(x)))


def _iota2(shape, dim):
    return lax.broadcasted_iota(jnp.int32, shape, dim)


def _chunk_tri(tb, upper=False):
    r = _iota2((tb, tb), 0)
    c = _iota2((tb, tb), 1)
    same = jnp.right_shift(r, 6) == jnp.right_shift(c, 6)
    return (same & ((c >= r) if upper else (c <= r))).astype(F32)


def _masks():
    r = _iota2((CH, CH), 0)
    c = _iota2((CH, CH), 1)
    return r >= c, r > c, (r == c).astype(F32)


def _put_lane(col, lane_idx, width=128):
    lane = _iota2((col.shape[0], width), 1)
    return jnp.where(lane == lane_idx, col, 0.0)


def _conv_taps(raw, halo8, tb):
    ext = jnp.concatenate([halo8, raw], axis=0)
    return [raw] + [pltpu.roll(ext, s, axis=0)[8:] for s in (1, 2, 3)]


def _conv_back(dpre, nxt8, tb):
    ext = jnp.concatenate([dpre, nxt8], axis=0)
    return [dpre] + [pltpu.roll(ext, tb + 8 - s, axis=0)[:tb] for s in (1, 2, 3)]


def _rms_fwd(o, w, n):
    r = lax.rsqrt(jnp.sum(o * o, axis=-1, keepdims=True) * (1.0 / n) + EPS)
    on = o * r
    return on, r, on * w


def _rms_bwd(dy, on, r, w, n):
    don = dy * w
    return r * (don - on * (jnp.sum(don * on, axis=-1, keepdims=True) * (1.0 / n))), dy * on


def _put_cols(v, g, gw):
    z = jnp.zeros_like(v)
    return jnp.concatenate([v, z] if g == 0 else [z, v], axis=1)


def _gdn_common(pg_ref, halo8, sm, cw, prm, tb, pre=None):
    raw = pg_ref[:, 0:1536]
    if pre is None:
        taps = _conv_taps(raw, halo8, tb)
        pre = taps[0] * cw[3:4, :] + taps[1] * cw[2:3, :] + taps[2] * cw[1:2, :] + taps[3] * cw[0:1, :]
    act = _silu(pre)
    beta = _sigmoid(sm)
    sp_in = sm + prm[1:2, :]
    g = -jnp.exp(prm[0:1, :]) * _softplus(sp_in)
    gc = _dot01l(_chunk_tri(tb), g)
    return raw, pre, act, beta, sp_in, g, gc


_NN = (((2,), (1,)), ((0,), (0,)))
_NT = (((2,), (2,)), ((0,), (0,)))
_TN = (((1,), (1,)), ((0,), (0,)))


def _bdot(a, b, dn):
    return lax.dot_general(a.astype(BF16), b.astype(BF16), dn, preferred_element_type=F32)


def _binv_unit_lower(a, eye):
    r = _iota2((CH, CH), 0)
    c = _iota2((CH, CH), 1)
    d = eye - jnp.where((jnp.right_shift(r, 1) == jnp.right_shift(c, 1)), a, 0.0)
    ab = a.astype(BF16)
    zero = jnp.zeros((), BF16)
    for lb in range(1, 6):
        same = jnp.right_shift(r, lb + 1) == jnp.right_shift(c, lb + 1)
        low = (jnp.bitwise_and(jnp.right_shift(r, lb), 1) == 1) & (jnp.bitwise_and(jnp.right_shift(c, lb), 1) == 0)
        db = d.astype(BF16)
        t = _bdot(jnp.where(same & low, ab, zero), db, _NN)
        d = d - _bdot(db, t, _NN)
    return d


def _rsum(v):
    return jnp.sum(v, axis=-1, keepdims=True)


def _gdn_batch(act, beta, gc, gct, eg_all, ncb, masks):
    causal, strict, _ = masks

    def st(fn):
        return jnp.stack([fn(c, h, slice(c * CH, (c + 1) * CH)) for c in range(ncb) for h in range(GDN_H)])

    qr = st(lambda c, h, r: act[r, h * 128:(h + 1) * 128])
    kr = st(lambda c, h, r: act[r, 512 + h * 128:512 + (h + 1) * 128])
    vh = st(lambda c, h, r: act[r, 1024 + h * 128:1024 + (h + 1) * 128])
    bh = st(lambda c, h, r: beta[r, h:h + 1])
    gcol = st(lambda c, h, r: gc[r, 4 + h:5 + h])
    grow = st(lambda c, h, r: gct[4 + h:5 + h, r])
    eg = st(lambda c, h, r: eg_all[r, 4 + h:5 + h])
    glast = st(lambda c, h, r: gc[(c + 1) * CH - 1:(c + 1) * CH, 4 + h:5 + h])
    rq = lax.rsqrt(_rsum(qr * qr) + EPS)
    rk = lax.rsqrt(_rsum(kr * kr) + EPS)
    qn = qr * rq
    kh = kr * rk
    qh = qn * (GDN_D ** -0.5)
    decay = jnp.exp(jnp.where(causal, gcol - grow, NEG))
    kb = kh * bh
    kd_scale = jnp.exp(glast - gcol)
    return dict(qn=qn, rq=rq, kh=kh, rk=rk, qh=qh, vh=vh, bh=bh, eg=eg, decay=decay, kb=kb, vb=vh * bh, kg=kb * eg,
                qg=qh * eg, kd_scale=kd_scale, kdec=kh * kd_scale, egl=jnp.exp(glast),
                a=jnp.where(strict, _bdot(kb, kh, _NT) * decay, 0.0), attn=_bdot(qh, kh, _NT) * decay)


def _make_gdn_fwd(seq, tb):
    ncb = tb // CH
    nb = seq // tb
    n = ncb * GDN_H

    def body(pg_ref, sm_ref, cw_ref, prm_ref, nw_ref, oa_ref, st_ref, ti_ref, uw_ref, pre_ref, s_scr, halo_scr):
        @pl.when(pl.program_id(0) == 0)
        def _():
            s_scr[...] = jnp.zeros_like(s_scr)
            halo_scr[...] = jnp.zeros_like(halo_scr)

        masks = _masks()
        sm = sm_ref[...]
        raw, pre, act, beta, _, _, gc = _gdn_common(pg_ref, halo_scr[...], sm, cw_ref[...], prm_ref[...], tb)
        halo_scr[...] = raw[tb - 8:tb, :]
        pre_ref[...] = pre
        d = _gdn_batch(act, beta, gc, gc.T, jnp.exp(gc), ncb, masks)
        t = _binv_unit_lower(d["a"], masks[2])
        sol = _bdot(t, jnp.concatenate([d["vb"], d["kg"]], axis=2), _NN)
        ti_ref[...] = t.reshape(ncb, GDN_H, CH, CH)
        uw_ref[...] = sol.reshape(ncb, GDN_H, CH, 256)
        u, w = sol[:, :, :128], sol[:, :, 128:]
        vns = []
        for c in range(ncb):
            bs = slice(c * GDN_H, (c + 1) * GDN_H)
            s = s_scr[...]
            st_ref[c] = s
            vn = u[bs] - _bdot(w[bs], s, _NN)
            s_scr[...] = s * d["egl"][bs] + _bdot(d["kdec"][bs], vn, _TN)
            vns.append(vn)
        v_new = jnp.concatenate(vns, axis=0)
        s_prev = st_ref[...].reshape(n, 128, 128)
        o = _bdot(d["qg"], s_prev, _NN) + _bdot(d["attn"], v_new, _NN)
        _, _, y = _rms_fwd(o, nw_ref[0:1, :], GDN_D)
        for c in range(ncb):
            rows = slice(c * CH, (c + 1) * CH)
            for h in range(GDN_H):
                z = pg_ref[rows, 1536 + h * 128:1536 + (h + 1) * 128]
                oa_ref[rows, h * 128:(h + 1) * 128] = (y[c * GDN_H + h] * _silu(z)).astype(oa_ref.dtype)

    def call(pg, sm, cw, prm, nw, comm=None, comm_args=()):
        blk4 = lambda i: (i, 0, 0, 0)
        cx = _exchange_specs(comm)
        return pl.pallas_call(
            _with_exchange(body, comm, 5, 5, nb),
            grid=(nb,),
            in_specs=[
                pl.BlockSpec((tb, 2048), lambda i: (i, 0)),
                pl.BlockSpec((tb, 128), lambda i: (i, 0)),
                pl.BlockSpec((8, 1536), lambda i: (0, 0)),
                pl.BlockSpec((8, 128), lambda i: (0, 0)),
                pl.BlockSpec((8, 128), lambda i: (0, 0)),
            ] + cx["specs"],
            out_specs=[
                pl.BlockSpec((tb, 512), lambda i: (i, 0)),
                pl.BlockSpec((ncb, GDN_H, 128, 128), blk4),
                pl.BlockSpec((ncb, GDN_H, CH, CH), blk4),
                pl.BlockSpec((ncb, GDN_H, CH, 256), blk4),
                pl.BlockSpec((tb, 1536), lambda i: (i, 0)),
            ] + cx["specs"],
            out_shape=[
                jax.ShapeDtypeStruct((seq, 512), BF16),
                jax.ShapeDtypeStruct((seq // CH, GDN_H, 128, 128), F32),
                jax.ShapeDtypeStruct((seq // CH, GDN_H, CH, CH), F32),
                jax.ShapeDtypeStruct((seq // CH, GDN_H, CH, 256), F32),
                jax.ShapeDtypeStruct((seq, 1536), F32),
            ] + cx["out_shape"],
            scratch_shapes=[pltpu.VMEM((GDN_H, 128, 128), F32), pltpu.VMEM((8, 1536), F32)] + cx["scratch"],
            compiler_params=pltpu.CompilerParams(dimension_semantics=("arbitrary",), vmem_limit_bytes=VMEM_LIMIT,
                                                 has_side_effects=comm is not None),
            name="gdn_fwd" + cx["tag"],
        )(pg, sm, cw, prm, nw, *comm_args)

    return call


def _make_gdn_bwd(seq, tb):
    ncb = tb // CH
    nb = seq // tb
    hb = tb // 8
    n = ncb * GDN_H

    def body(pg_ref, pre_ref, sm_ref, cw_ref, prm_ref, nw_ref, st_ref, ti_ref, uw_ref, doa_ref,
             dpg_ref, dsm_ref, dcw_ref, dprm_ref, dnw_ref, ds_scr, nxt_scr):
        i = pl.program_id(0)

        @pl.when(i == 0)
        def _():
            ds_scr[...] = jnp.zeros_like(ds_scr)
            nxt_scr[...] = jnp.zeros_like(nxt_scr)
            dcw_ref[...] = jnp.zeros_like(dcw_ref)
            dprm_ref[...] = jnp.zeros_like(dprm_ref)
            dnw_ref[...] = jnp.zeros_like(dnw_ref)

        masks = _masks()
        strict = masks[1]
        sm = sm_ref[...]
        cw = cw_ref[...]
        prm = prm_ref[...]
        raw, pre, act, beta, sp_in, g, gc = _gdn_common(pg_ref, None, sm, cw, prm, tb, pre=pre_ref[...])
        nw = nw_ref[0:1, :]
        row_id = _iota2((CH, 1), 0)
        d = _gdn_batch(act, beta, gc, gc.T, jnp.exp(gc), ncb, masks)
        t = ti_ref[...].reshape(n, CH, CH)
        sol = uw_ref[...].reshape(n, CH, 256)
        u, w = sol[:, :, :128], sol[:, :, 128:]
        s_prev = st_ref[...].reshape(n, 128, 128)
        v_new = u - _bdot(w, s_prev, _NN)
        o = _bdot(d["qg"], s_prev, _NN) + _bdot(d["attn"], v_new, _NN)

        pairs = [(c, h) for c in range(ncb) for h in range(GDN_H)]
        z = jnp.stack([pg_ref[c * CH:(c + 1) * CH, 1536 + h * 128:1536 + (h + 1) * 128] for c, h in pairs])
        doa = jnp.stack([doa_ref[c * CH:(c + 1) * CH, h * 128:(h + 1) * 128] for c, h in pairs])
        on, r, y = _rms_fwd(o, nw, GDN_D)
        dz = doa * y * _dsilu(z)
        do, dnw_rows = _rms_bwd(doa * _silu(z), on, r, nw, GDN_D)
        dnw_acc = jnp.sum(jnp.sum(dnw_rows, axis=0), axis=0, keepdims=True)

        dvn_in = _bdot(d["attn"], do, _TN)
        qgtdo = _bdot(d["qg"], do, _TN)
        dvn_l, dkdec_l, dgl_l = [None] * ncb, [None] * ncb, [None] * ncb
        for c in reversed(range(ncb)):
            bs = slice(c * GDN_H, (c + 1) * GDN_H)
            dsn = ds_scr[...]
            dvn_c = dvn_in[bs] + _bdot(d["kdec"][bs], dsn, _NN)
            ds_scr[...] = d["egl"][bs] * dsn + qgtdo[bs] - _bdot(w[bs], dvn_c, _TN)
            dvn_l[c] = dvn_c
            dkdec_l[c] = _bdot(v_new[bs], dsn, _NT)
            dgl_l[c] = d["egl"][bs] * jnp.sum(_rsum(s_prev[bs] * dsn), axis=1, keepdims=True)
        dvn = jnp.concatenate(dvn_l, axis=0)
        dkdec = jnp.concatenate(dkdec_l, axis=0)
        dglast = jnp.concatenate(dgl_l, axis=0)

        dqg = _bdot(do, s_prev, _NT)
        dattn = _bdot(do, v_new, _NT)
        dw = -_bdot(dvn, s_prev, _NT)
        drhs = _bdot(t, jnp.concatenate([dvn, dw], axis=2), _TN)
        dvb, dkg = drhs[:, :, :128], drhs[:, :, 128:]
        da = jnp.where(strict, -(_bdot(dvb, u, _NT) + _bdot(dkg, w, _NT)), 0.0)
        dp = da * d["decay"]
        dq_m = dattn * d["decay"]
        m = da * d["a"] + dattn * d["attn"]
        upper_tri = jnp.broadcast_to((_iota2((CH, CH), 1) >= _iota2((CH, CH), 0)).astype(BF16), (n, CH, CH))
        dg_in = _rsum(jnp.where(strict, _bdot(upper_tri, m, _NN), 0.0))
        dkb = _bdot(dp, d["kh"], _NN) + dkg * d["eg"]
        kdk_row = _rsum(dkdec * d["kdec"])
        dk = _bdot(dp, d["kb"], _TN) + _bdot(dq_m, d["qh"], _TN) + dkdec * d["kd_scale"] + dkb * d["bh"]
        dq = _bdot(dq_m, d["kh"], _NN) + dqg * d["eg"]
        dglast = dglast + jnp.sum(kdk_row, axis=1, keepdims=True)
        dgcol = (_rsum(dqg * d["qg"]) + _rsum(dkg * d["kg"]) - kdk_row + jnp.where(row_id == CH - 1, dglast, 0.0))
        dbeta = _rsum(dkb * d["kh"]) + _rsum(dvb * d["vh"])
        dn = dq * (GDN_D ** -0.5)
        dact_q = d["rq"] * (dn - d["qn"] * _rsum(dn * d["qn"]))
        dact_k = d["rk"] * (dk - d["kh"] * _rsum(dk * d["kh"]))
        dact_v = dvb * d["bh"]

        def lanes(v, lane0):
            return jnp.concatenate(
                [sum(_put_lane(v[c * GDN_H + h], lane0 + h) for h in range(GDN_H)) for c in range(ncb)], axis=0)

        def tokens(v):
            return jnp.concatenate(
                [jnp.concatenate([v[c * GDN_H + h] for h in range(GDN_H)], axis=1) for c in range(ncb)], axis=0)

        dbeta_all = lanes(dbeta, 0)
        dg = _dot01l(_chunk_tri(tb, upper=True), lanes(dgcol, 4)) + lanes(dg_in, 4)
        neg_ea = -jnp.exp(prm[0:1, :])
        da_raw = dg * neg_ea * _sigmoid(sp_in)
        db_raw = dbeta_all * beta * (1.0 - beta)
        dsm_ref[...] = (da_raw + db_raw).astype(dsm_ref.dtype)
        lane8 = _iota2((8, 128), 1)
        sub8 = _iota2((8, 128), 0)
        dalog = jnp.sum(dg * g, axis=0, keepdims=True)
        ddtb = jnp.sum(da_raw, axis=0, keepdims=True)
        dprm_ref[...] += jnp.where(sub8 == 0, dalog, 0.0) + jnp.where(sub8 == 1, ddtb, 0.0)
        dnw_ref[...] += jnp.where(sub8 == 0, dnw_acc, 0.0)

        dact = jnp.concatenate([tokens(dact_q), tokens(dact_k), tokens(dact_v)], axis=1)
        dpre = dact * _dsilu(pre)
        back = _conv_back(dpre, nxt_scr[...], tb)
        nxt_scr[...] = dpre[0:8, :]
        draw = back[0] * cw[3:4, :] + back[1] * cw[2:3, :] + back[2] * cw[1:2, :] + back[3] * cw[0:1, :]
        dpg_ref[:, 0:1536] = draw.astype(dpg_ref.dtype)
        dpg_ref[:, 1536:2048] = tokens(dz).astype(dpg_ref.dtype)
        sub_c = _iota2((8, 1536), 0)
        dcw_new = jnp.zeros((8, 1536), F32)
        for s_ in range(CONV_W):
            dcw_new = dcw_new + jnp.where(sub_c == 3 - s_, jnp.sum(back[s_] * raw, axis=0, keepdims=True), 0.0)
        dcw_ref[...] += dcw_new

    def call(pg, pre, sm, cw, prm, nw, st, ti, uw, doa, comm=None, comm_args=()):
        rev = lambda i: (nb - 1 - i, 0)
        const = lambda i: (0, 0)
        cx = _exchange_specs(comm)
        return pl.pallas_call(
            _with_exchange(body, comm, 10, 5, nb),
            grid=(nb,),
            in_specs=[
                pl.BlockSpec((tb, 2048), rev),
                pl.BlockSpec((tb, 1536), rev),
                pl.BlockSpec((tb, 128), rev),
                pl.BlockSpec((8, 1536), const),
                pl.BlockSpec((8, 128), const),
                pl.BlockSpec((8, 128), const),
                pl.BlockSpec((ncb, GDN_H, 128, 128), lambda i: (nb - 1 - i, 0, 0, 0)),
                pl.BlockSpec((ncb, GDN_H, CH, CH), lambda i: (nb - 1 - i, 0, 0, 0)),
                pl.BlockSpec((ncb, GDN_H, CH, 256), lambda i: (nb - 1 - i, 0, 0, 0)),
                pl.BlockSpec((tb, 512), rev),
            ] + cx["specs"],
            out_specs=[
                pl.BlockSpec((tb, 2048), rev),
                pl.BlockSpec((tb, 128), rev),
                pl.BlockSpec((8, 1536), const),
                pl.BlockSpec((8, 128), const),
                pl.BlockSpec((8, 128), const),
            ] + cx["specs"],
            out_shape=[
                jax.ShapeDtypeStruct((seq, 2048), BF16),
                jax.ShapeDtypeStruct((seq, 128), BF16),
                jax.ShapeDtypeStruct((8, 1536), F32),
                jax.ShapeDtypeStruct((8, 128), F32),
                jax.ShapeDtypeStruct((8, 128), F32),
            ] + cx["out_shape"],
            scratch_shapes=[pltpu.VMEM((GDN_H, 128, 128), F32), pltpu.VMEM((8, 1536), F32)] + cx["scratch"],
            compiler_params=pltpu.CompilerParams(dimension_semantics=("arbitrary",), vmem_limit_bytes=VMEM_LIMIT,
                                                 has_side_effects=comm is not None),
            name="gdn_bwd" + cx["tag"],
        )(pg, pre, sm, cw, prm, nw, st, ti, uw, doa, *comm_args)

    return call


def _expand_mat():
    r = _iota2((128, SSD_W), 0)
    c = _iota2((128, SSD_W), 1)
    return (jnp.right_shift(c, 6) == r).astype(F32)


def _reduce_heads(v, e):
    vh, vl = _split(v)
    eb = e.astype(BF16)
    nt = (((1,), (1,)), ((), ()))
    return (lax.dot_general(vh, eb, nt, preferred_element_type=F32)
            + lax.dot_general(vl, eb, nt, preferred_element_type=F32))


def _reduce_heads1(v, e):
    nt = (((1,), (1,)), ((), ()))
    return lax.dot_general(v.astype(BF16), e.astype(BF16), nt, preferred_element_type=F32)


def _row8(v):
    return jnp.broadcast_to(v, (8, v.shape[1]))


def _ssd_common(ps_ref, halo8, ss, cw, cb, prm, tb, pre=None):
    raw = ps_ref[:, 0:1536]
    taps = None
    if pre is None:
        taps = _conv_taps(raw, halo8, tb)
        pre = taps[0] * cw[3:4, :] + taps[1] * cw[2:3, :] + taps[2] * cw[1:2, :] + taps[3] * cw[0:1, :] + cb[0:1, :]
    act = _silu(pre)
    dt_in = ss + prm[1:2, :]
    dt = _softplus(dt_in)
    a = dt * (-jnp.exp(prm[0:1, :]))
    acum = _dot01l(_chunk_tri(tb), a)
    e = _expand_mat()
    dt_e = _dot01r(dt, e)
    xdt = act[:, 0:SSD_W] * dt_e
    ea_e = _dot01r(jnp.exp(acum), e)
    d_e = _dot01r(_row8(prm[2:3, :]), e)[0:1, :]
    return raw, taps, pre, act, dt_in, dt, a, acum, e, dt_e, xdt, ea_e, d_e


def _ssd_chunk(act, acum, act_t, e, c):
    r0 = c * CH
    rows = slice(r0, r0 + CH)
    alast = acum[r0 + CH - 1:r0 + CH, :]
    wdec = jnp.exp(alast - acum[rows, :])
    wd_e = _dot01r(wdec, e)
    eal_e = _dot01r(_row8(jnp.exp(alast)), e)[0:1, :]
    return rows, wd_e, eal_e


def _ssd_lmat(acum, act_t, c, h, causal):
    r0 = c * CH
    acol = acum[r0:r0 + CH, h:h + 1]
    arow = act_t[h:h + 1, r0:r0 + CH]
    return jnp.exp(jnp.where(causal, acol - arow, NEG))


def _make_ssd_fwd(seq, tb):
    ncb = tb // CH
    nb = seq // tb
    hg = SSD_H // SSD_G
    gw = SSD_W // SSD_G

    def body(ps_ref, ss_ref, cw_ref, cb_ref, prm_ref, nw_ref, ob_ref, st_ref, pre_ref, y_ref, hs_scr, halo_scr):
        @pl.when(pl.program_id(0) == 0)
        def _():
            hs_scr[...] = jnp.zeros_like(hs_scr)
            halo_scr[...] = jnp.zeros_like(halo_scr)

        causal, _, _ = _masks()
        (raw, _, pre, act, _, _, _, acum, e, _, xdt, ea_e, d_e) = _ssd_common(
            ps_ref, halo_scr[...], ss_ref[...], cw_ref[...], cb_ref[...], prm_ref[...], tb)
        halo_scr[...] = raw[tb - 8:tb, :]
        pre_ref[...] = pre
        act_t = acum.T
        nw = nw_ref[0:1, :]
        for c in range(ncb):
            rows, wd_e, eal_e = _ssd_chunk(act, acum, act_t, e, c)
            st_ref[c] = hs_scr[...]
            ys = []
            for g in range(SSD_G):
                gc_ = slice(g * gw, (g + 1) * gw)
                bg = act[rows, SSD_W + g * 128:SSD_W + (g + 1) * 128]
                cg = act[rows, SSD_W + 256 + g * 128:SSD_W + 256 + (g + 1) * 128]
                cbm = _dot_nt(cg, bg)
                hs = hs_scr[:, gc_]
                yin = _dot(cg, hs)
                yh = []
                for hh in range(hg):
                    h = g * hg + hh
                    lm = _ssd_lmat(acum, act_t, c, h, causal)
                    yh.append(_dot(cbm * lm, xdt[rows, h * SSD_P:(h + 1) * SSD_P]))
                ys.append(jnp.concatenate(yh, axis=1) + yin * ea_e[rows, gc_])
                hs_scr[:, gc_] = hs * eal_e[:, gc_] + _dot_tn(bg, xdt[rows, gc_] * wd_e[:, gc_])
            y = jnp.concatenate(ys, axis=1) + act[rows, 0:SSD_W] * d_e
            y_ref[rows, :] = y
            yz = y * _silu(ps_ref[rows, 1536:2560])
            outs = [_rms_fwd(yz[:, g * gw:(g + 1) * gw], nw[:, g * gw:(g + 1) * gw], gw)[2] for g in range(SSD_G)]
            ob_ref[rows, :] = jnp.concatenate(outs, axis=1).astype(ob_ref.dtype)

    def call(ps, ss, cw, cb, prm, nw):
        const = lambda i: (0, 0)
        return pl.pallas_call(
            body,
            grid=(nb,),
            in_specs=[
                pl.BlockSpec((tb, 2560), lambda i: (i, 0)),
                pl.BlockSpec((tb, 128), lambda i: (i, 0)),
                pl.BlockSpec((8, 1536), const),
                pl.BlockSpec((8, 1536), const),
                pl.BlockSpec((8, 128), const),
                pl.BlockSpec((8, SSD_W), const),
            ],
            out_specs=[
                pl.BlockSpec((tb, SSD_W), lambda i: (i, 0)),
                pl.BlockSpec((ncb, SSD_N, SSD_W), lambda i: (i, 0, 0)),
                pl.BlockSpec((tb, 1536), lambda i: (i, 0)),
                pl.BlockSpec((tb, SSD_W), lambda i: (i, 0)),
            ],
            out_shape=[
                jax.ShapeDtypeStruct((seq, SSD_W), BF16),
                jax.ShapeDtypeStruct((seq // CH, SSD_N, SSD_W), F32),
                jax.ShapeDtypeStruct((seq, 1536), F32),
                jax.ShapeDtypeStruct((seq, SSD_W), F32),
            ],
            scratch_shapes=[pltpu.VMEM((SSD_N, SSD_W), F32), pltpu.VMEM((8, 1536), F32)],
            compiler_params=pltpu.CompilerParams(dimension_semantics=("arbitrary",), vmem_limit_bytes=VMEM_LIMIT),
            name="ssd_fwd",
        )(ps, ss, cw, cb, prm, nw)

    return call


def _make_ssd_bwd(seq, tb):
    ncb = tb // CH
    nb = seq // tb
    hb = tb // 8
    hg = SSD_H // SSD_G
    gw = SSD_W // SSD_G

    def body(ps_ref, pre_ref, y_ref, ss_ref, cw_ref, cb_ref, prm_ref, nw_ref, st_ref, dob_ref,
             dps_ref, dss_ref, dcw_ref, dcb_ref, dprm_ref, dnw_ref, dhs_scr, nxt_scr):
        i = pl.program_id(0)

        @pl.when(i == 0)
        def _():
            dhs_scr[...] = jnp.zeros_like(dhs_scr)
            nxt_scr[...] = jnp.zeros_like(nxt_scr)
            dcw_ref[...] = jnp.zeros_like(dcw_ref)
            dcb_ref[...] = jnp.zeros_like(dcb_ref)
            dprm_ref[...] = jnp.zeros_like(dprm_ref)
            dnw_ref[...] = jnp.zeros_like(dnw_ref)

        causal, _, _ = _masks()
        cw = cw_ref[...]
        prm = prm_ref[...]
        (raw, _, pre, act, dt_in, dt, a, acum, e, dt_e, xdt, ea_e, d_e) = _ssd_common(
            ps_ref, None, ss_ref[...], cw, cb_ref[...], prm, tb, pre=pre_ref[...])
        act_t = acum.T
        nw = nw_ref[0:1, :]
        row_id = _iota2((CH, 1), 0)

        dx_l, db_l, dc_l, dz_l, dacum_l, ddt_l, da_in_l = ([None] * ncb for _ in range(7))
        upper_tri = (_iota2((CH, CH), 1) >= _iota2((CH, CH), 0)).astype(F32)
        below = jnp.bitwise_and(_iota2((CH, gw), 1), CH - 1) < _iota2((CH, gw), 0)
        dnw_acc = jnp.zeros((1, SSD_W), F32)
        dd_acc = jnp.zeros((1, SSD_W), F32)

        for c in reversed(range(ncb)):
            rows, wd_e, eal_e = _ssd_chunk(act, acum, act_t, e, c)
            xc = act[rows, 0:SSD_W]
            z = ps_ref[rows, 1536:2560]
            dob = dob_ref[rows, :]
            sz = _silu(z)
            dy_g, dz_g, zacc_g, dxdt_g, dal_g, db_g, dc_g, da_in_g = [], [], [], [], [], [], [], []
            for g in range(SSD_G):
                gc_ = slice(g * gw, (g + 1) * gw)
                bg = act[rows, SSD_W + g * 128:SSD_W + (g + 1) * 128]
                cg = act[rows, SSD_W + 256 + g * 128:SSD_W + 256 + (g + 1) * 128]
                cbm = _dot_nt(cg, bg)
                hs = st_ref[c, :, gc_]
                yin = _dot(cg, hs)
                lmats = [_ssd_lmat(acum, act_t, c, g * hg + hh, causal) for hh in range(hg)]
                ea_g = ea_e[rows, gc_]
                y = y_ref[rows, gc_]
                yz = y * sz[:, gc_]
                on, r, _ = _rms_fwd(yz, nw[:, gc_], gw)
                dyz, dnw_rows = _rms_bwd(dob[:, gc_], on, r, nw[:, gc_], gw)
                dnw_acc = dnw_acc + _put_cols(jnp.sum(dnw_rows, axis=0, keepdims=True), g, gw)
                dy = dyz * sz[:, gc_]
                dz_g.append(dyz * y * _dsilu(z[:, gc_]))
                dd_acc = dd_acc + _put_cols(jnp.sum(dy * xc[:, gc_], axis=0, keepdims=True), g, gw)
                dhs_n = dhs_scr[:, gc_]
                dyin = dy * ea_g
                dcg = _dot_nt(dyin, hs)
                xw = xdt[rows, gc_] * wd_e[:, gc_]
                dbg = _dot_nt(xw, dhs_n)
                dxw = _dot(bg, dhs_n)
                dhs_scr[:, gc_] = dhs_n * eal_e[:, gc_] + _dot_tn(cg, dyin)
                dal_g.append(jnp.sum(hs * dhs_n, axis=0, keepdims=True) * eal_e[:, gc_]
                             + jnp.sum(dxw * xw, axis=0, keepdims=True))
                dxi, ms, dcbm = [], [], jnp.zeros((CH, CH), F32)
                for hh in range(hg):
                    h = g * hg + hh
                    hc = slice(hh * SSD_P, (hh + 1) * SSD_P)
                    dyh = dy[:, hc]
                    lm = cbm * lmats[hh]
                    dxi.append(_dot_tn(lm, dyh))
                    dlm = _dot_nt(dyh, xdt[rows, h * SSD_P:(h + 1) * SSD_P])
                    ms.append(dlm * lm)
                    dcbm = dcbm + dlm * lmats[hh]
                dx_intra = jnp.concatenate(dxi, axis=1)
                ncat = _dot(upper_tri, jnp.concatenate(ms, axis=1))
                da_in_g.append(jnp.where(below, ncat, 0.0))
                zacc_g.append(dy * yin * ea_g - dxw * xw)
                dxdt_g.append(dx_intra + dxw * wd_e[:, gc_])
                dy_g.append(dy)
                db_g.append(dbg + _dot_tn(dcbm, cg))
                dc_g.append(dcg + _dot(dcbm, bg))
            dy = jnp.concatenate(dy_g, axis=1)
            dxdt = jnp.concatenate(dxdt_g, axis=1)
            dx_l[c] = dxdt * dt_e[rows, :] + dy * d_e
            db_l[c] = jnp.concatenate(db_g, axis=1)
            dc_l[c] = jnp.concatenate(dc_g, axis=1)
            dz_l[c] = jnp.concatenate(dz_g, axis=1)
            ddt_l[c] = _reduce_heads1(dxdt * xc, e)
            dalast = _reduce_heads(_row8(jnp.concatenate(dal_g, axis=1)), e)[0:1, :]
            dacum_l[c] = _reduce_heads(jnp.concatenate(zacc_g, axis=1), e) + jnp.where(row_id == CH - 1, dalast, 0.0)
            da_in_l[c] = _reduce_heads1(jnp.concatenate(da_in_g, axis=1), e)

        dacum_all = jnp.concatenate(dacum_l, axis=0)
        da = _dot01l(_chunk_tri(tb, upper=True), dacum_all) + jnp.concatenate(da_in_l, axis=0)
        neg_ea = -jnp.exp(prm[0:1, :])
        ddt = jnp.concatenate(ddt_l, axis=0) + da * neg_ea
        ddt_in = ddt * _sigmoid(dt_in)
        dss_ref[...] = ddt_in.astype(dss_ref.dtype)
        sub8 = _iota2((8, 128), 0)
        dalog = jnp.sum(da * a, axis=0, keepdims=True)
        ddtb = jnp.sum(ddt_in, axis=0, keepdims=True)
        dd = _reduce_heads(_row8(dd_acc), e)[0:1, :]
        dprm_ref[...] += (jnp.where(sub8 == 0, dalog, 0.0) + jnp.where(sub8 == 1, ddtb, 0.0)
                          + jnp.where(sub8 == 2, dd, 0.0))
        dnw_ref[...] += jnp.where(_iota2((8, SSD_W), 0) == 0, dnw_acc, 0.0)

        dact = jnp.concatenate([jnp.concatenate(dx_l, axis=0), jnp.concatenate(db_l, axis=0),
                                jnp.concatenate(dc_l, axis=0)], axis=1)
        dpre = dact * _dsilu(pre)
        back = _conv_back(dpre, nxt_scr[...], tb)
        nxt_scr[...] = dpre[0:8, :]
        draw = back[0] * cw[3:4, :] + back[1] * cw[2:3, :] + back[2] * cw[1:2, :] + back[3] * cw[0:1, :]
        dps_ref[:, 0:1536] = draw.astype(dps_ref.dtype)
        dps_ref[:, 1536:2560] = jnp.concatenate(dz_l, axis=0).astype(dps_ref.dtype)
        sub_c = _iota2((8, 1536), 0)
        dcw_new = jnp.zeros((8, 1536), F32)
        for s_ in range(CONV_W):
            dcw_new = dcw_new + jnp.where(sub_c == 3 - s_, jnp.sum(back[s_] * raw, axis=0, keepdims=True), 0.0)
        dcw_ref[...] += dcw_new
        dcb_ref[...] += jnp.where(sub_c == 0, jnp.sum(dpre, axis=0, keepdims=True), 0.0)

    def call(ps, pre, y, ss, cw, cb, prm, nw, st, dob, comm=None, comm_args=()):
        rev = lambda i: (nb - 1 - i, 0)
        const = lambda i: (0, 0)
        cx = _exchange_specs(comm)
        return pl.pallas_call(
            _with_exchange(body, comm, 10, 6, nb),
            grid=(nb,),
            in_specs=[
                pl.BlockSpec((tb, 2560), rev),
                pl.BlockSpec((tb, 1536), rev),
                pl.BlockSpec((tb, SSD_W), rev),
                pl.BlockSpec((tb, 128), rev),
                pl.BlockSpec((8, 1536), const),
                pl.BlockSpec((8, 1536), const),
                pl.BlockSpec((8, 128), const),
                pl.BlockSpec((8, SSD_W), const),
                pl.BlockSpec((ncb, SSD_N, SSD_W), lambda i: (nb - 1 - i, 0, 0)),
                pl.BlockSpec((tb, SSD_W), rev),
            ] + cx["specs"],
            out_specs=[
                pl.BlockSpec((tb, 2560), rev),
                pl.BlockSpec((tb, 128), rev),
                pl.BlockSpec((8, 1536), const),
                pl.BlockSpec((8, 1536), const),
                pl.BlockSpec((8, 128), const),
                pl.BlockSpec((8, SSD_W), const),
            ] + cx["specs"],
            out_shape=[
                jax.ShapeDtypeStruct((seq, 2560), BF16),
                jax.ShapeDtypeStruct((seq, 128), BF16),
                jax.ShapeDtypeStruct((8, 1536), F32),
                jax.ShapeDtypeStruct((8, 1536), F32),
                jax.ShapeDtypeStruct((8, 128), F32),
                jax.ShapeDtypeStruct((8, SSD_W), F32),
            ] + cx["out_shape"],
            scratch_shapes=[pltpu.VMEM((SSD_N, SSD_W), F32), pltpu.VMEM((8, 1536), F32)] + cx["scratch"],
            compiler_params=pltpu.CompilerParams(dimension_semantics=("arbitrary",), vmem_limit_bytes=VMEM_LIMIT,
                                                 has_side_effects=comm is not None),
            name="ssd_bwd" + cx["tag"],
        )(ps, pre, y, ss, cw, cb, prm, nw, st, dob, *comm_args)

    return call


def _ret_consts(h):
    lg = math.log(1.0 - 2.0 ** (-5.0 - h))
    r = _iota2((CH, CH), 0)
    c = _iota2((CH, CH), 1)
    rel = (r - c).astype(F32)
    dmat = jnp.where(r >= c, jnp.exp(jnp.maximum(rel, 0.0) * lg), 0.0)
    idx = _iota2((CH, 1), 0).astype(F32)
    qdec = jnp.exp((idx + 1.0) * lg)
    kdec = jnp.exp((CH - 1.0 - idx) * lg)
    cdec = math.exp(CH * lg)
    return dmat, qdec, kdec, cdec


def _ret_batch(pr_ref, cc_ref, ss_ref, ncb):
    pairs = [(c, h) for c in range(ncb) for h in range(RET_H)]

    def st(off):
        return jnp.stack([pr_ref[c * CH:(c + 1) * CH, off + h * 128:off + (h + 1) * 128] for c, h in pairs])

    cc = jnp.stack([cc_ref[c * CH:(c + 1) * CH, :] for c, _ in pairs])
    ss = jnp.stack([ss_ref[c * CH:(c + 1) * CH, :] for c, _ in pairs])
    consts = [_ret_consts(h) for h in range(RET_H)]
    dmat = jnp.stack([consts[h][0] for _, h in pairs])
    qdec = jnp.stack([consts[h][1] for _, h in pairs])
    kdec = jnp.stack([consts[h][2] for _, h in pairs])
    cdec = jnp.stack([jnp.full((1, 1), consts[h][3], F32) for h in range(RET_H)])
    q = _rot(st(0), cc, ss)
    k = _rot(st(512), cc, ss) * (RET_D ** -0.5)
    return dict(q=q, k=k, v=st(1024), z=st(1536), cc=cc, ss=ss, dmat=dmat, qdec=qdec, kdec=kdec, cdec=cdec,
                s=_bdot(q, k, _NT) * dmat)


def _rot(t, cc, ss):
    return t * cc + pltpu.roll(t, 64, axis=t.ndim - 1) * ss


def _rot_bwd(d, cc, ss):
    return d * cc + pltpu.roll(d * ss, 64, axis=d.ndim - 1)


def _make_ret_fwd(seq, tb):
    ncb = tb // CH
    nb = seq // tb

    def body(pr_ref, cc_ref, ss_ref, nw_ref, oc_ref, st_ref, r_scr):
        @pl.when(pl.program_id(0) == 0)
        def _():
            r_scr[...] = jnp.zeros_like(r_scr)

        d = _ret_batch(pr_ref, cc_ref, ss_ref, ncb)
        kd = d["k"] * d["kdec"]
        for c in range(ncb):
            bs = slice(c * RET_H, (c + 1) * RET_H)
            rs = r_scr[...]
            st_ref[c] = rs
            r_scr[...] = rs * d["cdec"] + _bdot(kd[bs], d["v"][bs], _TN)
        r_prev = st_ref[...].reshape(ncb * RET_H, 128, 128)
        o = _bdot(d["s"], d["v"], _NN) + _bdot(d["q"], r_prev, _NN) * d["qdec"]
        _, _, y = _rms_fwd(o, nw_ref[0:1, :], RET_D)
        out = y * _silu(d["z"])
        for c in range(ncb):
            for h in range(RET_H):
                oc_ref[c * CH:(c + 1) * CH, h * 128:(h + 1) * 128] = out[c * RET_H + h].astype(oc_ref.dtype)

    def call(pr, cc, ss, nw):
        return pl.pallas_call(
            body,
            grid=(nb,),
            in_specs=[
                pl.BlockSpec((tb, 2048), lambda i: (i, 0)),
                pl.BlockSpec((tb, 128), lambda i: (i, 0)),
                pl.BlockSpec((tb, 128), lambda i: (i, 0)),
                pl.BlockSpec((8, 128), lambda i: (0, 0)),
            ],
            out_specs=[
                pl.BlockSpec((tb, 512), lambda i: (i, 0)),
                pl.BlockSpec((ncb, RET_H, 128, 128), lambda i: (i, 0, 0, 0)),
            ],
            out_shape=[
                jax.ShapeDtypeStruct((seq, 512), BF16),
                jax.ShapeDtypeStruct((seq // CH, RET_H, 128, 128), F32),
            ],
            scratch_shapes=[pltpu.VMEM((RET_H, 128, 128), F32)],
            compiler_params=pltpu.CompilerParams(dimension_semantics=("arbitrary",), vmem_limit_bytes=VMEM_LIMIT),
            name="ret_fwd",
        )(pr, cc, ss, nw)

    return call


def _make_ret_bwd(seq, tb):
    ncb = tb // CH
    nb = seq // tb

    def body(pr_ref, cc_ref, ss_ref, nw_ref, st_ref, doc_ref, dpr_ref, dnw_ref, dr_scr):
        @pl.when(pl.program_id(0) == 0)
        def _():
            dr_scr[...] = jnp.zeros_like(dr_scr)
            dnw_ref[...] = jnp.zeros_like(dnw_ref)

        nw = nw_ref[0:1, :]
        scale = RET_D ** -0.5
        n = ncb * RET_H
        d = _ret_batch(pr_ref, cc_ref, ss_ref, ncb)
        q, k, v, z, s = d["q"], d["k"], d["v"], d["z"], d["s"]
        r_prev = st_ref[...].reshape(n, 128, 128)
        o = _bdot(s, v, _NN) + _bdot(q, r_prev, _NN) * d["qdec"]
        doc = jnp.stack([doc_ref[c * CH:(c + 1) * CH, h * 128:(h + 1) * 128]
                         for c in range(ncb) for h in range(RET_H)])
        on, r, y = _rms_fwd(o, nw, RET_D)
        dz = doc * y * _dsilu(z)
        do, dnw_rows = _rms_bwd(doc * _silu(z), on, r, nw, RET_D)
        dnw_acc = jnp.sum(jnp.sum(dnw_rows, axis=0), axis=0, keepdims=True)
        dqd = do * d["qdec"]
        qtd = _bdot(q, dqd, _TN)
        drn_l = [None] * ncb
        for c in reversed(range(ncb)):
            drn_l[c] = dr_scr[...]
            dr_scr[...] = qtd[c * RET_H:(c + 1) * RET_H] + d["cdec"] * drn_l[c]
        drn = jnp.concatenate(drn_l, axis=0)
        ds = _bdot(do, v, _NT) * d["dmat"]
        dq = _rot_bwd(_bdot(ds, k, _NN) + _bdot(dqd, r_prev, _NT), d["cc"], d["ss"])
        dk = _rot_bwd((_bdot(ds, q, _TN) + _bdot(v, drn, _NT) * d["kdec"]) * scale, d["cc"], d["ss"])
        dv = _bdot(s, do, _TN) + _bdot(k * d["kdec"], drn, _NN)
        for c in range(ncb):
            rows = slice(c * CH, (c + 1) * CH)
            for h in range(RET_H):
                b = c * RET_H + h
                for j, val in enumerate((dq, dk, dv, dz)):
                    dpr_ref[rows, j * 512 + h * 128:j * 512 + (h + 1) * 128] = val[b].astype(dpr_ref.dtype)
        dnw_ref[...] += jnp.where(_iota2((8, 128), 0) == 0, dnw_acc, 0.0)

    def call(pr, cc, ss, nw, st, doc):
        rev = lambda i: (nb - 1 - i, 0)
        return pl.pallas_call(
            body,
            grid=(nb,),
            in_specs=[
                pl.BlockSpec((tb, 2048), rev),
                pl.BlockSpec((tb, 128), rev),
                pl.BlockSpec((tb, 128), rev),
                pl.BlockSpec((8, 128), lambda i: (0, 0)),
                pl.BlockSpec((ncb, RET_H, 128, 128), lambda i: (nb - 1 - i, 0, 0, 0)),
                pl.BlockSpec((tb, 512), rev),
            ],
            out_specs=[
                pl.BlockSpec((tb, 2048), rev),
                pl.BlockSpec((8, 128), lambda i: (0, 0)),
            ],
            out_shape=[
                jax.ShapeDtypeStruct((seq, 2048), BF16),
                jax.ShapeDtypeStruct((8, 128), F32),
            ],
            scratch_shapes=[pltpu.VMEM((RET_H, 128, 128), F32)],
            compiler_params=pltpu.CompilerParams(dimension_semantics=("arbitrary",), vmem_limit_bytes=VMEM_LIMIT),
            name="ret_bwd",
        )(pr, cc, ss, nw, st, doc)

    return call


def _rope_tables(seq):
    half = RET_D // 2
    inv = ROPE_BASE ** (-jnp.arange(half, dtype=F32) / half)
    ang = jnp.arange(seq, dtype=jnp.int32).astype(F32)[:, None] * inv[None, :]
    cos, sin = jnp.cos(ang), jnp.sin(ang)
    return jnp.concatenate([cos, cos], axis=1), jnp.concatenate([-sin, sin], axis=1)


SEG_G, SEG_S, SEG_R, SEG_GS, SEG_SS = (0, 2048), (2048, 4608), (4608, 6656), (6656, 6784), (6784, 6912)
NP = 6912
SEGS = (SEG_G, SEG_S, SEG_R, SEG_GS, SEG_SS)


def _resident(shape):
    return pl.BlockSpec(shape, lambda i: (0,) * len(shape), pipeline_mode=pl.Buffered(1))


def _make_inproj(seq, tl):
    def body(x_ref, pn_ref, w_ref, pg_ref, ps_ref, pr_ref, gs_ref, ss_ref, ht_ref):
        x = x_ref[...]
        _, _, hn = _rms_fwd(x, pn_ref[0:1, :], D_MODEL)
        h = hn.astype(BF16)
        ht_ref[...] = hn.T.astype(BF16)
        for (a, b), o_ref in zip(SEGS, (pg_ref, ps_ref, pr_ref, gs_ref, ss_ref)):
            o_ref[...] = jnp.dot(h, w_ref[:, a:b], preferred_element_type=F32)

    def call(x, pn, w, comm=None, comm_args=()):
        row = lambda i: (i, 0)
        cx = _exchange_specs(comm)
        return pl.pallas_call(
            _with_exchange(body, comm, 3, 6, seq // tl),
            grid=(seq // tl,),
            in_specs=[pl.BlockSpec((tl, D_MODEL), row), _resident((8, D_MODEL)), _resident((D_MODEL, NP))]
            + cx["specs"],
            out_specs=[pl.BlockSpec((tl, b - a), row) for a, b in SEGS]
            + [pl.BlockSpec((D_MODEL, tl), lambda i: (0, i))] + cx["specs"],
            out_shape=[jax.ShapeDtypeStruct((seq, b - a), F32) for a, b in SEGS]
            + [jax.ShapeDtypeStruct((D_MODEL, seq), BF16)] + cx["out_shape"],
            scratch_shapes=cx["scratch"],
            compiler_params=pltpu.CompilerParams(dimension_semantics=("arbitrary",), vmem_limit_bytes=VMEM_LIMIT,
                                                 has_side_effects=comm is not None),
            name="inproj" + cx["tag"],
        )(x, pn, w, *comm_args)

    return call


def _make_outproj(seq, tl):
    def body(oa_ref, ob_ref, oc_ref, w_ref, x_ref, qn_ref, out_ref, xn_ref):
        out = (jnp.dot(oa_ref[...], w_ref[0:512, :], preferred_element_type=F32)
               + jnp.dot(ob_ref[...], w_ref[512:1536, :], preferred_element_type=F32)
               + jnp.dot(oc_ref[...], w_ref[1536:2048, :], preferred_element_type=F32))
        out_ref[...] = out
        _, _, y = _rms_fwd(out, qn_ref[0:1, :], D_MODEL)
        xn_ref[...] = x_ref[...] + y

    def call(oa, ob, oc, w, x, qn):
        row = lambda i: (i, 0)
        return pl.pallas_call(
            body,
            grid=(seq // tl,),
            in_specs=[pl.BlockSpec((tl, 512), row), pl.BlockSpec((tl, 1024), row), pl.BlockSpec((tl, 512), row),
                      _resident((2048, D_MODEL)), pl.BlockSpec((tl, D_MODEL), row), _resident((8, D_MODEL))],
            out_specs=[pl.BlockSpec((tl, D_MODEL), row), pl.BlockSpec((tl, D_MODEL), row)],
            out_shape=[jax.ShapeDtypeStruct((seq, D_MODEL), F32), jax.ShapeDtypeStruct((seq, D_MODEL), F32)],
            compiler_params=pltpu.CompilerParams(dimension_semantics=("arbitrary",), vmem_limit_bytes=VMEM_LIMIT),
            name="outproj",
        )(oa, ob, oc, w, x, qn)

    return call


def _make_outproj_loss(seq, tl):
    def body(oa_ref, ob_ref, oc_ref, w_ref, x_ref, qn_ref, t_ref, out_ref, dy_ref, loss_ref):
        @pl.when(pl.program_id(0) == 0)
        def _():
            loss_ref[...] = jnp.zeros_like(loss_ref)

        out = (jnp.dot(oa_ref[...], w_ref[0:512, :], preferred_element_type=F32)
               + jnp.dot(ob_ref[...], w_ref[512:1536, :], preferred_element_type=F32)
               + jnp.dot(oc_ref[...], w_ref[1536:2048, :], preferred_element_type=F32))
        out_ref[...] = out
        _, _, y = _rms_fwd(out, qn_ref[0:1, :], D_MODEL)
        err = (x_ref[...] + y) - t_ref[...]
        dy_ref[...] = err * (1.0 / D_MODEL)
        part = jnp.sum(jnp.sum(err * err, axis=1, keepdims=True), axis=0, keepdims=True) * (0.5 / D_MODEL)
        loss_ref[...] += jnp.where((_iota2((8, 128), 0) == 0) & (_iota2((8, 128), 1) == 0), part, 0.0)

    def call(oa, ob, oc, w, x, qn, t):
        row = lambda i: (i, 0)
        return pl.pallas_call(
            body,
            grid=(seq // tl,),
            in_specs=[pl.BlockSpec((tl, 512), row), pl.BlockSpec((tl, 1024), row), pl.BlockSpec((tl, 512), row),
                      _resident((2048, D_MODEL)), pl.BlockSpec((tl, D_MODEL), row), _resident((8, D_MODEL)),
                      pl.BlockSpec((tl, D_MODEL), row)],
            out_specs=[pl.BlockSpec((tl, D_MODEL), row), pl.BlockSpec((tl, D_MODEL), row),
                       pl.BlockSpec((8, 128), lambda i: (0, 0))],
            out_shape=[jax.ShapeDtypeStruct((seq, D_MODEL), F32), jax.ShapeDtypeStruct((seq, D_MODEL), F32),
                       jax.ShapeDtypeStruct((8, 128), F32)],
            compiler_params=pltpu.CompilerParams(dimension_semantics=("arbitrary",), vmem_limit_bytes=VMEM_LIMIT),
            name="outproj_loss",
        )(oa, ob, oc, w, x, qn, t)

    return call


def _make_outproj_bwd(seq, tl):
    def body(dxn_ref, out_ref, oa_ref, ob_ref, oc_ref, w_ref, qn_ref, doa_ref, dob_ref, doc_ref, dqn_ref, dw_ref):
        @pl.when(pl.program_id(0) == 0)
        def _():
            dqn_ref[...] = jnp.zeros_like(dqn_ref)
            dw_ref[...] = jnp.zeros_like(dw_ref)

        qn = qn_ref[0:1, :]
        on, r, _ = _rms_fwd(out_ref[...], qn, D_MODEL)
        dout, dqn_rows = _rms_bwd(dxn_ref[...], on, r, qn, D_MODEL)
        dqn_ref[...] += jnp.where(_iota2((8, D_MODEL), 0) == 0, jnp.sum(dqn_rows, axis=0, keepdims=True), 0.0)
        db = dout.astype(BF16)
        nt = (((1,), (1,)), ((), ()))
        tn = (((0,), (0,)), ((), ()))
        doa_ref[...] = lax.dot_general(db, w_ref[0:512, :], nt, preferred_element_type=F32).astype(BF16)
        dob_ref[...] = lax.dot_general(db, w_ref[512:1536, :], nt, preferred_element_type=F32).astype(BF16)
        doc_ref[...] = lax.dot_general(db, w_ref[1536:2048, :], nt, preferred_element_type=F32).astype(BF16)
        dw_ref[0:512, :] += lax.dot_general(oa_ref[...], db, tn, preferred_element_type=F32)
        dw_ref[512:1536, :] += lax.dot_general(ob_ref[...], db, tn, preferred_element_type=F32)
        dw_ref[1536:2048, :] += lax.dot_general(oc_ref[...], db, tn, preferred_element_type=F32)

    def call(dxn, out, oa, ob, oc, w, qn):
        row = lambda i: (i, 0)
        const = lambda i: (0, 0)
        return pl.pallas_call(
            body,
            grid=(seq // tl,),
            in_specs=[pl.BlockSpec((tl, D_MODEL), row), pl.BlockSpec((tl, D_MODEL), row),
                      pl.BlockSpec((tl, 512), row), pl.BlockSpec((tl, 1024), row), pl.BlockSpec((tl, 512), row),
                      _resident((2048, D_MODEL)), _resident((8, D_MODEL))],
            out_specs=[pl.BlockSpec((tl, 512), row), pl.BlockSpec((tl, 1024), row), pl.BlockSpec((tl, 512), row),
                       pl.BlockSpec((8, D_MODEL), const), pl.BlockSpec((2048, D_MODEL), const)],
            out_shape=[jax.ShapeDtypeStruct((seq, 512), BF16), jax.ShapeDtypeStruct((seq, 1024), BF16),
                       jax.ShapeDtypeStruct((seq, 512), BF16), jax.ShapeDtypeStruct((8, D_MODEL), F32),
                       jax.ShapeDtypeStruct((2048, D_MODEL), F32)],
            compiler_params=pltpu.CompilerParams(dimension_semantics=("arbitrary",), vmem_limit_bytes=VMEM_LIMIT),
            name="outproj_bwd",
        )(dxn, out, oa, ob, oc, w, qn)

    return call


def _make_inproj_bwd_dx(seq, tl):
    def body(dg_ref, ds_ref, dr_ref, dgs_ref, dss_ref, w_ref, x_ref, pn_ref, dxn_ref, dx_ref, dpn_ref):
        @pl.when(pl.program_id(0) == 0)
        def _():
            dpn_ref[...] = jnp.zeros_like(dpn_ref)

        nt = (((1,), (1,)), ((), ()))
        dh = jnp.zeros((tl, D_MODEL), F32)
        for (a, b), d_ref in zip(SEGS, (dg_ref, ds_ref, dr_ref, dgs_ref, dss_ref)):
            dh = dh + lax.dot_general(d_ref[...], w_ref[:, a:b], nt, preferred_element_type=F32)
        pn = pn_ref[0:1, :]
        on, r, _ = _rms_fwd(x_ref[...], pn, D_MODEL)
        dx, dpn_rows = _rms_bwd(dh, on, r, pn, D_MODEL)
        dx_ref[...] = dx + dxn_ref[...]
        dpn_ref[...] += jnp.where(_iota2((8, D_MODEL), 0) == 0, jnp.sum(dpn_rows, axis=0, keepdims=True), 0.0)

    def call(dg, ds, dr, dgs, dss, w, x, pn, dxn, comm=None, comm_args=()):
        row = lambda i: (i, 0)
        cx = _exchange_specs(comm)
        return pl.pallas_call(
            _with_exchange(body, comm, 9, 2, seq // tl),
            grid=(seq // tl,),
            in_specs=[pl.BlockSpec((tl, b - a), row) for a, b in SEGS]
            + [_resident((D_MODEL, NP)), pl.BlockSpec((tl, D_MODEL), row), _resident((8, D_MODEL)),
               pl.BlockSpec((tl, D_MODEL), row)] + cx["specs"],
            out_specs=[pl.BlockSpec((tl, D_MODEL), row), pl.BlockSpec((8, D_MODEL), lambda i: (0, 0))] + cx["specs"],
            out_shape=[jax.ShapeDtypeStruct((seq, D_MODEL), F32), jax.ShapeDtypeStruct((8, D_MODEL), F32)]
            + cx["out_shape"],
            scratch_shapes=cx["scratch"],
            compiler_params=pltpu.CompilerParams(dimension_semantics=("arbitrary",), vmem_limit_bytes=VMEM_LIMIT,
                                                 has_side_effects=comm is not None),
            name="inproj_bwd_dx" + cx["tag"],
        )(dg, ds, dr, dgs, dss, w, x, pn, dxn, *comm_args)

    return call


def _make_inproj_bwd_dw(seq, tl, width, tn, name):
    def body(ht_ref, d_ref, dw_ref):
        @pl.when(pl.program_id(1) == 0)
        def _():
            dw_ref[...] = jnp.zeros_like(dw_ref)

        dw_ref[...] += jnp.dot(ht_ref[...], d_ref[...], preferred_element_type=F32)

    def call(ht, d):
        return pl.pallas_call(
            body,
            grid=(width // tn, seq // tl),
            in_specs=[pl.BlockSpec((D_MODEL, tl), lambda j, i: (0, i)), pl.BlockSpec((tl, tn), lambda j, i: (i, j))],
            out_specs=pl.BlockSpec((D_MODEL, tn), lambda j, i: (0, j)),
            out_shape=jax.ShapeDtypeStruct((D_MODEL, width), F32),
            compiler_params=pltpu.CompilerParams(dimension_semantics=("arbitrary", "arbitrary"),
                                                 vmem_limit_bytes=VMEM_LIMIT),
            name=name,
        )(ht, d)

    return call


ADAM_LR, ADAM_B1, ADAM_B2, ADAM_EPS, ADAM_WD, ADAM_STEP = 0.001, 0.9, 0.999, 1e-08, 0.01, 10


def _adam_math(w, g, m, v):
    m = ADAM_B1 * m + (1.0 - ADAM_B1) * g
    v = ADAM_B2 * v + (1.0 - ADAM_B2) * (g * g)
    m_hat = m / (1.0 - ADAM_B1 ** ADAM_STEP)
    v_hat = v / (1.0 - ADAM_B2 ** ADAM_STEP)
    delta = -ADAM_LR * (m_hat / (jnp.sqrt(v_hat) + ADAM_EPS) + ADAM_WD * w)
    return delta, m, v


def _adamw(w, g, m, v, name):
    shape = w.shape
    cols = shape[-1]
    rows = w.size // cols
    tr = rows if rows <= 512 else 256
    assert rows % tr == 0

    def body(w_ref, g_ref, m_ref, v_ref, d_ref, mo_ref, vo_ref):
        d_ref[...], mo_ref[...], vo_ref[...] = _adam_math(w_ref[...], g_ref[...], m_ref[...], v_ref[...])

    spec = pl.BlockSpec((tr, cols), lambda i: (i, 0))
    outs = pl.pallas_call(
        body,
        grid=(rows // tr,),
        in_specs=[spec] * 4,
        out_specs=[spec] * 3,
        out_shape=[jax.ShapeDtypeStruct((rows, cols), F32)] * 3,
        compiler_params=pltpu.CompilerParams(dimension_semantics=("arbitrary",), vmem_limit_bytes=VMEM_LIMIT),
        name=name,
    )(*[a.reshape(rows, cols) for a in (w, g, m, v)])
    return (g,) + tuple(o.reshape(shape) for o in outs)


def _adamw_pairs(w, mine, theirs, m, v, name):
    na, r, cols = w.shape
    assert na == 2
    tr = 256
    assert r % tr == 0

    def body(w_ref, a0_ref, b0_ref, a1_ref, b1_ref, m_ref, v_ref, g_ref, d_ref, mo_ref, vo_ref):
        g = jnp.where(pl.program_id(0) == 0, a0_ref[...] + b0_ref[...], a1_ref[...] + b1_ref[...])
        g_ref[...] = g
        d_ref[...], mo_ref[...], vo_ref[...] = _adam_math(w_ref[...], g, m_ref[...], v_ref[...])

    nblk = r // tr
    full = pl.BlockSpec((None, tr, cols), lambda a, i: (a, i, 0))
    lay0 = pl.BlockSpec((None, tr, cols), lambda a, i: (0, i * (1 - a) + (nblk - 1) * a, 0))
    lay1 = pl.BlockSpec((None, tr, cols), lambda a, i: (0, i * a, 0))
    return pl.pallas_call(
        body,
        grid=(na, nblk),
        in_specs=[full, lay0, lay0, lay1, lay1, full, full],
        out_specs=[full] * 4,
        out_shape=[jax.ShapeDtypeStruct(w.shape, F32)] * 4,
        compiler_params=pltpu.CompilerParams(dimension_semantics=("arbitrary",) * 2, vmem_limit_bytes=VMEM_LIMIT),
        name=name,
    )(w, mine[0], theirs[0], mine[1], theirs[1], m, v)


MESH = pl.DeviceIdType.MESH
ANY = pl.BlockSpec(memory_space=pl.ANY)
CHIP_REL = ((1, 0), (0, 1), (1, 1))


def _flip(v, d):
    return 1 - v if d else v


def _ag_chips(arrs, name):
    n = len(arrs)

    def body(*refs):
        ins, outs = refs[:n], refs[n:2 * n]
        send_sems, recv_sems, loc_sems = refs[2 * n:]
        x, y, c = lax.axis_index("x"), lax.axis_index("y"), lax.axis_index("c")
        me = 2 * x + y

        def remote(a, k, slot):
            dx, dy = CHIP_REL[k]
            return pltpu.make_async_remote_copy(
                src_ref=ins[a], dst_ref=outs[a].at[slot], send_sem=send_sems.at[a * 3 + k],
                recv_sem=recv_sems.at[a * 3 + k], device_id=(_flip(x, dx), _flip(y, dy), c), device_id_type=MESH)

        local = [pltpu.make_async_copy(ins[a], outs[a].at[me], loc_sems.at[a]) for a in range(n)]
        for cp in local:
            cp.start()
        for a in range(n):
            for k in range(3):
                remote(a, k, me).start()
        for a in range(n):
            for k, (dx, dy) in enumerate(CHIP_REL):
                remote(a, k, 2 * _flip(x, dx) + _flip(y, dy)).wait_recv()
        for a in range(n):
            for k in range(3):
                remote(a, k, me).wait_send()
        for cp in local:
            cp.wait()

    return pl.pallas_call(
        body,
        in_specs=[ANY] * n,
        out_specs=[ANY] * n,
        out_shape=[jax.ShapeDtypeStruct((4,) + a.shape, a.dtype) for a in arrs],
        scratch_shapes=[pltpu.SemaphoreType.DMA((3 * n,)), pltpu.SemaphoreType.DMA((3 * n,)),
                        pltpu.SemaphoreType.DMA((n,))],
        compiler_params=pltpu.CompilerParams(has_side_effects=True),
        name=name,
    )(*arrs)


class _ChipExchange:
    def __init__(self, kind, arrs):
        self.kind, self.n = kind, len(arrs)
        if kind == "gather":
            self.out_shape = [jax.ShapeDtypeStruct((4,) + a.shape, a.dtype) for a in arrs]
        else:
            self.out_shape = [jax.ShapeDtypeStruct((3,) + a.shape[1:], a.dtype) for a in arrs]
        self.scratch = [pltpu.SemaphoreType.DMA((4 * self.n,)), pltpu.SemaphoreType.DMA((4 * self.n,))]

    def _copies(self, ins, outs, sems):
        send_sems, recv_sems = sems
        x, y, c = lax.axis_index("x"), lax.axis_index("y"), lax.axis_index("c")
        me = 2 * x + y
        pairs = []
        for a in range(self.n):
            for k, (dx, dy) in enumerate(CHIP_REL):
                px, py = _flip(x, dx), _flip(y, dy)
                sem = dict(send_sem=send_sems.at[4 * a + k], recv_sem=recv_sems.at[4 * a + k],
                           device_id=(px, py, c), device_id_type=MESH)
                if self.kind == "gather":
                    out = pltpu.make_async_remote_copy(src_ref=ins[a], dst_ref=outs[a].at[me], **sem)
                    inc = pltpu.make_async_remote_copy(src_ref=ins[a], dst_ref=outs[a].at[2 * px + py], **sem)
                else:
                    out = pltpu.make_async_remote_copy(src_ref=ins[a].at[2 * px + py], dst_ref=outs[a].at[k], **sem)
                    inc = out
                pairs.append((out, inc))
            if self.kind == "gather":
                own = pltpu.make_async_remote_copy(
                    src_ref=ins[a], dst_ref=outs[a].at[me], send_sem=send_sems.at[4 * a + 3],
                    recv_sem=recv_sems.at[4 * a + 3], device_id=(x, y, 1 - c), device_id_type=MESH)
                pairs.append((own, own))
        return pairs

    def start(self, ins, outs, sems):
        for out, _ in self._copies(ins, outs, sems):
            out.start()

    def finish(self, ins, outs, sems):
        pairs = self._copies(ins, outs, sems)
        for _, inc in pairs:
            inc.wait_recv()
        for out, _ in pairs:
            out.wait_send()


def _with_exchange(body, comm, n_in, n_out, nb):
    if comm is None:
        return body

    def wrapped(*refs):
        ins = refs[:n_in]
        c_in = refs[n_in:n_in + comm.n]
        outs = refs[n_in + comm.n:n_in + comm.n + n_out]
        c_out = refs[n_in + comm.n + n_out:n_in + 2 * comm.n + n_out]
        rest = refs[n_in + 2 * comm.n + n_out:]
        scratch, sems = rest[:len(rest) - 2], rest[len(rest) - 2:]

        @pl.when(pl.program_id(0) == 0)
        def _():
            comm.start(c_in, c_out, sems)

        body(*ins, *outs, *scratch)

        @pl.when(pl.program_id(0) == nb - 1)
        def _():
            comm.finish(c_in, c_out, sems)

    return wrapped


def _exchange_specs(comm):
    if comm is None:
        return dict(specs=[], out_shape=[], scratch=[], tag="")
    return dict(specs=[pl.BlockSpec(memory_space=pl.ANY)] * comm.n, out_shape=list(comm.out_shape),
                scratch=list(comm.scratch), tag="_" + comm.kind)


def _half(ref_or_shape, half):
    r = ref_or_shape[-2] // 2
    return pl.ds(half * r, r)


def _ag_rows(arrs, name):
    n = len(arrs)

    def body(*refs):
        ins, outs = refs[:n], refs[n:2 * n]
        send_sems, recv_sems, fsend_sems, frecv_sems, loc_sems = refs[2 * n:]
        x, y, c = lax.axis_index("x"), lax.axis_index("y"), lax.axis_index("c")
        me = 2 * x + y
        sib = (x, y, 1 - c)

        def chip_of(k):
            dx, dy = CHIP_REL[k]
            return _flip(x, dx), _flip(y, dy)

        def ici(a, k, slot):
            px, py = chip_of(k)
            rows = _half(arrs[a].shape, c)
            return pltpu.make_async_remote_copy(
                src_ref=ins[a].at[:, rows, :], dst_ref=outs[a].at[slot, :, rows, :], send_sem=send_sems.at[a * 3 + k],
                recv_sem=recv_sems.at[a * 3 + k], device_id=(px, py, c), device_id_type=MESH)

        def fwd(a, k, half):
            px, py = chip_of(k)
            blk = outs[a].at[2 * px + py, :, _half(arrs[a].shape, half), :]
            return pltpu.make_async_remote_copy(
                src_ref=blk, dst_ref=blk, send_sem=fsend_sems.at[a * 3 + k], recv_sem=frecv_sems.at[a * 3 + k],
                device_id=sib, device_id_type=MESH)

        own = [pltpu.make_async_remote_copy(src_ref=ins[a], dst_ref=outs[a].at[me], send_sem=loc_sems.at[a],
                                            recv_sem=loc_sems.at[n + a], device_id=sib, device_id_type=MESH)
               for a in range(n)]
        for cp in own:
            cp.start()
        for a in range(n):
            for k in range(3):
                ici(a, k, me).start()
        for a in range(n):
            for k in range(3):
                px, py = chip_of(k)
                ici(a, k, 2 * px + py).wait_recv()
                fwd(a, k, c).start()
        for a in range(n):
            for k in range(3):
                fwd(a, k, 1 - c).wait_recv()
        for a in range(n):
            for k in range(3):
                ici(a, k, me).wait_send()
                fwd(a, k, c).wait_send()
        for cp in own:
            cp.wait()

    return pl.pallas_call(
        body,
        in_specs=[ANY] * n,
        out_specs=[ANY] * n,
        out_shape=[jax.ShapeDtypeStruct((4,) + a.shape, a.dtype) for a in arrs],
        scratch_shapes=[pltpu.SemaphoreType.DMA((3 * n,)) for _ in range(4)] + [pltpu.SemaphoreType.DMA((2 * n,))],
        compiler_params=pltpu.CompilerParams(has_side_effects=True),
        name=name,
    )(*arrs)


def _sum_chips(own, recv, chip, name):
    _, na, r, cols = own.shape
    tr = 256
    assert r % tr == 0

    def body(chip_ref, o_ref, r_ref, s_ref):
        s_ref[...] = ((o_ref[...] + r_ref[0].astype(F32)) + r_ref[1].astype(F32)) + r_ref[2].astype(F32)

    return pl.pallas_call(
        body,
        grid_spec=pltpu.PrefetchScalarGridSpec(
            num_scalar_prefetch=1,
            grid=(na, r // tr),
            in_specs=[pl.BlockSpec((None, None, tr, cols), lambda a, i, ch: (ch[0], a, i, 0)),
                      pl.BlockSpec((3, None, tr, cols), lambda a, i, ch: (0, a, i, 0))],
            out_specs=pl.BlockSpec((None, tr, cols), lambda a, i, ch: (a, i, 0))),
        out_shape=jax.ShapeDtypeStruct((na, r, cols), F32),
        compiler_params=pltpu.CompilerParams(dimension_semantics=("arbitrary",) * 2, vmem_limit_bytes=VMEM_LIMIT),
        name=name,
    )(chip, own, recv)


def _swap_sibling(arrs, name):
    n = len(arrs)

    def body(*refs):
        ins, outs = refs[:n], refs[n:2 * n]
        send_sems, recv_sems = refs[2 * n:]
        x, y, c = lax.axis_index("x"), lax.axis_index("y"), lax.axis_index("c")
        cps = [pltpu.make_async_remote_copy(src_ref=ins[a], dst_ref=outs[a], send_sem=send_sems.at[a],
                                            recv_sem=recv_sems.at[a], device_id=(x, y, 1 - c), device_id_type=MESH)
               for a in range(n)]
        for cp in cps:
            cp.start()
        for cp in cps:
            cp.wait_recv()
        for cp in cps:
            cp.wait_send()

    return pl.pallas_call(
        body,
        in_specs=[ANY] * n,
        out_specs=[ANY] * n,
        out_shape=[jax.ShapeDtypeStruct(a.shape, a.dtype) for a in arrs],
        scratch_shapes=[pltpu.SemaphoreType.DMA((n,)), pltpu.SemaphoreType.DMA((n,))],
        compiler_params=pltpu.CompilerParams(has_side_effects=True),
        name=name,
    )(*arrs)


def _allreduce_small(vec, name):
    rows = vec.shape[0]

    def body(v_ref, out_ref, gat_ref, send_sems, recv_sems):
        x, y, c = lax.axis_index("x"), lax.axis_index("y"), lax.axis_index("c")
        me = 4 * x + 2 * y + c

        def remote(k, slot):
            dx, dy, dc = (k >> 2) & 1, (k >> 1) & 1, k & 1
            return pltpu.make_async_remote_copy(
                src_ref=v_ref, dst_ref=gat_ref.at[slot], send_sem=send_sems.at[k - 1], recv_sem=recv_sems.at[k - 1],
                device_id=(_flip(x, dx), _flip(y, dy), _flip(c, dc)), device_id_type=MESH)

        gat_ref[me] = v_ref[...]
        for k in range(1, 8):
            remote(k, me).start()
        for k in range(1, 8):
            dx, dy, dc = (k >> 2) & 1, (k >> 1) & 1, k & 1
            remote(k, 4 * _flip(x, dx) + 2 * _flip(y, dy) + _flip(c, dc)).wait_recv()
        for k in range(1, 8):
            remote(k, me).wait_send()
        acc = gat_ref[0]
        for j in range(1, 8):
            acc = acc + gat_ref[j]
        out_ref[...] = acc

    vm = pl.BlockSpec(memory_space=pltpu.VMEM)
    return pl.pallas_call(
        body,
        in_specs=[vm],
        out_specs=vm,
        out_shape=jax.ShapeDtypeStruct(vec.shape, F32),
        scratch_shapes=[pltpu.VMEM((8, rows, 128), F32), pltpu.SemaphoreType.DMA((7,)), pltpu.SemaphoreType.DMA((7,))],
        compiler_params=pltpu.CompilerParams(has_side_effects=True),
        name=name,
    )(vec)


def _pad8(v, width, lane0=0):
    v = v.reshape(1, -1) if v.ndim == 1 else v
    return jnp.zeros((8, width), F32).at[:v.shape[0], lane0:lane0 + v.shape[1]].set(v.astype(F32))


def _relayout_w_in(g):
    tr = 128
    q = N_IN // 4

    def body(g_ref, o_ref):
        w = jnp.concatenate([g_ref[j] for j in range(4)], axis=1)
        z = lambda n: jnp.zeros((tr, n), w.dtype)
        o_ref[...] = jnp.concatenate([w[:, 0:2048], w[:, 2056:4616], w[:, 4632:6680],
                                      w[:, 2048:2056], z(120), w[:, 4616:4632], z(112)], axis=1)

    return pl.pallas_call(
        body,
        grid=(D_MODEL // tr,),
        in_specs=[pl.BlockSpec((4, tr, q), lambda i: (0, i, 0))],
        out_specs=pl.BlockSpec((tr, NP), lambda i: (i, 0)),
        out_shape=jax.ShapeDtypeStruct((D_MODEL, NP), g.dtype),
        compiler_params=pltpu.CompilerParams(dimension_semantics=("arbitrary",), vmem_limit_bytes=VMEM_LIMIT),
        name="relayout_w_in",
    )(g)


def _unlayout_dw_in(dg, ds, dr, dsm):
    tr = 128
    q = N_IN // 4

    def body(g_ref, s_ref, r_ref, sm_ref, o_ref, ob_ref):
        w = jnp.concatenate([g_ref[...], sm_ref[:, 0:8], s_ref[...], sm_ref[:, 128:144], r_ref[...]], axis=1)
        for j in range(4):
            blk = w[:, q * j:q * (j + 1)]
            o_ref[j] = blk
            ob_ref[j] = blk.astype(BF16)

    row = lambda i: (i, 0)
    return pl.pallas_call(
        body,
        grid=(D_MODEL // tr,),
        in_specs=[pl.BlockSpec((tr, d.shape[1]), row) for d in (dg, ds, dr, dsm)],
        out_specs=[pl.BlockSpec((4, tr, q), lambda i: (0, i, 0))] * 2,
        out_shape=[jax.ShapeDtypeStruct((4, D_MODEL, q), F32), jax.ShapeDtypeStruct((4, D_MODEL, q), BF16)],
        compiler_params=pltpu.CompilerParams(dimension_semantics=("arbitrary",), vmem_limit_bytes=VMEM_LIMIT),
        name="unlayout_dw_in",
    )(dg, ds, dr, dsm)


TB = 256
TL = 256
TL_IN = 512
TL_OB = 1024
TK = 2048


def kernel(x, pre_norm, post_norm, w_in, gdn_conv, gdn_A_log, gdn_dt_bias, gdn_norm, ssd_conv, ssd_conv_b, ssd_A_log, ssd_dt_bias, ssd_D, ssd_norm, ret_norm, w_out, loss_target, m_pre_norm, m_post_norm, m_w_in, m_gdn_conv, m_gdn_A_log, m_gdn_dt_bias, m_gdn_norm, m_ssd_conv, m_ssd_conv_b, m_ssd_A_log, m_ssd_dt_bias, m_ssd_D, m_ssd_norm, m_ret_norm, m_w_out, v_pre_norm, v_post_norm, v_w_in, v_gdn_conv, v_gdn_A_log, v_gdn_dt_bias, v_gdn_norm, v_ssd_conv, v_ssd_conv_b, v_ssd_A_log, v_ssd_dt_bias, v_ssd_D, v_ssd_norm, v_ret_norm, v_w_out):
    seq = x.shape[1]
    chip = 2 * lax.axis_index("x") + lax.axis_index("y")
    x0 = x[0]

    wi_b, wo_b = w_in.astype(BF16), w_out.astype(BF16)
    (wi0_g,) = _ag_rows([wi_b[0:1]], "ag_weights")
    gcv_g, scv_g = _ag_chips([gdn_conv, ssd_conv], "ag_conv")
    full_w_in = _relayout_w_in
    wp = [full_w_in(wi0_g[:, 0]), None]
    wo = [None, None]
    ag0 = _ChipExchange("gather", [wo_b[0]])
    ag1 = _ChipExchange("gather", [wi_b[1], wo_b[1]])
    gcv = jnp.transpose(gcv_g, (1, 2, 0, 3)).reshape(DEPTH, CONV_W, 1536)
    scv = jnp.transpose(scv_g, (1, 2, 0, 3)).reshape(DEPTH, CONV_W, 1536)
    rope_c, rope_s = _rope_tables(seq)

    saved = []
    xc = x0
    for l in range(DEPTH):
        p = dict(
            pn=_pad8(pre_norm[l], D_MODEL), qn=_pad8(post_norm[l], D_MODEL),
            g_cw=_pad8(gcv[l], 1536), g_prm=_pad8(jnp.stack([gdn_A_log[l], gdn_dt_bias[l]]), 128, 4),
            g_nw=_pad8(gdn_norm[l], 128),
            s_cw=_pad8(scv[l], 1536), s_cb=_pad8(ssd_conv_b[l], 1536),
            s_prm=_pad8(jnp.stack([ssd_A_log[l], ssd_dt_bias[l], ssd_D[l]]), 128), s_nw=_pad8(ssd_norm[l], SSD_W),
            r_nw=_pad8(ret_norm[l], 128))
        if l == 0:
            pg, ps, pr, gs, ss, ht, wo0_g = _make_inproj(seq, TL_IN)(xc, p["pn"], wp[l], comm=ag0, comm_args=(wo_b[0],))
            wo[0] = wo0_g.reshape(2048, D_MODEL)
        else:
            pg, ps, pr, gs, ss, ht = _make_inproj(seq, TL_IN)(xc, p["pn"], wp[l])
        if l == 0:
            oa, stg, tig, uwg, gpre, wi1_g, wo1_g = _make_gdn_fwd(seq, TB)(
                pg, gs, p["g_cw"], p["g_prm"], p["g_nw"], comm=ag1, comm_args=(wi_b[1], wo_b[1]))
            wp[1], wo[1] = full_w_in(wi1_g), wo1_g.reshape(2048, D_MODEL)
        else:
            oa, stg, tig, uwg, gpre = _make_gdn_fwd(seq, TB)(pg, gs, p["g_cw"], p["g_prm"], p["g_nw"])
        ob, sts, spre, sy = _make_ssd_fwd(seq, TB)(ps, ss, p["s_cw"], p["s_cb"], p["s_prm"], p["s_nw"])
        oc, str_ = _make_ret_fwd(seq, TB)(pr, rope_c, rope_s, p["r_nw"])
        if l == DEPTH - 1:
            out, dxn, lossp = _make_outproj_loss(seq, TL)(oa, ob, oc, wo[l], xc, p["qn"], loss_target[0])
            xn = None
        else:
            out, xn = _make_outproj(seq, TL)(oa, ob, oc, wo[l], xc, p["qn"])
        saved.append(dict(p=p, x=xc, ht=ht, spre=spre, sy=sy, gpre=gpre, pg=pg, ps=ps, pr=pr, gs=gs, ss=ss, stg=stg, tig=tig, uwg=uwg, sts=sts, str=str_,
                          oa=oa, ob=ob, oc=oc, out=out))
        xc = xn

    small = [None] * DEPTH
    gin, gin_b, gout, q_in, q_out = ([None] * DEPTH for _ in range(5))

    for l in reversed(range(DEPTH)):
        s = saved[l]
        p = s["p"]
        doa, dob, doc, dqn, dwo_l = _make_outproj_bwd(seq, TL_OB)(dxn, s["out"], s["oa"], s["ob"], s["oc"], wo[l], p["qn"])
        gout[l] = dwo_l.reshape(4, 512, D_MODEL)
        gdn_args = (s["pg"], s["gpre"], s["gs"], p["g_cw"], p["g_prm"], p["g_nw"], s["stg"], s["tig"], s["uwg"], doa)
        if l == 0:
            payload = (gout[0].astype(BF16),)
            dpg, dgs, dcw_g, dprm_g, dnw_g, q_out[0] = _make_gdn_bwd(seq, TB)(
                *gdn_args, comm=_ChipExchange("scatter", payload), comm_args=payload)
        else:
            dpg, dgs, dcw_g, dprm_g, dnw_g = _make_gdn_bwd(seq, TB)(*gdn_args)
        ssd_args = (s["ps"], s["spre"], s["sy"], s["ss"], p["s_cw"], p["s_cb"], p["s_prm"], p["s_nw"], s["sts"], dob)
        if l == 0:
            payload = (gin_b[1], gout[1].astype(BF16))
            dps, dss, dcw_s, dcb_s, dprm_s, dnw_s, q_in[1], q_out[1] = _make_ssd_bwd(seq, TB)(
                *ssd_args, comm=_ChipExchange("scatter", payload), comm_args=payload)
        else:
            dps, dss, dcw_s, dcb_s, dprm_s, dnw_s = _make_ssd_bwd(seq, TB)(*ssd_args)
        dpr, dnw_r = _make_ret_bwd(seq, TB)(s["pr"], rope_c, rope_s, p["r_nw"], s["str"], doc)
        dws = [_make_inproj_bwd_dw(seq, TK, d.shape[1], tn, f"inproj_bwd_dw{i}")(s["ht"], d)
               for i, (d, tn) in enumerate(((dpg, 2048), (dps, 1280), (dpr, 2048),
                                            (jnp.concatenate([dgs, dss], axis=1), 256)))]
        gin[l], gin_b[l] = _unlayout_dw_in(*dws)
        dx_args = (dpg, dps, dpr, dgs, dss, wp[l], s["x"], p["pn"], dxn)
        if l == 0:
            payload = (gin_b[0],)
            dx, dpn, q_in[0] = _make_inproj_bwd_dx(seq, TL_IN)(
                *dx_args, comm=_ChipExchange("scatter", payload), comm_args=payload)
        else:
            dx, dpn = _make_inproj_bwd_dx(seq, TL_IN)(*dx_args)
        small[l] = [dpn[0], dqn[0], dcw_g[0:4].reshape(-1), dprm_g[0, 4:8], dprm_g[1, 4:8], dnw_g[0],
                    dcw_s[0:4].reshape(-1), dcb_s[0], dprm_s[0, 0:16], dprm_s[1, 0:16], dprm_s[2, 0:16],
                    dnw_s[0], dnw_r[0]]
        dxn = dx
    grad_x = dxn[None]

    sizes = [a.shape[0] for a in small[0]]
    flat = jnp.concatenate(small[0] + small[1] + [lossp[0, 0:1]])
    n_flat = flat.shape[0]
    rows = -(-n_flat // 1024) * 8
    red = _allreduce_small(jnp.pad(flat, (0, rows * 128 - n_flat)).reshape(rows, 128), "allreduce_small").reshape(-1)
    per = sum(sizes)
    loss = red[2 * per]

    def pick(i):
        off = sum(sizes[:i])
        return jnp.stack([red[l * per + off:l * per + off + sizes[i]] for l in range(DEPTH)])

    g_small = dict(
        pre_norm=pick(0), post_norm=pick(1),
        gdn_conv=lax.dynamic_slice_in_dim(pick(2).reshape(DEPTH, CONV_W, 1536), chip * 384, 384, axis=2),
        gdn_A_log=pick(3), gdn_dt_bias=pick(4), gdn_norm=pick(5),
        ssd_conv=lax.dynamic_slice_in_dim(pick(6).reshape(DEPTH, CONV_W, 1536), chip * 384, 384, axis=2),
        ssd_conv_b=pick(7), ssd_A_log=pick(8), ssd_dt_bias=pick(9), ssd_D=pick(10), ssd_norm=pick(11),
        ret_norm=pick(12))

    chip1 = chip.astype(jnp.int32).reshape(1)
    s_in = [_sum_chips(gin[l][:, None], q_in[l][:, None], chip1, f"sum_chips_w_in{l}") for l in range(DEPTH)]
    s_out = [_sum_chips(gout[l][:, None], q_out[l][:, None], chip1, f"sum_chips_w_out{l}") for l in range(DEPTH)]
    t_all = _swap_sibling(s_in + s_out, "swap_grads")
    t_in, t_out = t_all[:DEPTH], t_all[DEPTH:]

    weights = dict(pre_norm=pre_norm, post_norm=post_norm, w_in=w_in, gdn_conv=gdn_conv, gdn_A_log=gdn_A_log,
                   gdn_dt_bias=gdn_dt_bias, gdn_norm=gdn_norm, ssd_conv=ssd_conv, ssd_conv_b=ssd_conv_b,
                   ssd_A_log=ssd_A_log, ssd_dt_bias=ssd_dt_bias, ssd_D=ssd_D, ssd_norm=ssd_norm, ret_norm=ret_norm,
                   w_out=w_out)
    ms = dict(pre_norm=m_pre_norm, post_norm=m_post_norm, w_in=m_w_in, gdn_conv=m_gdn_conv, gdn_A_log=m_gdn_A_log,
              gdn_dt_bias=m_gdn_dt_bias, gdn_norm=m_gdn_norm, ssd_conv=m_ssd_conv, ssd_conv_b=m_ssd_conv_b,
              ssd_A_log=m_ssd_A_log, ssd_dt_bias=m_ssd_dt_bias, ssd_D=m_ssd_D, ssd_norm=m_ssd_norm,
              ret_norm=m_ret_norm, w_out=m_w_out)
    vs = dict(pre_norm=v_pre_norm, post_norm=v_post_norm, w_in=v_w_in, gdn_conv=v_gdn_conv, gdn_A_log=v_gdn_A_log,
              gdn_dt_bias=v_gdn_dt_bias, gdn_norm=v_gdn_norm, ssd_conv=v_ssd_conv, ssd_conv_b=v_ssd_conv_b,
              ssd_A_log=v_ssd_A_log, ssd_dt_bias=v_ssd_dt_bias, ssd_D=v_ssd_D, ssd_norm=v_ssd_norm,
              ret_norm=v_ret_norm, w_out=v_w_out)
    names = list(weights)
    res = {}
    for nme in names:
        if nme == "w_in":
            res[nme] = _adamw_pairs(w_in, s_in, t_in, m_w_in, v_w_in, "adamw_w_in")
        elif nme == "w_out":
            res[nme] = _adamw_pairs(w_out, s_out, t_out, m_w_out, v_w_out, "adamw_w_out")
        else:
            res[nme] = _adamw(weights[nme], g_small[nme], ms[nme], vs[nme], "adamw_" + nme)
    return (loss, grad_x, *[res[n][0] for n in names], *[res[n][1] for n in names],
            *[res[n][2] for n in names], *[res[n][3] for n in names])
```

```python
import functools
import math

import jax
import jax.numpy as jnp
from jax import lax
from jax.experimental import pallas as pl
from jax.experimental.pallas import tpu as pltpu

F32 = jnp.float32
BF16 = jnp.bfloat16
HI = lax.Precision.HIGHEST

D_MODEL = 1024
DEPTH = 2
CH = 64
CONV_W = 4
EPS = 1e-6
GDN_H, GDN_D = 4, 128
SSD_H, SSD_P, SSD_N, SSD_G = 16, 64, 128, 2
SSD_W = SSD_H * SSD_P
RET_H, RET_D = 4, 128
ROPE_BASE = 10000.0
N_IN = 6680
NEG = -1e30

VMEM_LIMIT = 56 * 1024 * 1024


def _dot(a, b):
    return jnp.dot(a.astype(BF16), b.astype(BF16), preferred_element_type=F32)


def _dot_nt(a, b):
    return lax.dot_general(a.astype(BF16), b.astype(BF16), (((1,), (1,)), ((), ())), preferred_element_type=F32)


def _dot_tn(a, b):
    return lax.dot_general(a.astype(BF16), b.astype(BF16), (((0,), (0,)), ((), ())), preferred_element_type=F32)


def _split(a):
    hi = a.astype(BF16)
    return hi, (a - hi.astype(F32)).astype(BF16)


def _dot01l(m, v):
    vh, vl = _split(v)
    mb = m.astype(BF16)
    return jnp.dot(mb, vh, preferred_element_type=F32) + jnp.dot(mb, vl, preferred_element_type=F32)


def _dot01r(v, m):
    vh, vl = _split(v)
    mb = m.astype(BF16)
    return jnp.dot(vh, mb, preferred_element_type=F32) + jnp.dot(vl, mb, preferred_element_type=F32)


def _sigmoid(x):
    return jax.nn.sigmoid(x)


def _silu(x):
    return x * _sigmoid(x)


def _dsilu(x):
    s = _sigmoid(x)
    return s * (1.0 + x * (1.0 - s))


def _softplus(x):
    return jnp.maximum(x, 0.0) + jnp.log1p(jnp.exp(-jnp.abs(x)))


def _iota2(shape, dim):
    return lax.broadcasted_iota(jnp.int32, shape, dim)


def _chunk_tri(tb, upper=False):
    r = _iota2((tb, tb), 0)
    c = _iota2((tb, tb), 1)
    same = jnp.right_shift(r, 6) == jnp.right_shift(c, 6)
    return (same & ((c >= r) if upper else (c <= r))).astype(F32)


def _masks():
    r = _iota2((CH, CH), 0)
    c = _iota2((CH, CH), 1)
    return r >= c, r > c, (r == c).astype(F32)


def _put_lane(col, lane_idx, width=128):
    lane = _iota2((col.shape[0], width), 1)
    return jnp.where(lane == lane_idx, col, 0.0)


def _conv_taps(raw, halo8, tb):
    ext = jnp.concatenate([halo8, raw], axis=0)
    return [raw] + [pltpu.roll(ext, s, axis=0)[8:] for s in (1, 2, 3)]


def _conv_back(dpre, nxt8, tb):
    ext = jnp.concatenate([dpre, nxt8], axis=0)
    return [dpre] + [pltpu.roll(ext, tb + 8 - s, axis=0)[:tb] for s in (1, 2, 3)]


def _rms_fwd(o, w, n):
    r = lax.rsqrt(jnp.sum(o * o, axis=-1, keepdims=True) * (1.0 / n) + EPS)
    on = o * r
    return on, r, on * w


def _rms_bwd(dy, on, r, w, n):
    don = dy * w
    return r * (don - on * (jnp.sum(don * on, axis=-1, keepdims=True) * (1.0 / n))), dy * on


def _put_cols(v, g, gw):
    z = jnp.zeros_like(v)
    return jnp.concatenate([v, z] if g == 0 else [z, v], axis=1)


def _gdn_common(pg_ref, halo8, sm, cw, prm, tb, pre=None):
    raw = pg_ref[:, 0:1536]
    if pre is None:
        taps = _conv_taps(raw, halo8, tb)
        pre = taps[0] * cw[3:4, :] + taps[1] * cw[2:3, :] + taps[2] * cw[1:2, :] + taps[3] * cw[0:1, :]
    act = _silu(pre)
    beta = _sigmoid(sm)
    sp_in = sm + prm[1:2, :]
    g = -jnp.exp(prm[0:1, :]) * _softplus(sp_in)
    gc = _dot01l(_chunk_tri(tb), g)
    return raw, pre, act, beta, sp_in, g, gc


_NN = (((2,), (1,)), ((0,), (0,)))
_NT = (((2,), (2,)), ((0,), (0,)))
_TN = (((1,), (1,)), ((0,), (0,)))


def _bdot(a, b, dn):
    return lax.dot_general(a.astype(BF16), b.astype(BF16), dn, preferred_element_type=F32)


def _binv_unit_lower(a, eye):
    r = _iota2((CH, CH), 0)
    c = _iota2((CH, CH), 1)
    d = eye - jnp.where((jnp.right_shift(r, 1) == jnp.right_shift(c, 1)), a, 0.0)
    ab = a.astype(BF16)
    zero = jnp.zeros((), BF16)
    for lb in range(1, 6):
        same = jnp.right_shift(r, lb + 1) == jnp.right_shift(c, lb + 1)
        low = (jnp.bitwise_and(jnp.right_shift(r, lb), 1) == 1) & (jnp.bitwise_and(jnp.right_shift(c, lb), 1) == 0)
        db = d.astype(BF16)
        t = _bdot(jnp.where(same & low, ab, zero), db, _NN)
        d = d - _bdot(db, t, _NN)
    return d


def _rsum(v):
    return jnp.sum(v, axis=-1, keepdims=True)


def _gdn_batch(act, beta, gc, gct, eg_all, ncb, masks):
    causal, strict, _ = masks

    def st(fn):
        return jnp.stack([fn(c, h, slice(c * CH, (c + 1) * CH)) for c in range(ncb) for h in range(GDN_H)])

    qr = st(lambda c, h, r: act[r, h * 128:(h + 1) * 128])
    kr = st(lambda c, h, r: act[r, 512 + h * 128:512 + (h + 1) * 128])
    vh = st(lambda c, h, r: act[r, 1024 + h * 128:1024 + (h + 1) * 128])
    bh = st(lambda c, h, r: beta[r, h:h + 1])
    gcol = st(lambda c, h, r: gc[r, 4 + h:5 + h])
    grow = st(lambda c, h, r: gct[4 + h:5 + h, r])
    eg = st(lambda c, h, r: eg_all[r, 4 + h:5 + h])
    glast = st(lambda c, h, r: gc[(c + 1) * CH - 1:(c + 1) * CH, 4 + h:5 + h])
    rq = lax.rsqrt(_rsum(qr * qr) + EPS)
    rk = lax.rsqrt(_rsum(kr * kr) + EPS)
    qn = qr * rq
    kh = kr * rk
    qh = qn * (GDN_D ** -0.5)
    decay = jnp.exp(jnp.where(causal, gcol - grow, NEG))
    kb = kh * bh
    kd_scale = jnp.exp(glast - gcol)
    return dict(qn=qn, rq=rq, kh=kh, rk=rk, qh=qh, vh=vh, bh=bh, eg=eg, decay=decay, kb=kb, vb=vh * bh, kg=kb * eg,
                qg=qh * eg, kd_scale=kd_scale, kdec=kh * kd_scale, egl=jnp.exp(glast),
                a=jnp.where(strict, _bdot(kb, kh, _NT) * decay, 0.0), attn=_bdot(qh, kh, _NT) * decay)


def _make_gdn_fwd(seq, tb):
    ncb = tb // CH
    nb = seq // tb
    n = ncb * GDN_H

    def body(pg_ref, sm_ref, cw_ref, prm_ref, nw_ref, oa_ref, st_ref, ti_ref, uw_ref, pre_ref, s_scr, halo_scr):
        @pl.when(pl.program_id(0) == 0)
        def _():
            s_scr[...] = jnp.zeros_like(s_scr)
            halo_scr[...] = jnp.zeros_like(halo_scr)

        masks = _masks()
        sm = sm_ref[...]
        raw, pre, act, beta, _, _, gc = _gdn_common(pg_ref, halo_scr[...], sm, cw_ref[...], prm_ref[...], tb)
        halo_scr[...] = raw[tb - 8:tb, :]
        pre_ref[...] = pre
        d = _gdn_batch(act, beta, gc, gc.T, jnp.exp(gc), ncb, masks)
        t = _binv_unit_lower(d["a"], masks[2])
        sol = _bdot(t, jnp.concatenate([d["vb"], d["kg"]], axis=2), _NN)
        ti_ref[...] = t.reshape(ncb, GDN_H, CH, CH)
        uw_ref[...] = sol.reshape(ncb, GDN_H, CH, 256)
        u, w = sol[:, :, :128], sol[:, :, 128:]
        vns = []
        for c in range(ncb):
            bs = slice(c * GDN_H, (c + 1) * GDN_H)
            s = s_scr[...]
            st_ref[c] = s
            vn = u[bs] - _bdot(w[bs], s, _NN)
            s_scr[...] = s * d["egl"][bs] + _bdot(d["kdec"][bs], vn, _TN)
            vns.append(vn)
        v_new = jnp.concatenate(vns, axis=0)
        s_prev = st_ref[...].reshape(n, 128, 128)
        o = _bdot(d["qg"], s_prev, _NN) + _bdot(d["attn"], v_new, _NN)
        _, _, y = _rms_fwd(o, nw_ref[0:1, :], GDN_D)
        for c in range(ncb):
            rows = slice(c * CH, (c + 1) * CH)
            for h in range(GDN_H):
                z = pg_ref[rows, 1536 + h * 128:1536 + (h + 1) * 128]
                oa_ref[rows, h * 128:(h + 1) * 128] = (y[c * GDN_H + h] * _silu(z)).astype(oa_ref.dtype)

    def call(pg, sm, cw, prm, nw, comm=None, comm_args=()):
        blk4 = lambda i: (i, 0, 0, 0)
        cx = _exchange_specs(comm)
        return pl.pallas_call(
            _with_exchange(body, comm, 5, 5, nb),
            grid=(nb,),
            in_specs=[
                pl.BlockSpec((tb, 2048), lambda i: (i, 0)),
                pl.BlockSpec((tb, 128), lambda i: (i, 0)),
                pl.BlockSpec((8, 1536), lambda i: (0, 0)),
                pl.BlockSpec((8, 128), lambda i: (0, 0)),
                pl.BlockSpec((8, 128), lambda i: (0, 0)),
            ] + cx["specs"],
            out_specs=[
                pl.BlockSpec((tb, 512), lambda i: (i, 0)),
                pl.BlockSpec((ncb, GDN_H, 128, 128), blk4),
                pl.BlockSpec((ncb, GDN_H, CH, CH), blk4),
                pl.BlockSpec((ncb, GDN_H, CH, 256), blk4),
                pl.BlockSpec((tb, 1536), lambda i: (i, 0)),
            ] + cx["specs"],
            out_shape=[
                jax.ShapeDtypeStruct((seq, 512), BF16),
                jax.ShapeDtypeStruct((seq // CH, GDN_H, 128, 128), F32),
                jax.ShapeDtypeStruct((seq // CH, GDN_H, CH, CH), F32),
                jax.ShapeDtypeStruct((seq // CH, GDN_H, CH, 256), F32),
                jax.ShapeDtypeStruct((seq, 1536), F32),
            ] + cx["out_shape"],
            scratch_shapes=[pltpu.VMEM((GDN_H, 128, 128), F32), pltpu.VMEM((8, 1536), F32)] + cx["scratch"],
            compiler_params=pltpu.CompilerParams(dimension_semantics=("arbitrary",), vmem_limit_bytes=VMEM_LIMIT,
                                                 has_side_effects=comm is not None),
            name="gdn_fwd" + cx["tag"],
        )(pg, sm, cw, prm, nw, *comm_args)

    return call


def _make_gdn_bwd(seq, tb):
    ncb = tb // CH
    nb = seq // tb
    hb = tb // 8
    n = ncb * GDN_H

    def body(pg_ref, pre_ref, sm_ref, cw_ref, prm_ref, nw_ref, st_ref, ti_ref, uw_ref, doa_ref,
             dpg_ref, dsm_ref, dcw_ref, dprm_ref, dnw_ref, ds_scr, nxt_scr):
        i = pl.program_id(0)

        @pl.when(i == 0)
        def _():
            ds_scr[...] = jnp.zeros_like(ds_scr)
            nxt_scr[...] = jnp.zeros_like(nxt_scr)
            dcw_ref[...] = jnp.zeros_like(dcw_ref)
            dprm_ref[...] = jnp.zeros_like(dprm_ref)
            dnw_ref[...] = jnp.zeros_like(dnw_ref)

        masks = _masks()
        strict = masks[1]
        sm = sm_ref[...]
        cw = cw_ref[...]
        prm = prm_ref[...]
        raw, pre, act, beta, sp_in, g, gc = _gdn_common(pg_ref, None, sm, cw, prm, tb, pre=pre_ref[...])
        nw = nw_ref[0:1, :]
        row_id = _iota2((CH, 1), 0)
        d = _gdn_batch(act, beta, gc, gc.T, jnp.exp(gc), ncb, masks)
        t = ti_ref[...].reshape(n, CH, CH)
        sol = uw_ref[...].reshape(n, CH, 256)
        u, w = sol[:, :, :128], sol[:, :, 128:]
        s_prev = st_ref[...].reshape(n, 128, 128)
        v_new = u - _bdot(w, s_prev, _NN)
        o = _bdot(d["qg"], s_prev, _NN) + _bdot(d["attn"], v_new, _NN)

        pairs = [(c, h) for c in range(ncb) for h in range(GDN_H)]
        z = jnp.stack([pg_ref[c * CH:(c + 1) * CH, 1536 + h * 128:1536 + (h + 1) * 128] for c, h in pairs])
        doa = jnp.stack([doa_ref[c * CH:(c + 1) * CH, h * 128:(h + 1) * 128] for c, h in pairs])
        on, r, y = _rms_fwd(o, nw, GDN_D)
        dz = doa * y * _dsilu(z)
        do, dnw_rows = _rms_bwd(doa * _silu(z), on, r, nw, GDN_D)
        dnw_acc = jnp.sum(jnp.sum(dnw_rows, axis=0), axis=0, keepdims=True)

        dvn_in = _bdot(d["attn"], do, _TN)
        qgtdo = _bdot(d["qg"], do, _TN)
        dvn_l, dkdec_l, dgl_l = [None] * ncb, [None] * ncb, [None] * ncb
        for c in reversed(range(ncb)):
            bs = slice(c * GDN_H, (c + 1) * GDN_H)
            dsn = ds_scr[...]
            dvn_c = dvn_in[bs] + _bdot(d["kdec"][bs], dsn, _NN)
            ds_scr[...] = d["egl"][bs] * dsn + qgtdo[bs] - _bdot(w[bs], dvn_c, _TN)
            dvn_l[c] = dvn_c
            dkdec_l[c] = _bdot(v_new[bs], dsn, _NT)
            dgl_l[c] = d["egl"][bs] * jnp.sum(_rsum(s_prev[bs] * dsn), axis=1, keepdims=True)
        dvn = jnp.concatenate(dvn_l, axis=0)
        dkdec = jnp.concatenate(dkdec_l, axis=0)
        dglast = jnp.concatenate(dgl_l, axis=0)

        dqg = _bdot(do, s_prev, _NT)
        dattn = _bdot(do, v_new, _NT)
        dw = -_bdot(dvn, s_prev, _NT)
        drhs = _bdot(t, jnp.concatenate([dvn, dw], axis=2), _TN)
        dvb, dkg = drhs[:, :, :128], drhs[:, :, 128:]
        da = jnp.where(strict, -(_bdot(dvb, u, _NT) + _bdot(dkg, w, _NT)), 0.0)
        dp = da * d["decay"]
        dq_m = dattn * d["decay"]
        m = da * d["a"] + dattn * d["attn"]
        upper_tri = jnp.broadcast_to((_iota2((CH, CH), 1) >= _iota2((CH, CH), 0)).astype(BF16), (n, CH, CH))
        dg_in = _rsum(jnp.where(strict, _bdot(upper_tri, m, _NN), 0.0))
        dkb = _bdot(dp, d["kh"], _NN) + dkg * d["eg"]
        kdk_row = _rsum(dkdec * d["kdec"])
        dk = _bdot(dp, d["kb"], _TN) + _bdot(dq_m, d["qh"], _TN) + dkdec * d["kd_scale"] + dkb * d["bh"]
        dq = _bdot(dq_m, d["kh"], _NN) + dqg * d["eg"]
        dglast = dglast + jnp.sum(kdk_row, axis=1, keepdims=True)
        dgcol = (_rsum(dqg * d["qg"]) + _rsum(dkg * d["kg"]) - kdk_row + jnp.where(row_id == CH - 1, dglast, 0.0))
        dbeta = _rsum(dkb * d["kh"]) + _rsum(dvb * d["vh"])
        dn = dq * (GDN_D ** -0.5)
        dact_q = d["rq"] * (dn - d["qn"] * _rsum(dn * d["qn"]))
        dact_k = d["rk"] * (dk - d["kh"] * _rsum(dk * d["kh"]))
        dact_v = dvb * d["bh"]

        def lanes(v, lane0):
            return jnp.concatenate(
                [sum(_put_lane(v[c * GDN_H + h], lane0 + h) for h in range(GDN_H)) for c in range(ncb)], axis=0)

        def tokens(v):
            return jnp.concatenate(
                [jnp.concatenate([v[c * GDN_H + h] for h in range(GDN_H)], axis=1) for c in range(ncb)], axis=0)

        dbeta_all = lanes(dbeta, 0)
        dg = _dot01l(_chunk_tri(tb, upper=True), lanes(dgcol, 4)) + lanes(dg_in, 4)
        neg_ea = -jnp.exp(prm[0:1, :])
        da_raw = dg * neg_ea * _sigmoid(sp_in)
        db_raw = dbeta_all * beta * (1.0 - beta)
        dsm_ref[...] = (da_raw + db_raw).astype(dsm_ref.dtype)
        lane8 = _iota2((8, 128), 1)
        sub8 = _iota2((8, 128), 0)
        dalog = jnp.sum(dg * g, axis=0, keepdims=True)
        ddtb = jnp.sum(da_raw, axis=0, keepdims=True)
        dprm_ref[...] += jnp.where(sub8 == 0, dalog, 0.0) + jnp.where(sub8 == 1, ddtb, 0.0)
        dnw_ref[...] += jnp.where(sub8 == 0, dnw_acc, 0.0)

        dact = jnp.concatenate([tokens(dact_q), tokens(dact_k), tokens(dact_v)], axis=1)
        dpre = dact * _dsilu(pre)
        back = _conv_back(dpre, nxt_scr[...], tb)
        nxt_scr[...] = dpre[0:8, :]
        draw = back[0] * cw[3:4, :] + back[1] * cw[2:3, :] + back[2] * cw[1:2, :] + back[3] * cw[0:1, :]
        dpg_ref[:, 0:1536] = draw.astype(dpg_ref.dtype)
        dpg_ref[:, 1536:2048] = tokens(dz).astype(dpg_ref.dtype)
        sub_c = _iota2((8, 1536), 0)
        dcw_new = jnp.zeros((8, 1536), F32)
        for s_ in range(CONV_W):
            dcw_new = dcw_new + jnp.where(sub_c == 3 - s_, jnp.sum(back[s_] * raw, axis=0, keepdims=True), 0.0)
        dcw_ref[...] += dcw_new

    def call(pg, pre, sm, cw, prm, nw, st, ti, uw, doa, comm=None, comm_args=()):
        rev = lambda i: (nb - 1 - i, 0)
        const = lambda i: (0, 0)
        cx = _exchange_specs(comm)
        return pl.pallas_call(
            _with_exchange(body, comm, 10, 5, nb),
            grid=(nb,),
            in_specs=[
                pl.BlockSpec((tb, 2048), rev),
                pl.BlockSpec((tb, 1536), rev),
                pl.BlockSpec((tb, 128), rev),
                pl.BlockSpec((8, 1536), const),
                pl.BlockSpec((8, 128), const),
                pl.BlockSpec((8, 128), const),
                pl.BlockSpec((ncb, GDN_H, 128, 128), lambda i: (nb - 1 - i, 0, 0, 0)),
                pl.BlockSpec((ncb, GDN_H, CH, CH), lambda i: (nb - 1 - i, 0, 0, 0)),
                pl.BlockSpec((ncb, GDN_H, CH, 256), lambda i: (nb - 1 - i, 0, 0, 0)),
                pl.BlockSpec((tb, 512), rev),
            ] + cx["specs"],
            out_specs=[
                pl.BlockSpec((tb, 2048), rev),
                pl.BlockSpec((tb, 128), rev),
                pl.BlockSpec((8, 1536), const),
                pl.BlockSpec((8, 128), const),
                pl.BlockSpec((8, 128), const),
            ] + cx["specs"],
            out_shape=[
                jax.ShapeDtypeStruct((seq, 2048), BF16),
                jax.ShapeDtypeStruct((seq, 128), BF16),
                jax.ShapeDtypeStruct((8, 1536), F32),
                jax.ShapeDtypeStruct((8, 128), F32),
                jax.ShapeDtypeStruct((8, 128), F32),
            ] + cx["out_shape"],
            scratch_shapes=[pltpu.VMEM((GDN_H, 128, 128), F32), pltpu.VMEM((8, 1536), F32)] + cx["scratch"],
            compiler_params=pltpu.CompilerParams(dimension_semantics=("arbitrary",), vmem_limit_bytes=VMEM_LIMIT,
                                                 has_side_effects=comm is not None),
            name="gdn_bwd" + cx["tag"],
        )(pg, pre, sm, cw, prm, nw, st, ti, uw, doa, *comm_args)

    return call


def _expand_mat():
    r = _iota2((128, SSD_W), 0)
    c = _iota2((128, SSD_W), 1)
    return (jnp.right_shift(c, 6) == r).astype(F32)


def _reduce_heads(v, e):
    vh, vl = _split(v)
    eb = e.astype(BF16)
    nt = (((1,), (1,)), ((), ()))
    return (lax.dot_general(vh, eb, nt, preferred_element_type=F32)
            + lax.dot_general(vl, eb, nt, preferred_element_type=F32))


def _reduce_heads1(v, e):
    nt = (((1,), (1,)), ((), ()))
    return lax.dot_general(v.astype(BF16), e.astype(BF16), nt, preferred_element_type=F32)


def _row8(v):
    return jnp.broadcast_to(v, (8, v.shape[1]))


def _ssd_common(ps_ref, halo8, ss, cw, cb, prm, tb, pre=None):
    raw = ps_ref[:, 0:1536]
    taps = None
    if pre is None:
        taps = _conv_taps(raw, halo8, tb)
        pre = taps[0] * cw[3:4, :] + taps[1] * cw[2:3, :] + taps[2] * cw[1:2, :] + taps[3] * cw[0:1, :] + cb[0:1, :]
    act = _silu(pre)
    dt_in = ss + prm[1:2, :]
    dt = _softplus(dt_in)
    a = dt * (-jnp.exp(prm[0:1, :]))
    acum = _dot01l(_chunk_tri(tb), a)
    e = _expand_mat()
    dt_e = _dot01r(dt, e)
    xdt = act[:, 0:SSD_W] * dt_e
    ea_e = _dot01r(jnp.exp(acum), e)
    d_e = _dot01r(_row8(prm[2:3, :]), e)[0:1, :]
    return raw, taps, pre, act, dt_in, dt, a, acum, e, dt_e, xdt, ea_e, d_e


def _ssd_chunk(act, acum, act_t, e, c):
    r0 = c * CH
    rows = slice(r0, r0 + CH)
    alast = acum[r0 + CH - 1:r0 + CH, :]
    wdec = jnp.exp(alast - acum[rows, :])
    wd_e = _dot01r(wdec, e)
    eal_e = _dot01r(_row8(jnp.exp(alast)), e)[0:1, :]
    return rows, wd_e, eal_e


def _ssd_lmat(acum, act_t, c, h, causal):
    r0 = c * CH
    acol = acum[r0:r0 + CH, h:h + 1]
    arow = act_t[h:h + 1, r0:r0 + CH]
    return jnp.exp(jnp.where(causal, acol - arow, NEG))


def _make_ssd_fwd(seq, tb):
    ncb = tb // CH
    nb = seq // tb
    hg = SSD_H // SSD_G
    gw = SSD_W // SSD_G

    def body(ps_ref, ss_ref, cw_ref, cb_ref, prm_ref, nw_ref, ob_ref, st_ref, pre_ref, y_ref, hs_scr, halo_scr):
        @pl.when(pl.program_id(0) == 0)
        def _():
            hs_scr[...] = jnp.zeros_like(hs_scr)
            halo_scr[...] = jnp.zeros_like(halo_scr)

        causal, _, _ = _masks()
        (raw, _, pre, act, _, _, _, acum, e, _, xdt, ea_e, d_e) = _ssd_common(
            ps_ref, halo_scr[...], ss_ref[...], cw_ref[...], cb_ref[...], prm_ref[...], tb)
        halo_scr[...] = raw[tb - 8:tb, :]
        pre_ref[...] = pre
        act_t = acum.T
        nw = nw_ref[0:1, :]
        for c in range(ncb):
            rows, wd_e, eal_e = _ssd_chunk(act, acum, act_t, e, c)
            st_ref[c] = hs_scr[...]
            ys = []
            for g in range(SSD_G):
                gc_ = slice(g * gw, (g + 1) * gw)
                bg = act[rows, SSD_W + g * 128:SSD_W + (g + 1) * 128]
                cg = act[rows, SSD_W + 256 + g * 128:SSD_W + 256 + (g + 1) * 128]
                cbm = _dot_nt(cg, bg)
                hs = hs_scr[:, gc_]
                yin = _dot(cg, hs)
                yh = []
                for hh in range(hg):
                    h = g * hg + hh
                    lm = _ssd_lmat(acum, act_t, c, h, causal)
                    yh.append(_dot(cbm * lm, xdt[rows, h * SSD_P:(h + 1) * SSD_P]))
                ys.append(jnp.concatenate(yh, axis=1) + yin * ea_e[rows, gc_])
                hs_scr[:, gc_] = hs * eal_e[:, gc_] + _dot_tn(bg, xdt[rows, gc_] * wd_e[:, gc_])
            y = jnp.concatenate(ys, axis=1) + act[rows, 0:SSD_W] * d_e
            y_ref[rows, :] = y
            yz = y * _silu(ps_ref[rows, 1536:2560])
            outs = [_rms_fwd(yz[:, g * gw:(g + 1) * gw], nw[:, g * gw:(g + 1) * gw], gw)[2] for g in range(SSD_G)]
            ob_ref[rows, :] = jnp.concatenate(outs, axis=1).astype(ob_ref.dtype)

    def call(ps, ss, cw, cb, prm, nw):
        const = lambda i: (0, 0)
        return pl.pallas_call(
            body,
            grid=(nb,),
            in_specs=[
                pl.BlockSpec((tb, 2560), lambda i: (i, 0)),
                pl.BlockSpec((tb, 128), lambda i: (i, 0)),
                pl.BlockSpec((8, 1536), const),
                pl.BlockSpec((8, 1536), const),
                pl.BlockSpec((8, 128), const),
                pl.BlockSpec((8, SSD_W), const),
            ],
            out_specs=[
                pl.BlockSpec((tb, SSD_W), lambda i: (i, 0)),
                pl.BlockSpec((ncb, SSD_N, SSD_W), lambda i: (i, 0, 0)),
                pl.BlockSpec((tb, 1536), lambda i: (i, 0)),
                pl.BlockSpec((tb, SSD_W), lambda i: (i, 0)),
            ],
            out_shape=[
                jax.ShapeDtypeStruct((seq, SSD_W), BF16),
                jax.ShapeDtypeStruct((seq // CH, SSD_N, SSD_W), F32),
                jax.ShapeDtypeStruct((seq, 1536), F32),
                jax.ShapeDtypeStruct((seq, SSD_W), F32),
            ],
            scratch_shapes=[pltpu.VMEM((SSD_N, SSD_W), F32), pltpu.VMEM((8, 1536), F32)],
            compiler_params=pltpu.CompilerParams(dimension_semantics=("arbitrary",), vmem_limit_bytes=VMEM_LIMIT),
            name="ssd_fwd",
        )(ps, ss, cw, cb, prm, nw)

    return call


def _make_ssd_bwd(seq, tb):
    ncb = tb // CH
    nb = seq // tb
    hb = tb // 8
    hg = SSD_H // SSD_G
    gw = SSD_W // SSD_G

    def body(ps_ref, pre_ref, y_ref, ss_ref, cw_ref, cb_ref, prm_ref, nw_ref, st_ref, dob_ref,
             dps_ref, dss_ref, dcw_ref, dcb_ref, dprm_ref, dnw_ref, dhs_scr, nxt_scr):
        i = pl.program_id(0)

        @pl.when(i == 0)
        def _():
            dhs_scr[...] = jnp.zeros_like(dhs_scr)
            nxt_scr[...] = jnp.zeros_like(nxt_scr)
            dcw_ref[...] = jnp.zeros_like(dcw_ref)
            dcb_ref[...] = jnp.zeros_like(dcb_ref)
            dprm_ref[...] = jnp.zeros_like(dprm_ref)
            dnw_ref[...] = jnp.zeros_like(dnw_ref)

        causal, _, _ = _masks()
        cw = cw_ref[...]
        prm = prm_ref[...]
        (raw, _, pre, act, dt_in, dt, a, acum, e, dt_e, xdt, ea_e, d_e) = _ssd_common(
            ps_ref, None, ss_ref[...], cw, cb_ref[...], prm, tb, pre=pre_ref[...])
        act_t = acum.T
        nw = nw_ref[0:1, :]
        row_id = _iota2((CH, 1), 0)

        dx_l, db_l, dc_l, dz_l, dacum_l, ddt_l, da_in_l = ([None] * ncb for _ in range(7))
        upper_tri = (_iota2((CH, CH), 1) >= _iota2((CH, CH), 0)).astype(F32)
        below = jnp.bitwise_and(_iota2((CH, gw), 1), CH - 1) < _iota2((CH, gw), 0)
        dnw_acc = jnp.zeros((1, SSD_W), F32)
        dd_acc = jnp.zeros((1, SSD_W), F32)

        for c in reversed(range(ncb)):
            rows, wd_e, eal_e = _ssd_chunk(act, acum, act_t, e, c)
            xc = act[rows, 0:SSD_W]
            z = ps_ref[rows, 1536:2560]
            dob = dob_ref[rows, :]
            sz = _silu(z)
            dy_g, dz_g, zacc_g, dxdt_g, dal_g, db_g, dc_g, da_in_g = [], [], [], [], [], [], [], []
            for g in range(SSD_G):
                gc_ = slice(g * gw, (g + 1) * gw)
                bg = act[rows, SSD_W + g * 128:SSD_W + (g + 1) * 128]
                cg = act[rows, SSD_W + 256 + g * 128:SSD_W + 256 + (g + 1) * 128]
                cbm = _dot_nt(cg, bg)
                hs = st_ref[c, :, gc_]
                yin = _dot(cg, hs)
                lmats = [_ssd_lmat(acum, act_t, c, g * hg + hh, causal) for hh in range(hg)]
                ea_g = ea_e[rows, gc_]
                y = y_ref[rows, gc_]
                yz = y * sz[:, gc_]
                on, r, _ = _rms_fwd(yz, nw[:, gc_], gw)
                dyz, dnw_rows = _rms_bwd(dob[:, gc_], on, r, nw[:, gc_], gw)
                dnw_acc = dnw_acc + _put_cols(jnp.sum(dnw_rows, axis=0, keepdims=True), g, gw)
                dy = dyz * sz[:, gc_]
                dz_g.append(dyz * y * _dsilu(z[:, gc_]))
                dd_acc = dd_acc + _put_cols(jnp.sum(dy * xc[:, gc_], axis=0, keepdims=True), g, gw)
                dhs_n = dhs_scr[:, gc_]
                dyin = dy * ea_g
                dcg = _dot_nt(dyin, hs)
                xw = xdt[rows, gc_] * wd_e[:, gc_]
                dbg = _dot_nt(xw, dhs_n)
                dxw = _dot(bg, dhs_n)
                dhs_scr[:, gc_] = dhs_n * eal_e[:, gc_] + _dot_tn(cg, dyin)
                dal_g.append(jnp.sum(hs * dhs_n, axis=0, keepdims=True) * eal_e[:, gc_]
                             + jnp.sum(dxw * xw, axis=0, keepdims=True))
                dxi, ms, dcbm = [], [], jnp.zeros((CH, CH), F32)
                for hh in range(hg):
                    h = g * hg + hh
                    hc = slice(hh * SSD_P, (hh + 1) * SSD_P)
                    dyh = dy[:, hc]
                    lm = cbm * lmats[hh]
                    dxi.append(_dot_tn(lm, dyh))
                    dlm = _dot_nt(dyh, xdt[rows, h * SSD_P:(h + 1) * SSD_P])
                    ms.append(dlm * lm)
                    dcbm = dcbm + dlm * lmats[hh]
                dx_intra = jnp.concatenate(dxi, axis=1)
                ncat = _dot(upper_tri, jnp.concatenate(ms, axis=1))
                da_in_g.append(jnp.where(below, ncat, 0.0))
                zacc_g.append(dy * yin * ea_g - dxw * xw)
                dxdt_g.append(dx_intra + dxw * wd_e[:, gc_])
                dy_g.append(dy)
                db_g.append(dbg + _dot_tn(dcbm, cg))
                dc_g.append(dcg + _dot(dcbm, bg))
            dy = jnp.concatenate(dy_g, axis=1)
            dxdt = jnp.concatenate(dxdt_g, axis=1)
            dx_l[c] = dxdt * dt_e[rows, :] + dy * d_e
            db_l[c] = jnp.concatenate(db_g, axis=1)
            dc_l[c] = jnp.concatenate(dc_g, axis=1)
            dz_l[c] = jnp.concatenate(dz_g, axis=1)
            ddt_l[c] = _reduce_heads1(dxdt * xc, e)
            dalast = _reduce_heads(_row8(jnp.concatenate(dal_g, axis=1)), e)[0:1, :]
            dacum_l[c] = _reduce_heads(jnp.concatenate(zacc_g, axis=1), e) + jnp.where(row_id == CH - 1, dalast, 0.0)
            da_in_l[c] = _reduce_heads1(jnp.concatenate(da_in_g, axis=1), e)

        dacum_all = jnp.concatenate(dacum_l, axis=0)
        da = _dot01l(_chunk_tri(tb, upper=True), dacum_all) + jnp.concatenate(da_in_l, axis=0)
        neg_ea = -jnp.exp(prm[0:1, :])
        ddt = jnp.concatenate(ddt_l, axis=0) + da * neg_ea
        ddt_in = ddt * _sigmoid(dt_in)
        dss_ref[...] = ddt_in.astype(dss_ref.dtype)
        sub8 = _iota2((8, 128), 0)
        dalog = jnp.sum(da * a, axis=0, keepdims=True)
        ddtb = jnp.sum(ddt_in, axis=0, keepdims=True)
        dd = _reduce_heads(_row8(dd_acc), e)[0:1, :]
        dprm_ref[...] += (jnp.where(sub8 == 0, dalog, 0.0) + jnp.where(sub8 == 1, ddtb, 0.0)
                          + jnp.where(sub8 == 2, dd, 0.0))
        dnw_ref[...] += jnp.where(_iota2((8, SSD_W), 0) == 0, dnw_acc, 0.0)

        dact = jnp.concatenate([jnp.concatenate(dx_l, axis=0), jnp.concatenate(db_l, axis=0),
                                jnp.concatenate(dc_l, axis=0)], axis=1)
        dpre = dact * _dsilu(pre)
        back = _conv_back(dpre, nxt_scr[...], tb)
        nxt_scr[...] = dpre[0:8, :]
        draw = back[0] * cw[3:4, :] + back[1] * cw[2:3, :] + back[2] * cw[1:2, :] + back[3] * cw[0:1, :]
        dps_ref[:, 0:1536] = draw.astype(dps_ref.dtype)
        dps_ref[:, 1536:2560] = jnp.concatenate(dz_l, axis=0).astype(dps_ref.dtype)
        sub_c = _iota2((8, 1536), 0)
        dcw_new = jnp.zeros((8, 1536), F32)
        for s_ in range(CONV_W):
            dcw_new = dcw_new + jnp.where(sub_c == 3 - s_, jnp.sum(back[s_] * raw, axis=0, keepdims=True), 0.0)
        dcw_ref[...] += dcw_new
        dcb_ref[...] += jnp.where(sub_c == 0, jnp.sum(dpre, axis=0, keepdims=True), 0.0)

    def call(ps, pre, y, ss, cw, cb, prm, nw, st, dob, comm=None, comm_args=()):
        rev = lambda i: (nb - 1 - i, 0)
        const = lambda i: (0, 0)
        cx = _exchange_specs(comm)
        return pl.pallas_call(
            _with_exchange(body, comm, 10, 6, nb),
            grid=(nb,),
            in_specs=[
                pl.BlockSpec((tb, 2560), rev),
                pl.BlockSpec((tb, 1536), rev),
                pl.BlockSpec((tb, SSD_W), rev),
                pl.BlockSpec((tb, 128), rev),
                pl.BlockSpec((8, 1536), const),
                pl.BlockSpec((8, 1536), const),
                pl.BlockSpec((8, 128), const),
                pl.BlockSpec((8, SSD_W), const),
                pl.BlockSpec((ncb, SSD_N, SSD_W), lambda i: (nb - 1 - i, 0, 0)),
                pl.BlockSpec((tb, SSD_W), rev),
            ] + cx["specs"],
            out_specs=[
                pl.BlockSpec((tb, 2560), rev),
                pl.BlockSpec((tb, 128), rev),
                pl.BlockSpec((8, 1536), const),
                pl.BlockSpec((8, 1536), const),
                pl.BlockSpec((8, 128), const),
                pl.BlockSpec((8, SSD_W), const),
            ] + cx["specs"],
            out_shape=[
                jax.ShapeDtypeStruct((seq, 2560), BF16),
                jax.ShapeDtypeStruct((seq, 128), BF16),
                jax.ShapeDtypeStruct((8, 1536), F32),
                jax.ShapeDtypeStruct((8, 1536), F32),
                jax.ShapeDtypeStruct((8, 128), F32),
                jax.ShapeDtypeStruct((8, SSD_W), F32),
            ] + cx["out_shape"],
            scratch_shapes=[pltpu.VMEM((SSD_N, SSD_W), F32), pltpu.VMEM((8, 1536), F32)] + cx["scratch"],
            compiler_params=pltpu.CompilerParams(dimension_semantics=("arbitrary",), vmem_limit_bytes=VMEM_LIMIT,
                                                 has_side_effects=comm is not None),
            name="ssd_bwd" + cx["tag"],
        )(ps, pre, y, ss, cw, cb, prm, nw, st, dob, *comm_args)

    return call


def _ret_consts(h):
    lg = math.log(1.0 - 2.0 ** (-5.0 - h))
    r = _iota2((CH, CH), 0)
    c = _iota2((CH, CH), 1)
    rel = (r - c).astype(F32)
    dmat = jnp.where(r >= c, jnp.exp(jnp.maximum(rel, 0.0) * lg), 0.0)
    idx = _iota2((CH, 1), 0).astype(F32)
    qdec = jnp.exp((idx + 1.0) * lg)
    kdec = jnp.exp((CH - 1.0 - idx) * lg)
    cdec = math.exp(CH * lg)
    return dmat, qdec, kdec, cdec


def _ret_batch(pr_ref, cc_ref, ss_ref, ncb):
    pairs = [(c, h) for c in range(ncb) for h in range(RET_H)]

    def st(off):
        return jnp.stack([pr_ref[c * CH:(c + 1) * CH, off + h * 128:off + (h + 1) * 128] for c, h in pairs])

    cc = jnp.stack([cc_ref[c * CH:(c + 1) * CH, :] for c, _ in pairs])
    ss = jnp.stack([ss_ref[c * CH:(c + 1) * CH, :] for c, _ in pairs])
    consts = [_ret_consts(h) for h in range(RET_H)]
    dmat = jnp.stack([consts[h][0] for _, h in pairs])
    qdec = jnp.stack([consts[h][1] for _, h in pairs])
    kdec = jnp.stack([consts[h][2] for _, h in pairs])
    cdec = jnp.stack([jnp.full((1, 1), consts[h][3], F32) for h in range(RET_H)])
    q = _rot(st(0), cc, ss)
    k = _rot(st(512), cc, ss) * (RET_D ** -0.5)
    return dict(q=q, k=k, v=st(1024), z=st(1536), cc=cc, ss=ss, dmat=dmat, qdec=qdec, kdec=kdec, cdec=cdec,
                s=_bdot(q, k, _NT) * dmat)


def _rot(t, cc, ss):
    return t * cc + pltpu.roll(t, 64, axis=t.ndim - 1) * ss


def _rot_bwd(d, cc, ss):
    return d * cc + pltpu.roll(d * ss, 64, axis=d.ndim - 1)


def _make_ret_fwd(seq, tb):
    ncb = tb // CH
    nb = seq // tb

    def body(pr_ref, cc_ref, ss_ref, nw_ref, oc_ref, st_ref, r_scr):
        @pl.when(pl.program_id(0) == 0)
        def _():
            r_scr[...] = jnp.zeros_like(r_scr)

        d = _ret_batch(pr_ref, cc_ref, ss_ref, ncb)
        kd = d["k"] * d["kdec"]
        for c in range(ncb):
            bs = slice(c * RET_H, (c + 1) * RET_H)
            rs = r_scr[...]
            st_ref[c] = rs
            r_scr[...] = rs * d["cdec"] + _bdot(kd[bs], d["v"][bs], _TN)
        r_prev = st_ref[...].reshape(ncb * RET_H, 128, 128)
        o = _bdot(d["s"], d["v"], _NN) + _bdot(d["q"], r_prev, _NN) * d["qdec"]
        _, _, y = _rms_fwd(o, nw_ref[0:1, :], RET_D)
        out = y * _silu(d["z"])
        for c in range(ncb):
            for h in range(RET_H):
                oc_ref[c * CH:(c + 1) * CH, h * 128:(h + 1) * 128] = out[c * RET_H + h].astype(oc_ref.dtype)

    def call(pr, cc, ss, nw):
        return pl.pallas_call(
            body,
            grid=(nb,),
            in_specs=[
                pl.BlockSpec((tb, 2048), lambda i: (i, 0)),
                pl.BlockSpec((tb, 128), lambda i: (i, 0)),
                pl.BlockSpec((tb, 128), lambda i: (i, 0)),
                pl.BlockSpec((8, 128), lambda i: (0, 0)),
            ],
            out_specs=[
                pl.BlockSpec((tb, 512), lambda i: (i, 0)),
                pl.BlockSpec((ncb, RET_H, 128, 128), lambda i: (i, 0, 0, 0)),
            ],
            out_shape=[
                jax.ShapeDtypeStruct((seq, 512), BF16),
                jax.ShapeDtypeStruct((seq // CH, RET_H, 128, 128), F32),
            ],
            scratch_shapes=[pltpu.VMEM((RET_H, 128, 128), F32)],
            compiler_params=pltpu.CompilerParams(dimension_semantics=("arbitrary",), vmem_limit_bytes=VMEM_LIMIT),
            name="ret_fwd",
        )(pr, cc, ss, nw)

    return call


def _make_ret_bwd(seq, tb):
    ncb = tb // CH
    nb = seq // tb

    def body(pr_ref, cc_ref, ss_ref, nw_ref, st_ref, doc_ref, dpr_ref, dnw_ref, dr_scr):
        @pl.when(pl.program_id(0) == 0)
        def _():
            dr_scr[...] = jnp.zeros_like(dr_scr)
            dnw_ref[...] = jnp.zeros_like(dnw_ref)

        nw = nw_ref[0:1, :]
        scale = RET_D ** -0.5
        n = ncb * RET_H
        d = _ret_batch(pr_ref, cc_ref, ss_ref, ncb)
        q, k, v, z, s = d["q"], d["k"], d["v"], d["z"], d["s"]
        r_prev = st_ref[...].reshape(n, 128, 128)
        o = _bdot(s, v, _NN) + _bdot(q, r_prev, _NN) * d["qdec"]
        doc = jnp.stack([doc_ref[c * CH:(c + 1) * CH, h * 128:(h + 1) * 128]
                         for c in range(ncb) for h in range(RET_H)])
        on, r, y = _rms_fwd(o, nw, RET_D)
        dz = doc * y * _dsilu(z)
        do, dnw_rows = _rms_bwd(doc * _silu(z), on, r, nw, RET_D)
        dnw_acc = jnp.sum(jnp.sum(dnw_rows, axis=0), axis=0, keepdims=True)
        dqd = do * d["qdec"]
        qtd = _bdot(q, dqd, _TN)
        drn_l = [None] * ncb
        for c in reversed(range(ncb)):
            drn_l[c] = dr_scr[...]
            dr_scr[...] = qtd[c * RET_H:(c + 1) * RET_H] + d["cdec"] * drn_l[c]
        drn = jnp.concatenate(drn_l, axis=0)
        ds = _bdot(do, v, _NT) * d["dmat"]
        dq = _rot_bwd(_bdot(ds, k, _NN) + _bdot(dqd, r_prev, _NT), d["cc"], d["ss"])
        dk = _rot_bwd((_bdot(ds, q, _TN) + _bdot(v, drn, _NT) * d["kdec"]) * scale, d["cc"], d["ss"])
        dv = _bdot(s, do, _TN) + _bdot(k * d["kdec"], drn, _NN)
        for c in range(ncb):
            rows = slice(c * CH, (c + 1) * CH)
            for h in range(RET_H):
                b = c * RET_H + h
                for j, val in enumerate((dq, dk, dv, dz)):
                    dpr_ref[rows, j * 512 + h * 128:j * 512 + (h + 1) * 128] = val[b].astype(dpr_ref.dtype)
        dnw_ref[...] += jnp.where(_iota2((8, 128), 0) == 0, dnw_acc, 0.0)

    def call(pr, cc, ss, nw, st, doc):
        rev = lambda i: (nb - 1 - i, 0)
        return pl.pallas_call(
            body,
            grid=(nb,),
            in_specs=[
                pl.BlockSpec((tb, 2048), rev),
                pl.BlockSpec((tb, 128), rev),
                pl.BlockSpec((tb, 128), rev),
                pl.BlockSpec((8, 128), lambda i: (0, 0)),
                pl.BlockSpec((ncb, RET_H, 128, 128), lambda i: (nb - 1 - i, 0, 0, 0)),
                pl.BlockSpec((tb, 512), rev),
            ],
            out_specs=[
                pl.BlockSpec((tb, 2048), rev),
                pl.BlockSpec((8, 128), lambda i: (0, 0)),
            ],
            out_shape=[
                jax.ShapeDtypeStruct((seq, 2048), BF16),
                jax.ShapeDtypeStruct((8, 128), F32),
            ],
            scratch_shapes=[pltpu.VMEM((RET_H, 128, 128), F32)],
            compiler_params=pltpu.CompilerParams(dimension_semantics=("arbitrary",), vmem_limit_bytes=VMEM_LIMIT),
            name="ret_bwd",
        )(pr, cc, ss, nw, st, doc)

    return call


def _rope_tables(seq):
    half = RET_D // 2
    inv = ROPE_BASE ** (-jnp.arange(half, dtype=F32) / half)
    ang = jnp.arange(seq, dtype=jnp.int32).astype(F32)[:, None] * inv[None, :]
    cos, sin = jnp.cos(ang), jnp.sin(ang)
    return jnp.concatenate([cos, cos], axis=1), jnp.concatenate([-sin, sin], axis=1)


SEG_G, SEG_S, SEG_R, SEG_GS, SEG_SS = (0, 2048), (2048, 4608), (4608, 6656), (6656, 6784), (6784, 6912)
NP = 6912
SEGS = (SEG_G, SEG_S, SEG_R, SEG_GS, SEG_SS)


def _resident(shape):
    return pl.BlockSpec(shape, lambda i: (0,) * len(shape), pipeline_mode=pl.Buffered(1))


def _make_inproj(seq, tl):
    def body(x_ref, pn_ref, w_ref, pg_ref, ps_ref, pr_ref, gs_ref, ss_ref, ht_ref):
        x = x_ref[...]
        _, _, hn = _rms_fwd(x, pn_ref[0:1, :], D_MODEL)
        h = hn.astype(BF16)
        ht_ref[...] = hn.T.astype(BF16)
        for (a, b), o_ref in zip(SEGS, (pg_ref, ps_ref, pr_ref, gs_ref, ss_ref)):
            o_ref[...] = jnp.dot(h, w_ref[:, a:b], preferred_element_type=F32)

    def call(x, pn, w, comm=None, comm_args=()):
        row = lambda i: (i, 0)
        cx = _exchange_specs(comm)
        return pl.pallas_call(
            _with_exchange(body, comm, 3, 6, seq // tl),
            grid=(seq // tl,),
            in_specs=[pl.BlockSpec((tl, D_MODEL), row), _resident((8, D_MODEL)), _resident((D_MODEL, NP))]
            + cx["specs"],
            out_specs=[pl.BlockSpec((tl, b - a), row) for a, b in SEGS]
            + [pl.BlockSpec((D_MODEL, tl), lambda i: (0, i))] + cx["specs"],
            out_shape=[jax.ShapeDtypeStruct((seq, b - a), F32) for a, b in SEGS]
            + [jax.ShapeDtypeStruct((D_MODEL, seq), BF16)] + cx["out_shape"],
            scratch_shapes=cx["scratch"],
            compiler_params=pltpu.CompilerParams(dimension_semantics=("arbitrary",), vmem_limit_bytes=VMEM_LIMIT,
                                                 has_side_effects=comm is not None),
            name="inproj" + cx["tag"],
        )(x, pn, w, *comm_args)

    return call


def _make_outproj(seq, tl):
    def body(oa_ref, ob_ref, oc_ref, w_ref, x_ref, qn_ref, out_ref, xn_ref):
        out = (jnp.dot(oa_ref[...], w_ref[0:512, :], preferred_element_type=F32)
               + jnp.dot(ob_ref[...], w_ref[512:1536, :], preferred_element_type=F32)
               + jnp.dot(oc_ref[...], w_ref[1536:2048, :], preferred_element_type=F32))
        out_ref[...] = out
        _, _, y = _rms_fwd(out, qn_ref[0:1, :], D_MODEL)
        xn_ref[...] = x_ref[...] + y

    def call(oa, ob, oc, w, x, qn):
        row = lambda i: (i, 0)
        return pl.pallas_call(
            body,
            grid=(seq // tl,),
            in_specs=[pl.BlockSpec((tl, 512), row), pl.BlockSpec((tl, 1024), row), pl.BlockSpec((tl, 512), row),
                      _resident((2048, D_MODEL)), pl.BlockSpec((tl, D_MODEL), row), _resident((8, D_MODEL))],
            out_specs=[pl.BlockSpec((tl, D_MODEL), row), pl.BlockSpec((tl, D_MODEL), row)],
            out_shape=[jax.ShapeDtypeStruct((seq, D_MODEL), F32), jax.ShapeDtypeStruct((seq, D_MODEL), F32)],
            compiler_params=pltpu.CompilerParams(dimension_semantics=("arbitrary",), vmem_limit_bytes=VMEM_LIMIT),
            name="outproj",
        )(oa, ob, oc, w, x, qn)

    return call


def _make_outproj_loss(seq, tl):
    def body(oa_ref, ob_ref, oc_ref, w_ref, x_ref, qn_ref, t_ref, out_ref, dy_ref, loss_ref):
        @pl.when(pl.program_id(0) == 0)
        def _():
            loss_ref[...] = jnp.zeros_like(loss_ref)

        out = (jnp.dot(oa_ref[...], w_ref[0:512, :], preferred_element_type=F32)
               + jnp.dot(ob_ref[...], w_ref[512:1536, :], preferred_element_type=F32)
               + jnp.dot(oc_ref[...], w_ref[1536:2048, :], preferred_element_type=F32))
        out_ref[...] = out
        _, _, y = _rms_fwd(out, qn_ref[0:1, :], D_MODEL)
        err = (x_ref[...] + y) - t_ref[...]
        dy_ref[...] = err * (1.0 / D_MODEL)
        part = jnp.sum(jnp.sum(err * err, axis=1, keepdims=True), axis=0, keepdims=True) * (0.5 / D_MODEL)
        loss_ref[...] += jnp.where((_iota2((8, 128), 0) == 0) & (_iota2((8, 128), 1) == 0), part, 0.0)

    def call(oa, ob, oc, w, x, qn, t):
        row = lambda i: (i, 0)
        return pl.pallas_call(
            body,
            grid=(seq // tl,),
            in_specs=[pl.BlockSpec((tl, 512), row), pl.BlockSpec((tl, 1024), row), pl.BlockSpec((tl, 512), row),
                      _resident((2048, D_MODEL)), pl.BlockSpec((tl, D_MODEL), row), _resident((8, D_MODEL)),
                      pl.BlockSpec((tl, D_MODEL), row)],
            out_specs=[pl.BlockSpec((tl, D_MODEL), row), pl.BlockSpec((tl, D_MODEL), row),
                       pl.BlockSpec((8, 128), lambda i: (0, 0))],
            out_shape=[jax.ShapeDtypeStruct((seq, D_MODEL), F32), jax.ShapeDtypeStruct((seq, D_MODEL), F32),
                       jax.ShapeDtypeStruct((8, 128), F32)],
            compiler_params=pltpu.CompilerParams(dimension_semantics=("arbitrary",), vmem_limit_bytes=VMEM_LIMIT),
            name="outproj_loss",
        )(oa, ob, oc, w, x, qn, t)

    return call


def _make_outproj_bwd(seq, tl):
    def body(dxn_ref, out_ref, oa_ref, ob_ref, oc_ref, w_ref, qn_ref, doa_ref, dob_ref, doc_ref, dqn_ref, dw_ref):
        @pl.when(pl.program_id(0) == 0)
        def _():
            dqn_ref[...] = jnp.zeros_like(dqn_ref)
            dw_ref[...] = jnp.zeros_like(dw_ref)

        qn = qn_ref[0:1, :]
        on, r, _ = _rms_fwd(out_ref[...], qn, D_MODEL)
        dout, dqn_rows = _rms_bwd(dxn_ref[...], on, r, qn, D_MODEL)
        dqn_ref[...] += jnp.where(_iota2((8, D_MODEL), 0) == 0, jnp.sum(dqn_rows, axis=0, keepdims=True), 0.0)
        db = dout.astype(BF16)
        nt = (((1,), (1,)), ((), ()))
        tn = (((0,), (0,)), ((), ()))
        doa_ref[...] = lax.dot_general(db, w_ref[0:512, :], nt, preferred_element_type=F32).astype(BF16)
        dob_ref[...] = lax.dot_general(db, w_ref[512:1536, :], nt, preferred_element_type=F32).astype(BF16)
        doc_ref[...] = lax.dot_general(db, w_ref[1536:2048, :], nt, preferred_element_type=F32).astype(BF16)
        dw_ref[0:512, :] += lax.dot_general(oa_ref[...], db, tn, preferred_element_type=F32)
        dw_ref[512:1536, :] += lax.dot_general(ob_ref[...], db, tn, preferred_element_type=F32)
        dw_ref[1536:2048, :] += lax.dot_general(oc_ref[...], db, tn, preferred_element_type=F32)

    def call(dxn, out, oa, ob, oc, w, qn):
        row = lambda i: (i, 0)
        const = lambda i: (0, 0)
        return pl.pallas_call(
            body,
            grid=(seq // tl,),
            in_specs=[pl.BlockSpec((tl, D_MODEL), row), pl.BlockSpec((tl, D_MODEL), row),
                      pl.BlockSpec((tl, 512), row), pl.BlockSpec((tl, 1024), row), pl.BlockSpec((tl, 512), row),
                      _resident((2048, D_MODEL)), _resident((8, D_MODEL))],
            out_specs=[pl.BlockSpec((tl, 512), row), pl.BlockSpec((tl, 1024), row), pl.BlockSpec((tl, 512), row),
                       pl.BlockSpec((8, D_MODEL), const), pl.BlockSpec((2048, D_MODEL), const)],
            out_shape=[jax.ShapeDtypeStruct((seq, 512), BF16), jax.ShapeDtypeStruct((seq, 1024), BF16),
                       jax.ShapeDtypeStruct((seq, 512), BF16), jax.ShapeDtypeStruct((8, D_MODEL), F32),
                       jax.ShapeDtypeStruct((2048, D_MODEL), F32)],
            compiler_params=pltpu.CompilerParams(dimension_semantics=("arbitrary",), vmem_limit_bytes=VMEM_LIMIT),
            name="outproj_bwd",
        )(dxn, out, oa, ob, oc, w, qn)

    return call


def _make_inproj_bwd_dx(seq, tl):
    def body(dg_ref, ds_ref, dr_ref, dgs_ref, dss_ref, w_ref, x_ref, pn_ref, dxn_ref, dx_ref, dpn_ref):
        @pl.when(pl.program_id(0) == 0)
        def _():
            dpn_ref[...] = jnp.zeros_like(dpn_ref)

        nt = (((1,), (1,)), ((), ()))
        dh = jnp.zeros((tl, D_MODEL), F32)
        for (a, b), d_ref in zip(SEGS, (dg_ref, ds_ref, dr_ref, dgs_ref, dss_ref)):
            dh = dh + lax.dot_general(d_ref[...], w_ref[:, a:b], nt, preferred_element_type=F32)
        pn = pn_ref[0:1, :]
        on, r, _ = _rms_fwd(x_ref[...], pn, D_MODEL)
        dx, dpn_rows = _rms_bwd(dh, on, r, pn, D_MODEL)
        dx_ref[...] = dx + dxn_ref[...]
        dpn_ref[...] += jnp.where(_iota2((8, D_MODEL), 0) == 0, jnp.sum(dpn_rows, axis=0, keepdims=True), 0.0)

    def call(dg, ds, dr, dgs, dss, w, x, pn, dxn, comm=None, comm_args=()):
        row = lambda i: (i, 0)
        cx = _exchange_specs(comm)
        return pl.pallas_call(
            _with_exchange(body, comm, 9, 2, seq // tl),
            grid=(seq // tl,),
            in_specs=[pl.BlockSpec((tl, b - a), row) for a, b in SEGS]
            + [_resident((D_MODEL, NP)), pl.BlockSpec((tl, D_MODEL), row), _resident((8, D_MODEL)),
               pl.BlockSpec((tl, D_MODEL), row)] + cx["specs"],
            out_specs=[pl.BlockSpec((tl, D_MODEL), row), pl.BlockSpec((8, D_MODEL), lambda i: (0, 0))] + cx["specs"],
            out_shape=[jax.ShapeDtypeStruct((seq, D_MODEL), F32), jax.ShapeDtypeStruct((8, D_MODEL), F32)]
            + cx["out_shape"],
            scratch_shapes=cx["scratch"],
            compiler_params=pltpu.CompilerParams(dimension_semantics=("arbitrary",), vmem_limit_bytes=VMEM_LIMIT,
                                                 has_side_effects=comm is not None),
            name="inproj_bwd_dx" + cx["tag"],
        )(dg, ds, dr, dgs, dss, w, x, pn, dxn, *comm_args)

    return call


def _make_inproj_bwd_dw(seq, tl, width, tn, name):
    def body(ht_ref, d_ref, dw_ref):
        @pl.when(pl.program_id(1) == 0)
        def _():
            dw_ref[...] = jnp.zeros_like(dw_ref)

        dw_ref[...] += jnp.dot(ht_ref[...], d_ref[...], preferred_element_type=F32)

    def call(ht, d):
        return pl.pallas_call(
            body,
            grid=(width // tn, seq // tl),
            in_specs=[pl.BlockSpec((D_MODEL, tl), lambda j, i: (0, i)), pl.BlockSpec((tl, tn), lambda j, i: (i, j))],
            out_specs=pl.BlockSpec((D_MODEL, tn), lambda j, i: (0, j)),
            out_shape=jax.ShapeDtypeStruct((D_MODEL, width), F32),
            compiler_params=pltpu.CompilerParams(dimension_semantics=("arbitrary", "arbitrary"),
                                                 vmem_limit_bytes=VMEM_LIMIT),
            name=name,
        )(ht, d)

    return call


ADAM_LR, ADAM_B1, ADAM_B2, ADAM_EPS, ADAM_WD, ADAM_STEP = 0.001, 0.9, 0.999, 1e-08, 0.01, 10


def _adam_math(w, g, m, v):
    m = ADAM_B1 * m + (1.0 - ADAM_B1) * g
    v = ADAM_B2 * v + (1.0 - ADAM_B2) * (g * g)
    m_hat = m / (1.0 - ADAM_B1 ** ADAM_STEP)
    v_hat = v / (1.0 - ADAM_B2 ** ADAM_STEP)
    delta = -ADAM_LR * (m_hat / (jnp.sqrt(v_hat) + ADAM_EPS) + ADAM_WD * w)
    return delta, m, v


def _adamw(w, g, m, v, name):
    shape = w.shape
    cols = shape[-1]
    rows = w.size // cols
    tr = rows if rows <= 512 else 256
    assert rows % tr == 0

    def body(w_ref, g_ref, m_ref, v_ref, d_ref, mo_ref, vo_ref):
        d_ref[...], mo_ref[...], vo_ref[...] = _adam_math(w_ref[...], g_ref[...], m_ref[...], v_ref[...])

    spec = pl.BlockSpec((tr, cols), lambda i: (i, 0))
    outs = pl.pallas_call(
        body,
        grid=(rows // tr,),
        in_specs=[spec] * 4,
        out_specs=[spec] * 3,
        out_shape=[jax.ShapeDtypeStruct((rows, cols), F32)] * 3,
        compiler_params=pltpu.CompilerParams(dimension_semantics=("arbitrary",), vmem_limit_bytes=VMEM_LIMIT),
        name=name,
    )(*[a.reshape(rows, cols) for a in (w, g, m, v)])
    return (g,) + tuple(o.reshape(shape) for o in outs)


def _adamw_pairs(w, mine, theirs, m, v, name):
    na, r, cols = w.shape
    assert na == 2
    tr = 256
    assert r % tr == 0

    def body(w_ref, a0_ref, b0_ref, a1_ref, b1_ref, m_ref, v_ref, g_ref, d_ref, mo_ref, vo_ref):
        g = jnp.where(pl.program_id(0) == 0, a0_ref[...] + b0_ref[...], a1_ref[...] + b1_ref[...])
        g_ref[...] = g
        d_ref[...], mo_ref[...], vo_ref[...] = _adam_math(w_ref[...], g, m_ref[...], v_ref[...])

    nblk = r // tr
    full = pl.BlockSpec((None, tr, cols), lambda a, i: (a, i, 0))
    lay0 = pl.BlockSpec((None, tr, cols), lambda a, i: (0, i * (1 - a) + (nblk - 1) * a, 0))
    lay1 = pl.BlockSpec((None, tr, cols), lambda a, i: (0, i * a, 0))
    return pl.pallas_call(
        body,
        grid=(na, nblk),
        in_specs=[full, lay0, lay0, lay1, lay1, full, full],
        out_specs=[full] * 4,
        out_shape=[jax.ShapeDtypeStruct(w.shape, F32)] * 4,
        compiler_params=pltpu.CompilerParams(dimension_semantics=("arbitrary",) * 2, vmem_limit_bytes=VMEM_LIMIT),
        name=name,
    )(w, mine[0], theirs[0], mine[1], theirs[1], m, v)


MESH = pl.DeviceIdType.MESH
ANY = pl.BlockSpec(memory_space=pl.ANY)
CHIP_REL = ((1, 0), (0, 1), (1, 1))


def _flip(v, d):
    return 1 - v if d else v


def _ag_chips(arrs, name):
    n = len(arrs)

    def body(*refs):
        ins, outs = refs[:n], refs[n:2 * n]
        send_sems, recv_sems, loc_sems = refs[2 * n:]
        x, y, c = lax.axis_index("x"), lax.axis_index("y"), lax.axis_index("c")
        me = 2 * x + y

        def remote(a, k, slot):
            dx, dy = CHIP_REL[k]
            return pltpu.make_async_remote_copy(
                src_ref=ins[a], dst_ref=outs[a].at[slot], send_sem=send_sems.at[a * 3 + k],
                recv_sem=recv_sems.at[a * 3 + k], device_id=(_flip(x, dx), _flip(y, dy), c), device_id_type=MESH)

        local = [pltpu.make_async_copy(ins[a], outs[a].at[me], loc_sems.at[a]) for a in range(n)]
        for cp in local:
            cp.start()
        for a in range(n):
            for k in range(3):
                remote(a, k, me).start()
        for a in range(n):
            for k, (dx, dy) in enumerate(CHIP_REL):
                remote(a, k, 2 * _flip(x, dx) + _flip(y, dy)).wait_recv()
        for a in range(n):
            for k in range(3):
                remote(a, k, me).wait_send()
        for cp in local:
            cp.wait()

    return pl.pallas_call(
        body,
        in_specs=[ANY] * n,
        out_specs=[ANY] * n,
        out_shape=[jax.ShapeDtypeStruct((4,) + a.shape, a.dtype) for a in arrs],
        scratch_shapes=[pltpu.SemaphoreType.DMA((3 * n,)), pltpu.SemaphoreType.DMA((3 * n,)),
                        pltpu.SemaphoreType.DMA((n,))],
        compiler_params=pltpu.CompilerParams(has_side_effects=True),
        name=name,
    )(*arrs)


class _ChipExchange:
    def __init__(self, kind, arrs):
        self.kind, self.n = kind, len(arrs)
        if kind == "gather":
            self.out_shape = [jax.ShapeDtypeStruct((4,) + a.shape, a.dtype) for a in arrs]
        else:
            self.out_shape = [jax.ShapeDtypeStruct((3,) + a.shape[1:], a.dtype) for a in arrs]
        self.scratch = [pltpu.SemaphoreType.DMA((4 * self.n,)), pltpu.SemaphoreType.DMA((4 * self.n,))]

    def _copies(self, ins, outs, sems):
        send_sems, recv_sems = sems
        x, y, c = lax.axis_index("x"), lax.axis_index("y"), lax.axis_index("c")
        me = 2 * x + y
        pairs = []
        for a in range(self.n):
            for k, (dx, dy) in enumerate(CHIP_REL):
                px, py = _flip(x, dx), _flip(y, dy)
                sem = dict(send_sem=send_sems.at[4 * a + k], recv_sem=recv_sems.at[4 * a + k],
                           device_id=(px, py, c), device_id_type=MESH)
                if self.kind == "gather":
                    out = pltpu.make_async_remote_copy(src_ref=ins[a], dst_ref=outs[a].at[me], **sem)
                    inc = pltpu.make_async_remote_copy(src_ref=ins[a], dst_ref=outs[a].at[2 * px + py], **sem)
                else:
                    out = pltpu.make_async_remote_copy(src_ref=ins[a].at[2 * px + py], dst_ref=outs[a].at[k], **sem)
                    inc = out
                pairs.append((out, inc))
            if self.kind == "gather":
                own = pltpu.make_async_remote_copy(
                    src_ref=ins[a], dst_ref=outs[a].at[me], send_sem=send_sems.at[4 * a + 3],
                    recv_sem=recv_sems.at[4 * a + 3], device_id=(x, y, 1 - c), device_id_type=MESH)
                pairs.append((own, own))
        return pairs

    def start(self, ins, outs, sems):
        for out, _ in self._copies(ins, outs, sems):
            out.start()

    def finish(self, ins, outs, sems):
        pairs = self._copies(ins, outs, sems)
        for _, inc in pairs:
            inc.wait_recv()
        for out, _ in pairs:
            out.wait_send()


def _with_exchange(body, comm, n_in, n_out, nb):
    if comm is None:
        return body

    def wrapped(*refs):
        ins = refs[:n_in]
        c_in = refs[n_in:n_in + comm.n]
        outs = refs[n_in + comm.n:n_in + comm.n + n_out]
        c_out = refs[n_in + comm.n + n_out:n_in + 2 * comm.n + n_out]
        rest = refs[n_in + 2 * comm.n + n_out:]
        scratch, sems = rest[:len(rest) - 2], rest[len(rest) - 2:]

        @pl.when(pl.program_id(0) == 0)
        def _():
            comm.start(c_in, c_out, sems)

        body(*ins, *outs, *scratch)

        @pl.when(pl.program_id(0) == nb - 1)
        def _():
            comm.finish(c_in, c_out, sems)

    return wrapped


def _exchange_specs(comm):
    if comm is None:
        return dict(specs=[], out_shape=[], scratch=[], tag="")
    return dict(specs=[pl.BlockSpec(memory_space=pl.ANY)] * comm.n, out_shape=list(comm.out_shape),
                scratch=list(comm.scratch), tag="_" + comm.kind)


def _half(ref_or_shape, half):
    r = ref_or_shape[-2] // 2
    return pl.ds(half * r, r)


def _ag_rows(arrs, name):
    n = len(arrs)

    def body(*refs):
        ins, outs = refs[:n], refs[n:2 * n]
        send_sems, recv_sems, fsend_sems, frecv_sems, loc_sems = refs[2 * n:]
        x, y, c = lax.axis_index("x"), lax.axis_index("y"), lax.axis_index("c")
        me = 2 * x + y
        sib = (x, y, 1 - c)

        def chip_of(k):
            dx, dy = CHIP_REL[k]
            return _flip(x, dx), _flip(y, dy)

        def ici(a, k, slot):
            px, py = chip_of(k)
            rows = _half(arrs[a].shape, c)
            return pltpu.make_async_remote_copy(
                src_ref=ins[a].at[:, rows, :], dst_ref=outs[a].at[slot, :, rows, :], send_sem=send_sems.at[a * 3 + k],
                recv_sem=recv_sems.at[a * 3 + k], device_id=(px, py, c), device_id_type=MESH)

        def fwd(a, k, half):
            px, py = chip_of(k)
            blk = outs[a].at[2 * px + py, :, _half(arrs[a].shape, half), :]
            return pltpu.make_async_remote_copy(
                src_ref=blk, dst_ref=blk, send_sem=fsend_sems.at[a * 3 + k], recv_sem=frecv_sems.at[a * 3 + k],
                device_id=sib, device_id_type=MESH)

        own = [pltpu.make_async_remote_copy(src_ref=ins[a], dst_ref=outs[a].at[me], send_sem=loc_sems.at[a],
                                            recv_sem=loc_sems.at[n + a], device_id=sib, device_id_type=MESH)
               for a in range(n)]
        for cp in own:
            cp.start()
        for a in range(n):
            for k in range(3):
                ici(a, k, me).start()
        for a in range(n):
            for k in range(3):
                px, py = chip_of(k)
                ici(a, k, 2 * px + py).wait_recv()
                fwd(a, k, c).start()
        for a in range(n):
            for k in range(3):
                fwd(a, k, 1 - c).wait_recv()
        for a in range(n):
            for k in range(3):
                ici(a, k, me).wait_send()
                fwd(a, k, c).wait_send()
        for cp in own:
            cp.wait()

    return pl.pallas_call(
        body,
        in_specs=[ANY] * n,
        out_specs=[ANY] * n,
        out_shape=[jax.ShapeDtypeStruct((4,) + a.shape, a.dtype) for a in arrs],
        scratch_shapes=[pltpu.SemaphoreType.DMA((3 * n,)) for _ in range(4)] + [pltpu.SemaphoreType.DMA((2 * n,))],
        compiler_params=pltpu.CompilerParams(has_side_effects=True),
        name=name,
    )(*arrs)


def _sum_chips(own, recv, chip, name):
    _, na, r, cols = own.shape
    tr = 256
    assert r % tr == 0

    def body(chip_ref, o_ref, r_ref, s_ref):
        s_ref[...] = ((o_ref[...] + r_ref[0].astype(F32)) + r_ref[1].astype(F32)) + r_ref[2].astype(F32)

    return pl.pallas_call(
        body,
        grid_spec=pltpu.PrefetchScalarGridSpec(
            num_scalar_prefetch=1,
            grid=(na, r // tr),
            in_specs=[pl.BlockSpec((None, None, tr, cols), lambda a, i, ch: (ch[0], a, i, 0)),
                      pl.BlockSpec((3, None, tr, cols), lambda a, i, ch: (0, a, i, 0))],
            out_specs=pl.BlockSpec((None, tr, cols), lambda a, i, ch: (a, i, 0))),
        out_shape=jax.ShapeDtypeStruct((na, r, cols), F32),
        compiler_params=pltpu.CompilerParams(dimension_semantics=("arbitrary",) * 2, vmem_limit_bytes=VMEM_LIMIT),
        name=name,
    )(chip, own, recv)


def _swap_sibling(arrs, name):
    n = len(arrs)

    def body(*refs):
        ins, outs = refs[:n], refs[n:2 * n]
        send_sems, recv_sems = refs[2 * n:]
        x, y, c = lax.axis_index("x"), lax.axis_index("y"), lax.axis_index("c")
        cps = [pltpu.make_async_remote_copy(src_ref=ins[a], dst_ref=outs[a], send_sem=send_sems.at[a],
                                            recv_sem=recv_sems.at[a], device_id=(x, y, 1 - c), device_id_type=MESH)
               for a in range(n)]
        for cp in cps:
            cp.start()
        for cp in cps:
            cp.wait_recv()
        for cp in cps:
            cp.wait_send()

    return pl.pallas_call(
        body,
        in_specs=[ANY] * n,
        out_specs=[ANY] * n,
        out_shape=[jax.ShapeDtypeStruct(a.shape, a.dtype) for a in arrs],
        scratch_shapes=[pltpu.SemaphoreType.DMA((n,)), pltpu.SemaphoreType.DMA((n,))],
        compiler_params=pltpu.CompilerParams(has_side_effects=True),
        name=name,
    )(*arrs)


def _allreduce_small(vec, name):
    rows = vec.shape[0]

    def body(v_ref, out_ref, gat_ref, send_sems, recv_sems):
        x, y, c = lax.axis_index("x"), lax.axis_index("y"), lax.axis_index("c")
        me = 4 * x + 2 * y + c

        def remote(k, slot):
            dx, dy, dc = (k >> 2) & 1, (k >> 1) & 1, k & 1
            return pltpu.make_async_remote_copy(
                src_ref=v_ref, dst_ref=gat_ref.at[slot], send_sem=send_sems.at[k - 1], recv_sem=recv_sems.at[k - 1],
                device_id=(_flip(x, dx), _flip(y, dy), _flip(c, dc)), device_id_type=MESH)

        gat_ref[me] = v_ref[...]
        for k in range(1, 8):
            remote(k, me).start()
        for k in range(1, 8):
            dx, dy, dc = (k >> 2) & 1, (k >> 1) & 1, k & 1
            remote(k, 4 * _flip(x, dx) + 2 * _flip(y, dy) + _flip(c, dc)).wait_recv()
        for k in range(1, 8):
            remote(k, me).wait_send()
        acc = gat_ref[0]
        for j in range(1, 8):
            acc = acc + gat_ref[j]
        out_ref[...] = acc

    vm = pl.BlockSpec(memory_space=pltpu.VMEM)
    return pl.pallas_call(
        body,
        in_specs=[vm],
        out_specs=vm,
        out_shape=jax.ShapeDtypeStruct(vec.shape, F32),
        scratch_shapes=[pltpu.VMEM((8, rows, 128), F32), pltpu.SemaphoreType.DMA((7,)), pltpu.SemaphoreType.DMA((7,))],
        compiler_params=pltpu.CompilerParams(has_side_effects=True),
        name=name,
    )(vec)


def _pad8(v, width, lane0=0):
    v = v.reshape(1, -1) if v.ndim == 1 else v
    return jnp.zeros((8, width), F32).at[:v.shape[0], lane0:lane0 + v.shape[1]].set(v.astype(F32))


def _relayout_w_in(g):
    tr = 128
    q = N_IN // 4

    def body(g_ref, o_ref):
        w = jnp.concatenate([g_ref[j] for j in range(4)], axis=1)
        z = lambda n: jnp.zeros((tr, n), w.dtype)
        o_ref[...] = jnp.concatenate([w[:, 0:2048], w[:, 2056:4616], w[:, 4632:6680],
                                      w[:, 2048:2056], z(120), w[:, 4616:4632], z(112)], axis=1)

    return pl.pallas_call(
        body,
        grid=(D_MODEL // tr,),
        in_specs=[pl.BlockSpec((4, tr, q), lambda i: (0, i, 0))],
        out_specs=pl.BlockSpec((tr, NP), lambda i: (i, 0)),
        out_shape=jax.ShapeDtypeStruct((D_MODEL, NP), g.dtype),
        compiler_params=pltpu.CompilerParams(dimension_semantics=("arbitrary",), vmem_limit_bytes=VMEM_LIMIT),
        name="relayout_w_in",
    )(g)


def _unlayout_dw_in(dg, ds, dr, dsm):
    tr = 128
    q = N_IN // 4

    def body(g_ref, s_ref, r_ref, sm_ref, o_ref, ob_ref):
        w = jnp.concatenate([g_ref[...], sm_ref[:, 0:8], s_ref[...], sm_ref[:, 128:144], r_ref[...]], axis=1)
        for j in range(4):
            blk = w[:, q * j:q * (j + 1)]
            o_ref[j] = blk
            ob_ref[j] = blk.astype(BF16)

    row = lambda i: (i, 0)
    return pl.pallas_call(
        body,
        grid=(D_MODEL // tr,),
        in_specs=[pl.BlockSpec((tr, d.shape[1]), row) for d in (dg, ds, dr, dsm)],
        out_specs=[pl.BlockSpec((4, tr, q), lambda i: (0, i, 0))] * 2,
        out_shape=[jax.ShapeDtypeStruct((4, D_MODEL, q), F32), jax.ShapeDtypeStruct((4, D_MODEL, q), BF16)],
        compiler_params=pltpu.CompilerParams(dimension_semantics=("arbitrary",), vmem_limit_bytes=VMEM_LIMIT),
        name="unlayout_dw_in",
    )(dg, ds, dr, dsm)


TB = 256
TL = 256
TL_IN = 512
TL_OB = 1024
TK = 2048


def kernel(x, pre_norm, post_norm, w_in, gdn_conv, gdn_A_log, gdn_dt_bias, gdn_norm, ssd_conv, ssd_conv_b, ssd_A_log, ssd_dt_bias, ssd_D, ssd_norm, ret_norm, w_out, loss_target, m_pre_norm, m_post_norm, m_w_in, m_gdn_conv, m_gdn_A_log, m_gdn_dt_bias, m_gdn_norm, m_ssd_conv, m_ssd_conv_b, m_ssd_A_log, m_ssd_dt_bias, m_ssd_D, m_ssd_norm, m_ret_norm, m_w_out, v_pre_norm, v_post_norm, v_w_in, v_gdn_conv, v_gdn_A_log, v_gdn_dt_bias, v_gdn_norm, v_ssd_conv, v_ssd_conv_b, v_ssd_A_log, v_ssd_dt_bias, v_ssd_D, v_ssd_norm, v_ret_norm, v_w_out):
    seq = x.shape[1]
    chip = 2 * lax.axis_index("x") + lax.axis_index("y")
    x0 = x[0]

    wi_b, wo_b = w_in.astype(BF16), w_out.astype(BF16)
    (wi0_g,) = _ag_rows([wi_b[0:1]], "ag_weights")
    gcv_g, scv_g = _ag_chips([gdn_conv, ssd_conv], "ag_conv")
    full_w_in = _relayout_w_in
    wp = [full_w_in(wi0_g[:, 0]), None]
    wo = [None, None]
    ag0 = _ChipExchange("gather", [wo_b[0], wo_b[1]])
    ag1 = _ChipExchange("gather", [wi_b[1]])
    gcv = jnp.transpose(gcv_g, (1, 2, 0, 3)).reshape(DEPTH, CONV_W, 1536)
    scv = jnp.transpose(scv_g, (1, 2, 0, 3)).reshape(DEPTH, CONV_W, 1536)
    rope_c, rope_s = _rope_tables(seq)

    saved = []
    xc = x0
    for l in range(DEPTH):
        p = dict(
            pn=_pad8(pre_norm[l], D_MODEL), qn=_pad8(post_norm[l], D_MODEL),
            g_cw=_pad8(gcv[l], 1536), g_prm=_pad8(jnp.stack([gdn_A_log[l], gdn_dt_bias[l]]), 128, 4),
            g_nw=_pad8(gdn_norm[l], 128),
            s_cw=_pad8(scv[l], 1536), s_cb=_pad8(ssd_conv_b[l], 1536),
            s_prm=_pad8(jnp.stack([ssd_A_log[l], ssd_dt_bias[l], ssd_D[l]]), 128), s_nw=_pad8(ssd_norm[l], SSD_W),
            r_nw=_pad8(ret_norm[l], 128))
        if l == 0:
            pg, ps, pr, gs, ss, ht, wo0_g, wo1_g = _make_inproj(seq, TL_IN)(
                xc, p["pn"], wp[l], comm=ag0, comm_args=(wo_b[0], wo_b[1]))
            wo = [wo0_g.reshape(2048, D_MODEL), wo1_g.reshape(2048, D_MODEL)]
        else:
            pg, ps, pr, gs, ss, ht = _make_inproj(seq, TL_IN)(xc, p["pn"], wp[l])
        if l == 0:
            oa, stg, tig, uwg, gpre, wi1_g = _make_gdn_fwd(seq, TB)(
                pg, gs, p["g_cw"], p["g_prm"], p["g_nw"], comm=ag1, comm_args=(wi_b[1],))
            wp[1] = full_w_in(wi1_g)
        else:
            oa, stg, tig, uwg, gpre = _make_gdn_fwd(seq, TB)(pg, gs, p["g_cw"], p["g_prm"], p["g_nw"])
        ob, sts, spre, sy = _make_ssd_fwd(seq, TB)(ps, ss, p["s_cw"], p["s_cb"], p["s_prm"], p["s_nw"])
        oc, str_ = _make_ret_fwd(seq, TB)(pr, rope_c, rope_s, p["r_nw"])
        if l == DEPTH - 1:
            out, dxn, lossp = _make_outproj_loss(seq, TL)(oa, ob, oc, wo[l], xc, p["qn"], loss_target[0])
            xn = None
        else:
            out, xn = _make_outproj(seq, TL)(oa, ob, oc, wo[l], xc, p["qn"])
        saved.append(dict(p=p, x=xc, ht=ht, spre=spre, sy=sy, gpre=gpre, pg=pg, ps=ps, pr=pr, gs=gs, ss=ss, stg=stg, tig=tig, uwg=uwg, sts=sts, str=str_,
                          oa=oa, ob=ob, oc=oc, out=out))
        xc = xn

    small = [None] * DEPTH
    gin, gin_b, gout, q_in, q_out = ([None] * DEPTH for _ in range(5))

    for l in reversed(range(DEPTH)):
        s = saved[l]
        p = s["p"]
        doa, dob, doc, dqn, dwo_l = _make_outproj_bwd(seq, TL_OB)(dxn, s["out"], s["oa"], s["ob"], s["oc"], wo[l], p["qn"])
        gout[l] = dwo_l.reshape(4, 512, D_MODEL)
        gdn_args = (s["pg"], s["gpre"], s["gs"], p["g_cw"], p["g_prm"], p["g_nw"], s["stg"], s["tig"], s["uwg"], doa)
        if l == 0:
            payload = (gout[0].astype(BF16),)
            dpg, dgs, dcw_g, dprm_g, dnw_g, q_out[0] = _make_gdn_bwd(seq, TB)(
                *gdn_args, comm=_ChipExchange("scatter", payload), comm_args=payload)
        else:
            dpg, dgs, dcw_g, dprm_g, dnw_g = _make_gdn_bwd(seq, TB)(*gdn_args)
        ssd_args = (s["ps"], s["spre"], s["sy"], s["ss"], p["s_cw"], p["s_cb"], p["s_prm"], p["s_nw"], s["sts"], dob)
        if l == 0:
            payload = (gin_b[1], gout[1].astype(BF16))
            dps, dss, dcw_s, dcb_s, dprm_s, dnw_s, q_in[1], q_out[1] = _make_ssd_bwd(seq, TB)(
                *ssd_args, comm=_ChipExchange("scatter", payload), comm_args=payload)
        else:
            dps, dss, dcw_s, dcb_s, dprm_s, dnw_s = _make_ssd_bwd(seq, TB)(*ssd_args)
        dpr, dnw_r = _make_ret_bwd(seq, TB)(s["pr"], rope_c, rope_s, p["r_nw"], s["str"], doc)
        dws = [_make_inproj_bwd_dw(seq, TK, d.shape[1], tn, f"inproj_bwd_dw{i}")(s["ht"], d)
               for i, (d, tn) in enumerate(((dpg, 2048), (dps, 1280), (dpr, 2048),
                                            (jnp.concatenate([dgs, dss], axis=1), 256)))]
        gin[l], gin_b[l] = _unlayout_dw_in(*dws)
        dx_args = (dpg, dps, dpr, dgs, dss, wp[l], s["x"], p["pn"], dxn)
        if l == 0:
            payload = (gin_b[0],)
            dx, dpn, q_in[0] = _make_inproj_bwd_dx(seq, TL_IN)(
                *dx_args, comm=_ChipExchange("scatter", payload), comm_args=payload)
        else:
            dx, dpn = _make_inproj_bwd_dx(seq, TL_IN)(*dx_args)
        small[l] = [dpn[0], dqn[0], dcw_g[0:4].reshape(-1), dprm_g[0, 4:8], dprm_g[1, 4:8], dnw_g[0],
                    dcw_s[0:4].reshape(-1), dcb_s[0], dprm_s[0, 0:16], dprm_s[1, 0:16], dprm_s[2, 0:16],
                    dnw_s[0], dnw_r[0]]
        dxn = dx
    grad_x = dxn[None]

    sizes = [a.shape[0] for a in small[0]]
    flat = jnp.concatenate(small[0] + small[1] + [lossp[0, 0:1]])
    n_flat = flat.shape[0]
    rows = -(-n_flat // 1024) * 8
    red = _allreduce_small(jnp.pad(flat, (0, rows * 128 - n_flat)).reshape(rows, 128), "allreduce_small").reshape(-1)
    per = sum(sizes)
    loss = red[2 * per]

    def pick(i):
        off = sum(sizes[:i])
        return jnp.stack([red[l * per + off:l * per + off + sizes[i]] for l in range(DEPTH)])

    g_small = dict(
        pre_norm=pick(0), post_norm=pick(1),
        gdn_conv=lax.dynamic_slice_in_dim(pick(2).reshape(DEPTH, CONV_W, 1536), chip * 384, 384, axis=2),
        gdn_A_log=pick(3), gdn_dt_bias=pick(4), gdn_norm=pick(5),
        ssd_conv=lax.dynamic_slice_in_dim(pick(6).reshape(DEPTH, CONV_W, 1536), chip * 384, 384, axis=2),
        ssd_conv_b=pick(7), ssd_A_log=pick(8), ssd_dt_bias=pick(9), ssd_D=pick(10), ssd_norm=pick(11),
        ret_norm=pick(12))

    chip1 = chip.astype(jnp.int32).reshape(1)
    s_in = [_sum_chips(gin[l][:, None], q_in[l][:, None], chip1, f"sum_chips_w_in{l}") for l in range(DEPTH)]
    s_out = [_sum_chips(gout[l][:, None], q_out[l][:, None], chip1, f"sum_chips_w_out{l}") for l in range(DEPTH)]
    t_all = _swap_sibling(s_in + s_out, "swap_grads")
    t_in, t_out = t_all[:DEPTH], t_all[DEPTH:]

    weights = dict(pre_norm=pre_norm, post_norm=post_norm, w_in=w_in, gdn_conv=gdn_conv, gdn_A_log=gdn_A_log,
                   gdn_dt_bias=gdn_dt_bias, gdn_norm=gdn_norm, ssd_conv=ssd_conv, ssd_conv_b=ssd_conv_b,
                   ssd_A_log=ssd_A_log, ssd_dt_bias=ssd_dt_bias, ssd_D=ssd_D, ssd_norm=ssd_norm, ret_norm=ret_norm,
                   w_out=w_out)
    ms = dict(pre_norm=m_pre_norm, post_norm=m_post_norm, w_in=m_w_in, gdn_conv=m_gdn_conv, gdn_A_log=m_gdn_A_log,
              gdn_dt_bias=m_gdn_dt_bias, gdn_norm=m_gdn_norm, ssd_conv=m_ssd_conv, ssd_conv_b=m_ssd_conv_b,
              ssd_A_log=m_ssd_A_log, ssd_dt_bias=m_ssd_dt_bias, ssd_D=m_ssd_D, ssd_norm=m_ssd_norm,
              ret_norm=m_ret_norm, w_out=m_w_out)
    vs = dict(pre_norm=v_pre_norm, post_norm=v_post_norm, w_in=v_w_in, gdn_conv=v_gdn_conv, gdn_A_log=v_gdn_A_log,
              gdn_dt_bias=v_gdn_dt_bias, gdn_norm=v_gdn_norm, ssd_conv=v_ssd_conv, ssd_conv_b=v_ssd_conv_b,
              ssd_A_log=v_ssd_A_log, ssd_dt_bias=v_ssd_dt_bias, ssd_D=v_ssd_D, ssd_norm=v_ssd_norm,
              ret_norm=v_ret_norm, w_out=v_w_out)
    names = list(weights)
    res = {}
    for nme in names:
        if nme == "w_in":
            res[nme] = _adamw_pairs(w_in, s_in, t_in, m_w_in, v_w_in, "adamw_w_in")
        elif nme == "w_out":
            res[nme] = _adamw_pairs(w_out, s_out, t_out, m_w_out, v_w_out, "adamw_w_out")
        else:
            res[nme] = _adamw(weights[nme], g_small[nme], ms[nme], vs[nme], "adamw_" + nme)
    return (loss, grad_x, *[res[n][0] for n in names], *[res[n][1] for n in names],
            *[res[n][2] for n in names], *[res[n][3] for n in names])
```

```python
import math

import jax
import jax.numpy as jnp
from jax import lax
from jax.experimental import pallas as pl
from jax.experimental.pallas import tpu as pltpu

F32 = jnp.float32
BF16 = jnp.bfloat16

D_MODEL = 1024
DEPTH = 2
CH = 64
CONV_W = 4
EPS = 1e-6
GDN_H, GDN_D = 4, 128
SSD_H, SSD_P, SSD_N, SSD_G = 16, 64, 128, 2
SSD_W = SSD_H * SSD_P
RET_H, RET_D = 4, 128
ROPE_BASE = 10000.0
N_IN = 6680
NEG = -1e30

V7X_VMEM_BYTES = 64 * 1024 * 1024
VMEM_LIMIT = V7X_VMEM_BYTES * 7 // 8


def _dot(a, b):
    return jnp.dot(a.astype(BF16), b.astype(BF16), preferred_element_type=F32)


def _dot_nt(a, b):
    return lax.dot_general(a.astype(BF16), b.astype(BF16), (((1,), (1,)), ((), ())), preferred_element_type=F32)


def _dot_tn(a, b):
    return lax.dot_general(a.astype(BF16), b.astype(BF16), (((0,), (0,)), ((), ())), preferred_element_type=F32)


def _split(a):
    hi = a.astype(BF16)
    return hi, (a - hi.astype(F32)).astype(BF16)


def _dot01l(m, v):
    vh, vl = _split(v)
    mb = m.astype(BF16)
    return jnp.dot(mb, vh, preferred_element_type=F32) + jnp.dot(mb, vl, preferred_element_type=F32)


def _dot01r(v, m):
    vh, vl = _split(v)
    mb = m.astype(BF16)
    return jnp.dot(vh, mb, preferred_element_type=F32) + jnp.dot(vl, mb, preferred_element_type=F32)


def _sigmoid(x):
    return jax.nn.sigmoid(x)


def _silu(x):
    return x * _sigmoid(x)


def _dsilu(x):
    s = _sigmoid(x)
    return s * (1.0 + x * (1.0 - s))


def _softplus(x):
    return jnp.maximum(x, 0.0) + jnp.log1p(jnp.exp(-jnp.abs(x)))


def _iota2(shape, dim):
    return lax.broadcasted_iota(jnp.int32, shape, dim)


def _chunk_tri(tb, upper=False):
    r = _iota2((tb, tb), 0)
    c = _iota2((tb, tb), 1)
    same = jnp.right_shift(r, 6) == jnp.right_shift(c, 6)
    return (same & ((c >= r) if upper else (c <= r))).astype(F32)


def _masks():
    r = _iota2((CH, CH), 0)
    c = _iota2((CH, CH), 1)
    return r >= c, r > c, (r == c).astype(F32)


def _put_lane(col, lane_idx, width=128):
    lane = _iota2((col.shape[0], width), 1)
    return jnp.where(lane == lane_idx, col, 0.0)


def _conv_taps(raw, halo8, tb):
    ext = jnp.concatenate([halo8, raw], axis=0)
    return [raw] + [pltpu.roll(ext, s, axis=0)[8:] for s in (1, 2, 3)]


def _conv_back(dpre, nxt8, tb):
    ext = jnp.concatenate([dpre, nxt8], axis=0)
    return [dpre] + [pltpu.roll(ext, tb + 8 - s, axis=0)[:tb] for s in (1, 2, 3)]


def _rms_fwd(o, w, n):
    r = lax.rsqrt(jnp.sum(o * o, axis=-1, keepdims=True) * (1.0 / n) + EPS)
    on = o * r
    return on, r, on * w


def _rms_bwd(dy, on, r, w, n):
    don = dy * w
    return r * (don - on * (jnp.sum(don * on, axis=-1, keepdims=True) * (1.0 / n))), dy * on


def _put_cols(v, g, gw):
    z = jnp.zeros_like(v)
    return jnp.concatenate([v, z] if g == 0 else [z, v], axis=1)


def _gdn_common(pg_ref, halo8, sm, cw, prm, tb, pre=None):
    raw = pg_ref[:, 0:1536]
    if pre is None:
        taps = _conv_taps(raw, halo8, tb)
        pre = taps[0] * cw[3:4, :] + taps[1] * cw[2:3, :] + taps[2] * cw[1:2, :] + taps[3] * cw[0:1, :]
    act = _silu(pre)
    beta = _sigmoid(sm)
    sp_in = sm + prm[1:2, :]
    g = -jnp.exp(prm[0:1, :]) * _softplus(sp_in)
    gc = _dot01l(_chunk_tri(tb), g)
    return raw, pre, act, beta, sp_in, g, gc


_NN = (((2,), (1,)), ((0,), (0,)))
_NT = (((2,), (2,)), ((0,), (0,)))
_TN = (((1,), (1,)), ((0,), (0,)))


def _bdot(a, b, dn):
    return lax.dot_general(a.astype(BF16), b.astype(BF16), dn, preferred_element_type=F32)


def _binv_unit_lower(a, eye):
    r = _iota2((CH, CH), 0)
    c = _iota2((CH, CH), 1)
    d = eye - jnp.where((jnp.right_shift(r, 1) == jnp.right_shift(c, 1)), a, 0.0)
    ab = a.astype(BF16)
    zero = jnp.zeros((), BF16)
    for lb in range(1, 6):
        same = jnp.right_shift(r, lb + 1) == jnp.right_shift(c, lb + 1)
        low = (jnp.bitwise_and(jnp.right_shift(r, lb), 1) == 1) & (jnp.bitwise_and(jnp.right_shift(c, lb), 1) == 0)
        db = d.astype(BF16)
        t = _bdot(jnp.where(same & low, ab, zero), db, _NN)
        d = d - _bdot(db, t, _NN)
    return d


def _rsum(v):
    return jnp.sum(v, axis=-1, keepdims=True)


def _gdn_batch(act, beta, gc, gct, eg_all, ncb, masks):
    causal, strict, _ = masks

    def st(fn):
        return jnp.stack([fn(c, h, slice(c * CH, (c + 1) * CH)) for c in range(ncb) for h in range(GDN_H)])

    qr = st(lambda c, h, r: act[r, h * 128:(h + 1) * 128])
    kr = st(lambda c, h, r: act[r, 512 + h * 128:512 + (h + 1) * 128])
    vh = st(lambda c, h, r: act[r, 1024 + h * 128:1024 + (h + 1) * 128])
    bh = st(lambda c, h, r: beta[r, h:h + 1])
    gcol = st(lambda c, h, r: gc[r, 4 + h:5 + h])
    grow = st(lambda c, h, r: gct[4 + h:5 + h, r])
    eg = st(lambda c, h, r: eg_all[r, 4 + h:5 + h])
    glast = st(lambda c, h, r: gc[(c + 1) * CH - 1:(c + 1) * CH, 4 + h:5 + h])
    rq = lax.rsqrt(_rsum(qr * qr) + EPS)
    rk = lax.rsqrt(_rsum(kr * kr) + EPS)
    qn = qr * rq
    kh = kr * rk
    qh = qn * (GDN_D ** -0.5)
    decay = jnp.exp(jnp.where(causal, gcol - grow, NEG))
    kb = kh * bh
    kd_scale = jnp.exp(glast - gcol)
    return dict(qn=qn, rq=rq, kh=kh, rk=rk, qh=qh, vh=vh, bh=bh, eg=eg, decay=decay, kb=kb, vb=vh * bh, kg=kb * eg,
                qg=qh * eg, kd_scale=kd_scale, kdec=kh * kd_scale, egl=jnp.exp(glast),
                a=jnp.where(strict, _bdot(kb, kh, _NT) * decay, 0.0), attn=_bdot(qh, kh, _NT) * decay)


def _make_gdn_fwd(seq, tb):
    ncb = tb // CH
    nb = seq // tb
    n = ncb * GDN_H

    def body(pg_ref, sm_ref, cw_ref, prm_ref, nw_ref, oa_ref, st_ref, ti_ref, uw_ref, pre_ref, s_scr, halo_scr):
        @pl.when(pl.program_id(0) == 0)
        def _():
            s_scr[...] = jnp.zeros_like(s_scr)
            halo_scr[...] = jnp.zeros_like(halo_scr)

        masks = _masks()
        sm = sm_ref[...]
        raw, pre, act, beta, _, _, gc = _gdn_common(pg_ref, halo_scr[...], sm, cw_ref[...], prm_ref[...], tb)
        halo_scr[...] = raw[tb - 8:tb, :]
        pre_ref[...] = pre
        d = _gdn_batch(act, beta, gc, gc.T, jnp.exp(gc), ncb, masks)
        t = _binv_unit_lower(d["a"], masks[2])
        sol = _bdot(t, jnp.concatenate([d["vb"], d["kg"]], axis=2), _NN)
        ti_ref[...] = t.reshape(ncb, GDN_H, CH, CH)
        uw_ref[...] = sol.reshape(ncb, GDN_H, CH, 256)
        u, w = sol[:, :, :128], sol[:, :, 128:]
        vns = []
        for c in range(ncb):
            bs = slice(c * GDN_H, (c + 1) * GDN_H)
            s = s_scr[...]
            st_ref[c] = s
            vn = u[bs] - _bdot(w[bs], s, _NN)
            s_scr[...] = s * d["egl"][bs] + _bdot(d["kdec"][bs], vn, _TN)
            vns.append(vn)
        v_new = jnp.concatenate(vns, axis=0)
        s_prev = st_ref[...].reshape(n, 128, 128)
        o = _bdot(d["qg"], s_prev, _NN) + _bdot(d["attn"], v_new, _NN)
        _, _, y = _rms_fwd(o, nw_ref[0:1, :], GDN_D)
        for c in range(ncb):
            rows = slice(c * CH, (c + 1) * CH)
            for h in range(GDN_H):
                z = pg_ref[rows, 1536 + h * 128:1536 + (h + 1) * 128]
                oa_ref[rows, h * 128:(h + 1) * 128] = (y[c * GDN_H + h] * _silu(z)).astype(oa_ref.dtype)

    def call(pg, sm, cw, prm, nw, comm=None, comm_args=()):
        blk4 = lambda i: (i, 0, 0, 0)
        cx = _exchange_specs(comm)
        return pl.pallas_call(
            _with_exchange(body, comm, 5, 5, nb),
            grid=(nb,),
            in_specs=[
                pl.BlockSpec((tb, 2048), lambda i: (i, 0)),
                pl.BlockSpec((tb, 128), lambda i: (i, 0)),
                pl.BlockSpec((8, 1536), lambda i: (0, 0)),
                pl.BlockSpec((8, 128), lambda i: (0, 0)),
                pl.BlockSpec((8, 128), lambda i: (0, 0)),
            ] + cx["specs"],
            out_specs=[
                pl.BlockSpec((tb, 512), lambda i: (i, 0)),
                pl.BlockSpec((ncb, GDN_H, 128, 128), blk4),
                pl.BlockSpec((ncb, GDN_H, CH, CH), blk4),
                pl.BlockSpec((ncb, GDN_H, CH, 256), blk4),
                pl.BlockSpec((tb, 1536), lambda i: (i, 0)),
            ] + cx["specs"],
            out_shape=[
                jax.ShapeDtypeStruct((seq, 512), BF16),
                jax.ShapeDtypeStruct((seq // CH, GDN_H, 128, 128), F32),
                jax.ShapeDtypeStruct((seq // CH, GDN_H, CH, CH), F32),
                jax.ShapeDtypeStruct((seq // CH, GDN_H, CH, 256), F32),
                jax.ShapeDtypeStruct((seq, 1536), F32),
            ] + cx["out_shape"],
            scratch_shapes=[pltpu.VMEM((GDN_H, 128, 128), F32), pltpu.VMEM((8, 1536), F32)] + cx["scratch"],
            compiler_params=pltpu.CompilerParams(dimension_semantics=("arbitrary",), vmem_limit_bytes=VMEM_LIMIT,
                                                 has_side_effects=comm is not None),
            name="gdn_fwd" + cx["tag"],
        )(pg, sm, cw, prm, nw, *comm_args)

    return call


def _make_gdn_bwd(seq, tb):
    ncb = tb // CH
    nb = seq // tb
    hb = tb // 8
    n = ncb * GDN_H

    def body(pg_ref, pre_ref, sm_ref, cw_ref, prm_ref, nw_ref, st_ref, ti_ref, uw_ref, doa_ref,
             dpg_ref, dsm_ref, dcw_ref, dprm_ref, dnw_ref, ds_scr, nxt_scr):
        i = pl.program_id(0)

        @pl.when(i == 0)
        def _():
            ds_scr[...] = jnp.zeros_like(ds_scr)
            nxt_scr[...] = jnp.zeros_like(nxt_scr)
            dcw_ref[...] = jnp.zeros_like(dcw_ref)
            dprm_ref[...] = jnp.zeros_like(dprm_ref)
            dnw_ref[...] = jnp.zeros_like(dnw_ref)

        masks = _masks()
        strict = masks[1]
        sm = sm_ref[...]
        cw = cw_ref[...]
        prm = prm_ref[...]
        raw, pre, act, beta, sp_in, g, gc = _gdn_common(pg_ref, None, sm, cw, prm, tb, pre=pre_ref[...])
        nw = nw_ref[0:1, :]
        row_id = _iota2((CH, 1), 0)
        d = _gdn_batch(act, beta, gc, gc.T, jnp.exp(gc), ncb, masks)
        t = ti_ref[...].reshape(n, CH, CH)
        sol = uw_ref[...].reshape(n, CH, 256)
        u, w = sol[:, :, :128], sol[:, :, 128:]
        s_prev = st_ref[...].reshape(n, 128, 128)
        v_new = u - _bdot(w, s_prev, _NN)
        o = _bdot(d["qg"], s_prev, _NN) + _bdot(d["attn"], v_new, _NN)

        pairs = [(c, h) for c in range(ncb) for h in range(GDN_H)]
        z = jnp.stack([pg_ref[c * CH:(c + 1) * CH, 1536 + h * 128:1536 + (h + 1) * 128] for c, h in pairs])
        doa = jnp.stack([doa_ref[c * CH:(c + 1) * CH, h * 128:(h + 1) * 128] for c, h in pairs])
        on, r, y = _rms_fwd(o, nw, GDN_D)
        dz = doa * y * _dsilu(z)
        do, dnw_rows = _rms_bwd(doa * _silu(z), on, r, nw, GDN_D)
        dnw_acc = jnp.sum(jnp.sum(dnw_rows, axis=0), axis=0, keepdims=True)

        dvn_in = _bdot(d["attn"], do, _TN)
        qgtdo = _bdot(d["qg"], do, _TN)
        dvn_l, dkdec_l, dgl_l = [None] * ncb, [None] * ncb, [None] * ncb
        for c in reversed(range(ncb)):
            bs = slice(c * GDN_H, (c + 1) * GDN_H)
            dsn = ds_scr[...]
            dvn_c = dvn_in[bs] + _bdot(d["kdec"][bs], dsn, _NN)
            ds_scr[...] = d["egl"][bs] * dsn + qgtdo[bs] - _bdot(w[bs], dvn_c, _TN)
            dvn_l[c] = dvn_c
            dkdec_l[c] = _bdot(v_new[bs], dsn, _NT)
            dgl_l[c] = d["egl"][bs] * jnp.sum(_rsum(s_prev[bs] * dsn), axis=1, keepdims=True)
        dvn = jnp.concatenate(dvn_l, axis=0)
        dkdec = jnp.concatenate(dkdec_l, axis=0)
        dglast = jnp.concatenate(dgl_l, axis=0)

        dqg = _bdot(do, s_prev, _NT)
        dattn = _bdot(do, v_new, _NT)
        dw = -_bdot(dvn, s_prev, _NT)
        drhs = _bdot(t, jnp.concatenate([dvn, dw], axis=2), _TN)
        dvb, dkg = drhs[:, :, :128], drhs[:, :, 128:]
        da = jnp.where(strict, -(_bdot(dvb, u, _NT) + _bdot(dkg, w, _NT)), 0.0)
        dp = da * d["decay"]
        dq_m = dattn * d["decay"]
        m = da * d["a"] + dattn * d["attn"]
        upper_tri = jnp.broadcast_to((_iota2((CH, CH), 1) >= _iota2((CH, CH), 0)).astype(BF16), (n, CH, CH))
        dg_in = _rsum(jnp.where(strict, _bdot(upper_tri, m, _NN), 0.0))
        dkb = _bdot(dp, d["kh"], _NN) + dkg * d["eg"]
        kdk_row = _rsum(dkdec * d["kdec"])
        dk = _bdot(dp, d["kb"], _TN) + _bdot(dq_m, d["qh"], _TN) + dkdec * d["kd_scale"] + dkb * d["bh"]
        dq = _bdot(dq_m, d["kh"], _NN) + dqg * d["eg"]
        dglast = dglast + jnp.sum(kdk_row, axis=1, keepdims=True)
        dgcol = (_rsum(dqg * d["qg"]) + _rsum(dkg * d["kg"]) - kdk_row + jnp.where(row_id == CH - 1, dglast, 0.0))
        dbeta = _rsum(dkb * d["kh"]) + _rsum(dvb * d["vh"])
        dn = dq * (GDN_D ** -0.5)
        dact_q = d["rq"] * (dn - d["qn"] * _rsum(dn * d["qn"]))
        dact_k = d["rk"] * (dk - d["kh"] * _rsum(dk * d["kh"]))
        dact_v = dvb * d["bh"]

        def lanes(v, lane0):
            return jnp.concatenate(
                [sum(_put_lane(v[c * GDN_H + h], lane0 + h) for h in range(GDN_H)) for c in range(ncb)], axis=0)

        def tokens(v):
            return jnp.concatenate(
                [jnp.concatenate([v[c * GDN_H + h] for h in range(GDN_H)], axis=1) for c in range(ncb)], axis=0)

        dbeta_all = lanes(dbeta, 0)
        dg = _dot01l(_chunk_tri(tb, upper=True), lanes(dgcol, 4)) + lanes(dg_in, 4)
        neg_ea = -jnp.exp(prm[0:1, :])
        da_raw = dg * neg_ea * _sigmoid(sp_in)
        db_raw = dbeta_all * beta * (1.0 - beta)
        dsm_ref[...] = (da_raw + db_raw).astype(dsm_ref.dtype)
        lane8 = _iota2((8, 128), 1)
        sub8 = _iota2((8, 128), 0)
        dalog = jnp.sum(dg * g, axis=0, keepdims=True)
        ddtb = jnp.sum(da_raw, axis=0, keepdims=True)
        dprm_ref[...] += jnp.where(sub8 == 0, dalog, 0.0) + jnp.where(sub8 == 1, ddtb, 0.0)
        dnw_ref[...] += jnp.where(sub8 == 0, dnw_acc, 0.0)

        dact = jnp.concatenate([tokens(dact_q), tokens(dact_k), tokens(dact_v)], axis=1)
        dpre = dact * _dsilu(pre)
        back = _conv_back(dpre, nxt_scr[...], tb)
        nxt_scr[...] = dpre[0:8, :]
        draw = back[0] * cw[3:4, :] + back[1] * cw[2:3, :] + back[2] * cw[1:2, :] + back[3] * cw[0:1, :]
        dpg_ref[:, 0:1536] = draw.astype(dpg_ref.dtype)
        dpg_ref[:, 1536:2048] = tokens(dz).astype(dpg_ref.dtype)
        sub_c = _iota2((8, 1536), 0)
        dcw_new = jnp.zeros((8, 1536), F32)
        for s_ in range(CONV_W):
            dcw_new = dcw_new + jnp.where(sub_c == 3 - s_, jnp.sum(back[s_] * raw, axis=0, keepdims=True), 0.0)
        dcw_ref[...] += dcw_new

    def call(pg, pre, sm, cw, prm, nw, st, ti, uw, doa, comm=None, comm_args=()):
        rev = lambda i: (nb - 1 - i, 0)
        const = lambda i: (0, 0)
        cx = _exchange_specs(comm)
        return pl.pallas_call(
            _with_exchange(body, comm, 10, 5, nb),
            grid=(nb,),
            in_specs=[
                pl.BlockSpec((tb, 2048), rev),
                pl.BlockSpec((tb, 1536), rev),
                pl.BlockSpec((tb, 128), rev),
                pl.BlockSpec((8, 1536), const),
                pl.BlockSpec((8, 128), const),
                pl.BlockSpec((8, 128), const),
                pl.BlockSpec((ncb, GDN_H, 128, 128), lambda i: (nb - 1 - i, 0, 0, 0)),
                pl.BlockSpec((ncb, GDN_H, CH, CH), lambda i: (nb - 1 - i, 0, 0, 0)),
                pl.BlockSpec((ncb, GDN_H, CH, 256), lambda i: (nb - 1 - i, 0, 0, 0)),
                pl.BlockSpec((tb, 512), rev),
            ] + cx["specs"],
            out_specs=[
                pl.BlockSpec((tb, 2048), rev),
                pl.BlockSpec((tb, 128), rev),
                pl.BlockSpec((8, 1536), const),
                pl.BlockSpec((8, 128), const),
                pl.BlockSpec((8, 128), const),
            ] + cx["specs"],
            out_shape=[
                jax.ShapeDtypeStruct((seq, 2048), BF16),
                jax.ShapeDtypeStruct((seq, 128), BF16),
                jax.ShapeDtypeStruct((8, 1536), F32),
                jax.ShapeDtypeStruct((8, 128), F32),
                jax.ShapeDtypeStruct((8, 128), F32),
            ] + cx["out_shape"],
            scratch_shapes=[pltpu.VMEM((GDN_H, 128, 128), F32), pltpu.VMEM((8, 1536), F32)] + cx["scratch"],
            compiler_params=pltpu.CompilerParams(dimension_semantics=("arbitrary",), vmem_limit_bytes=VMEM_LIMIT,
                                                 has_side_effects=comm is not None),
            name="gdn_bwd" + cx["tag"],
        )(pg, pre, sm, cw, prm, nw, st, ti, uw, doa, *comm_args)

    return call


def _expand_mat():
    r = _iota2((128, SSD_W), 0)
    c = _iota2((128, SSD_W), 1)
    return (jnp.right_shift(c, 6) == r).astype(F32)


def _reduce_heads(v, e):
    vh, vl = _split(v)
    eb = e.astype(BF16)
    nt = (((1,), (1,)), ((), ()))
    return (lax.dot_general(vh, eb, nt, preferred_element_type=F32)
            + lax.dot_general(vl, eb, nt, preferred_element_type=F32))


def _reduce_heads1(v, e):
    nt = (((1,), (1,)), ((), ()))
    return lax.dot_general(v.astype(BF16), e.astype(BF16), nt, preferred_element_type=F32)


def _row8(v):
    return jnp.broadcast_to(v, (8, v.shape[1]))


def _ssd_common(ps_ref, halo8, ss, cw, cb, prm, tb, pre=None):
    raw = ps_ref[:, 0:1536]
    taps = None
    if pre is None:
        taps = _conv_taps(raw, halo8, tb)
        pre = taps[0] * cw[3:4, :] + taps[1] * cw[2:3, :] + taps[2] * cw[1:2, :] + taps[3] * cw[0:1, :] + cb[0:1, :]
    act = _silu(pre)
    dt_in = ss + prm[1:2, :]
    dt = _softplus(dt_in)
    a = dt * (-jnp.exp(prm[0:1, :]))
    acum = _dot01l(_chunk_tri(tb), a)
    e = _expand_mat()
    dt_e = _dot01r(dt, e)
    xdt = act[:, 0:SSD_W] * dt_e
    ea_e = _dot01r(jnp.exp(acum), e)
    d_e = _dot01r(_row8(prm[2:3, :]), e)[0:1, :]
    return raw, taps, pre, act, dt_in, dt, a, acum, e, dt_e, xdt, ea_e, d_e


def _ssd_chunk(act, acum, act_t, e, c):
    r0 = c * CH
    rows = slice(r0, r0 + CH)
    alast = acum[r0 + CH - 1:r0 + CH, :]
    wdec = jnp.exp(alast - acum[rows, :])
    wd_e = _dot01r(wdec, e)
    eal_e = _dot01r(_row8(jnp.exp(alast)), e)[0:1, :]
    return rows, wd_e, eal_e


def _ssd_lmat(acum, act_t, c, h, causal):
    r0 = c * CH
    acol = acum[r0:r0 + CH, h:h + 1]
    arow = act_t[h:h + 1, r0:r0 + CH]
    return jnp.exp(jnp.where(causal, acol - arow, NEG))


def _make_ssd_fwd(seq, tb):
    ncb = tb // CH
    nb = seq // tb
    hg = SSD_H // SSD_G
    gw = SSD_W // SSD_G

    def body(ps_ref, ss_ref, cw_ref, cb_ref, prm_ref, nw_ref, ob_ref, st_ref, pre_ref, y_ref, hs_scr, halo_scr):
        @pl.when(pl.program_id(0) == 0)
        def _():
            hs_scr[...] = jnp.zeros_like(hs_scr)
            halo_scr[...] = jnp.zeros_like(halo_scr)

        causal, _, _ = _masks()
        (raw, _, pre, act, _, _, _, acum, e, _, xdt, ea_e, d_e) = _ssd_common(
            ps_ref, halo_scr[...], ss_ref[...], cw_ref[...], cb_ref[...], prm_ref[...], tb)
        halo_scr[...] = raw[tb - 8:tb, :]
        pre_ref[...] = pre
        act_t = acum.T
        nw = nw_ref[0:1, :]
        for c in range(ncb):
            rows, wd_e, eal_e = _ssd_chunk(act, acum, act_t, e, c)
            st_ref[c] = hs_scr[...]
            ys = []
            for g in range(SSD_G):
                gc_ = slice(g * gw, (g + 1) * gw)
                bg = act[rows, SSD_W + g * 128:SSD_W + (g + 1) * 128]
                cg = act[rows, SSD_W + 256 + g * 128:SSD_W + 256 + (g + 1) * 128]
                cbm = _dot_nt(cg, bg)
                hs = hs_scr[:, gc_]
                yin = _dot(cg, hs)
                yh = []
                for hh in range(hg):
                    h = g * hg + hh
                    lm = _ssd_lmat(acum, act_t, c, h, causal)
                    yh.append(_dot(cbm * lm, xdt[rows, h * SSD_P:(h + 1) * SSD_P]))
                ys.append(jnp.concatenate(yh, axis=1) + yin * ea_e[rows, gc_])
                hs_scr[:, gc_] = hs * eal_e[:, gc_] + _dot_tn(bg, xdt[rows, gc_] * wd_e[:, gc_])
            y = jnp.concatenate(ys, axis=1) + act[rows, 0:SSD_W] * d_e
            y_ref[rows, :] = y
            yz = y * _silu(ps_ref[rows, 1536:2560])
            outs = [_rms_fwd(yz[:, g * gw:(g + 1) * gw], nw[:, g * gw:(g + 1) * gw], gw)[2] for g in range(SSD_G)]
            ob_ref[rows, :] = jnp.concatenate(outs, axis=1).astype(ob_ref.dtype)

    def call(ps, ss, cw, cb, prm, nw):
        const = lambda i: (0, 0)
        return pl.pallas_call(
            body,
            grid=(nb,),
            in_specs=[
                pl.BlockSpec((tb, 2560), lambda i: (i, 0)),
                pl.BlockSpec((tb, 128), lambda i: (i, 0)),
                pl.BlockSpec((8, 1536), const),
                pl.BlockSpec((8, 1536), const),
                pl.BlockSpec((8, 128), const),
                pl.BlockSpec((8, SSD_W), const),
            ],
            out_specs=[
                pl.BlockSpec((tb, SSD_W), lambda i: (i, 0)),
                pl.BlockSpec((ncb, SSD_N, SSD_W), lambda i: (i, 0, 0)),
                pl.BlockSpec((tb, 1536), lambda i: (i, 0)),
                pl.BlockSpec((tb, SSD_W), lambda i: (i, 0)),
            ],
            out_shape=[
                jax.ShapeDtypeStruct((seq, SSD_W), BF16),
                jax.ShapeDtypeStruct((seq // CH, SSD_N, SSD_W), F32),
                jax.ShapeDtypeStruct((seq, 1536), F32),
                jax.ShapeDtypeStruct((seq, SSD_W), F32),
            ],
            scratch_shapes=[pltpu.VMEM((SSD_N, SSD_W), F32), pltpu.VMEM((8, 1536), F32)],
            compiler_params=pltpu.CompilerParams(dimension_semantics=("arbitrary",), vmem_limit_bytes=VMEM_LIMIT),
            name="ssd_fwd",
        )(ps, ss, cw, cb, prm, nw)

    return call


def _make_ssd_bwd(seq, tb):
    ncb = tb // CH
    nb = seq // tb
    hb = tb // 8
    hg = SSD_H // SSD_G
    gw = SSD_W // SSD_G

    def body(ps_ref, pre_ref, y_ref, ss_ref, cw_ref, cb_ref, prm_ref, nw_ref, st_ref, dob_ref,
             dps_ref, dss_ref, dcw_ref, dcb_ref, dprm_ref, dnw_ref, dhs_scr, nxt_scr):
        i = pl.program_id(0)

        @pl.when(i == 0)
        def _():
            dhs_scr[...] = jnp.zeros_like(dhs_scr)
            nxt_scr[...] = jnp.zeros_like(nxt_scr)
            dcw_ref[...] = jnp.zeros_like(dcw_ref)
            dcb_ref[...] = jnp.zeros_like(dcb_ref)
            dprm_ref[...] = jnp.zeros_like(dprm_ref)
            dnw_ref[...] = jnp.zeros_like(dnw_ref)

        causal, _, _ = _masks()
        cw = cw_ref[...]
        prm = prm_ref[...]
        (raw, _, pre, act, dt_in, dt, a, acum, e, dt_e, xdt, ea_e, d_e) = _ssd_common(
            ps_ref, None, ss_ref[...], cw, cb_ref[...], prm, tb, pre=pre_ref[...])
        act_t = acum.T
        nw = nw_ref[0:1, :]

        dx_l, db_l, dc_l, dz_l, ddt_l, da_l = ([None] * ncb for _ in range(6))
        upper_tri = (_iota2((CH, CH), 1) >= _iota2((CH, CH), 0)).astype(F32)
        tri_pair = jnp.concatenate([upper_tri, (_iota2((CH, CH), 1) < _iota2((CH, CH), 0)).astype(F32)], axis=1)
        below = jnp.bitwise_and(_iota2((CH, gw), 1), CH - 1) < _iota2((CH, gw), 0)
        dnw_acc = jnp.zeros((1, SSD_W), F32)
        dd_acc = jnp.zeros((1, SSD_W), F32)

        for c in reversed(range(ncb)):
            rows, wd_e, eal_e = _ssd_chunk(act, acum, act_t, e, c)
            xc = act[rows, 0:SSD_W]
            z = ps_ref[rows, 1536:2560]
            dob = dob_ref[rows, :]
            sz = _silu(z)
            dy_g, dz_g, dxdt_g, db_g, dc_g, da_g = [], [], [], [], [], []
            for g in range(SSD_G):
                gc_ = slice(g * gw, (g + 1) * gw)
                bg = act[rows, SSD_W + g * 128:SSD_W + (g + 1) * 128]
                cg = act[rows, SSD_W + 256 + g * 128:SSD_W + 256 + (g + 1) * 128]
                cbm = _dot_nt(cg, bg)
                hs = st_ref[c, :, gc_]
                yin = _dot(cg, hs)
                lmats = [_ssd_lmat(acum, act_t, c, g * hg + hh, causal) for hh in range(hg)]
                ea_g = ea_e[rows, gc_]
                y = y_ref[rows, gc_]
                yz = y * sz[:, gc_]
                on, r, _ = _rms_fwd(yz, nw[:, gc_], gw)
                dyz, dnw_rows = _rms_bwd(dob[:, gc_], on, r, nw[:, gc_], gw)
                dnw_acc = dnw_acc + _put_cols(jnp.sum(dnw_rows, axis=0, keepdims=True), g, gw)
                dy = dyz * sz[:, gc_]
                dz_g.append(dyz * y * _dsilu(z[:, gc_]))
                dd_acc = dd_acc + _put_cols(jnp.sum(dy * xc[:, gc_], axis=0, keepdims=True), g, gw)
                dhs_n = dhs_scr[:, gc_]
                dyin = dy * ea_g
                dcg = _dot_nt(dyin, hs)
                xw = xdt[rows, gc_] * wd_e[:, gc_]
                dbg = _dot_nt(xw, dhs_n)
                dxw = _dot(bg, dhs_n)
                dhs_scr[:, gc_] = dhs_n * eal_e[:, gc_] + _dot_tn(cg, dyin)
                dxi, ms, dcbm = [], [], jnp.zeros((CH, CH), F32)
                for hh in range(hg):
                    h = g * hg + hh
                    hc = slice(hh * SSD_P, (hh + 1) * SSD_P)
                    dyh = dy[:, hc]
                    lm = cbm * lmats[hh]
                    dxi.append(_dot_tn(lm, dyh))
                    dlm = _dot_nt(dyh, xdt[rows, h * SSD_P:(h + 1) * SSD_P])
                    ms.append(dlm * lm)
                    dcbm = dcbm + dlm * lmats[hh]
                dx_intra = jnp.concatenate(dxi, axis=1)
                ncat = _dot(upper_tri, jnp.concatenate(ms, axis=1))
                cum = _dot(tri_pair, jnp.concatenate([dy * yin * ea_g, dxw * xw], axis=0))
                da_g.append(jnp.where(below, ncat, 0.0) + cum
                            + jnp.sum(hs * dhs_n, axis=0, keepdims=True) * eal_e[:, gc_])
                dxdt_g.append(dx_intra + dxw * wd_e[:, gc_])
                dy_g.append(dy)
                db_g.append(dbg + _dot_tn(dcbm, cg))
                dc_g.append(dcg + _dot(dcbm, bg))
            dy = jnp.concatenate(dy_g, axis=1)
            dxdt = jnp.concatenate(dxdt_g, axis=1)
            dx_l[c] = dxdt * dt_e[rows, :] + dy * d_e
            db_l[c] = jnp.concatenate(db_g, axis=1)
            dc_l[c] = jnp.concatenate(dc_g, axis=1)
            dz_l[c] = jnp.concatenate(dz_g, axis=1)
            ddt_l[c] = _reduce_heads1(dxdt * xc, e)
            da_l[c] = _reduce_heads1(jnp.concatenate(da_g, axis=1), e)

        da = jnp.concatenate(da_l, axis=0)
        neg_ea = -jnp.exp(prm[0:1, :])
        ddt = jnp.concatenate(ddt_l, axis=0) + da * neg_ea
        ddt_in = ddt * _sigmoid(dt_in)
        dss_ref[...] = ddt_in.astype(dss_ref.dtype)
        sub8 = _iota2((8, 128), 0)
        dalog = jnp.sum(da * a, axis=0, keepdims=True)
        ddtb = jnp.sum(ddt_in, axis=0, keepdims=True)
        dd = _reduce_heads(_row8(dd_acc), e)[0:1, :]
        dprm_ref[...] += (jnp.where(sub8 == 0, dalog, 0.0) + jnp.where(sub8 == 1, ddtb, 0.0)
                          + jnp.where(sub8 == 2, dd, 0.0))
        dnw_ref[...] += jnp.where(_iota2((8, SSD_W), 0) == 0, dnw_acc, 0.0)

        dact = jnp.concatenate([jnp.concatenate(dx_l, axis=0), jnp.concatenate(db_l, axis=0),
                                jnp.concatenate(dc_l, axis=0)], axis=1)
        dpre = dact * _dsilu(pre)
        back = _conv_back(dpre, nxt_scr[...], tb)
        nxt_scr[...] = dpre[0:8, :]
        draw = back[0] * cw[3:4, :] + back[1] * cw[2:3, :] + back[2] * cw[1:2, :] + back[3] * cw[0:1, :]
        dps_ref[:, 0:1536] = draw.astype(dps_ref.dtype)
        dps_ref[:, 1536:2560] = jnp.concatenate(dz_l, axis=0).astype(dps_ref.dtype)
        sub_c = _iota2((8, 1536), 0)
        dcw_new = jnp.zeros((8, 1536), F32)
        for s_ in range(CONV_W):
            dcw_new = dcw_new + jnp.where(sub_c == 3 - s_, jnp.sum(back[s_] * raw, axis=0, keepdims=True), 0.0)
        dcw_ref[...] += dcw_new
        dcb_ref[...] += jnp.where(sub_c == 0, jnp.sum(dpre, axis=0, keepdims=True), 0.0)

    def call(ps, pre, y, ss, cw, cb, prm, nw, st, dob, comm=None, comm_args=()):
        rev = lambda i: (nb - 1 - i, 0)
        const = lambda i: (0, 0)
        cx = _exchange_specs(comm)
        return pl.pallas_call(
            _with_exchange(body, comm, 10, 6, nb),
            grid=(nb,),
            in_specs=[
                pl.BlockSpec((tb, 2560), rev),
                pl.BlockSpec((tb, 1536), rev),
                pl.BlockSpec((tb, SSD_W), rev),
                pl.BlockSpec((tb, 128), rev),
                pl.BlockSpec((8, 1536), const),
                pl.BlockSpec((8, 1536), const),
                pl.BlockSpec((8, 128), const),
                pl.BlockSpec((8, SSD_W), const),
                pl.BlockSpec((ncb, SSD_N, SSD_W), lambda i: (nb - 1 - i, 0, 0)),
                pl.BlockSpec((tb, SSD_W), rev),
            ] + cx["specs"],
            out_specs=[
                pl.BlockSpec((tb, 2560), rev),
                pl.BlockSpec((tb, 128), rev),
                pl.BlockSpec((8, 1536), const),
                pl.BlockSpec((8, 1536), const),
                pl.BlockSpec((8, 128), const),
                pl.BlockSpec((8, SSD_W), const),
            ] + cx["specs"],
            out_shape=[
                jax.ShapeDtypeStruct((seq, 2560), BF16),
                jax.ShapeDtypeStruct((seq, 128), BF16),
                jax.ShapeDtypeStruct((8, 1536), F32),
                jax.ShapeDtypeStruct((8, 1536), F32),
                jax.ShapeDtypeStruct((8, 128), F32),
                jax.ShapeDtypeStruct((8, SSD_W), F32),
            ] + cx["out_shape"],
            scratch_shapes=[pltpu.VMEM((SSD_N, SSD_W), F32), pltpu.VMEM((8, 1536), F32)] + cx["scratch"],
            compiler_params=pltpu.CompilerParams(dimension_semantics=("arbitrary",), vmem_limit_bytes=VMEM_LIMIT,
                                                 has_side_effects=comm is not None),
            name="ssd_bwd" + cx["tag"],
        )(ps, pre, y, ss, cw, cb, prm, nw, st, dob, *comm_args)

    return call


def _ret_consts(h):
    lg = math.log(1.0 - 2.0 ** (-5.0 - h))
    r = _iota2((CH, CH), 0)
    c = _iota2((CH, CH), 1)
    rel = (r - c).astype(F32)
    dmat = jnp.where(r >= c, jnp.exp(jnp.maximum(rel, 0.0) * lg), 0.0)
    idx = _iota2((CH, 1), 0).astype(F32)
    qdec = jnp.exp((idx + 1.0) * lg)
    kdec = jnp.exp((CH - 1.0 - idx) * lg)
    cdec = math.exp(CH * lg)
    return dmat, qdec, kdec, cdec


def _ret_batch(pr_ref, cc_ref, ss_ref, ncb):
    pairs = [(c, h) for c in range(ncb) for h in range(RET_H)]

    def st(off):
        return jnp.stack([pr_ref[c * CH:(c + 1) * CH, off + h * 128:off + (h + 1) * 128] for c, h in pairs])

    cc = jnp.stack([cc_ref[c * CH:(c + 1) * CH, :] for c, _ in pairs])
    ss = jnp.stack([ss_ref[c * CH:(c + 1) * CH, :] for c, _ in pairs])
    consts = [_ret_consts(h) for h in range(RET_H)]
    dmat = jnp.stack([consts[h][0] for _, h in pairs])
    qdec = jnp.stack([consts[h][1] for _, h in pairs])
    kdec = jnp.stack([consts[h][2] for _, h in pairs])
    cdec = jnp.stack([jnp.full((1, 1), consts[h][3], F32) for h in range(RET_H)])
    q = _rot(st(0), cc, ss)
    k = _rot(st(512), cc, ss) * (RET_D ** -0.5)
    return dict(q=q, k=k, v=st(1024), z=st(1536), cc=cc, ss=ss, dmat=dmat, qdec=qdec, kdec=kdec, cdec=cdec,
                s=_bdot(q, k, _NT) * dmat)


def _rot(t, cc, ss):
    return t * cc + pltpu.roll(t, 64, axis=t.ndim - 1) * ss


def _rot_bwd(d, cc, ss):
    return d * cc + pltpu.roll(d * ss, 64, axis=d.ndim - 1)


def _make_ret_fwd(seq, tb):
    ncb = tb // CH
    nb = seq // tb

    def body(pr_ref, cc_ref, ss_ref, nw_ref, oc_ref, st_ref, r_scr):
        @pl.when(pl.program_id(0) == 0)
        def _():
            r_scr[...] = jnp.zeros_like(r_scr)

        d = _ret_batch(pr_ref, cc_ref, ss_ref, ncb)
        kd = d["k"] * d["kdec"]
        for c in range(ncb):
            bs = slice(c * RET_H, (c + 1) * RET_H)
            rs = r_scr[...]
            st_ref[c] = rs
            r_scr[...] = rs * d["cdec"] + _bdot(kd[bs], d["v"][bs], _TN)
        r_prev = st_ref[...].reshape(ncb * RET_H, 128, 128)
        o = _bdot(d["s"], d["v"], _NN) + _bdot(d["q"], r_prev, _NN) * d["qdec"]
        _, _, y = _rms_fwd(o, nw_ref[0:1, :], RET_D)
        out = y * _silu(d["z"])
        for c in range(ncb):
            for h in range(RET_H):
                oc_ref[c * CH:(c + 1) * CH, h * 128:(h + 1) * 128] = out[c * RET_H + h].astype(oc_ref.dtype)

    def call(pr, cc, ss, nw):
        return pl.pallas_call(
            body,
            grid=(nb,),
            in_specs=[
                pl.BlockSpec((tb, 2048), lambda i: (i, 0)),
                pl.BlockSpec((tb, 128), lambda i: (i, 0)),
                pl.BlockSpec((tb, 128), lambda i: (i, 0)),
                pl.BlockSpec((8, 128), lambda i: (0, 0)),
            ],
            out_specs=[
                pl.BlockSpec((tb, 512), lambda i: (i, 0)),
                pl.BlockSpec((ncb, RET_H, 128, 128), lambda i: (i, 0, 0, 0)),
            ],
            out_shape=[
                jax.ShapeDtypeStruct((seq, 512), BF16),
                jax.ShapeDtypeStruct((seq // CH, RET_H, 128, 128), F32),
            ],
            scratch_shapes=[pltpu.VMEM((RET_H, 128, 128), F32)],
            compiler_params=pltpu.CompilerParams(dimension_semantics=("arbitrary",), vmem_limit_bytes=VMEM_LIMIT),
            name="ret_fwd",
        )(pr, cc, ss, nw)

    return call


def _make_ret_bwd(seq, tb):
    ncb = tb // CH
    nb = seq // tb

    def body(pr_ref, cc_ref, ss_ref, nw_ref, st_ref, doc_ref, dpr_ref, dnw_ref, dr_scr):
        @pl.when(pl.program_id(0) == 0)
        def _():
            dr_scr[...] = jnp.zeros_like(dr_scr)
            dnw_ref[...] = jnp.zeros_like(dnw_ref)

        nw = nw_ref[0:1, :]
        scale = RET_D ** -0.5
        n = ncb * RET_H
        d = _ret_batch(pr_ref, cc_ref, ss_ref, ncb)
        q, k, v, z, s = d["q"], d["k"], d["v"], d["z"], d["s"]
        r_prev = st_ref[...].reshape(n, 128, 128)
        o = _bdot(s, v, _NN) + _bdot(q, r_prev, _NN) * d["qdec"]
        doc = jnp.stack([doc_ref[c * CH:(c + 1) * CH, h * 128:(h + 1) * 128]
                         for c in range(ncb) for h in range(RET_H)])
        on, r, y = _rms_fwd(o, nw, RET_D)
        dz = doc * y * _dsilu(z)
        do, dnw_rows = _rms_bwd(doc * _silu(z), on, r, nw, RET_D)
        dnw_acc = jnp.sum(jnp.sum(dnw_rows, axis=0), axis=0, keepdims=True)
        dqd = do * d["qdec"]
        qtd = _bdot(q, dqd, _TN)
        drn_l = [None] * ncb
        for c in reversed(range(ncb)):
            drn_l[c] = dr_scr[...]
            dr_scr[...] = qtd[c * RET_H:(c + 1) * RET_H] + d["cdec"] * drn_l[c]
        drn = jnp.concatenate(drn_l, axis=0)
        ds = _bdot(do, v, _NT) * d["dmat"]
        dq = _rot_bwd(_bdot(ds, k, _NN) + _bdot(dqd, r_prev, _NT), d["cc"], d["ss"])
        dk = _rot_bwd((_bdot(ds, q, _TN) + _bdot(v, drn, _NT) * d["kdec"]) * scale, d["cc"], d["ss"])
        dv = _bdot(s, do, _TN) + _bdot(k * d["kdec"], drn, _NN)
        for c in range(ncb):
            rows = slice(c * CH, (c + 1) * CH)
            for h in range(RET_H):
                b = c * RET_H + h
                for j, val in enumerate((dq, dk, dv, dz)):
                    dpr_ref[rows, j * 512 + h * 128:j * 512 + (h + 1) * 128] = val[b].astype(dpr_ref.dtype)
        dnw_ref[...] += jnp.where(_iota2((8, 128), 0) == 0, dnw_acc, 0.0)

    def call(pr, cc, ss, nw, st, doc):
        rev = lambda i: (nb - 1 - i, 0)
        return pl.pallas_call(
            body,
            grid=(nb,),
            in_specs=[
                pl.BlockSpec((tb, 2048), rev),
                pl.BlockSpec((tb, 128), rev),
                pl.BlockSpec((tb, 128), rev),
                pl.BlockSpec((8, 128), lambda i: (0, 0)),
                pl.BlockSpec((ncb, RET_H, 128, 128), lambda i: (nb - 1 - i, 0, 0, 0)),
                pl.BlockSpec((tb, 512), rev),
            ],
            out_specs=[
                pl.BlockSpec((tb, 2048), rev),
                pl.BlockSpec((8, 128), lambda i: (0, 0)),
            ],
            out_shape=[
                jax.ShapeDtypeStruct((seq, 2048), BF16),
                jax.ShapeDtypeStruct((8, 128), F32),
            ],
            scratch_shapes=[pltpu.VMEM((RET_H, 128, 128), F32)],
            compiler_params=pltpu.CompilerParams(dimension_semantics=("arbitrary",), vmem_limit_bytes=VMEM_LIMIT),
            name="ret_bwd",
        )(pr, cc, ss, nw, st, doc)

    return call


def _rope_tables(seq):
    half = RET_D // 2
    inv = ROPE_BASE ** (-jnp.arange(half, dtype=F32) / half)
    ang = jnp.arange(seq, dtype=jnp.int32).astype(F32)[:, None] * inv[None, :]
    cos, sin = jnp.cos(ang), jnp.sin(ang)
    return jnp.concatenate([cos, cos], axis=1), jnp.concatenate([-sin, sin], axis=1)


SEG_G, SEG_S, SEG_R, SEG_GS, SEG_SS = (0, 2048), (2048, 4608), (4608, 6656), (6656, 6784), (6784, 6912)
NP = 6912
SEGS = (SEG_G, SEG_S, SEG_R, SEG_GS, SEG_SS)


def _resident(shape):
    return pl.BlockSpec(shape, lambda i: (0,) * len(shape), pipeline_mode=pl.Buffered(1))


def _make_inproj(seq, tl):
    def body(x_ref, pn_ref, w_ref, pg_ref, ps_ref, pr_ref, gs_ref, ss_ref, ht_ref):
        x = x_ref[...]
        _, _, hn = _rms_fwd(x, pn_ref[0:1, :], D_MODEL)
        h = hn.astype(BF16)
        ht_ref[...] = hn.T.astype(BF16)
        for (a, b), o_ref in zip(SEGS, (pg_ref, ps_ref, pr_ref, gs_ref, ss_ref)):
            o_ref[...] = jnp.dot(h, w_ref[:, a:b], preferred_element_type=F32)

    def call(x, pn, w, comm=None, comm_args=()):
        row = lambda i: (i, 0)
        cx = _exchange_specs(comm)
        return pl.pallas_call(
            _with_exchange(body, comm, 3, 6, seq // tl),
            grid=(seq // tl,),
            in_specs=[pl.BlockSpec((tl, D_MODEL), row), _resident((8, D_MODEL)), _resident((D_MODEL, NP))]
            + cx["specs"],
            out_specs=[pl.BlockSpec((tl, b - a), row) for a, b in SEGS]
            + [pl.BlockSpec((D_MODEL, tl), lambda i: (0, i))] + cx["specs"],
            out_shape=[jax.ShapeDtypeStruct((seq, b - a), F32) for a, b in SEGS]
            + [jax.ShapeDtypeStruct((D_MODEL, seq), BF16)] + cx["out_shape"],
            scratch_shapes=cx["scratch"],
            compiler_params=pltpu.CompilerParams(dimension_semantics=("arbitrary",), vmem_limit_bytes=VMEM_LIMIT,
                                                 has_side_effects=comm is not None),
            name="inproj" + cx["tag"],
        )(x, pn, w, *comm_args)

    return call


def _make_outproj(seq, tl):
    def body(oa_ref, ob_ref, oc_ref, w_ref, x_ref, qn_ref, out_ref, xn_ref):
        out = (jnp.dot(oa_ref[...], w_ref[0:512, :], preferred_element_type=F32)
               + jnp.dot(ob_ref[...], w_ref[512:1536, :], preferred_element_type=F32)
               + jnp.dot(oc_ref[...], w_ref[1536:2048, :], preferred_element_type=F32))
        out_ref[...] = out
        _, _, y = _rms_fwd(out, qn_ref[0:1, :], D_MODEL)
        xn_ref[...] = x_ref[...] + y

    def call(oa, ob, oc, w, x, qn):
        row = lambda i: (i, 0)
        return pl.pallas_call(
            body,
            grid=(seq // tl,),
            in_specs=[pl.BlockSpec((tl, 512), row), pl.BlockSpec((tl, 1024), row), pl.BlockSpec((tl, 512), row),
                      _resident((2048, D_MODEL)), pl.BlockSpec((tl, D_MODEL), row), _resident((8, D_MODEL))],
            out_specs=[pl.BlockSpec((tl, D_MODEL), row), pl.BlockSpec((tl, D_MODEL), row)],
            out_shape=[jax.ShapeDtypeStruct((seq, D_MODEL), F32), jax.ShapeDtypeStruct((seq, D_MODEL), F32)],
            compiler_params=pltpu.CompilerParams(dimension_semantics=("arbitrary",), vmem_limit_bytes=VMEM_LIMIT),
            name="outproj",
        )(oa, ob, oc, w, x, qn)

    return call


def _make_outproj_loss(seq, tl):
    def body(oa_ref, ob_ref, oc_ref, w_ref, x_ref, qn_ref, t_ref, out_ref, dy_ref, loss_ref):
        @pl.when(pl.program_id(0) == 0)
        def _():
            loss_ref[...] = jnp.zeros_like(loss_ref)

        out = (jnp.dot(oa_ref[...], w_ref[0:512, :], preferred_element_type=F32)
               + jnp.dot(ob_ref[...], w_ref[512:1536, :], preferred_element_type=F32)
               + jnp.dot(oc_ref[...], w_ref[1536:2048, :], preferred_element_type=F32))
        out_ref[...] = out
        _, _, y = _rms_fwd(out, qn_ref[0:1, :], D_MODEL)
        err = (x_ref[...] + y) - t_ref[...]
        dy_ref[...] = err * (1.0 / D_MODEL)
        part = jnp.sum(jnp.sum(err * err, axis=1, keepdims=True), axis=0, keepdims=True) * (0.5 / D_MODEL)
        loss_ref[...] += jnp.where((_iota2((8, 128), 0) == 0) & (_iota2((8, 128), 1) == 0), part, 0.0)

    def call(oa, ob, oc, w, x, qn, t):
        row = lambda i: (i, 0)
        return pl.pallas_call(
            body,
            grid=(seq // tl,),
            in_specs=[pl.BlockSpec((tl, 512), row), pl.BlockSpec((tl, 1024), row), pl.BlockSpec((tl, 512), row),
                      _resident((2048, D_MODEL)), pl.BlockSpec((tl, D_MODEL), row), _resident((8, D_MODEL)),
                      pl.BlockSpec((tl, D_MODEL), row)],
            out_specs=[pl.BlockSpec((tl, D_MODEL), row), pl.BlockSpec((tl, D_MODEL), row),
                       pl.BlockSpec((8, 128), lambda i: (0, 0))],
            out_shape=[jax.ShapeDtypeStruct((seq, D_MODEL), F32), jax.ShapeDtypeStruct((seq, D_MODEL), F32),
                       jax.ShapeDtypeStruct((8, 128), F32)],
            compiler_params=pltpu.CompilerParams(dimension_semantics=("arbitrary",), vmem_limit_bytes=VMEM_LIMIT),
            name="outproj_loss",
        )(oa, ob, oc, w, x, qn, t)

    return call


def _make_outproj_bwd(seq, tl):
    def body(dxn_ref, out_ref, oa_ref, ob_ref, oc_ref, w_ref, qn_ref, doa_ref, dob_ref, doc_ref, dqn_ref, dw_ref):
        @pl.when(pl.program_id(0) == 0)
        def _():
            dqn_ref[...] = jnp.zeros_like(dqn_ref)
            dw_ref[...] = jnp.zeros_like(dw_ref)

        qn = qn_ref[0:1, :]
        on, r, _ = _rms_fwd(out_ref[...], qn, D_MODEL)
        dout, dqn_rows = _rms_bwd(dxn_ref[...], on, r, qn, D_MODEL)
        dqn_ref[...] += jnp.where(_iota2((8, D_MODEL), 0) == 0, jnp.sum(dqn_rows, axis=0, keepdims=True), 0.0)
        db = dout.astype(BF16)
        nt = (((1,), (1,)), ((), ()))
        tn = (((0,), (0,)), ((), ()))
        doa_ref[...] = lax.dot_general(db, w_ref[0:512, :], nt, preferred_element_type=F32).astype(BF16)
        dob_ref[...] = lax.dot_general(db, w_ref[512:1536, :], nt, preferred_element_type=F32).astype(BF16)
        doc_ref[...] = lax.dot_general(db, w_ref[1536:2048, :], nt, preferred_element_type=F32).astype(BF16)
        dw_ref[0:512, :] += lax.dot_general(oa_ref[...], db, tn, preferred_element_type=F32)
        dw_ref[512:1536, :] += lax.dot_general(ob_ref[...], db, tn, preferred_element_type=F32)
        dw_ref[1536:2048, :] += lax.dot_general(oc_ref[...], db, tn, preferred_element_type=F32)

    def call(dxn, out, oa, ob, oc, w, qn):
        row = lambda i: (i, 0)
        const = lambda i: (0, 0)
        return pl.pallas_call(
            body,
            grid=(seq // tl,),
            in_specs=[pl.BlockSpec((tl, D_MODEL), row), pl.BlockSpec((tl, D_MODEL), row),
                      pl.BlockSpec((tl, 512), row), pl.BlockSpec((tl, 1024), row), pl.BlockSpec((tl, 512), row),
                      _resident((2048, D_MODEL)), _resident((8, D_MODEL))],
            out_specs=[pl.BlockSpec((tl, 512), row), pl.BlockSpec((tl, 1024), row), pl.BlockSpec((tl, 512), row),
                       pl.BlockSpec((8, D_MODEL), const), pl.BlockSpec((2048, D_MODEL), const)],
            out_shape=[jax.ShapeDtypeStruct((seq, 512), BF16), jax.ShapeDtypeStruct((seq, 1024), BF16),
                       jax.ShapeDtypeStruct((seq, 512), BF16), jax.ShapeDtypeStruct((8, D_MODEL), F32),
                       jax.ShapeDtypeStruct((2048, D_MODEL), F32)],
            compiler_params=pltpu.CompilerParams(dimension_semantics=("arbitrary",), vmem_limit_bytes=VMEM_LIMIT),
            name="outproj_bwd",
        )(dxn, out, oa, ob, oc, w, qn)

    return call


def _make_inproj_bwd_dx(seq, tl):
    def body(dg_ref, ds_ref, dr_ref, dgs_ref, dss_ref, w_ref, x_ref, pn_ref, dxn_ref, dx_ref, dpn_ref):
        @pl.when(pl.program_id(0) == 0)
        def _():
            dpn_ref[...] = jnp.zeros_like(dpn_ref)

        nt = (((1,), (1,)), ((), ()))
        dh = jnp.zeros((tl, D_MODEL), F32)
        for (a, b), d_ref in zip(SEGS, (dg_ref, ds_ref, dr_ref, dgs_ref, dss_ref)):
            dh = dh + lax.dot_general(d_ref[...], w_ref[:, a:b], nt, preferred_element_type=F32)
        pn = pn_ref[0:1, :]
        on, r, _ = _rms_fwd(x_ref[...], pn, D_MODEL)
        dx, dpn_rows = _rms_bwd(dh, on, r, pn, D_MODEL)
        dx_ref[...] = dx + dxn_ref[...]
        dpn_ref[...] += jnp.where(_iota2((8, D_MODEL), 0) == 0, jnp.sum(dpn_rows, axis=0, keepdims=True), 0.0)

    def call(dg, ds, dr, dgs, dss, w, x, pn, dxn, comm=None, comm_args=()):
        row = lambda i: (i, 0)
        cx = _exchange_specs(comm)
        return pl.pallas_call(
            _with_exchange(body, comm, 9, 2, seq // tl),
            grid=(seq // tl,),
            in_specs=[pl.BlockSpec((tl, b - a), row) for a, b in SEGS]
            + [_resident((D_MODEL, NP)), pl.BlockSpec((tl, D_MODEL), row), _resident((8, D_MODEL)),
               pl.BlockSpec((tl, D_MODEL), row)] + cx["specs"],
            out_specs=[pl.BlockSpec((tl, D_MODEL), row), pl.BlockSpec((8, D_MODEL), lambda i: (0, 0))] + cx["specs"],
            out_shape=[jax.ShapeDtypeStruct((seq, D_MODEL), F32), jax.ShapeDtypeStruct((8, D_MODEL), F32)]
            + cx["out_shape"],
            scratch_shapes=cx["scratch"],
            compiler_params=pltpu.CompilerParams(dimension_semantics=("arbitrary",), vmem_limit_bytes=VMEM_LIMIT,
                                                 has_side_effects=comm is not None),
            name="inproj_bwd_dx" + cx["tag"],
        )(dg, ds, dr, dgs, dss, w, x, pn, dxn, *comm_args)

    return call


def _make_inproj_bwd_dw(seq, tl, width, tn, name):
    def body(ht_ref, d_ref, dw_ref):
        @pl.when(pl.program_id(1) == 0)
        def _():
            dw_ref[...] = jnp.zeros_like(dw_ref)

        dw_ref[...] += jnp.dot(ht_ref[...], d_ref[...], preferred_element_type=F32)

    def call(ht, d):
        return pl.pallas_call(
            body,
            grid=(width // tn, seq // tl),
            in_specs=[pl.BlockSpec((D_MODEL, tl), lambda j, i: (0, i)), pl.BlockSpec((tl, tn), lambda j, i: (i, j))],
            out_specs=pl.BlockSpec((D_MODEL, tn), lambda j, i: (0, j)),
            out_shape=jax.ShapeDtypeStruct((D_MODEL, width), F32),
            compiler_params=pltpu.CompilerParams(dimension_semantics=("arbitrary", "arbitrary"),
                                                 vmem_limit_bytes=VMEM_LIMIT),
            name=name,
        )(ht, d)

    return call


ADAM_LR, ADAM_B1, ADAM_B2, ADAM_EPS, ADAM_WD, ADAM_STEP = 0.001, 0.9, 0.999, 1e-08, 0.01, 10


def _adam_math(w, g, m, v):
    m = ADAM_B1 * m + (1.0 - ADAM_B1) * g
    v = ADAM_B2 * v + (1.0 - ADAM_B2) * (g * g)
    m_hat = m / (1.0 - ADAM_B1 ** ADAM_STEP)
    v_hat = v / (1.0 - ADAM_B2 ** ADAM_STEP)
    delta = -ADAM_LR * (m_hat / (jnp.sqrt(v_hat) + ADAM_EPS) + ADAM_WD * w)
    return delta, m, v


def _adamw(w, g, m, v, name):
    shape = w.shape
    cols = shape[-1]
    rows = w.size // cols
    tr = rows if rows <= 512 else 256
    assert rows % tr == 0

    def body(w_ref, g_ref, m_ref, v_ref, d_ref, mo_ref, vo_ref):
        d_ref[...], mo_ref[...], vo_ref[...] = _adam_math(w_ref[...], g_ref[...], m_ref[...], v_ref[...])

    spec = pl.BlockSpec((tr, cols), lambda i: (i, 0))
    outs = pl.pallas_call(
        body,
        grid=(rows // tr,),
        in_specs=[spec] * 4,
        out_specs=[spec] * 3,
        out_shape=[jax.ShapeDtypeStruct((rows, cols), F32)] * 3,
        compiler_params=pltpu.CompilerParams(dimension_semantics=("arbitrary",), vmem_limit_bytes=VMEM_LIMIT),
        name=name,
    )(*[a.reshape(rows, cols) for a in (w, g, m, v)])
    return (g,) + tuple(o.reshape(shape) for o in outs)


def _adamw_pairs(w, mine, theirs, m, v, name):
    na, r, cols = w.shape
    assert na == 2
    tr = 256
    assert r % tr == 0

    def body(w_ref, a0_ref, b0_ref, a1_ref, b1_ref, m_ref, v_ref, g_ref, d_ref, mo_ref, vo_ref):
        g = jnp.where(pl.program_id(0) == 0, a0_ref[...] + b0_ref[...], a1_ref[...] + b1_ref[...])
        g_ref[...] = g
        d_ref[...], mo_ref[...], vo_ref[...] = _adam_math(w_ref[...], g, m_ref[...], v_ref[...])

    nblk = r // tr
    full = pl.BlockSpec((None, tr, cols), lambda a, i: (a, i, 0))
    lay0 = pl.BlockSpec((None, tr, cols), lambda a, i: (0, i * (1 - a) + (nblk - 1) * a, 0))
    lay1 = pl.BlockSpec((None, tr, cols), lambda a, i: (0, i * a, 0))
    return pl.pallas_call(
        body,
        grid=(na, nblk),
        in_specs=[full, lay0, lay0, lay1, lay1, full, full],
        out_specs=[full] * 4,
        out_shape=[jax.ShapeDtypeStruct(w.shape, F32)] * 4,
        compiler_params=pltpu.CompilerParams(dimension_semantics=("arbitrary",) * 2, vmem_limit_bytes=VMEM_LIMIT),
        name=name,
    )(w, mine[0], theirs[0], mine[1], theirs[1], m, v)


MESH = pl.DeviceIdType.MESH
ANY = pl.BlockSpec(memory_space=pl.ANY)
CHIP_REL = ((1, 0), (0, 1), (1, 1))


def _flip(v, d):
    return 1 - v if d else v


def _ag_chips(arrs, name):
    n = len(arrs)

    def body(*refs):
        ins, outs = refs[:n], refs[n:2 * n]
        send_sems, recv_sems, loc_sems = refs[2 * n:]
        x, y, c = lax.axis_index("x"), lax.axis_index("y"), lax.axis_index("c")
        me = 2 * x + y

        def remote(a, k, slot):
            dx, dy = CHIP_REL[k]
            return pltpu.make_async_remote_copy(
                src_ref=ins[a], dst_ref=outs[a].at[slot], send_sem=send_sems.at[a * 3 + k],
                recv_sem=recv_sems.at[a * 3 + k], device_id=(_flip(x, dx), _flip(y, dy), c), device_id_type=MESH)

        local = [pltpu.make_async_copy(ins[a], outs[a].at[me], loc_sems.at[a]) for a in range(n)]
        for cp in local:
            cp.start()
        for a in range(n):
            for k in range(3):
                remote(a, k, me).start()
        for a in range(n):
            for k, (dx, dy) in enumerate(CHIP_REL):
                remote(a, k, 2 * _flip(x, dx) + _flip(y, dy)).wait_recv()
        for a in range(n):
            for k in range(3):
                remote(a, k, me).wait_send()
        for cp in local:
            cp.wait()

    return pl.pallas_call(
        body,
        in_specs=[ANY] * n,
        out_specs=[ANY] * n,
        out_shape=[jax.ShapeDtypeStruct((4,) + a.shape, a.dtype) for a in arrs],
        scratch_shapes=[pltpu.SemaphoreType.DMA((3 * n,)), pltpu.SemaphoreType.DMA((3 * n,)),
                        pltpu.SemaphoreType.DMA((n,))],
        compiler_params=pltpu.CompilerParams(has_side_effects=True),
        name=name,
    )(*arrs)


class _ChipExchange:
    def __init__(self, kind, arrs):
        self.kind, self.n = kind, len(arrs)
        if kind == "gather":
            self.out_shape = [jax.ShapeDtypeStruct((4,) + a.shape, a.dtype) for a in arrs]
        else:
            self.out_shape = [jax.ShapeDtypeStruct((3,) + a.shape[1:], a.dtype) for a in arrs]
        self.scratch = [pltpu.SemaphoreType.DMA((4 * self.n,)), pltpu.SemaphoreType.DMA((4 * self.n,))]

    def _copies(self, ins, outs, sems):
        send_sems, recv_sems = sems
        x, y, c = lax.axis_index("x"), lax.axis_index("y"), lax.axis_index("c")
        me = 2 * x + y
        pairs = []
        for a in range(self.n):
            for k, (dx, dy) in enumerate(CHIP_REL):
                px, py = _flip(x, dx), _flip(y, dy)
                sem = dict(send_sem=send_sems.at[4 * a + k], recv_sem=recv_sems.at[4 * a + k],
                           device_id=(px, py, c), device_id_type=MESH)
                if self.kind == "gather":
                    out = pltpu.make_async_remote_copy(src_ref=ins[a], dst_ref=outs[a].at[me], **sem)
                    inc = pltpu.make_async_remote_copy(src_ref=ins[a], dst_ref=outs[a].at[2 * px + py], **sem)
                else:
                    out = pltpu.make_async_remote_copy(src_ref=ins[a].at[2 * px + py], dst_ref=outs[a].at[k], **sem)
                    inc = out
                pairs.append((out, inc))
            if self.kind == "gather":
                own = pltpu.make_async_remote_copy(
                    src_ref=ins[a], dst_ref=outs[a].at[me], send_sem=send_sems.at[4 * a + 3],
                    recv_sem=recv_sems.at[4 * a + 3], device_id=(x, y, 1 - c), device_id_type=MESH)
                pairs.append((own, own))
        return pairs

    def start(self, ins, outs, sems):
        for out, _ in self._copies(ins, outs, sems):
            out.start()

    def finish(self, ins, outs, sems):
        pairs = self._copies(ins, outs, sems)
        for _, inc in pairs:
            inc.wait_recv()
        for out, _ in pairs:
            out.wait_send()


def _with_exchange(body, comm, n_in, n_out, nb):
    if comm is None:
        return body

    def wrapped(*refs):
        ins = refs[:n_in]
        c_in = refs[n_in:n_in + comm.n]
        outs = refs[n_in + comm.n:n_in + comm.n + n_out]
        c_out = refs[n_in + comm.n + n_out:n_in + 2 * comm.n + n_out]
        rest = refs[n_in + 2 * comm.n + n_out:]
        scratch, sems = rest[:len(rest) - 2], rest[len(rest) - 2:]

        @pl.when(pl.program_id(0) == 0)
        def _():
            comm.start(c_in, c_out, sems)

        body(*ins, *outs, *scratch)

        @pl.when(pl.program_id(0) == nb - 1)
        def _():
            comm.finish(c_in, c_out, sems)

    return wrapped


def _exchange_specs(comm):
    if comm is None:
        return dict(specs=[], out_shape=[], scratch=[], tag="")
    return dict(specs=[pl.BlockSpec(memory_space=pl.ANY)] * comm.n, out_shape=list(comm.out_shape),
                scratch=list(comm.scratch), tag="_" + comm.kind)


def _half(ref_or_shape, half):
    r = ref_or_shape[-2] // 2
    return pl.ds(half * r, r)


def _ag_rows(arrs, name):
    n = len(arrs)

    def body(*refs):
        ins, outs = refs[:n], refs[n:2 * n]
        send_sems, recv_sems, fsend_sems, frecv_sems, loc_sems = refs[2 * n:]
        x, y, c = lax.axis_index("x"), lax.axis_index("y"), lax.axis_index("c")
        me = 2 * x + y
        sib = (x, y, 1 - c)

        def chip_of(k):
            dx, dy = CHIP_REL[k]
            return _flip(x, dx), _flip(y, dy)

        def ici(a, k, slot):
            px, py = chip_of(k)
            rows = _half(arrs[a].shape, c)
            return pltpu.make_async_remote_copy(
                src_ref=ins[a].at[:, rows, :], dst_ref=outs[a].at[slot, :, rows, :], send_sem=send_sems.at[a * 3 + k],
                recv_sem=recv_sems.at[a * 3 + k], device_id=(px, py, c), device_id_type=MESH)

        def fwd(a, k, half):
            px, py = chip_of(k)
            blk = outs[a].at[2 * px + py, :, _half(arrs[a].shape, half), :]
            return pltpu.make_async_remote_copy(
                src_ref=blk, dst_ref=blk, send_sem=fsend_sems.at[a * 3 + k], recv_sem=frecv_sems.at[a * 3 + k],
                device_id=sib, device_id_type=MESH)

        own = [pltpu.make_async_remote_copy(src_ref=ins[a], dst_ref=outs[a].at[me], send_sem=loc_sems.at[a],
                                            recv_sem=loc_sems.at[n + a], device_id=sib, device_id_type=MESH)
               for a in range(n)]
        for cp in own:
            cp.start()
        for a in range(n):
            for k in range(3):
                ici(a, k, me).start()
        for a in range(n):
            for k in range(3):
                px, py = chip_of(k)
                ici(a, k, 2 * px + py).wait_recv()
                fwd(a, k, c).start()
        for a in range(n):
            for k in range(3):
                fwd(a, k, 1 - c).wait_recv()
        for a in range(n):
            for k in range(3):
                ici(a, k, me).wait_send()
                fwd(a, k, c).wait_send()
        for cp in own:
            cp.wait()

    return pl.pallas_call(
        body,
        in_specs=[ANY] * n,
        out_specs=[ANY] * n,
        out_shape=[jax.ShapeDtypeStruct((4,) + a.shape, a.dtype) for a in arrs],
        scratch_shapes=[pltpu.SemaphoreType.DMA((3 * n,)) for _ in range(4)] + [pltpu.SemaphoreType.DMA((2 * n,))],
        compiler_params=pltpu.CompilerParams(has_side_effects=True),
        name=name,
    )(*arrs)


def _sum_chips(own, recv, chip, name):
    _, na, r, cols = own.shape
    tr = 256
    assert r % tr == 0

    def body(chip_ref, o_ref, r_ref, s_ref):
        s_ref[...] = ((o_ref[...] + r_ref[0].astype(F32)) + r_ref[1].astype(F32)) + r_ref[2].astype(F32)

    return pl.pallas_call(
        body,
        grid_spec=pltpu.PrefetchScalarGridSpec(
            num_scalar_prefetch=1,
            grid=(na, r // tr),
            in_specs=[pl.BlockSpec((None, None, tr, cols), lambda a, i, ch: (ch[0], a, i, 0)),
                      pl.BlockSpec((3, None, tr, cols), lambda a, i, ch: (0, a, i, 0))],
            out_specs=pl.BlockSpec((None, tr, cols), lambda a, i, ch: (a, i, 0))),
        out_shape=jax.ShapeDtypeStruct((na, r, cols), F32),
        compiler_params=pltpu.CompilerParams(dimension_semantics=("arbitrary",) * 2, vmem_limit_bytes=VMEM_LIMIT),
        name=name,
    )(chip, own, recv)


def _swap_sibling(arrs, name):
    n = len(arrs)

    def body(*refs):
        ins, outs = refs[:n], refs[n:2 * n]
        send_sems, recv_sems = refs[2 * n:]
        x, y, c = lax.axis_index("x"), lax.axis_index("y"), lax.axis_index("c")
        cps = [pltpu.make_async_remote_copy(src_ref=ins[a], dst_ref=outs[a], send_sem=send_sems.at[a],
                                            recv_sem=recv_sems.at[a], device_id=(x, y, 1 - c), device_id_type=MESH)
               for a in range(n)]
        for cp in cps:
            cp.start()
        for cp in cps:
            cp.wait_recv()
        for cp in cps:
            cp.wait_send()

    return pl.pallas_call(
        body,
        in_specs=[ANY] * n,
        out_specs=[ANY] * n,
        out_shape=[jax.ShapeDtypeStruct(a.shape, a.dtype) for a in arrs],
        scratch_shapes=[pltpu.SemaphoreType.DMA((n,)), pltpu.SemaphoreType.DMA((n,))],
        compiler_params=pltpu.CompilerParams(has_side_effects=True),
        name=name,
    )(*arrs)


def _allreduce_small(vec, name):
    rows = vec.shape[0]

    def body(v_ref, out_ref, gat_ref, send_sems, recv_sems):
        x, y, c = lax.axis_index("x"), lax.axis_index("y"), lax.axis_index("c")
        me = 4 * x + 2 * y + c

        def remote(k, slot):
            dx, dy, dc = (k >> 2) & 1, (k >> 1) & 1, k & 1
            return pltpu.make_async_remote_copy(
                src_ref=v_ref, dst_ref=gat_ref.at[slot], send_sem=send_sems.at[k - 1], recv_sem=recv_sems.at[k - 1],
                device_id=(_flip(x, dx), _flip(y, dy), _flip(c, dc)), device_id_type=MESH)

        gat_ref[me] = v_ref[...]
        for k in range(1, 8):
            remote(k, me).start()
        for k in range(1, 8):
            dx, dy, dc = (k >> 2) & 1, (k >> 1) & 1, k & 1
            remote(k, 4 * _flip(x, dx) + 2 * _flip(y, dy) + _flip(c, dc)).wait_recv()
        for k in range(1, 8):
            remote(k, me).wait_send()
        acc = gat_ref[0]
        for j in range(1, 8):
            acc = acc + gat_ref[j]
        out_ref[...] = acc

    vm = pl.BlockSpec(memory_space=pltpu.VMEM)
    return pl.pallas_call(
        body,
        in_specs=[vm],
        out_specs=vm,
        out_shape=jax.ShapeDtypeStruct(vec.shape, F32),
        scratch_shapes=[pltpu.VMEM((8, rows, 128), F32), pltpu.SemaphoreType.DMA((7,)), pltpu.SemaphoreType.DMA((7,))],
        compiler_params=pltpu.CompilerParams(has_side_effects=True),
        name=name,
    )(vec)


def _pad8(v, width, lane0=0):
    v = v.reshape(1, -1) if v.ndim == 1 else v
    return jnp.zeros((8, width), F32).at[:v.shape[0], lane0:lane0 + v.shape[1]].set(v.astype(F32))


def _relayout_w_in(g):
    tr = 128
    q = N_IN // 4

    def body(g_ref, o_ref):
        w = jnp.concatenate([g_ref[j] for j in range(4)], axis=1)
        z = lambda n: jnp.zeros((tr, n), w.dtype)
        o_ref[...] = jnp.concatenate([w[:, 0:2048], w[:, 2056:4616], w[:, 4632:6680],
                                      w[:, 2048:2056], z(120), w[:, 4616:4632], z(112)], axis=1)

    return pl.pallas_call(
        body,
        grid=(D_MODEL // tr,),
        in_specs=[pl.BlockSpec((4, tr, q), lambda i: (0, i, 0))],
        out_specs=pl.BlockSpec((tr, NP), lambda i: (i, 0)),
        out_shape=jax.ShapeDtypeStruct((D_MODEL, NP), g.dtype),
        compiler_params=pltpu.CompilerParams(dimension_semantics=("arbitrary",), vmem_limit_bytes=VMEM_LIMIT),
        name="relayout_w_in",
    )(g)


def _unlayout_dw_in(dg, ds, dr, dsm):
    tr = 128
    q = N_IN // 4

    def body(g_ref, s_ref, r_ref, sm_ref, o_ref, ob_ref):
        w = jnp.concatenate([g_ref[...], sm_ref[:, 0:8], s_ref[...], sm_ref[:, 128:144], r_ref[...]], axis=1)
        for j in range(4):
            blk = w[:, q * j:q * (j + 1)]
            o_ref[j] = blk
            ob_ref[j] = blk.astype(BF16)

    row = lambda i: (i, 0)
    return pl.pallas_call(
        body,
        grid=(D_MODEL // tr,),
        in_specs=[pl.BlockSpec((tr, d.shape[1]), row) for d in (dg, ds, dr, dsm)],
        out_specs=[pl.BlockSpec((4, tr, q), lambda i: (0, i, 0))] * 2,
        out_shape=[jax.ShapeDtypeStruct((4, D_MODEL, q), F32), jax.ShapeDtypeStruct((4, D_MODEL, q), BF16)],
        compiler_params=pltpu.CompilerParams(dimension_semantics=("arbitrary",), vmem_limit_bytes=VMEM_LIMIT),
        name="unlayout_dw_in",
    )(dg, ds, dr, dsm)


TB = 256
TL = 256
TL_IN = 512
TL_OB = 1024
TK = 2048


def kernel(x, pre_norm, post_norm, w_in, gdn_conv, gdn_A_log, gdn_dt_bias, gdn_norm, ssd_conv, ssd_conv_b, ssd_A_log, ssd_dt_bias, ssd_D, ssd_norm, ret_norm, w_out, loss_target, m_pre_norm, m_post_norm, m_w_in, m_gdn_conv, m_gdn_A_log, m_gdn_dt_bias, m_gdn_norm, m_ssd_conv, m_ssd_conv_b, m_ssd_A_log, m_ssd_dt_bias, m_ssd_D, m_ssd_norm, m_ret_norm, m_w_out, v_pre_norm, v_post_norm, v_w_in, v_gdn_conv, v_gdn_A_log, v_gdn_dt_bias, v_gdn_norm, v_ssd_conv, v_ssd_conv_b, v_ssd_A_log, v_ssd_dt_bias, v_ssd_D, v_ssd_norm, v_ret_norm, v_w_out):
    seq = x.shape[1]
    chip = 2 * lax.axis_index("x") + lax.axis_index("y")
    x0 = x[0]

    wi_b, wo_b = w_in.astype(BF16), w_out.astype(BF16)
    (wi0_g,) = _ag_rows([wi_b[0:1]], "ag_weights")
    gcv_g, scv_g = _ag_chips([gdn_conv, ssd_conv], "ag_conv")
    full_w_in = _relayout_w_in
    wp = [full_w_in(wi0_g[:, 0]), None]
    wo = [None, None]
    ag0 = _ChipExchange("gather", [wo_b[0], wo_b[1]])
    ag1 = _ChipExchange("gather", [wi_b[1]])
    gcv = jnp.transpose(gcv_g, (1, 2, 0, 3)).reshape(DEPTH, CONV_W, 1536)
    scv = jnp.transpose(scv_g, (1, 2, 0, 3)).reshape(DEPTH, CONV_W, 1536)
    rope_c, rope_s = _rope_tables(seq)

    saved = []
    xc = x0
    for l in range(DEPTH):
        p = dict(
            pn=_pad8(pre_norm[l], D_MODEL), qn=_pad8(post_norm[l], D_MODEL),
            g_cw=_pad8(gcv[l], 1536), g_prm=_pad8(jnp.stack([gdn_A_log[l], gdn_dt_bias[l]]), 128, 4),
            g_nw=_pad8(gdn_norm[l], 128),
            s_cw=_pad8(scv[l], 1536), s_cb=_pad8(ssd_conv_b[l], 1536),
            s_prm=_pad8(jnp.stack([ssd_A_log[l], ssd_dt_bias[l], ssd_D[l]]), 128), s_nw=_pad8(ssd_norm[l], SSD_W),
            r_nw=_pad8(ret_norm[l], 128))
        if l == 0:
            pg, ps, pr, gs, ss, ht, wo0_g, wo1_g = _make_inproj(seq, TL_IN)(
                xc, p["pn"], wp[l], comm=ag0, comm_args=(wo_b[0], wo_b[1]))
            wo = [wo0_g.reshape(2048, D_MODEL), wo1_g.reshape(2048, D_MODEL)]
        else:
            pg, ps, pr, gs, ss, ht = _make_inproj(seq, TL_IN)(xc, p["pn"], wp[l])
        if l == 0:
            oa, stg, tig, uwg, gpre, wi1_g = _make_gdn_fwd(seq, TB)(
                pg, gs, p["g_cw"], p["g_prm"], p["g_nw"], comm=ag1, comm_args=(wi_b[1],))
            wp[1] = full_w_in(wi1_g)
        else:
            oa, stg, tig, uwg, gpre = _make_gdn_fwd(seq, TB)(pg, gs, p["g_cw"], p["g_prm"], p["g_nw"])
        ob, sts, spre, sy = _make_ssd_fwd(seq, TB)(ps, ss, p["s_cw"], p["s_cb"], p["s_prm"], p["s_nw"])
        oc, str_ = _make_ret_fwd(seq, TB)(pr, rope_c, rope_s, p["r_nw"])
        if l == DEPTH - 1:
            out, dxn, lossp = _make_outproj_loss(seq, TL)(oa, ob, oc, wo[l], xc, p["qn"], loss_target[0])
            xn = None
        else:
            out, xn = _make_outproj(seq, TL)(oa, ob, oc, wo[l], xc, p["qn"])
        saved.append(dict(p=p, x=xc, ht=ht, spre=spre, sy=sy, gpre=gpre, pg=pg, ps=ps, pr=pr, gs=gs, ss=ss, stg=stg, tig=tig, uwg=uwg, sts=sts, str=str_,
                          oa=oa, ob=ob, oc=oc, out=out))
        xc = xn

    small = [None] * DEPTH
    gin, gin_b, gout, q_in, q_out = ([None] * DEPTH for _ in range(5))

    for l in reversed(range(DEPTH)):
        s = saved[l]
        p = s["p"]
        doa, dob, doc, dqn, dwo_l = _make_outproj_bwd(seq, TL_OB)(dxn, s["out"], s["oa"], s["ob"], s["oc"], wo[l], p["qn"])
        gout[l] = dwo_l.reshape(4, 512, D_MODEL)
        gdn_args = (s["pg"], s["gpre"], s["gs"], p["g_cw"], p["g_prm"], p["g_nw"], s["stg"], s["tig"], s["uwg"], doa)
        if l == 0:
            payload = (gout[0].astype(BF16),)
            dpg, dgs, dcw_g, dprm_g, dnw_g, q_out[0] = _make_gdn_bwd(seq, TB)(
                *gdn_args, comm=_ChipExchange("scatter", payload), comm_args=payload)
        else:
            dpg, dgs, dcw_g, dprm_g, dnw_g = _make_gdn_bwd(seq, TB)(*gdn_args)
        ssd_args = (s["ps"], s["spre"], s["sy"], s["ss"], p["s_cw"], p["s_cb"], p["s_prm"], p["s_nw"], s["sts"], dob)
        if l == 0:
            payload = (gin_b[1], gout[1].astype(BF16))
            dps, dss, dcw_s, dcb_s, dprm_s, dnw_s, q_in[1], q_out[1] = _make_ssd_bwd(seq, TB)(
                *ssd_args, comm=_ChipExchange("scatter", payload), comm_args=payload)
        else:
            dps, dss, dcw_s, dcb_s, dprm_s, dnw_s = _make_ssd_bwd(seq, TB)(*ssd_args)
        dpr, dnw_r = _make_ret_bwd(seq, TB)(s["pr"], rope_c, rope_s, p["r_nw"], s["str"], doc)
        dws = [_make_inproj_bwd_dw(seq, TK, d.shape[1], tn, f"inproj_bwd_dw{i}")(s["ht"], d)
               for i, (d, tn) in enumerate(((dpg, 2048), (dps, 1280), (dpr, 2048),
                                            (jnp.concatenate([dgs, dss], axis=1), 256)))]
        gin[l], gin_b[l] = _unlayout_dw_in(*dws)
        dx_args = (dpg, dps, dpr, dgs, dss, wp[l], s["x"], p["pn"], dxn)
        if l == 0:
            payload = (gin_b[0],)
            dx, dpn, q_in[0] = _make_inproj_bwd_dx(seq, TL_IN)(
                *dx_args, comm=_ChipExchange("scatter", payload), comm_args=payload)
        else:
            dx, dpn = _make_inproj_bwd_dx(seq, TL_IN)(*dx_args)
        small[l] = [dpn[0], dqn[0], dcw_g[0:4].reshape(-1), dprm_g[0, 4:8], dprm_g[1, 4:8], dnw_g[0],
                    dcw_s[0:4].reshape(-1), dcb_s[0], dprm_s[0, 0:16], dprm_s[1, 0:16], dprm_s[2, 0:16],
                    dnw_s[0], dnw_r[0]]
        dxn = dx
    grad_x = dxn[None]

    sizes = [a.shape[0] for a in small[0]]
    flat = jnp.concatenate(small[0] + small[1] + [lossp[0, 0:1]])
    n_flat = flat.shape[0]
    rows = -(-n_flat // 1024) * 8
    red = _allreduce_small(jnp.pad(flat, (0, rows * 128 - n_flat)).reshape(rows, 128), "allreduce_small").reshape(-1)
    per = sum(sizes)
    loss = red[2 * per]

    def pick(i):
        off = sum(sizes[:i])
        return jnp.stack([red[l * per + off:l * per + off + sizes[i]] for l in range(DEPTH)])

    g_small = dict(
        pre_norm=pick(0), post_norm=pick(1),
        gdn_conv=lax.dynamic_slice_in_dim(pick(2).reshape(DEPTH, CONV_W, 1536), chip * 384, 384, axis=2),
        gdn_A_log=pick(3), gdn_dt_bias=pick(4), gdn_norm=pick(5),
        ssd_conv=lax.dynamic_slice_in_dim(pick(6).reshape(DEPTH, CONV_W, 1536), chip * 384, 384, axis=2),
        ssd_conv_b=pick(7), ssd_A_log=pick(8), ssd_dt_bias=pick(9), ssd_D=pick(10), ssd_norm=pick(11),
        ret_norm=pick(12))

    chip1 = chip.astype(jnp.int32).reshape(1)
    s_in = [_sum_chips(gin[l][:, None], q_in[l][:, None], chip1, f"sum_chips_w_in{l}") for l in range(DEPTH)]
    s_out = [_sum_chips(gout[l][:, None], q_out[l][:, None], chip1, f"sum_chips_w_out{l}") for l in range(DEPTH)]
    t_all = _swap_sibling(s_in + s_out, "swap_grads")
    t_in, t_out = t_all[:DEPTH], t_all[DEPTH:]

    weights = dict(pre_norm=pre_norm, post_norm=post_norm, w_in=w_in, gdn_conv=gdn_conv, gdn_A_log=gdn_A_log,
                   gdn_dt_bias=gdn_dt_bias, gdn_norm=gdn_norm, ssd_conv=ssd_conv, ssd_conv_b=ssd_conv_b,
                   ssd_A_log=ssd_A_log, ssd_dt_bias=ssd_dt_bias, ssd_D=ssd_D, ssd_norm=ssd_norm, ret_norm=ret_norm,
                   w_out=w_out)
    ms = dict(pre_norm=m_pre_norm, post_norm=m_post_norm, w_in=m_w_in, gdn_conv=m_gdn_conv, gdn_A_log=m_gdn_A_log,
              gdn_dt_bias=m_gdn_dt_bias, gdn_norm=m_gdn_norm, ssd_conv=m_ssd_conv, ssd_conv_b=m_ssd_conv_b,
              ssd_A_log=m_ssd_A_log, ssd_dt_bias=m_ssd_dt_bias, ssd_D=m_ssd_D, ssd_norm=m_ssd_norm,
              ret_norm=m_ret_norm, w_out=m_w_out)
    vs = dict(pre_norm=v_pre_norm, post_norm=v_post_norm, w_in=v_w_in, gdn_conv=v_gdn_conv, gdn_A_log=v_gdn_A_log,
              gdn_dt_bias=v_gdn_dt_bias, gdn_norm=v_gdn_norm, ssd_conv=v_ssd_conv, ssd_conv_b=v_ssd_conv_b,
              ssd_A_log=v_ssd_A_log, ssd_dt_bias=v_ssd_dt_bias, ssd_D=v_ssd_D, ssd_norm=v_ssd_norm,
              ret_norm=v_ret_norm, w_out=v_w_out)
    names = list(weights)
    res = {}
    for nme in names:
        if nme == "w_in":
            res[nme] = _adamw_pairs(w_in, s_in, t_in, m_w_in, v_w_in, "adamw_w_in")
        elif nme == "w_out":
            res[nme] = _adamw_pairs(w_out, s_out, t_out, m_w_out, v_w_out, "adamw_w_out")
        else:
            res[nme] = _adamw(weights[nme], g_small[nme], ms[nme], vs[nme], "adamw_" + nme)
    return (loss, grad_x, *[res[n][0] for n in names], *[res[n][1] for n in names],
            *[res[n][2] for n in names], *[res[n][3] for n in names])
```

```python
import math

import jax
import jax.numpy as jnp
from jax import lax
from jax.experimental import pallas as pl
from jax.experimental.pallas import tpu as pltpu

F32 = jnp.float32
BF16 = jnp.bfloat16

D_MODEL = 1024
DEPTH = 2
CH = 64
CONV_W = 4
EPS = 1e-6
GDN_H, GDN_D = 4, 128
SSD_H, SSD_P, SSD_N, SSD_G = 16, 64, 128, 2
SSD_W = SSD_H * SSD_P
RET_H, RET_D = 4, 128
ROPE_BASE = 10000.0
N_IN = 6680
NEG = -1e30

V7X_VMEM_BYTES = 64 * 1024 * 1024
VMEM_LIMIT = V7X_VMEM_BYTES * 7 // 8


def _dot(a, b):
    return jnp.dot(a.astype(BF16), b.astype(BF16), preferred_element_type=F32)


def _dot_nt(a, b):
    return lax.dot_general(a.astype(BF16), b.astype(BF16), (((1,), (1,)), ((), ())), preferred_element_type=F32)


def _dot_tn(a, b):
    return lax.dot_general(a.astype(BF16), b.astype(BF16), (((0,), (0,)), ((), ())), preferred_element_type=F32)


def _split(a):
    hi = a.astype(BF16)
    return hi, (a - hi.astype(F32)).astype(BF16)


def _dot01l(m, v):
    vh, vl = _split(v)
    mb = m.astype(BF16)
    return jnp.dot(mb, vh, preferred_element_type=F32) + jnp.dot(mb, vl, preferred_element_type=F32)


def _dot01r(v, m):
    vh, vl = _split(v)
    mb = m.astype(BF16)
    return jnp.dot(vh, mb, preferred_element_type=F32) + jnp.dot(vl, mb, preferred_element_type=F32)


def _sigmoid(x):
    return jax.nn.sigmoid(x)


def _silu(x):
    return x * _sigmoid(x)


def _dsilu(x):
    s = _sigmoid(x)
    return s * (1.0 + x * (1.0 - s))


def _softplus(x):
    return jnp.maximum(x, 0.0) + jnp.log1p(jnp.exp(-jnp.abs(x)))


def _iota2(shape, dim):
    return lax.broadcasted_iota(jnp.int32, shape, dim)


def _chunk_tri(tb, upper=False):
    r = _iota2((tb, tb), 0)
    c = _iota2((tb, tb), 1)
    same = jnp.right_shift(r, 6) == jnp.right_shift(c, 6)
    return (same & ((c >= r) if upper else (c <= r))).astype(F32)


def _masks():
    r = _iota2((CH, CH), 0)
    c = _iota2((CH, CH), 1)
    return r >= c, r > c, (r == c).astype(F32)


def _put_lane(col, lane_idx, width=128):
    lane = _iota2((col.shape[0], width), 1)
    return jnp.where(lane == lane_idx, col, 0.0)


def _conv_taps(raw, halo8, tb):
    ext = jnp.concatenate([halo8, raw], axis=0)
    return [raw] + [pltpu.roll(ext, s, axis=0)[8:] for s in (1, 2, 3)]


def _conv_back(dpre, nxt8, tb):
    ext = jnp.concatenate([dpre, nxt8], axis=0)
    return [dpre] + [pltpu.roll(ext, tb + 8 - s, axis=0)[:tb] for s in (1, 2, 3)]


def _rms_fwd(o, w, n):
    r = lax.rsqrt(jnp.sum(o * o, axis=-1, keepdims=True) * (1.0 / n) + EPS)
    on = o * r
    return on, r, on * w


def _rms_bwd(dy, on, r, w, n):
    don = dy * w
    return r * (don - on * (jnp.sum(don * on, axis=-1, keepdims=True) * (1.0 / n))), dy * on


def _put_cols(v, g, gw):
    z = jnp.zeros_like(v)
    return jnp.concatenate([v, z] if g == 0 else [z, v], axis=1)


def _gdn_common(pg_ref, halo8, sm, cw, prm, tb, pre=None):
    raw = pg_ref[:, 0:1536]
    if pre is None:
        taps = _conv_taps(raw, halo8, tb)
        pre = taps[0] * cw[3:4, :] + taps[1] * cw[2:3, :] + taps[2] * cw[1:2, :] + taps[3] * cw[0:1, :]
    act = _silu(pre)
    beta = _sigmoid(sm)
    sp_in = sm + prm[1:2, :]
    g = -jnp.exp(prm[0:1, :]) * _softplus(sp_in)
    gc = _dot01l(_chunk_tri(tb), g)
    return raw, pre, act, beta, sp_in, g, gc


_NN = (((2,), (1,)), ((0,), (0,)))
_NT = (((2,), (2,)), ((0,), (0,)))
_TN = (((1,), (1,)), ((0,), (0,)))


def _bdot(a, b, dn):
    return lax.dot_general(a.astype(BF16), b.astype(BF16), dn, preferred_element_type=F32)


def _binv_unit_lower(a, eye):
    r = _iota2((CH, CH), 0)
    c = _iota2((CH, CH), 1)
    d = eye - jnp.where((jnp.right_shift(r, 1) == jnp.right_shift(c, 1)), a, 0.0)
    ab = a.astype(BF16)
    zero = jnp.zeros((), BF16)
    for lb in range(1, 6):
        same = jnp.right_shift(r, lb + 1) == jnp.right_shift(c, lb + 1)
        low = (jnp.bitwise_and(jnp.right_shift(r, lb), 1) == 1) & (jnp.bitwise_and(jnp.right_shift(c, lb), 1) == 0)
        db = d.astype(BF16)
        t = _bdot(jnp.where(same & low, ab, zero), db, _NN)
        d = d - _bdot(db, t, _NN)
    return d


def _rsum(v):
    return jnp.sum(v, axis=-1, keepdims=True)


def _gdn_batch(act, beta, gc, gct, eg_all, ncb, masks):
    causal, strict, _ = masks

    def st(fn):
        return jnp.stack([fn(c, h, slice(c * CH, (c + 1) * CH)) for c in range(ncb) for h in range(GDN_H)])

    qr = st(lambda c, h, r: act[r, h * 128:(h + 1) * 128])
    kr = st(lambda c, h, r: act[r, 512 + h * 128:512 + (h + 1) * 128])
    vh = st(lambda c, h, r: act[r, 1024 + h * 128:1024 + (h + 1) * 128])
    bh = st(lambda c, h, r: beta[r, h:h + 1])
    gcol = st(lambda c, h, r: gc[r, 4 + h:5 + h])
    grow = st(lambda c, h, r: gct[4 + h:5 + h, r])
    eg = st(lambda c, h, r: eg_all[r, 4 + h:5 + h])
    glast = st(lambda c, h, r: gc[(c + 1) * CH - 1:(c + 1) * CH, 4 + h:5 + h])
    rq = lax.rsqrt(_rsum(qr * qr) + EPS)
    rk = lax.rsqrt(_rsum(kr * kr) + EPS)
    qn = qr * rq
    kh = kr * rk
    qh = qn * (GDN_D ** -0.5)
    decay = jnp.exp(jnp.where(causal, gcol - grow, NEG))
    kb = kh * bh
    kd_scale = jnp.exp(glast - gcol)
    return dict(qn=qn, rq=rq, kh=kh, rk=rk, qh=qh, vh=vh, bh=bh, eg=eg, decay=decay, kb=kb, vb=vh * bh, kg=kb * eg,
                qg=qh * eg, kd_scale=kd_scale, kdec=kh * kd_scale, egl=jnp.exp(glast),
                a=jnp.where(strict, _bdot(kb, kh, _NT) * decay, 0.0), attn=_bdot(qh, kh, _NT) * decay)


def _make_gdn_fwd(seq, tb):
    ncb = tb // CH
    nb = seq // tb
    n = ncb * GDN_H

    def body(pg_ref, sm_ref, cw_ref, prm_ref, nw_ref, oa_ref, st_ref, ti_ref, uw_ref, pre_ref, s_scr, halo_scr):
        @pl.when(pl.program_id(0) == 0)
        def _():
            s_scr[...] = jnp.zeros_like(s_scr)
            halo_scr[...] = jnp.zeros_like(halo_scr)

        masks = _masks()
        sm = sm_ref[...]
        raw, pre, act, beta, _, _, gc = _gdn_common(pg_ref, halo_scr[...], sm, cw_ref[...], prm_ref[...], tb)
        halo_scr[...] = raw[tb - 8:tb, :]
        pre_ref[...] = pre
        d = _gdn_batch(act, beta, gc, gc.T, jnp.exp(gc), ncb, masks)
        t = _binv_unit_lower(d["a"], masks[2])
        sol = _bdot(t, jnp.concatenate([d["vb"], d["kg"]], axis=2), _NN)
        ti_ref[...] = t.reshape(ncb, GDN_H, CH, CH)
        uw_ref[...] = sol.reshape(ncb, GDN_H, CH, 256)
        u, w = sol[:, :, :128], sol[:, :, 128:]
        vns = []
        for c in range(ncb):
            bs = slice(c * GDN_H, (c + 1) * GDN_H)
            s = s_scr[...]
            st_ref[c] = s
            vn = u[bs] - _bdot(w[bs], s, _NN)
            s_scr[...] = s * d["egl"][bs] + _bdot(d["kdec"][bs], vn, _TN)
            vns.append(vn)
        v_new = jnp.concatenate(vns, axis=0)
        s_prev = st_ref[...].reshape(n, 128, 128)
        o = _bdot(d["qg"], s_prev, _NN) + _bdot(d["attn"], v_new, _NN)
        _, _, y = _rms_fwd(o, nw_ref[0:1, :], GDN_D)
        for c in range(ncb):
            rows = slice(c * CH, (c + 1) * CH)
            for h in range(GDN_H):
                z = pg_ref[rows, 1536 + h * 128:1536 + (h + 1) * 128]
                oa_ref[rows, h * 128:(h + 1) * 128] = (y[c * GDN_H + h] * _silu(z)).astype(oa_ref.dtype)

    def call(pg, sm, cw, prm, nw, comm=None, comm_args=()):
        blk4 = lambda i: (i, 0, 0, 0)
        cx = _exchange_specs(comm)
        return pl.pallas_call(
            _with_exchange(body, comm, 5, 5, nb),
            grid=(nb,),
            in_specs=[
                pl.BlockSpec((tb, 2048), lambda i: (i, 0)),
                pl.BlockSpec((tb, 128), lambda i: (i, 0)),
                pl.BlockSpec((8, 1536), lambda i: (0, 0)),
                pl.BlockSpec((8, 128), lambda i: (0, 0)),
                pl.BlockSpec((8, 128), lambda i: (0, 0)),
            ] + cx["specs"],
            out_specs=[
                pl.BlockSpec((tb, 512), lambda i: (i, 0)),
                pl.BlockSpec((ncb, GDN_H, 128, 128), blk4),
                pl.BlockSpec((ncb, GDN_H, CH, CH), blk4),
                pl.BlockSpec((ncb, GDN_H, CH, 256), blk4),
                pl.BlockSpec((tb, 1536), lambda i: (i, 0)),
            ] + cx["specs"],
            out_shape=[
                jax.ShapeDtypeStruct((seq, 512), BF16),
                jax.ShapeDtypeStruct((seq // CH, GDN_H, 128, 128), F32),
                jax.ShapeDtypeStruct((seq // CH, GDN_H, CH, CH), F32),
                jax.ShapeDtypeStruct((seq // CH, GDN_H, CH, 256), F32),
                jax.ShapeDtypeStruct((seq, 1536), F32),
            ] + cx["out_shape"],
            scratch_shapes=[pltpu.VMEM((GDN_H, 128, 128), F32), pltpu.VMEM((8, 1536), F32)] + cx["scratch"],
            compiler_params=pltpu.CompilerParams(dimension_semantics=("arbitrary",), vmem_limit_bytes=VMEM_LIMIT,
                                                 has_side_effects=comm is not None),
            name="gdn_fwd" + cx["tag"],
        )(pg, sm, cw, prm, nw, *comm_args)

    return call


def _make_gdn_bwd(seq, tb):
    ncb = tb // CH
    nb = seq // tb
    hb = tb // 8
    n = ncb * GDN_H

    def body(pg_ref, pre_ref, sm_ref, cw_ref, prm_ref, nw_ref, st_ref, ti_ref, uw_ref, doa_ref,
             dpg_ref, dsm_ref, dcw_ref, dprm_ref, dnw_ref, ds_scr, nxt_scr):
        i = pl.program_id(0)

        @pl.when(i == 0)
        def _():
            ds_scr[...] = jnp.zeros_like(ds_scr)
            nxt_scr[...] = jnp.zeros_like(nxt_scr)
            dcw_ref[...] = jnp.zeros_like(dcw_ref)
            dprm_ref[...] = jnp.zeros_like(dprm_ref)
            dnw_ref[...] = jnp.zeros_like(dnw_ref)

        masks = _masks()
        strict = masks[1]
        sm = sm_ref[...]
        cw = cw_ref[...]
        prm = prm_ref[...]
        raw, pre, act, beta, sp_in, g, gc = _gdn_common(pg_ref, None, sm, cw, prm, tb, pre=pre_ref[...])
        nw = nw_ref[0:1, :]
        row_id = _iota2((CH, 1), 0)
        d = _gdn_batch(act, beta, gc, gc.T, jnp.exp(gc), ncb, masks)
        t = ti_ref[...].reshape(n, CH, CH)
        sol = uw_ref[...].reshape(n, CH, 256)
        u, w = sol[:, :, :128], sol[:, :, 128:]
        s_prev = st_ref[...].reshape(n, 128, 128)
        v_new = u - _bdot(w, s_prev, _NN)
        o = _bdot(d["qg"], s_prev, _NN) + _bdot(d["attn"], v_new, _NN)

        pairs = [(c, h) for c in range(ncb) for h in range(GDN_H)]
        z = jnp.stack([pg_ref[c * CH:(c + 1) * CH, 1536 + h * 128:1536 + (h + 1) * 128] for c, h in pairs])
        doa = jnp.stack([doa_ref[c * CH:(c + 1) * CH, h * 128:(h + 1) * 128] for c, h in pairs])
        on, r, y = _rms_fwd(o, nw, GDN_D)
        dz = doa * y * _dsilu(z)
        do, dnw_rows = _rms_bwd(doa * _silu(z), on, r, nw, GDN_D)
        dnw_acc = jnp.sum(jnp.sum(dnw_rows, axis=0), axis=0, keepdims=True)

        dvn_in = _bdot(d["attn"], do, _TN)
        qgtdo = _bdot(d["qg"], do, _TN)
        dvn_l, dkdec_l, dgl_l = [None] * ncb, [None] * ncb, [None] * ncb
        for c in reversed(range(ncb)):
            bs = slice(c * GDN_H, (c + 1) * GDN_H)
            dsn = ds_scr[...]
            dvn_c = dvn_in[bs] + _bdot(d["kdec"][bs], dsn, _NN)
            ds_scr[...] = d["egl"][bs] * dsn + qgtdo[bs] - _bdot(w[bs], dvn_c, _TN)
            dvn_l[c] = dvn_c
            dkdec_l[c] = _bdot(v_new[bs], dsn, _NT)
            dgl_l[c] = d["egl"][bs] * jnp.sum(_rsum(s_prev[bs] * dsn), axis=1, keepdims=True)
        dvn = jnp.concatenate(dvn_l, axis=0)
        dkdec = jnp.concatenate(dkdec_l, axis=0)
        dglast = jnp.concatenate(dgl_l, axis=0)

        dqg = _bdot(do, s_prev, _NT)
        dattn = _bdot(do, v_new, _NT)
        dw = -_bdot(dvn, s_prev, _NT)
        drhs = _bdot(t, jnp.concatenate([dvn, dw], axis=2), _TN)
        dvb, dkg = drhs[:, :, :128], drhs[:, :, 128:]
        da = jnp.where(strict, -(_bdot(dvb, u, _NT) + _bdot(dkg, w, _NT)), 0.0)
        dp = da * d["decay"]
        dq_m = dattn * d["decay"]
        m = da * d["a"] + dattn * d["attn"]
        upper_tri = jnp.broadcast_to((_iota2((CH, CH), 1) >= _iota2((CH, CH), 0)).astype(BF16), (n, CH, CH))
        dg_in = _rsum(jnp.where(strict, _bdot(upper_tri, m, _NN), 0.0))
        dkb = _bdot(dp, d["kh"], _NN) + dkg * d["eg"]
        kdk_row = _rsum(dkdec * d["kdec"])
        dk = _bdot(dp, d["kb"], _TN) + _bdot(dq_m, d["qh"], _TN) + dkdec * d["kd_scale"] + dkb * d["bh"]
        dq = _bdot(dq_m, d["kh"], _NN) + dqg * d["eg"]
        dglast = dglast + jnp.sum(kdk_row, axis=1, keepdims=True)
        dgcol = (_rsum(dqg * d["qg"]) + _rsum(dkg * d["kg"]) - kdk_row + jnp.where(row_id == CH - 1, dglast, 0.0))
        dbeta = _rsum(dkb * d["kh"]) + _rsum(dvb * d["vh"])
        dn = dq * (GDN_D ** -0.5)
        dact_q = d["rq"] * (dn - d["qn"] * _rsum(dn * d["qn"]))
        dact_k = d["rk"] * (dk - d["kh"] * _rsum(dk * d["kh"]))
        dact_v = dvb * d["bh"]

        def lanes(v, lane0):
            return jnp.concatenate(
                [sum(_put_lane(v[c * GDN_H + h], lane0 + h) for h in range(GDN_H)) for c in range(ncb)], axis=0)

        def tokens(v):
            return jnp.concatenate(
                [jnp.concatenate([v[c * GDN_H + h] for h in range(GDN_H)], axis=1) for c in range(ncb)], axis=0)

        dbeta_all = lanes(dbeta, 0)
        dg = _dot01l(_chunk_tri(tb, upper=True), lanes(dgcol, 4)) + lanes(dg_in, 4)
        neg_ea = -jnp.exp(prm[0:1, :])
        da_raw = dg * neg_ea * _sigmoid(sp_in)
        db_raw = dbeta_all * beta * (1.0 - beta)
        dsm_ref[...] = (da_raw + db_raw).astype(dsm_ref.dtype)
        lane8 = _iota2((8, 128), 1)
        sub8 = _iota2((8, 128), 0)
        dalog = jnp.sum(dg * g, axis=0, keepdims=True)
        ddtb = jnp.sum(da_raw, axis=0, keepdims=True)
        dprm_ref[...] += jnp.where(sub8 == 0, dalog, 0.0) + jnp.where(sub8 == 1, ddtb, 0.0)
        dnw_ref[...] += jnp.where(sub8 == 0, dnw_acc, 0.0)

        dact = jnp.concatenate([tokens(dact_q), tokens(dact_k), tokens(dact_v)], axis=1)
        dpre = dact * _dsilu(pre)
        back = _conv_back(dpre, nxt_scr[...], tb)
        nxt_scr[...] = dpre[0:8, :]
        draw = back[0] * cw[3:4, :] + back[1] * cw[2:3, :] + back[2] * cw[1:2, :] + back[3] * cw[0:1, :]
        dpg_ref[:, 0:1536] = draw.astype(dpg_ref.dtype)
        dpg_ref[:, 1536:2048] = tokens(dz).astype(dpg_ref.dtype)
        sub_c = _iota2((8, 1536), 0)
        dcw_new = jnp.zeros((8, 1536), F32)
        for s_ in range(CONV_W):
            dcw_new = dcw_new + jnp.where(sub_c == 3 - s_, jnp.sum(back[s_] * raw, axis=0, keepdims=True), 0.0)
        dcw_ref[...] += dcw_new

    def call(pg, pre, sm, cw, prm, nw, st, ti, uw, doa, comm=None, comm_args=()):
        rev = lambda i: (nb - 1 - i, 0)
        const = lambda i: (0, 0)
        cx = _exchange_specs(comm)
        return pl.pallas_call(
            _with_exchange(body, comm, 10, 5, nb),
            grid=(nb,),
            in_specs=[
                pl.BlockSpec((tb, 2048), rev),
                pl.BlockSpec((tb, 1536), rev),
                pl.BlockSpec((tb, 128), rev),
                pl.BlockSpec((8, 1536), const),
                pl.BlockSpec((8, 128), const),
                pl.BlockSpec((8, 128), const),
                pl.BlockSpec((ncb, GDN_H, 128, 128), lambda i: (nb - 1 - i, 0, 0, 0)),
                pl.BlockSpec((ncb, GDN_H, CH, CH), lambda i: (nb - 1 - i, 0, 0, 0)),
                pl.BlockSpec((ncb, GDN_H, CH, 256), lambda i: (nb - 1 - i, 0, 0, 0)),
                pl.BlockSpec((tb, 512), rev),
            ] + cx["specs"],
            out_specs=[
                pl.BlockSpec((tb, 2048), rev),
                pl.BlockSpec((tb, 128), rev),
                pl.BlockSpec((8, 1536), const),
                pl.BlockSpec((8, 128), const),
                pl.BlockSpec((8, 128), const),
            ] + cx["specs"],
            out_shape=[
                jax.ShapeDtypeStruct((seq, 2048), BF16),
                jax.ShapeDtypeStruct((seq, 128), BF16),
                jax.ShapeDtypeStruct((8, 1536), F32),
                jax.ShapeDtypeStruct((8, 128), F32),
                jax.ShapeDtypeStruct((8, 128), F32),
            ] + cx["out_shape"],
            scratch_shapes=[pltpu.VMEM((GDN_H, 128, 128), F32), pltpu.VMEM((8, 1536), F32)] + cx["scratch"],
            compiler_params=pltpu.CompilerParams(dimension_semantics=("arbitrary",), vmem_limit_bytes=VMEM_LIMIT,
                                                 has_side_effects=comm is not None),
            name="gdn_bwd" + cx["tag"],
        )(pg, pre, sm, cw, prm, nw, st, ti, uw, doa, *comm_args)

    return call


def _expand_mat():
    r = _iota2((128, SSD_W), 0)
    c = _iota2((128, SSD_W), 1)
    return (jnp.right_shift(c, 6) == r).astype(F32)


def _reduce_heads(v, e):
    vh, vl = _split(v)
    eb = e.astype(BF16)
    nt = (((1,), (1,)), ((), ()))
    return (lax.dot_general(vh, eb, nt, preferred_element_type=F32)
            + lax.dot_general(vl, eb, nt, preferred_element_type=F32))


def _reduce_heads1(v, e):
    nt = (((1,), (1,)), ((), ()))
    return lax.dot_general(v.astype(BF16), e.astype(BF16), nt, preferred_element_type=F32)


def _row8(v):
    return jnp.broadcast_to(v, (8, v.shape[1]))


def _ssd_common(ps_ref, halo8, ss, cw, cb, prm, tb, pre=None):
    raw = ps_ref[:, 0:1536]
    taps = None
    if pre is None:
        taps = _conv_taps(raw, halo8, tb)
        pre = taps[0] * cw[3:4, :] + taps[1] * cw[2:3, :] + taps[2] * cw[1:2, :] + taps[3] * cw[0:1, :] + cb[0:1, :]
    act = _silu(pre)
    dt_in = ss + prm[1:2, :]
    dt = _softplus(dt_in)
    a = dt * (-jnp.exp(prm[0:1, :]))
    acum = _dot01l(_chunk_tri(tb), a)
    e = _expand_mat()
    dt_e = _dot01r(dt, e)
    xdt = act[:, 0:SSD_W] * dt_e
    ea_e = _dot01r(jnp.exp(acum), e)
    d_e = _dot01r(_row8(prm[2:3, :]), e)[0:1, :]
    return raw, taps, pre, act, dt_in, dt, a, acum, e, dt_e, xdt, ea_e, d_e


def _ssd_chunk(act, acum, act_t, e, c):
    r0 = c * CH
    rows = slice(r0, r0 + CH)
    alast = acum[r0 + CH - 1:r0 + CH, :]
    wdec = jnp.exp(alast - acum[rows, :])
    wd_e = _dot01r(wdec, e)
    eal_e = _dot01r(_row8(jnp.exp(alast)), e)[0:1, :]
    return rows, wd_e, eal_e


def _ssd_lmat(acum, act_t, c, h, causal):
    r0 = c * CH
    acol = acum[r0:r0 + CH, h:h + 1]
    arow = act_t[h:h + 1, r0:r0 + CH]
    return jnp.exp(jnp.where(causal, acol - arow, NEG))


def _make_ssd_fwd(seq, tb):
    ncb = tb // CH
    nb = seq // tb
    hg = SSD_H // SSD_G
    gw = SSD_W // SSD_G

    def body(ps_ref, ss_ref, cw_ref, cb_ref, prm_ref, nw_ref, ob_ref, st_ref, pre_ref, y_ref, hs_scr, halo_scr):
        @pl.when(pl.program_id(0) == 0)
        def _():
            hs_scr[...] = jnp.zeros_like(hs_scr)
            halo_scr[...] = jnp.zeros_like(halo_scr)

        causal, _, _ = _masks()
        (raw, _, pre, act, _, _, _, acum, e, _, xdt, ea_e, d_e) = _ssd_common(
            ps_ref, halo_scr[...], ss_ref[...], cw_ref[...], cb_ref[...], prm_ref[...], tb)
        halo_scr[...] = raw[tb - 8:tb, :]
        pre_ref[...] = pre
        act_t = acum.T
        nw = nw_ref[0:1, :]
        for c in range(ncb):
            rows, wd_e, eal_e = _ssd_chunk(act, acum, act_t, e, c)
            st_ref[c] = hs_scr[...]
            ys = []
            for g in range(SSD_G):
                gc_ = slice(g * gw, (g + 1) * gw)
                bg = act[rows, SSD_W + g * 128:SSD_W + (g + 1) * 128]
                cg = act[rows, SSD_W + 256 + g * 128:SSD_W + 256 + (g + 1) * 128]
                cbm = _dot_nt(cg, bg)
                hs = hs_scr[:, gc_]
                yin = _dot(cg, hs)
                yh = []
                for hh in range(hg):
                    h = g * hg + hh
                    lm = _ssd_lmat(acum, act_t, c, h, causal)
                    yh.append(_dot(cbm * lm, xdt[rows, h * SSD_P:(h + 1) * SSD_P]))
                ys.append(jnp.concatenate(yh, axis=1) + yin * ea_e[rows, gc_])
                hs_scr[:, gc_] = hs * eal_e[:, gc_] + _dot_tn(bg, xdt[rows, gc_] * wd_e[:, gc_])
            y = jnp.concatenate(ys, axis=1) + act[rows, 0:SSD_W] * d_e
            y_ref[rows, :] = y
            yz = y * _silu(ps_ref[rows, 1536:2560])
            outs = [_rms_fwd(yz[:, g * gw:(g + 1) * gw], nw[:, g * gw:(g + 1) * gw], gw)[2] for g in range(SSD_G)]
            ob_ref[rows, :] = jnp.concatenate(outs, axis=1).astype(ob_ref.dtype)

    def call(ps, ss, cw, cb, prm, nw):
        const = lambda i: (0, 0)
        return pl.pallas_call(
            body,
            grid=(nb,),
            in_specs=[
                pl.BlockSpec((tb, 2560), lambda i: (i, 0)),
                pl.BlockSpec((tb, 128), lambda i: (i, 0)),
                pl.BlockSpec((8, 1536), const),
                pl.BlockSpec((8, 1536), const),
                pl.BlockSpec((8, 128), const),
                pl.BlockSpec((8, SSD_W), const),
            ],
            out_specs=[
                pl.BlockSpec((tb, SSD_W), lambda i: (i, 0)),
                pl.BlockSpec((ncb, SSD_N, SSD_W), lambda i: (i, 0, 0)),
                pl.BlockSpec((tb, 1536), lambda i: (i, 0)),
                pl.BlockSpec((tb, SSD_W), lambda i: (i, 0)),
            ],
            out_shape=[
                jax.ShapeDtypeStruct((seq, SSD_W), BF16),
                jax.ShapeDtypeStruct((seq // CH, SSD_N, SSD_W), F32),
                jax.ShapeDtypeStruct((seq, 1536), F32),
                jax.ShapeDtypeStruct((seq, SSD_W), F32),
            ],
            scratch_shapes=[pltpu.VMEM((SSD_N, SSD_W), F32), pltpu.VMEM((8, 1536), F32)],
            compiler_params=pltpu.CompilerParams(dimension_semantics=("arbitrary",), vmem_limit_bytes=VMEM_LIMIT),
            name="ssd_fwd",
        )(ps, ss, cw, cb, prm, nw)

    return call


def _make_ssd_bwd(seq, tb):
    ncb = tb // CH
    nb = seq // tb
    hb = tb // 8
    hg = SSD_H // SSD_G
    gw = SSD_W // SSD_G

    def body(ps_ref, pre_ref, y_ref, ss_ref, cw_ref, cb_ref, prm_ref, nw_ref, st_ref, dob_ref,
             dps_ref, dss_ref, dcw_ref, dcb_ref, dprm_ref, dnw_ref, dhs_scr, nxt_scr):
        i = pl.program_id(0)

        @pl.when(i == 0)
        def _():
            dhs_scr[...] = jnp.zeros_like(dhs_scr)
            nxt_scr[...] = jnp.zeros_like(nxt_scr)
            dcw_ref[...] = jnp.zeros_like(dcw_ref)
            dcb_ref[...] = jnp.zeros_like(dcb_ref)
            dprm_ref[...] = jnp.zeros_like(dprm_ref)
            dnw_ref[...] = jnp.zeros_like(dnw_ref)

        causal, _, _ = _masks()
        cw = cw_ref[...]
        prm = prm_ref[...]
        (raw, _, pre, act, dt_in, dt, a, acum, e, dt_e, xdt, ea_e, d_e) = _ssd_common(
            ps_ref, None, ss_ref[...], cw, cb_ref[...], prm, tb, pre=pre_ref[...])
        act_t = acum.T
        nw = nw_ref[0:1, :]

        dx_l, db_l, dc_l, dz_l, ddt_l, da_l = ([None] * ncb for _ in range(6))
        upper_tri = (_iota2((CH, CH), 1) >= _iota2((CH, CH), 0)).astype(F32)
        tri_pair = jnp.concatenate([upper_tri, (_iota2((CH, CH), 1) < _iota2((CH, CH), 0)).astype(F32)], axis=1)
        below = jnp.bitwise_and(_iota2((CH, gw), 1), CH - 1) < _iota2((CH, gw), 0)
        dnw_acc = jnp.zeros((1, SSD_W), F32)
        dd_acc = jnp.zeros((1, SSD_W), F32)

        for c in reversed(range(ncb)):
            rows, wd_e, eal_e = _ssd_chunk(act, acum, act_t, e, c)
            xc = act[rows, 0:SSD_W]
            z = ps_ref[rows, 1536:2560]
            dob = dob_ref[rows, :]
            sz = _silu(z)
            dy_g, dz_g, dxdt_g, db_g, dc_g, da_g = [], [], [], [], [], []
            for g in range(SSD_G):
                gc_ = slice(g * gw, (g + 1) * gw)
                bg = act[rows, SSD_W + g * 128:SSD_W + (g + 1) * 128]
                cg = act[rows, SSD_W + 256 + g * 128:SSD_W + 256 + (g + 1) * 128]
                cbm = _dot_nt(cg, bg)
                hs = st_ref[c, :, gc_]
                yin = _dot(cg, hs)
                lmats = [_ssd_lmat(acum, act_t, c, g * hg + hh, causal) for hh in range(hg)]
                ea_g = ea_e[rows, gc_]
                y = y_ref[rows, gc_]
                yz = y * sz[:, gc_]
                on, r, _ = _rms_fwd(yz, nw[:, gc_], gw)
                dyz, dnw_rows = _rms_bwd(dob[:, gc_], on, r, nw[:, gc_], gw)
                dnw_acc = dnw_acc + _put_cols(jnp.sum(dnw_rows, axis=0, keepdims=True), g, gw)
                dy = dyz * sz[:, gc_]
                dz_g.append(dyz * y * _dsilu(z[:, gc_]))
                dd_acc = dd_acc + _put_cols(jnp.sum(dy * xc[:, gc_], axis=0, keepdims=True), g, gw)
                dhs_n = dhs_scr[:, gc_]
                dyin = dy * ea_g
                dcg = _dot_nt(dyin, hs)
                xw = xdt[rows, gc_] * wd_e[:, gc_]
                dbg = _dot_nt(xw, dhs_n)
                dxw = _dot(bg, dhs_n)
                dhs_scr[:, gc_] = dhs_n * eal_e[:, gc_] + _dot_tn(cg, dyin)
                dxi, ms, dcbm = [], [], jnp.zeros((CH, CH), F32)
                for hh in range(hg):
                    h = g * hg + hh
                    hc = slice(hh * SSD_P, (hh + 1) * SSD_P)
                    dyh = dy[:, hc]
                    lm = cbm * lmats[hh]
                    dxi.append(_dot_tn(lm, dyh))
                    dlm = _dot_nt(dyh, xdt[rows, h * SSD_P:(h + 1) * SSD_P])
                    ms.append(dlm * lm)
                    dcbm = dcbm + dlm * lmats[hh]
                dx_intra = jnp.concatenate(dxi, axis=1)
                ncat = _dot(upper_tri, jnp.concatenate(ms, axis=1))
                cum = _dot(tri_pair, jnp.concatenate([dy * yin * ea_g, dxw * xw], axis=0))
                da_g.append(jnp.where(below, ncat, 0.0) + cum
                            + jnp.sum(hs * dhs_n, axis=0, keepdims=True) * eal_e[:, gc_])
                dxdt_g.append(dx_intra + dxw * wd_e[:, gc_])
                dy_g.append(dy)
                db_g.append(dbg + _dot_tn(dcbm, cg))
                dc_g.append(dcg + _dot(dcbm, bg))
            dy = jnp.concatenate(dy_g, axis=1)
            dxdt = jnp.concatenate(dxdt_g, axis=1)
            dx_l[c] = dxdt * dt_e[rows, :] + dy * d_e
            db_l[c] = jnp.concatenate(db_g, axis=1)
            dc_l[c] = jnp.concatenate(dc_g, axis=1)
            dz_l[c] = jnp.concatenate(dz_g, axis=1)
            ddt_l[c] = _reduce_heads1(dxdt * xc, e)
            da_l[c] = _reduce_heads1(jnp.concatenate(da_g, axis=1), e)

        da = jnp.concatenate(da_l, axis=0)
        neg_ea = -jnp.exp(prm[0:1, :])
        ddt = jnp.concatenate(ddt_l, axis=0) + da * neg_ea
        ddt_in = ddt * _sigmoid(dt_in)
        dss_ref[...] = ddt_in.astype(dss_ref.dtype)
        sub8 = _iota2((8, 128), 0)
        dalog = jnp.sum(da * a, axis=0, keepdims=True)
        ddtb = jnp.sum(ddt_in, axis=0, keepdims=True)
        dd = _reduce_heads(_row8(dd_acc), e)[0:1, :]
        dprm_ref[...] += (jnp.where(sub8 == 0, dalog, 0.0) + jnp.where(sub8 == 1, ddtb, 0.0)
                          + jnp.where(sub8 == 2, dd, 0.0))
        dnw_ref[...] += jnp.where(_iota2((8, SSD_W), 0) == 0, dnw_acc, 0.0)

        dact = jnp.concatenate([jnp.concatenate(dx_l, axis=0), jnp.concatenate(db_l, axis=0),
                                jnp.concatenate(dc_l, axis=0)], axis=1)
        dpre = dact * _dsilu(pre)
        back = _conv_back(dpre, nxt_scr[...], tb)
        nxt_scr[...] = dpre[0:8, :]
        draw = back[0] * cw[3:4, :] + back[1] * cw[2:3, :] + back[2] * cw[1:2, :] + back[3] * cw[0:1, :]
        dps_ref[:, 0:1536] = draw.astype(dps_ref.dtype)
        dps_ref[:, 1536:2560] = jnp.concatenate(dz_l, axis=0).astype(dps_ref.dtype)
        sub_c = _iota2((8, 1536), 0)
        dcw_new = jnp.zeros((8, 1536), F32)
        for s_ in range(CONV_W):
            dcw_new = dcw_new + jnp.where(sub_c == 3 - s_, jnp.sum(back[s_] * raw, axis=0, keepdims=True), 0.0)
        dcw_ref[...] += dcw_new
        dcb_ref[...] += jnp.where(sub_c == 0, jnp.sum(dpre, axis=0, keepdims=True), 0.0)

    def call(ps, pre, y, ss, cw, cb, prm, nw, st, dob, comm=None, comm_args=()):
        rev = lambda i: (nb - 1 - i, 0)
        const = lambda i: (0, 0)
        cx = _exchange_specs(comm)
        return pl.pallas_call(
            _with_exchange(body, comm, 10, 6, nb),
            grid=(nb,),
            in_specs=[
                pl.BlockSpec((tb, 2560), rev),
                pl.BlockSpec((tb, 1536), rev),
                pl.BlockSpec((tb, SSD_W), rev),
                pl.BlockSpec((tb, 128), rev),
                pl.BlockSpec((8, 1536), const),
                pl.BlockSpec((8, 1536), const),
                pl.BlockSpec((8, 128), const),
                pl.BlockSpec((8, SSD_W), const),
                pl.BlockSpec((ncb, SSD_N, SSD_W), lambda i: (nb - 1 - i, 0, 0)),
                pl.BlockSpec((tb, SSD_W), rev),
            ] + cx["specs"],
            out_specs=[
                pl.BlockSpec((tb, 2560), rev),
                pl.BlockSpec((tb, 128), rev),
                pl.BlockSpec((8, 1536), const),
                pl.BlockSpec((8, 1536), const),
                pl.BlockSpec((8, 128), const),
                pl.BlockSpec((8, SSD_W), const),
            ] + cx["specs"],
            out_shape=[
                jax.ShapeDtypeStruct((seq, 2560), BF16),
                jax.ShapeDtypeStruct((seq, 128), BF16),
                jax.ShapeDtypeStruct((8, 1536), F32),
                jax.ShapeDtypeStruct((8, 1536), F32),
                jax.ShapeDtypeStruct((8, 128), F32),
                jax.ShapeDtypeStruct((8, SSD_W), F32),
            ] + cx["out_shape"],
            scratch_shapes=[pltpu.VMEM((SSD_N, SSD_W), F32), pltpu.VMEM((8, 1536), F32)] + cx["scratch"],
            compiler_params=pltpu.CompilerParams(dimension_semantics=("arbitrary",), vmem_limit_bytes=VMEM_LIMIT,
                                                 has_side_effects=comm is not None),
            name="ssd_bwd" + cx["tag"],
        )(ps, pre, y, ss, cw, cb, prm, nw, st, dob, *comm_args)

    return call


def _ret_consts(h):
    lg = math.log(1.0 - 2.0 ** (-5.0 - h))
    r = _iota2((CH, CH), 0)
    c = _iota2((CH, CH), 1)
    rel = (r - c).astype(F32)
    dmat = jnp.where(r >= c, jnp.exp(jnp.maximum(rel, 0.0) * lg), 0.0)
    idx = _iota2((CH, 1), 0).astype(F32)
    qdec = jnp.exp((idx + 1.0) * lg)
    kdec = jnp.exp((CH - 1.0 - idx) * lg)
    cdec = math.exp(CH * lg)
    return dmat, qdec, kdec, cdec


def _ret_batch(pr_ref, cc_ref, ss_ref, ncb):
    pairs = [(c, h) for c in range(ncb) for h in range(RET_H)]

    def st(off):
        return jnp.stack([pr_ref[c * CH:(c + 1) * CH, off + h * 128:off + (h + 1) * 128] for c, h in pairs])

    cc = jnp.stack([cc_ref[c * CH:(c + 1) * CH, :] for c, _ in pairs])
    ss = jnp.stack([ss_ref[c * CH:(c + 1) * CH, :] for c, _ in pairs])
    consts = [_ret_consts(h) for h in range(RET_H)]
    dmat = jnp.stack([consts[h][0] for _, h in pairs])
    qdec = jnp.stack([consts[h][1] for _, h in pairs])
    kdec = jnp.stack([consts[h][2] for _, h in pairs])
    cdec = jnp.stack([jnp.full((1, 1), consts[h][3], F32) for h in range(RET_H)])
    q = _rot(st(0), cc, ss)
    k = _rot(st(512), cc, ss) * (RET_D ** -0.5)
    return dict(q=q, k=k, v=st(1024), z=st(1536), cc=cc, ss=ss, dmat=dmat, qdec=qdec, kdec=kdec, cdec=cdec,
                s=_bdot(q, k, _NT) * dmat)


def _rot(t, cc, ss):
    return t * cc + pltpu.roll(t, 64, axis=t.ndim - 1) * ss


def _rot_bwd(d, cc, ss):
    return d * cc + pltpu.roll(d * ss, 64, axis=d.ndim - 1)


def _make_ret_fwd(seq, tb):
    ncb = tb // CH
    nb = seq // tb

    def body(pr_ref, cc_ref, ss_ref, nw_ref, oc_ref, st_ref, r_scr):
        @pl.when(pl.program_id(0) == 0)
        def _():
            r_scr[...] = jnp.zeros_like(r_scr)

        d = _ret_batch(pr_ref, cc_ref, ss_ref, ncb)
        kd = d["k"] * d["kdec"]
        for c in range(ncb):
            bs = slice(c * RET_H, (c + 1) * RET_H)
            rs = r_scr[...]
            st_ref[c] = rs
            r_scr[...] = rs * d["cdec"] + _bdot(kd[bs], d["v"][bs], _TN)
        r_prev = st_ref[...].reshape(ncb * RET_H, 128, 128)
        o = _bdot(d["s"], d["v"], _NN) + _bdot(d["q"], r_prev, _NN) * d["qdec"]
        _, _, y = _rms_fwd(o, nw_ref[0:1, :], RET_D)
        out = y * _silu(d["z"])
        for c in range(ncb):
            for h in range(RET_H):
                oc_ref[c * CH:(c + 1) * CH, h * 128:(h + 1) * 128] = out[c * RET_H + h].astype(oc_ref.dtype)

    def call(pr, cc, ss, nw):
        return pl.pallas_call(
            body,
            grid=(nb,),
            in_specs=[
                pl.BlockSpec((tb, 2048), lambda i: (i, 0)),
                pl.BlockSpec((tb, 128), lambda i: (i, 0)),
                pl.BlockSpec((tb, 128), lambda i: (i, 0)),
                pl.BlockSpec((8, 128), lambda i: (0, 0)),
            ],
            out_specs=[
                pl.BlockSpec((tb, 512), lambda i: (i, 0)),
                pl.BlockSpec((ncb, RET_H, 128, 128), lambda i: (i, 0, 0, 0)),
            ],
            out_shape=[
                jax.ShapeDtypeStruct((seq, 512), BF16),
                jax.ShapeDtypeStruct((seq // CH, RET_H, 128, 128), F32),
            ],
            scratch_shapes=[pltpu.VMEM((RET_H, 128, 128), F32)],
            compiler_params=pltpu.CompilerParams(dimension_semantics=("arbitrary",), vmem_limit_bytes=VMEM_LIMIT),
            name="ret_fwd",
        )(pr, cc, ss, nw)

    return call


def _make_ret_bwd(seq, tb):
    ncb = tb // CH
    nb = seq // tb

    def body(pr_ref, cc_ref, ss_ref, nw_ref, st_ref, doc_ref, dpr_ref, dnw_ref, dr_scr):
        @pl.when(pl.program_id(0) == 0)
        def _():
            dr_scr[...] = jnp.zeros_like(dr_scr)
            dnw_ref[...] = jnp.zeros_like(dnw_ref)

        nw = nw_ref[0:1, :]
        scale = RET_D ** -0.5
        n = ncb * RET_H
        d = _ret_batch(pr_ref, cc_ref, ss_ref, ncb)
        q, k, v, z, s = d["q"], d["k"], d["v"], d["z"], d["s"]
        r_prev = st_ref[...].reshape(n, 128, 128)
        o = _bdot(s, v, _NN) + _bdot(q, r_prev, _NN) * d["qdec"]
        doc = jnp.stack([doc_ref[c * CH:(c + 1) * CH, h * 128:(h + 1) * 128]
                         for c in range(ncb) for h in range(RET_H)])
        on, r, y = _rms_fwd(o, nw, RET_D)
        dz = doc * y * _dsilu(z)
        do, dnw_rows = _rms_bwd(doc * _silu(z), on, r, nw, RET_D)
        dnw_acc = jnp.sum(jnp.sum(dnw_rows, axis=0), axis=0, keepdims=True)
        dqd = do * d["qdec"]
        qtd = _bdot(q, dqd, _TN)
        drn_l = [None] * ncb
        for c in reversed(range(ncb)):
            drn_l[c] = dr_scr[...]
            dr_scr[...] = qtd[c * RET_H:(c + 1) * RET_H] + d["cdec"] * drn_l[c]
        drn = jnp.concatenate(drn_l, axis=0)
        ds = _bdot(do, v, _NT) * d["dmat"]
        dq = _rot_bwd(_bdot(ds, k, _NN) + _bdot(dqd, r_prev, _NT), d["cc"], d["ss"])
        dk = _rot_bwd((_bdot(ds, q, _TN) + _bdot(v, drn, _NT) * d["kdec"]) * scale, d["cc"], d["ss"])
        dv = _bdot(s, do, _TN) + _bdot(k * d["kdec"], drn, _NN)
        for c in range(ncb):
            rows = slice(c * CH, (c + 1) * CH)
            for h in range(RET_H):
                b = c * RET_H + h
                for j, val in enumerate((dq, dk, dv, dz)):
                    dpr_ref[rows, j * 512 + h * 128:j * 512 + (h + 1) * 128] = val[b].astype(dpr_ref.dtype)
        dnw_ref[...] += jnp.where(_iota2((8, 128), 0) == 0, dnw_acc, 0.0)

    def call(pr, cc, ss, nw, st, doc):
        rev = lambda i: (nb - 1 - i, 0)
        return pl.pallas_call(
            body,
            grid=(nb,),
            in_specs=[
                pl.BlockSpec((tb, 2048), rev),
                pl.BlockSpec((tb, 128), rev),
                pl.BlockSpec((tb, 128), rev),
                pl.BlockSpec((8, 128), lambda i: (0, 0)),
                pl.BlockSpec((ncb, RET_H, 128, 128), lambda i: (nb - 1 - i, 0, 0, 0)),
                pl.BlockSpec((tb, 512), rev),
            ],
            out_specs=[
                pl.BlockSpec((tb, 2048), rev),
                pl.BlockSpec((8, 128), lambda i: (0, 0)),
            ],
            out_shape=[
                jax.ShapeDtypeStruct((seq, 2048), BF16),
                jax.ShapeDtypeStruct((8, 128), F32),
            ],
            scratch_shapes=[pltpu.VMEM((RET_H, 128, 128), F32)],
            compiler_params=pltpu.CompilerParams(dimension_semantics=("arbitrary",), vmem_limit_bytes=VMEM_LIMIT),
            name="ret_bwd",
        )(pr, cc, ss, nw, st, doc)

    return call


def _rope_tables(seq):
    half = RET_D // 2
    inv = ROPE_BASE ** (-jnp.arange(half, dtype=F32) / half)
    ang = jnp.arange(seq, dtype=jnp.int32).astype(F32)[:, None] * inv[None, :]
    cos, sin = jnp.cos(ang), jnp.sin(ang)
    return jnp.concatenate([cos, cos], axis=1), jnp.concatenate([-sin, sin], axis=1)


SEG_G, SEG_S, SEG_R, SEG_GS, SEG_SS = (0, 2048), (2048, 4608), (4608, 6656), (6656, 6784), (6784, 6912)
NP = 6912
SEGS = (SEG_G, SEG_S, SEG_R, SEG_GS, SEG_SS)


def _resident(shape):
    return pl.BlockSpec(shape, lambda i: (0,) * len(shape), pipeline_mode=pl.Buffered(1))


def _make_inproj(seq, tl):
    def body(x_ref, pn_ref, w_ref, pg_ref, ps_ref, pr_ref, gs_ref, ss_ref, ht_ref):
        x = x_ref[...]
        _, _, hn = _rms_fwd(x, pn_ref[0:1, :], D_MODEL)
        h = hn.astype(BF16)
        ht_ref[...] = hn.T.astype(BF16)
        for (a, b), o_ref in zip(SEGS, (pg_ref, ps_ref, pr_ref, gs_ref, ss_ref)):
            o_ref[...] = jnp.dot(h, w_ref[:, a:b], preferred_element_type=F32)

    def call(x, pn, w, comm=None, comm_args=()):
        row = lambda i: (i, 0)
        cx = _exchange_specs(comm)
        return pl.pallas_call(
            _with_exchange(body, comm, 3, 6, seq // tl),
            grid=(seq // tl,),
            in_specs=[pl.BlockSpec((tl, D_MODEL), row), _resident((8, D_MODEL)), _resident((D_MODEL, NP))]
            + cx["specs"],
            out_specs=[pl.BlockSpec((tl, b - a), row) for a, b in SEGS]
            + [pl.BlockSpec((D_MODEL, tl), lambda i: (0, i))] + cx["specs"],
            out_shape=[jax.ShapeDtypeStruct((seq, b - a), F32) for a, b in SEGS]
            + [jax.ShapeDtypeStruct((D_MODEL, seq), BF16)] + cx["out_shape"],
            scratch_shapes=cx["scratch"],
            compiler_params=pltpu.CompilerParams(dimension_semantics=("arbitrary",), vmem_limit_bytes=VMEM_LIMIT,
                                                 has_side_effects=comm is not None),
            name="inproj" + cx["tag"],
        )(x, pn, w, *comm_args)

    return call


def _make_outproj(seq, tl):
    def body(oa_ref, ob_ref, oc_ref, w_ref, x_ref, qn_ref, out_ref, xn_ref):
        out = (jnp.dot(oa_ref[...], w_ref[0:512, :], preferred_element_type=F32)
               + jnp.dot(ob_ref[...], w_ref[512:1536, :], preferred_element_type=F32)
               + jnp.dot(oc_ref[...], w_ref[1536:2048, :], preferred_element_type=F32))
        out_ref[...] = out
        _, _, y = _rms_fwd(out, qn_ref[0:1, :], D_MODEL)
        xn_ref[...] = x_ref[...] + y

    def call(oa, ob, oc, w, x, qn):
        row = lambda i: (i, 0)
        return pl.pallas_call(
            body,
            grid=(seq // tl,),
            in_specs=[pl.BlockSpec((tl, 512), row), pl.BlockSpec((tl, 1024), row), pl.BlockSpec((tl, 512), row),
                      _resident((2048, D_MODEL)), pl.BlockSpec((tl, D_MODEL), row), _resident((8, D_MODEL))],
            out_specs=[pl.BlockSpec((tl, D_MODEL), row), pl.BlockSpec((tl, D_MODEL), row)],
            out_shape=[jax.ShapeDtypeStruct((seq, D_MODEL), F32), jax.ShapeDtypeStruct((seq, D_MODEL), F32)],
            compiler_params=pltpu.CompilerParams(dimension_semantics=("arbitrary",), vmem_limit_bytes=VMEM_LIMIT),
            name="outproj",
        )(oa, ob, oc, w, x, qn)

    return call


def _make_outproj_loss(seq, tl):
    def body(oa_ref, ob_ref, oc_ref, w_ref, x_ref, qn_ref, t_ref, out_ref, dy_ref, loss_ref):
        @pl.when(pl.program_id(0) == 0)
        def _():
            loss_ref[...] = jnp.zeros_like(loss_ref)

        out = (jnp.dot(oa_ref[...], w_ref[0:512, :], preferred_element_type=F32)
               + jnp.dot(ob_ref[...], w_ref[512:1536, :], preferred_element_type=F32)
               + jnp.dot(oc_ref[...], w_ref[1536:2048, :], preferred_element_type=F32))
        out_ref[...] = out
        _, _, y = _rms_fwd(out, qn_ref[0:1, :], D_MODEL)
        err = (x_ref[...] + y) - t_ref[...]
        dy_ref[...] = err * (1.0 / D_MODEL)
        part = jnp.sum(jnp.sum(err * err, axis=1, keepdims=True), axis=0, keepdims=True) * (0.5 / D_MODEL)
        loss_ref[...] += jnp.where((_iota2((8, 128), 0) == 0) & (_iota2((8, 128), 1) == 0), part, 0.0)

    def call(oa, ob, oc, w, x, qn, t):
        row = lambda i: (i, 0)
        return pl.pallas_call(
            body,
            grid=(seq // tl,),
            in_specs=[pl.BlockSpec((tl, 512), row), pl.BlockSpec((tl, 1024), row), pl.BlockSpec((tl, 512), row),
                      _resident((2048, D_MODEL)), pl.BlockSpec((tl, D_MODEL), row), _resident((8, D_MODEL)),
                      pl.BlockSpec((tl, D_MODEL), row)],
            out_specs=[pl.BlockSpec((tl, D_MODEL), row), pl.BlockSpec((tl, D_MODEL), row),
                       pl.BlockSpec((8, 128), lambda i: (0, 0))],
            out_shape=[jax.ShapeDtypeStruct((seq, D_MODEL), F32), jax.ShapeDtypeStruct((seq, D_MODEL), F32),
                       jax.ShapeDtypeStruct((8, 128), F32)],
            compiler_params=pltpu.CompilerParams(dimension_semantics=("arbitrary",), vmem_limit_bytes=VMEM_LIMIT),
            name="outproj_loss",
        )(oa, ob, oc, w, x, qn, t)

    return call


def _make_outproj_bwd(seq, tl):
    def body(dxn_ref, out_ref, oa_ref, ob_ref, oc_ref, w_ref, qn_ref, doa_ref, dob_ref, doc_ref, dqn_ref, dw_ref):
        @pl.when(pl.program_id(0) == 0)
        def _():
            dqn_ref[...] = jnp.zeros_like(dqn_ref)
            dw_ref[...] = jnp.zeros_like(dw_ref)

        qn = qn_ref[0:1, :]
        on, r, _ = _rms_fwd(out_ref[...], qn, D_MODEL)
        dout, dqn_rows = _rms_bwd(dxn_ref[...], on, r, qn, D_MODEL)
        dqn_ref[...] += jnp.where(_iota2((8, D_MODEL), 0) == 0, jnp.sum(dqn_rows, axis=0, keepdims=True), 0.0)
        db = dout.astype(BF16)
        nt = (((1,), (1,)), ((), ()))
        tn = (((0,), (0,)), ((), ()))
        doa_ref[...] = lax.dot_general(db, w_ref[0:512, :], nt, preferred_element_type=F32).astype(BF16)
        dob_ref[...] = lax.dot_general(db, w_ref[512:1536, :], nt, preferred_element_type=F32).astype(BF16)
        doc_ref[...] = lax.dot_general(db, w_ref[1536:2048, :], nt, preferred_element_type=F32).astype(BF16)
        dw_ref[0:512, :] += lax.dot_general(oa_ref[...], db, tn, preferred_element_type=F32)
        dw_ref[512:1536, :] += lax.dot_general(ob_ref[...], db, tn, preferred_element_type=F32)
        dw_ref[1536:2048, :] += lax.dot_general(oc_ref[...], db, tn, preferred_element_type=F32)

    def call(dxn, out, oa, ob, oc, w, qn):
        row = lambda i: (i, 0)
        const = lambda i: (0, 0)
        return pl.pallas_call(
            body,
            grid=(seq // tl,),
            in_specs=[pl.BlockSpec((tl, D_MODEL), row), pl.BlockSpec((tl, D_MODEL), row),
                      pl.BlockSpec((tl, 512), row), pl.BlockSpec((tl, 1024), row), pl.BlockSpec((tl, 512), row),
                      _resident((2048, D_MODEL)), _resident((8, D_MODEL))],
            out_specs=[pl.BlockSpec((tl, 512), row), pl.BlockSpec((tl, 1024), row), pl.BlockSpec((tl, 512), row),
                       pl.BlockSpec((8, D_MODEL), const), pl.BlockSpec((2048, D_MODEL), const)],
            out_shape=[jax.ShapeDtypeStruct((seq, 512), BF16), jax.ShapeDtypeStruct((seq, 1024), BF16),
                       jax.ShapeDtypeStruct((seq, 512), BF16), jax.ShapeDtypeStruct((8, D_MODEL), F32),
                       jax.ShapeDtypeStruct((2048, D_MODEL), F32)],
            compiler_params=pltpu.CompilerParams(dimension_semantics=("arbitrary",), vmem_limit_bytes=VMEM_LIMIT),
            name="outproj_bwd",
        )(dxn, out, oa, ob, oc, w, qn)

    return call


def _make_inproj_bwd_dx(seq, tl):
    def body(dg_ref, ds_ref, dr_ref, dgs_ref, dss_ref, w_ref, x_ref, pn_ref, dxn_ref, dx_ref, dpn_ref):
        @pl.when(pl.program_id(0) == 0)
        def _():
            dpn_ref[...] = jnp.zeros_like(dpn_ref)

        nt = (((1,), (1,)), ((), ()))
        dh = jnp.zeros((tl, D_MODEL), F32)
        for (a, b), d_ref in zip(SEGS, (dg_ref, ds_ref, dr_ref, dgs_ref, dss_ref)):
            dh = dh + lax.dot_general(d_ref[...], w_ref[:, a:b], nt, preferred_element_type=F32)
        pn = pn_ref[0:1, :]
        on, r, _ = _rms_fwd(x_ref[...], pn, D_MODEL)
        dx, dpn_rows = _rms_bwd(dh, on, r, pn, D_MODEL)
        dx_ref[...] = dx + dxn_ref[...]
        dpn_ref[...] += jnp.where(_iota2((8, D_MODEL), 0) == 0, jnp.sum(dpn_rows, axis=0, keepdims=True), 0.0)

    def call(dg, ds, dr, dgs, dss, w, x, pn, dxn, comm=None, comm_args=()):
        row = lambda i: (i, 0)
        cx = _exchange_specs(comm)
        return pl.pallas_call(
            _with_exchange(body, comm, 9, 2, seq // tl),
            grid=(seq // tl,),
            in_specs=[pl.BlockSpec((tl, b - a), row) for a, b in SEGS]
            + [_resident((D_MODEL, NP)), pl.BlockSpec((tl, D_MODEL), row), _resident((8, D_MODEL)),
               pl.BlockSpec((tl, D_MODEL), row)] + cx["specs"],
            out_specs=[pl.BlockSpec((tl, D_MODEL), row), pl.BlockSpec((8, D_MODEL), lambda i: (0, 0))] + cx["specs"],
            out_shape=[jax.ShapeDtypeStruct((seq, D_MODEL), F32), jax.ShapeDtypeStruct((8, D_MODEL), F32)]
            + cx["out_shape"],
            scratch_shapes=cx["scratch"],
            compiler_params=pltpu.CompilerParams(dimension_semantics=("arbitrary",), vmem_limit_bytes=VMEM_LIMIT,
                                                 has_side_effects=comm is not None),
            name="inproj_bwd_dx" + cx["tag"],
        )(dg, ds, dr, dgs, dss, w, x, pn, dxn, *comm_args)

    return call


def _make_inproj_bwd_dw(seq, tl, width, tn, name):
    def body(ht_ref, d_ref, dw_ref):
        @pl.when(pl.program_id(1) == 0)
        def _():
            dw_ref[...] = jnp.zeros_like(dw_ref)

        dw_ref[...] += jnp.dot(ht_ref[...], d_ref[...], preferred_element_type=F32)

    def call(ht, d):
        return pl.pallas_call(
            body,
            grid=(width // tn, seq // tl),
            in_specs=[pl.BlockSpec((D_MODEL, tl), lambda j, i: (0, i)), pl.BlockSpec((tl, tn), lambda j, i: (i, j))],
            out_specs=pl.BlockSpec((D_MODEL, tn), lambda j, i: (0, j)),
            out_shape=jax.ShapeDtypeStruct((D_MODEL, width), F32),
            compiler_params=pltpu.CompilerParams(dimension_semantics=("arbitrary", "arbitrary"),
                                                 vmem_limit_bytes=VMEM_LIMIT),
            name=name,
        )(ht, d)

    return call


ADAM_LR, ADAM_B1, ADAM_B2, ADAM_EPS, ADAM_WD, ADAM_STEP = 0.001, 0.9, 0.999, 1e-08, 0.01, 10


def _adam_math(w, g, m, v):
    m = ADAM_B1 * m + (1.0 - ADAM_B1) * g
    v = ADAM_B2 * v + (1.0 - ADAM_B2) * (g * g)
    m_hat = m / (1.0 - ADAM_B1 ** ADAM_STEP)
    v_hat = v / (1.0 - ADAM_B2 ** ADAM_STEP)
    delta = -ADAM_LR * (m_hat / (jnp.sqrt(v_hat) + ADAM_EPS) + ADAM_WD * w)
    return delta, m, v


def _adamw(w, g, m, v, name):
    shape = w.shape
    cols = shape[-1]
    rows = w.size // cols
    tr = rows if rows <= 512 else 256
    assert rows % tr == 0

    def body(w_ref, g_ref, m_ref, v_ref, d_ref, mo_ref, vo_ref):
        d_ref[...], mo_ref[...], vo_ref[...] = _adam_math(w_ref[...], g_ref[...], m_ref[...], v_ref[...])

    spec = pl.BlockSpec((tr, cols), lambda i: (i, 0))
    outs = pl.pallas_call(
        body,
        grid=(rows // tr,),
        in_specs=[spec] * 4,
        out_specs=[spec] * 3,
        out_shape=[jax.ShapeDtypeStruct((rows, cols), F32)] * 3,
        compiler_params=pltpu.CompilerParams(dimension_semantics=("arbitrary",), vmem_limit_bytes=VMEM_LIMIT),
        name=name,
    )(*[a.reshape(rows, cols) for a in (w, g, m, v)])
    return (g,) + tuple(o.reshape(shape) for o in outs)


def _adamw_pairs(w, mine, theirs, m, v, name):
    na, r, cols = w.shape
    assert na == 2
    tr = 256
    assert r % tr == 0

    def body(w_ref, a0_ref, b0_ref, a1_ref, b1_ref, m_ref, v_ref, g_ref, d_ref, mo_ref, vo_ref):
        g = jnp.where(pl.program_id(0) == 0, a0_ref[...] + b0_ref[...], a1_ref[...] + b1_ref[...])
        g_ref[...] = g
        d_ref[...], mo_ref[...], vo_ref[...] = _adam_math(w_ref[...], g, m_ref[...], v_ref[...])

    nblk = r // tr
    full = pl.BlockSpec((None, tr, cols), lambda a, i: (a, i, 0))
    lay0 = pl.BlockSpec((None, tr, cols), lambda a, i: (0, i * (1 - a) + (nblk - 1) * a, 0))
    lay1 = pl.BlockSpec((None, tr, cols), lambda a, i: (0, i * a, 0))
    return pl.pallas_call(
        body,
        grid=(na, nblk),
        in_specs=[full, lay0, lay0, lay1, lay1, full, full],
        out_specs=[full] * 4,
        out_shape=[jax.ShapeDtypeStruct(w.shape, F32)] * 4,
        compiler_params=pltpu.CompilerParams(dimension_semantics=("arbitrary",) * 2, vmem_limit_bytes=VMEM_LIMIT),
        name=name,
    )(w, mine[0], theirs[0], mine[1], theirs[1], m, v)


MESH = pl.DeviceIdType.MESH
ANY = pl.BlockSpec(memory_space=pl.ANY)
CHIP_REL = ((1, 0), (0, 1), (1, 1))


def _flip(v, d):
    return 1 - v if d else v


def _ag_chips(arrs, name):
    n = len(arrs)

    def body(*refs):
        ins, outs = refs[:n], refs[n:2 * n]
        send_sems, recv_sems, loc_sems = refs[2 * n:]
        x, y, c = lax.axis_index("x"), lax.axis_index("y"), lax.axis_index("c")
        me = 2 * x + y

        def remote(a, k, slot):
            dx, dy = CHIP_REL[k]
            return pltpu.make_async_remote_copy(
                src_ref=ins[a], dst_ref=outs[a].at[slot], send_sem=send_sems.at[a * 3 + k],
                recv_sem=recv_sems.at[a * 3 + k], device_id=(_flip(x, dx), _flip(y, dy), c), device_id_type=MESH)

        local = [pltpu.make_async_copy(ins[a], outs[a].at[me], loc_sems.at[a]) for a in range(n)]
        for cp in local:
            cp.start()
        for a in range(n):
            for k in range(3):
                remote(a, k, me).start()
        for a in range(n):
            for k, (dx, dy) in enumerate(CHIP_REL):
                remote(a, k, 2 * _flip(x, dx) + _flip(y, dy)).wait_recv()
        for a in range(n):
            for k in range(3):
                remote(a, k, me).wait_send()
        for cp in local:
            cp.wait()

    return pl.pallas_call(
        body,
        in_specs=[ANY] * n,
        out_specs=[ANY] * n,
        out_shape=[jax.ShapeDtypeStruct((4,) + a.shape, a.dtype) for a in arrs],
        scratch_shapes=[pltpu.SemaphoreType.DMA((3 * n,)), pltpu.SemaphoreType.DMA((3 * n,)),
                        pltpu.SemaphoreType.DMA((n,))],
        compiler_params=pltpu.CompilerParams(has_side_effects=True),
        name=name,
    )(*arrs)


class _ChipExchange:
    def __init__(self, kind, arrs):
        self.kind, self.n = kind, len(arrs)
        if kind == "gather":
            self.out_shape = [jax.ShapeDtypeStruct((4,) + a.shape, a.dtype) for a in arrs]
        else:
            self.out_shape = [jax.ShapeDtypeStruct((3,) + a.shape[1:], a.dtype) for a in arrs]
        self.scratch = [pltpu.SemaphoreType.DMA((4 * self.n,)), pltpu.SemaphoreType.DMA((4 * self.n,))]

    def _copies(self, ins, outs, sems):
        send_sems, recv_sems = sems
        x, y, c = lax.axis_index("x"), lax.axis_index("y"), lax.axis_index("c")
        me = 2 * x + y
        pairs = []
        for a in range(self.n):
            for k, (dx, dy) in enumerate(CHIP_REL):
                px, py = _flip(x, dx), _flip(y, dy)
                sem = dict(send_sem=send_sems.at[4 * a + k], recv_sem=recv_sems.at[4 * a + k],
                           device_id=(px, py, c), device_id_type=MESH)
                if self.kind == "gather":
                    out = pltpu.make_async_remote_copy(src_ref=ins[a], dst_ref=outs[a].at[me], **sem)
                    inc = pltpu.make_async_remote_copy(src_ref=ins[a], dst_ref=outs[a].at[2 * px + py], **sem)
                else:
                    out = pltpu.make_async_remote_copy(src_ref=ins[a].at[2 * px + py], dst_ref=outs[a].at[k], **sem)
                    inc = out
                pairs.append((out, inc))
            if self.kind == "gather":
                own = pltpu.make_async_remote_copy(
                    src_ref=ins[a], dst_ref=outs[a].at[me], send_sem=send_sems.at[4 * a + 3],
                    recv_sem=recv_sems.at[4 * a + 3], device_id=(x, y, 1 - c), device_id_type=MESH)
                pairs.append((own, own))
        return pairs

    def start(self, ins, outs, sems):
        for out, _ in self._copies(ins, outs, sems):
            out.start()

    def finish(self, ins, outs, sems):
        pairs = self._copies(ins, outs, sems)
        for _, inc in pairs:
            inc.wait_recv()
        for out, _ in pairs:
            out.wait_send()


def _with_exchange(body, comm, n_in, n_out, nb):
    if comm is None:
        return body

    def wrapped(*refs):
        ins = refs[:n_in]
        c_in = refs[n_in:n_in + comm.n]
        outs = refs[n_in + comm.n:n_in + comm.n + n_out]
        c_out = refs[n_in + comm.n + n_out:n_in + 2 * comm.n + n_out]
        rest = refs[n_in + 2 * comm.n + n_out:]
        scratch, sems = rest[:len(rest) - 2], rest[len(rest) - 2:]

        @pl.when(pl.program_id(0) == 0)
        def _():
            comm.start(c_in, c_out, sems)

        body(*ins, *outs, *scratch)

        @pl.when(pl.program_id(0) == nb - 1)
        def _():
            comm.finish(c_in, c_out, sems)

    return wrapped


def _exchange_specs(comm):
    if comm is None:
        return dict(specs=[], out_shape=[], scratch=[], tag="")
    return dict(specs=[pl.BlockSpec(memory_space=pl.ANY)] * comm.n, out_shape=list(comm.out_shape),
                scratch=list(comm.scratch), tag="_" + comm.kind)


def _half(ref_or_shape, half):
    r = ref_or_shape[-2] // 2
    return pl.ds(half * r, r)


def _ag_rows(arrs, name):
    n = len(arrs)

    def body(*refs):
        ins, outs = refs[:n], refs[n:2 * n]
        send_sems, recv_sems, fsend_sems, frecv_sems, loc_sems = refs[2 * n:]
        x, y, c = lax.axis_index("x"), lax.axis_index("y"), lax.axis_index("c")
        me = 2 * x + y
        sib = (x, y, 1 - c)

        def chip_of(k):
            dx, dy = CHIP_REL[k]
            return _flip(x, dx), _flip(y, dy)

        def ici(a, k, slot):
            px, py = chip_of(k)
            rows = _half(arrs[a].shape, c)
            return pltpu.make_async_remote_copy(
                src_ref=ins[a].at[:, rows, :], dst_ref=outs[a].at[slot, :, rows, :], send_sem=send_sems.at[a * 3 + k],
                recv_sem=recv_sems.at[a * 3 + k], device_id=(px, py, c), device_id_type=MESH)

        def fwd(a, k, half):
            px, py = chip_of(k)
            blk = outs[a].at[2 * px + py, :, _half(arrs[a].shape, half), :]
            return pltpu.make_async_remote_copy(
                src_ref=blk, dst_ref=blk, send_sem=fsend_sems.at[a * 3 + k], recv_sem=frecv_sems.at[a * 3 + k],
                device_id=sib, device_id_type=MESH)

        own = [pltpu.make_async_remote_copy(src_ref=ins[a], dst_ref=outs[a].at[me], send_sem=loc_sems.at[a],
                                            recv_sem=loc_sems.at[n + a], device_id=sib, device_id_type=MESH)
               for a in range(n)]
        for cp in own:
            cp.start()
        for a in range(n):
            for k in range(3):
                ici(a, k, me).start()
        for a in range(n):
            for k in range(3):
                px, py = chip_of(k)
                ici(a, k, 2 * px + py).wait_recv()
                fwd(a, k, c).start()
        for a in range(n):
            for k in range(3):
                fwd(a, k, 1 - c).wait_recv()
        for a in range(n):
            for k in range(3):
                ici(a, k, me).wait_send()
                fwd(a, k, c).wait_send()
        for cp in own:
            cp.wait()

    return pl.pallas_call(
        body,
        in_specs=[ANY] * n,
        out_specs=[ANY] * n,
        out_shape=[jax.ShapeDtypeStruct((4,) + a.shape, a.dtype) for a in arrs],
        scratch_shapes=[pltpu.SemaphoreType.DMA((3 * n,)) for _ in range(4)] + [pltpu.SemaphoreType.DMA((2 * n,))],
        compiler_params=pltpu.CompilerParams(has_side_effects=True),
        name=name,
    )(*arrs)


def _sum_chips(own, recv, chip, name):
    _, na, r, cols = own.shape
    tr = 256
    assert r % tr == 0

    def body(chip_ref, o_ref, r_ref, s_ref):
        s_ref[...] = ((o_ref[...] + r_ref[0].astype(F32)) + r_ref[1].astype(F32)) + r_ref[2].astype(F32)

    return pl.pallas_call(
        body,
        grid_spec=pltpu.PrefetchScalarGridSpec(
            num_scalar_prefetch=1,
            grid=(na, r // tr),
            in_specs=[pl.BlockSpec((None, None, tr, cols), lambda a, i, ch: (ch[0], a, i, 0)),
                      pl.BlockSpec((3, None, tr, cols), lambda a, i, ch: (0, a, i, 0))],
            out_specs=pl.BlockSpec((None, tr, cols), lambda a, i, ch: (a, i, 0))),
        out_shape=jax.ShapeDtypeStruct((na, r, cols), F32),
        compiler_params=pltpu.CompilerParams(dimension_semantics=("arbitrary",) * 2, vmem_limit_bytes=VMEM_LIMIT),
        name=name,
    )(chip, own, recv)


def _swap_sibling(arrs, name):
    n = len(arrs)

    def body(*refs):
        ins, outs = refs[:n], refs[n:2 * n]
        send_sems, recv_sems = refs[2 * n:]
        x, y, c = lax.axis_index("x"), lax.axis_index("y"), lax.axis_index("c")
        cps = [pltpu.make_async_remote_copy(src_ref=ins[a], dst_ref=outs[a], send_sem=send_sems.at[a],
                                            recv_sem=recv_sems.at[a], device_id=(x, y, 1 - c), device_id_type=MESH)
               for a in range(n)]
        for cp in cps:
            cp.start()
        for cp in cps:
            cp.wait_recv()
        for cp in cps:
            cp.wait_send()

    return pl.pallas_call(
        body,
        in_specs=[ANY] * n,
        out_specs=[ANY] * n,
        out_shape=[jax.ShapeDtypeStruct(a.shape, a.dtype) for a in arrs],
        scratch_shapes=[pltpu.SemaphoreType.DMA((n,)), pltpu.SemaphoreType.DMA((n,))],
        compiler_params=pltpu.CompilerParams(has_side_effects=True),
        name=name,
    )(*arrs)


def _allreduce_small(vec, name):
    rows = vec.shape[0]

    def body(v_ref, out_ref, gat_ref, send_sems, recv_sems):
        x, y, c = lax.axis_index("x"), lax.axis_index("y"), lax.axis_index("c")
        me = 4 * x + 2 * y + c

        def remote(k, slot):
            dx, dy, dc = (k >> 2) & 1, (k >> 1) & 1, k & 1
            return pltpu.make_async_remote_copy(
                src_ref=v_ref, dst_ref=gat_ref.at[slot], send_sem=send_sems.at[k - 1], recv_sem=recv_sems.at[k - 1],
                device_id=(_flip(x, dx), _flip(y, dy), _flip(c, dc)), device_id_type=MESH)

        gat_ref[me] = v_ref[...]
        for k in range(1, 8):
            remote(k, me).start()
        for k in range(1, 8):
            dx, dy, dc = (k >> 2) & 1, (k >> 1) & 1, k & 1
            remote(k, 4 * _flip(x, dx) + 2 * _flip(y, dy) + _flip(c, dc)).wait_recv()
        for k in range(1, 8):
            remote(k, me).wait_send()
        acc = gat_ref[0]
        for j in range(1, 8):
            acc = acc + gat_ref[j]
        out_ref[...] = acc

    vm = pl.BlockSpec(memory_space=pltpu.VMEM)
    return pl.pallas_call(
        body,
        in_specs=[vm],
        out_specs=vm,
        out_shape=jax.ShapeDtypeStruct(vec.shape, F32),
        scratch_shapes=[pltpu.VMEM((8, rows, 128), F32), pltpu.SemaphoreType.DMA((7,)), pltpu.SemaphoreType.DMA((7,))],
        compiler_params=pltpu.CompilerParams(has_side_effects=True),
        name=name,
    )(vec)


def _pad8(v, width, lane0=0):
    v = v.reshape(1, -1) if v.ndim == 1 else v
    return jnp.zeros((8, width), F32).at[:v.shape[0], lane0:lane0 + v.shape[1]].set(v.astype(F32))


def _relayout_w_in(g):
    tr = 128
    q = N_IN // 4

    def body(g_ref, o_ref):
        w = jnp.concatenate([g_ref[j] for j in range(4)], axis=1)
        z = lambda n: jnp.zeros((tr, n), w.dtype)
        o_ref[...] = jnp.concatenate([w[:, 0:2048], w[:, 2056:4616], w[:, 4632:6680],
                                      w[:, 2048:2056], z(120), w[:, 4616:4632], z(112)], axis=1)

    return pl.pallas_call(
        body,
        grid=(D_MODEL // tr,),
        in_specs=[pl.BlockSpec((4, tr, q), lambda i: (0, i, 0))],
        out_specs=pl.BlockSpec((tr, NP), lambda i: (i, 0)),
        out_shape=jax.ShapeDtypeStruct((D_MODEL, NP), g.dtype),
        compiler_params=pltpu.CompilerParams(dimension_semantics=("arbitrary",), vmem_limit_bytes=VMEM_LIMIT),
        name="relayout_w_in",
    )(g)


def _unlayout_dw_in(dg, ds, dr, dsm):
    tr = 128
    q = N_IN // 4

    def body(g_ref, s_ref, r_ref, sm_ref, o_ref, ob_ref):
        w = jnp.concatenate([g_ref[...], sm_ref[:, 0:8], s_ref[...], sm_ref[:, 128:144], r_ref[...]], axis=1)
        for j in range(4):
            blk = w[:, q * j:q * (j + 1)]
            o_ref[j] = blk
            ob_ref[j] = blk.astype(BF16)

    row = lambda i: (i, 0)
    return pl.pallas_call(
        body,
        grid=(D_MODEL // tr,),
        in_specs=[pl.BlockSpec((tr, d.shape[1]), row) for d in (dg, ds, dr, dsm)],
        out_specs=[pl.BlockSpec((4, tr, q), lambda i: (0, i, 0))] * 2,
        out_shape=[jax.ShapeDtypeStruct((4, D_MODEL, q), F32), jax.ShapeDtypeStruct((4, D_MODEL, q), BF16)],
        compiler_params=pltpu.CompilerParams(dimension_semantics=("arbitrary",), vmem_limit_bytes=VMEM_LIMIT),
        name="unlayout_dw_in",
    )(dg, ds, dr, dsm)


TB = 256
TL = 256
TL_IN = 512
TL_OB = 1024
TK = 2048


def kernel(x, pre_norm, post_norm, w_in, gdn_conv, gdn_A_log, gdn_dt_bias, gdn_norm, ssd_conv, ssd_conv_b, ssd_A_log, ssd_dt_bias, ssd_D, ssd_norm, ret_norm, w_out, loss_target, m_pre_norm, m_post_norm, m_w_in, m_gdn_conv, m_gdn_A_log, m_gdn_dt_bias, m_gdn_norm, m_ssd_conv, m_ssd_conv_b, m_ssd_A_log, m_ssd_dt_bias, m_ssd_D, m_ssd_norm, m_ret_norm, m_w_out, v_pre_norm, v_post_norm, v_w_in, v_gdn_conv, v_gdn_A_log, v_gdn_dt_bias, v_gdn_norm, v_ssd_conv, v_ssd_conv_b, v_ssd_A_log, v_ssd_dt_bias, v_ssd_D, v_ssd_norm, v_ret_norm, v_w_out):
    seq = x.shape[1]
    chip = 2 * lax.axis_index("x") + lax.axis_index("y")
    x0 = x[0]

    wi_b, wo_b = w_in.astype(BF16), w_out.astype(BF16)
    (wi0_g,) = _ag_rows([wi_b[0:1]], "ag_weights")
    gcv_g, scv_g = _ag_chips([gdn_conv, ssd_conv], "ag_conv")
    full_w_in = _relayout_w_in
    wp = [full_w_in(wi0_g[:, 0]), None]
    wo = [None, None]
    ag0 = _ChipExchange("gather", [wo_b[0], wo_b[1]])
    ag1 = _ChipExchange("gather", [wi_b[1]])
    gcv = jnp.transpose(gcv_g, (1, 2, 0, 3)).reshape(DEPTH, CONV_W, 1536)
    scv = jnp.transpose(scv_g, (1, 2, 0, 3)).reshape(DEPTH, CONV_W, 1536)
    rope_c, rope_s = _rope_tables(seq)

    saved = []
    xc = x0
    for l in range(DEPTH):
        p = dict(
            pn=_pad8(pre_norm[l], D_MODEL), qn=_pad8(post_norm[l], D_MODEL),
            g_cw=_pad8(gcv[l], 1536), g_prm=_pad8(jnp.stack([gdn_A_log[l], gdn_dt_bias[l]]), 128, 4),
            g_nw=_pad8(gdn_norm[l], 128),
            s_cw=_pad8(scv[l], 1536), s_cb=_pad8(ssd_conv_b[l], 1536),
            s_prm=_pad8(jnp.stack([ssd_A_log[l], ssd_dt_bias[l], ssd_D[l]]), 128), s_nw=_pad8(ssd_norm[l], SSD_W),
            r_nw=_pad8(ret_norm[l], 128))
        if l == 0:
            pg, ps, pr, gs, ss, ht, wo0_g, wo1_g = _make_inproj(seq, TL_IN)(
                xc, p["pn"], wp[l], comm=ag0, comm_args=(wo_b[0], wo_b[1]))
            wo = [wo0_g.reshape(2048, D_MODEL), wo1_g.reshape(2048, D_MODEL)]
        else:
            pg, ps, pr, gs, ss, ht = _make_inproj(seq, TL_IN)(xc, p["pn"], wp[l])
        if l == 0:
            oa, stg, tig, uwg, gpre, wi1_g = _make_gdn_fwd(seq, TB)(
                pg, gs, p["g_cw"], p["g_prm"], p["g_nw"], comm=ag1, comm_args=(wi_b[1],))
            wp[1] = full_w_in(wi1_g)
        else:
            oa, stg, tig, uwg, gpre = _make_gdn_fwd(seq, TB)(pg, gs, p["g_cw"], p["g_prm"], p["g_nw"])
        ob, sts, spre, sy = _make_ssd_fwd(seq, TB)(ps, ss, p["s_cw"], p["s_cb"], p["s_prm"], p["s_nw"])
        oc, str_ = _make_ret_fwd(seq, TB)(pr, rope_c, rope_s, p["r_nw"])
        if l == DEPTH - 1:
            out, dxn, lossp = _make_outproj_loss(seq, TL)(oa, ob, oc, wo[l], xc, p["qn"], loss_target[0])
            xn = None
        else:
            out, xn = _make_outproj(seq, TL)(oa, ob, oc, wo[l], xc, p["qn"])
        saved.append(dict(p=p, x=xc, ht=ht, spre=spre, sy=sy, gpre=gpre, pg=pg, ps=ps, pr=pr, gs=gs, ss=ss, stg=stg, tig=tig, uwg=uwg, sts=sts, str=str_,
                          oa=oa, ob=ob, oc=oc, out=out))
        xc = xn

    small = [None] * DEPTH
    gin, gin_b, gout, q_in, q_out = ([None] * DEPTH for _ in range(5))

    for l in reversed(range(DEPTH)):
        s = saved[l]
        p = s["p"]
        doa, dob, doc, dqn, dwo_l = _make_outproj_bwd(seq, TL_OB)(dxn, s["out"], s["oa"], s["ob"], s["oc"], wo[l], p["qn"])
        gout[l] = dwo_l.reshape(4, 512, D_MODEL)
        gdn_args = (s["pg"], s["gpre"], s["gs"], p["g_cw"], p["g_prm"], p["g_nw"], s["stg"], s["tig"], s["uwg"], doa)
        dpg, dgs, dcw_g, dprm_g, dnw_g = _make_gdn_bwd(seq, TB)(*gdn_args)
        ssd_args = (s["ps"], s["spre"], s["sy"], s["ss"], p["s_cw"], p["s_cb"], p["s_prm"], p["s_nw"], s["sts"], dob)
        if l == 0:
            payload = (gin_b[1], gout[1].astype(BF16), gout[0].astype(BF16))
            dps, dss, dcw_s, dcb_s, dprm_s, dnw_s, q_in[1], q_out[1], q_out[0] = _make_ssd_bwd(seq, TB)(
                *ssd_args, comm=_ChipExchange("scatter", payload), comm_args=payload)
        else:
            dps, dss, dcw_s, dcb_s, dprm_s, dnw_s = _make_ssd_bwd(seq, TB)(*ssd_args)
        dpr, dnw_r = _make_ret_bwd(seq, TB)(s["pr"], rope_c, rope_s, p["r_nw"], s["str"], doc)
        dws = [_make_inproj_bwd_dw(seq, TK, d.shape[1], tn, f"inproj_bwd_dw{i}")(s["ht"], d)
               for i, (d, tn) in enumerate(((dpg, 2048), (dps, 1280), (dpr, 2048),
                                            (jnp.concatenate([dgs, dss], axis=1), 256)))]
        gin[l], gin_b[l] = _unlayout_dw_in(*dws)
        dx_args = (dpg, dps, dpr, dgs, dss, wp[l], s["x"], p["pn"], dxn)
        if l == 0:
            payload = (gin_b[0],)
            dx, dpn, q_in[0] = _make_inproj_bwd_dx(seq, TL_IN)(
                *dx_args, comm=_ChipExchange("scatter", payload), comm_args=payload)
        else:
            dx, dpn = _make_inproj_bwd_dx(seq, TL_IN)(*dx_args)
        small[l] = [dpn[0], dqn[0], dcw_g[0:4].reshape(-1), dprm_g[0, 4:8], dprm_g[1, 4:8], dnw_g[0],
                    dcw_s[0:4].reshape(-1), dcb_s[0], dprm_s[0, 0:16], dprm_s[1, 0:16], dprm_s[2, 0:16],
                    dnw_s[0], dnw_r[0]]
        dxn = dx
    grad_x = dxn[None]

    sizes = [a.shape[0] for a in small[0]]
    flat = jnp.concatenate(small[0] + small[1] + [lossp[0, 0:1]])
    n_flat = flat.shape[0]
    rows = -(-n_flat // 1024) * 8
    red = _allreduce_small(jnp.pad(flat, (0, rows * 128 - n_flat)).reshape(rows, 128), "allreduce_small").reshape(-1)
    per = sum(sizes)
    loss = red[2 * per]

    def pick(i):
        off = sum(sizes[:i])
        return jnp.stack([red[l * per + off:l * per + off + sizes[i]] for l in range(DEPTH)])

    g_small = dict(
        pre_norm=pick(0), post_norm=pick(1),
        gdn_conv=lax.dynamic_slice_in_dim(pick(2).reshape(DEPTH, CONV_W, 1536), chip * 384, 384, axis=2),
        gdn_A_log=pick(3), gdn_dt_bias=pick(4), gdn_norm=pick(5),
        ssd_conv=lax.dynamic_slice_in_dim(pick(6).reshape(DEPTH, CONV_W, 1536), chip * 384, 384, axis=2),
        ssd_conv_b=pick(7), ssd_A_log=pick(8), ssd_dt_bias=pick(9), ssd_D=pick(10), ssd_norm=pick(11),
        ret_norm=pick(12))

    chip1 = chip.astype(jnp.int32).reshape(1)
    s_in = [_sum_chips(gin[l][:, None], q_in[l][:, None], chip1, f"sum_chips_w_in{l}") for l in range(DEPTH)]
    s_out = [_sum_chips(gout[l][:, None], q_out[l][:, None], chip1, f"sum_chips_w_out{l}") for l in range(DEPTH)]
    t_all = _swap_sibling(s_in + s_out, "swap_grads")
    t_in, t_out = t_all[:DEPTH], t_all[DEPTH:]

    weights = dict(pre_norm=pre_norm, post_norm=post_norm, w_in=w_in, gdn_conv=gdn_conv, gdn_A_log=gdn_A_log,
                   gdn_dt_bias=gdn_dt_bias, gdn_norm=gdn_norm, ssd_conv=ssd_conv, ssd_conv_b=ssd_conv_b,
                   ssd_A_log=ssd_A_log, ssd_dt_bias=ssd_dt_bias, ssd_D=ssd_D, ssd_norm=ssd_norm, ret_norm=ret_norm,
                   w_out=w_out)
    ms = dict(pre_norm=m_pre_norm, post_norm=m_post_norm, w_in=m_w_in, gdn_conv=m_gdn_conv, gdn_A_log=m_gdn_A_log,
              gdn_dt_bias=m_gdn_dt_bias, gdn_norm=m_gdn_norm, ssd_conv=m_ssd_conv, ssd_conv_b=m_ssd_conv_b,
              ssd_A_log=m_ssd_A_log, ssd_dt_bias=m_ssd_dt_bias, ssd_D=m_ssd_D, ssd_norm=m_ssd_norm,
              ret_norm=m_ret_norm, w_out=m_w_out)
    vs = dict(pre_norm=v_pre_norm, post_norm=v_post_norm, w_in=v_w_in, gdn_conv=v_gdn_conv, gdn_A_log=v_gdn_A_log,
              gdn_dt_bias=v_gdn_dt_bias, gdn_norm=v_gdn_norm, ssd_conv=v_ssd_conv, ssd_conv_b=v_ssd_conv_b,
              ssd_A_log=v_ssd_A_log, ssd_dt_bias=v_ssd_dt_bias, ssd_D=v_ssd_D, ssd_norm=v_ssd_norm,
              ret_norm=v_ret_norm, w_out=v_w_out)
    names = list(weights)
    res = {}
    for nme in names:
        if nme == "w_in":
            res[nme] = _adamw_pairs(w_in, s_in, t_in, m_w_in, v_w_in, "adamw_w_in")
        elif nme == "w_out":
            res[nme] = _adamw_pairs(w_out, s_out, t_out, m_w_out, v_w_out, "adamw_w_out")
        else:
            res[nme] = _adamw(weights[nme], g_small[nme], ms[nme], vs[nme], "adamw_" + nme)
    return (loss, grad_x, *[res[n][0] for n in names], *[res[n][1] for n in names],
            *[res[n][2] for n in names], *[res[n][3] for n in names])
```

```python
import math

import jax
import jax.numpy as jnp
from jax import lax
from jax.experimental import pallas as pl
from jax.experimental.pallas import tpu as pltpu

F32 = jnp.float32
BF16 = jnp.bfloat16

D_MODEL = 1024
DEPTH = 2
CH = 64
CONV_W = 4
EPS = 1e-6
GDN_H, GDN_D = 4, 128
SSD_H, SSD_P, SSD_N, SSD_G = 16, 64, 128, 2
SSD_W = SSD_H * SSD_P
RET_H, RET_D = 4, 128
ROPE_BASE = 10000.0
N_IN = 6680
NEG = -1e30

V7X_VMEM_BYTES = 64 * 1024 * 1024
VMEM_LIMIT = V7X_VMEM_BYTES * 7 // 8


def _dot(a, b):
    return jnp.dot(a.astype(BF16), b.astype(BF16), preferred_element_type=F32)


def _dot_nt(a, b):
    return lax.dot_general(a.astype(BF16), b.astype(BF16), (((1,), (1,)), ((), ())), preferred_element_type=F32)


def _dot_tn(a, b):
    return lax.dot_general(a.astype(BF16), b.astype(BF16), (((0,), (0,)), ((), ())), preferred_element_type=F32)


def _split(a):
    hi = a.astype(BF16)
    return hi, (a - hi.astype(F32)).astype(BF16)


def _dot01l(m, v):
    vh, vl = _split(v)
    mb = m.astype(BF16)
    return jnp.dot(mb, vh, preferred_element_type=F32) + jnp.dot(mb, vl, preferred_element_type=F32)


def _dot01r(v, m):
    vh, vl = _split(v)
    mb = m.astype(BF16)
    return jnp.dot(vh, mb, preferred_element_type=F32) + jnp.dot(vl, mb, preferred_element_type=F32)


def _sigmoid(x):
    return jax.nn.sigmoid(x)


def _silu(x):
    return x * _sigmoid(x)


def _dsilu(x):
    s = _sigmoid(x)
    return s * (1.0 + x * (1.0 - s))


def _softplus(x):
    return jnp.maximum(x, 0.0) + jnp.log1p(jnp.exp(-jnp.abs(x)))


def _iota2(shape, dim):
    return lax.broadcasted_iota(jnp.int32, shape, dim)


def _chunk_tri(tb, upper=False):
    r = _iota2((tb, tb), 0)
    c = _iota2((tb, tb), 1)
    same = jnp.right_shift(r, 6) == jnp.right_shift(c, 6)
    return (same & ((c >= r) if upper else (c <= r))).astype(F32)


def _masks():
    r = _iota2((CH, CH), 0)
    c = _iota2((CH, CH), 1)
    return r >= c, r > c, (r == c).astype(F32)


def _put_lane(col, lane_idx, width=128):
    lane = _iota2((col.shape[0], width), 1)
    return jnp.where(lane == lane_idx, col, 0.0)


def _conv_taps(raw, halo8, tb):
    ext = jnp.concatenate([halo8, raw], axis=0)
    return [raw] + [pltpu.roll(ext, s, axis=0)[8:] for s in (1, 2, 3)]


def _conv_back(dpre, nxt8, tb):
    ext = jnp.concatenate([dpre, nxt8], axis=0)
    return [dpre] + [pltpu.roll(ext, tb + 8 - s, axis=0)[:tb] for s in (1, 2, 3)]


def _rms_fwd(o, w, n):
    r = lax.rsqrt(jnp.sum(o * o, axis=-1, keepdims=True) * (1.0 / n) + EPS)
    on = o * r
    return on, r, on * w


def _rms_bwd(dy, on, r, w, n):
    don = dy * w
    return r * (don - on * (jnp.sum(don * on, axis=-1, keepdims=True) * (1.0 / n))), dy * on


def _put_cols(v, g, gw):
    z = jnp.zeros_like(v)
    return jnp.concatenate([v, z] if g == 0 else [z, v], axis=1)


def _gdn_common(pg_ref, halo8, sm, cw, prm, tb, pre=None):
    raw = pg_ref[:, 0:1536]
    if pre is None:
        taps = _conv_taps(raw, halo8, tb)
        pre = taps[0] * cw[3:4, :] + taps[1] * cw[2:3, :] + taps[2] * cw[1:2, :] + taps[3] * cw[0:1, :]
    act = _silu(pre)
    beta = _sigmoid(sm)
    sp_in = sm + prm[1:2, :]
    g = -jnp.exp(prm[0:1, :]) * _softplus(sp_in)
    gc = _dot01l(_chunk_tri(tb), g)
    return raw, pre, act, beta, sp_in, g, gc


_NN = (((2,), (1,)), ((0,), (0,)))
_NT = (((2,), (2,)), ((0,), (0,)))
_TN = (((1,), (1,)), ((0,), (0,)))


def _bdot(a, b, dn):
    return lax.dot_general(a.astype(BF16), b.astype(BF16), dn, preferred_element_type=F32)


def _binv_unit_lower(a, eye):
    r = _iota2((CH, CH), 0)
    c = _iota2((CH, CH), 1)
    d = eye - jnp.where((jnp.right_shift(r, 1) == jnp.right_shift(c, 1)), a, 0.0)
    ab = a.astype(BF16)
    zero = jnp.zeros((), BF16)
    for lb in range(1, 6):
        same = jnp.right_shift(r, lb + 1) == jnp.right_shift(c, lb + 1)
        low = (jnp.bitwise_and(jnp.right_shift(r, lb), 1) == 1) & (jnp.bitwise_and(jnp.right_shift(c, lb), 1) == 0)
        db = d.astype(BF16)
        t = _bdot(jnp.where(same & low, ab, zero), db, _NN)
        d = d - _bdot(db, t, _NN)
    return d


def _rsum(v):
    return jnp.sum(v, axis=-1, keepdims=True)


def _gdn_batch(act, beta, gc, gct, eg_all, ncb, masks):
    causal, strict, _ = masks

    def st(fn):
        return jnp.stack([fn(c, h, slice(c * CH, (c + 1) * CH)) for c in range(ncb) for h in range(GDN_H)])

    qr = st(lambda c, h, r: act[r, h * 128:(h + 1) * 128])
    kr = st(lambda c, h, r: act[r, 512 + h * 128:512 + (h + 1) * 128])
    vh = st(lambda c, h, r: act[r, 1024 + h * 128:1024 + (h + 1) * 128])
    bh = st(lambda c, h, r: beta[r, h:h + 1])
    gcol = st(lambda c, h, r: gc[r, 4 + h:5 + h])
    grow = st(lambda c, h, r: gct[4 + h:5 + h, r])
    eg = st(lambda c, h, r: eg_all[r, 4 + h:5 + h])
    glast = st(lambda c, h, r: gc[(c + 1) * CH - 1:(c + 1) * CH, 4 + h:5 + h])
    rq = lax.rsqrt(_rsum(qr * qr) + EPS)
    rk = lax.rsqrt(_rsum(kr * kr) + EPS)
    qn = qr * rq
    kh = kr * rk
    qh = qn * (GDN_D ** -0.5)
    decay = jnp.exp(jnp.where(causal, gcol - grow, NEG))
    kb = kh * bh
    kd_scale = jnp.exp(glast - gcol)
    return dict(qn=qn, rq=rq, kh=kh, rk=rk, qh=qh, vh=vh, bh=bh, eg=eg, decay=decay, kb=kb, vb=vh * bh, kg=kb * eg,
                qg=qh * eg, kd_scale=kd_scale, kdec=kh * kd_scale, egl=jnp.exp(glast),
                a=jnp.where(strict, _bdot(kb, kh, _NT) * decay, 0.0), attn=_bdot(qh, kh, _NT) * decay)


def _make_gdn_fwd(seq, tb):
    ncb = tb // CH
    nb = seq // tb
    n = ncb * GDN_H

    def body(pg_ref, sm_ref, cw_ref, prm_ref, nw_ref, oa_ref, st_ref, ti_ref, uw_ref, pre_ref, s_scr, halo_scr):
        @pl.when(pl.program_id(0) == 0)
        def _():
            s_scr[...] = jnp.zeros_like(s_scr)
            halo_scr[...] = jnp.zeros_like(halo_scr)

        masks = _masks()
        sm = sm_ref[...]
        raw, pre, act, beta, _, _, gc = _gdn_common(pg_ref, halo_scr[...], sm, cw_ref[...], prm_ref[...], tb)
        halo_scr[...] = raw[tb - 8:tb, :]
        pre_ref[...] = pre
        d = _gdn_batch(act, beta, gc, gc.T, jnp.exp(gc), ncb, masks)
        t = _binv_unit_lower(d["a"], masks[2])
        sol = _bdot(t, jnp.concatenate([d["vb"], d["kg"]], axis=2), _NN)
        ti_ref[...] = t.reshape(ncb, GDN_H, CH, CH)
        uw_ref[...] = sol.reshape(ncb, GDN_H, CH, 256)
        u, w = sol[:, :, :128], sol[:, :, 128:]
        vns = []
        for c in range(ncb):
            bs = slice(c * GDN_H, (c + 1) * GDN_H)
            s = s_scr[...]
            st_ref[c] = s
            vn = u[bs] - _bdot(w[bs], s, _NN)
            s_scr[...] = s * d["egl"][bs] + _bdot(d["kdec"][bs], vn, _TN)
            vns.append(vn)
        v_new = jnp.concatenate(vns, axis=0)
        s_prev = st_ref[...].reshape(n, 128, 128)
        o = _bdot(d["qg"], s_prev, _NN) + _bdot(d["attn"], v_new, _NN)
        _, _, y = _rms_fwd(o, nw_ref[0:1, :], GDN_D)
        for c in range(ncb):
            rows = slice(c * CH, (c + 1) * CH)
            for h in range(GDN_H):
                z = pg_ref[rows, 1536 + h * 128:1536 + (h + 1) * 128]
                oa_ref[rows, h * 128:(h + 1) * 128] = (y[c * GDN_H + h] * _silu(z)).astype(oa_ref.dtype)

    def call(pg, sm, cw, prm, nw, comm=None, comm_args=()):
        blk4 = lambda i: (i, 0, 0, 0)
        cx = _exchange_specs(comm)
        return pl.pallas_call(
            _with_exchange(body, comm, 5, 5, nb),
            grid=(nb,),
            in_specs=[
                pl.BlockSpec((tb, 2048), lambda i: (i, 0)),
                pl.BlockSpec((tb, 128), lambda i: (i, 0)),
                pl.BlockSpec((8, 1536), lambda i: (0, 0)),
                pl.BlockSpec((8, 128), lambda i: (0, 0)),
                pl.BlockSpec((8, 128), lambda i: (0, 0)),
            ] + cx["specs"],
            out_specs=[
                pl.BlockSpec((tb, 512), lambda i: (i, 0)),
                pl.BlockSpec((ncb, GDN_H, 128, 128), blk4),
                pl.BlockSpec((ncb, GDN_H, CH, CH), blk4),
                pl.BlockSpec((ncb, GDN_H, CH, 256), blk4),
                pl.BlockSpec((tb, 1536), lambda i: (i, 0)),
            ] + cx["specs"],
            out_shape=[
                jax.ShapeDtypeStruct((seq, 512), BF16),
                jax.ShapeDtypeStruct((seq // CH, GDN_H, 128, 128), F32),
                jax.ShapeDtypeStruct((seq // CH, GDN_H, CH, CH), F32),
                jax.ShapeDtypeStruct((seq // CH, GDN_H, CH, 256), F32),
                jax.ShapeDtypeStruct((seq, 1536), F32),
            ] + cx["out_shape"],
            scratch_shapes=[pltpu.VMEM((GDN_H, 128, 128), F32), pltpu.VMEM((8, 1536), F32)] + cx["scratch"],
            compiler_params=pltpu.CompilerParams(dimension_semantics=("arbitrary",), vmem_limit_bytes=VMEM_LIMIT,
                                                 has_side_effects=comm is not None),
            name="gdn_fwd" + cx["tag"],
        )(pg, sm, cw, prm, nw, *comm_args)

    return call


def _make_gdn_bwd(seq, tb):
    ncb = tb // CH
    nb = seq // tb
    hb = tb // 8
    n = ncb * GDN_H

    def body(pg_ref, pre_ref, sm_ref, cw_ref, prm_ref, nw_ref, st_ref, ti_ref, uw_ref, doa_ref,
             dpg_ref, dsm_ref, dcw_ref, dprm_ref, dnw_ref, ds_scr, nxt_scr):
        i = pl.program_id(0)

        @pl.when(i == 0)
        def _():
            ds_scr[...] = jnp.zeros_like(ds_scr)
            nxt_scr[...] = jnp.zeros_like(nxt_scr)
            dcw_ref[...] = jnp.zeros_like(dcw_ref)
            dprm_ref[...] = jnp.zeros_like(dprm_ref)
            dnw_ref[...] = jnp.zeros_like(dnw_ref)

        masks = _masks()
        strict = masks[1]
        sm = sm_ref[...]
        cw = cw_ref[...]
        prm = prm_ref[...]
        raw, pre, act, beta, sp_in, g, gc = _gdn_common(pg_ref, None, sm, cw, prm, tb, pre=pre_ref[...])
        nw = nw_ref[0:1, :]
        row_id = _iota2((CH, 1), 0)
        d = _gdn_batch(act, beta, gc, gc.T, jnp.exp(gc), ncb, masks)
        t = ti_ref[...].reshape(n, CH, CH)
        sol = uw_ref[...].reshape(n, CH, 256)
        u, w = sol[:, :, :128], sol[:, :, 128:]
        s_prev = st_ref[...].reshape(n, 128, 128)
        v_new = u - _bdot(w, s_prev, _NN)
        o = _bdot(d["qg"], s_prev, _NN) + _bdot(d["attn"], v_new, _NN)

        pairs = [(c, h) for c in range(ncb) for h in range(GDN_H)]
        z = jnp.stack([pg_ref[c * CH:(c + 1) * CH, 1536 + h * 128:1536 + (h + 1) * 128] for c, h in pairs])
        doa = jnp.stack([doa_ref[c * CH:(c + 1) * CH, h * 128:(h + 1) * 128] for c, h in pairs])
        on, r, y = _rms_fwd(o, nw, GDN_D)
        dz = doa * y * _dsilu(z)
        do, dnw_rows = _rms_bwd(doa * _silu(z), on, r, nw, GDN_D)
        dnw_acc = jnp.sum(jnp.sum(dnw_rows, axis=0), axis=0, keepdims=True)

        dvn_in = _bdot(d["attn"], do, _TN)
        qgtdo = _bdot(d["qg"], do, _TN)
        dvn_l, dkdec_l, dgl_l = [None] * ncb, [None] * ncb, [None] * ncb
        for c in reversed(range(ncb)):
            bs = slice(c * GDN_H, (c + 1) * GDN_H)
            dsn = ds_scr[...]
            dvn_c = dvn_in[bs] + _bdot(d["kdec"][bs], dsn, _NN)
            ds_scr[...] = d["egl"][bs] * dsn + qgtdo[bs] - _bdot(w[bs], dvn_c, _TN)
            dvn_l[c] = dvn_c
            dkdec_l[c] = _bdot(v_new[bs], dsn, _NT)
            dgl_l[c] = d["egl"][bs] * jnp.sum(_rsum(s_prev[bs] * dsn), axis=1, keepdims=True)
        dvn = jnp.concatenate(dvn_l, axis=0)
        dkdec = jnp.concatenate(dkdec_l, axis=0)
        dglast = jnp.concatenate(dgl_l, axis=0)

        dqg = _bdot(do, s_prev, _NT)
        dattn = _bdot(do, v_new, _NT)
        dw = -_bdot(dvn, s_prev, _NT)
        drhs = _bdot(t, jnp.concatenate([dvn, dw], axis=2), _TN)
        dvb, dkg = drhs[:, :, :128], drhs[:, :, 128:]
        da = jnp.where(strict, -(_bdot(dvb, u, _NT) + _bdot(dkg, w, _NT)), 0.0)
        dp = da * d["decay"]
        dq_m = dattn * d["decay"]
        m = da * d["a"] + dattn * d["attn"]
        upper_tri = jnp.broadcast_to((_iota2((CH, CH), 1) >= _iota2((CH, CH), 0)).astype(BF16), (n, CH, CH))
        dg_in = _rsum(jnp.where(strict, _bdot(upper_tri, m, _NN), 0.0))
        dkb = _bdot(dp, d["kh"], _NN) + dkg * d["eg"]
        kdk_row = _rsum(dkdec * d["kdec"])
        dk = _bdot(dp, d["kb"], _TN) + _bdot(dq_m, d["qh"], _TN) + dkdec * d["kd_scale"] + dkb * d["bh"]
        dq = _bdot(dq_m, d["kh"], _NN) + dqg * d["eg"]
        dglast = dglast + jnp.sum(kdk_row, axis=1, keepdims=True)
        dgcol = (_rsum(dqg * d["qg"]) + _rsum(dkg * d["kg"]) - kdk_row + jnp.where(row_id == CH - 1, dglast, 0.0))
        dbeta = _rsum(dkb * d["kh"]) + _rsum(dvb * d["vh"])
        dn = dq * (GDN_D ** -0.5)
        dact_q = d["rq"] * (dn - d["qn"] * _rsum(dn * d["qn"]))
        dact_k = d["rk"] * (dk - d["kh"] * _rsum(dk * d["kh"]))
        dact_v = dvb * d["bh"]

        def lanes(v, lane0):
            return jnp.concatenate(
                [sum(_put_lane(v[c * GDN_H + h], lane0 + h) for h in range(GDN_H)) for c in range(ncb)], axis=0)

        def tokens(v):
            return jnp.concatenate(
                [jnp.concatenate([v[c * GDN_H + h] for h in range(GDN_H)], axis=1) for c in range(ncb)], axis=0)

        dbeta_all = lanes(dbeta, 0)
        dg = _dot01l(_chunk_tri(tb, upper=True), lanes(dgcol, 4)) + lanes(dg_in, 4)
        neg_ea = -jnp.exp(prm[0:1, :])
        da_raw = dg * neg_ea * _sigmoid(sp_in)
        db_raw = dbeta_all * beta * (1.0 - beta)
        dsm_ref[...] = (da_raw + db_raw).astype(dsm_ref.dtype)
        lane8 = _iota2((8, 128), 1)
        sub8 = _iota2((8, 128), 0)
        dalog = jnp.sum(dg * g, axis=0, keepdims=True)
        ddtb = jnp.sum(da_raw, axis=0, keepdims=True)
        dprm_ref[...] += jnp.where(sub8 == 0, dalog, 0.0) + jnp.where(sub8 == 1, ddtb, 0.0)
        dnw_ref[...] += jnp.where(sub8 == 0, dnw_acc, 0.0)

        dact = jnp.concatenate([tokens(dact_q), tokens(dact_k), tokens(dact_v)], axis=1)
        dpre = dact * _dsilu(pre)
        back = _conv_back(dpre, nxt_scr[...], tb)
        nxt_scr[...] = dpre[0:8, :]
        draw = back[0] * cw[3:4, :] + back[1] * cw[2:3, :] + back[2] * cw[1:2, :] + back[3] * cw[0:1, :]
        dpg_ref[:, 0:1536] = draw.astype(dpg_ref.dtype)
        dpg_ref[:, 1536:2048] = tokens(dz).astype(dpg_ref.dtype)
        sub_c = _iota2((8, 1536), 0)
        dcw_new = jnp.zeros((8, 1536), F32)
        for s_ in range(CONV_W):
            dcw_new = dcw_new + jnp.where(sub_c == 3 - s_, jnp.sum(back[s_] * raw, axis=0, keepdims=True), 0.0)
        dcw_ref[...] += dcw_new

    def call(pg, pre, sm, cw, prm, nw, st, ti, uw, doa, comm=None, comm_args=()):
        rev = lambda i: (nb - 1 - i, 0)
        const = lambda i: (0, 0)
        cx = _exchange_specs(comm)
        return pl.pallas_call(
            _with_exchange(body, comm, 10, 5, nb),
            grid=(nb,),
            in_specs=[
                pl.BlockSpec((tb, 2048), rev),
                pl.BlockSpec((tb, 1536), rev),
                pl.BlockSpec((tb, 128), rev),
                pl.BlockSpec((8, 1536), const),
                pl.BlockSpec((8, 128), const),
                pl.BlockSpec((8, 128), const),
                pl.BlockSpec((ncb, GDN_H, 128, 128), lambda i: (nb - 1 - i, 0, 0, 0)),
                pl.BlockSpec((ncb, GDN_H, CH, CH), lambda i: (nb - 1 - i, 0, 0, 0)),
                pl.BlockSpec((ncb, GDN_H, CH, 256), lambda i: (nb - 1 - i, 0, 0, 0)),
                pl.BlockSpec((tb, 512), rev),
            ] + cx["specs"],
            out_specs=[
                pl.BlockSpec((tb, 2048), rev),
                pl.BlockSpec((tb, 128), rev),
                pl.BlockSpec((8, 1536), const),
                pl.BlockSpec((8, 128), const),
                pl.BlockSpec((8, 128), const),
            ] + cx["specs"],
            out_shape=[
                jax.ShapeDtypeStruct((seq, 2048), BF16),
                jax.ShapeDtypeStruct((seq, 128), BF16),
                jax.ShapeDtypeStruct((8, 1536), F32),
                jax.ShapeDtypeStruct((8, 128), F32),
                jax.ShapeDtypeStruct((8, 128), F32),
            ] + cx["out_shape"],
            scratch_shapes=[pltpu.VMEM((GDN_H, 128, 128), F32), pltpu.VMEM((8, 1536), F32)] + cx["scratch"],
            compiler_params=pltpu.CompilerParams(dimension_semantics=("arbitrary",), vmem_limit_bytes=VMEM_LIMIT,
                                                 has_side_effects=comm is not None),
            name="gdn_bwd" + cx["tag"],
        )(pg, pre, sm, cw, prm, nw, st, ti, uw, doa, *comm_args)

    return call


def _expand_mat():
    r = _iota2((128, SSD_W), 0)
    c = _iota2((128, SSD_W), 1)
    return (jnp.right_shift(c, 6) == r).astype(F32)


def _reduce_heads(v, e):
    vh, vl = _split(v)
    eb = e.astype(BF16)
    nt = (((1,), (1,)), ((), ()))
    return (lax.dot_general(vh, eb, nt, preferred_element_type=F32)
            + lax.dot_general(vl, eb, nt, preferred_element_type=F32))


def _reduce_heads1(v, e):
    nt = (((1,), (1,)), ((), ()))
    return lax.dot_general(v.astype(BF16), e.astype(BF16), nt, preferred_element_type=F32)


def _row8(v):
    return jnp.broadcast_to(v, (8, v.shape[1]))


def _ssd_common(ps_ref, halo8, ss, cw, cb, prm, tb, pre=None):
    raw = ps_ref[:, 0:1536]
    taps = None
    if pre is None:
        taps = _conv_taps(raw, halo8, tb)
        pre = taps[0] * cw[3:4, :] + taps[1] * cw[2:3, :] + taps[2] * cw[1:2, :] + taps[3] * cw[0:1, :] + cb[0:1, :]
    act = _silu(pre)
    dt_in = ss + prm[1:2, :]
    dt = _softplus(dt_in)
    a = dt * (-jnp.exp(prm[0:1, :]))
    acum = _dot01l(_chunk_tri(tb), a)
    e = _expand_mat()
    dt_e = _dot01r(dt, e)
    xdt = act[:, 0:SSD_W] * dt_e
    ea_e = _dot01r(jnp.exp(acum), e)
    d_e = _dot01r(_row8(prm[2:3, :]), e)[0:1, :]
    return raw, taps, pre, act, dt_in, dt, a, acum, e, dt_e, xdt, ea_e, d_e


def _ssd_chunk(act, acum, act_t, e, c):
    r0 = c * CH
    rows = slice(r0, r0 + CH)
    alast = acum[r0 + CH - 1:r0 + CH, :]
    wdec = jnp.exp(alast - acum[rows, :])
    wd_e = _dot01r(wdec, e)
    eal_e = _dot01r(_row8(jnp.exp(alast)), e)[0:1, :]
    return rows, wd_e, eal_e


def _ssd_lmat(acum, act_t, c, h, causal):
    r0 = c * CH
    acol = acum[r0:r0 + CH, h:h + 1]
    arow = act_t[h:h + 1, r0:r0 + CH]
    return jnp.exp(jnp.where(causal, acol - arow, NEG))


def _make_ssd_fwd(seq, tb):
    ncb = tb // CH
    nb = seq // tb
    hg = SSD_H // SSD_G
    gw = SSD_W // SSD_G

    def body(ps_ref, ss_ref, cw_ref, cb_ref, prm_ref, nw_ref, ob_ref, st_ref, pre_ref, y_ref, hs_scr, halo_scr):
        @pl.when(pl.program_id(0) == 0)
        def _():
            hs_scr[...] = jnp.zeros_like(hs_scr)
            halo_scr[...] = jnp.zeros_like(halo_scr)

        causal, _, _ = _masks()
        (raw, _, pre, act, _, _, _, acum, e, _, xdt, ea_e, d_e) = _ssd_common(
            ps_ref, halo_scr[...], ss_ref[...], cw_ref[...], cb_ref[...], prm_ref[...], tb)
        halo_scr[...] = raw[tb - 8:tb, :]
        pre_ref[...] = pre
        act_t = acum.T
        nw = nw_ref[0:1, :]
        for c in range(ncb):
            rows, wd_e, eal_e = _ssd_chunk(act, acum, act_t, e, c)
            st_ref[c] = hs_scr[...]
            ys = []
            for g in range(SSD_G):
                gc_ = slice(g * gw, (g + 1) * gw)
                bg = act[rows, SSD_W + g * 128:SSD_W + (g + 1) * 128]
                cg = act[rows, SSD_W + 256 + g * 128:SSD_W + 256 + (g + 1) * 128]
                cbm = _dot_nt(cg, bg)
                hs = hs_scr[:, gc_]
                yin = _dot(cg, hs)
                yh = []
                for hh in range(hg):
                    h = g * hg + hh
                    lm = _ssd_lmat(acum, act_t, c, h, causal)
                    yh.append(_dot(cbm * lm, xdt[rows, h * SSD_P:(h + 1) * SSD_P]))
                ys.append(jnp.concatenate(yh, axis=1) + yin * ea_e[rows, gc_])
                hs_scr[:, gc_] = hs * eal_e[:, gc_] + _dot_tn(bg, xdt[rows, gc_] * wd_e[:, gc_])
            y = jnp.concatenate(ys, axis=1) + act[rows, 0:SSD_W] * d_e
            y_ref[rows, :] = y
            yz = y * _silu(ps_ref[rows, 1536:2560])
            outs = [_rms_fwd(yz[:, g * gw:(g + 1) * gw], nw[:, g * gw:(g + 1) * gw], gw)[2] for g in range(SSD_G)]
            ob_ref[rows, :] = jnp.concatenate(outs, axis=1).astype(ob_ref.dtype)

    def call(ps, ss, cw, cb, prm, nw):
        const = lambda i: (0, 0)
        return pl.pallas_call(
            body,
            grid=(nb,),
            in_specs=[
                pl.BlockSpec((tb, 2560), lambda i: (i, 0)),
                pl.BlockSpec((tb, 128), lambda i: (i, 0)),
                pl.BlockSpec((8, 1536), const),
                pl.BlockSpec((8, 1536), const),
                pl.BlockSpec((8, 128), const),
                pl.BlockSpec((8, SSD_W), const),
            ],
            out_specs=[
                pl.BlockSpec((tb, SSD_W), lambda i: (i, 0)),
                pl.BlockSpec((ncb, SSD_N, SSD_W), lambda i: (i, 0, 0)),
                pl.BlockSpec((tb, 1536), lambda i: (i, 0)),
                pl.BlockSpec((tb, SSD_W), lambda i: (i, 0)),
            ],
            out_shape=[
                jax.ShapeDtypeStruct((seq, SSD_W), BF16),
                jax.ShapeDtypeStruct((seq // CH, SSD_N, SSD_W), F32),
                jax.ShapeDtypeStruct((seq, 1536), F32),
                jax.ShapeDtypeStruct((seq, SSD_W), F32),
            ],
            scratch_shapes=[pltpu.VMEM((SSD_N, SSD_W), F32), pltpu.VMEM((8, 1536), F32)],
            compiler_params=pltpu.CompilerParams(dimension_semantics=("arbitrary",), vmem_limit_bytes=VMEM_LIMIT),
            name="ssd_fwd",
        )(ps, ss, cw, cb, prm, nw)

    return call


def _make_ssd_bwd(seq, tb):
    ncb = tb // CH
    nb = seq // tb
    hb = tb // 8
    hg = SSD_H // SSD_G
    gw = SSD_W // SSD_G

    def body(ps_ref, pre_ref, y_ref, ss_ref, cw_ref, cb_ref, prm_ref, nw_ref, st_ref, dob_ref,
             dps_ref, dss_ref, dcw_ref, dcb_ref, dprm_ref, dnw_ref, dhs_scr, nxt_scr):
        i = pl.program_id(0)

        @pl.when(i == 0)
        def _():
            dhs_scr[...] = jnp.zeros_like(dhs_scr)
            nxt_scr[...] = jnp.zeros_like(nxt_scr)
            dcw_ref[...] = jnp.zeros_like(dcw_ref)
            dcb_ref[...] = jnp.zeros_like(dcb_ref)
            dprm_ref[...] = jnp.zeros_like(dprm_ref)
            dnw_ref[...] = jnp.zeros_like(dnw_ref)

        causal, _, _ = _masks()
        cw = cw_ref[...]
        prm = prm_ref[...]
        (raw, _, pre, act, dt_in, dt, a, acum, e, dt_e, xdt, ea_e, d_e) = _ssd_common(
            ps_ref, None, ss_ref[...], cw, cb_ref[...], prm, tb, pre=pre_ref[...])
        act_t = acum.T
        nw = nw_ref[0:1, :]

        dx_l, db_l, dc_l, dz_l, ddt_l, da_l = ([None] * ncb for _ in range(6))
        upper_tri = (_iota2((CH, CH), 1) >= _iota2((CH, CH), 0)).astype(F32)
        tri_pair = jnp.concatenate([upper_tri, (_iota2((CH, CH), 1) < _iota2((CH, CH), 0)).astype(F32)], axis=1)
        below = jnp.bitwise_and(_iota2((CH, gw), 1), CH - 1) < _iota2((CH, gw), 0)
        dnw_acc = jnp.zeros((1, SSD_W), F32)
        dd_acc = jnp.zeros((1, SSD_W), F32)

        for c in reversed(range(ncb)):
            rows, wd_e, eal_e = _ssd_chunk(act, acum, act_t, e, c)
            xc = act[rows, 0:SSD_W]
            z = ps_ref[rows, 1536:2560]
            dob = dob_ref[rows, :]
            sz = _silu(z)
            dy_g, dz_g, dxdt_g, db_g, dc_g, da_g = [], [], [], [], [], []
            for g in range(SSD_G):
                gc_ = slice(g * gw, (g + 1) * gw)
                bg = act[rows, SSD_W + g * 128:SSD_W + (g + 1) * 128]
                cg = act[rows, SSD_W + 256 + g * 128:SSD_W + 256 + (g + 1) * 128]
                cbm = _dot_nt(cg, bg)
                hs = st_ref[c, :, gc_]
                yin = _dot(cg, hs)
                lmats = [_ssd_lmat(acum, act_t, c, g * hg + hh, causal) for hh in range(hg)]
                ea_g = ea_e[rows, gc_]
                y = y_ref[rows, gc_]
                yz = y * sz[:, gc_]
                on, r, _ = _rms_fwd(yz, nw[:, gc_], gw)
                dyz, dnw_rows = _rms_bwd(dob[:, gc_], on, r, nw[:, gc_], gw)
                dnw_acc = dnw_acc + _put_cols(jnp.sum(dnw_rows, axis=0, keepdims=True), g, gw)
                dy = dyz * sz[:, gc_]
                dz_g.append(dyz * y * _dsilu(z[:, gc_]))
                dd_acc = dd_acc + _put_cols(jnp.sum(dy * xc[:, gc_], axis=0, keepdims=True), g, gw)
                dhs_n = dhs_scr[:, gc_]
                dyin = dy * ea_g
                dcg = _dot_nt(dyin, hs)
                xw = xdt[rows, gc_] * wd_e[:, gc_]
                dbg = _dot_nt(xw, dhs_n)
                dxw = _dot(bg, dhs_n)
                dhs_scr[:, gc_] = dhs_n * eal_e[:, gc_] + _dot_tn(cg, dyin)
                dxi, ms, dcbm = [], [], jnp.zeros((CH, CH), F32)
                for hh in range(hg):
                    h = g * hg + hh
                    hc = slice(hh * SSD_P, (hh + 1) * SSD_P)
                    dyh = dy[:, hc]
                    lm = cbm * lmats[hh]
                    dxi.append(_dot_tn(lm, dyh))
                    dlm = _dot_nt(dyh, xdt[rows, h * SSD_P:(h + 1) * SSD_P])
                    ms.append(dlm * lm)
                    dcbm = dcbm + dlm * lmats[hh]
                dx_intra = jnp.concatenate(dxi, axis=1)
                ncat = _dot(upper_tri, jnp.concatenate(ms, axis=1))
                cum = _dot(tri_pair, jnp.concatenate([dy * yin * ea_g, dxw * xw], axis=0))
                da_g.append(jnp.where(below, ncat, 0.0) + cum
                            + jnp.sum(hs * dhs_n, axis=0, keepdims=True) * eal_e[:, gc_])
                dxdt_g.append(dx_intra + dxw * wd_e[:, gc_])
                dy_g.append(dy)
                db_g.append(dbg + _dot_tn(dcbm, cg))
                dc_g.append(dcg + _dot(dcbm, bg))
            dy = jnp.concatenate(dy_g, axis=1)
            dxdt = jnp.concatenate(dxdt_g, axis=1)
            dx_l[c] = dxdt * dt_e[rows, :] + dy * d_e
            db_l[c] = jnp.concatenate(db_g, axis=1)
            dc_l[c] = jnp.concatenate(dc_g, axis=1)
            dz_l[c] = jnp.concatenate(dz_g, axis=1)
            ddt_l[c] = _reduce_heads1(dxdt * xc, e)
            da_l[c] = _reduce_heads1(jnp.concatenate(da_g, axis=1), e)

        da = jnp.concatenate(da_l, axis=0)
        neg_ea = -jnp.exp(prm[0:1, :])
        ddt = jnp.concatenate(ddt_l, axis=0) + da * neg_ea
        ddt_in = ddt * _sigmoid(dt_in)
        dss_ref[...] = ddt_in.astype(dss_ref.dtype)
        sub8 = _iota2((8, 128), 0)
        dalog = jnp.sum(da * a, axis=0, keepdims=True)
        ddtb = jnp.sum(ddt_in, axis=0, keepdims=True)
        dd = _reduce_heads(_row8(dd_acc), e)[0:1, :]
        dprm_ref[...] += (jnp.where(sub8 == 0, dalog, 0.0) + jnp.where(sub8 == 1, ddtb, 0.0)
                          + jnp.where(sub8 == 2, dd, 0.0))
        dnw_ref[...] += jnp.where(_iota2((8, SSD_W), 0) == 0, dnw_acc, 0.0)

        dact = jnp.concatenate([jnp.concatenate(dx_l, axis=0), jnp.concatenate(db_l, axis=0),
                                jnp.concatenate(dc_l, axis=0)], axis=1)
        dpre = dact * _dsilu(pre)
        back = _conv_back(dpre, nxt_scr[...], tb)
        nxt_scr[...] = dpre[0:8, :]
        draw = back[0] * cw[3:4, :] + back[1] * cw[2:3, :] + back[2] * cw[1:2, :] + back[3] * cw[0:1, :]
        dps_ref[:, 0:1536] = draw.astype(dps_ref.dtype)
        dps_ref[:, 1536:2560] = jnp.concatenate(dz_l, axis=0).astype(dps_ref.dtype)
        sub_c = _iota2((8, 1536), 0)
        dcw_new = jnp.zeros((8, 1536), F32)
        for s_ in range(CONV_W):
            dcw_new = dcw_new + jnp.where(sub_c == 3 - s_, jnp.sum(back[s_] * raw, axis=0, keepdims=True), 0.0)
        dcw_ref[...] += dcw_new
        dcb_ref[...] += jnp.where(sub_c == 0, jnp.sum(dpre, axis=0, keepdims=True), 0.0)

    def call(ps, pre, y, ss, cw, cb, prm, nw, st, dob, comm=None, comm_args=()):
        rev = lambda i: (nb - 1 - i, 0)
        const = lambda i: (0, 0)
        cx = _exchange_specs(comm)
        return pl.pallas_call(
            _with_exchange(body, comm, 10, 6, nb),
            grid=(nb,),
            in_specs=[
                pl.BlockSpec((tb, 2560), rev),
                pl.BlockSpec((tb, 1536), rev),
                pl.BlockSpec((tb, SSD_W), rev),
                pl.BlockSpec((tb, 128), rev),
                pl.BlockSpec((8, 1536), const),
                pl.BlockSpec((8, 1536), const),
                pl.BlockSpec((8, 128), const),
                pl.BlockSpec((8, SSD_W), const),
                pl.BlockSpec((ncb, SSD_N, SSD_W), lambda i: (nb - 1 - i, 0, 0)),
                pl.BlockSpec((tb, SSD_W), rev),
            ] + cx["specs"],
            out_specs=[
                pl.BlockSpec((tb, 2560), rev),
                pl.BlockSpec((tb, 128), rev),
                pl.BlockSpec((8, 1536), const),
                pl.BlockSpec((8, 1536), const),
                pl.BlockSpec((8, 128), const),
                pl.BlockSpec((8, SSD_W), const),
            ] + cx["specs"],
            out_shape=[
                jax.ShapeDtypeStruct((seq, 2560), BF16),
                jax.ShapeDtypeStruct((seq, 128), BF16),
                jax.ShapeDtypeStruct((8, 1536), F32),
                jax.ShapeDtypeStruct((8, 1536), F32),
                jax.ShapeDtypeStruct((8, 128), F32),
                jax.ShapeDtypeStruct((8, SSD_W), F32),
            ] + cx["out_shape"],
            scratch_shapes=[pltpu.VMEM((SSD_N, SSD_W), F32), pltpu.VMEM((8, 1536), F32)] + cx["scratch"],
            compiler_params=pltpu.CompilerParams(dimension_semantics=("arbitrary",), vmem_limit_bytes=VMEM_LIMIT,
                                                 has_side_effects=comm is not None),
            name="ssd_bwd" + cx["tag"],
        )(ps, pre, y, ss, cw, cb, prm, nw, st, dob, *comm_args)

    return call


def _ret_consts(h):
    lg = math.log(1.0 - 2.0 ** (-5.0 - h))
    r = _iota2((CH, CH), 0)
    c = _iota2((CH, CH), 1)
    rel = (r - c).astype(F32)
    dmat = jnp.where(r >= c, jnp.exp(jnp.maximum(rel, 0.0) * lg), 0.0)
    idx = _iota2((CH, 1), 0).astype(F32)
    qdec = jnp.exp((idx + 1.0) * lg)
    kdec = jnp.exp((CH - 1.0 - idx) * lg)
    cdec = math.exp(CH * lg)
    return dmat, qdec, kdec, cdec


def _ret_batch(pr_ref, cc_ref, ss_ref, ncb):
    pairs = [(c, h) for c in range(ncb) for h in range(RET_H)]

    def st(off):
        return jnp.stack([pr_ref[c * CH:(c + 1) * CH, off + h * 128:off + (h + 1) * 128] for c, h in pairs])

    cc = jnp.stack([cc_ref[c * CH:(c + 1) * CH, :] for c, _ in pairs])
    ss = jnp.stack([ss_ref[c * CH:(c + 1) * CH, :] for c, _ in pairs])
    consts = [_ret_consts(h) for h in range(RET_H)]
    dmat = jnp.stack([consts[h][0] for _, h in pairs])
    qdec = jnp.stack([consts[h][1] for _, h in pairs])
    kdec = jnp.stack([consts[h][2] for _, h in pairs])
    cdec = jnp.stack([jnp.full((1, 1), consts[h][3], F32) for h in range(RET_H)])
    q = _rot(st(0), cc, ss)
    k = _rot(st(512), cc, ss) * (RET_D ** -0.5)
    return dict(q=q, k=k, v=st(1024), z=st(1536), cc=cc, ss=ss, dmat=dmat, qdec=qdec, kdec=kdec, cdec=cdec,
                s=_bdot(q, k, _NT) * dmat)


def _rot(t, cc, ss):
    return t * cc + pltpu.roll(t, 64, axis=t.ndim - 1) * ss


def _rot_bwd(d, cc, ss):
    return d * cc + pltpu.roll(d * ss, 64, axis=d.ndim - 1)


def _make_ret_fwd(seq, tb):
    ncb = tb // CH
    nb = seq // tb

    def body(pr_ref, cc_ref, ss_ref, nw_ref, oc_ref, st_ref, r_scr):
        @pl.when(pl.program_id(0) == 0)
        def _():
            r_scr[...] = jnp.zeros_like(r_scr)

        d = _ret_batch(pr_ref, cc_ref, ss_ref, ncb)
        kd = d["k"] * d["kdec"]
        for c in range(ncb):
            bs = slice(c * RET_H, (c + 1) * RET_H)
            rs = r_scr[...]
            st_ref[c] = rs
            r_scr[...] = rs * d["cdec"] + _bdot(kd[bs], d["v"][bs], _TN)
        r_prev = st_ref[...].reshape(ncb * RET_H, 128, 128)
        o = _bdot(d["s"], d["v"], _NN) + _bdot(d["q"], r_prev, _NN) * d["qdec"]
        _, _, y = _rms_fwd(o, nw_ref[0:1, :], RET_D)
        out = y * _silu(d["z"])
        for c in range(ncb):
            for h in range(RET_H):
                oc_ref[c * CH:(c + 1) * CH, h * 128:(h + 1) * 128] = out[c * RET_H + h].astype(oc_ref.dtype)

    def call(pr, cc, ss, nw):
        return pl.pallas_call(
            body,
            grid=(nb,),
            in_specs=[
                pl.BlockSpec((tb, 2048), lambda i: (i, 0)),
                pl.BlockSpec((tb, 128), lambda i: (i, 0)),
                pl.BlockSpec((tb, 128), lambda i: (i, 0)),
                pl.BlockSpec((8, 128), lambda i: (0, 0)),
            ],
            out_specs=[
                pl.BlockSpec((tb, 512), lambda i: (i, 0)),
                pl.BlockSpec((ncb, RET_H, 128, 128), lambda i: (i, 0, 0, 0)),
            ],
            out_shape=[
                jax.ShapeDtypeStruct((seq, 512), BF16),
                jax.ShapeDtypeStruct((seq // CH, RET_H, 128, 128), F32),
            ],
            scratch_shapes=[pltpu.VMEM((RET_H, 128, 128), F32)],
            compiler_params=pltpu.CompilerParams(dimension_semantics=("arbitrary",), vmem_limit_bytes=VMEM_LIMIT),
            name="ret_fwd",
        )(pr, cc, ss, nw)

    return call


def _make_ret_bwd(seq, tb):
    ncb = tb // CH
    nb = seq // tb

    def body(pr_ref, cc_ref, ss_ref, nw_ref, st_ref, doc_ref, dpr_ref, dnw_ref, dr_scr):
        @pl.when(pl.program_id(0) == 0)
        def _():
            dr_scr[...] = jnp.zeros_like(dr_scr)
            dnw_ref[...] = jnp.zeros_like(dnw_ref)

        nw = nw_ref[0:1, :]
        scale = RET_D ** -0.5
        n = ncb * RET_H
        d = _ret_batch(pr_ref, cc_ref, ss_ref, ncb)
        q, k, v, z, s = d["q"], d["k"], d["v"], d["z"], d["s"]
        r_prev = st_ref[...].reshape(n, 128, 128)
        o = _bdot(s, v, _NN) + _bdot(q, r_prev, _NN) * d["qdec"]
        doc = jnp.stack([doc_ref[c * CH:(c + 1) * CH, h * 128:(h + 1) * 128]
                         for c in range(ncb) for h in range(RET_H)])
        on, r, y = _rms_fwd(o, nw, RET_D)
        dz = doc * y * _dsilu(z)
        do, dnw_rows = _rms_bwd(doc * _silu(z), on, r, nw, RET_D)
        dnw_acc = jnp.sum(jnp.sum(dnw_rows, axis=0), axis=0, keepdims=True)
        dqd = do * d["qdec"]
        qtd = _bdot(q, dqd, _TN)
        drn_l = [None] * ncb
        for c in reversed(range(ncb)):
            drn_l[c] = dr_scr[...]
            dr_scr[...] = qtd[c * RET_H:(c + 1) * RET_H] + d["cdec"] * drn_l[c]
        drn = jnp.concatenate(drn_l, axis=0)
        ds = _bdot(do, v, _NT) * d["dmat"]
        dq = _rot_bwd(_bdot(ds, k, _NN) + _bdot(dqd, r_prev, _NT), d["cc"], d["ss"])
        dk = _rot_bwd((_bdot(ds, q, _TN) + _bdot(v, drn, _NT) * d["kdec"]) * scale, d["cc"], d["ss"])
        dv = _bdot(s, do, _TN) + _bdot(k * d["kdec"], drn, _NN)
        for c in range(ncb):
            rows = slice(c * CH, (c + 1) * CH)
            for h in range(RET_H):
                b = c * RET_H + h
                for j, val in enumerate((dq, dk, dv, dz)):
                    dpr_ref[rows, j * 512 + h * 128:j * 512 + (h + 1) * 128] = val[b].astype(dpr_ref.dtype)
        dnw_ref[...] += jnp.where(_iota2((8, 128), 0) == 0, dnw_acc, 0.0)

    def call(pr, cc, ss, nw, st, doc):
        rev = lambda i: (nb - 1 - i, 0)
        return pl.pallas_call(
            body,
            grid=(nb,),
            in_specs=[
                pl.BlockSpec((tb, 2048), rev),
                pl.BlockSpec((tb, 128), rev),
                pl.BlockSpec((tb, 128), rev),
                pl.BlockSpec((8, 128), lambda i: (0, 0)),
                pl.BlockSpec((ncb, RET_H, 128, 128), lambda i: (nb - 1 - i, 0, 0, 0)),
                pl.BlockSpec((tb, 512), rev),
            ],
            out_specs=[
                pl.BlockSpec((tb, 2048), rev),
                pl.BlockSpec((8, 128), lambda i: (0, 0)),
            ],
            out_shape=[
                jax.ShapeDtypeStruct((seq, 2048), BF16),
                jax.ShapeDtypeStruct((8, 128), F32),
            ],
            scratch_shapes=[pltpu.VMEM((RET_H, 128, 128), F32)],
            compiler_params=pltpu.CompilerParams(dimension_semantics=("arbitrary",), vmem_limit_bytes=VMEM_LIMIT),
            name="ret_bwd",
        )(pr, cc, ss, nw, st, doc)

    return call


def _rope_tables(seq):
    half = RET_D // 2
    inv = ROPE_BASE ** (-jnp.arange(half, dtype=F32) / half)
    ang = jnp.arange(seq, dtype=jnp.int32).astype(F32)[:, None] * inv[None, :]
    cos, sin = jnp.cos(ang), jnp.sin(ang)
    return jnp.concatenate([cos, cos], axis=1), jnp.concatenate([-sin, sin], axis=1)


SEG_G, SEG_S, SEG_R, SEG_GS, SEG_SS = (0, 2048), (2048, 4608), (4608, 6656), (6656, 6784), (6784, 6912)
NP = 6912
SEGS = (SEG_G, SEG_S, SEG_R, SEG_GS, SEG_SS)


def _resident(shape):
    return pl.BlockSpec(shape, lambda i: (0,) * len(shape), pipeline_mode=pl.Buffered(1))


def _make_inproj(seq, tl):
    def body(x_ref, pn_ref, w_ref, pg_ref, ps_ref, pr_ref, gs_ref, ss_ref, ht_ref):
        x = x_ref[...]
        _, _, hn = _rms_fwd(x, pn_ref[0:1, :], D_MODEL)
        h = hn.astype(BF16)
        ht_ref[...] = hn.T.astype(BF16)
        for (a, b), o_ref in zip(SEGS, (pg_ref, ps_ref, pr_ref, gs_ref, ss_ref)):
            o_ref[...] = jnp.dot(h, w_ref[:, a:b], preferred_element_type=F32)

    def call(x, pn, w, comm=None, comm_args=()):
        row = lambda i: (i, 0)
        cx = _exchange_specs(comm)
        return pl.pallas_call(
            _with_exchange(body, comm, 3, 6, seq // tl),
            grid=(seq // tl,),
            in_specs=[pl.BlockSpec((tl, D_MODEL), row), _resident((8, D_MODEL)), _resident((D_MODEL, NP))]
            + cx["specs"],
            out_specs=[pl.BlockSpec((tl, b - a), row) for a, b in SEGS]
            + [pl.BlockSpec((D_MODEL, tl), lambda i: (0, i))] + cx["specs"],
            out_shape=[jax.ShapeDtypeStruct((seq, b - a), F32) for a, b in SEGS]
            + [jax.ShapeDtypeStruct((D_MODEL, seq), BF16)] + cx["out_shape"],
            scratch_shapes=cx["scratch"],
            compiler_params=pltpu.CompilerParams(dimension_semantics=("arbitrary",), vmem_limit_bytes=VMEM_LIMIT,
                                                 has_side_effects=comm is not None),
            name="inproj" + cx["tag"],
        )(x, pn, w, *comm_args)

    return call


def _make_outproj(seq, tl):
    def body(oa_ref, ob_ref, oc_ref, w_ref, x_ref, qn_ref, out_ref, xn_ref):
        out = (jnp.dot(oa_ref[...], w_ref[0:512, :], preferred_element_type=F32)
               + jnp.dot(ob_ref[...], w_ref[512:1536, :], preferred_element_type=F32)
               + jnp.dot(oc_ref[...], w_ref[1536:2048, :], preferred_element_type=F32))
        out_ref[...] = out
        _, _, y = _rms_fwd(out, qn_ref[0:1, :], D_MODEL)
        xn_ref[...] = x_ref[...] + y

    def call(oa, ob, oc, w, x, qn):
        row = lambda i: (i, 0)
        return pl.pallas_call(
            body,
            grid=(seq // tl,),
            in_specs=[pl.BlockSpec((tl, 512), row), pl.BlockSpec((tl, 1024), row), pl.BlockSpec((tl, 512), row),
                      _resident((2048, D_MODEL)), pl.BlockSpec((tl, D_MODEL), row), _resident((8, D_MODEL))],
            out_specs=[pl.BlockSpec((tl, D_MODEL), row), pl.BlockSpec((tl, D_MODEL), row)],
            out_shape=[jax.ShapeDtypeStruct((seq, D_MODEL), F32), jax.ShapeDtypeStruct((seq, D_MODEL), F32)],
            compiler_params=pltpu.CompilerParams(dimension_semantics=("arbitrary",), vmem_limit_bytes=VMEM_LIMIT),
            name="outproj",
        )(oa, ob, oc, w, x, qn)

    return call


def _make_outproj_loss(seq, tl):
    def body(oa_ref, ob_ref, oc_ref, w_ref, x_ref, qn_ref, t_ref, out_ref, dy_ref, loss_ref):
        @pl.when(pl.program_id(0) == 0)
        def _():
            loss_ref[...] = jnp.zeros_like(loss_ref)

        out = (jnp.dot(oa_ref[...], w_ref[0:512, :], preferred_element_type=F32)
               + jnp.dot(ob_ref[...], w_ref[512:1536, :], preferred_element_type=F32)
               + jnp.dot(oc_ref[...], w_ref[1536:2048, :], preferred_element_type=F32))
        out_ref[...] = out
        _, _, y = _rms_fwd(out, qn_ref[0:1, :], D_MODEL)
        err = (x_ref[...] + y) - t_ref[...]
        dy_ref[...] = err * (1.0 / D_MODEL)
        part = jnp.sum(jnp.sum(err * err, axis=1, keepdims=True), axis=0, keepdims=True) * (0.5 / D_MODEL)
        loss_ref[...] += jnp.where((_iota2((8, 128), 0) == 0) & (_iota2((8, 128), 1) == 0), part, 0.0)

    def call(oa, ob, oc, w, x, qn, t):
        row = lambda i: (i, 0)
        return pl.pallas_call(
            body,
            grid=(seq // tl,),
            in_specs=[pl.BlockSpec((tl, 512), row), pl.BlockSpec((tl, 1024), row), pl.BlockSpec((tl, 512), row),
                      _resident((2048, D_MODEL)), pl.BlockSpec((tl, D_MODEL), row), _resident((8, D_MODEL)),
                      pl.BlockSpec((tl, D_MODEL), row)],
            out_specs=[pl.BlockSpec((tl, D_MODEL), row), pl.BlockSpec((tl, D_MODEL), row),
                       pl.BlockSpec((8, 128), lambda i: (0, 0))],
            out_shape=[jax.ShapeDtypeStruct((seq, D_MODEL), F32), jax.ShapeDtypeStruct((seq, D_MODEL), F32),
                       jax.ShapeDtypeStruct((8, 128), F32)],
            compiler_params=pltpu.CompilerParams(dimension_semantics=("arbitrary",), vmem_limit_bytes=VMEM_LIMIT),
            name="outproj_loss",
        )(oa, ob, oc, w, x, qn, t)

    return call


def _make_outproj_bwd(seq, tl):
    def body(dxn_ref, out_ref, oa_ref, ob_ref, oc_ref, w_ref, qn_ref, doa_ref, dob_ref, doc_ref, dqn_ref, dw_ref):
        @pl.when(pl.program_id(0) == 0)
        def _():
            dqn_ref[...] = jnp.zeros_like(dqn_ref)
            dw_ref[...] = jnp.zeros_like(dw_ref)

        qn = qn_ref[0:1, :]
        on, r, _ = _rms_fwd(out_ref[...], qn, D_MODEL)
        dout, dqn_rows = _rms_bwd(dxn_ref[...], on, r, qn, D_MODEL)
        dqn_ref[...] += jnp.where(_iota2((8, D_MODEL), 0) == 0, jnp.sum(dqn_rows, axis=0, keepdims=True), 0.0)
        db = dout.astype(BF16)
        nt = (((1,), (1,)), ((), ()))
        tn = (((0,), (0,)), ((), ()))
        doa_ref[...] = lax.dot_general(db, w_ref[0:512, :], nt, preferred_element_type=F32).astype(BF16)
        dob_ref[...] = lax.dot_general(db, w_ref[512:1536, :], nt, preferred_element_type=F32).astype(BF16)
        doc_ref[...] = lax.dot_general(db, w_ref[1536:2048, :], nt, preferred_element_type=F32).astype(BF16)
        dw_ref[0:512, :] += lax.dot_general(oa_ref[...], db, tn, preferred_element_type=F32)
        dw_ref[512:1536, :] += lax.dot_general(ob_ref[...], db, tn, preferred_element_type=F32)
        dw_ref[1536:2048, :] += lax.dot_general(oc_ref[...], db, tn, preferred_element_type=F32)

    def call(dxn, out, oa, ob, oc, w, qn):
        row = lambda i: (i, 0)
        const = lambda i: (0, 0)
        return pl.pallas_call(
            body,
            grid=(seq // tl,),
            in_specs=[pl.BlockSpec((tl, D_MODEL), row), pl.BlockSpec((tl, D_MODEL), row),
                      pl.BlockSpec((tl, 512), row), pl.BlockSpec((tl, 1024), row), pl.BlockSpec((tl, 512), row),
                      _resident((2048, D_MODEL)), _resident((8, D_MODEL))],
            out_specs=[pl.BlockSpec((tl, 512), row), pl.BlockSpec((tl, 1024), row), pl.BlockSpec((tl, 512), row),
                       pl.BlockSpec((8, D_MODEL), const), pl.BlockSpec((2048, D_MODEL), const)],
            out_shape=[jax.ShapeDtypeStruct((seq, 512), BF16), jax.ShapeDtypeStruct((seq, 1024), BF16),
                       jax.ShapeDtypeStruct((seq, 512), BF16), jax.ShapeDtypeStruct((8, D_MODEL), F32),
                       jax.ShapeDtypeStruct((2048, D_MODEL), F32)],
            compiler_params=pltpu.CompilerParams(dimension_semantics=("arbitrary",), vmem_limit_bytes=VMEM_LIMIT),
            name="outproj_bwd",
        )(dxn, out, oa, ob, oc, w, qn)

    return call


def _make_inproj_bwd_dx(seq, tl):
    def body(dg_ref, ds_ref, dr_ref, dgs_ref, dss_ref, w_ref, x_ref, pn_ref, dxn_ref, dx_ref, dpn_ref):
        @pl.when(pl.program_id(0) == 0)
        def _():
            dpn_ref[...] = jnp.zeros_like(dpn_ref)

        nt = (((1,), (1,)), ((), ()))
        dh = jnp.zeros((tl, D_MODEL), F32)
        for (a, b), d_ref in zip(SEGS, (dg_ref, ds_ref, dr_ref, dgs_ref, dss_ref)):
            dh = dh + lax.dot_general(d_ref[...], w_ref[:, a:b], nt, preferred_element_type=F32)
        pn = pn_ref[0:1, :]
        on, r, _ = _rms_fwd(x_ref[...], pn, D_MODEL)
        dx, dpn_rows = _rms_bwd(dh, on, r, pn, D_MODEL)
        dx_ref[...] = dx + dxn_ref[...]
        dpn_ref[...] += jnp.where(_iota2((8, D_MODEL), 0) == 0, jnp.sum(dpn_rows, axis=0, keepdims=True), 0.0)

    def call(dg, ds, dr, dgs, dss, w, x, pn, dxn, comm=None, comm_args=()):
        row = lambda i: (i, 0)
        cx = _exchange_specs(comm)
        return pl.pallas_call(
            _with_exchange(body, comm, 9, 2, seq // tl),
            grid=(seq // tl,),
            in_specs=[pl.BlockSpec((tl, b - a), row) for a, b in SEGS]
            + [_resident((D_MODEL, NP)), pl.BlockSpec((tl, D_MODEL), row), _resident((8, D_MODEL)),
               pl.BlockSpec((tl, D_MODEL), row)] + cx["specs"],
            out_specs=[pl.BlockSpec((tl, D_MODEL), row), pl.BlockSpec((8, D_MODEL), lambda i: (0, 0))] + cx["specs"],
            out_shape=[jax.ShapeDtypeStruct((seq, D_MODEL), F32), jax.ShapeDtypeStruct((8, D_MODEL), F32)]
            + cx["out_shape"],
            scratch_shapes=cx["scratch"],
            compiler_params=pltpu.CompilerParams(dimension_semantics=("arbitrary",), vmem_limit_bytes=VMEM_LIMIT,
                                                 has_side_effects=comm is not None),
            name="inproj_bwd_dx" + cx["tag"],
        )(dg, ds, dr, dgs, dss, w, x, pn, dxn, *comm_args)

    return call


def _make_inproj_bwd_dw(seq, tl, width, tn, name):
    def body(ht_ref, d_ref, dw_ref):
        @pl.when(pl.program_id(1) == 0)
        def _():
            dw_ref[...] = jnp.zeros_like(dw_ref)

        dw_ref[...] += jnp.dot(ht_ref[...], d_ref[...], preferred_element_type=F32)

    def call(ht, d):
        return pl.pallas_call(
            body,
            grid=(width // tn, seq // tl),
            in_specs=[pl.BlockSpec((D_MODEL, tl), lambda j, i: (0, i)), pl.BlockSpec((tl, tn), lambda j, i: (i, j))],
            out_specs=pl.BlockSpec((D_MODEL, tn), lambda j, i: (0, j)),
            out_shape=jax.ShapeDtypeStruct((D_MODEL, width), F32),
            compiler_params=pltpu.CompilerParams(dimension_semantics=("arbitrary", "arbitrary"),
                                                 vmem_limit_bytes=VMEM_LIMIT),
            name=name,
        )(ht, d)

    return call


ADAM_LR, ADAM_B1, ADAM_B2, ADAM_EPS, ADAM_WD, ADAM_STEP = 0.001, 0.9, 0.999, 1e-08, 0.01, 10


def _adam_math(w, g, m, v):
    m = ADAM_B1 * m + (1.0 - ADAM_B1) * g
    v = ADAM_B2 * v + (1.0 - ADAM_B2) * (g * g)
    m_hat = m / (1.0 - ADAM_B1 ** ADAM_STEP)
    v_hat = v / (1.0 - ADAM_B2 ** ADAM_STEP)
    delta = -ADAM_LR * (m_hat / (jnp.sqrt(v_hat) + ADAM_EPS) + ADAM_WD * w)
    return delta, m, v


def _adamw(w, g, m, v, name):
    shape = w.shape
    cols = shape[-1]
    rows = w.size // cols
    tr = rows if rows <= 512 else 256
    assert rows % tr == 0

    def body(w_ref, g_ref, m_ref, v_ref, d_ref, mo_ref, vo_ref):
        d_ref[...], mo_ref[...], vo_ref[...] = _adam_math(w_ref[...], g_ref[...], m_ref[...], v_ref[...])

    spec = pl.BlockSpec((tr, cols), lambda i: (i, 0))
    outs = pl.pallas_call(
        body,
        grid=(rows // tr,),
        in_specs=[spec] * 4,
        out_specs=[spec] * 3,
        out_shape=[jax.ShapeDtypeStruct((rows, cols), F32)] * 3,
        compiler_params=pltpu.CompilerParams(dimension_semantics=("arbitrary",), vmem_limit_bytes=VMEM_LIMIT),
        name=name,
    )(*[a.reshape(rows, cols) for a in (w, g, m, v)])
    return (g,) + tuple(o.reshape(shape) for o in outs)


def _adamw_pairs(w, mine, theirs, m, v, name):
    na, r, cols = w.shape
    assert na == 2
    tr = 256
    assert r % tr == 0

    def body(w_ref, a0_ref, b0_ref, a1_ref, b1_ref, m_ref, v_ref, g_ref, d_ref, mo_ref, vo_ref):
        g = jnp.where(pl.program_id(0) == 0, a0_ref[...] + b0_ref[...], a1_ref[...] + b1_ref[...])
        g_ref[...] = g
        d_ref[...], mo_ref[...], vo_ref[...] = _adam_math(w_ref[...], g, m_ref[...], v_ref[...])

    nblk = r // tr
    full = pl.BlockSpec((None, tr, cols), lambda a, i: (a, i, 0))
    lay0 = pl.BlockSpec((None, tr, cols), lambda a, i: (0, i * (1 - a) + (nblk - 1) * a, 0))
    lay1 = pl.BlockSpec((None, tr, cols), lambda a, i: (0, i * a, 0))
    return pl.pallas_call(
        body,
        grid=(na, nblk),
        in_specs=[full, lay0, lay0, lay1, lay1, full, full],
        out_specs=[full] * 4,
        out_shape=[jax.ShapeDtypeStruct(w.shape, F32)] * 4,
        compiler_params=pltpu.CompilerParams(dimension_semantics=("arbitrary",) * 2, vmem_limit_bytes=VMEM_LIMIT),
        name=name,
    )(w, mine[0], theirs[0], mine[1], theirs[1], m, v)


MESH = pl.DeviceIdType.MESH
ANY = pl.BlockSpec(memory_space=pl.ANY)
CHIP_REL = ((1, 0), (0, 1), (1, 1))


def _flip(v, d):
    return 1 - v if d else v


def _ag_chips(arrs, name):
    n = len(arrs)

    def body(*refs):
        ins, outs = refs[:n], refs[n:2 * n]
        send_sems, recv_sems, loc_sems = refs[2 * n:]
        x, y, c = lax.axis_index("x"), lax.axis_index("y"), lax.axis_index("c")
        me = 2 * x + y

        def remote(a, k, slot):
            dx, dy = CHIP_REL[k]
            return pltpu.make_async_remote_copy(
                src_ref=ins[a], dst_ref=outs[a].at[slot], send_sem=send_sems.at[a * 3 + k],
                recv_sem=recv_sems.at[a * 3 + k], device_id=(_flip(x, dx), _flip(y, dy), c), device_id_type=MESH)

        local = [pltpu.make_async_copy(ins[a], outs[a].at[me], loc_sems.at[a]) for a in range(n)]
        for cp in local:
            cp.start()
        for a in range(n):
            for k in range(3):
                remote(a, k, me).start()
        for a in range(n):
            for k, (dx, dy) in enumerate(CHIP_REL):
                remote(a, k, 2 * _flip(x, dx) + _flip(y, dy)).wait_recv()
        for a in range(n):
            for k in range(3):
                remote(a, k, me).wait_send()
        for cp in local:
            cp.wait()

    return pl.pallas_call(
        body,
        in_specs=[ANY] * n,
        out_specs=[ANY] * n,
        out_shape=[jax.ShapeDtypeStruct((4,) + a.shape, a.dtype) for a in arrs],
        scratch_shapes=[pltpu.SemaphoreType.DMA((3 * n,)), pltpu.SemaphoreType.DMA((3 * n,)),
                        pltpu.SemaphoreType.DMA((n,))],
        compiler_params=pltpu.CompilerParams(has_side_effects=True),
        name=name,
    )(*arrs)


class _ChipExchange:
    def __init__(self, kind, arrs):
        self.kind, self.n = kind, len(arrs)
        if kind == "gather":
            self.out_shape = [jax.ShapeDtypeStruct((4,) + a.shape, a.dtype) for a in arrs]
        else:
            self.out_shape = [jax.ShapeDtypeStruct((3,) + a.shape[1:], a.dtype) for a in arrs]
        self.scratch = [pltpu.SemaphoreType.DMA((4 * self.n,)), pltpu.SemaphoreType.DMA((4 * self.n,))]

    def _copies(self, ins, outs, sems):
        send_sems, recv_sems = sems
        x, y, c = lax.axis_index("x"), lax.axis_index("y"), lax.axis_index("c")
        me = 2 * x + y
        pairs = []
        for a in range(self.n):
            for k, (dx, dy) in enumerate(CHIP_REL):
                px, py = _flip(x, dx), _flip(y, dy)
                sem = dict(send_sem=send_sems.at[4 * a + k], recv_sem=recv_sems.at[4 * a + k],
                           device_id=(px, py, c), device_id_type=MESH)
                if self.kind == "gather":
                    out = pltpu.make_async_remote_copy(src_ref=ins[a], dst_ref=outs[a].at[me], **sem)
                    inc = pltpu.make_async_remote_copy(src_ref=ins[a], dst_ref=outs[a].at[2 * px + py], **sem)
                else:
                    out = pltpu.make_async_remote_copy(src_ref=ins[a].at[2 * px + py], dst_ref=outs[a].at[k], **sem)
                    inc = out
                pairs.append((out, inc))
            if self.kind == "gather":
                own = pltpu.make_async_remote_copy(
                    src_ref=ins[a], dst_ref=outs[a].at[me], send_sem=send_sems.at[4 * a + 3],
                    recv_sem=recv_sems.at[4 * a + 3], device_id=(x, y, 1 - c), device_id_type=MESH)
                pairs.append((own, own))
        return pairs

    def start(self, ins, outs, sems):
        for out, _ in self._copies(ins, outs, sems):
            out.start()

    def finish(self, ins, outs, sems):
        pairs = self._copies(ins, outs, sems)
        for _, inc in pairs:
            inc.wait_recv()
        for out, _ in pairs:
            out.wait_send()


def _with_exchange(body, comm, n_in, n_out, nb):
    if comm is None:
        return body

    def wrapped(*refs):
        ins = refs[:n_in]
        c_in = refs[n_in:n_in + comm.n]
        outs = refs[n_in + comm.n:n_in + comm.n + n_out]
        c_out = refs[n_in + comm.n + n_out:n_in + 2 * comm.n + n_out]
        rest = refs[n_in + 2 * comm.n + n_out:]
        scratch, sems = rest[:len(rest) - 2], rest[len(rest) - 2:]

        @pl.when(pl.program_id(0) == 0)
        def _():
            comm.start(c_in, c_out, sems)

        body(*ins, *outs, *scratch)

        @pl.when(pl.program_id(0) == nb - 1)
        def _():
            comm.finish(c_in, c_out, sems)

    return wrapped


def _exchange_specs(comm):
    if comm is None:
        return dict(specs=[], out_shape=[], scratch=[], tag="")
    return dict(specs=[pl.BlockSpec(memory_space=pl.ANY)] * comm.n, out_shape=list(comm.out_shape),
                scratch=list(comm.scratch), tag="_" + comm.kind)


def _half(ref_or_shape, half):
    r = ref_or_shape[-2] // 2
    return pl.ds(half * r, r)


def _ag_rows(arrs, name):
    n = len(arrs)

    def body(*refs):
        ins, outs = refs[:n], refs[n:2 * n]
        send_sems, recv_sems, fsend_sems, frecv_sems, loc_sems = refs[2 * n:]
        x, y, c = lax.axis_index("x"), lax.axis_index("y"), lax.axis_index("c")
        me = 2 * x + y
        sib = (x, y, 1 - c)

        def chip_of(k):
            dx, dy = CHIP_REL[k]
            return _flip(x, dx), _flip(y, dy)

        def ici(a, k, slot):
            px, py = chip_of(k)
            rows = _half(arrs[a].shape, c)
            return pltpu.make_async_remote_copy(
                src_ref=ins[a].at[:, rows, :], dst_ref=outs[a].at[slot, :, rows, :], send_sem=send_sems.at[a * 3 + k],
                recv_sem=recv_sems.at[a * 3 + k], device_id=(px, py, c), device_id_type=MESH)

        def fwd(a, k, half):
            px, py = chip_of(k)
            blk = outs[a].at[2 * px + py, :, _half(arrs[a].shape, half), :]
            return pltpu.make_async_remote_copy(
                src_ref=blk, dst_ref=blk, send_sem=fsend_sems.at[a * 3 + k], recv_sem=frecv_sems.at[a * 3 + k],
                device_id=sib, device_id_type=MESH)

        own = [pltpu.make_async_remote_copy(src_ref=ins[a], dst_ref=outs[a].at[me], send_sem=loc_sems.at[a],
                                            recv_sem=loc_sems.at[n + a], device_id=sib, device_id_type=MESH)
               for a in range(n)]
        for cp in own:
            cp.start()
        for a in range(n):
            for k in range(3):
                ici(a, k, me).start()
        for a in range(n):
            for k in range(3):
                px, py = chip_of(k)
                ici(a, k, 2 * px + py).wait_recv()
                fwd(a, k, c).start()
        for a in range(n):
            for k in range(3):
                fwd(a, k, 1 - c).wait_recv()
        for a in range(n):
            for k in range(3):
                ici(a, k, me).wait_send()
                fwd(a, k, c).wait_send()
        for cp in own:
            cp.wait()

    return pl.pallas_call(
        body,
        in_specs=[ANY] * n,
        out_specs=[ANY] * n,
        out_shape=[jax.ShapeDtypeStruct((4,) + a.shape, a.dtype) for a in arrs],
        scratch_shapes=[pltpu.SemaphoreType.DMA((3 * n,)) for _ in range(4)] + [pltpu.SemaphoreType.DMA((2 * n,))],
        compiler_params=pltpu.CompilerParams(has_side_effects=True),
        name=name,
    )(*arrs)


def _sum_chips(own, recv, chip, name):
    _, na, r, cols = own.shape
    tr = 256
    assert r % tr == 0

    def body(chip_ref, o_ref, r_ref, s_ref):
        s_ref[...] = ((o_ref[...] + r_ref[0].astype(F32)) + r_ref[1].astype(F32)) + r_ref[2].astype(F32)

    return pl.pallas_call(
        body,
        grid_spec=pltpu.PrefetchScalarGridSpec(
            num_scalar_prefetch=1,
            grid=(na, r // tr),
            in_specs=[pl.BlockSpec((None, None, tr, cols), lambda a, i, ch: (ch[0], a, i, 0)),
                      pl.BlockSpec((3, None, tr, cols), lambda a, i, ch: (0, a, i, 0))],
            out_specs=pl.BlockSpec((None, tr, cols), lambda a, i, ch: (a, i, 0))),
        out_shape=jax.ShapeDtypeStruct((na, r, cols), F32),
        compiler_params=pltpu.CompilerParams(dimension_semantics=("arbitrary",) * 2, vmem_limit_bytes=VMEM_LIMIT),
        name=name,
    )(chip, own, recv)


def _swap_sibling(arrs, name):
    n = len(arrs)

    def body(*refs):
        ins, outs = refs[:n], refs[n:2 * n]
        send_sems, recv_sems = refs[2 * n:]
        x, y, c = lax.axis_index("x"), lax.axis_index("y"), lax.axis_index("c")
        cps = [pltpu.make_async_remote_copy(src_ref=ins[a], dst_ref=outs[a], send_sem=send_sems.at[a],
                                            recv_sem=recv_sems.at[a], device_id=(x, y, 1 - c), device_id_type=MESH)
               for a in range(n)]
        for cp in cps:
            cp.start()
        for cp in cps:
            cp.wait_recv()
        for cp in cps:
            cp.wait_send()

    return pl.pallas_call(
        body,
        in_specs=[ANY] * n,
        out_specs=[ANY] * n,
        out_shape=[jax.ShapeDtypeStruct(a.shape, a.dtype) for a in arrs],
        scratch_shapes=[pltpu.SemaphoreType.DMA((n,)), pltpu.SemaphoreType.DMA((n,))],
        compiler_params=pltpu.CompilerParams(has_side_effects=True),
        name=name,
    )(*arrs)


def _allreduce_small(vec, name):
    rows = vec.shape[0]

    def body(v_ref, out_ref, gat_ref, send_sems, recv_sems):
        x, y, c = lax.axis_index("x"), lax.axis_index("y"), lax.axis_index("c")
        me = 4 * x + 2 * y + c

        def remote(k, slot):
            dx, dy, dc = (k >> 2) & 1, (k >> 1) & 1, k & 1
            return pltpu.make_async_remote_copy(
                src_ref=v_ref, dst_ref=gat_ref.at[slot], send_sem=send_sems.at[k - 1], recv_sem=recv_sems.at[k - 1],
                device_id=(_flip(x, dx), _flip(y, dy), _flip(c, dc)), device_id_type=MESH)

        gat_ref[me] = v_ref[...]
        for k in range(1, 8):
            remote(k, me).start()
        for k in range(1, 8):
            dx, dy, dc = (k >> 2) & 1, (k >> 1) & 1, k & 1
            remote(k, 4 * _flip(x, dx) + 2 * _flip(y, dy) + _flip(c, dc)).wait_recv()
        for k in range(1, 8):
            remote(k, me).wait_send()
        acc = gat_ref[0]
        for j in range(1, 8):
            acc = acc + gat_ref[j]
        out_ref[...] = acc

    vm = pl.BlockSpec(memory_space=pltpu.VMEM)
    return pl.pallas_call(
        body,
        in_specs=[vm],
        out_specs=vm,
        out_shape=jax.ShapeDtypeStruct(vec.shape, F32),
        scratch_shapes=[pltpu.VMEM((8, rows, 128), F32), pltpu.SemaphoreType.DMA((7,)), pltpu.SemaphoreType.DMA((7,))],
        compiler_params=pltpu.CompilerParams(has_side_effects=True),
        name=name,
    )(vec)


def _pad8(v, width, lane0=0):
    v = v.reshape(1, -1) if v.ndim == 1 else v
    return jnp.zeros((8, width), F32).at[:v.shape[0], lane0:lane0 + v.shape[1]].set(v.astype(F32))


def _relayout_w_in(g):
    tr = 128
    q = N_IN // 4

    def body(g_ref, o_ref):
        w = jnp.concatenate([g_ref[j] for j in range(4)], axis=1)
        z = lambda n: jnp.zeros((tr, n), w.dtype)
        o_ref[...] = jnp.concatenate([w[:, 0:2048], w[:, 2056:4616], w[:, 4632:6680],
                                      w[:, 2048:2056], z(120), w[:, 4616:4632], z(112)], axis=1)

    return pl.pallas_call(
        body,
        grid=(D_MODEL // tr,),
        in_specs=[pl.BlockSpec((4, tr, q), lambda i: (0, i, 0))],
        out_specs=pl.BlockSpec((tr, NP), lambda i: (i, 0)),
        out_shape=jax.ShapeDtypeStruct((D_MODEL, NP), g.dtype),
        compiler_params=pltpu.CompilerParams(dimension_semantics=("arbitrary",), vmem_limit_bytes=VMEM_LIMIT),
        name="relayout_w_in",
    )(g)


def _unlayout_dw_in(dg, ds, dr, dsm):
    tr = 128
    q = N_IN // 4

    def body(g_ref, s_ref, r_ref, sm_ref, o_ref, ob_ref):
        w = jnp.concatenate([g_ref[...], sm_ref[:, 0:8], s_ref[...], sm_ref[:, 128:144], r_ref[...]], axis=1)
        for j in range(4):
            blk = w[:, q * j:q * (j + 1)]
            o_ref[j] = blk
            ob_ref[j] = blk.astype(BF16)

    row = lambda i: (i, 0)
    return pl.pallas_call(
        body,
        grid=(D_MODEL // tr,),
        in_specs=[pl.BlockSpec((tr, d.shape[1]), row) for d in (dg, ds, dr, dsm)],
        out_specs=[pl.BlockSpec((4, tr, q), lambda i: (0, i, 0))] * 2,
        out_shape=[jax.ShapeDtypeStruct((4, D_MODEL, q), F32), jax.ShapeDtypeStruct((4, D_MODEL, q), BF16)],
        compiler_params=pltpu.CompilerParams(dimension_semantics=("arbitrary",), vmem_limit_bytes=VMEM_LIMIT),
        name="unlayout_dw_in",
    )(dg, ds, dr, dsm)


TB = 256
TB_RET = 512
TL = 1024
TL_IN = 512
TL_OB = 1024
TK = 2048


def kernel(x, pre_norm, post_norm, w_in, gdn_conv, gdn_A_log, gdn_dt_bias, gdn_norm, ssd_conv, ssd_conv_b, ssd_A_log, ssd_dt_bias, ssd_D, ssd_norm, ret_norm, w_out, loss_target, m_pre_norm, m_post_norm, m_w_in, m_gdn_conv, m_gdn_A_log, m_gdn_dt_bias, m_gdn_norm, m_ssd_conv, m_ssd_conv_b, m_ssd_A_log, m_ssd_dt_bias, m_ssd_D, m_ssd_norm, m_ret_norm, m_w_out, v_pre_norm, v_post_norm, v_w_in, v_gdn_conv, v_gdn_A_log, v_gdn_dt_bias, v_gdn_norm, v_ssd_conv, v_ssd_conv_b, v_ssd_A_log, v_ssd_dt_bias, v_ssd_D, v_ssd_norm, v_ret_norm, v_w_out):
    seq = x.shape[1]
    chip = 2 * lax.axis_index("x") + lax.axis_index("y")
    x0 = x[0]

    wi_b, wo_b = w_in.astype(BF16), w_out.astype(BF16)
    (wi0_g,) = _ag_rows([wi_b[0:1]], "ag_weights")
    gcv_g, scv_g = _ag_chips([gdn_conv, ssd_conv], "ag_conv")
    full_w_in = _relayout_w_in
    wp = [full_w_in(wi0_g[:, 0]), None]
    wo = [None, None]
    ag0 = _ChipExchange("gather", [wo_b[0], wo_b[1]])
    ag1 = _ChipExchange("gather", [wi_b[1]])
    gcv = jnp.transpose(gcv_g, (1, 2, 0, 3)).reshape(DEPTH, CONV_W, 1536)
    scv = jnp.transpose(scv_g, (1, 2, 0, 3)).reshape(DEPTH, CONV_W, 1536)
    rope_c, rope_s = _rope_tables(seq)

    saved = []
    xc = x0
    for l in range(DEPTH):
        p = dict(
            pn=_pad8(pre_norm[l], D_MODEL), qn=_pad8(post_norm[l], D_MODEL),
            g_cw=_pad8(gcv[l], 1536), g_prm=_pad8(jnp.stack([gdn_A_log[l], gdn_dt_bias[l]]), 128, 4),
            g_nw=_pad8(gdn_norm[l], 128),
            s_cw=_pad8(scv[l], 1536), s_cb=_pad8(ssd_conv_b[l], 1536),
            s_prm=_pad8(jnp.stack([ssd_A_log[l], ssd_dt_bias[l], ssd_D[l]]), 128), s_nw=_pad8(ssd_norm[l], SSD_W),
            r_nw=_pad8(ret_norm[l], 128))
        if l == 0:
            pg, ps, pr, gs, ss, ht, wo0_g, wo1_g = _make_inproj(seq, TL_IN)(
                xc, p["pn"], wp[l], comm=ag0, comm_args=(wo_b[0], wo_b[1]))
            wo = [wo0_g.reshape(2048, D_MODEL), wo1_g.reshape(2048, D_MODEL)]
        else:
            pg, ps, pr, gs, ss, ht = _make_inproj(seq, TL_IN)(xc, p["pn"], wp[l])
        if l == 0:
            oa, stg, tig, uwg, gpre, wi1_g = _make_gdn_fwd(seq, TB)(
                pg, gs, p["g_cw"], p["g_prm"], p["g_nw"], comm=ag1, comm_args=(wi_b[1],))
            wp[1] = full_w_in(wi1_g)
        else:
            oa, stg, tig, uwg, gpre = _make_gdn_fwd(seq, TB)(pg, gs, p["g_cw"], p["g_prm"], p["g_nw"])
        ob, sts, spre, sy = _make_ssd_fwd(seq, TB)(ps, ss, p["s_cw"], p["s_cb"], p["s_prm"], p["s_nw"])
        oc, str_ = _make_ret_fwd(seq, TB_RET)(pr, rope_c, rope_s, p["r_nw"])
        if l == DEPTH - 1:
            out, dxn, lossp = _make_outproj_loss(seq, TL)(oa, ob, oc, wo[l], xc, p["qn"], loss_target[0])
            xn = None
        else:
            out, xn = _make_outproj(seq, TL)(oa, ob, oc, wo[l], xc, p["qn"])
        saved.append(dict(p=p, x=xc, ht=ht, spre=spre, sy=sy, gpre=gpre, pg=pg, ps=ps, pr=pr, gs=gs, ss=ss, stg=stg, tig=tig, uwg=uwg, sts=sts, str=str_,
                          oa=oa, ob=ob, oc=oc, out=out))
        xc = xn

    small = [None] * DEPTH
    gin, gin_b, gout, q_in, q_out = ([None] * DEPTH for _ in range(5))

    for l in reversed(range(DEPTH)):
        s = saved[l]
        p = s["p"]
        doa, dob, doc, dqn, dwo_l = _make_outproj_bwd(seq, TL_OB)(dxn, s["out"], s["oa"], s["ob"], s["oc"], wo[l], p["qn"])
        gout[l] = dwo_l.reshape(4, 512, D_MODEL)
        gdn_args = (s["pg"], s["gpre"], s["gs"], p["g_cw"], p["g_prm"], p["g_nw"], s["stg"], s["tig"], s["uwg"], doa)
        if l == 0:
            payload = (gout[0].astype(BF16),)
            dpg, dgs, dcw_g, dprm_g, dnw_g, q_out[0] = _make_gdn_bwd(seq, TB)(
                *gdn_args, comm=_ChipExchange("scatter", payload), comm_args=payload)
        else:
            dpg, dgs, dcw_g, dprm_g, dnw_g = _make_gdn_bwd(seq, TB)(*gdn_args)
        ssd_args = (s["ps"], s["spre"], s["sy"], s["ss"], p["s_cw"], p["s_cb"], p["s_prm"], p["s_nw"], s["sts"], dob)
        if l == 0:
            payload = (gin_b[1], gout[1].astype(BF16))
            dps, dss, dcw_s, dcb_s, dprm_s, dnw_s, q_in[1], q_out[1] = _make_ssd_bwd(seq, TB)(
                *ssd_args, comm=_ChipExchange("scatter", payload), comm_args=payload)
        else:
            dps, dss, dcw_s, dcb_s, dprm_s, dnw_s = _make_ssd_bwd(seq, TB)(*ssd_args)
        dpr, dnw_r = _make_ret_bwd(seq, TB_RET)(s["pr"], rope_c, rope_s, p["r_nw"], s["str"], doc)
        dws = [_make_inproj_bwd_dw(seq, TK, d.shape[1], tn, f"inproj_bwd_dw{i}")(s["ht"], d)
               for i, (d, tn) in enumerate(((dpg, 2048), (dps, 1280), (dpr, 2048),
                                            (jnp.concatenate([dgs, dss], axis=1), 256)))]
        gin[l], gin_b[l] = _unlayout_dw_in(*dws)
        dx_args = (dpg, dps, dpr, dgs, dss, wp[l], s["x"], p["pn"], dxn)
        if l == 0:
            payload = (gin_b[0],)
            dx, dpn, q_in[0] = _make_inproj_bwd_dx(seq, TL_IN)(
                *dx_args, comm=_ChipExchange("scatter", payload), comm_args=payload)
        else:
            dx, dpn = _make_inproj_bwd_dx(seq, TL_IN)(*dx_args)
        small[l] = [dpn[0], dqn[0], dcw_g[0:4].reshape(-1), dprm_g[0, 4:8], dprm_g[1, 4:8], dnw_g[0],
                    dcw_s[0:4].reshape(-1), dcb_s[0], dprm_s[0, 0:16], dprm_s[1, 0:16], dprm_s[2, 0:16],
                    dnw_s[0], dnw_r[0]]
        dxn = dx
    grad_x = dxn[None]

    sizes = [a.shape[0] for a in small[0]]
    flat = jnp.concatenate(small[0] + small[1] + [lossp[0, 0:1]])
    n_flat = flat.shape[0]
    rows = -(-n_flat // 1024) * 8
    red = _allreduce_small(jnp.pad(flat, (0, rows * 128 - n_flat)).reshape(rows, 128), "allreduce_small").reshape(-1)
    per = sum(sizes)
    loss = red[2 * per]

    def pick(i):
        off = sum(sizes[:i])
        return jnp.stack([red[l * per + off:l * per + off + sizes[i]] for l in range(DEPTH)])

    g_small = dict(
        pre_norm=pick(0), post_norm=pick(1),
        gdn_conv=lax.dynamic_slice_in_dim(pick(2).reshape(DEPTH, CONV_W, 1536), chip * 384, 384, axis=2),
        gdn_A_log=pick(3), gdn_dt_bias=pick(4), gdn_norm=pick(5),
        ssd_conv=lax.dynamic_slice_in_dim(pick(6).reshape(DEPTH, CONV_W, 1536), chip * 384, 384, axis=2),
        ssd_conv_b=pick(7), ssd_A_log=pick(8), ssd_dt_bias=pick(9), ssd_D=pick(10), ssd_norm=pick(11),
        ret_norm=pick(12))

    chip1 = chip.astype(jnp.int32).reshape(1)
    s_in = [_sum_chips(gin[l][:, None], q_in[l][:, None], chip1, f"sum_chips_w_in{l}") for l in range(DEPTH)]
    s_out = [_sum_chips(gout[l][:, None], q_out[l][:, None], chip1, f"sum_chips_w_out{l}") for l in range(DEPTH)]
    t_all = _swap_sibling(s_in + s_out, "swap_grads")
    t_in, t_out = t_all[:DEPTH], t_all[DEPTH:]

    weights = dict(pre_norm=pre_norm, post_norm=post_norm, w_in=w_in, gdn_conv=gdn_conv, gdn_A_log=gdn_A_log,
                   gdn_dt_bias=gdn_dt_bias, gdn_norm=gdn_norm, ssd_conv=ssd_conv, ssd_conv_b=ssd_conv_b,
                   ssd_A_log=ssd_A_log, ssd_dt_bias=ssd_dt_bias, ssd_D=ssd_D, ssd_norm=ssd_norm, ret_norm=ret_norm,
                   w_out=w_out)
    ms = dict(pre_norm=m_pre_norm, post_norm=m_post_norm, w_in=m_w_in, gdn_conv=m_gdn_conv, gdn_A_log=m_gdn_A_log,
              gdn_dt_bias=m_gdn_dt_bias, gdn_norm=m_gdn_norm, ssd_conv=m_ssd_conv, ssd_conv_b=m_ssd_conv_b,
              ssd_A_log=m_ssd_A_log, ssd_dt_bias=m_ssd_dt_bias, ssd_D=m_ssd_D, ssd_norm=m_ssd_norm,
              ret_norm=m_ret_norm, w_out=m_w_out)
    vs = dict(pre_norm=v_pre_norm, post_norm=v_post_norm, w_in=v_w_in, gdn_conv=v_gdn_conv, gdn_A_log=v_gdn_A_log,
              gdn_dt_bias=v_gdn_dt_bias, gdn_norm=v_gdn_norm, ssd_conv=v_ssd_conv, ssd_conv_b=v_ssd_conv_b,
              ssd_A_log=v_ssd_A_log, ssd_dt_bias=v_ssd_dt_bias, ssd_D=v_ssd_D, ssd_norm=v_ssd_norm,
              ret_norm=v_ret_norm, w_out=v_w_out)
    names = list(weights)
    res = {}
    for nme in names:
        if nme == "w_in":
            res[nme] = _adamw_pairs(w_in, s_in, t_in, m_w_in, v_w_in, "adamw_w_in")
        elif nme == "w_out":
            res[nme] = _adamw_pairs(w_out, s_out, t_out, m_w_out, v_w_out, "adamw_w_out")
        else:
            res[nme] = _adamw(weights[nme], g_small[nme], ms[nme], vs[nme], "adamw_" + nme)
    return (loss, grad_x, *[res[n][0] for n in names], *[res[n][1] for n in names],
            *[res[n][2] for n in names], *[res[n][3] for n in names])
```

```python
import math

import jax
import jax.numpy as jnp
from jax import lax
from jax.experimental import pallas as pl
from jax.experimental.pallas import tpu as pltpu

F32 = jnp.float32
BF16 = jnp.bfloat16

D_MODEL = 1024
DEPTH = 2
CH = 64
CONV_W = 4
EPS = 1e-6
GDN_H, GDN_D = 4, 128
SSD_H, SSD_P, SSD_N, SSD_G = 16, 64, 128, 2
SSD_W = SSD_H * SSD_P
RET_H, RET_D = 4, 128
ROPE_BASE = 10000.0
N_IN = 6680
NEG = -1e30

V7X_VMEM_BYTES = 64 * 1024 * 1024
VMEM_LIMIT = V7X_VMEM_BYTES * 7 // 8


def _dot(a, b):
    return jnp.dot(a.astype(BF16), b.astype(BF16), preferred_element_type=F32)


def _dot_nt(a, b):
    return lax.dot_general(a.astype(BF16), b.astype(BF16), (((1,), (1,)), ((), ())), preferred_element_type=F32)


def _dot_tn(a, b):
    return lax.dot_general(a.astype(BF16), b.astype(BF16), (((0,), (0,)), ((), ())), preferred_element_type=F32)


def _split(a):
    hi = a.astype(BF16)
    return hi, (a - hi.astype(F32)).astype(BF16)


def _dot01l(m, v):
    vh, vl = _split(v)
    mb = m.astype(BF16)
    return jnp.dot(mb, vh, preferred_element_type=F32) + jnp.dot(mb, vl, preferred_element_type=F32)


def _dot01r(v, m):
    vh, vl = _split(v)
    mb = m.astype(BF16)
    return jnp.dot(vh, mb, preferred_element_type=F32) + jnp.dot(vl, mb, preferred_element_type=F32)


def _sigmoid(x):
    return jax.nn.sigmoid(x)


def _silu(x):
    return x * _sigmoid(x)


def _dsilu(x):
    s = _sigmoid(x)
    return s * (1.0 + x * (1.0 - s))


def _softplus(x):
    return jnp.maximum(x, 0.0) + jnp.log1p(jnp.exp(-jnp.abs(x)))


def _iota2(shape, dim):
    return lax.broadcasted_iota(jnp.int32, shape, dim)


def _chunk_tri(tb, upper=False):
    r = _iota2((tb, tb), 0)
    c = _iota2((tb, tb), 1)
    same = jnp.right_shift(r, 6) == jnp.right_shift(c, 6)
    return (same & ((c >= r) if upper else (c <= r))).astype(F32)


def _masks():
    r = _iota2((CH, CH), 0)
    c = _iota2((CH, CH), 1)
    return r >= c, r > c, (r == c).astype(F32)


def _put_lane(col, lane_idx, width=128):
    lane = _iota2((col.shape[0], width), 1)
    return jnp.where(lane == lane_idx, col, 0.0)


def _conv_taps(raw, halo8, tb):
    ext = jnp.concatenate([halo8, raw], axis=0)
    return [raw] + [pltpu.roll(ext, s, axis=0)[8:] for s in (1, 2, 3)]


def _conv_back(dpre, nxt8, tb):
    ext = jnp.concatenate([dpre, nxt8], axis=0)
    return [dpre] + [pltpu.roll(ext, tb + 8 - s, axis=0)[:tb] for s in (1, 2, 3)]


def _rms_fwd(o, w, n):
    r = lax.rsqrt(jnp.sum(o * o, axis=-1, keepdims=True) * (1.0 / n) + EPS)
    on = o * r
    return on, r, on * w


def _rms_bwd(dy, on, r, w, n):
    don = dy * w
    return r * (don - on * (jnp.sum(don * on, axis=-1, keepdims=True) * (1.0 / n))), dy * on


def _put_cols(v, g, gw):
    z = jnp.zeros_like(v)
    return jnp.concatenate([v, z] if g == 0 else [z, v], axis=1)


def _gdn_common(pg_ref, halo8, sm, cw, prm, tb, pre=None):
    raw = pg_ref[:, 0:1536]
    if pre is None:
        taps = _conv_taps(raw, halo8, tb)
        pre = taps[0] * cw[3:4, :] + taps[1] * cw[2:3, :] + taps[2] * cw[1:2, :] + taps[3] * cw[0:1, :]
    act = _silu(pre)
    beta = _sigmoid(sm)
    sp_in = sm + prm[1:2, :]
    g = -jnp.exp(prm[0:1, :]) * _softplus(sp_in)
    gc = _dot01l(_chunk_tri(tb), g)
    return raw, pre, act, beta, sp_in, g, gc


_NN = (((2,), (1,)), ((0,), (0,)))
_NT = (((2,), (2,)), ((0,), (0,)))
_TN = (((1,), (1,)), ((0,), (0,)))


def _bdot(a, b, dn):
    return lax.dot_general(a.astype(BF16), b.astype(BF16), dn, preferred_element_type=F32)


def _binv_unit_lower(a, eye):
    r = _iota2((CH, CH), 0)
    c = _iota2((CH, CH), 1)
    d = eye - jnp.where((jnp.right_shift(r, 1) == jnp.right_shift(c, 1)), a, 0.0)
    ab = a.astype(BF16)
    zero = jnp.zeros((), BF16)
    for lb in range(1, 6):
        same = jnp.right_shift(r, lb + 1) == jnp.right_shift(c, lb + 1)
        low = (jnp.bitwise_and(jnp.right_shift(r, lb), 1) == 1) & (jnp.bitwise_and(jnp.right_shift(c, lb), 1) == 0)
        db = d.astype(BF16)
        t = _bdot(jnp.where(same & low, ab, zero), db, _NN)
        d = d - _bdot(db, t, _NN)
    return d


def _rsum(v):
    return jnp.sum(v, axis=-1, keepdims=True)


def _gdn_batch(act, beta, gc, gct, eg_all, ncb, masks):
    causal, strict, _ = masks

    def st(fn):
        return jnp.stack([fn(c, h, slice(c * CH, (c + 1) * CH)) for c in range(ncb) for h in range(GDN_H)])

    qr = st(lambda c, h, r: act[r, h * 128:(h + 1) * 128])
    kr = st(lambda c, h, r: act[r, 512 + h * 128:512 + (h + 1) * 128])
    vh = st(lambda c, h, r: act[r, 1024 + h * 128:1024 + (h + 1) * 128])
    bh = st(lambda c, h, r: beta[r, h:h + 1])
    gcol = st(lambda c, h, r: gc[r, 4 + h:5 + h])
    grow = st(lambda c, h, r: gct[4 + h:5 + h, r])
    eg = st(lambda c, h, r: eg_all[r, 4 + h:5 + h])
    glast = st(lambda c, h, r: gc[(c + 1) * CH - 1:(c + 1) * CH, 4 + h:5 + h])
    rq = lax.rsqrt(_rsum(qr * qr) + EPS)
    rk = lax.rsqrt(_rsum(kr * kr) + EPS)
    qn = qr * rq
    kh = kr * rk
    qh = qn * (GDN_D ** -0.5)
    decay = jnp.exp(jnp.where(causal, gcol - grow, NEG))
    kb = kh * bh
    kd_scale = jnp.exp(glast - gcol)
    return dict(qn=qn, rq=rq, kh=kh, rk=rk, qh=qh, vh=vh, bh=bh, eg=eg, decay=decay, kb=kb, vb=vh * bh, kg=kb * eg,
                qg=qh * eg, kd_scale=kd_scale, kdec=kh * kd_scale, egl=jnp.exp(glast),
                a=jnp.where(strict, _bdot(kb, kh, _NT) * decay, 0.0), attn=_bdot(qh, kh, _NT) * decay)


def _make_gdn_fwd(seq, tb):
    ncb = tb // CH
    nb = seq // tb
    n = ncb * GDN_H

    def body(pg_ref, sm_ref, cw_ref, prm_ref, nw_ref, oa_ref, st_ref, ti_ref, uw_ref, pre_ref, s_scr, halo_scr):
        @pl.when(pl.program_id(0) == 0)
        def _():
            s_scr[...] = jnp.zeros_like(s_scr)
            halo_scr[...] = jnp.zeros_like(halo_scr)

        masks = _masks()
        sm = sm_ref[...]
        raw, pre, act, beta, _, _, gc = _gdn_common(pg_ref, halo_scr[...], sm, cw_ref[...], prm_ref[...], tb)
        halo_scr[...] = raw[tb - 8:tb, :]
        pre_ref[...] = pre
        d = _gdn_batch(act, beta, gc, gc.T, jnp.exp(gc), ncb, masks)
        t = _binv_unit_lower(d["a"], masks[2])
        sol = _bdot(t, jnp.concatenate([d["vb"], d["kg"]], axis=2), _NN)
        ti_ref[...] = t.reshape(ncb, GDN_H, CH, CH)
        uw_ref[...] = sol.reshape(ncb, GDN_H, CH, 256)
        u, w = sol[:, :, :128], sol[:, :, 128:]
        vns = []
        for c in range(ncb):
            bs = slice(c * GDN_H, (c + 1) * GDN_H)
            s = s_scr[...]
            st_ref[c] = s
            vn = u[bs] - _bdot(w[bs], s, _NN)
            s_scr[...] = s * d["egl"][bs] + _bdot(d["kdec"][bs], vn, _TN)
            vns.append(vn)
        v_new = jnp.concatenate(vns, axis=0)
        s_prev = st_ref[...].reshape(n, 128, 128)
        o = _bdot(d["qg"], s_prev, _NN) + _bdot(d["attn"], v_new, _NN)
        _, _, y = _rms_fwd(o, nw_ref[0:1, :], GDN_D)
        for c in range(ncb):
            rows = slice(c * CH, (c + 1) * CH)
            for h in range(GDN_H):
                z = pg_ref[rows, 1536 + h * 128:1536 + (h + 1) * 128]
                oa_ref[rows, h * 128:(h + 1) * 128] = (y[c * GDN_H + h] * _silu(z)).astype(oa_ref.dtype)

    def call(pg, sm, cw, prm, nw, comm=None, comm_args=()):
        blk4 = lambda i: (i, 0, 0, 0)
        cx = _exchange_specs(comm)
        return pl.pallas_call(
            _with_exchange(body, comm, 5, 5, nb),
            grid=(nb,),
            in_specs=[
                pl.BlockSpec((tb, 2048), lambda i: (i, 0)),
                pl.BlockSpec((tb, 128), lambda i: (i, 0)),
                pl.BlockSpec((8, 1536), lambda i: (0, 0)),
                pl.BlockSpec((8, 128), lambda i: (0, 0)),
                pl.BlockSpec((8, 128), lambda i: (0, 0)),
            ] + cx["specs"],
            out_specs=[
                pl.BlockSpec((tb, 512), lambda i: (i, 0)),
                pl.BlockSpec((ncb, GDN_H, 128, 128), blk4),
                pl.BlockSpec((ncb, GDN_H, CH, CH), blk4),
                pl.BlockSpec((ncb, GDN_H, CH, 256), blk4),
                pl.BlockSpec((tb, 1536), lambda i: (i, 0)),
            ] + cx["specs"],
            out_shape=[
                jax.ShapeDtypeStruct((seq, 512), BF16),
                jax.ShapeDtypeStruct((seq // CH, GDN_H, 128, 128), F32),
                jax.ShapeDtypeStruct((seq // CH, GDN_H, CH, CH), F32),
                jax.ShapeDtypeStruct((seq // CH, GDN_H, CH, 256), F32),
                jax.ShapeDtypeStruct((seq, 1536), F32),
            ] + cx["out_shape"],
            scratch_shapes=[pltpu.VMEM((GDN_H, 128, 128), F32), pltpu.VMEM((8, 1536), F32)] + cx["scratch"],
            compiler_params=pltpu.CompilerParams(dimension_semantics=("arbitrary",), vmem_limit_bytes=VMEM_LIMIT,
                                                 has_side_effects=comm is not None),
            name="gdn_fwd" + cx["tag"],
        )(pg, sm, cw, prm, nw, *comm_args)

    return call


def _make_gdn_bwd(seq, tb):
    ncb = tb // CH
    nb = seq // tb
    hb = tb // 8
    n = ncb * GDN_H

    def body(pg_ref, pre_ref, sm_ref, cw_ref, prm_ref, nw_ref, st_ref, ti_ref, uw_ref, doa_ref,
             dpg_ref, dsm_ref, dcw_ref, dprm_ref, dnw_ref, ds_scr, nxt_scr):
        i = pl.program_id(0)

        @pl.when(i == 0)
        def _():
            ds_scr[...] = jnp.zeros_like(ds_scr)
            nxt_scr[...] = jnp.zeros_like(nxt_scr)
            dcw_ref[...] = jnp.zeros_like(dcw_ref)
            dprm_ref[...] = jnp.zeros_like(dprm_ref)
            dnw_ref[...] = jnp.zeros_like(dnw_ref)

        masks = _masks()
        strict = masks[1]
        sm = sm_ref[...]
        cw = cw_ref[...]
        prm = prm_ref[...]
        raw, pre, act, beta, sp_in, g, gc = _gdn_common(pg_ref, None, sm, cw, prm, tb, pre=pre_ref[...])
        nw = nw_ref[0:1, :]
        row_id = _iota2((CH, 1), 0)
        d = _gdn_batch(act, beta, gc, gc.T, jnp.exp(gc), ncb, masks)
        t = ti_ref[...].reshape(n, CH, CH)
        sol = uw_ref[...].reshape(n, CH, 256)
        u, w = sol[:, :, :128], sol[:, :, 128:]
        s_prev = st_ref[...].reshape(n, 128, 128)
        v_new = u - _bdot(w, s_prev, _NN)
        o = _bdot(d["qg"], s_prev, _NN) + _bdot(d["attn"], v_new, _NN)

        pairs = [(c, h) for c in range(ncb) for h in range(GDN_H)]
        z = jnp.stack([pg_ref[c * CH:(c + 1) * CH, 1536 + h * 128:1536 + (h + 1) * 128] for c, h in pairs])
        doa = jnp.stack([doa_ref[c * CH:(c + 1) * CH, h * 128:(h + 1) * 128] for c, h in pairs])
        on, r, y = _rms_fwd(o, nw, GDN_D)
        dz = doa * y * _dsilu(z)
        do, dnw_rows = _rms_bwd(doa * _silu(z), on, r, nw, GDN_D)
        dnw_acc = jnp.sum(jnp.sum(dnw_rows, axis=0), axis=0, keepdims=True)

        dvn_in = _bdot(d["attn"], do, _TN)
        qgtdo = _bdot(d["qg"], do, _TN)
        dvn_l, dkdec_l, dgl_l = [None] * ncb, [None] * ncb, [None] * ncb
        for c in reversed(range(ncb)):
            bs = slice(c * GDN_H, (c + 1) * GDN_H)
            dsn = ds_scr[...]
            dvn_c = dvn_in[bs] + _bdot(d["kdec"][bs], dsn, _NN)
            ds_scr[...] = d["egl"][bs] * dsn + qgtdo[bs] - _bdot(w[bs], dvn_c, _TN)
            dvn_l[c] = dvn_c
            dkdec_l[c] = _bdot(v_new[bs], dsn, _NT)
            dgl_l[c] = d["egl"][bs] * jnp.sum(_rsum(s_prev[bs] * dsn), axis=1, keepdims=True)
        dvn = jnp.concatenate(dvn_l, axis=0)
        dkdec = jnp.concatenate(dkdec_l, axis=0)
        dglast = jnp.concatenate(dgl_l, axis=0)

        dqg = _bdot(do, s_prev, _NT)
        dattn = _bdot(do, v_new, _NT)
        dw = -_bdot(dvn, s_prev, _NT)
        drhs = _bdot(t, jnp.concatenate([dvn, dw], axis=2), _TN)
        dvb, dkg = drhs[:, :, :128], drhs[:, :, 128:]
        da = jnp.where(strict, -(_bdot(dvb, u, _NT) + _bdot(dkg, w, _NT)), 0.0)
        dp = da * d["decay"]
        dq_m = dattn * d["decay"]
        m = da * d["a"] + dattn * d["attn"]
        upper_tri = jnp.broadcast_to((_iota2((CH, CH), 1) >= _iota2((CH, CH), 0)).astype(BF16), (n, CH, CH))
        dg_in = _rsum(jnp.where(strict, _bdot(upper_tri, m, _NN), 0.0))
        dkb = _bdot(dp, d["kh"], _NN) + dkg * d["eg"]
        kdk_row = _rsum(dkdec * d["kdec"])
        dk = _bdot(dp, d["kb"], _TN) + _bdot(dq_m, d["qh"], _TN) + dkdec * d["kd_scale"] + dkb * d["bh"]
        dq = _bdot(dq_m, d["kh"], _NN) + dqg * d["eg"]
        dglast = dglast + jnp.sum(kdk_row, axis=1, keepdims=True)
        dgcol = (_rsum(dqg * d["qg"]) + _rsum(dkg * d["kg"]) - kdk_row + jnp.where(row_id == CH - 1, dglast, 0.0))
        dbeta = _rsum(dkb * d["kh"]) + _rsum(dvb * d["vh"])
        dn = dq * (GDN_D ** -0.5)
        dact_q = d["rq"] * (dn - d["qn"] * _rsum(dn * d["qn"]))
        dact_k = d["rk"] * (dk - d["kh"] * _rsum(dk * d["kh"]))
        dact_v = dvb * d["bh"]

        def lanes(v, lane0):
            return jnp.concatenate(
                [sum(_put_lane(v[c * GDN_H + h], lane0 + h) for h in range(GDN_H)) for c in range(ncb)], axis=0)

        def tokens(v):
            return jnp.concatenate(
                [jnp.concatenate([v[c * GDN_H + h] for h in range(GDN_H)], axis=1) for c in range(ncb)], axis=0)

        dbeta_all = lanes(dbeta, 0)
        dg = _dot01l(_chunk_tri(tb, upper=True), lanes(dgcol, 4)) + lanes(dg_in, 4)
        neg_ea = -jnp.exp(prm[0:1, :])
        da_raw = dg * neg_ea * _sigmoid(sp_in)
        db_raw = dbeta_all * beta * (1.0 - beta)
        dsm_ref[...] = (da_raw + db_raw).astype(dsm_ref.dtype)
        lane8 = _iota2((8, 128), 1)
        sub8 = _iota2((8, 128), 0)
        dalog = jnp.sum(dg * g, axis=0, keepdims=True)
        ddtb = jnp.sum(da_raw, axis=0, keepdims=True)
        dprm_ref[...] += jnp.where(sub8 == 0, dalog, 0.0) + jnp.where(sub8 == 1, ddtb, 0.0)
        dnw_ref[...] += jnp.where(sub8 == 0, dnw_acc, 0.0)

        dact = jnp.concatenate([tokens(dact_q), tokens(dact_k), tokens(dact_v)], axis=1)
        dpre = dact * _dsilu(pre)
        back = _conv_back(dpre, nxt_scr[...], tb)
        nxt_scr[...] = dpre[0:8, :]
        draw = back[0] * cw[3:4, :] + back[1] * cw[2:3, :] + back[2] * cw[1:2, :] + back[3] * cw[0:1, :]
        dpg_ref[:, 0:1536] = draw.astype(dpg_ref.dtype)
        dpg_ref[:, 1536:2048] = tokens(dz).astype(dpg_ref.dtype)
        sub_c = _iota2((8, 1536), 0)
        dcw_new = jnp.zeros((8, 1536), F32)
        for s_ in range(CONV_W):
            dcw_new = dcw_new + jnp.where(sub_c == 3 - s_, jnp.sum(back[s_] * raw, axis=0, keepdims=True), 0.0)
        dcw_ref[...] += dcw_new

    def call(pg, pre, sm, cw, prm, nw, st, ti, uw, doa, comm=None, comm_args=()):
        rev = lambda i: (nb - 1 - i, 0)
        const = lambda i: (0, 0)
        cx = _exchange_specs(comm)
        return pl.pallas_call(
            _with_exchange(body, comm, 10, 5, nb),
            grid=(nb,),
            in_specs=[
                pl.BlockSpec((tb, 2048), rev),
                pl.BlockSpec((tb, 1536), rev),
                pl.BlockSpec((tb, 128), rev),
                pl.BlockSpec((8, 1536), const),
                pl.BlockSpec((8, 128), const),
                pl.BlockSpec((8, 128), const),
                pl.BlockSpec((ncb, GDN_H, 128, 128), lambda i: (nb - 1 - i, 0, 0, 0)),
                pl.BlockSpec((ncb, GDN_H, CH, CH), lambda i: (nb - 1 - i, 0, 0, 0)),
                pl.BlockSpec((ncb, GDN_H, CH, 256), lambda i: (nb - 1 - i, 0, 0, 0)),
                pl.BlockSpec((tb, 512), rev),
            ] + cx["specs"],
            out_specs=[
                pl.BlockSpec((tb, 2048), rev),
                pl.BlockSpec((tb, 128), rev),
                pl.BlockSpec((8, 1536), const),
                pl.BlockSpec((8, 128), const),
                pl.BlockSpec((8, 128), const),
            ] + cx["specs"],
            out_shape=[
                jax.ShapeDtypeStruct((seq, 2048), BF16),
                jax.ShapeDtypeStruct((seq, 128), BF16),
                jax.ShapeDtypeStruct((8, 1536), F32),
                jax.ShapeDtypeStruct((8, 128), F32),
                jax.ShapeDtypeStruct((8, 128), F32),
            ] + cx["out_shape"],
            scratch_shapes=[pltpu.VMEM((GDN_H, 128, 128), F32), pltpu.VMEM((8, 1536), F32)] + cx["scratch"],
            compiler_params=pltpu.CompilerParams(dimension_semantics=("arbitrary",), vmem_limit_bytes=VMEM_LIMIT,
                                                 has_side_effects=comm is not None),
            name="gdn_bwd" + cx["tag"],
        )(pg, pre, sm, cw, prm, nw, st, ti, uw, doa, *comm_args)

    return call


def _expand_mat():
    r = _iota2((128, SSD_W), 0)
    c = _iota2((128, SSD_W), 1)
    return (jnp.right_shift(c, 6) == r).astype(F32)


def _reduce_heads(v, e):
    vh, vl = _split(v)
    eb = e.astype(BF16)
    nt = (((1,), (1,)), ((), ()))
    return (lax.dot_general(vh, eb, nt, preferred_element_type=F32)
            + lax.dot_general(vl, eb, nt, preferred_element_type=F32))


def _reduce_heads1(v, e):
    nt = (((1,), (1,)), ((), ()))
    return lax.dot_general(v.astype(BF16), e.astype(BF16), nt, preferred_element_type=F32)


def _row8(v):
    return jnp.broadcast_to(v, (8, v.shape[1]))


def _ssd_common(ps_ref, halo8, ss, cw, cb, prm, tb, pre=None):
    raw = ps_ref[:, 0:1536]
    taps = None
    if pre is None:
        taps = _conv_taps(raw, halo8, tb)
        pre = taps[0] * cw[3:4, :] + taps[1] * cw[2:3, :] + taps[2] * cw[1:2, :] + taps[3] * cw[0:1, :] + cb[0:1, :]
    act = _silu(pre)
    dt_in = ss + prm[1:2, :]
    dt = _softplus(dt_in)
    a = dt * (-jnp.exp(prm[0:1, :]))
    acum = _dot01l(_chunk_tri(tb), a)
    e = _expand_mat()
    dt_e = _dot01r(dt, e)
    xdt = act[:, 0:SSD_W] * dt_e
    ea_e = _dot01r(jnp.exp(acum), e)
    d_e = _dot01r(_row8(prm[2:3, :]), e)[0:1, :]
    return raw, taps, pre, act, dt_in, dt, a, acum, e, dt_e, xdt, ea_e, d_e


def _ssd_chunk(act, acum, act_t, e, c):
    r0 = c * CH
    rows = slice(r0, r0 + CH)
    alast = acum[r0 + CH - 1:r0 + CH, :]
    wdec = jnp.exp(alast - acum[rows, :])
    wd_e = _dot01r(wdec, e)
    eal_e = _dot01r(_row8(jnp.exp(alast)), e)[0:1, :]
    return rows, wd_e, eal_e


def _ssd_lmat(acum, act_t, c, h, causal):
    r0 = c * CH
    acol = acum[r0:r0 + CH, h:h + 1]
    arow = act_t[h:h + 1, r0:r0 + CH]
    return jnp.exp(jnp.where(causal, acol - arow, NEG))


def _make_ssd_fwd(seq, tb):
    ncb = tb // CH
    nb = seq // tb
    hg = SSD_H // SSD_G
    gw = SSD_W // SSD_G

    def body(ps_ref, ss_ref, cw_ref, cb_ref, prm_ref, nw_ref, ob_ref, st_ref, pre_ref, y_ref, hs_scr, halo_scr):
        @pl.when(pl.program_id(0) == 0)
        def _():
            hs_scr[...] = jnp.zeros_like(hs_scr)
            halo_scr[...] = jnp.zeros_like(halo_scr)

        causal, _, _ = _masks()
        (raw, _, pre, act, _, _, _, acum, e, _, xdt, ea_e, d_e) = _ssd_common(
            ps_ref, halo_scr[...], ss_ref[...], cw_ref[...], cb_ref[...], prm_ref[...], tb)
        halo_scr[...] = raw[tb - 8:tb, :]
        pre_ref[...] = pre
        act_t = acum.T
        nw = nw_ref[0:1, :]
        for c in range(ncb):
            rows, wd_e, eal_e = _ssd_chunk(act, acum, act_t, e, c)
            st_ref[c] = hs_scr[...]
            ys = []
            for g in range(SSD_G):
                gc_ = slice(g * gw, (g + 1) * gw)
                bg = act[rows, SSD_W + g * 128:SSD_W + (g + 1) * 128]
                cg = act[rows, SSD_W + 256 + g * 128:SSD_W + 256 + (g + 1) * 128]
                cbm = _dot_nt(cg, bg)
                hs = hs_scr[:, gc_]
                yin = _dot(cg, hs)
                yh = []
                for hh in range(hg):
                    h = g * hg + hh
                    lm = _ssd_lmat(acum, act_t, c, h, causal)
                    yh.append(_dot(cbm * lm, xdt[rows, h * SSD_P:(h + 1) * SSD_P]))
                ys.append(jnp.concatenate(yh, axis=1) + yin * ea_e[rows, gc_])
                hs_scr[:, gc_] = hs * eal_e[:, gc_] + _dot_tn(bg, xdt[rows, gc_] * wd_e[:, gc_])
            y = jnp.concatenate(ys, axis=1) + act[rows, 0:SSD_W] * d_e
            y_ref[rows, :] = y
            yz = y * _silu(ps_ref[rows, 1536:2560])
            outs = [_rms_fwd(yz[:, g * gw:(g + 1) * gw], nw[:, g * gw:(g + 1) * gw], gw)[2] for g in range(SSD_G)]
            ob_ref[rows, :] = jnp.concatenate(outs, axis=1).astype(ob_ref.dtype)

    def call(ps, ss, cw, cb, prm, nw):
        const = lambda i: (0, 0)
        return pl.pallas_call(
            body,
            grid=(nb,),
            in_specs=[
                pl.BlockSpec((tb, 2560), lambda i: (i, 0)),
                pl.BlockSpec((tb, 128), lambda i: (i, 0)),
                pl.BlockSpec((8, 1536), const),
                pl.BlockSpec((8, 1536), const),
                pl.BlockSpec((8, 128), const),
                pl.BlockSpec((8, SSD_W), const),
            ],
            out_specs=[
                pl.BlockSpec((tb, SSD_W), lambda i: (i, 0)),
                pl.BlockSpec((ncb, SSD_N, SSD_W), lambda i: (i, 0, 0)),
                pl.BlockSpec((tb, 1536), lambda i: (i, 0)),
                pl.BlockSpec((tb, SSD_W), lambda i: (i, 0)),
            ],
            out_shape=[
                jax.ShapeDtypeStruct((seq, SSD_W), BF16),
                jax.ShapeDtypeStruct((seq // CH, SSD_N, SSD_W), F32),
                jax.ShapeDtypeStruct((seq, 1536), F32),
                jax.ShapeDtypeStruct((seq, SSD_W), F32),
            ],
            scratch_shapes=[pltpu.VMEM((SSD_N, SSD_W), F32), pltpu.VMEM((8, 1536), F32)],
            compiler_params=pltpu.CompilerParams(dimension_semantics=("arbitrary",), vmem_limit_bytes=VMEM_LIMIT),
            name="ssd_fwd",
        )(ps, ss, cw, cb, prm, nw)

    return call


def _make_ssd_bwd(seq, tb):
    ncb = tb // CH
    nb = seq // tb
    hb = tb // 8
    hg = SSD_H // SSD_G
    gw = SSD_W // SSD_G

    def body(ps_ref, pre_ref, y_ref, ss_ref, cw_ref, cb_ref, prm_ref, nw_ref, st_ref, dob_ref,
             dps_ref, dss_ref, dcw_ref, dcb_ref, dprm_ref, dnw_ref, dhs_scr, nxt_scr):
        i = pl.program_id(0)

        @pl.when(i == 0)
        def _():
            dhs_scr[...] = jnp.zeros_like(dhs_scr)
            nxt_scr[...] = jnp.zeros_like(nxt_scr)
            dcw_ref[...] = jnp.zeros_like(dcw_ref)
            dcb_ref[...] = jnp.zeros_like(dcb_ref)
            dprm_ref[...] = jnp.zeros_like(dprm_ref)
            dnw_ref[...] = jnp.zeros_like(dnw_ref)

        causal, _, _ = _masks()
        cw = cw_ref[...]
        prm = prm_ref[...]
        (raw, _, pre, act, dt_in, dt, a, acum, e, dt_e, xdt, ea_e, d_e) = _ssd_common(
            ps_ref, None, ss_ref[...], cw, cb_ref[...], prm, tb, pre=pre_ref[...])
        act_t = acum.T
        nw = nw_ref[0:1, :]

        dx_l, db_l, dc_l, dz_l, ddt_l, da_l = ([None] * ncb for _ in range(6))
        upper_tri = (_iota2((CH, CH), 1) >= _iota2((CH, CH), 0)).astype(F32)
        tri_pair = jnp.concatenate([upper_tri, (_iota2((CH, CH), 1) < _iota2((CH, CH), 0)).astype(F32)], axis=1)
        below = jnp.bitwise_and(_iota2((CH, gw), 1), CH - 1) < _iota2((CH, gw), 0)
        dnw_acc = jnp.zeros((1, SSD_W), F32)
        dd_acc = jnp.zeros((1, SSD_W), F32)

        for c in reversed(range(ncb)):
            rows, wd_e, eal_e = _ssd_chunk(act, acum, act_t, e, c)
            xc = act[rows, 0:SSD_W]
            z = ps_ref[rows, 1536:2560]
            dob = dob_ref[rows, :]
            sz = _silu(z)
            dy_g, dz_g, dxdt_g, db_g, dc_g, da_g = [], [], [], [], [], []
            for g in range(SSD_G):
                gc_ = slice(g * gw, (g + 1) * gw)
                bg = act[rows, SSD_W + g * 128:SSD_W + (g + 1) * 128]
                cg = act[rows, SSD_W + 256 + g * 128:SSD_W + 256 + (g + 1) * 128]
                cbm = _dot_nt(cg, bg)
                hs = st_ref[c, :, gc_]
                yin = _dot(cg, hs)
                lmats = [_ssd_lmat(acum, act_t, c, g * hg + hh, causal) for hh in range(hg)]
                ea_g = ea_e[rows, gc_]
                y = y_ref[rows, gc_]
                yz = y * sz[:, gc_]
                on, r, _ = _rms_fwd(yz, nw[:, gc_], gw)
                dyz, dnw_rows = _rms_bwd(dob[:, gc_], on, r, nw[:, gc_], gw)
                dnw_acc = dnw_acc + _put_cols(jnp.sum(dnw_rows, axis=0, keepdims=True), g, gw)
                dy = dyz * sz[:, gc_]
                dz_g.append(dyz * y * _dsilu(z[:, gc_]))
                dd_acc = dd_acc + _put_cols(jnp.sum(dy * xc[:, gc_], axis=0, keepdims=True), g, gw)
                dhs_n = dhs_scr[:, gc_]
                dyin = dy * ea_g
                dcg = _dot_nt(dyin, hs)
                xw = xdt[rows, gc_] * wd_e[:, gc_]
                dbg = _dot_nt(xw, dhs_n)
                dxw = _dot(bg, dhs_n)
                dhs_scr[:, gc_] = dhs_n * eal_e[:, gc_] + _dot_tn(cg, dyin)
                dxi, ms, dcbm = [], [], jnp.zeros((CH, CH), F32)
                for hh in range(hg):
                    h = g * hg + hh
                    hc = slice(hh * SSD_P, (hh + 1) * SSD_P)
                    dyh = dy[:, hc]
                    lm = cbm * lmats[hh]
                    dxi.append(_dot_tn(lm, dyh))
                    dlm = _dot_nt(dyh, xdt[rows, h * SSD_P:(h + 1) * SSD_P])
                    ms.append(dlm * lm)
                    dcbm = dcbm + dlm * lmats[hh]
                dx_intra = jnp.concatenate(dxi, axis=1)
                ncat = _dot(upper_tri, jnp.concatenate(ms, axis=1))
                cum = _dot(tri_pair, jnp.concatenate([dy * yin * ea_g, dxw * xw], axis=0))
                da_g.append(jnp.where(below, ncat, 0.0) + cum
                            + jnp.sum(hs * dhs_n, axis=0, keepdims=True) * eal_e[:, gc_])
                dxdt_g.append(dx_intra + dxw * wd_e[:, gc_])
                dy_g.append(dy)
                db_g.append(dbg + _dot_tn(dcbm, cg))
                dc_g.append(dcg + _dot(dcbm, bg))
            dy = jnp.concatenate(dy_g, axis=1)
            dxdt = jnp.concatenate(dxdt_g, axis=1)
            dx_l[c] = dxdt * dt_e[rows, :] + dy * d_e
            db_l[c] = jnp.concatenate(db_g, axis=1)
            dc_l[c] = jnp.concatenate(dc_g, axis=1)
            dz_l[c] = jnp.concatenate(dz_g, axis=1)
            ddt_l[c] = _reduce_heads1(dxdt * xc, e)
            da_l[c] = _reduce_heads1(jnp.concatenate(da_g, axis=1), e)

        da = jnp.concatenate(da_l, axis=0)
        neg_ea = -jnp.exp(prm[0:1, :])
        ddt = jnp.concatenate(ddt_l, axis=0) + da * neg_ea
        ddt_in = ddt * _sigmoid(dt_in)
        dss_ref[...] = ddt_in.astype(dss_ref.dtype)
        sub8 = _iota2((8, 128), 0)
        dalog = jnp.sum(da * a, axis=0, keepdims=True)
        ddtb = jnp.sum(ddt_in, axis=0, keepdims=True)
        dd = _reduce_heads(_row8(dd_acc), e)[0:1, :]
        dprm_ref[...] += (jnp.where(sub8 == 0, dalog, 0.0) + jnp.where(sub8 == 1, ddtb, 0.0)
                          + jnp.where(sub8 == 2, dd, 0.0))
        dnw_ref[...] += jnp.where(_iota2((8, SSD_W), 0) == 0, dnw_acc, 0.0)

        dact = jnp.concatenate([jnp.concatenate(dx_l, axis=0), jnp.concatenate(db_l, axis=0),
                                jnp.concatenate(dc_l, axis=0)], axis=1)
        dpre = dact * _dsilu(pre)
        back = _conv_back(dpre, nxt_scr[...], tb)
        nxt_scr[...] = dpre[0:8, :]
        draw = back[0] * cw[3:4, :] + back[1] * cw[2:3, :] + back[2] * cw[1:2, :] + back[3] * cw[0:1, :]
        dps_ref[:, 0:1536] = draw.astype(dps_ref.dtype)
        dps_ref[:, 1536:2560] = jnp.concatenate(dz_l, axis=0).astype(dps_ref.dtype)
        sub_c = _iota2((8, 1536), 0)
        dcw_new = jnp.zeros((8, 1536), F32)
        for s_ in range(CONV_W):
            dcw_new = dcw_new + jnp.where(sub_c == 3 - s_, jnp.sum(back[s_] * raw, axis=0, keepdims=True), 0.0)
        dcw_ref[...] += dcw_new
        dcb_ref[...] += jnp.where(sub_c == 0, jnp.sum(dpre, axis=0, keepdims=True), 0.0)

    def call(ps, pre, y, ss, cw, cb, prm, nw, st, dob, comm=None, comm_args=()):
        rev = lambda i: (nb - 1 - i, 0)
        const = lambda i: (0, 0)
        cx = _exchange_specs(comm)
        return pl.pallas_call(
            _with_exchange(body, comm, 10, 6, nb),
            grid=(nb,),
            in_specs=[
                pl.BlockSpec((tb, 2560), rev),
                pl.BlockSpec((tb, 1536), rev),
                pl.BlockSpec((tb, SSD_W), rev),
                pl.BlockSpec((tb, 128), rev),
                pl.BlockSpec((8, 1536), const),
                pl.BlockSpec((8, 1536), const),
                pl.BlockSpec((8, 128), const),
                pl.BlockSpec((8, SSD_W), const),
                pl.BlockSpec((ncb, SSD_N, SSD_W), lambda i: (nb - 1 - i, 0, 0)),
                pl.BlockSpec((tb, SSD_W), rev),
            ] + cx["specs"],
            out_specs=[
                pl.BlockSpec((tb, 2560), rev),
                pl.BlockSpec((tb, 128), rev),
                pl.BlockSpec((8, 1536), const),
                pl.BlockSpec((8, 1536), const),
                pl.BlockSpec((8, 128), const),
                pl.BlockSpec((8, SSD_W), const),
            ] + cx["specs"],
            out_shape=[
                jax.ShapeDtypeStruct((seq, 2560), BF16),
                jax.ShapeDtypeStruct((seq, 128), BF16),
                jax.ShapeDtypeStruct((8, 1536), F32),
                jax.ShapeDtypeStruct((8, 1536), F32),
                jax.ShapeDtypeStruct((8, 128), F32),
                jax.ShapeDtypeStruct((8, SSD_W), F32),
            ] + cx["out_shape"],
            scratch_shapes=[pltpu.VMEM((SSD_N, SSD_W), F32), pltpu.VMEM((8, 1536), F32)] + cx["scratch"],
            compiler_params=pltpu.CompilerParams(dimension_semantics=("arbitrary",), vmem_limit_bytes=VMEM_LIMIT,
                                                 has_side_effects=comm is not None),
            name="ssd_bwd" + cx["tag"],
        )(ps, pre, y, ss, cw, cb, prm, nw, st, dob, *comm_args)

    return call


def _ret_consts(h):
    lg = math.log(1.0 - 2.0 ** (-5.0 - h))
    r = _iota2((CH, CH), 0)
    c = _iota2((CH, CH), 1)
    rel = (r - c).astype(F32)
    dmat = jnp.where(r >= c, jnp.exp(jnp.maximum(rel, 0.0) * lg), 0.0)
    idx = _iota2((CH, 1), 0).astype(F32)
    qdec = jnp.exp((idx + 1.0) * lg)
    kdec = jnp.exp((CH - 1.0 - idx) * lg)
    cdec = math.exp(CH * lg)
    return dmat, qdec, kdec, cdec


def _ret_batch(pr_ref, cc_ref, ss_ref, ncb):
    pairs = [(c, h) for c in range(ncb) for h in range(RET_H)]

    def st(off):
        return jnp.stack([pr_ref[c * CH:(c + 1) * CH, off + h * 128:off + (h + 1) * 128] for c, h in pairs])

    cc = jnp.stack([cc_ref[c * CH:(c + 1) * CH, :] for c, _ in pairs])
    ss = jnp.stack([ss_ref[c * CH:(c + 1) * CH, :] for c, _ in pairs])
    consts = [_ret_consts(h) for h in range(RET_H)]
    dmat = jnp.stack([consts[h][0] for _, h in pairs])
    qdec = jnp.stack([consts[h][1] for _, h in pairs])
    kdec = jnp.stack([consts[h][2] for _, h in pairs])
    cdec = jnp.stack([jnp.full((1, 1), consts[h][3], F32) for h in range(RET_H)])
    q = _rot(st(0), cc, ss)
    k = _rot(st(512), cc, ss) * (RET_D ** -0.5)
    return dict(q=q, k=k, v=st(1024), z=st(1536), cc=cc, ss=ss, dmat=dmat, qdec=qdec, kdec=kdec, cdec=cdec,
                s=_bdot(q, k, _NT) * dmat)


def _rot(t, cc, ss):
    return t * cc + pltpu.roll(t, 64, axis=t.ndim - 1) * ss


def _rot_bwd(d, cc, ss):
    return d * cc + pltpu.roll(d * ss, 64, axis=d.ndim - 1)


def _make_ret_fwd(seq, tb):
    ncb = tb // CH
    nb = seq // tb

    def body(pr_ref, cc_ref, ss_ref, nw_ref, oc_ref, st_ref, r_scr):
        @pl.when(pl.program_id(0) == 0)
        def _():
            r_scr[...] = jnp.zeros_like(r_scr)

        d = _ret_batch(pr_ref, cc_ref, ss_ref, ncb)
        kd = d["k"] * d["kdec"]
        for c in range(ncb):
            bs = slice(c * RET_H, (c + 1) * RET_H)
            rs = r_scr[...]
            st_ref[c] = rs
            r_scr[...] = rs * d["cdec"] + _bdot(kd[bs], d["v"][bs], _TN)
        r_prev = st_ref[...].reshape(ncb * RET_H, 128, 128)
        o = _bdot(d["s"], d["v"], _NN) + _bdot(d["q"], r_prev, _NN) * d["qdec"]
        _, _, y = _rms_fwd(o, nw_ref[0:1, :], RET_D)
        out = y * _silu(d["z"])
        for c in range(ncb):
            for h in range(RET_H):
                oc_ref[c * CH:(c + 1) * CH, h * 128:(h + 1) * 128] = out[c * RET_H + h].astype(oc_ref.dtype)

    def call(pr, cc, ss, nw):
        return pl.pallas_call(
            body,
            grid=(nb,),
            in_specs=[
                pl.BlockSpec((tb, 2048), lambda i: (i, 0)),
                pl.BlockSpec((tb, 128), lambda i: (i, 0)),
                pl.BlockSpec((tb, 128), lambda i: (i, 0)),
                pl.BlockSpec((8, 128), lambda i: (0, 0)),
            ],
            out_specs=[
                pl.BlockSpec((tb, 512), lambda i: (i, 0)),
                pl.BlockSpec((ncb, RET_H, 128, 128), lambda i: (i, 0, 0, 0)),
            ],
            out_shape=[
                jax.ShapeDtypeStruct((seq, 512), BF16),
                jax.ShapeDtypeStruct((seq // CH, RET_H, 128, 128), F32),
            ],
            scratch_shapes=[pltpu.VMEM((RET_H, 128, 128), F32)],
            compiler_params=pltpu.CompilerParams(dimension_semantics=("arbitrary",), vmem_limit_bytes=VMEM_LIMIT),
            name="ret_fwd",
        )(pr, cc, ss, nw)

    return call


def _make_ret_bwd(seq, tb):
    ncb = tb // CH
    nb = seq // tb

    def body(pr_ref, cc_ref, ss_ref, nw_ref, st_ref, doc_ref, dpr_ref, dnw_ref, dr_scr):
        @pl.when(pl.program_id(0) == 0)
        def _():
            dr_scr[...] = jnp.zeros_like(dr_scr)
            dnw_ref[...] = jnp.zeros_like(dnw_ref)

        nw = nw_ref[0:1, :]
        scale = RET_D ** -0.5
        n = ncb * RET_H
        d = _ret_batch(pr_ref, cc_ref, ss_ref, ncb)
        q, k, v, z, s = d["q"], d["k"], d["v"], d["z"], d["s"]
        r_prev = st_ref[...].reshape(n, 128, 128)
        o = _bdot(s, v, _NN) + _bdot(q, r_prev, _NN) * d["qdec"]
        doc = jnp.stack([doc_ref[c * CH:(c + 1) * CH, h * 128:(h + 1) * 128]
                         for c in range(ncb) for h in range(RET_H)])
        on, r, y = _rms_fwd(o, nw, RET_D)
        dz = doc * y * _dsilu(z)
        do, dnw_rows = _rms_bwd(doc * _silu(z), on, r, nw, RET_D)
        dnw_acc = jnp.sum(jnp.sum(dnw_rows, axis=0), axis=0, keepdims=True)
        dqd = do * d["qdec"]
        qtd = _bdot(q, dqd, _TN)
        drn_l = [None] * ncb
        for c in reversed(range(ncb)):
            drn_l[c] = dr_scr[...]
            dr_scr[...] = qtd[c * RET_H:(c + 1) * RET_H] + d["cdec"] * drn_l[c]
        drn = jnp.concatenate(drn_l, axis=0)
        ds = _bdot(do, v, _NT) * d["dmat"]
        dq = _rot_bwd(_bdot(ds, k, _NN) + _bdot(dqd, r_prev, _NT), d["cc"], d["ss"])
        dk = _rot_bwd((_bdot(ds, q, _TN) + _bdot(v, drn, _NT) * d["kdec"]) * scale, d["cc"], d["ss"])
        dv = _bdot(s, do, _TN) + _bdot(k * d["kdec"], drn, _NN)
        for c in range(ncb):
            rows = slice(c * CH, (c + 1) * CH)
            for h in range(RET_H):
                b = c * RET_H + h
                for j, val in enumerate((dq, dk, dv, dz)):
                    dpr_ref[rows, j * 512 + h * 128:j * 512 + (h + 1) * 128] = val[b].astype(dpr_ref.dtype)
        dnw_ref[...] += jnp.where(_iota2((8, 128), 0) == 0, dnw_acc, 0.0)

    def call(pr, cc, ss, nw, st, doc):
        rev = lambda i: (nb - 1 - i, 0)
        return pl.pallas_call(
            body,
            grid=(nb,),
            in_specs=[
                pl.BlockSpec((tb, 2048), rev),
                pl.BlockSpec((tb, 128), rev),
                pl.BlockSpec((tb, 128), rev),
                pl.BlockSpec((8, 128), lambda i: (0, 0)),
                pl.BlockSpec((ncb, RET_H, 128, 128), lambda i: (nb - 1 - i, 0, 0, 0)),
                pl.BlockSpec((tb, 512), rev),
            ],
            out_specs=[
                pl.BlockSpec((tb, 2048), rev),
                pl.BlockSpec((8, 128), lambda i: (0, 0)),
            ],
            out_shape=[
                jax.ShapeDtypeStruct((seq, 2048), BF16),
                jax.ShapeDtypeStruct((8, 128), F32),
            ],
            scratch_shapes=[pltpu.VMEM((RET_H, 128, 128), F32)],
            compiler_params=pltpu.CompilerParams(dimension_semantics=("arbitrary",), vmem_limit_bytes=VMEM_LIMIT),
            name="ret_bwd",
        )(pr, cc, ss, nw, st, doc)

    return call


def _rope_tables(seq):
    half = RET_D // 2
    inv = ROPE_BASE ** (-jnp.arange(half, dtype=F32) / half)
    ang = jnp.arange(seq, dtype=jnp.int32).astype(F32)[:, None] * inv[None, :]
    cos, sin = jnp.cos(ang), jnp.sin(ang)
    return jnp.concatenate([cos, cos], axis=1), jnp.concatenate([-sin, sin], axis=1)


SEG_G, SEG_S, SEG_R, SEG_GS, SEG_SS = (0, 2048), (2048, 4608), (4608, 6656), (6656, 6784), (6784, 6912)
NP = 6912
SEGS = (SEG_G, SEG_S, SEG_R, SEG_GS, SEG_SS)


def _resident(shape):
    return pl.BlockSpec(shape, lambda i: (0,) * len(shape), pipeline_mode=pl.Buffered(1))


def _make_inproj(seq, tl):
    def body(x_ref, pn_ref, w_ref, pg_ref, ps_ref, pr_ref, gs_ref, ss_ref, ht_ref):
        x = x_ref[...]
        _, _, hn = _rms_fwd(x, pn_ref[0:1, :], D_MODEL)
        h = hn.astype(BF16)
        ht_ref[...] = hn.T.astype(BF16)
        for (a, b), o_ref in zip(SEGS, (pg_ref, ps_ref, pr_ref, gs_ref, ss_ref)):
            o_ref[...] = jnp.dot(h, w_ref[:, a:b], preferred_element_type=F32)

    def call(x, pn, w, comm=None, comm_args=()):
        row = lambda i: (i, 0)
        cx = _exchange_specs(comm)
        return pl.pallas_call(
            _with_exchange(body, comm, 3, 6, seq // tl),
            grid=(seq // tl,),
            in_specs=[pl.BlockSpec((tl, D_MODEL), row), _resident((8, D_MODEL)), _resident((D_MODEL, NP))]
            + cx["specs"],
            out_specs=[pl.BlockSpec((tl, b - a), row) for a, b in SEGS]
            + [pl.BlockSpec((D_MODEL, tl), lambda i: (0, i))] + cx["specs"],
            out_shape=[jax.ShapeDtypeStruct((seq, b - a), F32) for a, b in SEGS]
            + [jax.ShapeDtypeStruct((D_MODEL, seq), BF16)] + cx["out_shape"],
            scratch_shapes=cx["scratch"],
            compiler_params=pltpu.CompilerParams(dimension_semantics=("arbitrary",), vmem_limit_bytes=VMEM_LIMIT,
                                                 has_side_effects=comm is not None),
            name="inproj" + cx["tag"],
        )(x, pn, w, *comm_args)

    return call


def _make_outproj(seq, tl):
    def body(oa_ref, ob_ref, oc_ref, w_ref, x_ref, qn_ref, out_ref, xn_ref):
        out = (jnp.dot(oa_ref[...], w_ref[0:512, :], preferred_element_type=F32)
               + jnp.dot(ob_ref[...], w_ref[512:1536, :], preferred_element_type=F32)
               + jnp.dot(oc_ref[...], w_ref[1536:2048, :], preferred_element_type=F32))
        out_ref[...] = out
        _, _, y = _rms_fwd(out, qn_ref[0:1, :], D_MODEL)
        xn_ref[...] = x_ref[...] + y

    def call(oa, ob, oc, w, x, qn):
        row = lambda i: (i, 0)
        return pl.pallas_call(
            body,
            grid=(seq // tl,),
            in_specs=[pl.BlockSpec((tl, 512), row), pl.BlockSpec((tl, 1024), row), pl.BlockSpec((tl, 512), row),
                      _resident((2048, D_MODEL)), pl.BlockSpec((tl, D_MODEL), row), _resident((8, D_MODEL))],
            out_specs=[pl.BlockSpec((tl, D_MODEL), row), pl.BlockSpec((tl, D_MODEL), row)],
            out_shape=[jax.ShapeDtypeStruct((seq, D_MODEL), F32), jax.ShapeDtypeStruct((seq, D_MODEL), F32)],
            compiler_params=pltpu.CompilerParams(dimension_semantics=("arbitrary",), vmem_limit_bytes=VMEM_LIMIT),
            name="outproj",
        )(oa, ob, oc, w, x, qn)

    return call


def _make_outproj_loss(seq, tl):
    def body(oa_ref, ob_ref, oc_ref, w_ref, x_ref, qn_ref, t_ref, out_ref, dy_ref, loss_ref):
        @pl.when(pl.program_id(0) == 0)
        def _():
            loss_ref[...] = jnp.zeros_like(loss_ref)

        out = (jnp.dot(oa_ref[...], w_ref[0:512, :], preferred_element_type=F32)
               + jnp.dot(ob_ref[...], w_ref[512:1536, :], preferred_element_type=F32)
               + jnp.dot(oc_ref[...], w_ref[1536:2048, :], preferred_element_type=F32))
        out_ref[...] = out
        _, _, y = _rms_fwd(out, qn_ref[0:1, :], D_MODEL)
        err = (x_ref[...] + y) - t_ref[...]
        dy_ref[...] = err * (1.0 / D_MODEL)
        part = jnp.sum(jnp.sum(err * err, axis=1, keepdims=True), axis=0, keepdims=True) * (0.5 / D_MODEL)
        loss_ref[...] += jnp.where((_iota2((8, 128), 0) == 0) & (_iota2((8, 128), 1) == 0), part, 0.0)

    def call(oa, ob, oc, w, x, qn, t):
        row = lambda i: (i, 0)
        return pl.pallas_call(
            body,
            grid=(seq // tl,),
            in_specs=[pl.BlockSpec((tl, 512), row), pl.BlockSpec((tl, 1024), row), pl.BlockSpec((tl, 512), row),
                      _resident((2048, D_MODEL)), pl.BlockSpec((tl, D_MODEL), row), _resident((8, D_MODEL)),
                      pl.BlockSpec((tl, D_MODEL), row)],
            out_specs=[pl.BlockSpec((tl, D_MODEL), row), pl.BlockSpec((tl, D_MODEL), row),
                       pl.BlockSpec((8, 128), lambda i: (0, 0))],
            out_shape=[jax.ShapeDtypeStruct((seq, D_MODEL), F32), jax.ShapeDtypeStruct((seq, D_MODEL), F32),
                       jax.ShapeDtypeStruct((8, 128), F32)],
            compiler_params=pltpu.CompilerParams(dimension_semantics=("arbitrary",), vmem_limit_bytes=VMEM_LIMIT),
            name="outproj_loss",
        )(oa, ob, oc, w, x, qn, t)

    return call


def _make_outproj_bwd(seq, tl):
    def body(dxn_ref, out_ref, oa_ref, ob_ref, oc_ref, w_ref, qn_ref, doa_ref, dob_ref, doc_ref, dqn_ref, dw_ref):
        @pl.when(pl.program_id(0) == 0)
        def _():
            dqn_ref[...] = jnp.zeros_like(dqn_ref)
            dw_ref[...] = jnp.zeros_like(dw_ref)

        qn = qn_ref[0:1, :]
        on, r, _ = _rms_fwd(out_ref[...], qn, D_MODEL)
        dout, dqn_rows = _rms_bwd(dxn_ref[...], on, r, qn, D_MODEL)
        dqn_ref[...] += jnp.where(_iota2((8, D_MODEL), 0) == 0, jnp.sum(dqn_rows, axis=0, keepdims=True), 0.0)
        db = dout.astype(BF16)
        nt = (((1,), (1,)), ((), ()))
        tn = (((0,), (0,)), ((), ()))
        doa_ref[...] = lax.dot_general(db, w_ref[0:512, :], nt, preferred_element_type=F32).astype(BF16)
        dob_ref[...] = lax.dot_general(db, w_ref[512:1536, :], nt, preferred_element_type=F32).astype(BF16)
        doc_ref[...] = lax.dot_general(db, w_ref[1536:2048, :], nt, preferred_element_type=F32).astype(BF16)
        dw_ref[0:512, :] += lax.dot_general(oa_ref[...], db, tn, preferred_element_type=F32)
        dw_ref[512:1536, :] += lax.dot_general(ob_ref[...], db, tn, preferred_element_type=F32)
        dw_ref[1536:2048, :] += lax.dot_general(oc_ref[...], db, tn, preferred_element_type=F32)

    def call(dxn, out, oa, ob, oc, w, qn):
        row = lambda i: (i, 0)
        const = lambda i: (0, 0)
        return pl.pallas_call(
            body,
            grid=(seq // tl,),
            in_specs=[pl.BlockSpec((tl, D_MODEL), row), pl.BlockSpec((tl, D_MODEL), row),
                      pl.BlockSpec((tl, 512), row), pl.BlockSpec((tl, 1024), row), pl.BlockSpec((tl, 512), row),
                      _resident((2048, D_MODEL)), _resident((8, D_MODEL))],
            out_specs=[pl.BlockSpec((tl, 512), row), pl.BlockSpec((tl, 1024), row), pl.BlockSpec((tl, 512), row),
                       pl.BlockSpec((8, D_MODEL), const), pl.BlockSpec((2048, D_MODEL), const)],
            out_shape=[jax.ShapeDtypeStruct((seq, 512), BF16), jax.ShapeDtypeStruct((seq, 1024), BF16),
                       jax.ShapeDtypeStruct((seq, 512), BF16), jax.ShapeDtypeStruct((8, D_MODEL), F32),
                       jax.ShapeDtypeStruct((2048, D_MODEL), F32)],
            compiler_params=pltpu.CompilerParams(dimension_semantics=("arbitrary",), vmem_limit_bytes=VMEM_LIMIT),
            name="outproj_bwd",
        )(dxn, out, oa, ob, oc, w, qn)

    return call


def _make_inproj_bwd_dx(seq, tl):
    def body(dg_ref, ds_ref, dr_ref, dgs_ref, dss_ref, w_ref, x_ref, pn_ref, dxn_ref, dx_ref, dpn_ref):
        @pl.when(pl.program_id(0) == 0)
        def _():
            dpn_ref[...] = jnp.zeros_like(dpn_ref)

        nt = (((1,), (1,)), ((), ()))
        dh = jnp.zeros((tl, D_MODEL), F32)
        for (a, b), d_ref in zip(SEGS, (dg_ref, ds_ref, dr_ref, dgs_ref, dss_ref)):
            dh = dh + lax.dot_general(d_ref[...], w_ref[:, a:b], nt, preferred_element_type=F32)
        pn = pn_ref[0:1, :]
        on, r, _ = _rms_fwd(x_ref[...], pn, D_MODEL)
        dx, dpn_rows = _rms_bwd(dh, on, r, pn, D_MODEL)
        dx_ref[...] = dx + dxn_ref[...]
        dpn_ref[...] += jnp.where(_iota2((8, D_MODEL), 0) == 0, jnp.sum(dpn_rows, axis=0, keepdims=True), 0.0)

    def call(dg, ds, dr, dgs, dss, w, x, pn, dxn, comm=None, comm_args=()):
        row = lambda i: (i, 0)
        cx = _exchange_specs(comm)
        return pl.pallas_call(
            _with_exchange(body, comm, 9, 2, seq // tl),
            grid=(seq // tl,),
            in_specs=[pl.BlockSpec((tl, b - a), row) for a, b in SEGS]
            + [_resident((D_MODEL, NP)), pl.BlockSpec((tl, D_MODEL), row), _resident((8, D_MODEL)),
               pl.BlockSpec((tl, D_MODEL), row)] + cx["specs"],
            out_specs=[pl.BlockSpec((tl, D_MODEL), row), pl.BlockSpec((8, D_MODEL), lambda i: (0, 0))] + cx["specs"],
            out_shape=[jax.ShapeDtypeStruct((seq, D_MODEL), F32), jax.ShapeDtypeStruct((8, D_MODEL), F32)]
            + cx["out_shape"],
            scratch_shapes=cx["scratch"],
            compiler_params=pltpu.CompilerParams(dimension_semantics=("arbitrary",), vmem_limit_bytes=VMEM_LIMIT,
                                                 has_side_effects=comm is not None),
            name="inproj_bwd_dx" + cx["tag"],
        )(dg, ds, dr, dgs, dss, w, x, pn, dxn, *comm_args)

    return call


def _make_inproj_bwd_dw(seq, tl, width, tn, name):
    def body(ht_ref, d_ref, dw_ref):
        @pl.when(pl.program_id(1) == 0)
        def _():
            dw_ref[...] = jnp.zeros_like(dw_ref)

        dw_ref[...] += jnp.dot(ht_ref[...], d_ref[...], preferred_element_type=F32)

    def call(ht, d):
        return pl.pallas_call(
            body,
            grid=(width // tn, seq // tl),
            in_specs=[pl.BlockSpec((D_MODEL, tl), lambda j, i: (0, i)), pl.BlockSpec((tl, tn), lambda j, i: (i, j))],
            out_specs=pl.BlockSpec((D_MODEL, tn), lambda j, i: (0, j)),
            out_shape=jax.ShapeDtypeStruct((D_MODEL, width), F32),
            compiler_params=pltpu.CompilerParams(dimension_semantics=("arbitrary", "arbitrary"),
                                                 vmem_limit_bytes=VMEM_LIMIT),
            name=name,
        )(ht, d)

    return call


ADAM_LR, ADAM_B1, ADAM_B2, ADAM_EPS, ADAM_WD, ADAM_STEP = 0.001, 0.9, 0.999, 1e-08, 0.01, 10


def _adam_math(w, g, m, v):
    m = ADAM_B1 * m + (1.0 - ADAM_B1) * g
    v = ADAM_B2 * v + (1.0 - ADAM_B2) * (g * g)
    m_hat = m / (1.0 - ADAM_B1 ** ADAM_STEP)
    v_hat = v / (1.0 - ADAM_B2 ** ADAM_STEP)
    delta = -ADAM_LR * (m_hat / (jnp.sqrt(v_hat) + ADAM_EPS) + ADAM_WD * w)
    return delta, m, v


def _adamw(w, g, m, v, name):
    shape = w.shape
    cols = shape[-1]
    rows = w.size // cols
    tr = rows if rows <= 512 else 256
    assert rows % tr == 0

    def body(w_ref, g_ref, m_ref, v_ref, d_ref, mo_ref, vo_ref):
        d_ref[...], mo_ref[...], vo_ref[...] = _adam_math(w_ref[...], g_ref[...], m_ref[...], v_ref[...])

    spec = pl.BlockSpec((tr, cols), lambda i: (i, 0))
    outs = pl.pallas_call(
        body,
        grid=(rows // tr,),
        in_specs=[spec] * 4,
        out_specs=[spec] * 3,
        out_shape=[jax.ShapeDtypeStruct((rows, cols), F32)] * 3,
        compiler_params=pltpu.CompilerParams(dimension_semantics=("arbitrary",), vmem_limit_bytes=VMEM_LIMIT),
        name=name,
    )(*[a.reshape(rows, cols) for a in (w, g, m, v)])
    return (g,) + tuple(o.reshape(shape) for o in outs)


def _adamw_pairs(w, mine, theirs, m, v, name):
    na, r, cols = w.shape
    assert na == 2
    tr = 256
    assert r % tr == 0

    def body(w_ref, a0_ref, b0_ref, a1_ref, b1_ref, m_ref, v_ref, g_ref, d_ref, mo_ref, vo_ref):
        g = jnp.where(pl.program_id(0) == 0, a0_ref[...] + b0_ref[...], a1_ref[...] + b1_ref[...])
        g_ref[...] = g
        d_ref[...], mo_ref[...], vo_ref[...] = _adam_math(w_ref[...], g, m_ref[...], v_ref[...])

    nblk = r // tr
    full = pl.BlockSpec((None, tr, cols), lambda a, i: (a, i, 0))
    lay0 = pl.BlockSpec((None, tr, cols), lambda a, i: (0, i * (1 - a) + (nblk - 1) * a, 0))
    lay1 = pl.BlockSpec((None, tr, cols), lambda a, i: (0, i * a, 0))
    return pl.pallas_call(
        body,
        grid=(na, nblk),
        in_specs=[full, lay0, lay0, lay1, lay1, full, full],
        out_specs=[full] * 4,
        out_shape=[jax.ShapeDtypeStruct(w.shape, F32)] * 4,
        compiler_params=pltpu.CompilerParams(dimension_semantics=("arbitrary",) * 2, vmem_limit_bytes=VMEM_LIMIT),
        name=name,
    )(w, mine[0], theirs[0], mine[1], theirs[1], m, v)


MESH = pl.DeviceIdType.MESH
ANY = pl.BlockSpec(memory_space=pl.ANY)
CHIP_REL = ((1, 0), (0, 1), (1, 1))


def _flip(v, d):
    return 1 - v if d else v


def _ag_chips(arrs, name):
    n = len(arrs)

    def body(*refs):
        ins, outs = refs[:n], refs[n:2 * n]
        send_sems, recv_sems, loc_sems = refs[2 * n:]
        x, y, c = lax.axis_index("x"), lax.axis_index("y"), lax.axis_index("c")
        me = 2 * x + y

        def remote(a, k, slot):
            dx, dy = CHIP_REL[k]
            return pltpu.make_async_remote_copy(
                src_ref=ins[a], dst_ref=outs[a].at[slot], send_sem=send_sems.at[a * 3 + k],
                recv_sem=recv_sems.at[a * 3 + k], device_id=(_flip(x, dx), _flip(y, dy), c), device_id_type=MESH)

        local = [pltpu.make_async_copy(ins[a], outs[a].at[me], loc_sems.at[a]) for a in range(n)]
        for cp in local:
            cp.start()
        for a in range(n):
            for k in range(3):
                remote(a, k, me).start()
        for a in range(n):
            for k, (dx, dy) in enumerate(CHIP_REL):
                remote(a, k, 2 * _flip(x, dx) + _flip(y, dy)).wait_recv()
        for a in range(n):
            for k in range(3):
                remote(a, k, me).wait_send()
        for cp in local:
            cp.wait()

    return pl.pallas_call(
        body,
        in_specs=[ANY] * n,
        out_specs=[ANY] * n,
        out_shape=[jax.ShapeDtypeStruct((4,) + a.shape, a.dtype) for a in arrs],
        scratch_shapes=[pltpu.SemaphoreType.DMA((3 * n,)), pltpu.SemaphoreType.DMA((3 * n,)),
                        pltpu.SemaphoreType.DMA((n,))],
        compiler_params=pltpu.CompilerParams(has_side_effects=True),
        name=name,
    )(*arrs)


class _ChipExchange:
    def __init__(self, kind, arrs, swap=()):
        self.kind, self.n_chip, self.n = kind, len(arrs), len(arrs) + len(swap)
        if kind == "gather":
            self.out_shape = [jax.ShapeDtypeStruct((4,) + a.shape, a.dtype) for a in arrs]
        else:
            self.out_shape = [jax.ShapeDtypeStruct((3,) + a.shape[1:], a.dtype) for a in arrs]
        self.out_shape += [jax.ShapeDtypeStruct(a.shape, a.dtype) for a in swap]
        self.scratch = [pltpu.SemaphoreType.DMA((4 * self.n,)), pltpu.SemaphoreType.DMA((4 * self.n,))]

    def _copies(self, ins, outs, sems):
        send_sems, recv_sems = sems
        x, y, c = lax.axis_index("x"), lax.axis_index("y"), lax.axis_index("c")
        me = 2 * x + y
        pairs = []
        for a in range(self.n_chip, self.n):
            cp = pltpu.make_async_remote_copy(
                src_ref=ins[a], dst_ref=outs[a], send_sem=send_sems.at[4 * a], recv_sem=recv_sems.at[4 * a],
                device_id=(x, y, 1 - c), device_id_type=MESH)
            pairs.append((cp, cp))
        for a in range(self.n_chip):
            for k, (dx, dy) in enumerate(CHIP_REL):
                px, py = _flip(x, dx), _flip(y, dy)
                sem = dict(send_sem=send_sems.at[4 * a + k], recv_sem=recv_sems.at[4 * a + k],
                           device_id=(px, py, c), device_id_type=MESH)
                if self.kind == "gather":
                    out = pltpu.make_async_remote_copy(src_ref=ins[a], dst_ref=outs[a].at[me], **sem)
                    inc = pltpu.make_async_remote_copy(src_ref=ins[a], dst_ref=outs[a].at[2 * px + py], **sem)
                else:
                    out = pltpu.make_async_remote_copy(src_ref=ins[a].at[2 * px + py], dst_ref=outs[a].at[k], **sem)
                    inc = out
                pairs.append((out, inc))
            if self.kind == "gather":
                own = pltpu.make_async_remote_copy(
                    src_ref=ins[a], dst_ref=outs[a].at[me], send_sem=send_sems.at[4 * a + 3],
                    recv_sem=recv_sems.at[4 * a + 3], device_id=(x, y, 1 - c), device_id_type=MESH)
                pairs.append((own, own))
        return pairs

    def start(self, ins, outs, sems):
        for out, _ in self._copies(ins, outs, sems):
            out.start()

    def finish(self, ins, outs, sems):
        pairs = self._copies(ins, outs, sems)
        for _, inc in pairs:
            inc.wait_recv()
        for out, _ in pairs:
            out.wait_send()


def _with_exchange(body, comm, n_in, n_out, nb):
    if comm is None:
        return body

    def wrapped(*refs):
        ins = refs[:n_in]
        c_in = refs[n_in:n_in + comm.n]
        outs = refs[n_in + comm.n:n_in + comm.n + n_out]
        c_out = refs[n_in + comm.n + n_out:n_in + 2 * comm.n + n_out]
        rest = refs[n_in + 2 * comm.n + n_out:]
        scratch, sems = rest[:len(rest) - 2], rest[len(rest) - 2:]

        @pl.when(pl.program_id(0) == 0)
        def _():
            comm.start(c_in, c_out, sems)

        body(*ins, *outs, *scratch)

        @pl.when(pl.program_id(0) == nb - 1)
        def _():
            comm.finish(c_in, c_out, sems)

    return wrapped


def _exchange_specs(comm):
    if comm is None:
        return dict(specs=[], out_shape=[], scratch=[], tag="")
    return dict(specs=[pl.BlockSpec(memory_space=pl.ANY)] * comm.n, out_shape=list(comm.out_shape),
                scratch=list(comm.scratch), tag="_" + comm.kind)


def _half(ref_or_shape, half):
    r = ref_or_shape[-2] // 2
    return pl.ds(half * r, r)


def _ag_rows(arrs, name):
    n = len(arrs)

    def body(*refs):
        ins, outs = refs[:n], refs[n:2 * n]
        send_sems, recv_sems, fsend_sems, frecv_sems, loc_sems = refs[2 * n:]
        x, y, c = lax.axis_index("x"), lax.axis_index("y"), lax.axis_index("c")
        me = 2 * x + y
        sib = (x, y, 1 - c)

        def chip_of(k):
            dx, dy = CHIP_REL[k]
            return _flip(x, dx), _flip(y, dy)

        def ici(a, k, slot):
            px, py = chip_of(k)
            rows = _half(arrs[a].shape, c)
            return pltpu.make_async_remote_copy(
                src_ref=ins[a].at[:, rows, :], dst_ref=outs[a].at[slot, :, rows, :], send_sem=send_sems.at[a * 3 + k],
                recv_sem=recv_sems.at[a * 3 + k], device_id=(px, py, c), device_id_type=MESH)

        def fwd(a, k, half):
            px, py = chip_of(k)
            blk = outs[a].at[2 * px + py, :, _half(arrs[a].shape, half), :]
            return pltpu.make_async_remote_copy(
                src_ref=blk, dst_ref=blk, send_sem=fsend_sems.at[a * 3 + k], recv_sem=frecv_sems.at[a * 3 + k],
                device_id=sib, device_id_type=MESH)

        own = [pltpu.make_async_remote_copy(src_ref=ins[a], dst_ref=outs[a].at[me], send_sem=loc_sems.at[a],
                                            recv_sem=loc_sems.at[n + a], device_id=sib, device_id_type=MESH)
               for a in range(n)]
        for cp in own:
            cp.start()
        for a in range(n):
            for k in range(3):
                ici(a, k, me).start()
        for a in range(n):
            for k in range(3):
                px, py = chip_of(k)
                ici(a, k, 2 * px + py).wait_recv()
                fwd(a, k, c).start()
        for a in range(n):
            for k in range(3):
                fwd(a, k, 1 - c).wait_recv()
        for a in range(n):
            for k in range(3):
                ici(a, k, me).wait_send()
                fwd(a, k, c).wait_send()
        for cp in own:
            cp.wait()

    return pl.pallas_call(
        body,
        in_specs=[ANY] * n,
        out_specs=[ANY] * n,
        out_shape=[jax.ShapeDtypeStruct((4,) + a.shape, a.dtype) for a in arrs],
        scratch_shapes=[pltpu.SemaphoreType.DMA((3 * n,)) for _ in range(4)] + [pltpu.SemaphoreType.DMA((2 * n,))],
        compiler_params=pltpu.CompilerParams(has_side_effects=True),
        name=name,
    )(*arrs)


def _sum_chips(own, recv, chip, name):
    _, na, r, cols = own.shape
    tr = 256
    assert r % tr == 0

    def body(chip_ref, o_ref, r_ref, s_ref):
        s_ref[...] = ((o_ref[...] + r_ref[0].astype(F32)) + r_ref[1].astype(F32)) + r_ref[2].astype(F32)

    return pl.pallas_call(
        body,
        grid_spec=pltpu.PrefetchScalarGridSpec(
            num_scalar_prefetch=1,
            grid=(na, r // tr),
            in_specs=[pl.BlockSpec((None, None, tr, cols), lambda a, i, ch: (ch[0], a, i, 0)),
                      pl.BlockSpec((3, None, tr, cols), lambda a, i, ch: (0, a, i, 0))],
            out_specs=pl.BlockSpec((None, tr, cols), lambda a, i, ch: (a, i, 0))),
        out_shape=jax.ShapeDtypeStruct((na, r, cols), F32),
        compiler_params=pltpu.CompilerParams(dimension_semantics=("arbitrary",) * 2, vmem_limit_bytes=VMEM_LIMIT),
        name=name,
    )(chip, own, recv)


def _swap_sibling(arrs, name):
    n = len(arrs)

    def body(*refs):
        ins, outs = refs[:n], refs[n:2 * n]
        send_sems, recv_sems = refs[2 * n:]
        x, y, c = lax.axis_index("x"), lax.axis_index("y"), lax.axis_index("c")
        cps = [pltpu.make_async_remote_copy(src_ref=ins[a], dst_ref=outs[a], send_sem=send_sems.at[a],
                                            recv_sem=recv_sems.at[a], device_id=(x, y, 1 - c), device_id_type=MESH)
               for a in range(n)]
        for cp in cps:
            cp.start()
        for cp in cps:
            cp.wait_recv()
        for cp in cps:
            cp.wait_send()

    return pl.pallas_call(
        body,
        in_specs=[ANY] * n,
        out_specs=[ANY] * n,
        out_shape=[jax.ShapeDtypeStruct(a.shape, a.dtype) for a in arrs],
        scratch_shapes=[pltpu.SemaphoreType.DMA((n,)), pltpu.SemaphoreType.DMA((n,))],
        compiler_params=pltpu.CompilerParams(has_side_effects=True),
        name=name,
    )(*arrs)


def _allreduce_small(vec, name):
    rows = vec.shape[0]

    def body(v_ref, out_ref, gat_ref, send_sems, recv_sems):
        x, y, c = lax.axis_index("x"), lax.axis_index("y"), lax.axis_index("c")
        me = 4 * x + 2 * y + c

        def remote(k, slot):
            dx, dy, dc = (k >> 2) & 1, (k >> 1) & 1, k & 1
            return pltpu.make_async_remote_copy(
                src_ref=v_ref, dst_ref=gat_ref.at[slot], send_sem=send_sems.at[k - 1], recv_sem=recv_sems.at[k - 1],
                device_id=(_flip(x, dx), _flip(y, dy), _flip(c, dc)), device_id_type=MESH)

        gat_ref[me] = v_ref[...]
        for k in range(1, 8):
            remote(k, me).start()
        for k in range(1, 8):
            dx, dy, dc = (k >> 2) & 1, (k >> 1) & 1, k & 1
            remote(k, 4 * _flip(x, dx) + 2 * _flip(y, dy) + _flip(c, dc)).wait_recv()
        for k in range(1, 8):
            remote(k, me).wait_send()
        acc = gat_ref[0]
        for j in range(1, 8):
            acc = acc + gat_ref[j]
        out_ref[...] = acc

    vm = pl.BlockSpec(memory_space=pltpu.VMEM)
    return pl.pallas_call(
        body,
        in_specs=[vm],
        out_specs=vm,
        out_shape=jax.ShapeDtypeStruct(vec.shape, F32),
        scratch_shapes=[pltpu.VMEM((8, rows, 128), F32), pltpu.SemaphoreType.DMA((7,)), pltpu.SemaphoreType.DMA((7,))],
        compiler_params=pltpu.CompilerParams(has_side_effects=True),
        name=name,
    )(vec)


def _pad8(v, width, lane0=0):
    v = v.reshape(1, -1) if v.ndim == 1 else v
    return jnp.zeros((8, width), F32).at[:v.shape[0], lane0:lane0 + v.shape[1]].set(v.astype(F32))


def _relayout_w_in(g):
    tr = 128
    q = N_IN // 4

    def body(g_ref, o_ref):
        w = jnp.concatenate([g_ref[j] for j in range(4)], axis=1)
        z = lambda n: jnp.zeros((tr, n), w.dtype)
        o_ref[...] = jnp.concatenate([w[:, 0:2048], w[:, 2056:4616], w[:, 4632:6680],
                                      w[:, 2048:2056], z(120), w[:, 4616:4632], z(112)], axis=1)

    return pl.pallas_call(
        body,
        grid=(D_MODEL // tr,),
        in_specs=[pl.BlockSpec((4, tr, q), lambda i: (0, i, 0))],
        out_specs=pl.BlockSpec((tr, NP), lambda i: (i, 0)),
        out_shape=jax.ShapeDtypeStruct((D_MODEL, NP), g.dtype),
        compiler_params=pltpu.CompilerParams(dimension_semantics=("arbitrary",), vmem_limit_bytes=VMEM_LIMIT),
        name="relayout_w_in",
    )(g)


def _unlayout_dw_in(dg, ds, dr, dsm):
    tr = 128
    q = N_IN // 4

    def body(g_ref, s_ref, r_ref, sm_ref, o_ref, ob_ref):
        w = jnp.concatenate([g_ref[...], sm_ref[:, 0:8], s_ref[...], sm_ref[:, 128:144], r_ref[...]], axis=1)
        for j in range(4):
            blk = w[:, q * j:q * (j + 1)]
            o_ref[j] = blk
            ob_ref[j] = blk.astype(BF16)

    row = lambda i: (i, 0)
    return pl.pallas_call(
        body,
        grid=(D_MODEL // tr,),
        in_specs=[pl.BlockSpec((tr, d.shape[1]), row) for d in (dg, ds, dr, dsm)],
        out_specs=[pl.BlockSpec((4, tr, q), lambda i: (0, i, 0))] * 2,
        out_shape=[jax.ShapeDtypeStruct((4, D_MODEL, q), F32), jax.ShapeDtypeStruct((4, D_MODEL, q), BF16)],
        compiler_params=pltpu.CompilerParams(dimension_semantics=("arbitrary",), vmem_limit_bytes=VMEM_LIMIT),
        name="unlayout_dw_in",
    )(dg, ds, dr, dsm)


TB = 256
TB_RET = 512
TL = 1024
TL_IN = 512
TL_OB = 1024
TK = 2048


def kernel(x, pre_norm, post_norm, w_in, gdn_conv, gdn_A_log, gdn_dt_bias, gdn_norm, ssd_conv, ssd_conv_b, ssd_A_log, ssd_dt_bias, ssd_D, ssd_norm, ret_norm, w_out, loss_target, m_pre_norm, m_post_norm, m_w_in, m_gdn_conv, m_gdn_A_log, m_gdn_dt_bias, m_gdn_norm, m_ssd_conv, m_ssd_conv_b, m_ssd_A_log, m_ssd_dt_bias, m_ssd_D, m_ssd_norm, m_ret_norm, m_w_out, v_pre_norm, v_post_norm, v_w_in, v_gdn_conv, v_gdn_A_log, v_gdn_dt_bias, v_gdn_norm, v_ssd_conv, v_ssd_conv_b, v_ssd_A_log, v_ssd_dt_bias, v_ssd_D, v_ssd_norm, v_ret_norm, v_w_out):
    seq = x.shape[1]
    chip = 2 * lax.axis_index("x") + lax.axis_index("y")
    x0 = x[0]

    wi_b, wo_b = w_in.astype(BF16), w_out.astype(BF16)
    (wi0_g,) = _ag_rows([wi_b[0:1]], "ag_weights")
    gcv_g, scv_g = _ag_chips([gdn_conv, ssd_conv], "ag_conv")
    full_w_in = _relayout_w_in
    wp = [full_w_in(wi0_g[:, 0]), None]
    wo = [None, None]
    ag0 = _ChipExchange("gather", [wo_b[0], wo_b[1]])
    ag1 = _ChipExchange("gather", [wi_b[1]])
    gcv = jnp.transpose(gcv_g, (1, 2, 0, 3)).reshape(DEPTH, CONV_W, 1536)
    scv = jnp.transpose(scv_g, (1, 2, 0, 3)).reshape(DEPTH, CONV_W, 1536)
    rope_c, rope_s = _rope_tables(seq)

    saved = []
    xc = x0
    for l in range(DEPTH):
        p = dict(
            pn=_pad8(pre_norm[l], D_MODEL), qn=_pad8(post_norm[l], D_MODEL),
            g_cw=_pad8(gcv[l], 1536), g_prm=_pad8(jnp.stack([gdn_A_log[l], gdn_dt_bias[l]]), 128, 4),
            g_nw=_pad8(gdn_norm[l], 128),
            s_cw=_pad8(scv[l], 1536), s_cb=_pad8(ssd_conv_b[l], 1536),
            s_prm=_pad8(jnp.stack([ssd_A_log[l], ssd_dt_bias[l], ssd_D[l]]), 128), s_nw=_pad8(ssd_norm[l], SSD_W),
            r_nw=_pad8(ret_norm[l], 128))
        if l == 0:
            pg, ps, pr, gs, ss, ht, wo0_g, wo1_g = _make_inproj(seq, TL_IN)(
                xc, p["pn"], wp[l], comm=ag0, comm_args=(wo_b[0], wo_b[1]))
            wo = [wo0_g.reshape(2048, D_MODEL), wo1_g.reshape(2048, D_MODEL)]
        else:
            pg, ps, pr, gs, ss, ht = _make_inproj(seq, TL_IN)(xc, p["pn"], wp[l])
        if l == 0:
            oa, stg, tig, uwg, gpre, wi1_g = _make_gdn_fwd(seq, TB)(
                pg, gs, p["g_cw"], p["g_prm"], p["g_nw"], comm=ag1, comm_args=(wi_b[1],))
            wp[1] = full_w_in(wi1_g)
        else:
            oa, stg, tig, uwg, gpre = _make_gdn_fwd(seq, TB)(pg, gs, p["g_cw"], p["g_prm"], p["g_nw"])
        ob, sts, spre, sy = _make_ssd_fwd(seq, TB)(ps, ss, p["s_cw"], p["s_cb"], p["s_prm"], p["s_nw"])
        oc, str_ = _make_ret_fwd(seq, TB_RET)(pr, rope_c, rope_s, p["r_nw"])
        if l == DEPTH - 1:
            out, dxn, lossp = _make_outproj_loss(seq, TL)(oa, ob, oc, wo[l], xc, p["qn"], loss_target[0])
            xn = None
        else:
            out, xn = _make_outproj(seq, TL)(oa, ob, oc, wo[l], xc, p["qn"])
        saved.append(dict(p=p, x=xc, ht=ht, spre=spre, sy=sy, gpre=gpre, pg=pg, ps=ps, pr=pr, gs=gs, ss=ss, stg=stg, tig=tig, uwg=uwg, sts=sts, str=str_,
                          oa=oa, ob=ob, oc=oc, out=out))
        xc = xn

    small = [None] * DEPTH
    gin, gin_b, gout, q_in, q_out, s_in, s_out, t_in, t_out = ([None] * DEPTH for _ in range(9))
    chip1 = chip.astype(jnp.int32).reshape(1)

    def sum_chips(l):
        return (_sum_chips(gin[l][:, None], q_in[l][:, None], chip1, f"sum_chips_w_in{l}"),
                _sum_chips(gout[l][:, None], q_out[l][:, None], chip1, f"sum_chips_w_out{l}"))

    for l in reversed(range(DEPTH)):
        s = saved[l]
        p = s["p"]
        doa, dob, doc, dqn, dwo_l = _make_outproj_bwd(seq, TL_OB)(dxn, s["out"], s["oa"], s["ob"], s["oc"], wo[l], p["qn"])
        gout[l] = dwo_l.reshape(4, 512, D_MODEL)
        gdn_args = (s["pg"], s["gpre"], s["gs"], p["g_cw"], p["g_prm"], p["g_nw"], s["stg"], s["tig"], s["uwg"], doa)
        if l == 0:
            payload = (gout[0].astype(BF16),)
            dpg, dgs, dcw_g, dprm_g, dnw_g, q_out[0] = _make_gdn_bwd(seq, TB)(
                *gdn_args, comm=_ChipExchange("scatter", payload), comm_args=payload)
        else:
            dpg, dgs, dcw_g, dprm_g, dnw_g = _make_gdn_bwd(seq, TB)(*gdn_args)
        ssd_args = (s["ps"], s["spre"], s["sy"], s["ss"], p["s_cw"], p["s_cb"], p["s_prm"], p["s_nw"], s["sts"], dob)
        if l == 0:
            payload = (gin_b[1], gout[1].astype(BF16))
            dps, dss, dcw_s, dcb_s, dprm_s, dnw_s, q_in[1], q_out[1] = _make_ssd_bwd(seq, TB)(
                *ssd_args, comm=_ChipExchange("scatter", payload), comm_args=payload)
        else:
            dps, dss, dcw_s, dcb_s, dprm_s, dnw_s = _make_ssd_bwd(seq, TB)(*ssd_args)
        dpr, dnw_r = _make_ret_bwd(seq, TB_RET)(s["pr"], rope_c, rope_s, p["r_nw"], s["str"], doc)
        dws = [_make_inproj_bwd_dw(seq, TK, d.shape[1], tn, f"inproj_bwd_dw{i}")(s["ht"], d)
               for i, (d, tn) in enumerate(((dpg, 2048), (dps, 1280), (dpr, 2048),
                                            (jnp.concatenate([dgs, dss], axis=1), 256)))]
        gin[l], gin_b[l] = _unlayout_dw_in(*dws)
        dx_args = (dpg, dps, dpr, dgs, dss, wp[l], s["x"], p["pn"], dxn)
        if l == 0:
            s_in[1], s_out[1] = sum_chips(1)
            payload, swap = (gin_b[0],), (s_in[1], s_out[1])
            dx, dpn, q_in[0], t_in[1], t_out[1] = _make_inproj_bwd_dx(seq, TL_IN)(
                *dx_args, comm=_ChipExchange("scatter", payload, swap), comm_args=payload + swap)
        else:
            dx, dpn = _make_inproj_bwd_dx(seq, TL_IN)(*dx_args)
        small[l] = [dpn[0], dqn[0], dcw_g[0:4].reshape(-1), dprm_g[0, 4:8], dprm_g[1, 4:8], dnw_g[0],
                    dcw_s[0:4].reshape(-1), dcb_s[0], dprm_s[0, 0:16], dprm_s[1, 0:16], dprm_s[2, 0:16],
                    dnw_s[0], dnw_r[0]]
        dxn = dx
    grad_x = dxn[None]

    sizes = [a.shape[0] for a in small[0]]
    flat = jnp.concatenate(small[0] + small[1] + [lossp[0, 0:1]])
    n_flat = flat.shape[0]
    rows = -(-n_flat // 1024) * 8
    red = _allreduce_small(jnp.pad(flat, (0, rows * 128 - n_flat)).reshape(rows, 128), "allreduce_small").reshape(-1)
    per = sum(sizes)
    loss = red[2 * per]

    def pick(i):
        off = sum(sizes[:i])
        return jnp.stack([red[l * per + off:l * per + off + sizes[i]] for l in range(DEPTH)])

    g_small = dict(
        pre_norm=pick(0), post_norm=pick(1),
        gdn_conv=lax.dynamic_slice_in_dim(pick(2).reshape(DEPTH, CONV_W, 1536), chip * 384, 384, axis=2),
        gdn_A_log=pick(3), gdn_dt_bias=pick(4), gdn_norm=pick(5),
        ssd_conv=lax.dynamic_slice_in_dim(pick(6).reshape(DEPTH, CONV_W, 1536), chip * 384, 384, axis=2),
        ssd_conv_b=pick(7), ssd_A_log=pick(8), ssd_dt_bias=pick(9), ssd_D=pick(10), ssd_norm=pick(11),
        ret_norm=pick(12))

    s_in[0], s_out[0] = sum_chips(0)
    t_in[0], t_out[0] = _swap_sibling([s_in[0], s_out[0]], "swap_grads")

    weights = dict(pre_norm=pre_norm, post_norm=post_norm, w_in=w_in, gdn_conv=gdn_conv, gdn_A_log=gdn_A_log,
                   gdn_dt_bias=gdn_dt_bias, gdn_norm=gdn_norm, ssd_conv=ssd_conv, ssd_conv_b=ssd_conv_b,
                   ssd_A_log=ssd_A_log, ssd_dt_bias=ssd_dt_bias, ssd_D=ssd_D, ssd_norm=ssd_norm, ret_norm=ret_norm,
                   w_out=w_out)
    ms = dict(pre_norm=m_pre_norm, post_norm=m_post_norm, w_in=m_w_in, gdn_conv=m_gdn_conv, gdn_A_log=m_gdn_A_log,
              gdn_dt_bias=m_gdn_dt_bias, gdn_norm=m_gdn_norm, ssd_conv=m_ssd_conv, ssd_conv_b=m_ssd_conv_b,
              ssd_A_log=m_ssd_A_log, ssd_dt_bias=m_ssd_dt_bias, ssd_D=m_ssd_D, ssd_norm=m_ssd_norm,
              ret_norm=m_ret_norm, w_out=m_w_out)
    vs = dict(pre_norm=v_pre_norm, post_norm=v_post_norm, w_in=v_w_in, gdn_conv=v_gdn_conv, gdn_A_log=v_gdn_A_log,
              gdn_dt_bias=v_gdn_dt_bias, gdn_norm=v_gdn_norm, ssd_conv=v_ssd_conv, ssd_conv_b=v_ssd_conv_b,
              ssd_A_log=v_ssd_A_log, ssd_dt_bias=v_ssd_dt_bias, ssd_D=v_ssd_D, ssd_norm=v_ssd_norm,
              ret_norm=v_ret_norm, w_out=v_w_out)
    names = list(weights)
    res = {}
    for nme in names:
        if nme == "w_in":
            res[nme] = _adamw_pairs(w_in, s_in, t_in, m_w_in, v_w_in, "adamw_w_in")
        elif nme == "w_out":
            res[nme] = _adamw_pairs(w_out, s_out, t_out, m_w_out, v_w_out, "adamw_w_out")
        else:
            res[nme] = _adamw(weights[nme], g_small[nme], ms[nme], vs[nme], "adamw_" + nme)
    return (loss, grad_x, *[res[n][0] for n in names], *[res[n][1] for n in names],
            *[res[n][2] for n in names], *[res[n][3] for n in names])
```

```python
import math

import jax
import jax.numpy as jnp
from jax import lax
from jax.experimental import pallas as pl
from jax.experimental.pallas import tpu as pltpu

F32 = jnp.float32
BF16 = jnp.bfloat16

D_MODEL = 1024
DEPTH = 2
CH = 64
CONV_W = 4
EPS = 1e-6
GDN_H, GDN_D = 4, 128
SSD_H, SSD_P, SSD_N, SSD_G = 16, 64, 128, 2
SSD_W = SSD_H * SSD_P
RET_H, RET_D = 4, 128
ROPE_BASE = 10000.0
N_IN = 6680
NEG = -1e30

V7X_VMEM_BYTES = 64 * 1024 * 1024
VMEM_LIMIT = V7X_VMEM_BYTES * 7 // 8


def _dot(a, b):
    return jnp.dot(a.astype(BF16), b.astype(BF16), preferred_element_type=F32)


def _dot_nt(a, b):
    return lax.dot_general(a.astype(BF16), b.astype(BF16), (((1,), (1,)), ((), ())), preferred_element_type=F32)


def _dot_tn(a, b):
    return lax.dot_general(a.astype(BF16), b.astype(BF16), (((0,), (0,)), ((), ())), preferred_element_type=F32)


def _split(a):
    hi = a.astype(BF16)
    return hi, (a - hi.astype(F32)).astype(BF16)


def _dot01l(m, v):
    vh, vl = _split(v)
    mb = m.astype(BF16)
    return jnp.dot(mb, vh, preferred_element_type=F32) + jnp.dot(mb, vl, preferred_element_type=F32)


def _dot01r(v, m):
    vh, vl = _split(v)
    mb = m.astype(BF16)
    return jnp.dot(vh, mb, preferred_element_type=F32) + jnp.dot(vl, mb, preferred_element_type=F32)


def _sigmoid(x):
    return jax.nn.sigmoid(x)


def _silu(x):
    return x * _sigmoid(x)


def _dsilu(x):
    s = _sigmoid(x)
    return s * (1.0 + x * (1.0 - s))


def _softplus(x):
    return jnp.maximum(x, 0.0) + jnp.log1p(jnp.exp(-jnp.abs(x)))


def _iota2(shape, dim):
    return lax.broadcasted_iota(jnp.int32, shape, dim)


def _chunk_tri(tb, upper=False):
    r = _iota2((tb, tb), 0)
    c = _iota2((tb, tb), 1)
    same = jnp.right_shift(r, 6) == jnp.right_shift(c, 6)
    return (same & ((c >= r) if upper else (c <= r))).astype(F32)


def _masks():
    r = _iota2((CH, CH), 0)
    c = _iota2((CH, CH), 1)
    return r >= c, r > c, (r == c).astype(F32)


def _put_lane(col, lane_idx, width=128):
    lane = _iota2((col.shape[0], width), 1)
    return jnp.where(lane == lane_idx, col, 0.0)


def _conv_taps(raw, halo8, tb):
    ext = jnp.concatenate([halo8, raw], axis=0)
    return [raw] + [pltpu.roll(ext, s, axis=0)[8:] for s in (1, 2, 3)]


def _conv_back(dpre, nxt8, tb):
    ext = jnp.concatenate([dpre, nxt8], axis=0)
    return [dpre] + [pltpu.roll(ext, tb + 8 - s, axis=0)[:tb] for s in (1, 2, 3)]


def _rms_fwd(o, w, n):
    r = lax.rsqrt(jnp.sum(o * o, axis=-1, keepdims=True) * (1.0 / n) + EPS)
    on = o * r
    return on, r, on * w


def _rms_bwd(dy, on, r, w, n):
    don = dy * w
    return r * (don - on * (jnp.sum(don * on, axis=-1, keepdims=True) * (1.0 / n))), dy * on


def _put_cols(v, g, gw):
    z = jnp.zeros_like(v)
    return jnp.concatenate([v, z] if g == 0 else [z, v], axis=1)


def _gdn_common(pg_ref, halo8, sm, cw, prm, tb, pre=None):
    raw = pg_ref[:, 0:1536]
    if pre is None:
        taps = _conv_taps(raw, halo8, tb)
        pre = taps[0] * cw[3:4, :] + taps[1] * cw[2:3, :] + taps[2] * cw[1:2, :] + taps[3] * cw[0:1, :]
    act = _silu(pre)
    beta = _sigmoid(sm)
    sp_in = sm + prm[1:2, :]
    g = -jnp.exp(prm[0:1, :]) * _softplus(sp_in)
    gc = _dot01l(_chunk_tri(tb), g)
    return raw, pre, act, beta, sp_in, g, gc


_NN = (((2,), (1,)), ((0,), (0,)))
_NT = (((2,), (2,)), ((0,), (0,)))
_TN = (((1,), (1,)), ((0,), (0,)))


def _bdot(a, b, dn):
    return lax.dot_general(a.astype(BF16), b.astype(BF16), dn, preferred_element_type=F32)


def _binv_unit_lower(a, eye):
    r = _iota2((CH, CH), 0)
    c = _iota2((CH, CH), 1)
    d = eye - jnp.where((jnp.right_shift(r, 1) == jnp.right_shift(c, 1)), a, 0.0)
    ab = a.astype(BF16)
    zero = jnp.zeros((), BF16)
    for lb in range(1, 6):
        same = jnp.right_shift(r, lb + 1) == jnp.right_shift(c, lb + 1)
        low = (jnp.bitwise_and(jnp.right_shift(r, lb), 1) == 1) & (jnp.bitwise_and(jnp.right_shift(c, lb), 1) == 0)
        db = d.astype(BF16)
        t = _bdot(jnp.where(same & low, ab, zero), db, _NN)
        d = d - _bdot(db, t, _NN)
    return d


def _rsum(v):
    return jnp.sum(v, axis=-1, keepdims=True)


def _gdn_batch(act, beta, gc, gct, eg_all, ncb, masks):
    causal, strict, _ = masks

    def st(fn):
        return jnp.stack([fn(c, h, slice(c * CH, (c + 1) * CH)) for c in range(ncb) for h in range(GDN_H)])

    qr = st(lambda c, h, r: act[r, h * 128:(h + 1) * 128])
    kr = st(lambda c, h, r: act[r, 512 + h * 128:512 + (h + 1) * 128])
    vh = st(lambda c, h, r: act[r, 1024 + h * 128:1024 + (h + 1) * 128])
    bh = st(lambda c, h, r: beta[r, h:h + 1])
    gcol = st(lambda c, h, r: gc[r, 4 + h:5 + h])
    grow = st(lambda c, h, r: gct[4 + h:5 + h, r])
    eg = st(lambda c, h, r: eg_all[r, 4 + h:5 + h])
    glast = st(lambda c, h, r: gc[(c + 1) * CH - 1:(c + 1) * CH, 4 + h:5 + h])
    rq = lax.rsqrt(_rsum(qr * qr) + EPS)
    rk = lax.rsqrt(_rsum(kr * kr) + EPS)
    qn = qr * rq
    kh = kr * rk
    qh = qn * (GDN_D ** -0.5)
    decay = jnp.exp(jnp.where(causal, gcol - grow, NEG))
    kb = kh * bh
    kd_scale = jnp.exp(glast - gcol)
    return dict(qn=qn, rq=rq, kh=kh, rk=rk, qh=qh, vh=vh, bh=bh, eg=eg, decay=decay, kb=kb, vb=vh * bh, kg=kb * eg,
                qg=qh * eg, kd_scale=kd_scale, kdec=kh * kd_scale, egl=jnp.exp(glast),
                a=jnp.where(strict, _bdot(kb, kh, _NT) * decay, 0.0), attn=_bdot(qh, kh, _NT) * decay)


def _make_gdn_fwd(seq, tb):
    ncb = tb // CH
    nb = seq // tb
    n = ncb * GDN_H

    def body(pg_ref, sm_ref, cw_ref, prm_ref, nw_ref, oa_ref, st_ref, ti_ref, uw_ref, pre_ref, s_scr, halo_scr):
        @pl.when(pl.program_id(0) == 0)
        def _():
            s_scr[...] = jnp.zeros_like(s_scr)
            halo_scr[...] = jnp.zeros_like(halo_scr)

        masks = _masks()
        sm = sm_ref[...]
        raw, pre, act, beta, _, _, gc = _gdn_common(pg_ref, halo_scr[...], sm, cw_ref[...], prm_ref[...], tb)
        halo_scr[...] = raw[tb - 8:tb, :]
        pre_ref[...] = pre
        d = _gdn_batch(act, beta, gc, gc.T, jnp.exp(gc), ncb, masks)
        t = _binv_unit_lower(d["a"], masks[2])
        sol = _bdot(t, jnp.concatenate([d["vb"], d["kg"]], axis=2), _NN)
        ti_ref[...] = t.reshape(ncb, GDN_H, CH, CH)
        uw_ref[...] = sol.reshape(ncb, GDN_H, CH, 256)
        u, w = sol[:, :, :128], sol[:, :, 128:]
        vns = []
        for c in range(ncb):
            bs = slice(c * GDN_H, (c + 1) * GDN_H)
            s = s_scr[...]
            st_ref[c] = s
            vn = u[bs] - _bdot(w[bs], s, _NN)
            s_scr[...] = s * d["egl"][bs] + _bdot(d["kdec"][bs], vn, _TN)
            vns.append(vn)
        v_new = jnp.concatenate(vns, axis=0)
        s_prev = st_ref[...].reshape(n, 128, 128)
        o = _bdot(d["qg"], s_prev, _NN) + _bdot(d["attn"], v_new, _NN)
        _, _, y = _rms_fwd(o, nw_ref[0:1, :], GDN_D)
        for c in range(ncb):
            rows = slice(c * CH, (c + 1) * CH)
            for h in range(GDN_H):
                z = pg_ref[rows, 1536 + h * 128:1536 + (h + 1) * 128]
                oa_ref[rows, h * 128:(h + 1) * 128] = (y[c * GDN_H + h] * _silu(z)).astype(oa_ref.dtype)

    def call(pg, sm, cw, prm, nw, comm=None, comm_args=()):
        blk4 = lambda i: (i, 0, 0, 0)
        cx = _exchange_specs(comm)
        return pl.pallas_call(
            _with_exchange(body, comm, 5, 5, nb),
            grid=(nb,),
            in_specs=[
                pl.BlockSpec((tb, 2048), lambda i: (i, 0)),
                pl.BlockSpec((tb, 128), lambda i: (i, 0)),
                pl.BlockSpec((8, 1536), lambda i: (0, 0)),
                pl.BlockSpec((8, 128), lambda i: (0, 0)),
                pl.BlockSpec((8, 128), lambda i: (0, 0)),
            ] + cx["specs"],
            out_specs=[
                pl.BlockSpec((tb, 512), lambda i: (i, 0)),
                pl.BlockSpec((ncb, GDN_H, 128, 128), blk4),
                pl.BlockSpec((ncb, GDN_H, CH, CH), blk4),
                pl.BlockSpec((ncb, GDN_H, CH, 256), blk4),
                pl.BlockSpec((tb, 1536), lambda i: (i, 0)),
            ] + cx["specs"],
            out_shape=[
                jax.ShapeDtypeStruct((seq, 512), BF16),
                jax.ShapeDtypeStruct((seq // CH, GDN_H, 128, 128), F32),
                jax.ShapeDtypeStruct((seq // CH, GDN_H, CH, CH), F32),
                jax.ShapeDtypeStruct((seq // CH, GDN_H, CH, 256), F32),
                jax.ShapeDtypeStruct((seq, 1536), F32),
            ] + cx["out_shape"],
            scratch_shapes=[pltpu.VMEM((GDN_H, 128, 128), F32), pltpu.VMEM((8, 1536), F32)] + cx["scratch"],
            compiler_params=pltpu.CompilerParams(dimension_semantics=("arbitrary",), vmem_limit_bytes=VMEM_LIMIT,
                                                 has_side_effects=comm is not None),
            name="gdn_fwd" + cx["tag"],
        )(pg, sm, cw, prm, nw, *comm_args)

    return call


def _make_gdn_bwd(seq, tb):
    ncb = tb // CH
    nb = seq // tb
    hb = tb // 8
    n = ncb * GDN_H

    def body(pg_ref, pre_ref, sm_ref, cw_ref, prm_ref, nw_ref, st_ref, ti_ref, uw_ref, doa_ref,
             dpg_ref, dsm_ref, dcw_ref, dprm_ref, dnw_ref, ds_scr, nxt_scr):
        i = pl.program_id(0)

        @pl.when(i == 0)
        def _():
            ds_scr[...] = jnp.zeros_like(ds_scr)
            nxt_scr[...] = jnp.zeros_like(nxt_scr)
            dcw_ref[...] = jnp.zeros_like(dcw_ref)
            dprm_ref[...] = jnp.zeros_like(dprm_ref)
            dnw_ref[...] = jnp.zeros_like(dnw_ref)

        masks = _masks()
        strict = masks[1]
        sm = sm_ref[...]
        cw = cw_ref[...]
        prm = prm_ref[...]
        raw, pre, act, beta, sp_in, g, gc = _gdn_common(pg_ref, None, sm, cw, prm, tb, pre=pre_ref[...])
        nw = nw_ref[0:1, :]
        row_id = _iota2((CH, 1), 0)
        d = _gdn_batch(act, beta, gc, gc.T, jnp.exp(gc), ncb, masks)
        t = ti_ref[...].reshape(n, CH, CH)
        sol = uw_ref[...].reshape(n, CH, 256)
        u, w = sol[:, :, :128], sol[:, :, 128:]
        s_prev = st_ref[...].reshape(n, 128, 128)
        v_new = u - _bdot(w, s_prev, _NN)
        o = _bdot(d["qg"], s_prev, _NN) + _bdot(d["attn"], v_new, _NN)

        pairs = [(c, h) for c in range(ncb) for h in range(GDN_H)]
        z = jnp.stack([pg_ref[c * CH:(c + 1) * CH, 1536 + h * 128:1536 + (h + 1) * 128] for c, h in pairs])
        doa = jnp.stack([doa_ref[c * CH:(c + 1) * CH, h * 128:(h + 1) * 128] for c, h in pairs])
        on, r, y = _rms_fwd(o, nw, GDN_D)
        dz = doa * y * _dsilu(z)
        do, dnw_rows = _rms_bwd(doa * _silu(z), on, r, nw, GDN_D)
        dnw_acc = jnp.sum(jnp.sum(dnw_rows, axis=0), axis=0, keepdims=True)

        dvn_in = _bdot(d["attn"], do, _TN)
        qgtdo = _bdot(d["qg"], do, _TN)
        dvn_l, dkdec_l, dgl_l = [None] * ncb, [None] * ncb, [None] * ncb
        for c in reversed(range(ncb)):
            bs = slice(c * GDN_H, (c + 1) * GDN_H)
            dsn = ds_scr[...]
            dvn_c = dvn_in[bs] + _bdot(d["kdec"][bs], dsn, _NN)
            ds_scr[...] = d["egl"][bs] * dsn + qgtdo[bs] - _bdot(w[bs], dvn_c, _TN)
            dvn_l[c] = dvn_c
            dkdec_l[c] = _bdot(v_new[bs], dsn, _NT)
            dgl_l[c] = d["egl"][bs] * jnp.sum(_rsum(s_prev[bs] * dsn), axis=1, keepdims=True)
        dvn = jnp.concatenate(dvn_l, axis=0)
        dkdec = jnp.concatenate(dkdec_l, axis=0)
        dglast = jnp.concatenate(dgl_l, axis=0)

        dqg = _bdot(do, s_prev, _NT)
        dattn = _bdot(do, v_new, _NT)
        dw = -_bdot(dvn, s_prev, _NT)
        drhs = _bdot(t, jnp.concatenate([dvn, dw], axis=2), _TN)
        dvb, dkg = drhs[:, :, :128], drhs[:, :, 128:]
        da = jnp.where(strict, -(_bdot(dvb, u, _NT) + _bdot(dkg, w, _NT)), 0.0)
        dp = da * d["decay"]
        dq_m = dattn * d["decay"]
        m = da * d["a"] + dattn * d["attn"]
        upper_tri = jnp.broadcast_to((_iota2((CH, CH), 1) >= _iota2((CH, CH), 0)).astype(BF16), (n, CH, CH))
        dg_in = _rsum(jnp.where(strict, _bdot(upper_tri, m, _NN), 0.0))
        dkb = _bdot(dp, d["kh"], _NN) + dkg * d["eg"]
        kdk_row = _rsum(dkdec * d["kdec"])
        dk = _bdot(dp, d["kb"], _TN) + _bdot(dq_m, d["qh"], _TN) + dkdec * d["kd_scale"] + dkb * d["bh"]
        dq = _bdot(dq_m, d["kh"], _NN) + dqg * d["eg"]
        dglast = dglast + jnp.sum(kdk_row, axis=1, keepdims=True)
        dgcol = (_rsum(dqg * d["qg"]) + _rsum(dkg * d["kg"]) - kdk_row + jnp.where(row_id == CH - 1, dglast, 0.0))
        dbeta = _rsum(dkb * d["kh"]) + _rsum(dvb * d["vh"])
        dn = dq * (GDN_D ** -0.5)
        dact_q = d["rq"] * (dn - d["qn"] * _rsum(dn * d["qn"]))
        dact_k = d["rk"] * (dk - d["kh"] * _rsum(dk * d["kh"]))
        dact_v = dvb * d["bh"]

        def lanes(v, lane0):
            return jnp.concatenate(
                [sum(_put_lane(v[c * GDN_H + h], lane0 + h) for h in range(GDN_H)) for c in range(ncb)], axis=0)

        def tokens(v):
            return jnp.concatenate(
                [jnp.concatenate([v[c * GDN_H + h] for h in range(GDN_H)], axis=1) for c in range(ncb)], axis=0)

        dbeta_all = lanes(dbeta, 0)
        dg = _dot01l(_chunk_tri(tb, upper=True), lanes(dgcol, 4)) + lanes(dg_in, 4)
        neg_ea = -jnp.exp(prm[0:1, :])
        da_raw = dg * neg_ea * _sigmoid(sp_in)
        db_raw = dbeta_all * beta * (1.0 - beta)
        dsm_ref[...] = (da_raw + db_raw).astype(dsm_ref.dtype)
        lane8 = _iota2((8, 128), 1)
        sub8 = _iota2((8, 128), 0)
        dalog = jnp.sum(dg * g, axis=0, keepdims=True)
        ddtb = jnp.sum(da_raw, axis=0, keepdims=True)
        dprm_ref[...] += jnp.where(sub8 == 0, dalog, 0.0) + jnp.where(sub8 == 1, ddtb, 0.0)
        dnw_ref[...] += jnp.where(sub8 == 0, dnw_acc, 0.0)

        dact = jnp.concatenate([tokens(dact_q), tokens(dact_k), tokens(dact_v)], axis=1)
        dpre = dact * _dsilu(pre)
        back = _conv_back(dpre, nxt_scr[...], tb)
        nxt_scr[...] = dpre[0:8, :]
        draw = back[0] * cw[3:4, :] + back[1] * cw[2:3, :] + back[2] * cw[1:2, :] + back[3] * cw[0:1, :]
        dpg_ref[:, 0:1536] = draw.astype(dpg_ref.dtype)
        dpg_ref[:, 1536:2048] = tokens(dz).astype(dpg_ref.dtype)
        sub_c = _iota2((8, 1536), 0)
        dcw_new = jnp.zeros((8, 1536), F32)
        for s_ in range(CONV_W):
            dcw_new = dcw_new + jnp.where(sub_c == 3 - s_, jnp.sum(back[s_] * raw, axis=0, keepdims=True), 0.0)
        dcw_ref[...] += dcw_new

    def call(pg, pre, sm, cw, prm, nw, st, ti, uw, doa, comm=None, comm_args=()):
        rev = lambda i: (nb - 1 - i, 0)
        const = lambda i: (0, 0)
        cx = _exchange_specs(comm)
        return pl.pallas_call(
            _with_exchange(body, comm, 10, 5, nb),
            grid=(nb,),
            in_specs=[
                pl.BlockSpec((tb, 2048), rev),
                pl.BlockSpec((tb, 1536), rev),
                pl.BlockSpec((tb, 128), rev),
                pl.BlockSpec((8, 1536), const),
                pl.BlockSpec((8, 128), const),
                pl.BlockSpec((8, 128), const),
                pl.BlockSpec((ncb, GDN_H, 128, 128), lambda i: (nb - 1 - i, 0, 0, 0)),
                pl.BlockSpec((ncb, GDN_H, CH, CH), lambda i: (nb - 1 - i, 0, 0, 0)),
                pl.BlockSpec((ncb, GDN_H, CH, 256), lambda i: (nb - 1 - i, 0, 0, 0)),
                pl.BlockSpec((tb, 512), rev),
            ] + cx["specs"],
            out_specs=[
                pl.BlockSpec((tb, 2048), rev),
                pl.BlockSpec((tb, 128), rev),
                pl.BlockSpec((8, 1536), const),
                pl.BlockSpec((8, 128), const),
                pl.BlockSpec((8, 128), const),
            ] + cx["specs"],
            out_shape=[
                jax.ShapeDtypeStruct((seq, 2048), BF16),
                jax.ShapeDtypeStruct((seq, 128), BF16),
                jax.ShapeDtypeStruct((8, 1536), F32),
                jax.ShapeDtypeStruct((8, 128), F32),
                jax.ShapeDtypeStruct((8, 128), F32),
            ] + cx["out_shape"],
            scratch_shapes=[pltpu.VMEM((GDN_H, 128, 128), F32), pltpu.VMEM((8, 1536), F32)] + cx["scratch"],
            compiler_params=pltpu.CompilerParams(dimension_semantics=("arbitrary",), vmem_limit_bytes=VMEM_LIMIT,
                                                 has_side_effects=comm is not None),
            name="gdn_bwd" + cx["tag"],
        )(pg, pre, sm, cw, prm, nw, st, ti, uw, doa, *comm_args)

    return call


def _expand_mat():
    r = _iota2((128, SSD_W), 0)
    c = _iota2((128, SSD_W), 1)
    return (jnp.right_shift(c, 6) == r).astype(F32)


def _reduce_heads(v, e):
    vh, vl = _split(v)
    eb = e.astype(BF16)
    nt = (((1,), (1,)), ((), ()))
    return (lax.dot_general(vh, eb, nt, preferred_element_type=F32)
            + lax.dot_general(vl, eb, nt, preferred_element_type=F32))


def _reduce_heads1(v, e):
    nt = (((1,), (1,)), ((), ()))
    return lax.dot_general(v.astype(BF16), e.astype(BF16), nt, preferred_element_type=F32)


def _row8(v):
    return jnp.broadcast_to(v, (8, v.shape[1]))


def _ssd_common(ps_ref, halo8, ss, cw, cb, prm, tb, pre=None):
    raw = ps_ref[:, 0:1536]
    taps = None
    if pre is None:
        taps = _conv_taps(raw, halo8, tb)
        pre = taps[0] * cw[3:4, :] + taps[1] * cw[2:3, :] + taps[2] * cw[1:2, :] + taps[3] * cw[0:1, :] + cb[0:1, :]
    act = _silu(pre)
    dt_in = ss + prm[1:2, :]
    dt = _softplus(dt_in)
    a = dt * (-jnp.exp(prm[0:1, :]))
    acum = _dot01l(_chunk_tri(tb), a)
    e = _expand_mat()
    dt_e = _dot01r(dt, e)
    xdt = act[:, 0:SSD_W] * dt_e
    ea_e = _dot01r(jnp.exp(acum), e)
    d_e = _dot01r(_row8(prm[2:3, :]), e)[0:1, :]
    return raw, taps, pre, act, dt_in, dt, a, acum, e, dt_e, xdt, ea_e, d_e


def _ssd_chunk(act, acum, act_t, e, c):
    r0 = c * CH
    rows = slice(r0, r0 + CH)
    alast = acum[r0 + CH - 1:r0 + CH, :]
    wdec = jnp.exp(alast - acum[rows, :])
    wd_e = _dot01r(wdec, e)
    eal_e = _dot01r(_row8(jnp.exp(alast)), e)[0:1, :]
    return rows, wd_e, eal_e


def _ssd_lmat(acum, act_t, c, h, causal):
    r0 = c * CH
    acol = acum[r0:r0 + CH, h:h + 1]
    arow = act_t[h:h + 1, r0:r0 + CH]
    return jnp.exp(jnp.where(causal, acol - arow, NEG))


def _make_ssd_fwd(seq, tb):
    ncb = tb // CH
    nb = seq // tb
    hg = SSD_H // SSD_G
    gw = SSD_W // SSD_G

    def body(ps_ref, ss_ref, cw_ref, cb_ref, prm_ref, nw_ref, ob_ref, st_ref, pre_ref, y_ref, hs_scr, halo_scr):
        @pl.when(pl.program_id(0) == 0)
        def _():
            hs_scr[...] = jnp.zeros_like(hs_scr)
            halo_scr[...] = jnp.zeros_like(halo_scr)

        causal, _, _ = _masks()
        (raw, _, pre, act, _, _, _, acum, e, _, xdt, ea_e, d_e) = _ssd_common(
            ps_ref, halo_scr[...], ss_ref[...], cw_ref[...], cb_ref[...], prm_ref[...], tb)
        halo_scr[...] = raw[tb - 8:tb, :]
        pre_ref[...] = pre
        act_t = acum.T
        nw = nw_ref[0:1, :]
        for c in range(ncb):
            rows, wd_e, eal_e = _ssd_chunk(act, acum, act_t, e, c)
            st_ref[c] = hs_scr[...]
            ys = []
            for g in range(SSD_G):
                gc_ = slice(g * gw, (g + 1) * gw)
                bg = act[rows, SSD_W + g * 128:SSD_W + (g + 1) * 128]
                cg = act[rows, SSD_W + 256 + g * 128:SSD_W + 256 + (g + 1) * 128]
                cbm = _dot_nt(cg, bg)
                hs = hs_scr[:, gc_]
                yin = _dot(cg, hs)
                yh = []
                for hh in range(hg):
                    h = g * hg + hh
                    lm = _ssd_lmat(acum, act_t, c, h, causal)
                    yh.append(_dot(cbm * lm, xdt[rows, h * SSD_P:(h + 1) * SSD_P]))
                ys.append(jnp.concatenate(yh, axis=1) + yin * ea_e[rows, gc_])
                hs_scr[:, gc_] = hs * eal_e[:, gc_] + _dot_tn(bg, xdt[rows, gc_] * wd_e[:, gc_])
            y = jnp.concatenate(ys, axis=1) + act[rows, 0:SSD_W] * d_e
            y_ref[rows, :] = y
            yz = y * _silu(ps_ref[rows, 1536:2560])
            outs = [_rms_fwd(yz[:, g * gw:(g + 1) * gw], nw[:, g * gw:(g + 1) * gw], gw)[2] for g in range(SSD_G)]
            ob_ref[rows, :] = jnp.concatenate(outs, axis=1).astype(ob_ref.dtype)

    def call(ps, ss, cw, cb, prm, nw):
        const = lambda i: (0, 0)
        return pl.pallas_call(
            body,
            grid=(nb,),
            in_specs=[
                pl.BlockSpec((tb, 2560), lambda i: (i, 0)),
                pl.BlockSpec((tb, 128), lambda i: (i, 0)),
                pl.BlockSpec((8, 1536), const),
                pl.BlockSpec((8, 1536), const),
                pl.BlockSpec((8, 128), const),
                pl.BlockSpec((8, SSD_W), const),
            ],
            out_specs=[
                pl.BlockSpec((tb, SSD_W), lambda i: (i, 0)),
                pl.BlockSpec((ncb, SSD_N, SSD_W), lambda i: (i, 0, 0)),
                pl.BlockSpec((tb, 1536), lambda i: (i, 0)),
                pl.BlockSpec((tb, SSD_W), lambda i: (i, 0)),
            ],
            out_shape=[
                jax.ShapeDtypeStruct((seq, SSD_W), BF16),
                jax.ShapeDtypeStruct((seq // CH, SSD_N, SSD_W), F32),
                jax.ShapeDtypeStruct((seq, 1536), F32),
                jax.ShapeDtypeStruct((seq, SSD_W), F32),
            ],
            scratch_shapes=[pltpu.VMEM((SSD_N, SSD_W), F32), pltpu.VMEM((8, 1536), F32)],
            compiler_params=pltpu.CompilerParams(dimension_semantics=("arbitrary",), vmem_limit_bytes=VMEM_LIMIT),
            name="ssd_fwd",
        )(ps, ss, cw, cb, prm, nw)

    return call


def _make_ssd_bwd(seq, tb):
    ncb = tb // CH
    nb = seq // tb
    hb = tb // 8
    hg = SSD_H // SSD_G
    gw = SSD_W // SSD_G

    def body(ps_ref, pre_ref, y_ref, ss_ref, cw_ref, cb_ref, prm_ref, nw_ref, st_ref, dob_ref,
             dps_ref, dss_ref, dcw_ref, dcb_ref, dprm_ref, dnw_ref, dhs_scr, nxt_scr):
        i = pl.program_id(0)

        @pl.when(i == 0)
        def _():
            dhs_scr[...] = jnp.zeros_like(dhs_scr)
            nxt_scr[...] = jnp.zeros_like(nxt_scr)
            dcw_ref[...] = jnp.zeros_like(dcw_ref)
            dcb_ref[...] = jnp.zeros_like(dcb_ref)
            dprm_ref[...] = jnp.zeros_like(dprm_ref)
            dnw_ref[...] = jnp.zeros_like(dnw_ref)

        causal, _, _ = _masks()
        cw = cw_ref[...]
        prm = prm_ref[...]
        (raw, _, pre, act, dt_in, dt, a, acum, e, dt_e, xdt, ea_e, d_e) = _ssd_common(
            ps_ref, None, ss_ref[...], cw, cb_ref[...], prm, tb, pre=pre_ref[...])
        act_t = acum.T
        nw = nw_ref[0:1, :]

        dx_l, db_l, dc_l, dz_l, ddt_l, da_l = ([None] * ncb for _ in range(6))
        upper_tri = (_iota2((CH, CH), 1) >= _iota2((CH, CH), 0)).astype(F32)
        tri_pair = jnp.concatenate([upper_tri, (_iota2((CH, CH), 1) < _iota2((CH, CH), 0)).astype(F32)], axis=1)
        below = jnp.bitwise_and(_iota2((CH, gw), 1), CH - 1) < _iota2((CH, gw), 0)
        dnw_acc = jnp.zeros((1, SSD_W), F32)
        dd_acc = jnp.zeros((1, SSD_W), F32)

        for c in reversed(range(ncb)):
            rows, wd_e, eal_e = _ssd_chunk(act, acum, act_t, e, c)
            xc = act[rows, 0:SSD_W]
            z = ps_ref[rows, 1536:2560]
            dob = dob_ref[rows, :]
            sz = _silu(z)
            dy_g, dz_g, dxdt_g, db_g, dc_g, da_g = [], [], [], [], [], []
            for g in range(SSD_G):
                gc_ = slice(g * gw, (g + 1) * gw)
                bg = act[rows, SSD_W + g * 128:SSD_W + (g + 1) * 128]
                cg = act[rows, SSD_W + 256 + g * 128:SSD_W + 256 + (g + 1) * 128]
                cbm = _dot_nt(cg, bg)
                hs = st_ref[c, :, gc_]
                yin = _dot(cg, hs)
                lmats = [_ssd_lmat(acum, act_t, c, g * hg + hh, causal) for hh in range(hg)]
                ea_g = ea_e[rows, gc_]
                y = y_ref[rows, gc_]
                yz = y * sz[:, gc_]
                on, r, _ = _rms_fwd(yz, nw[:, gc_], gw)
                dyz, dnw_rows = _rms_bwd(dob[:, gc_], on, r, nw[:, gc_], gw)
                dnw_acc = dnw_acc + _put_cols(jnp.sum(dnw_rows, axis=0, keepdims=True), g, gw)
                dy = dyz * sz[:, gc_]
                dz_g.append(dyz * y * _dsilu(z[:, gc_]))
                dd_acc = dd_acc + _put_cols(jnp.sum(dy * xc[:, gc_], axis=0, keepdims=True), g, gw)
                dhs_n = dhs_scr[:, gc_]
                dyin = dy * ea_g
                dcg = _dot_nt(dyin, hs)
                xw = xdt[rows, gc_] * wd_e[:, gc_]
                dbg = _dot_nt(xw, dhs_n)
                dxw = _dot(bg, dhs_n)
                dhs_scr[:, gc_] = dhs_n * eal_e[:, gc_] + _dot_tn(cg, dyin)
                dxi, ms, dcbm = [], [], jnp.zeros((CH, CH), F32)
                for hh in range(hg):
                    h = g * hg + hh
                    hc = slice(hh * SSD_P, (hh + 1) * SSD_P)
                    dyh = dy[:, hc]
                    lm = cbm * lmats[hh]
                    dxi.append(_dot_tn(lm, dyh))
                    dlm = _dot_nt(dyh, xdt[rows, h * SSD_P:(h + 1) * SSD_P])
                    ms.append(dlm * lm)
                    dcbm = dcbm + dlm * lmats[hh]
                dx_intra = jnp.concatenate(dxi, axis=1)
                ncat = _dot(upper_tri, jnp.concatenate(ms, axis=1))
                cum = _dot(tri_pair, jnp.concatenate([dy * yin * ea_g, dxw * xw], axis=0))
                da_g.append(jnp.where(below, ncat, 0.0) + cum
                            + jnp.sum(hs * dhs_n, axis=0, keepdims=True) * eal_e[:, gc_])
                dxdt_g.append(dx_intra + dxw * wd_e[:, gc_])
                dy_g.append(dy)
                db_g.append(dbg + _dot_tn(dcbm, cg))
                dc_g.append(dcg + _dot(dcbm, bg))
            dy = jnp.concatenate(dy_g, axis=1)
            dxdt = jnp.concatenate(dxdt_g, axis=1)
            dx_l[c] = dxdt * dt_e[rows, :] + dy * d_e
            db_l[c] = jnp.concatenate(db_g, axis=1)
            dc_l[c] = jnp.concatenate(dc_g, axis=1)
            dz_l[c] = jnp.concatenate(dz_g, axis=1)
            ddt_l[c] = _reduce_heads1(dxdt * xc, e)
            da_l[c] = _reduce_heads1(jnp.concatenate(da_g, axis=1), e)

        da = jnp.concatenate(da_l, axis=0)
        neg_ea = -jnp.exp(prm[0:1, :])
        ddt = jnp.concatenate(ddt_l, axis=0) + da * neg_ea
        ddt_in = ddt * _sigmoid(dt_in)
        dss_ref[...] = ddt_in.astype(dss_ref.dtype)
        sub8 = _iota2((8, 128), 0)
        dalog = jnp.sum(da * a, axis=0, keepdims=True)
        ddtb = jnp.sum(ddt_in, axis=0, keepdims=True)
        dd = _reduce_heads(_row8(dd_acc), e)[0:1, :]
        dprm_ref[...] += (jnp.where(sub8 == 0, dalog, 0.0) + jnp.where(sub8 == 1, ddtb, 0.0)
                          + jnp.where(sub8 == 2, dd, 0.0))
        dnw_ref[...] += jnp.where(_iota2((8, SSD_W), 0) == 0, dnw_acc, 0.0)

        dact = jnp.concatenate([jnp.concatenate(dx_l, axis=0), jnp.concatenate(db_l, axis=0),
                                jnp.concatenate(dc_l, axis=0)], axis=1)
        dpre = dact * _dsilu(pre)
        back = _conv_back(dpre, nxt_scr[...], tb)
        nxt_scr[...] = dpre[0:8, :]
        draw = back[0] * cw[3:4, :] + back[1] * cw[2:3, :] + back[2] * cw[1:2, :] + back[3] * cw[0:1, :]
        dps_ref[:, 0:1536] = draw.astype(dps_ref.dtype)
        dps_ref[:, 1536:2560] = jnp.concatenate(dz_l, axis=0).astype(dps_ref.dtype)
        sub_c = _iota2((8, 1536), 0)
        dcw_new = jnp.zeros((8, 1536), F32)
        for s_ in range(CONV_W):
            dcw_new = dcw_new + jnp.where(sub_c == 3 - s_, jnp.sum(back[s_] * raw, axis=0, keepdims=True), 0.0)
        dcw_ref[...] += dcw_new
        dcb_ref[...] += jnp.where(sub_c == 0, jnp.sum(dpre, axis=0, keepdims=True), 0.0)

    def call(ps, pre, y, ss, cw, cb, prm, nw, st, dob, comm=None, comm_args=()):
        rev = lambda i: (nb - 1 - i, 0)
        const = lambda i: (0, 0)
        cx = _exchange_specs(comm)
        return pl.pallas_call(
            _with_exchange(body, comm, 10, 6, nb),
            grid=(nb,),
            in_specs=[
                pl.BlockSpec((tb, 2560), rev),
                pl.BlockSpec((tb, 1536), rev),
                pl.BlockSpec((tb, SSD_W), rev),
                pl.BlockSpec((tb, 128), rev),
                pl.BlockSpec((8, 1536), const),
                pl.BlockSpec((8, 1536), const),
                pl.BlockSpec((8, 128), const),
                pl.BlockSpec((8, SSD_W), const),
                pl.BlockSpec((ncb, SSD_N, SSD_W), lambda i: (nb - 1 - i, 0, 0)),
                pl.BlockSpec((tb, SSD_W), rev),
            ] + cx["specs"],
            out_specs=[
                pl.BlockSpec((tb, 2560), rev),
                pl.BlockSpec((tb, 128), rev),
                pl.BlockSpec((8, 1536), const),
                pl.BlockSpec((8, 1536), const),
                pl.BlockSpec((8, 128), const),
                pl.BlockSpec((8, SSD_W), const),
            ] + cx["specs"],
            out_shape=[
                jax.ShapeDtypeStruct((seq, 2560), BF16),
                jax.ShapeDtypeStruct((seq, 128), BF16),
                jax.ShapeDtypeStruct((8, 1536), F32),
                jax.ShapeDtypeStruct((8, 1536), F32),
                jax.ShapeDtypeStruct((8, 128), F32),
                jax.ShapeDtypeStruct((8, SSD_W), F32),
            ] + cx["out_shape"],
            scratch_shapes=[pltpu.VMEM((SSD_N, SSD_W), F32), pltpu.VMEM((8, 1536), F32)] + cx["scratch"],
            compiler_params=pltpu.CompilerParams(dimension_semantics=("arbitrary",), vmem_limit_bytes=VMEM_LIMIT,
                                                 has_side_effects=comm is not None),
            name="ssd_bwd" + cx["tag"],
        )(ps, pre, y, ss, cw, cb, prm, nw, st, dob, *comm_args)

    return call


def _ret_consts(h):
    lg = math.log(1.0 - 2.0 ** (-5.0 - h))
    r = _iota2((CH, CH), 0)
    c = _iota2((CH, CH), 1)
    rel = (r - c).astype(F32)
    dmat = jnp.where(r >= c, jnp.exp(jnp.maximum(rel, 0.0) * lg), 0.0)
    idx = _iota2((CH, 1), 0).astype(F32)
    qdec = jnp.exp((idx + 1.0) * lg)
    kdec = jnp.exp((CH - 1.0 - idx) * lg)
    cdec = math.exp(CH * lg)
    return dmat, qdec, kdec, cdec


def _ret_batch(pr_ref, cc_ref, ss_ref, ncb):
    pairs = [(c, h) for c in range(ncb) for h in range(RET_H)]

    def st(off):
        return jnp.stack([pr_ref[c * CH:(c + 1) * CH, off + h * 128:off + (h + 1) * 128] for c, h in pairs])

    cc = jnp.stack([cc_ref[c * CH:(c + 1) * CH, :] for c, _ in pairs])
    ss = jnp.stack([ss_ref[c * CH:(c + 1) * CH, :] for c, _ in pairs])
    consts = [_ret_consts(h) for h in range(RET_H)]
    dmat = jnp.stack([consts[h][0] for _, h in pairs])
    qdec = jnp.stack([consts[h][1] for _, h in pairs])
    kdec = jnp.stack([consts[h][2] for _, h in pairs])
    cdec = jnp.stack([jnp.full((1, 1), consts[h][3], F32) for h in range(RET_H)])
    q = _rot(st(0), cc, ss)
    k = _rot(st(512), cc, ss) * (RET_D ** -0.5)
    return dict(q=q, k=k, v=st(1024), z=st(1536), cc=cc, ss=ss, dmat=dmat, qdec=qdec, kdec=kdec, cdec=cdec,
                s=_bdot(q, k, _NT) * dmat)


def _rot(t, cc, ss):
    return t * cc + pltpu.roll(t, 64, axis=t.ndim - 1) * ss


def _rot_bwd(d, cc, ss):
    return d * cc + pltpu.roll(d * ss, 64, axis=d.ndim - 1)


def _make_ret_fwd(seq, tb):
    ncb = tb // CH
    nb = seq // tb

    def body(pr_ref, cc_ref, ss_ref, nw_ref, oc_ref, st_ref, r_scr):
        @pl.when(pl.program_id(0) == 0)
        def _():
            r_scr[...] = jnp.zeros_like(r_scr)

        d = _ret_batch(pr_ref, cc_ref, ss_ref, ncb)
        kd = d["k"] * d["kdec"]
        for c in range(ncb):
            bs = slice(c * RET_H, (c + 1) * RET_H)
            rs = r_scr[...]
            st_ref[c] = rs
            r_scr[...] = rs * d["cdec"] + _bdot(kd[bs], d["v"][bs], _TN)
        r_prev = st_ref[...].reshape(ncb * RET_H, 128, 128)
        o = _bdot(d["s"], d["v"], _NN) + _bdot(d["q"], r_prev, _NN) * d["qdec"]
        _, _, y = _rms_fwd(o, nw_ref[0:1, :], RET_D)
        out = y * _silu(d["z"])
        for c in range(ncb):
            for h in range(RET_H):
                oc_ref[c * CH:(c + 1) * CH, h * 128:(h + 1) * 128] = out[c * RET_H + h].astype(oc_ref.dtype)

    def call(pr, cc, ss, nw):
        return pl.pallas_call(
            body,
            grid=(nb,),
            in_specs=[
                pl.BlockSpec((tb, 2048), lambda i: (i, 0)),
                pl.BlockSpec((tb, 128), lambda i: (i, 0)),
                pl.BlockSpec((tb, 128), lambda i: (i, 0)),
                pl.BlockSpec((8, 128), lambda i: (0, 0)),
            ],
            out_specs=[
                pl.BlockSpec((tb, 512), lambda i: (i, 0)),
                pl.BlockSpec((ncb, RET_H, 128, 128), lambda i: (i, 0, 0, 0)),
            ],
            out_shape=[
                jax.ShapeDtypeStruct((seq, 512), BF16),
                jax.ShapeDtypeStruct((seq // CH, RET_H, 128, 128), F32),
            ],
            scratch_shapes=[pltpu.VMEM((RET_H, 128, 128), F32)],
            compiler_params=pltpu.CompilerParams(dimension_semantics=("arbitrary",), vmem_limit_bytes=VMEM_LIMIT),
            name="ret_fwd",
        )(pr, cc, ss, nw)

    return call


def _make_ret_bwd(seq, tb):
    ncb = tb // CH
    nb = seq // tb

    def body(pr_ref, cc_ref, ss_ref, nw_ref, st_ref, doc_ref, dpr_ref, dnw_ref, dr_scr):
        @pl.when(pl.program_id(0) == 0)
        def _():
            dr_scr[...] = jnp.zeros_like(dr_scr)
            dnw_ref[...] = jnp.zeros_like(dnw_ref)

        nw = nw_ref[0:1, :]
        scale = RET_D ** -0.5
        n = ncb * RET_H
        d = _ret_batch(pr_ref, cc_ref, ss_ref, ncb)
        q, k, v, z, s = d["q"], d["k"], d["v"], d["z"], d["s"]
        r_prev = st_ref[...].reshape(n, 128, 128)
        o = _bdot(s, v, _NN) + _bdot(q, r_prev, _NN) * d["qdec"]
        doc = jnp.stack([doc_ref[c * CH:(c + 1) * CH, h * 128:(h + 1) * 128]
                         for c in range(ncb) for h in range(RET_H)])
        on, r, y = _rms_fwd(o, nw, RET_D)
        dz = doc * y * _dsilu(z)
        do, dnw_rows = _rms_bwd(doc * _silu(z), on, r, nw, RET_D)
        dnw_acc = jnp.sum(jnp.sum(dnw_rows, axis=0), axis=0, keepdims=True)
        dqd = do * d["qdec"]
        qtd = _bdot(q, dqd, _TN)
        drn_l = [None] * ncb
        for c in reversed(range(ncb)):
            drn_l[c] = dr_scr[...]
            dr_scr[...] = qtd[c * RET_H:(c + 1) * RET_H] + d["cdec"] * drn_l[c]
        drn = jnp.concatenate(drn_l, axis=0)
        ds = _bdot(do, v, _NT) * d["dmat"]
        dq = _rot_bwd(_bdot(ds, k, _NN) + _bdot(dqd, r_prev, _NT), d["cc"], d["ss"])
        dk = _rot_bwd((_bdot(ds, q, _TN) + _bdot(v, drn, _NT) * d["kdec"]) * scale, d["cc"], d["ss"])
        dv = _bdot(s, do, _TN) + _bdot(k * d["kdec"], drn, _NN)
        for c in range(ncb):
            rows = slice(c * CH, (c + 1) * CH)
            for h in range(RET_H):
                b = c * RET_H + h
                for j, val in enumerate((dq, dk, dv, dz)):
                    dpr_ref[rows, j * 512 + h * 128:j * 512 + (h + 1) * 128] = val[b].astype(dpr_ref.dtype)
        dnw_ref[...] += jnp.where(_iota2((8, 128), 0) == 0, dnw_acc, 0.0)

    def call(pr, cc, ss, nw, st, doc):
        rev = lambda i: (nb - 1 - i, 0)
        return pl.pallas_call(
            body,
            grid=(nb,),
            in_specs=[
                pl.BlockSpec((tb, 2048), rev),
                pl.BlockSpec((tb, 128), rev),
                pl.BlockSpec((tb, 128), rev),
                pl.BlockSpec((8, 128), lambda i: (0, 0)),
                pl.BlockSpec((ncb, RET_H, 128, 128), lambda i: (nb - 1 - i, 0, 0, 0)),
                pl.BlockSpec((tb, 512), rev),
            ],
            out_specs=[
                pl.BlockSpec((tb, 2048), rev),
                pl.BlockSpec((8, 128), lambda i: (0, 0)),
            ],
            out_shape=[
                jax.ShapeDtypeStruct((seq, 2048), BF16),
                jax.ShapeDtypeStruct((8, 128), F32),
            ],
            scratch_shapes=[pltpu.VMEM((RET_H, 128, 128), F32)],
            compiler_params=pltpu.CompilerParams(dimension_semantics=("arbitrary",), vmem_limit_bytes=VMEM_LIMIT),
            name="ret_bwd",
        )(pr, cc, ss, nw, st, doc)

    return call


def _rope_tables(seq):
    half = RET_D // 2
    inv = ROPE_BASE ** (-jnp.arange(half, dtype=F32) / half)
    hi = (CH * jnp.arange(seq // CH, dtype=jnp.int32)).astype(F32)[:, None] * inv[None, :]
    lo = jnp.arange(CH, dtype=jnp.int32).astype(F32)[:, None] * inv[None, :]
    ch, sh, cl, sl = jnp.cos(hi)[:, None, :], jnp.sin(hi)[:, None, :], jnp.cos(lo)[None], jnp.sin(lo)[None]
    cos = (ch * cl - sh * sl).reshape(seq, half)
    sin = (sh * cl + ch * sl).reshape(seq, half)
    return jnp.concatenate([cos, cos], axis=1), jnp.concatenate([-sin, sin], axis=1)


SEG_G, SEG_S, SEG_R, SEG_GS, SEG_SS = (0, 2048), (2048, 4608), (4608, 6656), (6656, 6784), (6784, 6912)
NP = 6912
SEGS = (SEG_G, SEG_S, SEG_R, SEG_GS, SEG_SS)


def _resident(shape):
    return pl.BlockSpec(shape, lambda i: (0,) * len(shape), pipeline_mode=pl.Buffered(1))


def _make_inproj(seq, tl):
    def body(x_ref, pn_ref, w_ref, pg_ref, ps_ref, pr_ref, gs_ref, ss_ref, ht_ref):
        x = x_ref[...]
        _, _, hn = _rms_fwd(x, pn_ref[0:1, :], D_MODEL)
        h = hn.astype(BF16)
        ht_ref[...] = hn.T.astype(BF16)
        for (a, b), o_ref in zip(SEGS, (pg_ref, ps_ref, pr_ref, gs_ref, ss_ref)):
            o_ref[...] = jnp.dot(h, w_ref[:, a:b], preferred_element_type=F32)

    def call(x, pn, w, comm=None, comm_args=()):
        row = lambda i: (i, 0)
        cx = _exchange_specs(comm)
        return pl.pallas_call(
            _with_exchange(body, comm, 3, 6, seq // tl),
            grid=(seq // tl,),
            in_specs=[pl.BlockSpec((tl, D_MODEL), row), _resident((8, D_MODEL)), _resident((D_MODEL, NP))]
            + cx["specs"],
            out_specs=[pl.BlockSpec((tl, b - a), row) for a, b in SEGS]
            + [pl.BlockSpec((D_MODEL, tl), lambda i: (0, i))] + cx["specs"],
            out_shape=[jax.ShapeDtypeStruct((seq, b - a), F32) for a, b in SEGS]
            + [jax.ShapeDtypeStruct((D_MODEL, seq), BF16)] + cx["out_shape"],
            scratch_shapes=cx["scratch"],
            compiler_params=pltpu.CompilerParams(dimension_semantics=("arbitrary",), vmem_limit_bytes=VMEM_LIMIT,
                                                 has_side_effects=comm is not None),
            name="inproj" + cx["tag"],
        )(x, pn, w, *comm_args)

    return call


def _make_outproj(seq, tl):
    def body(oa_ref, ob_ref, oc_ref, w_ref, x_ref, qn_ref, out_ref, xn_ref):
        out = (jnp.dot(oa_ref[...], w_ref[0:512, :], preferred_element_type=F32)
               + jnp.dot(ob_ref[...], w_ref[512:1536, :], preferred_element_type=F32)
               + jnp.dot(oc_ref[...], w_ref[1536:2048, :], preferred_element_type=F32))
        out_ref[...] = out
        _, _, y = _rms_fwd(out, qn_ref[0:1, :], D_MODEL)
        xn_ref[...] = x_ref[...] + y

    def call(oa, ob, oc, w, x, qn):
        row = lambda i: (i, 0)
        return pl.pallas_call(
            body,
            grid=(seq // tl,),
            in_specs=[pl.BlockSpec((tl, 512), row), pl.BlockSpec((tl, 1024), row), pl.BlockSpec((tl, 512), row),
                      _resident((2048, D_MODEL)), pl.BlockSpec((tl, D_MODEL), row), _resident((8, D_MODEL))],
            out_specs=[pl.BlockSpec((tl, D_MODEL), row), pl.BlockSpec((tl, D_MODEL), row)],
            out_shape=[jax.ShapeDtypeStruct((seq, D_MODEL), F32), jax.ShapeDtypeStruct((seq, D_MODEL), F32)],
            compiler_params=pltpu.CompilerParams(dimension_semantics=("arbitrary",), vmem_limit_bytes=VMEM_LIMIT),
            name="outproj",
        )(oa, ob, oc, w, x, qn)

    return call


def _make_outproj_loss(seq, tl):
    def body(oa_ref, ob_ref, oc_ref, w_ref, x_ref, qn_ref, t_ref, out_ref, dy_ref, loss_ref):
        @pl.when(pl.program_id(0) == 0)
        def _():
            loss_ref[...] = jnp.zeros_like(loss_ref)

        out = (jnp.dot(oa_ref[...], w_ref[0:512, :], preferred_element_type=F32)
               + jnp.dot(ob_ref[...], w_ref[512:1536, :], preferred_element_type=F32)
               + jnp.dot(oc_ref[...], w_ref[1536:2048, :], preferred_element_type=F32))
        out_ref[...] = out
        _, _, y = _rms_fwd(out, qn_ref[0:1, :], D_MODEL)
        err = (x_ref[...] + y) - t_ref[...]
        dy_ref[...] = err * (1.0 / D_MODEL)
        part = jnp.sum(jnp.sum(err * err, axis=1, keepdims=True), axis=0, keepdims=True) * (0.5 / D_MODEL)
        loss_ref[...] += jnp.where((_iota2((8, 128), 0) == 0) & (_iota2((8, 128), 1) == 0), part, 0.0)

    def call(oa, ob, oc, w, x, qn, t):
        row = lambda i: (i, 0)
        return pl.pallas_call(
            body,
            grid=(seq // tl,),
            in_specs=[pl.BlockSpec((tl, 512), row), pl.BlockSpec((tl, 1024), row), pl.BlockSpec((tl, 512), row),
                      _resident((2048, D_MODEL)), pl.BlockSpec((tl, D_MODEL), row), _resident((8, D_MODEL)),
                      pl.BlockSpec((tl, D_MODEL), row)],
            out_specs=[pl.BlockSpec((tl, D_MODEL), row), pl.BlockSpec((tl, D_MODEL), row),
                       pl.BlockSpec((8, 128), lambda i: (0, 0))],
            out_shape=[jax.ShapeDtypeStruct((seq, D_MODEL), F32), jax.ShapeDtypeStruct((seq, D_MODEL), F32),
                       jax.ShapeDtypeStruct((8, 128), F32)],
            compiler_params=pltpu.CompilerParams(dimension_semantics=("arbitrary",), vmem_limit_bytes=VMEM_LIMIT),
            name="outproj_loss",
        )(oa, ob, oc, w, x, qn, t)

    return call


def _make_outproj_bwd(seq, tl):
    def body(dxn_ref, out_ref, oa_ref, ob_ref, oc_ref, w_ref, qn_ref, doa_ref, dob_ref, doc_ref, dqn_ref, dw_ref):
        @pl.when(pl.program_id(0) == 0)
        def _():
            dqn_ref[...] = jnp.zeros_like(dqn_ref)
            dw_ref[...] = jnp.zeros_like(dw_ref)

        qn = qn_ref[0:1, :]
        on, r, _ = _rms_fwd(out_ref[...], qn, D_MODEL)
        dout, dqn_rows = _rms_bwd(dxn_ref[...], on, r, qn, D_MODEL)
        dqn_ref[...] += jnp.where(_iota2((8, D_MODEL), 0) == 0, jnp.sum(dqn_rows, axis=0, keepdims=True), 0.0)
        db = dout.astype(BF16)
        nt = (((1,), (1,)), ((), ()))
        tn = (((0,), (0,)), ((), ()))
        doa_ref[...] = lax.dot_general(db, w_ref[0:512, :], nt, preferred_element_type=F32).astype(BF16)
        dob_ref[...] = lax.dot_general(db, w_ref[512:1536, :], nt, preferred_element_type=F32).astype(BF16)
        doc_ref[...] = lax.dot_general(db, w_ref[1536:2048, :], nt, preferred_element_type=F32).astype(BF16)
        dw_ref[0:512, :] += lax.dot_general(oa_ref[...], db, tn, preferred_element_type=F32)
        dw_ref[512:1536, :] += lax.dot_general(ob_ref[...], db, tn, preferred_element_type=F32)
        dw_ref[1536:2048, :] += lax.dot_general(oc_ref[...], db, tn, preferred_element_type=F32)

    def call(dxn, out, oa, ob, oc, w, qn):
        row = lambda i: (i, 0)
        const = lambda i: (0, 0)
        return pl.pallas_call(
            body,
            grid=(seq // tl,),
            in_specs=[pl.BlockSpec((tl, D_MODEL), row), pl.BlockSpec((tl, D_MODEL), row),
                      pl.BlockSpec((tl, 512), row), pl.BlockSpec((tl, 1024), row), pl.BlockSpec((tl, 512), row),
                      _resident((2048, D_MODEL)), _resident((8, D_MODEL))],
            out_specs=[pl.BlockSpec((tl, 512), row), pl.BlockSpec((tl, 1024), row), pl.BlockSpec((tl, 512), row),
                       pl.BlockSpec((8, D_MODEL), const), pl.BlockSpec((2048, D_MODEL), const)],
            out_shape=[jax.ShapeDtypeStruct((seq, 512), BF16), jax.ShapeDtypeStruct((seq, 1024), BF16),
                       jax.ShapeDtypeStruct((seq, 512), BF16), jax.ShapeDtypeStruct((8, D_MODEL), F32),
                       jax.ShapeDtypeStruct((2048, D_MODEL), F32)],
            compiler_params=pltpu.CompilerParams(dimension_semantics=("arbitrary",), vmem_limit_bytes=VMEM_LIMIT),
            name="outproj_bwd",
        )(dxn, out, oa, ob, oc, w, qn)

    return call


def _make_inproj_bwd_dx(seq, tl):
    def body(dg_ref, ds_ref, dr_ref, dgs_ref, dss_ref, w_ref, x_ref, pn_ref, dxn_ref, dx_ref, dpn_ref):
        @pl.when(pl.program_id(0) == 0)
        def _():
            dpn_ref[...] = jnp.zeros_like(dpn_ref)

        nt = (((1,), (1,)), ((), ()))
        dh = jnp.zeros((tl, D_MODEL), F32)
        for (a, b), d_ref in zip(SEGS, (dg_ref, ds_ref, dr_ref, dgs_ref, dss_ref)):
            dh = dh + lax.dot_general(d_ref[...], w_ref[:, a:b], nt, preferred_element_type=F32)
        pn = pn_ref[0:1, :]
        on, r, _ = _rms_fwd(x_ref[...], pn, D_MODEL)
        dx, dpn_rows = _rms_bwd(dh, on, r, pn, D_MODEL)
        dx_ref[...] = dx + dxn_ref[...]
        dpn_ref[...] += jnp.where(_iota2((8, D_MODEL), 0) == 0, jnp.sum(dpn_rows, axis=0, keepdims=True), 0.0)

    def call(dg, ds, dr, dgs, dss, w, x, pn, dxn, comm=None, comm_args=()):
        row = lambda i: (i, 0)
        cx = _exchange_specs(comm)
        return pl.pallas_call(
            _with_exchange(body, comm, 9, 2, seq // tl),
            grid=(seq // tl,),
            in_specs=[pl.BlockSpec((tl, b - a), row) for a, b in SEGS]
            + [_resident((D_MODEL, NP)), pl.BlockSpec((tl, D_MODEL), row), _resident((8, D_MODEL)),
               pl.BlockSpec((tl, D_MODEL), row)] + cx["specs"],
            out_specs=[pl.BlockSpec((tl, D_MODEL), row), pl.BlockSpec((8, D_MODEL), lambda i: (0, 0))] + cx["specs"],
            out_shape=[jax.ShapeDtypeStruct((seq, D_MODEL), F32), jax.ShapeDtypeStruct((8, D_MODEL), F32)]
            + cx["out_shape"],
            scratch_shapes=cx["scratch"],
            compiler_params=pltpu.CompilerParams(dimension_semantics=("arbitrary",), vmem_limit_bytes=VMEM_LIMIT,
                                                 has_side_effects=comm is not None),
            name="inproj_bwd_dx" + cx["tag"],
        )(dg, ds, dr, dgs, dss, w, x, pn, dxn, *comm_args)

    return call


def _make_inproj_bwd_dw(seq, tl, width, tn, name):
    def body(ht_ref, d_ref, dw_ref):
        @pl.when(pl.program_id(1) == 0)
        def _():
            dw_ref[...] = jnp.zeros_like(dw_ref)

        dw_ref[...] += jnp.dot(ht_ref[...], d_ref[...], preferred_element_type=F32)

    def call(ht, d):
        return pl.pallas_call(
            body,
            grid=(width // tn, seq // tl),
            in_specs=[pl.BlockSpec((D_MODEL, tl), lambda j, i: (0, i)), pl.BlockSpec((tl, tn), lambda j, i: (i, j))],
            out_specs=pl.BlockSpec((D_MODEL, tn), lambda j, i: (0, j)),
            out_shape=jax.ShapeDtypeStruct((D_MODEL, width), F32),
            compiler_params=pltpu.CompilerParams(dimension_semantics=("arbitrary", "arbitrary"),
                                                 vmem_limit_bytes=VMEM_LIMIT),
            name=name,
        )(ht, d)

    return call


ADAM_LR, ADAM_B1, ADAM_B2, ADAM_EPS, ADAM_WD, ADAM_STEP = 0.001, 0.9, 0.999, 1e-08, 0.01, 10


def _adam_math(w, g, m, v):
    m = ADAM_B1 * m + (1.0 - ADAM_B1) * g
    v = ADAM_B2 * v + (1.0 - ADAM_B2) * (g * g)
    m_hat = m / (1.0 - ADAM_B1 ** ADAM_STEP)
    v_hat = v / (1.0 - ADAM_B2 ** ADAM_STEP)
    delta = -ADAM_LR * (m_hat / (jnp.sqrt(v_hat) + ADAM_EPS) + ADAM_WD * w)
    return delta, m, v


def _adamw(w, g, m, v, name):
    shape = w.shape
    cols = shape[-1]
    rows = w.size // cols
    tr = rows if rows <= 512 else 256
    assert rows % tr == 0

    def body(w_ref, g_ref, m_ref, v_ref, d_ref, mo_ref, vo_ref):
        d_ref[...], mo_ref[...], vo_ref[...] = _adam_math(w_ref[...], g_ref[...], m_ref[...], v_ref[...])

    spec = pl.BlockSpec((tr, cols), lambda i: (i, 0))
    outs = pl.pallas_call(
        body,
        grid=(rows // tr,),
        in_specs=[spec] * 4,
        out_specs=[spec] * 3,
        out_shape=[jax.ShapeDtypeStruct((rows, cols), F32)] * 3,
        compiler_params=pltpu.CompilerParams(dimension_semantics=("arbitrary",), vmem_limit_bytes=VMEM_LIMIT),
        name=name,
    )(*[a.reshape(rows, cols) for a in (w, g, m, v)])
    return (g,) + tuple(o.reshape(shape) for o in outs)


def _adamw_pairs(w, mine, theirs, m, v, name):
    na, r, cols = w.shape
    assert na == 2
    tr = 256
    assert r % tr == 0

    def body(w_ref, a0_ref, b0_ref, a1_ref, b1_ref, m_ref, v_ref, g_ref, d_ref, mo_ref, vo_ref):
        g = jnp.where(pl.program_id(0) == 0, a0_ref[...] + b0_ref[...], a1_ref[...] + b1_ref[...])
        g_ref[...] = g
        d_ref[...], mo_ref[...], vo_ref[...] = _adam_math(w_ref[...], g, m_ref[...], v_ref[...])

    nblk = r // tr
    full = pl.BlockSpec((None, tr, cols), lambda a, i: (a, i, 0))
    lay0 = pl.BlockSpec((None, tr, cols), lambda a, i: (0, i * (1 - a) + (nblk - 1) * a, 0))
    lay1 = pl.BlockSpec((None, tr, cols), lambda a, i: (0, i * a, 0))
    return pl.pallas_call(
        body,
        grid=(na, nblk),
        in_specs=[full, lay0, lay0, lay1, lay1, full, full],
        out_specs=[full] * 4,
        out_shape=[jax.ShapeDtypeStruct(w.shape, F32)] * 4,
        compiler_params=pltpu.CompilerParams(dimension_semantics=("arbitrary",) * 2, vmem_limit_bytes=VMEM_LIMIT),
        name=name,
    )(w, mine[0], theirs[0], mine[1], theirs[1], m, v)


MESH = pl.DeviceIdType.MESH
ANY = pl.BlockSpec(memory_space=pl.ANY)
CHIP_REL = ((1, 0), (0, 1), (1, 1))


def _flip(v, d):
    return 1 - v if d else v


def _ag_chips(arrs, name):
    n = len(arrs)

    def body(*refs):
        ins, outs = refs[:n], refs[n:2 * n]
        send_sems, recv_sems, loc_sems = refs[2 * n:]
        x, y, c = lax.axis_index("x"), lax.axis_index("y"), lax.axis_index("c")
        me = 2 * x + y

        def remote(a, k, slot):
            dx, dy = CHIP_REL[k]
            return pltpu.make_async_remote_copy(
                src_ref=ins[a], dst_ref=outs[a].at[slot], send_sem=send_sems.at[a * 3 + k],
                recv_sem=recv_sems.at[a * 3 + k], device_id=(_flip(x, dx), _flip(y, dy), c), device_id_type=MESH)

        local = [pltpu.make_async_copy(ins[a], outs[a].at[me], loc_sems.at[a]) for a in range(n)]
        for cp in local:
            cp.start()
        for a in range(n):
            for k in range(3):
                remote(a, k, me).start()
        for a in range(n):
            for k, (dx, dy) in enumerate(CHIP_REL):
                remote(a, k, 2 * _flip(x, dx) + _flip(y, dy)).wait_recv()
        for a in range(n):
            for k in range(3):
                remote(a, k, me).wait_send()
        for cp in local:
            cp.wait()

    return pl.pallas_call(
        body,
        in_specs=[ANY] * n,
        out_specs=[ANY] * n,
        out_shape=[jax.ShapeDtypeStruct((4,) + a.shape, a.dtype) for a in arrs],
        scratch_shapes=[pltpu.SemaphoreType.DMA((3 * n,)), pltpu.SemaphoreType.DMA((3 * n,)),
                        pltpu.SemaphoreType.DMA((n,))],
        compiler_params=pltpu.CompilerParams(has_side_effects=True),
        name=name,
    )(*arrs)


class _ChipExchange:
    def __init__(self, kind, arrs, swap=()):
        self.kind, self.n_chip, self.n = kind, len(arrs), len(arrs) + len(swap)
        if kind == "gather":
            self.out_shape = [jax.ShapeDtypeStruct((4,) + a.shape, a.dtype) for a in arrs]
        else:
            self.out_shape = [jax.ShapeDtypeStruct((3,) + a.shape[1:], a.dtype) for a in arrs]
        self.out_shape += [jax.ShapeDtypeStruct(a.shape, a.dtype) for a in swap]
        self.scratch = [pltpu.SemaphoreType.DMA((4 * self.n,)), pltpu.SemaphoreType.DMA((4 * self.n,))]

    def _copies(self, ins, outs, sems):
        send_sems, recv_sems = sems
        x, y, c = lax.axis_index("x"), lax.axis_index("y"), lax.axis_index("c")
        me = 2 * x + y
        pairs = []
        for a in range(self.n_chip, self.n):
            cp = pltpu.make_async_remote_copy(
                src_ref=ins[a], dst_ref=outs[a], send_sem=send_sems.at[4 * a], recv_sem=recv_sems.at[4 * a],
                device_id=(x, y, 1 - c), device_id_type=MESH)
            pairs.append((cp, cp))
        for a in range(self.n_chip):
            for k, (dx, dy) in enumerate(CHIP_REL):
                px, py = _flip(x, dx), _flip(y, dy)
                sem = dict(send_sem=send_sems.at[4 * a + k], recv_sem=recv_sems.at[4 * a + k],
                           device_id=(px, py, c), device_id_type=MESH)
                if self.kind == "gather":
                    out = pltpu.make_async_remote_copy(src_ref=ins[a], dst_ref=outs[a].at[me], **sem)
                    inc = pltpu.make_async_remote_copy(src_ref=ins[a], dst_ref=outs[a].at[2 * px + py], **sem)
                else:
                    out = pltpu.make_async_remote_copy(src_ref=ins[a].at[2 * px + py], dst_ref=outs[a].at[k], **sem)
                    inc = out
                pairs.append((out, inc))
            if self.kind == "gather":
                own = pltpu.make_async_remote_copy(
                    src_ref=ins[a], dst_ref=outs[a].at[me], send_sem=send_sems.at[4 * a + 3],
                    recv_sem=recv_sems.at[4 * a + 3], device_id=(x, y, 1 - c), device_id_type=MESH)
                pairs.append((own, own))
        return pairs

    def start(self, ins, outs, sems):
        for out, _ in self._copies(ins, outs, sems):
            out.start()

    def finish(self, ins, outs, sems):
        pairs = self._copies(ins, outs, sems)
        for _, inc in pairs:
            inc.wait_recv()
        for out, _ in pairs:
            out.wait_send()


def _with_exchange(body, comm, n_in, n_out, nb):
    if comm is None:
        return body

    def wrapped(*refs):
        ins = refs[:n_in]
        c_in = refs[n_in:n_in + comm.n]
        outs = refs[n_in + comm.n:n_in + comm.n + n_out]
        c_out = refs[n_in + comm.n + n_out:n_in + 2 * comm.n + n_out]
        rest = refs[n_in + 2 * comm.n + n_out:]
        scratch, sems = rest[:len(rest) - 2], rest[len(rest) - 2:]

        @pl.when(pl.program_id(0) == 0)
        def _():
            comm.start(c_in, c_out, sems)

        body(*ins, *outs, *scratch)

        @pl.when(pl.program_id(0) == nb - 1)
        def _():
            comm.finish(c_in, c_out, sems)

    return wrapped


def _exchange_specs(comm):
    if comm is None:
        return dict(specs=[], out_shape=[], scratch=[], tag="")
    return dict(specs=[pl.BlockSpec(memory_space=pl.ANY)] * comm.n, out_shape=list(comm.out_shape),
                scratch=list(comm.scratch), tag="_" + comm.kind)


def _half(ref_or_shape, half):
    r = ref_or_shape[-2] // 2
    return pl.ds(half * r, r)


def _ag_rows(arrs, name):
    n = len(arrs)

    def body(*refs):
        ins, outs = refs[:n], refs[n:2 * n]
        send_sems, recv_sems, fsend_sems, frecv_sems, loc_sems = refs[2 * n:]
        x, y, c = lax.axis_index("x"), lax.axis_index("y"), lax.axis_index("c")
        me = 2 * x + y
        sib = (x, y, 1 - c)

        def chip_of(k):
            dx, dy = CHIP_REL[k]
            return _flip(x, dx), _flip(y, dy)

        def ici(a, k, slot):
            px, py = chip_of(k)
            rows = _half(arrs[a].shape, c)
            return pltpu.make_async_remote_copy(
                src_ref=ins[a].at[:, rows, :], dst_ref=outs[a].at[slot, :, rows, :], send_sem=send_sems.at[a * 3 + k],
                recv_sem=recv_sems.at[a * 3 + k], device_id=(px, py, c), device_id_type=MESH)

        def fwd(a, k, half):
            px, py = chip_of(k)
            blk = outs[a].at[2 * px + py, :, _half(arrs[a].shape, half), :]
            return pltpu.make_async_remote_copy(
                src_ref=blk, dst_ref=blk, send_sem=fsend_sems.at[a * 3 + k], recv_sem=frecv_sems.at[a * 3 + k],
                device_id=sib, device_id_type=MESH)

        own = [pltpu.make_async_remote_copy(src_ref=ins[a], dst_ref=outs[a].at[me], send_sem=loc_sems.at[a],
                                            recv_sem=loc_sems.at[n + a], device_id=sib, device_id_type=MESH)
               for a in range(n)]
        for cp in own:
            cp.start()
        for a in range(n):
            for k in range(3):
                ici(a, k, me).start()
        for a in range(n):
            for k in range(3):
                px, py = chip_of(k)
                ici(a, k, 2 * px + py).wait_recv()
                fwd(a, k, c).start()
        for a in range(n):
            for k in range(3):
                fwd(a, k, 1 - c).wait_recv()
        for a in range(n):
            for k in range(3):
                ici(a, k, me).wait_send()
                fwd(a, k, c).wait_send()
        for cp in own:
            cp.wait()

    return pl.pallas_call(
        body,
        in_specs=[ANY] * n,
        out_specs=[ANY] * n,
        out_shape=[jax.ShapeDtypeStruct((4,) + a.shape, a.dtype) for a in arrs],
        scratch_shapes=[pltpu.SemaphoreType.DMA((3 * n,)) for _ in range(4)] + [pltpu.SemaphoreType.DMA((2 * n,))],
        compiler_params=pltpu.CompilerParams(has_side_effects=True),
        name=name,
    )(*arrs)


def _sum_chips(own, recv, chip, name):
    _, na, r, cols = own.shape
    tr = 256
    assert r % tr == 0

    def body(chip_ref, o_ref, r_ref, s_ref):
        s_ref[...] = ((o_ref[...] + r_ref[0].astype(F32)) + r_ref[1].astype(F32)) + r_ref[2].astype(F32)

    return pl.pallas_call(
        body,
        grid_spec=pltpu.PrefetchScalarGridSpec(
            num_scalar_prefetch=1,
            grid=(na, r // tr),
            in_specs=[pl.BlockSpec((None, None, tr, cols), lambda a, i, ch: (ch[0], a, i, 0)),
                      pl.BlockSpec((3, None, tr, cols), lambda a, i, ch: (0, a, i, 0))],
            out_specs=pl.BlockSpec((None, tr, cols), lambda a, i, ch: (a, i, 0))),
        out_shape=jax.ShapeDtypeStruct((na, r, cols), F32),
        compiler_params=pltpu.CompilerParams(dimension_semantics=("arbitrary",) * 2, vmem_limit_bytes=VMEM_LIMIT),
        name=name,
    )(chip, own, recv)


def _swap_sibling(arrs, name):
    n = len(arrs)

    def body(*refs):
        ins, outs = refs[:n], refs[n:2 * n]
        send_sems, recv_sems = refs[2 * n:]
        x, y, c = lax.axis_index("x"), lax.axis_index("y"), lax.axis_index("c")
        cps = [pltpu.make_async_remote_copy(src_ref=ins[a], dst_ref=outs[a], send_sem=send_sems.at[a],
                                            recv_sem=recv_sems.at[a], device_id=(x, y, 1 - c), device_id_type=MESH)
               for a in range(n)]
        for cp in cps:
            cp.start()
        for cp in cps:
            cp.wait_recv()
        for cp in cps:
            cp.wait_send()

    return pl.pallas_call(
        body,
        in_specs=[ANY] * n,
        out_specs=[ANY] * n,
        out_shape=[jax.ShapeDtypeStruct(a.shape, a.dtype) for a in arrs],
        scratch_shapes=[pltpu.SemaphoreType.DMA((n,)), pltpu.SemaphoreType.DMA((n,))],
        compiler_params=pltpu.CompilerParams(has_side_effects=True),
        name=name,
    )(*arrs)


def _allreduce_small(vec, name):
    rows = vec.shape[0]

    def body(v_ref, out_ref, gat_ref, send_sems, recv_sems):
        x, y, c = lax.axis_index("x"), lax.axis_index("y"), lax.axis_index("c")
        me = 4 * x + 2 * y + c

        def remote(k, slot):
            dx, dy, dc = (k >> 2) & 1, (k >> 1) & 1, k & 1
            return pltpu.make_async_remote_copy(
                src_ref=v_ref, dst_ref=gat_ref.at[slot], send_sem=send_sems.at[k - 1], recv_sem=recv_sems.at[k - 1],
                device_id=(_flip(x, dx), _flip(y, dy), _flip(c, dc)), device_id_type=MESH)

        gat_ref[me] = v_ref[...]
        for k in range(1, 8):
            remote(k, me).start()
        for k in range(1, 8):
            dx, dy, dc = (k >> 2) & 1, (k >> 1) & 1, k & 1
            remote(k, 4 * _flip(x, dx) + 2 * _flip(y, dy) + _flip(c, dc)).wait_recv()
        for k in range(1, 8):
            remote(k, me).wait_send()
        acc = gat_ref[0]
        for j in range(1, 8):
            acc = acc + gat_ref[j]
        out_ref[...] = acc

    vm = pl.BlockSpec(memory_space=pltpu.VMEM)
    return pl.pallas_call(
        body,
        in_specs=[vm],
        out_specs=vm,
        out_shape=jax.ShapeDtypeStruct(vec.shape, F32),
        scratch_shapes=[pltpu.VMEM((8, rows, 128), F32), pltpu.SemaphoreType.DMA((7,)), pltpu.SemaphoreType.DMA((7,))],
        compiler_params=pltpu.CompilerParams(has_side_effects=True),
        name=name,
    )(vec)


def _pad8(v, width, lane0=0):
    v = v.reshape(1, -1) if v.ndim == 1 else v
    return jnp.zeros((8, width), F32).at[:v.shape[0], lane0:lane0 + v.shape[1]].set(v.astype(F32))


def _relayout_w_in(g):
    tr = 128
    q = N_IN // 4

    def body(g_ref, o_ref):
        w = jnp.concatenate([g_ref[j] for j in range(4)], axis=1)
        z = lambda n: jnp.zeros((tr, n), w.dtype)
        o_ref[...] = jnp.concatenate([w[:, 0:2048], w[:, 2056:4616], w[:, 4632:6680],
                                      w[:, 2048:2056], z(120), w[:, 4616:4632], z(112)], axis=1)

    return pl.pallas_call(
        body,
        grid=(D_MODEL // tr,),
        in_specs=[pl.BlockSpec((4, tr, q), lambda i: (0, i, 0))],
        out_specs=pl.BlockSpec((tr, NP), lambda i: (i, 0)),
        out_shape=jax.ShapeDtypeStruct((D_MODEL, NP), g.dtype),
        compiler_params=pltpu.CompilerParams(dimension_semantics=("arbitrary",), vmem_limit_bytes=VMEM_LIMIT),
        name="relayout_w_in",
    )(g)


def _unlayout_dw_in(dg, ds, dr, dsm):
    tr = 128
    q = N_IN // 4

    def body(g_ref, s_ref, r_ref, sm_ref, o_ref, ob_ref):
        w = jnp.concatenate([g_ref[...], sm_ref[:, 0:8], s_ref[...], sm_ref[:, 128:144], r_ref[...]], axis=1)
        for j in range(4):
            blk = w[:, q * j:q * (j + 1)]
            o_ref[j] = blk
            ob_ref[j] = blk.astype(BF16)

    row = lambda i: (i, 0)
    return pl.pallas_call(
        body,
        grid=(D_MODEL // tr,),
        in_specs=[pl.BlockSpec((tr, d.shape[1]), row) for d in (dg, ds, dr, dsm)],
        out_specs=[pl.BlockSpec((4, tr, q), lambda i: (0, i, 0))] * 2,
        out_shape=[jax.ShapeDtypeStruct((4, D_MODEL, q), F32), jax.ShapeDtypeStruct((4, D_MODEL, q), BF16)],
        compiler_params=pltpu.CompilerParams(dimension_semantics=("arbitrary",), vmem_limit_bytes=VMEM_LIMIT),
        name="unlayout_dw_in",
    )(dg, ds, dr, dsm)


TB = 256
TB_RET = 512
TL = 1024
TL_IN = 512
TL_OB = 1024
TK = 2048


def kernel(x, pre_norm, post_norm, w_in, gdn_conv, gdn_A_log, gdn_dt_bias, gdn_norm, ssd_conv, ssd_conv_b, ssd_A_log, ssd_dt_bias, ssd_D, ssd_norm, ret_norm, w_out, loss_target, m_pre_norm, m_post_norm, m_w_in, m_gdn_conv, m_gdn_A_log, m_gdn_dt_bias, m_gdn_norm, m_ssd_conv, m_ssd_conv_b, m_ssd_A_log, m_ssd_dt_bias, m_ssd_D, m_ssd_norm, m_ret_norm, m_w_out, v_pre_norm, v_post_norm, v_w_in, v_gdn_conv, v_gdn_A_log, v_gdn_dt_bias, v_gdn_norm, v_ssd_conv, v_ssd_conv_b, v_ssd_A_log, v_ssd_dt_bias, v_ssd_D, v_ssd_norm, v_ret_norm, v_w_out):
    seq = x.shape[1]
    chip = 2 * lax.axis_index("x") + lax.axis_index("y")
    x0 = x[0]

    wi_b, wo_b = w_in.astype(BF16), w_out.astype(BF16)
    (wi0_g,) = _ag_rows([wi_b[0:1]], "ag_weights")
    gcv_g, scv_g = _ag_chips([gdn_conv, ssd_conv], "ag_conv")
    full_w_in = _relayout_w_in
    wp = [full_w_in(wi0_g[:, 0]), None]
    wo = [None, None]
    ag0 = _ChipExchange("gather", [wo_b[0], wo_b[1]])
    ag1 = _ChipExchange("gather", [wi_b[1]])
    gcv = jnp.transpose(gcv_g, (1, 2, 0, 3)).reshape(DEPTH, CONV_W, 1536)
    scv = jnp.transpose(scv_g, (1, 2, 0, 3)).reshape(DEPTH, CONV_W, 1536)
    rope_c, rope_s = _rope_tables(seq)

    saved = []
    xc = x0
    for l in range(DEPTH):
        p = dict(
            pn=_pad8(pre_norm[l], D_MODEL), qn=_pad8(post_norm[l], D_MODEL),
            g_cw=_pad8(gcv[l], 1536), g_prm=_pad8(jnp.stack([gdn_A_log[l], gdn_dt_bias[l]]), 128, 4),
            g_nw=_pad8(gdn_norm[l], 128),
            s_cw=_pad8(scv[l], 1536), s_cb=_pad8(ssd_conv_b[l], 1536),
            s_prm=_pad8(jnp.stack([ssd_A_log[l], ssd_dt_bias[l], ssd_D[l]]), 128), s_nw=_pad8(ssd_norm[l], SSD_W),
            r_nw=_pad8(ret_norm[l], 128))
        if l == 0:
            pg, ps, pr, gs, ss, ht, wo0_g, wo1_g = _make_inproj(seq, TL_IN)(
                xc, p["pn"], wp[l], comm=ag0, comm_args=(wo_b[0], wo_b[1]))
            wo = [wo0_g.reshape(2048, D_MODEL), wo1_g.reshape(2048, D_MODEL)]
        else:
            pg, ps, pr, gs, ss, ht = _make_inproj(seq, TL_IN)(xc, p["pn"], wp[l])
        if l == 0:
            oa, stg, tig, uwg, gpre, wi1_g = _make_gdn_fwd(seq, TB)(
                pg, gs, p["g_cw"], p["g_prm"], p["g_nw"], comm=ag1, comm_args=(wi_b[1],))
            wp[1] = full_w_in(wi1_g)
        else:
            oa, stg, tig, uwg, gpre = _make_gdn_fwd(seq, TB)(pg, gs, p["g_cw"], p["g_prm"], p["g_nw"])
        ob, sts, spre, sy = _make_ssd_fwd(seq, TB)(ps, ss, p["s_cw"], p["s_cb"], p["s_prm"], p["s_nw"])
        oc, str_ = _make_ret_fwd(seq, TB_RET)(pr, rope_c, rope_s, p["r_nw"])
        if l == DEPTH - 1:
            out, dxn, lossp = _make_outproj_loss(seq, TL)(oa, ob, oc, wo[l], xc, p["qn"], loss_target[0])
            xn = None
        else:
            out, xn = _make_outproj(seq, TL)(oa, ob, oc, wo[l], xc, p["qn"])
        saved.append(dict(p=p, x=xc, ht=ht, spre=spre, sy=sy, gpre=gpre, pg=pg, ps=ps, pr=pr, gs=gs, ss=ss, stg=stg, tig=tig, uwg=uwg, sts=sts, str=str_,
                          oa=oa, ob=ob, oc=oc, out=out))
        xc = xn

    small = [None] * DEPTH
    gin, gin_b, gout, q_in, q_out, s_in, s_out, t_in, t_out = ([None] * DEPTH for _ in range(9))
    chip1 = chip.astype(jnp.int32).reshape(1)

    def sum_chips(l):
        return (_sum_chips(gin[l][:, None], q_in[l][:, None], chip1, f"sum_chips_w_in{l}"),
                _sum_chips(gout[l][:, None], q_out[l][:, None], chip1, f"sum_chips_w_out{l}"))

    for l in reversed(range(DEPTH)):
        s = saved[l]
        p = s["p"]
        doa, dob, doc, dqn, dwo_l = _make_outproj_bwd(seq, TL_OB)(dxn, s["out"], s["oa"], s["ob"], s["oc"], wo[l], p["qn"])
        gout[l] = dwo_l.reshape(4, 512, D_MODEL)
        gdn_args = (s["pg"], s["gpre"], s["gs"], p["g_cw"], p["g_prm"], p["g_nw"], s["stg"], s["tig"], s["uwg"], doa)
        if l == 0:
            payload = (gout[0].astype(BF16),)
            dpg, dgs, dcw_g, dprm_g, dnw_g, q_out[0] = _make_gdn_bwd(seq, TB)(
                *gdn_args, comm=_ChipExchange("scatter", payload), comm_args=payload)
        else:
            dpg, dgs, dcw_g, dprm_g, dnw_g = _make_gdn_bwd(seq, TB)(*gdn_args)
        ssd_args = (s["ps"], s["spre"], s["sy"], s["ss"], p["s_cw"], p["s_cb"], p["s_prm"], p["s_nw"], s["sts"], dob)
        if l == 0:
            payload = (gin_b[1], gout[1].astype(BF16))
            dps, dss, dcw_s, dcb_s, dprm_s, dnw_s, q_in[1], q_out[1] = _make_ssd_bwd(seq, TB)(
                *ssd_args, comm=_ChipExchange("scatter", payload), comm_args=payload)
        else:
            dps, dss, dcw_s, dcb_s, dprm_s, dnw_s = _make_ssd_bwd(seq, TB)(*ssd_args)
        dpr, dnw_r = _make_ret_bwd(seq, TB_RET)(s["pr"], rope_c, rope_s, p["r_nw"], s["str"], doc)
        dws = [_make_inproj_bwd_dw(seq, TK, d.shape[1], tn, f"inproj_bwd_dw{i}")(s["ht"], d)
               for i, (d, tn) in enumerate(((dpg, 2048), (dps, 1280), (dpr, 2048),
                                            (jnp.concatenate([dgs, dss], axis=1), 256)))]
        gin[l], gin_b[l] = _unlayout_dw_in(*dws)
        dx_args = (dpg, dps, dpr, dgs, dss, wp[l], s["x"], p["pn"], dxn)
        if l == 0:
            s_in[1], s_out[1] = sum_chips(1)
            payload, swap = (gin_b[0],), (s_in[1], s_out[1])
            dx, dpn, q_in[0], t_in[1], t_out[1] = _make_inproj_bwd_dx(seq, TL_IN)(
                *dx_args, comm=_ChipExchange("scatter", payload, swap), comm_args=payload + swap)
        else:
            dx, dpn = _make_inproj_bwd_dx(seq, TL_IN)(*dx_args)
        small[l] = [dpn[0], dqn[0], dcw_g[0:4].reshape(-1), dprm_g[0, 4:8], dprm_g[1, 4:8], dnw_g[0],
                    dcw_s[0:4].reshape(-1), dcb_s[0], dprm_s[0, 0:16], dprm_s[1, 0:16], dprm_s[2, 0:16],
                    dnw_s[0], dnw_r[0]]
        dxn = dx
    grad_x = dxn[None]

    sizes = [a.shape[0] for a in small[0]]
    flat = jnp.concatenate(small[0] + small[1] + [lossp[0, 0:1]])
    n_flat = flat.shape[0]
    rows = -(-n_flat // 1024) * 8
    red = _allreduce_small(jnp.pad(flat, (0, rows * 128 - n_flat)).reshape(rows, 128), "allreduce_small").reshape(-1)
    per = sum(sizes)
    loss = red[2 * per]

    def pick(i):
        off = sum(sizes[:i])
        return jnp.stack([red[l * per + off:l * per + off + sizes[i]] for l in range(DEPTH)])

    g_small = dict(
        pre_norm=pick(0), post_norm=pick(1),
        gdn_conv=lax.dynamic_slice_in_dim(pick(2).reshape(DEPTH, CONV_W, 1536), chip * 384, 384, axis=2),
        gdn_A_log=pick(3), gdn_dt_bias=pick(4), gdn_norm=pick(5),
        ssd_conv=lax.dynamic_slice_in_dim(pick(6).reshape(DEPTH, CONV_W, 1536), chip * 384, 384, axis=2),
        ssd_conv_b=pick(7), ssd_A_log=pick(8), ssd_dt_bias=pick(9), ssd_D=pick(10), ssd_norm=pick(11),
        ret_norm=pick(12))

    s_in[0], s_out[0] = sum_chips(0)
    t_in[0], t_out[0] = _swap_sibling([s_in[0], s_out[0]], "swap_grads")

    weights = dict(pre_norm=pre_norm, post_norm=post_norm, w_in=w_in, gdn_conv=gdn_conv, gdn_A_log=gdn_A_log,
                   gdn_dt_bias=gdn_dt_bias, gdn_norm=gdn_norm, ssd_conv=ssd_conv, ssd_conv_b=ssd_conv_b,
                   ssd_A_log=ssd_A_log, ssd_dt_bias=ssd_dt_bias, ssd_D=ssd_D, ssd_norm=ssd_norm, ret_norm=ret_norm,
                   w_out=w_out)
    ms = dict(pre_norm=m_pre_norm, post_norm=m_post_norm, w_in=m_w_in, gdn_conv=m_gdn_conv, gdn_A_log=m_gdn_A_log,
              gdn_dt_bias=m_gdn_dt_bias, gdn_norm=m_gdn_norm, ssd_conv=m_ssd_conv, ssd_conv_b=m_ssd_conv_b,
              ssd_A_log=m_ssd_A_log, ssd_dt_bias=m_ssd_dt_bias, ssd_D=m_ssd_D, ssd_norm=m_ssd_norm,
              ret_norm=m_ret_norm, w_out=m_w_out)
    vs = dict(pre_norm=v_pre_norm, post_norm=v_post_norm, w_in=v_w_in, gdn_conv=v_gdn_conv, gdn_A_log=v_gdn_A_log,
              gdn_dt_bias=v_gdn_dt_bias, gdn_norm=v_gdn_norm, ssd_conv=v_ssd_conv, ssd_conv_b=v_ssd_conv_b,
              ssd_A_log=v_ssd_A_log, ssd_dt_bias=v_ssd_dt_bias, ssd_D=v_ssd_D, ssd_norm=v_ssd_norm,
              ret_norm=v_ret_norm, w_out=v_w_out)
    names = list(weights)
    res = {}
    for nme in names:
        if nme == "w_in":
            res[nme] = _adamw_pairs(w_in, s_in, t_in, m_w_in, v_w_in, "adamw_w_in")
        elif nme == "w_out":
            res[nme] = _adamw_pairs(w_out, s_out, t_out, m_w_out, v_w_out, "adamw_w_out")
        else:
            res[nme] = _adamw(weights[nme], g_small[nme], ms[nme], vs[nme], "adamw_" + nme)
    return (loss, grad_x, *[res[n][0] for n in names], *[res[n][1] for n in names],
            *[res[n][2] for n in names], *[res[n][3] for n in names])
```

```python
import math

import jax
import jax.numpy as jnp
from jax import lax
from jax.experimental import pallas as pl
from jax.experimental.pallas import tpu as pltpu

F32 = jnp.float32
BF16 = jnp.bfloat16

D_MODEL = 1024
DEPTH = 2
CH = 64
CONV_W = 4
EPS = 1e-6
GDN_H, GDN_D = 4, 128
SSD_H, SSD_P, SSD_N, SSD_G = 16, 64, 128, 2
SSD_W = SSD_H * SSD_P
RET_H, RET_D = 4, 128
ROPE_BASE = 10000.0
N_IN = 6680
NEG = -1e30

V7X_VMEM_BYTES = 64 * 1024 * 1024
VMEM_LIMIT = V7X_VMEM_BYTES * 7 // 8


def _dot(a, b):
    return jnp.dot(a.astype(BF16), b.astype(BF16), preferred_element_type=F32)


def _dot_nt(a, b):
    return lax.dot_general(a.astype(BF16), b.astype(BF16), (((1,), (1,)), ((), ())), preferred_element_type=F32)


def _dot_tn(a, b):
    return lax.dot_general(a.astype(BF16), b.astype(BF16), (((0,), (0,)), ((), ())), preferred_element_type=F32)


def _split(a):
    hi = a.astype(BF16)
    return hi, (a - hi.astype(F32)).astype(BF16)


def _dot01l(m, v):
    vh, vl = _split(v)
    mb = m.astype(BF16)
    return jnp.dot(mb, vh, preferred_element_type=F32) + jnp.dot(mb, vl, preferred_element_type=F32)


def _dot01r(v, m):
    vh, vl = _split(v)
    mb = m.astype(BF16)
    return jnp.dot(vh, mb, preferred_element_type=F32) + jnp.dot(vl, mb, preferred_element_type=F32)


def _sigmoid(x):
    return jax.nn.sigmoid(x)


def _silu(x):
    return x * _sigmoid(x)


def _dsilu(x):
    s = _sigmoid(x)
    return s * (1.0 + x * (1.0 - s))


def _softplus(x):
    return jnp.maximum(x, 0.0) + jnp.log1p(jnp.exp(-jnp.abs(x)))


def _iota2(shape, dim):
    return lax.broadcasted_iota(jnp.int32, shape, dim)


def _chunk_tri(tb, upper=False):
    r = _iota2((tb, tb), 0)
    c = _iota2((tb, tb), 1)
    same = jnp.right_shift(r, 6) == jnp.right_shift(c, 6)
    return (same & ((c >= r) if upper else (c <= r))).astype(F32)


def _masks():
    r = _iota2((CH, CH), 0)
    c = _iota2((CH, CH), 1)
    return r >= c, r > c, (r == c).astype(F32)


def _put_lane(col, lane_idx, width=128):
    lane = _iota2((col.shape[0], width), 1)
    return jnp.where(lane == lane_idx, col, 0.0)


def _conv_taps(raw, halo8, tb):
    ext = jnp.concatenate([halo8, raw], axis=0)
    return [raw] + [pltpu.roll(ext, s, axis=0)[8:] for s in (1, 2, 3)]


def _conv_back(dpre, nxt8, tb):
    ext = jnp.concatenate([dpre, nxt8], axis=0)
    return [dpre] + [pltpu.roll(ext, tb + 8 - s, axis=0)[:tb] for s in (1, 2, 3)]


def _rms_fwd(o, w, n):
    r = lax.rsqrt(jnp.sum(o * o, axis=-1, keepdims=True) * (1.0 / n) + EPS)
    on = o * r
    return on, r, on * w


def _rms_bwd(dy, on, r, w, n):
    don = dy * w
    return r * (don - on * (jnp.sum(don * on, axis=-1, keepdims=True) * (1.0 / n))), dy * on


def _put_cols(v, g, gw):
    z = jnp.zeros_like(v)
    return jnp.concatenate([v, z] if g == 0 else [z, v], axis=1)


def _gdn_common(pg_ref, halo8, sm, cw, prm, tb, pre=None):
    raw = pg_ref[:, 0:1536]
    if pre is None:
        taps = _conv_taps(raw, halo8, tb)
        pre = taps[0] * cw[3:4, :] + taps[1] * cw[2:3, :] + taps[2] * cw[1:2, :] + taps[3] * cw[0:1, :]
    act = _silu(pre)
    beta = _sigmoid(sm)
    sp_in = sm + prm[1:2, :]
    g = -jnp.exp(prm[0:1, :]) * _softplus(sp_in)
    gc = _dot01l(_chunk_tri(tb), g)
    return raw, pre, act, beta, sp_in, g, gc


_NN = (((2,), (1,)), ((0,), (0,)))
_NT = (((2,), (2,)), ((0,), (0,)))
_TN = (((1,), (1,)), ((0,), (0,)))


def _bdot(a, b, dn):
    return lax.dot_general(a.astype(BF16), b.astype(BF16), dn, preferred_element_type=F32)


def _binv_unit_lower(a, eye):
    r = _iota2((CH, CH), 0)
    c = _iota2((CH, CH), 1)
    d = eye - jnp.where((jnp.right_shift(r, 1) == jnp.right_shift(c, 1)), a, 0.0)
    ab = a.astype(BF16)
    zero = jnp.zeros((), BF16)
    for lb in range(1, 6):
        same = jnp.right_shift(r, lb + 1) == jnp.right_shift(c, lb + 1)
        low = (jnp.bitwise_and(jnp.right_shift(r, lb), 1) == 1) & (jnp.bitwise_and(jnp.right_shift(c, lb), 1) == 0)
        db = d.astype(BF16)
        t = _bdot(jnp.where(same & low, ab, zero), db, _NN)
        d = d - _bdot(db, t, _NN)
    return d


def _rsum(v):
    return jnp.sum(v, axis=-1, keepdims=True)


def _gdn_batch(act, beta, gc, gct, eg_all, ncb, masks):
    causal, strict, _ = masks

    def st(fn):
        return jnp.stack([fn(c, h, slice(c * CH, (c + 1) * CH)) for c in range(ncb) for h in range(GDN_H)])

    qr = st(lambda c, h, r: act[r, h * 128:(h + 1) * 128])
    kr = st(lambda c, h, r: act[r, 512 + h * 128:512 + (h + 1) * 128])
    vh = st(lambda c, h, r: act[r, 1024 + h * 128:1024 + (h + 1) * 128])
    bh = st(lambda c, h, r: beta[r, h:h + 1])
    gcol = st(lambda c, h, r: gc[r, 4 + h:5 + h])
    grow = st(lambda c, h, r: gct[4 + h:5 + h, r])
    eg = st(lambda c, h, r: eg_all[r, 4 + h:5 + h])
    glast = st(lambda c, h, r: gc[(c + 1) * CH - 1:(c + 1) * CH, 4 + h:5 + h])
    rq = lax.rsqrt(_rsum(qr * qr) + EPS)
    rk = lax.rsqrt(_rsum(kr * kr) + EPS)
    qn = qr * rq
    kh = kr * rk
    qh = qn * (GDN_D ** -0.5)
    decay = jnp.exp(jnp.where(causal, gcol - grow, NEG))
    kb = kh * bh
    kd_scale = jnp.exp(glast - gcol)
    return dict(qn=qn, rq=rq, kh=kh, rk=rk, qh=qh, vh=vh, bh=bh, eg=eg, decay=decay, kb=kb, vb=vh * bh, kg=kb * eg,
                qg=qh * eg, kd_scale=kd_scale, kdec=kh * kd_scale, egl=jnp.exp(glast),
                a=jnp.where(strict, _bdot(kb, kh, _NT) * decay, 0.0), attn=_bdot(qh, kh, _NT) * decay)


def _make_gdn_fwd(seq, tb):
    ncb = tb // CH
    nb = seq // tb
    n = ncb * GDN_H

    def body(pg_ref, sm_ref, cw_ref, prm_ref, nw_ref, oa_ref, st_ref, ti_ref, uw_ref, pre_ref, s_scr, halo_scr):
        @pl.when(pl.program_id(0) == 0)
        def _():
            s_scr[...] = jnp.zeros_like(s_scr)
            halo_scr[...] = jnp.zeros_like(halo_scr)

        masks = _masks()
        sm = sm_ref[...]
        raw, pre, act, beta, _, _, gc = _gdn_common(pg_ref, halo_scr[...], sm, cw_ref[...], prm_ref[...], tb)
        halo_scr[...] = raw[tb - 8:tb, :]
        pre_ref[...] = pre
        d = _gdn_batch(act, beta, gc, gc.T, jnp.exp(gc), ncb, masks)
        t = _binv_unit_lower(d["a"], masks[2])
        sol = _bdot(t, jnp.concatenate([d["vb"], d["kg"]], axis=2), _NN)
        ti_ref[...] = t.reshape(ncb, GDN_H, CH, CH)
        uw_ref[...] = sol.reshape(ncb, GDN_H, CH, 256)
        u, w = sol[:, :, :128], sol[:, :, 128:]
        vns = []
        for c in range(ncb):
            bs = slice(c * GDN_H, (c + 1) * GDN_H)
            s = s_scr[...]
            st_ref[c] = s
            vn = u[bs] - _bdot(w[bs], s, _NN)
            s_scr[...] = s * d["egl"][bs] + _bdot(d["kdec"][bs], vn, _TN)
            vns.append(vn)
        v_new = jnp.concatenate(vns, axis=0)
        s_prev = st_ref[...].reshape(n, 128, 128)
        o = _bdot(d["qg"], s_prev, _NN) + _bdot(d["attn"], v_new, _NN)
        _, _, y = _rms_fwd(o, nw_ref[0:1, :], GDN_D)
        for c in range(ncb):
            rows = slice(c * CH, (c + 1) * CH)
            for h in range(GDN_H):
                z = pg_ref[rows, 1536 + h * 128:1536 + (h + 1) * 128]
                oa_ref[rows, h * 128:(h + 1) * 128] = (y[c * GDN_H + h] * _silu(z)).astype(oa_ref.dtype)

    def call(pg, sm, cw, prm, nw, comm=None, comm_args=()):
        blk4 = lambda i: (i, 0, 0, 0)
        cx = _exchange_specs(comm)
        return pl.pallas_call(
            _with_exchange(body, comm, 5, 5, nb),
            grid=(nb,),
            in_specs=[
                pl.BlockSpec((tb, 2048), lambda i: (i, 0)),
                pl.BlockSpec((tb, 128), lambda i: (i, 0)),
                pl.BlockSpec((8, 1536), lambda i: (0, 0)),
                pl.BlockSpec((8, 128), lambda i: (0, 0)),
                pl.BlockSpec((8, 128), lambda i: (0, 0)),
            ] + cx["specs"],
            out_specs=[
                pl.BlockSpec((tb, 512), lambda i: (i, 0)),
                pl.BlockSpec((ncb, GDN_H, 128, 128), blk4),
                pl.BlockSpec((ncb, GDN_H, CH, CH), blk4),
                pl.BlockSpec((ncb, GDN_H, CH, 256), blk4),
                pl.BlockSpec((tb, 1536), lambda i: (i, 0)),
            ] + cx["specs"],
            out_shape=[
                jax.ShapeDtypeStruct((seq, 512), BF16),
                jax.ShapeDtypeStruct((seq // CH, GDN_H, 128, 128), F32),
                jax.ShapeDtypeStruct((seq // CH, GDN_H, CH, CH), F32),
                jax.ShapeDtypeStruct((seq // CH, GDN_H, CH, 256), F32),
                jax.ShapeDtypeStruct((seq, 1536), F32),
            ] + cx["out_shape"],
            scratch_shapes=[pltpu.VMEM((GDN_H, 128, 128), F32), pltpu.VMEM((8, 1536), F32)] + cx["scratch"],
            compiler_params=pltpu.CompilerParams(dimension_semantics=("arbitrary",), vmem_limit_bytes=VMEM_LIMIT,
                                                 has_side_effects=comm is not None),
            name="gdn_fwd" + cx["tag"],
        )(pg, sm, cw, prm, nw, *comm_args)

    return call


def _make_gdn_bwd(seq, tb):
    ncb = tb // CH
    nb = seq // tb
    hb = tb // 8
    n = ncb * GDN_H

    def body(pg_ref, pre_ref, sm_ref, cw_ref, prm_ref, nw_ref, st_ref, ti_ref, uw_ref, doa_ref,
             dpg_ref, dsm_ref, dcw_ref, dprm_ref, dnw_ref, ds_scr, nxt_scr):
        i = pl.program_id(0)

        @pl.when(i == 0)
        def _():
            ds_scr[...] = jnp.zeros_like(ds_scr)
            nxt_scr[...] = jnp.zeros_like(nxt_scr)
            dcw_ref[...] = jnp.zeros_like(dcw_ref)
            dprm_ref[...] = jnp.zeros_like(dprm_ref)
            dnw_ref[...] = jnp.zeros_like(dnw_ref)

        masks = _masks()
        strict = masks[1]
        sm = sm_ref[...]
        cw = cw_ref[...]
        prm = prm_ref[...]
        raw, pre, act, beta, sp_in, g, gc = _gdn_common(pg_ref, None, sm, cw, prm, tb, pre=pre_ref[...])
        nw = nw_ref[0:1, :]
        row_id = _iota2((CH, 1), 0)
        d = _gdn_batch(act, beta, gc, gc.T, jnp.exp(gc), ncb, masks)
        t = ti_ref[...].reshape(n, CH, CH)
        sol = uw_ref[...].reshape(n, CH, 256)
        u, w = sol[:, :, :128], sol[:, :, 128:]
        s_prev = st_ref[...].reshape(n, 128, 128)
        v_new = u - _bdot(w, s_prev, _NN)
        o = _bdot(d["qg"], s_prev, _NN) + _bdot(d["attn"], v_new, _NN)

        pairs = [(c, h) for c in range(ncb) for h in range(GDN_H)]
        z = jnp.stack([pg_ref[c * CH:(c + 1) * CH, 1536 + h * 128:1536 + (h + 1) * 128] for c, h in pairs])
        doa = jnp.stack([doa_ref[c * CH:(c + 1) * CH, h * 128:(h + 1) * 128] for c, h in pairs])
        on, r, y = _rms_fwd(o, nw, GDN_D)
        dz = doa * y * _dsilu(z)
        do, dnw_rows = _rms_bwd(doa * _silu(z), on, r, nw, GDN_D)
        dnw_acc = jnp.sum(jnp.sum(dnw_rows, axis=0), axis=0, keepdims=True)

        dvn_in = _bdot(d["attn"], do, _TN)
        qgtdo = _bdot(d["qg"], do, _TN)
        dvn_l, dkdec_l, dgl_l = [None] * ncb, [None] * ncb, [None] * ncb
        for c in reversed(range(ncb)):
            bs = slice(c * GDN_H, (c + 1) * GDN_H)
            dsn = ds_scr[...]
            dvn_c = dvn_in[bs] + _bdot(d["kdec"][bs], dsn, _NN)
            ds_scr[...] = d["egl"][bs] * dsn + qgtdo[bs] - _bdot(w[bs], dvn_c, _TN)
            dvn_l[c] = dvn_c
            dkdec_l[c] = _bdot(v_new[bs], dsn, _NT)
            dgl_l[c] = d["egl"][bs] * jnp.sum(_rsum(s_prev[bs] * dsn), axis=1, keepdims=True)
        dvn = jnp.concatenate(dvn_l, axis=0)
        dkdec = jnp.concatenate(dkdec_l, axis=0)
        dglast = jnp.concatenate(dgl_l, axis=0)

        dqg = _bdot(do, s_prev, _NT)
        dattn = _bdot(do, v_new, _NT)
        dw = -_bdot(dvn, s_prev, _NT)
        drhs = _bdot(t, jnp.concatenate([dvn, dw], axis=2), _TN)
        dvb, dkg = drhs[:, :, :128], drhs[:, :, 128:]
        da = jnp.where(strict, -(_bdot(dvb, u, _NT) + _bdot(dkg, w, _NT)), 0.0)
        dp = da * d["decay"]
        dq_m = dattn * d["decay"]
        m = da * d["a"] + dattn * d["attn"]
        upper_tri = jnp.broadcast_to((_iota2((CH, CH), 1) >= _iota2((CH, CH), 0)).astype(BF16), (n, CH, CH))
        dg_in = _rsum(jnp.where(strict, _bdot(upper_tri, m, _NN), 0.0))
        dkb = _bdot(dp, d["kh"], _NN) + dkg * d["eg"]
        kdk_row = _rsum(dkdec * d["kdec"])
        dk = _bdot(dp, d["kb"], _TN) + _bdot(dq_m, d["qh"], _TN) + dkdec * d["kd_scale"] + dkb * d["bh"]
        dq = _bdot(dq_m, d["kh"], _NN) + dqg * d["eg"]
        dglast = dglast + jnp.sum(kdk_row, axis=1, keepdims=True)
        dgcol = (_rsum(dqg * d["qg"]) + _rsum(dkg * d["kg"]) - kdk_row + jnp.where(row_id == CH - 1, dglast, 0.0))
        dbeta = _rsum(dkb * d["kh"]) + _rsum(dvb * d["vh"])
        dn = dq * (GDN_D ** -0.5)
        dact_q = d["rq"] * (dn - d["qn"] * _rsum(dn * d["qn"]))
        dact_k = d["rk"] * (dk - d["kh"] * _rsum(dk * d["kh"]))
        dact_v = dvb * d["bh"]

        def lanes(v, lane0):
            return jnp.concatenate(
                [sum(_put_lane(v[c * GDN_H + h], lane0 + h) for h in range(GDN_H)) for c in range(ncb)], axis=0)

        def tokens(v):
            return jnp.concatenate(
                [jnp.concatenate([v[c * GDN_H + h] for h in range(GDN_H)], axis=1) for c in range(ncb)], axis=0)

        dbeta_all = lanes(dbeta, 0)
        dg = _dot01l(_chunk_tri(tb, upper=True), lanes(dgcol, 4)) + lanes(dg_in, 4)
        neg_ea = -jnp.exp(prm[0:1, :])
        da_raw = dg * neg_ea * _sigmoid(sp_in)
        db_raw = dbeta_all * beta * (1.0 - beta)
        dsm_ref[...] = (da_raw + db_raw).astype(dsm_ref.dtype)
        lane8 = _iota2((8, 128), 1)
        sub8 = _iota2((8, 128), 0)
        dalog = jnp.sum(dg * g, axis=0, keepdims=True)
        ddtb = jnp.sum(da_raw, axis=0, keepdims=True)
        dprm_ref[...] += jnp.where(sub8 == 0, dalog, 0.0) + jnp.where(sub8 == 1, ddtb, 0.0)
        dnw_ref[...] += jnp.where(sub8 == 0, dnw_acc, 0.0)

        dact = jnp.concatenate([tokens(dact_q), tokens(dact_k), tokens(dact_v)], axis=1)
        dpre = dact * _dsilu(pre)
        back = _conv_back(dpre, nxt_scr[...], tb)
        nxt_scr[...] = dpre[0:8, :]
        draw = back[0] * cw[3:4, :] + back[1] * cw[2:3, :] + back[2] * cw[1:2, :] + back[3] * cw[0:1, :]
        dpg_ref[:, 0:1536] = draw.astype(dpg_ref.dtype)
        dpg_ref[:, 1536:2048] = tokens(dz).astype(dpg_ref.dtype)
        sub_c = _iota2((8, 1536), 0)
        dcw_new = jnp.zeros((8, 1536), F32)
        for s_ in range(CONV_W):
            dcw_new = dcw_new + jnp.where(sub_c == 3 - s_, jnp.sum(back[s_] * raw, axis=0, keepdims=True), 0.0)
        dcw_ref[...] += dcw_new

    def call(pg, pre, sm, cw, prm, nw, st, ti, uw, doa, comm=None, comm_args=()):
        rev = lambda i: (nb - 1 - i, 0)
        const = lambda i: (0, 0)
        cx = _exchange_specs(comm)
        return pl.pallas_call(
            _with_exchange(body, comm, 10, 5, nb),
            grid=(nb,),
            in_specs=[
                pl.BlockSpec((tb, 2048), rev),
                pl.BlockSpec((tb, 1536), rev),
                pl.BlockSpec((tb, 128), rev),
                pl.BlockSpec((8, 1536), const),
                pl.BlockSpec((8, 128), const),
                pl.BlockSpec((8, 128), const),
                pl.BlockSpec((ncb, GDN_H, 128, 128), lambda i: (nb - 1 - i, 0, 0, 0)),
                pl.BlockSpec((ncb, GDN_H, CH, CH), lambda i: (nb - 1 - i, 0, 0, 0)),
                pl.BlockSpec((ncb, GDN_H, CH, 256), lambda i: (nb - 1 - i, 0, 0, 0)),
                pl.BlockSpec((tb, 512), rev),
            ] + cx["specs"],
            out_specs=[
                pl.BlockSpec((tb, 2048), rev),
                pl.BlockSpec((tb, 128), rev),
                pl.BlockSpec((8, 1536), const),
                pl.BlockSpec((8, 128), const),
                pl.BlockSpec((8, 128), const),
            ] + cx["specs"],
            out_shape=[
                jax.ShapeDtypeStruct((seq, 2048), BF16),
                jax.ShapeDtypeStruct((seq, 128), BF16),
                jax.ShapeDtypeStruct((8, 1536), F32),
                jax.ShapeDtypeStruct((8, 128), F32),
                jax.ShapeDtypeStruct((8, 128), F32),
            ] + cx["out_shape"],
            scratch_shapes=[pltpu.VMEM((GDN_H, 128, 128), F32), pltpu.VMEM((8, 1536), F32)] + cx["scratch"],
            compiler_params=pltpu.CompilerParams(dimension_semantics=("arbitrary",), vmem_limit_bytes=VMEM_LIMIT,
                                                 has_side_effects=comm is not None),
            name="gdn_bwd" + cx["tag"],
        )(pg, pre, sm, cw, prm, nw, st, ti, uw, doa, *comm_args)

    return call


def _expand_mat():
    r = _iota2((128, SSD_W), 0)
    c = _iota2((128, SSD_W), 1)
    return (jnp.right_shift(c, 6) == r).astype(F32)


def _reduce_heads(v, e):
    vh, vl = _split(v)
    eb = e.astype(BF16)
    nt = (((1,), (1,)), ((), ()))
    return (lax.dot_general(vh, eb, nt, preferred_element_type=F32)
            + lax.dot_general(vl, eb, nt, preferred_element_type=F32))


def _reduce_heads1(v, e):
    nt = (((1,), (1,)), ((), ()))
    return lax.dot_general(v.astype(BF16), e.astype(BF16), nt, preferred_element_type=F32)


def _row8(v):
    return jnp.broadcast_to(v, (8, v.shape[1]))


def _ssd_common(ps_ref, halo8, ss, cw, cb, prm, tb, pre=None):
    raw = ps_ref[:, 0:1536]
    taps = None
    if pre is None:
        taps = _conv_taps(raw, halo8, tb)
        pre = taps[0] * cw[3:4, :] + taps[1] * cw[2:3, :] + taps[2] * cw[1:2, :] + taps[3] * cw[0:1, :] + cb[0:1, :]
    act = _silu(pre)
    dt_in = ss + prm[1:2, :]
    dt = _softplus(dt_in)
    a = dt * (-jnp.exp(prm[0:1, :]))
    acum = _dot01l(_chunk_tri(tb), a)
    e = _expand_mat()
    dt_e = _dot01r(dt, e)
    xdt = act[:, 0:SSD_W] * dt_e
    ea_e = _dot01r(jnp.exp(acum), e)
    d_e = _dot01r(_row8(prm[2:3, :]), e)[0:1, :]
    return raw, taps, pre, act, dt_in, dt, a, acum, e, dt_e, xdt, ea_e, d_e


def _ssd_chunk(act, acum, act_t, e, c):
    r0 = c * CH
    rows = slice(r0, r0 + CH)
    alast = acum[r0 + CH - 1:r0 + CH, :]
    wdec = jnp.exp(alast - acum[rows, :])
    wd_e = _dot01r(wdec, e)
    eal_e = _dot01r(_row8(jnp.exp(alast)), e)[0:1, :]
    return rows, wd_e, eal_e


def _ssd_lmat(acum, act_t, c, h, causal):
    r0 = c * CH
    acol = acum[r0:r0 + CH, h:h + 1]
    arow = act_t[h:h + 1, r0:r0 + CH]
    return jnp.exp(jnp.where(causal, acol - arow, NEG))


def _make_ssd_fwd(seq, tb):
    ncb = tb // CH
    nb = seq // tb
    hg = SSD_H // SSD_G
    gw = SSD_W // SSD_G

    def body(ps_ref, ss_ref, cw_ref, cb_ref, prm_ref, nw_ref, ob_ref, st_ref, pre_ref, y_ref, hs_scr, halo_scr):
        @pl.when(pl.program_id(0) == 0)
        def _():
            hs_scr[...] = jnp.zeros_like(hs_scr)
            halo_scr[...] = jnp.zeros_like(halo_scr)

        causal, _, _ = _masks()
        (raw, _, pre, act, _, _, _, acum, e, _, xdt, ea_e, d_e) = _ssd_common(
            ps_ref, halo_scr[...], ss_ref[...], cw_ref[...], cb_ref[...], prm_ref[...], tb)
        halo_scr[...] = raw[tb - 8:tb, :]
        pre_ref[...] = pre
        act_t = acum.T
        nw = nw_ref[0:1, :]
        for c in range(ncb):
            rows, wd_e, eal_e = _ssd_chunk(act, acum, act_t, e, c)
            st_ref[c] = hs_scr[...]
            ys = []
            for g in range(SSD_G):
                gc_ = slice(g * gw, (g + 1) * gw)
                bg = act[rows, SSD_W + g * 128:SSD_W + (g + 1) * 128]
                cg = act[rows, SSD_W + 256 + g * 128:SSD_W + 256 + (g + 1) * 128]
                cbm = _dot_nt(cg, bg)
                hs = hs_scr[:, gc_]
                yin = _dot(cg, hs)
                yh = []
                for hh in range(hg):
                    h = g * hg + hh
                    lm = _ssd_lmat(acum, act_t, c, h, causal)
                    yh.append(_dot(cbm * lm, xdt[rows, h * SSD_P:(h + 1) * SSD_P]))
                ys.append(jnp.concatenate(yh, axis=1) + yin * ea_e[rows, gc_])
                hs_scr[:, gc_] = hs * eal_e[:, gc_] + _dot_tn(bg, xdt[rows, gc_] * wd_e[:, gc_])
            y = jnp.concatenate(ys, axis=1) + act[rows, 0:SSD_W] * d_e
            y_ref[rows, :] = y
            yz = y * _silu(ps_ref[rows, 1536:2560])
            outs = [_rms_fwd(yz[:, g * gw:(g + 1) * gw], nw[:, g * gw:(g + 1) * gw], gw)[2] for g in range(SSD_G)]
            ob_ref[rows, :] = jnp.concatenate(outs, axis=1).astype(ob_ref.dtype)

    def call(ps, ss, cw, cb, prm, nw):
        const = lambda i: (0, 0)
        return pl.pallas_call(
            body,
            grid=(nb,),
            in_specs=[
                pl.BlockSpec((tb, 2560), lambda i: (i, 0)),
                pl.BlockSpec((tb, 128), lambda i: (i, 0)),
                pl.BlockSpec((8, 1536), const),
                pl.BlockSpec((8, 1536), const),
                pl.BlockSpec((8, 128), const),
                pl.BlockSpec((8, SSD_W), const),
            ],
            out_specs=[
                pl.BlockSpec((tb, SSD_W), lambda i: (i, 0)),
                pl.BlockSpec((ncb, SSD_N, SSD_W), lambda i: (i, 0, 0)),
                pl.BlockSpec((tb, 1536), lambda i: (i, 0)),
                pl.BlockSpec((tb, SSD_W), lambda i: (i, 0)),
            ],
            out_shape=[
                jax.ShapeDtypeStruct((seq, SSD_W), BF16),
                jax.ShapeDtypeStruct((seq // CH, SSD_N, SSD_W), F32),
                jax.ShapeDtypeStruct((seq, 1536), F32),
                jax.ShapeDtypeStruct((seq, SSD_W), F32),
            ],
            scratch_shapes=[pltpu.VMEM((SSD_N, SSD_W), F32), pltpu.VMEM((8, 1536), F32)],
            compiler_params=pltpu.CompilerParams(dimension_semantics=("arbitrary",), vmem_limit_bytes=VMEM_LIMIT),
            name="ssd_fwd",
        )(ps, ss, cw, cb, prm, nw)

    return call


def _make_ssd_bwd(seq, tb):
    ncb = tb // CH
    nb = seq // tb
    hb = tb // 8
    hg = SSD_H // SSD_G
    gw = SSD_W // SSD_G

    def body(ps_ref, pre_ref, y_ref, ss_ref, cw_ref, cb_ref, prm_ref, nw_ref, st_ref, dob_ref,
             dps_ref, dss_ref, dcw_ref, dcb_ref, dprm_ref, dnw_ref, dhs_scr, nxt_scr):
        i = pl.program_id(0)

        @pl.when(i == 0)
        def _():
            dhs_scr[...] = jnp.zeros_like(dhs_scr)
            nxt_scr[...] = jnp.zeros_like(nxt_scr)
            dcw_ref[...] = jnp.zeros_like(dcw_ref)
            dcb_ref[...] = jnp.zeros_like(dcb_ref)
            dprm_ref[...] = jnp.zeros_like(dprm_ref)
            dnw_ref[...] = jnp.zeros_like(dnw_ref)

        causal, _, _ = _masks()
        cw = cw_ref[...]
        prm = prm_ref[...]
        (raw, _, pre, act, dt_in, dt, a, acum, e, dt_e, xdt, ea_e, d_e) = _ssd_common(
            ps_ref, None, ss_ref[...], cw, cb_ref[...], prm, tb, pre=pre_ref[...])
        act_t = acum.T
        nw = nw_ref[0:1, :]

        dx_l, db_l, dc_l, dz_l, ddt_l, da_l = ([None] * ncb for _ in range(6))
        upper_tri = (_iota2((CH, CH), 1) >= _iota2((CH, CH), 0)).astype(F32)
        tri_pair = jnp.concatenate([upper_tri, (_iota2((CH, CH), 1) < _iota2((CH, CH), 0)).astype(F32)], axis=1)
        below = jnp.bitwise_and(_iota2((CH, gw), 1), CH - 1) < _iota2((CH, gw), 0)
        dnw_acc = jnp.zeros((1, SSD_W), F32)
        dd_acc = jnp.zeros((1, SSD_W), F32)

        for c in reversed(range(ncb)):
            rows, wd_e, eal_e = _ssd_chunk(act, acum, act_t, e, c)
            xc = act[rows, 0:SSD_W]
            z = ps_ref[rows, 1536:2560]
            dob = dob_ref[rows, :]
            sz = _silu(z)
            dy_g, dz_g, dxdt_g, db_g, dc_g, da_g = [], [], [], [], [], []
            for g in range(SSD_G):
                gc_ = slice(g * gw, (g + 1) * gw)
                bg = act[rows, SSD_W + g * 128:SSD_W + (g + 1) * 128]
                cg = act[rows, SSD_W + 256 + g * 128:SSD_W + 256 + (g + 1) * 128]
                cbm = _dot_nt(cg, bg)
                hs = st_ref[c, :, gc_]
                yin = _dot(cg, hs)
                lmats = [_ssd_lmat(acum, act_t, c, g * hg + hh, causal) for hh in range(hg)]
                ea_g = ea_e[rows, gc_]
                y = y_ref[rows, gc_]
                yz = y * sz[:, gc_]
                on, r, _ = _rms_fwd(yz, nw[:, gc_], gw)
                dyz, dnw_rows = _rms_bwd(dob[:, gc_], on, r, nw[:, gc_], gw)
                dnw_acc = dnw_acc + _put_cols(jnp.sum(dnw_rows, axis=0, keepdims=True), g, gw)
                dy = dyz * sz[:, gc_]
                dz_g.append(dyz * y * _dsilu(z[:, gc_]))
                dd_acc = dd_acc + _put_cols(jnp.sum(dy * xc[:, gc_], axis=0, keepdims=True), g, gw)
                dhs_n = dhs_scr[:, gc_]
                dyin = dy * ea_g
                dcg = _dot_nt(dyin, hs)
                xw = xdt[rows, gc_] * wd_e[:, gc_]
                dbg = _dot_nt(xw, dhs_n)
                dxw = _dot(bg, dhs_n)
                dhs_scr[:, gc_] = dhs_n * eal_e[:, gc_] + _dot_tn(cg, dyin)
                dxi, ms, dcbm = [], [], jnp.zeros((CH, CH), F32)
                for hh in range(hg):
                    h = g * hg + hh
                    hc = slice(hh * SSD_P, (hh + 1) * SSD_P)
                    dyh = dy[:, hc]
                    lm = cbm * lmats[hh]
                    dxi.append(_dot_tn(lm, dyh))
                    dlm = _dot_nt(dyh, xdt[rows, h * SSD_P:(h + 1) * SSD_P])
                    ms.append(dlm * lm)
                    dcbm = dcbm + dlm * lmats[hh]
                dx_intra = jnp.concatenate(dxi, axis=1)
                ncat = _dot(upper_tri, jnp.concatenate(ms, axis=1))
                cum = _dot(tri_pair, jnp.concatenate([dy * yin * ea_g, dxw * xw], axis=0))
                da_g.append(jnp.where(below, ncat, 0.0) + cum
                            + jnp.sum(hs * dhs_n, axis=0, keepdims=True) * eal_e[:, gc_])
                dxdt_g.append(dx_intra + dxw * wd_e[:, gc_])
                dy_g.append(dy)
                db_g.append(dbg + _dot_tn(dcbm, cg))
                dc_g.append(dcg + _dot(dcbm, bg))
            dy = jnp.concatenate(dy_g, axis=1)
            dxdt = jnp.concatenate(dxdt_g, axis=1)
            dx_l[c] = dxdt * dt_e[rows, :] + dy * d_e
            db_l[c] = jnp.concatenate(db_g, axis=1)
            dc_l[c] = jnp.concatenate(dc_g, axis=1)
            dz_l[c] = jnp.concatenate(dz_g, axis=1)
            ddt_l[c] = _reduce_heads1(dxdt * xc, e)
            da_l[c] = _reduce_heads1(jnp.concatenate(da_g, axis=1), e)

        da = jnp.concatenate(da_l, axis=0)
        neg_ea = -jnp.exp(prm[0:1, :])
        ddt = jnp.concatenate(ddt_l, axis=0) + da * neg_ea
        ddt_in = ddt * _sigmoid(dt_in)
        dss_ref[...] = ddt_in.astype(dss_ref.dtype)
        sub8 = _iota2((8, 128), 0)
        dalog = jnp.sum(da * a, axis=0, keepdims=True)
        ddtb = jnp.sum(ddt_in, axis=0, keepdims=True)
        dd = _reduce_heads(_row8(dd_acc), e)[0:1, :]
        dprm_ref[...] += (jnp.where(sub8 == 0, dalog, 0.0) + jnp.where(sub8 == 1, ddtb, 0.0)
                          + jnp.where(sub8 == 2, dd, 0.0))
        dnw_ref[...] += jnp.where(_iota2((8, SSD_W), 0) == 0, dnw_acc, 0.0)

        dact = jnp.concatenate([jnp.concatenate(dx_l, axis=0), jnp.concatenate(db_l, axis=0),
                                jnp.concatenate(dc_l, axis=0)], axis=1)
        dpre = dact * _dsilu(pre)
        back = _conv_back(dpre, nxt_scr[...], tb)
        nxt_scr[...] = dpre[0:8, :]
        draw = back[0] * cw[3:4, :] + back[1] * cw[2:3, :] + back[2] * cw[1:2, :] + back[3] * cw[0:1, :]
        dps_ref[:, 0:1536] = draw.astype(dps_ref.dtype)
        dps_ref[:, 1536:2560] = jnp.concatenate(dz_l, axis=0).astype(dps_ref.dtype)
        sub_c = _iota2((8, 1536), 0)
        dcw_new = jnp.zeros((8, 1536), F32)
        for s_ in range(CONV_W):
            dcw_new = dcw_new + jnp.where(sub_c == 3 - s_, jnp.sum(back[s_] * raw, axis=0, keepdims=True), 0.0)
        dcw_ref[...] += dcw_new
        dcb_ref[...] += jnp.where(sub_c == 0, jnp.sum(dpre, axis=0, keepdims=True), 0.0)

    def call(ps, pre, y, ss, cw, cb, prm, nw, st, dob, comm=None, comm_args=()):
        rev = lambda i: (nb - 1 - i, 0)
        const = lambda i: (0, 0)
        cx = _exchange_specs(comm)
        return pl.pallas_call(
            _with_exchange(body, comm, 10, 6, nb),
            grid=(nb,),
            in_specs=[
                pl.BlockSpec((tb, 2560), rev),
                pl.BlockSpec((tb, 1536), rev),
                pl.BlockSpec((tb, SSD_W), rev),
                pl.BlockSpec((tb, 128), rev),
                pl.BlockSpec((8, 1536), const),
                pl.BlockSpec((8, 1536), const),
                pl.BlockSpec((8, 128), const),
                pl.BlockSpec((8, SSD_W), const),
                pl.BlockSpec((ncb, SSD_N, SSD_W), lambda i: (nb - 1 - i, 0, 0)),
                pl.BlockSpec((tb, SSD_W), rev),
            ] + cx["specs"],
            out_specs=[
                pl.BlockSpec((tb, 2560), rev),
                pl.BlockSpec((tb, 128), rev),
                pl.BlockSpec((8, 1536), const),
                pl.BlockSpec((8, 1536), const),
                pl.BlockSpec((8, 128), const),
                pl.BlockSpec((8, SSD_W), const),
            ] + cx["specs"],
            out_shape=[
                jax.ShapeDtypeStruct((seq, 2560), BF16),
                jax.ShapeDtypeStruct((seq, 128), BF16),
                jax.ShapeDtypeStruct((8, 1536), F32),
                jax.ShapeDtypeStruct((8, 1536), F32),
                jax.ShapeDtypeStruct((8, 128), F32),
                jax.ShapeDtypeStruct((8, SSD_W), F32),
            ] + cx["out_shape"],
            scratch_shapes=[pltpu.VMEM((SSD_N, SSD_W), F32), pltpu.VMEM((8, 1536), F32)] + cx["scratch"],
            compiler_params=pltpu.CompilerParams(dimension_semantics=("arbitrary",), vmem_limit_bytes=VMEM_LIMIT,
                                                 has_side_effects=comm is not None),
            name="ssd_bwd" + cx["tag"],
        )(ps, pre, y, ss, cw, cb, prm, nw, st, dob, *comm_args)

    return call


def _ret_consts(h):
    lg = math.log(1.0 - 2.0 ** (-5.0 - h))
    r = _iota2((CH, CH), 0)
    c = _iota2((CH, CH), 1)
    rel = (r - c).astype(F32)
    dmat = jnp.where(r >= c, jnp.exp(jnp.maximum(rel, 0.0) * lg), 0.0)
    idx = _iota2((CH, 1), 0).astype(F32)
    qdec = jnp.exp((idx + 1.0) * lg)
    kdec = jnp.exp((CH - 1.0 - idx) * lg)
    cdec = math.exp(CH * lg)
    return dmat, qdec, kdec, cdec


def _ret_batch(pr_ref, cc_ref, ss_ref, ncb):
    pairs = [(c, h) for c in range(ncb) for h in range(RET_H)]

    def st(off):
        return jnp.stack([pr_ref[c * CH:(c + 1) * CH, off + h * 128:off + (h + 1) * 128] for c, h in pairs])

    cc = jnp.stack([cc_ref[c * CH:(c + 1) * CH, :] for c, _ in pairs])
    ss = jnp.stack([ss_ref[c * CH:(c + 1) * CH, :] for c, _ in pairs])
    consts = [_ret_consts(h) for h in range(RET_H)]
    dmat = jnp.stack([consts[h][0] for _, h in pairs])
    qdec = jnp.stack([consts[h][1] for _, h in pairs])
    kdec = jnp.stack([consts[h][2] for _, h in pairs])
    cdec = jnp.stack([jnp.full((1, 1), consts[h][3], F32) for h in range(RET_H)])
    q = _rot(st(0), cc, ss)
    k = _rot(st(512), cc, ss) * (RET_D ** -0.5)
    return dict(q=q, k=k, v=st(1024), z=st(1536), cc=cc, ss=ss, dmat=dmat, qdec=qdec, kdec=kdec, cdec=cdec,
                s=_bdot(q, k, _NT) * dmat)


def _rot(t, cc, ss):
    return t * cc + pltpu.roll(t, 64, axis=t.ndim - 1) * ss


def _rot_bwd(d, cc, ss):
    return d * cc + pltpu.roll(d * ss, 64, axis=d.ndim - 1)


def _make_ret_fwd(seq, tb):
    ncb = tb // CH
    nb = seq // tb

    def body(pr_ref, cc_ref, ss_ref, nw_ref, oc_ref, st_ref, r_scr):
        @pl.when(pl.program_id(0) == 0)
        def _():
            r_scr[...] = jnp.zeros_like(r_scr)

        d = _ret_batch(pr_ref, cc_ref, ss_ref, ncb)
        kd = d["k"] * d["kdec"]
        for c in range(ncb):
            bs = slice(c * RET_H, (c + 1) * RET_H)
            rs = r_scr[...]
            st_ref[c] = rs
            r_scr[...] = rs * d["cdec"] + _bdot(kd[bs], d["v"][bs], _TN)
        r_prev = st_ref[...].reshape(ncb * RET_H, 128, 128)
        o = _bdot(d["s"], d["v"], _NN) + _bdot(d["q"], r_prev, _NN) * d["qdec"]
        _, _, y = _rms_fwd(o, nw_ref[0:1, :], RET_D)
        out = y * _silu(d["z"])
        for c in range(ncb):
            for h in range(RET_H):
                oc_ref[c * CH:(c + 1) * CH, h * 128:(h + 1) * 128] = out[c * RET_H + h].astype(oc_ref.dtype)

    def call(pr, cc, ss, nw):
        return pl.pallas_call(
            body,
            grid=(nb,),
            in_specs=[
                pl.BlockSpec((tb, 2048), lambda i: (i, 0)),
                pl.BlockSpec((tb, 128), lambda i: (i, 0)),
                pl.BlockSpec((tb, 128), lambda i: (i, 0)),
                pl.BlockSpec((8, 128), lambda i: (0, 0)),
            ],
            out_specs=[
                pl.BlockSpec((tb, 512), lambda i: (i, 0)),
                pl.BlockSpec((ncb, RET_H, 128, 128), lambda i: (i, 0, 0, 0)),
            ],
            out_shape=[
                jax.ShapeDtypeStruct((seq, 512), BF16),
                jax.ShapeDtypeStruct((seq // CH, RET_H, 128, 128), F32),
            ],
            scratch_shapes=[pltpu.VMEM((RET_H, 128, 128), F32)],
            compiler_params=pltpu.CompilerParams(dimension_semantics=("arbitrary",), vmem_limit_bytes=VMEM_LIMIT),
            name="ret_fwd",
        )(pr, cc, ss, nw)

    return call


def _make_ret_bwd(seq, tb):
    ncb = tb // CH
    nb = seq // tb

    def body(pr_ref, cc_ref, ss_ref, nw_ref, st_ref, doc_ref, dpr_ref, dnw_ref, dr_scr):
        @pl.when(pl.program_id(0) == 0)
        def _():
            dr_scr[...] = jnp.zeros_like(dr_scr)
            dnw_ref[...] = jnp.zeros_like(dnw_ref)

        nw = nw_ref[0:1, :]
        scale = RET_D ** -0.5
        n = ncb * RET_H
        d = _ret_batch(pr_ref, cc_ref, ss_ref, ncb)
        q, k, v, z, s = d["q"], d["k"], d["v"], d["z"], d["s"]
        r_prev = st_ref[...].reshape(n, 128, 128)
        o = _bdot(s, v, _NN) + _bdot(q, r_prev, _NN) * d["qdec"]
        doc = jnp.stack([doc_ref[c * CH:(c + 1) * CH, h * 128:(h + 1) * 128]
                         for c in range(ncb) for h in range(RET_H)])
        on, r, y = _rms_fwd(o, nw, RET_D)
        dz = doc * y * _dsilu(z)
        do, dnw_rows = _rms_bwd(doc * _silu(z), on, r, nw, RET_D)
        dnw_acc = jnp.sum(jnp.sum(dnw_rows, axis=0), axis=0, keepdims=True)
        dqd = do * d["qdec"]
        qtd = _bdot(q, dqd, _TN)
        drn_l = [None] * ncb
        for c in reversed(range(ncb)):
            drn_l[c] = dr_scr[...]
            dr_scr[...] = qtd[c * RET_H:(c + 1) * RET_H] + d["cdec"] * drn_l[c]
        drn = jnp.concatenate(drn_l, axis=0)
        ds = _bdot(do, v, _NT) * d["dmat"]
        dq = _rot_bwd(_bdot(ds, k, _NN) + _bdot(dqd, r_prev, _NT), d["cc"], d["ss"])
        dk = _rot_bwd((_bdot(ds, q, _TN) + _bdot(v, drn, _NT) * d["kdec"]) * scale, d["cc"], d["ss"])
        dv = _bdot(s, do, _TN) + _bdot(k * d["kdec"], drn, _NN)
        for c in range(ncb):
            rows = slice(c * CH, (c + 1) * CH)
            for h in range(RET_H):
                b = c * RET_H + h
                for j, val in enumerate((dq, dk, dv, dz)):
                    dpr_ref[rows, j * 512 + h * 128:j * 512 + (h + 1) * 128] = val[b].astype(dpr_ref.dtype)
        dnw_ref[...] += jnp.where(_iota2((8, 128), 0) == 0, dnw_acc, 0.0)

    def call(pr, cc, ss, nw, st, doc):
        rev = lambda i: (nb - 1 - i, 0)
        return pl.pallas_call(
            body,
            grid=(nb,),
            in_specs=[
                pl.BlockSpec((tb, 2048), rev),
                pl.BlockSpec((tb, 128), rev),
                pl.BlockSpec((tb, 128), rev),
                pl.BlockSpec((8, 128), lambda i: (0, 0)),
                pl.BlockSpec((ncb, RET_H, 128, 128), lambda i: (nb - 1 - i, 0, 0, 0)),
                pl.BlockSpec((tb, 512), rev),
            ],
            out_specs=[
                pl.BlockSpec((tb, 2048), rev),
                pl.BlockSpec((8, 128), lambda i: (0, 0)),
            ],
            out_shape=[
                jax.ShapeDtypeStruct((seq, 2048), BF16),
                jax.ShapeDtypeStruct((8, 128), F32),
            ],
            scratch_shapes=[pltpu.VMEM((RET_H, 128, 128), F32)],
            compiler_params=pltpu.CompilerParams(dimension_semantics=("arbitrary",), vmem_limit_bytes=VMEM_LIMIT),
            name="ret_bwd",
        )(pr, cc, ss, nw, st, doc)

    return call


def _rope_tables(seq):
    half = RET_D // 2
    inv = ROPE_BASE ** (-jnp.arange(half, dtype=F32) / half)
    hi = (CH * jnp.arange(seq // CH, dtype=jnp.int32)).astype(F32)[:, None] * inv[None, :]
    lo = jnp.arange(CH, dtype=jnp.int32).astype(F32)[:, None] * inv[None, :]
    ch, sh, cl, sl = jnp.cos(hi)[:, None, :], jnp.sin(hi)[:, None, :], jnp.cos(lo)[None], jnp.sin(lo)[None]
    cos = (ch * cl - sh * sl).reshape(seq, half)
    sin = (sh * cl + ch * sl).reshape(seq, half)
    return jnp.concatenate([cos, cos], axis=1), jnp.concatenate([-sin, sin], axis=1)


SEG_G, SEG_S, SEG_R, SEG_GS, SEG_SS = (0, 2048), (2048, 4608), (4608, 6656), (6656, 6784), (6784, 6912)
NP = 6912
SEGS = (SEG_G, SEG_S, SEG_R, SEG_GS, SEG_SS)


def _resident(shape):
    return pl.BlockSpec(shape, lambda i: (0,) * len(shape), pipeline_mode=pl.Buffered(1))


def _make_inproj(seq, tl):
    def body(x_ref, pn_ref, w_ref, pg_ref, ps_ref, pr_ref, gs_ref, ss_ref, ht_ref):
        x = x_ref[...]
        _, _, hn = _rms_fwd(x, pn_ref[0:1, :], D_MODEL)
        h = hn.astype(BF16)
        ht_ref[...] = hn.T.astype(BF16)
        for (a, b), o_ref in zip(SEGS, (pg_ref, ps_ref, pr_ref, gs_ref, ss_ref)):
            o_ref[...] = jnp.dot(h, w_ref[:, a:b], preferred_element_type=F32)

    def call(x, pn, w, comm=None, comm_args=()):
        row = lambda i: (i, 0)
        cx = _exchange_specs(comm)
        return pl.pallas_call(
            _with_exchange(body, comm, 3, 6, seq // tl),
            grid=(seq // tl,),
            in_specs=[pl.BlockSpec((tl, D_MODEL), row), _resident((8, D_MODEL)), _resident((D_MODEL, NP))]
            + cx["specs"],
            out_specs=[pl.BlockSpec((tl, b - a), row) for a, b in SEGS]
            + [pl.BlockSpec((D_MODEL, tl), lambda i: (0, i))] + cx["specs"],
            out_shape=[jax.ShapeDtypeStruct((seq, b - a), F32) for a, b in SEGS]
            + [jax.ShapeDtypeStruct((D_MODEL, seq), BF16)] + cx["out_shape"],
            scratch_shapes=cx["scratch"],
            compiler_params=pltpu.CompilerParams(dimension_semantics=("arbitrary",), vmem_limit_bytes=VMEM_LIMIT,
                                                 has_side_effects=comm is not None),
            name="inproj" + cx["tag"],
        )(x, pn, w, *comm_args)

    return call


def _make_outproj(seq, tl):
    def body(oa_ref, ob_ref, oc_ref, w_ref, x_ref, qn_ref, out_ref, xn_ref):
        out = (jnp.dot(oa_ref[...], w_ref[0:512, :], preferred_element_type=F32)
               + jnp.dot(ob_ref[...], w_ref[512:1536, :], preferred_element_type=F32)
               + jnp.dot(oc_ref[...], w_ref[1536:2048, :], preferred_element_type=F32))
        out_ref[...] = out
        _, _, y = _rms_fwd(out, qn_ref[0:1, :], D_MODEL)
        xn_ref[...] = x_ref[...] + y

    def call(oa, ob, oc, w, x, qn):
        row = lambda i: (i, 0)
        return pl.pallas_call(
            body,
            grid=(seq // tl,),
            in_specs=[pl.BlockSpec((tl, 512), row), pl.BlockSpec((tl, 1024), row), pl.BlockSpec((tl, 512), row),
                      _resident((2048, D_MODEL)), pl.BlockSpec((tl, D_MODEL), row), _resident((8, D_MODEL))],
            out_specs=[pl.BlockSpec((tl, D_MODEL), row), pl.BlockSpec((tl, D_MODEL), row)],
            out_shape=[jax.ShapeDtypeStruct((seq, D_MODEL), F32), jax.ShapeDtypeStruct((seq, D_MODEL), F32)],
            compiler_params=pltpu.CompilerParams(dimension_semantics=("arbitrary",), vmem_limit_bytes=VMEM_LIMIT),
            name="outproj",
        )(oa, ob, oc, w, x, qn)

    return call


def _make_outproj_loss(seq, tl):
    def body(oa_ref, ob_ref, oc_ref, w_ref, x_ref, qn_ref, t_ref, out_ref, dy_ref, loss_ref):
        @pl.when(pl.program_id(0) == 0)
        def _():
            loss_ref[...] = jnp.zeros_like(loss_ref)

        out = (jnp.dot(oa_ref[...], w_ref[0:512, :], preferred_element_type=F32)
               + jnp.dot(ob_ref[...], w_ref[512:1536, :], preferred_element_type=F32)
               + jnp.dot(oc_ref[...], w_ref[1536:2048, :], preferred_element_type=F32))
        out_ref[...] = out
        _, _, y = _rms_fwd(out, qn_ref[0:1, :], D_MODEL)
        err = (x_ref[...] + y) - t_ref[...]
        dy_ref[...] = err * (1.0 / D_MODEL)
        part = jnp.sum(jnp.sum(err * err, axis=1, keepdims=True), axis=0, keepdims=True) * (0.5 / D_MODEL)
        loss_ref[...] += jnp.where((_iota2((8, 128), 0) == 0) & (_iota2((8, 128), 1) == 0), part, 0.0)

    def call(oa, ob, oc, w, x, qn, t):
        row = lambda i: (i, 0)
        return pl.pallas_call(
            body,
            grid=(seq // tl,),
            in_specs=[pl.BlockSpec((tl, 512), row), pl.BlockSpec((tl, 1024), row), pl.BlockSpec((tl, 512), row),
                      _resident((2048, D_MODEL)), pl.BlockSpec((tl, D_MODEL), row), _resident((8, D_MODEL)),
                      pl.BlockSpec((tl, D_MODEL), row)],
            out_specs=[pl.BlockSpec((tl, D_MODEL), row), pl.BlockSpec((tl, D_MODEL), row),
                       pl.BlockSpec((8, 128), lambda i: (0, 0))],
            out_shape=[jax.ShapeDtypeStruct((seq, D_MODEL), F32), jax.ShapeDtypeStruct((seq, D_MODEL), F32),
                       jax.ShapeDtypeStruct((8, 128), F32)],
            compiler_params=pltpu.CompilerParams(dimension_semantics=("arbitrary",), vmem_limit_bytes=VMEM_LIMIT),
            name="outproj_loss",
        )(oa, ob, oc, w, x, qn, t)

    return call


def _make_outproj_bwd(seq, tl):
    def body(dxn_ref, out_ref, oa_ref, ob_ref, oc_ref, w_ref, qn_ref,
             doa_ref, dob_ref, doc_ref, dqn_ref, dw_ref, dwb_ref):
        @pl.when(pl.program_id(0) == 0)
        def _():
            dqn_ref[...] = jnp.zeros_like(dqn_ref)
            dw_ref[...] = jnp.zeros_like(dw_ref)

        qn = qn_ref[0:1, :]
        on, r, _ = _rms_fwd(out_ref[...], qn, D_MODEL)
        dout, dqn_rows = _rms_bwd(dxn_ref[...], on, r, qn, D_MODEL)
        dqn_ref[...] += jnp.where(_iota2((8, D_MODEL), 0) == 0, jnp.sum(dqn_rows, axis=0, keepdims=True), 0.0)
        db = dout.astype(BF16)
        nt = (((1,), (1,)), ((), ()))
        tn = (((0,), (0,)), ((), ()))
        doa_ref[...] = lax.dot_general(db, w_ref[0:512, :], nt, preferred_element_type=F32).astype(BF16)
        dob_ref[...] = lax.dot_general(db, w_ref[512:1536, :], nt, preferred_element_type=F32).astype(BF16)
        doc_ref[...] = lax.dot_general(db, w_ref[1536:2048, :], nt, preferred_element_type=F32).astype(BF16)
        dw_ref[0:512, :] += lax.dot_general(oa_ref[...], db, tn, preferred_element_type=F32)
        dw_ref[512:1536, :] += lax.dot_general(ob_ref[...], db, tn, preferred_element_type=F32)
        dw_ref[1536:2048, :] += lax.dot_general(oc_ref[...], db, tn, preferred_element_type=F32)

        @pl.when(pl.program_id(0) == seq // tl - 1)
        def _():
            dwb_ref[...] = dw_ref[...].astype(BF16)

    def call(dxn, out, oa, ob, oc, w, qn):
        row = lambda i: (i, 0)
        const = lambda i: (0, 0)
        return pl.pallas_call(
            body,
            grid=(seq // tl,),
            in_specs=[pl.BlockSpec((tl, D_MODEL), row), pl.BlockSpec((tl, D_MODEL), row),
                      pl.BlockSpec((tl, 512), row), pl.BlockSpec((tl, 1024), row), pl.BlockSpec((tl, 512), row),
                      _resident((2048, D_MODEL)), _resident((8, D_MODEL))],
            out_specs=[pl.BlockSpec((tl, 512), row), pl.BlockSpec((tl, 1024), row), pl.BlockSpec((tl, 512), row),
                       pl.BlockSpec((8, D_MODEL), const), pl.BlockSpec((2048, D_MODEL), const),
                       pl.BlockSpec((2048, D_MODEL), const)],
            out_shape=[jax.ShapeDtypeStruct((seq, 512), BF16), jax.ShapeDtypeStruct((seq, 1024), BF16),
                       jax.ShapeDtypeStruct((seq, 512), BF16), jax.ShapeDtypeStruct((8, D_MODEL), F32),
                       jax.ShapeDtypeStruct((2048, D_MODEL), F32), jax.ShapeDtypeStruct((2048, D_MODEL), BF16)],
            compiler_params=pltpu.CompilerParams(dimension_semantics=("arbitrary",), vmem_limit_bytes=VMEM_LIMIT),
            name="outproj_bwd",
        )(dxn, out, oa, ob, oc, w, qn)

    return call


def _make_inproj_bwd_dx(seq, tl):
    def body(dg_ref, ds_ref, dr_ref, dgs_ref, dss_ref, w_ref, x_ref, pn_ref, dxn_ref, dx_ref, dpn_ref):
        @pl.when(pl.program_id(0) == 0)
        def _():
            dpn_ref[...] = jnp.zeros_like(dpn_ref)

        nt = (((1,), (1,)), ((), ()))
        dh = jnp.zeros((tl, D_MODEL), F32)
        for (a, b), d_ref in zip(SEGS, (dg_ref, ds_ref, dr_ref, dgs_ref, dss_ref)):
            dh = dh + lax.dot_general(d_ref[...], w_ref[:, a:b], nt, preferred_element_type=F32)
        pn = pn_ref[0:1, :]
        on, r, _ = _rms_fwd(x_ref[...], pn, D_MODEL)
        dx, dpn_rows = _rms_bwd(dh, on, r, pn, D_MODEL)
        dx_ref[...] = dx + dxn_ref[...]
        dpn_ref[...] += jnp.where(_iota2((8, D_MODEL), 0) == 0, jnp.sum(dpn_rows, axis=0, keepdims=True), 0.0)

    def call(dg, ds, dr, dgs, dss, w, x, pn, dxn, comm=None, comm_args=()):
        row = lambda i: (i, 0)
        cx = _exchange_specs(comm)
        return pl.pallas_call(
            _with_exchange(body, comm, 9, 2, seq // tl),
            grid=(seq // tl,),
            in_specs=[pl.BlockSpec((tl, b - a), row) for a, b in SEGS]
            + [_resident((D_MODEL, NP)), pl.BlockSpec((tl, D_MODEL), row), _resident((8, D_MODEL)),
               pl.BlockSpec((tl, D_MODEL), row)] + cx["specs"],
            out_specs=[pl.BlockSpec((tl, D_MODEL), row), pl.BlockSpec((8, D_MODEL), lambda i: (0, 0))] + cx["specs"],
            out_shape=[jax.ShapeDtypeStruct((seq, D_MODEL), F32), jax.ShapeDtypeStruct((8, D_MODEL), F32)]
            + cx["out_shape"],
            scratch_shapes=cx["scratch"],
            compiler_params=pltpu.CompilerParams(dimension_semantics=("arbitrary",), vmem_limit_bytes=VMEM_LIMIT,
                                                 has_side_effects=comm is not None),
            name="inproj_bwd_dx" + cx["tag"],
        )(dg, ds, dr, dgs, dss, w, x, pn, dxn, *comm_args)

    return call


def _make_inproj_bwd_dw_small(seq, tl, name):
    def body(ht_ref, a_ref, b_ref, dw_ref):
        @pl.when(pl.program_id(0) == 0)
        def _():
            dw_ref[...] = jnp.zeros_like(dw_ref)

        ht = ht_ref[...]
        dw_ref[:, 0:128] += jnp.dot(ht, a_ref[...], preferred_element_type=F32)
        dw_ref[:, 128:256] += jnp.dot(ht, b_ref[...], preferred_element_type=F32)

    def call(ht, a, b):
        return pl.pallas_call(
            body,
            grid=(seq // tl,),
            in_specs=[pl.BlockSpec((D_MODEL, tl), lambda i: (0, i)), pl.BlockSpec((tl, 128), lambda i: (i, 0)),
                      pl.BlockSpec((tl, 128), lambda i: (i, 0))],
            out_specs=pl.BlockSpec((D_MODEL, 256), lambda i: (0, 0)),
            out_shape=jax.ShapeDtypeStruct((D_MODEL, 256), F32),
            compiler_params=pltpu.CompilerParams(dimension_semantics=("arbitrary",), vmem_limit_bytes=VMEM_LIMIT),
            name=name,
        )(ht, a, b)

    return call


def _make_inproj_bwd_dw(seq, tl, width, tn, name):
    def body(ht_ref, d_ref, dw_ref):
        @pl.when(pl.program_id(1) == 0)
        def _():
            dw_ref[...] = jnp.zeros_like(dw_ref)

        dw_ref[...] += jnp.dot(ht_ref[...], d_ref[...], preferred_element_type=F32)

    def call(ht, d):
        return pl.pallas_call(
            body,
            grid=(width // tn, seq // tl),
            in_specs=[pl.BlockSpec((D_MODEL, tl), lambda j, i: (0, i)), pl.BlockSpec((tl, tn), lambda j, i: (i, j))],
            out_specs=pl.BlockSpec((D_MODEL, tn), lambda j, i: (0, j)),
            out_shape=jax.ShapeDtypeStruct((D_MODEL, width), F32),
            compiler_params=pltpu.CompilerParams(dimension_semantics=("arbitrary", "arbitrary"),
                                                 vmem_limit_bytes=VMEM_LIMIT),
            name=name,
        )(ht, d)

    return call


ADAM_LR, ADAM_B1, ADAM_B2, ADAM_EPS, ADAM_WD, ADAM_STEP = 0.001, 0.9, 0.999, 1e-08, 0.01, 10


def _adam_math(w, g, m, v):
    m = ADAM_B1 * m + (1.0 - ADAM_B1) * g
    v = ADAM_B2 * v + (1.0 - ADAM_B2) * (g * g)
    m_hat = m / (1.0 - ADAM_B1 ** ADAM_STEP)
    v_hat = v / (1.0 - ADAM_B2 ** ADAM_STEP)
    delta = -ADAM_LR * (m_hat / (jnp.sqrt(v_hat) + ADAM_EPS) + ADAM_WD * w)
    return delta, m, v


def _adamw(w, g, m, v, name):
    shape = w.shape
    cols = shape[-1]
    rows = w.size // cols
    tr = rows if rows <= 512 else 256
    assert rows % tr == 0

    def body(w_ref, g_ref, m_ref, v_ref, d_ref, mo_ref, vo_ref):
        d_ref[...], mo_ref[...], vo_ref[...] = _adam_math(w_ref[...], g_ref[...], m_ref[...], v_ref[...])

    spec = pl.BlockSpec((tr, cols), lambda i: (i, 0))
    outs = pl.pallas_call(
        body,
        grid=(rows // tr,),
        in_specs=[spec] * 4,
        out_specs=[spec] * 3,
        out_shape=[jax.ShapeDtypeStruct((rows, cols), F32)] * 3,
        compiler_params=pltpu.CompilerParams(dimension_semantics=("arbitrary",), vmem_limit_bytes=VMEM_LIMIT),
        name=name,
    )(*[a.reshape(rows, cols) for a in (w, g, m, v)])
    return (g,) + tuple(o.reshape(shape) for o in outs)


def _adamw_pairs(w, mine, theirs, m, v, name):
    na, r, cols = w.shape
    assert na == 2
    tr = 256
    assert r % tr == 0

    def body(w_ref, a0_ref, b0_ref, a1_ref, b1_ref, m_ref, v_ref, g_ref, d_ref, mo_ref, vo_ref):
        g = jnp.where(pl.program_id(0) == 0, a0_ref[...] + b0_ref[...], a1_ref[...] + b1_ref[...])
        g_ref[...] = g
        d_ref[...], mo_ref[...], vo_ref[...] = _adam_math(w_ref[...], g, m_ref[...], v_ref[...])

    nblk = r // tr
    full = pl.BlockSpec((None, tr, cols), lambda a, i: (a, i, 0))
    lay0 = pl.BlockSpec((None, tr, cols), lambda a, i: (0, i * (1 - a) + (nblk - 1) * a, 0))
    lay1 = pl.BlockSpec((None, tr, cols), lambda a, i: (0, i * a, 0))
    return pl.pallas_call(
        body,
        grid=(na, nblk),
        in_specs=[full, lay0, lay0, lay1, lay1, full, full],
        out_specs=[full] * 4,
        out_shape=[jax.ShapeDtypeStruct(w.shape, F32)] * 4,
        compiler_params=pltpu.CompilerParams(dimension_semantics=("arbitrary",) * 2, vmem_limit_bytes=VMEM_LIMIT),
        name=name,
    )(w, mine[0], theirs[0], mine[1], theirs[1], m, v)


MESH = pl.DeviceIdType.MESH
ANY = pl.BlockSpec(memory_space=pl.ANY)
CHIP_REL = ((1, 0), (0, 1), (1, 1))


def _flip(v, d):
    return 1 - v if d else v


def _ag_chips(arrs, name):
    n = len(arrs)

    def body(*refs):
        ins, outs = refs[:n], refs[n:2 * n]
        send_sems, recv_sems, loc_sems = refs[2 * n:]
        x, y, c = lax.axis_index("x"), lax.axis_index("y"), lax.axis_index("c")
        me = 2 * x + y

        def remote(a, k, slot):
            dx, dy = CHIP_REL[k]
            return pltpu.make_async_remote_copy(
                src_ref=ins[a], dst_ref=outs[a].at[slot], send_sem=send_sems.at[a * 3 + k],
                recv_sem=recv_sems.at[a * 3 + k], device_id=(_flip(x, dx), _flip(y, dy), c), device_id_type=MESH)

        local = [pltpu.make_async_copy(ins[a], outs[a].at[me], loc_sems.at[a]) for a in range(n)]
        for cp in local:
            cp.start()
        for a in range(n):
            for k in range(3):
                remote(a, k, me).start()
        for a in range(n):
            for k, (dx, dy) in enumerate(CHIP_REL):
                remote(a, k, 2 * _flip(x, dx) + _flip(y, dy)).wait_recv()
        for a in range(n):
            for k in range(3):
                remote(a, k, me).wait_send()
        for cp in local:
            cp.wait()

    return pl.pallas_call(
        body,
        in_specs=[ANY] * n,
        out_specs=[ANY] * n,
        out_shape=[jax.ShapeDtypeStruct((4,) + a.shape, a.dtype) for a in arrs],
        scratch_shapes=[pltpu.SemaphoreType.DMA((3 * n,)), pltpu.SemaphoreType.DMA((3 * n,)),
                        pltpu.SemaphoreType.DMA((n,))],
        compiler_params=pltpu.CompilerParams(has_side_effects=True),
        name=name,
    )(*arrs)


class _ChipExchange:
    def __init__(self, kind, arrs, swap=()):
        self.kind, self.n_chip, self.n = kind, len(arrs), len(arrs) + len(swap)
        if kind == "gather":
            self.out_shape = [jax.ShapeDtypeStruct((4,) + a.shape, a.dtype) for a in arrs]
        else:
            self.out_shape = [jax.ShapeDtypeStruct((3,) + a.shape[1:], a.dtype) for a in arrs]
        self.out_shape += [jax.ShapeDtypeStruct(a.shape, a.dtype) for a in swap]
        self.scratch = [pltpu.SemaphoreType.DMA((4 * self.n,)), pltpu.SemaphoreType.DMA((4 * self.n,))]

    def _copies(self, ins, outs, sems):
        send_sems, recv_sems = sems
        x, y, c = lax.axis_index("x"), lax.axis_index("y"), lax.axis_index("c")
        me = 2 * x + y
        pairs = []
        for a in range(self.n_chip, self.n):
            cp = pltpu.make_async_remote_copy(
                src_ref=ins[a], dst_ref=outs[a], send_sem=send_sems.at[4 * a], recv_sem=recv_sems.at[4 * a],
                device_id=(x, y, 1 - c), device_id_type=MESH)
            pairs.append((cp, cp))
        for a in range(self.n_chip):
            for k, (dx, dy) in enumerate(CHIP_REL):
                px, py = _flip(x, dx), _flip(y, dy)
                sem = dict(send_sem=send_sems.at[4 * a + k], recv_sem=recv_sems.at[4 * a + k],
                           device_id=(px, py, c), device_id_type=MESH)
                if self.kind == "gather":
                    out = pltpu.make_async_remote_copy(src_ref=ins[a], dst_ref=outs[a].at[me], **sem)
                    inc = pltpu.make_async_remote_copy(src_ref=ins[a], dst_ref=outs[a].at[2 * px + py], **sem)
                else:
                    out = pltpu.make_async_remote_copy(src_ref=ins[a].at[2 * px + py], dst_ref=outs[a].at[k], **sem)
                    inc = out
                pairs.append((out, inc))
            if self.kind == "gather":
                own = pltpu.make_async_remote_copy(
                    src_ref=ins[a], dst_ref=outs[a].at[me], send_sem=send_sems.at[4 * a + 3],
                    recv_sem=recv_sems.at[4 * a + 3], device_id=(x, y, 1 - c), device_id_type=MESH)
                pairs.append((own, own))
        return pairs

    def start(self, ins, outs, sems):
        for out, _ in self._copies(ins, outs, sems):
            out.start()

    def finish(self, ins, outs, sems):
        pairs = self._copies(ins, outs, sems)
        for _, inc in pairs:
            inc.wait_recv()
        for out, _ in pairs:
            out.wait_send()


def _with_exchange(body, comm, n_in, n_out, nb):
    if comm is None:
        return body

    def wrapped(*refs):
        ins = refs[:n_in]
        c_in = refs[n_in:n_in + comm.n]
        outs = refs[n_in + comm.n:n_in + comm.n + n_out]
        c_out = refs[n_in + comm.n + n_out:n_in + 2 * comm.n + n_out]
        rest = refs[n_in + 2 * comm.n + n_out:]
        scratch, sems = rest[:len(rest) - 2], rest[len(rest) - 2:]

        @pl.when(pl.program_id(0) == 0)
        def _():
            comm.start(c_in, c_out, sems)

        body(*ins, *outs, *scratch)

        @pl.when(pl.program_id(0) == nb - 1)
        def _():
            comm.finish(c_in, c_out, sems)

    return wrapped


def _exchange_specs(comm):
    if comm is None:
        return dict(specs=[], out_shape=[], scratch=[], tag="")
    return dict(specs=[pl.BlockSpec(memory_space=pl.ANY)] * comm.n, out_shape=list(comm.out_shape),
                scratch=list(comm.scratch), tag="_" + comm.kind)


def _half(ref_or_shape, half):
    r = ref_or_shape[-2] // 2
    return pl.ds(half * r, r)


def _ag_rows(arrs, name):
    n = len(arrs)

    def body(*refs):
        ins, outs = refs[:n], refs[n:2 * n]
        send_sems, recv_sems, fsend_sems, frecv_sems, loc_sems = refs[2 * n:]
        x, y, c = lax.axis_index("x"), lax.axis_index("y"), lax.axis_index("c")
        me = 2 * x + y
        sib = (x, y, 1 - c)

        def chip_of(k):
            dx, dy = CHIP_REL[k]
            return _flip(x, dx), _flip(y, dy)

        def ici(a, k, slot):
            px, py = chip_of(k)
            rows = _half(arrs[a].shape, c)
            return pltpu.make_async_remote_copy(
                src_ref=ins[a].at[:, rows, :], dst_ref=outs[a].at[slot, :, rows, :], send_sem=send_sems.at[a * 3 + k],
                recv_sem=recv_sems.at[a * 3 + k], device_id=(px, py, c), device_id_type=MESH)

        def fwd(a, k, half):
            px, py = chip_of(k)
            blk = outs[a].at[2 * px + py, :, _half(arrs[a].shape, half), :]
            return pltpu.make_async_remote_copy(
                src_ref=blk, dst_ref=blk, send_sem=fsend_sems.at[a * 3 + k], recv_sem=frecv_sems.at[a * 3 + k],
                device_id=sib, device_id_type=MESH)

        own = [pltpu.make_async_remote_copy(src_ref=ins[a], dst_ref=outs[a].at[me], send_sem=loc_sems.at[a],
                                            recv_sem=loc_sems.at[n + a], device_id=sib, device_id_type=MESH)
               for a in range(n)]
        for cp in own:
            cp.start()
        for a in range(n):
            for k in range(3):
                ici(a, k, me).start()
        for a in range(n):
            for k in range(3):
                px, py = chip_of(k)
                ici(a, k, 2 * px + py).wait_recv()
                fwd(a, k, c).start()
        for a in range(n):
            for k in range(3):
                fwd(a, k, 1 - c).wait_recv()
        for a in range(n):
            for k in range(3):
                ici(a, k, me).wait_send()
                fwd(a, k, c).wait_send()
        for cp in own:
            cp.wait()

    return pl.pallas_call(
        body,
        in_specs=[ANY] * n,
        out_specs=[ANY] * n,
        out_shape=[jax.ShapeDtypeStruct((4,) + a.shape, a.dtype) for a in arrs],
        scratch_shapes=[pltpu.SemaphoreType.DMA((3 * n,)) for _ in range(4)] + [pltpu.SemaphoreType.DMA((2 * n,))],
        compiler_params=pltpu.CompilerParams(has_side_effects=True),
        name=name,
    )(*arrs)


def _sum_chips(own, recv, chip, name):
    _, na, r, cols = own.shape
    tr = 256
    assert r % tr == 0

    def body(chip_ref, o_ref, r_ref, s_ref):
        s_ref[...] = ((o_ref[...] + r_ref[0].astype(F32)) + r_ref[1].astype(F32)) + r_ref[2].astype(F32)

    return pl.pallas_call(
        body,
        grid_spec=pltpu.PrefetchScalarGridSpec(
            num_scalar_prefetch=1,
            grid=(na, r // tr),
            in_specs=[pl.BlockSpec((None, None, tr, cols), lambda a, i, ch: (ch[0], a, i, 0)),
                      pl.BlockSpec((3, None, tr, cols), lambda a, i, ch: (0, a, i, 0))],
            out_specs=pl.BlockSpec((None, tr, cols), lambda a, i, ch: (a, i, 0))),
        out_shape=jax.ShapeDtypeStruct((na, r, cols), F32),
        compiler_params=pltpu.CompilerParams(dimension_semantics=("arbitrary",) * 2, vmem_limit_bytes=VMEM_LIMIT),
        name=name,
    )(chip, own, recv)


def _swap_sibling(arrs, name):
    n = len(arrs)

    def body(*refs):
        ins, outs = refs[:n], refs[n:2 * n]
        send_sems, recv_sems = refs[2 * n:]
        x, y, c = lax.axis_index("x"), lax.axis_index("y"), lax.axis_index("c")
        cps = [pltpu.make_async_remote_copy(src_ref=ins[a], dst_ref=outs[a], send_sem=send_sems.at[a],
                                            recv_sem=recv_sems.at[a], device_id=(x, y, 1 - c), device_id_type=MESH)
               for a in range(n)]
        for cp in cps:
            cp.start()
        for cp in cps:
            cp.wait_recv()
        for cp in cps:
            cp.wait_send()

    return pl.pallas_call(
        body,
        in_specs=[ANY] * n,
        out_specs=[ANY] * n,
        out_shape=[jax.ShapeDtypeStruct(a.shape, a.dtype) for a in arrs],
        scratch_shapes=[pltpu.SemaphoreType.DMA((n,)), pltpu.SemaphoreType.DMA((n,))],
        compiler_params=pltpu.CompilerParams(has_side_effects=True),
        name=name,
    )(*arrs)


def _allreduce_small(vec, name):
    rows = vec.shape[0]

    def body(v_ref, out_ref, gat_ref, send_sems, recv_sems):
        x, y, c = lax.axis_index("x"), lax.axis_index("y"), lax.axis_index("c")
        me = 4 * x + 2 * y + c

        def remote(k, slot):
            dx, dy, dc = (k >> 2) & 1, (k >> 1) & 1, k & 1
            return pltpu.make_async_remote_copy(
                src_ref=v_ref, dst_ref=gat_ref.at[slot], send_sem=send_sems.at[k - 1], recv_sem=recv_sems.at[k - 1],
                device_id=(_flip(x, dx), _flip(y, dy), _flip(c, dc)), device_id_type=MESH)

        gat_ref[me] = v_ref[...]
        for k in range(1, 8):
            remote(k, me).start()
        for k in range(1, 8):
            dx, dy, dc = (k >> 2) & 1, (k >> 1) & 1, k & 1
            remote(k, 4 * _flip(x, dx) + 2 * _flip(y, dy) + _flip(c, dc)).wait_recv()
        for k in range(1, 8):
            remote(k, me).wait_send()
        acc = gat_ref[0]
        for j in range(1, 8):
            acc = acc + gat_ref[j]
        out_ref[...] = acc

    vm = pl.BlockSpec(memory_space=pltpu.VMEM)
    return pl.pallas_call(
        body,
        in_specs=[vm],
        out_specs=vm,
        out_shape=jax.ShapeDtypeStruct(vec.shape, F32),
        scratch_shapes=[pltpu.VMEM((8, rows, 128), F32), pltpu.SemaphoreType.DMA((7,)), pltpu.SemaphoreType.DMA((7,))],
        compiler_params=pltpu.CompilerParams(has_side_effects=True),
        name=name,
    )(vec)


def _pad8(v, width, lane0=0):
    v = v.reshape(1, -1) if v.ndim == 1 else v
    return jnp.zeros((8, width), F32).at[:v.shape[0], lane0:lane0 + v.shape[1]].set(v.astype(F32))


def _relayout_w_in(g):
    tr = 128
    q = N_IN // 4

    def body(g_ref, o_ref):
        w = jnp.concatenate([g_ref[j] for j in range(4)], axis=1)
        z = lambda n: jnp.zeros((tr, n), w.dtype)
        o_ref[...] = jnp.concatenate([w[:, 0:2048], w[:, 2056:4616], w[:, 4632:6680],
                                      w[:, 2048:2056], z(120), w[:, 4616:4632], z(112)], axis=1)

    return pl.pallas_call(
        body,
        grid=(D_MODEL // tr,),
        in_specs=[pl.BlockSpec((4, tr, q), lambda i: (0, i, 0))],
        out_specs=pl.BlockSpec((tr, NP), lambda i: (i, 0)),
        out_shape=jax.ShapeDtypeStruct((D_MODEL, NP), g.dtype),
        compiler_params=pltpu.CompilerParams(dimension_semantics=("arbitrary",), vmem_limit_bytes=VMEM_LIMIT),
        name="relayout_w_in",
    )(g)


def _unlayout_dw_in(dg, ds, dr, dsm):
    tr = 128
    q = N_IN // 4

    def body(g_ref, s_ref, r_ref, sm_ref, o_ref, ob_ref):
        w = jnp.concatenate([g_ref[...], sm_ref[:, 0:8], s_ref[...], sm_ref[:, 128:144], r_ref[...]], axis=1)
        for j in range(4):
            blk = w[:, q * j:q * (j + 1)]
            o_ref[j] = blk
            ob_ref[j] = blk.astype(BF16)

    row = lambda i: (i, 0)
    return pl.pallas_call(
        body,
        grid=(D_MODEL // tr,),
        in_specs=[pl.BlockSpec((tr, d.shape[1]), row) for d in (dg, ds, dr, dsm)],
        out_specs=[pl.BlockSpec((4, tr, q), lambda i: (0, i, 0))] * 2,
        out_shape=[jax.ShapeDtypeStruct((4, D_MODEL, q), F32), jax.ShapeDtypeStruct((4, D_MODEL, q), BF16)],
        compiler_params=pltpu.CompilerParams(dimension_semantics=("arbitrary",), vmem_limit_bytes=VMEM_LIMIT),
        name="unlayout_dw_in",
    )(dg, ds, dr, dsm)


TB = 256
TB_RET = 512
TL = 1024
TL_IN = 512
TL_OB = 1024
TK = 2048


def kernel(x, pre_norm, post_norm, w_in, gdn_conv, gdn_A_log, gdn_dt_bias, gdn_norm, ssd_conv, ssd_conv_b, ssd_A_log, ssd_dt_bias, ssd_D, ssd_norm, ret_norm, w_out, loss_target, m_pre_norm, m_post_norm, m_w_in, m_gdn_conv, m_gdn_A_log, m_gdn_dt_bias, m_gdn_norm, m_ssd_conv, m_ssd_conv_b, m_ssd_A_log, m_ssd_dt_bias, m_ssd_D, m_ssd_norm, m_ret_norm, m_w_out, v_pre_norm, v_post_norm, v_w_in, v_gdn_conv, v_gdn_A_log, v_gdn_dt_bias, v_gdn_norm, v_ssd_conv, v_ssd_conv_b, v_ssd_A_log, v_ssd_dt_bias, v_ssd_D, v_ssd_norm, v_ret_norm, v_w_out):
    seq = x.shape[1]
    chip = 2 * lax.axis_index("x") + lax.axis_index("y")
    x0 = x[0]

    wi_b, wo_b = w_in.astype(BF16), w_out.astype(BF16)
    (wi0_g,) = _ag_rows([wi_b[0:1]], "ag_weights")
    gcv_g, scv_g = _ag_chips([gdn_conv, ssd_conv], "ag_conv")
    full_w_in = _relayout_w_in
    wp = [full_w_in(wi0_g[:, 0]), None]
    wo = [None, None]
    ag0 = _ChipExchange("gather", [wo_b[0], wo_b[1]])
    ag1 = _ChipExchange("gather", [wi_b[1]])
    gcv = jnp.transpose(gcv_g, (1, 2, 0, 3)).reshape(DEPTH, CONV_W, 1536)
    scv = jnp.transpose(scv_g, (1, 2, 0, 3)).reshape(DEPTH, CONV_W, 1536)
    rope_c, rope_s = _rope_tables(seq)

    saved = []
    xc = x0
    for l in range(DEPTH):
        p = dict(
            pn=_pad8(pre_norm[l], D_MODEL), qn=_pad8(post_norm[l], D_MODEL),
            g_cw=_pad8(gcv[l], 1536), g_prm=_pad8(jnp.stack([gdn_A_log[l], gdn_dt_bias[l]]), 128, 4),
            g_nw=_pad8(gdn_norm[l], 128),
            s_cw=_pad8(scv[l], 1536), s_cb=_pad8(ssd_conv_b[l], 1536),
            s_prm=_pad8(jnp.stack([ssd_A_log[l], ssd_dt_bias[l], ssd_D[l]]), 128), s_nw=_pad8(ssd_norm[l], SSD_W),
            r_nw=_pad8(ret_norm[l], 128))
        if l == 0:
            pg, ps, pr, gs, ss, ht, wo0_g, wo1_g = _make_inproj(seq, TL_IN)(
                xc, p["pn"], wp[l], comm=ag0, comm_args=(wo_b[0], wo_b[1]))
            wo = [wo0_g.reshape(2048, D_MODEL), wo1_g.reshape(2048, D_MODEL)]
        else:
            pg, ps, pr, gs, ss, ht = _make_inproj(seq, TL_IN)(xc, p["pn"], wp[l])
        if l == 0:
            oa, stg, tig, uwg, gpre, wi1_g = _make_gdn_fwd(seq, TB)(
                pg, gs, p["g_cw"], p["g_prm"], p["g_nw"], comm=ag1, comm_args=(wi_b[1],))
            wp[1] = full_w_in(wi1_g)
        else:
            oa, stg, tig, uwg, gpre = _make_gdn_fwd(seq, TB)(pg, gs, p["g_cw"], p["g_prm"], p["g_nw"])
        ob, sts, spre, sy = _make_ssd_fwd(seq, TB)(ps, ss, p["s_cw"], p["s_cb"], p["s_prm"], p["s_nw"])
        oc, str_ = _make_ret_fwd(seq, TB_RET)(pr, rope_c, rope_s, p["r_nw"])
        if l == DEPTH - 1:
            out, dxn, lossp = _make_outproj_loss(seq, TL)(oa, ob, oc, wo[l], xc, p["qn"], loss_target[0])
            xn = None
        else:
            out, xn = _make_outproj(seq, TL)(oa, ob, oc, wo[l], xc, p["qn"])
        saved.append(dict(p=p, x=xc, ht=ht, spre=spre, sy=sy, gpre=gpre, pg=pg, ps=ps, pr=pr, gs=gs, ss=ss, stg=stg, tig=tig, uwg=uwg, sts=sts, str=str_,
                          oa=oa, ob=ob, oc=oc, out=out))
        xc = xn

    small = [None] * DEPTH
    gin, gin_b, gout, gout_b, q_in, q_out, s_in, s_out, t_in, t_out = ([None] * DEPTH for _ in range(10))
    chip1 = chip.astype(jnp.int32).reshape(1)

    def sum_chips(l):
        return (_sum_chips(gin[l][:, None], q_in[l][:, None], chip1, f"sum_chips_w_in{l}"),
                _sum_chips(gout[l][:, None], q_out[l][:, None], chip1, f"sum_chips_w_out{l}"))

    for l in reversed(range(DEPTH)):
        s = saved[l]
        p = s["p"]
        doa, dob, doc, dqn, dwo_l, dwo_b = _make_outproj_bwd(seq, TL_OB)(
            dxn, s["out"], s["oa"], s["ob"], s["oc"], wo[l], p["qn"])
        gout[l], gout_b[l] = dwo_l.reshape(4, 512, D_MODEL), dwo_b.reshape(4, 512, D_MODEL)
        gdn_args = (s["pg"], s["gpre"], s["gs"], p["g_cw"], p["g_prm"], p["g_nw"], s["stg"], s["tig"], s["uwg"], doa)
        if l == 0:
            payload = (gout_b[0],)
            dpg, dgs, dcw_g, dprm_g, dnw_g, q_out[0] = _make_gdn_bwd(seq, TB)(
                *gdn_args, comm=_ChipExchange("scatter", payload), comm_args=payload)
        else:
            dpg, dgs, dcw_g, dprm_g, dnw_g = _make_gdn_bwd(seq, TB)(*gdn_args)
        ssd_args = (s["ps"], s["spre"], s["sy"], s["ss"], p["s_cw"], p["s_cb"], p["s_prm"], p["s_nw"], s["sts"], dob)
        if l == 0:
            payload = (gin_b[1], gout_b[1])
            dps, dss, dcw_s, dcb_s, dprm_s, dnw_s, q_in[1], q_out[1] = _make_ssd_bwd(seq, TB)(
                *ssd_args, comm=_ChipExchange("scatter", payload), comm_args=payload)
        else:
            dps, dss, dcw_s, dcb_s, dprm_s, dnw_s = _make_ssd_bwd(seq, TB)(*ssd_args)
        dpr, dnw_r = _make_ret_bwd(seq, TB_RET)(s["pr"], rope_c, rope_s, p["r_nw"], s["str"], doc)
        dws = [_make_inproj_bwd_dw(seq, TK, d.shape[1], tn, f"inproj_bwd_dw{i}")(s["ht"], d)
               for i, (d, tn) in enumerate(((dpg, 2048), (dps, 1280), (dpr, 2048)))]
        dws.append(_make_inproj_bwd_dw_small(seq, TK, "inproj_bwd_dw3")(s["ht"], dgs, dss))
        gin[l], gin_b[l] = _unlayout_dw_in(*dws)
        dx_args = (dpg, dps, dpr, dgs, dss, wp[l], s["x"], p["pn"], dxn)
        if l == 0:
            s_in[1], s_out[1] = sum_chips(1)
            payload, swap = (gin_b[0],), (s_in[1], s_out[1])
            dx, dpn, q_in[0], t_in[1], t_out[1] = _make_inproj_bwd_dx(seq, TL_IN)(
                *dx_args, comm=_ChipExchange("scatter", payload, swap), comm_args=payload + swap)
        else:
            dx, dpn = _make_inproj_bwd_dx(seq, TL_IN)(*dx_args)
        small[l] = [dpn[0], dqn[0], dcw_g[0:4].reshape(-1), dprm_g[0, 4:8], dprm_g[1, 4:8], dnw_g[0],
                    dcw_s[0:4].reshape(-1), dcb_s[0], dprm_s[0, 0:16], dprm_s[1, 0:16], dprm_s[2, 0:16],
                    dnw_s[0], dnw_r[0]]
        dxn = dx
    grad_x = dxn[None]

    sizes = [a.shape[0] for a in small[0]]
    flat = jnp.concatenate(small[0] + small[1] + [lossp[0, 0:1]])
    n_flat = flat.shape[0]
    rows = -(-n_flat // 1024) * 8
    red = _allreduce_small(jnp.pad(flat, (0, rows * 128 - n_flat)).reshape(rows, 128), "allreduce_small").reshape(-1)
    per = sum(sizes)
    loss = red[2 * per]

    def pick(i):
        off = sum(sizes[:i])
        return jnp.stack([red[l * per + off:l * per + off + sizes[i]] for l in range(DEPTH)])

    g_small = dict(
        pre_norm=pick(0), post_norm=pick(1),
        gdn_conv=lax.dynamic_slice_in_dim(pick(2).reshape(DEPTH, CONV_W, 1536), chip * 384, 384, axis=2),
        gdn_A_log=pick(3), gdn_dt_bias=pick(4), gdn_norm=pick(5),
        ssd_conv=lax.dynamic_slice_in_dim(pick(6).reshape(DEPTH, CONV_W, 1536), chip * 384, 384, axis=2),
        ssd_conv_b=pick(7), ssd_A_log=pick(8), ssd_dt_bias=pick(9), ssd_D=pick(10), ssd_norm=pick(11),
        ret_norm=pick(12))

    s_in[0], s_out[0] = sum_chips(0)
    t_in[0], t_out[0] = _swap_sibling([s_in[0], s_out[0]], "swap_grads")

    weights = dict(pre_norm=pre_norm, post_norm=post_norm, w_in=w_in, gdn_conv=gdn_conv, gdn_A_log=gdn_A_log,
                   gdn_dt_bias=gdn_dt_bias, gdn_norm=gdn_norm, ssd_conv=ssd_conv, ssd_conv_b=ssd_conv_b,
                   ssd_A_log=ssd_A_log, ssd_dt_bias=ssd_dt_bias, ssd_D=ssd_D, ssd_norm=ssd_norm, ret_norm=ret_norm,
                   w_out=w_out)
    ms = dict(pre_norm=m_pre_norm, post_norm=m_post_norm, w_in=m_w_in, gdn_conv=m_gdn_conv, gdn_A_log=m_gdn_A_log,
              gdn_dt_bias=m_gdn_dt_bias, gdn_norm=m_gdn_norm, ssd_conv=m_ssd_conv, ssd_conv_b=m_ssd_conv_b,
              ssd_A_log=m_ssd_A_log, ssd_dt_bias=m_ssd_dt_bias, ssd_D=m_ssd_D, ssd_norm=m_ssd_norm,
              ret_norm=m_ret_norm, w_out=m_w_out)
    vs = dict(pre_norm=v_pre_norm, post_norm=v_post_norm, w_in=v_w_in, gdn_conv=v_gdn_conv, gdn_A_log=v_gdn_A_log,
              gdn_dt_bias=v_gdn_dt_bias, gdn_norm=v_gdn_norm, ssd_conv=v_ssd_conv, ssd_conv_b=v_ssd_conv_b,
              ssd_A_log=v_ssd_A_log, ssd_dt_bias=v_ssd_dt_bias, ssd_D=v_ssd_D, ssd_norm=v_ssd_norm,
              ret_norm=v_ret_norm, w_out=v_w_out)
    names = list(weights)
    res = {}
    for nme in names:
        if nme == "w_in":
            res[nme] = _adamw_pairs(w_in, s_in, t_in, m_w_in, v_w_in, "adamw_w_in")
        elif nme == "w_out":
            res[nme] = _adamw_pairs(w_out, s_out, t_out, m_w_out, v_w_out, "adamw_w_out")
        else:
            res[nme] = _adamw(weights[nme], g_small[nme], ms[nme], vs[nme], "adamw_" + nme)
    return (loss, grad_x, *[res[n][0] for n in names], *[res[n][1] for n in names],
            *[res[n][2] for n in names], *[res[n][3] for n in names])
```

```python
import math

import jax
import jax.numpy as jnp
from jax import lax
from jax.experimental import pallas as pl
from jax.experimental.pallas import tpu as pltpu

F32 = jnp.float32
BF16 = jnp.bfloat16

D_MODEL = 1024
DEPTH = 2
CH = 64
CONV_W = 4
EPS = 1e-6
GDN_H, GDN_D = 4, 128
SSD_H, SSD_P, SSD_N, SSD_G = 16, 64, 128, 2
SSD_W = SSD_H * SSD_P
RET_H, RET_D = 4, 128
ROPE_BASE = 10000.0
N_IN = 6680
NEG = -1e30

V7X_VMEM_BYTES = 64 * 1024 * 1024
VMEM_LIMIT = V7X_VMEM_BYTES * 7 // 8


def _dot(a, b):
    return jnp.dot(a.astype(BF16), b.astype(BF16), preferred_element_type=F32)


def _dot_nt(a, b):
    return lax.dot_general(a.astype(BF16), b.astype(BF16), (((1,), (1,)), ((), ())), preferred_element_type=F32)


def _dot_tn(a, b):
    return lax.dot_general(a.astype(BF16), b.astype(BF16), (((0,), (0,)), ((), ())), preferred_element_type=F32)


def _split(a):
    hi = a.astype(BF16)
    return hi, (a - hi.astype(F32)).astype(BF16)


def _dot01l(m, v):
    vh, vl = _split(v)
    mb = m.astype(BF16)
    return jnp.dot(mb, vh, preferred_element_type=F32) + jnp.dot(mb, vl, preferred_element_type=F32)


def _dot01r(v, m):
    vh, vl = _split(v)
    mb = m.astype(BF16)
    return jnp.dot(vh, mb, preferred_element_type=F32) + jnp.dot(vl, mb, preferred_element_type=F32)


def _sigmoid(x):
    return jax.nn.sigmoid(x)


def _silu(x):
    return x * _sigmoid(x)


def _dsilu(x):
    s = _sigmoid(x)
    return s * (1.0 + x * (1.0 - s))


def _softplus(x):
    return jnp.maximum(x, 0.0) + jnp.log1p(jnp.exp(-jnp.abs(x)))


def _iota2(shape, dim):
    return lax.broadcasted_iota(jnp.int32, shape, dim)


def _chunk_tri(tb, upper=False):
    r = _iota2((tb, tb), 0)
    c = _iota2((tb, tb), 1)
    same = jnp.right_shift(r, 6) == jnp.right_shift(c, 6)
    return (same & ((c >= r) if upper else (c <= r))).astype(F32)


def _masks():
    r = _iota2((CH, CH), 0)
    c = _iota2((CH, CH), 1)
    return r >= c, r > c, (r == c).astype(F32)


def _put_lane(col, lane_idx, width=128):
    lane = _iota2((col.shape[0], width), 1)
    return jnp.where(lane == lane_idx, col, 0.0)


def _conv_taps(raw, halo8, tb):
    ext = jnp.concatenate([halo8, raw], axis=0)
    return [raw] + [pltpu.roll(ext, s, axis=0)[8:] for s in (1, 2, 3)]


def _conv_back(dpre, nxt8, tb):
    ext = jnp.concatenate([dpre, nxt8], axis=0)
    return [dpre] + [pltpu.roll(ext, tb + 8 - s, axis=0)[:tb] for s in (1, 2, 3)]


def _rms_fwd(o, w, n):
    r = lax.rsqrt(jnp.sum(o * o, axis=-1, keepdims=True) * (1.0 / n) + EPS)
    on = o * r
    return on, r, on * w


def _rms_bwd(dy, on, r, w, n):
    don = dy * w
    return r * (don - on * (jnp.sum(don * on, axis=-1, keepdims=True) * (1.0 / n))), dy * on


def _put_cols(v, g, gw):
    z = jnp.zeros_like(v)
    return jnp.concatenate([v, z] if g == 0 else [z, v], axis=1)


def _gdn_common(pg_ref, halo8, sm, cw, prm, tb, pre=None):
    raw = pg_ref[:, 0:1536]
    if pre is None:
        taps = _conv_taps(raw, halo8, tb)
        pre = taps[0] * cw[3:4, :] + taps[1] * cw[2:3, :] + taps[2] * cw[1:2, :] + taps[3] * cw[0:1, :]
    act = _silu(pre)
    beta = _sigmoid(sm)
    sp_in = sm + prm[1:2, :]
    g = -jnp.exp(prm[0:1, :]) * _softplus(sp_in)
    gc = _dot01l(_chunk_tri(tb), g)
    return raw, pre, act, beta, sp_in, g, gc


_NN = (((2,), (1,)), ((0,), (0,)))
_NT = (((2,), (2,)), ((0,), (0,)))
_TN = (((1,), (1,)), ((0,), (0,)))


def _bdot(a, b, dn):
    return lax.dot_general(a.astype(BF16), b.astype(BF16), dn, preferred_element_type=F32)


def _binv_unit_lower(a, eye):
    r = _iota2((CH, CH), 0)
    c = _iota2((CH, CH), 1)
    d = eye - jnp.where((jnp.right_shift(r, 1) == jnp.right_shift(c, 1)), a, 0.0)
    ab = a.astype(BF16)
    zero = jnp.zeros((), BF16)
    for lb in range(1, 6):
        same = jnp.right_shift(r, lb + 1) == jnp.right_shift(c, lb + 1)
        low = (jnp.bitwise_and(jnp.right_shift(r, lb), 1) == 1) & (jnp.bitwise_and(jnp.right_shift(c, lb), 1) == 0)
        db = d.astype(BF16)
        t = _bdot(jnp.where(same & low, ab, zero), db, _NN)
        d = d - _bdot(db, t, _NN)
    return d


def _rsum(v):
    return jnp.sum(v, axis=-1, keepdims=True)


def _gdn_batch(act, beta, gc, gct, eg_all, ncb, masks):
    causal, strict, _ = masks

    def st(fn):
        return jnp.stack([fn(c, h, slice(c * CH, (c + 1) * CH)) for c in range(ncb) for h in range(GDN_H)])

    qr = st(lambda c, h, r: act[r, h * 128:(h + 1) * 128])
    kr = st(lambda c, h, r: act[r, 512 + h * 128:512 + (h + 1) * 128])
    vh = st(lambda c, h, r: act[r, 1024 + h * 128:1024 + (h + 1) * 128])
    bh = st(lambda c, h, r: beta[r, h:h + 1])
    gcol = st(lambda c, h, r: gc[r, 4 + h:5 + h])
    grow = st(lambda c, h, r: gct[4 + h:5 + h, r])
    eg = st(lambda c, h, r: eg_all[r, 4 + h:5 + h])
    glast = st(lambda c, h, r: gc[(c + 1) * CH - 1:(c + 1) * CH, 4 + h:5 + h])
    rq = lax.rsqrt(_rsum(qr * qr) + EPS)
    rk = lax.rsqrt(_rsum(kr * kr) + EPS)
    qn = qr * rq
    kh = kr * rk
    qh = qn * (GDN_D ** -0.5)
    decay = jnp.exp(jnp.where(causal, gcol - grow, NEG))
    kb = kh * bh
    kd_scale = jnp.exp(glast - gcol)
    return dict(qn=qn, rq=rq, kh=kh, rk=rk, qh=qh, vh=vh, bh=bh, eg=eg, decay=decay, kb=kb, vb=vh * bh, kg=kb * eg,
                qg=qh * eg, kd_scale=kd_scale, kdec=kh * kd_scale, egl=jnp.exp(glast),
                a=jnp.where(strict, _bdot(kb, kh, _NT) * decay, 0.0), attn=_bdot(qh, kh, _NT) * decay)


def _make_gdn_fwd(seq, tb):
    ncb = tb // CH
    nb = seq // tb
    n = ncb * GDN_H

    def body(pg_ref, sm_ref, cw_ref, prm_ref, nw_ref, oa_ref, st_ref, ti_ref, uw_ref, pre_ref, s_scr, halo_scr):
        @pl.when(pl.program_id(0) == 0)
        def _():
            s_scr[...] = jnp.zeros_like(s_scr)
            halo_scr[...] = jnp.zeros_like(halo_scr)

        masks = _masks()
        sm = sm_ref[...]
        raw, pre, act, beta, _, _, gc = _gdn_common(pg_ref, halo_scr[...], sm, cw_ref[...], prm_ref[...], tb)
        halo_scr[...] = raw[tb - 8:tb, :]
        pre_ref[...] = pre
        d = _gdn_batch(act, beta, gc, gc.T, jnp.exp(gc), ncb, masks)
        t = _binv_unit_lower(d["a"], masks[2])
        sol = _bdot(t, jnp.concatenate([d["vb"], d["kg"]], axis=2), _NN)
        ti_ref[...] = t.reshape(ncb, GDN_H, CH, CH)
        uw_ref[...] = sol.reshape(ncb, GDN_H, CH, 256)
        u, w = sol[:, :, :128], sol[:, :, 128:]
        vns = []
        for c in range(ncb):
            bs = slice(c * GDN_H, (c + 1) * GDN_H)
            s = s_scr[...]
            st_ref[c] = s
            vn = u[bs] - _bdot(w[bs], s, _NN)
            s_scr[...] = s * d["egl"][bs] + _bdot(d["kdec"][bs], vn, _TN)
            vns.append(vn)
        v_new = jnp.concatenate(vns, axis=0)
        s_prev = st_ref[...].reshape(n, 128, 128)
        o = _bdot(d["qg"], s_prev, _NN) + _bdot(d["attn"], v_new, _NN)
        _, _, y = _rms_fwd(o, nw_ref[0:1, :], GDN_D)
        for c in range(ncb):
            rows = slice(c * CH, (c + 1) * CH)
            for h in range(GDN_H):
                z = pg_ref[rows, 1536 + h * 128:1536 + (h + 1) * 128]
                oa_ref[rows, h * 128:(h + 1) * 128] = (y[c * GDN_H + h] * _silu(z)).astype(oa_ref.dtype)

    def call(pg, sm, cw, prm, nw, comm=None, comm_args=()):
        blk4 = lambda i: (i, 0, 0, 0)
        cx = _exchange_specs(comm)
        return pl.pallas_call(
            _with_exchange(body, comm, 5, 5, nb),
            grid=(nb,),
            in_specs=[
                pl.BlockSpec((tb, 2048), lambda i: (i, 0)),
                pl.BlockSpec((tb, 128), lambda i: (i, 0)),
                pl.BlockSpec((8, 1536), lambda i: (0, 0)),
                pl.BlockSpec((8, 128), lambda i: (0, 0)),
                pl.BlockSpec((8, 128), lambda i: (0, 0)),
            ] + cx["specs"],
            out_specs=[
                pl.BlockSpec((tb, 512), lambda i: (i, 0)),
                pl.BlockSpec((ncb, GDN_H, 128, 128), blk4),
                pl.BlockSpec((ncb, GDN_H, CH, CH), blk4),
                pl.BlockSpec((ncb, GDN_H, CH, 256), blk4),
                pl.BlockSpec((tb, 1536), lambda i: (i, 0)),
            ] + cx["specs"],
            out_shape=[
                jax.ShapeDtypeStruct((seq, 512), BF16),
                jax.ShapeDtypeStruct((seq // CH, GDN_H, 128, 128), F32),
                jax.ShapeDtypeStruct((seq // CH, GDN_H, CH, CH), F32),
                jax.ShapeDtypeStruct((seq // CH, GDN_H, CH, 256), F32),
                jax.ShapeDtypeStruct((seq, 1536), F32),
            ] + cx["out_shape"],
            scratch_shapes=[pltpu.VMEM((GDN_H, 128, 128), F32), pltpu.VMEM((8, 1536), F32)] + cx["scratch"],
            compiler_params=pltpu.CompilerParams(dimension_semantics=("arbitrary",), vmem_limit_bytes=VMEM_LIMIT,
                                                 has_side_effects=comm is not None),
            name="gdn_fwd" + cx["tag"],
        )(pg, sm, cw, prm, nw, *comm_args)

    return call


def _make_gdn_bwd(seq, tb):
    ncb = tb // CH
    nb = seq // tb
    hb = tb // 8
    n = ncb * GDN_H

    def body(pg_ref, pre_ref, sm_ref, cw_ref, prm_ref, nw_ref, st_ref, ti_ref, uw_ref, doa_ref,
             dpg_ref, dsm_ref, dcw_ref, dprm_ref, dnw_ref, ds_scr, nxt_scr):
        i = pl.program_id(0)

        @pl.when(i == 0)
        def _():
            ds_scr[...] = jnp.zeros_like(ds_scr)
            nxt_scr[...] = jnp.zeros_like(nxt_scr)
            dcw_ref[...] = jnp.zeros_like(dcw_ref)
            dprm_ref[...] = jnp.zeros_like(dprm_ref)
            dnw_ref[...] = jnp.zeros_like(dnw_ref)

        masks = _masks()
        strict = masks[1]
        sm = sm_ref[...]
        cw = cw_ref[...]
        prm = prm_ref[...]
        raw, pre, act, beta, sp_in, g, gc = _gdn_common(pg_ref, None, sm, cw, prm, tb, pre=pre_ref[...])
        nw = nw_ref[0:1, :]
        row_id = _iota2((CH, 1), 0)
        d = _gdn_batch(act, beta, gc, gc.T, jnp.exp(gc), ncb, masks)
        t = ti_ref[...].reshape(n, CH, CH)
        sol = uw_ref[...].reshape(n, CH, 256)
        u, w = sol[:, :, :128], sol[:, :, 128:]
        s_prev = st_ref[...].reshape(n, 128, 128)
        v_new = u - _bdot(w, s_prev, _NN)
        o = _bdot(d["qg"], s_prev, _NN) + _bdot(d["attn"], v_new, _NN)

        pairs = [(c, h) for c in range(ncb) for h in range(GDN_H)]
        z = jnp.stack([pg_ref[c * CH:(c + 1) * CH, 1536 + h * 128:1536 + (h + 1) * 128] for c, h in pairs])
        doa = jnp.stack([doa_ref[c * CH:(c + 1) * CH, h * 128:(h + 1) * 128] for c, h in pairs])
        on, r, y = _rms_fwd(o, nw, GDN_D)
        dz = doa * y * _dsilu(z)
        do, dnw_rows = _rms_bwd(doa * _silu(z), on, r, nw, GDN_D)
        dnw_acc = jnp.sum(jnp.sum(dnw_rows, axis=0), axis=0, keepdims=True)

        dvn_in = _bdot(d["attn"], do, _TN)
        qgtdo = _bdot(d["qg"], do, _TN)
        dvn_l, dkdec_l, dgl_l = [None] * ncb, [None] * ncb, [None] * ncb
        for c in reversed(range(ncb)):
            bs = slice(c * GDN_H, (c + 1) * GDN_H)
            dsn = ds_scr[...]
            dvn_c = dvn_in[bs] + _bdot(d["kdec"][bs], dsn, _NN)
            ds_scr[...] = d["egl"][bs] * dsn + qgtdo[bs] - _bdot(w[bs], dvn_c, _TN)
            dvn_l[c] = dvn_c
            dkdec_l[c] = _bdot(v_new[bs], dsn, _NT)
            dgl_l[c] = d["egl"][bs] * jnp.sum(_rsum(s_prev[bs] * dsn), axis=1, keepdims=True)
        dvn = jnp.concatenate(dvn_l, axis=0)
        dkdec = jnp.concatenate(dkdec_l, axis=0)
        dglast = jnp.concatenate(dgl_l, axis=0)

        dqg = _bdot(do, s_prev, _NT)
        dattn = _bdot(do, v_new, _NT)
        dw = -_bdot(dvn, s_prev, _NT)
        drhs = _bdot(t, jnp.concatenate([dvn, dw], axis=2), _TN)
        dvb, dkg = drhs[:, :, :128], drhs[:, :, 128:]
        da = jnp.where(strict, -(_bdot(dvb, u, _NT) + _bdot(dkg, w, _NT)), 0.0)
        dp = da * d["decay"]
        dq_m = dattn * d["decay"]
        m = da * d["a"] + dattn * d["attn"]
        upper_tri = jnp.broadcast_to((_iota2((CH, CH), 1) >= _iota2((CH, CH), 0)).astype(BF16), (n, CH, CH))
        dg_in = _rsum(jnp.where(strict, _bdot(upper_tri, m, _NN), 0.0))
        dkb = _bdot(dp, d["kh"], _NN) + dkg * d["eg"]
        kdk_row = _rsum(dkdec * d["kdec"])
        dk = _bdot(dp, d["kb"], _TN) + _bdot(dq_m, d["qh"], _TN) + dkdec * d["kd_scale"] + dkb * d["bh"]
        dq = _bdot(dq_m, d["kh"], _NN) + dqg * d["eg"]
        dglast = dglast + jnp.sum(kdk_row, axis=1, keepdims=True)
        dgcol = (_rsum(dqg * d["qg"]) + _rsum(dkg * d["kg"]) - kdk_row + jnp.where(row_id == CH - 1, dglast, 0.0))
        dbeta = _rsum(dkb * d["kh"]) + _rsum(dvb * d["vh"])
        dn = dq * (GDN_D ** -0.5)
        dact_q = d["rq"] * (dn - d["qn"] * _rsum(dn * d["qn"]))
        dact_k = d["rk"] * (dk - d["kh"] * _rsum(dk * d["kh"]))
        dact_v = dvb * d["bh"]

        def lanes(v, lane0):
            return jnp.concatenate(
                [sum(_put_lane(v[c * GDN_H + h], lane0 + h) for h in range(GDN_H)) for c in range(ncb)], axis=0)

        def tokens(v):
            return jnp.concatenate(
                [jnp.concatenate([v[c * GDN_H + h] for h in range(GDN_H)], axis=1) for c in range(ncb)], axis=0)

        dbeta_all = lanes(dbeta, 0)
        dg = _dot01l(_chunk_tri(tb, upper=True), lanes(dgcol, 4)) + lanes(dg_in, 4)
        neg_ea = -jnp.exp(prm[0:1, :])
        da_raw = dg * neg_ea * _sigmoid(sp_in)
        db_raw = dbeta_all * beta * (1.0 - beta)
        dsm_ref[...] = (da_raw + db_raw).astype(dsm_ref.dtype)
        lane8 = _iota2((8, 128), 1)
        sub8 = _iota2((8, 128), 0)
        dalog = jnp.sum(dg * g, axis=0, keepdims=True)
        ddtb = jnp.sum(da_raw, axis=0, keepdims=True)
        dprm_ref[...] += jnp.where(sub8 == 0, dalog, 0.0) + jnp.where(sub8 == 1, ddtb, 0.0)
        dnw_ref[...] += jnp.where(sub8 == 0, dnw_acc, 0.0)

        dact = jnp.concatenate([tokens(dact_q), tokens(dact_k), tokens(dact_v)], axis=1)
        dpre = dact * _dsilu(pre)
        back = _conv_back(dpre, nxt_scr[...], tb)
        nxt_scr[...] = dpre[0:8, :]
        draw = back[0] * cw[3:4, :] + back[1] * cw[2:3, :] + back[2] * cw[1:2, :] + back[3] * cw[0:1, :]
        dpg_ref[:, 0:1536] = draw.astype(dpg_ref.dtype)
        dpg_ref[:, 1536:2048] = tokens(dz).astype(dpg_ref.dtype)
        sub_c = _iota2((8, 1536), 0)
        dcw_new = jnp.zeros((8, 1536), F32)
        for s_ in range(CONV_W):
            dcw_new = dcw_new + jnp.where(sub_c == 3 - s_, jnp.sum(back[s_] * raw, axis=0, keepdims=True), 0.0)
        dcw_ref[...] += dcw_new

    def call(pg, pre, sm, cw, prm, nw, st, ti, uw, doa, comm=None, comm_args=()):
        rev = lambda i: (nb - 1 - i, 0)
        const = lambda i: (0, 0)
        cx = _exchange_specs(comm)
        return pl.pallas_call(
            _with_exchange(body, comm, 10, 5, nb),
            grid=(nb,),
            in_specs=[
                pl.BlockSpec((tb, 2048), rev),
                pl.BlockSpec((tb, 1536), rev),
                pl.BlockSpec((tb, 128), rev),
                pl.BlockSpec((8, 1536), const),
                pl.BlockSpec((8, 128), const),
                pl.BlockSpec((8, 128), const),
                pl.BlockSpec((ncb, GDN_H, 128, 128), lambda i: (nb - 1 - i, 0, 0, 0)),
                pl.BlockSpec((ncb, GDN_H, CH, CH), lambda i: (nb - 1 - i, 0, 0, 0)),
                pl.BlockSpec((ncb, GDN_H, CH, 256), lambda i: (nb - 1 - i, 0, 0, 0)),
                pl.BlockSpec((tb, 512), rev),
            ] + cx["specs"],
            out_specs=[
                pl.BlockSpec((tb, 2048), rev),
                pl.BlockSpec((tb, 128), rev),
                pl.BlockSpec((8, 1536), const),
                pl.BlockSpec((8, 128), const),
                pl.BlockSpec((8, 128), const),
            ] + cx["specs"],
            out_shape=[
                jax.ShapeDtypeStruct((seq, 2048), BF16),
                jax.ShapeDtypeStruct((seq, 128), BF16),
                jax.ShapeDtypeStruct((8, 1536), F32),
                jax.ShapeDtypeStruct((8, 128), F32),
                jax.ShapeDtypeStruct((8, 128), F32),
            ] + cx["out_shape"],
            scratch_shapes=[pltpu.VMEM((GDN_H, 128, 128), F32), pltpu.VMEM((8, 1536), F32)] + cx["scratch"],
            compiler_params=pltpu.CompilerParams(dimension_semantics=("arbitrary",), vmem_limit_bytes=VMEM_LIMIT,
                                                 has_side_effects=comm is not None),
            name="gdn_bwd" + cx["tag"],
        )(pg, pre, sm, cw, prm, nw, st, ti, uw, doa, *comm_args)

    return call


def _expand_mat():
    r = _iota2((128, SSD_W), 0)
    c = _iota2((128, SSD_W), 1)
    return (jnp.right_shift(c, 6) == r).astype(F32)


def _reduce_heads(v, e):
    vh, vl = _split(v)
    eb = e.astype(BF16)
    nt = (((1,), (1,)), ((), ()))
    return (lax.dot_general(vh, eb, nt, preferred_element_type=F32)
            + lax.dot_general(vl, eb, nt, preferred_element_type=F32))


def _reduce_heads1(v, e):
    nt = (((1,), (1,)), ((), ()))
    return lax.dot_general(v.astype(BF16), e.astype(BF16), nt, preferred_element_type=F32)


def _row8(v):
    return jnp.broadcast_to(v, (8, v.shape[1]))


def _ssd_common(ps_ref, halo8, ss, cw, cb, prm, tb, pre=None):
    raw = ps_ref[:, 0:1536]
    taps = None
    if pre is None:
        taps = _conv_taps(raw, halo8, tb)
        pre = taps[0] * cw[3:4, :] + taps[1] * cw[2:3, :] + taps[2] * cw[1:2, :] + taps[3] * cw[0:1, :] + cb[0:1, :]
    act = _silu(pre)
    dt_in = ss + prm[1:2, :]
    dt = _softplus(dt_in)
    a = dt * (-jnp.exp(prm[0:1, :]))
    acum = _dot01l(_chunk_tri(tb), a)
    e = _expand_mat()
    dt_e = _dot01r(dt, e)
    xdt = act[:, 0:SSD_W] * dt_e
    ea_e = _dot01r(jnp.exp(acum), e)
    d_e = _dot01r(_row8(prm[2:3, :]), e)[0:1, :]
    return raw, taps, pre, act, dt_in, dt, a, acum, e, dt_e, xdt, ea_e, d_e


def _ssd_chunk(act, acum, act_t, e, c):
    r0 = c * CH
    rows = slice(r0, r0 + CH)
    alast = acum[r0 + CH - 1:r0 + CH, :]
    wdec = jnp.exp(alast - acum[rows, :])
    wd_e = _dot01r(wdec, e)
    eal_e = _dot01r(_row8(jnp.exp(alast)), e)[0:1, :]
    return rows, wd_e, eal_e


def _ssd_lmat(acum, act_t, c, h, causal):
    r0 = c * CH
    acol = acum[r0:r0 + CH, h:h + 1]
    arow = act_t[h:h + 1, r0:r0 + CH]
    return jnp.exp(jnp.where(causal, acol - arow, NEG))


def _make_ssd_fwd(seq, tb):
    ncb = tb // CH
    nb = seq // tb
    hg = SSD_H // SSD_G
    gw = SSD_W // SSD_G

    def body(ps_ref, ss_ref, cw_ref, cb_ref, prm_ref, nw_ref, ob_ref, st_ref, pre_ref, y_ref, hs_scr, halo_scr):
        @pl.when(pl.program_id(0) == 0)
        def _():
            hs_scr[...] = jnp.zeros_like(hs_scr)
            halo_scr[...] = jnp.zeros_like(halo_scr)

        causal, _, _ = _masks()
        (raw, _, pre, act, _, _, _, acum, e, _, xdt, ea_e, d_e) = _ssd_common(
            ps_ref, halo_scr[...], ss_ref[...], cw_ref[...], cb_ref[...], prm_ref[...], tb)
        halo_scr[...] = raw[tb - 8:tb, :]
        pre_ref[...] = pre
        act_t = acum.T
        nw = nw_ref[0:1, :]
        for c in range(ncb):
            rows, wd_e, eal_e = _ssd_chunk(act, acum, act_t, e, c)
            st_ref[c] = hs_scr[...]
            ys = []
            for g in range(SSD_G):
                gc_ = slice(g * gw, (g + 1) * gw)
                bg = act[rows, SSD_W + g * 128:SSD_W + (g + 1) * 128]
                cg = act[rows, SSD_W + 256 + g * 128:SSD_W + 256 + (g + 1) * 128]
                cbm = _dot_nt(cg, bg)
                hs = hs_scr[:, gc_]
                yin = _dot(cg, hs)
                yh = []
                for hh in range(hg):
                    h = g * hg + hh
                    lm = _ssd_lmat(acum, act_t, c, h, causal)
                    yh.append(_dot(cbm * lm, xdt[rows, h * SSD_P:(h + 1) * SSD_P]))
                ys.append(jnp.concatenate(yh, axis=1) + yin * ea_e[rows, gc_])
                hs_scr[:, gc_] = hs * eal_e[:, gc_] + _dot_tn(bg, xdt[rows, gc_] * wd_e[:, gc_])
            y = jnp.concatenate(ys, axis=1) + act[rows, 0:SSD_W] * d_e
            y_ref[rows, :] = y
            yz = y * _silu(ps_ref[rows, 1536:2560])
            outs = [_rms_fwd(yz[:, g * gw:(g + 1) * gw], nw[:, g * gw:(g + 1) * gw], gw)[2] for g in range(SSD_G)]
            ob_ref[rows, :] = jnp.concatenate(outs, axis=1).astype(ob_ref.dtype)

    def call(ps, ss, cw, cb, prm, nw):
        const = lambda i: (0, 0)
        return pl.pallas_call(
            body,
            grid=(nb,),
            in_specs=[
                pl.BlockSpec((tb, 2560), lambda i: (i, 0)),
                pl.BlockSpec((tb, 128), lambda i: (i, 0)),
                pl.BlockSpec((8, 1536), const),
                pl.BlockSpec((8, 1536), const),
                pl.BlockSpec((8, 128), const),
                pl.BlockSpec((8, SSD_W), const),
            ],
            out_specs=[
                pl.BlockSpec((tb, SSD_W), lambda i: (i, 0)),
                pl.BlockSpec((ncb, SSD_N, SSD_W), lambda i: (i, 0, 0)),
                pl.BlockSpec((tb, 1536), lambda i: (i, 0)),
                pl.BlockSpec((tb, SSD_W), lambda i: (i, 0)),
            ],
            out_shape=[
                jax.ShapeDtypeStruct((seq, SSD_W), BF16),
                jax.ShapeDtypeStruct((seq // CH, SSD_N, SSD_W), F32),
                jax.ShapeDtypeStruct((seq, 1536), F32),
                jax.ShapeDtypeStruct((seq, SSD_W), F32),
            ],
            scratch_shapes=[pltpu.VMEM((SSD_N, SSD_W), F32), pltpu.VMEM((8, 1536), F32)],
            compiler_params=pltpu.CompilerParams(dimension_semantics=("arbitrary",), vmem_limit_bytes=VMEM_LIMIT),
            name="ssd_fwd",
        )(ps, ss, cw, cb, prm, nw)

    return call


def _make_ssd_bwd(seq, tb):
    ncb = tb // CH
    nb = seq // tb
    hb = tb // 8
    hg = SSD_H // SSD_G
    gw = SSD_W // SSD_G

    def body(ps_ref, pre_ref, y_ref, ss_ref, cw_ref, cb_ref, prm_ref, nw_ref, st_ref, dob_ref,
             dps_ref, dss_ref, dcw_ref, dcb_ref, dprm_ref, dnw_ref, dhs_scr, nxt_scr):
        i = pl.program_id(0)

        @pl.when(i == 0)
        def _():
            dhs_scr[...] = jnp.zeros_like(dhs_scr)
            nxt_scr[...] = jnp.zeros_like(nxt_scr)
            dcw_ref[...] = jnp.zeros_like(dcw_ref)
            dcb_ref[...] = jnp.zeros_like(dcb_ref)
            dprm_ref[...] = jnp.zeros_like(dprm_ref)
            dnw_ref[...] = jnp.zeros_like(dnw_ref)

        causal, _, _ = _masks()
        cw = cw_ref[...]
        prm = prm_ref[...]
        (raw, _, pre, act, dt_in, dt, a, acum, e, dt_e, xdt, ea_e, d_e) = _ssd_common(
            ps_ref, None, ss_ref[...], cw, cb_ref[...], prm, tb, pre=pre_ref[...])
        act_t = acum.T
        nw = nw_ref[0:1, :]

        dx_l, db_l, dc_l, dz_l, ddt_l, da_l = ([None] * ncb for _ in range(6))
        upper_tri = (_iota2((CH, CH), 1) >= _iota2((CH, CH), 0)).astype(F32)
        tri_pair = jnp.concatenate([upper_tri, (_iota2((CH, CH), 1) < _iota2((CH, CH), 0)).astype(F32)], axis=1)
        below = jnp.bitwise_and(_iota2((CH, gw), 1), CH - 1) < _iota2((CH, gw), 0)
        dnw_acc = jnp.zeros((1, SSD_W), F32)
        dd_acc = jnp.zeros((1, SSD_W), F32)

        for c in reversed(range(ncb)):
            rows, wd_e, eal_e = _ssd_chunk(act, acum, act_t, e, c)
            xc = act[rows, 0:SSD_W]
            z = ps_ref[rows, 1536:2560]
            dob = dob_ref[rows, :]
            sz = _silu(z)
            dy_g, dz_g, dxdt_g, db_g, dc_g, da_g = [], [], [], [], [], []
            for g in range(SSD_G):
                gc_ = slice(g * gw, (g + 1) * gw)
                bg = act[rows, SSD_W + g * 128:SSD_W + (g + 1) * 128]
                cg = act[rows, SSD_W + 256 + g * 128:SSD_W + 256 + (g + 1) * 128]
                cbm = _dot_nt(cg, bg)
                hs = st_ref[c, :, gc_]
                yin = _dot(cg, hs)
                lmats = [_ssd_lmat(acum, act_t, c, g * hg + hh, causal) for hh in range(hg)]
                ea_g = ea_e[rows, gc_]
                y = y_ref[rows, gc_]
                yz = y * sz[:, gc_]
                on, r, _ = _rms_fwd(yz, nw[:, gc_], gw)
                dyz, dnw_rows = _rms_bwd(dob[:, gc_], on, r, nw[:, gc_], gw)
                dnw_acc = dnw_acc + _put_cols(jnp.sum(dnw_rows, axis=0, keepdims=True), g, gw)
                dy = dyz * sz[:, gc_]
                dz_g.append(dyz * y * _dsilu(z[:, gc_]))
                dd_acc = dd_acc + _put_cols(jnp.sum(dy * xc[:, gc_], axis=0, keepdims=True), g, gw)
                dhs_n = dhs_scr[:, gc_]
                dyin = dy * ea_g
                dcg = _dot_nt(dyin, hs)
                xw = xdt[rows, gc_] * wd_e[:, gc_]
                dbg = _dot_nt(xw, dhs_n)
                dxw = _dot(bg, dhs_n)
                dhs_scr[:, gc_] = dhs_n * eal_e[:, gc_] + _dot_tn(cg, dyin)
                dxi, ms, dcbm = [], [], jnp.zeros((CH, CH), F32)
                for hh in range(hg):
                    h = g * hg + hh
                    hc = slice(hh * SSD_P, (hh + 1) * SSD_P)
                    dyh = dy[:, hc]
                    lm = cbm * lmats[hh]
                    dxi.append(_dot_tn(lm, dyh))
                    dlm = _dot_nt(dyh, xdt[rows, h * SSD_P:(h + 1) * SSD_P])
                    ms.append(dlm * lm)
                    dcbm = dcbm + dlm * lmats[hh]
                dx_intra = jnp.concatenate(dxi, axis=1)
                ncat = _dot(upper_tri, jnp.concatenate(ms, axis=1))
                cum = _dot(tri_pair, jnp.concatenate([dy * yin * ea_g, dxw * xw], axis=0))
                da_g.append(jnp.where(below, ncat, 0.0) + cum
                            + jnp.sum(hs * dhs_n, axis=0, keepdims=True) * eal_e[:, gc_])
                dxdt_g.append(dx_intra + dxw * wd_e[:, gc_])
                dy_g.append(dy)
                db_g.append(dbg + _dot_tn(dcbm, cg))
                dc_g.append(dcg + _dot(dcbm, bg))
            dy = jnp.concatenate(dy_g, axis=1)
            dxdt = jnp.concatenate(dxdt_g, axis=1)
            dx_l[c] = dxdt * dt_e[rows, :] + dy * d_e
            db_l[c] = jnp.concatenate(db_g, axis=1)
            dc_l[c] = jnp.concatenate(dc_g, axis=1)
            dz_l[c] = jnp.concatenate(dz_g, axis=1)
            ddt_l[c] = _reduce_heads1(dxdt * xc, e)
            da_l[c] = _reduce_heads1(jnp.concatenate(da_g, axis=1), e)

        da = jnp.concatenate(da_l, axis=0)
        neg_ea = -jnp.exp(prm[0:1, :])
        ddt = jnp.concatenate(ddt_l, axis=0) + da * neg_ea
        ddt_in = ddt * _sigmoid(dt_in)
        dss_ref[...] = ddt_in.astype(dss_ref.dtype)
        sub8 = _iota2((8, 128), 0)
        dalog = jnp.sum(da * a, axis=0, keepdims=True)
        ddtb = jnp.sum(ddt_in, axis=0, keepdims=True)
        dd = _reduce_heads(_row8(dd_acc), e)[0:1, :]
        dprm_ref[...] += (jnp.where(sub8 == 0, dalog, 0.0) + jnp.where(sub8 == 1, ddtb, 0.0)
                          + jnp.where(sub8 == 2, dd, 0.0))
        dnw_ref[...] += jnp.where(_iota2((8, SSD_W), 0) == 0, dnw_acc, 0.0)

        dact = jnp.concatenate([jnp.concatenate(dx_l, axis=0), jnp.concatenate(db_l, axis=0),
                                jnp.concatenate(dc_l, axis=0)], axis=1)
        dpre = dact * _dsilu(pre)
        back = _conv_back(dpre, nxt_scr[...], tb)
        nxt_scr[...] = dpre[0:8, :]
        draw = back[0] * cw[3:4, :] + back[1] * cw[2:3, :] + back[2] * cw[1:2, :] + back[3] * cw[0:1, :]
        dps_ref[:, 0:1536] = draw.astype(dps_ref.dtype)
        dps_ref[:, 1536:2560] = jnp.concatenate(dz_l, axis=0).astype(dps_ref.dtype)
        sub_c = _iota2((8, 1536), 0)
        dcw_new = jnp.zeros((8, 1536), F32)
        for s_ in range(CONV_W):
            dcw_new = dcw_new + jnp.where(sub_c == 3 - s_, jnp.sum(back[s_] * raw, axis=0, keepdims=True), 0.0)
        dcw_ref[...] += dcw_new
        dcb_ref[...] += jnp.where(sub_c == 0, jnp.sum(dpre, axis=0, keepdims=True), 0.0)

    def call(ps, pre, y, ss, cw, cb, prm, nw, st, dob, comm=None, comm_args=()):
        rev = lambda i: (nb - 1 - i, 0)
        const = lambda i: (0, 0)
        cx = _exchange_specs(comm)
        return pl.pallas_call(
            _with_exchange(body, comm, 10, 6, nb),
            grid=(nb,),
            in_specs=[
                pl.BlockSpec((tb, 2560), rev),
                pl.BlockSpec((tb, 1536), rev),
                pl.BlockSpec((tb, SSD_W), rev),
                pl.BlockSpec((tb, 128), rev),
                pl.BlockSpec((8, 1536), const),
                pl.BlockSpec((8, 1536), const),
                pl.BlockSpec((8, 128), const),
                pl.BlockSpec((8, SSD_W), const),
                pl.BlockSpec((ncb, SSD_N, SSD_W), lambda i: (nb - 1 - i, 0, 0)),
                pl.BlockSpec((tb, SSD_W), rev),
            ] + cx["specs"],
            out_specs=[
                pl.BlockSpec((tb, 2560), rev),
                pl.BlockSpec((tb, 128), rev),
                pl.BlockSpec((8, 1536), const),
                pl.BlockSpec((8, 1536), const),
                pl.BlockSpec((8, 128), const),
                pl.BlockSpec((8, SSD_W), const),
            ] + cx["specs"],
            out_shape=[
                jax.ShapeDtypeStruct((seq, 2560), BF16),
                jax.ShapeDtypeStruct((seq, 128), BF16),
                jax.ShapeDtypeStruct((8, 1536), F32),
                jax.ShapeDtypeStruct((8, 1536), F32),
                jax.ShapeDtypeStruct((8, 128), F32),
                jax.ShapeDtypeStruct((8, SSD_W), F32),
            ] + cx["out_shape"],
            scratch_shapes=[pltpu.VMEM((SSD_N, SSD_W), F32), pltpu.VMEM((8, 1536), F32)] + cx["scratch"],
            compiler_params=pltpu.CompilerParams(dimension_semantics=("arbitrary",), vmem_limit_bytes=VMEM_LIMIT,
                                                 has_side_effects=comm is not None),
            name="ssd_bwd" + cx["tag"],
        )(ps, pre, y, ss, cw, cb, prm, nw, st, dob, *comm_args)

    return call


def _ret_consts(h):
    lg = math.log(1.0 - 2.0 ** (-5.0 - h))
    r = _iota2((CH, CH), 0)
    c = _iota2((CH, CH), 1)
    rel = (r - c).astype(F32)
    dmat = jnp.where(r >= c, jnp.exp(jnp.maximum(rel, 0.0) * lg), 0.0)
    idx = _iota2((CH, 1), 0).astype(F32)
    qdec = jnp.exp((idx + 1.0) * lg)
    kdec = jnp.exp((CH - 1.0 - idx) * lg)
    cdec = math.exp(CH * lg)
    return dmat, qdec, kdec, cdec


def _ret_batch(pr_ref, cc_ref, ss_ref, ncb):
    pairs = [(c, h) for c in range(ncb) for h in range(RET_H)]

    def st(off):
        return jnp.stack([pr_ref[c * CH:(c + 1) * CH, off + h * 128:off + (h + 1) * 128] for c, h in pairs])

    cc = jnp.stack([cc_ref[c * CH:(c + 1) * CH, :] for c, _ in pairs])
    ss = jnp.stack([ss_ref[c * CH:(c + 1) * CH, :] for c, _ in pairs])
    consts = [_ret_consts(h) for h in range(RET_H)]
    dmat = jnp.stack([consts[h][0] for _, h in pairs])
    qdec = jnp.stack([consts[h][1] for _, h in pairs])
    kdec = jnp.stack([consts[h][2] for _, h in pairs])
    cdec = jnp.stack([jnp.full((1, 1), consts[h][3], F32) for h in range(RET_H)])
    q = _rot(st(0), cc, ss)
    k = _rot(st(512), cc, ss) * (RET_D ** -0.5)
    return dict(q=q, k=k, v=st(1024), z=st(1536), cc=cc, ss=ss, dmat=dmat, qdec=qdec, kdec=kdec, cdec=cdec,
                s=_bdot(q, k, _NT) * dmat)


def _rot(t, cc, ss):
    return t * cc + pltpu.roll(t, 64, axis=t.ndim - 1) * ss


def _rot_bwd(d, cc, ss):
    return d * cc + pltpu.roll(d * ss, 64, axis=d.ndim - 1)


def _make_ret_fwd(seq, tb):
    ncb = tb // CH
    nb = seq // tb

    def body(pr_ref, cc_ref, ss_ref, nw_ref, oc_ref, st_ref, r_scr):
        @pl.when(pl.program_id(0) == 0)
        def _():
            r_scr[...] = jnp.zeros_like(r_scr)

        d = _ret_batch(pr_ref, cc_ref, ss_ref, ncb)
        kd = d["k"] * d["kdec"]
        for c in range(ncb):
            bs = slice(c * RET_H, (c + 1) * RET_H)
            rs = r_scr[...]
            st_ref[c] = rs
            r_scr[...] = rs * d["cdec"] + _bdot(kd[bs], d["v"][bs], _TN)
        r_prev = st_ref[...].reshape(ncb * RET_H, 128, 128)
        o = _bdot(d["s"], d["v"], _NN) + _bdot(d["q"], r_prev, _NN) * d["qdec"]
        _, _, y = _rms_fwd(o, nw_ref[0:1, :], RET_D)
        out = y * _silu(d["z"])
        for c in range(ncb):
            for h in range(RET_H):
                oc_ref[c * CH:(c + 1) * CH, h * 128:(h + 1) * 128] = out[c * RET_H + h].astype(oc_ref.dtype)

    def call(pr, cc, ss, nw):
        return pl.pallas_call(
            body,
            grid=(nb,),
            in_specs=[
                pl.BlockSpec((tb, 2048), lambda i: (i, 0)),
                pl.BlockSpec((tb, 128), lambda i: (i, 0)),
                pl.BlockSpec((tb, 128), lambda i: (i, 0)),
                pl.BlockSpec((8, 128), lambda i: (0, 0)),
            ],
            out_specs=[
                pl.BlockSpec((tb, 512), lambda i: (i, 0)),
                pl.BlockSpec((ncb, RET_H, 128, 128), lambda i: (i, 0, 0, 0)),
            ],
            out_shape=[
                jax.ShapeDtypeStruct((seq, 512), BF16),
                jax.ShapeDtypeStruct((seq // CH, RET_H, 128, 128), F32),
            ],
            scratch_shapes=[pltpu.VMEM((RET_H, 128, 128), F32)],
            compiler_params=pltpu.CompilerParams(dimension_semantics=("arbitrary",), vmem_limit_bytes=VMEM_LIMIT),
            name="ret_fwd",
        )(pr, cc, ss, nw)

    return call


def _make_ret_bwd(seq, tb):
    ncb = tb // CH
    nb = seq // tb

    def body(pr_ref, cc_ref, ss_ref, nw_ref, st_ref, doc_ref, dpr_ref, dnw_ref, dr_scr):
        @pl.when(pl.program_id(0) == 0)
        def _():
            dr_scr[...] = jnp.zeros_like(dr_scr)
            dnw_ref[...] = jnp.zeros_like(dnw_ref)

        nw = nw_ref[0:1, :]
        scale = RET_D ** -0.5
        n = ncb * RET_H
        d = _ret_batch(pr_ref, cc_ref, ss_ref, ncb)
        q, k, v, z, s = d["q"], d["k"], d["v"], d["z"], d["s"]
        r_prev = st_ref[...].reshape(n, 128, 128)
        o = _bdot(s, v, _NN) + _bdot(q, r_prev, _NN) * d["qdec"]
        doc = jnp.stack([doc_ref[c * CH:(c + 1) * CH, h * 128:(h + 1) * 128]
                         for c in range(ncb) for h in range(RET_H)])
        on, r, y = _rms_fwd(o, nw, RET_D)
        dz = doc * y * _dsilu(z)
        do, dnw_rows = _rms_bwd(doc * _silu(z), on, r, nw, RET_D)
        dnw_acc = jnp.sum(jnp.sum(dnw_rows, axis=0), axis=0, keepdims=True)
        dqd = do * d["qdec"]
        qtd = _bdot(q, dqd, _TN)
        drn_l = [None] * ncb
        for c in reversed(range(ncb)):
            drn_l[c] = dr_scr[...]
            dr_scr[...] = qtd[c * RET_H:(c + 1) * RET_H] + d["cdec"] * drn_l[c]
        drn = jnp.concatenate(drn_l, axis=0)
        ds = _bdot(do, v, _NT) * d["dmat"]
        dq = _rot_bwd(_bdot(ds, k, _NN) + _bdot(dqd, r_prev, _NT), d["cc"], d["ss"])
        dk = _rot_bwd((_bdot(ds, q, _TN) + _bdot(v, drn, _NT) * d["kdec"]) * scale, d["cc"], d["ss"])
        dv = _bdot(s, do, _TN) + _bdot(k * d["kdec"], drn, _NN)
        for c in range(ncb):
            rows = slice(c * CH, (c + 1) * CH)
            for h in range(RET_H):
                b = c * RET_H + h
                for j, val in enumerate((dq, dk, dv, dz)):
                    dpr_ref[rows, j * 512 + h * 128:j * 512 + (h + 1) * 128] = val[b].astype(dpr_ref.dtype)
        dnw_ref[...] += jnp.where(_iota2((8, 128), 0) == 0, dnw_acc, 0.0)

    def call(pr, cc, ss, nw, st, doc):
        rev = lambda i: (nb - 1 - i, 0)
        return pl.pallas_call(
            body,
            grid=(nb,),
            in_specs=[
                pl.BlockSpec((tb, 2048), rev),
                pl.BlockSpec((tb, 128), rev),
                pl.BlockSpec((tb, 128), rev),
                pl.BlockSpec((8, 128), lambda i: (0, 0)),
                pl.BlockSpec((ncb, RET_H, 128, 128), lambda i: (nb - 1 - i, 0, 0, 0)),
                pl.BlockSpec((tb, 512), rev),
            ],
            out_specs=[
                pl.BlockSpec((tb, 2048), rev),
                pl.BlockSpec((8, 128), lambda i: (0, 0)),
            ],
            out_shape=[
                jax.ShapeDtypeStruct((seq, 2048), BF16),
                jax.ShapeDtypeStruct((8, 128), F32),
            ],
            scratch_shapes=[pltpu.VMEM((RET_H, 128, 128), F32)],
            compiler_params=pltpu.CompilerParams(dimension_semantics=("arbitrary",), vmem_limit_bytes=VMEM_LIMIT),
            name="ret_bwd",
        )(pr, cc, ss, nw, st, doc)

    return call


def _rope_tables(seq):
    half = RET_D // 2
    inv = ROPE_BASE ** (-jnp.arange(half, dtype=F32) / half)
    hi = (CH * jnp.arange(seq // CH, dtype=jnp.int32)).astype(F32)[:, None] * inv[None, :]
    lo = jnp.arange(CH, dtype=jnp.int32).astype(F32)[:, None] * inv[None, :]
    ch, sh, cl, sl = jnp.cos(hi)[:, None, :], jnp.sin(hi)[:, None, :], jnp.cos(lo)[None], jnp.sin(lo)[None]
    cos = (ch * cl - sh * sl).reshape(seq, half)
    sin = (sh * cl + ch * sl).reshape(seq, half)
    return jnp.concatenate([cos, cos], axis=1), jnp.concatenate([-sin, sin], axis=1)


SEG_G, SEG_S, SEG_R, SEG_GS, SEG_SS = (0, 2048), (2048, 4608), (4608, 6656), (6656, 6784), (6784, 6912)
NP = 6912
SEGS = (SEG_G, SEG_S, SEG_R, SEG_GS, SEG_SS)


def _resident(shape):
    return pl.BlockSpec(shape, lambda i: (0,) * len(shape), pipeline_mode=pl.Buffered(1))


def _make_inproj(seq, tl):
    def body(x_ref, pn_ref, w_ref, pg_ref, ps_ref, pr_ref, gs_ref, ss_ref, ht_ref):
        x = x_ref[...]
        _, _, hn = _rms_fwd(x, pn_ref[0:1, :], D_MODEL)
        h = hn.astype(BF16)
        ht_ref[...] = hn.T.astype(BF16)
        for (a, b), o_ref in zip(SEGS, (pg_ref, ps_ref, pr_ref, gs_ref, ss_ref)):
            o_ref[...] = jnp.dot(h, w_ref[:, a:b], preferred_element_type=F32)

    def call(x, pn, w, comm=None, comm_args=()):
        row = lambda i: (i, 0)
        cx = _exchange_specs(comm)
        return pl.pallas_call(
            _with_exchange(body, comm, 3, 6, seq // tl),
            grid=(seq // tl,),
            in_specs=[pl.BlockSpec((tl, D_MODEL), row), _resident((8, D_MODEL)), _resident((D_MODEL, NP))]
            + cx["specs"],
            out_specs=[pl.BlockSpec((tl, b - a), row) for a, b in SEGS]
            + [pl.BlockSpec((D_MODEL, tl), lambda i: (0, i))] + cx["specs"],
            out_shape=[jax.ShapeDtypeStruct((seq, b - a), F32) for a, b in SEGS]
            + [jax.ShapeDtypeStruct((D_MODEL, seq), BF16)] + cx["out_shape"],
            scratch_shapes=cx["scratch"],
            compiler_params=pltpu.CompilerParams(dimension_semantics=("arbitrary",), vmem_limit_bytes=VMEM_LIMIT,
                                                 has_side_effects=comm is not None),
            name="inproj" + cx["tag"],
        )(x, pn, w, *comm_args)

    return call


def _make_outproj(seq, tl):
    def body(oa_ref, ob_ref, oc_ref, w_ref, x_ref, qn_ref, out_ref, xn_ref):
        out = (jnp.dot(oa_ref[...], w_ref[0:512, :], preferred_element_type=F32)
               + jnp.dot(ob_ref[...], w_ref[512:1536, :], preferred_element_type=F32)
               + jnp.dot(oc_ref[...], w_ref[1536:2048, :], preferred_element_type=F32))
        out_ref[...] = out
        _, _, y = _rms_fwd(out, qn_ref[0:1, :], D_MODEL)
        xn_ref[...] = x_ref[...] + y

    def call(oa, ob, oc, w, x, qn):
        row = lambda i: (i, 0)
        return pl.pallas_call(
            body,
            grid=(seq // tl,),
            in_specs=[pl.BlockSpec((tl, 512), row), pl.BlockSpec((tl, 1024), row), pl.BlockSpec((tl, 512), row),
                      _resident((2048, D_MODEL)), pl.BlockSpec((tl, D_MODEL), row), _resident((8, D_MODEL))],
            out_specs=[pl.BlockSpec((tl, D_MODEL), row), pl.BlockSpec((tl, D_MODEL), row)],
            out_shape=[jax.ShapeDtypeStruct((seq, D_MODEL), F32), jax.ShapeDtypeStruct((seq, D_MODEL), F32)],
            compiler_params=pltpu.CompilerParams(dimension_semantics=("arbitrary",), vmem_limit_bytes=VMEM_LIMIT),
            name="outproj",
        )(oa, ob, oc, w, x, qn)

    return call


def _make_outproj_loss(seq, tl):
    def body(oa_ref, ob_ref, oc_ref, w_ref, x_ref, qn_ref, t_ref, out_ref, dy_ref, loss_ref):
        @pl.when(pl.program_id(0) == 0)
        def _():
            loss_ref[...] = jnp.zeros_like(loss_ref)

        out = (jnp.dot(oa_ref[...], w_ref[0:512, :], preferred_element_type=F32)
               + jnp.dot(ob_ref[...], w_ref[512:1536, :], preferred_element_type=F32)
               + jnp.dot(oc_ref[...], w_ref[1536:2048, :], preferred_element_type=F32))
        out_ref[...] = out
        _, _, y = _rms_fwd(out, qn_ref[0:1, :], D_MODEL)
        err = (x_ref[...] + y) - t_ref[...]
        dy_ref[...] = err * (1.0 / D_MODEL)
        part = jnp.sum(jnp.sum(err * err, axis=1, keepdims=True), axis=0, keepdims=True) * (0.5 / D_MODEL)
        loss_ref[...] += jnp.where((_iota2((8, 128), 0) == 0) & (_iota2((8, 128), 1) == 0), part, 0.0)

    def call(oa, ob, oc, w, x, qn, t):
        row = lambda i: (i, 0)
        return pl.pallas_call(
            body,
            grid=(seq // tl,),
            in_specs=[pl.BlockSpec((tl, 512), row), pl.BlockSpec((tl, 1024), row), pl.BlockSpec((tl, 512), row),
                      _resident((2048, D_MODEL)), pl.BlockSpec((tl, D_MODEL), row), _resident((8, D_MODEL)),
                      pl.BlockSpec((tl, D_MODEL), row)],
            out_specs=[pl.BlockSpec((tl, D_MODEL), row), pl.BlockSpec((tl, D_MODEL), row),
                       pl.BlockSpec((8, 128), lambda i: (0, 0))],
            out_shape=[jax.ShapeDtypeStruct((seq, D_MODEL), F32), jax.ShapeDtypeStruct((seq, D_MODEL), F32),
                       jax.ShapeDtypeStruct((8, 128), F32)],
            compiler_params=pltpu.CompilerParams(dimension_semantics=("arbitrary",), vmem_limit_bytes=VMEM_LIMIT),
            name="outproj_loss",
        )(oa, ob, oc, w, x, qn, t)

    return call


def _make_outproj_bwd(seq, tl):
    def body(dxn_ref, out_ref, oa_ref, ob_ref, oc_ref, w_ref, qn_ref,
             doa_ref, dob_ref, doc_ref, dqn_ref, dw_ref, dwb_ref):
        @pl.when(pl.program_id(0) == 0)
        def _():
            dqn_ref[...] = jnp.zeros_like(dqn_ref)
            dw_ref[...] = jnp.zeros_like(dw_ref)

        qn = qn_ref[0:1, :]
        on, r, _ = _rms_fwd(out_ref[...], qn, D_MODEL)
        dout, dqn_rows = _rms_bwd(dxn_ref[...], on, r, qn, D_MODEL)
        dqn_ref[...] += jnp.where(_iota2((8, D_MODEL), 0) == 0, jnp.sum(dqn_rows, axis=0, keepdims=True), 0.0)
        db = dout.astype(BF16)
        nt = (((1,), (1,)), ((), ()))
        tn = (((0,), (0,)), ((), ()))
        doa_ref[...] = lax.dot_general(db, w_ref[0:512, :], nt, preferred_element_type=F32).astype(BF16)
        dob_ref[...] = lax.dot_general(db, w_ref[512:1536, :], nt, preferred_element_type=F32).astype(BF16)
        doc_ref[...] = lax.dot_general(db, w_ref[1536:2048, :], nt, preferred_element_type=F32).astype(BF16)
        dw_ref[0:512, :] += lax.dot_general(oa_ref[...], db, tn, preferred_element_type=F32)
        dw_ref[512:1536, :] += lax.dot_general(ob_ref[...], db, tn, preferred_element_type=F32)
        dw_ref[1536:2048, :] += lax.dot_general(oc_ref[...], db, tn, preferred_element_type=F32)

        @pl.when(pl.program_id(0) == seq // tl - 1)
        def _():
            dwb_ref[...] = dw_ref[...].astype(BF16)

    def call(dxn, out, oa, ob, oc, w, qn):
        row = lambda i: (i, 0)
        const = lambda i: (0, 0)
        return pl.pallas_call(
            body,
            grid=(seq // tl,),
            in_specs=[pl.BlockSpec((tl, D_MODEL), row), pl.BlockSpec((tl, D_MODEL), row),
                      pl.BlockSpec((tl, 512), row), pl.BlockSpec((tl, 1024), row), pl.BlockSpec((tl, 512), row),
                      _resident((2048, D_MODEL)), _resident((8, D_MODEL))],
            out_specs=[pl.BlockSpec((tl, 512), row), pl.BlockSpec((tl, 1024), row), pl.BlockSpec((tl, 512), row),
                       pl.BlockSpec((8, D_MODEL), const), pl.BlockSpec((2048, D_MODEL), const),
                       pl.BlockSpec((2048, D_MODEL), const)],
            out_shape=[jax.ShapeDtypeStruct((seq, 512), BF16), jax.ShapeDtypeStruct((seq, 1024), BF16),
                       jax.ShapeDtypeStruct((seq, 512), BF16), jax.ShapeDtypeStruct((8, D_MODEL), F32),
                       jax.ShapeDtypeStruct((2048, D_MODEL), F32), jax.ShapeDtypeStruct((2048, D_MODEL), BF16)],
            compiler_params=pltpu.CompilerParams(dimension_semantics=("arbitrary",), vmem_limit_bytes=VMEM_LIMIT),
            name="outproj_bwd",
        )(dxn, out, oa, ob, oc, w, qn)

    return call


def _make_inproj_bwd_dx(seq, tl):
    def body(dg_ref, ds_ref, dr_ref, dgs_ref, dss_ref, w_ref, x_ref, pn_ref, dxn_ref, dx_ref, dpn_ref):
        @pl.when(pl.program_id(0) == 0)
        def _():
            dpn_ref[...] = jnp.zeros_like(dpn_ref)

        nt = (((1,), (1,)), ((), ()))
        dh = jnp.zeros((tl, D_MODEL), F32)
        for (a, b), d_ref in zip(SEGS, (dg_ref, ds_ref, dr_ref, dgs_ref, dss_ref)):
            dh = dh + lax.dot_general(d_ref[...], w_ref[:, a:b], nt, preferred_element_type=F32)
        pn = pn_ref[0:1, :]
        on, r, _ = _rms_fwd(x_ref[...], pn, D_MODEL)
        dx, dpn_rows = _rms_bwd(dh, on, r, pn, D_MODEL)
        dx_ref[...] = dx + dxn_ref[...]
        dpn_ref[...] += jnp.where(_iota2((8, D_MODEL), 0) == 0, jnp.sum(dpn_rows, axis=0, keepdims=True), 0.0)

    def call(dg, ds, dr, dgs, dss, w, x, pn, dxn, comm=None, comm_args=()):
        row = lambda i: (i, 0)
        cx = _exchange_specs(comm)
        return pl.pallas_call(
            _with_exchange(body, comm, 9, 2, seq // tl),
            grid=(seq // tl,),
            in_specs=[pl.BlockSpec((tl, b - a), row) for a, b in SEGS]
            + [_resident((D_MODEL, NP)), pl.BlockSpec((tl, D_MODEL), row), _resident((8, D_MODEL)),
               pl.BlockSpec((tl, D_MODEL), row)] + cx["specs"],
            out_specs=[pl.BlockSpec((tl, D_MODEL), row), pl.BlockSpec((8, D_MODEL), lambda i: (0, 0))] + cx["specs"],
            out_shape=[jax.ShapeDtypeStruct((seq, D_MODEL), F32), jax.ShapeDtypeStruct((8, D_MODEL), F32)]
            + cx["out_shape"],
            scratch_shapes=cx["scratch"],
            compiler_params=pltpu.CompilerParams(dimension_semantics=("arbitrary",), vmem_limit_bytes=VMEM_LIMIT,
                                                 has_side_effects=comm is not None),
            name="inproj_bwd_dx" + cx["tag"],
        )(dg, ds, dr, dgs, dss, w, x, pn, dxn, *comm_args)

    return call


def _make_inproj_bwd_dw_small(seq, tl, name):
    def body(ht_ref, a_ref, b_ref, dw_ref):
        @pl.when(pl.program_id(0) == 0)
        def _():
            dw_ref[...] = jnp.zeros_like(dw_ref)

        ht = ht_ref[...]
        dw_ref[:, 0:128] += jnp.dot(ht, a_ref[...], preferred_element_type=F32)
        dw_ref[:, 128:256] += jnp.dot(ht, b_ref[...], preferred_element_type=F32)

    def call(ht, a, b):
        return pl.pallas_call(
            body,
            grid=(seq // tl,),
            in_specs=[pl.BlockSpec((D_MODEL, tl), lambda i: (0, i)), pl.BlockSpec((tl, 128), lambda i: (i, 0)),
                      pl.BlockSpec((tl, 128), lambda i: (i, 0))],
            out_specs=pl.BlockSpec((D_MODEL, 256), lambda i: (0, 0)),
            out_shape=jax.ShapeDtypeStruct((D_MODEL, 256), F32),
            compiler_params=pltpu.CompilerParams(dimension_semantics=("arbitrary",), vmem_limit_bytes=VMEM_LIMIT),
            name=name,
        )(ht, a, b)

    return call


def _make_inproj_bwd_dw(seq, tl, width, tn, name):
    def body(ht_ref, d_ref, dw_ref):
        @pl.when(pl.program_id(1) == 0)
        def _():
            dw_ref[...] = jnp.zeros_like(dw_ref)

        dw_ref[...] += jnp.dot(ht_ref[...], d_ref[...], preferred_element_type=F32)

    def call(ht, d):
        return pl.pallas_call(
            body,
            grid=(width // tn, seq // tl),
            in_specs=[pl.BlockSpec((D_MODEL, tl), lambda j, i: (0, i)), pl.BlockSpec((tl, tn), lambda j, i: (i, j))],
            out_specs=pl.BlockSpec((D_MODEL, tn), lambda j, i: (0, j)),
            out_shape=jax.ShapeDtypeStruct((D_MODEL, width), F32),
            compiler_params=pltpu.CompilerParams(dimension_semantics=("arbitrary", "arbitrary"),
                                                 vmem_limit_bytes=VMEM_LIMIT),
            name=name,
        )(ht, d)

    return call


ADAM_LR, ADAM_B1, ADAM_B2, ADAM_EPS, ADAM_WD, ADAM_STEP = 0.001, 0.9, 0.999, 1e-08, 0.01, 10


def _adam_math(w, g, m, v):
    m = ADAM_B1 * m + (1.0 - ADAM_B1) * g
    v = ADAM_B2 * v + (1.0 - ADAM_B2) * (g * g)
    m_hat = m / (1.0 - ADAM_B1 ** ADAM_STEP)
    v_hat = v / (1.0 - ADAM_B2 ** ADAM_STEP)
    delta = -ADAM_LR * (m_hat / (jnp.sqrt(v_hat) + ADAM_EPS) + ADAM_WD * w)
    return delta, m, v


def _adamw(w, g, m, v, name):
    shape = w.shape
    cols = shape[-1]
    rows = w.size // cols
    tr = rows if rows <= 512 else 256
    assert rows % tr == 0

    def body(w_ref, g_ref, m_ref, v_ref, d_ref, mo_ref, vo_ref):
        d_ref[...], mo_ref[...], vo_ref[...] = _adam_math(w_ref[...], g_ref[...], m_ref[...], v_ref[...])

    spec = pl.BlockSpec((tr, cols), lambda i: (i, 0))
    outs = pl.pallas_call(
        body,
        grid=(rows // tr,),
        in_specs=[spec] * 4,
        out_specs=[spec] * 3,
        out_shape=[jax.ShapeDtypeStruct((rows, cols), F32)] * 3,
        compiler_params=pltpu.CompilerParams(dimension_semantics=("arbitrary",), vmem_limit_bytes=VMEM_LIMIT),
        name=name,
    )(*[a.reshape(rows, cols) for a in (w, g, m, v)])
    return (g,) + tuple(o.reshape(shape) for o in outs)


def _adamw_pairs(w, mine, theirs, m, v, name):
    na, r, cols = w.shape
    assert na == 2
    tr = 256
    assert r % tr == 0

    def body(w_ref, a0_ref, b0_ref, a1_ref, b1_ref, m_ref, v_ref, g_ref, d_ref, mo_ref, vo_ref):
        g = jnp.where(pl.program_id(0) == 0, a0_ref[...] + b0_ref[...], a1_ref[...] + b1_ref[...])
        g_ref[...] = g
        d_ref[...], mo_ref[...], vo_ref[...] = _adam_math(w_ref[...], g, m_ref[...], v_ref[...])

    nblk = r // tr
    full = pl.BlockSpec((None, tr, cols), lambda a, i: (a, i, 0))
    lay0 = pl.BlockSpec((None, tr, cols), lambda a, i: (0, i * (1 - a) + (nblk - 1) * a, 0))
    lay1 = pl.BlockSpec((None, tr, cols), lambda a, i: (0, i * a, 0))
    return pl.pallas_call(
        body,
        grid=(na, nblk),
        in_specs=[full, lay0, lay0, lay1, lay1, full, full],
        out_specs=[full] * 4,
        out_shape=[jax.ShapeDtypeStruct(w.shape, F32)] * 4,
        compiler_params=pltpu.CompilerParams(dimension_semantics=("arbitrary",) * 2, vmem_limit_bytes=VMEM_LIMIT),
        name=name,
    )(w, mine[0], theirs[0], mine[1], theirs[1], m, v)


MESH = pl.DeviceIdType.MESH
ANY = pl.BlockSpec(memory_space=pl.ANY)
CHIP_REL = ((1, 0), (0, 1), (1, 1))


def _flip(v, d):
    return 1 - v if d else v


class _ChipExchange:
    def __init__(self, kind, arrs, swap=()):
        self.kind, self.n_chip, self.n = kind, len(arrs), len(arrs) + len(swap)
        if kind == "gather":
            self.out_shape = [jax.ShapeDtypeStruct((4,) + a.shape, a.dtype) for a in arrs]
        else:
            self.out_shape = [jax.ShapeDtypeStruct((3,) + a.shape[1:], a.dtype) for a in arrs]
        self.out_shape += [jax.ShapeDtypeStruct(a.shape, a.dtype) for a in swap]
        self.scratch = [pltpu.SemaphoreType.DMA((4 * self.n,)), pltpu.SemaphoreType.DMA((4 * self.n,))]

    def _copies(self, ins, outs, sems):
        send_sems, recv_sems = sems
        x, y, c = lax.axis_index("x"), lax.axis_index("y"), lax.axis_index("c")
        me = 2 * x + y
        pairs = []
        for a in range(self.n_chip, self.n):
            cp = pltpu.make_async_remote_copy(
                src_ref=ins[a], dst_ref=outs[a], send_sem=send_sems.at[4 * a], recv_sem=recv_sems.at[4 * a],
                device_id=(x, y, 1 - c), device_id_type=MESH)
            pairs.append((cp, cp))
        for a in range(self.n_chip):
            for k, (dx, dy) in enumerate(CHIP_REL):
                px, py = _flip(x, dx), _flip(y, dy)
                sem = dict(send_sem=send_sems.at[4 * a + k], recv_sem=recv_sems.at[4 * a + k],
                           device_id=(px, py, c), device_id_type=MESH)
                if self.kind == "gather":
                    out = pltpu.make_async_remote_copy(src_ref=ins[a], dst_ref=outs[a].at[me], **sem)
                    inc = pltpu.make_async_remote_copy(src_ref=ins[a], dst_ref=outs[a].at[2 * px + py], **sem)
                else:
                    out = pltpu.make_async_remote_copy(src_ref=ins[a].at[2 * px + py], dst_ref=outs[a].at[k], **sem)
                    inc = out
                pairs.append((out, inc))
            if self.kind == "gather":
                own = pltpu.make_async_remote_copy(
                    src_ref=ins[a], dst_ref=outs[a].at[me], send_sem=send_sems.at[4 * a + 3],
                    recv_sem=recv_sems.at[4 * a + 3], device_id=(x, y, 1 - c), device_id_type=MESH)
                pairs.append((own, own))
        return pairs

    def start(self, ins, outs, sems):
        for out, _ in self._copies(ins, outs, sems):
            out.start()

    def finish(self, ins, outs, sems):
        pairs = self._copies(ins, outs, sems)
        for _, inc in pairs:
            inc.wait_recv()
        for out, _ in pairs:
            out.wait_send()


def _with_exchange(body, comm, n_in, n_out, nb):
    if comm is None:
        return body

    def wrapped(*refs):
        ins = refs[:n_in]
        c_in = refs[n_in:n_in + comm.n]
        outs = refs[n_in + comm.n:n_in + comm.n + n_out]
        c_out = refs[n_in + comm.n + n_out:n_in + 2 * comm.n + n_out]
        rest = refs[n_in + 2 * comm.n + n_out:]
        scratch, sems = rest[:len(rest) - 2], rest[len(rest) - 2:]

        @pl.when(pl.program_id(0) == 0)
        def _():
            comm.start(c_in, c_out, sems)

        body(*ins, *outs, *scratch)

        @pl.when(pl.program_id(0) == nb - 1)
        def _():
            comm.finish(c_in, c_out, sems)

    return wrapped


def _exchange_specs(comm):
    if comm is None:
        return dict(specs=[], out_shape=[], scratch=[], tag="")
    return dict(specs=[pl.BlockSpec(memory_space=pl.ANY)] * comm.n, out_shape=list(comm.out_shape),
                scratch=list(comm.scratch), tag="_" + comm.kind)


def _half(ref_or_shape, half):
    r = ref_or_shape[-2] // 2
    return pl.ds(half * r, r)


def _ag_rows(arrs, name):
    n = len(arrs)

    def body(*refs):
        ins, outs = refs[:n], refs[n:2 * n]
        send_sems, recv_sems, fsend_sems, frecv_sems, loc_sems = refs[2 * n:]
        x, y, c = lax.axis_index("x"), lax.axis_index("y"), lax.axis_index("c")
        me = 2 * x + y
        sib = (x, y, 1 - c)

        def chip_of(k):
            dx, dy = CHIP_REL[k]
            return _flip(x, dx), _flip(y, dy)

        def ici(a, k, slot):
            px, py = chip_of(k)
            rows = _half(arrs[a].shape, c)
            return pltpu.make_async_remote_copy(
                src_ref=ins[a].at[:, rows, :], dst_ref=outs[a].at[slot, :, rows, :], send_sem=send_sems.at[a * 3 + k],
                recv_sem=recv_sems.at[a * 3 + k], device_id=(px, py, c), device_id_type=MESH)

        def fwd(a, k, half):
            px, py = chip_of(k)
            blk = outs[a].at[2 * px + py, :, _half(arrs[a].shape, half), :]
            return pltpu.make_async_remote_copy(
                src_ref=blk, dst_ref=blk, send_sem=fsend_sems.at[a * 3 + k], recv_sem=frecv_sems.at[a * 3 + k],
                device_id=sib, device_id_type=MESH)

        own = [pltpu.make_async_remote_copy(src_ref=ins[a], dst_ref=outs[a].at[me], send_sem=loc_sems.at[a],
                                            recv_sem=loc_sems.at[n + a], device_id=sib, device_id_type=MESH)
               for a in range(n)]
        for cp in own:
            cp.start()
        for a in range(n):
            for k in range(3):
                ici(a, k, me).start()
        for a in range(n):
            for k in range(3):
                px, py = chip_of(k)
                ici(a, k, 2 * px + py).wait_recv()
                fwd(a, k, c).start()
        for a in range(n):
            for k in range(3):
                fwd(a, k, 1 - c).wait_recv()
        for a in range(n):
            for k in range(3):
                ici(a, k, me).wait_send()
                fwd(a, k, c).wait_send()
        for cp in own:
            cp.wait()

    return pl.pallas_call(
        body,
        in_specs=[ANY] * n,
        out_specs=[ANY] * n,
        out_shape=[jax.ShapeDtypeStruct((4,) + a.shape, a.dtype) for a in arrs],
        scratch_shapes=[pltpu.SemaphoreType.DMA((3 * n,)) for _ in range(4)] + [pltpu.SemaphoreType.DMA((2 * n,))],
        compiler_params=pltpu.CompilerParams(has_side_effects=True),
        name=name,
    )(*arrs)


def _sum_chips(own, recv, chip, name):
    _, na, r, cols = own.shape
    tr = 256
    assert r % tr == 0

    def body(chip_ref, o_ref, r_ref, s_ref):
        s_ref[...] = ((o_ref[...] + r_ref[0].astype(F32)) + r_ref[1].astype(F32)) + r_ref[2].astype(F32)

    return pl.pallas_call(
        body,
        grid_spec=pltpu.PrefetchScalarGridSpec(
            num_scalar_prefetch=1,
            grid=(na, r // tr),
            in_specs=[pl.BlockSpec((None, None, tr, cols), lambda a, i, ch: (ch[0], a, i, 0)),
                      pl.BlockSpec((3, None, tr, cols), lambda a, i, ch: (0, a, i, 0))],
            out_specs=pl.BlockSpec((None, tr, cols), lambda a, i, ch: (a, i, 0))),
        out_shape=jax.ShapeDtypeStruct((na, r, cols), F32),
        compiler_params=pltpu.CompilerParams(dimension_semantics=("arbitrary",) * 2, vmem_limit_bytes=VMEM_LIMIT),
        name=name,
    )(chip, own, recv)


def _swap_sibling(arrs, name):
    n = len(arrs)

    def body(*refs):
        ins, outs = refs[:n], refs[n:2 * n]
        send_sems, recv_sems = refs[2 * n:]
        x, y, c = lax.axis_index("x"), lax.axis_index("y"), lax.axis_index("c")
        cps = [pltpu.make_async_remote_copy(src_ref=ins[a], dst_ref=outs[a], send_sem=send_sems.at[a],
                                            recv_sem=recv_sems.at[a], device_id=(x, y, 1 - c), device_id_type=MESH)
               for a in range(n)]
        for cp in cps:
            cp.start()
        for cp in cps:
            cp.wait_recv()
        for cp in cps:
            cp.wait_send()

    return pl.pallas_call(
        body,
        in_specs=[ANY] * n,
        out_specs=[ANY] * n,
        out_shape=[jax.ShapeDtypeStruct(a.shape, a.dtype) for a in arrs],
        scratch_shapes=[pltpu.SemaphoreType.DMA((n,)), pltpu.SemaphoreType.DMA((n,))],
        compiler_params=pltpu.CompilerParams(has_side_effects=True),
        name=name,
    )(*arrs)


def _allreduce_small(vec, name):
    rows = vec.shape[0]

    def body(v_ref, out_ref, gat_ref, send_sems, recv_sems):
        x, y, c = lax.axis_index("x"), lax.axis_index("y"), lax.axis_index("c")
        me = 4 * x + 2 * y + c

        def remote(k, slot):
            dx, dy, dc = (k >> 2) & 1, (k >> 1) & 1, k & 1
            return pltpu.make_async_remote_copy(
                src_ref=v_ref, dst_ref=gat_ref.at[slot], send_sem=send_sems.at[k - 1], recv_sem=recv_sems.at[k - 1],
                device_id=(_flip(x, dx), _flip(y, dy), _flip(c, dc)), device_id_type=MESH)

        gat_ref[me] = v_ref[...]
        for k in range(1, 8):
            remote(k, me).start()
        for k in range(1, 8):
            dx, dy, dc = (k >> 2) & 1, (k >> 1) & 1, k & 1
            remote(k, 4 * _flip(x, dx) + 2 * _flip(y, dy) + _flip(c, dc)).wait_recv()
        for k in range(1, 8):
            remote(k, me).wait_send()
        acc = gat_ref[0]
        for j in range(1, 8):
            acc = acc + gat_ref[j]
        out_ref[...] = acc

    vm = pl.BlockSpec(memory_space=pltpu.VMEM)
    return pl.pallas_call(
        body,
        in_specs=[vm],
        out_specs=vm,
        out_shape=jax.ShapeDtypeStruct(vec.shape, F32),
        scratch_shapes=[pltpu.VMEM((8, rows, 128), F32), pltpu.SemaphoreType.DMA((7,)), pltpu.SemaphoreType.DMA((7,))],
        compiler_params=pltpu.CompilerParams(has_side_effects=True),
        name=name,
    )(vec)


def _pad8(v, width, lane0=0):
    v = v.reshape(1, -1) if v.ndim == 1 else v
    return jnp.zeros((8, width), F32).at[:v.shape[0], lane0:lane0 + v.shape[1]].set(v.astype(F32))


def _relayout_w_in(g):
    tr = 128
    q = N_IN // 4

    def body(g_ref, o_ref):
        w = jnp.concatenate([g_ref[j] for j in range(4)], axis=1)
        z = lambda n: jnp.zeros((tr, n), w.dtype)
        o_ref[...] = jnp.concatenate([w[:, 0:2048], w[:, 2056:4616], w[:, 4632:6680],
                                      w[:, 2048:2056], z(120), w[:, 4616:4632], z(112)], axis=1)

    return pl.pallas_call(
        body,
        grid=(D_MODEL // tr,),
        in_specs=[pl.BlockSpec((4, tr, q), lambda i: (0, i, 0))],
        out_specs=pl.BlockSpec((tr, NP), lambda i: (i, 0)),
        out_shape=jax.ShapeDtypeStruct((D_MODEL, NP), g.dtype),
        compiler_params=pltpu.CompilerParams(dimension_semantics=("arbitrary",), vmem_limit_bytes=VMEM_LIMIT),
        name="relayout_w_in",
    )(g)


def _unlayout_dw_in(dg, ds, dr, dsm):
    tr = 128
    q = N_IN // 4

    def body(g_ref, s_ref, r_ref, sm_ref, o_ref, ob_ref):
        w = jnp.concatenate([g_ref[...], sm_ref[:, 0:8], s_ref[...], sm_ref[:, 128:144], r_ref[...]], axis=1)
        for j in range(4):
            blk = w[:, q * j:q * (j + 1)]
            o_ref[j] = blk
            ob_ref[j] = blk.astype(BF16)

    row = lambda i: (i, 0)
    return pl.pallas_call(
        body,
        grid=(D_MODEL // tr,),
        in_specs=[pl.BlockSpec((tr, d.shape[1]), row) for d in (dg, ds, dr, dsm)],
        out_specs=[pl.BlockSpec((4, tr, q), lambda i: (0, i, 0))] * 2,
        out_shape=[jax.ShapeDtypeStruct((4, D_MODEL, q), F32), jax.ShapeDtypeStruct((4, D_MODEL, q), BF16)],
        compiler_params=pltpu.CompilerParams(dimension_semantics=("arbitrary",), vmem_limit_bytes=VMEM_LIMIT),
        name="unlayout_dw_in",
    )(dg, ds, dr, dsm)


TB = 256
TB_RET = 512
TL = 1024
TL_IN = 512
TL_OB = 1024
TK = 2048


def kernel(x, pre_norm, post_norm, w_in, gdn_conv, gdn_A_log, gdn_dt_bias, gdn_norm, ssd_conv, ssd_conv_b, ssd_A_log, ssd_dt_bias, ssd_D, ssd_norm, ret_norm, w_out, loss_target, m_pre_norm, m_post_norm, m_w_in, m_gdn_conv, m_gdn_A_log, m_gdn_dt_bias, m_gdn_norm, m_ssd_conv, m_ssd_conv_b, m_ssd_A_log, m_ssd_dt_bias, m_ssd_D, m_ssd_norm, m_ret_norm, m_w_out, v_pre_norm, v_post_norm, v_w_in, v_gdn_conv, v_gdn_A_log, v_gdn_dt_bias, v_gdn_norm, v_ssd_conv, v_ssd_conv_b, v_ssd_A_log, v_ssd_dt_bias, v_ssd_D, v_ssd_norm, v_ret_norm, v_w_out):
    seq = x.shape[1]
    chip = 2 * lax.axis_index("x") + lax.axis_index("y")
    x0 = x[0]

    wi_b, wo_b = w_in.astype(BF16), w_out.astype(BF16)
    (wi0_g,) = _ag_rows([wi_b[0:1]], "ag_weights")
    full_w_in = _relayout_w_in
    wp = [full_w_in(wi0_g[:, 0]), None]
    ag0 = _ChipExchange("gather", [wo_b[0], wo_b[1], gdn_conv, ssd_conv])
    ag1 = _ChipExchange("gather", [wi_b[1]])
    rope_c, rope_s = _rope_tables(seq)

    saved = []
    xc = x0
    for l in range(DEPTH):
        p = dict(
            pn=_pad8(pre_norm[l], D_MODEL), qn=_pad8(post_norm[l], D_MODEL),
            g_prm=_pad8(jnp.stack([gdn_A_log[l], gdn_dt_bias[l]]), 128, 4), g_nw=_pad8(gdn_norm[l], 128),
            s_cb=_pad8(ssd_conv_b[l], 1536),
            s_prm=_pad8(jnp.stack([ssd_A_log[l], ssd_dt_bias[l], ssd_D[l]]), 128), s_nw=_pad8(ssd_norm[l], SSD_W),
            r_nw=_pad8(ret_norm[l], 128))
        if l == 0:
            pg, ps, pr, gs, ss, ht, wo0_g, wo1_g, gcv_g, scv_g = _make_inproj(seq, TL_IN)(
                xc, p["pn"], wp[l], comm=ag0, comm_args=(wo_b[0], wo_b[1], gdn_conv, ssd_conv))
            wo = [wo0_g.reshape(2048, D_MODEL), wo1_g.reshape(2048, D_MODEL)]
            gcv = jnp.transpose(gcv_g, (1, 2, 0, 3)).reshape(DEPTH, CONV_W, 1536)
            scv = jnp.transpose(scv_g, (1, 2, 0, 3)).reshape(DEPTH, CONV_W, 1536)
        else:
            pg, ps, pr, gs, ss, ht = _make_inproj(seq, TL_IN)(xc, p["pn"], wp[l])
        p.update(g_cw=_pad8(gcv[l], 1536), s_cw=_pad8(scv[l], 1536))
        if l == 0:
            oa, stg, tig, uwg, gpre, wi1_g = _make_gdn_fwd(seq, TB)(
                pg, gs, p["g_cw"], p["g_prm"], p["g_nw"], comm=ag1, comm_args=(wi_b[1],))
            wp[1] = full_w_in(wi1_g)
        else:
            oa, stg, tig, uwg, gpre = _make_gdn_fwd(seq, TB)(pg, gs, p["g_cw"], p["g_prm"], p["g_nw"])
        ob, sts, spre, sy = _make_ssd_fwd(seq, TB)(ps, ss, p["s_cw"], p["s_cb"], p["s_prm"], p["s_nw"])
        oc, str_ = _make_ret_fwd(seq, TB_RET)(pr, rope_c, rope_s, p["r_nw"])
        if l == DEPTH - 1:
            out, dxn, lossp = _make_outproj_loss(seq, TL)(oa, ob, oc, wo[l], xc, p["qn"], loss_target[0])
            xn = None
        else:
            out, xn = _make_outproj(seq, TL)(oa, ob, oc, wo[l], xc, p["qn"])
        saved.append(dict(p=p, x=xc, ht=ht, spre=spre, sy=sy, gpre=gpre, pg=pg, ps=ps, pr=pr, gs=gs, ss=ss, stg=stg, tig=tig, uwg=uwg, sts=sts, str=str_,
                          oa=oa, ob=ob, oc=oc, out=out))
        xc = xn

    small = [None] * DEPTH
    gin, gin_b, gout, gout_b, q_in, q_out, s_in, s_out, t_in, t_out = ([None] * DEPTH for _ in range(10))
    chip1 = chip.astype(jnp.int32).reshape(1)

    def sum_chips(l):
        return (_sum_chips(gin[l][:, None], q_in[l][:, None], chip1, f"sum_chips_w_in{l}"),
                _sum_chips(gout[l][:, None], q_out[l][:, None], chip1, f"sum_chips_w_out{l}"))

    for l in reversed(range(DEPTH)):
        s = saved[l]
        p = s["p"]
        doa, dob, doc, dqn, dwo_l, dwo_b = _make_outproj_bwd(seq, TL_OB)(
            dxn, s["out"], s["oa"], s["ob"], s["oc"], wo[l], p["qn"])
        gout[l], gout_b[l] = dwo_l.reshape(4, 512, D_MODEL), dwo_b.reshape(4, 512, D_MODEL)
        gdn_args = (s["pg"], s["gpre"], s["gs"], p["g_cw"], p["g_prm"], p["g_nw"], s["stg"], s["tig"], s["uwg"], doa)
        if l == 0:
            payload = (gout_b[0],)
            dpg, dgs, dcw_g, dprm_g, dnw_g, q_out[0] = _make_gdn_bwd(seq, TB)(
                *gdn_args, comm=_ChipExchange("scatter", payload), comm_args=payload)
        else:
            dpg, dgs, dcw_g, dprm_g, dnw_g = _make_gdn_bwd(seq, TB)(*gdn_args)
        ssd_args = (s["ps"], s["spre"], s["sy"], s["ss"], p["s_cw"], p["s_cb"], p["s_prm"], p["s_nw"], s["sts"], dob)
        if l == 0:
            payload = (gin_b[1], gout_b[1])
            dps, dss, dcw_s, dcb_s, dprm_s, dnw_s, q_in[1], q_out[1] = _make_ssd_bwd(seq, TB)(
                *ssd_args, comm=_ChipExchange("scatter", payload), comm_args=payload)
        else:
            dps, dss, dcw_s, dcb_s, dprm_s, dnw_s = _make_ssd_bwd(seq, TB)(*ssd_args)
        dpr, dnw_r = _make_ret_bwd(seq, TB_RET)(s["pr"], rope_c, rope_s, p["r_nw"], s["str"], doc)
        dws = [_make_inproj_bwd_dw(seq, TK, d.shape[1], tn, f"inproj_bwd_dw{i}")(s["ht"], d)
               for i, (d, tn) in enumerate(((dpg, 2048), (dps, 1280), (dpr, 2048)))]
        dws.append(_make_inproj_bwd_dw_small(seq, TK, "inproj_bwd_dw3")(s["ht"], dgs, dss))
        gin[l], gin_b[l] = _unlayout_dw_in(*dws)
        dx_args = (dpg, dps, dpr, dgs, dss, wp[l], s["x"], p["pn"], dxn)
        if l == 0:
            s_in[1], s_out[1] = sum_chips(1)
            payload, swap = (gin_b[0],), (s_in[1], s_out[1])
            dx, dpn, q_in[0], t_in[1], t_out[1] = _make_inproj_bwd_dx(seq, TL_IN)(
                *dx_args, comm=_ChipExchange("scatter", payload, swap), comm_args=payload + swap)
        else:
            dx, dpn = _make_inproj_bwd_dx(seq, TL_IN)(*dx_args)
        small[l] = [dpn[0], dqn[0], dcw_g[0:4].reshape(-1), dprm_g[0, 4:8], dprm_g[1, 4:8], dnw_g[0],
                    dcw_s[0:4].reshape(-1), dcb_s[0], dprm_s[0, 0:16], dprm_s[1, 0:16], dprm_s[2, 0:16],
                    dnw_s[0], dnw_r[0]]
        dxn = dx
    grad_x = dxn[None]

    sizes = [a.shape[0] for a in small[0]]
    flat = jnp.concatenate(small[0] + small[1] + [lossp[0, 0:1]])
    n_flat = flat.shape[0]
    rows = -(-n_flat // 1024) * 8
    red = _allreduce_small(jnp.pad(flat, (0, rows * 128 - n_flat)).reshape(rows, 128), "allreduce_small").reshape(-1)
    per = sum(sizes)
    loss = red[2 * per]

    def pick(i):
        off = sum(sizes[:i])
        return jnp.stack([red[l * per + off:l * per + off + sizes[i]] for l in range(DEPTH)])

    g_small = dict(
        pre_norm=pick(0), post_norm=pick(1),
        gdn_conv=lax.dynamic_slice_in_dim(pick(2).reshape(DEPTH, CONV_W, 1536), chip * 384, 384, axis=2),
        gdn_A_log=pick(3), gdn_dt_bias=pick(4), gdn_norm=pick(5),
        ssd_conv=lax.dynamic_slice_in_dim(pick(6).reshape(DEPTH, CONV_W, 1536), chip * 384, 384, axis=2),
        ssd_conv_b=pick(7), ssd_A_log=pick(8), ssd_dt_bias=pick(9), ssd_D=pick(10), ssd_norm=pick(11),
        ret_norm=pick(12))

    s_in[0], s_out[0] = sum_chips(0)
    t_in[0], t_out[0] = _swap_sibling([s_in[0], s_out[0]], "swap_grads")

    weights = dict(pre_norm=pre_norm, post_norm=post_norm, w_in=w_in, gdn_conv=gdn_conv, gdn_A_log=gdn_A_log,
                   gdn_dt_bias=gdn_dt_bias, gdn_norm=gdn_norm, ssd_conv=ssd_conv, ssd_conv_b=ssd_conv_b,
                   ssd_A_log=ssd_A_log, ssd_dt_bias=ssd_dt_bias, ssd_D=ssd_D, ssd_norm=ssd_norm, ret_norm=ret_norm,
                   w_out=w_out)
    ms = dict(pre_norm=m_pre_norm, post_norm=m_post_norm, w_in=m_w_in, gdn_conv=m_gdn_conv, gdn_A_log=m_gdn_A_log,
              gdn_dt_bias=m_gdn_dt_bias, gdn_norm=m_gdn_norm, ssd_conv=m_ssd_conv, ssd_conv_b=m_ssd_conv_b,
              ssd_A_log=m_ssd_A_log, ssd_dt_bias=m_ssd_dt_bias, ssd_D=m_ssd_D, ssd_norm=m_ssd_norm,
              ret_norm=m_ret_norm, w_out=m_w_out)
    vs = dict(pre_norm=v_pre_norm, post_norm=v_post_norm, w_in=v_w_in, gdn_conv=v_gdn_conv, gdn_A_log=v_gdn_A_log,
              gdn_dt_bias=v_gdn_dt_bias, gdn_norm=v_gdn_norm, ssd_conv=v_ssd_conv, ssd_conv_b=v_ssd_conv_b,
              ssd_A_log=v_ssd_A_log, ssd_dt_bias=v_ssd_dt_bias, ssd_D=v_ssd_D, ssd_norm=v_ssd_norm,
              ret_norm=v_ret_norm, w_out=v_w_out)
    names = list(weights)
    res = {}
    for nme in names:
        if nme == "w_in":
            res[nme] = _adamw_pairs(w_in, s_in, t_in, m_w_in, v_w_in, "adamw_w_in")
        elif nme == "w_out":
            res[nme] = _adamw_pairs(w_out, s_out, t_out, m_w_out, v_w_out, "adamw_w_out")
        else:
            res[nme] = _adamw(weights[nme], g_small[nme], ms[nme], vs[nme], "adamw_" + nme)
    return (loss, grad_x, *[res[n][0] for n in names], *[res[n][1] for n in names],
            *[res[n][2] for n in names], *[res[n][3] for n in names])
```

```python
import math

import jax
import jax.numpy as jnp
from jax import lax
from jax.experimental import pallas as pl
from jax.experimental.pallas import tpu as pltpu

F32 = jnp.float32
BF16 = jnp.bfloat16

D_MODEL = 1024
DEPTH = 2
CH = 64
CONV_W = 4
EPS = 1e-6
GDN_H, GDN_D = 4, 128
SSD_H, SSD_P, SSD_N, SSD_G = 16, 64, 128, 2
SSD_W = SSD_H * SSD_P
RET_H, RET_D = 4, 128
ROPE_BASE = 10000.0
N_IN = 6680
NEG = -1e30

V7X_VMEM_BYTES = 64 * 1024 * 1024
VMEM_LIMIT = V7X_VMEM_BYTES * 7 // 8


def _dot(a, b):
    return jnp.dot(a.astype(BF16), b.astype(BF16), preferred_element_type=F32)


def _dot_nt(a, b):
    return lax.dot_general(a.astype(BF16), b.astype(BF16), (((1,), (1,)), ((), ())), preferred_element_type=F32)


def _dot_tn(a, b):
    return lax.dot_general(a.astype(BF16), b.astype(BF16), (((0,), (0,)), ((), ())), preferred_element_type=F32)


def _split(a):
    hi = a.astype(BF16)
    return hi, (a - hi.astype(F32)).astype(BF16)


def _dot01l(m, v):
    vh, vl = _split(v)
    mb = m.astype(BF16)
    return jnp.dot(mb, vh, preferred_element_type=F32) + jnp.dot(mb, vl, preferred_element_type=F32)


def _dot01r(v, m):
    return jnp.dot(v.astype(BF16), m.astype(BF16), preferred_element_type=F32)


def _sigmoid(x):
    return jax.nn.sigmoid(x)


def _silu(x):
    return x * _sigmoid(x)


def _dsilu(x):
    s = _sigmoid(x)
    return s * (1.0 + x * (1.0 - s))


def _softplus(x):
    return jnp.maximum(x, 0.0) + jnp.log1p(jnp.exp(-jnp.abs(x)))


def _iota2(shape, dim):
    return lax.broadcasted_iota(jnp.int32, shape, dim)


def _chunk_tri(tb, upper=False):
    r = _iota2((tb, tb), 0)
    c = _iota2((tb, tb), 1)
    same = jnp.right_shift(r, 6) == jnp.right_shift(c, 6)
    return (same & ((c >= r) if upper else (c <= r))).astype(F32)


def _masks():
    r = _iota2((CH, CH), 0)
    c = _iota2((CH, CH), 1)
    return r >= c, r > c, (r == c).astype(F32)


def _put_lane(col, lane_idx, width=128):
    lane = _iota2((col.shape[0], width), 1)
    return jnp.where(lane == lane_idx, col, 0.0)


def _conv_taps(raw, halo8, tb):
    ext = jnp.concatenate([halo8, raw], axis=0)
    return [raw] + [pltpu.roll(ext, s, axis=0)[8:] for s in (1, 2, 3)]


def _conv_back(dpre, nxt8, tb):
    ext = jnp.concatenate([dpre, nxt8], axis=0)
    return [dpre] + [pltpu.roll(ext, tb + 8 - s, axis=0)[:tb] for s in (1, 2, 3)]


def _rms_fwd(o, w, n):
    r = lax.rsqrt(jnp.sum(o * o, axis=-1, keepdims=True) * (1.0 / n) + EPS)
    on = o * r
    return on, r, on * w


def _rms_bwd(dy, on, r, w, n):
    don = dy * w
    return r * (don - on * (jnp.sum(don * on, axis=-1, keepdims=True) * (1.0 / n))), dy * on


def _put_cols(v, g, gw):
    z = jnp.zeros_like(v)
    return jnp.concatenate([v, z] if g == 0 else [z, v], axis=1)


def _gdn_common(pg_ref, halo8, sm, cw, prm, tb, pre=None):
    raw = pg_ref[:, 0:1536]
    if pre is None:
        taps = _conv_taps(raw, halo8, tb)
        pre = taps[0] * cw[3:4, :] + taps[1] * cw[2:3, :] + taps[2] * cw[1:2, :] + taps[3] * cw[0:1, :]
    act = _silu(pre)
    beta = _sigmoid(sm)
    sp_in = sm + prm[1:2, :]
    g = -jnp.exp(prm[0:1, :]) * _softplus(sp_in)
    gc = _dot01l(_chunk_tri(tb), g)
    return raw, pre, act, beta, sp_in, g, gc


_NN = (((2,), (1,)), ((0,), (0,)))
_NT = (((2,), (2,)), ((0,), (0,)))
_TN = (((1,), (1,)), ((0,), (0,)))


def _bdot(a, b, dn):
    return lax.dot_general(a.astype(BF16), b.astype(BF16), dn, preferred_element_type=F32)


def _binv_unit_lower(a, eye):
    r = _iota2((CH, CH), 0)
    c = _iota2((CH, CH), 1)
    d = eye - jnp.where((jnp.right_shift(r, 1) == jnp.right_shift(c, 1)), a, 0.0)
    ab = a.astype(BF16)
    zero = jnp.zeros((), BF16)
    for lb in range(1, 6):
        same = jnp.right_shift(r, lb + 1) == jnp.right_shift(c, lb + 1)
        low = (jnp.bitwise_and(jnp.right_shift(r, lb), 1) == 1) & (jnp.bitwise_and(jnp.right_shift(c, lb), 1) == 0)
        db = d.astype(BF16)
        t = _bdot(jnp.where(same & low, ab, zero), db, _NN)
        d = d - _bdot(db, t, _NN)
    return d


def _rsum(v):
    return jnp.sum(v, axis=-1, keepdims=True)


def _gdn_batch(act, beta, gc, gct, eg_all, ncb, masks):
    causal, strict, _ = masks

    def st(fn):
        return jnp.stack([fn(c, h, slice(c * CH, (c + 1) * CH)) for c in range(ncb) for h in range(GDN_H)])

    qr = st(lambda c, h, r: act[r, h * 128:(h + 1) * 128])
    kr = st(lambda c, h, r: act[r, 512 + h * 128:512 + (h + 1) * 128])
    vh = st(lambda c, h, r: act[r, 1024 + h * 128:1024 + (h + 1) * 128])
    bh = st(lambda c, h, r: beta[r, h:h + 1])
    gcol = st(lambda c, h, r: gc[r, 4 + h:5 + h])
    grow = st(lambda c, h, r: gct[4 + h:5 + h, r])
    eg = st(lambda c, h, r: eg_all[r, 4 + h:5 + h])
    glast = st(lambda c, h, r: gc[(c + 1) * CH - 1:(c + 1) * CH, 4 + h:5 + h])
    rq = lax.rsqrt(_rsum(qr * qr) + EPS)
    rk = lax.rsqrt(_rsum(kr * kr) + EPS)
    qn = qr * rq
    kh = kr * rk
    qh = qn * (GDN_D ** -0.5)
    decay = jnp.exp(jnp.where(causal, gcol - grow, NEG))
    kb = kh * bh
    kd_scale = jnp.exp(glast - gcol)
    return dict(qn=qn, rq=rq, kh=kh, rk=rk, qh=qh, vh=vh, bh=bh, eg=eg, decay=decay, kb=kb, vb=vh * bh, kg=kb * eg,
                qg=qh * eg, kd_scale=kd_scale, kdec=kh * kd_scale, egl=jnp.exp(glast),
                a=jnp.where(strict, _bdot(kb, kh, _NT) * decay, 0.0), attn=_bdot(qh, kh, _NT) * decay)


def _make_gdn_fwd(seq, tb):
    ncb = tb // CH
    nb = seq // tb
    n = ncb * GDN_H

    def body(pg_ref, sm_ref, cw_ref, prm_ref, nw_ref, oa_ref, st_ref, ti_ref, uw_ref, pre_ref, s_scr, halo_scr):
        @pl.when(pl.program_id(0) == 0)
        def _():
            s_scr[...] = jnp.zeros_like(s_scr)
            halo_scr[...] = jnp.zeros_like(halo_scr)

        masks = _masks()
        sm = sm_ref[...]
        raw, pre, act, beta, _, _, gc = _gdn_common(pg_ref, halo_scr[...], sm, cw_ref[...], prm_ref[...], tb)
        halo_scr[...] = raw[tb - 8:tb, :]
        pre_ref[...] = pre
        d = _gdn_batch(act, beta, gc, gc.T, jnp.exp(gc), ncb, masks)
        t = _binv_unit_lower(d["a"], masks[2])
        sol = _bdot(t, jnp.concatenate([d["vb"], d["kg"]], axis=2), _NN)
        ti_ref[...] = t.reshape(ncb, GDN_H, CH, CH)
        uw_ref[...] = sol.reshape(ncb, GDN_H, CH, 256)
        u, w = sol[:, :, :128], sol[:, :, 128:]
        vns = []
        for c in range(ncb):
            bs = slice(c * GDN_H, (c + 1) * GDN_H)
            s = s_scr[...]
            st_ref[c] = s
            vn = u[bs] - _bdot(w[bs], s, _NN)
            s_scr[...] = s * d["egl"][bs] + _bdot(d["kdec"][bs], vn, _TN)
            vns.append(vn)
        v_new = jnp.concatenate(vns, axis=0)
        s_prev = st_ref[...].reshape(n, 128, 128)
        o = _bdot(d["qg"], s_prev, _NN) + _bdot(d["attn"], v_new, _NN)
        _, _, y = _rms_fwd(o, nw_ref[0:1, :], GDN_D)
        for c in range(ncb):
            rows = slice(c * CH, (c + 1) * CH)
            for h in range(GDN_H):
                z = pg_ref[rows, 1536 + h * 128:1536 + (h + 1) * 128]
                oa_ref[rows, h * 128:(h + 1) * 128] = (y[c * GDN_H + h] * _silu(z)).astype(oa_ref.dtype)

    def call(pg, sm, cw, prm, nw, comm=None, comm_args=()):
        blk4 = lambda i: (i, 0, 0, 0)
        cx = _exchange_specs(comm)
        return pl.pallas_call(
            _with_exchange(body, comm, 5, 5, nb),
            grid=(nb,),
            in_specs=[
                pl.BlockSpec((tb, 2048), lambda i: (i, 0)),
                pl.BlockSpec((tb, 128), lambda i: (i, 0)),
                pl.BlockSpec((8, 1536), lambda i: (0, 0)),
                pl.BlockSpec((8, 128), lambda i: (0, 0)),
                pl.BlockSpec((8, 128), lambda i: (0, 0)),
            ] + cx["specs"],
            out_specs=[
                pl.BlockSpec((tb, 512), lambda i: (i, 0)),
                pl.BlockSpec((ncb, GDN_H, 128, 128), blk4),
                pl.BlockSpec((ncb, GDN_H, CH, CH), blk4),
                pl.BlockSpec((ncb, GDN_H, CH, 256), blk4),
                pl.BlockSpec((tb, 1536), lambda i: (i, 0)),
            ] + cx["specs"],
            out_shape=[
                jax.ShapeDtypeStruct((seq, 512), BF16),
                jax.ShapeDtypeStruct((seq // CH, GDN_H, 128, 128), F32),
                jax.ShapeDtypeStruct((seq // CH, GDN_H, CH, CH), F32),
                jax.ShapeDtypeStruct((seq // CH, GDN_H, CH, 256), F32),
                jax.ShapeDtypeStruct((seq, 1536), F32),
            ] + cx["out_shape"],
            scratch_shapes=[pltpu.VMEM((GDN_H, 128, 128), F32), pltpu.VMEM((8, 1536), F32)] + cx["scratch"],
            compiler_params=pltpu.CompilerParams(dimension_semantics=("arbitrary",), vmem_limit_bytes=VMEM_LIMIT,
                                                 has_side_effects=comm is not None),
            name="gdn_fwd" + cx["tag"],
        )(pg, sm, cw, prm, nw, *comm_args)

    return call


def _make_gdn_bwd(seq, tb):
    ncb = tb // CH
    nb = seq // tb
    hb = tb // 8
    n = ncb * GDN_H

    def body(pg_ref, pre_ref, sm_ref, cw_ref, prm_ref, nw_ref, st_ref, ti_ref, uw_ref, doa_ref,
             dpg_ref, dsm_ref, dcw_ref, dprm_ref, dnw_ref, ds_scr, nxt_scr):
        i = pl.program_id(0)

        @pl.when(i == 0)
        def _():
            ds_scr[...] = jnp.zeros_like(ds_scr)
            nxt_scr[...] = jnp.zeros_like(nxt_scr)
            dcw_ref[...] = jnp.zeros_like(dcw_ref)
            dprm_ref[...] = jnp.zeros_like(dprm_ref)
            dnw_ref[...] = jnp.zeros_like(dnw_ref)

        masks = _masks()
        strict = masks[1]
        sm = sm_ref[...]
        cw = cw_ref[...]
        prm = prm_ref[...]
        raw, pre, act, beta, sp_in, g, gc = _gdn_common(pg_ref, None, sm, cw, prm, tb, pre=pre_ref[...])
        nw = nw_ref[0:1, :]
        row_id = _iota2((CH, 1), 0)
        d = _gdn_batch(act, beta, gc, gc.T, jnp.exp(gc), ncb, masks)
        t = ti_ref[...].reshape(n, CH, CH)
        sol = uw_ref[...].reshape(n, CH, 256)
        u, w = sol[:, :, :128], sol[:, :, 128:]
        s_prev = st_ref[...].reshape(n, 128, 128)
        v_new = u - _bdot(w, s_prev, _NN)
        o = _bdot(d["qg"], s_prev, _NN) + _bdot(d["attn"], v_new, _NN)

        pairs = [(c, h) for c in range(ncb) for h in range(GDN_H)]
        z = jnp.stack([pg_ref[c * CH:(c + 1) * CH, 1536 + h * 128:1536 + (h + 1) * 128] for c, h in pairs])
        doa = jnp.stack([doa_ref[c * CH:(c + 1) * CH, h * 128:(h + 1) * 128] for c, h in pairs])
        on, r, y = _rms_fwd(o, nw, GDN_D)
        dz = doa * y * _dsilu(z)
        do, dnw_rows = _rms_bwd(doa * _silu(z), on, r, nw, GDN_D)
        dnw_acc = jnp.sum(jnp.sum(dnw_rows, axis=0), axis=0, keepdims=True)

        dvn_in = _bdot(d["attn"], do, _TN)
        qgtdo = _bdot(d["qg"], do, _TN)
        dvn_l, dkdec_l, dgl_l = [None] * ncb, [None] * ncb, [None] * ncb
        for c in reversed(range(ncb)):
            bs = slice(c * GDN_H, (c + 1) * GDN_H)
            dsn = ds_scr[...]
            dvn_c = dvn_in[bs] + _bdot(d["kdec"][bs], dsn, _NN)
            ds_scr[...] = d["egl"][bs] * dsn + qgtdo[bs] - _bdot(w[bs], dvn_c, _TN)
            dvn_l[c] = dvn_c
            dkdec_l[c] = _bdot(v_new[bs], dsn, _NT)
            dgl_l[c] = d["egl"][bs] * jnp.sum(_rsum(s_prev[bs] * dsn), axis=1, keepdims=True)
        dvn = jnp.concatenate(dvn_l, axis=0)
        dkdec = jnp.concatenate(dkdec_l, axis=0)
        dglast = jnp.concatenate(dgl_l, axis=0)

        dqg = _bdot(do, s_prev, _NT)
        dattn = _bdot(do, v_new, _NT)
        dw = -_bdot(dvn, s_prev, _NT)
        drhs = _bdot(t, jnp.concatenate([dvn, dw], axis=2), _TN)
        dvb, dkg = drhs[:, :, :128], drhs[:, :, 128:]
        da = jnp.where(strict, -(_bdot(dvb, u, _NT) + _bdot(dkg, w, _NT)), 0.0)
        dp = da * d["decay"]
        dq_m = dattn * d["decay"]
        m = da * d["a"] + dattn * d["attn"]
        upper_tri = jnp.broadcast_to((_iota2((CH, CH), 1) >= _iota2((CH, CH), 0)).astype(BF16), (n, CH, CH))
        dg_in = _rsum(jnp.where(strict, _bdot(upper_tri, m, _NN), 0.0))
        dkb = _bdot(dp, d["kh"], _NN) + dkg * d["eg"]
        kdk_row = _rsum(dkdec * d["kdec"])
        dk = _bdot(dp, d["kb"], _TN) + _bdot(dq_m, d["qh"], _TN) + dkdec * d["kd_scale"] + dkb * d["bh"]
        dq = _bdot(dq_m, d["kh"], _NN) + dqg * d["eg"]
        dglast = dglast + jnp.sum(kdk_row, axis=1, keepdims=True)
        dgcol = (_rsum(dqg * d["qg"]) + _rsum(dkg * d["kg"]) - kdk_row + jnp.where(row_id == CH - 1, dglast, 0.0))
        dbeta = _rsum(dkb * d["kh"]) + _rsum(dvb * d["vh"])
        dn = dq * (GDN_D ** -0.5)
        dact_q = d["rq"] * (dn - d["qn"] * _rsum(dn * d["qn"]))
        dact_k = d["rk"] * (dk - d["kh"] * _rsum(dk * d["kh"]))
        dact_v = dvb * d["bh"]

        def lanes(v, lane0):
            return jnp.concatenate(
                [sum(_put_lane(v[c * GDN_H + h], lane0 + h) for h in range(GDN_H)) for c in range(ncb)], axis=0)

        def tokens(v):
            return jnp.concatenate(
                [jnp.concatenate([v[c * GDN_H + h] for h in range(GDN_H)], axis=1) for c in range(ncb)], axis=0)

        dbeta_all = lanes(dbeta, 0)
        dg = _dot01l(_chunk_tri(tb, upper=True), lanes(dgcol, 4)) + lanes(dg_in, 4)
        neg_ea = -jnp.exp(prm[0:1, :])
        da_raw = dg * neg_ea * _sigmoid(sp_in)
        db_raw = dbeta_all * beta * (1.0 - beta)
        dsm_ref[...] = (da_raw + db_raw).astype(dsm_ref.dtype)
        lane8 = _iota2((8, 128), 1)
        sub8 = _iota2((8, 128), 0)
        dalog = jnp.sum(dg * g, axis=0, keepdims=True)
        ddtb = jnp.sum(da_raw, axis=0, keepdims=True)
        dprm_ref[...] += jnp.where(sub8 == 0, dalog, 0.0) + jnp.where(sub8 == 1, ddtb, 0.0)
        dnw_ref[...] += jnp.where(sub8 == 0, dnw_acc, 0.0)

        dact = jnp.concatenate([tokens(dact_q), tokens(dact_k), tokens(dact_v)], axis=1)
        dpre = dact * _dsilu(pre)
        back = _conv_back(dpre, nxt_scr[...], tb)
        nxt_scr[...] = dpre[0:8, :]
        draw = back[0] * cw[3:4, :] + back[1] * cw[2:3, :] + back[2] * cw[1:2, :] + back[3] * cw[0:1, :]
        dpg_ref[:, 0:1536] = draw.astype(dpg_ref.dtype)
        dpg_ref[:, 1536:2048] = tokens(dz).astype(dpg_ref.dtype)
        sub_c = _iota2((8, 1536), 0)
        dcw_new = jnp.zeros((8, 1536), F32)
        for s_ in range(CONV_W):
            dcw_new = dcw_new + jnp.where(sub_c == 3 - s_, jnp.sum(back[s_] * raw, axis=0, keepdims=True), 0.0)
        dcw_ref[...] += dcw_new

    def call(pg, pre, sm, cw, prm, nw, st, ti, uw, doa, comm=None, comm_args=()):
        rev = lambda i: (nb - 1 - i, 0)
        const = lambda i: (0, 0)
        cx = _exchange_specs(comm)
        return pl.pallas_call(
            _with_exchange(body, comm, 10, 5, nb),
            grid=(nb,),
            in_specs=[
                pl.BlockSpec((tb, 2048), rev),
                pl.BlockSpec((tb, 1536), rev),
                pl.BlockSpec((tb, 128), rev),
                pl.BlockSpec((8, 1536), const),
                pl.BlockSpec((8, 128), const),
                pl.BlockSpec((8, 128), const),
                pl.BlockSpec((ncb, GDN_H, 128, 128), lambda i: (nb - 1 - i, 0, 0, 0)),
                pl.BlockSpec((ncb, GDN_H, CH, CH), lambda i: (nb - 1 - i, 0, 0, 0)),
                pl.BlockSpec((ncb, GDN_H, CH, 256), lambda i: (nb - 1 - i, 0, 0, 0)),
                pl.BlockSpec((tb, 512), rev),
            ] + cx["specs"],
            out_specs=[
                pl.BlockSpec((tb, 2048), rev),
                pl.BlockSpec((tb, 128), rev),
                pl.BlockSpec((8, 1536), const),
                pl.BlockSpec((8, 128), const),
                pl.BlockSpec((8, 128), const),
            ] + cx["specs"],
            out_shape=[
                jax.ShapeDtypeStruct((seq, 2048), BF16),
                jax.ShapeDtypeStruct((seq, 128), BF16),
                jax.ShapeDtypeStruct((8, 1536), F32),
                jax.ShapeDtypeStruct((8, 128), F32),
                jax.ShapeDtypeStruct((8, 128), F32),
            ] + cx["out_shape"],
            scratch_shapes=[pltpu.VMEM((GDN_H, 128, 128), F32), pltpu.VMEM((8, 1536), F32)] + cx["scratch"],
            compiler_params=pltpu.CompilerParams(dimension_semantics=("arbitrary",), vmem_limit_bytes=VMEM_LIMIT,
                                                 has_side_effects=comm is not None),
            name="gdn_bwd" + cx["tag"],
        )(pg, pre, sm, cw, prm, nw, st, ti, uw, doa, *comm_args)

    return call


def _expand_mat():
    r = _iota2((128, SSD_W), 0)
    c = _iota2((128, SSD_W), 1)
    return (jnp.right_shift(c, 6) == r).astype(F32)


def _reduce_heads(v, e):
    vh, vl = _split(v)
    eb = e.astype(BF16)
    nt = (((1,), (1,)), ((), ()))
    return (lax.dot_general(vh, eb, nt, preferred_element_type=F32)
            + lax.dot_general(vl, eb, nt, preferred_element_type=F32))


def _reduce_heads1(v, e):
    nt = (((1,), (1,)), ((), ()))
    return lax.dot_general(v.astype(BF16), e.astype(BF16), nt, preferred_element_type=F32)


def _row8(v):
    return jnp.broadcast_to(v, (8, v.shape[1]))


def _ssd_common(ps_ref, halo8, ss, cw, cb, prm, tb, pre=None):
    raw = ps_ref[:, 0:1536]
    taps = None
    if pre is None:
        taps = _conv_taps(raw, halo8, tb)
        pre = taps[0] * cw[3:4, :] + taps[1] * cw[2:3, :] + taps[2] * cw[1:2, :] + taps[3] * cw[0:1, :] + cb[0:1, :]
    act = _silu(pre)
    dt_in = ss + prm[1:2, :]
    dt = _softplus(dt_in)
    a = dt * (-jnp.exp(prm[0:1, :]))
    acum = _dot01l(_chunk_tri(tb), a)
    e = _expand_mat()
    dt_e = _dot01r(dt, e)
    xdt = act[:, 0:SSD_W] * dt_e
    ea_e = _dot01r(jnp.exp(acum), e)
    d_e = _dot01r(_row8(prm[2:3, :]), e)[0:1, :]
    return raw, taps, pre, act, dt_in, dt, a, acum, e, dt_e, xdt, ea_e, d_e


def _ssd_chunk(act, acum, act_t, e, c):
    r0 = c * CH
    rows = slice(r0, r0 + CH)
    alast = acum[r0 + CH - 1:r0 + CH, :]
    wdec = jnp.exp(alast - acum[rows, :])
    wd_e = _dot01r(wdec, e)
    eal_e = _dot01r(_row8(jnp.exp(alast)), e)[0:1, :]
    return rows, wd_e, eal_e


def _ssd_lmat(acum, act_t, c, h, causal):
    r0 = c * CH
    acol = acum[r0:r0 + CH, h:h + 1]
    arow = act_t[h:h + 1, r0:r0 + CH]
    return jnp.exp(jnp.where(causal, acol - arow, NEG))


def _make_ssd_fwd(seq, tb):
    ncb = tb // CH
    nb = seq // tb
    hg = SSD_H // SSD_G
    gw = SSD_W // SSD_G

    def body(ps_ref, ss_ref, cw_ref, cb_ref, prm_ref, nw_ref, ob_ref, st_ref, pre_ref, y_ref, hs_scr, halo_scr):
        @pl.when(pl.program_id(0) == 0)
        def _():
            hs_scr[...] = jnp.zeros_like(hs_scr)
            halo_scr[...] = jnp.zeros_like(halo_scr)

        causal, _, _ = _masks()
        (raw, _, pre, act, _, _, _, acum, e, _, xdt, ea_e, d_e) = _ssd_common(
            ps_ref, halo_scr[...], ss_ref[...], cw_ref[...], cb_ref[...], prm_ref[...], tb)
        halo_scr[...] = raw[tb - 8:tb, :]
        pre_ref[...] = pre
        act_t = acum.T
        nw = nw_ref[0:1, :]
        for c in range(ncb):
            rows, wd_e, eal_e = _ssd_chunk(act, acum, act_t, e, c)
            st_ref[c] = hs_scr[...]
            ys = []
            for g in range(SSD_G):
                gc_ = slice(g * gw, (g + 1) * gw)
                bg = act[rows, SSD_W + g * 128:SSD_W + (g + 1) * 128]
                cg = act[rows, SSD_W + 256 + g * 128:SSD_W + 256 + (g + 1) * 128]
                cbm = _dot_nt(cg, bg)
                hs = hs_scr[:, gc_]
                yin = _dot(cg, hs)
                yh = []
                for hh in range(hg):
                    h = g * hg + hh
                    lm = _ssd_lmat(acum, act_t, c, h, causal)
                    yh.append(_dot(cbm * lm, xdt[rows, h * SSD_P:(h + 1) * SSD_P]))
                ys.append(jnp.concatenate(yh, axis=1) + yin * ea_e[rows, gc_])
                hs_scr[:, gc_] = hs * eal_e[:, gc_] + _dot_tn(bg, xdt[rows, gc_] * wd_e[:, gc_])
            y = jnp.concatenate(ys, axis=1) + act[rows, 0:SSD_W] * d_e
            y_ref[rows, :] = y
            yz = y * _silu(ps_ref[rows, 1536:2560])
            outs = [_rms_fwd(yz[:, g * gw:(g + 1) * gw], nw[:, g * gw:(g + 1) * gw], gw)[2] for g in range(SSD_G)]
            ob_ref[rows, :] = jnp.concatenate(outs, axis=1).astype(ob_ref.dtype)

    def call(ps, ss, cw, cb, prm, nw):
        const = lambda i: (0, 0)
        return pl.pallas_call(
            body,
            grid=(nb,),
            in_specs=[
                pl.BlockSpec((tb, 2560), lambda i: (i, 0)),
                pl.BlockSpec((tb, 128), lambda i: (i, 0)),
                pl.BlockSpec((8, 1536), const),
                pl.BlockSpec((8, 1536), const),
                pl.BlockSpec((8, 128), const),
                pl.BlockSpec((8, SSD_W), const),
            ],
            out_specs=[
                pl.BlockSpec((tb, SSD_W), lambda i: (i, 0)),
                pl.BlockSpec((ncb, SSD_N, SSD_W), lambda i: (i, 0, 0)),
                pl.BlockSpec((tb, 1536), lambda i: (i, 0)),
                pl.BlockSpec((tb, SSD_W), lambda i: (i, 0)),
            ],
            out_shape=[
                jax.ShapeDtypeStruct((seq, SSD_W), BF16),
                jax.ShapeDtypeStruct((seq // CH, SSD_N, SSD_W), F32),
                jax.ShapeDtypeStruct((seq, 1536), F32),
                jax.ShapeDtypeStruct((seq, SSD_W), F32),
            ],
            scratch_shapes=[pltpu.VMEM((SSD_N, SSD_W), F32), pltpu.VMEM((8, 1536), F32)],
            compiler_params=pltpu.CompilerParams(dimension_semantics=("arbitrary",), vmem_limit_bytes=VMEM_LIMIT),
            name="ssd_fwd",
        )(ps, ss, cw, cb, prm, nw)

    return call


def _make_ssd_bwd(seq, tb):
    ncb = tb // CH
    nb = seq // tb
    hb = tb // 8
    hg = SSD_H // SSD_G
    gw = SSD_W // SSD_G

    def body(ps_ref, pre_ref, y_ref, ss_ref, cw_ref, cb_ref, prm_ref, nw_ref, st_ref, dob_ref,
             dps_ref, dss_ref, dcw_ref, dcb_ref, dprm_ref, dnw_ref, dhs_scr, nxt_scr):
        i = pl.program_id(0)

        @pl.when(i == 0)
        def _():
            dhs_scr[...] = jnp.zeros_like(dhs_scr)
            nxt_scr[...] = jnp.zeros_like(nxt_scr)
            dcw_ref[...] = jnp.zeros_like(dcw_ref)
            dcb_ref[...] = jnp.zeros_like(dcb_ref)
            dprm_ref[...] = jnp.zeros_like(dprm_ref)
            dnw_ref[...] = jnp.zeros_like(dnw_ref)

        causal, _, _ = _masks()
        cw = cw_ref[...]
        prm = prm_ref[...]
        (raw, _, pre, act, dt_in, dt, a, acum, e, dt_e, xdt, ea_e, d_e) = _ssd_common(
            ps_ref, None, ss_ref[...], cw, cb_ref[...], prm, tb, pre=pre_ref[...])
        act_t = acum.T
        nw = nw_ref[0:1, :]

        dx_l, db_l, dc_l, dz_l, ddt_l, da_l = ([None] * ncb for _ in range(6))
        upper_tri = (_iota2((CH, CH), 1) >= _iota2((CH, CH), 0)).astype(F32)
        tri_pair = jnp.concatenate([upper_tri, (_iota2((CH, CH), 1) < _iota2((CH, CH), 0)).astype(F32)], axis=1)
        below = jnp.bitwise_and(_iota2((CH, gw), 1), CH - 1) < _iota2((CH, gw), 0)
        dnw_acc = jnp.zeros((1, SSD_W), F32)
        dd_acc = jnp.zeros((1, SSD_W), F32)

        for c in reversed(range(ncb)):
            rows, wd_e, eal_e = _ssd_chunk(act, acum, act_t, e, c)
            xc = act[rows, 0:SSD_W]
            z = ps_ref[rows, 1536:2560]
            dob = dob_ref[rows, :]
            sz = _silu(z)
            dy_g, dz_g, dxdt_g, db_g, dc_g, da_g = [], [], [], [], [], []
            for g in range(SSD_G):
                gc_ = slice(g * gw, (g + 1) * gw)
                bg = act[rows, SSD_W + g * 128:SSD_W + (g + 1) * 128]
                cg = act[rows, SSD_W + 256 + g * 128:SSD_W + 256 + (g + 1) * 128]
                cbm = _dot_nt(cg, bg)
                hs = st_ref[c, :, gc_]
                yin = _dot(cg, hs)
                lmats = [_ssd_lmat(acum, act_t, c, g * hg + hh, causal) for hh in range(hg)]
                ea_g = ea_e[rows, gc_]
                y = y_ref[rows, gc_]
                yz = y * sz[:, gc_]
                on, r, _ = _rms_fwd(yz, nw[:, gc_], gw)
                dyz, dnw_rows = _rms_bwd(dob[:, gc_], on, r, nw[:, gc_], gw)
                dnw_acc = dnw_acc + _put_cols(jnp.sum(dnw_rows, axis=0, keepdims=True), g, gw)
                dy = dyz * sz[:, gc_]
                dz_g.append(dyz * y * _dsilu(z[:, gc_]))
                dd_acc = dd_acc + _put_cols(jnp.sum(dy * xc[:, gc_], axis=0, keepdims=True), g, gw)
                dhs_n = dhs_scr[:, gc_]
                dyin = dy * ea_g
                dcg = _dot_nt(dyin, hs)
                xw = xdt[rows, gc_] * wd_e[:, gc_]
                dbg = _dot_nt(xw, dhs_n)
                dxw = _dot(bg, dhs_n)
                dhs_scr[:, gc_] = dhs_n * eal_e[:, gc_] + _dot_tn(cg, dyin)
                dxi, ms, dcbm = [], [], jnp.zeros((CH, CH), F32)
                for hh in range(hg):
                    h = g * hg + hh
                    hc = slice(hh * SSD_P, (hh + 1) * SSD_P)
                    dyh = dy[:, hc]
                    lm = cbm * lmats[hh]
                    dxi.append(_dot_tn(lm, dyh))
                    dlm = _dot_nt(dyh, xdt[rows, h * SSD_P:(h + 1) * SSD_P])
                    ms.append(dlm * lm)
                    dcbm = dcbm + dlm * lmats[hh]
                dx_intra = jnp.concatenate(dxi, axis=1)
                ncat = _dot(upper_tri, jnp.concatenate(ms, axis=1))
                cum = _dot(tri_pair, jnp.concatenate([dy * yin * ea_g, dxw * xw], axis=0))
                da_g.append(jnp.where(below, ncat, 0.0) + cum
                            + jnp.sum(hs * dhs_n, axis=0, keepdims=True) * eal_e[:, gc_])
                dxdt_g.append(dx_intra + dxw * wd_e[:, gc_])
                dy_g.append(dy)
                db_g.append(dbg + _dot_tn(dcbm, cg))
                dc_g.append(dcg + _dot(dcbm, bg))
            dy = jnp.concatenate(dy_g, axis=1)
            dxdt = jnp.concatenate(dxdt_g, axis=1)
            dx_l[c] = dxdt * dt_e[rows, :] + dy * d_e
            db_l[c] = jnp.concatenate(db_g, axis=1)
            dc_l[c] = jnp.concatenate(dc_g, axis=1)
            dz_l[c] = jnp.concatenate(dz_g, axis=1)
            ddt_l[c] = _reduce_heads1(dxdt * xc, e)
            da_l[c] = _reduce_heads1(jnp.concatenate(da_g, axis=1), e)

        da = jnp.concatenate(da_l, axis=0)
        neg_ea = -jnp.exp(prm[0:1, :])
        ddt = jnp.concatenate(ddt_l, axis=0) + da * neg_ea
        ddt_in = ddt * _sigmoid(dt_in)
        dss_ref[...] = ddt_in.astype(dss_ref.dtype)
        sub8 = _iota2((8, 128), 0)
        dalog = jnp.sum(da * a, axis=0, keepdims=True)
        ddtb = jnp.sum(ddt_in, axis=0, keepdims=True)
        dd = _reduce_heads(_row8(dd_acc), e)[0:1, :]
        dprm_ref[...] += (jnp.where(sub8 == 0, dalog, 0.0) + jnp.where(sub8 == 1, ddtb, 0.0)
                          + jnp.where(sub8 == 2, dd, 0.0))
        dnw_ref[...] += jnp.where(_iota2((8, SSD_W), 0) == 0, dnw_acc, 0.0)

        dact = jnp.concatenate([jnp.concatenate(dx_l, axis=0), jnp.concatenate(db_l, axis=0),
                                jnp.concatenate(dc_l, axis=0)], axis=1)
        dpre = dact * _dsilu(pre)
        back = _conv_back(dpre, nxt_scr[...], tb)
        nxt_scr[...] = dpre[0:8, :]
        draw = back[0] * cw[3:4, :] + back[1] * cw[2:3, :] + back[2] * cw[1:2, :] + back[3] * cw[0:1, :]
        dps_ref[:, 0:1536] = draw.astype(dps_ref.dtype)
        dps_ref[:, 1536:2560] = jnp.concatenate(dz_l, axis=0).astype(dps_ref.dtype)
        sub_c = _iota2((8, 1536), 0)
        dcw_new = jnp.zeros((8, 1536), F32)
        for s_ in range(CONV_W):
            dcw_new = dcw_new + jnp.where(sub_c == 3 - s_, jnp.sum(back[s_] * raw, axis=0, keepdims=True), 0.0)
        dcw_ref[...] += dcw_new
        dcb_ref[...] += jnp.where(sub_c == 0, jnp.sum(dpre, axis=0, keepdims=True), 0.0)

    def call(ps, pre, y, ss, cw, cb, prm, nw, st, dob, comm=None, comm_args=()):
        rev = lambda i: (nb - 1 - i, 0)
        const = lambda i: (0, 0)
        cx = _exchange_specs(comm)
        return pl.pallas_call(
            _with_exchange(body, comm, 10, 6, nb),
            grid=(nb,),
            in_specs=[
                pl.BlockSpec((tb, 2560), rev),
                pl.BlockSpec((tb, 1536), rev),
                pl.BlockSpec((tb, SSD_W), rev),
                pl.BlockSpec((tb, 128), rev),
                pl.BlockSpec((8, 1536), const),
                pl.BlockSpec((8, 1536), const),
                pl.BlockSpec((8, 128), const),
                pl.BlockSpec((8, SSD_W), const),
                pl.BlockSpec((ncb, SSD_N, SSD_W), lambda i: (nb - 1 - i, 0, 0)),
                pl.BlockSpec((tb, SSD_W), rev),
            ] + cx["specs"],
            out_specs=[
                pl.BlockSpec((tb, 2560), rev),
                pl.BlockSpec((tb, 128), rev),
                pl.BlockSpec((8, 1536), const),
                pl.BlockSpec((8, 1536), const),
                pl.BlockSpec((8, 128), const),
                pl.BlockSpec((8, SSD_W), const),
            ] + cx["specs"],
            out_shape=[
                jax.ShapeDtypeStruct((seq, 2560), BF16),
                jax.ShapeDtypeStruct((seq, 128), BF16),
                jax.ShapeDtypeStruct((8, 1536), F32),
                jax.ShapeDtypeStruct((8, 1536), F32),
                jax.ShapeDtypeStruct((8, 128), F32),
                jax.ShapeDtypeStruct((8, SSD_W), F32),
            ] + cx["out_shape"],
            scratch_shapes=[pltpu.VMEM((SSD_N, SSD_W), F32), pltpu.VMEM((8, 1536), F32)] + cx["scratch"],
            compiler_params=pltpu.CompilerParams(dimension_semantics=("arbitrary",), vmem_limit_bytes=VMEM_LIMIT,
                                                 has_side_effects=comm is not None),
            name="ssd_bwd" + cx["tag"],
        )(ps, pre, y, ss, cw, cb, prm, nw, st, dob, *comm_args)

    return call


def _ret_consts(h):
    lg = math.log(1.0 - 2.0 ** (-5.0 - h))
    r = _iota2((CH, CH), 0)
    c = _iota2((CH, CH), 1)
    rel = (r - c).astype(F32)
    dmat = jnp.where(r >= c, jnp.exp(jnp.maximum(rel, 0.0) * lg), 0.0)
    idx = _iota2((CH, 1), 0).astype(F32)
    qdec = jnp.exp((idx + 1.0) * lg)
    kdec = jnp.exp((CH - 1.0 - idx) * lg)
    cdec = math.exp(CH * lg)
    return dmat, qdec, kdec, cdec


def _ret_batch(pr_ref, cc_ref, ss_ref, ncb):
    pairs = [(c, h) for c in range(ncb) for h in range(RET_H)]

    def st(off):
        return jnp.stack([pr_ref[c * CH:(c + 1) * CH, off + h * 128:off + (h + 1) * 128] for c, h in pairs])

    cc = jnp.stack([cc_ref[c * CH:(c + 1) * CH, :] for c, _ in pairs])
    ss = jnp.stack([ss_ref[c * CH:(c + 1) * CH, :] for c, _ in pairs])
    consts = [_ret_consts(h) for h in range(RET_H)]
    dmat = jnp.stack([consts[h][0] for _, h in pairs])
    qdec = jnp.stack([consts[h][1] for _, h in pairs])
    kdec = jnp.stack([consts[h][2] for _, h in pairs])
    cdec = jnp.stack([jnp.full((1, 1), consts[h][3], F32) for h in range(RET_H)])
    q = _rot(st(0), cc, ss)
    k = _rot(st(512), cc, ss) * (RET_D ** -0.5)
    return dict(q=q, k=k, v=st(1024), z=st(1536), cc=cc, ss=ss, dmat=dmat, qdec=qdec, kdec=kdec, cdec=cdec,
                s=_bdot(q, k, _NT) * dmat)


def _rot(t, cc, ss):
    return t * cc + pltpu.roll(t, 64, axis=t.ndim - 1) * ss


def _rot_bwd(d, cc, ss):
    return d * cc + pltpu.roll(d * ss, 64, axis=d.ndim - 1)


def _make_ret_fwd(seq, tb):
    ncb = tb // CH
    nb = seq // tb

    def body(pr_ref, cc_ref, ss_ref, nw_ref, oc_ref, st_ref, r_scr):
        @pl.when(pl.program_id(0) == 0)
        def _():
            r_scr[...] = jnp.zeros_like(r_scr)

        d = _ret_batch(pr_ref, cc_ref, ss_ref, ncb)
        kd = d["k"] * d["kdec"]
        for c in range(ncb):
            bs = slice(c * RET_H, (c + 1) * RET_H)
            rs = r_scr[...]
            st_ref[c] = rs
            r_scr[...] = rs * d["cdec"] + _bdot(kd[bs], d["v"][bs], _TN)
        r_prev = st_ref[...].reshape(ncb * RET_H, 128, 128)
        o = _bdot(d["s"], d["v"], _NN) + _bdot(d["q"], r_prev, _NN) * d["qdec"]
        _, _, y = _rms_fwd(o, nw_ref[0:1, :], RET_D)
        out = y * _silu(d["z"])
        for c in range(ncb):
            for h in range(RET_H):
                oc_ref[c * CH:(c + 1) * CH, h * 128:(h + 1) * 128] = out[c * RET_H + h].astype(oc_ref.dtype)

    def call(pr, cc, ss, nw):
        return pl.pallas_call(
            body,
            grid=(nb,),
            in_specs=[
                pl.BlockSpec((tb, 2048), lambda i: (i, 0)),
                pl.BlockSpec((tb, 128), lambda i: (i, 0)),
                pl.BlockSpec((tb, 128), lambda i: (i, 0)),
                pl.BlockSpec((8, 128), lambda i: (0, 0)),
            ],
            out_specs=[
                pl.BlockSpec((tb, 512), lambda i: (i, 0)),
                pl.BlockSpec((ncb, RET_H, 128, 128), lambda i: (i, 0, 0, 0)),
            ],
            out_shape=[
                jax.ShapeDtypeStruct((seq, 512), BF16),
                jax.ShapeDtypeStruct((seq // CH, RET_H, 128, 128), F32),
            ],
            scratch_shapes=[pltpu.VMEM((RET_H, 128, 128), F32)],
            compiler_params=pltpu.CompilerParams(dimension_semantics=("arbitrary",), vmem_limit_bytes=VMEM_LIMIT),
            name="ret_fwd",
        )(pr, cc, ss, nw)

    return call


def _make_ret_bwd(seq, tb):
    ncb = tb // CH
    nb = seq // tb

    def body(pr_ref, cc_ref, ss_ref, nw_ref, st_ref, doc_ref, dpr_ref, dnw_ref, dr_scr):
        @pl.when(pl.program_id(0) == 0)
        def _():
            dr_scr[...] = jnp.zeros_like(dr_scr)
            dnw_ref[...] = jnp.zeros_like(dnw_ref)

        nw = nw_ref[0:1, :]
        scale = RET_D ** -0.5
        n = ncb * RET_H
        d = _ret_batch(pr_ref, cc_ref, ss_ref, ncb)
        q, k, v, z, s = d["q"], d["k"], d["v"], d["z"], d["s"]
        r_prev = st_ref[...].reshape(n, 128, 128)
        o = _bdot(s, v, _NN) + _bdot(q, r_prev, _NN) * d["qdec"]
        doc = jnp.stack([doc_ref[c * CH:(c + 1) * CH, h * 128:(h + 1) * 128]
                         for c in range(ncb) for h in range(RET_H)])
        on, r, y = _rms_fwd(o, nw, RET_D)
        dz = doc * y * _dsilu(z)
        do, dnw_rows = _rms_bwd(doc * _silu(z), on, r, nw, RET_D)
        dnw_acc = jnp.sum(jnp.sum(dnw_rows, axis=0), axis=0, keepdims=True)
        dqd = do * d["qdec"]
        qtd = _bdot(q, dqd, _TN)
        drn_l = [None] * ncb
        for c in reversed(range(ncb)):
            drn_l[c] = dr_scr[...]
            dr_scr[...] = qtd[c * RET_H:(c + 1) * RET_H] + d["cdec"] * drn_l[c]
        drn = jnp.concatenate(drn_l, axis=0)
        ds = _bdot(do, v, _NT) * d["dmat"]
        dq = _rot_bwd(_bdot(ds, k, _NN) + _bdot(dqd, r_prev, _NT), d["cc"], d["ss"])
        dk = _rot_bwd((_bdot(ds, q, _TN) + _bdot(v, drn, _NT) * d["kdec"]) * scale, d["cc"], d["ss"])
        dv = _bdot(s, do, _TN) + _bdot(k * d["kdec"], drn, _NN)
        for c in range(ncb):
            rows = slice(c * CH, (c + 1) * CH)
            for h in range(RET_H):
                b = c * RET_H + h
                for j, val in enumerate((dq, dk, dv, dz)):
                    dpr_ref[rows, j * 512 + h * 128:j * 512 + (h + 1) * 128] = val[b].astype(dpr_ref.dtype)
        dnw_ref[...] += jnp.where(_iota2((8, 128), 0) == 0, dnw_acc, 0.0)

    def call(pr, cc, ss, nw, st, doc):
        rev = lambda i: (nb - 1 - i, 0)
        return pl.pallas_call(
            body,
            grid=(nb,),
            in_specs=[
                pl.BlockSpec((tb, 2048), rev),
                pl.BlockSpec((tb, 128), rev),
                pl.BlockSpec((tb, 128), rev),
                pl.BlockSpec((8, 128), lambda i: (0, 0)),
                pl.BlockSpec((ncb, RET_H, 128, 128), lambda i: (nb - 1 - i, 0, 0, 0)),
                pl.BlockSpec((tb, 512), rev),
            ],
            out_specs=[
                pl.BlockSpec((tb, 2048), rev),
                pl.BlockSpec((8, 128), lambda i: (0, 0)),
            ],
            out_shape=[
                jax.ShapeDtypeStruct((seq, 2048), BF16),
                jax.ShapeDtypeStruct((8, 128), F32),
            ],
            scratch_shapes=[pltpu.VMEM((RET_H, 128, 128), F32)],
            compiler_params=pltpu.CompilerParams(dimension_semantics=("arbitrary",), vmem_limit_bytes=VMEM_LIMIT),
            name="ret_bwd",
        )(pr, cc, ss, nw, st, doc)

    return call


def _rope_tables(seq):
    half = RET_D // 2
    inv = ROPE_BASE ** (-jnp.arange(half, dtype=F32) / half)
    hi = (CH * jnp.arange(seq // CH, dtype=jnp.int32)).astype(F32)[:, None] * inv[None, :]
    lo = jnp.arange(CH, dtype=jnp.int32).astype(F32)[:, None] * inv[None, :]
    ch, sh, cl, sl = jnp.cos(hi)[:, None, :], jnp.sin(hi)[:, None, :], jnp.cos(lo)[None], jnp.sin(lo)[None]
    cos = (ch * cl - sh * sl).reshape(seq, half)
    sin = (sh * cl + ch * sl).reshape(seq, half)
    return jnp.concatenate([cos, cos], axis=1), jnp.concatenate([-sin, sin], axis=1)


SEG_G, SEG_S, SEG_R, SEG_GS, SEG_SS = (0, 2048), (2048, 4608), (4608, 6656), (6656, 6784), (6784, 6912)
NP = 6912
SEGS = (SEG_G, SEG_S, SEG_R, SEG_GS, SEG_SS)


def _resident(shape):
    return pl.BlockSpec(shape, lambda i: (0,) * len(shape), pipeline_mode=pl.Buffered(1))


def _make_inproj(seq, tl):
    def body(x_ref, pn_ref, w_ref, pg_ref, ps_ref, pr_ref, gs_ref, ss_ref, ht_ref):
        x = x_ref[...]
        _, _, hn = _rms_fwd(x, pn_ref[0:1, :], D_MODEL)
        h = hn.astype(BF16)
        ht_ref[...] = hn.T.astype(BF16)
        for (a, b), o_ref in zip(SEGS, (pg_ref, ps_ref, pr_ref, gs_ref, ss_ref)):
            o_ref[...] = jnp.dot(h, w_ref[:, a:b], preferred_element_type=F32)

    def call(x, pn, w, comm=None, comm_args=()):
        row = lambda i: (i, 0)
        cx = _exchange_specs(comm)
        return pl.pallas_call(
            _with_exchange(body, comm, 3, 6, seq // tl),
            grid=(seq // tl,),
            in_specs=[pl.BlockSpec((tl, D_MODEL), row), _resident((8, D_MODEL)), _resident((D_MODEL, NP))]
            + cx["specs"],
            out_specs=[pl.BlockSpec((tl, b - a), row) for a, b in SEGS]
            + [pl.BlockSpec((D_MODEL, tl), lambda i: (0, i))] + cx["specs"],
            out_shape=[jax.ShapeDtypeStruct((seq, b - a), F32) for a, b in SEGS]
            + [jax.ShapeDtypeStruct((D_MODEL, seq), BF16)] + cx["out_shape"],
            scratch_shapes=cx["scratch"],
            compiler_params=pltpu.CompilerParams(dimension_semantics=("arbitrary",), vmem_limit_bytes=VMEM_LIMIT,
                                                 has_side_effects=comm is not None),
            name="inproj" + cx["tag"],
        )(x, pn, w, *comm_args)

    return call


def _make_outproj(seq, tl):
    def body(oa_ref, ob_ref, oc_ref, w_ref, x_ref, qn_ref, out_ref, xn_ref):
        out = (jnp.dot(oa_ref[...], w_ref[0:512, :], preferred_element_type=F32)
               + jnp.dot(ob_ref[...], w_ref[512:1536, :], preferred_element_type=F32)
               + jnp.dot(oc_ref[...], w_ref[1536:2048, :], preferred_element_type=F32))
        out_ref[...] = out
        _, _, y = _rms_fwd(out, qn_ref[0:1, :], D_MODEL)
        xn_ref[...] = x_ref[...] + y

    def call(oa, ob, oc, w, x, qn):
        row = lambda i: (i, 0)
        return pl.pallas_call(
            body,
            grid=(seq // tl,),
            in_specs=[pl.BlockSpec((tl, 512), row), pl.BlockSpec((tl, 1024), row), pl.BlockSpec((tl, 512), row),
                      _resident((2048, D_MODEL)), pl.BlockSpec((tl, D_MODEL), row), _resident((8, D_MODEL))],
            out_specs=[pl.BlockSpec((tl, D_MODEL), row), pl.BlockSpec((tl, D_MODEL), row)],
            out_shape=[jax.ShapeDtypeStruct((seq, D_MODEL), F32), jax.ShapeDtypeStruct((seq, D_MODEL), F32)],
            compiler_params=pltpu.CompilerParams(dimension_semantics=("arbitrary",), vmem_limit_bytes=VMEM_LIMIT),
            name="outproj",
        )(oa, ob, oc, w, x, qn)

    return call


def _make_outproj_loss(seq, tl):
    def body(oa_ref, ob_ref, oc_ref, w_ref, x_ref, qn_ref, t_ref, out_ref, dy_ref, loss_ref):
        @pl.when(pl.program_id(0) == 0)
        def _():
            loss_ref[...] = jnp.zeros_like(loss_ref)

        out = (jnp.dot(oa_ref[...], w_ref[0:512, :], preferred_element_type=F32)
               + jnp.dot(ob_ref[...], w_ref[512:1536, :], preferred_element_type=F32)
               + jnp.dot(oc_ref[...], w_ref[1536:2048, :], preferred_element_type=F32))
        out_ref[...] = out
        _, _, y = _rms_fwd(out, qn_ref[0:1, :], D_MODEL)
        err = (x_ref[...] + y) - t_ref[...]
        dy_ref[...] = err * (1.0 / D_MODEL)
        part = jnp.sum(jnp.sum(err * err, axis=1, keepdims=True), axis=0, keepdims=True) * (0.5 / D_MODEL)
        loss_ref[...] += jnp.where((_iota2((8, 128), 0) == 0) & (_iota2((8, 128), 1) == 0), part, 0.0)

    def call(oa, ob, oc, w, x, qn, t):
        row = lambda i: (i, 0)
        return pl.pallas_call(
            body,
            grid=(seq // tl,),
            in_specs=[pl.BlockSpec((tl, 512), row), pl.BlockSpec((tl, 1024), row), pl.BlockSpec((tl, 512), row),
                      _resident((2048, D_MODEL)), pl.BlockSpec((tl, D_MODEL), row), _resident((8, D_MODEL)),
                      pl.BlockSpec((tl, D_MODEL), row)],
            out_specs=[pl.BlockSpec((tl, D_MODEL), row), pl.BlockSpec((tl, D_MODEL), row),
                       pl.BlockSpec((8, 128), lambda i: (0, 0))],
            out_shape=[jax.ShapeDtypeStruct((seq, D_MODEL), F32), jax.ShapeDtypeStruct((seq, D_MODEL), F32),
                       jax.ShapeDtypeStruct((8, 128), F32)],
            compiler_params=pltpu.CompilerParams(dimension_semantics=("arbitrary",), vmem_limit_bytes=VMEM_LIMIT),
            name="outproj_loss",
        )(oa, ob, oc, w, x, qn, t)

    return call


def _make_outproj_bwd(seq, tl):
    def body(dxn_ref, out_ref, oa_ref, ob_ref, oc_ref, w_ref, qn_ref,
             doa_ref, dob_ref, doc_ref, dqn_ref, dw_ref, dwb_ref):
        @pl.when(pl.program_id(0) == 0)
        def _():
            dqn_ref[...] = jnp.zeros_like(dqn_ref)
            dw_ref[...] = jnp.zeros_like(dw_ref)

        qn = qn_ref[0:1, :]
        on, r, _ = _rms_fwd(out_ref[...], qn, D_MODEL)
        dout, dqn_rows = _rms_bwd(dxn_ref[...], on, r, qn, D_MODEL)
        dqn_ref[...] += jnp.where(_iota2((8, D_MODEL), 0) == 0, jnp.sum(dqn_rows, axis=0, keepdims=True), 0.0)
        db = dout.astype(BF16)
        nt = (((1,), (1,)), ((), ()))
        tn = (((0,), (0,)), ((), ()))
        doa_ref[...] = lax.dot_general(db, w_ref[0:512, :], nt, preferred_element_type=F32).astype(BF16)
        dob_ref[...] = lax.dot_general(db, w_ref[512:1536, :], nt, preferred_element_type=F32).astype(BF16)
        doc_ref[...] = lax.dot_general(db, w_ref[1536:2048, :], nt, preferred_element_type=F32).astype(BF16)
        dw_ref[0:512, :] += lax.dot_general(oa_ref[...], db, tn, preferred_element_type=F32)
        dw_ref[512:1536, :] += lax.dot_general(ob_ref[...], db, tn, preferred_element_type=F32)
        dw_ref[1536:2048, :] += lax.dot_general(oc_ref[...], db, tn, preferred_element_type=F32)

        @pl.when(pl.program_id(0) == seq // tl - 1)
        def _():
            dwb_ref[...] = dw_ref[...].astype(BF16)

    def call(dxn, out, oa, ob, oc, w, qn):
        row = lambda i: (i, 0)
        const = lambda i: (0, 0)
        return pl.pallas_call(
            body,
            grid=(seq // tl,),
            in_specs=[pl.BlockSpec((tl, D_MODEL), row), pl.BlockSpec((tl, D_MODEL), row),
                      pl.BlockSpec((tl, 512), row), pl.BlockSpec((tl, 1024), row), pl.BlockSpec((tl, 512), row),
                      _resident((2048, D_MODEL)), _resident((8, D_MODEL))],
            out_specs=[pl.BlockSpec((tl, 512), row), pl.BlockSpec((tl, 1024), row), pl.BlockSpec((tl, 512), row),
                       pl.BlockSpec((8, D_MODEL), const), pl.BlockSpec((2048, D_MODEL), const),
                       pl.BlockSpec((2048, D_MODEL), const)],
            out_shape=[jax.ShapeDtypeStruct((seq, 512), BF16), jax.ShapeDtypeStruct((seq, 1024), BF16),
                       jax.ShapeDtypeStruct((seq, 512), BF16), jax.ShapeDtypeStruct((8, D_MODEL), F32),
                       jax.ShapeDtypeStruct((2048, D_MODEL), F32), jax.ShapeDtypeStruct((2048, D_MODEL), BF16)],
            compiler_params=pltpu.CompilerParams(dimension_semantics=("arbitrary",), vmem_limit_bytes=VMEM_LIMIT),
            name="outproj_bwd",
        )(dxn, out, oa, ob, oc, w, qn)

    return call


def _make_inproj_bwd_dx(seq, tl):
    def body(dg_ref, ds_ref, dr_ref, dgs_ref, dss_ref, w_ref, x_ref, pn_ref, dxn_ref, dx_ref, dpn_ref):
        @pl.when(pl.program_id(0) == 0)
        def _():
            dpn_ref[...] = jnp.zeros_like(dpn_ref)

        nt = (((1,), (1,)), ((), ()))
        dh = jnp.zeros((tl, D_MODEL), F32)
        for (a, b), d_ref in zip(SEGS, (dg_ref, ds_ref, dr_ref, dgs_ref, dss_ref)):
            dh = dh + lax.dot_general(d_ref[...], w_ref[:, a:b], nt, preferred_element_type=F32)
        pn = pn_ref[0:1, :]
        on, r, _ = _rms_fwd(x_ref[...], pn, D_MODEL)
        dx, dpn_rows = _rms_bwd(dh, on, r, pn, D_MODEL)
        dx_ref[...] = dx + dxn_ref[...]
        dpn_ref[...] += jnp.where(_iota2((8, D_MODEL), 0) == 0, jnp.sum(dpn_rows, axis=0, keepdims=True), 0.0)

    def call(dg, ds, dr, dgs, dss, w, x, pn, dxn, comm=None, comm_args=()):
        row = lambda i: (i, 0)
        cx = _exchange_specs(comm)
        return pl.pallas_call(
            _with_exchange(body, comm, 9, 2, seq // tl),
            grid=(seq // tl,),
            in_specs=[pl.BlockSpec((tl, b - a), row) for a, b in SEGS]
            + [_resident((D_MODEL, NP)), pl.BlockSpec((tl, D_MODEL), row), _resident((8, D_MODEL)),
               pl.BlockSpec((tl, D_MODEL), row)] + cx["specs"],
            out_specs=[pl.BlockSpec((tl, D_MODEL), row), pl.BlockSpec((8, D_MODEL), lambda i: (0, 0))] + cx["specs"],
            out_shape=[jax.ShapeDtypeStruct((seq, D_MODEL), F32), jax.ShapeDtypeStruct((8, D_MODEL), F32)]
            + cx["out_shape"],
            scratch_shapes=cx["scratch"],
            compiler_params=pltpu.CompilerParams(dimension_semantics=("arbitrary",), vmem_limit_bytes=VMEM_LIMIT,
                                                 has_side_effects=comm is not None),
            name="inproj_bwd_dx" + cx["tag"],
        )(dg, ds, dr, dgs, dss, w, x, pn, dxn, *comm_args)

    return call


def _make_inproj_bwd_dw_small(seq, tl, name):
    def body(ht_ref, a_ref, b_ref, dw_ref):
        @pl.when(pl.program_id(0) == 0)
        def _():
            dw_ref[...] = jnp.zeros_like(dw_ref)

        ht = ht_ref[...]
        dw_ref[:, 0:128] += jnp.dot(ht, a_ref[...], preferred_element_type=F32)
        dw_ref[:, 128:256] += jnp.dot(ht, b_ref[...], preferred_element_type=F32)

    def call(ht, a, b):
        return pl.pallas_call(
            body,
            grid=(seq // tl,),
            in_specs=[pl.BlockSpec((D_MODEL, tl), lambda i: (0, i)), pl.BlockSpec((tl, 128), lambda i: (i, 0)),
                      pl.BlockSpec((tl, 128), lambda i: (i, 0))],
            out_specs=pl.BlockSpec((D_MODEL, 256), lambda i: (0, 0)),
            out_shape=jax.ShapeDtypeStruct((D_MODEL, 256), F32),
            compiler_params=pltpu.CompilerParams(dimension_semantics=("arbitrary",), vmem_limit_bytes=VMEM_LIMIT),
            name=name,
        )(ht, a, b)

    return call


def _make_inproj_bwd_dw(seq, tl, width, tn, name):
    def body(ht_ref, d_ref, dw_ref):
        @pl.when(pl.program_id(1) == 0)
        def _():
            dw_ref[...] = jnp.zeros_like(dw_ref)

        dw_ref[...] += jnp.dot(ht_ref[...], d_ref[...], preferred_element_type=F32)

    def call(ht, d):
        return pl.pallas_call(
            body,
            grid=(width // tn, seq // tl),
            in_specs=[pl.BlockSpec((D_MODEL, tl), lambda j, i: (0, i)), pl.BlockSpec((tl, tn), lambda j, i: (i, j))],
            out_specs=pl.BlockSpec((D_MODEL, tn), lambda j, i: (0, j)),
            out_shape=jax.ShapeDtypeStruct((D_MODEL, width), F32),
            compiler_params=pltpu.CompilerParams(dimension_semantics=("arbitrary", "arbitrary"),
                                                 vmem_limit_bytes=VMEM_LIMIT),
            name=name,
        )(ht, d)

    return call


ADAM_LR, ADAM_B1, ADAM_B2, ADAM_EPS, ADAM_WD, ADAM_STEP = 0.001, 0.9, 0.999, 1e-08, 0.01, 10


def _adam_math(w, g, m, v):
    m = ADAM_B1 * m + (1.0 - ADAM_B1) * g
    v = ADAM_B2 * v + (1.0 - ADAM_B2) * (g * g)
    m_hat = m / (1.0 - ADAM_B1 ** ADAM_STEP)
    v_hat = v / (1.0 - ADAM_B2 ** ADAM_STEP)
    delta = -ADAM_LR * (m_hat / (jnp.sqrt(v_hat) + ADAM_EPS) + ADAM_WD * w)
    return delta, m, v


def _adamw(w, g, m, v, name):
    shape = w.shape
    cols = shape[-1]
    rows = w.size // cols
    tr = rows if rows <= 512 else 256
    assert rows % tr == 0

    def body(w_ref, g_ref, m_ref, v_ref, d_ref, mo_ref, vo_ref):
        d_ref[...], mo_ref[...], vo_ref[...] = _adam_math(w_ref[...], g_ref[...], m_ref[...], v_ref[...])

    spec = pl.BlockSpec((tr, cols), lambda i: (i, 0))
    outs = pl.pallas_call(
        body,
        grid=(rows // tr,),
        in_specs=[spec] * 4,
        out_specs=[spec] * 3,
        out_shape=[jax.ShapeDtypeStruct((rows, cols), F32)] * 3,
        compiler_params=pltpu.CompilerParams(dimension_semantics=("arbitrary",), vmem_limit_bytes=VMEM_LIMIT),
        name=name,
    )(*[a.reshape(rows, cols) for a in (w, g, m, v)])
    return (g,) + tuple(o.reshape(shape) for o in outs)


def _adamw_pairs(w, mine, theirs, m, v, name):
    na, r, cols = w.shape
    assert na == 2
    tr = 256
    assert r % tr == 0

    def body(w_ref, a0_ref, b0_ref, a1_ref, b1_ref, m_ref, v_ref, g_ref, d_ref, mo_ref, vo_ref):
        g = jnp.where(pl.program_id(0) == 0, a0_ref[...] + b0_ref[...], a1_ref[...] + b1_ref[...])
        g_ref[...] = g
        d_ref[...], mo_ref[...], vo_ref[...] = _adam_math(w_ref[...], g, m_ref[...], v_ref[...])

    nblk = r // tr
    full = pl.BlockSpec((None, tr, cols), lambda a, i: (a, i, 0))
    lay0 = pl.BlockSpec((None, tr, cols), lambda a, i: (0, i * (1 - a) + (nblk - 1) * a, 0))
    lay1 = pl.BlockSpec((None, tr, cols), lambda a, i: (0, i * a, 0))
    return pl.pallas_call(
        body,
        grid=(na, nblk),
        in_specs=[full, lay0, lay0, lay1, lay1, full, full],
        out_specs=[full] * 4,
        out_shape=[jax.ShapeDtypeStruct(w.shape, F32)] * 4,
        compiler_params=pltpu.CompilerParams(dimension_semantics=("arbitrary",) * 2, vmem_limit_bytes=VMEM_LIMIT),
        name=name,
    )(w, mine[0], theirs[0], mine[1], theirs[1], m, v)


MESH = pl.DeviceIdType.MESH
ANY = pl.BlockSpec(memory_space=pl.ANY)
CHIP_REL = ((1, 0), (0, 1), (1, 1))


def _flip(v, d):
    return 1 - v if d else v


class _ChipExchange:
    def __init__(self, kind, arrs, swap=()):
        self.kind, self.n_chip, self.n = kind, len(arrs), len(arrs) + len(swap)
        if kind == "gather":
            self.out_shape = [jax.ShapeDtypeStruct((4,) + a.shape, a.dtype) for a in arrs]
        else:
            self.out_shape = [jax.ShapeDtypeStruct((3,) + a.shape[1:], a.dtype) for a in arrs]
        self.out_shape += [jax.ShapeDtypeStruct(a.shape, a.dtype) for a in swap]
        self.scratch = [pltpu.SemaphoreType.DMA((4 * self.n,)), pltpu.SemaphoreType.DMA((4 * self.n,))]

    def _copies(self, ins, outs, sems):
        send_sems, recv_sems = sems
        x, y, c = lax.axis_index("x"), lax.axis_index("y"), lax.axis_index("c")
        me = 2 * x + y
        pairs = []
        for a in range(self.n_chip, self.n):
            cp = pltpu.make_async_remote_copy(
                src_ref=ins[a], dst_ref=outs[a], send_sem=send_sems.at[4 * a], recv_sem=recv_sems.at[4 * a],
                device_id=(x, y, 1 - c), device_id_type=MESH)
            pairs.append((cp, cp))
        for a in range(self.n_chip):
            for k, (dx, dy) in enumerate(CHIP_REL):
                px, py = _flip(x, dx), _flip(y, dy)
                sem = dict(send_sem=send_sems.at[4 * a + k], recv_sem=recv_sems.at[4 * a + k],
                           device_id=(px, py, c), device_id_type=MESH)
                if self.kind == "gather":
                    out = pltpu.make_async_remote_copy(src_ref=ins[a], dst_ref=outs[a].at[me], **sem)
                    inc = pltpu.make_async_remote_copy(src_ref=ins[a], dst_ref=outs[a].at[2 * px + py], **sem)
                else:
                    out = pltpu.make_async_remote_copy(src_ref=ins[a].at[2 * px + py], dst_ref=outs[a].at[k], **sem)
                    inc = out
                pairs.append((out, inc))
            if self.kind == "gather":
                own = pltpu.make_async_remote_copy(
                    src_ref=ins[a], dst_ref=outs[a].at[me], send_sem=send_sems.at[4 * a + 3],
                    recv_sem=recv_sems.at[4 * a + 3], device_id=(x, y, 1 - c), device_id_type=MESH)
                pairs.append((own, own))
        return pairs

    def start(self, ins, outs, sems):
        for out, _ in self._copies(ins, outs, sems):
            out.start()

    def finish(self, ins, outs, sems):
        pairs = self._copies(ins, outs, sems)
        for _, inc in pairs:
            inc.wait_recv()
        for out, _ in pairs:
            out.wait_send()


def _with_exchange(body, comm, n_in, n_out, nb):
    if comm is None:
        return body

    def wrapped(*refs):
        ins = refs[:n_in]
        c_in = refs[n_in:n_in + comm.n]
        outs = refs[n_in + comm.n:n_in + comm.n + n_out]
        c_out = refs[n_in + comm.n + n_out:n_in + 2 * comm.n + n_out]
        rest = refs[n_in + 2 * comm.n + n_out:]
        scratch, sems = rest[:len(rest) - 2], rest[len(rest) - 2:]

        @pl.when(pl.program_id(0) == 0)
        def _():
            comm.start(c_in, c_out, sems)

        body(*ins, *outs, *scratch)

        @pl.when(pl.program_id(0) == nb - 1)
        def _():
            comm.finish(c_in, c_out, sems)

    return wrapped


def _exchange_specs(comm):
    if comm is None:
        return dict(specs=[], out_shape=[], scratch=[], tag="")
    return dict(specs=[pl.BlockSpec(memory_space=pl.ANY)] * comm.n, out_shape=list(comm.out_shape),
                scratch=list(comm.scratch), tag="_" + comm.kind)


def _half(ref_or_shape, half):
    r = ref_or_shape[-2] // 2
    return pl.ds(half * r, r)


def _ag_rows(arrs, name):
    n = len(arrs)

    def body(*refs):
        ins, outs = refs[:n], refs[n:2 * n]
        send_sems, recv_sems, fsend_sems, frecv_sems, loc_sems = refs[2 * n:]
        x, y, c = lax.axis_index("x"), lax.axis_index("y"), lax.axis_index("c")
        me = 2 * x + y
        sib = (x, y, 1 - c)

        def chip_of(k):
            dx, dy = CHIP_REL[k]
            return _flip(x, dx), _flip(y, dy)

        def ici(a, k, slot):
            px, py = chip_of(k)
            rows = _half(arrs[a].shape, c)
            return pltpu.make_async_remote_copy(
                src_ref=ins[a].at[:, rows, :], dst_ref=outs[a].at[slot, :, rows, :], send_sem=send_sems.at[a * 3 + k],
                recv_sem=recv_sems.at[a * 3 + k], device_id=(px, py, c), device_id_type=MESH)

        def fwd(a, k, half):
            px, py = chip_of(k)
            blk = outs[a].at[2 * px + py, :, _half(arrs[a].shape, half), :]
            return pltpu.make_async_remote_copy(
                src_ref=blk, dst_ref=blk, send_sem=fsend_sems.at[a * 3 + k], recv_sem=frecv_sems.at[a * 3 + k],
                device_id=sib, device_id_type=MESH)

        own = [pltpu.make_async_remote_copy(src_ref=ins[a], dst_ref=outs[a].at[me], send_sem=loc_sems.at[a],
                                            recv_sem=loc_sems.at[n + a], device_id=sib, device_id_type=MESH)
               for a in range(n)]
        for cp in own:
            cp.start()
        for a in range(n):
            for k in range(3):
                ici(a, k, me).start()
        for a in range(n):
            for k in range(3):
                px, py = chip_of(k)
                ici(a, k, 2 * px + py).wait_recv()
                fwd(a, k, c).start()
        for a in range(n):
            for k in range(3):
                fwd(a, k, 1 - c).wait_recv()
        for a in range(n):
            for k in range(3):
                ici(a, k, me).wait_send()
                fwd(a, k, c).wait_send()
        for cp in own:
            cp.wait()

    return pl.pallas_call(
        body,
        in_specs=[ANY] * n,
        out_specs=[ANY] * n,
        out_shape=[jax.ShapeDtypeStruct((4,) + a.shape, a.dtype) for a in arrs],
        scratch_shapes=[pltpu.SemaphoreType.DMA((3 * n,)) for _ in range(4)] + [pltpu.SemaphoreType.DMA((2 * n,))],
        compiler_params=pltpu.CompilerParams(has_side_effects=True),
        name=name,
    )(*arrs)


def _sum_chips(own, recv, chip, name):
    _, na, r, cols = own.shape
    tr = 256
    assert r % tr == 0

    def body(chip_ref, o_ref, r_ref, s_ref):
        s_ref[...] = ((o_ref[...] + r_ref[0].astype(F32)) + r_ref[1].astype(F32)) + r_ref[2].astype(F32)

    return pl.pallas_call(
        body,
        grid_spec=pltpu.PrefetchScalarGridSpec(
            num_scalar_prefetch=1,
            grid=(na, r // tr),
            in_specs=[pl.BlockSpec((None, None, tr, cols), lambda a, i, ch: (ch[0], a, i, 0)),
                      pl.BlockSpec((3, None, tr, cols), lambda a, i, ch: (0, a, i, 0))],
            out_specs=pl.BlockSpec((None, tr, cols), lambda a, i, ch: (a, i, 0))),
        out_shape=jax.ShapeDtypeStruct((na, r, cols), F32),
        compiler_params=pltpu.CompilerParams(dimension_semantics=("arbitrary",) * 2, vmem_limit_bytes=VMEM_LIMIT),
        name=name,
    )(chip, own, recv)


def _swap_sibling(arrs, name):
    n = len(arrs)

    def body(*refs):
        ins, outs = refs[:n], refs[n:2 * n]
        send_sems, recv_sems = refs[2 * n:]
        x, y, c = lax.axis_index("x"), lax.axis_index("y"), lax.axis_index("c")
        cps = [pltpu.make_async_remote_copy(src_ref=ins[a], dst_ref=outs[a], send_sem=send_sems.at[a],
                                            recv_sem=recv_sems.at[a], device_id=(x, y, 1 - c), device_id_type=MESH)
               for a in range(n)]
        for cp in cps:
            cp.start()
        for cp in cps:
            cp.wait_recv()
        for cp in cps:
            cp.wait_send()

    return pl.pallas_call(
        body,
        in_specs=[ANY] * n,
        out_specs=[ANY] * n,
        out_shape=[jax.ShapeDtypeStruct(a.shape, a.dtype) for a in arrs],
        scratch_shapes=[pltpu.SemaphoreType.DMA((n,)), pltpu.SemaphoreType.DMA((n,))],
        compiler_params=pltpu.CompilerParams(has_side_effects=True),
        name=name,
    )(*arrs)


def _allreduce_small(vec, name):
    rows = vec.shape[0]

    def body(v_ref, out_ref, gat_ref, send_sems, recv_sems):
        x, y, c = lax.axis_index("x"), lax.axis_index("y"), lax.axis_index("c")
        me = 4 * x + 2 * y + c

        def remote(k, slot):
            dx, dy, dc = (k >> 2) & 1, (k >> 1) & 1, k & 1
            return pltpu.make_async_remote_copy(
                src_ref=v_ref, dst_ref=gat_ref.at[slot], send_sem=send_sems.at[k - 1], recv_sem=recv_sems.at[k - 1],
                device_id=(_flip(x, dx), _flip(y, dy), _flip(c, dc)), device_id_type=MESH)

        gat_ref[me] = v_ref[...]
        for k in range(1, 8):
            remote(k, me).start()
        for k in range(1, 8):
            dx, dy, dc = (k >> 2) & 1, (k >> 1) & 1, k & 1
            remote(k, 4 * _flip(x, dx) + 2 * _flip(y, dy) + _flip(c, dc)).wait_recv()
        for k in range(1, 8):
            remote(k, me).wait_send()
        acc = gat_ref[0]
        for j in range(1, 8):
            acc = acc + gat_ref[j]
        out_ref[...] = acc

    vm = pl.BlockSpec(memory_space=pltpu.VMEM)
    return pl.pallas_call(
        body,
        in_specs=[vm],
        out_specs=vm,
        out_shape=jax.ShapeDtypeStruct(vec.shape, F32),
        scratch_shapes=[pltpu.VMEM((8, rows, 128), F32), pltpu.SemaphoreType.DMA((7,)), pltpu.SemaphoreType.DMA((7,))],
        compiler_params=pltpu.CompilerParams(has_side_effects=True),
        name=name,
    )(vec)


def _pad8(v, width, lane0=0):
    v = v.reshape(1, -1) if v.ndim == 1 else v
    return jnp.zeros((8, width), F32).at[:v.shape[0], lane0:lane0 + v.shape[1]].set(v.astype(F32))


def _relayout_w_in(g):
    tr = 128
    q = N_IN // 4

    def body(g_ref, o_ref):
        w = jnp.concatenate([g_ref[j] for j in range(4)], axis=1)
        z = lambda n: jnp.zeros((tr, n), w.dtype)
        o_ref[...] = jnp.concatenate([w[:, 0:2048], w[:, 2056:4616], w[:, 4632:6680],
                                      w[:, 2048:2056], z(120), w[:, 4616:4632], z(112)], axis=1)

    return pl.pallas_call(
        body,
        grid=(D_MODEL // tr,),
        in_specs=[pl.BlockSpec((4, tr, q), lambda i: (0, i, 0))],
        out_specs=pl.BlockSpec((tr, NP), lambda i: (i, 0)),
        out_shape=jax.ShapeDtypeStruct((D_MODEL, NP), g.dtype),
        compiler_params=pltpu.CompilerParams(dimension_semantics=("arbitrary",), vmem_limit_bytes=VMEM_LIMIT),
        name="relayout_w_in",
    )(g)


def _unlayout_dw_in(dg, ds, dr, dsm):
    tr = 128
    q = N_IN // 4

    def body(g_ref, s_ref, r_ref, sm_ref, o_ref, ob_ref):
        w = jnp.concatenate([g_ref[...], sm_ref[:, 0:8], s_ref[...], sm_ref[:, 128:144], r_ref[...]], axis=1)
        for j in range(4):
            blk = w[:, q * j:q * (j + 1)]
            o_ref[j] = blk
            ob_ref[j] = blk.astype(BF16)

    row = lambda i: (i, 0)
    return pl.pallas_call(
        body,
        grid=(D_MODEL // tr,),
        in_specs=[pl.BlockSpec((tr, d.shape[1]), row) for d in (dg, ds, dr, dsm)],
        out_specs=[pl.BlockSpec((4, tr, q), lambda i: (0, i, 0))] * 2,
        out_shape=[jax.ShapeDtypeStruct((4, D_MODEL, q), F32), jax.ShapeDtypeStruct((4, D_MODEL, q), BF16)],
        compiler_params=pltpu.CompilerParams(dimension_semantics=("arbitrary",), vmem_limit_bytes=VMEM_LIMIT),
        name="unlayout_dw_in",
    )(dg, ds, dr, dsm)


TB = 256
TB_RET = 512
TL = 1024
TL_IN = 512
TL_OB = 1024
TK = 2048


def kernel(x, pre_norm, post_norm, w_in, gdn_conv, gdn_A_log, gdn_dt_bias, gdn_norm, ssd_conv, ssd_conv_b, ssd_A_log, ssd_dt_bias, ssd_D, ssd_norm, ret_norm, w_out, loss_target, m_pre_norm, m_post_norm, m_w_in, m_gdn_conv, m_gdn_A_log, m_gdn_dt_bias, m_gdn_norm, m_ssd_conv, m_ssd_conv_b, m_ssd_A_log, m_ssd_dt_bias, m_ssd_D, m_ssd_norm, m_ret_norm, m_w_out, v_pre_norm, v_post_norm, v_w_in, v_gdn_conv, v_gdn_A_log, v_gdn_dt_bias, v_gdn_norm, v_ssd_conv, v_ssd_conv_b, v_ssd_A_log, v_ssd_dt_bias, v_ssd_D, v_ssd_norm, v_ret_norm, v_w_out):
    seq = x.shape[1]
    chip = 2 * lax.axis_index("x") + lax.axis_index("y")
    x0 = x[0]

    wi_b, wo_b = w_in.astype(BF16), w_out.astype(BF16)
    (wi0_g,) = _ag_rows([wi_b[0:1]], "ag_weights")
    full_w_in = _relayout_w_in
    wp = [full_w_in(wi0_g[:, 0]), None]
    ag0 = _ChipExchange("gather", [wo_b[0], wo_b[1], gdn_conv, ssd_conv])
    ag1 = _ChipExchange("gather", [wi_b[1]])
    rope_c, rope_s = _rope_tables(seq)

    saved = []
    xc = x0
    for l in range(DEPTH):
        p = dict(
            pn=_pad8(pre_norm[l], D_MODEL), qn=_pad8(post_norm[l], D_MODEL),
            g_prm=_pad8(jnp.stack([gdn_A_log[l], gdn_dt_bias[l]]), 128, 4), g_nw=_pad8(gdn_norm[l], 128),
            s_cb=_pad8(ssd_conv_b[l], 1536),
            s_prm=_pad8(jnp.stack([ssd_A_log[l], ssd_dt_bias[l], ssd_D[l]]), 128), s_nw=_pad8(ssd_norm[l], SSD_W),
            r_nw=_pad8(ret_norm[l], 128))
        if l == 0:
            pg, ps, pr, gs, ss, ht, wo0_g, wo1_g, gcv_g, scv_g = _make_inproj(seq, TL_IN)(
                xc, p["pn"], wp[l], comm=ag0, comm_args=(wo_b[0], wo_b[1], gdn_conv, ssd_conv))
            wo = [wo0_g.reshape(2048, D_MODEL), wo1_g.reshape(2048, D_MODEL)]
            gcv = jnp.transpose(gcv_g, (1, 2, 0, 3)).reshape(DEPTH, CONV_W, 1536)
            scv = jnp.transpose(scv_g, (1, 2, 0, 3)).reshape(DEPTH, CONV_W, 1536)
        else:
            pg, ps, pr, gs, ss, ht = _make_inproj(seq, TL_IN)(xc, p["pn"], wp[l])
        p.update(g_cw=_pad8(gcv[l], 1536), s_cw=_pad8(scv[l], 1536))
        if l == 0:
            oa, stg, tig, uwg, gpre, wi1_g = _make_gdn_fwd(seq, TB)(
                pg, gs, p["g_cw"], p["g_prm"], p["g_nw"], comm=ag1, comm_args=(wi_b[1],))
            wp[1] = full_w_in(wi1_g)
        else:
            oa, stg, tig, uwg, gpre = _make_gdn_fwd(seq, TB)(pg, gs, p["g_cw"], p["g_prm"], p["g_nw"])
        ob, sts, spre, sy = _make_ssd_fwd(seq, TB)(ps, ss, p["s_cw"], p["s_cb"], p["s_prm"], p["s_nw"])
        oc, str_ = _make_ret_fwd(seq, TB_RET)(pr, rope_c, rope_s, p["r_nw"])
        if l == DEPTH - 1:
            out, dxn, lossp = _make_outproj_loss(seq, TL)(oa, ob, oc, wo[l], xc, p["qn"], loss_target[0])
            xn = None
        else:
            out, xn = _make_outproj(seq, TL)(oa, ob, oc, wo[l], xc, p["qn"])
        saved.append(dict(p=p, x=xc, ht=ht, spre=spre, sy=sy, gpre=gpre, pg=pg, ps=ps, pr=pr, gs=gs, ss=ss, stg=stg, tig=tig, uwg=uwg, sts=sts, str=str_,
                          oa=oa, ob=ob, oc=oc, out=out))
        xc = xn

    small = [None] * DEPTH
    gin, gin_b, gout, gout_b, q_in, q_out, s_in, s_out, t_in, t_out = ([None] * DEPTH for _ in range(10))
    chip1 = chip.astype(jnp.int32).reshape(1)

    def sum_chips(l):
        return (_sum_chips(gin[l][:, None], q_in[l][:, None], chip1, f"sum_chips_w_in{l}"),
                _sum_chips(gout[l][:, None], q_out[l][:, None], chip1, f"sum_chips_w_out{l}"))

    for l in reversed(range(DEPTH)):
        s = saved[l]
        p = s["p"]
        doa, dob, doc, dqn, dwo_l, dwo_b = _make_outproj_bwd(seq, TL_OB)(
            dxn, s["out"], s["oa"], s["ob"], s["oc"], wo[l], p["qn"])
        gout[l], gout_b[l] = dwo_l.reshape(4, 512, D_MODEL), dwo_b.reshape(4, 512, D_MODEL)
        gdn_args = (s["pg"], s["gpre"], s["gs"], p["g_cw"], p["g_prm"], p["g_nw"], s["stg"], s["tig"], s["uwg"], doa)
        if l == 0:
            payload = (gout_b[0],)
            dpg, dgs, dcw_g, dprm_g, dnw_g, q_out[0] = _make_gdn_bwd(seq, TB)(
                *gdn_args, comm=_ChipExchange("scatter", payload), comm_args=payload)
        else:
            dpg, dgs, dcw_g, dprm_g, dnw_g = _make_gdn_bwd(seq, TB)(*gdn_args)
        ssd_args = (s["ps"], s["spre"], s["sy"], s["ss"], p["s_cw"], p["s_cb"], p["s_prm"], p["s_nw"], s["sts"], dob)
        if l == 0:
            payload = (gin_b[1], gout_b[1])
            dps, dss, dcw_s, dcb_s, dprm_s, dnw_s, q_in[1], q_out[1] = _make_ssd_bwd(seq, TB)(
                *ssd_args, comm=_ChipExchange("scatter", payload), comm_args=payload)
        else:
            dps, dss, dcw_s, dcb_s, dprm_s, dnw_s = _make_ssd_bwd(seq, TB)(*ssd_args)
        dpr, dnw_r = _make_ret_bwd(seq, TB_RET)(s["pr"], rope_c, rope_s, p["r_nw"], s["str"], doc)
        dws = [_make_inproj_bwd_dw(seq, TK, d.shape[1], tn, f"inproj_bwd_dw{i}")(s["ht"], d)
               for i, (d, tn) in enumerate(((dpg, 2048), (dps, 1280), (dpr, 2048)))]
        dws.append(_make_inproj_bwd_dw_small(seq, TK, "inproj_bwd_dw3")(s["ht"], dgs, dss))
        gin[l], gin_b[l] = _unlayout_dw_in(*dws)
        dx_args = (dpg, dps, dpr, dgs, dss, wp[l], s["x"], p["pn"], dxn)
        if l == 0:
            s_in[1], s_out[1] = sum_chips(1)
            payload, swap = (gin_b[0],), (s_in[1], s_out[1])
            dx, dpn, q_in[0], t_in[1], t_out[1] = _make_inproj_bwd_dx(seq, TL_IN)(
                *dx_args, comm=_ChipExchange("scatter", payload, swap), comm_args=payload + swap)
        else:
            dx, dpn = _make_inproj_bwd_dx(seq, TL_IN)(*dx_args)
        small[l] = [dpn[0], dqn[0], dcw_g[0:4].reshape(-1), dprm_g[0, 4:8], dprm_g[1, 4:8], dnw_g[0],
                    dcw_s[0:4].reshape(-1), dcb_s[0], dprm_s[0, 0:16], dprm_s[1, 0:16], dprm_s[2, 0:16],
                    dnw_s[0], dnw_r[0]]
        dxn = dx
    grad_x = dxn[None]

    sizes = [a.shape[0] for a in small[0]]
    flat = jnp.concatenate(small[0] + small[1] + [lossp[0, 0:1]])
    n_flat = flat.shape[0]
    rows = -(-n_flat // 1024) * 8
    red = _allreduce_small(jnp.pad(flat, (0, rows * 128 - n_flat)).reshape(rows, 128), "allreduce_small").reshape(-1)
    per = sum(sizes)
    loss = red[2 * per]

    def pick(i):
        off = sum(sizes[:i])
        return jnp.stack([red[l * per + off:l * per + off + sizes[i]] for l in range(DEPTH)])

    g_small = dict(
        pre_norm=pick(0), post_norm=pick(1),
        gdn_conv=lax.dynamic_slice_in_dim(pick(2).reshape(DEPTH, CONV_W, 1536), chip * 384, 384, axis=2),
        gdn_A_log=pick(3), gdn_dt_bias=pick(4), gdn_norm=pick(5),
        ssd_conv=lax.dynamic_slice_in_dim(pick(6).reshape(DEPTH, CONV_W, 1536), chip * 384, 384, axis=2),
        ssd_conv_b=pick(7), ssd_A_log=pick(8), ssd_dt_bias=pick(9), ssd_D=pick(10), ssd_norm=pick(11),
        ret_norm=pick(12))

    s_in[0], s_out[0] = sum_chips(0)
    t_in[0], t_out[0] = _swap_sibling([s_in[0], s_out[0]], "swap_grads")

    weights = dict(pre_norm=pre_norm, post_norm=post_norm, w_in=w_in, gdn_conv=gdn_conv, gdn_A_log=gdn_A_log,
                   gdn_dt_bias=gdn_dt_bias, gdn_norm=gdn_norm, ssd_conv=ssd_conv, ssd_conv_b=ssd_conv_b,
                   ssd_A_log=ssd_A_log, ssd_dt_bias=ssd_dt_bias, ssd_D=ssd_D, ssd_norm=ssd_norm, ret_norm=ret_norm,
                   w_out=w_out)
    ms = dict(pre_norm=m_pre_norm, post_norm=m_post_norm, w_in=m_w_in, gdn_conv=m_gdn_conv, gdn_A_log=m_gdn_A_log,
              gdn_dt_bias=m_gdn_dt_bias, gdn_norm=m_gdn_norm, ssd_conv=m_ssd_conv, ssd_conv_b=m_ssd_conv_b,
              ssd_A_log=m_ssd_A_log, ssd_dt_bias=m_ssd_dt_bias, ssd_D=m_ssd_D, ssd_norm=m_ssd_norm,
              ret_norm=m_ret_norm, w_out=m_w_out)
    vs = dict(pre_norm=v_pre_norm, post_norm=v_post_norm, w_in=v_w_in, gdn_conv=v_gdn_conv, gdn_A_log=v_gdn_A_log,
              gdn_dt_bias=v_gdn_dt_bias, gdn_norm=v_gdn_norm, ssd_conv=v_ssd_conv, ssd_conv_b=v_ssd_conv_b,
              ssd_A_log=v_ssd_A_log, ssd_dt_bias=v_ssd_dt_bias, ssd_D=v_ssd_D, ssd_norm=v_ssd_norm,
              ret_norm=v_ret_norm, w_out=v_w_out)
    names = list(weights)
    res = {}
    for nme in names:
        if nme == "w_in":
            res[nme] = _adamw_pairs(w_in, s_in, t_in, m_w_in, v_w_in, "adamw_w_in")
        elif nme == "w_out":
            res[nme] = _adamw_pairs(w_out, s_out, t_out, m_w_out, v_w_out, "adamw_w_out")
        else:
            res[nme] = _adamw(weights[nme], g_small[nme], ms[nme], vs[nme], "adamw_" + nme)
    return (loss, grad_x, *[res[n][0] for n in names], *[res[n][1] for n in names],
            *[res[n][2] for n in names], *[res[n][3] for n in names])
```

```python
import math

import jax
import jax.numpy as jnp
from jax import lax
from jax.experimental import pallas as pl
from jax.experimental.pallas import tpu as pltpu

F32 = jnp.float32
BF16 = jnp.bfloat16

D_MODEL = 1024
DEPTH = 2
CH = 64
CONV_W = 4
EPS = 1e-6
GDN_H, GDN_D = 4, 128
SSD_H, SSD_P, SSD_N, SSD_G = 16, 64, 128, 2
SSD_W = SSD_H * SSD_P
RET_H, RET_D = 4, 128
ROPE_BASE = 10000.0
N_IN = 6680
NEG = -1e30

V7X_VMEM_BYTES = 64 * 1024 * 1024
VMEM_LIMIT = V7X_VMEM_BYTES * 7 // 8


def _dot(a, b):
    return jnp.dot(a.astype(BF16), b.astype(BF16), preferred_element_type=F32)


def _dot_nt(a, b):
    return lax.dot_general(a.astype(BF16), b.astype(BF16), (((1,), (1,)), ((), ())), preferred_element_type=F32)


def _dot_tn(a, b):
    return lax.dot_general(a.astype(BF16), b.astype(BF16), (((0,), (0,)), ((), ())), preferred_element_type=F32)


def _split(a):
    hi = a.astype(BF16)
    return hi, (a - hi.astype(F32)).astype(BF16)


def _dot01l(m, v):
    vh, vl = _split(v)
    mb = m.astype(BF16)
    return jnp.dot(mb, vh, preferred_element_type=F32) + jnp.dot(mb, vl, preferred_element_type=F32)


def _dot01r(v, m):
    return jnp.dot(v.astype(BF16), m.astype(BF16), preferred_element_type=F32)


def _sigmoid(x):
    return jax.nn.sigmoid(x)


def _silu(x):
    return x * _sigmoid(x)


def _dsilu(x):
    s = _sigmoid(x)
    return s * (1.0 + x * (1.0 - s))


def _softplus(x):
    return jnp.maximum(x, 0.0) + jnp.log1p(jnp.exp(-jnp.abs(x)))


def _iota2(shape, dim):
    return lax.broadcasted_iota(jnp.int32, shape, dim)


def _chunk_tri(tb, upper=False):
    r = _iota2((tb, tb), 0)
    c = _iota2((tb, tb), 1)
    same = jnp.right_shift(r, 6) == jnp.right_shift(c, 6)
    return (same & ((c >= r) if upper else (c <= r))).astype(F32)


def _masks():
    r = _iota2((CH, CH), 0)
    c = _iota2((CH, CH), 1)
    return r >= c, r > c, (r == c).astype(F32)


def _put_lane(col, lane_idx, width=128):
    lane = _iota2((col.shape[0], width), 1)
    return jnp.where(lane == lane_idx, col, 0.0)


def _conv_taps(raw, halo8, tb):
    ext = jnp.concatenate([halo8, raw], axis=0)
    return [raw] + [pltpu.roll(ext, s, axis=0)[8:] for s in (1, 2, 3)]


def _conv_back(dpre, nxt8, tb):
    ext = jnp.concatenate([dpre, nxt8], axis=0)
    return [dpre] + [pltpu.roll(ext, tb + 8 - s, axis=0)[:tb] for s in (1, 2, 3)]


def _rms_fwd(o, w, n):
    r = lax.rsqrt(jnp.sum(o * o, axis=-1, keepdims=True) * (1.0 / n) + EPS)
    on = o * r
    return on, r, on * w


def _rms_bwd(dy, on, r, w, n):
    don = dy * w
    return r * (don - on * (jnp.sum(don * on, axis=-1, keepdims=True) * (1.0 / n))), dy * on


def _put_cols(v, g, gw):
    z = jnp.zeros_like(v)
    return jnp.concatenate([v, z] if g == 0 else [z, v], axis=1)


def _gdn_common(pg_ref, halo8, sm, cw, prm, tb, pre=None):
    raw = pg_ref[:, 0:1536]
    if pre is None:
        taps = _conv_taps(raw, halo8, tb)
        pre = taps[0] * cw[3:4, :] + taps[1] * cw[2:3, :] + taps[2] * cw[1:2, :] + taps[3] * cw[0:1, :]
    act = _silu(pre)
    beta = _sigmoid(sm)
    sp_in = sm + prm[1:2, :]
    g = -jnp.exp(prm[0:1, :]) * _softplus(sp_in)
    gc = _dot01l(_chunk_tri(tb), g)
    return raw, pre, act, beta, sp_in, g, gc


_NN = (((2,), (1,)), ((0,), (0,)))
_NT = (((2,), (2,)), ((0,), (0,)))
_TN = (((1,), (1,)), ((0,), (0,)))


def _bdot(a, b, dn):
    return lax.dot_general(a.astype(BF16), b.astype(BF16), dn, preferred_element_type=F32)


def _binv_unit_lower(a, eye):
    r = _iota2((CH, CH), 0)
    c = _iota2((CH, CH), 1)
    d = eye - jnp.where((jnp.right_shift(r, 1) == jnp.right_shift(c, 1)), a, 0.0)
    ab = a.astype(BF16)
    zero = jnp.zeros((), BF16)
    for lb in range(1, 6):
        same = jnp.right_shift(r, lb + 1) == jnp.right_shift(c, lb + 1)
        low = (jnp.bitwise_and(jnp.right_shift(r, lb), 1) == 1) & (jnp.bitwise_and(jnp.right_shift(c, lb), 1) == 0)
        db = d.astype(BF16)
        t = _bdot(jnp.where(same & low, ab, zero), db, _NN)
        d = d - _bdot(db, t, _NN)
    return d


def _rsum(v):
    return jnp.sum(v, axis=-1, keepdims=True)


def _gdn_batch(act, beta, gc, gct, eg_all, ncb, masks):
    causal, strict, _ = masks

    def st(fn):
        return jnp.stack([fn(c, h, slice(c * CH, (c + 1) * CH)) for c in range(ncb) for h in range(GDN_H)])

    qr = st(lambda c, h, r: act[r, h * 128:(h + 1) * 128])
    kr = st(lambda c, h, r: act[r, 512 + h * 128:512 + (h + 1) * 128])
    vh = st(lambda c, h, r: act[r, 1024 + h * 128:1024 + (h + 1) * 128])
    bh = st(lambda c, h, r: beta[r, h:h + 1])
    gcol = st(lambda c, h, r: gc[r, 4 + h:5 + h])
    grow = st(lambda c, h, r: gct[4 + h:5 + h, r])
    eg = st(lambda c, h, r: eg_all[r, 4 + h:5 + h])
    glast = st(lambda c, h, r: gc[(c + 1) * CH - 1:(c + 1) * CH, 4 + h:5 + h])
    rq = lax.rsqrt(_rsum(qr * qr) + EPS)
    rk = lax.rsqrt(_rsum(kr * kr) + EPS)
    qn = qr * rq
    kh = kr * rk
    qh = qn * (GDN_D ** -0.5)
    decay = jnp.exp(jnp.where(causal, gcol - grow, NEG))
    kb = kh * bh
    kd_scale = jnp.exp(glast - gcol)
    return dict(qn=qn, rq=rq, kh=kh, rk=rk, qh=qh, vh=vh, bh=bh, eg=eg, decay=decay, kb=kb, vb=vh * bh, kg=kb * eg,
                qg=qh * eg, kd_scale=kd_scale, kdec=kh * kd_scale, egl=jnp.exp(glast),
                a=jnp.where(strict, _bdot(kb, kh, _NT) * decay, 0.0), attn=_bdot(qh, kh, _NT) * decay)


def _make_gdn_fwd(seq, tb):
    ncb = tb // CH
    nb = seq // tb
    n = ncb * GDN_H

    def body(pg_ref, sm_ref, cw_ref, prm_ref, nw_ref, oa_ref, st_ref, ti_ref, uw_ref, pre_ref, s_scr, halo_scr):
        @pl.when(pl.program_id(0) == 0)
        def _():
            s_scr[...] = jnp.zeros_like(s_scr)
            halo_scr[...] = jnp.zeros_like(halo_scr)

        masks = _masks()
        sm = sm_ref[...]
        raw, pre, act, beta, _, _, gc = _gdn_common(pg_ref, halo_scr[...], sm, cw_ref[...], prm_ref[...], tb)
        halo_scr[...] = raw[tb - 8:tb, :]
        pre_ref[...] = pre
        d = _gdn_batch(act, beta, gc, gc.T, jnp.exp(gc), ncb, masks)
        t = _binv_unit_lower(d["a"], masks[2])
        sol = _bdot(t, jnp.concatenate([d["vb"], d["kg"]], axis=2), _NN)
        ti_ref[...] = t.reshape(ncb, GDN_H, CH, CH)
        uw_ref[...] = sol.reshape(ncb, GDN_H, CH, 256)
        u, w = sol[:, :, :128], sol[:, :, 128:]
        vns = []
        for c in range(ncb):
            bs = slice(c * GDN_H, (c + 1) * GDN_H)
            s = s_scr[...]
            st_ref[c] = s
            vn = u[bs] - _bdot(w[bs], s, _NN)
            s_scr[...] = s * d["egl"][bs] + _bdot(d["kdec"][bs], vn, _TN)
            vns.append(vn)
        v_new = jnp.concatenate(vns, axis=0)
        s_prev = st_ref[...].reshape(n, 128, 128)
        o = _bdot(d["qg"], s_prev, _NN) + _bdot(d["attn"], v_new, _NN)
        _, _, y = _rms_fwd(o, nw_ref[0:1, :], GDN_D)
        for c in range(ncb):
            rows = slice(c * CH, (c + 1) * CH)
            for h in range(GDN_H):
                z = pg_ref[rows, 1536 + h * 128:1536 + (h + 1) * 128]
                oa_ref[rows, h * 128:(h + 1) * 128] = (y[c * GDN_H + h] * _silu(z)).astype(oa_ref.dtype)

    def call(pg, sm, cw, prm, nw, comm=None, comm_args=()):
        blk4 = lambda i: (i, 0, 0, 0)
        cx = _exchange_specs(comm)
        return pl.pallas_call(
            _with_exchange(body, comm, 5, 5, nb),
            grid=(nb,),
            in_specs=[
                pl.BlockSpec((tb, 2048), lambda i: (i, 0)),
                pl.BlockSpec((tb, 128), lambda i: (i, 0)),
                pl.BlockSpec((8, 1536), lambda i: (0, 0)),
                pl.BlockSpec((8, 128), lambda i: (0, 0)),
                pl.BlockSpec((8, 128), lambda i: (0, 0)),
            ] + cx["specs"],
            out_specs=[
                pl.BlockSpec((tb, 512), lambda i: (i, 0)),
                pl.BlockSpec((ncb, GDN_H, 128, 128), blk4),
                pl.BlockSpec((ncb, GDN_H, CH, CH), blk4),
                pl.BlockSpec((ncb, GDN_H, CH, 256), blk4),
                pl.BlockSpec((tb, 1536), lambda i: (i, 0)),
            ] + cx["specs"],
            out_shape=[
                jax.ShapeDtypeStruct((seq, 512), BF16),
                jax.ShapeDtypeStruct((seq // CH, GDN_H, 128, 128), F32),
                jax.ShapeDtypeStruct((seq // CH, GDN_H, CH, CH), F32),
                jax.ShapeDtypeStruct((seq // CH, GDN_H, CH, 256), F32),
                jax.ShapeDtypeStruct((seq, 1536), F32),
            ] + cx["out_shape"],
            scratch_shapes=[pltpu.VMEM((GDN_H, 128, 128), F32), pltpu.VMEM((8, 1536), F32)] + cx["scratch"],
            compiler_params=pltpu.CompilerParams(dimension_semantics=("arbitrary",), vmem_limit_bytes=VMEM_LIMIT,
                                                 has_side_effects=comm is not None),
            name="gdn_fwd" + cx["tag"],
        )(pg, sm, cw, prm, nw, *comm_args)

    return call


def _make_gdn_bwd(seq, tb):
    ncb = tb // CH
    nb = seq // tb
    hb = tb // 8
    n = ncb * GDN_H

    def body(pg_ref, pre_ref, sm_ref, cw_ref, prm_ref, nw_ref, st_ref, ti_ref, uw_ref, doa_ref,
             dpg_ref, dsm_ref, dcw_ref, dprm_ref, dnw_ref, ds_scr, nxt_scr):
        i = pl.program_id(0)

        @pl.when(i == 0)
        def _():
            ds_scr[...] = jnp.zeros_like(ds_scr)
            nxt_scr[...] = jnp.zeros_like(nxt_scr)
            dcw_ref[...] = jnp.zeros_like(dcw_ref)
            dprm_ref[...] = jnp.zeros_like(dprm_ref)
            dnw_ref[...] = jnp.zeros_like(dnw_ref)

        masks = _masks()
        strict = masks[1]
        sm = sm_ref[...]
        cw = cw_ref[...]
        prm = prm_ref[...]
        raw, pre, act, beta, sp_in, g, gc = _gdn_common(pg_ref, None, sm, cw, prm, tb, pre=pre_ref[...])
        nw = nw_ref[0:1, :]
        row_id = _iota2((CH, 1), 0)
        d = _gdn_batch(act, beta, gc, gc.T, jnp.exp(gc), ncb, masks)
        t = ti_ref[...].reshape(n, CH, CH)
        sol = uw_ref[...].reshape(n, CH, 256)
        u, w = sol[:, :, :128], sol[:, :, 128:]
        s_prev = st_ref[...].reshape(n, 128, 128)
        v_new = u - _bdot(w, s_prev, _NN)
        o = _bdot(d["qg"], s_prev, _NN) + _bdot(d["attn"], v_new, _NN)

        pairs = [(c, h) for c in range(ncb) for h in range(GDN_H)]
        z = jnp.stack([pg_ref[c * CH:(c + 1) * CH, 1536 + h * 128:1536 + (h + 1) * 128] for c, h in pairs])
        doa = jnp.stack([doa_ref[c * CH:(c + 1) * CH, h * 128:(h + 1) * 128] for c, h in pairs])
        on, r, y = _rms_fwd(o, nw, GDN_D)
        dz = doa * y * _dsilu(z)
        do, dnw_rows = _rms_bwd(doa * _silu(z), on, r, nw, GDN_D)
        dnw_acc = jnp.sum(jnp.sum(dnw_rows, axis=0), axis=0, keepdims=True)

        dvn_in = _bdot(d["attn"], do, _TN)
        qgtdo = _bdot(d["qg"], do, _TN)
        dvn_l, dkdec_l, dgl_l = [None] * ncb, [None] * ncb, [None] * ncb
        for c in reversed(range(ncb)):
            bs = slice(c * GDN_H, (c + 1) * GDN_H)
            dsn = ds_scr[...]
            dvn_c = dvn_in[bs] + _bdot(d["kdec"][bs], dsn, _NN)
            ds_scr[...] = d["egl"][bs] * dsn + qgtdo[bs] - _bdot(w[bs], dvn_c, _TN)
            dvn_l[c] = dvn_c
            dkdec_l[c] = _bdot(v_new[bs], dsn, _NT)
            dgl_l[c] = d["egl"][bs] * jnp.sum(_rsum(s_prev[bs] * dsn), axis=1, keepdims=True)
        dvn = jnp.concatenate(dvn_l, axis=0)
        dkdec = jnp.concatenate(dkdec_l, axis=0)
        dglast = jnp.concatenate(dgl_l, axis=0)

        dqg = _bdot(do, s_prev, _NT)
        dattn = _bdot(do, v_new, _NT)
        dw = -_bdot(dvn, s_prev, _NT)
        drhs = _bdot(t, jnp.concatenate([dvn, dw], axis=2), _TN)
        dvb, dkg = drhs[:, :, :128], drhs[:, :, 128:]
        da = jnp.where(strict, -(_bdot(dvb, u, _NT) + _bdot(dkg, w, _NT)), 0.0)
        dp = da * d["decay"]
        dq_m = dattn * d["decay"]
        m = da * d["a"] + dattn * d["attn"]
        upper_tri = jnp.broadcast_to((_iota2((CH, CH), 1) >= _iota2((CH, CH), 0)).astype(BF16), (n, CH, CH))
        dg_in = _rsum(jnp.where(strict, _bdot(upper_tri, m, _NN), 0.0))
        dkb = _bdot(dp, d["kh"], _NN) + dkg * d["eg"]
        kdk_row = _rsum(dkdec * d["kdec"])
        dk = _bdot(dp, d["kb"], _TN) + _bdot(dq_m, d["qh"], _TN) + dkdec * d["kd_scale"] + dkb * d["bh"]
        dq = _bdot(dq_m, d["kh"], _NN) + dqg * d["eg"]
        dglast = dglast + jnp.sum(kdk_row, axis=1, keepdims=True)
        dgcol = (_rsum(dqg * d["qg"]) + _rsum(dkg * d["kg"]) - kdk_row + jnp.where(row_id == CH - 1, dglast, 0.0))
        dbeta = _rsum(dkb * d["kh"]) + _rsum(dvb * d["vh"])
        dn = dq * (GDN_D ** -0.5)
        dact_q = d["rq"] * (dn - d["qn"] * _rsum(dn * d["qn"]))
        dact_k = d["rk"] * (dk - d["kh"] * _rsum(dk * d["kh"]))
        dact_v = dvb * d["bh"]

        def lanes(v, lane0):
            return jnp.concatenate(
                [sum(_put_lane(v[c * GDN_H + h], lane0 + h) for h in range(GDN_H)) for c in range(ncb)], axis=0)

        def tokens(v):
            return jnp.concatenate(
                [jnp.concatenate([v[c * GDN_H + h] for h in range(GDN_H)], axis=1) for c in range(ncb)], axis=0)

        dbeta_all = lanes(dbeta, 0)
        dg = _dot01l(_chunk_tri(tb, upper=True), lanes(dgcol, 4)) + lanes(dg_in, 4)
        neg_ea = -jnp.exp(prm[0:1, :])
        da_raw = dg * neg_ea * _sigmoid(sp_in)
        db_raw = dbeta_all * beta * (1.0 - beta)
        dsm_ref[...] = (da_raw + db_raw).astype(dsm_ref.dtype)
        lane8 = _iota2((8, 128), 1)
        sub8 = _iota2((8, 128), 0)
        dalog = jnp.sum(dg * g, axis=0, keepdims=True)
        ddtb = jnp.sum(da_raw, axis=0, keepdims=True)
        dprm_ref[...] += jnp.where(sub8 == 0, dalog, 0.0) + jnp.where(sub8 == 1, ddtb, 0.0)
        dnw_ref[...] += jnp.where(sub8 == 0, dnw_acc, 0.0)

        dact = jnp.concatenate([tokens(dact_q), tokens(dact_k), tokens(dact_v)], axis=1)
        dpre = dact * _dsilu(pre)
        back = _conv_back(dpre, nxt_scr[...], tb)
        nxt_scr[...] = dpre[0:8, :]
        draw = back[0] * cw[3:4, :] + back[1] * cw[2:3, :] + back[2] * cw[1:2, :] + back[3] * cw[0:1, :]
        dpg_ref[:, 0:1536] = draw.astype(dpg_ref.dtype)
        dpg_ref[:, 1536:2048] = tokens(dz).astype(dpg_ref.dtype)
        sub_c = _iota2((8, 1536), 0)
        dcw_new = jnp.zeros((8, 1536), F32)
        for s_ in range(CONV_W):
            dcw_new = dcw_new + jnp.where(sub_c == 3 - s_, jnp.sum(back[s_] * raw, axis=0, keepdims=True), 0.0)
        dcw_ref[...] += dcw_new

    def call(pg, pre, sm, cw, prm, nw, st, ti, uw, doa, comm=None, comm_args=()):
        rev = lambda i: (nb - 1 - i, 0)
        const = lambda i: (0, 0)
        cx = _exchange_specs(comm)
        return pl.pallas_call(
            _with_exchange(body, comm, 10, 5, nb),
            grid=(nb,),
            in_specs=[
                pl.BlockSpec((tb, 2048), rev),
                pl.BlockSpec((tb, 1536), rev),
                pl.BlockSpec((tb, 128), rev),
                pl.BlockSpec((8, 1536), const),
                pl.BlockSpec((8, 128), const),
                pl.BlockSpec((8, 128), const),
                pl.BlockSpec((ncb, GDN_H, 128, 128), lambda i: (nb - 1 - i, 0, 0, 0)),
                pl.BlockSpec((ncb, GDN_H, CH, CH), lambda i: (nb - 1 - i, 0, 0, 0)),
                pl.BlockSpec((ncb, GDN_H, CH, 256), lambda i: (nb - 1 - i, 0, 0, 0)),
                pl.BlockSpec((tb, 512), rev),
            ] + cx["specs"],
            out_specs=[
                pl.BlockSpec((tb, 2048), rev),
                pl.BlockSpec((tb, 128), rev),
                pl.BlockSpec((8, 1536), const),
                pl.BlockSpec((8, 128), const),
                pl.BlockSpec((8, 128), const),
            ] + cx["specs"],
            out_shape=[
                jax.ShapeDtypeStruct((seq, 2048), BF16),
                jax.ShapeDtypeStruct((seq, 128), BF16),
                jax.ShapeDtypeStruct((8, 1536), F32),
                jax.ShapeDtypeStruct((8, 128), F32),
                jax.ShapeDtypeStruct((8, 128), F32),
            ] + cx["out_shape"],
            scratch_shapes=[pltpu.VMEM((GDN_H, 128, 128), F32), pltpu.VMEM((8, 1536), F32)] + cx["scratch"],
            compiler_params=pltpu.CompilerParams(dimension_semantics=("arbitrary",), vmem_limit_bytes=VMEM_LIMIT,
                                                 has_side_effects=comm is not None),
            name="gdn_bwd" + cx["tag"],
        )(pg, pre, sm, cw, prm, nw, st, ti, uw, doa, *comm_args)

    return call


def _expand_mat():
    r = _iota2((128, SSD_W), 0)
    c = _iota2((128, SSD_W), 1)
    return (jnp.right_shift(c, 6) == r).astype(F32)


def _reduce_heads(v, e):
    vh, vl = _split(v)
    eb = e.astype(BF16)
    nt = (((1,), (1,)), ((), ()))
    return (lax.dot_general(vh, eb, nt, preferred_element_type=F32)
            + lax.dot_general(vl, eb, nt, preferred_element_type=F32))


def _reduce_heads1(v, e):
    nt = (((1,), (1,)), ((), ()))
    return lax.dot_general(v.astype(BF16), e.astype(BF16), nt, preferred_element_type=F32)


def _row8(v):
    return jnp.broadcast_to(v, (8, v.shape[1]))


def _ssd_common(ps_ref, halo8, ss, cw, cb, prm, tb, pre=None):
    raw = ps_ref[:, 0:1536]
    taps = None
    if pre is None:
        taps = _conv_taps(raw, halo8, tb)
        pre = taps[0] * cw[3:4, :] + taps[1] * cw[2:3, :] + taps[2] * cw[1:2, :] + taps[3] * cw[0:1, :] + cb[0:1, :]
    act = _silu(pre)
    dt_in = ss + prm[1:2, :]
    dt = _softplus(dt_in)
    a = dt * (-jnp.exp(prm[0:1, :]))
    acum = _dot01l(_chunk_tri(tb), a)
    e = _expand_mat()
    dt_e = _dot01r(dt, e)
    xdt = act[:, 0:SSD_W] * dt_e
    ea_e = _dot01r(jnp.exp(acum), e)
    d_e = _dot01r(_row8(prm[2:3, :]), e)[0:1, :]
    return raw, taps, pre, act, dt_in, dt, a, acum, e, dt_e, xdt, ea_e, d_e


def _ssd_chunk(act, acum, act_t, e, c):
    r0 = c * CH
    rows = slice(r0, r0 + CH)
    alast = acum[r0 + CH - 1:r0 + CH, :]
    wdec = jnp.exp(alast - acum[rows, :])
    wd_e = _dot01r(wdec, e)
    eal_e = _dot01r(_row8(jnp.exp(alast)), e)[0:1, :]
    return rows, wd_e, eal_e


def _ssd_lmat(acum, act_t, c, h, causal):
    r0 = c * CH
    acol = acum[r0:r0 + CH, h:h + 1]
    arow = act_t[h:h + 1, r0:r0 + CH]
    return jnp.exp(jnp.where(causal, acol - arow, NEG))


def _make_ssd_fwd(seq, tb):
    ncb = tb // CH
    nb = seq // tb
    hg = SSD_H // SSD_G
    gw = SSD_W // SSD_G

    def body(ps_ref, ss_ref, cw_ref, cb_ref, prm_ref, nw_ref, ob_ref, st_ref, pre_ref, y_ref, hs_scr, halo_scr):
        @pl.when(pl.program_id(0) == 0)
        def _():
            hs_scr[...] = jnp.zeros_like(hs_scr)
            halo_scr[...] = jnp.zeros_like(halo_scr)

        causal, _, _ = _masks()
        (raw, _, pre, act, _, _, _, acum, e, _, xdt, ea_e, d_e) = _ssd_common(
            ps_ref, halo_scr[...], ss_ref[...], cw_ref[...], cb_ref[...], prm_ref[...], tb)
        halo_scr[...] = raw[tb - 8:tb, :]
        pre_ref[...] = pre
        act_t = acum.T
        nw = nw_ref[0:1, :]
        for c in range(ncb):
            rows, wd_e, eal_e = _ssd_chunk(act, acum, act_t, e, c)
            st_ref[c] = hs_scr[...]
            ys = []
            for g in range(SSD_G):
                gc_ = slice(g * gw, (g + 1) * gw)
                bg = act[rows, SSD_W + g * 128:SSD_W + (g + 1) * 128]
                cg = act[rows, SSD_W + 256 + g * 128:SSD_W + 256 + (g + 1) * 128]
                cbm = _dot_nt(cg, bg)
                hs = hs_scr[:, gc_]
                yin = _dot(cg, hs)
                yh = []
                for hh in range(hg):
                    h = g * hg + hh
                    lm = _ssd_lmat(acum, act_t, c, h, causal)
                    yh.append(_dot(cbm * lm, xdt[rows, h * SSD_P:(h + 1) * SSD_P]))
                ys.append(jnp.concatenate(yh, axis=1) + yin * ea_e[rows, gc_])
                hs_scr[:, gc_] = hs * eal_e[:, gc_] + _dot_tn(bg, xdt[rows, gc_] * wd_e[:, gc_])
            y = jnp.concatenate(ys, axis=1) + act[rows, 0:SSD_W] * d_e
            y_ref[rows, :] = y
            yz = y * _silu(ps_ref[rows, 1536:2560])
            outs = [_rms_fwd(yz[:, g * gw:(g + 1) * gw], nw[:, g * gw:(g + 1) * gw], gw)[2] for g in range(SSD_G)]
            ob_ref[rows, :] = jnp.concatenate(outs, axis=1).astype(ob_ref.dtype)

    def call(ps, ss, cw, cb, prm, nw):
        const = lambda i: (0, 0)
        return pl.pallas_call(
            body,
            grid=(nb,),
            in_specs=[
                pl.BlockSpec((tb, 2560), lambda i: (i, 0)),
                pl.BlockSpec((tb, 128), lambda i: (i, 0)),
                pl.BlockSpec((8, 1536), const),
                pl.BlockSpec((8, 1536), const),
                pl.BlockSpec((8, 128), const),
                pl.BlockSpec((8, SSD_W), const),
            ],
            out_specs=[
                pl.BlockSpec((tb, SSD_W), lambda i: (i, 0)),
                pl.BlockSpec((ncb, SSD_N, SSD_W), lambda i: (i, 0, 0)),
                pl.BlockSpec((tb, 1536), lambda i: (i, 0)),
                pl.BlockSpec((tb, SSD_W), lambda i: (i, 0)),
            ],
            out_shape=[
                jax.ShapeDtypeStruct((seq, SSD_W), BF16),
                jax.ShapeDtypeStruct((seq // CH, SSD_N, SSD_W), F32),
                jax.ShapeDtypeStruct((seq, 1536), F32),
                jax.ShapeDtypeStruct((seq, SSD_W), F32),
            ],
            scratch_shapes=[pltpu.VMEM((SSD_N, SSD_W), F32), pltpu.VMEM((8, 1536), F32)],
            compiler_params=pltpu.CompilerParams(dimension_semantics=("arbitrary",), vmem_limit_bytes=VMEM_LIMIT),
            name="ssd_fwd",
        )(ps, ss, cw, cb, prm, nw)

    return call


def _make_ssd_bwd(seq, tb):
    ncb = tb // CH
    nb = seq // tb
    hb = tb // 8
    hg = SSD_H // SSD_G
    gw = SSD_W // SSD_G

    def body(ps_ref, pre_ref, y_ref, ss_ref, cw_ref, cb_ref, prm_ref, nw_ref, st_ref, dob_ref,
             dps_ref, dss_ref, dcw_ref, dcb_ref, dprm_ref, dnw_ref, dhs_scr, nxt_scr):
        i = pl.program_id(0)

        @pl.when(i == 0)
        def _():
            dhs_scr[...] = jnp.zeros_like(dhs_scr)
            nxt_scr[...] = jnp.zeros_like(nxt_scr)
            dcw_ref[...] = jnp.zeros_like(dcw_ref)
            dcb_ref[...] = jnp.zeros_like(dcb_ref)
            dprm_ref[...] = jnp.zeros_like(dprm_ref)
            dnw_ref[...] = jnp.zeros_like(dnw_ref)

        causal, _, _ = _masks()
        cw = cw_ref[...]
        prm = prm_ref[...]
        (raw, _, pre, act, dt_in, dt, a, acum, e, dt_e, xdt, ea_e, d_e) = _ssd_common(
            ps_ref, None, ss_ref[...], cw, cb_ref[...], prm, tb, pre=pre_ref[...])
        act_t = acum.T
        nw = nw_ref[0:1, :]

        dx_l, db_l, dc_l, dz_l, ddt_l, da_l = ([None] * ncb for _ in range(6))
        upper_tri = (_iota2((CH, CH), 1) >= _iota2((CH, CH), 0)).astype(F32)
        tri_pair = jnp.concatenate([upper_tri, (_iota2((CH, CH), 1) < _iota2((CH, CH), 0)).astype(F32)], axis=1)
        below = jnp.bitwise_and(_iota2((CH, gw), 1), CH - 1) < _iota2((CH, gw), 0)
        dnw_acc = jnp.zeros((1, SSD_W), F32)
        dd_acc = jnp.zeros((1, SSD_W), F32)

        for c in reversed(range(ncb)):
            rows, wd_e, eal_e = _ssd_chunk(act, acum, act_t, e, c)
            xc = act[rows, 0:SSD_W]
            z = ps_ref[rows, 1536:2560]
            dob = dob_ref[rows, :]
            sz = _silu(z)
            dy_g, dz_g, dxdt_g, db_g, dc_g, da_g = [], [], [], [], [], []
            for g in range(SSD_G):
                gc_ = slice(g * gw, (g + 1) * gw)
                bg = act[rows, SSD_W + g * 128:SSD_W + (g + 1) * 128]
                cg = act[rows, SSD_W + 256 + g * 128:SSD_W + 256 + (g + 1) * 128]
                cbm = _dot_nt(cg, bg)
                hs = st_ref[c, :, gc_]
                yin = _dot(cg, hs)
                lmats = [_ssd_lmat(acum, act_t, c, g * hg + hh, causal) for hh in range(hg)]
                ea_g = ea_e[rows, gc_]
                y = y_ref[rows, gc_]
                yz = y * sz[:, gc_]
                on, r, _ = _rms_fwd(yz, nw[:, gc_], gw)
                dyz, dnw_rows = _rms_bwd(dob[:, gc_], on, r, nw[:, gc_], gw)
                dnw_acc = dnw_acc + _put_cols(jnp.sum(dnw_rows, axis=0, keepdims=True), g, gw)
                dy = dyz * sz[:, gc_]
                dz_g.append(dyz * y * _dsilu(z[:, gc_]))
                dd_acc = dd_acc + _put_cols(jnp.sum(dy * xc[:, gc_], axis=0, keepdims=True), g, gw)
                dhs_n = dhs_scr[:, gc_]
                dyin = dy * ea_g
                dcg = _dot_nt(dyin, hs)
                xw = xdt[rows, gc_] * wd_e[:, gc_]
                dbg = _dot_nt(xw, dhs_n)
                dxw = _dot(bg, dhs_n)
                dhs_scr[:, gc_] = dhs_n * eal_e[:, gc_] + _dot_tn(cg, dyin)
                dxi, ms, dcbm = [], [], jnp.zeros((CH, CH), F32)
                for hh in range(hg):
                    h = g * hg + hh
                    hc = slice(hh * SSD_P, (hh + 1) * SSD_P)
                    dyh = dy[:, hc]
                    lm = cbm * lmats[hh]
                    dxi.append(_dot_tn(lm, dyh))
                    dlm = _dot_nt(dyh, xdt[rows, h * SSD_P:(h + 1) * SSD_P])
                    ms.append(dlm * lm)
                    dcbm = dcbm + dlm * lmats[hh]
                dx_intra = jnp.concatenate(dxi, axis=1)
                ncat = _dot(upper_tri, jnp.concatenate(ms, axis=1))
                cum = _dot(tri_pair, jnp.concatenate([dy * yin * ea_g, dxw * xw], axis=0))
                da_g.append(jnp.where(below, ncat, 0.0) + cum
                            + jnp.sum(hs * dhs_n, axis=0, keepdims=True) * eal_e[:, gc_])
                dxdt_g.append(dx_intra + dxw * wd_e[:, gc_])
                dy_g.append(dy)
                db_g.append(dbg + _dot_tn(dcbm, cg))
                dc_g.append(dcg + _dot(dcbm, bg))
            dy = jnp.concatenate(dy_g, axis=1)
            dxdt = jnp.concatenate(dxdt_g, axis=1)
            dx_l[c] = dxdt * dt_e[rows, :] + dy * d_e
            db_l[c] = jnp.concatenate(db_g, axis=1)
            dc_l[c] = jnp.concatenate(dc_g, axis=1)
            dz_l[c] = jnp.concatenate(dz_g, axis=1)
            ddt_l[c] = _reduce_heads1(dxdt * xc, e)
            da_l[c] = _reduce_heads1(jnp.concatenate(da_g, axis=1), e)

        da = jnp.concatenate(da_l, axis=0)
        neg_ea = -jnp.exp(prm[0:1, :])
        ddt = jnp.concatenate(ddt_l, axis=0) + da * neg_ea
        ddt_in = ddt * _sigmoid(dt_in)
        dss_ref[...] = ddt_in.astype(dss_ref.dtype)
        sub8 = _iota2((8, 128), 0)
        dalog = jnp.sum(da * a, axis=0, keepdims=True)
        ddtb = jnp.sum(ddt_in, axis=0, keepdims=True)
        dd = _reduce_heads(_row8(dd_acc), e)[0:1, :]
        dprm_ref[...] += (jnp.where(sub8 == 0, dalog, 0.0) + jnp.where(sub8 == 1, ddtb, 0.0)
                          + jnp.where(sub8 == 2, dd, 0.0))
        dnw_ref[...] += jnp.where(_iota2((8, SSD_W), 0) == 0, dnw_acc, 0.0)

        dact = jnp.concatenate([jnp.concatenate(dx_l, axis=0), jnp.concatenate(db_l, axis=0),
                                jnp.concatenate(dc_l, axis=0)], axis=1)
        dpre = dact * _dsilu(pre)
        back = _conv_back(dpre, nxt_scr[...], tb)
        nxt_scr[...] = dpre[0:8, :]
        draw = back[0] * cw[3:4, :] + back[1] * cw[2:3, :] + back[2] * cw[1:2, :] + back[3] * cw[0:1, :]
        dps_ref[:, 0:1536] = draw.astype(dps_ref.dtype)
        dps_ref[:, 1536:2560] = jnp.concatenate(dz_l, axis=0).astype(dps_ref.dtype)
        sub_c = _iota2((8, 1536), 0)
        dcw_new = jnp.zeros((8, 1536), F32)
        for s_ in range(CONV_W):
            dcw_new = dcw_new + jnp.where(sub_c == 3 - s_, jnp.sum(back[s_] * raw, axis=0, keepdims=True), 0.0)
        dcw_ref[...] += dcw_new
        dcb_ref[...] += jnp.where(sub_c == 0, jnp.sum(dpre, axis=0, keepdims=True), 0.0)

    def call(ps, pre, y, ss, cw, cb, prm, nw, st, dob, comm=None, comm_args=()):
        rev = lambda i: (nb - 1 - i, 0)
        const = lambda i: (0, 0)
        cx = _exchange_specs(comm)
        return pl.pallas_call(
            _with_exchange(body, comm, 10, 6, nb),
            grid=(nb,),
            in_specs=[
                pl.BlockSpec((tb, 2560), rev),
                pl.BlockSpec((tb, 1536), rev),
                pl.BlockSpec((tb, SSD_W), rev),
                pl.BlockSpec((tb, 128), rev),
                pl.BlockSpec((8, 1536), const),
                pl.BlockSpec((8, 1536), const),
                pl.BlockSpec((8, 128), const),
                pl.BlockSpec((8, SSD_W), const),
                pl.BlockSpec((ncb, SSD_N, SSD_W), lambda i: (nb - 1 - i, 0, 0)),
                pl.BlockSpec((tb, SSD_W), rev),
            ] + cx["specs"],
            out_specs=[
                pl.BlockSpec((tb, 2560), rev),
                pl.BlockSpec((tb, 128), rev),
                pl.BlockSpec((8, 1536), const),
                pl.BlockSpec((8, 1536), const),
                pl.BlockSpec((8, 128), const),
                pl.BlockSpec((8, SSD_W), const),
            ] + cx["specs"],
            out_shape=[
                jax.ShapeDtypeStruct((seq, 2560), BF16),
                jax.ShapeDtypeStruct((seq, 128), BF16),
                jax.ShapeDtypeStruct((8, 1536), F32),
                jax.ShapeDtypeStruct((8, 1536), F32),
                jax.ShapeDtypeStruct((8, 128), F32),
                jax.ShapeDtypeStruct((8, SSD_W), F32),
            ] + cx["out_shape"],
            scratch_shapes=[pltpu.VMEM((SSD_N, SSD_W), F32), pltpu.VMEM((8, 1536), F32)] + cx["scratch"],
            compiler_params=pltpu.CompilerParams(dimension_semantics=("arbitrary",), vmem_limit_bytes=VMEM_LIMIT,
                                                 has_side_effects=comm is not None),
            name="ssd_bwd" + cx["tag"],
        )(ps, pre, y, ss, cw, cb, prm, nw, st, dob, *comm_args)

    return call


def _ret_consts(h):
    lg = math.log(1.0 - 2.0 ** (-5.0 - h))
    r = _iota2((CH, CH), 0)
    c = _iota2((CH, CH), 1)
    rel = (r - c).astype(F32)
    dmat = jnp.where(r >= c, jnp.exp(jnp.maximum(rel, 0.0) * lg), 0.0)
    idx = _iota2((CH, 1), 0).astype(F32)
    qdec = jnp.exp((idx + 1.0) * lg)
    kdec = jnp.exp((CH - 1.0 - idx) * lg)
    cdec = math.exp(CH * lg)
    return dmat, qdec, kdec, cdec


def _ret_batch(pr_ref, cc_ref, ss_ref, ncb):
    pairs = [(c, h) for c in range(ncb) for h in range(RET_H)]

    def st(off):
        return jnp.stack([pr_ref[c * CH:(c + 1) * CH, off + h * 128:off + (h + 1) * 128] for c, h in pairs])

    cc = jnp.stack([cc_ref[c * CH:(c + 1) * CH, :] for c, _ in pairs])
    ss = jnp.stack([ss_ref[c * CH:(c + 1) * CH, :] for c, _ in pairs])
    consts = [_ret_consts(h) for h in range(RET_H)]
    dmat = jnp.stack([consts[h][0] for _, h in pairs])
    qdec = jnp.stack([consts[h][1] for _, h in pairs])
    kdec = jnp.stack([consts[h][2] for _, h in pairs])
    cdec = jnp.stack([jnp.full((1, 1), consts[h][3], F32) for h in range(RET_H)])
    q = _rot(st(0), cc, ss)
    k = _rot(st(512), cc, ss) * (RET_D ** -0.5)
    return dict(q=q, k=k, v=st(1024), z=st(1536), cc=cc, ss=ss, dmat=dmat, qdec=qdec, kdec=kdec, cdec=cdec,
                s=_bdot(q, k, _NT) * dmat)


def _rot(t, cc, ss):
    return t * cc + pltpu.roll(t, 64, axis=t.ndim - 1) * ss


def _rot_bwd(d, cc, ss):
    return d * cc + pltpu.roll(d * ss, 64, axis=d.ndim - 1)


def _make_ret_fwd(seq, tb):
    ncb = tb // CH
    nb = seq // tb

    def body(pr_ref, cc_ref, ss_ref, nw_ref, oc_ref, st_ref, r_scr):
        @pl.when(pl.program_id(0) == 0)
        def _():
            r_scr[...] = jnp.zeros_like(r_scr)

        d = _ret_batch(pr_ref, cc_ref, ss_ref, ncb)
        kd = d["k"] * d["kdec"]
        for c in range(ncb):
            bs = slice(c * RET_H, (c + 1) * RET_H)
            rs = r_scr[...]
            st_ref[c] = rs
            r_scr[...] = rs * d["cdec"] + _bdot(kd[bs], d["v"][bs], _TN)
        r_prev = st_ref[...].reshape(ncb * RET_H, 128, 128)
        o = _bdot(d["s"], d["v"], _NN) + _bdot(d["q"], r_prev, _NN) * d["qdec"]
        _, _, y = _rms_fwd(o, nw_ref[0:1, :], RET_D)
        out = y * _silu(d["z"])
        for c in range(ncb):
            for h in range(RET_H):
                oc_ref[c * CH:(c + 1) * CH, h * 128:(h + 1) * 128] = out[c * RET_H + h].astype(oc_ref.dtype)

    def call(pr, cc, ss, nw):
        return pl.pallas_call(
            body,
            grid=(nb,),
            in_specs=[
                pl.BlockSpec((tb, 2048), lambda i: (i, 0)),
                pl.BlockSpec((tb, 128), lambda i: (i, 0)),
                pl.BlockSpec((tb, 128), lambda i: (i, 0)),
                pl.BlockSpec((8, 128), lambda i: (0, 0)),
            ],
            out_specs=[
                pl.BlockSpec((tb, 512), lambda i: (i, 0)),
                pl.BlockSpec((ncb, RET_H, 128, 128), lambda i: (i, 0, 0, 0)),
            ],
            out_shape=[
                jax.ShapeDtypeStruct((seq, 512), BF16),
                jax.ShapeDtypeStruct((seq // CH, RET_H, 128, 128), F32),
            ],
            scratch_shapes=[pltpu.VMEM((RET_H, 128, 128), F32)],
            compiler_params=pltpu.CompilerParams(dimension_semantics=("arbitrary",), vmem_limit_bytes=VMEM_LIMIT),
            name="ret_fwd",
        )(pr, cc, ss, nw)

    return call


def _make_ret_bwd(seq, tb):
    ncb = tb // CH
    nb = seq // tb

    def body(pr_ref, cc_ref, ss_ref, nw_ref, st_ref, doc_ref, dpr_ref, dnw_ref, dr_scr):
        @pl.when(pl.program_id(0) == 0)
        def _():
            dr_scr[...] = jnp.zeros_like(dr_scr)
            dnw_ref[...] = jnp.zeros_like(dnw_ref)

        nw = nw_ref[0:1, :]
        scale = RET_D ** -0.5
        n = ncb * RET_H
        d = _ret_batch(pr_ref, cc_ref, ss_ref, ncb)
        q, k, v, z, s = d["q"], d["k"], d["v"], d["z"], d["s"]
        r_prev = st_ref[...].reshape(n, 128, 128)
        o = _bdot(s, v, _NN) + _bdot(q, r_prev, _NN) * d["qdec"]
        doc = jnp.stack([doc_ref[c * CH:(c + 1) * CH, h * 128:(h + 1) * 128]
                         for c in range(ncb) for h in range(RET_H)])
        on, r, y = _rms_fwd(o, nw, RET_D)
        dz = doc * y * _dsilu(z)
        do, dnw_rows = _rms_bwd(doc * _silu(z), on, r, nw, RET_D)
        dnw_acc = jnp.sum(jnp.sum(dnw_rows, axis=0), axis=0, keepdims=True)
        dqd = do * d["qdec"]
        qtd = _bdot(q, dqd, _TN)
        drn_l = [None] * ncb
        for c in reversed(range(ncb)):
            drn_l[c] = dr_scr[...]
            dr_scr[...] = qtd[c * RET_H:(c + 1) * RET_H] + d["cdec"] * drn_l[c]
        drn = jnp.concatenate(drn_l, axis=0)
        ds = _bdot(do, v, _NT) * d["dmat"]
        dq = _rot_bwd(_bdot(ds, k, _NN) + _bdot(dqd, r_prev, _NT), d["cc"], d["ss"])
        dk = _rot_bwd((_bdot(ds, q, _TN) + _bdot(v, drn, _NT) * d["kdec"]) * scale, d["cc"], d["ss"])
        dv = _bdot(s, do, _TN) + _bdot(k * d["kdec"], drn, _NN)
        for c in range(ncb):
            rows = slice(c * CH, (c + 1) * CH)
            for h in range(RET_H):
                b = c * RET_H + h
                for j, val in enumerate((dq, dk, dv, dz)):
                    dpr_ref[rows, j * 512 + h * 128:j * 512 + (h + 1) * 128] = val[b].astype(dpr_ref.dtype)
        dnw_ref[...] += jnp.where(_iota2((8, 128), 0) == 0, dnw_acc, 0.0)

    def call(pr, cc, ss, nw, st, doc):
        rev = lambda i: (nb - 1 - i, 0)
        return pl.pallas_call(
            body,
            grid=(nb,),
            in_specs=[
                pl.BlockSpec((tb, 2048), rev),
                pl.BlockSpec((tb, 128), rev),
                pl.BlockSpec((tb, 128), rev),
                pl.BlockSpec((8, 128), lambda i: (0, 0)),
                pl.BlockSpec((ncb, RET_H, 128, 128), lambda i: (nb - 1 - i, 0, 0, 0)),
                pl.BlockSpec((tb, 512), rev),
            ],
            out_specs=[
                pl.BlockSpec((tb, 2048), rev),
                pl.BlockSpec((8, 128), lambda i: (0, 0)),
            ],
            out_shape=[
                jax.ShapeDtypeStruct((seq, 2048), BF16),
                jax.ShapeDtypeStruct((8, 128), F32),
            ],
            scratch_shapes=[pltpu.VMEM((RET_H, 128, 128), F32)],
            compiler_params=pltpu.CompilerParams(dimension_semantics=("arbitrary",), vmem_limit_bytes=VMEM_LIMIT),
            name="ret_bwd",
        )(pr, cc, ss, nw, st, doc)

    return call


def _rope_tables(seq):
    half = RET_D // 2
    inv = ROPE_BASE ** (-jnp.arange(half, dtype=F32) / half)
    hi = (CH * jnp.arange(seq // CH, dtype=jnp.int32)).astype(F32)[:, None] * inv[None, :]
    lo = jnp.arange(CH, dtype=jnp.int32).astype(F32)[:, None] * inv[None, :]
    ch, sh, cl, sl = jnp.cos(hi)[:, None, :], jnp.sin(hi)[:, None, :], jnp.cos(lo)[None], jnp.sin(lo)[None]
    cos = (ch * cl - sh * sl).reshape(seq, half)
    sin = (sh * cl + ch * sl).reshape(seq, half)
    return jnp.concatenate([cos, cos], axis=1), jnp.concatenate([-sin, sin], axis=1)


SEG_G, SEG_S, SEG_R, SEG_GS, SEG_SS = (0, 2048), (2048, 4608), (4608, 6656), (6656, 6784), (6784, 6912)
NP = 6912
SEGS = (SEG_G, SEG_S, SEG_R, SEG_GS, SEG_SS)


def _resident(shape):
    return pl.BlockSpec(shape, lambda i: (0,) * len(shape), pipeline_mode=pl.Buffered(1))


def _make_inproj(seq, tl):
    def body(x_ref, pn_ref, w_ref, pg_ref, ps_ref, pr_ref, gs_ref, ss_ref, ht_ref):
        x = x_ref[...]
        _, _, hn = _rms_fwd(x, pn_ref[0:1, :], D_MODEL)
        h = hn.astype(BF16)
        ht_ref[...] = hn.T.astype(BF16)
        for (a, b), o_ref in zip(SEGS, (pg_ref, ps_ref, pr_ref, gs_ref, ss_ref)):
            o_ref[...] = jnp.dot(h, w_ref[:, a:b], preferred_element_type=F32)

    def call(x, pn, w, comm=None, comm_args=()):
        row = lambda i: (i, 0)
        cx = _exchange_specs(comm)
        return pl.pallas_call(
            _with_exchange(body, comm, 3, 6, seq // tl),
            grid=(seq // tl,),
            in_specs=[pl.BlockSpec((tl, D_MODEL), row), _resident((8, D_MODEL)), _resident((D_MODEL, NP))]
            + cx["specs"],
            out_specs=[pl.BlockSpec((tl, b - a), row) for a, b in SEGS]
            + [pl.BlockSpec((D_MODEL, tl), lambda i: (0, i))] + cx["specs"],
            out_shape=[jax.ShapeDtypeStruct((seq, b - a), F32) for a, b in SEGS]
            + [jax.ShapeDtypeStruct((D_MODEL, seq), BF16)] + cx["out_shape"],
            scratch_shapes=cx["scratch"],
            compiler_params=pltpu.CompilerParams(dimension_semantics=("arbitrary",), vmem_limit_bytes=VMEM_LIMIT,
                                                 has_side_effects=comm is not None),
            name="inproj" + cx["tag"],
        )(x, pn, w, *comm_args)

    return call


def _make_outproj(seq, tl):
    def body(oa_ref, ob_ref, oc_ref, w_ref, x_ref, qn_ref, out_ref, xn_ref):
        out = (jnp.dot(oa_ref[...], w_ref[0:512, :], preferred_element_type=F32)
               + jnp.dot(ob_ref[...], w_ref[512:1536, :], preferred_element_type=F32)
               + jnp.dot(oc_ref[...], w_ref[1536:2048, :], preferred_element_type=F32))
        out_ref[...] = out
        _, _, y = _rms_fwd(out, qn_ref[0:1, :], D_MODEL)
        xn_ref[...] = x_ref[...] + y

    def call(oa, ob, oc, w, x, qn):
        row = lambda i: (i, 0)
        return pl.pallas_call(
            body,
            grid=(seq // tl,),
            in_specs=[pl.BlockSpec((tl, 512), row), pl.BlockSpec((tl, 1024), row), pl.BlockSpec((tl, 512), row),
                      _resident((2048, D_MODEL)), pl.BlockSpec((tl, D_MODEL), row), _resident((8, D_MODEL))],
            out_specs=[pl.BlockSpec((tl, D_MODEL), row), pl.BlockSpec((tl, D_MODEL), row)],
            out_shape=[jax.ShapeDtypeStruct((seq, D_MODEL), F32), jax.ShapeDtypeStruct((seq, D_MODEL), F32)],
            compiler_params=pltpu.CompilerParams(dimension_semantics=("arbitrary",), vmem_limit_bytes=VMEM_LIMIT),
            name="outproj",
        )(oa, ob, oc, w, x, qn)

    return call


def _make_outproj_loss(seq, tl):
    def body(oa_ref, ob_ref, oc_ref, w_ref, x_ref, qn_ref, t_ref, out_ref, dy_ref, loss_ref):
        @pl.when(pl.program_id(0) == 0)
        def _():
            loss_ref[...] = jnp.zeros_like(loss_ref)

        out = (jnp.dot(oa_ref[...], w_ref[0:512, :], preferred_element_type=F32)
               + jnp.dot(ob_ref[...], w_ref[512:1536, :], preferred_element_type=F32)
               + jnp.dot(oc_ref[...], w_ref[1536:2048, :], preferred_element_type=F32))
        out_ref[...] = out
        _, _, y = _rms_fwd(out, qn_ref[0:1, :], D_MODEL)
        err = (x_ref[...] + y) - t_ref[...]
        dy_ref[...] = err * (1.0 / D_MODEL)
        part = jnp.sum(jnp.sum(err * err, axis=1, keepdims=True), axis=0, keepdims=True) * (0.5 / D_MODEL)
        loss_ref[...] += jnp.where((_iota2((8, 128), 0) == 0) & (_iota2((8, 128), 1) == 0), part, 0.0)

    def call(oa, ob, oc, w, x, qn, t):
        row = lambda i: (i, 0)
        return pl.pallas_call(
            body,
            grid=(seq // tl,),
            in_specs=[pl.BlockSpec((tl, 512), row), pl.BlockSpec((tl, 1024), row), pl.BlockSpec((tl, 512), row),
                      _resident((2048, D_MODEL)), pl.BlockSpec((tl, D_MODEL), row), _resident((8, D_MODEL)),
                      pl.BlockSpec((tl, D_MODEL), row)],
            out_specs=[pl.BlockSpec((tl, D_MODEL), row), pl.BlockSpec((tl, D_MODEL), row),
                       pl.BlockSpec((8, 128), lambda i: (0, 0))],
            out_shape=[jax.ShapeDtypeStruct((seq, D_MODEL), F32), jax.ShapeDtypeStruct((seq, D_MODEL), F32),
                       jax.ShapeDtypeStruct((8, 128), F32)],
            compiler_params=pltpu.CompilerParams(dimension_semantics=("arbitrary",), vmem_limit_bytes=VMEM_LIMIT),
            name="outproj_loss",
        )(oa, ob, oc, w, x, qn, t)

    return call


def _make_outproj_bwd(seq, tl):
    def body(dxn_ref, out_ref, oa_ref, ob_ref, oc_ref, w_ref, qn_ref,
             doa_ref, dob_ref, doc_ref, dqn_ref, dw_ref, dwb_ref):
        @pl.when(pl.program_id(0) == 0)
        def _():
            dqn_ref[...] = jnp.zeros_like(dqn_ref)
            dw_ref[...] = jnp.zeros_like(dw_ref)

        qn = qn_ref[0:1, :]
        on, r, _ = _rms_fwd(out_ref[...], qn, D_MODEL)
        dout, dqn_rows = _rms_bwd(dxn_ref[...], on, r, qn, D_MODEL)
        dqn_ref[...] += jnp.where(_iota2((8, D_MODEL), 0) == 0, jnp.sum(dqn_rows, axis=0, keepdims=True), 0.0)
        db = dout.astype(BF16)
        nt = (((1,), (1,)), ((), ()))
        tn = (((0,), (0,)), ((), ()))
        doa_ref[...] = lax.dot_general(db, w_ref[0:512, :], nt, preferred_element_type=F32).astype(BF16)
        dob_ref[...] = lax.dot_general(db, w_ref[512:1536, :], nt, preferred_element_type=F32).astype(BF16)
        doc_ref[...] = lax.dot_general(db, w_ref[1536:2048, :], nt, preferred_element_type=F32).astype(BF16)
        dw_ref[0:512, :] += lax.dot_general(oa_ref[...], db, tn, preferred_element_type=F32)
        dw_ref[512:1536, :] += lax.dot_general(ob_ref[...], db, tn, preferred_element_type=F32)
        dw_ref[1536:2048, :] += lax.dot_general(oc_ref[...], db, tn, preferred_element_type=F32)

        @pl.when(pl.program_id(0) == seq // tl - 1)
        def _():
            dwb_ref[...] = dw_ref[...].astype(BF16)

    def call(dxn, out, oa, ob, oc, w, qn):
        row = lambda i: (i, 0)
        const = lambda i: (0, 0)
        return pl.pallas_call(
            body,
            grid=(seq // tl,),
            in_specs=[pl.BlockSpec((tl, D_MODEL), row), pl.BlockSpec((tl, D_MODEL), row),
                      pl.BlockSpec((tl, 512), row), pl.BlockSpec((tl, 1024), row), pl.BlockSpec((tl, 512), row),
                      _resident((2048, D_MODEL)), _resident((8, D_MODEL))],
            out_specs=[pl.BlockSpec((tl, 512), row), pl.BlockSpec((tl, 1024), row), pl.BlockSpec((tl, 512), row),
                       pl.BlockSpec((8, D_MODEL), const), pl.BlockSpec((2048, D_MODEL), const),
                       pl.BlockSpec((2048, D_MODEL), const)],
            out_shape=[jax.ShapeDtypeStruct((seq, 512), BF16), jax.ShapeDtypeStruct((seq, 1024), BF16),
                       jax.ShapeDtypeStruct((seq, 512), BF16), jax.ShapeDtypeStruct((8, D_MODEL), F32),
                       jax.ShapeDtypeStruct((2048, D_MODEL), F32), jax.ShapeDtypeStruct((2048, D_MODEL), BF16)],
            compiler_params=pltpu.CompilerParams(dimension_semantics=("arbitrary",), vmem_limit_bytes=VMEM_LIMIT),
            name="outproj_bwd",
        )(dxn, out, oa, ob, oc, w, qn)

    return call


def _make_inproj_bwd_dx(seq, tl):
    def body(dg_ref, ds_ref, dr_ref, dgs_ref, dss_ref, w_ref, x_ref, pn_ref, dxn_ref, dx_ref, dpn_ref):
        @pl.when(pl.program_id(0) == 0)
        def _():
            dpn_ref[...] = jnp.zeros_like(dpn_ref)

        nt = (((1,), (1,)), ((), ()))
        dh = jnp.zeros((tl, D_MODEL), F32)
        for (a, b), d_ref in zip(SEGS, (dg_ref, ds_ref, dr_ref, dgs_ref, dss_ref)):
            dh = dh + lax.dot_general(d_ref[...], w_ref[:, a:b], nt, preferred_element_type=F32)
        pn = pn_ref[0:1, :]
        on, r, _ = _rms_fwd(x_ref[...], pn, D_MODEL)
        dx, dpn_rows = _rms_bwd(dh, on, r, pn, D_MODEL)
        dx_ref[...] = dx + dxn_ref[...]
        dpn_ref[...] += jnp.where(_iota2((8, D_MODEL), 0) == 0, jnp.sum(dpn_rows, axis=0, keepdims=True), 0.0)

    def call(dg, ds, dr, dgs, dss, w, x, pn, dxn, comm=None, comm_args=()):
        row = lambda i: (i, 0)
        cx = _exchange_specs(comm)
        return pl.pallas_call(
            _with_exchange(body, comm, 9, 2, seq // tl),
            grid=(seq // tl,),
            in_specs=[pl.BlockSpec((tl, b - a), row) for a, b in SEGS]
            + [_resident((D_MODEL, NP)), pl.BlockSpec((tl, D_MODEL), row), _resident((8, D_MODEL)),
               pl.BlockSpec((tl, D_MODEL), row)] + cx["specs"],
            out_specs=[pl.BlockSpec((tl, D_MODEL), row), pl.BlockSpec((8, D_MODEL), lambda i: (0, 0))] + cx["specs"],
            out_shape=[jax.ShapeDtypeStruct((seq, D_MODEL), F32), jax.ShapeDtypeStruct((8, D_MODEL), F32)]
            + cx["out_shape"],
            scratch_shapes=cx["scratch"],
            compiler_params=pltpu.CompilerParams(dimension_semantics=("arbitrary",), vmem_limit_bytes=VMEM_LIMIT,
                                                 has_side_effects=comm is not None),
            name="inproj_bwd_dx" + cx["tag"],
        )(dg, ds, dr, dgs, dss, w, x, pn, dxn, *comm_args)

    return call


def _make_inproj_bwd_dw_small(seq, tl, name):
    def body(ht_ref, a_ref, b_ref, dw_ref):
        @pl.when(pl.program_id(0) == 0)
        def _():
            dw_ref[...] = jnp.zeros_like(dw_ref)

        ht = ht_ref[...]
        dw_ref[:, 0:128] += jnp.dot(ht, a_ref[...], preferred_element_type=F32)
        dw_ref[:, 128:256] += jnp.dot(ht, b_ref[...], preferred_element_type=F32)

    def call(ht, a, b):
        return pl.pallas_call(
            body,
            grid=(seq // tl,),
            in_specs=[pl.BlockSpec((D_MODEL, tl), lambda i: (0, i)), pl.BlockSpec((tl, 128), lambda i: (i, 0)),
                      pl.BlockSpec((tl, 128), lambda i: (i, 0))],
            out_specs=pl.BlockSpec((D_MODEL, 256), lambda i: (0, 0)),
            out_shape=jax.ShapeDtypeStruct((D_MODEL, 256), F32),
            compiler_params=pltpu.CompilerParams(dimension_semantics=("arbitrary",), vmem_limit_bytes=VMEM_LIMIT),
            name=name,
        )(ht, a, b)

    return call


def _make_inproj_bwd_dw(seq, tl, width, tn, name):
    def body(ht_ref, d_ref, dw_ref):
        @pl.when(pl.program_id(1) == 0)
        def _():
            dw_ref[...] = jnp.zeros_like(dw_ref)

        dw_ref[...] += jnp.dot(ht_ref[...], d_ref[...], preferred_element_type=F32)

    def call(ht, d):
        return pl.pallas_call(
            body,
            grid=(width // tn, seq // tl),
            in_specs=[pl.BlockSpec((D_MODEL, tl), lambda j, i: (0, i)), pl.BlockSpec((tl, tn), lambda j, i: (i, j))],
            out_specs=pl.BlockSpec((D_MODEL, tn), lambda j, i: (0, j)),
            out_shape=jax.ShapeDtypeStruct((D_MODEL, width), F32),
            compiler_params=pltpu.CompilerParams(dimension_semantics=("arbitrary", "arbitrary"),
                                                 vmem_limit_bytes=VMEM_LIMIT),
            name=name,
        )(ht, d)

    return call


ADAM_LR, ADAM_B1, ADAM_B2, ADAM_EPS, ADAM_WD, ADAM_STEP = 0.001, 0.9, 0.999, 1e-08, 0.01, 10


def _adam_math(w, g, m, v):
    m = ADAM_B1 * m + (1.0 - ADAM_B1) * g
    v = ADAM_B2 * v + (1.0 - ADAM_B2) * (g * g)
    m_hat = m / (1.0 - ADAM_B1 ** ADAM_STEP)
    v_hat = v / (1.0 - ADAM_B2 ** ADAM_STEP)
    delta = -ADAM_LR * (m_hat / (jnp.sqrt(v_hat) + ADAM_EPS) + ADAM_WD * w)
    return delta, m, v


def _adamw(w, g, m, v, name):
    shape = w.shape
    cols = shape[-1]
    rows = w.size // cols
    tr = rows if rows <= 512 else 256
    assert rows % tr == 0

    def body(w_ref, g_ref, m_ref, v_ref, d_ref, mo_ref, vo_ref):
        d_ref[...], mo_ref[...], vo_ref[...] = _adam_math(w_ref[...], g_ref[...], m_ref[...], v_ref[...])

    spec = pl.BlockSpec((tr, cols), lambda i: (i, 0))
    outs = pl.pallas_call(
        body,
        grid=(rows // tr,),
        in_specs=[spec] * 4,
        out_specs=[spec] * 3,
        out_shape=[jax.ShapeDtypeStruct((rows, cols), F32)] * 3,
        compiler_params=pltpu.CompilerParams(dimension_semantics=("arbitrary",), vmem_limit_bytes=VMEM_LIMIT),
        name=name,
    )(*[a.reshape(rows, cols) for a in (w, g, m, v)])
    return (g,) + tuple(o.reshape(shape) for o in outs)


def _adamw_pairs(w, mine, theirs, m, v, name):
    na, r, cols = w.shape
    assert na == 2
    tr = 256
    assert r % tr == 0

    def body(w_ref, a0_ref, b0_ref, a1_ref, b1_ref, m_ref, v_ref, g_ref, d_ref, mo_ref, vo_ref):
        g = jnp.where(pl.program_id(0) == 0, a0_ref[...] + b0_ref[...], a1_ref[...] + b1_ref[...])
        g_ref[...] = g
        d_ref[...], mo_ref[...], vo_ref[...] = _adam_math(w_ref[...], g, m_ref[...], v_ref[...])

    nblk = r // tr
    full = pl.BlockSpec((None, tr, cols), lambda a, i: (a, i, 0))
    lay0 = pl.BlockSpec((None, tr, cols), lambda a, i: (0, i * (1 - a) + (nblk - 1) * a, 0))
    lay1 = pl.BlockSpec((None, tr, cols), lambda a, i: (0, i * a, 0))
    return pl.pallas_call(
        body,
        grid=(na, nblk),
        in_specs=[full, lay0, lay0, lay1, lay1, full, full],
        out_specs=[full] * 4,
        out_shape=[jax.ShapeDtypeStruct(w.shape, F32)] * 4,
        compiler_params=pltpu.CompilerParams(dimension_semantics=("arbitrary",) * 2, vmem_limit_bytes=VMEM_LIMIT),
        name=name,
    )(w, mine[0], theirs[0], mine[1], theirs[1], m, v)


MESH = pl.DeviceIdType.MESH
ANY = pl.BlockSpec(memory_space=pl.ANY)
CHIP_REL = ((1, 0), (0, 1), (1, 1))


def _flip(v, d):
    return 1 - v if d else v


class _ChipExchange:
    def __init__(self, kind, arrs, swap=()):
        self.kind, self.n_chip, self.n = kind, len(arrs), len(arrs) + len(swap)
        if kind == "gather":
            self.out_shape = [jax.ShapeDtypeStruct((4,) + a.shape, a.dtype) for a in arrs]
        else:
            self.out_shape = [jax.ShapeDtypeStruct((3,) + a.shape[1:], a.dtype) for a in arrs]
        self.out_shape += [jax.ShapeDtypeStruct(a.shape, a.dtype) for a in swap]
        self.scratch = [pltpu.SemaphoreType.DMA((4 * self.n,)), pltpu.SemaphoreType.DMA((4 * self.n,))]

    def _copies(self, ins, outs, sems):
        send_sems, recv_sems = sems
        x, y, c = lax.axis_index("x"), lax.axis_index("y"), lax.axis_index("c")
        me = 2 * x + y
        pairs = []
        for a in range(self.n_chip, self.n):
            cp = pltpu.make_async_remote_copy(
                src_ref=ins[a], dst_ref=outs[a], send_sem=send_sems.at[4 * a], recv_sem=recv_sems.at[4 * a],
                device_id=(x, y, 1 - c), device_id_type=MESH)
            pairs.append((cp, cp))
        for a in range(self.n_chip):
            for k, (dx, dy) in enumerate(CHIP_REL):
                px, py = _flip(x, dx), _flip(y, dy)
                sem = dict(send_sem=send_sems.at[4 * a + k], recv_sem=recv_sems.at[4 * a + k],
                           device_id=(px, py, c), device_id_type=MESH)
                if self.kind == "gather":
                    out = pltpu.make_async_remote_copy(src_ref=ins[a], dst_ref=outs[a].at[me], **sem)
                    inc = pltpu.make_async_remote_copy(src_ref=ins[a], dst_ref=outs[a].at[2 * px + py], **sem)
                else:
                    out = pltpu.make_async_remote_copy(src_ref=ins[a].at[2 * px + py], dst_ref=outs[a].at[k], **sem)
                    inc = out
                pairs.append((out, inc))
            if self.kind == "gather":
                own = pltpu.make_async_remote_copy(
                    src_ref=ins[a], dst_ref=outs[a].at[me], send_sem=send_sems.at[4 * a + 3],
                    recv_sem=recv_sems.at[4 * a + 3], device_id=(x, y, 1 - c), device_id_type=MESH)
                pairs.append((own, own))
        return pairs

    def start(self, ins, outs, sems):
        for out, _ in self._copies(ins, outs, sems):
            out.start()

    def finish(self, ins, outs, sems):
        pairs = self._copies(ins, outs, sems)
        for _, inc in pairs:
            inc.wait_recv()
        for out, _ in pairs:
            out.wait_send()


def _with_exchange(body, comm, n_in, n_out, nb):
    if comm is None:
        return body

    def wrapped(*refs):
        ins = refs[:n_in]
        c_in = refs[n_in:n_in + comm.n]
        outs = refs[n_in + comm.n:n_in + comm.n + n_out]
        c_out = refs[n_in + comm.n + n_out:n_in + 2 * comm.n + n_out]
        rest = refs[n_in + 2 * comm.n + n_out:]
        scratch, sems = rest[:len(rest) - 2], rest[len(rest) - 2:]

        @pl.when(pl.program_id(0) == 0)
        def _():
            comm.start(c_in, c_out, sems)

        body(*ins, *outs, *scratch)

        @pl.when(pl.program_id(0) == nb - 1)
        def _():
            comm.finish(c_in, c_out, sems)

    return wrapped


def _exchange_specs(comm):
    if comm is None:
        return dict(specs=[], out_shape=[], scratch=[], tag="")
    return dict(specs=[pl.BlockSpec(memory_space=pl.ANY)] * comm.n, out_shape=list(comm.out_shape),
                scratch=list(comm.scratch), tag="_" + comm.kind)


def _half(ref_or_shape, half):
    r = ref_or_shape[-2] // 2
    return pl.ds(half * r, r)


def _ag_rows(arrs, name):
    n = len(arrs)

    def body(*refs):
        ins, outs = refs[:n], refs[n:2 * n]
        send_sems, recv_sems, fsend_sems, frecv_sems, loc_sems = refs[2 * n:]
        x, y, c = lax.axis_index("x"), lax.axis_index("y"), lax.axis_index("c")
        me = 2 * x + y
        sib = (x, y, 1 - c)

        def chip_of(k):
            dx, dy = CHIP_REL[k]
            return _flip(x, dx), _flip(y, dy)

        def ici(a, k, slot):
            px, py = chip_of(k)
            rows = _half(arrs[a].shape, c)
            return pltpu.make_async_remote_copy(
                src_ref=ins[a].at[:, rows, :], dst_ref=outs[a].at[slot, :, rows, :], send_sem=send_sems.at[a * 3 + k],
                recv_sem=recv_sems.at[a * 3 + k], device_id=(px, py, c), device_id_type=MESH)

        def fwd(a, k, half):
            px, py = chip_of(k)
            blk = outs[a].at[2 * px + py, :, _half(arrs[a].shape, half), :]
            return pltpu.make_async_remote_copy(
                src_ref=blk, dst_ref=blk, send_sem=fsend_sems.at[a * 3 + k], recv_sem=frecv_sems.at[a * 3 + k],
                device_id=sib, device_id_type=MESH)

        own = [pltpu.make_async_remote_copy(src_ref=ins[a], dst_ref=outs[a].at[me], send_sem=loc_sems.at[a],
                                            recv_sem=loc_sems.at[n + a], device_id=sib, device_id_type=MESH)
               for a in range(n)]
        for cp in own:
            cp.start()
        for a in range(n):
            for k in range(3):
                ici(a, k, me).start()
        for a in range(n):
            for k in range(3):
                px, py = chip_of(k)
                ici(a, k, 2 * px + py).wait_recv()
                fwd(a, k, c).start()
        for a in range(n):
            for k in range(3):
                fwd(a, k, 1 - c).wait_recv()
        for a in range(n):
            for k in range(3):
                ici(a, k, me).wait_send()
                fwd(a, k, c).wait_send()
        for cp in own:
            cp.wait()

    return pl.pallas_call(
        body,
        in_specs=[ANY] * n,
        out_specs=[ANY] * n,
        out_shape=[jax.ShapeDtypeStruct((4,) + a.shape, a.dtype) for a in arrs],
        scratch_shapes=[pltpu.SemaphoreType.DMA((3 * n,)) for _ in range(4)] + [pltpu.SemaphoreType.DMA((2 * n,))],
        compiler_params=pltpu.CompilerParams(has_side_effects=True),
        name=name,
    )(*arrs)


def _sum_chips(own, recv, chip, name):
    _, na, r, cols = own.shape
    tr = 256
    assert r % tr == 0

    def body(chip_ref, o_ref, r_ref, s_ref):
        s_ref[...] = ((o_ref[...] + r_ref[0].astype(F32)) + r_ref[1].astype(F32)) + r_ref[2].astype(F32)

    return pl.pallas_call(
        body,
        grid_spec=pltpu.PrefetchScalarGridSpec(
            num_scalar_prefetch=1,
            grid=(na, r // tr),
            in_specs=[pl.BlockSpec((None, None, tr, cols), lambda a, i, ch: (ch[0], a, i, 0)),
                      pl.BlockSpec((3, None, tr, cols), lambda a, i, ch: (0, a, i, 0))],
            out_specs=pl.BlockSpec((None, tr, cols), lambda a, i, ch: (a, i, 0))),
        out_shape=jax.ShapeDtypeStruct((na, r, cols), F32),
        compiler_params=pltpu.CompilerParams(dimension_semantics=("arbitrary",) * 2, vmem_limit_bytes=VMEM_LIMIT),
        name=name,
    )(chip, own, recv)


def _swap_sibling(arrs, name):
    n = len(arrs)

    def body(*refs):
        ins, outs = refs[:n], refs[n:2 * n]
        send_sems, recv_sems = refs[2 * n:]
        x, y, c = lax.axis_index("x"), lax.axis_index("y"), lax.axis_index("c")
        cps = [pltpu.make_async_remote_copy(src_ref=ins[a], dst_ref=outs[a], send_sem=send_sems.at[a],
                                            recv_sem=recv_sems.at[a], device_id=(x, y, 1 - c), device_id_type=MESH)
               for a in range(n)]
        for cp in cps:
            cp.start()
        for cp in cps:
            cp.wait_recv()
        for cp in cps:
            cp.wait_send()

    return pl.pallas_call(
        body,
        in_specs=[ANY] * n,
        out_specs=[ANY] * n,
        out_shape=[jax.ShapeDtypeStruct(a.shape, a.dtype) for a in arrs],
        scratch_shapes=[pltpu.SemaphoreType.DMA((n,)), pltpu.SemaphoreType.DMA((n,))],
        compiler_params=pltpu.CompilerParams(has_side_effects=True),
        name=name,
    )(*arrs)


def _allreduce_small(vec, name):
    rows = vec.shape[0]

    def body(v_ref, out_ref, gat_ref, send_sems, recv_sems):
        x, y, c = lax.axis_index("x"), lax.axis_index("y"), lax.axis_index("c")
        me = 4 * x + 2 * y + c

        def remote(k, slot):
            dx, dy, dc = (k >> 2) & 1, (k >> 1) & 1, k & 1
            return pltpu.make_async_remote_copy(
                src_ref=v_ref, dst_ref=gat_ref.at[slot], send_sem=send_sems.at[k - 1], recv_sem=recv_sems.at[k - 1],
                device_id=(_flip(x, dx), _flip(y, dy), _flip(c, dc)), device_id_type=MESH)

        gat_ref[me] = v_ref[...]
        for k in range(1, 8):
            remote(k, me).start()
        for k in range(1, 8):
            dx, dy, dc = (k >> 2) & 1, (k >> 1) & 1, k & 1
            remote(k, 4 * _flip(x, dx) + 2 * _flip(y, dy) + _flip(c, dc)).wait_recv()
        for k in range(1, 8):
            remote(k, me).wait_send()
        acc = gat_ref[0]
        for j in range(1, 8):
            acc = acc + gat_ref[j]
        out_ref[...] = acc

    vm = pl.BlockSpec(memory_space=pltpu.VMEM)
    return pl.pallas_call(
        body,
        in_specs=[vm],
        out_specs=vm,
        out_shape=jax.ShapeDtypeStruct(vec.shape, F32),
        scratch_shapes=[pltpu.VMEM((8, rows, 128), F32), pltpu.SemaphoreType.DMA((7,)), pltpu.SemaphoreType.DMA((7,))],
        compiler_params=pltpu.CompilerParams(has_side_effects=True),
        name=name,
    )(vec)


def _pad8(v, width, lane0=0):
    v = v.reshape(1, -1) if v.ndim == 1 else v
    return jnp.zeros((8, width), F32).at[:v.shape[0], lane0:lane0 + v.shape[1]].set(v.astype(F32))


def _relayout_w_in(g):
    tr = 128
    q = N_IN // 4

    def body(g_ref, o_ref):
        w = jnp.concatenate([g_ref[j] for j in range(4)], axis=1)
        z = lambda n: jnp.zeros((tr, n), w.dtype)
        o_ref[...] = jnp.concatenate([w[:, 0:2048], w[:, 2056:4616], w[:, 4632:6680],
                                      w[:, 2048:2056], z(120), w[:, 4616:4632], z(112)], axis=1)

    return pl.pallas_call(
        body,
        grid=(D_MODEL // tr,),
        in_specs=[pl.BlockSpec((4, tr, q), lambda i: (0, i, 0))],
        out_specs=pl.BlockSpec((tr, NP), lambda i: (i, 0)),
        out_shape=jax.ShapeDtypeStruct((D_MODEL, NP), g.dtype),
        compiler_params=pltpu.CompilerParams(dimension_semantics=("arbitrary",), vmem_limit_bytes=VMEM_LIMIT),
        name="relayout_w_in",
    )(g)


def _unlayout_dw_in(dg, ds, dr, dsm):
    tr = 128
    q = N_IN // 4

    def body(g_ref, s_ref, r_ref, sm_ref, o_ref, ob_ref):
        w = jnp.concatenate([g_ref[...], sm_ref[:, 0:8], s_ref[...], sm_ref[:, 128:144], r_ref[...]], axis=1)
        for j in range(4):
            blk = w[:, q * j:q * (j + 1)]
            o_ref[j] = blk
            ob_ref[j] = blk.astype(BF16)

    row = lambda i: (i, 0)
    return pl.pallas_call(
        body,
        grid=(D_MODEL // tr,),
        in_specs=[pl.BlockSpec((tr, d.shape[1]), row) for d in (dg, ds, dr, dsm)],
        out_specs=[pl.BlockSpec((4, tr, q), lambda i: (0, i, 0))] * 2,
        out_shape=[jax.ShapeDtypeStruct((4, D_MODEL, q), F32), jax.ShapeDtypeStruct((4, D_MODEL, q), BF16)],
        compiler_params=pltpu.CompilerParams(dimension_semantics=("arbitrary",), vmem_limit_bytes=VMEM_LIMIT),
        name="unlayout_dw_in",
    )(dg, ds, dr, dsm)


TB = 256
TB_RET = 512
TB_SSD_BWD = 512
TL = 1024
TL_IN = 512
TL_OB = 1024
TK = 2048


def kernel(x, pre_norm, post_norm, w_in, gdn_conv, gdn_A_log, gdn_dt_bias, gdn_norm, ssd_conv, ssd_conv_b, ssd_A_log, ssd_dt_bias, ssd_D, ssd_norm, ret_norm, w_out, loss_target, m_pre_norm, m_post_norm, m_w_in, m_gdn_conv, m_gdn_A_log, m_gdn_dt_bias, m_gdn_norm, m_ssd_conv, m_ssd_conv_b, m_ssd_A_log, m_ssd_dt_bias, m_ssd_D, m_ssd_norm, m_ret_norm, m_w_out, v_pre_norm, v_post_norm, v_w_in, v_gdn_conv, v_gdn_A_log, v_gdn_dt_bias, v_gdn_norm, v_ssd_conv, v_ssd_conv_b, v_ssd_A_log, v_ssd_dt_bias, v_ssd_D, v_ssd_norm, v_ret_norm, v_w_out):
    seq = x.shape[1]
    chip = 2 * lax.axis_index("x") + lax.axis_index("y")
    x0 = x[0]

    wi_b, wo_b = w_in.astype(BF16), w_out.astype(BF16)
    (wi0_g,) = _ag_rows([wi_b[0:1]], "ag_weights")
    full_w_in = _relayout_w_in
    wp = [full_w_in(wi0_g[:, 0]), None]
    ag0 = _ChipExchange("gather", [wo_b[0], wo_b[1], gdn_conv, ssd_conv])
    ag1 = _ChipExchange("gather", [wi_b[1]])
    rope_c, rope_s = _rope_tables(seq)

    saved = []
    xc = x0
    for l in range(DEPTH):
        p = dict(
            pn=_pad8(pre_norm[l], D_MODEL), qn=_pad8(post_norm[l], D_MODEL),
            g_prm=_pad8(jnp.stack([gdn_A_log[l], gdn_dt_bias[l]]), 128, 4), g_nw=_pad8(gdn_norm[l], 128),
            s_cb=_pad8(ssd_conv_b[l], 1536),
            s_prm=_pad8(jnp.stack([ssd_A_log[l], ssd_dt_bias[l], ssd_D[l]]), 128), s_nw=_pad8(ssd_norm[l], SSD_W),
            r_nw=_pad8(ret_norm[l], 128))
        if l == 0:
            pg, ps, pr, gs, ss, ht, wo0_g, wo1_g, gcv_g, scv_g = _make_inproj(seq, TL_IN)(
                xc, p["pn"], wp[l], comm=ag0, comm_args=(wo_b[0], wo_b[1], gdn_conv, ssd_conv))
            wo = [wo0_g.reshape(2048, D_MODEL), wo1_g.reshape(2048, D_MODEL)]
            gcv = jnp.transpose(gcv_g, (1, 2, 0, 3)).reshape(DEPTH, CONV_W, 1536)
            scv = jnp.transpose(scv_g, (1, 2, 0, 3)).reshape(DEPTH, CONV_W, 1536)
        else:
            pg, ps, pr, gs, ss, ht = _make_inproj(seq, TL_IN)(xc, p["pn"], wp[l])
        p.update(g_cw=_pad8(gcv[l], 1536), s_cw=_pad8(scv[l], 1536))
        if l == 0:
            oa, stg, tig, uwg, gpre, wi1_g = _make_gdn_fwd(seq, TB)(
                pg, gs, p["g_cw"], p["g_prm"], p["g_nw"], comm=ag1, comm_args=(wi_b[1],))
            wp[1] = full_w_in(wi1_g)
        else:
            oa, stg, tig, uwg, gpre = _make_gdn_fwd(seq, TB)(pg, gs, p["g_cw"], p["g_prm"], p["g_nw"])
        ob, sts, spre, sy = _make_ssd_fwd(seq, TB)(ps, ss, p["s_cw"], p["s_cb"], p["s_prm"], p["s_nw"])
        oc, str_ = _make_ret_fwd(seq, TB_RET)(pr, rope_c, rope_s, p["r_nw"])
        if l == DEPTH - 1:
            out, dxn, lossp = _make_outproj_loss(seq, TL)(oa, ob, oc, wo[l], xc, p["qn"], loss_target[0])
            xn = None
        else:
            out, xn = _make_outproj(seq, TL)(oa, ob, oc, wo[l], xc, p["qn"])
        saved.append(dict(p=p, x=xc, ht=ht, spre=spre, sy=sy, gpre=gpre, pg=pg, ps=ps, pr=pr, gs=gs, ss=ss, stg=stg, tig=tig, uwg=uwg, sts=sts, str=str_,
                          oa=oa, ob=ob, oc=oc, out=out))
        xc = xn

    small = [None] * DEPTH
    gin, gin_b, gout, gout_b, q_in, q_out, s_in, s_out, t_in, t_out = ([None] * DEPTH for _ in range(10))
    chip1 = chip.astype(jnp.int32).reshape(1)

    def sum_chips(l):
        return (_sum_chips(gin[l][:, None], q_in[l][:, None], chip1, f"sum_chips_w_in{l}"),
                _sum_chips(gout[l][:, None], q_out[l][:, None], chip1, f"sum_chips_w_out{l}"))

    for l in reversed(range(DEPTH)):
        s = saved[l]
        p = s["p"]
        doa, dob, doc, dqn, dwo_l, dwo_b = _make_outproj_bwd(seq, TL_OB)(
            dxn, s["out"], s["oa"], s["ob"], s["oc"], wo[l], p["qn"])
        gout[l], gout_b[l] = dwo_l.reshape(4, 512, D_MODEL), dwo_b.reshape(4, 512, D_MODEL)
        gdn_args = (s["pg"], s["gpre"], s["gs"], p["g_cw"], p["g_prm"], p["g_nw"], s["stg"], s["tig"], s["uwg"], doa)
        if l == 0:
            payload = (gout_b[0],)
            dpg, dgs, dcw_g, dprm_g, dnw_g, q_out[0] = _make_gdn_bwd(seq, TB)(
                *gdn_args, comm=_ChipExchange("scatter", payload), comm_args=payload)
        else:
            dpg, dgs, dcw_g, dprm_g, dnw_g = _make_gdn_bwd(seq, TB)(*gdn_args)
        ssd_args = (s["ps"], s["spre"], s["sy"], s["ss"], p["s_cw"], p["s_cb"], p["s_prm"], p["s_nw"], s["sts"], dob)
        if l == 0:
            payload = (gin_b[1], gout_b[1])
            dps, dss, dcw_s, dcb_s, dprm_s, dnw_s, q_in[1], q_out[1] = _make_ssd_bwd(seq, TB_SSD_BWD)(
                *ssd_args, comm=_ChipExchange("scatter", payload), comm_args=payload)
        else:
            dps, dss, dcw_s, dcb_s, dprm_s, dnw_s = _make_ssd_bwd(seq, TB_SSD_BWD)(*ssd_args)
        dpr, dnw_r = _make_ret_bwd(seq, TB_RET)(s["pr"], rope_c, rope_s, p["r_nw"], s["str"], doc)
        dws = [_make_inproj_bwd_dw(seq, TK, d.shape[1], tn, f"inproj_bwd_dw{i}")(s["ht"], d)
               for i, (d, tn) in enumerate(((dpg, 2048), (dps, 1280), (dpr, 2048)))]
        dws.append(_make_inproj_bwd_dw_small(seq, TK, "inproj_bwd_dw3")(s["ht"], dgs, dss))
        gin[l], gin_b[l] = _unlayout_dw_in(*dws)
        dx_args = (dpg, dps, dpr, dgs, dss, wp[l], s["x"], p["pn"], dxn)
        if l == 0:
            s_in[1], s_out[1] = sum_chips(1)
            payload, swap = (gin_b[0],), (s_in[1], s_out[1])
            dx, dpn, q_in[0], t_in[1], t_out[1] = _make_inproj_bwd_dx(seq, TL_IN)(
                *dx_args, comm=_ChipExchange("scatter", payload, swap), comm_args=payload + swap)
        else:
            dx, dpn = _make_inproj_bwd_dx(seq, TL_IN)(*dx_args)
        small[l] = [dpn[0], dqn[0], dcw_g[0:4].reshape(-1), dprm_g[0, 4:8], dprm_g[1, 4:8], dnw_g[0],
                    dcw_s[0:4].reshape(-1), dcb_s[0], dprm_s[0, 0:16], dprm_s[1, 0:16], dprm_s[2, 0:16],
                    dnw_s[0], dnw_r[0]]
        dxn = dx
    grad_x = dxn[None]

    sizes = [a.shape[0] for a in small[0]]
    flat = jnp.concatenate(small[0] + small[1] + [lossp[0, 0:1]])
    n_flat = flat.shape[0]
    rows = -(-n_flat // 1024) * 8
    red = _allreduce_small(jnp.pad(flat, (0, rows * 128 - n_flat)).reshape(rows, 128), "allreduce_small").reshape(-1)
    per = sum(sizes)
    loss = red[2 * per]

    def pick(i):
        off = sum(sizes[:i])
        return jnp.stack([red[l * per + off:l * per + off + sizes[i]] for l in range(DEPTH)])

    g_small = dict(
        pre_norm=pick(0), post_norm=pick(1),
        gdn_conv=lax.dynamic_slice_in_dim(pick(2).reshape(DEPTH, CONV_W, 1536), chip * 384, 384, axis=2),
        gdn_A_log=pick(3), gdn_dt_bias=pick(4), gdn_norm=pick(5),
        ssd_conv=lax.dynamic_slice_in_dim(pick(6).reshape(DEPTH, CONV_W, 1536), chip * 384, 384, axis=2),
        ssd_conv_b=pick(7), ssd_A_log=pick(8), ssd_dt_bias=pick(9), ssd_D=pick(10), ssd_norm=pick(11),
        ret_norm=pick(12))

    s_in[0], s_out[0] = sum_chips(0)
    t_in[0], t_out[0] = _swap_sibling([s_in[0], s_out[0]], "swap_grads")

    weights = dict(pre_norm=pre_norm, post_norm=post_norm, w_in=w_in, gdn_conv=gdn_conv, gdn_A_log=gdn_A_log,
                   gdn_dt_bias=gdn_dt_bias, gdn_norm=gdn_norm, ssd_conv=ssd_conv, ssd_conv_b=ssd_conv_b,
                   ssd_A_log=ssd_A_log, ssd_dt_bias=ssd_dt_bias, ssd_D=ssd_D, ssd_norm=ssd_norm, ret_norm=ret_norm,
                   w_out=w_out)
    ms = dict(pre_norm=m_pre_norm, post_norm=m_post_norm, w_in=m_w_in, gdn_conv=m_gdn_conv, gdn_A_log=m_gdn_A_log,
              gdn_dt_bias=m_gdn_dt_bias, gdn_norm=m_gdn_norm, ssd_conv=m_ssd_conv, ssd_conv_b=m_ssd_conv_b,
              ssd_A_log=m_ssd_A_log, ssd_dt_bias=m_ssd_dt_bias, ssd_D=m_ssd_D, ssd_norm=m_ssd_norm,
              ret_norm=m_ret_norm, w_out=m_w_out)
    vs = dict(pre_norm=v_pre_norm, post_norm=v_post_norm, w_in=v_w_in, gdn_conv=v_gdn_conv, gdn_A_log=v_gdn_A_log,
              gdn_dt_bias=v_gdn_dt_bias, gdn_norm=v_gdn_norm, ssd_conv=v_ssd_conv, ssd_conv_b=v_ssd_conv_b,
              ssd_A_log=v_ssd_A_log, ssd_dt_bias=v_ssd_dt_bias, ssd_D=v_ssd_D, ssd_norm=v_ssd_norm,
              ret_norm=v_ret_norm, w_out=v_w_out)
    names = list(weights)
    res = {}
    for nme in names:
        if nme == "w_in":
            res[nme] = _adamw_pairs(w_in, s_in, t_in, m_w_in, v_w_in, "adamw_w_in")
        elif nme == "w_out":
            res[nme] = _adamw_pairs(w_out, s_out, t_out, m_w_out, v_w_out, "adamw_w_out")
        else:
            res[nme] = _adamw(weights[nme], g_small[nme], ms[nme], vs[nme], "adamw_" + nme)
    return (loss, grad_x, *[res[n][0] for n in names], *[res[n][1] for n in names],
            *[res[n][2] for n in names], *[res[n][3] for n in names])
```
